```python
import jax, jax.numpy as jnp
from jax import lax
import numpy as np

D_MODEL = 1024
BATCH = 16
SEQ = 2048
DEPTH = 1

HEAD_DIM = 64
A_GROUPS = 6
B_HEADS = 6
M_HEADS = 4
A_WIDTH = A_GROUPS * HEAD_DIM
B_WIDTH = B_HEADS * HEAD_DIM
M_WIDTH = M_HEADS * HEAD_DIM
D_MIX = A_WIDTH + B_WIDTH + M_WIDTH
IN_COLS = 2 * A_WIDTH + 3 * B_WIDTH + B_HEADS + M_WIDTH
CHUNK = 128
QBLOCK = 128
MEM_TOKENS = 256
D_FF = ((8 * D_MODEL // 3 + 255) // 256) * 256
EPS = 1e-6
NEG_INF = -1e30

kernel_name = "hybrid_gmlp_fox_memxattn_block"


def rms_norm(x, g):
    xf = x.astype(jnp.float32)
    y = xf * lax.rsqrt(jnp.mean(xf * xf, axis=-1, keepdims=True) + EPS)
    return (y * g.astype(jnp.float32)).astype(x.dtype)


def chunked_sgu(zu, zv, g_sgu, w_s, b_s):
    B, S, _ = zu.shape
    u = jax.nn.gelu(zu)
    v = rms_norm(jax.nn.gelu(zv), g_sgu)
    v = v.reshape(B, S // CHUNK, CHUNK, A_GROUPS, HEAD_DIM)
    causal = jnp.tril(jnp.ones((CHUNK, CHUNK), dtype=bool))
    ws = jnp.where(causal[None], w_s, jnp.zeros_like(w_s))
    z = jnp.einsum('gts,bcsgd->bctgd', ws, v) + b_s.T[None, None, :, :, None]
    return u * z.reshape(B, S, A_WIDTH)


def forgetting_attention(q, k, v, f_logit):
    B, S, _ = q.shape
    q = q.reshape(B, S, B_HEADS, HEAD_DIM).transpose(0, 2, 1, 3)
    k = k.reshape(B, S, B_HEADS, HEAD_DIM).transpose(0, 2, 1, 3)
    v = v.reshape(B, S, B_HEADS, HEAD_DIM).transpose(0, 2, 1, 3)
    log_f = jax.nn.log_sigmoid(f_logit.astype(jnp.float32))
    c = jnp.cumsum(log_f, axis=1).transpose(0, 2, 1)
    nb = S // QBLOCK
    qb = q.reshape(B, B_HEADS, nb, QBLOCK, HEAD_DIM).transpose(2, 0, 1, 3, 4)
    cb = c.reshape(B, B_HEADS, nb, QBLOCK).transpose(2, 0, 1, 3)
    kpos = jnp.arange(S)
    scale = HEAD_DIM ** -0.5

    def block(args):
        qi, ci, i = args
        s = jnp.einsum('bhqd,bhkd->bhqk', qi, k).astype(jnp.float32) * scale
        s = s + ci[..., None] - c[:, :, None, :]
        qpos = i * QBLOCK + jnp.arange(QBLOCK)
        s = jnp.where(kpos[None, :] <= qpos[:, None], s, NEG_INF)
        p = jax.nn.softmax(s, axis=-1).astype(v.dtype)
        return jnp.einsum('bhqk,bhkd->bhqd', p, v)

    o = lax.map(block, (qb, cb, jnp.arange(nb)))
    return o.transpose(1, 0, 3, 2, 4).reshape(B, S, B_WIDTH)


def memory_attention(qm, mem_n, w_mem_kv):
    B, S, _ = qm.shape
    M = mem_n.shape[1]
    kv = mem_n @ w_mem_kv
    km, vm = jnp.split(kv, 2, axis=-1)
    qm = qm.reshape(B, S, M_HEADS, HEAD_DIM)
    km = km.reshape(B, M, M_HEADS, HEAD_DIM)
    vm = vm.reshape(B, M, M_HEADS, HEAD_DIM)
    s = jnp.einsum('bshd,bmhd->bhsm', qm, km).astype(jnp.float32) * HEAD_DIM ** -0.5
    p = jax.nn.softmax(s, axis=-1).astype(vm.dtype)
    return jnp.einsum('bhsm,bmhd->bshd', p, vm).reshape(B, S, M_WIDTH)


def _fwd_setup_inputs(seed: int = 0) -> dict:
    key = jax.random.key(seed)
    ks = jax.random.split(key, 24)
    f32 = jnp.float32

    def nrm(k, shape, scale):
        return jax.random.normal(k, shape, f32) * scale

    def gain(k, shape):
        return 1.0 + 0.02 * jax.random.normal(k, shape, f32)

    L = DEPTH
    return {
        "x": jax.random.normal(ks[0], (BATCH, SEQ, D_MODEL), f32),
        "mem": jax.random.normal(ks[1], (BATCH, MEM_TOKENS, D_MODEL), f32),
        "g_pre_mix": gain(ks[2], (L, D_MODEL)),
        "w_in": nrm(ks[3], (L, D_MODEL, IN_COLS), D_MODEL ** -0.5),
        "b_f": 3.0 + 0.5 * jax.random.normal(ks[4], (L, B_HEADS), f32),
        "g_sgu": gain(ks[5], (L, A_WIDTH)),
        "w_s": nrm(ks[6], (L, A_GROUPS, CHUNK, CHUNK), CHUNK ** -0.5),
        "b_s": gain(ks[7], (L, A_GROUPS, CHUNK)),
        "g_out_a": gain(ks[8], (L, A_WIDTH)),
        "g_out_b": gain(ks[9], (L, B_WIDTH)),
        "g_out_m": gain(ks[10], (L, M_WIDTH)),
        "g_mem": gain(ks[11], (L, D_MODEL)),
        "w_mem_kv": nrm(ks[12], (L, D_MODEL, 2 * M_WIDTH), D_MODEL ** -0.5),
        "w_out": nrm(ks[13], (L, D_MIX, D_MODEL), D_MIX ** -0.5),
        "g_post_mix": gain(ks[14], (L, D_MODEL)),
        "g_pre_ffn": gain(ks[15], (L, D_MODEL)),
        "w_gate": nrm(ks[16], (L, D_MODEL, D_FF), D_MODEL ** -0.5),
        "w_up": nrm(ks[17], (L, D_MODEL, D_FF), D_MODEL ** -0.5),
        "w_down": nrm(ks[18], (L, D_FF, D_MODEL), D_FF ** -0.5),
        "g_post_ffn": gain(ks[19], (L, D_MODEL)),
    }


def _fwd_reference(x, mem, g_pre_mix, w_in, b_f, g_sgu, w_s, b_s, g_out_a, g_out_b, g_out_m,
              g_mem, w_mem_kv, w_out, g_post_mix, g_pre_ffn, w_gate, w_up, w_down, g_post_ffn):
    split_at = [A_WIDTH, 2 * A_WIDTH,
                2 * A_WIDTH + B_WIDTH, 2 * A_WIDTH + 2 * B_WIDTH, 2 * A_WIDTH + 3 * B_WIDTH,
                2 * A_WIDTH + 3 * B_WIDTH + B_HEADS]
    for l in range(DEPTH):
        h = rms_norm(x, g_pre_mix[l])
        proj = h @ w_in[l]
        zu, zv, q, k, v, f_logit, qm = jnp.split(proj, split_at, axis=-1)
        y_a = chunked_sgu(zu, zv, g_sgu[l], w_s[l], b_s[l])
        y_b = forgetting_attention(q, k, v, f_logit + b_f[l])
        y_m = memory_attention(qm, rms_norm(mem, g_mem[l]), w_mem_kv[l])
        y = jnp.concatenate([rms_norm(y_a, g_out_a[l]),
                             rms_norm(y_b, g_out_b[l]),
                             rms_norm(y_m, g_out_m[l])], axis=-1)
        x = x + rms_norm(y @ w_out[l], g_post_mix[l])
        h = rms_norm(x, g_pre_ffn[l])
        ff = (jax.nn.silu(h @ w_gate[l]) * (h @ w_up[l])) @ w_down[l]
        x = x + rms_norm(ff, g_post_ffn[l])
    return x


import jax as _jax
import jax.numpy as _jnp

TWIN_FORMAT = 'train_step'
FWD_PARAMS = ['x', 'mem', 'g_pre_mix', 'w_in', 'b_f', 'g_sgu', 'w_s', 'b_s', 'g_out_a', 'g_out_b', 'g_out_m', 'g_mem', 'w_mem_kv', 'w_out', 'g_post_mix', 'g_pre_ffn', 'w_gate', 'w_up', 'w_down', 'g_post_ffn']
TWIN_WEIGHTS = ['g_pre_mix', 'w_in', 'b_f', 'g_sgu', 'w_s', 'b_s', 'g_out_a', 'g_out_b', 'g_out_m', 'g_mem', 'w_mem_kv', 'w_out', 'g_post_mix', 'g_pre_ffn', 'w_gate', 'w_up', 'w_down', 'g_post_ffn']
TWIN_DIFF_INPUT = 'x'
TWIN_INPUTS = ['x', 'mem', 'g_pre_mix', 'w_in', 'b_f', 'g_sgu', 'w_s', 'b_s', 'g_out_a', 'g_out_b', 'g_out_m', 'g_mem', 'w_mem_kv', 'w_out', 'g_post_mix', 'g_pre_ffn', 'w_gate', 'w_up', 'w_down', 'g_post_ffn', 'loss_target', 'm_g_pre_mix', 'm_w_in', 'm_b_f', 'm_g_sgu', 'm_w_s', 'm_b_s', 'm_g_out_a', 'm_g_out_b', 'm_g_out_m', 'm_g_mem', 'm_w_mem_kv', 'm_w_out', 'm_g_post_mix', 'm_g_pre_ffn', 'm_w_gate', 'm_w_up', 'm_w_down', 'm_g_post_ffn', 'v_g_pre_mix', 'v_w_in', 'v_b_f', 'v_g_sgu', 'v_w_s', 'v_b_s', 'v_g_out_a', 'v_g_out_b', 'v_g_out_m', 'v_g_mem', 'v_w_mem_kv', 'v_w_out', 'v_g_post_mix', 'v_g_pre_ffn', 'v_w_gate', 'v_w_up', 'v_w_down', 'v_g_post_ffn']
TWIN_OUTPUTS = ['loss', 'grad_x', 'grad_g_pre_mix', 'grad_w_in', 'grad_b_f', 'grad_g_sgu', 'grad_w_s', 'grad_b_s', 'grad_g_out_a', 'grad_g_out_b', 'grad_g_out_m', 'grad_g_mem', 'grad_w_mem_kv', 'grad_w_out', 'grad_g_post_mix', 'grad_g_pre_ffn', 'grad_w_gate', 'grad_w_up', 'grad_w_down', 'grad_g_post_ffn', 'delta_g_pre_mix', 'delta_w_in', 'delta_b_f', 'delta_g_sgu', 'delta_w_s', 'delta_b_s', 'delta_g_out_a', 'delta_g_out_b', 'delta_g_out_m', 'delta_g_mem', 'delta_w_mem_kv', 'delta_w_out', 'delta_g_post_mix', 'delta_g_pre_ffn', 'delta_w_gate', 'delta_w_up', 'delta_w_down', 'delta_g_post_ffn', 'new_m_g_pre_mix', 'new_m_w_in', 'new_m_b_f', 'new_m_g_sgu', 'new_m_w_s', 'new_m_b_s', 'new_m_g_out_a', 'new_m_g_out_b', 'new_m_g_out_m', 'new_m_g_mem', 'new_m_w_mem_kv', 'new_m_w_out', 'new_m_g_post_mix', 'new_m_g_pre_ffn', 'new_m_w_gate', 'new_m_w_up', 'new_m_w_down', 'new_m_g_post_ffn', 'new_v_g_pre_mix', 'new_v_w_in', 'new_v_b_f', 'new_v_g_sgu', 'new_v_w_s', 'new_v_b_s', 'new_v_g_out_a', 'new_v_g_out_b', 'new_v_g_out_m', 'new_v_g_mem', 'new_v_w_mem_kv', 'new_v_w_out', 'new_v_g_post_mix', 'new_v_g_pre_ffn', 'new_v_w_gate', 'new_v_w_up', 'new_v_w_down', 'new_v_g_post_ffn']
TWIN_LEAF_KINDS = {'loss': 'loss', 'grad_x': 'grad_x', 'grad_g_pre_mix': 'grad_w', 'grad_w_in': 'grad_w', 'grad_b_f': 'grad_w', 'grad_g_sgu': 'grad_w', 'grad_w_s': 'grad_w', 'grad_b_s': 'grad_w', 'grad_g_out_a': 'grad_w', 'grad_g_out_b': 'grad_w', 'grad_g_out_m': 'grad_w', 'grad_g_mem': 'grad_w', 'grad_w_mem_kv': 'grad_w', 'grad_w_out': 'grad_w', 'grad_g_post_mix': 'grad_w', 'grad_g_pre_ffn': 'grad_w', 'grad_w_gate': 'grad_w', 'grad_w_up': 'grad_w', 'grad_w_down': 'grad_w', 'grad_g_post_ffn': 'grad_w', 'delta_g_pre_mix': 'delta_w', 'delta_w_in': 'delta_w', 'delta_b_f': 'delta_w', 'delta_g_sgu': 'delta_w', 'delta_w_s': 'delta_w', 'delta_b_s': 'delta_w', 'delta_g_out_a': 'delta_w', 'delta_g_out_b': 'delta_w', 'delta_g_out_m': 'delta_w', 'delta_g_mem': 'delta_w', 'delta_w_mem_kv': 'delta_w', 'delta_w_out': 'delta_w', 'delta_g_post_mix': 'delta_w', 'delta_g_pre_ffn': 'delta_w', 'delta_w_gate': 'delta_w', 'delta_w_up': 'delta_w', 'delta_w_down': 'delta_w', 'delta_g_post_ffn': 'delta_w', 'new_m_g_pre_mix': 'new_m', 'new_m_w_in': 'new_m', 'new_m_b_f': 'new_m', 'new_m_g_sgu': 'new_m', 'new_m_w_s': 'new_m', 'new_m_b_s': 'new_m', 'new_m_g_out_a': 'new_m', 'new_m_g_out_b': 'new_m', 'new_m_g_out_m': 'new_m', 'new_m_g_mem': 'new_m', 'new_m_w_mem_kv': 'new_m', 'new_m_w_out': 'new_m', 'new_m_g_post_mix': 'new_m', 'new_m_g_pre_ffn': 'new_m', 'new_m_w_gate': 'new_m', 'new_m_w_up': 'new_m', 'new_m_w_down': 'new_m', 'new_m_g_post_ffn': 'new_m', 'new_v_g_pre_mix': 'new_v', 'new_v_w_in': 'new_v', 'new_v_b_f': 'new_v', 'new_v_g_sgu': 'new_v', 'new_v_w_s': 'new_v', 'new_v_b_s': 'new_v', 'new_v_g_out_a': 'new_v', 'new_v_g_out_b': 'new_v', 'new_v_g_out_m': 'new_v', 'new_v_g_mem': 'new_v', 'new_v_w_mem_kv': 'new_v', 'new_v_w_out': 'new_v', 'new_v_g_post_mix': 'new_v', 'new_v_g_pre_ffn': 'new_v', 'new_v_w_gate': 'new_v', 'new_v_w_up': 'new_v', 'new_v_w_down': 'new_v', 'new_v_g_post_ffn': 'new_v'}


def _forward(args):
    return _fwd_reference(*[args[k] for k in FWD_PARAMS])


def _output_shape():
    out = _jax.eval_shape(lambda: _forward(_fwd_setup_inputs(0)))
    return out.shape, out.dtype

N_MICROBATCH = 1
ADAM_LR = 0.001
ADAM_B1 = 0.9
ADAM_B2 = 0.999
ADAM_EPS = 1e-08
ADAM_WD = 0.01
ADAM_STEP = 10
PER_EXAMPLE_BATCH_AXIS = {'x': 0, 'mem': 0, 'loss_target': 0}
SHARED_INPUTS = []
_WEIGHT_DTYPES = {'g_pre_mix': _jnp.float32, 'w_in': _jnp.float32, 'b_f': _jnp.float32, 'g_sgu': _jnp.float32, 'w_s': _jnp.float32, 'b_s': _jnp.float32, 'g_out_a': _jnp.float32, 'g_out_b': _jnp.float32, 'g_out_m': _jnp.float32, 'g_mem': _jnp.float32, 'w_mem_kv': _jnp.float32, 'w_out': _jnp.float32, 'g_post_mix': _jnp.float32, 'g_pre_ffn': _jnp.float32, 'w_gate': _jnp.float32, 'w_up': _jnp.float32, 'w_down': _jnp.float32, 'g_post_ffn': _jnp.float32}
MOMENT_SCALE = {'g_pre_mix': 6.674560e-01, 'w_in': 4.451865e-01, 'b_f': 7.350477e+00, 'g_sgu': 2.519361e-01, 'w_s': 1.747295e-01, 'b_s': 2.598879e-01, 'g_out_a': 1.272694e+00, 'g_out_b': 6.458548e-01, 'g_out_m': 1.176282e+00, 'g_mem': 6.122684e-01, 'w_mem_kv': 8.597548e-01, 'w_out': 9.895270e-01, 'g_post_mix': 3.220808e+01, 'g_pre_ffn': 7.940125e-01, 'w_gate': 2.793459e-01, 'w_up': 4.560056e-01, 'w_down': 7.358480e-01, 'g_post_ffn': 3.186380e+01}


def _to_microbatches(a, axis):
    t = _jnp.moveaxis(a, axis, 0)
    t = t.reshape((N_MICROBATCH, t.shape[0] // N_MICROBATCH) + t.shape[1:])
    return _jnp.moveaxis(t, 1, axis + 1)


def setup_inputs(seed: int = 0) -> dict:
    inp = _fwd_setup_inputs(seed)
    key = _jax.random.fold_in(_jax.random.key(seed), 7919)
    shape, _ = _output_shape()
    out = dict(inp)
    out["loss_target"] = _jax.random.normal(_jax.random.fold_in(key, 0), shape, _jnp.float32)
    for i, name in enumerate(TWIN_WEIGHTS):
        w = inp[name].astype(_jnp.float32)
        if MOMENT_SCALE is None:
            s = _jnp.sqrt(_jnp.mean(_jnp.square(w)) + 1e-30)
        else:
            s = MOMENT_SCALE[name]
        km, kv = _jax.random.split(_jax.random.fold_in(key, i + 1))
        out[name] = w
        out["m_" + name] = s * _jax.random.normal(km, w.shape, _jnp.float32)
        out["v_" + name] = (s * s) * _jax.random.uniform(kv, w.shape, _jnp.float32, 0.5, 1.5)
    if N_MICROBATCH > 1:
        for name, axis in PER_EXAMPLE_BATCH_AXIS.items():
            out[name] = _to_microbatches(out[name], axis)
    return {'x': out['x'], 'mem': out['mem'], 'g_pre_mix': out['g_pre_mix'], 'w_in': out['w_in'], 'b_f': out['b_f'], 'g_sgu': out['g_sgu'], 'w_s': out['w_s'], 'b_s': out['b_s'], 'g_out_a': out['g_out_a'], 'g_out_b': out['g_out_b'], 'g_out_m': out['g_out_m'], 'g_mem': out['g_mem'], 'w_mem_kv': out['w_mem_kv'], 'w_out': out['w_out'], 'g_post_mix': out['g_post_mix'], 'g_pre_ffn': out['g_pre_ffn'], 'w_gate': out['w_gate'], 'w_up': out['w_up'], 'w_down': out['w_down'], 'g_post_ffn': out['g_post_ffn'], 'loss_target': out['loss_target'], 'm_g_pre_mix': out['m_g_pre_mix'], 'm_w_in': out['m_w_in'], 'm_b_f': out['m_b_f'], 'm_g_sgu': out['m_g_sgu'], 'm_w_s': out['m_w_s'], 'm_b_s': out['m_b_s'], 'm_g_out_a': out['m_g_out_a'], 'm_g_out_b': out['m_g_out_b'], 'm_g_out_m': out['m_g_out_m'], 'm_g_mem': out['m_g_mem'], 'm_w_mem_kv': out['m_w_mem_kv'], 'm_w_out': out['m_w_out'], 'm_g_post_mix': out['m_g_post_mix'], 'm_g_pre_ffn': out['m_g_pre_ffn'], 'm_w_gate': out['m_w_gate'], 'm_w_up': out['m_w_up'], 'm_w_down': out['m_w_down'], 'm_g_post_ffn': out['m_g_post_ffn'], 'v_g_pre_mix': out['v_g_pre_mix'], 'v_w_in': out['v_w_in'], 'v_b_f': out['v_b_f'], 'v_g_sgu': out['v_g_sgu'], 'v_w_s': out['v_w_s'], 'v_b_s': out['v_b_s'], 'v_g_out_a': out['v_g_out_a'], 'v_g_out_b': out['v_g_out_b'], 'v_g_out_m': out['v_g_out_m'], 'v_g_mem': out['v_g_mem'], 'v_w_mem_kv': out['v_w_mem_kv'], 'v_w_out': out['v_w_out'], 'v_g_post_mix': out['v_g_post_mix'], 'v_g_pre_ffn': out['v_g_pre_ffn'], 'v_w_gate': out['v_w_gate'], 'v_w_up': out['v_w_up'], 'v_w_down': out['v_w_down'], 'v_g_post_ffn': out['v_g_post_ffn']}


def _loss(weights, diff, rest, loss_target):
    with _jax.named_scope("forward"):
        args = {**rest, TWIN_DIFF_INPUT: diff, **{k: w.astype(_WEIGHT_DTYPES[k]) for k, w in weights.items()}}
        y = _forward(args)
    with _jax.named_scope("loss_head"):
        err = _jnp.square(y.astype(_jnp.float32) - loss_target)
        return 0.5 * _jnp.sum(_jnp.mean(err, axis=-1)) if err.ndim else 0.5 * err


def _adamw(w, g, m, v):
    m = ADAM_B1 * m + (1.0 - ADAM_B1) * g
    v = ADAM_B2 * v + (1.0 - ADAM_B2) * _jnp.square(g)
    m_hat = m / (1.0 - ADAM_B1 ** ADAM_STEP)
    v_hat = v / (1.0 - ADAM_B2 ** ADAM_STEP)
    delta = -ADAM_LR * (m_hat / (_jnp.sqrt(v_hat) + ADAM_EPS) + ADAM_WD * w)
    return delta, m, v


def reference(x, mem, g_pre_mix, w_in, b_f, g_sgu, w_s, b_s, g_out_a, g_out_b, g_out_m, g_mem, w_mem_kv, w_out, g_post_mix, g_pre_ffn, w_gate, w_up, w_down, g_post_ffn, loss_target, m_g_pre_mix, m_w_in, m_b_f, m_g_sgu, m_w_s, m_b_s, m_g_out_a, m_g_out_b, m_g_out_m, m_g_mem, m_w_mem_kv, m_w_out, m_g_post_mix, m_g_pre_ffn, m_w_gate, m_w_up, m_w_down, m_g_post_ffn, v_g_pre_mix, v_w_in, v_b_f, v_g_sgu, v_w_s, v_b_s, v_g_out_a, v_g_out_b, v_g_out_m, v_g_mem, v_w_mem_kv, v_w_out, v_g_post_mix, v_g_pre_ffn, v_w_gate, v_w_up, v_w_down, v_g_post_ffn):
    given = dict(x=x, mem=mem, g_pre_mix=g_pre_mix, w_in=w_in, b_f=b_f, g_sgu=g_sgu, w_s=w_s, b_s=b_s, g_out_a=g_out_a, g_out_b=g_out_b, g_out_m=g_out_m, g_mem=g_mem, w_mem_kv=w_mem_kv, w_out=w_out, g_post_mix=g_post_mix, g_pre_ffn=g_pre_ffn, w_gate=w_gate, w_up=w_up, w_down=w_down, g_post_ffn=g_post_ffn, loss_target=loss_target, m_g_pre_mix=m_g_pre_mix, m_w_in=m_w_in, m_b_f=m_b_f, m_g_sgu=m_g_sgu, m_w_s=m_w_s, m_b_s=m_b_s, m_g_out_a=m_g_out_a, m_g_out_b=m_g_out_b, m_g_out_m=m_g_out_m, m_g_mem=m_g_mem, m_w_mem_kv=m_w_mem_kv, m_w_out=m_w_out, m_g_post_mix=m_g_post_mix, m_g_pre_ffn=m_g_pre_ffn, m_w_gate=m_w_gate, m_w_up=m_w_up, m_w_down=m_w_down, m_g_post_ffn=m_g_post_ffn, v_g_pre_mix=v_g_pre_mix, v_w_in=v_w_in, v_b_f=v_b_f, v_g_sgu=v_g_sgu, v_w_s=v_w_s, v_b_s=v_b_s, v_g_out_a=v_g_out_a, v_g_out_b=v_g_out_b, v_g_out_m=v_g_out_m, v_g_mem=v_g_mem, v_w_mem_kv=v_w_mem_kv, v_w_out=v_w_out, v_g_post_mix=v_g_post_mix, v_g_pre_ffn=v_g_pre_ffn, v_w_gate=v_w_gate, v_w_up=v_w_up, v_w_down=v_w_down, v_g_post_ffn=v_g_post_ffn)
    weights = {n: given[n] for n in TWIN_WEIGHTS}
    shared = {n: given[n] for n in SHARED_INPUTS}
    per_example = {n: given[n] for n in ['x', 'mem']}
    grad_fn = _jax.value_and_grad(_loss, argnums=(0, 1))

    def one_microbatch(ex, loss_target):
        ex = dict(ex)
        diff = ex.pop(TWIN_DIFF_INPUT)
        return grad_fn(weights, diff, {**shared, **ex}, loss_target)

    if N_MICROBATCH == 1:
        loss, (grad_w, grad_x) = one_microbatch(per_example, given["loss_target"])
    else:
        def body(carry, xs):
            loss_sum, grad_sum = carry
            l_k, (gw_k, gx_k) = one_microbatch(xs[0], xs[1])
            with _jax.named_scope("update"):
                return (loss_sum + l_k, _jax.tree.map(_jnp.add, grad_sum, gw_k)), gx_k

        init = (_jnp.zeros((), _jnp.float32), _jax.tree.map(_jnp.zeros_like, weights))
        (loss, grad_w), grad_x = _jax.lax.scan(body, init, (per_example, given["loss_target"]))
    with _jax.named_scope("update"):
        delta_w, new_m, new_v = {}, {}, {}
        for n in TWIN_WEIGHTS:
            delta_w[n], new_m[n], new_v[n] = _adamw(weights[n], grad_w[n], given["m_" + n], given["v_" + n])
    return (loss, grad_x, *[grad_w[n] for n in TWIN_WEIGHTS], *[delta_w[n] for n in TWIN_WEIGHTS],
            *[new_m[n] for n in TWIN_WEIGHTS], *[new_v[n] for n in TWIN_WEIGHTS])
```

```python
import functools

import jax
import jax.numpy as jnp
from jax import lax
from jax.experimental import pallas as pl
from jax.experimental.pallas import tpu as pltpu

F32 = jnp.float32
BF16 = jnp.bfloat16
EPS = 1e-6
NEG = -1e30
HEAD = 64
A_W, B_W, M_W = 384, 384, 256
N_FOX_HEADS = 6
CHUNK = 128
IN_COLS = 2 * A_W + 3 * B_W + N_FOX_HEADS + M_W
P_MAIN = 2 * A_W + 3 * B_W + M_W
P_COLS = P_MAIN + 128
LANES = 128
Q_BLK, K_BLK = 256, 128
ADAM_LR, ADAM_B1, ADAM_B2, ADAM_EPS, ADAM_WD, ADAM_STEP = 0.001, 0.9, 0.999, 1e-08, 0.01, 10
VMEM_LIMIT = 56 * 1024 * 1024
MESH = pl.DeviceIdType.MESH
ANY = pl.BlockSpec(memory_space=pl.ANY)
BS = pl.BlockSpec


def _cp(sem=None):
    return pltpu.CompilerParams(dimension_semantics=sem, vmem_limit_bytes=VMEM_LIMIT)


def _iota(shape, dim):
    return lax.broadcasted_iota(jnp.int32, shape, dim)


def _dot(a, b):
    return jnp.dot(a.astype(BF16), b.astype(BF16), preferred_element_type=F32)


def _dot_nt(a, b):
    return lax.dot_general(a.astype(BF16), b.astype(BF16), (((1,), (1,)), ((), ())), preferred_element_type=F32)


def _dot_tn(a, b):
    return lax.dot_general(a.astype(BF16), b.astype(BF16), (((0,), (0,)), ((), ())), preferred_element_type=F32)


def _rms(x, g):
    return x * lax.rsqrt(jnp.mean(x * x, axis=-1, keepdims=True) + EPS) * g


def _gelu(x):
    return 0.5 * x * (1.0 + jnp.tanh(0.7978845608028654 * (x + 0.044715 * (x * x * x))))


def _sigmoid(x):
    return 1.0 / (1.0 + jnp.exp(-x))


def _silu_mul(g, u):
    return g * _sigmoid(g) * u


def _logsig(x):
    return jnp.minimum(x, 0.0) - jnp.log(1.0 + jnp.exp(-jnp.abs(x)))


def _colsum(x):
    return jnp.sum(x, axis=0, keepdims=True)


def _acc(ref, val, first):
    @pl.when(first)
    def _():
        ref[...] = val

    @pl.when(jnp.logical_not(first))
    def _():
        ref[...] += val


def _inproj_fwd(x2d, g_pre, w_in_p, tm):
    T, D = x2d.shape
    nchunk = P_COLS // 384

    def body(x_ref, g_ref, w_ref, h_ref, proj_ref, fl_ref):
        h = _rms(x_ref[...], g_ref[...]).astype(BF16)
        h_ref[...] = h
        for n in range(nchunk):
            r = jnp.dot(h, w_ref[:, n * 384:(n + 1) * 384], preferred_element_type=F32)
            if n < nchunk - 1:
                proj_ref[:, n * 384:(n + 1) * 384] = r.astype(BF16)
            else:
                proj_ref[:, n * 384:n * 384 + 256] = r[:, :256].astype(BF16)
                fl_ref[...] = r[:, 256:384]

    return pl.pallas_call(
        body, name="inproj_fwd", grid=(T // tm,),
        in_specs=[BS((tm, D), lambda i: (i, 0)), BS((1, D), lambda i: (0, 0)), BS((D, P_COLS), lambda i: (0, 0))],
        out_specs=[BS((tm, D), lambda i: (i, 0)), BS((tm, P_MAIN), lambda i: (i, 0)), BS((tm, LANES), lambda i: (i, 0))],
        out_shape=[jax.ShapeDtypeStruct((T, D), BF16), jax.ShapeDtypeStruct((T, P_MAIN), BF16),
                   jax.ShapeDtypeStruct((T, LANES), F32)],
        compiler_params=_cp(("arbitrary",)),
    )(x2d, g_pre, w_in_p)


def _gate_fwd(flog3, bf_row):
    Bl, S, _ = flog3.shape
    nb = S // LANES

    def body(f_ref, b_ref, cc_ref, cr_ref, fr_ref):
        row = _iota((LANES, LANES), 0)

        def blk(j, carry):
            r0 = pl.multiple_of(j * LANES, LANES)
            fl = f_ref[0, pl.ds(r0, LANES), :] + b_ref[...]
            fr_ref[0, j] = fl.T[0:8, :]
            c = _logsig(fl)
            for k in (1, 2, 4, 8, 16, 32, 64):
                c = c + jnp.where(row >= k, pltpu.roll(c, k, 0), 0.0)
            c = c + carry
            cc_ref[0, pl.ds(r0, LANES), :] = c
            cr_ref[0, j] = c.T[0:8, :]
            return _colsum(jnp.where(row == LANES - 1, c, 0.0))

        lax.fori_loop(0, nb, blk, jnp.zeros((1, LANES), F32))

    rowblk = BS((1, nb, 8, LANES), lambda b: (b, 0, 0, 0))
    return pl.pallas_call(
        body, name="gate_fwd", grid=(Bl,),
        in_specs=[BS((1, S, LANES), lambda b: (b, 0, 0)), BS((1, LANES), lambda b: (0, 0))],
        out_specs=[BS((1, S, LANES), lambda b: (b, 0, 0)), rowblk, rowblk],
        out_shape=[jax.ShapeDtypeStruct((Bl, S, LANES), F32), jax.ShapeDtypeStruct((Bl, nb, 8, LANES), F32),
                   jax.ShapeDtypeStruct((Bl, nb, 8, LANES), F32)],
        compiler_params=_cp(("arbitrary",)),
    )(flog3, bf_row)


def _sgu_pre(zu, zv, g_sgu):
    return _gelu(zu), _rms(_gelu(zv), g_sgu)


def _sgu_fwd(proj, g_sgu, ws_tril, bs_full, tm):
    T = proj.shape[0]
    nch = tm // CHUNK

    def body(zu_ref, zv_ref, g_ref, ws_ref, b_ref, ya_ref):
        lane = _iota((CHUNK, LANES), 1)
        u, vn = _sgu_pre(zu_ref[...].astype(F32), zv_ref[...].astype(F32), g_ref[...])
        vn = vn.astype(BF16)
        for c in range(nch):
            rs = slice(c * CHUNK, (c + 1) * CHUNK)
            for j in range(3):
                cs = slice(j * LANES, (j + 1) * LANES)
                vp = vn[rs, cs]
                z = jnp.where(lane < HEAD, _dot(ws_ref[2 * j], vp), _dot(ws_ref[2 * j + 1], vp)) + b_ref[:, cs]
                ya_ref[rs, cs] = u[rs, cs] * z

    return pl.pallas_call(
        body, name="sgu_fwd", grid=(T // tm,),
        in_specs=[BS((tm, A_W), lambda i: (i, 0)), BS((tm, A_W), lambda i: (i, 1)), BS((1, A_W), lambda i: (0, 0)),
                  BS((6, CHUNK, CHUNK), lambda i: (0, 0, 0)), BS((CHUNK, A_W), lambda i: (0, 0))],
        out_specs=BS((tm, A_W), lambda i: (i, 0)),
        out_shape=jax.ShapeDtypeStruct((T, A_W), F32),
        compiler_params=_cp(("arbitrary",)),
    )(proj, proj, g_sgu, ws_tril, bs_full)


def _fox_fwd(proj, c_col, c_row, Bl, S):
    T = Bl * S
    nq = S // Q_BLK
    nb = S // K_BLK
    qc, kc, vc = 768 // LANES, 1152 // LANES, 1536 // LANES

    def body(q_ref, k_ref, v_ref, cc_ref, cr_ref, o_ref, lse_ref):
        p = pl.program_id(1)
        i = pl.program_id(2)
        lane = _iota((Q_BLK, LANES), 1)
        rows = _iota((Q_BLK, K_BLK), 0) + i * Q_BLK
        cols = _iota((Q_BLK, K_BLK), 1)
        sub = _iota((8, K_BLK), 0)
        q = q_ref[...]
        cc = cc_ref[0]
        o_out = jnp.zeros((Q_BLK, LANES), F32)
        lse_out = jnp.zeros((Q_BLK, LANES), F32)
        for hh in range(2):
            hmask = (lane < HEAD) if hh == 0 else (lane >= HEAD)
            h = 2 * p + hh
            qs = jnp.where(hmask, q, jnp.zeros_like(q)) * 0.125
            cq = jnp.sum(jnp.where(lane == h, cc, 0.0), axis=1, keepdims=True)

            def step(kb, carry, qs=qs, cq=cq, h=h):
                m, l, acc = carry
                k0 = pl.multiple_of(kb * K_BLK, K_BLK)
                kblk = k_ref[pl.ds(k0, K_BLK), :]
                vblk = v_ref[pl.ds(k0, K_BLK), :]
                ck = _colsum(jnp.where(sub == h, cr_ref[0, kb], 0.0))
                s = _dot_nt(qs, kblk) + cq - ck
                s = jnp.where(cols + k0 <= rows, s, NEG)
                m_new = jnp.maximum(m, jnp.max(s, axis=1, keepdims=True))
                alpha = jnp.exp(m - m_new)
                pe = jnp.exp(s - m_new)
                l = alpha * l + jnp.sum(pe, axis=1, keepdims=True)
                acc = alpha * acc + _dot(pe, vblk)
                return m_new, l, acc

            m, l, acc = lax.fori_loop(
                0, (i + 1) * (Q_BLK // K_BLK), step,
                (jnp.full((Q_BLK, 1), NEG, F32), jnp.zeros((Q_BLK, 1), F32), jnp.zeros((Q_BLK, LANES), F32)))
            o_out = jnp.where(hmask, acc / l, o_out)
            lse_out = jnp.where(hmask, m + jnp.log(l), lse_out)
        o_ref[...] = o_out
        lse_ref[0] = lse_out

    return pl.pallas_call(
        body, name="fox_fwd", grid=(Bl, 3, nq),
        in_specs=[BS((Q_BLK, LANES), lambda b, p, i: (b * nq + i, qc + p)),
                  BS((S, LANES), lambda b, p, i: (b, kc + p)),
                  BS((S, LANES), lambda b, p, i: (b, vc + p)),
                  BS((1, Q_BLK, LANES), lambda b, p, i: (b, i, 0)),
                  BS((1, nb, 8, LANES), lambda b, p, i: (b, 0, 0, 0))],
        out_specs=[BS((Q_BLK, LANES), lambda b, p, i: (b * nq + i, p)),
                   BS((1, Q_BLK, LANES), lambda b, p, i: (p, b * nq + i, 0))],
        out_shape=[jax.ShapeDtypeStruct((T, B_W), F32), jax.ShapeDtypeStruct((3, T, LANES), F32)],
        compiler_params=_cp(("arbitrary", "arbitrary", "arbitrary")),
    )(proj, proj, proj, c_col, c_row)


def _memkv_fwd(mem, g_mem, w_kv):
    Bl, Mt, D = mem.shape

    def body(m_ref, g_ref, w_ref, mn_ref, kv_ref):
        mn = _rms(m_ref[0], g_ref[...]).astype(BF16)
        mn_ref[0] = mn
        kv_ref[0] = jnp.dot(mn, w_ref[...], preferred_element_type=F32).astype(BF16)

    return pl.pallas_call(
        body, name="memkv_fwd", grid=(Bl,),
        in_specs=[BS((1, Mt, D), lambda b: (b, 0, 0)), BS((1, D), lambda b: (0, 0)), BS((D, 2 * M_W), lambda b: (0, 0))],
        out_specs=[BS((1, Mt, D), lambda b: (b, 0, 0)), BS((1, Mt, 2 * M_W), lambda b: (b, 0, 0))],
        out_shape=[jax.ShapeDtypeStruct((Bl, Mt, D), BF16), jax.ShapeDtypeStruct((Bl, Mt, 2 * M_W), BF16)],
        compiler_params=_cp(("arbitrary",)),
    )(mem, g_mem, w_kv)


def _memattn_fwd(proj, kv, Bl, S, tq):
    T = Bl * S
    nq = S // tq
    Mt = kv.shape[1]
    qc = 1920 // LANES

    def body(q_ref, km_ref, vm_ref, o_ref):
        lane = _iota((tq, LANES), 1)
        q = q_ref[...]
        out = jnp.zeros((tq, LANES), F32)
        for hh in range(2):
            hmask = (lane < HEAD) if hh == 0 else (lane >= HEAD)
            qs = jnp.where(hmask, q, jnp.zeros_like(q)) * 0.125
            s = _dot_nt(qs, km_ref[0])
            pe = jnp.exp(s - jnp.max(s, axis=1, keepdims=True))
            pn = pe / jnp.sum(pe, axis=1, keepdims=True)
            out = jnp.where(hmask, _dot(pn, vm_ref[0]), out)
        o_ref[...] = out

    return pl.pallas_call(
        body, name="memattn_fwd", grid=(Bl, 2, nq),
        in_specs=[BS((tq, LANES), lambda b, p, i: (b * nq + i, qc + p)),
                  BS((1, Mt, LANES), lambda b, p, i: (b, 0, p)),
                  BS((1, Mt, LANES), lambda b, p, i: (b, 0, 2 + p))],
        out_specs=BS((tq, LANES), lambda b, p, i: (b * nq + i, p)),
        out_shape=jax.ShapeDtypeStruct((T, M_W), F32),
        compiler_params=_cp(("arbitrary", "arbitrary", "arbitrary")),
    )(proj, kv, kv)


def _mix_norms(ya, yb, ym, ga, gb, gm):
    return _rms(ya, ga), _rms(yb, gb), _rms(ym, gm)


def _outproj_fwd(ya, yb, ym, x2d, ga, gb, gm, g_post, g_pre2, w_out, tm):
    T, D = x2d.shape

    def body(ya_ref, yb_ref, ym_ref, x_ref, ga_ref, gb_ref, gm_ref, gp_ref, g2_ref, w_ref,
             y_ref, o_ref, x1_ref, h2_ref):
        na, nb_, nm = _mix_norms(ya_ref[...], yb_ref[...], ym_ref[...], ga_ref[...], gb_ref[...], gm_ref[...])
        y_ref[:, 0:A_W] = na.astype(BF16)
        y_ref[:, A_W:A_W + B_W] = nb_.astype(BF16)
        y_ref[:, A_W + B_W:] = nm.astype(BF16)
        o = jnp.dot(y_ref[...], w_ref[...], preferred_element_type=F32)
        o_ref[...] = o
        x1 = x_ref[...] + _rms(o, gp_ref[...])
        x1_ref[...] = x1
        h2_ref[...] = _rms(x1, g2_ref[...]).astype(BF16)

    row = lambda w: BS((tm, w), lambda i: (i, 0))
    vec = lambda w: BS((1, w), lambda i: (0, 0))
    return pl.pallas_call(
        body, name="outproj_fwd", grid=(T // tm,),
        in_specs=[row(A_W), row(B_W), row(M_W), row(D), vec(A_W), vec(B_W), vec(M_W), vec(D), vec(D),
                  BS((A_W + B_W + M_W, D), lambda i: (0, 0))],
        out_specs=[row(A_W + B_W + M_W), row(D), row(D), row(D)],
        out_shape=[jax.ShapeDtypeStruct((T, A_W + B_W + M_W), BF16), jax.ShapeDtypeStruct((T, D), F32),
                   jax.ShapeDtypeStruct((T, D), F32), jax.ShapeDtypeStruct((T, D), BF16)],
        compiler_params=_cp(("arbitrary",)),
    )(ya, yb, ym, x2d, ga, gb, gm, g_post, g_pre2, w_out)


def _ffn_fwd(h2, x1, target, wg, wu, wd, g_post, tm):
    T, D = x1.shape
    ns, _, F = wg.shape

    def body(h_ref, x1_ref, t_ref, wg_ref, wu_ref, wd_ref, gp_ref,
             gs_ref, us_ref, dff_ref, dx2_ref, dgp_ref, loss_ref, acc_ref):
        i = pl.program_id(0)
        j = pl.program_id(1)
        h = h_ref[...]
        g = jnp.dot(h, wg_ref[0], preferred_element_type=F32)
        u = jnp.dot(h, wu_ref[0], preferred_element_type=F32)
        gs_ref[0] = g.astype(BF16)
        us_ref[0] = u.astype(BF16)
        part = _dot(_silu_mul(g, u), wd_ref[0])
        _acc(acc_ref, part, j == 0)

        @pl.when(j == ns - 1)
        def _():
            normed, vjp = jax.vjp(_rms, acc_ref[...], gp_ref[...])
            diff = x1_ref[...] + normed - t_ref[...]
            dx2 = diff * (1.0 / D)
            dff, dgp = vjp(dx2)
            dx2_ref[...] = dx2
            dff_ref[...] = dff.astype(BF16)
            lpart = jnp.sum(_colsum(diff * diff), axis=1, keepdims=True) * (0.5 / D)
            _acc(dgp_ref, dgp, i == 0)
            _acc(loss_ref, jnp.broadcast_to(lpart, (1, LANES)), i == 0)

    row = lambda w: BS((tm, w), lambda i, j: (i, 0))
    return pl.pallas_call(
        body, name="ffn_fwd", grid=(T // tm, ns),
        in_specs=[row(D), row(D), row(D), BS((1, D, F), lambda i, j: (j, 0, 0)), BS((1, D, F), lambda i, j: (j, 0, 0)),
                  BS((1, F, D), lambda i, j: (j, 0, 0)), BS((1, D), lambda i, j: (0, 0))],
        out_specs=[BS((1, tm, F), lambda i, j: (j, i, 0)), BS((1, tm, F), lambda i, j: (j, i, 0)), row(D), row(D),
                   BS((1, D), lambda i, j: (0, 0)), BS((1, LANES), lambda i, j: (0, 0))],
        out_shape=[jax.ShapeDtypeStruct((ns, T, F), BF16), jax.ShapeDtypeStruct((ns, T, F), BF16),
                   jax.ShapeDtypeStruct((T, D), BF16), jax.ShapeDtypeStruct((T, D), F32),
                   jax.ShapeDtypeStruct((1, D), F32), jax.ShapeDtypeStruct((1, LANES), F32)],
        scratch_shapes=[pltpu.VMEM((tm, D), F32)],
        compiler_params=_cp(("arbitrary", "arbitrary")),
    )(h2, x1, target, wg, wu, wd, g_post)


def _ffn_bwd(dff, gs, us, x1, dx2, wg, wu, wd, g_pre2, tm):
    T, D = x1.shape
    ns, _, F = wg.shape

    def body(dff_ref, gs_ref, us_ref, x1_ref, dx2_ref, wg_ref, wu_ref, wd_ref, g2_ref,
             a_ref, dg_ref, du_ref, dx1_ref, dg2_ref, acc_ref):
        i = pl.program_id(0)
        j = pl.program_id(1)
        dact = _dot_nt(dff_ref[...], wd_ref[0])
        a, vjp = jax.vjp(_silu_mul, gs_ref[0].astype(F32), us_ref[0].astype(F32))
        dg, du = vjp(dact)
        a_ref[0] = a.astype(BF16)
        dg = dg.astype(BF16)
        du = du.astype(BF16)
        dg_ref[0] = dg
        du_ref[0] = du
        part = _dot_nt(dg, wg_ref[0]) + _dot_nt(du, wu_ref[0])
        _acc(acc_ref, part, j == 0)

        @pl.when(j == ns - 1)
        def _():
            _, vjp2 = jax.vjp(_rms, x1_ref[...], g2_ref[...])
            dxa, dg2 = vjp2(acc_ref[...])
            dx1_ref[...] = dx2_ref[...] + dxa
            _acc(dg2_ref, dg2, i == 0)

    row = lambda w: BS((tm, w), lambda i, j: (i, 0))
    sh = BS((1, tm, F), lambda i, j: (j, i, 0))
    return pl.pallas_call(
        body, name="ffn_bwd", grid=(T // tm, ns),
        in_specs=[row(D), sh, sh, row(D), row(D), BS((1, D, F), lambda i, j: (j, 0, 0)),
                  BS((1, D, F), lambda i, j: (j, 0, 0)), BS((1, F, D), lambda i, j: (j, 0, 0)),
                  BS((1, D), lambda i, j: (0, 0))],
        out_specs=[sh, sh, sh, row(D), BS((1, D), lambda i, j: (0, 0))],
        out_shape=[jax.ShapeDtypeStruct((ns, T, F), BF16)] * 3 + [jax.ShapeDtypeStruct((T, D), F32),
                                                                  jax.ShapeDtypeStruct((1, D), F32)],
        scratch_shapes=[pltpu.VMEM((tm, D), F32)],
        compiler_params=_cp(("arbitrary", "arbitrary")),
    )(dff, gs, us, x1, dx2, wg, wu, wd, g_pre2)


def _mm_tn(a, b, name, tk):
    ab, bb = a.ndim == 3, b.ndim == 3
    nbatch = a.shape[0] if ab else (b.shape[0] if bb else 1)
    T, M = a.shape[-2:]
    N = b.shape[-1]
    tk = min(tk, T)

    def body(a_ref, b_ref, o_ref):
        av = a_ref[0] if ab else a_ref[...]
        bv = b_ref[0] if bb else b_ref[...]
        _acc(o_ref, _dot_tn(av, bv)[None], pl.program_id(1) == 0)

    a_spec = BS((1, tk, M), lambda s, t: (s, t, 0)) if ab else BS((tk, M), lambda s, t: (t, 0))
    b_spec = BS((1, tk, N), lambda s, t: (s, t, 0)) if bb else BS((tk, N), lambda s, t: (t, 0))
    return pl.pallas_call(
        body, name=name, grid=(nbatch, T // tk),
        in_specs=[a_spec, b_spec],
        out_specs=BS((1, M, N), lambda s, t: (s, 0, 0)),
        out_shape=jax.ShapeDtypeStruct((nbatch, M, N), F32),
        compiler_params=_cp(("arbitrary", "arbitrary")),
    )(a, b)


def _outproj_bwd(dx1, o, ya, yb, ym, ga, gb, gm, g_post, w_out, tm):
    T, D = dx1.shape

    def body(dx1_ref, o_ref, ya_ref, yb_ref, ym_ref, ga_ref, gb_ref, gm_ref, gp_ref, w_ref,
             do_ref, dya_ref, dyb_ref, dym_ref, dga_ref, dgb_ref, dgm_ref, dgp_ref):
        first = pl.program_id(0) == 0
        _, vjp = jax.vjp(_rms, o_ref[...], gp_ref[...])
        do, dgp = vjp(dx1_ref[...])
        do = do.astype(BF16)
        do_ref[...] = do
        dy = _dot_nt(do, w_ref[...])
        _, vjp2 = jax.vjp(_mix_norms, ya_ref[...], yb_ref[...], ym_ref[...], ga_ref[...], gb_ref[...], gm_ref[...])
        dya, dyb, dym, dga, dgb, dgm = vjp2((dy[:, 0:A_W], dy[:, A_W:A_W + B_W], dy[:, A_W + B_W:]))
        dya_ref[...] = dya
        dyb_ref[...] = dyb
        dym_ref[...] = dym
        _acc(dga_ref, dga, first)
        _acc(dgb_ref, dgb, first)
        _acc(dgm_ref, dgm, first)
        _acc(dgp_ref, dgp, first)

    row = lambda w: BS((tm, w), lambda i: (i, 0))
    vec = lambda w: BS((1, w), lambda i: (0, 0))
    sds = jax.ShapeDtypeStruct
    return pl.pallas_call(
        body, name="outproj_bwd", grid=(T // tm,),
        in_specs=[row(D), row(D), row(A_W), row(B_W), row(M_W), vec(A_W), vec(B_W), vec(M_W), vec(D),
                  BS((A_W + B_W + M_W, D), lambda i: (0, 0))],
        out_specs=[row(D), row(A_W), row(B_W), row(M_W), vec(A_W), vec(B_W), vec(M_W), vec(D)],
        out_shape=[sds((T, D), BF16), sds((T, A_W), F32), sds((T, B_W), F32), sds((T, M_W), F32),
                   sds((1, A_W), F32), sds((1, B_W), F32), sds((1, M_W), F32), sds((1, D), F32)],
        compiler_params=_cp(("arbitrary",)),
    )(dx1, o, ya, yb, ym, ga, gb, gm, g_post, w_out)


def _sgu_bwd(proj, dya, g_sgu, ws_tril, bs_full, tm):
    T = proj.shape[0]
    nch = tm // CHUNK

    def body(zu_ref, zv_ref, dy_ref, g_ref, ws_ref, b_ref, dzu_ref, dzv_ref, dws_ref, dbs_ref, dg_ref,
             du_ref, dvn_ref, dbf_ref):
        step = pl.program_id(0)
        first = step == 0
        lane = _iota((CHUNK, LANES), 1)
        tril = _iota((CHUNK, CHUNK), 0) >= _iota((CHUNK, CHUNK), 1)
        (u, vn), vjp = jax.vjp(_sgu_pre, zu_ref[...].astype(F32), zv_ref[...].astype(F32), g_ref[...])
        vnb = vn.astype(BF16)
        dy = dy_ref[...]

        @pl.when(first)
        def _():
            dws_ref[...] = jnp.zeros_like(dws_ref)
            dbf_ref[...] = jnp.zeros_like(dbf_ref)

        for c in range(nch):
            rs = slice(c * CHUNK, (c + 1) * CHUNK)
            for j in range(3):
                cs = slice(j * LANES, (j + 1) * LANES)
                vp = vnb[rs, cs]
                z = jnp.where(lane < HEAD, _dot(ws_ref[2 * j], vp), _dot(ws_ref[2 * j + 1], vp)) + b_ref[:, cs]
                du_ref[rs, cs] = dy[rs, cs] * z
                dz = dy[rs, cs] * u[rs, cs]
                dbf_ref[:, cs] += dz
                dzb = dz.astype(BF16)
                dz0 = jnp.where(lane < HEAD, dzb, jnp.zeros_like(dzb))
                dz1 = jnp.where(lane >= HEAD, dzb, jnp.zeros_like(dzb))
                dvn_ref[rs, cs] = jnp.where(lane < HEAD, _dot_tn(ws_ref[2 * j], dzb), _dot_tn(ws_ref[2 * j + 1], dzb))
                dws_ref[2 * j] += jnp.where(tril, _dot_nt(dz0, vp), 0.0)
                dws_ref[2 * j + 1] += jnp.where(tril, _dot_nt(dz1, vp), 0.0)
        dzu, dzv, dg = vjp((du_ref[...], dvn_ref[...]))
        dzu_ref[...] = dzu.astype(BF16)
        dzv_ref[...] = dzv.astype(BF16)
        _acc(dg_ref, dg, first)

        @pl.when(step == pl.num_programs(0) - 1)
        def _():
            out = jnp.zeros((CHUNK, LANES), F32)
            for j in range(3):
                slab = dbf_ref[:, j * LANES:(j + 1) * LANES]
                lo = jnp.sum(jnp.where(lane < HEAD, slab, 0.0), axis=1, keepdims=True)
                hi = jnp.sum(jnp.where(lane >= HEAD, slab, 0.0), axis=1, keepdims=True)
                out = out + jnp.where(lane == 2 * j, lo, 0.0) + jnp.where(lane == 2 * j + 1, hi, 0.0)
            dbs_ref[...] = out

    return pl.pallas_call(
        body, name="sgu_bwd", grid=(T // tm,),
        in_specs=[BS((tm, A_W), lambda i: (i, 0)), BS((tm, A_W), lambda i: (i, 1)), BS((tm, A_W), lambda i: (i, 0)),
                  BS((1, A_W), lambda i: (0, 0)), BS((6, CHUNK, CHUNK), lambda i: (0, 0, 0)),
                  BS((CHUNK, A_W), lambda i: (0, 0))],
        out_specs=[BS((tm, A_W), lambda i: (i, 0)), BS((tm, A_W), lambda i: (i, 0)),
                   BS((6, CHUNK, CHUNK), lambda i: (0, 0, 0)), BS((CHUNK, LANES), lambda i: (0, 0)),
                   BS((1, A_W), lambda i: (0, 0))],
        out_shape=[jax.ShapeDtypeStruct((T, A_W), BF16), jax.ShapeDtypeStruct((T, A_W), BF16),
                   jax.ShapeDtypeStruct((6, CHUNK, CHUNK), F32), jax.ShapeDtypeStruct((CHUNK, LANES), F32),
                   jax.ShapeDtypeStruct((1, A_W), F32)],
        scratch_shapes=[pltpu.VMEM((tm, A_W), F32), pltpu.VMEM((tm, A_W), F32), pltpu.VMEM((CHUNK, A_W), F32)],
        compiler_params=_cp(("arbitrary",)),
    )(proj, proj, dya, g_sgu, ws_tril, bs_full)


def _memattn_bwd(proj, kv, dym, Bl, S, tq):
    T = Bl * S
    nq = S // tq
    Mt = kv.shape[1]
    qc = 1920 // LANES

    def body(q_ref, km_ref, vm_ref, do_ref, dq_ref, dkm_ref, dvm_ref):
        first = pl.program_id(2) == 0
        lane = _iota((tq, LANES), 1)
        q = q_ref[...]
        do = do_ref[...]
        dq_out = jnp.zeros((tq, LANES), F32)
        dkm = jnp.zeros((Mt, LANES), F32)
        dvm = jnp.zeros((Mt, LANES), F32)
        for hh in range(2):
            hmask = (lane < HEAD) if hh == 0 else (lane >= HEAD)
            qs = jnp.where(hmask, q, jnp.zeros_like(q)) * 0.125
            dom = jnp.where(hmask, do, 0.0).astype(BF16)
            s = _dot_nt(qs, km_ref[0])
            pe = jnp.exp(s - jnp.max(s, axis=1, keepdims=True))
            pn = pe / jnp.sum(pe, axis=1, keepdims=True)
            dp = _dot_nt(dom, vm_ref[0])
            ds = (pn * (dp - jnp.sum(pn * dp, axis=1, keepdims=True))).astype(BF16)
            dq_out = jnp.where(hmask, _dot(ds, km_ref[0]) * 0.125, dq_out)
            dkm = dkm + _dot_tn(ds, qs)
            dvm = dvm + _dot_tn(pn, dom)
        dq_ref[...] = dq_out.astype(BF16)
        _acc(dkm_ref, dkm[None], first)
        _acc(dvm_ref, dvm[None], first)

    return pl.pallas_call(
        body, name="memattn_bwd", grid=(Bl, 2, nq),
        in_specs=[BS((tq, LANES), lambda b, p, i: (b * nq + i, qc + p)),
                  BS((1, Mt, LANES), lambda b, p, i: (b, 0, p)),
                  BS((1, Mt, LANES), lambda b, p, i: (b, 0, 2 + p)),
                  BS((tq, LANES), lambda b, p, i: (b * nq + i, p))],
        out_specs=[BS((tq, LANES), lambda b, p, i: (b * nq + i, p)),
                   BS((1, Mt, LANES), lambda b, p, i: (b, 0, p)),
                   BS((1, Mt, LANES), lambda b, p, i: (b, 0, p))],
        out_shape=[jax.ShapeDtypeStruct((T, M_W), BF16), jax.ShapeDtypeStruct((Bl, Mt, M_W), F32),
                   jax.ShapeDtypeStruct((Bl, Mt, M_W), F32)],
        compiler_params=_cp(("arbitrary", "arbitrary", "arbitrary")),
    )(proj, kv, kv, dym)


def _memkv_bwd(dkm, dvm, memn, mem, g_mem, w_kv):
    Bl, Mt, D = mem.shape

    def body(dk_ref, dv_ref, mn_ref, m_ref, g_ref, w_ref, dw_ref, dg_ref):
        first = pl.program_id(0) == 0
        dk = dk_ref[0].astype(BF16)
        dv = dv_ref[0].astype(BF16)
        mn = mn_ref[0]
        dmn = _dot_nt(dk, w_ref[:, 0:M_W]) + _dot_nt(dv, w_ref[:, M_W:])
        _, vjp = jax.vjp(_rms, m_ref[0], g_ref[...])
        _, dg = vjp(dmn)
        _acc(dg_ref, dg, first)

        @pl.when(first)
        def _():
            dw_ref[...] = jnp.zeros_like(dw_ref)

        dw_ref[:, 0:M_W] += _dot_tn(mn, dk)
        dw_ref[:, M_W:] += _dot_tn(mn, dv)

    return pl.pallas_call(
        body, name="memkv_bwd", grid=(Bl,),
        in_specs=[BS((1, Mt, M_W), lambda b: (b, 0, 0)), BS((1, Mt, M_W), lambda b: (b, 0, 0)),
                  BS((1, Mt, D), lambda b: (b, 0, 0)), BS((1, Mt, D), lambda b: (b, 0, 0)),
                  BS((1, D), lambda b: (0, 0)), BS((D, 2 * M_W), lambda b: (0, 0))],
        out_specs=[BS((D, 2 * M_W), lambda b: (0, 0)), BS((1, D), lambda b: (0, 0))],
        out_shape=[jax.ShapeDtypeStruct((D, 2 * M_W), F32), jax.ShapeDtypeStruct((1, D), F32)],
        compiler_params=_cp(("arbitrary",)),
    )(dkm, dvm, memn, mem, g_mem, w_kv)


def _fox_bwd(proj, dyb, lse, c_col, c_row, Bl, S):
    T = Bl * S
    nq = S // Q_BLK
    nb = S // K_BLK
    qc, kc, vc = 768 // LANES, 1152 // LANES, 1536 // LANES

    def body(q_ref, k_ref, v_ref, do_ref, lse_ref, cc_ref, cr_ref,
             dq_ref, dk_ref, dv_ref, dcr_ref, dka_ref, dva_ref, pm_ref, dp_ref):
        p = pl.program_id(1)
        lane = _iota((Q_BLK, LANES), 1)
        cols = _iota((Q_BLK, K_BLK), 1)
        sub = _iota((8, K_BLK), 0)
        dka_ref[...] = jnp.zeros_like(dka_ref)
        dva_ref[...] = jnp.zeros_like(dva_ref)

        @pl.when(p == 0)
        def _():
            dcr_ref[...] = jnp.zeros_like(dcr_ref)

        def qblock(i, _):
            r0 = pl.multiple_of(i * Q_BLK, Q_BLK)
            rows = _iota((Q_BLK, K_BLK), 0) + r0
            nk = (i + 1) * (Q_BLK // K_BLK)
            q = q_ref[pl.ds(r0, Q_BLK), :]
            do = do_ref[pl.ds(r0, Q_BLK), :]
            lse_b = lse_ref[0, pl.ds(r0, Q_BLK), :]
            cc = cc_ref[0, pl.ds(r0, Q_BLK), :]
            dq_out = jnp.zeros((Q_BLK, LANES), F32)
            for hh in range(2):
                hmask = (lane < HEAD) if hh == 0 else (lane >= HEAD)
                first_lane = 0 if hh == 0 else HEAD
                h = 2 * p + hh
                qs = jnp.where(hmask, q, jnp.zeros_like(q)) * 0.125
                dob = jnp.where(hmask, do, 0.0).astype(BF16)
                lse_h = jnp.sum(jnp.where(lane == first_lane, lse_b, 0.0), axis=1, keepdims=True)
                cq = jnp.sum(jnp.where(lane == h, cc, 0.0), axis=1, keepdims=True)

                def walk1(kb, delta, qs=qs, dob=dob, lse_h=lse_h, cq=cq, h=h):
                    k0 = pl.multiple_of(kb * K_BLK, K_BLK)
                    ck = _colsum(jnp.where(sub == h, cr_ref[0, kb], 0.0))
                    s = _dot_nt(qs, k_ref[pl.ds(k0, K_BLK), :]) + cq - ck
                    pm = jnp.where(cols + k0 <= rows, jnp.exp(s - lse_h), 0.0)
                    dp = _dot_nt(dob, v_ref[pl.ds(k0, K_BLK), :])
                    pm_ref[kb] = pm
                    dp_ref[kb] = dp
                    return delta + jnp.sum(pm * dp, axis=1, keepdims=True)

                delta = lax.fori_loop(0, nk, walk1, jnp.zeros((Q_BLK, 1), F32))

                def walk2(kb, dq_acc, qs=qs, dob=dob, delta=delta, h=h):
                    k0 = pl.multiple_of(kb * K_BLK, K_BLK)
                    pm = pm_ref[kb]
                    ds = pm * (dp_ref[kb] - delta)
                    dcr_ref[0, kb] += jnp.where(sub == h, _colsum(ds), 0.0)
                    dsb = ds.astype(BF16)
                    dka_ref[pl.ds(k0, K_BLK), :] += _dot_tn(dsb, qs)
                    dva_ref[pl.ds(k0, K_BLK), :] += _dot_tn(pm, dob)
                    return dq_acc + _dot(dsb, k_ref[pl.ds(k0, K_BLK), :])

                dq_h = lax.fori_loop(0, nk, walk2, jnp.zeros((Q_BLK, LANES), F32))
                dq_out = jnp.where(hmask, dq_h * 0.125, dq_out)
            dq_ref[pl.ds(r0, Q_BLK), :] = dq_out.astype(BF16)
            return 0

        lax.fori_loop(0, nq, qblock, 0)
        dk_ref[...] = dka_ref[...].astype(BF16)
        dv_ref[...] = dva_ref[...].astype(BF16)

    seq = lambda c0: BS((S, LANES), lambda b, p: (b, c0 + p))
    rowblk = BS((1, nb, 8, LANES), lambda b, p: (b, 0, 0, 0))
    return pl.pallas_call(
        body, name="fox_bwd", grid=(Bl, 3),
        in_specs=[seq(qc), seq(kc), seq(vc), seq(0),
                  BS((1, S, LANES), lambda b, p: (p, b, 0)),
                  BS((1, S, LANES), lambda b, p: (b, 0, 0)),
                  rowblk],
        out_specs=[seq(0), seq(0), seq(0), rowblk],
        out_shape=[jax.ShapeDtypeStruct((T, B_W), BF16)] * 3 + [jax.ShapeDtypeStruct((Bl, nb, 8, LANES), F32)],
        scratch_shapes=[pltpu.VMEM((S, LANES), F32), pltpu.VMEM((S, LANES), F32),
                        pltpu.VMEM((nb, Q_BLK, K_BLK), F32), pltpu.VMEM((nb, Q_BLK, K_BLK), F32)],
        compiler_params=_cp(("arbitrary", "arbitrary")),
    )(proj, proj, proj, dyb, lse, c_col, c_row)


def _gate_bwd(dc_row, fl_row):
    Bl, nb, _, _ = dc_row.shape

    def body(dc_ref, fl_ref, o_ref):
        lane = _iota((8, LANES), 1)

        def blk(jj, carry):
            j = nb - 1 - jj
            r = -dc_ref[0, j]
            for k in (1, 2, 4, 8, 16, 32, 64):
                r = r + jnp.where(lane < LANES - k, pltpu.roll(r, LANES - k, 1), 0.0)
            r = r + carry
            dfl = r * _sigmoid(-fl_ref[0, j])
            o_ref[0, pl.ds(pl.multiple_of(j * LANES, LANES), LANES), :] = jnp.concatenate(
                [dfl, jnp.zeros((LANES - 8, LANES), F32)], axis=0).T
            return jnp.sum(jnp.where(lane == 0, r, 0.0), axis=1, keepdims=True)

        lax.fori_loop(0, nb, blk, jnp.zeros((8, 1), F32))

    rowblk = BS((1, nb, 8, LANES), lambda b: (b, 0, 0, 0))
    return pl.pallas_call(
        body, name="gate_bwd", grid=(Bl,),
        in_specs=[rowblk, rowblk],
        out_specs=BS((1, nb * LANES, LANES), lambda b: (b, 0, 0)),
        out_shape=jax.ShapeDtypeStruct((Bl, nb * LANES, LANES), F32),
        compiler_params=_cp(("arbitrary",)),
    )(dc_row, fl_row)


def _inproj_bwd(dzu, dzv, dq, dk, dv, dqm, dfl, x2d, dx1, g_pre, w_in_p, tm):
    T, D = x2d.shape
    nchunk = P_COLS // 384

    def body(dzu_ref, dzv_ref, dq_ref, dk_ref, dv_ref, dqm_ref, dfl_ref, x_ref, dx1_ref, g_ref, w_ref,
             dp_ref, gx_ref, dg_ref, dbf_ref):
        first = pl.program_id(0) == 0
        dfl = dfl_ref[...]
        dp_ref[:, 0:384] = dzu_ref[...]
        dp_ref[:, 384:768] = dzv_ref[...]
        dp_ref[:, 768:1152] = dq_ref[...]
        dp_ref[:, 1152:1536] = dk_ref[...]
        dp_ref[:, 1536:1920] = dv_ref[...]
        dp_ref[:, 1920:2176] = dqm_ref[...]
        dp_ref[:, 2176:2304] = dfl.astype(BF16)
        dh = jnp.zeros((tm, D), F32)
        for n in range(nchunk):
            cs = slice(n * 384, (n + 1) * 384)
            dh = dh + _dot_nt(dp_ref[:, cs], w_ref[:, cs])
        _, vjp = jax.vjp(_rms, x_ref[...], g_ref[...])
        dxa, dg = vjp(dh)
        gx_ref[...] = dx1_ref[...] + dxa
        _acc(dg_ref, dg, first)
        _acc(dbf_ref, _colsum(dfl), first)

    row = lambda w: BS((tm, w), lambda i: (i, 0))
    return pl.pallas_call(
        body, name="inproj_bwd", grid=(T // tm,),
        in_specs=[row(A_W), row(A_W), row(B_W), row(B_W), row(B_W), row(M_W), row(LANES), row(D), row(D),
                  BS((1, D), lambda i: (0, 0)), BS((D, P_COLS), lambda i: (0, 0))],
        out_specs=[row(P_COLS), row(D), BS((1, D), lambda i: (0, 0)), BS((1, LANES), lambda i: (0, 0))],
        out_shape=[jax.ShapeDtypeStruct((T, P_COLS), BF16), jax.ShapeDtypeStruct((T, D), F32),
                   jax.ShapeDtypeStruct((1, D), F32), jax.ShapeDtypeStruct((1, LANES), F32)],
        compiler_params=_cp(("arbitrary",)),
    )(dzu, dzv, dq, dk, dv, dqm, dfl, x2d, dx1, g_pre, w_in_p)


def _local_step(x, mem, target, W, P):
    Bl, S, D = x.shape
    T = Bl * S
    tm = min(512, T)
    x2d = x.reshape(T, D)
    t2d = target.reshape(T, D)
    vec = lambda a: a.reshape(1, -1)
    bf_row = jnp.pad(P["b_f"].reshape(1, -1), ((0, 0), (0, LANES - N_FOX_HEADS)))
    tril = jnp.tril(jnp.ones((CHUNK, CHUNK), bool))
    ws_tril = jnp.where(tril[None], P["w_s"][0], 0.0).astype(BF16)
    bs_full = jnp.repeat(P["b_s"][0].T, HEAD, axis=1)
    g_pre, g_sgu = vec(P["g_pre_mix"]), vec(P["g_sgu"])
    ga, gb, gm = vec(P["g_out_a"]), vec(P["g_out_b"]), vec(P["g_out_m"])
    g_mem, g_post, g_pre2, g_post2 = vec(P["g_mem"]), vec(P["g_post_mix"]), vec(P["g_pre_ffn"]), vec(P["g_post_ffn"])

    h, proj, flog = _inproj_fwd(x2d, g_pre, W["w_in"], tm)
    c_col, c_row, fl_row = _gate_fwd(flog.reshape(Bl, S, LANES), bf_row)
    ya = _sgu_fwd(proj, g_sgu, ws_tril, bs_full, tm)
    yb, lse = _fox_fwd(proj, c_col, c_row, Bl, S)
    memn, kv = _memkv_fwd(mem, g_mem, W["w_mem_kv"])
    ym = _memattn_fwd(proj, kv, Bl, S, min(512, S))
    y, o, x1, h2 = _outproj_fwd(ya, yb, ym, x2d, ga, gb, gm, g_post, g_pre2, W["w_out"], tm)
    gs, us, dff, dx2, dg_post2, loss = _ffn_fwd(h2, x1, t2d, W["w_gate"], W["w_up"], W["w_down"], g_post2, tm)

    a, dgs, dus, dx1, dg_pre2 = _ffn_bwd(dff, gs, us, x1, dx2, W["w_gate"], W["w_up"], W["w_down"], g_pre2, tm)
    d_w_down = _mm_tn(a, dff, "dw_down", 1024)
    d_w_gate = _mm_tn(h2, dgs, "dw_gate", 1024)
    d_w_up = _mm_tn(h2, dus, "dw_up", 1024)
    do, dya, dyb, dym, dga, dgb, dgm, dg_post = _outproj_bwd(dx1, o, ya, yb, ym, ga, gb, gm, g_post, W["w_out"], tm)
    d_w_out = _mm_tn(y, do, "dw_out", 1024)[0]
    dzu, dzv, dws, dbs_cols, dg_sgu = _sgu_bwd(proj, dya, g_sgu, ws_tril, bs_full, tm)
    dqm, dkm, dvm = _memattn_bwd(proj, kv, dym, Bl, S, min(512, S))
    d_w_kv, dg_mem = _memkv_bwd(dkm, dvm, memn, mem, g_mem, W["w_mem_kv"])
    dq, dk, dv, dc_row = _fox_bwd(proj, dyb, lse, c_col, c_row, Bl, S)
    dfl = _gate_bwd(dc_row, fl_row).reshape(T, LANES)
    dproj, grad_x, dg_pre, dbf = _inproj_bwd(dzu, dzv, dq, dk, dv, dqm, dfl, x2d, dx1, g_pre, W["w_in"], tm)
    d_w_in = _mm_tn(h, dproj, "dw_in", 1024)[0]

    big = {"w_in": d_w_in, "w_mem_kv": d_w_kv, "w_out": d_w_out, "w_gate": d_w_gate, "w_up": d_w_up,
           "w_down": d_w_down}
    small = {"g_pre_mix": dg_pre, "b_f": dbf[:, :N_FOX_HEADS], "g_sgu": dg_sgu, "w_s": dws, "b_s": dbs_cols[:, :N_FOX_HEADS].T,
             "g_out_a": dga, "g_out_b": dgb, "g_out_m": dgm, "g_mem": dg_mem, "g_post_mix": dg_post,
             "g_pre_ffn": dg_pre2, "g_post_ffn": dg_post2, "loss": loss[:, :1]}
    return grad_x.reshape(Bl, S, D), big, small


def _place():
    return lax.axis_index("x"), lax.axis_index("y"), lax.axis_index("c")


def _exchange(srcs, own_full, name):
    n = len(srcs)
    nd2d = 3 if own_full else 4

    def body(*refs):
        src, dst = refs[:n], refs[n:2 * n]
        lsem, isend, irecv, dsend, drecv = refs[2 * n:]
        x, y, c = _place()
        oc = 1 - c
        s_me = 2 * x + y
        sib = (x, y, oc)
        chips = [(1 - x, y), (x, 1 - y), (1 - x, 1 - y)]

        def remote(a, b, ssem, rsem, dev):
            return pltpu.make_async_remote_copy(src_ref=a, dst_ref=b, send_sem=ssem, recv_sem=rsem,
                                                device_id=dev, device_id_type=MESH)

        sends, local = [], []
        for w in range(n):
            if own_full:
                loc = pltpu.make_async_copy(src[w], dst[w].at[s_me], lsem.at[w])
            else:
                loc = pltpu.make_async_copy(src[w].at[s_me], dst[w].at[s_me, c], lsem.at[w])
            loc.start()
            local.append(loc)
        for w in range(n):
            for j, (cx, cy) in enumerate(chips):
                half = src[w].at[c] if own_full else src[w].at[2 * cx + cy]
                cp = remote(half, dst[w].at[s_me, c], isend.at[w, j], irecv.at[w, j], (cx, cy, c))
                cp.start()
                sends.append(cp)
            if not own_full:
                cp = remote(src[w].at[s_me], dst[w].at[s_me, c], dsend.at[w, 3], drecv.at[w, 3], sib)
                cp.start()
                sends.append(cp)
        for w in range(n):
            for j, (cx, cy) in enumerate(chips):
                landed = dst[w].at[2 * cx + cy, c]
                remote(landed, landed, isend.at[w, j], irecv.at[w, j], (cx, cy, c)).wait_recv()
                cp = remote(landed, landed, dsend.at[w, j], drecv.at[w, j], sib)
                cp.start()
                sends.append(cp)
        for w in range(n):
            for j, (cx, cy) in enumerate(chips):
                landed = dst[w].at[2 * cx + cy, oc]
                remote(landed, landed, dsend.at[w, j], drecv.at[w, j], sib).wait_recv()
            if not own_full:
                landed = dst[w].at[s_me, oc]
                remote(landed, landed, dsend.at[w, 3], drecv.at[w, 3], sib).wait_recv()
        for cp in sends:
            cp.wait_send()
        for loc in local:
            loc.wait()

    out_shape = [jax.ShapeDtypeStruct((4, 2) + s.shape[1:], s.dtype) for s in srcs]
    return pl.pallas_call(
        body, name=name, in_specs=[ANY] * n, out_specs=[ANY] * n, out_shape=out_shape,
        scratch_shapes=[pltpu.SemaphoreType.DMA((n,)), pltpu.SemaphoreType.DMA((n, 3)), pltpu.SemaphoreType.DMA((n, 3)),
                        pltpu.SemaphoreType.DMA((n, nd2d)), pltpu.SemaphoreType.DMA((n, nd2d))],
    )(*srcs)


def _sibling_swap(grads):
    n = len(grads)

    def body(*refs):
        g, mine, theirs = refs[:n], refs[n:2 * n], refs[2 * n:3 * n]
        lsem, ssem, rsem = refs[3 * n:]
        x, y, c = _place()
        cps = []
        for w in range(n):
            loc = pltpu.make_async_copy(g[w].at[:, c], mine[w], lsem.at[w])
            loc.start()
            cp = pltpu.make_async_remote_copy(src_ref=g[w].at[:, 1 - c], dst_ref=theirs[w], send_sem=ssem.at[w],
                                              recv_sem=rsem.at[w], device_id=(x, y, 1 - c), device_id_type=MESH)
            cp.start()
            cps.append((loc, cp))
        for loc, cp in cps:
            cp.wait()
            loc.wait()

    half = [jax.ShapeDtypeStruct((4,) + g.shape[2:], g.dtype) for g in grads]
    outs = pl.pallas_call(
        body, name="sibling_swap", in_specs=[ANY] * n, out_specs=[ANY] * (2 * n), out_shape=half + half,
        scratch_shapes=[pltpu.SemaphoreType.DMA((n,)), pltpu.SemaphoreType.DMA((n,)), pltpu.SemaphoreType.DMA((n,))],
    )(*grads)
    return outs[:n], outs[n:]


def _add_pair(a, b, name):
    _, hr, C = a.shape

    def body(a_ref, b_ref, o_ref):
        o_ref[...] = (a_ref[...] + b_ref[...]).astype(BF16)

    blk = BS((1, hr, C), lambda s: (s, 0, 0))
    return pl.pallas_call(body, name=name, grid=(4,), in_specs=[blk, blk], out_specs=blk,
                          out_shape=jax.ShapeDtypeStruct(a.shape, BF16), compiler_params=_cp(("arbitrary",)))(a, b)


def _sum_chips(r, name):
    _, _, hr, C = r.shape

    def body(r_ref, o_ref):
        o_ref[...] = ((r_ref[0, 0].astype(F32) + r_ref[1, 0].astype(F32)) + r_ref[2, 0].astype(F32)) + r_ref[3, 0].astype(F32)

    return pl.pallas_call(body, name=name, grid=(2,), in_specs=[BS((4, 1, hr, C), lambda h: (0, h, 0, 0))],
                          out_specs=BS((hr, C), lambda h: (h, 0)), out_shape=jax.ShapeDtypeStruct((2 * hr, C), F32),
                          compiler_params=_cp(("arbitrary",)))(r)


def _small_allreduce(part):
    R = part.shape[0]
    rs = R // 8
    masks = [(mx, my, mc) for mx in (0, 1) for my in (0, 1) for mc in (0, 1)][1:]

    def body(p_ref, o_ref, buf_ref, s1, r1, s2, r2):
        x, y, c = _place()
        d = 4 * x + 2 * y + c
        mine = pl.ds(pl.multiple_of(d * rs, 8), rs)
        peers = [((x + mx) % 2, (y + my) % 2, (c + mc) % 2) for mx, my, mc in masks]
        first, second = [], []
        for k, (px, py, pc) in enumerate(peers):
            theirs = pl.ds(pl.multiple_of((4 * px + 2 * py + pc) * rs, 8), rs)
            cp = pltpu.make_async_remote_copy(src_ref=p_ref.at[theirs, :], dst_ref=buf_ref.at[d], send_sem=s1.at[k],
                                              recv_sem=r1.at[k], device_id=(px, py, pc), device_id_type=MESH)
            cp.start()
            first.append(cp)
        buf_ref[d] = p_ref[mine, :]
        for k, (px, py, pc) in enumerate(peers):
            slot = buf_ref.at[4 * px + 2 * py + pc]
            pltpu.make_async_remote_copy(src_ref=slot, dst_ref=slot, send_sem=s1.at[k], recv_sem=r1.at[k],
                                         device_id=(px, py, pc), device_id_type=MESH).wait_recv()
        total = buf_ref[0]
        for k in range(1, 8):
            total = total + buf_ref[k]
        o_ref[mine, :] = total
        for k, (px, py, pc) in enumerate(peers):
            cp = pltpu.make_async_remote_copy(src_ref=o_ref.at[mine, :], dst_ref=o_ref.at[mine, :], send_sem=s2.at[k],
                                              recv_sem=r2.at[k], device_id=(px, py, pc), device_id_type=MESH)
            cp.start()
            second.append(cp)
        for k, (px, py, pc) in enumerate(peers):
            rows = o_ref.at[pl.ds(pl.multiple_of((4 * px + 2 * py + pc) * rs, 8), rs), :]
            pltpu.make_async_remote_copy(src_ref=rows, dst_ref=rows, send_sem=s2.at[k], recv_sem=r2.at[k],
                                         device_id=(px, py, pc), device_id_type=MESH).wait_recv()
        for cp in first + second:
            cp.wait_send()

    vm = pl.BlockSpec(memory_space=pltpu.VMEM)
    return pl.pallas_call(
        body, name="small_allreduce", in_specs=[vm], out_specs=vm, out_shape=jax.ShapeDtypeStruct(part.shape, F32),
        scratch_shapes=[pltpu.VMEM((8, rs, LANES), F32)] + [pltpu.SemaphoreType.DMA((7,))] * 4,
    )(part)


def _adamw(w, g, m, v, name):
    R, C = w.shape
    tr = R if R * C * 4 <= (1 << 21) else R // 2
    if tr % 8:
        tr = R
    c1 = 1.0 / (1.0 - ADAM_B1 ** ADAM_STEP)
    c2 = 1.0 / (1.0 - ADAM_B2 ** ADAM_STEP)

    def body(w_ref, g_ref, m_ref, v_ref, d_ref, mo_ref, vo_ref):
        g_ = g_ref[...]
        m_ = ADAM_B1 * m_ref[...] + (1.0 - ADAM_B1) * g_
        v_ = ADAM_B2 * v_ref[...] + (1.0 - ADAM_B2) * (g_ * g_)
        mo_ref[...] = m_
        vo_ref[...] = v_
        d_ref[...] = -ADAM_LR * ((m_ * c1) / (jnp.sqrt(v_ * c2) + ADAM_EPS) + ADAM_WD * w_ref[...])

    blk = BS((tr, C), lambda i: (i, 0))
    return pl.pallas_call(body, name=name, grid=(R // tr,), in_specs=[blk] * 4, out_specs=[blk] * 3,
                          out_shape=[jax.ShapeDtypeStruct((R, C), F32)] * 3, compiler_params=_cp(("arbitrary",)))(w, g, m, v)


SMALL = ("g_pre_mix", "b_f", "g_sgu", "w_s", "b_s", "g_out_a", "g_out_b", "g_out_m", "g_mem", "g_post_mix",
         "g_pre_ffn", "g_post_ffn")
BIG = ("w_in", "w_mem_kv", "w_out", "w_gate", "w_up", "w_down")
WEIGHTS = ("g_pre_mix", "w_in", "b_f", "g_sgu", "w_s", "b_s", "g_out_a", "g_out_b", "g_out_m", "g_mem", "w_mem_kv",
           "w_out", "g_post_mix", "g_pre_ffn", "w_gate", "w_up", "w_down", "g_post_ffn")


def _rows_of(n):
    return -(-n // (8 * LANES)) * 8


def _pack(parts):
    tiles = []
    for a in parts:
        flat = a.reshape(-1).astype(F32)
        rows = _rows_of(flat.shape[0])
        tiles.append(jnp.pad(flat, (0, rows * LANES - flat.shape[0])).reshape(rows, LANES))
    total = sum(t.shape[0] for t in tiles)
    pad = -total % 64
    if pad:
        tiles.append(jnp.zeros((pad, LANES), F32))
    return jnp.concatenate(tiles, axis=0)


def _unpack(packed, shapes):
    out, r = [], 0
    for shp in shapes:
        n = 1
        for s in shp:
            n *= s
        rows = _rows_of(n)
        out.append(packed[r:r + rows].reshape(-1)[:n].reshape(shp))
        r += rows
    return out


def kernel(x, mem, g_pre_mix, w_in, b_f, g_sgu, w_s, b_s, g_out_a, g_out_b, g_out_m, g_mem, w_mem_kv, w_out, g_post_mix, g_pre_ffn, w_gate, w_up, w_down, g_post_ffn, loss_target, m_g_pre_mix, m_w_in, m_b_f, m_g_sgu, m_w_s, m_b_s, m_g_out_a, m_g_out_b, m_g_out_m, m_g_mem, m_w_mem_kv, m_w_out, m_g_post_mix, m_g_pre_ffn, m_w_gate, m_w_up, m_w_down, m_g_post_ffn, v_g_pre_mix, v_w_in, v_b_f, v_g_sgu, v_w_s, v_b_s, v_g_out_a, v_g_out_b, v_g_out_m, v_g_mem, v_w_mem_kv, v_w_out, v_g_post_mix, v_g_pre_ffn, v_w_gate, v_w_up, v_w_down, v_g_post_ffn):
    Wt = dict(g_pre_mix=g_pre_mix, w_in=w_in, b_f=b_f, g_sgu=g_sgu, w_s=w_s, b_s=b_s, g_out_a=g_out_a, g_out_b=g_out_b,
              g_out_m=g_out_m, g_mem=g_mem, w_mem_kv=w_mem_kv, w_out=w_out, g_post_mix=g_post_mix, g_pre_ffn=g_pre_ffn,
              w_gate=w_gate, w_up=w_up, w_down=w_down, g_post_ffn=g_post_ffn)
    Mo = dict(g_pre_mix=m_g_pre_mix, w_in=m_w_in, b_f=m_b_f, g_sgu=m_g_sgu, w_s=m_w_s, b_s=m_b_s, g_out_a=m_g_out_a,
              g_out_b=m_g_out_b, g_out_m=m_g_out_m, g_mem=m_g_mem, w_mem_kv=m_w_mem_kv, w_out=m_w_out,
              g_post_mix=m_g_post_mix, g_pre_ffn=m_g_pre_ffn, w_gate=m_w_gate, w_up=m_w_up, w_down=m_w_down,
              g_post_ffn=m_g_post_ffn)
    Vo = dict(g_pre_mix=v_g_pre_mix, w_in=v_w_in, b_f=v_b_f, g_sgu=v_g_sgu, w_s=v_w_s, b_s=v_b_s, g_out_a=v_g_out_a,
              g_out_b=v_g_out_b, g_out_m=v_g_out_m, g_mem=v_g_mem, w_mem_kv=v_w_mem_kv, w_out=v_w_out,
              g_post_mix=v_g_post_mix, g_pre_ffn=v_g_pre_ffn, w_gate=v_w_gate, w_up=v_w_up, w_down=v_w_down,
              g_post_ffn=v_g_post_ffn)

    def regroup(w):
        return jnp.concatenate([w[:, :1920], w[:, 1926:IN_COLS], w[:, 1920:1926],
                                jnp.zeros((w.shape[0], P_COLS - IN_COLS), w.dtype)], axis=1)

    def ungroup(g):
        return jnp.concatenate([g[:, :1920], g[:, P_MAIN:P_MAIN + N_FOX_HEADS], g[:, 1920:P_MAIN]], axis=1)

    shards = {n: (regroup(Wt[n][0]) if n == "w_in" else Wt[n][0]) for n in BIG}
    srcs = [shards[n].astype(BF16).reshape(2, shards[n].shape[0] // 2, shards[n].shape[1]) for n in BIG]
    fulls = _exchange(srcs, True, "gather_weights")
    W = {}
    for n, f in zip(BIG, fulls):
        _, _, hr, C = f.shape
        W[n] = f.reshape(4, 2 * hr, C) if n in ("w_gate", "w_up", "w_down") else f.reshape(8 * hr, C)

    P = {n: Wt[n] for n in SMALL}
    grad_x, big, small = _local_step(x, mem, loss_target, W, P)

    g4 = []
    for n in BIG:
        g = big[n]
        C = g.shape[-1]
        g4.append(g.reshape(4, 2, -1, C))
    mine, theirs = _sibling_swap(g4)
    chip_sums = [_add_pair(a, b, "chip_sum_" + n) for n, a, b in zip(BIG, mine, theirs)]
    landed = _exchange(chip_sums, False, "scatter_grads")
    grads, deltas, new_m, new_v = {}, {}, {}, {}
    for n, r in zip(BIG, landed):
        g = _sum_chips(r, "sum_chips_" + n)
        if n == "w_in":
            g = ungroup(g)
        d, m1, v1 = _adamw(Wt[n][0], g, Mo[n][0], Vo[n][0], "adamw_" + n)
        grads[n], deltas[n], new_m[n], new_v[n] = g[None], d[None], m1[None], v1[None]

    total = _small_allreduce(_pack([small[n] for n in SMALL] + [small["loss"]]))
    slot = [jnp.zeros((1, 1), F32)]
    shapes = [Wt[n].shape for n in SMALL] + [(1, 1)]
    d, m1, v1 = _adamw(_pack([Wt[n] for n in SMALL] + slot), total, _pack([Mo[n] for n in SMALL] + slot),
                       _pack([Vo[n] for n in SMALL] + slot), "adamw_small")
    g_s, d_s, m_s, v_s = _unpack(total, shapes), _unpack(d, shapes), _unpack(m1, shapes), _unpack(v1, shapes)
    for k, n in enumerate(SMALL):
        grads[n], deltas[n], new_m[n], new_v[n] = g_s[k], d_s[k], m_s[k], v_s[k]
    loss = g_s[-1][0, 0]

    return (loss, grad_x, *[grads[n] for n in WEIGHTS], *[deltas[n] for n in WEIGHTS],
            *[new_m[n] for n in WEIGHTS], *[new_v[n] for n in WEIGHTS])
```

```python
import functools

import jax
import jax.numpy as jnp
from jax import lax
from jax.experimental import pallas as pl
from jax.experimental.pallas import tpu as pltpu

F32 = jnp.float32
BF16 = jnp.bfloat16
EPS = 1e-6
NEG = -1e30
HEAD = 64
A_W, B_W, M_W = 384, 384, 256
N_FOX_HEADS = 6
CHUNK = 128
IN_COLS = 2 * A_W + 3 * B_W + N_FOX_HEADS + M_W
P_MAIN = 2 * A_W + 3 * B_W + M_W
P_COLS = P_MAIN + 128
LANES = 128
Q_BLK, K_BLK = 256, 128
ADAM_LR, ADAM_B1, ADAM_B2, ADAM_EPS, ADAM_WD, ADAM_STEP = 0.001, 0.9, 0.999, 1e-08, 0.01, 10
VMEM_LIMIT = 56 * 1024 * 1024
MESH = pl.DeviceIdType.MESH
ANY = pl.BlockSpec(memory_space=pl.ANY)
BS = pl.BlockSpec


def _cp(sem=None):
    return pltpu.CompilerParams(dimension_semantics=sem, vmem_limit_bytes=VMEM_LIMIT)


def _iota(shape, dim):
    return lax.broadcasted_iota(jnp.int32, shape, dim)


def _dot(a, b):
    return jnp.dot(a.astype(BF16), b.astype(BF16), preferred_element_type=F32)


def _dot_nt(a, b):
    return lax.dot_general(a.astype(BF16), b.astype(BF16), (((1,), (1,)), ((), ())), preferred_element_type=F32)


def _dot_tn(a, b):
    return lax.dot_general(a.astype(BF16), b.astype(BF16), (((0,), (0,)), ((), ())), preferred_element_type=F32)


def _rms(x, g):
    return x * lax.rsqrt(jnp.mean(x * x, axis=-1, keepdims=True) + EPS) * g


def _gelu(x):
    return 0.5 * x * (1.0 + jnp.tanh(0.7978845608028654 * (x + 0.044715 * (x * x * x))))


def _sigmoid(x):
    return 1.0 / (1.0 + jnp.exp(-x))


def _silu_mul(g, u):
    return g * _sigmoid(g) * u


def _logsig(x):
    return jnp.minimum(x, 0.0) - jnp.log(1.0 + jnp.exp(-jnp.abs(x)))


def _colsum(x):
    return jnp.sum(x, axis=0, keepdims=True)


def _acc(ref, val, first):
    @pl.when(first)
    def _():
        ref[...] = val

    @pl.when(jnp.logical_not(first))
    def _():
        ref[...] += val


def _inproj_fwd(x2d, g_pre, w_in_p, tm):
    T, D = x2d.shape
    nchunk = P_COLS // 384

    def body(x_ref, g_ref, w_ref, h_ref, proj_ref, fl_ref):
        h = _rms(x_ref[...], g_ref[...]).astype(BF16)
        h_ref[...] = h
        for n in range(nchunk):
            r = jnp.dot(h, w_ref[:, n * 384:(n + 1) * 384], preferred_element_type=F32)
            if n < nchunk - 1:
                proj_ref[:, n * 384:(n + 1) * 384] = r.astype(BF16)
            else:
                proj_ref[:, n * 384:n * 384 + 256] = r[:, :256].astype(BF16)
                fl_ref[...] = r[:, 256:384]

    return pl.pallas_call(
        body, name="inproj_fwd", grid=(T // tm,),
        in_specs=[BS((tm, D), lambda i: (i, 0)), BS((1, D), lambda i: (0, 0)), BS((D, P_COLS), lambda i: (0, 0))],
        out_specs=[BS((tm, D), lambda i: (i, 0)), BS((tm, P_MAIN), lambda i: (i, 0)), BS((tm, LANES), lambda i: (i, 0))],
        out_shape=[jax.ShapeDtypeStruct((T, D), BF16), jax.ShapeDtypeStruct((T, P_MAIN), BF16),
                   jax.ShapeDtypeStruct((T, LANES), F32)],
        compiler_params=_cp(("arbitrary",)),
    )(x2d, g_pre, w_in_p)


def _gate_fwd(flog3, bf_row):
    Bl, S, _ = flog3.shape
    nb = S // LANES

    def body(f_ref, b_ref, cc_ref, cr_ref, fr_ref):
        row = _iota((LANES, LANES), 0)

        def blk(j, carry):
            r0 = pl.multiple_of(j * LANES, LANES)
            fl = f_ref[0, pl.ds(r0, LANES), :] + b_ref[...]
            fr_ref[0, j] = fl.T[0:8, :]
            c = _logsig(fl)
            for k in (1, 2, 4, 8, 16, 32, 64):
                c = c + jnp.where(row >= k, pltpu.roll(c, k, 0), 0.0)
            c = c + carry
            cc_ref[0, pl.ds(r0, LANES), :] = c
            cr_ref[0, j] = c.T[0:8, :]
            return _colsum(jnp.where(row == LANES - 1, c, 0.0))

        lax.fori_loop(0, nb, blk, jnp.zeros((1, LANES), F32))

    rowblk = BS((1, nb, 8, LANES), lambda b: (b, 0, 0, 0))
    return pl.pallas_call(
        body, name="gate_fwd", grid=(Bl,),
        in_specs=[BS((1, S, LANES), lambda b: (b, 0, 0)), BS((1, LANES), lambda b: (0, 0))],
        out_specs=[BS((1, S, LANES), lambda b: (b, 0, 0)), rowblk, rowblk],
        out_shape=[jax.ShapeDtypeStruct((Bl, S, LANES), F32), jax.ShapeDtypeStruct((Bl, nb, 8, LANES), F32),
                   jax.ShapeDtypeStruct((Bl, nb, 8, LANES), F32)],
        compiler_params=_cp(("arbitrary",)),
    )(flog3, bf_row)


def _sgu_pre(zu, zv, g_sgu):
    return _gelu(zu), _rms(_gelu(zv), g_sgu)


def _sgu_fwd(proj, g_sgu, ws_tril, bs_full, tm):
    T = proj.shape[0]
    nch = tm // CHUNK

    def body(zu_ref, zv_ref, g_ref, ws_ref, b_ref, ya_ref):
        lane = _iota((CHUNK, LANES), 1)
        u, vn = _sgu_pre(zu_ref[...].astype(F32), zv_ref[...].astype(F32), g_ref[...])
        vn = vn.astype(BF16)
        for c in range(nch):
            rs = slice(c * CHUNK, (c + 1) * CHUNK)
            for j in range(3):
                cs = slice(j * LANES, (j + 1) * LANES)
                vp = vn[rs, cs]
                z = jnp.where(lane < HEAD, _dot(ws_ref[2 * j], vp), _dot(ws_ref[2 * j + 1], vp)) + b_ref[:, cs]
                ya_ref[rs, cs] = u[rs, cs] * z

    return pl.pallas_call(
        body, name="sgu_fwd", grid=(T // tm,),
        in_specs=[BS((tm, A_W), lambda i: (i, 0)), BS((tm, A_W), lambda i: (i, 1)), BS((1, A_W), lambda i: (0, 0)),
                  BS((6, CHUNK, CHUNK), lambda i: (0, 0, 0)), BS((CHUNK, A_W), lambda i: (0, 0))],
        out_specs=BS((tm, A_W), lambda i: (i, 0)),
        out_shape=jax.ShapeDtypeStruct((T, A_W), F32),
        compiler_params=_cp(("arbitrary",)),
    )(proj, proj, g_sgu, ws_tril, bs_full)


def _fox_fwd(proj, c_col, c_row, Bl, S):
    T = Bl * S
    nq = S // Q_BLK
    nb = S // K_BLK
    qc, kc, vc = 768 // LANES, 1152 // LANES, 1536 // LANES

    def body(q_ref, k_ref, v_ref, cc_ref, cr_ref, o_ref, lse_ref):
        p = pl.program_id(1)
        i = pl.program_id(2)
        lane = _iota((Q_BLK, LANES), 1)
        rows = _iota((Q_BLK, K_BLK), 0) + i * Q_BLK
        cols = _iota((Q_BLK, K_BLK), 1)
        sub = _iota((8, K_BLK), 0)
        q = q_ref[...]
        cc = cc_ref[0]
        o_out = jnp.zeros((Q_BLK, LANES), F32)
        lse_out = jnp.zeros((Q_BLK, LANES), F32)
        for hh in range(2):
            hmask = (lane < HEAD) if hh == 0 else (lane >= HEAD)
            h = 2 * p + hh
            qs = jnp.where(hmask, q, jnp.zeros_like(q)) * 0.125
            cq = jnp.sum(jnp.where(lane == h, cc, 0.0), axis=1, keepdims=True)

            def step(kb, carry, qs=qs, cq=cq, h=h):
                m, l, acc = carry
                k0 = pl.multiple_of(kb * K_BLK, K_BLK)
                kblk = k_ref[pl.ds(k0, K_BLK), :]
                vblk = v_ref[pl.ds(k0, K_BLK), :]
                ck = _colsum(jnp.where(sub == h, cr_ref[0, kb], 0.0))
                s = _dot_nt(qs, kblk) + cq - ck
                s = jnp.where(cols + k0 <= rows, s, NEG)
                m_new = jnp.maximum(m, jnp.max(s, axis=1, keepdims=True))
                alpha = jnp.exp(m - m_new)
                pe = jnp.exp(s - m_new)
                l = alpha * l + jnp.sum(pe, axis=1, keepdims=True)
                acc = alpha * acc + _dot(pe, vblk)
                return m_new, l, acc

            m, l, acc = lax.fori_loop(
                0, (i + 1) * (Q_BLK // K_BLK), step,
                (jnp.full((Q_BLK, 1), NEG, F32), jnp.zeros((Q_BLK, 1), F32), jnp.zeros((Q_BLK, LANES), F32)))
            o_out = jnp.where(hmask, acc / l, o_out)
            lse_out = jnp.where(hmask, m + jnp.log(l), lse_out)
        o_ref[...] = o_out
        lse_ref[0] = lse_out

    return pl.pallas_call(
        body, name="fox_fwd", grid=(Bl, 3, nq),
        in_specs=[BS((Q_BLK, LANES), lambda b, p, i: (b * nq + i, qc + p)),
                  BS((S, LANES), lambda b, p, i: (b, kc + p)),
                  BS((S, LANES), lambda b, p, i: (b, vc + p)),
                  BS((1, Q_BLK, LANES), lambda b, p, i: (b, i, 0)),
                  BS((1, nb, 8, LANES), lambda b, p, i: (b, 0, 0, 0))],
        out_specs=[BS((Q_BLK, LANES), lambda b, p, i: (b * nq + i, p)),
                   BS((1, Q_BLK, LANES), lambda b, p, i: (p, b * nq + i, 0))],
        out_shape=[jax.ShapeDtypeStruct((T, B_W), F32), jax.ShapeDtypeStruct((3, T, LANES), F32)],
        compiler_params=_cp(("arbitrary", "arbitrary", "arbitrary")),
    )(proj, proj, proj, c_col, c_row)


def _memkv_fwd(mem, g_mem, w_kv):
    Bl, Mt, D = mem.shape

    def body(m_ref, g_ref, w_ref, mn_ref, kv_ref):
        mn = _rms(m_ref[0], g_ref[...]).astype(BF16)
        mn_ref[0] = mn
        kv_ref[0] = jnp.dot(mn, w_ref[...], preferred_element_type=F32).astype(BF16)

    return pl.pallas_call(
        body, name="memkv_fwd", grid=(Bl,),
        in_specs=[BS((1, Mt, D), lambda b: (b, 0, 0)), BS((1, D), lambda b: (0, 0)), BS((D, 2 * M_W), lambda b: (0, 0))],
        out_specs=[BS((1, Mt, D), lambda b: (b, 0, 0)), BS((1, Mt, 2 * M_W), lambda b: (b, 0, 0))],
        out_shape=[jax.ShapeDtypeStruct((Bl, Mt, D), BF16), jax.ShapeDtypeStruct((Bl, Mt, 2 * M_W), BF16)],
        compiler_params=_cp(("arbitrary",)),
    )(mem, g_mem, w_kv)


def _memattn_fwd(proj, kv, Bl, S, tq):
    T = Bl * S
    nq = S // tq
    Mt = kv.shape[1]
    qc = 1920 // LANES

    def body(q_ref, km_ref, vm_ref, o_ref):
        lane = _iota((tq, LANES), 1)
        q = q_ref[...]
        out = jnp.zeros((tq, LANES), F32)
        for hh in range(2):
            hmask = (lane < HEAD) if hh == 0 else (lane >= HEAD)
            qs = jnp.where(hmask, q, jnp.zeros_like(q)) * 0.125
            s = _dot_nt(qs, km_ref[0])
            pe = jnp.exp(s - jnp.max(s, axis=1, keepdims=True))
            pn = pe / jnp.sum(pe, axis=1, keepdims=True)
            out = jnp.where(hmask, _dot(pn, vm_ref[0]), out)
        o_ref[...] = out

    return pl.pallas_call(
        body, name="memattn_fwd", grid=(Bl, 2, nq),
        in_specs=[BS((tq, LANES), lambda b, p, i: (b * nq + i, qc + p)),
                  BS((1, Mt, LANES), lambda b, p, i: (b, 0, p)),
                  BS((1, Mt, LANES), lambda b, p, i: (b, 0, 2 + p))],
        out_specs=BS((tq, LANES), lambda b, p, i: (b * nq + i, p)),
        out_shape=jax.ShapeDtypeStruct((T, M_W), F32),
        compiler_params=_cp(("arbitrary", "arbitrary", "arbitrary")),
    )(proj, kv, kv)


def _mix_norms(ya, yb, ym, ga, gb, gm):
    return _rms(ya, ga), _rms(yb, gb), _rms(ym, gm)


def _outproj_fwd(ya, yb, ym, x2d, ga, gb, gm, g_post, g_pre2, w_out, tm):
    T, D = x2d.shape

    def body(ya_ref, yb_ref, ym_ref, x_ref, ga_ref, gb_ref, gm_ref, gp_ref, g2_ref, w_ref,
             y_ref, o_ref, x1_ref, h2_ref):
        na, nb_, nm = _mix_norms(ya_ref[...], yb_ref[...], ym_ref[...], ga_ref[...], gb_ref[...], gm_ref[...])
        y_ref[:, 0:A_W] = na.astype(BF16)
        y_ref[:, A_W:A_W + B_W] = nb_.astype(BF16)
        y_ref[:, A_W + B_W:] = nm.astype(BF16)
        o = jnp.dot(y_ref[...], w_ref[...], preferred_element_type=F32)
        o_ref[...] = o
        x1 = x_ref[...] + _rms(o, gp_ref[...])
        x1_ref[...] = x1
        h2_ref[...] = _rms(x1, g2_ref[...]).astype(BF16)

    row = lambda w: BS((tm, w), lambda i: (i, 0))
    vec = lambda w: BS((1, w), lambda i: (0, 0))
    return pl.pallas_call(
        body, name="outproj_fwd", grid=(T // tm,),
        in_specs=[row(A_W), row(B_W), row(M_W), row(D), vec(A_W), vec(B_W), vec(M_W), vec(D), vec(D),
                  BS((A_W + B_W + M_W, D), lambda i: (0, 0))],
        out_specs=[row(A_W + B_W + M_W), row(D), row(D), row(D)],
        out_shape=[jax.ShapeDtypeStruct((T, A_W + B_W + M_W), BF16), jax.ShapeDtypeStruct((T, D), F32),
                   jax.ShapeDtypeStruct((T, D), F32), jax.ShapeDtypeStruct((T, D), BF16)],
        compiler_params=_cp(("arbitrary",)),
    )(ya, yb, ym, x2d, ga, gb, gm, g_post, g_pre2, w_out)


def _ffn_fwd(h2, x1, target, wg, wu, wd, g_post, tm):
    T, D = x1.shape
    ns, _, F = wg.shape

    def body(h_ref, x1_ref, t_ref, wg_ref, wu_ref, wd_ref, gp_ref,
             gs_ref, us_ref, dff_ref, dx2_ref, dgp_ref, loss_ref, acc_ref):
        i = pl.program_id(0)
        j = pl.program_id(1)
        h = h_ref[...]
        g = jnp.dot(h, wg_ref[0], preferred_element_type=F32)
        u = jnp.dot(h, wu_ref[0], preferred_element_type=F32)
        gs_ref[0] = g.astype(BF16)
        us_ref[0] = u.astype(BF16)
        part = _dot(_silu_mul(g, u), wd_ref[0])
        _acc(acc_ref, part, j == 0)

        @pl.when(j == ns - 1)
        def _():
            normed, vjp = jax.vjp(_rms, acc_ref[...], gp_ref[...])
            diff = x1_ref[...] + normed - t_ref[...]
            dx2 = diff * (1.0 / D)
            dff, dgp = vjp(dx2)
            dx2_ref[...] = dx2
            dff_ref[...] = dff.astype(BF16)
            lpart = jnp.sum(_colsum(diff * diff), axis=1, keepdims=True) * (0.5 / D)
            _acc(dgp_ref, dgp, i == 0)
            _acc(loss_ref, jnp.broadcast_to(lpart, (1, LANES)), i == 0)

    row = lambda w: BS((tm, w), lambda i, j: (i, 0))
    return pl.pallas_call(
        body, name="ffn_fwd", grid=(T // tm, ns),
        in_specs=[row(D), row(D), row(D), BS((1, D, F), lambda i, j: (j, 0, 0)), BS((1, D, F), lambda i, j: (j, 0, 0)),
                  BS((1, F, D), lambda i, j: (j, 0, 0)), BS((1, D), lambda i, j: (0, 0))],
        out_specs=[BS((1, tm, F), lambda i, j: (j, i, 0)), BS((1, tm, F), lambda i, j: (j, i, 0)), row(D), row(D),
                   BS((1, D), lambda i, j: (0, 0)), BS((1, LANES), lambda i, j: (0, 0))],
        out_shape=[jax.ShapeDtypeStruct((ns, T, F), BF16), jax.ShapeDtypeStruct((ns, T, F), BF16),
                   jax.ShapeDtypeStruct((T, D), BF16), jax.ShapeDtypeStruct((T, D), F32),
                   jax.ShapeDtypeStruct((1, D), F32), jax.ShapeDtypeStruct((1, LANES), F32)],
        scratch_shapes=[pltpu.VMEM((tm, D), F32)],
        compiler_params=_cp(("arbitrary", "arbitrary")),
    )(h2, x1, target, wg, wu, wd, g_post)


def _ffn_bwd(dff, gs, us, x1, dx2, wg, wu, wd, g_pre2, tm):
    T, D = x1.shape
    ns, _, F = wg.shape

    def body(dff_ref, gs_ref, us_ref, x1_ref, dx2_ref, wg_ref, wu_ref, wd_ref, g2_ref,
             a_ref, dg_ref, du_ref, dx1_ref, dg2_ref, acc_ref):
        i = pl.program_id(0)
        j = pl.program_id(1)
        dact = _dot_nt(dff_ref[...], wd_ref[0])
        a, vjp = jax.vjp(_silu_mul, gs_ref[0].astype(F32), us_ref[0].astype(F32))
        dg, du = vjp(dact)
        a_ref[0] = a.astype(BF16)
        dg = dg.astype(BF16)
        du = du.astype(BF16)
        dg_ref[0] = dg
        du_ref[0] = du
        part = _dot_nt(dg, wg_ref[0]) + _dot_nt(du, wu_ref[0])
        _acc(acc_ref, part, j == 0)

        @pl.when(j == ns - 1)
        def _():
            _, vjp2 = jax.vjp(_rms, x1_ref[...], g2_ref[...])
            dxa, dg2 = vjp2(acc_ref[...])
            dx1_ref[...] = dx2_ref[...] + dxa
            _acc(dg2_ref, dg2, i == 0)

    row = lambda w: BS((tm, w), lambda i, j: (i, 0))
    sh = BS((1, tm, F), lambda i, j: (j, i, 0))
    return pl.pallas_call(
        body, name="ffn_bwd", grid=(T // tm, ns),
        in_specs=[row(D), sh, sh, row(D), row(D), BS((1, D, F), lambda i, j: (j, 0, 0)),
                  BS((1, D, F), lambda i, j: (j, 0, 0)), BS((1, F, D), lambda i, j: (j, 0, 0)),
                  BS((1, D), lambda i, j: (0, 0))],
        out_specs=[sh, sh, sh, row(D), BS((1, D), lambda i, j: (0, 0))],
        out_shape=[jax.ShapeDtypeStruct((ns, T, F), BF16)] * 3 + [jax.ShapeDtypeStruct((T, D), F32),
                                                                  jax.ShapeDtypeStruct((1, D), F32)],
        scratch_shapes=[pltpu.VMEM((tm, D), F32)],
        compiler_params=_cp(("arbitrary", "arbitrary")),
    )(dff, gs, us, x1, dx2, wg, wu, wd, g_pre2)


def _mm_tn(a, b, name, tk):
    ab, bb = a.ndim == 3, b.ndim == 3
    nbatch = a.shape[0] if ab else (b.shape[0] if bb else 1)
    T, M = a.shape[-2:]
    N = b.shape[-1]
    tk = min(tk, T)

    def body(a_ref, b_ref, o_ref):
        av = a_ref[0] if ab else a_ref[...]
        bv = b_ref[0] if bb else b_ref[...]
        _acc(o_ref, _dot_tn(av, bv)[None], pl.program_id(1) == 0)

    a_spec = BS((1, tk, M), lambda s, t: (s, t, 0)) if ab else BS((tk, M), lambda s, t: (t, 0))
    b_spec = BS((1, tk, N), lambda s, t: (s, t, 0)) if bb else BS((tk, N), lambda s, t: (t, 0))
    return pl.pallas_call(
        body, name=name, grid=(nbatch, T // tk),
        in_specs=[a_spec, b_spec],
        out_specs=BS((1, M, N), lambda s, t: (s, 0, 0)),
        out_shape=jax.ShapeDtypeStruct((nbatch, M, N), F32),
        compiler_params=_cp(("arbitrary", "arbitrary")),
    )(a, b)


def _outproj_bwd(dx1, o, ya, yb, ym, ga, gb, gm, g_post, w_out, tm):
    T, D = dx1.shape

    def body(dx1_ref, o_ref, ya_ref, yb_ref, ym_ref, ga_ref, gb_ref, gm_ref, gp_ref, w_ref,
             do_ref, dya_ref, dyb_ref, dym_ref, dga_ref, dgb_ref, dgm_ref, dgp_ref):
        first = pl.program_id(0) == 0
        _, vjp = jax.vjp(_rms, o_ref[...], gp_ref[...])
        do, dgp = vjp(dx1_ref[...])
        do = do.astype(BF16)
        do_ref[...] = do
        dy = _dot_nt(do, w_ref[...])
        _, vjp2 = jax.vjp(_mix_norms, ya_ref[...], yb_ref[...], ym_ref[...], ga_ref[...], gb_ref[...], gm_ref[...])
        dya, dyb, dym, dga, dgb, dgm = vjp2((dy[:, 0:A_W], dy[:, A_W:A_W + B_W], dy[:, A_W + B_W:]))
        dya_ref[...] = dya
        dyb_ref[...] = dyb
        dym_ref[...] = dym
        _acc(dga_ref, dga, first)
        _acc(dgb_ref, dgb, first)
        _acc(dgm_ref, dgm, first)
        _acc(dgp_ref, dgp, first)

    row = lambda w: BS((tm, w), lambda i: (i, 0))
    vec = lambda w: BS((1, w), lambda i: (0, 0))
    sds = jax.ShapeDtypeStruct
    return pl.pallas_call(
        body, name="outproj_bwd", grid=(T // tm,),
        in_specs=[row(D), row(D), row(A_W), row(B_W), row(M_W), vec(A_W), vec(B_W), vec(M_W), vec(D),
                  BS((A_W + B_W + M_W, D), lambda i: (0, 0))],
        out_specs=[row(D), row(A_W), row(B_W), row(M_W), vec(A_W), vec(B_W), vec(M_W), vec(D)],
        out_shape=[sds((T, D), BF16), sds((T, A_W), F32), sds((T, B_W), F32), sds((T, M_W), F32),
                   sds((1, A_W), F32), sds((1, B_W), F32), sds((1, M_W), F32), sds((1, D), F32)],
        compiler_params=_cp(("arbitrary",)),
    )(dx1, o, ya, yb, ym, ga, gb, gm, g_post, w_out)


def _sgu_bwd(proj, dya, g_sgu, ws_tril, bs_full, tm):
    T = proj.shape[0]
    nch = tm // CHUNK

    def body(zu_ref, zv_ref, dy_ref, g_ref, ws_ref, b_ref, dzu_ref, dzv_ref, dws_ref, dbs_ref, dg_ref,
             du_ref, dvn_ref, dbf_ref):
        step = pl.program_id(0)
        first = step == 0
        lane = _iota((CHUNK, LANES), 1)
        tril = _iota((CHUNK, CHUNK), 0) >= _iota((CHUNK, CHUNK), 1)
        (u, vn), vjp = jax.vjp(_sgu_pre, zu_ref[...].astype(F32), zv_ref[...].astype(F32), g_ref[...])
        vnb = vn.astype(BF16)
        dy = dy_ref[...]

        @pl.when(first)
        def _():
            dws_ref[...] = jnp.zeros_like(dws_ref)
            dbf_ref[...] = jnp.zeros_like(dbf_ref)

        for c in range(nch):
            rs = slice(c * CHUNK, (c + 1) * CHUNK)
            for j in range(3):
                cs = slice(j * LANES, (j + 1) * LANES)
                vp = vnb[rs, cs]
                z = jnp.where(lane < HEAD, _dot(ws_ref[2 * j], vp), _dot(ws_ref[2 * j + 1], vp)) + b_ref[:, cs]
                du_ref[rs, cs] = dy[rs, cs] * z
                dz = dy[rs, cs] * u[rs, cs]
                dbf_ref[:, cs] += dz
                dzb = dz.astype(BF16)
                dz0 = jnp.where(lane < HEAD, dzb, jnp.zeros_like(dzb))
                dz1 = jnp.where(lane >= HEAD, dzb, jnp.zeros_like(dzb))
                dvn_ref[rs, cs] = jnp.where(lane < HEAD, _dot_tn(ws_ref[2 * j], dzb), _dot_tn(ws_ref[2 * j + 1], dzb))
                dws_ref[2 * j] += jnp.where(tril, _dot_nt(dz0, vp), 0.0)
                dws_ref[2 * j + 1] += jnp.where(tril, _dot_nt(dz1, vp), 0.0)
        dzu, dzv, dg = vjp((du_ref[...], dvn_ref[...]))
        dzu_ref[...] = dzu.astype(BF16)
        dzv_ref[...] = dzv.astype(BF16)
        _acc(dg_ref, dg, first)

        @pl.when(step == pl.num_programs(0) - 1)
        def _():
            out = jnp.zeros((CHUNK, LANES), F32)
            for j in range(3):
                slab = dbf_ref[:, j * LANES:(j + 1) * LANES]
                lo = jnp.sum(jnp.where(lane < HEAD, slab, 0.0), axis=1, keepdims=True)
                hi = jnp.sum(jnp.where(lane >= HEAD, slab, 0.0), axis=1, keepdims=True)
                out = out + jnp.where(lane == 2 * j, lo, 0.0) + jnp.where(lane == 2 * j + 1, hi, 0.0)
            dbs_ref[...] = out

    return pl.pallas_call(
        body, name="sgu_bwd", grid=(T // tm,),
        in_specs=[BS((tm, A_W), lambda i: (i, 0)), BS((tm, A_W), lambda i: (i, 1)), BS((tm, A_W), lambda i: (i, 0)),
                  BS((1, A_W), lambda i: (0, 0)), BS((6, CHUNK, CHUNK), lambda i: (0, 0, 0)),
                  BS((CHUNK, A_W), lambda i: (0, 0))],
        out_specs=[BS((tm, A_W), lambda i: (i, 0)), BS((tm, A_W), lambda i: (i, 0)),
                   BS((6, CHUNK, CHUNK), lambda i: (0, 0, 0)), BS((CHUNK, LANES), lambda i: (0, 0)),
                   BS((1, A_W), lambda i: (0, 0))],
        out_shape=[jax.ShapeDtypeStruct((T, A_W), BF16), jax.ShapeDtypeStruct((T, A_W), BF16),
                   jax.ShapeDtypeStruct((6, CHUNK, CHUNK), F32), jax.ShapeDtypeStruct((CHUNK, LANES), F32),
                   jax.ShapeDtypeStruct((1, A_W), F32)],
        scratch_shapes=[pltpu.VMEM((tm, A_W), F32), pltpu.VMEM((tm, A_W), F32), pltpu.VMEM((CHUNK, A_W), F32)],
        compiler_params=_cp(("arbitrary",)),
    )(proj, proj, dya, g_sgu, ws_tril, bs_full)


def _memattn_bwd(proj, kv, dym, Bl, S, tq):
    T = Bl * S
    nq = S // tq
    Mt = kv.shape[1]
    qc = 1920 // LANES

    def body(q_ref, km_ref, vm_ref, do_ref, dq_ref, dkm_ref, dvm_ref):
        first = pl.program_id(2) == 0
        lane = _iota((tq, LANES), 1)
        q = q_ref[...]
        do = do_ref[...]
        dq_out = jnp.zeros((tq, LANES), F32)
        dkm = jnp.zeros((Mt, LANES), F32)
        dvm = jnp.zeros((Mt, LANES), F32)
        for hh in range(2):
            hmask = (lane < HEAD) if hh == 0 else (lane >= HEAD)
            qs = jnp.where(hmask, q, jnp.zeros_like(q)) * 0.125
            dom = jnp.where(hmask, do, 0.0).astype(BF16)
            s = _dot_nt(qs, km_ref[0])
            pe = jnp.exp(s - jnp.max(s, axis=1, keepdims=True))
            pn = pe / jnp.sum(pe, axis=1, keepdims=True)
            dp = _dot_nt(dom, vm_ref[0])
            ds = (pn * (dp - jnp.sum(pn * dp, axis=1, keepdims=True))).astype(BF16)
            dq_out = jnp.where(hmask, _dot(ds, km_ref[0]) * 0.125, dq_out)
            dkm = dkm + _dot_tn(ds, qs)
            dvm = dvm + _dot_tn(pn, dom)
        dq_ref[...] = dq_out.astype(BF16)
        _acc(dkm_ref, dkm[None], first)
        _acc(dvm_ref, dvm[None], first)

    return pl.pallas_call(
        body, name="memattn_bwd", grid=(Bl, 2, nq),
        in_specs=[BS((tq, LANES), lambda b, p, i: (b * nq + i, qc + p)),
                  BS((1, Mt, LANES), lambda b, p, i: (b, 0, p)),
                  BS((1, Mt, LANES), lambda b, p, i: (b, 0, 2 + p)),
                  BS((tq, LANES), lambda b, p, i: (b * nq + i, p))],
        out_specs=[BS((tq, LANES), lambda b, p, i: (b * nq + i, p)),
                   BS((1, Mt, LANES), lambda b, p, i: (b, 0, p)),
                   BS((1, Mt, LANES), lambda b, p, i: (b, 0, p))],
        out_shape=[jax.ShapeDtypeStruct((T, M_W), BF16), jax.ShapeDtypeStruct((Bl, Mt, M_W), F32),
                   jax.ShapeDtypeStruct((Bl, Mt, M_W), F32)],
        compiler_params=_cp(("arbitrary", "arbitrary", "arbitrary")),
    )(proj, kv, kv, dym)


def _memkv_bwd(dkm, dvm, memn, mem, g_mem, w_kv):
    Bl, Mt, D = mem.shape

    def body(dk_ref, dv_ref, mn_ref, m_ref, g_ref, w_ref, dw_ref, dg_ref):
        first = pl.program_id(0) == 0
        dk = dk_ref[0].astype(BF16)
        dv = dv_ref[0].astype(BF16)
        mn = mn_ref[0]
        dmn = _dot_nt(dk, w_ref[:, 0:M_W]) + _dot_nt(dv, w_ref[:, M_W:])
        _, vjp = jax.vjp(_rms, m_ref[0], g_ref[...])
        _, dg = vjp(dmn)
        _acc(dg_ref, dg, first)

        @pl.when(first)
        def _():
            dw_ref[...] = jnp.zeros_like(dw_ref)

        dw_ref[:, 0:M_W] += _dot_tn(mn, dk)
        dw_ref[:, M_W:] += _dot_tn(mn, dv)

    return pl.pallas_call(
        body, name="memkv_bwd", grid=(Bl,),
        in_specs=[BS((1, Mt, M_W), lambda b: (b, 0, 0)), BS((1, Mt, M_W), lambda b: (b, 0, 0)),
                  BS((1, Mt, D), lambda b: (b, 0, 0)), BS((1, Mt, D), lambda b: (b, 0, 0)),
                  BS((1, D), lambda b: (0, 0)), BS((D, 2 * M_W), lambda b: (0, 0))],
        out_specs=[BS((D, 2 * M_W), lambda b: (0, 0)), BS((1, D), lambda b: (0, 0))],
        out_shape=[jax.ShapeDtypeStruct((D, 2 * M_W), F32), jax.ShapeDtypeStruct((1, D), F32)],
        compiler_params=_cp(("arbitrary",)),
    )(dkm, dvm, memn, mem, g_mem, w_kv)


def _fox_bwd(proj, dyb, lse, c_col, c_row, Bl, S):
    T = Bl * S
    nq = S // Q_BLK
    nb = S // K_BLK
    qc, kc, vc = 768 // LANES, 1152 // LANES, 1536 // LANES

    def body(q_ref, k_ref, v_ref, do_ref, lse_ref, cc_ref, cr_ref,
             dq_ref, dk_ref, dv_ref, dcr_ref, dka_ref, dva_ref, pm_ref, dp_ref):
        p = pl.program_id(1)
        lane = _iota((Q_BLK, LANES), 1)
        cols = _iota((Q_BLK, K_BLK), 1)
        sub = _iota((8, K_BLK), 0)
        dka_ref[...] = jnp.zeros_like(dka_ref)
        dva_ref[...] = jnp.zeros_like(dva_ref)

        @pl.when(p == 0)
        def _():
            dcr_ref[...] = jnp.zeros_like(dcr_ref)

        def qblock(i, _):
            r0 = pl.multiple_of(i * Q_BLK, Q_BLK)
            rows = _iota((Q_BLK, K_BLK), 0) + r0
            nk = (i + 1) * (Q_BLK // K_BLK)
            q = q_ref[pl.ds(r0, Q_BLK), :]
            do = do_ref[pl.ds(r0, Q_BLK), :]
            lse_b = lse_ref[0, pl.ds(r0, Q_BLK), :]
            cc = cc_ref[0, pl.ds(r0, Q_BLK), :]
            dq_out = jnp.zeros((Q_BLK, LANES), F32)
            for hh in range(2):
                hmask = (lane < HEAD) if hh == 0 else (lane >= HEAD)
                first_lane = 0 if hh == 0 else HEAD
                h = 2 * p + hh
                qs = jnp.where(hmask, q, jnp.zeros_like(q)) * 0.125
                dob = jnp.where(hmask, do, 0.0).astype(BF16)
                lse_h = jnp.sum(jnp.where(lane == first_lane, lse_b, 0.0), axis=1, keepdims=True)
                cq = jnp.sum(jnp.where(lane == h, cc, 0.0), axis=1, keepdims=True)

                def walk1(kb, delta, qs=qs, dob=dob, lse_h=lse_h, cq=cq, h=h):
                    k0 = pl.multiple_of(kb * K_BLK, K_BLK)
                    ck = _colsum(jnp.where(sub == h, cr_ref[0, kb], 0.0))
                    s = _dot_nt(qs, k_ref[pl.ds(k0, K_BLK), :]) + cq - ck
                    pm = jnp.where(cols + k0 <= rows, jnp.exp(s - lse_h), 0.0)
                    dp = _dot_nt(dob, v_ref[pl.ds(k0, K_BLK), :])
                    pm_ref[kb] = pm
                    dp_ref[kb] = dp
                    return delta + jnp.sum(pm * dp, axis=1, keepdims=True)

                delta = lax.fori_loop(0, nk, walk1, jnp.zeros((Q_BLK, 1), F32))

                def walk2(kb, dq_acc, qs=qs, dob=dob, delta=delta, h=h):
                    k0 = pl.multiple_of(kb * K_BLK, K_BLK)
                    pm = pm_ref[kb]
                    ds = pm * (dp_ref[kb] - delta)
                    dcr_ref[0, kb] += jnp.where(sub == h, _colsum(ds), 0.0)
                    dsb = ds.astype(BF16)
                    dka_ref[pl.ds(k0, K_BLK), :] += _dot_tn(dsb, qs)
                    dva_ref[pl.ds(k0, K_BLK), :] += _dot_tn(pm, dob)
                    return dq_acc + _dot(dsb, k_ref[pl.ds(k0, K_BLK), :])

                dq_h = lax.fori_loop(0, nk, walk2, jnp.zeros((Q_BLK, LANES), F32))
                dq_out = jnp.where(hmask, dq_h * 0.125, dq_out)
            dq_ref[pl.ds(r0, Q_BLK), :] = dq_out.astype(BF16)
            return 0

        lax.fori_loop(0, nq, qblock, 0)
        dk_ref[...] = dka_ref[...].astype(BF16)
        dv_ref[...] = dva_ref[...].astype(BF16)

    seq = lambda c0: BS((S, LANES), lambda b, p: (b, c0 + p))
    rowblk = BS((1, nb, 8, LANES), lambda b, p: (b, 0, 0, 0))
    return pl.pallas_call(
        body, name="fox_bwd", grid=(Bl, 3),
        in_specs=[seq(qc), seq(kc), seq(vc), seq(0),
                  BS((1, S, LANES), lambda b, p: (p, b, 0)),
                  BS((1, S, LANES), lambda b, p: (b, 0, 0)),
                  rowblk],
        out_specs=[seq(0), seq(0), seq(0), rowblk],
        out_shape=[jax.ShapeDtypeStruct((T, B_W), BF16)] * 3 + [jax.ShapeDtypeStruct((Bl, nb, 8, LANES), F32)],
        scratch_shapes=[pltpu.VMEM((S, LANES), F32), pltpu.VMEM((S, LANES), F32),
                        pltpu.VMEM((nb, Q_BLK, K_BLK), F32), pltpu.VMEM((nb, Q_BLK, K_BLK), F32)],
        compiler_params=_cp(("arbitrary", "arbitrary")),
    )(proj, proj, proj, dyb, lse, c_col, c_row)


def _gate_bwd(dc_row, fl_row):
    Bl, nb, _, _ = dc_row.shape

    def body(dc_ref, fl_ref, o_ref):
        lane = _iota((8, LANES), 1)

        def blk(jj, carry):
            j = nb - 1 - jj
            r = -dc_ref[0, j]
            for k in (1, 2, 4, 8, 16, 32, 64):
                r = r + jnp.where(lane < LANES - k, pltpu.roll(r, LANES - k, 1), 0.0)
            r = r + carry
            dfl = r * _sigmoid(-fl_ref[0, j])
            o_ref[0, pl.ds(pl.multiple_of(j * LANES, LANES), LANES), :] = jnp.concatenate(
                [dfl, jnp.zeros((LANES - 8, LANES), F32)], axis=0).T
            return jnp.sum(jnp.where(lane == 0, r, 0.0), axis=1, keepdims=True)

        lax.fori_loop(0, nb, blk, jnp.zeros((8, 1), F32))

    rowblk = BS((1, nb, 8, LANES), lambda b: (b, 0, 0, 0))
    return pl.pallas_call(
        body, name="gate_bwd", grid=(Bl,),
        in_specs=[rowblk, rowblk],
        out_specs=BS((1, nb * LANES, LANES), lambda b: (b, 0, 0)),
        out_shape=jax.ShapeDtypeStruct((Bl, nb * LANES, LANES), F32),
        compiler_params=_cp(("arbitrary",)),
    )(dc_row, fl_row)


def _inproj_bwd(dzu, dzv, dq, dk, dv, dqm, dfl, x2d, dx1, g_pre, w_in_p, tm):
    T, D = x2d.shape
    nchunk = P_COLS // 384

    def body(dzu_ref, dzv_ref, dq_ref, dk_ref, dv_ref, dqm_ref, dfl_ref, x_ref, dx1_ref, g_ref, w_ref,
             dp_ref, gx_ref, dg_ref, dbf_ref):
        first = pl.program_id(0) == 0
        dfl = dfl_ref[...]
        dp_ref[:, 0:384] = dzu_ref[...]
        dp_ref[:, 384:768] = dzv_ref[...]
        dp_ref[:, 768:1152] = dq_ref[...]
        dp_ref[:, 1152:1536] = dk_ref[...]
        dp_ref[:, 1536:1920] = dv_ref[...]
        dp_ref[:, 1920:2176] = dqm_ref[...]
        dp_ref[:, 2176:2304] = dfl.astype(BF16)
        dh = jnp.zeros((tm, D), F32)
        for n in range(nchunk):
            cs = slice(n * 384, (n + 1) * 384)
            dh = dh + _dot_nt(dp_ref[:, cs], w_ref[:, cs])
        _, vjp = jax.vjp(_rms, x_ref[...], g_ref[...])
        dxa, dg = vjp(dh)
        gx_ref[...] = dx1_ref[...] + dxa
        _acc(dg_ref, dg, first)
        _acc(dbf_ref, _colsum(dfl), first)

    row = lambda w: BS((tm, w), lambda i: (i, 0))
    return pl.pallas_call(
        body, name="inproj_bwd", grid=(T // tm,),
        in_specs=[row(A_W), row(A_W), row(B_W), row(B_W), row(B_W), row(M_W), row(LANES), row(D), row(D),
                  BS((1, D), lambda i: (0, 0)), BS((D, P_COLS), lambda i: (0, 0))],
        out_specs=[row(P_COLS), row(D), BS((1, D), lambda i: (0, 0)), BS((1, LANES), lambda i: (0, 0))],
        out_shape=[jax.ShapeDtypeStruct((T, P_COLS), BF16), jax.ShapeDtypeStruct((T, D), F32),
                   jax.ShapeDtypeStruct((1, D), F32), jax.ShapeDtypeStruct((1, LANES), F32)],
        compiler_params=_cp(("arbitrary",)),
    )(dzu, dzv, dq, dk, dv, dqm, dfl, x2d, dx1, g_pre, w_in_p)


def _local_step(x, mem, target, W, P):
    Bl, S, D = x.shape
    T = Bl * S
    tm = min(512, T)
    x2d = x.reshape(T, D)
    t2d = target.reshape(T, D)
    vec = lambda a: a.reshape(1, -1)
    bf_row = jnp.pad(P["b_f"].reshape(1, -1), ((0, 0), (0, LANES - N_FOX_HEADS)))
    tril = jnp.tril(jnp.ones((CHUNK, CHUNK), bool))
    ws_tril = jnp.where(tril[None], P["w_s"][0], 0.0).astype(BF16)
    bs_full = jnp.repeat(P["b_s"][0].T, HEAD, axis=1)
    g_pre, g_sgu = vec(P["g_pre_mix"]), vec(P["g_sgu"])
    ga, gb, gm = vec(P["g_out_a"]), vec(P["g_out_b"]), vec(P["g_out_m"])
    g_mem, g_post, g_pre2, g_post2 = vec(P["g_mem"]), vec(P["g_post_mix"]), vec(P["g_pre_ffn"]), vec(P["g_post_ffn"])

    h, proj, flog = _inproj_fwd(x2d, g_pre, W["w_in"], tm)
    c_col, c_row, fl_row = _gate_fwd(flog.reshape(Bl, S, LANES), bf_row)
    ya = _sgu_fwd(proj, g_sgu, ws_tril, bs_full, tm)
    yb, lse = _fox_fwd(proj, c_col, c_row, Bl, S)
    memn, kv = _memkv_fwd(mem, g_mem, W["w_mem_kv"])
    ym = _memattn_fwd(proj, kv, Bl, S, min(512, S))
    y, o, x1, h2 = _outproj_fwd(ya, yb, ym, x2d, ga, gb, gm, g_post, g_pre2, W["w_out"], tm)
    gs, us, dff, dx2, dg_post2, loss = _ffn_fwd(h2, x1, t2d, W["w_gate"], W["w_up"], W["w_down"], g_post2, tm)

    a, dgs, dus, dx1, dg_pre2 = _ffn_bwd(dff, gs, us, x1, dx2, W["w_gate"], W["w_up"], W["w_down"], g_pre2, tm)
    d_w_down = _mm_tn(a, dff, "dw_down", 1024)
    d_w_gate = _mm_tn(h2, dgs, "dw_gate", 1024)
    d_w_up = _mm_tn(h2, dus, "dw_up", 1024)
    do, dya, dyb, dym, dga, dgb, dgm, dg_post = _outproj_bwd(dx1, o, ya, yb, ym, ga, gb, gm, g_post, W["w_out"], tm)
    d_w_out = _mm_tn(y, do, "dw_out", 1024)[0]
    dzu, dzv, dws, dbs_cols, dg_sgu = _sgu_bwd(proj, dya, g_sgu, ws_tril, bs_full, tm)
    dqm, dkm, dvm = _memattn_bwd(proj, kv, dym, Bl, S, min(512, S))
    d_w_kv, dg_mem = _memkv_bwd(dkm, dvm, memn, mem, g_mem, W["w_mem_kv"])
    dq, dk, dv, dc_row = _fox_bwd(proj, dyb, lse, c_col, c_row, Bl, S)
    dfl = _gate_bwd(dc_row, fl_row).reshape(T, LANES)
    dproj, grad_x, dg_pre, dbf = _inproj_bwd(dzu, dzv, dq, dk, dv, dqm, dfl, x2d, dx1, g_pre, W["w_in"], tm)
    d_w_in = _mm_tn(h, dproj, "dw_in", 1024)[0]

    big = {"w_in": d_w_in, "w_mem_kv": d_w_kv, "w_out": d_w_out, "w_gate": d_w_gate, "w_up": d_w_up,
           "w_down": d_w_down}
    small = {"g_pre_mix": dg_pre, "b_f": dbf[:, :N_FOX_HEADS], "g_sgu": dg_sgu, "w_s": dws, "b_s": dbs_cols[:, :N_FOX_HEADS].T,
             "g_out_a": dga, "g_out_b": dgb, "g_out_m": dgm, "g_mem": dg_mem, "g_post_mix": dg_post,
             "g_pre_ffn": dg_pre2, "g_post_ffn": dg_post2, "loss": loss[:, :1]}
    return grad_x.reshape(Bl, S, D), big, small


def _place():
    return lax.axis_index("x"), lax.axis_index("y"), lax.axis_index("c")


def _exchange(srcs, own_full, name):
    n = len(srcs)
    nd2d = 3 if own_full else 4

    def body(*refs):
        src, dst, own = refs[:n], refs[n:2 * n], refs[2 * n:3 * n]
        lsem, osem, isend, irecv, dsend, drecv = refs[3 * n:]
        x, y, c = _place()
        oc = 1 - c
        s_me = 2 * x + y
        sib = (x, y, oc)
        chips = [(1 - x, y), (x, 1 - y), (1 - x, 1 - y)]

        def remote(a, b, ssem, rsem, dev):
            return pltpu.make_async_remote_copy(src_ref=a, dst_ref=b, send_sem=ssem, recv_sem=rsem,
                                                device_id=dev, device_id_type=MESH)

        sends, loads, local = [], [], []
        for w in range(n):
            ld = pltpu.make_async_copy(src[w] if own_full else src[w].at[s_me], own[w], lsem.at[w])
            ld.start()
            loads.append(ld)
        for w in range(n):
            for j, (cx, cy) in enumerate(chips):
                half = src[w].at[c] if own_full else src[w].at[2 * cx + cy]
                cp = remote(half, dst[w].at[s_me, c], isend.at[w, j], irecv.at[w, j], (cx, cy, c))
                cp.start()
                sends.append(cp)
            if not own_full:
                cp = remote(src[w].at[s_me], dst[w].at[s_me, c], dsend.at[w, 3], drecv.at[w, 3], sib)
                cp.start()
                sends.append(cp)
        for w in range(n):
            loads[w].wait()
            st = pltpu.make_async_copy(own[w], dst[w].at[s_me] if own_full else dst[w].at[s_me, c], osem.at[w])
            st.start()
            local.append(st)
        for w in range(n):
            for j, (cx, cy) in enumerate(chips):
                landed = dst[w].at[2 * cx + cy, c]
                remote(landed, landed, isend.at[w, j], irecv.at[w, j], (cx, cy, c)).wait_recv()
                cp = remote(landed, landed, dsend.at[w, j], drecv.at[w, j], sib)
                cp.start()
                sends.append(cp)
        for w in range(n):
            for j, (cx, cy) in enumerate(chips):
                landed = dst[w].at[2 * cx + cy, oc]
                remote(landed, landed, dsend.at[w, j], drecv.at[w, j], sib).wait_recv()
            if not own_full:
                landed = dst[w].at[s_me, oc]
                remote(landed, landed, dsend.at[w, 3], drecv.at[w, 3], sib).wait_recv()
        for cp in sends:
            cp.wait_send()
        for loc in local:
            loc.wait()

    out_shape = [jax.ShapeDtypeStruct((4, 2) + s.shape[1:], s.dtype) for s in srcs]
    own_shapes = [pltpu.VMEM(s.shape if own_full else s.shape[1:], s.dtype) for s in srcs]
    return pl.pallas_call(
        body, name=name, in_specs=[ANY] * n, out_specs=[ANY] * n, out_shape=out_shape,
        scratch_shapes=own_shapes + [pltpu.SemaphoreType.DMA((n,)), pltpu.SemaphoreType.DMA((n,)),
                                     pltpu.SemaphoreType.DMA((n, 3)), pltpu.SemaphoreType.DMA((n, 3)),
                                     pltpu.SemaphoreType.DMA((n, nd2d)), pltpu.SemaphoreType.DMA((n, nd2d))],
        compiler_params=pltpu.CompilerParams(vmem_limit_bytes=VMEM_LIMIT),
    )(*srcs)


def _sibling_swap(grads):
    n = len(grads)

    def body(*refs):
        g, theirs = refs[:n], refs[n:2 * n]
        ssem, rsem = refs[2 * n:]
        x, y, c = _place()
        cps = []
        for w in range(n):
            cp = pltpu.make_async_remote_copy(src_ref=g[w].at[:, 1 - c], dst_ref=theirs[w], send_sem=ssem.at[w],
                                              recv_sem=rsem.at[w], device_id=(x, y, 1 - c), device_id_type=MESH)
            cp.start()
            cps.append(cp)
        for cp in cps:
            cp.wait()

    half = [jax.ShapeDtypeStruct((4,) + g.shape[2:], g.dtype) for g in grads]
    return pl.pallas_call(
        body, name="sibling_swap", in_specs=[ANY] * n, out_specs=[ANY] * n, out_shape=half,
        scratch_shapes=[pltpu.SemaphoreType.DMA((n,)), pltpu.SemaphoreType.DMA((n,))],
    )(*grads)


def _add_pair(core, g, theirs, name):
    _, _, hr, C = g.shape

    def body(core_ref, g_ref, t_ref, o_ref):
        o_ref[0] = (g_ref[0, 0] + t_ref[0]).astype(BF16)

    blk = BS((1, hr, C), lambda s, core_ref: (s, 0, 0))
    return pl.pallas_call(
        body, name=name,
        grid_spec=pltpu.PrefetchScalarGridSpec(
            num_scalar_prefetch=1, grid=(4,),
            in_specs=[BS((1, 1, hr, C), lambda s, core_ref: (s, core_ref[0], 0, 0)), blk], out_specs=blk),
        out_shape=jax.ShapeDtypeStruct(theirs.shape, BF16), compiler_params=_cp(("arbitrary",)))(core, g, theirs)


def _sum_chips(r, name):
    _, _, hr, C = r.shape

    def body(r_ref, o_ref):
        o_ref[...] = ((r_ref[0, 0].astype(F32) + r_ref[1, 0].astype(F32)) + r_ref[2, 0].astype(F32)) + r_ref[3, 0].astype(F32)

    return pl.pallas_call(body, name=name, grid=(2,), in_specs=[BS((4, 1, hr, C), lambda h: (0, h, 0, 0))],
                          out_specs=BS((hr, C), lambda h: (h, 0)), out_shape=jax.ShapeDtypeStruct((2 * hr, C), F32),
                          compiler_params=_cp(("arbitrary",)))(r)


def _small_allreduce(part):
    R = part.shape[0]
    rs = R // 8
    masks = [(mx, my, mc) for mx in (0, 1) for my in (0, 1) for mc in (0, 1)][1:]

    def body(p_ref, o_ref, buf_ref, s1, r1, s2, r2):
        x, y, c = _place()
        d = 4 * x + 2 * y + c
        mine = pl.ds(pl.multiple_of(d * rs, 8), rs)
        peers = [((x + mx) % 2, (y + my) % 2, (c + mc) % 2) for mx, my, mc in masks]
        first, second = [], []
        for k, (px, py, pc) in enumerate(peers):
            theirs = pl.ds(pl.multiple_of((4 * px + 2 * py + pc) * rs, 8), rs)
            cp = pltpu.make_async_remote_copy(src_ref=p_ref.at[theirs, :], dst_ref=buf_ref.at[d], send_sem=s1.at[k],
                                              recv_sem=r1.at[k], device_id=(px, py, pc), device_id_type=MESH)
            cp.start()
            first.append(cp)
        buf_ref[d] = p_ref[mine, :]
        for k, (px, py, pc) in enumerate(peers):
            slot = buf_ref.at[4 * px + 2 * py + pc]
            pltpu.make_async_remote_copy(src_ref=slot, dst_ref=slot, send_sem=s1.at[k], recv_sem=r1.at[k],
                                         device_id=(px, py, pc), device_id_type=MESH).wait_recv()
        total = buf_ref[0]
        for k in range(1, 8):
            total = total + buf_ref[k]
        o_ref[mine, :] = total
        for k, (px, py, pc) in enumerate(peers):
            cp = pltpu.make_async_remote_copy(src_ref=o_ref.at[mine, :], dst_ref=o_ref.at[mine, :], send_sem=s2.at[k],
                                              recv_sem=r2.at[k], device_id=(px, py, pc), device_id_type=MESH)
            cp.start()
            second.append(cp)
        for k, (px, py, pc) in enumerate(peers):
            rows = o_ref.at[pl.ds(pl.multiple_of((4 * px + 2 * py + pc) * rs, 8), rs), :]
            pltpu.make_async_remote_copy(src_ref=rows, dst_ref=rows, send_sem=s2.at[k], recv_sem=r2.at[k],
                                         device_id=(px, py, pc), device_id_type=MESH).wait_recv()
        for cp in first + second:
            cp.wait_send()

    vm = pl.BlockSpec(memory_space=pltpu.VMEM)
    return pl.pallas_call(
        body, name="small_allreduce", in_specs=[vm], out_specs=vm, out_shape=jax.ShapeDtypeStruct(part.shape, F32),
        scratch_shapes=[pltpu.VMEM((8, rs, LANES), F32)] + [pltpu.SemaphoreType.DMA((7,))] * 4,
    )(part)


def _adamw(w, g, m, v, name):
    R, C = w.shape
    tr = R if R * C * 4 <= (1 << 21) else R // 2
    if tr % 8:
        tr = R
    c1 = 1.0 / (1.0 - ADAM_B1 ** ADAM_STEP)
    c2 = 1.0 / (1.0 - ADAM_B2 ** ADAM_STEP)

    def body(w_ref, g_ref, m_ref, v_ref, d_ref, mo_ref, vo_ref):
        g_ = g_ref[...]
        m_ = ADAM_B1 * m_ref[...] + (1.0 - ADAM_B1) * g_
        v_ = ADAM_B2 * v_ref[...] + (1.0 - ADAM_B2) * (g_ * g_)
        mo_ref[...] = m_
        vo_ref[...] = v_
        d_ref[...] = -ADAM_LR * ((m_ * c1) / (jnp.sqrt(v_ * c2) + ADAM_EPS) + ADAM_WD * w_ref[...])

    blk = BS((tr, C), lambda i: (i, 0))
    return pl.pallas_call(body, name=name, grid=(R // tr,), in_specs=[blk] * 4, out_specs=[blk] * 3,
                          out_shape=[jax.ShapeDtypeStruct((R, C), F32)] * 3, compiler_params=_cp(("arbitrary",)))(w, g, m, v)


SMALL = ("g_pre_mix", "b_f", "g_sgu", "w_s", "b_s", "g_out_a", "g_out_b", "g_out_m", "g_mem", "g_post_mix",
         "g_pre_ffn", "g_post_ffn")
BIG = ("w_in", "w_mem_kv", "w_out", "w_gate", "w_up", "w_down")
WEIGHTS = ("g_pre_mix", "w_in", "b_f", "g_sgu", "w_s", "b_s", "g_out_a", "g_out_b", "g_out_m", "g_mem", "w_mem_kv",
           "w_out", "g_post_mix", "g_pre_ffn", "w_gate", "w_up", "w_down", "g_post_ffn")


def _rows_of(n):
    return -(-n // (8 * LANES)) * 8


def _pack(parts):
    tiles = []
    for a in parts:
        flat = a.reshape(-1).astype(F32)
        rows = _rows_of(flat.shape[0])
        tiles.append(jnp.pad(flat, (0, rows * LANES - flat.shape[0])).reshape(rows, LANES))
    total = sum(t.shape[0] for t in tiles)
    pad = -total % 64
    if pad:
        tiles.append(jnp.zeros((pad, LANES), F32))
    return jnp.concatenate(tiles, axis=0)


def _unpack(packed, shapes):
    out, r = [], 0
    for shp in shapes:
        n = 1
        for s in shp:
            n *= s
        rows = _rows_of(n)
        out.append(packed[r:r + rows].reshape(-1)[:n].reshape(shp))
        r += rows
    return out


def kernel(x, mem, g_pre_mix, w_in, b_f, g_sgu, w_s, b_s, g_out_a, g_out_b, g_out_m, g_mem, w_mem_kv, w_out, g_post_mix, g_pre_ffn, w_gate, w_up, w_down, g_post_ffn, loss_target, m_g_pre_mix, m_w_in, m_b_f, m_g_sgu, m_w_s, m_b_s, m_g_out_a, m_g_out_b, m_g_out_m, m_g_mem, m_w_mem_kv, m_w_out, m_g_post_mix, m_g_pre_ffn, m_w_gate, m_w_up, m_w_down, m_g_post_ffn, v_g_pre_mix, v_w_in, v_b_f, v_g_sgu, v_w_s, v_b_s, v_g_out_a, v_g_out_b, v_g_out_m, v_g_mem, v_w_mem_kv, v_w_out, v_g_post_mix, v_g_pre_ffn, v_w_gate, v_w_up, v_w_down, v_g_post_ffn):
    Wt = dict(g_pre_mix=g_pre_mix, w_in=w_in, b_f=b_f, g_sgu=g_sgu, w_s=w_s, b_s=b_s, g_out_a=g_out_a, g_out_b=g_out_b,
              g_out_m=g_out_m, g_mem=g_mem, w_mem_kv=w_mem_kv, w_out=w_out, g_post_mix=g_post_mix, g_pre_ffn=g_pre_ffn,
              w_gate=w_gate, w_up=w_up, w_down=w_down, g_post_ffn=g_post_ffn)
    Mo = dict(g_pre_mix=m_g_pre_mix, w_in=m_w_in, b_f=m_b_f, g_sgu=m_g_sgu, w_s=m_w_s, b_s=m_b_s, g_out_a=m_g_out_a,
              g_out_b=m_g_out_b, g_out_m=m_g_out_m, g_mem=m_g_mem, w_mem_kv=m_w_mem_kv, w_out=m_w_out,
              g_post_mix=m_g_post_mix, g_pre_ffn=m_g_pre_ffn, w_gate=m_w_gate, w_up=m_w_up, w_down=m_w_down,
              g_post_ffn=m_g_post_ffn)
    Vo = dict(g_pre_mix=v_g_pre_mix, w_in=v_w_in, b_f=v_b_f, g_sgu=v_g_sgu, w_s=v_w_s, b_s=v_b_s, g_out_a=v_g_out_a,
              g_out_b=v_g_out_b, g_out_m=v_g_out_m, g_mem=v_g_mem, w_mem_kv=v_w_mem_kv, w_out=v_w_out,
              g_post_mix=v_g_post_mix, g_pre_ffn=v_g_pre_ffn, w_gate=v_w_gate, w_up=v_w_up, w_down=v_w_down,
              g_post_ffn=v_g_post_ffn)

    def regroup(w):
        return jnp.concatenate([w[:, :1920], w[:, 1926:IN_COLS], w[:, 1920:1926],
                                jnp.zeros((w.shape[0], P_COLS - IN_COLS), w.dtype)], axis=1)

    def ungroup(g):
        return jnp.concatenate([g[:, :1920], g[:, P_MAIN:P_MAIN + N_FOX_HEADS], g[:, 1920:P_MAIN]], axis=1)

    shards = {n: (regroup(Wt[n][0]) if n == "w_in" else Wt[n][0]) for n in BIG}
    srcs = [shards[n].astype(BF16).reshape(2, shards[n].shape[0] // 2, shards[n].shape[1]) for n in BIG]
    fulls = _exchange(srcs, True, "gather_weights")
    W = {}
    for n, f in zip(BIG, fulls):
        _, _, hr, C = f.shape
        W[n] = f.reshape(4, 2 * hr, C) if n in ("w_gate", "w_up", "w_down") else f.reshape(8 * hr, C)

    P = {n: Wt[n] for n in SMALL}
    grad_x, big, small = _local_step(x, mem, loss_target, W, P)

    g4 = []
    for n in BIG:
        g = big[n]
        C = g.shape[-1]
        g4.append(g.reshape(4, 2, -1, C))
    theirs = _sibling_swap(g4)
    core = lax.axis_index("c").astype(jnp.int32).reshape(1)
    chip_sums = [_add_pair(core, g, t, "chip_sum_" + n) for n, g, t in zip(BIG, g4, theirs)]
    landed = _exchange(chip_sums, False, "scatter_grads")
    grads, deltas, new_m, new_v = {}, {}, {}, {}
    for n, r in zip(BIG, landed):
        g = _sum_chips(r, "sum_chips_" + n)
        if n == "w_in":
            g = ungroup(g)
        d, m1, v1 = _adamw(Wt[n][0], g, Mo[n][0], Vo[n][0], "adamw_" + n)
        grads[n], deltas[n], new_m[n], new_v[n] = g[None], d[None], m1[None], v1[None]

    total = _small_allreduce(_pack([small[n] for n in SMALL] + [small["loss"]]))
    slot = [jnp.zeros((1, 1), F32)]
    shapes = [Wt[n].shape for n in SMALL] + [(1, 1)]
    d, m1, v1 = _adamw(_pack([Wt[n] for n in SMALL] + slot), total, _pack([Mo[n] for n in SMALL] + slot),
                       _pack([Vo[n] for n in SMALL] + slot), "adamw_small")
    g_s, d_s, m_s, v_s = _unpack(total, shapes), _unpack(d, shapes), _unpack(m1, shapes), _unpack(v1, shapes)
    for k, n in enumerate(SMALL):
        grads[n], deltas[n], new_m[n], new_v[n] = g_s[k], d_s[k], m_s[k], v_s[k]
    loss = g_s[-1][0, 0]

    return (loss, grad_x, *[grads[n] for n in WEIGHTS], *[deltas[n] for n in WEIGHTS],
            *[new_m[n] for n in WEIGHTS], *[new_v[n] for n in WEIGHTS])
```

```python
import functools

import jax
import jax.numpy as jnp
from jax import lax
from jax.experimental import pallas as pl
from jax.experimental.pallas import tpu as pltpu

F32 = jnp.float32
BF16 = jnp.bfloat16
EPS = 1e-6
NEG = -1e30
HEAD = 64
A_W, B_W, M_W = 384, 384, 256
N_FOX_HEADS = 6
CHUNK = 128
IN_COLS = 2 * A_W + 3 * B_W + N_FOX_HEADS + M_W
P_MAIN = 2 * A_W + 3 * B_W + M_W
P_COLS = P_MAIN + 128
LANES = 128
Q_BLK, K_BLK = 256, 128
ADAM_LR, ADAM_B1, ADAM_B2, ADAM_EPS, ADAM_WD, ADAM_STEP = 0.001, 0.9, 0.999, 1e-08, 0.01, 10
VMEM_LIMIT = 56 * 1024 * 1024
MESH = pl.DeviceIdType.MESH
ANY = pl.BlockSpec(memory_space=pl.ANY)
BS = pl.BlockSpec


def _cp(sem=None):
    return pltpu.CompilerParams(dimension_semantics=sem, vmem_limit_bytes=VMEM_LIMIT)


def _iota(shape, dim):
    return lax.broadcasted_iota(jnp.int32, shape, dim)


def _dot(a, b):
    return jnp.dot(a.astype(BF16), b.astype(BF16), preferred_element_type=F32)


def _dot_nt(a, b):
    return lax.dot_general(a.astype(BF16), b.astype(BF16), (((1,), (1,)), ((), ())), preferred_element_type=F32)


def _dot_tn(a, b):
    return lax.dot_general(a.astype(BF16), b.astype(BF16), (((0,), (0,)), ((), ())), preferred_element_type=F32)


def _rms(x, g):
    return x * lax.rsqrt(jnp.mean(x * x, axis=-1, keepdims=True) + EPS) * g


def _gelu(x):
    return 0.5 * x * (1.0 + jnp.tanh(0.7978845608028654 * (x + 0.044715 * (x * x * x))))


def _sigmoid(x):
    return 1.0 / (1.0 + jnp.exp(-x))


def _silu_mul(g, u):
    return g * _sigmoid(g) * u


def _logsig(x):
    return jnp.minimum(x, 0.0) - jnp.log(1.0 + jnp.exp(-jnp.abs(x)))


def _colsum(x):
    return jnp.sum(x, axis=0, keepdims=True)


def _acc(ref, val, first):
    @pl.when(first)
    def _():
        ref[...] = val

    @pl.when(jnp.logical_not(first))
    def _():
        ref[...] += val


def _inproj_fwd(x2d, g_pre, w_in_p, tm):
    T, D = x2d.shape
    nchunk = P_COLS // 384

    def body(x_ref, g_ref, w_ref, h_ref, proj_ref, fl_ref):
        h = _rms(x_ref[...], g_ref[...]).astype(BF16)
        h_ref[...] = h
        for n in range(nchunk):
            r = jnp.dot(h, w_ref[:, n * 384:(n + 1) * 384], preferred_element_type=F32)
            if n < nchunk - 1:
                proj_ref[:, n * 384:(n + 1) * 384] = r.astype(BF16)
            else:
                proj_ref[:, n * 384:n * 384 + 256] = r[:, :256].astype(BF16)
                fl_ref[...] = r[:, 256:384]

    return pl.pallas_call(
        body, name="inproj_fwd", grid=(T // tm,),
        in_specs=[BS((tm, D), lambda i: (i, 0)), BS((1, D), lambda i: (0, 0)), BS((D, P_COLS), lambda i: (0, 0))],
        out_specs=[BS((tm, D), lambda i: (i, 0)), BS((tm, P_MAIN), lambda i: (i, 0)), BS((tm, LANES), lambda i: (i, 0))],
        out_shape=[jax.ShapeDtypeStruct((T, D), BF16), jax.ShapeDtypeStruct((T, P_MAIN), BF16),
                   jax.ShapeDtypeStruct((T, LANES), F32)],
        compiler_params=_cp(("arbitrary",)),
    )(x2d, g_pre, w_in_p)


def _gate_fwd(flog3, bf_row):
    Bl, S, _ = flog3.shape
    nb = S // LANES

    def body(f_ref, b_ref, bq_ref, bk_ref, fr_ref):
        row = _iota((LANES, LANES), 0)
        lane = _iota((LANES, LANES), 1)
        one = jnp.ones((LANES, LANES), BF16)
        zero = jnp.zeros((LANES, LANES), BF16)

        def blk(j, carry):
            r0 = pl.multiple_of(j * LANES, LANES)
            fl = f_ref[0, pl.ds(r0, LANES), :] + b_ref[...]
            fr_ref[0, j] = fl.T[0:8, :]
            c = _logsig(fl)
            for k in (1, 2, 4, 8, 16, 32, 64):
                c = c + jnp.where(row >= k, pltpu.roll(c, k, 0), 0.0)
            c = c + carry
            for h in range(N_FOX_HEADS):
                col = jnp.sum(jnp.where(lane == h, c, 0.0), axis=1, keepdims=True)
                hi = col.astype(BF16)
                rest = col - hi.astype(F32)
                mid = rest.astype(BF16)
                lo = (rest - mid.astype(F32)).astype(BF16)
                base = _bias_lane(h)
                bq = jnp.where(lane == base, hi, jnp.where(lane == base + 1, mid, jnp.where(lane == base + 2, lo, zero)))
                bq = jnp.where((lane >= base + 3) & (lane < base + 6), one, bq)
                bk = jnp.where(lane == base + 3, -hi, jnp.where(lane == base + 4, -mid, jnp.where(lane == base + 5, -lo, zero)))
                bk = jnp.where((lane >= base) & (lane < base + 3), one, bk)
                bq_ref[0, h, pl.ds(r0, LANES), :] = bq
                bk_ref[0, h, pl.ds(r0, LANES), :] = bk
            return _colsum(jnp.where(row == LANES - 1, c, 0.0))

        lax.fori_loop(0, nb, blk, jnp.zeros((1, LANES), F32))

    slab = BS((1, N_FOX_HEADS, S, LANES), lambda b: (b, 0, 0, 0))
    return pl.pallas_call(
        body, name="gate_fwd", grid=(Bl,),
        in_specs=[BS((1, S, LANES), lambda b: (b, 0, 0)), BS((1, LANES), lambda b: (0, 0))],
        out_specs=[slab, slab, BS((1, nb, 8, LANES), lambda b: (b, 0, 0, 0))],
        out_shape=[jax.ShapeDtypeStruct((Bl, N_FOX_HEADS, S, LANES), BF16),
                   jax.ShapeDtypeStruct((Bl, N_FOX_HEADS, S, LANES), BF16),
                   jax.ShapeDtypeStruct((Bl, nb, 8, LANES), F32)],
        compiler_params=_cp(("arbitrary",)),
    )(flog3, bf_row)


def _bias_lane(h):
    return HEAD if h % 2 == 0 else 0


def _sgu_pre(zu, zv, g_sgu):
    return _gelu(zu), _rms(_gelu(zv), g_sgu)


def _sgu_fwd(proj, g_sgu, ws_tril, bs_full, tm):
    T = proj.shape[0]
    nch = tm // CHUNK

    def body(zu_ref, zv_ref, g_ref, ws_ref, b_ref, ya_ref):
        lane = _iota((CHUNK, LANES), 1)
        u, vn = _sgu_pre(zu_ref[...].astype(F32), zv_ref[...].astype(F32), g_ref[...])
        vn = vn.astype(BF16)
        for c in range(nch):
            rs = slice(c * CHUNK, (c + 1) * CHUNK)
            for j in range(3):
                cs = slice(j * LANES, (j + 1) * LANES)
                vp = vn[rs, cs]
                z = jnp.where(lane < HEAD, _dot(ws_ref[2 * j], vp), _dot(ws_ref[2 * j + 1], vp)) + b_ref[:, cs]
                ya_ref[rs, cs] = u[rs, cs] * z

    return pl.pallas_call(
        body, name="sgu_fwd", grid=(T // tm,),
        in_specs=[BS((tm, A_W), lambda i: (i, 0)), BS((tm, A_W), lambda i: (i, 1)), BS((1, A_W), lambda i: (0, 0)),
                  BS((6, CHUNK, CHUNK), lambda i: (0, 0, 0)), BS((CHUNK, A_W), lambda i: (0, 0))],
        out_specs=BS((tm, A_W), lambda i: (i, 0)),
        out_shape=jax.ShapeDtypeStruct((T, A_W), F32),
        compiler_params=_cp(("arbitrary",)),
    )(proj, proj, g_sgu, ws_tril, bs_full)


def _fox_fwd(proj, bq, bk, Bl, S):
    T = Bl * S
    nq = S // Q_BLK
    qc, kc, vc = 768 // LANES, 1152 // LANES, 1536 // LANES

    def body(q_ref, k_ref, v_ref, bq_ref, bk_ref, o_ref, lse_ref, ka_ref, va_ref):
        lane_s = _iota((S, LANES), 1)
        lane = _iota((Q_BLK, LANES), 1)
        tri = _iota((Q_BLK, Q_BLK), 1) <= _iota((Q_BLK, Q_BLK), 0)
        k = k_ref[...]
        v = v_ref[...]
        for hh in range(2):
            data = (lane_s < HEAD) if hh == 0 else (lane_s >= HEAD)
            ka_ref[hh] = jnp.where(data, k, bk_ref[0, hh])
            va_ref[hh] = jnp.where(lane_s == _bias_lane(hh), jnp.ones_like(v), v)
        for i in range(nq):
            r0 = i * Q_BLK
            q = q_ref[r0:r0 + Q_BLK, :]
            o_out = jnp.zeros((Q_BLK, LANES), F32)
            lse_out = jnp.zeros((Q_BLK, LANES), F32)
            for hh in range(2):
                hmask = (lane < HEAD) if hh == 0 else (lane >= HEAD)
                qa = jnp.where(hmask, q * 0.125, bq_ref[0, hh, r0:r0 + Q_BLK, :])
                sd = jnp.where(tri, _dot_nt(qa, ka_ref[hh, r0:r0 + Q_BLK, :]), NEG)
                m = jnp.max(sd, axis=1, keepdims=True)
                if i:
                    sf = _dot_nt(qa, ka_ref[hh, 0:r0, :])
                    m = jnp.maximum(m, jnp.max(sf, axis=1, keepdims=True))
                acc = _dot(jnp.exp(sd - m), va_ref[hh, r0:r0 + Q_BLK, :])
                if i:
                    acc = acc + _dot(jnp.exp(sf - m), va_ref[hh, 0:r0, :])
                l = jnp.sum(jnp.where(lane == _bias_lane(hh), acc, 0.0), axis=1, keepdims=True)
                o_out = jnp.where(hmask, acc / l, o_out)
                lse_out = jnp.where(hmask, m + jnp.log(l), lse_out)
            o_ref[r0:r0 + Q_BLK, :] = o_out
            lse_ref[0, r0:r0 + Q_BLK, :] = lse_out

    seq = lambda c0: BS((S, LANES), lambda b, p: (b, c0 + p))
    pair = BS((1, 2, S, LANES), lambda b, p: (b, p, 0, 0))
    return pl.pallas_call(
        body, name="fox_fwd", grid=(Bl, 3),
        in_specs=[seq(qc), seq(kc), seq(vc), pair, pair],
        out_specs=[seq(0), BS((1, S, LANES), lambda b, p: (p, b, 0))],
        out_shape=[jax.ShapeDtypeStruct((T, B_W), F32), jax.ShapeDtypeStruct((3, T, LANES), F32)],
        scratch_shapes=[pltpu.VMEM((2, S, LANES), BF16), pltpu.VMEM((2, S, LANES), BF16)],
        compiler_params=_cp(("arbitrary", "arbitrary")),
    )(proj, proj, proj, bq, bk)


def _memkv_fwd(mem, g_mem, w_kv):
    Bl, Mt, D = mem.shape

    def body(m_ref, g_ref, w_ref, mn_ref, kv_ref):
        mn = _rms(m_ref[0], g_ref[...]).astype(BF16)
        mn_ref[0] = mn
        kv_ref[0] = jnp.dot(mn, w_ref[...], preferred_element_type=F32).astype(BF16)

    return pl.pallas_call(
        body, name="memkv_fwd", grid=(Bl,),
        in_specs=[BS((1, Mt, D), lambda b: (b, 0, 0)), BS((1, D), lambda b: (0, 0)), BS((D, 2 * M_W), lambda b: (0, 0))],
        out_specs=[BS((1, Mt, D), lambda b: (b, 0, 0)), BS((1, Mt, 2 * M_W), lambda b: (b, 0, 0))],
        out_shape=[jax.ShapeDtypeStruct((Bl, Mt, D), BF16), jax.ShapeDtypeStruct((Bl, Mt, 2 * M_W), BF16)],
        compiler_params=_cp(("arbitrary",)),
    )(mem, g_mem, w_kv)


def _memattn_fwd(proj, kv, Bl, S, tq):
    T = Bl * S
    nq = S // tq
    Mt = kv.shape[1]
    qc = 1920 // LANES

    def body(q_ref, km_ref, vm_ref, o_ref):
        lane = _iota((tq, LANES), 1)
        q = q_ref[...]
        out = jnp.zeros((tq, LANES), F32)
        for hh in range(2):
            hmask = (lane < HEAD) if hh == 0 else (lane >= HEAD)
            qs = jnp.where(hmask, q, jnp.zeros_like(q)) * 0.125
            s = _dot_nt(qs, km_ref[0])
            pe = jnp.exp(s - jnp.max(s, axis=1, keepdims=True))
            pn = pe / jnp.sum(pe, axis=1, keepdims=True)
            out = jnp.where(hmask, _dot(pn, vm_ref[0]), out)
        o_ref[...] = out

    return pl.pallas_call(
        body, name="memattn_fwd", grid=(Bl, 2, nq),
        in_specs=[BS((tq, LANES), lambda b, p, i: (b * nq + i, qc + p)),
                  BS((1, Mt, LANES), lambda b, p, i: (b, 0, p)),
                  BS((1, Mt, LANES), lambda b, p, i: (b, 0, 2 + p))],
        out_specs=BS((tq, LANES), lambda b, p, i: (b * nq + i, p)),
        out_shape=jax.ShapeDtypeStruct((T, M_W), F32),
        compiler_params=_cp(("arbitrary", "arbitrary", "arbitrary")),
    )(proj, kv, kv)


def _mix_norms(ya, yb, ym, ga, gb, gm):
    return _rms(ya, ga), _rms(yb, gb), _rms(ym, gm)


def _outproj_fwd(ya, yb, ym, x2d, ga, gb, gm, g_post, g_pre2, w_out, tm):
    T, D = x2d.shape

    def body(ya_ref, yb_ref, ym_ref, x_ref, ga_ref, gb_ref, gm_ref, gp_ref, g2_ref, w_ref,
             y_ref, o_ref, x1_ref, h2_ref):
        na, nb_, nm = _mix_norms(ya_ref[...], yb_ref[...], ym_ref[...], ga_ref[...], gb_ref[...], gm_ref[...])
        y_ref[:, 0:A_W] = na.astype(BF16)
        y_ref[:, A_W:A_W + B_W] = nb_.astype(BF16)
        y_ref[:, A_W + B_W:] = nm.astype(BF16)
        o = jnp.dot(y_ref[...], w_ref[...], preferred_element_type=F32)
        o_ref[...] = o
        x1 = x_ref[...] + _rms(o, gp_ref[...])
        x1_ref[...] = x1
        h2_ref[...] = _rms(x1, g2_ref[...]).astype(BF16)

    row = lambda w: BS((tm, w), lambda i: (i, 0))
    vec = lambda w: BS((1, w), lambda i: (0, 0))
    return pl.pallas_call(
        body, name="outproj_fwd", grid=(T // tm,),
        in_specs=[row(A_W), row(B_W), row(M_W), row(D), vec(A_W), vec(B_W), vec(M_W), vec(D), vec(D),
                  BS((A_W + B_W + M_W, D), lambda i: (0, 0))],
        out_specs=[row(A_W + B_W + M_W), row(D), row(D), row(D)],
        out_shape=[jax.ShapeDtypeStruct((T, A_W + B_W + M_W), BF16), jax.ShapeDtypeStruct((T, D), F32),
                   jax.ShapeDtypeStruct((T, D), F32), jax.ShapeDtypeStruct((T, D), BF16)],
        compiler_params=_cp(("arbitrary",)),
    )(ya, yb, ym, x2d, ga, gb, gm, g_post, g_pre2, w_out)


def _ffn_fwd(h2, x1, target, wg, wu, wd, g_post, tm):
    T, D = x1.shape
    ns, _, F = wg.shape

    def body(h_ref, x1_ref, t_ref, wg_ref, wu_ref, wd_ref, gp_ref,
             gs_ref, us_ref, dff_ref, dx2_ref, dgp_ref, loss_ref, acc_ref):
        i = pl.program_id(0)
        j = pl.program_id(1)
        h = h_ref[...]
        g = jnp.dot(h, wg_ref[0], preferred_element_type=F32)
        u = jnp.dot(h, wu_ref[0], preferred_element_type=F32)
        gs_ref[0] = g.astype(BF16)
        us_ref[0] = u.astype(BF16)
        part = _dot(_silu_mul(g, u), wd_ref[0])
        _acc(acc_ref, part, j == 0)

        @pl.when(j == ns - 1)
        def _():
            normed, vjp = jax.vjp(_rms, acc_ref[...], gp_ref[...])
            diff = x1_ref[...] + normed - t_ref[...]
            dx2 = diff * (1.0 / D)
            dff, dgp = vjp(dx2)
            dx2_ref[...] = dx2
            dff_ref[...] = dff.astype(BF16)
            lpart = jnp.sum(_colsum(diff * diff), axis=1, keepdims=True) * (0.5 / D)
            _acc(dgp_ref, dgp, i == 0)
            _acc(loss_ref, jnp.broadcast_to(lpart, (1, LANES)), i == 0)

    row = lambda w: BS((tm, w), lambda i, j: (i, 0))
    return pl.pallas_call(
        body, name="ffn_fwd", grid=(T // tm, ns),
        in_specs=[row(D), row(D), row(D), BS((1, D, F), lambda i, j: (j, 0, 0)), BS((1, D, F), lambda i, j: (j, 0, 0)),
                  BS((1, F, D), lambda i, j: (j, 0, 0)), BS((1, D), lambda i, j: (0, 0))],
        out_specs=[BS((1, tm, F), lambda i, j: (j, i, 0)), BS((1, tm, F), lambda i, j: (j, i, 0)), row(D), row(D),
                   BS((1, D), lambda i, j: (0, 0)), BS((1, LANES), lambda i, j: (0, 0))],
        out_shape=[jax.ShapeDtypeStruct((ns, T, F), BF16), jax.ShapeDtypeStruct((ns, T, F), BF16),
                   jax.ShapeDtypeStruct((T, D), BF16), jax.ShapeDtypeStruct((T, D), F32),
                   jax.ShapeDtypeStruct((1, D), F32), jax.ShapeDtypeStruct((1, LANES), F32)],
        scratch_shapes=[pltpu.VMEM((tm, D), F32)],
        compiler_params=_cp(("arbitrary", "arbitrary")),
    )(h2, x1, target, wg, wu, wd, g_post)


def _ffn_bwd(dff, gs, us, x1, dx2, wg, wu, wd, g_pre2, tm):
    T, D = x1.shape
    ns, _, F = wg.shape

    def body(dff_ref, gs_ref, us_ref, x1_ref, dx2_ref, wg_ref, wu_ref, wd_ref, g2_ref,
             a_ref, dg_ref, du_ref, dx1_ref, dg2_ref, acc_ref):
        i = pl.program_id(0)
        j = pl.program_id(1)
        dact = _dot_nt(dff_ref[...], wd_ref[0])
        a, vjp = jax.vjp(_silu_mul, gs_ref[0].astype(F32), us_ref[0].astype(F32))
        dg, du = vjp(dact)
        a_ref[0] = a.astype(BF16)
        dg = dg.astype(BF16)
        du = du.astype(BF16)
        dg_ref[0] = dg
        du_ref[0] = du
        part = _dot_nt(dg, wg_ref[0]) + _dot_nt(du, wu_ref[0])
        _acc(acc_ref, part, j == 0)

        @pl.when(j == ns - 1)
        def _():
            _, vjp2 = jax.vjp(_rms, x1_ref[...], g2_ref[...])
            dxa, dg2 = vjp2(acc_ref[...])
            dx1_ref[...] = dx2_ref[...] + dxa
            _acc(dg2_ref, dg2, i == 0)

    row = lambda w: BS((tm, w), lambda i, j: (i, 0))
    sh = BS((1, tm, F), lambda i, j: (j, i, 0))
    return pl.pallas_call(
        body, name="ffn_bwd", grid=(T // tm, ns),
        in_specs=[row(D), sh, sh, row(D), row(D), BS((1, D, F), lambda i, j: (j, 0, 0)),
                  BS((1, D, F), lambda i, j: (j, 0, 0)), BS((1, F, D), lambda i, j: (j, 0, 0)),
                  BS((1, D), lambda i, j: (0, 0))],
        out_specs=[sh, sh, sh, row(D), BS((1, D), lambda i, j: (0, 0))],
        out_shape=[jax.ShapeDtypeStruct((ns, T, F), BF16)] * 3 + [jax.ShapeDtypeStruct((T, D), F32),
                                                                  jax.ShapeDtypeStruct((1, D), F32)],
        scratch_shapes=[pltpu.VMEM((tm, D), F32)],
        compiler_params=_cp(("arbitrary", "arbitrary")),
    )(dff, gs, us, x1, dx2, wg, wu, wd, g_pre2)


def _mm_tn(a, b, name, tk):
    ab, bb = a.ndim == 3, b.ndim == 3
    nbatch = a.shape[0] if ab else (b.shape[0] if bb else 1)
    T, M = a.shape[-2:]
    N = b.shape[-1]
    tk = min(tk, T)

    def body(a_ref, b_ref, o_ref):
        av = a_ref[0] if ab else a_ref[...]
        bv = b_ref[0] if bb else b_ref[...]
        _acc(o_ref, _dot_tn(av, bv)[None], pl.program_id(1) == 0)

    a_spec = BS((1, tk, M), lambda s, t: (s, t, 0)) if ab else BS((tk, M), lambda s, t: (t, 0))
    b_spec = BS((1, tk, N), lambda s, t: (s, t, 0)) if bb else BS((tk, N), lambda s, t: (t, 0))
    return pl.pallas_call(
        body, name=name, grid=(nbatch, T // tk),
        in_specs=[a_spec, b_spec],
        out_specs=BS((1, M, N), lambda s, t: (s, 0, 0)),
        out_shape=jax.ShapeDtypeStruct((nbatch, M, N), F32),
        compiler_params=_cp(("arbitrary", "arbitrary")),
    )(a, b)


def _outproj_bwd(dx1, o, ya, yb, ym, ga, gb, gm, g_post, w_out, tm):
    T, D = dx1.shape

    def body(dx1_ref, o_ref, ya_ref, yb_ref, ym_ref, ga_ref, gb_ref, gm_ref, gp_ref, w_ref,
             do_ref, dya_ref, dyb_ref, dym_ref, dga_ref, dgb_ref, dgm_ref, dgp_ref):
        first = pl.program_id(0) == 0
        _, vjp = jax.vjp(_rms, o_ref[...], gp_ref[...])
        do, dgp = vjp(dx1_ref[...])
        do = do.astype(BF16)
        do_ref[...] = do
        dy = _dot_nt(do, w_ref[...])
        _, vjp2 = jax.vjp(_mix_norms, ya_ref[...], yb_ref[...], ym_ref[...], ga_ref[...], gb_ref[...], gm_ref[...])
        dya, dyb, dym, dga, dgb, dgm = vjp2((dy[:, 0:A_W], dy[:, A_W:A_W + B_W], dy[:, A_W + B_W:]))
        dya_ref[...] = dya
        dyb_ref[...] = dyb
        dym_ref[...] = dym
        _acc(dga_ref, dga, first)
        _acc(dgb_ref, dgb, first)
        _acc(dgm_ref, dgm, first)
        _acc(dgp_ref, dgp, first)

    row = lambda w: BS((tm, w), lambda i: (i, 0))
    vec = lambda w: BS((1, w), lambda i: (0, 0))
    sds = jax.ShapeDtypeStruct
    return pl.pallas_call(
        body, name="outproj_bwd", grid=(T // tm,),
        in_specs=[row(D), row(D), row(A_W), row(B_W), row(M_W), vec(A_W), vec(B_W), vec(M_W), vec(D),
                  BS((A_W + B_W + M_W, D), lambda i: (0, 0))],
        out_specs=[row(D), row(A_W), row(B_W), row(M_W), vec(A_W), vec(B_W), vec(M_W), vec(D)],
        out_shape=[sds((T, D), BF16), sds((T, A_W), F32), sds((T, B_W), F32), sds((T, M_W), F32),
                   sds((1, A_W), F32), sds((1, B_W), F32), sds((1, M_W), F32), sds((1, D), F32)],
        compiler_params=_cp(("arbitrary",)),
    )(dx1, o, ya, yb, ym, ga, gb, gm, g_post, w_out)


def _sgu_bwd(proj, dya, g_sgu, ws_tril, bs_full, tm):
    T = proj.shape[0]
    nch = tm // CHUNK

    def body(zu_ref, zv_ref, dy_ref, g_ref, ws_ref, b_ref, dzu_ref, dzv_ref, dws_ref, dbs_ref, dg_ref,
             du_ref, dvn_ref, dbf_ref):
        step = pl.program_id(0)
        first = step == 0
        lane = _iota((CHUNK, LANES), 1)
        tril = _iota((CHUNK, CHUNK), 0) >= _iota((CHUNK, CHUNK), 1)
        (u, vn), vjp = jax.vjp(_sgu_pre, zu_ref[...].astype(F32), zv_ref[...].astype(F32), g_ref[...])
        vnb = vn.astype(BF16)
        dy = dy_ref[...]

        @pl.when(first)
        def _():
            dws_ref[...] = jnp.zeros_like(dws_ref)
            dbf_ref[...] = jnp.zeros_like(dbf_ref)

        for c in range(nch):
            rs = slice(c * CHUNK, (c + 1) * CHUNK)
            for j in range(3):
                cs = slice(j * LANES, (j + 1) * LANES)
                vp = vnb[rs, cs]
                z = jnp.where(lane < HEAD, _dot(ws_ref[2 * j], vp), _dot(ws_ref[2 * j + 1], vp)) + b_ref[:, cs]
                du_ref[rs, cs] = dy[rs, cs] * z
                dz = dy[rs, cs] * u[rs, cs]
                dbf_ref[:, cs] += dz
                dzb = dz.astype(BF16)
                dz0 = jnp.where(lane < HEAD, dzb, jnp.zeros_like(dzb))
                dz1 = jnp.where(lane >= HEAD, dzb, jnp.zeros_like(dzb))
                dvn_ref[rs, cs] = jnp.where(lane < HEAD, _dot_tn(ws_ref[2 * j], dzb), _dot_tn(ws_ref[2 * j + 1], dzb))
                dws_ref[2 * j] += jnp.where(tril, _dot_nt(dz0, vp), 0.0)
                dws_ref[2 * j + 1] += jnp.where(tril, _dot_nt(dz1, vp), 0.0)
        dzu, dzv, dg = vjp((du_ref[...], dvn_ref[...]))
        dzu_ref[...] = dzu.astype(BF16)
        dzv_ref[...] = dzv.astype(BF16)
        _acc(dg_ref, dg, first)

        @pl.when(step == pl.num_programs(0) - 1)
        def _():
            out = jnp.zeros((CHUNK, LANES), F32)
            for j in range(3):
                slab = dbf_ref[:, j * LANES:(j + 1) * LANES]
                lo = jnp.sum(jnp.where(lane < HEAD, slab, 0.0), axis=1, keepdims=True)
                hi = jnp.sum(jnp.where(lane >= HEAD, slab, 0.0), axis=1, keepdims=True)
                out = out + jnp.where(lane == 2 * j, lo, 0.0) + jnp.where(lane == 2 * j + 1, hi, 0.0)
            dbs_ref[...] = out

    return pl.pallas_call(
        body, name="sgu_bwd", grid=(T // tm,),
        in_specs=[BS((tm, A_W), lambda i: (i, 0)), BS((tm, A_W), lambda i: (i, 1)), BS((tm, A_W), lambda i: (i, 0)),
                  BS((1, A_W), lambda i: (0, 0)), BS((6, CHUNK, CHUNK), lambda i: (0, 0, 0)),
                  BS((CHUNK, A_W), lambda i: (0, 0))],
        out_specs=[BS((tm, A_W), lambda i: (i, 0)), BS((tm, A_W), lambda i: (i, 0)),
                   BS((6, CHUNK, CHUNK), lambda i: (0, 0, 0)), BS((CHUNK, LANES), lambda i: (0, 0)),
                   BS((1, A_W), lambda i: (0, 0))],
        out_shape=[jax.ShapeDtypeStruct((T, A_W), BF16), jax.ShapeDtypeStruct((T, A_W), BF16),
                   jax.ShapeDtypeStruct((6, CHUNK, CHUNK), F32), jax.ShapeDtypeStruct((CHUNK, LANES), F32),
                   jax.ShapeDtypeStruct((1, A_W), F32)],
        scratch_shapes=[pltpu.VMEM((tm, A_W), F32), pltpu.VMEM((tm, A_W), F32), pltpu.VMEM((CHUNK, A_W), F32)],
        compiler_params=_cp(("arbitrary",)),
    )(proj, proj, dya, g_sgu, ws_tril, bs_full)


def _memattn_bwd(proj, kv, dym, Bl, S, tq):
    T = Bl * S
    nq = S // tq
    Mt = kv.shape[1]
    qc = 1920 // LANES

    def body(q_ref, km_ref, vm_ref, do_ref, dq_ref, dkm_ref, dvm_ref):
        first = pl.program_id(2) == 0
        lane = _iota((tq, LANES), 1)
        q = q_ref[...]
        do = do_ref[...]
        dq_out = jnp.zeros((tq, LANES), F32)
        dkm = jnp.zeros((Mt, LANES), F32)
        dvm = jnp.zeros((Mt, LANES), F32)
        for hh in range(2):
            hmask = (lane < HEAD) if hh == 0 else (lane >= HEAD)
            qs = jnp.where(hmask, q, jnp.zeros_like(q)) * 0.125
            dom = jnp.where(hmask, do, 0.0).astype(BF16)
            s = _dot_nt(qs, km_ref[0])
            pe = jnp.exp(s - jnp.max(s, axis=1, keepdims=True))
            pn = pe / jnp.sum(pe, axis=1, keepdims=True)
            dp = _dot_nt(dom, vm_ref[0])
            ds = (pn * (dp - jnp.sum(pn * dp, axis=1, keepdims=True))).astype(BF16)
            dq_out = jnp.where(hmask, _dot(ds, km_ref[0]) * 0.125, dq_out)
            dkm = dkm + _dot_tn(ds, qs)
            dvm = dvm + _dot_tn(pn, dom)
        dq_ref[...] = dq_out.astype(BF16)
        _acc(dkm_ref, dkm[None], first)
        _acc(dvm_ref, dvm[None], first)

    return pl.pallas_call(
        body, name="memattn_bwd", grid=(Bl, 2, nq),
        in_specs=[BS((tq, LANES), lambda b, p, i: (b * nq + i, qc + p)),
                  BS((1, Mt, LANES), lambda b, p, i: (b, 0, p)),
                  BS((1, Mt, LANES), lambda b, p, i: (b, 0, 2 + p)),
                  BS((tq, LANES), lambda b, p, i: (b * nq + i, p))],
        out_specs=[BS((tq, LANES), lambda b, p, i: (b * nq + i, p)),
                   BS((1, Mt, LANES), lambda b, p, i: (b, 0, p)),
                   BS((1, Mt, LANES), lambda b, p, i: (b, 0, p))],
        out_shape=[jax.ShapeDtypeStruct((T, M_W), BF16), jax.ShapeDtypeStruct((Bl, Mt, M_W), F32),
                   jax.ShapeDtypeStruct((Bl, Mt, M_W), F32)],
        compiler_params=_cp(("arbitrary", "arbitrary", "arbitrary")),
    )(proj, kv, kv, dym)


def _memkv_bwd(dkm, dvm, memn, mem, g_mem, w_kv):
    Bl, Mt, D = mem.shape

    def body(dk_ref, dv_ref, mn_ref, m_ref, g_ref, w_ref, dw_ref, dg_ref):
        first = pl.program_id(0) == 0
        dk = dk_ref[0].astype(BF16)
        dv = dv_ref[0].astype(BF16)
        mn = mn_ref[0]
        dmn = _dot_nt(dk, w_ref[:, 0:M_W]) + _dot_nt(dv, w_ref[:, M_W:])
        _, vjp = jax.vjp(_rms, m_ref[0], g_ref[...])
        _, dg = vjp(dmn)
        _acc(dg_ref, dg, first)

        @pl.when(first)
        def _():
            dw_ref[...] = jnp.zeros_like(dw_ref)

        dw_ref[:, 0:M_W] += _dot_tn(mn, dk)
        dw_ref[:, M_W:] += _dot_tn(mn, dv)

    return pl.pallas_call(
        body, name="memkv_bwd", grid=(Bl,),
        in_specs=[BS((1, Mt, M_W), lambda b: (b, 0, 0)), BS((1, Mt, M_W), lambda b: (b, 0, 0)),
                  BS((1, Mt, D), lambda b: (b, 0, 0)), BS((1, Mt, D), lambda b: (b, 0, 0)),
                  BS((1, D), lambda b: (0, 0)), BS((D, 2 * M_W), lambda b: (0, 0))],
        out_specs=[BS((D, 2 * M_W), lambda b: (0, 0)), BS((1, D), lambda b: (0, 0))],
        out_shape=[jax.ShapeDtypeStruct((D, 2 * M_W), F32), jax.ShapeDtypeStruct((1, D), F32)],
        compiler_params=_cp(("arbitrary",)),
    )(dkm, dvm, memn, mem, g_mem, w_kv)


def _fox_bwd(proj, dyb, lse, bq, bk, Bl, S):
    T = Bl * S
    nq = S // Q_BLK
    nb = S // LANES
    qc, kc, vc = 768 // LANES, 1152 // LANES, 1536 // LANES

    def body(q_ref, k_ref, v_ref, do_ref, lse_ref, bq_ref, bk_ref,
             dq_ref, dk_ref, dv_ref, dcr_ref, ka_ref, dka_ref, dva_ref):
        p = pl.program_id(1)
        lane_s = _iota((S, LANES), 1)
        lane = _iota((Q_BLK, LANES), 1)
        sub = _iota((8, LANES), 0)
        tri = _iota((Q_BLK, Q_BLK), 1) <= _iota((Q_BLK, Q_BLK), 0)
        k = k_ref[...]
        for hh in range(2):
            data = (lane_s < HEAD) if hh == 0 else (lane_s >= HEAD)
            ka_ref[hh] = jnp.where(data, k, bk_ref[0, hh])
        dka_ref[...] = jnp.zeros_like(dka_ref)
        dva_ref[...] = jnp.zeros_like(dva_ref)

        @pl.when(p == 0)
        def _():
            dcr_ref[...] = jnp.zeros_like(dcr_ref)

        def add_colsums(ds, first_blk, h):
            cs = _colsum(ds)
            for jb in range(ds.shape[1] // LANES):
                dcr_ref[0, first_blk + jb] += jnp.where(sub == h, cs[:, jb * LANES:(jb + 1) * LANES], 0.0)

        for i in range(nq):
            r0 = i * Q_BLK
            r1 = r0 + Q_BLK
            q = q_ref[r0:r1, :]
            do = do_ref[r0:r1, :]
            lse_b = lse_ref[0, r0:r1, :]
            dq_out = jnp.zeros((Q_BLK, LANES), F32)
            for hh in range(2):
                hmask = (lane < HEAD) if hh == 0 else (lane >= HEAD)
                h = 2 * p + hh
                qs = jnp.where(hmask, q * 0.125, jnp.zeros_like(q))
                qa = jnp.where(hmask, q * 0.125, bq_ref[0, hh, r0:r1, :])
                dob = jnp.where(hmask, do, 0.0).astype(BF16)
                lse_h = jnp.sum(jnp.where(lane == hh * HEAD, lse_b, 0.0), axis=1, keepdims=True)
                pd = jnp.where(tri, jnp.exp(_dot_nt(qa, ka_ref[hh, r0:r1, :]) - lse_h), 0.0)
                dpd = _dot_nt(dob, v_ref[r0:r1, :])
                delta = jnp.sum(pd * dpd, axis=1, keepdims=True)
                psum = jnp.sum(pd, axis=1, keepdims=True)
                if i:
                    pf = jnp.exp(_dot_nt(qa, ka_ref[hh, 0:r0, :]) - lse_h)
                    dpf = _dot_nt(dob, v_ref[0:r0, :])
                    delta = delta + jnp.sum(pf * dpf, axis=1, keepdims=True)
                    psum = psum + jnp.sum(pf, axis=1, keepdims=True)
                delta = delta / psum
                dsd = pd * (dpd - delta)
                add_colsums(dsd, r0 // LANES, h)
                dsd = dsd.astype(BF16)
                dq_h = _dot(dsd, k_ref[r0:r1, :])
                dka_ref[r0:r1, :] += _dot_tn(dsd, qs)
                dva_ref[r0:r1, :] += _dot_tn(pd, dob)
                if i:
                    dsf = pf * (dpf - delta)
                    add_colsums(dsf, 0, h)
                    dsf = dsf.astype(BF16)
                    dq_h = dq_h + _dot(dsf, k_ref[0:r0, :])
                    dka_ref[0:r0, :] += _dot_tn(dsf, qs)
                    dva_ref[0:r0, :] += _dot_tn(pf, dob)
                dq_out = jnp.where(hmask, dq_h * 0.125, dq_out)
            dq_ref[r0:r1, :] = dq_out.astype(BF16)
        dk_ref[...] = dka_ref[...].astype(BF16)
        dv_ref[...] = dva_ref[...].astype(BF16)

    seq = lambda c0: BS((S, LANES), lambda b, p: (b, c0 + p))
    pair = BS((1, 2, S, LANES), lambda b, p: (b, p, 0, 0))
    rowblk = BS((1, nb, 8, LANES), lambda b, p: (b, 0, 0, 0))
    return pl.pallas_call(
        body, name="fox_bwd", grid=(Bl, 3),
        in_specs=[seq(qc), seq(kc), seq(vc), seq(0), BS((1, S, LANES), lambda b, p: (p, b, 0)), pair, pair],
        out_specs=[seq(0), seq(0), seq(0), rowblk],
        out_shape=[jax.ShapeDtypeStruct((T, B_W), BF16)] * 3 + [jax.ShapeDtypeStruct((Bl, nb, 8, LANES), F32)],
        scratch_shapes=[pltpu.VMEM((2, S, LANES), BF16), pltpu.VMEM((S, LANES), F32), pltpu.VMEM((S, LANES), F32)],
        compiler_params=_cp(("arbitrary", "arbitrary")),
    )(proj, proj, proj, dyb, lse, bq, bk)


def _gate_bwd(dc_row, fl_row):
    Bl, nb, _, _ = dc_row.shape

    def body(dc_ref, fl_ref, o_ref):
        lane = _iota((8, LANES), 1)

        def blk(jj, carry):
            j = nb - 1 - jj
            r = -dc_ref[0, j]
            for k in (1, 2, 4, 8, 16, 32, 64):
                r = r + jnp.where(lane < LANES - k, pltpu.roll(r, LANES - k, 1), 0.0)
            r = r + carry
            dfl = r * _sigmoid(-fl_ref[0, j])
            o_ref[0, pl.ds(pl.multiple_of(j * LANES, LANES), LANES), :] = jnp.concatenate(
                [dfl, jnp.zeros((LANES - 8, LANES), F32)], axis=0).T
            return jnp.sum(jnp.where(lane == 0, r, 0.0), axis=1, keepdims=True)

        lax.fori_loop(0, nb, blk, jnp.zeros((8, 1), F32))

    rowblk = BS((1, nb, 8, LANES), lambda b: (b, 0, 0, 0))
    return pl.pallas_call(
        body, name="gate_bwd", grid=(Bl,),
        in_specs=[rowblk, rowblk],
        out_specs=BS((1, nb * LANES, LANES), lambda b: (b, 0, 0)),
        out_shape=jax.ShapeDtypeStruct((Bl, nb * LANES, LANES), F32),
        compiler_params=_cp(("arbitrary",)),
    )(dc_row, fl_row)


def _inproj_bwd(dzu, dzv, dq, dk, dv, dqm, dfl, x2d, dx1, g_pre, w_in_p, tm):
    T, D = x2d.shape
    nchunk = P_COLS // 384

    def body(dzu_ref, dzv_ref, dq_ref, dk_ref, dv_ref, dqm_ref, dfl_ref, x_ref, dx1_ref, g_ref, w_ref,
             dp_ref, gx_ref, dg_ref, dbf_ref):
        first = pl.program_id(0) == 0
        dfl = dfl_ref[...]
        dp_ref[:, 0:384] = dzu_ref[...]
        dp_ref[:, 384:768] = dzv_ref[...]
        dp_ref[:, 768:1152] = dq_ref[...]
        dp_ref[:, 1152:1536] = dk_ref[...]
        dp_ref[:, 1536:1920] = dv_ref[...]
        dp_ref[:, 1920:2176] = dqm_ref[...]
        dp_ref[:, 2176:2304] = dfl.astype(BF16)
        dh = jnp.zeros((tm, D), F32)
        for n in range(nchunk):
            cs = slice(n * 384, (n + 1) * 384)
            dh = dh + _dot_nt(dp_ref[:, cs], w_ref[:, cs])
        _, vjp = jax.vjp(_rms, x_ref[...], g_ref[...])
        dxa, dg = vjp(dh)
        gx_ref[...] = dx1_ref[...] + dxa
        _acc(dg_ref, dg, first)
        _acc(dbf_ref, _colsum(dfl), first)

    row = lambda w: BS((tm, w), lambda i: (i, 0))
    return pl.pallas_call(
        body, name="inproj_bwd", grid=(T // tm,),
        in_specs=[row(A_W), row(A_W), row(B_W), row(B_W), row(B_W), row(M_W), row(LANES), row(D), row(D),
                  BS((1, D), lambda i: (0, 0)), BS((D, P_COLS), lambda i: (0, 0))],
        out_specs=[row(P_COLS), row(D), BS((1, D), lambda i: (0, 0)), BS((1, LANES), lambda i: (0, 0))],
        out_shape=[jax.ShapeDtypeStruct((T, P_COLS), BF16), jax.ShapeDtypeStruct((T, D), F32),
                   jax.ShapeDtypeStruct((1, D), F32), jax.ShapeDtypeStruct((1, LANES), F32)],
        compiler_params=_cp(("arbitrary",)),
    )(dzu, dzv, dq, dk, dv, dqm, dfl, x2d, dx1, g_pre, w_in_p)


def _local_step(x, mem, target, W, P):
    Bl, S, D = x.shape
    T = Bl * S
    tm = min(512, T)
    x2d = x.reshape(T, D)
    t2d = target.reshape(T, D)
    vec = lambda a: a.reshape(1, -1)
    bf_row = jnp.pad(P["b_f"].reshape(1, -1), ((0, 0), (0, LANES - N_FOX_HEADS)))
    tril = jnp.tril(jnp.ones((CHUNK, CHUNK), bool))
    ws_tril = jnp.where(tril[None], P["w_s"][0], 0.0).astype(BF16)
    bs_full = jnp.repeat(P["b_s"][0].T, HEAD, axis=1)
    g_pre, g_sgu = vec(P["g_pre_mix"]), vec(P["g_sgu"])
    ga, gb, gm = vec(P["g_out_a"]), vec(P["g_out_b"]), vec(P["g_out_m"])
    g_mem, g_post, g_pre2, g_post2 = vec(P["g_mem"]), vec(P["g_post_mix"]), vec(P["g_pre_ffn"]), vec(P["g_post_ffn"])

    h, proj, flog = _inproj_fwd(x2d, g_pre, W["w_in"], tm)
    bq, bk, fl_row = _gate_fwd(flog.reshape(Bl, S, LANES), bf_row)
    ya = _sgu_fwd(proj, g_sgu, ws_tril, bs_full, tm)
    yb, lse = _fox_fwd(proj, bq, bk, Bl, S)
    memn, kv = _memkv_fwd(mem, g_mem, W["w_mem_kv"])
    ym = _memattn_fwd(proj, kv, Bl, S, min(512, S))
    y, o, x1, h2 = _outproj_fwd(ya, yb, ym, x2d, ga, gb, gm, g_post, g_pre2, W["w_out"], tm)
    gs, us, dff, dx2, dg_post2, loss = _ffn_fwd(h2, x1, t2d, W["w_gate"], W["w_up"], W["w_down"], g_post2, tm)

    a, dgs, dus, dx1, dg_pre2 = _ffn_bwd(dff, gs, us, x1, dx2, W["w_gate"], W["w_up"], W["w_down"], g_pre2, tm)
    d_w_down = _mm_tn(a, dff, "dw_down", 1024)
    d_w_gate = _mm_tn(h2, dgs, "dw_gate", 1024)
    d_w_up = _mm_tn(h2, dus, "dw_up", 1024)
    do, dya, dyb, dym, dga, dgb, dgm, dg_post = _outproj_bwd(dx1, o, ya, yb, ym, ga, gb, gm, g_post, W["w_out"], tm)
    d_w_out = _mm_tn(y, do, "dw_out", 1024)[0]
    dzu, dzv, dws, dbs_cols, dg_sgu = _sgu_bwd(proj, dya, g_sgu, ws_tril, bs_full, tm)
    dqm, dkm, dvm = _memattn_bwd(proj, kv, dym, Bl, S, min(512, S))
    d_w_kv, dg_mem = _memkv_bwd(dkm, dvm, memn, mem, g_mem, W["w_mem_kv"])
    dq, dk, dv, dc_row = _fox_bwd(proj, dyb, lse, bq, bk, Bl, S)
    dfl = _gate_bwd(dc_row, fl_row).reshape(T, LANES)
    dproj, grad_x, dg_pre, dbf = _inproj_bwd(dzu, dzv, dq, dk, dv, dqm, dfl, x2d, dx1, g_pre, W["w_in"], tm)
    d_w_in = _mm_tn(h, dproj, "dw_in", 1024)[0]

    big = {"w_in": d_w_in, "w_mem_kv": d_w_kv, "w_out": d_w_out, "w_gate": d_w_gate, "w_up": d_w_up,
           "w_down": d_w_down}
    small = {"g_pre_mix": dg_pre, "b_f": dbf[:, :N_FOX_HEADS], "g_sgu": dg_sgu, "w_s": dws, "b_s": dbs_cols[:, :N_FOX_HEADS].T,
             "g_out_a": dga, "g_out_b": dgb, "g_out_m": dgm, "g_mem": dg_mem, "g_post_mix": dg_post,
             "g_pre_ffn": dg_pre2, "g_post_ffn": dg_post2, "loss": loss[:, :1]}
    return grad_x.reshape(Bl, S, D), big, small


def _place():
    return lax.axis_index("x"), lax.axis_index("y"), lax.axis_index("c")


def _exchange(srcs, own_full, name):
    n = len(srcs)
    nd2d = 3 if own_full else 4

    def body(*refs):
        src, dst, own = refs[:n], refs[n:2 * n], refs[2 * n:3 * n]
        lsem, osem, isend, irecv, dsend, drecv = refs[3 * n:]
        x, y, c = _place()
        oc = 1 - c
        s_me = 2 * x + y
        sib = (x, y, oc)
        chips = [(1 - x, y), (x, 1 - y), (1 - x, 1 - y)]

        def remote(a, b, ssem, rsem, dev):
            return pltpu.make_async_remote_copy(src_ref=a, dst_ref=b, send_sem=ssem, recv_sem=rsem,
                                                device_id=dev, device_id_type=MESH)

        sends, loads, local = [], [], []
        for w in range(n):
            ld = pltpu.make_async_copy(src[w] if own_full else src[w].at[s_me], own[w], lsem.at[w])
            ld.start()
            loads.append(ld)
        for w in range(n):
            for j, (cx, cy) in enumerate(chips):
                half = src[w].at[c] if own_full else src[w].at[2 * cx + cy]
                cp = remote(half, dst[w].at[s_me, c], isend.at[w, j], irecv.at[w, j], (cx, cy, c))
                cp.start()
                sends.append(cp)
            if not own_full:
                cp = remote(src[w].at[s_me], dst[w].at[s_me, c], dsend.at[w, 3], drecv.at[w, 3], sib)
                cp.start()
                sends.append(cp)
        for w in range(n):
            loads[w].wait()
            st = pltpu.make_async_copy(own[w], dst[w].at[s_me] if own_full else dst[w].at[s_me, c], osem.at[w])
            st.start()
            local.append(st)
        for w in range(n):
            for j, (cx, cy) in enumerate(chips):
                landed = dst[w].at[2 * cx + cy, c]
                remote(landed, landed, isend.at[w, j], irecv.at[w, j], (cx, cy, c)).wait_recv()
                cp = remote(landed, landed, dsend.at[w, j], drecv.at[w, j], sib)
                cp.start()
                sends.append(cp)
        for w in range(n):
            for j, (cx, cy) in enumerate(chips):
                landed = dst[w].at[2 * cx + cy, oc]
                remote(landed, landed, dsend.at[w, j], drecv.at[w, j], sib).wait_recv()
            if not own_full:
                landed = dst[w].at[s_me, oc]
                remote(landed, landed, dsend.at[w, 3], drecv.at[w, 3], sib).wait_recv()
        for cp in sends:
            cp.wait_send()
        for loc in local:
            loc.wait()

    out_shape = [jax.ShapeDtypeStruct((4, 2) + s.shape[1:], s.dtype) for s in srcs]
    own_shapes = [pltpu.VMEM(s.shape if own_full else s.shape[1:], s.dtype) for s in srcs]
    return pl.pallas_call(
        body, name=name, in_specs=[ANY] * n, out_specs=[ANY] * n, out_shape=out_shape,
        scratch_shapes=own_shapes + [pltpu.SemaphoreType.DMA((n,)), pltpu.SemaphoreType.DMA((n,)),
                                     pltpu.SemaphoreType.DMA((n, 3)), pltpu.SemaphoreType.DMA((n, 3)),
                                     pltpu.SemaphoreType.DMA((n, nd2d)), pltpu.SemaphoreType.DMA((n, nd2d))],
        compiler_params=pltpu.CompilerParams(vmem_limit_bytes=VMEM_LIMIT),
    )(*srcs)


def _sibling_swap(grads):
    n = len(grads)

    def body(*refs):
        g, theirs = refs[:n], refs[n:2 * n]
        ssem, rsem = refs[2 * n:]
        x, y, c = _place()
        cps = []
        for w in range(n):
            cp = pltpu.make_async_remote_copy(src_ref=g[w].at[:, 1 - c], dst_ref=theirs[w], send_sem=ssem.at[w],
                                              recv_sem=rsem.at[w], device_id=(x, y, 1 - c), device_id_type=MESH)
            cp.start()
            cps.append(cp)
        for cp in cps:
            cp.wait()

    half = [jax.ShapeDtypeStruct((4,) + g.shape[2:], g.dtype) for g in grads]
    return pl.pallas_call(
        body, name="sibling_swap", in_specs=[ANY] * n, out_specs=[ANY] * n, out_shape=half,
        scratch_shapes=[pltpu.SemaphoreType.DMA((n,)), pltpu.SemaphoreType.DMA((n,))],
    )(*grads)


def _add_pair(core, g, theirs, name):
    _, _, hr, C = g.shape

    def body(core_ref, g_ref, t_ref, o_ref):
        o_ref[0] = (g_ref[0, 0] + t_ref[0]).astype(BF16)

    blk = BS((1, hr, C), lambda s, core_ref: (s, 0, 0))
    return pl.pallas_call(
        body, name=name,
        grid_spec=pltpu.PrefetchScalarGridSpec(
            num_scalar_prefetch=1, grid=(4,),
            in_specs=[BS((1, 1, hr, C), lambda s, core_ref: (s, core_ref[0], 0, 0)), blk], out_specs=blk),
        out_shape=jax.ShapeDtypeStruct(theirs.shape, BF16), compiler_params=_cp(("arbitrary",)))(core, g, theirs)


def _sum_chips(r, name):
    _, _, hr, C = r.shape

    def body(r_ref, o_ref):
        o_ref[...] = ((r_ref[0, 0].astype(F32) + r_ref[1, 0].astype(F32)) + r_ref[2, 0].astype(F32)) + r_ref[3, 0].astype(F32)

    return pl.pallas_call(body, name=name, grid=(2,), in_specs=[BS((4, 1, hr, C), lambda h: (0, h, 0, 0))],
                          out_specs=BS((hr, C), lambda h: (h, 0)), out_shape=jax.ShapeDtypeStruct((2 * hr, C), F32),
                          compiler_params=_cp(("arbitrary",)))(r)


def _small_allreduce(part):
    R = part.shape[0]
    rs = R // 8
    masks = [(mx, my, mc) for mx in (0, 1) for my in (0, 1) for mc in (0, 1)][1:]

    def body(p_ref, o_ref, buf_ref, s1, r1, s2, r2):
        x, y, c = _place()
        d = 4 * x + 2 * y + c
        mine = pl.ds(pl.multiple_of(d * rs, 8), rs)
        peers = [((x + mx) % 2, (y + my) % 2, (c + mc) % 2) for mx, my, mc in masks]
        first, second = [], []
        for k, (px, py, pc) in enumerate(peers):
            theirs = pl.ds(pl.multiple_of((4 * px + 2 * py + pc) * rs, 8), rs)
            cp = pltpu.make_async_remote_copy(src_ref=p_ref.at[theirs, :], dst_ref=buf_ref.at[d], send_sem=s1.at[k],
                                              recv_sem=r1.at[k], device_id=(px, py, pc), device_id_type=MESH)
            cp.start()
            first.append(cp)
        buf_ref[d] = p_ref[mine, :]
        for k, (px, py, pc) in enumerate(peers):
            slot = buf_ref.at[4 * px + 2 * py + pc]
            pltpu.make_async_remote_copy(src_ref=slot, dst_ref=slot, send_sem=s1.at[k], recv_sem=r1.at[k],
                                         device_id=(px, py, pc), device_id_type=MESH).wait_recv()
        total = buf_ref[0]
        for k in range(1, 8):
            total = total + buf_ref[k]
        o_ref[mine, :] = total
        for k, (px, py, pc) in enumerate(peers):
            cp = pltpu.make_async_remote_copy(src_ref=o_ref.at[mine, :], dst_ref=o_ref.at[mine, :], send_sem=s2.at[k],
                                              recv_sem=r2.at[k], device_id=(px, py, pc), device_id_type=MESH)
            cp.start()
            second.append(cp)
        for k, (px, py, pc) in enumerate(peers):
            rows = o_ref.at[pl.ds(pl.multiple_of((4 * px + 2 * py + pc) * rs, 8), rs), :]
            pltpu.make_async_remote_copy(src_ref=rows, dst_ref=rows, send_sem=s2.at[k], recv_sem=r2.at[k],
                                         device_id=(px, py, pc), device_id_type=MESH).wait_recv()
        for cp in first + second:
            cp.wait_send()

    vm = pl.BlockSpec(memory_space=pltpu.VMEM)
    return pl.pallas_call(
        body, name="small_allreduce", in_specs=[vm], out_specs=vm, out_shape=jax.ShapeDtypeStruct(part.shape, F32),
        scratch_shapes=[pltpu.VMEM((8, rs, LANES), F32)] + [pltpu.SemaphoreType.DMA((7,))] * 4,
    )(part)


def _adamw(w, g, m, v, name):
    R, C = w.shape
    tr = R if R * C * 4 <= (1 << 21) else R // 2
    if tr % 8:
        tr = R
    c1 = 1.0 / (1.0 - ADAM_B1 ** ADAM_STEP)
    c2 = 1.0 / (1.0 - ADAM_B2 ** ADAM_STEP)

    def body(w_ref, g_ref, m_ref, v_ref, d_ref, mo_ref, vo_ref):
        g_ = g_ref[...]
        m_ = ADAM_B1 * m_ref[...] + (1.0 - ADAM_B1) * g_
        v_ = ADAM_B2 * v_ref[...] + (1.0 - ADAM_B2) * (g_ * g_)
        mo_ref[...] = m_
        vo_ref[...] = v_
        d_ref[...] = -ADAM_LR * ((m_ * c1) / (jnp.sqrt(v_ * c2) + ADAM_EPS) + ADAM_WD * w_ref[...])

    blk = BS((tr, C), lambda i: (i, 0))
    return pl.pallas_call(body, name=name, grid=(R // tr,), in_specs=[blk] * 4, out_specs=[blk] * 3,
                          out_shape=[jax.ShapeDtypeStruct((R, C), F32)] * 3, compiler_params=_cp(("arbitrary",)))(w, g, m, v)


SMALL = ("g_pre_mix", "b_f", "g_sgu", "w_s", "b_s", "g_out_a", "g_out_b", "g_out_m", "g_mem", "g_post_mix",
         "g_pre_ffn", "g_post_ffn")
BIG = ("w_in", "w_mem_kv", "w_out", "w_gate", "w_up", "w_down")
WEIGHTS = ("g_pre_mix", "w_in", "b_f", "g_sgu", "w_s", "b_s", "g_out_a", "g_out_b", "g_out_m", "g_mem", "w_mem_kv",
           "w_out", "g_post_mix", "g_pre_ffn", "w_gate", "w_up", "w_down", "g_post_ffn")


def _rows_of(n):
    return -(-n // (8 * LANES)) * 8


def _pack(parts):
    tiles = []
    for a in parts:
        flat = a.reshape(-1).astype(F32)
        rows = _rows_of(flat.shape[0])
        tiles.append(jnp.pad(flat, (0, rows * LANES - flat.shape[0])).reshape(rows, LANES))
    total = sum(t.shape[0] for t in tiles)
    pad = -total % 64
    if pad:
        tiles.append(jnp.zeros((pad, LANES), F32))
    return jnp.concatenate(tiles, axis=0)


def _unpack(packed, shapes):
    out, r = [], 0
    for shp in shapes:
        n = 1
        for s in shp:
            n *= s
        rows = _rows_of(n)
        out.append(packed[r:r + rows].reshape(-1)[:n].reshape(shp))
        r += rows
    return out


def kernel(x, mem, g_pre_mix, w_in, b_f, g_sgu, w_s, b_s, g_out_a, g_out_b, g_out_m, g_mem, w_mem_kv, w_out, g_post_mix, g_pre_ffn, w_gate, w_up, w_down, g_post_ffn, loss_target, m_g_pre_mix, m_w_in, m_b_f, m_g_sgu, m_w_s, m_b_s, m_g_out_a, m_g_out_b, m_g_out_m, m_g_mem, m_w_mem_kv, m_w_out, m_g_post_mix, m_g_pre_ffn, m_w_gate, m_w_up, m_w_down, m_g_post_ffn, v_g_pre_mix, v_w_in, v_b_f, v_g_sgu, v_w_s, v_b_s, v_g_out_a, v_g_out_b, v_g_out_m, v_g_mem, v_w_mem_kv, v_w_out, v_g_post_mix, v_g_pre_ffn, v_w_gate, v_w_up, v_w_down, v_g_post_ffn):
    Wt = dict(g_pre_mix=g_pre_mix, w_in=w_in, b_f=b_f, g_sgu=g_sgu, w_s=w_s, b_s=b_s, g_out_a=g_out_a, g_out_b=g_out_b,
              g_out_m=g_out_m, g_mem=g_mem, w_mem_kv=w_mem_kv, w_out=w_out, g_post_mix=g_post_mix, g_pre_ffn=g_pre_ffn,
              w_gate=w_gate, w_up=w_up, w_down=w_down, g_post_ffn=g_post_ffn)
    Mo = dict(g_pre_mix=m_g_pre_mix, w_in=m_w_in, b_f=m_b_f, g_sgu=m_g_sgu, w_s=m_w_s, b_s=m_b_s, g_out_a=m_g_out_a,
              g_out_b=m_g_out_b, g_out_m=m_g_out_m, g_mem=m_g_mem, w_mem_kv=m_w_mem_kv, w_out=m_w_out,
              g_post_mix=m_g_post_mix, g_pre_ffn=m_g_pre_ffn, w_gate=m_w_gate, w_up=m_w_up, w_down=m_w_down,
              g_post_ffn=m_g_post_ffn)
    Vo = dict(g_pre_mix=v_g_pre_mix, w_in=v_w_in, b_f=v_b_f, g_sgu=v_g_sgu, w_s=v_w_s, b_s=v_b_s, g_out_a=v_g_out_a,
              g_out_b=v_g_out_b, g_out_m=v_g_out_m, g_mem=v_g_mem, w_mem_kv=v_w_mem_kv, w_out=v_w_out,
              g_post_mix=v_g_post_mix, g_pre_ffn=v_g_pre_ffn, w_gate=v_w_gate, w_up=v_w_up, w_down=v_w_down,
              g_post_ffn=v_g_post_ffn)

    def regroup(w):
        return jnp.concatenate([w[:, :1920], w[:, 1926:IN_COLS], w[:, 1920:1926],
                                jnp.zeros((w.shape[0], P_COLS - IN_COLS), w.dtype)], axis=1)

    def ungroup(g):
        return jnp.concatenate([g[:, :1920], g[:, P_MAIN:P_MAIN + N_FOX_HEADS], g[:, 1920:P_MAIN]], axis=1)

    shards = {n: (regroup(Wt[n][0]) if n == "w_in" else Wt[n][0]) for n in BIG}
    srcs = [shards[n].astype(BF16).reshape(2, shards[n].shape[0] // 2, shards[n].shape[1]) for n in BIG]
    fulls = _exchange(srcs, True, "gather_weights")
    W = {}
    for n, f in zip(BIG, fulls):
        _, _, hr, C = f.shape
        W[n] = f.reshape(4, 2 * hr, C) if n in ("w_gate", "w_up", "w_down") else f.reshape(8 * hr, C)

    P = {n: Wt[n] for n in SMALL}
    grad_x, big, small = _local_step(x, mem, loss_target, W, P)

    g4 = []
    for n in BIG:
        g = big[n]
        C = g.shape[-1]
        g4.append(g.reshape(4, 2, -1, C))
    theirs = _sibling_swap(g4)
    core = lax.axis_index("c").astype(jnp.int32).reshape(1)
    chip_sums = [_add_pair(core, g, t, "chip_sum_" + n) for n, g, t in zip(BIG, g4, theirs)]
    landed = _exchange(chip_sums, False, "scatter_grads")
    grads, deltas, new_m, new_v = {}, {}, {}, {}
    for n, r in zip(BIG, landed):
        g = _sum_chips(r, "sum_chips_" + n)
        if n == "w_in":
            g = ungroup(g)
        d, m1, v1 = _adamw(Wt[n][0], g, Mo[n][0], Vo[n][0], "adamw_" + n)
        grads[n], deltas[n], new_m[n], new_v[n] = g[None], d[None], m1[None], v1[None]

    total = _small_allreduce(_pack([small[n] for n in SMALL] + [small["loss"]]))
    slot = [jnp.zeros((1, 1), F32)]
    shapes = [Wt[n].shape for n in SMALL] + [(1, 1)]
    d, m1, v1 = _adamw(_pack([Wt[n] for n in SMALL] + slot), total, _pack([Mo[n] for n in SMALL] + slot),
                       _pack([Vo[n] for n in SMALL] + slot), "adamw_small")
    g_s, d_s, m_s, v_s = _unpack(total, shapes), _unpack(d, shapes), _unpack(m1, shapes), _unpack(v1, shapes)
    for k, n in enumerate(SMALL):
        grads[n], deltas[n], new_m[n], new_v[n] = g_s[k], d_s[k], m_s[k], v_s[k]
    loss = g_s[-1][0, 0]

    return (loss, grad_x, *[grads[n] for n in WEIGHTS], *[deltas[n] for n in WEIGHTS],
            *[new_m[n] for n in WEIGHTS], *[new_v[n] for n in WEIGHTS])
```

```python
import functools

import jax
import jax.numpy as jnp
from jax import lax
from jax.experimental import pallas as pl
from jax.experimental.pallas import tpu as pltpu

F32 = jnp.float32
BF16 = jnp.bfloat16
EPS = 1e-6
NEG = -1e30
HEAD = 64
A_W, B_W, M_W = 384, 384, 256
N_FOX_HEADS = 6
CHUNK = 128
IN_COLS = 2 * A_W + 3 * B_W + N_FOX_HEADS + M_W
P_MAIN = 2 * A_W + 3 * B_W + M_W
P_COLS = P_MAIN + 128
F_END = 2 * A_W + 3 * B_W + N_FOX_HEADS
LANES = 128
Q_BLK, K_BLK = 256, 128
ADAM_LR, ADAM_B1, ADAM_B2, ADAM_EPS, ADAM_WD, ADAM_STEP = 0.001, 0.9, 0.999, 1e-08, 0.01, 10
VMEM_LIMIT = 56 * 1024 * 1024
MESH = pl.DeviceIdType.MESH
ANY = pl.BlockSpec(memory_space=pl.ANY)
BS = pl.BlockSpec


def _cp(sem=None):
    return pltpu.CompilerParams(dimension_semantics=sem, vmem_limit_bytes=VMEM_LIMIT)


def _iota(shape, dim):
    return lax.broadcasted_iota(jnp.int32, shape, dim)


def _dot(a, b):
    return jnp.dot(a.astype(BF16), b.astype(BF16), preferred_element_type=F32)


def _dot_nt(a, b):
    return lax.dot_general(a.astype(BF16), b.astype(BF16), (((1,), (1,)), ((), ())), preferred_element_type=F32)


def _dot_tn(a, b):
    return lax.dot_general(a.astype(BF16), b.astype(BF16), (((0,), (0,)), ((), ())), preferred_element_type=F32)


def _rms(x, g):
    return x * lax.rsqrt(jnp.mean(x * x, axis=-1, keepdims=True) + EPS) * g


def _gelu(x):
    return 0.5 * x * (1.0 + jnp.tanh(0.7978845608028654 * (x + 0.044715 * (x * x * x))))


def _sigmoid(x):
    return 1.0 / (1.0 + jnp.exp(-x))


def _silu_mul(g, u):
    return g * _sigmoid(g) * u


def _logsig(x):
    return jnp.minimum(x, 0.0) - jnp.log(1.0 + jnp.exp(-jnp.abs(x)))


def _colsum(x):
    return jnp.sum(x, axis=0, keepdims=True)


def _acc(ref, val, first):
    @pl.when(first)
    def _():
        ref[...] = val

    @pl.when(jnp.logical_not(first))
    def _():
        ref[...] += val


def _inproj_fwd(x2d, g_pre, w_in_p, tm):
    T, D = x2d.shape
    nchunk = P_COLS // 384
    ns, _, dsh = w_in_p.shape

    def body(x_ref, g_ref, w_ref, h_ref, proj_ref, fl_ref):
        h = _rms(x_ref[...], g_ref[...]).astype(BF16)
        h_ref[...] = h
        for n in range(nchunk):
            r = _dot_nt(h[:, 0:dsh], w_ref[0, n * 384:(n + 1) * 384, :])
            for s in range(1, ns):
                r = r + _dot_nt(h[:, s * dsh:(s + 1) * dsh], w_ref[s, n * 384:(n + 1) * 384, :])
            if n < nchunk - 1:
                proj_ref[:, n * 384:(n + 1) * 384] = r.astype(BF16)
            else:
                fl_ref[...] = r[:, :LANES]
                proj_ref[:, n * 384:n * 384 + M_W] = r[:, LANES:].astype(BF16)

    return pl.pallas_call(
        body, name="inproj_fwd", grid=(T // tm,),
        in_specs=[BS((tm, D), lambda i: (i, 0)), BS((1, D), lambda i: (0, 0)),
                  BS((ns, P_COLS, dsh), lambda i: (0, 0, 0))],
        out_specs=[BS((tm, D), lambda i: (i, 0)), BS((tm, P_MAIN), lambda i: (i, 0)), BS((tm, LANES), lambda i: (i, 0))],
        out_shape=[jax.ShapeDtypeStruct((T, D), BF16), jax.ShapeDtypeStruct((T, P_MAIN), BF16),
                   jax.ShapeDtypeStruct((T, LANES), F32)],
        compiler_params=_cp(("arbitrary",)),
    )(x2d, g_pre, w_in_p)


def _gate_fwd(flog3, bf_row):
    Bl, S, _ = flog3.shape
    nb = S // LANES

    def body(f_ref, b_ref, bq_ref, bk_ref, fr_ref):
        row = _iota((LANES, LANES), 0)
        lane = _iota((LANES, LANES), 1)
        one = jnp.ones((LANES, LANES), BF16)
        zero = jnp.zeros((LANES, LANES), BF16)

        def blk(j, carry):
            r0 = pl.multiple_of(j * LANES, LANES)
            fl = f_ref[0, pl.ds(r0, LANES), :] + b_ref[...]
            fr_ref[0, j] = fl.T[0:8, :]
            c = _logsig(fl)
            for k in (1, 2, 4, 8, 16, 32, 64):
                c = c + jnp.where(row >= k, pltpu.roll(c, k, 0), 0.0)
            c = c + carry
            for h in range(N_FOX_HEADS):
                col = jnp.sum(jnp.where(lane == h, c, 0.0), axis=1, keepdims=True)
                hi = col.astype(BF16)
                rest = col - hi.astype(F32)
                mid = rest.astype(BF16)
                lo = (rest - mid.astype(F32)).astype(BF16)
                base = _bias_lane(h)
                bq = jnp.where(lane == base, hi, jnp.where(lane == base + 1, mid, jnp.where(lane == base + 2, lo, zero)))
                bq = jnp.where((lane >= base + 3) & (lane < base + 6), one, bq)
                bk = jnp.where(lane == base + 3, -hi, jnp.where(lane == base + 4, -mid, jnp.where(lane == base + 5, -lo, zero)))
                bk = jnp.where((lane >= base) & (lane < base + 3), one, bk)
                bq_ref[0, h, pl.ds(r0, LANES), :] = bq
                bk_ref[0, h, pl.ds(r0, LANES), :] = bk
            return _colsum(jnp.where(row == LANES - 1, c, 0.0))

        lax.fori_loop(0, nb, blk, jnp.zeros((1, LANES), F32))

    slab = BS((1, N_FOX_HEADS, S, LANES), lambda b: (b, 0, 0, 0))
    return pl.pallas_call(
        body, name="gate_fwd", grid=(Bl,),
        in_specs=[BS((1, S, LANES), lambda b: (b, 0, 0)), BS((1, LANES), lambda b: (0, 0))],
        out_specs=[slab, slab, BS((1, nb, 8, LANES), lambda b: (b, 0, 0, 0))],
        out_shape=[jax.ShapeDtypeStruct((Bl, N_FOX_HEADS, S, LANES), BF16),
                   jax.ShapeDtypeStruct((Bl, N_FOX_HEADS, S, LANES), BF16),
                   jax.ShapeDtypeStruct((Bl, nb, 8, LANES), F32)],
        compiler_params=_cp(("arbitrary",)),
    )(flog3, bf_row)


def _bias_lane(h):
    return HEAD if h % 2 == 0 else 0


def _sgu_pre(zu, zv, g_sgu):
    return _gelu(zu), _rms(_gelu(zv), g_sgu)


def _sgu_fwd(proj, g_sgu, ws_tril, bs_full, tm):
    T = proj.shape[0]
    nch = tm // CHUNK

    def body(zu_ref, zv_ref, g_ref, ws_ref, b_ref, ya_ref):
        lane = _iota((CHUNK, LANES), 1)
        u, vn = _sgu_pre(zu_ref[...].astype(F32), zv_ref[...].astype(F32), g_ref[...])
        vn = vn.astype(BF16)
        for c in range(nch):
            rs = slice(c * CHUNK, (c + 1) * CHUNK)
            for j in range(3):
                cs = slice(j * LANES, (j + 1) * LANES)
                vp = vn[rs, cs]
                z = jnp.where(lane < HEAD, _dot(ws_ref[2 * j], vp), _dot(ws_ref[2 * j + 1], vp)) + b_ref[:, cs]
                ya_ref[rs, cs] = u[rs, cs] * z

    return pl.pallas_call(
        body, name="sgu_fwd", grid=(T // tm,),
        in_specs=[BS((tm, A_W), lambda i: (i, 0)), BS((tm, A_W), lambda i: (i, 1)), BS((1, A_W), lambda i: (0, 0)),
                  BS((6, CHUNK, CHUNK), lambda i: (0, 0, 0)), BS((CHUNK, A_W), lambda i: (0, 0))],
        out_specs=BS((tm, A_W), lambda i: (i, 0)),
        out_shape=jax.ShapeDtypeStruct((T, A_W), F32),
        compiler_params=_cp(("arbitrary",)),
    )(proj, proj, g_sgu, ws_tril, bs_full)


def _fox_fwd(proj, bq, bk, Bl, S):
    T = Bl * S
    nq = S // Q_BLK
    qc, kc, vc = 768 // LANES, 1152 // LANES, 1536 // LANES

    def body(q_ref, k_ref, v_ref, bq_ref, bk_ref, o_ref, lse_ref, ka_ref, va_ref):
        lane_s = _iota((S, LANES), 1)
        lane = _iota((Q_BLK, LANES), 1)
        tri = _iota((Q_BLK, Q_BLK), 1) <= _iota((Q_BLK, Q_BLK), 0)
        k = k_ref[...]
        v = v_ref[...]
        for hh in range(2):
            data = (lane_s < HEAD) if hh == 0 else (lane_s >= HEAD)
            ka_ref[hh] = jnp.where(data, k, bk_ref[0, hh])
            va_ref[hh] = jnp.where(lane_s == _bias_lane(hh), jnp.ones_like(v), v)
        for i in range(nq):
            r0 = i * Q_BLK
            q = q_ref[r0:r0 + Q_BLK, :]
            o_out = jnp.zeros((Q_BLK, LANES), F32)
            lse_out = jnp.zeros((Q_BLK, LANES), F32)
            for hh in range(2):
                hmask = (lane < HEAD) if hh == 0 else (lane >= HEAD)
                qa = jnp.where(hmask, q * 0.125, bq_ref[0, hh, r0:r0 + Q_BLK, :])
                sd = jnp.where(tri, _dot_nt(qa, ka_ref[hh, r0:r0 + Q_BLK, :]), NEG)
                m = jnp.max(sd, axis=1, keepdims=True)
                if i:
                    sf = _dot_nt(qa, ka_ref[hh, 0:r0, :])
                    m = jnp.maximum(m, jnp.max(sf, axis=1, keepdims=True))
                acc = _dot(jnp.exp(sd - m), va_ref[hh, r0:r0 + Q_BLK, :])
                if i:
                    acc = acc + _dot(jnp.exp(sf - m), va_ref[hh, 0:r0, :])
                l = jnp.sum(jnp.where(lane == _bias_lane(hh), acc, 0.0), axis=1, keepdims=True)
                o_out = jnp.where(hmask, acc / l, o_out)
                lse_out = jnp.where(hmask, m + jnp.log(l), lse_out)
            o_ref[r0:r0 + Q_BLK, :] = o_out
            lse_ref[0, r0:r0 + Q_BLK, :] = lse_out

    seq = lambda c0: BS((S, LANES), lambda b, p: (b, c0 + p))
    pair = BS((1, 2, S, LANES), lambda b, p: (b, p, 0, 0))
    return pl.pallas_call(
        body, name="fox_fwd", grid=(Bl, 3),
        in_specs=[seq(qc), seq(kc), seq(vc), pair, pair],
        out_specs=[seq(0), BS((1, S, LANES), lambda b, p: (p, b, 0))],
        out_shape=[jax.ShapeDtypeStruct((T, B_W), F32), jax.ShapeDtypeStruct((3, T, LANES), F32)],
        scratch_shapes=[pltpu.VMEM((2, S, LANES), BF16), pltpu.VMEM((2, S, LANES), BF16)],
        compiler_params=_cp(("arbitrary", "arbitrary")),
    )(proj, proj, proj, bq, bk)


def _memkv_fwd(mem, g_mem, w_kv):
    Bl, Mt, D = mem.shape

    def body(m_ref, g_ref, w_ref, mn_ref, kv_ref):
        mn = _rms(m_ref[0], g_ref[...]).astype(BF16)
        mn_ref[0] = mn
        kv_ref[0] = jnp.dot(mn, w_ref[...], preferred_element_type=F32).astype(BF16)

    return pl.pallas_call(
        body, name="memkv_fwd", grid=(Bl,),
        in_specs=[BS((1, Mt, D), lambda b: (b, 0, 0)), BS((1, D), lambda b: (0, 0)), BS((D, 2 * M_W), lambda b: (0, 0))],
        out_specs=[BS((1, Mt, D), lambda b: (b, 0, 0)), BS((1, Mt, 2 * M_W), lambda b: (b, 0, 0))],
        out_shape=[jax.ShapeDtypeStruct((Bl, Mt, D), BF16), jax.ShapeDtypeStruct((Bl, Mt, 2 * M_W), BF16)],
        compiler_params=_cp(("arbitrary",)),
    )(mem, g_mem, w_kv)


def _memattn_fwd(proj, kv, Bl, S, tq):
    T = Bl * S
    nq = S // tq
    Mt = kv.shape[1]
    qc = 1920 // LANES

    def body(q_ref, km_ref, vm_ref, o_ref):
        lane = _iota((tq, LANES), 1)
        q = q_ref[...]
        out = jnp.zeros((tq, LANES), F32)
        for hh in range(2):
            hmask = (lane < HEAD) if hh == 0 else (lane >= HEAD)
            qs = jnp.where(hmask, q, jnp.zeros_like(q)) * 0.125
            s = _dot_nt(qs, km_ref[0])
            pe = jnp.exp(s - jnp.max(s, axis=1, keepdims=True))
            pn = pe / jnp.sum(pe, axis=1, keepdims=True)
            out = jnp.where(hmask, _dot(pn, vm_ref[0]), out)
        o_ref[...] = out

    return pl.pallas_call(
        body, name="memattn_fwd", grid=(Bl, 2, nq),
        in_specs=[BS((tq, LANES), lambda b, p, i: (b * nq + i, qc + p)),
                  BS((1, Mt, LANES), lambda b, p, i: (b, 0, p)),
                  BS((1, Mt, LANES), lambda b, p, i: (b, 0, 2 + p))],
        out_specs=BS((tq, LANES), lambda b, p, i: (b * nq + i, p)),
        out_shape=jax.ShapeDtypeStruct((T, M_W), F32),
        compiler_params=_cp(("arbitrary", "arbitrary", "arbitrary")),
    )(proj, kv, kv)


def _mix_norms(ya, yb, ym, ga, gb, gm):
    return _rms(ya, ga), _rms(yb, gb), _rms(ym, gm)


def _outproj_fwd(ya, yb, ym, x2d, ga, gb, gm, g_post, g_pre2, w_out, tm):
    T, D = x2d.shape

    def body(ya_ref, yb_ref, ym_ref, x_ref, ga_ref, gb_ref, gm_ref, gp_ref, g2_ref, w_ref,
             y_ref, o_ref, x1_ref, h2_ref):
        na, nb_, nm = _mix_norms(ya_ref[...], yb_ref[...], ym_ref[...], ga_ref[...], gb_ref[...], gm_ref[...])
        y_ref[:, 0:A_W] = na.astype(BF16)
        y_ref[:, A_W:A_W + B_W] = nb_.astype(BF16)
        y_ref[:, A_W + B_W:] = nm.astype(BF16)
        o = jnp.dot(y_ref[...], w_ref[...], preferred_element_type=F32)
        o_ref[...] = o
        x1 = x_ref[...] + _rms(o, gp_ref[...])
        x1_ref[...] = x1
        h2_ref[...] = _rms(x1, g2_ref[...]).astype(BF16)

    row = lambda w: BS((tm, w), lambda i: (i, 0))
    vec = lambda w: BS((1, w), lambda i: (0, 0))
    return pl.pallas_call(
        body, name="outproj_fwd", grid=(T // tm,),
        in_specs=[row(A_W), row(B_W), row(M_W), row(D), vec(A_W), vec(B_W), vec(M_W), vec(D), vec(D),
                  BS((A_W + B_W + M_W, D), lambda i: (0, 0))],
        out_specs=[row(A_W + B_W + M_W), row(D), row(D), row(D)],
        out_shape=[jax.ShapeDtypeStruct((T, A_W + B_W + M_W), BF16), jax.ShapeDtypeStruct((T, D), F32),
                   jax.ShapeDtypeStruct((T, D), F32), jax.ShapeDtypeStruct((T, D), BF16)],
        compiler_params=_cp(("arbitrary",)),
    )(ya, yb, ym, x2d, ga, gb, gm, g_post, g_pre2, w_out)


def _ffn_fwd(h2, x1, target, wg, wu, wd, g_post, tm):
    T, D = x1.shape
    ns, F, _ = wg.shape

    def body(h_ref, x1_ref, t_ref, wg_ref, wu_ref, wd_ref, gp_ref,
             gs_ref, us_ref, dff_ref, dx2_ref, dgp_ref, loss_ref, acc_ref):
        i = pl.program_id(0)
        j = pl.program_id(1)
        h = h_ref[...]
        g = _dot_nt(h, wg_ref[0])
        u = _dot_nt(h, wu_ref[0])
        gs_ref[0] = g.astype(BF16)
        us_ref[0] = u.astype(BF16)
        part = _dot(_silu_mul(g, u), wd_ref[0])
        _acc(acc_ref, part, j == 0)

        @pl.when(j == ns - 1)
        def _():
            normed, vjp = jax.vjp(_rms, acc_ref[...], gp_ref[...])
            diff = x1_ref[...] + normed - t_ref[...]
            dx2 = diff * (1.0 / D)
            dff, dgp = vjp(dx2)
            dx2_ref[...] = dx2
            dff_ref[...] = dff.astype(BF16)
            lpart = jnp.sum(_colsum(diff * diff), axis=1, keepdims=True) * (0.5 / D)
            _acc(dgp_ref, dgp, i == 0)
            _acc(loss_ref, jnp.broadcast_to(lpart, (1, LANES)), i == 0)

    row = lambda w: BS((tm, w), lambda i, j: (i, 0))
    return pl.pallas_call(
        body, name="ffn_fwd", grid=(T // tm, ns),
        in_specs=[row(D), row(D), row(D), BS((1, F, D), lambda i, j: (j, 0, 0)), BS((1, F, D), lambda i, j: (j, 0, 0)),
                  BS((1, F, D), lambda i, j: (j, 0, 0)), BS((1, D), lambda i, j: (0, 0))],
        out_specs=[BS((1, tm, F), lambda i, j: (j, i, 0)), BS((1, tm, F), lambda i, j: (j, i, 0)), row(D), row(D),
                   BS((1, D), lambda i, j: (0, 0)), BS((1, LANES), lambda i, j: (0, 0))],
        out_shape=[jax.ShapeDtypeStruct((ns, T, F), BF16), jax.ShapeDtypeStruct((ns, T, F), BF16),
                   jax.ShapeDtypeStruct((T, D), BF16), jax.ShapeDtypeStruct((T, D), F32),
                   jax.ShapeDtypeStruct((1, D), F32), jax.ShapeDtypeStruct((1, LANES), F32)],
        scratch_shapes=[pltpu.VMEM((tm, D), F32)],
        compiler_params=_cp(("arbitrary", "arbitrary")),
    )(h2, x1, target, wg, wu, wd, g_post)


def _ffn_bwd(dff, h2, gs, us, wg, wu, wd, tm):
    T, D = h2.shape
    ns, F, _ = wg.shape

    def body(dff_ref, h_ref, gs_ref, us_ref, wg_ref, wu_ref, wd_ref, dh_ref, dwg_ref, dwu_ref, dwd_ref):
        first = pl.program_id(1) == 0
        dff = dff_ref[...]
        h = h_ref[...]
        dact = _dot_nt(dff, wd_ref[0])
        a, vjp = jax.vjp(_silu_mul, gs_ref[0].astype(F32), us_ref[0].astype(F32))
        dg, du = vjp(dact)
        dg = dg.astype(BF16)
        du = du.astype(BF16)
        dh_ref[0] = (_dot(dg, wg_ref[0]) + _dot(du, wu_ref[0])).astype(BF16)
        _acc(dwd_ref, _dot_tn(a, dff)[None], first)
        _acc(dwg_ref, _dot_tn(dg, h)[None], first)
        _acc(dwu_ref, _dot_tn(du, h)[None], first)

    row = BS((tm, D), lambda j, i: (i, 0))
    sh = BS((1, tm, F), lambda j, i: (j, i, 0))
    wsh = BS((1, F, D), lambda j, i: (j, 0, 0))
    return pl.pallas_call(
        body, name="ffn_bwd", grid=(ns, T // tm),
        in_specs=[row, row, sh, sh, wsh, wsh, wsh],
        out_specs=[BS((1, tm, D), lambda j, i: (j, i, 0)), wsh, wsh, wsh],
        out_shape=[jax.ShapeDtypeStruct((ns, T, D), BF16)] + [jax.ShapeDtypeStruct((ns, F, D), F32)] * 3,
        compiler_params=_cp(("arbitrary", "arbitrary")),
    )(dff, h2, gs, us, wg, wu, wd)


def _mm_tn(a, b, name, tk, col_shards=1):
    nbatch = col_shards
    T, M = a.shape
    N = b.shape[1] // col_shards
    tk = min(tk, T)

    def body(a_ref, b_ref, o_ref):
        _acc(o_ref, _dot_tn(a_ref[...], b_ref[...])[None], pl.program_id(1) == 0)

    a_spec = BS((tk, M), lambda s, t: (t, 0))
    b_spec = BS((tk, N), lambda s, t: (t, s))
    return pl.pallas_call(
        body, name=name, grid=(nbatch, T // tk),
        in_specs=[a_spec, b_spec],
        out_specs=BS((1, M, N), lambda s, t: (s, 0, 0)),
        out_shape=jax.ShapeDtypeStruct((nbatch, M, N), F32),
        compiler_params=_cp(("arbitrary", "arbitrary")),
    )(a, b)


def _outproj_bwd(dh2, x1, dx2, o, ya, yb, ym, ga, gb, gm, g_post, g_pre2, w_out, tm):
    T, D = x1.shape
    ns = dh2.shape[0]

    def body(dh_ref, x1_ref, dx2_ref, o_ref, ya_ref, yb_ref, ym_ref, ga_ref, gb_ref, gm_ref, gp_ref, g2_ref, w_ref,
             dx1_ref, do_ref, dya_ref, dyb_ref, dym_ref, dga_ref, dgb_ref, dgm_ref, dgp_ref, dg2_ref):
        first = pl.program_id(0) == 0
        dh = dh_ref[0].astype(F32)
        for j in range(1, ns):
            dh = dh + dh_ref[j].astype(F32)
        _, vjp0 = jax.vjp(_rms, x1_ref[...], g2_ref[...])
        dxa, dg2 = vjp0(dh)
        dx1 = dx2_ref[...] + dxa
        dx1_ref[...] = dx1
        _acc(dg2_ref, dg2, first)
        _, vjp = jax.vjp(_rms, o_ref[...], gp_ref[...])
        do, dgp = vjp(dx1)
        do = do.astype(BF16)
        do_ref[...] = do
        dy = _dot_nt(do, w_ref[...])
        _, vjp2 = jax.vjp(_mix_norms, ya_ref[...], yb_ref[...], ym_ref[...], ga_ref[...], gb_ref[...], gm_ref[...])
        dya, dyb, dym, dga, dgb, dgm = vjp2((dy[:, 0:A_W], dy[:, A_W:A_W + B_W], dy[:, A_W + B_W:]))
        dya_ref[...] = dya
        dyb_ref[...] = dyb
        dym_ref[...] = dym
        _acc(dga_ref, dga, first)
        _acc(dgb_ref, dgb, first)
        _acc(dgm_ref, dgm, first)
        _acc(dgp_ref, dgp, first)

    row = lambda w: BS((tm, w), lambda i: (i, 0))
    vec = lambda w: BS((1, w), lambda i: (0, 0))
    sds = jax.ShapeDtypeStruct
    return pl.pallas_call(
        body, name="outproj_bwd", grid=(T // tm,),
        in_specs=[BS((ns, tm, D), lambda i: (0, i, 0)), row(D), row(D), row(D), row(A_W), row(B_W), row(M_W),
                  vec(A_W), vec(B_W), vec(M_W), vec(D), vec(D), BS((A_W + B_W + M_W, D), lambda i: (0, 0))],
        out_specs=[row(D), row(D), row(A_W), row(B_W), row(M_W), vec(A_W), vec(B_W), vec(M_W), vec(D), vec(D)],
        out_shape=[sds((T, D), F32), sds((T, D), BF16), sds((T, A_W), F32), sds((T, B_W), F32), sds((T, M_W), F32),
                   sds((1, A_W), F32), sds((1, B_W), F32), sds((1, M_W), F32), sds((1, D), F32), sds((1, D), F32)],
        compiler_params=_cp(("arbitrary",)),
    )(dh2, x1, dx2, o, ya, yb, ym, ga, gb, gm, g_post, g_pre2, w_out)


def _sgu_bwd(proj, dya, g_sgu, ws_tril, bs_full, tm):
    T = proj.shape[0]
    nch = tm // CHUNK

    def body(zu_ref, zv_ref, dy_ref, g_ref, ws_ref, b_ref, dzu_ref, dzv_ref, dws_ref, dbs_ref, dg_ref,
             du_ref, dvn_ref, dbf_ref):
        step = pl.program_id(0)
        first = step == 0
        lane = _iota((CHUNK, LANES), 1)
        tril = _iota((CHUNK, CHUNK), 0) >= _iota((CHUNK, CHUNK), 1)
        (u, vn), vjp = jax.vjp(_sgu_pre, zu_ref[...].astype(F32), zv_ref[...].astype(F32), g_ref[...])
        vnb = vn.astype(BF16)
        dy = dy_ref[...]

        @pl.when(first)
        def _():
            dws_ref[...] = jnp.zeros_like(dws_ref)
            dbf_ref[...] = jnp.zeros_like(dbf_ref)

        for c in range(nch):
            rs = slice(c * CHUNK, (c + 1) * CHUNK)
            for j in range(3):
                cs = slice(j * LANES, (j + 1) * LANES)
                vp = vnb[rs, cs]
                z = jnp.where(lane < HEAD, _dot(ws_ref[2 * j], vp), _dot(ws_ref[2 * j + 1], vp)) + b_ref[:, cs]
                du_ref[rs, cs] = dy[rs, cs] * z
                dz = dy[rs, cs] * u[rs, cs]
                dbf_ref[:, cs] += dz
                dzb = dz.astype(BF16)
                dz0 = jnp.where(lane < HEAD, dzb, jnp.zeros_like(dzb))
                dz1 = jnp.where(lane >= HEAD, dzb, jnp.zeros_like(dzb))
                dvn_ref[rs, cs] = jnp.where(lane < HEAD, _dot_tn(ws_ref[2 * j], dzb), _dot_tn(ws_ref[2 * j + 1], dzb))
                dws_ref[2 * j] += jnp.where(tril, _dot_nt(dz0, vp), 0.0)
                dws_ref[2 * j + 1] += jnp.where(tril, _dot_nt(dz1, vp), 0.0)
        dzu, dzv, dg = vjp((du_ref[...], dvn_ref[...]))
        dzu_ref[...] = dzu.astype(BF16)
        dzv_ref[...] = dzv.astype(BF16)
        _acc(dg_ref, dg, first)

        @pl.when(step == pl.num_programs(0) - 1)
        def _():
            out = jnp.zeros((CHUNK, LANES), F32)
            for j in range(3):
                slab = dbf_ref[:, j * LANES:(j + 1) * LANES]
                lo = jnp.sum(jnp.where(lane < HEAD, slab, 0.0), axis=1, keepdims=True)
                hi = jnp.sum(jnp.where(lane >= HEAD, slab, 0.0), axis=1, keepdims=True)
                out = out + jnp.where(lane == 2 * j, lo, 0.0) + jnp.where(lane == 2 * j + 1, hi, 0.0)
            dbs_ref[...] = out

    return pl.pallas_call(
        body, name="sgu_bwd", grid=(T // tm,),
        in_specs=[BS((tm, A_W), lambda i: (i, 0)), BS((tm, A_W), lambda i: (i, 1)), BS((tm, A_W), lambda i: (i, 0)),
                  BS((1, A_W), lambda i: (0, 0)), BS((6, CHUNK, CHUNK), lambda i: (0, 0, 0)),
                  BS((CHUNK, A_W), lambda i: (0, 0))],
        out_specs=[BS((tm, A_W), lambda i: (i, 0)), BS((tm, A_W), lambda i: (i, 0)),
                   BS((6, CHUNK, CHUNK), lambda i: (0, 0, 0)), BS((CHUNK, LANES), lambda i: (0, 0)),
                   BS((1, A_W), lambda i: (0, 0))],
        out_shape=[jax.ShapeDtypeStruct((T, A_W), BF16), jax.ShapeDtypeStruct((T, A_W), BF16),
                   jax.ShapeDtypeStruct((6, CHUNK, CHUNK), F32), jax.ShapeDtypeStruct((CHUNK, LANES), F32),
                   jax.ShapeDtypeStruct((1, A_W), F32)],
        scratch_shapes=[pltpu.VMEM((tm, A_W), F32), pltpu.VMEM((tm, A_W), F32), pltpu.VMEM((CHUNK, A_W), F32)],
        compiler_params=_cp(("arbitrary",)),
    )(proj, proj, dya, g_sgu, ws_tril, bs_full)


def _memattn_bwd(proj, kv, dym, Bl, S, tq):
    T = Bl * S
    nq = S // tq
    Mt = kv.shape[1]
    qc = 1920 // LANES

    def body(q_ref, km_ref, vm_ref, do_ref, dq_ref, dkm_ref, dvm_ref):
        first = pl.program_id(2) == 0
        lane = _iota((tq, LANES), 1)
        q = q_ref[...]
        do = do_ref[...]
        dq_out = jnp.zeros((tq, LANES), F32)
        dkm = jnp.zeros((Mt, LANES), F32)
        dvm = jnp.zeros((Mt, LANES), F32)
        for hh in range(2):
            hmask = (lane < HEAD) if hh == 0 else (lane >= HEAD)
            qs = jnp.where(hmask, q, jnp.zeros_like(q)) * 0.125
            dom = jnp.where(hmask, do, 0.0).astype(BF16)
            s = _dot_nt(qs, km_ref[0])
            pe = jnp.exp(s - jnp.max(s, axis=1, keepdims=True))
            pn = pe / jnp.sum(pe, axis=1, keepdims=True)
            dp = _dot_nt(dom, vm_ref[0])
            ds = (pn * (dp - jnp.sum(pn * dp, axis=1, keepdims=True))).astype(BF16)
            dq_out = jnp.where(hmask, _dot(ds, km_ref[0]) * 0.125, dq_out)
            dkm = dkm + _dot_tn(ds, qs)
            dvm = dvm + _dot_tn(pn, dom)
        dq_ref[...] = dq_out.astype(BF16)
        _acc(dkm_ref, dkm[None], first)
        _acc(dvm_ref, dvm[None], first)

    return pl.pallas_call(
        body, name="memattn_bwd", grid=(Bl, 2, nq),
        in_specs=[BS((tq, LANES), lambda b, p, i: (b * nq + i, qc + p)),
                  BS((1, Mt, LANES), lambda b, p, i: (b, 0, p)),
                  BS((1, Mt, LANES), lambda b, p, i: (b, 0, 2 + p)),
                  BS((tq, LANES), lambda b, p, i: (b * nq + i, p))],
        out_specs=[BS((tq, LANES), lambda b, p, i: (b * nq + i, p)),
                   BS((1, Mt, LANES), lambda b, p, i: (b, 0, p)),
                   BS((1, Mt, LANES), lambda b, p, i: (b, 0, p))],
        out_shape=[jax.ShapeDtypeStruct((T, M_W), BF16), jax.ShapeDtypeStruct((Bl, Mt, M_W), F32),
                   jax.ShapeDtypeStruct((Bl, Mt, M_W), F32)],
        compiler_params=_cp(("arbitrary", "arbitrary", "arbitrary")),
    )(proj, kv, kv, dym)


def _memkv_bwd(dkm, dvm, memn, mem, g_mem, w_kv):
    Bl, Mt, D = mem.shape

    def body(dk_ref, dv_ref, mn_ref, m_ref, g_ref, w_ref, dw_ref, dg_ref):
        first = pl.program_id(0) == 0
        dk = dk_ref[0].astype(BF16)
        dv = dv_ref[0].astype(BF16)
        mn = mn_ref[0]
        dmn = _dot_nt(dk, w_ref[:, 0:M_W]) + _dot_nt(dv, w_ref[:, M_W:])
        _, vjp = jax.vjp(_rms, m_ref[0], g_ref[...])
        _, dg = vjp(dmn)
        _acc(dg_ref, dg, first)

        @pl.when(first)
        def _():
            dw_ref[...] = jnp.zeros_like(dw_ref)

        dw_ref[:, 0:M_W] += _dot_tn(mn, dk)
        dw_ref[:, M_W:] += _dot_tn(mn, dv)

    return pl.pallas_call(
        body, name="memkv_bwd", grid=(Bl,),
        in_specs=[BS((1, Mt, M_W), lambda b: (b, 0, 0)), BS((1, Mt, M_W), lambda b: (b, 0, 0)),
                  BS((1, Mt, D), lambda b: (b, 0, 0)), BS((1, Mt, D), lambda b: (b, 0, 0)),
                  BS((1, D), lambda b: (0, 0)), BS((D, 2 * M_W), lambda b: (0, 0))],
        out_specs=[BS((D, 2 * M_W), lambda b: (0, 0)), BS((1, D), lambda b: (0, 0))],
        out_shape=[jax.ShapeDtypeStruct((D, 2 * M_W), F32), jax.ShapeDtypeStruct((1, D), F32)],
        compiler_params=_cp(("arbitrary",)),
    )(dkm, dvm, memn, mem, g_mem, w_kv)


def _fox_bwd(proj, dyb, lse, bq, bk, Bl, S):
    T = Bl * S
    nq = S // Q_BLK
    nb = S // LANES
    qc, kc, vc = 768 // LANES, 1152 // LANES, 1536 // LANES

    def body(q_ref, k_ref, v_ref, do_ref, lse_ref, bq_ref, bk_ref,
             dq_ref, dk_ref, dv_ref, dcr_ref, ka_ref, dka_ref, dva_ref):
        p = pl.program_id(1)
        lane_s = _iota((S, LANES), 1)
        lane = _iota((Q_BLK, LANES), 1)
        sub = _iota((8, LANES), 0)
        tri = _iota((Q_BLK, Q_BLK), 1) <= _iota((Q_BLK, Q_BLK), 0)
        k = k_ref[...]
        for hh in range(2):
            data = (lane_s < HEAD) if hh == 0 else (lane_s >= HEAD)
            ka_ref[hh] = jnp.where(data, k, bk_ref[0, hh])
        dka_ref[...] = jnp.zeros_like(dka_ref)
        dva_ref[...] = jnp.zeros_like(dva_ref)

        @pl.when(p == 0)
        def _():
            dcr_ref[...] = jnp.zeros_like(dcr_ref)

        def add_colsums(ds, first_blk, h):
            cs = _colsum(ds)
            for jb in range(ds.shape[1] // LANES):
                dcr_ref[0, first_blk + jb] += jnp.where(sub == h, cs[:, jb * LANES:(jb + 1) * LANES], 0.0)

        for i in range(nq):
            r0 = i * Q_BLK
            r1 = r0 + Q_BLK
            q = q_ref[r0:r1, :]
            do = do_ref[r0:r1, :]
            lse_b = lse_ref[0, r0:r1, :]
            dq_out = jnp.zeros((Q_BLK, LANES), F32)
            for hh in range(2):
                hmask = (lane < HEAD) if hh == 0 else (lane >= HEAD)
                h = 2 * p + hh
                qs = jnp.where(hmask, q * 0.125, jnp.zeros_like(q))
                qa = jnp.where(hmask, q * 0.125, bq_ref[0, hh, r0:r1, :])
                dob = jnp.where(hmask, do, 0.0).astype(BF16)
                lse_h = jnp.sum(jnp.where(lane == hh * HEAD, lse_b, 0.0), axis=1, keepdims=True)
                pd = jnp.where(tri, jnp.exp(_dot_nt(qa, ka_ref[hh, r0:r1, :]) - lse_h), 0.0)
                dpd = _dot_nt(dob, v_ref[r0:r1, :])
                delta = jnp.sum(pd * dpd, axis=1, keepdims=True)
                psum = jnp.sum(pd, axis=1, keepdims=True)
                if i:
                    pf = jnp.exp(_dot_nt(qa, ka_ref[hh, 0:r0, :]) - lse_h)
                    dpf = _dot_nt(dob, v_ref[0:r0, :])
                    delta = delta + jnp.sum(pf * dpf, axis=1, keepdims=True)
                    psum = psum + jnp.sum(pf, axis=1, keepdims=True)
                delta = delta / psum
                dsd = pd * (dpd - delta)
                add_colsums(dsd, r0 // LANES, h)
                dsd = dsd.astype(BF16)
                dq_h = _dot(dsd, k_ref[r0:r1, :])
                dka_ref[r0:r1, :] += _dot_tn(dsd, qs)
                dva_ref[r0:r1, :] += _dot_tn(pd, dob)
                if i:
                    dsf = pf * (dpf - delta)
                    add_colsums(dsf, 0, h)
                    dsf = dsf.astype(BF16)
                    dq_h = dq_h + _dot(dsf, k_ref[0:r0, :])
                    dka_ref[0:r0, :] += _dot_tn(dsf, qs)
                    dva_ref[0:r0, :] += _dot_tn(pf, dob)
                dq_out = jnp.where(hmask, dq_h * 0.125, dq_out)
            dq_ref[r0:r1, :] = dq_out.astype(BF16)
        dk_ref[...] = dka_ref[...].astype(BF16)
        dv_ref[...] = dva_ref[...].astype(BF16)

    seq = lambda c0: BS((S, LANES), lambda b, p: (b, c0 + p))
    pair = BS((1, 2, S, LANES), lambda b, p: (b, p, 0, 0))
    rowblk = BS((1, nb, 8, LANES), lambda b, p: (b, 0, 0, 0))
    return pl.pallas_call(
        body, name="fox_bwd", grid=(Bl, 3),
        in_specs=[seq(qc), seq(kc), seq(vc), seq(0), BS((1, S, LANES), lambda b, p: (p, b, 0)), pair, pair],
        out_specs=[seq(0), seq(0), seq(0), rowblk],
        out_shape=[jax.ShapeDtypeStruct((T, B_W), BF16)] * 3 + [jax.ShapeDtypeStruct((Bl, nb, 8, LANES), F32)],
        scratch_shapes=[pltpu.VMEM((2, S, LANES), BF16), pltpu.VMEM((S, LANES), F32), pltpu.VMEM((S, LANES), F32)],
        compiler_params=_cp(("arbitrary", "arbitrary")),
    )(proj, proj, proj, dyb, lse, bq, bk)


def _gate_bwd(dc_row, fl_row):
    Bl, nb, _, _ = dc_row.shape

    def body(dc_ref, fl_ref, o_ref):
        lane = _iota((8, LANES), 1)

        def blk(jj, carry):
            j = nb - 1 - jj
            r = -dc_ref[0, j]
            for k in (1, 2, 4, 8, 16, 32, 64):
                r = r + jnp.where(lane < LANES - k, pltpu.roll(r, LANES - k, 1), 0.0)
            r = r + carry
            dfl = r * _sigmoid(-fl_ref[0, j])
            o_ref[0, pl.ds(pl.multiple_of(j * LANES, LANES), LANES), :] = jnp.concatenate(
                [dfl, jnp.zeros((LANES - 8, LANES), F32)], axis=0).T
            return jnp.sum(jnp.where(lane == 0, r, 0.0), axis=1, keepdims=True)

        lax.fori_loop(0, nb, blk, jnp.zeros((8, 1), F32))

    rowblk = BS((1, nb, 8, LANES), lambda b: (b, 0, 0, 0))
    return pl.pallas_call(
        body, name="gate_bwd", grid=(Bl,),
        in_specs=[rowblk, rowblk],
        out_specs=BS((1, nb * LANES, LANES), lambda b: (b, 0, 0)),
        out_shape=jax.ShapeDtypeStruct((Bl, nb * LANES, LANES), F32),
        compiler_params=_cp(("arbitrary",)),
    )(dc_row, fl_row)


def _inproj_bwd(dzu, dzv, dq, dk, dv, dqm, dfl, x2d, dx1, g_pre, w_in_p, tm):
    T, D = x2d.shape
    ns, _, dsh = w_in_p.shape

    def body(dzu_ref, dzv_ref, dq_ref, dk_ref, dv_ref, dqm_ref, dfl_ref, x_ref, dx1_ref, g_ref, w_ref,
             dp_ref, gx_ref, dg_ref, dbf_ref):
        first = pl.program_id(0) == 0
        dfl = dfl_ref[...]
        dp_ref[:, 0:384] = dzu_ref[...]
        dp_ref[:, 384:768] = dzv_ref[...]
        dp_ref[:, 768:1152] = dq_ref[...]
        dp_ref[:, 1152:1536] = dk_ref[...]
        dp_ref[:, 1536:1920] = dv_ref[...]
        dp_ref[:, 1920:2048] = dfl.astype(BF16)
        dp_ref[:, 2048:2304] = dqm_ref[...]
        dh = jnp.concatenate([_dot(dp_ref[...], w_ref[s]) for s in range(ns)], axis=1)
        _, vjp = jax.vjp(_rms, x_ref[...], g_ref[...])
        dxa, dg = vjp(dh)
        gx_ref[...] = dx1_ref[...] + dxa
        _acc(dg_ref, dg, first)
        _acc(dbf_ref, _colsum(dfl), first)

    row = lambda w: BS((tm, w), lambda i: (i, 0))
    return pl.pallas_call(
        body, name="inproj_bwd", grid=(T // tm,),
        in_specs=[row(A_W), row(A_W), row(B_W), row(B_W), row(B_W), row(M_W), row(LANES), row(D), row(D),
                  BS((1, D), lambda i: (0, 0)), BS((ns, P_COLS, dsh), lambda i: (0, 0, 0))],
        out_specs=[row(P_COLS), row(D), BS((1, D), lambda i: (0, 0)), BS((1, LANES), lambda i: (0, 0))],
        out_shape=[jax.ShapeDtypeStruct((T, P_COLS), BF16), jax.ShapeDtypeStruct((T, D), F32),
                   jax.ShapeDtypeStruct((1, D), F32), jax.ShapeDtypeStruct((1, LANES), F32)],
        compiler_params=_cp(("arbitrary",)),
    )(dzu, dzv, dq, dk, dv, dqm, dfl, x2d, dx1, g_pre, w_in_p)


def _local_step(x, mem, target, W, P):
    Bl, S, D = x.shape
    T = Bl * S
    tm = min(512, T)
    x2d = x.reshape(T, D)
    t2d = target.reshape(T, D)
    vec = lambda a: a.reshape(1, -1)
    bf_row = jnp.pad(P["b_f"].reshape(1, -1), ((0, 0), (0, LANES - N_FOX_HEADS)))
    tril = jnp.tril(jnp.ones((CHUNK, CHUNK), bool))
    ws_tril = jnp.where(tril[None], P["w_s"][0], 0.0).astype(BF16)
    bs_full = jnp.repeat(P["b_s"][0].T, HEAD, axis=1)
    g_pre, g_sgu = vec(P["g_pre_mix"]), vec(P["g_sgu"])
    ga, gb, gm = vec(P["g_out_a"]), vec(P["g_out_b"]), vec(P["g_out_m"])
    g_mem, g_post, g_pre2, g_post2 = vec(P["g_mem"]), vec(P["g_post_mix"]), vec(P["g_pre_ffn"]), vec(P["g_post_ffn"])

    h, proj, flog = _inproj_fwd(x2d, g_pre, W["w_in"], tm)
    bq, bk, fl_row = _gate_fwd(flog.reshape(Bl, S, LANES), bf_row)
    ya = _sgu_fwd(proj, g_sgu, ws_tril, bs_full, tm)
    yb, lse = _fox_fwd(proj, bq, bk, Bl, S)
    memn, kv = _memkv_fwd(mem, g_mem, W["w_mem_kv"])
    ym = _memattn_fwd(proj, kv, Bl, S, min(512, S))
    y, o, x1, h2 = _outproj_fwd(ya, yb, ym, x2d, ga, gb, gm, g_post, g_pre2, W["w_out"], tm)
    gs, us, dff, dx2, dg_post2, loss = _ffn_fwd(h2, x1, t2d, W["w_gate"], W["w_up"], W["w_down"], g_post2, tm)

    dh2, d_w_gate, d_w_up, d_w_down = _ffn_bwd(dff, h2, gs, us, W["w_gate"], W["w_up"], W["w_down"], tm)
    dx1, do, dya, dyb, dym, dga, dgb, dgm, dg_post, dg_pre2 = _outproj_bwd(
        dh2, x1, dx2, o, ya, yb, ym, ga, gb, gm, g_post, g_pre2, W["w_out"], tm)
    d_w_out = _mm_tn(y, do, "dw_out", 1024)[0]
    dzu, dzv, dws, dbs_cols, dg_sgu = _sgu_bwd(proj, dya, g_sgu, ws_tril, bs_full, tm)
    dqm, dkm, dvm = _memattn_bwd(proj, kv, dym, Bl, S, min(512, S))
    d_w_kv, dg_mem = _memkv_bwd(dkm, dvm, memn, mem, g_mem, W["w_mem_kv"])
    dq, dk, dv, dc_row = _fox_bwd(proj, dyb, lse, bq, bk, Bl, S)
    dfl = _gate_bwd(dc_row, fl_row).reshape(T, LANES)
    dproj, grad_x, dg_pre, dbf = _inproj_bwd(dzu, dzv, dq, dk, dv, dqm, dfl, x2d, dx1, g_pre, W["w_in"], tm)
    d_w_in = _mm_tn(dproj, h, "dw_in", 1024, col_shards=W["w_in"].shape[0])

    big = {"w_in": d_w_in, "w_mem_kv": d_w_kv, "w_out": d_w_out, "w_gate": d_w_gate, "w_up": d_w_up,
           "w_down": d_w_down}
    small = {"g_pre_mix": dg_pre, "b_f": dbf[:, :N_FOX_HEADS], "g_sgu": dg_sgu, "w_s": dws, "b_s": dbs_cols[:, :N_FOX_HEADS].T,
             "g_out_a": dga, "g_out_b": dgb, "g_out_m": dgm, "g_mem": dg_mem, "g_post_mix": dg_post,
             "g_pre_ffn": dg_pre2, "g_post_ffn": dg_post2, "loss": loss[:, :1]}
    return grad_x.reshape(Bl, S, D), big, small


def _place():
    return lax.axis_index("x"), lax.axis_index("y"), lax.axis_index("c")


def _exchange(srcs, own_full, name):
    n = len(srcs)
    nd2d = 3 if own_full else 4

    def body(*refs):
        src, dst, own = refs[:n], refs[n:2 * n], refs[2 * n:3 * n]
        lsem, osem, isend, irecv, dsend, drecv = refs[3 * n:]
        x, y, c = _place()
        oc = 1 - c
        s_me = 2 * x + y
        sib = (x, y, oc)
        chips = [(1 - x, y), (x, 1 - y), (1 - x, 1 - y)]

        def remote(a, b, ssem, rsem, dev):
            return pltpu.make_async_remote_copy(src_ref=a, dst_ref=b, send_sem=ssem, recv_sem=rsem,
                                                device_id=dev, device_id_type=MESH)

        sends, loads, local = [], [], []
        for w in range(n):
            ld = pltpu.make_async_copy(src[w] if own_full else src[w].at[s_me], own[w], lsem.at[w])
            ld.start()
            loads.append(ld)
        for w in range(n):
            for j, (cx, cy) in enumerate(chips):
                half = src[w].at[c] if own_full else src[w].at[2 * cx + cy]
                cp = remote(half, dst[w].at[s_me, c], isend.at[w, j], irecv.at[w, j], (cx, cy, c))
                cp.start()
                sends.append(cp)
            if not own_full:
                cp = remote(src[w].at[s_me], dst[w].at[s_me, c], dsend.at[w, 3], drecv.at[w, 3], sib)
                cp.start()
                sends.append(cp)
        for w in range(n):
            loads[w].wait()
            st = pltpu.make_async_copy(own[w], dst[w].at[s_me] if own_full else dst[w].at[s_me, c], osem.at[w])
            st.start()
            local.append(st)
        for w in range(n):
            for j, (cx, cy) in enumerate(chips):
                landed = dst[w].at[2 * cx + cy, c]
                remote(landed, landed, isend.at[w, j], irecv.at[w, j], (cx, cy, c)).wait_recv()
                cp = remote(landed, landed, dsend.at[w, j], drecv.at[w, j], sib)
                cp.start()
                sends.append(cp)
        for w in range(n):
            for j, (cx, cy) in enumerate(chips):
                landed = dst[w].at[2 * cx + cy, oc]
                remote(landed, landed, dsend.at[w, j], drecv.at[w, j], sib).wait_recv()
            if not own_full:
                landed = dst[w].at[s_me, oc]
                remote(landed, landed, dsend.at[w, 3], drecv.at[w, 3], sib).wait_recv()
        for cp in sends:
            cp.wait_send()
        for loc in local:
            loc.wait()

    out_shape = [jax.ShapeDtypeStruct((4, 2) + s.shape[1:], s.dtype) for s in srcs]
    own_shapes = [pltpu.VMEM(s.shape if own_full else s.shape[1:], s.dtype) for s in srcs]
    return pl.pallas_call(
        body, name=name, in_specs=[ANY] * n, out_specs=[ANY] * n, out_shape=out_shape,
        scratch_shapes=own_shapes + [pltpu.SemaphoreType.DMA((n,)), pltpu.SemaphoreType.DMA((n,)),
                                     pltpu.SemaphoreType.DMA((n, 3)), pltpu.SemaphoreType.DMA((n, 3)),
                                     pltpu.SemaphoreType.DMA((n, nd2d)), pltpu.SemaphoreType.DMA((n, nd2d))],
        compiler_params=pltpu.CompilerParams(vmem_limit_bytes=VMEM_LIMIT),
    )(*srcs)


def _sibling_swap(grads):
    n = len(grads)

    def body(*refs):
        g, theirs = refs[:n], refs[n:2 * n]
        ssem, rsem = refs[2 * n:]
        x, y, c = _place()
        cps = []
        for w in range(n):
            cp = pltpu.make_async_remote_copy(src_ref=g[w].at[:, 1 - c], dst_ref=theirs[w], send_sem=ssem.at[w],
                                              recv_sem=rsem.at[w], device_id=(x, y, 1 - c), device_id_type=MESH)
            cp.start()
            cps.append(cp)
        for cp in cps:
            cp.wait()

    half = [jax.ShapeDtypeStruct((4,) + g.shape[2:], g.dtype) for g in grads]
    return pl.pallas_call(
        body, name="sibling_swap", in_specs=[ANY] * n, out_specs=[ANY] * n, out_shape=half,
        scratch_shapes=[pltpu.SemaphoreType.DMA((n,)), pltpu.SemaphoreType.DMA((n,))],
    )(*grads)


def _add_pair(core, g, theirs, name):
    _, _, hr, C = g.shape

    def body(core_ref, g_ref, t_ref, o_ref):
        o_ref[0] = (g_ref[0, 0] + t_ref[0]).astype(BF16)

    blk = BS((1, hr, C), lambda s, core_ref: (s, 0, 0))
    return pl.pallas_call(
        body, name=name,
        grid_spec=pltpu.PrefetchScalarGridSpec(
            num_scalar_prefetch=1, grid=(4,),
            in_specs=[BS((1, 1, hr, C), lambda s, core_ref: (s, core_ref[0], 0, 0)), blk], out_specs=blk),
        out_shape=jax.ShapeDtypeStruct(theirs.shape, BF16), compiler_params=_cp(("arbitrary",)))(core, g, theirs)


def _sum_chips(r, name):
    _, _, hr, C = r.shape

    def body(r_ref, o_ref):
        o_ref[...] = ((r_ref[0, 0].astype(F32) + r_ref[1, 0].astype(F32)) + r_ref[2, 0].astype(F32)) + r_ref[3, 0].astype(F32)

    return pl.pallas_call(body, name=name, grid=(2,), in_specs=[BS((4, 1, hr, C), lambda h: (0, h, 0, 0))],
                          out_specs=BS((hr, C), lambda h: (h, 0)), out_shape=jax.ShapeDtypeStruct((2 * hr, C), F32),
                          compiler_params=_cp(("arbitrary",)))(r)


def _small_allreduce(part):
    R = part.shape[0]
    rs = R // 8
    masks = [(mx, my, mc) for mx in (0, 1) for my in (0, 1) for mc in (0, 1)][1:]

    def body(p_ref, o_ref, buf_ref, s1, r1, s2, r2):
        x, y, c = _place()
        d = 4 * x + 2 * y + c
        mine = pl.ds(pl.multiple_of(d * rs, 8), rs)
        peers = [((x + mx) % 2, (y + my) % 2, (c + mc) % 2) for mx, my, mc in masks]
        first, second = [], []
        for k, (px, py, pc) in enumerate(peers):
            theirs = pl.ds(pl.multiple_of((4 * px + 2 * py + pc) * rs, 8), rs)
            cp = pltpu.make_async_remote_copy(src_ref=p_ref.at[theirs, :], dst_ref=buf_ref.at[d], send_sem=s1.at[k],
                                              recv_sem=r1.at[k], device_id=(px, py, pc), device_id_type=MESH)
            cp.start()
            first.append(cp)
        buf_ref[d] = p_ref[mine, :]
        for k, (px, py, pc) in enumerate(peers):
            slot = buf_ref.at[4 * px + 2 * py + pc]
            pltpu.make_async_remote_copy(src_ref=slot, dst_ref=slot, send_sem=s1.at[k], recv_sem=r1.at[k],
                                         device_id=(px, py, pc), device_id_type=MESH).wait_recv()
        total = buf_ref[0]
        for k in range(1, 8):
            total = total + buf_ref[k]
        o_ref[mine, :] = total
        for k, (px, py, pc) in enumerate(peers):
            cp = pltpu.make_async_remote_copy(src_ref=o_ref.at[mine, :], dst_ref=o_ref.at[mine, :], send_sem=s2.at[k],
                                              recv_sem=r2.at[k], device_id=(px, py, pc), device_id_type=MESH)
            cp.start()
            second.append(cp)
        for k, (px, py, pc) in enumerate(peers):
            rows = o_ref.at[pl.ds(pl.multiple_of((4 * px + 2 * py + pc) * rs, 8), rs), :]
            pltpu.make_async_remote_copy(src_ref=rows, dst_ref=rows, send_sem=s2.at[k], recv_sem=r2.at[k],
                                         device_id=(px, py, pc), device_id_type=MESH).wait_recv()
        for cp in first + second:
            cp.wait_send()

    vm = pl.BlockSpec(memory_space=pltpu.VMEM)
    return pl.pallas_call(
        body, name="small_allreduce", in_specs=[vm], out_specs=vm, out_shape=jax.ShapeDtypeStruct(part.shape, F32),
        scratch_shapes=[pltpu.VMEM((8, rs, LANES), F32)] + [pltpu.SemaphoreType.DMA((7,))] * 4,
    )(part)


def _adamw(w, g, m, v, name):
    R, C = w.shape
    tr = R if R * C * 4 <= (1 << 21) else R // 2
    if tr % 8:
        tr = R
    c1 = 1.0 / (1.0 - ADAM_B1 ** ADAM_STEP)
    c2 = 1.0 / (1.0 - ADAM_B2 ** ADAM_STEP)

    def body(w_ref, g_ref, m_ref, v_ref, d_ref, mo_ref, vo_ref):
        g_ = g_ref[...]
        m_ = ADAM_B1 * m_ref[...] + (1.0 - ADAM_B1) * g_
        v_ = ADAM_B2 * v_ref[...] + (1.0 - ADAM_B2) * (g_ * g_)
        mo_ref[...] = m_
        vo_ref[...] = v_
        d_ref[...] = -ADAM_LR * ((m_ * c1) / (jnp.sqrt(v_ * c2) + ADAM_EPS) + ADAM_WD * w_ref[...])

    blk = BS((tr, C), lambda i: (i, 0))
    return pl.pallas_call(body, name=name, grid=(R // tr,), in_specs=[blk] * 4, out_specs=[blk] * 3,
                          out_shape=[jax.ShapeDtypeStruct((R, C), F32)] * 3, compiler_params=_cp(("arbitrary",)))(w, g, m, v)


SMALL = ("g_pre_mix", "b_f", "g_sgu", "w_s", "b_s", "g_out_a", "g_out_b", "g_out_m", "g_mem", "g_post_mix",
         "g_pre_ffn", "g_post_ffn")
BIG = ("w_in", "w_mem_kv", "w_out", "w_gate", "w_up", "w_down")
TRANSPOSED = ("w_in", "w_gate", "w_up")
WEIGHTS = ("g_pre_mix", "w_in", "b_f", "g_sgu", "w_s", "b_s", "g_out_a", "g_out_b", "g_out_m", "g_mem", "w_mem_kv",
           "w_out", "g_post_mix", "g_pre_ffn", "w_gate", "w_up", "w_down", "g_post_ffn")


def _rows_of(n):
    return -(-n // (8 * LANES)) * 8


def _pack(parts):
    tiles = []
    for a in parts:
        flat = a.reshape(-1).astype(F32)
        rows = _rows_of(flat.shape[0])
        tiles.append(jnp.pad(flat, (0, rows * LANES - flat.shape[0])).reshape(rows, LANES))
    total = sum(t.shape[0] for t in tiles)
    pad = -total % 64
    if pad:
        tiles.append(jnp.zeros((pad, LANES), F32))
    return jnp.concatenate(tiles, axis=0)


def _unpack(packed, shapes):
    out, r = [], 0
    for shp in shapes:
        n = 1
        for s in shp:
            n *= s
        rows = _rows_of(n)
        out.append(packed[r:r + rows].reshape(-1)[:n].reshape(shp))
        r += rows
    return out


def kernel(x, mem, g_pre_mix, w_in, b_f, g_sgu, w_s, b_s, g_out_a, g_out_b, g_out_m, g_mem, w_mem_kv, w_out, g_post_mix, g_pre_ffn, w_gate, w_up, w_down, g_post_ffn, loss_target, m_g_pre_mix, m_w_in, m_b_f, m_g_sgu, m_w_s, m_b_s, m_g_out_a, m_g_out_b, m_g_out_m, m_g_mem, m_w_mem_kv, m_w_out, m_g_post_mix, m_g_pre_ffn, m_w_gate, m_w_up, m_w_down, m_g_post_ffn, v_g_pre_mix, v_w_in, v_b_f, v_g_sgu, v_w_s, v_b_s, v_g_out_a, v_g_out_b, v_g_out_m, v_g_mem, v_w_mem_kv, v_w_out, v_g_post_mix, v_g_pre_ffn, v_w_gate, v_w_up, v_w_down, v_g_post_ffn):
    Wt = dict(g_pre_mix=g_pre_mix, w_in=w_in, b_f=b_f, g_sgu=g_sgu, w_s=w_s, b_s=b_s, g_out_a=g_out_a, g_out_b=g_out_b,
              g_out_m=g_out_m, g_mem=g_mem, w_mem_kv=w_mem_kv, w_out=w_out, g_post_mix=g_post_mix, g_pre_ffn=g_pre_ffn,
              w_gate=w_gate, w_up=w_up, w_down=w_down, g_post_ffn=g_post_ffn)
    Mo = dict(g_pre_mix=m_g_pre_mix, w_in=m_w_in, b_f=m_b_f, g_sgu=m_g_sgu, w_s=m_w_s, b_s=m_b_s, g_out_a=m_g_out_a,
              g_out_b=m_g_out_b, g_out_m=m_g_out_m, g_mem=m_g_mem, w_mem_kv=m_w_mem_kv, w_out=m_w_out,
              g_post_mix=m_g_post_mix, g_pre_ffn=m_g_pre_ffn, w_gate=m_w_gate, w_up=m_w_up, w_down=m_w_down,
              g_post_ffn=m_g_post_ffn)
    Vo = dict(g_pre_mix=v_g_pre_mix, w_in=v_w_in, b_f=v_b_f, g_sgu=v_g_sgu, w_s=v_w_s, b_s=v_b_s, g_out_a=v_g_out_a,
              g_out_b=v_g_out_b, g_out_m=v_g_out_m, g_mem=v_g_mem, w_mem_kv=v_w_mem_kv, w_out=v_w_out,
              g_post_mix=v_g_post_mix, g_pre_ffn=v_g_pre_ffn, w_gate=v_w_gate, w_up=v_w_up, w_down=v_w_down,
              g_post_ffn=v_g_post_ffn)

    gap = P_COLS - IN_COLS

    def to_kernel(n, w):
        if n in TRANSPOSED:
            w = w.T
        if n == "w_in":
            w = jnp.pad(w[:F_END], ((0, P_COLS - F_END), (0, 0))) + jnp.pad(w[F_END:], ((F_END + gap, 0), (0, 0)))
        return w

    def ungroup(g):
        return jnp.pad(g[:F_END], ((0, IN_COLS - F_END), (0, 0))) + jnp.pad(g[F_END + gap:], ((F_END, 0), (0, 0)))

    shards = {n: to_kernel(n, Wt[n][0]) for n in BIG}
    srcs = [shards[n].astype(BF16).reshape(2, shards[n].shape[0] // 2, shards[n].shape[1]) for n in BIG]
    fulls = _exchange(srcs, True, "gather_weights")
    W = {}
    for n, f in zip(BIG, fulls):
        _, _, hr, C = f.shape
        W[n] = f.reshape(8 * hr, C) if n in ("w_mem_kv", "w_out") else f.reshape(4, 2 * hr, C)

    P = {n: Wt[n] for n in SMALL}
    grad_x, big, small = _local_step(x, mem, loss_target, W, P)

    g4 = []
    for n in BIG:
        g = big[n]
        C = g.shape[-1]
        g4.append(g.reshape(4, 2, -1, C))
    theirs = _sibling_swap(g4)
    core = lax.axis_index("c").astype(jnp.int32).reshape(1)
    chip_sums = [_add_pair(core, g, t, "chip_sum_" + n) for n, g, t in zip(BIG, g4, theirs)]
    landed = _exchange(chip_sums, False, "scatter_grads")
    grads, deltas, new_m, new_v = {}, {}, {}, {}
    for n, r in zip(BIG, landed):
        g = _sum_chips(r, "sum_chips_" + n)
        if n == "w_in":
            g = ungroup(g)
        if n in TRANSPOSED:
            d, m1, v1 = _adamw(Wt[n][0].T, g, Mo[n][0].T, Vo[n][0].T, "adamw_" + n)
            g, d, m1, v1 = g.T, d.T, m1.T, v1.T
        else:
            d, m1, v1 = _adamw(Wt[n][0], g, Mo[n][0], Vo[n][0], "adamw_" + n)
        grads[n], deltas[n], new_m[n], new_v[n] = g[None], d[None], m1[None], v1[None]

    total = _small_allreduce(_pack([small[n] for n in SMALL] + [small["loss"]]))
    slot = [jnp.zeros((1, 1), F32)]
    shapes = [Wt[n].shape for n in SMALL] + [(1, 1)]
    d, m1, v1 = _adamw(_pack([Wt[n] for n in SMALL] + slot), total, _pack([Mo[n] for n in SMALL] + slot),
                       _pack([Vo[n] for n in SMALL] + slot), "adamw_small")
    g_s, d_s, m_s, v_s = _unpack(total, shapes), _unpack(d, shapes), _unpack(m1, shapes), _unpack(v1, shapes)
    for k, n in enumerate(SMALL):
        grads[n], deltas[n], new_m[n], new_v[n] = g_s[k], d_s[k], m_s[k], v_s[k]
    loss = g_s[-1][0, 0]

    return (loss, grad_x, *[grads[n] for n in WEIGHTS], *[deltas[n] for n in WEIGHTS],
            *[new_m[n] for n in WEIGHTS], *[new_v[n] for n in WEIGHTS])
```

```python
import functools

import jax
import jax.numpy as jnp
from jax import lax
from jax.experimental import pallas as pl
from jax.experimental.pallas import tpu as pltpu
from jax.experimental.pallas import tpu_sc as plsc

F32 = jnp.float32
BF16 = jnp.bfloat16
EPS = 1e-6
NEG = -1e30
HEAD = 64
A_W, B_W, M_W = 384, 384, 256
N_FOX_HEADS = 6
CHUNK = 128
IN_COLS = 2 * A_W + 3 * B_W + N_FOX_HEADS + M_W
P_MAIN = 2 * A_W + 3 * B_W + M_W
P_COLS = P_MAIN + 128
F_END = 2 * A_W + 3 * B_W + N_FOX_HEADS
LANES = 128
Q_BLK, K_BLK = 256, 128
ADAM_LR, ADAM_B1, ADAM_B2, ADAM_EPS, ADAM_WD, ADAM_STEP = 0.001, 0.9, 0.999, 1e-08, 0.01, 10
VMEM_LIMIT = 56 * 1024 * 1024
MESH = pl.DeviceIdType.MESH
ANY = pl.BlockSpec(memory_space=pl.ANY)
BS = pl.BlockSpec


def _cp(sem=None):
    return pltpu.CompilerParams(dimension_semantics=sem, vmem_limit_bytes=VMEM_LIMIT)


def _iota(shape, dim):
    return lax.broadcasted_iota(jnp.int32, shape, dim)


def _dot(a, b):
    return jnp.dot(a.astype(BF16), b.astype(BF16), preferred_element_type=F32)


def _dot_nt(a, b):
    return lax.dot_general(a.astype(BF16), b.astype(BF16), (((1,), (1,)), ((), ())), preferred_element_type=F32)


def _dot_tn(a, b):
    return lax.dot_general(a.astype(BF16), b.astype(BF16), (((0,), (0,)), ((), ())), preferred_element_type=F32)


def _rms(x, g):
    return x * lax.rsqrt(jnp.mean(x * x, axis=-1, keepdims=True) + EPS) * g


def _gelu(x):
    return 0.5 * x * (1.0 + jnp.tanh(0.7978845608028654 * (x + 0.044715 * (x * x * x))))


def _sigmoid(x):
    return 1.0 / (1.0 + jnp.exp(-x))


def _silu_mul(g, u):
    return g * _sigmoid(g) * u


def _logsig(x):
    return jnp.minimum(x, 0.0) - jnp.log(1.0 + jnp.exp(-jnp.abs(x)))


def _colsum(x):
    return jnp.sum(x, axis=0, keepdims=True)


def _acc(ref, val, first):
    @pl.when(first)
    def _():
        ref[...] = val

    @pl.when(jnp.logical_not(first))
    def _():
        ref[...] += val


def _inproj_fwd(x2d, g_pre, w_in_p, tm):
    T, D = x2d.shape
    nchunk = P_COLS // 384
    ns, _, dsh = w_in_p.shape

    def body(x_ref, g_ref, w_ref, h_ref, proj_ref, fl_ref):
        h = _rms(x_ref[...], g_ref[...]).astype(BF16)
        h_ref[...] = h
        for n in range(nchunk):
            r = _dot_nt(h[:, 0:dsh], w_ref[0, n * 384:(n + 1) * 384, :])
            for s in range(1, ns):
                r = r + _dot_nt(h[:, s * dsh:(s + 1) * dsh], w_ref[s, n * 384:(n + 1) * 384, :])
            if n < nchunk - 1:
                proj_ref[:, n * 384:(n + 1) * 384] = r.astype(BF16)
            else:
                fl_ref[...] = r[:, :LANES]
                proj_ref[:, n * 384:n * 384 + M_W] = r[:, LANES:].astype(BF16)

    return pl.pallas_call(
        body, name="inproj_fwd", grid=(T // tm,),
        in_specs=[BS((tm, D), lambda i: (i, 0)), BS((1, D), lambda i: (0, 0)),
                  BS((ns, P_COLS, dsh), lambda i: (0, 0, 0))],
        out_specs=[BS((tm, D), lambda i: (i, 0)), BS((tm, P_MAIN), lambda i: (i, 0)), BS((tm, LANES), lambda i: (i, 0))],
        out_shape=[jax.ShapeDtypeStruct((T, D), BF16), jax.ShapeDtypeStruct((T, P_MAIN), BF16),
                   jax.ShapeDtypeStruct((T, LANES), F32)],
        compiler_params=_cp(("arbitrary",)),
    )(x2d, g_pre, w_in_p)


def _gate_fwd(flog3, bf_row):
    Bl, S, _ = flog3.shape
    nb = S // LANES

    def body(f_ref, b_ref, bq_ref, bk_ref, fr_ref):
        row = _iota((LANES, LANES), 0)
        lane = _iota((LANES, LANES), 1)
        one = jnp.ones((LANES, LANES), BF16)
        zero = jnp.zeros((LANES, LANES), BF16)

        def blk(j, carry):
            r0 = pl.multiple_of(j * LANES, LANES)
            fl = f_ref[0, pl.ds(r0, LANES), :] + b_ref[...]
            fr_ref[0, j] = fl.T[0:8, :]
            c = _logsig(fl)
            for k in (1, 2, 4, 8, 16, 32, 64):
                c = c + jnp.where(row >= k, pltpu.roll(c, k, 0), 0.0)
            c = c + carry
            for h in range(N_FOX_HEADS):
                col = jnp.sum(jnp.where(lane == h, c, 0.0), axis=1, keepdims=True)
                hi = col.astype(BF16)
                rest = col - hi.astype(F32)
                mid = rest.astype(BF16)
                lo = (rest - mid.astype(F32)).astype(BF16)
                base = _bias_lane(h)
                bq = jnp.where(lane == base, hi, jnp.where(lane == base + 1, mid, jnp.where(lane == base + 2, lo, zero)))
                bq = jnp.where((lane >= base + 3) & (lane < base + 6), one, bq)
                bk = jnp.where(lane == base + 3, -hi, jnp.where(lane == base + 4, -mid, jnp.where(lane == base + 5, -lo, zero)))
                bk = jnp.where((lane >= base) & (lane < base + 3), one, bk)
                bq_ref[0, h, pl.ds(r0, LANES), :] = bq
                bk_ref[0, h, pl.ds(r0, LANES), :] = bk
            return _colsum(jnp.where(row == LANES - 1, c, 0.0))

        lax.fori_loop(0, nb, blk, jnp.zeros((1, LANES), F32))

    slab = BS((1, N_FOX_HEADS, S, LANES), lambda b: (b, 0, 0, 0))
    return pl.pallas_call(
        body, name="gate_fwd", grid=(Bl,),
        in_specs=[BS((1, S, LANES), lambda b: (b, 0, 0)), BS((1, LANES), lambda b: (0, 0))],
        out_specs=[slab, slab, BS((1, nb, 8, LANES), lambda b: (b, 0, 0, 0))],
        out_shape=[jax.ShapeDtypeStruct((Bl, N_FOX_HEADS, S, LANES), BF16),
                   jax.ShapeDtypeStruct((Bl, N_FOX_HEADS, S, LANES), BF16),
                   jax.ShapeDtypeStruct((Bl, nb, 8, LANES), F32)],
        compiler_params=_cp(("arbitrary",)),
    )(flog3, bf_row)


def _bias_lane(h):
    return HEAD if h % 2 == 0 else 0


def _sgu_pre(zu, zv, g_sgu):
    return _gelu(zu), _rms(_gelu(zv), g_sgu)


def _sgu_fwd(proj, g_sgu, ws_tril, bs_full, tm):
    T = proj.shape[0]
    nch = tm // CHUNK

    def body(zu_ref, zv_ref, g_ref, ws_ref, b_ref, ya_ref):
        lane = _iota((CHUNK, LANES), 1)
        u, vn = _sgu_pre(zu_ref[...].astype(F32), zv_ref[...].astype(F32), g_ref[...])
        vn = vn.astype(BF16)
        for c in range(nch):
            rs = slice(c * CHUNK, (c + 1) * CHUNK)
            for j in range(3):
                cs = slice(j * LANES, (j + 1) * LANES)
                vp = vn[rs, cs]
                z = jnp.where(lane < HEAD, _dot(ws_ref[2 * j], vp), _dot(ws_ref[2 * j + 1], vp)) + b_ref[:, cs]
                ya_ref[rs, cs] = u[rs, cs] * z

    return pl.pallas_call(
        body, name="sgu_fwd", grid=(T // tm,),
        in_specs=[BS((tm, A_W), lambda i: (i, 0)), BS((tm, A_W), lambda i: (i, 1)), BS((1, A_W), lambda i: (0, 0)),
                  BS((6, CHUNK, CHUNK), lambda i: (0, 0, 0)), BS((CHUNK, A_W), lambda i: (0, 0))],
        out_specs=BS((tm, A_W), lambda i: (i, 0)),
        out_shape=jax.ShapeDtypeStruct((T, A_W), F32),
        compiler_params=_cp(("arbitrary",)),
    )(proj, proj, g_sgu, ws_tril, bs_full)


def _fox_fwd(proj, bq, bk, Bl, S):
    T = Bl * S
    nq = S // Q_BLK
    qc, kc, vc = 768 // LANES, 1152 // LANES, 1536 // LANES

    def body(q_ref, k_ref, v_ref, bq_ref, bk_ref, o_ref, lse_ref, ka_ref, va_ref):
        lane_s = _iota((S, LANES), 1)
        lane = _iota((Q_BLK, LANES), 1)
        tri = _iota((Q_BLK, Q_BLK), 1) <= _iota((Q_BLK, Q_BLK), 0)
        k = k_ref[...]
        v = v_ref[...]
        for hh in range(2):
            data = (lane_s < HEAD) if hh == 0 else (lane_s >= HEAD)
            ka_ref[hh] = jnp.where(data, k, bk_ref[0, hh])
            va_ref[hh] = jnp.where(lane_s == _bias_lane(hh), jnp.ones_like(v), v)
        for i in range(nq):
            r0 = i * Q_BLK
            q = q_ref[r0:r0 + Q_BLK, :]
            o_out = jnp.zeros((Q_BLK, LANES), F32)
            lse_out = jnp.zeros((Q_BLK, LANES), F32)
            for hh in range(2):
                hmask = (lane < HEAD) if hh == 0 else (lane >= HEAD)
                qa = jnp.where(hmask, q * 0.125, bq_ref[0, hh, r0:r0 + Q_BLK, :])
                sd = jnp.where(tri, _dot_nt(qa, ka_ref[hh, r0:r0 + Q_BLK, :]), NEG)
                m = jnp.max(sd, axis=1, keepdims=True)
                if i:
                    sf = _dot_nt(qa, ka_ref[hh, 0:r0, :])
                    m = jnp.maximum(m, jnp.max(sf, axis=1, keepdims=True))
                acc = _dot(jnp.exp(sd - m), va_ref[hh, r0:r0 + Q_BLK, :])
                if i:
                    acc = acc + _dot(jnp.exp(sf - m), va_ref[hh, 0:r0, :])
                l = jnp.sum(jnp.where(lane == _bias_lane(hh), acc, 0.0), axis=1, keepdims=True)
                o_out = jnp.where(hmask, acc / l, o_out)
                lse_out = jnp.where(hmask, m + jnp.log(l), lse_out)
            o_ref[r0:r0 + Q_BLK, :] = o_out
            lse_ref[0, r0:r0 + Q_BLK, :] = lse_out

    seq = lambda c0: BS((S, LANES), lambda b, p: (b, c0 + p))
    pair = BS((1, 2, S, LANES), lambda b, p: (b, p, 0, 0))
    return pl.pallas_call(
        body, name="fox_fwd", grid=(Bl, 3),
        in_specs=[seq(qc), seq(kc), seq(vc), pair, pair],
        out_specs=[seq(0), BS((1, S, LANES), lambda b, p: (p, b, 0))],
        out_shape=[jax.ShapeDtypeStruct((T, B_W), F32), jax.ShapeDtypeStruct((3, T, LANES), F32)],
        scratch_shapes=[pltpu.VMEM((2, S, LANES), BF16), pltpu.VMEM((2, S, LANES), BF16)],
        compiler_params=_cp(("arbitrary", "arbitrary")),
    )(proj, proj, proj, bq, bk)


def _memkv_fwd(mem, g_mem, w_kv):
    Bl, Mt, D = mem.shape

    def body(m_ref, g_ref, w_ref, mn_ref, kv_ref):
        mn = _rms(m_ref[0], g_ref[...]).astype(BF16)
        mn_ref[0] = mn
        kv_ref[0] = jnp.dot(mn, w_ref[...], preferred_element_type=F32).astype(BF16)

    return pl.pallas_call(
        body, name="memkv_fwd", grid=(Bl,),
        in_specs=[BS((1, Mt, D), lambda b: (b, 0, 0)), BS((1, D), lambda b: (0, 0)), BS((D, 2 * M_W), lambda b: (0, 0))],
        out_specs=[BS((1, Mt, D), lambda b: (b, 0, 0)), BS((1, Mt, 2 * M_W), lambda b: (b, 0, 0))],
        out_shape=[jax.ShapeDtypeStruct((Bl, Mt, D), BF16), jax.ShapeDtypeStruct((Bl, Mt, 2 * M_W), BF16)],
        compiler_params=_cp(("arbitrary",)),
    )(mem, g_mem, w_kv)


def _memattn_fwd(proj, kv, Bl, S, tq):
    T = Bl * S
    nq = S // tq
    Mt = kv.shape[1]
    qc = 1920 // LANES

    def body(q_ref, km_ref, vm_ref, o_ref):
        lane = _iota((tq, LANES), 1)
        q = q_ref[...]
        out = jnp.zeros((tq, LANES), F32)
        for hh in range(2):
            hmask = (lane < HEAD) if hh == 0 else (lane >= HEAD)
            qs = jnp.where(hmask, q, jnp.zeros_like(q)) * 0.125
            s = _dot_nt(qs, km_ref[0])
            pe = jnp.exp(s - jnp.max(s, axis=1, keepdims=True))
            pn = pe / jnp.sum(pe, axis=1, keepdims=True)
            out = jnp.where(hmask, _dot(pn, vm_ref[0]), out)
        o_ref[...] = out

    return pl.pallas_call(
        body, name="memattn_fwd", grid=(Bl, 2, nq),
        in_specs=[BS((tq, LANES), lambda b, p, i: (b * nq + i, qc + p)),
                  BS((1, Mt, LANES), lambda b, p, i: (b, 0, p)),
                  BS((1, Mt, LANES), lambda b, p, i: (b, 0, 2 + p))],
        out_specs=BS((tq, LANES), lambda b, p, i: (b * nq + i, p)),
        out_shape=jax.ShapeDtypeStruct((T, M_W), F32),
        compiler_params=_cp(("arbitrary", "arbitrary", "arbitrary")),
    )(proj, kv, kv)


def _mix_norms(ya, yb, ym, ga, gb, gm):
    return _rms(ya, ga), _rms(yb, gb), _rms(ym, gm)


def _outproj_fwd(ya, yb, ym, x2d, ga, gb, gm, g_post, g_pre2, w_out, tm):
    T, D = x2d.shape

    def body(ya_ref, yb_ref, ym_ref, x_ref, ga_ref, gb_ref, gm_ref, gp_ref, g2_ref, w_ref,
             y_ref, o_ref, x1_ref, h2_ref):
        na, nb_, nm = _mix_norms(ya_ref[...], yb_ref[...], ym_ref[...], ga_ref[...], gb_ref[...], gm_ref[...])
        y_ref[:, 0:A_W] = na.astype(BF16)
        y_ref[:, A_W:A_W + B_W] = nb_.astype(BF16)
        y_ref[:, A_W + B_W:] = nm.astype(BF16)
        o = jnp.dot(y_ref[...], w_ref[...], preferred_element_type=F32)
        o_ref[...] = o
        x1 = x_ref[...] + _rms(o, gp_ref[...])
        x1_ref[...] = x1
        h2_ref[...] = _rms(x1, g2_ref[...]).astype(BF16)

    row = lambda w: BS((tm, w), lambda i: (i, 0))
    vec = lambda w: BS((1, w), lambda i: (0, 0))
    return pl.pallas_call(
        body, name="outproj_fwd", grid=(T // tm,),
        in_specs=[row(A_W), row(B_W), row(M_W), row(D), vec(A_W), vec(B_W), vec(M_W), vec(D), vec(D),
                  BS((A_W + B_W + M_W, D), lambda i: (0, 0))],
        out_specs=[row(A_W + B_W + M_W), row(D), row(D), row(D)],
        out_shape=[jax.ShapeDtypeStruct((T, A_W + B_W + M_W), BF16), jax.ShapeDtypeStruct((T, D), F32),
                   jax.ShapeDtypeStruct((T, D), F32), jax.ShapeDtypeStruct((T, D), BF16)],
        compiler_params=_cp(("arbitrary",)),
    )(ya, yb, ym, x2d, ga, gb, gm, g_post, g_pre2, w_out)


def _ffn_fwd(h2, x1, target, wg, wu, wd, g_post, tm):
    T, D = x1.shape
    ns, F, _ = wg.shape

    def body(h_ref, x1_ref, t_ref, wg_ref, wu_ref, wd_ref, gp_ref,
             gs_ref, us_ref, dff_ref, dx2_ref, dgp_ref, loss_ref, acc_ref):
        i = pl.program_id(0)
        j = pl.program_id(1)
        h = h_ref[...]
        g = _dot_nt(h, wg_ref[0])
        u = _dot_nt(h, wu_ref[0])
        gs_ref[0] = g.astype(BF16)
        us_ref[0] = u.astype(BF16)
        part = _dot(_silu_mul(g, u), wd_ref[0])
        _acc(acc_ref, part, j == 0)

        @pl.when(j == ns - 1)
        def _():
            normed, vjp = jax.vjp(_rms, acc_ref[...], gp_ref[...])
            diff = x1_ref[...] + normed - t_ref[...]
            dx2 = diff * (1.0 / D)
            dff, dgp = vjp(dx2)
            dx2_ref[...] = dx2
            dff_ref[...] = dff.astype(BF16)
            lpart = jnp.sum(_colsum(diff * diff), axis=1, keepdims=True) * (0.5 / D)
            _acc(dgp_ref, dgp, i == 0)
            _acc(loss_ref, jnp.broadcast_to(lpart, (1, LANES)), i == 0)

    row = lambda w: BS((tm, w), lambda i, j: (i, 0))
    return pl.pallas_call(
        body, name="ffn_fwd", grid=(T // tm, ns),
        in_specs=[row(D), row(D), row(D), BS((1, F, D), lambda i, j: (j, 0, 0)), BS((1, F, D), lambda i, j: (j, 0, 0)),
                  BS((1, F, D), lambda i, j: (j, 0, 0)), BS((1, D), lambda i, j: (0, 0))],
        out_specs=[BS((1, tm, F), lambda i, j: (j, i, 0)), BS((1, tm, F), lambda i, j: (j, i, 0)), row(D), row(D),
                   BS((1, D), lambda i, j: (0, 0)), BS((1, LANES), lambda i, j: (0, 0))],
        out_shape=[jax.ShapeDtypeStruct((ns, T, F), BF16), jax.ShapeDtypeStruct((ns, T, F), BF16),
                   jax.ShapeDtypeStruct((T, D), BF16), jax.ShapeDtypeStruct((T, D), F32),
                   jax.ShapeDtypeStruct((1, D), F32), jax.ShapeDtypeStruct((1, LANES), F32)],
        scratch_shapes=[pltpu.VMEM((tm, D), F32)],
        compiler_params=_cp(("arbitrary", "arbitrary")),
    )(h2, x1, target, wg, wu, wd, g_post)


def _ffn_bwd(dff, h2, gs, us, wg, wu, wd, tm):
    T, D = h2.shape
    ns, F, _ = wg.shape

    def body(dff_ref, h_ref, gs_ref, us_ref, wg_ref, wu_ref, wd_ref, dh_ref, dwg_ref, dwu_ref, dwd_ref):
        first = pl.program_id(1) == 0
        dff = dff_ref[...]
        h = h_ref[...]
        dact = _dot_nt(dff, wd_ref[0])
        a, vjp = jax.vjp(_silu_mul, gs_ref[0].astype(F32), us_ref[0].astype(F32))
        dg, du = vjp(dact)
        dg = dg.astype(BF16)
        du = du.astype(BF16)
        dh_ref[0] = (_dot(dg, wg_ref[0]) + _dot(du, wu_ref[0])).astype(BF16)
        _acc(dwd_ref, _dot_tn(a, dff)[None], first)
        _acc(dwg_ref, _dot_tn(dg, h)[None], first)
        _acc(dwu_ref, _dot_tn(du, h)[None], first)

    row = BS((tm, D), lambda j, i: (i, 0))
    sh = BS((1, tm, F), lambda j, i: (j, i, 0))
    wsh = BS((1, F, D), lambda j, i: (j, 0, 0))
    return pl.pallas_call(
        body, name="ffn_bwd", grid=(ns, T // tm),
        in_specs=[row, row, sh, sh, wsh, wsh, wsh],
        out_specs=[BS((1, tm, D), lambda j, i: (j, i, 0)), wsh, wsh, wsh],
        out_shape=[jax.ShapeDtypeStruct((ns, T, D), BF16)] + [jax.ShapeDtypeStruct((ns, F, D), F32)] * 3,
        compiler_params=_cp(("arbitrary", "arbitrary")),
    )(dff, h2, gs, us, wg, wu, wd)


def _mm_tn(a, b, name, tk, col_shards=1):
    nbatch = col_shards
    T, M = a.shape
    N = b.shape[1] // col_shards
    tk = min(tk, T)

    def body(a_ref, b_ref, o_ref):
        _acc(o_ref, _dot_tn(a_ref[...], b_ref[...])[None], pl.program_id(1) == 0)

    a_spec = BS((tk, M), lambda s, t: (t, 0))
    b_spec = BS((tk, N), lambda s, t: (t, s))
    return pl.pallas_call(
        body, name=name, grid=(nbatch, T // tk),
        in_specs=[a_spec, b_spec],
        out_specs=BS((1, M, N), lambda s, t: (s, 0, 0)),
        out_shape=jax.ShapeDtypeStruct((nbatch, M, N), F32),
        compiler_params=_cp(("arbitrary", "arbitrary")),
    )(a, b)


def _outproj_bwd(dh2, x1, dx2, o, ya, yb, ym, ga, gb, gm, g_post, g_pre2, w_out, tm):
    T, D = x1.shape
    ns = dh2.shape[0]

    def body(dh_ref, x1_ref, dx2_ref, o_ref, ya_ref, yb_ref, ym_ref, ga_ref, gb_ref, gm_ref, gp_ref, g2_ref, w_ref,
             dx1_ref, do_ref, dya_ref, dyb_ref, dym_ref, dga_ref, dgb_ref, dgm_ref, dgp_ref, dg2_ref):
        first = pl.program_id(0) == 0
        dh = dh_ref[0].astype(F32)
        for j in range(1, ns):
            dh = dh + dh_ref[j].astype(F32)
        _, vjp0 = jax.vjp(_rms, x1_ref[...], g2_ref[...])
        dxa, dg2 = vjp0(dh)
        dx1 = dx2_ref[...] + dxa
        dx1_ref[...] = dx1
        _acc(dg2_ref, dg2, first)
        _, vjp = jax.vjp(_rms, o_ref[...], gp_ref[...])
        do, dgp = vjp(dx1)
        do = do.astype(BF16)
        do_ref[...] = do
        dy = _dot_nt(do, w_ref[...])
        _, vjp2 = jax.vjp(_mix_norms, ya_ref[...], yb_ref[...], ym_ref[...], ga_ref[...], gb_ref[...], gm_ref[...])
        dya, dyb, dym, dga, dgb, dgm = vjp2((dy[:, 0:A_W], dy[:, A_W:A_W + B_W], dy[:, A_W + B_W:]))
        dya_ref[...] = dya
        dyb_ref[...] = dyb
        dym_ref[...] = dym
        _acc(dga_ref, dga, first)
        _acc(dgb_ref, dgb, first)
        _acc(dgm_ref, dgm, first)
        _acc(dgp_ref, dgp, first)

    row = lambda w: BS((tm, w), lambda i: (i, 0))
    vec = lambda w: BS((1, w), lambda i: (0, 0))
    sds = jax.ShapeDtypeStruct
    return pl.pallas_call(
        body, name="outproj_bwd", grid=(T // tm,),
        in_specs=[BS((ns, tm, D), lambda i: (0, i, 0)), row(D), row(D), row(D), row(A_W), row(B_W), row(M_W),
                  vec(A_W), vec(B_W), vec(M_W), vec(D), vec(D), BS((A_W + B_W + M_W, D), lambda i: (0, 0))],
        out_specs=[row(D), row(D), row(A_W), row(B_W), row(M_W), vec(A_W), vec(B_W), vec(M_W), vec(D), vec(D)],
        out_shape=[sds((T, D), F32), sds((T, D), BF16), sds((T, A_W), F32), sds((T, B_W), F32), sds((T, M_W), F32),
                   sds((1, A_W), F32), sds((1, B_W), F32), sds((1, M_W), F32), sds((1, D), F32), sds((1, D), F32)],
        compiler_params=_cp(("arbitrary",)),
    )(dh2, x1, dx2, o, ya, yb, ym, ga, gb, gm, g_post, g_pre2, w_out)


def _sgu_bwd(proj, dya, g_sgu, ws_tril, bs_full, tm):
    T = proj.shape[0]
    nch = tm // CHUNK

    def body(zu_ref, zv_ref, dy_ref, g_ref, ws_ref, b_ref, dzu_ref, dzv_ref, dws_ref, dbs_ref, dg_ref,
             du_ref, dvn_ref, dbf_ref):
        step = pl.program_id(0)
        first = step == 0
        lane = _iota((CHUNK, LANES), 1)
        tril = _iota((CHUNK, CHUNK), 0) >= _iota((CHUNK, CHUNK), 1)
        (u, vn), vjp = jax.vjp(_sgu_pre, zu_ref[...].astype(F32), zv_ref[...].astype(F32), g_ref[...])
        vnb = vn.astype(BF16)
        dy = dy_ref[...]

        @pl.when(first)
        def _():
            dws_ref[...] = jnp.zeros_like(dws_ref)
            dbf_ref[...] = jnp.zeros_like(dbf_ref)

        for c in range(nch):
            rs = slice(c * CHUNK, (c + 1) * CHUNK)
            for j in range(3):
                cs = slice(j * LANES, (j + 1) * LANES)
                vp = vnb[rs, cs]
                z = jnp.where(lane < HEAD, _dot(ws_ref[2 * j], vp), _dot(ws_ref[2 * j + 1], vp)) + b_ref[:, cs]
                du_ref[rs, cs] = dy[rs, cs] * z
                dz = dy[rs, cs] * u[rs, cs]
                dbf_ref[:, cs] += dz
                dzb = dz.astype(BF16)
                dz0 = jnp.where(lane < HEAD, dzb, jnp.zeros_like(dzb))
                dz1 = jnp.where(lane >= HEAD, dzb, jnp.zeros_like(dzb))
                dvn_ref[rs, cs] = jnp.where(lane < HEAD, _dot_tn(ws_ref[2 * j], dzb), _dot_tn(ws_ref[2 * j + 1], dzb))
                dws_ref[2 * j] += jnp.where(tril, _dot_nt(dz0, vp), 0.0)
                dws_ref[2 * j + 1] += jnp.where(tril, _dot_nt(dz1, vp), 0.0)
        dzu, dzv, dg = vjp((du_ref[...], dvn_ref[...]))
        dzu_ref[...] = dzu.astype(BF16)
        dzv_ref[...] = dzv.astype(BF16)
        _acc(dg_ref, dg, first)

        @pl.when(step == pl.num_programs(0) - 1)
        def _():
            out = jnp.zeros((CHUNK, LANES), F32)
            for j in range(3):
                slab = dbf_ref[:, j * LANES:(j + 1) * LANES]
                lo = jnp.sum(jnp.where(lane < HEAD, slab, 0.0), axis=1, keepdims=True)
                hi = jnp.sum(jnp.where(lane >= HEAD, slab, 0.0), axis=1, keepdims=True)
                out = out + jnp.where(lane == 2 * j, lo, 0.0) + jnp.where(lane == 2 * j + 1, hi, 0.0)
            dbs_ref[...] = out

    return pl.pallas_call(
        body, name="sgu_bwd", grid=(T // tm,),
        in_specs=[BS((tm, A_W), lambda i: (i, 0)), BS((tm, A_W), lambda i: (i, 1)), BS((tm, A_W), lambda i: (i, 0)),
                  BS((1, A_W), lambda i: (0, 0)), BS((6, CHUNK, CHUNK), lambda i: (0, 0, 0)),
                  BS((CHUNK, A_W), lambda i: (0, 0))],
        out_specs=[BS((tm, A_W), lambda i: (i, 0)), BS((tm, A_W), lambda i: (i, 0)),
                   BS((6, CHUNK, CHUNK), lambda i: (0, 0, 0)), BS((CHUNK, LANES), lambda i: (0, 0)),
                   BS((1, A_W), lambda i: (0, 0))],
        out_shape=[jax.ShapeDtypeStruct((T, A_W), BF16), jax.ShapeDtypeStruct((T, A_W), BF16),
                   jax.ShapeDtypeStruct((6, CHUNK, CHUNK), F32), jax.ShapeDtypeStruct((CHUNK, LANES), F32),
                   jax.ShapeDtypeStruct((1, A_W), F32)],
        scratch_shapes=[pltpu.VMEM((tm, A_W), F32), pltpu.VMEM((tm, A_W), F32), pltpu.VMEM((CHUNK, A_W), F32)],
        compiler_params=_cp(("arbitrary",)),
    )(proj, proj, dya, g_sgu, ws_tril, bs_full)


def _memattn_bwd(proj, kv, dym, Bl, S, tq):
    T = Bl * S
    nq = S // tq
    Mt = kv.shape[1]
    qc = 1920 // LANES

    def body(q_ref, km_ref, vm_ref, do_ref, dq_ref, dkm_ref, dvm_ref):
        first = pl.program_id(2) == 0
        lane = _iota((tq, LANES), 1)
        q = q_ref[...]
        do = do_ref[...]
        dq_out = jnp.zeros((tq, LANES), F32)
        dkm = jnp.zeros((Mt, LANES), F32)
        dvm = jnp.zeros((Mt, LANES), F32)
        for hh in range(2):
            hmask = (lane < HEAD) if hh == 0 else (lane >= HEAD)
            qs = jnp.where(hmask, q, jnp.zeros_like(q)) * 0.125
            dom = jnp.where(hmask, do, 0.0).astype(BF16)
            s = _dot_nt(qs, km_ref[0])
            pe = jnp.exp(s - jnp.max(s, axis=1, keepdims=True))
            pn = pe / jnp.sum(pe, axis=1, keepdims=True)
            dp = _dot_nt(dom, vm_ref[0])
            ds = (pn * (dp - jnp.sum(pn * dp, axis=1, keepdims=True))).astype(BF16)
            dq_out = jnp.where(hmask, _dot(ds, km_ref[0]) * 0.125, dq_out)
            dkm = dkm + _dot_tn(ds, qs)
            dvm = dvm + _dot_tn(pn, dom)
        dq_ref[...] = dq_out.astype(BF16)
        _acc(dkm_ref, dkm[None], first)
        _acc(dvm_ref, dvm[None], first)

    return pl.pallas_call(
        body, name="memattn_bwd", grid=(Bl, 2, nq),
        in_specs=[BS((tq, LANES), lambda b, p, i: (b * nq + i, qc + p)),
                  BS((1, Mt, LANES), lambda b, p, i: (b, 0, p)),
                  BS((1, Mt, LANES), lambda b, p, i: (b, 0, 2 + p)),
                  BS((tq, LANES), lambda b, p, i: (b * nq + i, p))],
        out_specs=[BS((tq, LANES), lambda b, p, i: (b * nq + i, p)),
                   BS((1, Mt, LANES), lambda b, p, i: (b, 0, p)),
                   BS((1, Mt, LANES), lambda b, p, i: (b, 0, p))],
        out_shape=[jax.ShapeDtypeStruct((T, M_W), BF16), jax.ShapeDtypeStruct((Bl, Mt, M_W), F32),
                   jax.ShapeDtypeStruct((Bl, Mt, M_W), F32)],
        compiler_params=_cp(("arbitrary", "arbitrary", "arbitrary")),
    )(proj, kv, kv, dym)


def _memkv_bwd(dkm, dvm, memn, mem, g_mem, w_kv):
    Bl, Mt, D = mem.shape

    def body(dk_ref, dv_ref, mn_ref, m_ref, g_ref, w_ref, dw_ref, dg_ref):
        first = pl.program_id(0) == 0
        dk = dk_ref[0].astype(BF16)
        dv = dv_ref[0].astype(BF16)
        mn = mn_ref[0]
        dmn = _dot_nt(dk, w_ref[:, 0:M_W]) + _dot_nt(dv, w_ref[:, M_W:])
        _, vjp = jax.vjp(_rms, m_ref[0], g_ref[...])
        _, dg = vjp(dmn)
        _acc(dg_ref, dg, first)

        @pl.when(first)
        def _():
            dw_ref[...] = jnp.zeros_like(dw_ref)

        dw_ref[:, 0:M_W] += _dot_tn(mn, dk)
        dw_ref[:, M_W:] += _dot_tn(mn, dv)

    return pl.pallas_call(
        body, name="memkv_bwd", grid=(Bl,),
        in_specs=[BS((1, Mt, M_W), lambda b: (b, 0, 0)), BS((1, Mt, M_W), lambda b: (b, 0, 0)),
                  BS((1, Mt, D), lambda b: (b, 0, 0)), BS((1, Mt, D), lambda b: (b, 0, 0)),
                  BS((1, D), lambda b: (0, 0)), BS((D, 2 * M_W), lambda b: (0, 0))],
        out_specs=[BS((D, 2 * M_W), lambda b: (0, 0)), BS((1, D), lambda b: (0, 0))],
        out_shape=[jax.ShapeDtypeStruct((D, 2 * M_W), F32), jax.ShapeDtypeStruct((1, D), F32)],
        compiler_params=_cp(("arbitrary",)),
    )(dkm, dvm, memn, mem, g_mem, w_kv)


def _fox_bwd(proj, dyb, lse, bq, bk, Bl, S):
    T = Bl * S
    nq = S // Q_BLK
    nb = S // LANES
    qc, kc, vc = 768 // LANES, 1152 // LANES, 1536 // LANES

    def body(q_ref, k_ref, v_ref, do_ref, lse_ref, bq_ref, bk_ref,
             dq_ref, dk_ref, dv_ref, dcr_ref, ka_ref, dka_ref, dva_ref):
        p = pl.program_id(1)
        lane_s = _iota((S, LANES), 1)
        lane = _iota((Q_BLK, LANES), 1)
        sub = _iota((8, LANES), 0)
        tri = _iota((Q_BLK, Q_BLK), 1) <= _iota((Q_BLK, Q_BLK), 0)
        k = k_ref[...]
        for hh in range(2):
            data = (lane_s < HEAD) if hh == 0 else (lane_s >= HEAD)
            ka_ref[hh] = jnp.where(data, k, bk_ref[0, hh])
        dka_ref[...] = jnp.zeros_like(dka_ref)
        dva_ref[...] = jnp.zeros_like(dva_ref)

        @pl.when(p == 0)
        def _():
            dcr_ref[...] = jnp.zeros_like(dcr_ref)

        def add_colsums(ds, first_blk, h):
            cs = _colsum(ds)
            for jb in range(ds.shape[1] // LANES):
                dcr_ref[0, first_blk + jb] += jnp.where(sub == h, cs[:, jb * LANES:(jb + 1) * LANES], 0.0)

        for i in range(nq):
            r0 = i * Q_BLK
            r1 = r0 + Q_BLK
            q = q_ref[r0:r1, :]
            do = do_ref[r0:r1, :]
            lse_b = lse_ref[0, r0:r1, :]
            dq_out = jnp.zeros((Q_BLK, LANES), F32)
            for hh in range(2):
                hmask = (lane < HEAD) if hh == 0 else (lane >= HEAD)
                h = 2 * p + hh
                qs = jnp.where(hmask, q * 0.125, jnp.zeros_like(q))
                qa = jnp.where(hmask, q * 0.125, bq_ref[0, hh, r0:r1, :])
                dob = jnp.where(hmask, do, 0.0).astype(BF16)
                lse_h = jnp.sum(jnp.where(lane == hh * HEAD, lse_b, 0.0), axis=1, keepdims=True)
                pd = jnp.where(tri, jnp.exp(_dot_nt(qa, ka_ref[hh, r0:r1, :]) - lse_h), 0.0)
                dpd = _dot_nt(dob, v_ref[r0:r1, :])
                delta = jnp.sum(pd * dpd, axis=1, keepdims=True)
                psum = jnp.sum(pd, axis=1, keepdims=True)
                if i:
                    pf = jnp.exp(_dot_nt(qa, ka_ref[hh, 0:r0, :]) - lse_h)
                    dpf = _dot_nt(dob, v_ref[0:r0, :])
                    delta = delta + jnp.sum(pf * dpf, axis=1, keepdims=True)
                    psum = psum + jnp.sum(pf, axis=1, keepdims=True)
                delta = delta / psum
                dsd = pd * (dpd - delta)
                add_colsums(dsd, r0 // LANES, h)
                dsd = dsd.astype(BF16)
                dq_h = _dot(dsd, k_ref[r0:r1, :])
                dka_ref[r0:r1, :] += _dot_tn(dsd, qs)
                dva_ref[r0:r1, :] += _dot_tn(pd, dob)
                if i:
                    dsf = pf * (dpf - delta)
                    add_colsums(dsf, 0, h)
                    dsf = dsf.astype(BF16)
                    dq_h = dq_h + _dot(dsf, k_ref[0:r0, :])
                    dka_ref[0:r0, :] += _dot_tn(dsf, qs)
                    dva_ref[0:r0, :] += _dot_tn(pf, dob)
                dq_out = jnp.where(hmask, dq_h * 0.125, dq_out)
            dq_ref[r0:r1, :] = dq_out.astype(BF16)
        dk_ref[...] = dka_ref[...].astype(BF16)
        dv_ref[...] = dva_ref[...].astype(BF16)

    seq = lambda c0: BS((S, LANES), lambda b, p: (b, c0 + p))
    pair = BS((1, 2, S, LANES), lambda b, p: (b, p, 0, 0))
    rowblk = BS((1, nb, 8, LANES), lambda b, p: (b, 0, 0, 0))
    return pl.pallas_call(
        body, name="fox_bwd", grid=(Bl, 3),
        in_specs=[seq(qc), seq(kc), seq(vc), seq(0), BS((1, S, LANES), lambda b, p: (p, b, 0)), pair, pair],
        out_specs=[seq(0), seq(0), seq(0), rowblk],
        out_shape=[jax.ShapeDtypeStruct((T, B_W), BF16)] * 3 + [jax.ShapeDtypeStruct((Bl, nb, 8, LANES), F32)],
        scratch_shapes=[pltpu.VMEM((2, S, LANES), BF16), pltpu.VMEM((S, LANES), F32), pltpu.VMEM((S, LANES), F32)],
        compiler_params=_cp(("arbitrary", "arbitrary")),
    )(proj, proj, proj, dyb, lse, bq, bk)


def _gate_bwd(dc_row, fl_row):
    Bl, nb, _, _ = dc_row.shape

    def body(dc_ref, fl_ref, o_ref):
        lane = _iota((8, LANES), 1)

        def blk(jj, carry):
            j = nb - 1 - jj
            r = -dc_ref[0, j]
            for k in (1, 2, 4, 8, 16, 32, 64):
                r = r + jnp.where(lane < LANES - k, pltpu.roll(r, LANES - k, 1), 0.0)
            r = r + carry
            dfl = r * _sigmoid(-fl_ref[0, j])
            o_ref[0, pl.ds(pl.multiple_of(j * LANES, LANES), LANES), :] = jnp.concatenate(
                [dfl, jnp.zeros((LANES - 8, LANES), F32)], axis=0).T
            return jnp.sum(jnp.where(lane == 0, r, 0.0), axis=1, keepdims=True)

        lax.fori_loop(0, nb, blk, jnp.zeros((8, 1), F32))

    rowblk = BS((1, nb, 8, LANES), lambda b: (b, 0, 0, 0))
    return pl.pallas_call(
        body, name="gate_bwd", grid=(Bl,),
        in_specs=[rowblk, rowblk],
        out_specs=BS((1, nb * LANES, LANES), lambda b: (b, 0, 0)),
        out_shape=jax.ShapeDtypeStruct((Bl, nb * LANES, LANES), F32),
        compiler_params=_cp(("arbitrary",)),
    )(dc_row, fl_row)


def _inproj_bwd(dzu, dzv, dq, dk, dv, dqm, dfl, x2d, dx1, g_pre, w_in_p, tm):
    T, D = x2d.shape
    ns, _, dsh = w_in_p.shape

    def body(dzu_ref, dzv_ref, dq_ref, dk_ref, dv_ref, dqm_ref, dfl_ref, x_ref, dx1_ref, g_ref, w_ref,
             dp_ref, gx_ref, dg_ref, dbf_ref):
        first = pl.program_id(0) == 0
        dfl = dfl_ref[...]
        dp_ref[:, 0:384] = dzu_ref[...]
        dp_ref[:, 384:768] = dzv_ref[...]
        dp_ref[:, 768:1152] = dq_ref[...]
        dp_ref[:, 1152:1536] = dk_ref[...]
        dp_ref[:, 1536:1920] = dv_ref[...]
        dp_ref[:, 1920:2048] = dfl.astype(BF16)
        dp_ref[:, 2048:2304] = dqm_ref[...]
        dh = jnp.concatenate([_dot(dp_ref[...], w_ref[s]) for s in range(ns)], axis=1)
        _, vjp = jax.vjp(_rms, x_ref[...], g_ref[...])
        dxa, dg = vjp(dh)
        gx_ref[...] = dx1_ref[...] + dxa
        _acc(dg_ref, dg, first)
        _acc(dbf_ref, _colsum(dfl), first)

    row = lambda w: BS((tm, w), lambda i: (i, 0))
    return pl.pallas_call(
        body, name="inproj_bwd", grid=(T // tm,),
        in_specs=[row(A_W), row(A_W), row(B_W), row(B_W), row(B_W), row(M_W), row(LANES), row(D), row(D),
                  BS((1, D), lambda i: (0, 0)), BS((ns, P_COLS, dsh), lambda i: (0, 0, 0))],
        out_specs=[row(P_COLS), row(D), BS((1, D), lambda i: (0, 0)), BS((1, LANES), lambda i: (0, 0))],
        out_shape=[jax.ShapeDtypeStruct((T, P_COLS), BF16), jax.ShapeDtypeStruct((T, D), F32),
                   jax.ShapeDtypeStruct((1, D), F32), jax.ShapeDtypeStruct((1, LANES), F32)],
        compiler_params=_cp(("arbitrary",)),
    )(dzu, dzv, dq, dk, dv, dqm, dfl, x2d, dx1, g_pre, w_in_p)


def _local_step(x, mem, target, W, P):
    Bl, S, D = x.shape
    T = Bl * S
    tm = min(512, T)
    x2d = x.reshape(T, D)
    t2d = target.reshape(T, D)
    vec = lambda a: a.reshape(1, -1)
    bf_row = jnp.pad(P["b_f"].reshape(1, -1), ((0, 0), (0, LANES - N_FOX_HEADS)))
    tril = jnp.tril(jnp.ones((CHUNK, CHUNK), bool))
    ws_tril = jnp.where(tril[None], P["w_s"][0], 0.0).astype(BF16)
    bs_full = jnp.repeat(P["b_s"][0].T, HEAD, axis=1)
    g_pre, g_sgu = vec(P["g_pre_mix"]), vec(P["g_sgu"])
    ga, gb, gm = vec(P["g_out_a"]), vec(P["g_out_b"]), vec(P["g_out_m"])
    g_mem, g_post, g_pre2, g_post2 = vec(P["g_mem"]), vec(P["g_post_mix"]), vec(P["g_pre_ffn"]), vec(P["g_post_ffn"])

    h, proj, flog = _inproj_fwd(x2d, g_pre, W["w_in"], tm)
    bq, bk, fl_row = _gate_fwd(flog.reshape(Bl, S, LANES), bf_row)
    ya = _sgu_fwd(proj, g_sgu, ws_tril, bs_full, tm)
    yb, lse = _fox_fwd(proj, bq, bk, Bl, S)
    memn, kv = _memkv_fwd(mem, g_mem, W["w_mem_kv"])
    ym = _memattn_fwd(proj, kv, Bl, S, min(512, S))
    y, o, x1, h2 = _outproj_fwd(ya, yb, ym, x2d, ga, gb, gm, g_post, g_pre2, W["w_out"], tm)
    gs, us, dff, dx2, dg_post2, loss = _ffn_fwd(h2, x1, t2d, W["w_gate"], W["w_up"], W["w_down"], g_post2, tm)

    dh2, d_w_gate, d_w_up, d_w_down = _ffn_bwd(dff, h2, gs, us, W["w_gate"], W["w_up"], W["w_down"], tm)
    dx1, do, dya, dyb, dym, dga, dgb, dgm, dg_post, dg_pre2 = _outproj_bwd(
        dh2, x1, dx2, o, ya, yb, ym, ga, gb, gm, g_post, g_pre2, W["w_out"], tm)
    d_w_out = _mm_tn(y, do, "dw_out", 1024)[0]
    dzu, dzv, dws, dbs_cols, dg_sgu = _sgu_bwd(proj, dya, g_sgu, ws_tril, bs_full, tm)
    dqm, dkm, dvm = _memattn_bwd(proj, kv, dym, Bl, S, min(512, S))
    d_w_kv, dg_mem = _memkv_bwd(dkm, dvm, memn, mem, g_mem, W["w_mem_kv"])
    dq, dk, dv, dc_row = _fox_bwd(proj, dyb, lse, bq, bk, Bl, S)
    dfl = _gate_bwd(dc_row, fl_row).reshape(T, LANES)
    dproj, grad_x, dg_pre, dbf = _inproj_bwd(dzu, dzv, dq, dk, dv, dqm, dfl, x2d, dx1, g_pre, W["w_in"], tm)
    d_w_in = _mm_tn(dproj, h, "dw_in", 1024, col_shards=W["w_in"].shape[0])

    big = {"w_in": d_w_in, "w_mem_kv": d_w_kv, "w_out": d_w_out, "w_gate": d_w_gate, "w_up": d_w_up,
           "w_down": d_w_down}
    small = {"g_pre_mix": dg_pre, "b_f": dbf[:, :N_FOX_HEADS], "g_sgu": dg_sgu, "w_s": dws, "b_s": dbs_cols[:, :N_FOX_HEADS].T,
             "g_out_a": dga, "g_out_b": dgb, "g_out_m": dgm, "g_mem": dg_mem, "g_post_mix": dg_post,
             "g_pre_ffn": dg_pre2, "g_post_ffn": dg_post2, "loss": loss[:, :1]}
    return grad_x.reshape(Bl, S, D), big, small


def _place():
    return lax.axis_index("x"), lax.axis_index("y"), lax.axis_index("c")


def _exchange(srcs, own_full, name):
    n = len(srcs)
    nd2d = 3 if own_full else 4

    def body(*refs):
        src, dst, own = refs[:n], refs[n:2 * n], refs[2 * n:3 * n]
        lsem, osem, isend, irecv, dsend, drecv = refs[3 * n:]
        x, y, c = _place()
        oc = 1 - c
        s_me = 2 * x + y
        sib = (x, y, oc)
        chips = [(1 - x, y), (x, 1 - y), (1 - x, 1 - y)]

        def remote(a, b, ssem, rsem, dev):
            return pltpu.make_async_remote_copy(src_ref=a, dst_ref=b, send_sem=ssem, recv_sem=rsem,
                                                device_id=dev, device_id_type=MESH)

        sends, loads, local = [], [], []
        for w in range(n):
            ld = pltpu.make_async_copy(src[w] if own_full else src[w].at[s_me], own[w], lsem.at[w])
            ld.start()
            loads.append(ld)
        for w in range(n):
            for j, (cx, cy) in enumerate(chips):
                half = src[w].at[c] if own_full else src[w].at[2 * cx + cy]
                cp = remote(half, dst[w].at[s_me, c], isend.at[w, j], irecv.at[w, j], (cx, cy, c))
                cp.start()
                sends.append(cp)
            if not own_full:
                cp = remote(src[w].at[s_me], dst[w].at[s_me, c], dsend.at[w, 3], drecv.at[w, 3], sib)
                cp.start()
                sends.append(cp)
        for w in range(n):
            loads[w].wait()
            st = pltpu.make_async_copy(own[w], dst[w].at[s_me] if own_full else dst[w].at[s_me, c], osem.at[w])
            st.start()
            local.append(st)
        for w in range(n):
            for j, (cx, cy) in enumerate(chips):
                landed = dst[w].at[2 * cx + cy, c]
                remote(landed, landed, isend.at[w, j], irecv.at[w, j], (cx, cy, c)).wait_recv()
                cp = remote(landed, landed, dsend.at[w, j], drecv.at[w, j], sib)
                cp.start()
                sends.append(cp)
        for w in range(n):
            for j, (cx, cy) in enumerate(chips):
                landed = dst[w].at[2 * cx + cy, oc]
                remote(landed, landed, dsend.at[w, j], drecv.at[w, j], sib).wait_recv()
            if not own_full:
                landed = dst[w].at[s_me, oc]
                remote(landed, landed, dsend.at[w, 3], drecv.at[w, 3], sib).wait_recv()
        for cp in sends:
            cp.wait_send()
        for loc in local:
            loc.wait()

    out_shape = [jax.ShapeDtypeStruct((4, 2) + s.shape[1:], s.dtype) for s in srcs]
    own_shapes = [pltpu.VMEM(s.shape if own_full else s.shape[1:], s.dtype) for s in srcs]
    return pl.pallas_call(
        body, name=name, in_specs=[ANY] * n, out_specs=[ANY] * n, out_shape=out_shape,
        scratch_shapes=own_shapes + [pltpu.SemaphoreType.DMA((n,)), pltpu.SemaphoreType.DMA((n,)),
                                     pltpu.SemaphoreType.DMA((n, 3)), pltpu.SemaphoreType.DMA((n, 3)),
                                     pltpu.SemaphoreType.DMA((n, nd2d)), pltpu.SemaphoreType.DMA((n, nd2d))],
        compiler_params=pltpu.CompilerParams(vmem_limit_bytes=VMEM_LIMIT),
    )(*srcs)


def _gather_on_sequencer(srcs, name, collective_id):
    n = len(srcs)

    def body(*refs):
        src, dst = refs[:n], refs[n:2 * n]
        isend, irecv, dsend, drecv = refs[2 * n:]
        x, y, c = _place()
        oc = 1 - c
        s_me = 2 * x + y
        sib = (x, y, oc)
        chips = [(1 - x, y), (x, 1 - y), (1 - x, 1 - y)]
        barrier = pltpu.get_barrier_semaphore()
        for dev in [(cx, cy, c) for cx, cy in chips] + [sib]:
            pl.semaphore_signal(barrier, inc=1, device_id=dev, device_id_type=MESH)
        pl.semaphore_wait(barrier, 4)

        def remote(a, b, ssem, rsem, dev):
            return pltpu.make_async_remote_copy(src_ref=a, dst_ref=b, send_sem=ssem, recv_sem=rsem,
                                                device_id=dev, device_id_type=MESH)

        sends = []
        for w in range(n):
            for j, (cx, cy) in enumerate(chips):
                cp = remote(src[w].at[c], dst[w].at[s_me, c], isend.at[w, j], irecv.at[w, j], (cx, cy, c))
                cp.start()
                sends.append(cp)
            cp = remote(src[w], dst[w].at[s_me], dsend.at[w, 3], drecv.at[w, 3], sib)
            cp.start()
            sends.append(cp)
        for w in range(n):
            for j, (cx, cy) in enumerate(chips):
                landed = dst[w].at[2 * cx + cy, c]
                remote(landed, landed, isend.at[w, j], irecv.at[w, j], (cx, cy, c)).wait_recv()
                cp = remote(landed, landed, dsend.at[w, j], drecv.at[w, j], sib)
                cp.start()
                sends.append(cp)
        for w in range(n):
            for j, (cx, cy) in enumerate(chips):
                landed = dst[w].at[2 * cx + cy, oc]
                remote(landed, landed, dsend.at[w, j], drecv.at[w, j], sib).wait_recv()
            landed = dst[w].at[s_me]
            remote(landed, landed, dsend.at[w, 3], drecv.at[w, 3], sib).wait_recv()
        for cp in sends:
            cp.wait_send()

    return pl.kernel(
        body, out_type=[jax.ShapeDtypeStruct((4, 2) + s.shape[1:], s.dtype) for s in srcs],
        mesh=plsc.ScalarSubcoreMesh(axis_name="sequencer", num_cores=1), name=name,
        scratch_types=[pltpu.SemaphoreType.DMA((n, 3)), pltpu.SemaphoreType.DMA((n, 3)),
                       pltpu.SemaphoreType.DMA((n, 4)), pltpu.SemaphoreType.DMA((n, 4))],
        compiler_params=pltpu.CompilerParams(collective_id=collective_id),
    )(*srcs)


def _sibling_swap(grads):
    n = len(grads)

    def body(*refs):
        g, theirs = refs[:n], refs[n:2 * n]
        ssem, rsem = refs[2 * n:]
        x, y, c = _place()
        cps = []
        for w in range(n):
            cp = pltpu.make_async_remote_copy(src_ref=g[w].at[:, 1 - c], dst_ref=theirs[w], send_sem=ssem.at[w],
                                              recv_sem=rsem.at[w], device_id=(x, y, 1 - c), device_id_type=MESH)
            cp.start()
            cps.append(cp)
        for cp in cps:
            cp.wait()

    half = [jax.ShapeDtypeStruct((4,) + g.shape[2:], g.dtype) for g in grads]
    return pl.pallas_call(
        body, name="sibling_swap", in_specs=[ANY] * n, out_specs=[ANY] * n, out_shape=half,
        scratch_shapes=[pltpu.SemaphoreType.DMA((n,)), pltpu.SemaphoreType.DMA((n,))],
    )(*grads)


def _add_pair(core, g, theirs, name):
    _, _, hr, C = g.shape

    def body(core_ref, g_ref, t_ref, o_ref):
        o_ref[0] = (g_ref[0, 0] + t_ref[0]).astype(BF16)

    blk = BS((1, hr, C), lambda s, core_ref: (s, 0, 0))
    return pl.pallas_call(
        body, name=name,
        grid_spec=pltpu.PrefetchScalarGridSpec(
            num_scalar_prefetch=1, grid=(4,),
            in_specs=[BS((1, 1, hr, C), lambda s, core_ref: (s, core_ref[0], 0, 0)), blk], out_specs=blk),
        out_shape=jax.ShapeDtypeStruct(theirs.shape, BF16), compiler_params=_cp(("arbitrary",)))(core, g, theirs)


def _sum_chips(r, name):
    _, _, hr, C = r.shape

    def body(r_ref, o_ref):
        o_ref[...] = ((r_ref[0, 0].astype(F32) + r_ref[1, 0].astype(F32)) + r_ref[2, 0].astype(F32)) + r_ref[3, 0].astype(F32)

    return pl.pallas_call(body, name=name, grid=(2,), in_specs=[BS((4, 1, hr, C), lambda h: (0, h, 0, 0))],
                          out_specs=BS((hr, C), lambda h: (h, 0)), out_shape=jax.ShapeDtypeStruct((2 * hr, C), F32),
                          compiler_params=_cp(("arbitrary",)))(r)


def _small_allreduce(part):
    R = part.shape[0]
    rs = R // 8
    masks = [(mx, my, mc) for mx in (0, 1) for my in (0, 1) for mc in (0, 1)][1:]

    def body(p_ref, o_ref, buf_ref, s1, r1, s2, r2):
        x, y, c = _place()
        d = 4 * x + 2 * y + c
        mine = pl.ds(pl.multiple_of(d * rs, 8), rs)
        peers = [((x + mx) % 2, (y + my) % 2, (c + mc) % 2) for mx, my, mc in masks]
        first, second = [], []
        for k, (px, py, pc) in enumerate(peers):
            theirs = pl.ds(pl.multiple_of((4 * px + 2 * py + pc) * rs, 8), rs)
            cp = pltpu.make_async_remote_copy(src_ref=p_ref.at[theirs, :], dst_ref=buf_ref.at[d], send_sem=s1.at[k],
                                              recv_sem=r1.at[k], device_id=(px, py, pc), device_id_type=MESH)
            cp.start()
            first.append(cp)
        buf_ref[d] = p_ref[mine, :]
        for k, (px, py, pc) in enumerate(peers):
            slot = buf_ref.at[4 * px + 2 * py + pc]
            pltpu.make_async_remote_copy(src_ref=slot, dst_ref=slot, send_sem=s1.at[k], recv_sem=r1.at[k],
                                         device_id=(px, py, pc), device_id_type=MESH).wait_recv()
        total = buf_ref[0]
        for k in range(1, 8):
            total = total + buf_ref[k]
        o_ref[mine, :] = total
        for k, (px, py, pc) in enumerate(peers):
            cp = pltpu.make_async_remote_copy(src_ref=o_ref.at[mine, :], dst_ref=o_ref.at[mine, :], send_sem=s2.at[k],
                                              recv_sem=r2.at[k], device_id=(px, py, pc), device_id_type=MESH)
            cp.start()
            second.append(cp)
        for k, (px, py, pc) in enumerate(peers):
            rows = o_ref.at[pl.ds(pl.multiple_of((4 * px + 2 * py + pc) * rs, 8), rs), :]
            pltpu.make_async_remote_copy(src_ref=rows, dst_ref=rows, send_sem=s2.at[k], recv_sem=r2.at[k],
                                         device_id=(px, py, pc), device_id_type=MESH).wait_recv()
        for cp in first + second:
            cp.wait_send()

    vm = pl.BlockSpec(memory_space=pltpu.VMEM)
    return pl.pallas_call(
        body, name="small_allreduce", in_specs=[vm], out_specs=vm, out_shape=jax.ShapeDtypeStruct(part.shape, F32),
        scratch_shapes=[pltpu.VMEM((8, rs, LANES), F32)] + [pltpu.SemaphoreType.DMA((7,))] * 4,
    )(part)


def _adamw(w, g, m, v, name):
    R, C = w.shape
    tr = R if R * C * 4 <= (1 << 21) else R // 2
    if tr % 8:
        tr = R
    c1 = 1.0 / (1.0 - ADAM_B1 ** ADAM_STEP)
    c2 = 1.0 / (1.0 - ADAM_B2 ** ADAM_STEP)

    def body(w_ref, g_ref, m_ref, v_ref, d_ref, mo_ref, vo_ref):
        g_ = g_ref[...]
        m_ = ADAM_B1 * m_ref[...] + (1.0 - ADAM_B1) * g_
        v_ = ADAM_B2 * v_ref[...] + (1.0 - ADAM_B2) * (g_ * g_)
        mo_ref[...] = m_
        vo_ref[...] = v_
        d_ref[...] = -ADAM_LR * ((m_ * c1) / (jnp.sqrt(v_ * c2) + ADAM_EPS) + ADAM_WD * w_ref[...])

    blk = BS((tr, C), lambda i: (i, 0))
    return pl.pallas_call(body, name=name, grid=(R // tr,), in_specs=[blk] * 4, out_specs=[blk] * 3,
                          out_shape=[jax.ShapeDtypeStruct((R, C), F32)] * 3, compiler_params=_cp(("arbitrary",)))(w, g, m, v)


SMALL = ("g_pre_mix", "b_f", "g_sgu", "w_s", "b_s", "g_out_a", "g_out_b", "g_out_m", "g_mem", "g_post_mix",
         "g_pre_ffn", "g_post_ffn")
BIG = ("w_in", "w_mem_kv", "w_out", "w_gate", "w_up", "w_down")
TRANSPOSED = ("w_in", "w_gate", "w_up")
WEIGHTS = ("g_pre_mix", "w_in", "b_f", "g_sgu", "w_s", "b_s", "g_out_a", "g_out_b", "g_out_m", "g_mem", "w_mem_kv",
           "w_out", "g_post_mix", "g_pre_ffn", "w_gate", "w_up", "w_down", "g_post_ffn")


def _rows_of(n):
    return -(-n // (8 * LANES)) * 8


def _pack(parts):
    tiles = []
    for a in parts:
        flat = a.reshape(-1).astype(F32)
        rows = _rows_of(flat.shape[0])
        tiles.append(jnp.pad(flat, (0, rows * LANES - flat.shape[0])).reshape(rows, LANES))
    total = sum(t.shape[0] for t in tiles)
    pad = -total % 64
    if pad:
        tiles.append(jnp.zeros((pad, LANES), F32))
    return jnp.concatenate(tiles, axis=0)


def _unpack(packed, shapes):
    out, r = [], 0
    for shp in shapes:
        n = 1
        for s in shp:
            n *= s
        rows = _rows_of(n)
        out.append(packed[r:r + rows].reshape(-1)[:n].reshape(shp))
        r += rows
    return out


def kernel(x, mem, g_pre_mix, w_in, b_f, g_sgu, w_s, b_s, g_out_a, g_out_b, g_out_m, g_mem, w_mem_kv, w_out, g_post_mix, g_pre_ffn, w_gate, w_up, w_down, g_post_ffn, loss_target, m_g_pre_mix, m_w_in, m_b_f, m_g_sgu, m_w_s, m_b_s, m_g_out_a, m_g_out_b, m_g_out_m, m_g_mem, m_w_mem_kv, m_w_out, m_g_post_mix, m_g_pre_ffn, m_w_gate, m_w_up, m_w_down, m_g_post_ffn, v_g_pre_mix, v_w_in, v_b_f, v_g_sgu, v_w_s, v_b_s, v_g_out_a, v_g_out_b, v_g_out_m, v_g_mem, v_w_mem_kv, v_w_out, v_g_post_mix, v_g_pre_ffn, v_w_gate, v_w_up, v_w_down, v_g_post_ffn):
    Wt = dict(g_pre_mix=g_pre_mix, w_in=w_in, b_f=b_f, g_sgu=g_sgu, w_s=w_s, b_s=b_s, g_out_a=g_out_a, g_out_b=g_out_b,
              g_out_m=g_out_m, g_mem=g_mem, w_mem_kv=w_mem_kv, w_out=w_out, g_post_mix=g_post_mix, g_pre_ffn=g_pre_ffn,
              w_gate=w_gate, w_up=w_up, w_down=w_down, g_post_ffn=g_post_ffn)
    Mo = dict(g_pre_mix=m_g_pre_mix, w_in=m_w_in, b_f=m_b_f, g_sgu=m_g_sgu, w_s=m_w_s, b_s=m_b_s, g_out_a=m_g_out_a,
              g_out_b=m_g_out_b, g_out_m=m_g_out_m, g_mem=m_g_mem, w_mem_kv=m_w_mem_kv, w_out=m_w_out,
              g_post_mix=m_g_post_mix, g_pre_ffn=m_g_pre_ffn, w_gate=m_w_gate, w_up=m_w_up, w_down=m_w_down,
              g_post_ffn=m_g_post_ffn)
    Vo = dict(g_pre_mix=v_g_pre_mix, w_in=v_w_in, b_f=v_b_f, g_sgu=v_g_sgu, w_s=v_w_s, b_s=v_b_s, g_out_a=v_g_out_a,
              g_out_b=v_g_out_b, g_out_m=v_g_out_m, g_mem=v_g_mem, w_mem_kv=v_w_mem_kv, w_out=v_w_out,
              g_post_mix=v_g_post_mix, g_pre_ffn=v_g_pre_ffn, w_gate=v_w_gate, w_up=v_w_up, w_down=v_w_down,
              g_post_ffn=v_g_post_ffn)

    gap = P_COLS - IN_COLS

    def to_kernel(n, w):
        if n in TRANSPOSED:
            w = w.T
        if n == "w_in":
            w = jnp.pad(w[:F_END], ((0, P_COLS - F_END), (0, 0))) + jnp.pad(w[F_END:], ((F_END + gap, 0), (0, 0)))
        return w

    def ungroup(g):
        return jnp.pad(g[:F_END], ((0, IN_COLS - F_END), (0, 0))) + jnp.pad(g[F_END + gap:], ((F_END, 0), (0, 0)))

    shards = {n: to_kernel(n, Wt[n][0]) for n in BIG}
    srcs = [shards[n].astype(BF16).reshape(2, shards[n].shape[0] // 2, shards[n].shape[1]) for n in BIG]
    fulls = _gather_on_sequencer(srcs[:1], "gather_w_in", 1) + _gather_on_sequencer(srcs[1:], "gather_rest", 2)
    W = {}
    for n, f in zip(BIG, fulls):
        _, _, hr, C = f.shape
        W[n] = f.reshape(8 * hr, C) if n in ("w_mem_kv", "w_out") else f.reshape(4, 2 * hr, C)

    P = {n: Wt[n] for n in SMALL}
    grad_x, big, small = _local_step(x, mem, loss_target, W, P)

    g4 = []
    for n in BIG:
        g = big[n]
        C = g.shape[-1]
        g4.append(g.reshape(4, 2, -1, C))
    theirs = _sibling_swap(g4)
    core = lax.axis_index("c").astype(jnp.int32).reshape(1)
    chip_sums = [_add_pair(core, g, t, "chip_sum_" + n) for n, g, t in zip(BIG, g4, theirs)]
    landed = _exchange(chip_sums, False, "scatter_grads")
    grads, deltas, new_m, new_v = {}, {}, {}, {}
    for n, r in zip(BIG, landed):
        g = _sum_chips(r, "sum_chips_" + n)
        if n == "w_in":
            g = ungroup(g)
        if n in TRANSPOSED:
            d, m1, v1 = _adamw(Wt[n][0].T, g, Mo[n][0].T, Vo[n][0].T, "adamw_" + n)
            g, d, m1, v1 = g.T, d.T, m1.T, v1.T
        else:
            d, m1, v1 = _adamw(Wt[n][0], g, Mo[n][0], Vo[n][0], "adamw_" + n)
        grads[n], deltas[n], new_m[n], new_v[n] = g[None], d[None], m1[None], v1[None]

    total = _small_allreduce(_pack([small[n] for n in SMALL] + [small["loss"]]))
    slot = [jnp.zeros((1, 1), F32)]
    shapes = [Wt[n].shape for n in SMALL] + [(1, 1)]
    d, m1, v1 = _adamw(_pack([Wt[n] for n in SMALL] + slot), total, _pack([Mo[n] for n in SMALL] + slot),
                       _pack([Vo[n] for n in SMALL] + slot), "adamw_small")
    g_s, d_s, m_s, v_s = _unpack(total, shapes), _unpack(d, shapes), _unpack(m1, shapes), _unpack(v1, shapes)
    for k, n in enumerate(SMALL):
        grads[n], deltas[n], new_m[n], new_v[n] = g_s[k], d_s[k], m_s[k], v_s[k]
    loss = g_s[-1][0, 0]

    return (loss, grad_x, *[grads[n] for n in WEIGHTS], *[deltas[n] for n in WEIGHTS],
            *[new_m[n] for n in WEIGHTS], *[new_v[n] for n in WEIGHTS])
```

```python
import functools

import jax
import jax.numpy as jnp
from jax import lax
from jax.experimental import pallas as pl
from jax.experimental.pallas import tpu as pltpu
from jax.experimental.pallas import tpu_sc as plsc

F32 = jnp.float32
BF16 = jnp.bfloat16
EPS = 1e-6
NEG = -1e30
HEAD = 64
A_W, B_W, M_W = 384, 384, 256
N_FOX_HEADS = 6
CHUNK = 128
IN_COLS = 2 * A_W + 3 * B_W + N_FOX_HEADS + M_W
P_MAIN = 2 * A_W + 3 * B_W + M_W
P_COLS = P_MAIN + 128
F_END = 2 * A_W + 3 * B_W + N_FOX_HEADS
LANES = 128
Q_BLK, K_BLK = 256, 128
ADAM_LR, ADAM_B1, ADAM_B2, ADAM_EPS, ADAM_WD, ADAM_STEP = 0.001, 0.9, 0.999, 1e-08, 0.01, 10
VMEM_LIMIT = 56 * 1024 * 1024
MESH = pl.DeviceIdType.MESH
ANY = pl.BlockSpec(memory_space=pl.ANY)
BS = pl.BlockSpec


def _cp(sem=None):
    return pltpu.CompilerParams(dimension_semantics=sem, vmem_limit_bytes=VMEM_LIMIT)


def _iota(shape, dim):
    return lax.broadcasted_iota(jnp.int32, shape, dim)


def _dot(a, b):
    return jnp.dot(a.astype(BF16), b.astype(BF16), preferred_element_type=F32)


def _dot_nt(a, b):
    return lax.dot_general(a.astype(BF16), b.astype(BF16), (((1,), (1,)), ((), ())), preferred_element_type=F32)


def _dot_tn(a, b):
    return lax.dot_general(a.astype(BF16), b.astype(BF16), (((0,), (0,)), ((), ())), preferred_element_type=F32)


def _rms(x, g):
    return x * lax.rsqrt(jnp.mean(x * x, axis=-1, keepdims=True) + EPS) * g


def _gelu(x):
    return 0.5 * x * (1.0 + jnp.tanh(0.7978845608028654 * (x + 0.044715 * (x * x * x))))


def _sigmoid(x):
    return 1.0 / (1.0 + jnp.exp(-x))


def _silu_mul(g, u):
    return g * _sigmoid(g) * u


def _logsig(x):
    return jnp.minimum(x, 0.0) - jnp.log(1.0 + jnp.exp(-jnp.abs(x)))


def _colsum(x):
    return jnp.sum(x, axis=0, keepdims=True)


def _acc(ref, val, first):
    @pl.when(first)
    def _():
        ref[...] = val

    @pl.when(jnp.logical_not(first))
    def _():
        ref[...] += val


def _inproj_fwd(x2d, g_pre, w_in_p, tm):
    T, D = x2d.shape
    nchunk = P_COLS // 384
    ns, _, dsh = w_in_p.shape

    def body(x_ref, g_ref, w_ref, h_ref, proj_ref, fl_ref):
        h = _rms(x_ref[...], g_ref[...]).astype(BF16)
        h_ref[...] = h
        for n in range(nchunk):
            r = _dot_nt(h[:, 0:dsh], w_ref[0, n * 384:(n + 1) * 384, :])
            for s in range(1, ns):
                r = r + _dot_nt(h[:, s * dsh:(s + 1) * dsh], w_ref[s, n * 384:(n + 1) * 384, :])
            if n < nchunk - 1:
                proj_ref[:, n * 384:(n + 1) * 384] = r.astype(BF16)
            else:
                fl_ref[...] = r[:, :LANES]
                proj_ref[:, n * 384:n * 384 + M_W] = r[:, LANES:].astype(BF16)

    return pl.pallas_call(
        body, name="inproj_fwd", grid=(T // tm,),
        in_specs=[BS((tm, D), lambda i: (i, 0)), BS((1, D), lambda i: (0, 0)),
                  BS((ns, P_COLS, dsh), lambda i: (0, 0, 0))],
        out_specs=[BS((tm, D), lambda i: (i, 0)), BS((tm, P_MAIN), lambda i: (i, 0)), BS((tm, LANES), lambda i: (i, 0))],
        out_shape=[jax.ShapeDtypeStruct((T, D), BF16), jax.ShapeDtypeStruct((T, P_MAIN), BF16),
                   jax.ShapeDtypeStruct((T, LANES), F32)],
        compiler_params=_cp(("arbitrary",)),
    )(x2d, g_pre, w_in_p)


def _gate_fwd(flog3, bf_row):
    Bl, S, _ = flog3.shape
    nb = S // LANES

    def body(f_ref, b_ref, bq_ref, bk_ref, fr_ref):
        row = _iota((LANES, LANES), 0)
        lane = _iota((LANES, LANES), 1)
        one = jnp.ones((LANES, LANES), BF16)
        zero = jnp.zeros((LANES, LANES), BF16)

        def blk(j, carry):
            r0 = pl.multiple_of(j * LANES, LANES)
            fl = f_ref[0, pl.ds(r0, LANES), :] + b_ref[...]
            fr_ref[0, j] = fl.T[0:8, :]
            c = _logsig(fl)
            for k in (1, 2, 4, 8, 16, 32, 64):
                c = c + jnp.where(row >= k, pltpu.roll(c, k, 0), 0.0)
            c = c + carry
            for h in range(N_FOX_HEADS):
                col = jnp.sum(jnp.where(lane == h, c, 0.0), axis=1, keepdims=True)
                hi = col.astype(BF16)
                rest = col - hi.astype(F32)
                mid = rest.astype(BF16)
                lo = (rest - mid.astype(F32)).astype(BF16)
                base = _bias_lane(h)
                bq = jnp.where(lane == base, hi, jnp.where(lane == base + 1, mid, jnp.where(lane == base + 2, lo, zero)))
                bq = jnp.where((lane >= base + 3) & (lane < base + 6), one, bq)
                bk = jnp.where(lane == base + 3, -hi, jnp.where(lane == base + 4, -mid, jnp.where(lane == base + 5, -lo, zero)))
                bk = jnp.where((lane >= base) & (lane < base + 3), one, bk)
                bq_ref[0, h, pl.ds(r0, LANES), :] = bq
                bk_ref[0, h, pl.ds(r0, LANES), :] = bk
            return _colsum(jnp.where(row == LANES - 1, c, 0.0))

        lax.fori_loop(0, nb, blk, jnp.zeros((1, LANES), F32))

    slab = BS((1, N_FOX_HEADS, S, LANES), lambda b: (b, 0, 0, 0))
    return pl.pallas_call(
        body, name="gate_fwd", grid=(Bl,),
        in_specs=[BS((1, S, LANES), lambda b: (b, 0, 0)), BS((1, LANES), lambda b: (0, 0))],
        out_specs=[slab, slab, BS((1, nb, 8, LANES), lambda b: (b, 0, 0, 0))],
        out_shape=[jax.ShapeDtypeStruct((Bl, N_FOX_HEADS, S, LANES), BF16),
                   jax.ShapeDtypeStruct((Bl, N_FOX_HEADS, S, LANES), BF16),
                   jax.ShapeDtypeStruct((Bl, nb, 8, LANES), F32)],
        compiler_params=_cp(("arbitrary",)),
    )(flog3, bf_row)


def _bias_lane(h):
    return HEAD if h % 2 == 0 else 0


def _sgu_pre(zu, zv, g_sgu):
    return _gelu(zu), _rms(_gelu(zv), g_sgu)


def _sgu_fwd(proj, g_sgu, ws_tril, bs_full, tm):
    T = proj.shape[0]
    nch = tm // CHUNK

    def body(zu_ref, zv_ref, g_ref, ws_ref, b_ref, ya_ref):
        lane = _iota((CHUNK, LANES), 1)
        u, vn = _sgu_pre(zu_ref[...].astype(F32), zv_ref[...].astype(F32), g_ref[...])
        vn = vn.astype(BF16)
        for c in range(nch):
            rs = slice(c * CHUNK, (c + 1) * CHUNK)
            for j in range(3):
                cs = slice(j * LANES, (j + 1) * LANES)
                vp = vn[rs, cs]
                z = jnp.where(lane < HEAD, _dot(ws_ref[2 * j], vp), _dot(ws_ref[2 * j + 1], vp)) + b_ref[:, cs]
                ya_ref[rs, cs] = u[rs, cs] * z

    return pl.pallas_call(
        body, name="sgu_fwd", grid=(T // tm,),
        in_specs=[BS((tm, A_W), lambda i: (i, 0)), BS((tm, A_W), lambda i: (i, 1)), BS((1, A_W), lambda i: (0, 0)),
                  BS((6, CHUNK, CHUNK), lambda i: (0, 0, 0)), BS((CHUNK, A_W), lambda i: (0, 0))],
        out_specs=BS((tm, A_W), lambda i: (i, 0)),
        out_shape=jax.ShapeDtypeStruct((T, A_W), F32),
        compiler_params=_cp(("arbitrary",)),
    )(proj, proj, g_sgu, ws_tril, bs_full)


def _fox_fwd(proj, bq, bk, Bl, S):
    T = Bl * S
    nq = S // Q_BLK
    qc, kc, vc = 768 // LANES, 1152 // LANES, 1536 // LANES

    def body(q_ref, k_ref, v_ref, bq_ref, bk_ref, o_ref, lse_ref, ka_ref, va_ref):
        lane_s = _iota((S, LANES), 1)
        lane = _iota((Q_BLK, LANES), 1)
        tri = _iota((Q_BLK, Q_BLK), 1) <= _iota((Q_BLK, Q_BLK), 0)
        k = k_ref[...]
        v = v_ref[...]
        for hh in range(2):
            data = (lane_s < HEAD) if hh == 0 else (lane_s >= HEAD)
            ka_ref[hh] = jnp.where(data, k, bk_ref[0, hh])
            va_ref[hh] = jnp.where(lane_s == _bias_lane(hh), jnp.ones_like(v), v)
        for i in range(nq):
            r0 = i * Q_BLK
            q = q_ref[r0:r0 + Q_BLK, :]
            o_out = jnp.zeros((Q_BLK, LANES), F32)
            lse_out = jnp.zeros((Q_BLK, LANES), F32)
            for hh in range(2):
                hmask = (lane < HEAD) if hh == 0 else (lane >= HEAD)
                qa = jnp.where(hmask, q * 0.125, bq_ref[0, hh, r0:r0 + Q_BLK, :])
                sd = jnp.where(tri, _dot_nt(qa, ka_ref[hh, r0:r0 + Q_BLK, :]), NEG)
                m = jnp.max(sd, axis=1, keepdims=True)
                if i:
                    sf = _dot_nt(qa, ka_ref[hh, 0:r0, :])
                    m = jnp.maximum(m, jnp.max(sf, axis=1, keepdims=True))
                acc = _dot(jnp.exp(sd - m), va_ref[hh, r0:r0 + Q_BLK, :])
                if i:
                    acc = acc + _dot(jnp.exp(sf - m), va_ref[hh, 0:r0, :])
                l = jnp.sum(jnp.where(lane == _bias_lane(hh), acc, 0.0), axis=1, keepdims=True)
                o_out = jnp.where(hmask, acc / l, o_out)
                lse_out = jnp.where(hmask, m + jnp.log(l), lse_out)
            o_ref[r0:r0 + Q_BLK, :] = o_out
            lse_ref[0, r0:r0 + Q_BLK, :] = lse_out

    seq = lambda c0: BS((S, LANES), lambda b, p: (b, c0 + p))
    pair = BS((1, 2, S, LANES), lambda b, p: (b, p, 0, 0))
    return pl.pallas_call(
        body, name="fox_fwd", grid=(Bl, 3),
        in_specs=[seq(qc), seq(kc), seq(vc), pair, pair],
        out_specs=[seq(0), BS((1, S, LANES), lambda b, p: (p, b, 0))],
        out_shape=[jax.ShapeDtypeStruct((T, B_W), F32), jax.ShapeDtypeStruct((3, T, LANES), F32)],
        scratch_shapes=[pltpu.VMEM((2, S, LANES), BF16), pltpu.VMEM((2, S, LANES), BF16)],
        compiler_params=_cp(("arbitrary", "arbitrary")),
    )(proj, proj, proj, bq, bk)


def _memkv_fwd(mem, g_mem, w_kv):
    Bl, Mt, D = mem.shape

    def body(m_ref, g_ref, w_ref, mn_ref, kv_ref):
        mn = _rms(m_ref[0], g_ref[...]).astype(BF16)
        mn_ref[0] = mn
        kv_ref[0] = jnp.dot(mn, w_ref[...], preferred_element_type=F32).astype(BF16)

    return pl.pallas_call(
        body, name="memkv_fwd", grid=(Bl,),
        in_specs=[BS((1, Mt, D), lambda b: (b, 0, 0)), BS((1, D), lambda b: (0, 0)), BS((D, 2 * M_W), lambda b: (0, 0))],
        out_specs=[BS((1, Mt, D), lambda b: (b, 0, 0)), BS((1, Mt, 2 * M_W), lambda b: (b, 0, 0))],
        out_shape=[jax.ShapeDtypeStruct((Bl, Mt, D), BF16), jax.ShapeDtypeStruct((Bl, Mt, 2 * M_W), BF16)],
        compiler_params=_cp(("arbitrary",)),
    )(mem, g_mem, w_kv)


def _memattn_fwd(proj, kv, Bl, S, tq):
    T = Bl * S
    nq = S // tq
    Mt = kv.shape[1]
    qc = 1920 // LANES

    def body(q_ref, km_ref, vm_ref, o_ref):
        lane = _iota((tq, LANES), 1)
        q = q_ref[...]
        out = jnp.zeros((tq, LANES), F32)
        for hh in range(2):
            hmask = (lane < HEAD) if hh == 0 else (lane >= HEAD)
            qs = jnp.where(hmask, q, jnp.zeros_like(q)) * 0.125
            s = _dot_nt(qs, km_ref[0])
            pe = jnp.exp(s - jnp.max(s, axis=1, keepdims=True))
            pn = pe / jnp.sum(pe, axis=1, keepdims=True)
            out = jnp.where(hmask, _dot(pn, vm_ref[0]), out)
        o_ref[...] = out

    return pl.pallas_call(
        body, name="memattn_fwd", grid=(Bl, 2, nq),
        in_specs=[BS((tq, LANES), lambda b, p, i: (b * nq + i, qc + p)),
                  BS((1, Mt, LANES), lambda b, p, i: (b, 0, p)),
                  BS((1, Mt, LANES), lambda b, p, i: (b, 0, 2 + p))],
        out_specs=BS((tq, LANES), lambda b, p, i: (b * nq + i, p)),
        out_shape=jax.ShapeDtypeStruct((T, M_W), F32),
        compiler_params=_cp(("arbitrary", "arbitrary", "arbitrary")),
    )(proj, kv, kv)


def _mix_norms(ya, yb, ym, ga, gb, gm):
    return _rms(ya, ga), _rms(yb, gb), _rms(ym, gm)


def _outproj_fwd(ya, yb, ym, x2d, ga, gb, gm, g_post, g_pre2, w_out, tm):
    T, D = x2d.shape

    def body(ya_ref, yb_ref, ym_ref, x_ref, ga_ref, gb_ref, gm_ref, gp_ref, g2_ref, w_ref,
             y_ref, o_ref, x1_ref, h2_ref):
        na, nb_, nm = _mix_norms(ya_ref[...], yb_ref[...], ym_ref[...], ga_ref[...], gb_ref[...], gm_ref[...])
        y_ref[:, 0:A_W] = na.astype(BF16)
        y_ref[:, A_W:A_W + B_W] = nb_.astype(BF16)
        y_ref[:, A_W + B_W:] = nm.astype(BF16)
        o = jnp.dot(y_ref[...], w_ref[...], preferred_element_type=F32)
        o_ref[...] = o
        x1 = x_ref[...] + _rms(o, gp_ref[...])
        x1_ref[...] = x1
        h2_ref[...] = _rms(x1, g2_ref[...]).astype(BF16)

    row = lambda w: BS((tm, w), lambda i: (i, 0))
    vec = lambda w: BS((1, w), lambda i: (0, 0))
    return pl.pallas_call(
        body, name="outproj_fwd", grid=(T // tm,),
        in_specs=[row(A_W), row(B_W), row(M_W), row(D), vec(A_W), vec(B_W), vec(M_W), vec(D), vec(D),
                  BS((A_W + B_W + M_W, D), lambda i: (0, 0))],
        out_specs=[row(A_W + B_W + M_W), row(D), row(D), row(D)],
        out_shape=[jax.ShapeDtypeStruct((T, A_W + B_W + M_W), BF16), jax.ShapeDtypeStruct((T, D), F32),
                   jax.ShapeDtypeStruct((T, D), F32), jax.ShapeDtypeStruct((T, D), BF16)],
        compiler_params=_cp(("arbitrary",)),
    )(ya, yb, ym, x2d, ga, gb, gm, g_post, g_pre2, w_out)


def _ffn_fwd(h2, x1, target, wg, wu, wd, g_post, tm):
    T, D = x1.shape
    ns, F, _ = wg.shape

    def body(h_ref, x1_ref, t_ref, wg_ref, wu_ref, wd_ref, gp_ref,
             gs_ref, us_ref, dff_ref, dx2_ref, dgp_ref, loss_ref, acc_ref):
        i = pl.program_id(0)
        j = pl.program_id(1)
        h = h_ref[...]
        g = _dot_nt(h, wg_ref[0])
        u = _dot_nt(h, wu_ref[0])
        gs_ref[0] = g.astype(BF16)
        us_ref[0] = u.astype(BF16)
        part = _dot(_silu_mul(g, u), wd_ref[0])
        _acc(acc_ref, part, j == 0)

        @pl.when(j == ns - 1)
        def _():
            normed, vjp = jax.vjp(_rms, acc_ref[...], gp_ref[...])
            diff = x1_ref[...] + normed - t_ref[...]
            dx2 = diff * (1.0 / D)
            dff, dgp = vjp(dx2)
            dx2_ref[...] = dx2
            dff_ref[...] = dff.astype(BF16)
            lpart = jnp.sum(_colsum(diff * diff), axis=1, keepdims=True) * (0.5 / D)
            _acc(dgp_ref, dgp, i == 0)
            _acc(loss_ref, jnp.broadcast_to(lpart, (1, LANES)), i == 0)

    row = lambda w: BS((tm, w), lambda i, j: (i, 0))
    return pl.pallas_call(
        body, name="ffn_fwd", grid=(T // tm, ns),
        in_specs=[row(D), row(D), row(D), BS((1, F, D), lambda i, j: (j, 0, 0)), BS((1, F, D), lambda i, j: (j, 0, 0)),
                  BS((1, F, D), lambda i, j: (j, 0, 0)), BS((1, D), lambda i, j: (0, 0))],
        out_specs=[BS((1, tm, F), lambda i, j: (j, i, 0)), BS((1, tm, F), lambda i, j: (j, i, 0)), row(D), row(D),
                   BS((1, D), lambda i, j: (0, 0)), BS((1, LANES), lambda i, j: (0, 0))],
        out_shape=[jax.ShapeDtypeStruct((ns, T, F), BF16), jax.ShapeDtypeStruct((ns, T, F), BF16),
                   jax.ShapeDtypeStruct((T, D), BF16), jax.ShapeDtypeStruct((T, D), F32),
                   jax.ShapeDtypeStruct((1, D), F32), jax.ShapeDtypeStruct((1, LANES), F32)],
        scratch_shapes=[pltpu.VMEM((tm, D), F32)],
        compiler_params=_cp(("arbitrary", "arbitrary")),
    )(h2, x1, target, wg, wu, wd, g_post)


def _ffn_bwd(dff, h2, gs, us, wg, wu, wd, tm):
    T, D = h2.shape
    ns, F, _ = wg.shape

    def body(dff_ref, h_ref, gs_ref, us_ref, wg_ref, wu_ref, wd_ref, dh_ref, dwg_ref, dwu_ref, dwd_ref):
        first = pl.program_id(1) == 0
        dff = dff_ref[...]
        h = h_ref[...]
        dact = _dot_nt(dff, wd_ref[0])
        a, vjp = jax.vjp(_silu_mul, gs_ref[0].astype(F32), us_ref[0].astype(F32))
        dg, du = vjp(dact)
        dg = dg.astype(BF16)
        du = du.astype(BF16)
        dh_ref[0] = (_dot(dg, wg_ref[0]) + _dot(du, wu_ref[0])).astype(BF16)
        _acc(dwd_ref, _dot_tn(a, dff)[None], first)
        _acc(dwg_ref, _dot_tn(dg, h)[None], first)
        _acc(dwu_ref, _dot_tn(du, h)[None], first)

    row = BS((tm, D), lambda j, i: (i, 0))
    sh = BS((1, tm, F), lambda j, i: (j, i, 0))
    wsh = BS((1, F, D), lambda j, i: (j, 0, 0))
    return pl.pallas_call(
        body, name="ffn_bwd", grid=(ns, T // tm),
        in_specs=[row, row, sh, sh, wsh, wsh, wsh],
        out_specs=[BS((1, tm, D), lambda j, i: (j, i, 0)), wsh, wsh, wsh],
        out_shape=[jax.ShapeDtypeStruct((ns, T, D), BF16)] + [jax.ShapeDtypeStruct((ns, F, D), F32)] * 3,
        compiler_params=_cp(("arbitrary", "arbitrary")),
    )(dff, h2, gs, us, wg, wu, wd)


def _mm_tn(a, b, name, tk, col_shards=1):
    nbatch = col_shards
    T, M = a.shape
    N = b.shape[1] // col_shards
    tk = min(tk, T)

    def body(a_ref, b_ref, o_ref):
        _acc(o_ref, _dot_tn(a_ref[...], b_ref[...])[None], pl.program_id(1) == 0)

    a_spec = BS((tk, M), lambda s, t: (t, 0))
    b_spec = BS((tk, N), lambda s, t: (t, s))
    return pl.pallas_call(
        body, name=name, grid=(nbatch, T // tk),
        in_specs=[a_spec, b_spec],
        out_specs=BS((1, M, N), lambda s, t: (s, 0, 0)),
        out_shape=jax.ShapeDtypeStruct((nbatch, M, N), F32),
        compiler_params=_cp(("arbitrary", "arbitrary")),
    )(a, b)


def _outproj_bwd(dh2, x1, dx2, o, ya, yb, ym, ga, gb, gm, g_post, g_pre2, w_out, tm):
    T, D = x1.shape
    ns = dh2.shape[0]

    def body(dh_ref, x1_ref, dx2_ref, o_ref, ya_ref, yb_ref, ym_ref, ga_ref, gb_ref, gm_ref, gp_ref, g2_ref, w_ref,
             dx1_ref, do_ref, dya_ref, dyb_ref, dym_ref, dga_ref, dgb_ref, dgm_ref, dgp_ref, dg2_ref):
        first = pl.program_id(0) == 0
        dh = dh_ref[0].astype(F32)
        for j in range(1, ns):
            dh = dh + dh_ref[j].astype(F32)
        _, vjp0 = jax.vjp(_rms, x1_ref[...], g2_ref[...])
        dxa, dg2 = vjp0(dh)
        dx1 = dx2_ref[...] + dxa
        dx1_ref[...] = dx1
        _acc(dg2_ref, dg2, first)
        _, vjp = jax.vjp(_rms, o_ref[...], gp_ref[...])
        do, dgp = vjp(dx1)
        do = do.astype(BF16)
        do_ref[...] = do
        dy = _dot_nt(do, w_ref[...])
        _, vjp2 = jax.vjp(_mix_norms, ya_ref[...], yb_ref[...], ym_ref[...], ga_ref[...], gb_ref[...], gm_ref[...])
        dya, dyb, dym, dga, dgb, dgm = vjp2((dy[:, 0:A_W], dy[:, A_W:A_W + B_W], dy[:, A_W + B_W:]))
        dya_ref[...] = dya
        dyb_ref[...] = dyb
        dym_ref[...] = dym
        _acc(dga_ref, dga, first)
        _acc(dgb_ref, dgb, first)
        _acc(dgm_ref, dgm, first)
        _acc(dgp_ref, dgp, first)

    row = lambda w: BS((tm, w), lambda i: (i, 0))
    vec = lambda w: BS((1, w), lambda i: (0, 0))
    sds = jax.ShapeDtypeStruct
    return pl.pallas_call(
        body, name="outproj_bwd", grid=(T // tm,),
        in_specs=[BS((ns, tm, D), lambda i: (0, i, 0)), row(D), row(D), row(D), row(A_W), row(B_W), row(M_W),
                  vec(A_W), vec(B_W), vec(M_W), vec(D), vec(D), BS((A_W + B_W + M_W, D), lambda i: (0, 0))],
        out_specs=[row(D), row(D), row(A_W), row(B_W), row(M_W), vec(A_W), vec(B_W), vec(M_W), vec(D), vec(D)],
        out_shape=[sds((T, D), F32), sds((T, D), BF16), sds((T, A_W), F32), sds((T, B_W), F32), sds((T, M_W), F32),
                   sds((1, A_W), F32), sds((1, B_W), F32), sds((1, M_W), F32), sds((1, D), F32), sds((1, D), F32)],
        compiler_params=_cp(("arbitrary",)),
    )(dh2, x1, dx2, o, ya, yb, ym, ga, gb, gm, g_post, g_pre2, w_out)


def _sgu_bwd(proj, dya, g_sgu, ws_tril, bs_full, tm):
    T = proj.shape[0]
    nch = tm // CHUNK

    def body(zu_ref, zv_ref, dy_ref, g_ref, ws_ref, b_ref, dzu_ref, dzv_ref, dws_ref, dbs_ref, dg_ref,
             du_ref, dvn_ref, dbf_ref):
        step = pl.program_id(0)
        first = step == 0
        lane = _iota((CHUNK, LANES), 1)
        tril = _iota((CHUNK, CHUNK), 0) >= _iota((CHUNK, CHUNK), 1)
        (u, vn), vjp = jax.vjp(_sgu_pre, zu_ref[...].astype(F32), zv_ref[...].astype(F32), g_ref[...])
        vnb = vn.astype(BF16)
        dy = dy_ref[...]

        @pl.when(first)
        def _():
            dws_ref[...] = jnp.zeros_like(dws_ref)
            dbf_ref[...] = jnp.zeros_like(dbf_ref)

        for c in range(nch):
            rs = slice(c * CHUNK, (c + 1) * CHUNK)
            for j in range(3):
                cs = slice(j * LANES, (j + 1) * LANES)
                vp = vnb[rs, cs]
                z = jnp.where(lane < HEAD, _dot(ws_ref[2 * j], vp), _dot(ws_ref[2 * j + 1], vp)) + b_ref[:, cs]
                du_ref[rs, cs] = dy[rs, cs] * z
                dz = dy[rs, cs] * u[rs, cs]
                dbf_ref[:, cs] += dz
                dzb = dz.astype(BF16)
                dz0 = jnp.where(lane < HEAD, dzb, jnp.zeros_like(dzb))
                dz1 = jnp.where(lane >= HEAD, dzb, jnp.zeros_like(dzb))
                dvn_ref[rs, cs] = jnp.where(lane < HEAD, _dot_tn(ws_ref[2 * j], dzb), _dot_tn(ws_ref[2 * j + 1], dzb))
                dws_ref[2 * j] += jnp.where(tril, _dot_nt(dz0, vp), 0.0)
                dws_ref[2 * j + 1] += jnp.where(tril, _dot_nt(dz1, vp), 0.0)
        dzu, dzv, dg = vjp((du_ref[...], dvn_ref[...]))
        dzu_ref[...] = dzu.astype(BF16)
        dzv_ref[...] = dzv.astype(BF16)
        _acc(dg_ref, dg, first)

        @pl.when(step == pl.num_programs(0) - 1)
        def _():
            out = jnp.zeros((CHUNK, LANES), F32)
            for j in range(3):
                slab = dbf_ref[:, j * LANES:(j + 1) * LANES]
                lo = jnp.sum(jnp.where(lane < HEAD, slab, 0.0), axis=1, keepdims=True)
                hi = jnp.sum(jnp.where(lane >= HEAD, slab, 0.0), axis=1, keepdims=True)
                out = out + jnp.where(lane == 2 * j, lo, 0.0) + jnp.where(lane == 2 * j + 1, hi, 0.0)
            dbs_ref[...] = out

    return pl.pallas_call(
        body, name="sgu_bwd", grid=(T // tm,),
        in_specs=[BS((tm, A_W), lambda i: (i, 0)), BS((tm, A_W), lambda i: (i, 1)), BS((tm, A_W), lambda i: (i, 0)),
                  BS((1, A_W), lambda i: (0, 0)), BS((6, CHUNK, CHUNK), lambda i: (0, 0, 0)),
                  BS((CHUNK, A_W), lambda i: (0, 0))],
        out_specs=[BS((tm, A_W), lambda i: (i, 0)), BS((tm, A_W), lambda i: (i, 0)),
                   BS((6, CHUNK, CHUNK), lambda i: (0, 0, 0)), BS((CHUNK, LANES), lambda i: (0, 0)),
                   BS((1, A_W), lambda i: (0, 0))],
        out_shape=[jax.ShapeDtypeStruct((T, A_W), BF16), jax.ShapeDtypeStruct((T, A_W), BF16),
                   jax.ShapeDtypeStruct((6, CHUNK, CHUNK), F32), jax.ShapeDtypeStruct((CHUNK, LANES), F32),
                   jax.ShapeDtypeStruct((1, A_W), F32)],
        scratch_shapes=[pltpu.VMEM((tm, A_W), F32), pltpu.VMEM((tm, A_W), F32), pltpu.VMEM((CHUNK, A_W), F32)],
        compiler_params=_cp(("arbitrary",)),
    )(proj, proj, dya, g_sgu, ws_tril, bs_full)


def _memattn_bwd(proj, kv, dym, Bl, S, tq):
    T = Bl * S
    nq = S // tq
    Mt = kv.shape[1]
    qc = 1920 // LANES

    def body(q_ref, km_ref, vm_ref, do_ref, dq_ref, dkm_ref, dvm_ref):
        first = pl.program_id(2) == 0
        lane = _iota((tq, LANES), 1)
        q = q_ref[...]
        do = do_ref[...]
        dq_out = jnp.zeros((tq, LANES), F32)
        dkm = jnp.zeros((Mt, LANES), F32)
        dvm = jnp.zeros((Mt, LANES), F32)
        for hh in range(2):
            hmask = (lane < HEAD) if hh == 0 else (lane >= HEAD)
            qs = jnp.where(hmask, q, jnp.zeros_like(q)) * 0.125
            dom = jnp.where(hmask, do, 0.0).astype(BF16)
            s = _dot_nt(qs, km_ref[0])
            pe = jnp.exp(s - jnp.max(s, axis=1, keepdims=True))
            pn = pe / jnp.sum(pe, axis=1, keepdims=True)
            dp = _dot_nt(dom, vm_ref[0])
            ds = (pn * (dp - jnp.sum(pn * dp, axis=1, keepdims=True))).astype(BF16)
            dq_out = jnp.where(hmask, _dot(ds, km_ref[0]) * 0.125, dq_out)
            dkm = dkm + _dot_tn(ds, qs)
            dvm = dvm + _dot_tn(pn, dom)
        dq_ref[...] = dq_out.astype(BF16)
        _acc(dkm_ref, dkm[None], first)
        _acc(dvm_ref, dvm[None], first)

    return pl.pallas_call(
        body, name="memattn_bwd", grid=(Bl, 2, nq),
        in_specs=[BS((tq, LANES), lambda b, p, i: (b * nq + i, qc + p)),
                  BS((1, Mt, LANES), lambda b, p, i: (b, 0, p)),
                  BS((1, Mt, LANES), lambda b, p, i: (b, 0, 2 + p)),
                  BS((tq, LANES), lambda b, p, i: (b * nq + i, p))],
        out_specs=[BS((tq, LANES), lambda b, p, i: (b * nq + i, p)),
                   BS((1, Mt, LANES), lambda b, p, i: (b, 0, p)),
                   BS((1, Mt, LANES), lambda b, p, i: (b, 0, p))],
        out_shape=[jax.ShapeDtypeStruct((T, M_W), BF16), jax.ShapeDtypeStruct((Bl, Mt, M_W), F32),
                   jax.ShapeDtypeStruct((Bl, Mt, M_W), F32)],
        compiler_params=_cp(("arbitrary", "arbitrary", "arbitrary")),
    )(proj, kv, kv, dym)


def _memkv_bwd(dkm, dvm, memn, mem, g_mem, w_kv):
    Bl, Mt, D = mem.shape

    def body(dk_ref, dv_ref, mn_ref, m_ref, g_ref, w_ref, dw_ref, dg_ref):
        first = pl.program_id(0) == 0
        dk = dk_ref[0].astype(BF16)
        dv = dv_ref[0].astype(BF16)
        mn = mn_ref[0]
        dmn = _dot_nt(dk, w_ref[:, 0:M_W]) + _dot_nt(dv, w_ref[:, M_W:])
        _, vjp = jax.vjp(_rms, m_ref[0], g_ref[...])
        _, dg = vjp(dmn)
        _acc(dg_ref, dg, first)

        @pl.when(first)
        def _():
            dw_ref[...] = jnp.zeros_like(dw_ref)

        dw_ref[:, 0:M_W] += _dot_tn(mn, dk)
        dw_ref[:, M_W:] += _dot_tn(mn, dv)

    return pl.pallas_call(
        body, name="memkv_bwd", grid=(Bl,),
        in_specs=[BS((1, Mt, M_W), lambda b: (b, 0, 0)), BS((1, Mt, M_W), lambda b: (b, 0, 0)),
                  BS((1, Mt, D), lambda b: (b, 0, 0)), BS((1, Mt, D), lambda b: (b, 0, 0)),
                  BS((1, D), lambda b: (0, 0)), BS((D, 2 * M_W), lambda b: (0, 0))],
        out_specs=[BS((D, 2 * M_W), lambda b: (0, 0)), BS((1, D), lambda b: (0, 0))],
        out_shape=[jax.ShapeDtypeStruct((D, 2 * M_W), F32), jax.ShapeDtypeStruct((1, D), F32)],
        compiler_params=_cp(("arbitrary",)),
    )(dkm, dvm, memn, mem, g_mem, w_kv)


def _fox_bwd(proj, dyb, lse, bq, bk, Bl, S):
    T = Bl * S
    nq = S // Q_BLK
    nb = S // LANES
    qc, kc, vc = 768 // LANES, 1152 // LANES, 1536 // LANES

    def body(q_ref, k_ref, v_ref, do_ref, lse_ref, bq_ref, bk_ref,
             dq_ref, dk_ref, dv_ref, dcr_ref, ka_ref, dka_ref, dva_ref):
        p = pl.program_id(1)
        lane_s = _iota((S, LANES), 1)
        lane = _iota((Q_BLK, LANES), 1)
        sub = _iota((8, LANES), 0)
        tri = _iota((Q_BLK, Q_BLK), 1) <= _iota((Q_BLK, Q_BLK), 0)
        k = k_ref[...]
        for hh in range(2):
            data = (lane_s < HEAD) if hh == 0 else (lane_s >= HEAD)
            ka_ref[hh] = jnp.where(data, k, bk_ref[0, hh])
        dka_ref[...] = jnp.zeros_like(dka_ref)
        dva_ref[...] = jnp.zeros_like(dva_ref)

        @pl.when(p == 0)
        def _():
            dcr_ref[...] = jnp.zeros_like(dcr_ref)

        def add_colsums(ds, first_blk, h):
            cs = _colsum(ds)
            for jb in range(ds.shape[1] // LANES):
                dcr_ref[0, first_blk + jb] += jnp.where(sub == h, cs[:, jb * LANES:(jb + 1) * LANES], 0.0)

        for i in range(nq):
            r0 = i * Q_BLK
            r1 = r0 + Q_BLK
            q = q_ref[r0:r1, :]
            do = do_ref[r0:r1, :]
            lse_b = lse_ref[0, r0:r1, :]
            dq_out = jnp.zeros((Q_BLK, LANES), F32)
            for hh in range(2):
                hmask = (lane < HEAD) if hh == 0 else (lane >= HEAD)
                h = 2 * p + hh
                qs = jnp.where(hmask, q * 0.125, jnp.zeros_like(q))
                qa = jnp.where(hmask, q * 0.125, bq_ref[0, hh, r0:r1, :])
                dob = jnp.where(hmask, do, 0.0).astype(BF16)
                lse_h = jnp.sum(jnp.where(lane == hh * HEAD, lse_b, 0.0), axis=1, keepdims=True)
                pd = jnp.where(tri, jnp.exp(_dot_nt(qa, ka_ref[hh, r0:r1, :]) - lse_h), 0.0)
                dpd = _dot_nt(dob, v_ref[r0:r1, :])
                delta = jnp.sum(pd * dpd, axis=1, keepdims=True)
                psum = jnp.sum(pd, axis=1, keepdims=True)
                if i:
                    pf = jnp.exp(_dot_nt(qa, ka_ref[hh, 0:r0, :]) - lse_h)
                    dpf = _dot_nt(dob, v_ref[0:r0, :])
                    delta = delta + jnp.sum(pf * dpf, axis=1, keepdims=True)
                    psum = psum + jnp.sum(pf, axis=1, keepdims=True)
                delta = delta / psum
                dsd = pd * (dpd - delta)
                add_colsums(dsd, r0 // LANES, h)
                dsd = dsd.astype(BF16)
                dq_h = _dot(dsd, k_ref[r0:r1, :])
                dka_ref[r0:r1, :] += _dot_tn(dsd, qs)
                dva_ref[r0:r1, :] += _dot_tn(pd, dob)
                if i:
                    dsf = pf * (dpf - delta)
                    add_colsums(dsf, 0, h)
                    dsf = dsf.astype(BF16)
                    dq_h = dq_h + _dot(dsf, k_ref[0:r0, :])
                    dka_ref[0:r0, :] += _dot_tn(dsf, qs)
                    dva_ref[0:r0, :] += _dot_tn(pf, dob)
                dq_out = jnp.where(hmask, dq_h * 0.125, dq_out)
            dq_ref[r0:r1, :] = dq_out.astype(BF16)
        dk_ref[...] = dka_ref[...].astype(BF16)
        dv_ref[...] = dva_ref[...].astype(BF16)

    seq = lambda c0: BS((S, LANES), lambda b, p: (b, c0 + p))
    pair = BS((1, 2, S, LANES), lambda b, p: (b, p, 0, 0))
    rowblk = BS((1, nb, 8, LANES), lambda b, p: (b, 0, 0, 0))
    return pl.pallas_call(
        body, name="fox_bwd", grid=(Bl, 3),
        in_specs=[seq(qc), seq(kc), seq(vc), seq(0), BS((1, S, LANES), lambda b, p: (p, b, 0)), pair, pair],
        out_specs=[seq(0), seq(0), seq(0), rowblk],
        out_shape=[jax.ShapeDtypeStruct((T, B_W), BF16)] * 3 + [jax.ShapeDtypeStruct((Bl, nb, 8, LANES), F32)],
        scratch_shapes=[pltpu.VMEM((2, S, LANES), BF16), pltpu.VMEM((S, LANES), F32), pltpu.VMEM((S, LANES), F32)],
        compiler_params=_cp(("arbitrary", "arbitrary")),
    )(proj, proj, proj, dyb, lse, bq, bk)


def _gate_bwd(dc_row, fl_row):
    Bl, nb, _, _ = dc_row.shape

    def body(dc_ref, fl_ref, o_ref):
        lane = _iota((8, LANES), 1)

        def blk(jj, carry):
            j = nb - 1 - jj
            r = -dc_ref[0, j]
            for k in (1, 2, 4, 8, 16, 32, 64):
                r = r + jnp.where(lane < LANES - k, pltpu.roll(r, LANES - k, 1), 0.0)
            r = r + carry
            dfl = r * _sigmoid(-fl_ref[0, j])
            o_ref[0, pl.ds(pl.multiple_of(j * LANES, LANES), LANES), :] = jnp.concatenate(
                [dfl, jnp.zeros((LANES - 8, LANES), F32)], axis=0).T
            return jnp.sum(jnp.where(lane == 0, r, 0.0), axis=1, keepdims=True)

        lax.fori_loop(0, nb, blk, jnp.zeros((8, 1), F32))

    rowblk = BS((1, nb, 8, LANES), lambda b: (b, 0, 0, 0))
    return pl.pallas_call(
        body, name="gate_bwd", grid=(Bl,),
        in_specs=[rowblk, rowblk],
        out_specs=BS((1, nb * LANES, LANES), lambda b: (b, 0, 0)),
        out_shape=jax.ShapeDtypeStruct((Bl, nb * LANES, LANES), F32),
        compiler_params=_cp(("arbitrary",)),
    )(dc_row, fl_row)


def _inproj_bwd(dzu, dzv, dq, dk, dv, dqm, dfl, x2d, dx1, g_pre, w_in_p, tm):
    T, D = x2d.shape
    ns, _, dsh = w_in_p.shape

    def body(dzu_ref, dzv_ref, dq_ref, dk_ref, dv_ref, dqm_ref, dfl_ref, x_ref, dx1_ref, g_ref, w_ref,
             dp_ref, gx_ref, dg_ref, dbf_ref):
        first = pl.program_id(0) == 0
        dfl = dfl_ref[...]
        dp_ref[:, 0:384] = dzu_ref[...]
        dp_ref[:, 384:768] = dzv_ref[...]
        dp_ref[:, 768:1152] = dq_ref[...]
        dp_ref[:, 1152:1536] = dk_ref[...]
        dp_ref[:, 1536:1920] = dv_ref[...]
        dp_ref[:, 1920:2048] = dfl.astype(BF16)
        dp_ref[:, 2048:2304] = dqm_ref[...]
        dh = jnp.concatenate([_dot(dp_ref[...], w_ref[s]) for s in range(ns)], axis=1)
        _, vjp = jax.vjp(_rms, x_ref[...], g_ref[...])
        dxa, dg = vjp(dh)
        gx_ref[...] = dx1_ref[...] + dxa
        _acc(dg_ref, dg, first)
        _acc(dbf_ref, _colsum(dfl), first)

    row = lambda w: BS((tm, w), lambda i: (i, 0))
    return pl.pallas_call(
        body, name="inproj_bwd", grid=(T // tm,),
        in_specs=[row(A_W), row(A_W), row(B_W), row(B_W), row(B_W), row(M_W), row(LANES), row(D), row(D),
                  BS((1, D), lambda i: (0, 0)), BS((ns, P_COLS, dsh), lambda i: (0, 0, 0))],
        out_specs=[row(P_COLS), row(D), BS((1, D), lambda i: (0, 0)), BS((1, LANES), lambda i: (0, 0))],
        out_shape=[jax.ShapeDtypeStruct((T, P_COLS), BF16), jax.ShapeDtypeStruct((T, D), F32),
                   jax.ShapeDtypeStruct((1, D), F32), jax.ShapeDtypeStruct((1, LANES), F32)],
        compiler_params=_cp(("arbitrary",)),
    )(dzu, dzv, dq, dk, dv, dqm, dfl, x2d, dx1, g_pre, w_in_p)


def _local_step(x, mem, target, W, P):
    Bl, S, D = x.shape
    T = Bl * S
    tm = min(512, T)
    x2d = x.reshape(T, D)
    t2d = target.reshape(T, D)
    vec = lambda a: a.reshape(1, -1)
    bf_row = jnp.pad(P["b_f"].reshape(1, -1), ((0, 0), (0, LANES - N_FOX_HEADS)))
    tril = jnp.tril(jnp.ones((CHUNK, CHUNK), bool))
    ws_tril = jnp.where(tril[None], P["w_s"][0], 0.0).astype(BF16)
    bs_full = jnp.repeat(P["b_s"][0].T, HEAD, axis=1)
    g_pre, g_sgu = vec(P["g_pre_mix"]), vec(P["g_sgu"])
    ga, gb, gm = vec(P["g_out_a"]), vec(P["g_out_b"]), vec(P["g_out_m"])
    g_mem, g_post, g_pre2, g_post2 = vec(P["g_mem"]), vec(P["g_post_mix"]), vec(P["g_pre_ffn"]), vec(P["g_post_ffn"])

    h, proj, flog = _inproj_fwd(x2d, g_pre, W["w_in"], tm)
    bq, bk, fl_row = _gate_fwd(flog.reshape(Bl, S, LANES), bf_row)
    ya = _sgu_fwd(proj, g_sgu, ws_tril, bs_full, tm)
    yb, lse = _fox_fwd(proj, bq, bk, Bl, S)
    memn, kv = _memkv_fwd(mem, g_mem, W["w_mem_kv"])
    ym = _memattn_fwd(proj, kv, Bl, S, min(512, S))
    y, o, x1, h2 = _outproj_fwd(ya, yb, ym, x2d, ga, gb, gm, g_post, g_pre2, W["w_out"], tm)
    gs, us, dff, dx2, dg_post2, loss = _ffn_fwd(h2, x1, t2d, W["w_gate"], W["w_up"], W["w_down"], g_post2, tm)

    dh2, d_w_gate, d_w_up, d_w_down = _ffn_bwd(dff, h2, gs, us, W["w_gate"], W["w_up"], W["w_down"], tm)
    dx1, do, dya, dyb, dym, dga, dgb, dgm, dg_post, dg_pre2 = _outproj_bwd(
        dh2, x1, dx2, o, ya, yb, ym, ga, gb, gm, g_post, g_pre2, W["w_out"], tm)
    d_w_out = _mm_tn(y, do, "dw_out", 1024)[0]
    dzu, dzv, dws, dbs_cols, dg_sgu = _sgu_bwd(proj, dya, g_sgu, ws_tril, bs_full, tm)
    dqm, dkm, dvm = _memattn_bwd(proj, kv, dym, Bl, S, min(512, S))
    d_w_kv, dg_mem = _memkv_bwd(dkm, dvm, memn, mem, g_mem, W["w_mem_kv"])
    dq, dk, dv, dc_row = _fox_bwd(proj, dyb, lse, bq, bk, Bl, S)
    dfl = _gate_bwd(dc_row, fl_row).reshape(T, LANES)
    dproj, grad_x, dg_pre, dbf = _inproj_bwd(dzu, dzv, dq, dk, dv, dqm, dfl, x2d, dx1, g_pre, W["w_in"], tm)
    d_w_in = _mm_tn(dproj, h, "dw_in", 1024, col_shards=W["w_in"].shape[0])

    big = {"w_in": d_w_in, "w_mem_kv": d_w_kv, "w_out": d_w_out, "w_gate": d_w_gate, "w_up": d_w_up,
           "w_down": d_w_down}
    small = {"g_pre_mix": dg_pre, "b_f": dbf[:, :N_FOX_HEADS], "g_sgu": dg_sgu, "w_s": dws, "b_s": dbs_cols[:, :N_FOX_HEADS].T,
             "g_out_a": dga, "g_out_b": dgb, "g_out_m": dgm, "g_mem": dg_mem, "g_post_mix": dg_post,
             "g_pre_ffn": dg_pre2, "g_post_ffn": dg_post2, "loss": loss[:, :1]}
    return grad_x.reshape(Bl, S, D), big, small


def _place():
    return lax.axis_index("x"), lax.axis_index("y"), lax.axis_index("c")


def _exchange_on_sequencer(srcs, own_full, name, collective_id):
    n = len(srcs)

    def body(*refs):
        src, dst = refs[:n], refs[n:2 * n]
        lsem, isend, irecv, dsend, drecv = refs[2 * n:]
        x, y, c = _place()
        oc = 1 - c
        s_me = 2 * x + y
        sib = (x, y, oc)
        chips = [(1 - x, y), (x, 1 - y), (1 - x, 1 - y)]
        barrier = pltpu.get_barrier_semaphore()
        for dev in [(cx, cy, c) for cx, cy in chips] + [sib]:
            pl.semaphore_signal(barrier, inc=1, device_id=dev, device_id_type=MESH)
        pl.semaphore_wait(barrier, 4)

        def remote(a, b, ssem, rsem, dev):
            return pltpu.make_async_remote_copy(src_ref=a, dst_ref=b, send_sem=ssem, recv_sem=rsem,
                                                device_id=dev, device_id_type=MESH)

        sends, local = [], []
        for w in range(n):
            for j, (cx, cy) in enumerate(chips):
                half = src[w].at[c] if own_full else src[w].at[2 * cx + cy]
                cp = remote(half, dst[w].at[s_me, c], isend.at[w, j], irecv.at[w, j], (cx, cy, c))
                cp.start()
                sends.append(cp)
            if own_full:
                cp = remote(src[w], dst[w].at[s_me], dsend.at[w, 3], drecv.at[w, 3], sib)
            else:
                cp = remote(src[w].at[s_me], dst[w].at[s_me, c], dsend.at[w, 3], drecv.at[w, 3], sib)
                loc = pltpu.make_async_copy(src[w].at[s_me], dst[w].at[s_me, c], lsem.at[w])
                loc.start()
                local.append(loc)
            cp.start()
            sends.append(cp)
        for w in range(n):
            for j, (cx, cy) in enumerate(chips):
                landed = dst[w].at[2 * cx + cy, c]
                remote(landed, landed, isend.at[w, j], irecv.at[w, j], (cx, cy, c)).wait_recv()
                cp = remote(landed, landed, dsend.at[w, j], drecv.at[w, j], sib)
                cp.start()
                sends.append(cp)
        for w in range(n):
            for j, (cx, cy) in enumerate(chips):
                landed = dst[w].at[2 * cx + cy, oc]
                remote(landed, landed, dsend.at[w, j], drecv.at[w, j], sib).wait_recv()
            landed = dst[w].at[s_me] if own_full else dst[w].at[s_me, oc]
            remote(landed, landed, dsend.at[w, 3], drecv.at[w, 3], sib).wait_recv()
        for cp in sends:
            cp.wait_send()
        for loc in local:
            loc.wait()

    return pl.kernel(
        body, out_type=[jax.ShapeDtypeStruct((4, 2) + s.shape[1:], s.dtype) for s in srcs],
        mesh=plsc.ScalarSubcoreMesh(axis_name="sequencer", num_cores=1), name=name,
        scratch_types=[pltpu.SemaphoreType.DMA((n,)), pltpu.SemaphoreType.DMA((n, 3)), pltpu.SemaphoreType.DMA((n, 3)),
                       pltpu.SemaphoreType.DMA((n, 4)), pltpu.SemaphoreType.DMA((n, 4))],
        compiler_params=pltpu.CompilerParams(collective_id=collective_id),
    )(*srcs)


def _sibling_swap(grads, name, collective_id):
    n = len(grads)

    def body(*refs):
        g, theirs = refs[:n], refs[n:2 * n]
        ssem, rsem = refs[2 * n:]
        x, y, c = _place()
        sib = (x, y, 1 - c)
        barrier = pltpu.get_barrier_semaphore()
        pl.semaphore_signal(barrier, inc=1, device_id=sib, device_id_type=MESH)
        pl.semaphore_wait(barrier, 1)
        cps = []
        for w in range(n):
            cp = pltpu.make_async_remote_copy(src_ref=g[w].at[:, 1 - c], dst_ref=theirs[w], send_sem=ssem.at[w],
                                              recv_sem=rsem.at[w], device_id=sib, device_id_type=MESH)
            cp.start()
            cps.append(cp)
        for cp in cps:
            cp.wait()

    return pl.kernel(
        body, out_type=[jax.ShapeDtypeStruct((4,) + g.shape[2:], g.dtype) for g in grads],
        mesh=plsc.ScalarSubcoreMesh(axis_name="sequencer", num_cores=1), name=name,
        scratch_types=[pltpu.SemaphoreType.DMA((n,)), pltpu.SemaphoreType.DMA((n,))],
        compiler_params=pltpu.CompilerParams(collective_id=collective_id),
    )(*grads)


def _add_pair(core, g, theirs, name):
    _, _, hr, C = g.shape

    def body(core_ref, g_ref, t_ref, o_ref):
        o_ref[0] = (g_ref[0, 0] + t_ref[0]).astype(BF16)

    blk = BS((1, hr, C), lambda s, core_ref: (s, 0, 0))
    return pl.pallas_call(
        body, name=name,
        grid_spec=pltpu.PrefetchScalarGridSpec(
            num_scalar_prefetch=1, grid=(4,),
            in_specs=[BS((1, 1, hr, C), lambda s, core_ref: (s, core_ref[0], 0, 0)), blk], out_specs=blk),
        out_shape=jax.ShapeDtypeStruct(theirs.shape, BF16), compiler_params=_cp(("arbitrary",)))(core, g, theirs)


def _sum_chips(r, name):
    _, _, hr, C = r.shape

    def body(r_ref, o_ref):
        o_ref[...] = ((r_ref[0, 0].astype(F32) + r_ref[1, 0].astype(F32)) + r_ref[2, 0].astype(F32)) + r_ref[3, 0].astype(F32)

    return pl.pallas_call(body, name=name, grid=(2,), in_specs=[BS((4, 1, hr, C), lambda h: (0, h, 0, 0))],
                          out_specs=BS((hr, C), lambda h: (h, 0)), out_shape=jax.ShapeDtypeStruct((2 * hr, C), F32),
                          compiler_params=_cp(("arbitrary",)))(r)


def _small_allreduce(part):
    R = part.shape[0]
    rs = R // 8
    masks = [(mx, my, mc) for mx in (0, 1) for my in (0, 1) for mc in (0, 1)][1:]

    def body(p_ref, o_ref, buf_ref, s1, r1, s2, r2):
        x, y, c = _place()
        d = 4 * x + 2 * y + c
        mine = pl.ds(pl.multiple_of(d * rs, 8), rs)
        peers = [((x + mx) % 2, (y + my) % 2, (c + mc) % 2) for mx, my, mc in masks]
        first, second = [], []
        for k, (px, py, pc) in enumerate(peers):
            theirs = pl.ds(pl.multiple_of((4 * px + 2 * py + pc) * rs, 8), rs)
            cp = pltpu.make_async_remote_copy(src_ref=p_ref.at[theirs, :], dst_ref=buf_ref.at[d], send_sem=s1.at[k],
                                              recv_sem=r1.at[k], device_id=(px, py, pc), device_id_type=MESH)
            cp.start()
            first.append(cp)
        buf_ref[d] = p_ref[mine, :]
        for k, (px, py, pc) in enumerate(peers):
            slot = buf_ref.at[4 * px + 2 * py + pc]
            pltpu.make_async_remote_copy(src_ref=slot, dst_ref=slot, send_sem=s1.at[k], recv_sem=r1.at[k],
                                         device_id=(px, py, pc), device_id_type=MESH).wait_recv()
        total = buf_ref[0]
        for k in range(1, 8):
            total = total + buf_ref[k]
        o_ref[mine, :] = total
        for k, (px, py, pc) in enumerate(peers):
            cp = pltpu.make_async_remote_copy(src_ref=o_ref.at[mine, :], dst_ref=o_ref.at[mine, :], send_sem=s2.at[k],
                                              recv_sem=r2.at[k], device_id=(px, py, pc), device_id_type=MESH)
            cp.start()
            second.append(cp)
        for k, (px, py, pc) in enumerate(peers):
            rows = o_ref.at[pl.ds(pl.multiple_of((4 * px + 2 * py + pc) * rs, 8), rs), :]
            pltpu.make_async_remote_copy(src_ref=rows, dst_ref=rows, send_sem=s2.at[k], recv_sem=r2.at[k],
                                         device_id=(px, py, pc), device_id_type=MESH).wait_recv()
        for cp in first + second:
            cp.wait_send()

    vm = pl.BlockSpec(memory_space=pltpu.VMEM)
    return pl.pallas_call(
        body, name="small_allreduce", in_specs=[vm], out_specs=vm, out_shape=jax.ShapeDtypeStruct(part.shape, F32),
        scratch_shapes=[pltpu.VMEM((8, rs, LANES), F32)] + [pltpu.SemaphoreType.DMA((7,))] * 4,
    )(part)


def _adamw(w, g, m, v, name):
    R, C = w.shape
    tr = R if R * C * 4 <= (1 << 21) else R // 2
    if tr % 8:
        tr = R
    c1 = 1.0 / (1.0 - ADAM_B1 ** ADAM_STEP)
    c2 = 1.0 / (1.0 - ADAM_B2 ** ADAM_STEP)

    def body(w_ref, g_ref, m_ref, v_ref, d_ref, mo_ref, vo_ref):
        g_ = g_ref[...]
        m_ = ADAM_B1 * m_ref[...] + (1.0 - ADAM_B1) * g_
        v_ = ADAM_B2 * v_ref[...] + (1.0 - ADAM_B2) * (g_ * g_)
        mo_ref[...] = m_
        vo_ref[...] = v_
        d_ref[...] = -ADAM_LR * ((m_ * c1) / (jnp.sqrt(v_ * c2) + ADAM_EPS) + ADAM_WD * w_ref[...])

    blk = BS((tr, C), lambda i: (i, 0))
    return pl.pallas_call(body, name=name, grid=(R // tr,), in_specs=[blk] * 4, out_specs=[blk] * 3,
                          out_shape=[jax.ShapeDtypeStruct((R, C), F32)] * 3, compiler_params=_cp(("arbitrary",)))(w, g, m, v)


SMALL = ("g_pre_mix", "b_f", "g_sgu", "w_s", "b_s", "g_out_a", "g_out_b", "g_out_m", "g_mem", "g_post_mix",
         "g_pre_ffn", "g_post_ffn")
BIG = ("w_in", "w_mem_kv", "w_out", "w_gate", "w_up", "w_down")
TRANSPOSED = ("w_in", "w_gate", "w_up")
WEIGHTS = ("g_pre_mix", "w_in", "b_f", "g_sgu", "w_s", "b_s", "g_out_a", "g_out_b", "g_out_m", "g_mem", "w_mem_kv",
           "w_out", "g_post_mix", "g_pre_ffn", "w_gate", "w_up", "w_down", "g_post_ffn")


def _rows_of(n):
    return -(-n // (8 * LANES)) * 8


def _pack(parts):
    tiles = []
    for a in parts:
        flat = a.reshape(-1).astype(F32)
        rows = _rows_of(flat.shape[0])
        tiles.append(jnp.pad(flat, (0, rows * LANES - flat.shape[0])).reshape(rows, LANES))
    total = sum(t.shape[0] for t in tiles)
    pad = -total % 64
    if pad:
        tiles.append(jnp.zeros((pad, LANES), F32))
    return jnp.concatenate(tiles, axis=0)


def _unpack(packed, shapes):
    out, r = [], 0
    for shp in shapes:
        n = 1
        for s in shp:
            n *= s
        rows = _rows_of(n)
        out.append(packed[r:r + rows].reshape(-1)[:n].reshape(shp))
        r += rows
    return out


def kernel(x, mem, g_pre_mix, w_in, b_f, g_sgu, w_s, b_s, g_out_a, g_out_b, g_out_m, g_mem, w_mem_kv, w_out, g_post_mix, g_pre_ffn, w_gate, w_up, w_down, g_post_ffn, loss_target, m_g_pre_mix, m_w_in, m_b_f, m_g_sgu, m_w_s, m_b_s, m_g_out_a, m_g_out_b, m_g_out_m, m_g_mem, m_w_mem_kv, m_w_out, m_g_post_mix, m_g_pre_ffn, m_w_gate, m_w_up, m_w_down, m_g_post_ffn, v_g_pre_mix, v_w_in, v_b_f, v_g_sgu, v_w_s, v_b_s, v_g_out_a, v_g_out_b, v_g_out_m, v_g_mem, v_w_mem_kv, v_w_out, v_g_post_mix, v_g_pre_ffn, v_w_gate, v_w_up, v_w_down, v_g_post_ffn):
    Wt = dict(g_pre_mix=g_pre_mix, w_in=w_in, b_f=b_f, g_sgu=g_sgu, w_s=w_s, b_s=b_s, g_out_a=g_out_a, g_out_b=g_out_b,
              g_out_m=g_out_m, g_mem=g_mem, w_mem_kv=w_mem_kv, w_out=w_out, g_post_mix=g_post_mix, g_pre_ffn=g_pre_ffn,
              w_gate=w_gate, w_up=w_up, w_down=w_down, g_post_ffn=g_post_ffn)
    Mo = dict(g_pre_mix=m_g_pre_mix, w_in=m_w_in, b_f=m_b_f, g_sgu=m_g_sgu, w_s=m_w_s, b_s=m_b_s, g_out_a=m_g_out_a,
              g_out_b=m_g_out_b, g_out_m=m_g_out_m, g_mem=m_g_mem, w_mem_kv=m_w_mem_kv, w_out=m_w_out,
              g_post_mix=m_g_post_mix, g_pre_ffn=m_g_pre_ffn, w_gate=m_w_gate, w_up=m_w_up, w_down=m_w_down,
              g_post_ffn=m_g_post_ffn)
    Vo = dict(g_pre_mix=v_g_pre_mix, w_in=v_w_in, b_f=v_b_f, g_sgu=v_g_sgu, w_s=v_w_s, b_s=v_b_s, g_out_a=v_g_out_a,
              g_out_b=v_g_out_b, g_out_m=v_g_out_m, g_mem=v_g_mem, w_mem_kv=v_w_mem_kv, w_out=v_w_out,
              g_post_mix=v_g_post_mix, g_pre_ffn=v_g_pre_ffn, w_gate=v_w_gate, w_up=v_w_up, w_down=v_w_down,
              g_post_ffn=v_g_post_ffn)

    gap = P_COLS - IN_COLS

    def to_kernel(n, w):
        if n in TRANSPOSED:
            w = w.T
        if n == "w_in":
            w = jnp.pad(w[:F_END], ((0, P_COLS - F_END), (0, 0))) + jnp.pad(w[F_END:], ((F_END + gap, 0), (0, 0)))
        return w

    def ungroup(g):
        return jnp.pad(g[:F_END], ((0, IN_COLS - F_END), (0, 0))) + jnp.pad(g[F_END + gap:], ((F_END, 0), (0, 0)))

    shards = {n: to_kernel(n, Wt[n][0]) for n in BIG}
    srcs = [shards[n].astype(BF16).reshape(2, shards[n].shape[0] // 2, shards[n].shape[1]) for n in BIG]
    fulls = (_exchange_on_sequencer(srcs[:1], True, "gather_w_in", 1)
             + _exchange_on_sequencer(srcs[1:3], True, "gather_kv_out", 2)
             + _exchange_on_sequencer(srcs[3:], True, "gather_ffn", 3))
    W = {}
    for n, f in zip(BIG, fulls):
        _, _, hr, C = f.shape
        W[n] = f.reshape(8 * hr, C) if n in ("w_mem_kv", "w_out") else f.reshape(4, 2 * hr, C)

    P = {n: Wt[n] for n in SMALL}
    grad_x, big, small = _local_step(x, mem, loss_target, W, P)

    g4 = []
    for n in BIG:
        g = big[n]
        C = g.shape[-1]
        g4.append(g.reshape(4, 2, -1, C))
    core = lax.axis_index("c").astype(jnp.int32).reshape(1)
    landed = []
    for tag, ids, lo, hi in (("ffn", (4, 5), 3, 6), ("mix", (6, 7), 0, 3)):
        theirs = _sibling_swap(g4[lo:hi], "swap_" + tag, ids[0])
        sums = [_add_pair(core, g, t, "chip_sum_" + n) for n, g, t in zip(BIG[lo:hi], g4[lo:hi], theirs)]
        landed.append((lo, _exchange_on_sequencer(sums, False, "scatter_" + tag, ids[1])))
    landed = [r for _, group in sorted(landed) for r in group]
    grads, deltas, new_m, new_v = {}, {}, {}, {}
    for n, r in zip(BIG, landed):
        g = _sum_chips(r, "sum_chips_" + n)
        if n == "w_in":
            g = ungroup(g)
        if n in TRANSPOSED:
            d, m1, v1 = _adamw(Wt[n][0].T, g, Mo[n][0].T, Vo[n][0].T, "adamw_" + n)
            g, d, m1, v1 = g.T, d.T, m1.T, v1.T
        else:
            d, m1, v1 = _adamw(Wt[n][0], g, Mo[n][0], Vo[n][0], "adamw_" + n)
        grads[n], deltas[n], new_m[n], new_v[n] = g[None], d[None], m1[None], v1[None]

    total = _small_allreduce(_pack([small[n] for n in SMALL] + [small["loss"]]))
    slot = [jnp.zeros((1, 1), F32)]
    shapes = [Wt[n].shape for n in SMALL] + [(1, 1)]
    d, m1, v1 = _adamw(_pack([Wt[n] for n in SMALL] + slot), total, _pack([Mo[n] for n in SMALL] + slot),
                       _pack([Vo[n] for n in SMALL] + slot), "adamw_small")
    g_s, d_s, m_s, v_s = _unpack(total, shapes), _unpack(d, shapes), _unpack(m1, shapes), _unpack(v1, shapes)
    for k, n in enumerate(SMALL):
        grads[n], deltas[n], new_m[n], new_v[n] = g_s[k], d_s[k], m_s[k], v_s[k]
    loss = g_s[-1][0, 0]

    return (loss, grad_x, *[grads[n] for n in WEIGHTS], *[deltas[n] for n in WEIGHTS],
            *[new_m[n] for n in WEIGHTS], *[new_v[n] for n in WEIGHTS])
```

```python
import functools

import jax
import jax.numpy as jnp
from jax import lax
from jax.experimental import pallas as pl
from jax.experimental.pallas import tpu as pltpu
from jax.experimental.pallas import tpu_sc as plsc

F32 = jnp.float32
BF16 = jnp.bfloat16
EPS = 1e-6
NEG = -1e30
HEAD = 64
A_W, B_W, M_W = 384, 384, 256
N_FOX_HEADS = 6
CHUNK = 128
IN_COLS = 2 * A_W + 3 * B_W + N_FOX_HEADS + M_W
P_MAIN = 2 * A_W + 3 * B_W + M_W
P_COLS = P_MAIN + 128
F_END = 2 * A_W + 3 * B_W + N_FOX_HEADS
LANES = 128
Q_BLK, K_BLK = 256, 128
ADAM_LR, ADAM_B1, ADAM_B2, ADAM_EPS, ADAM_WD, ADAM_STEP = 0.001, 0.9, 0.999, 1e-08, 0.01, 10
VMEM_LIMIT = 56 * 1024 * 1024
MESH = pl.DeviceIdType.MESH
ANY = pl.BlockSpec(memory_space=pl.ANY)
BS = pl.BlockSpec


def _cp(sem=None):
    return pltpu.CompilerParams(dimension_semantics=sem, vmem_limit_bytes=VMEM_LIMIT)


def _iota(shape, dim):
    return lax.broadcasted_iota(jnp.int32, shape, dim)


def _dot(a, b):
    return jnp.dot(a.astype(BF16), b.astype(BF16), preferred_element_type=F32)


def _dot_nt(a, b):
    return lax.dot_general(a.astype(BF16), b.astype(BF16), (((1,), (1,)), ((), ())), preferred_element_type=F32)


def _dot_tn(a, b):
    return lax.dot_general(a.astype(BF16), b.astype(BF16), (((0,), (0,)), ((), ())), preferred_element_type=F32)


def _rms(x, g):
    return x * lax.rsqrt(jnp.mean(x * x, axis=-1, keepdims=True) + EPS) * g


def _gelu(x):
    return 0.5 * x * (1.0 + jnp.tanh(0.7978845608028654 * (x + 0.044715 * (x * x * x))))


def _sigmoid(x):
    return 1.0 / (1.0 + jnp.exp(-x))


def _silu_mul(g, u):
    return g * _sigmoid(g) * u


def _logsig(x):
    return jnp.minimum(x, 0.0) - jnp.log(1.0 + jnp.exp(-jnp.abs(x)))


def _colsum(x):
    return jnp.sum(x, axis=0, keepdims=True)


def _acc(ref, val, first):
    @pl.when(first)
    def _():
        ref[...] = val

    @pl.when(jnp.logical_not(first))
    def _():
        ref[...] += val


def _inproj_fwd(x2d, g_pre, w_in_p, tm):
    T, D = x2d.shape
    nchunk = P_COLS // 384
    ns, _, dsh = w_in_p.shape

    def body(x_ref, g_ref, w_ref, h_ref, proj_ref, fl_ref):
        h = _rms(x_ref[...], g_ref[...]).astype(BF16)
        h_ref[...] = h
        for n in range(nchunk):
            r = _dot_nt(h[:, 0:dsh], w_ref[0, n * 384:(n + 1) * 384, :])
            for s in range(1, ns):
                r = r + _dot_nt(h[:, s * dsh:(s + 1) * dsh], w_ref[s, n * 384:(n + 1) * 384, :])
            if n < nchunk - 1:
                proj_ref[:, n * 384:(n + 1) * 384] = r.astype(BF16)
            else:
                fl_ref[...] = r[:, :LANES]
                proj_ref[:, n * 384:n * 384 + M_W] = r[:, LANES:].astype(BF16)

    return pl.pallas_call(
        body, name="inproj_fwd", grid=(T // tm,),
        in_specs=[BS((tm, D), lambda i: (i, 0)), BS((1, D), lambda i: (0, 0)),
                  BS((ns, P_COLS, dsh), lambda i: (0, 0, 0))],
        out_specs=[BS((tm, D), lambda i: (i, 0)), BS((tm, P_MAIN), lambda i: (i, 0)), BS((tm, LANES), lambda i: (i, 0))],
        out_shape=[jax.ShapeDtypeStruct((T, D), BF16), jax.ShapeDtypeStruct((T, P_MAIN), BF16),
                   jax.ShapeDtypeStruct((T, LANES), F32)],
        compiler_params=_cp(("arbitrary",)),
    )(x2d, g_pre, w_in_p)


def _gate_fwd(flog3, bf_row):
    Bl, S, _ = flog3.shape
    nb = S // LANES

    def body(f_ref, b_ref, bq_ref, bk_ref, fr_ref):
        row = _iota((LANES, LANES), 0)
        lane = _iota((LANES, LANES), 1)
        one = jnp.ones((LANES, LANES), BF16)
        zero = jnp.zeros((LANES, LANES), BF16)

        def blk(j, carry):
            r0 = pl.multiple_of(j * LANES, LANES)
            fl = f_ref[0, pl.ds(r0, LANES), :] + b_ref[...]
            fr_ref[0, j] = fl.T[0:8, :]
            c = _logsig(fl)
            for k in (1, 2, 4, 8, 16, 32, 64):
                c = c + jnp.where(row >= k, pltpu.roll(c, k, 0), 0.0)
            c = c + carry
            for h in range(N_FOX_HEADS):
                col = jnp.sum(jnp.where(lane == h, c, 0.0), axis=1, keepdims=True)
                hi = col.astype(BF16)
                rest = col - hi.astype(F32)
                mid = rest.astype(BF16)
                lo = (rest - mid.astype(F32)).astype(BF16)
                base = _bias_lane(h)
                bq = jnp.where(lane == base, hi, jnp.where(lane == base + 1, mid, jnp.where(lane == base + 2, lo, zero)))
                bq = jnp.where((lane >= base + 3) & (lane < base + 6), one, bq)
                bk = jnp.where(lane == base + 3, -hi, jnp.where(lane == base + 4, -mid, jnp.where(lane == base + 5, -lo, zero)))
                bk = jnp.where((lane >= base) & (lane < base + 3), one, bk)
                bq_ref[0, h, pl.ds(r0, LANES), :] = bq
                bk_ref[0, h, pl.ds(r0, LANES), :] = bk
            return _colsum(jnp.where(row == LANES - 1, c, 0.0))

        lax.fori_loop(0, nb, blk, jnp.zeros((1, LANES), F32))

    slab = BS((1, N_FOX_HEADS, S, LANES), lambda b: (b, 0, 0, 0))
    return pl.pallas_call(
        body, name="gate_fwd", grid=(Bl,),
        in_specs=[BS((1, S, LANES), lambda b: (b, 0, 0)), BS((1, LANES), lambda b: (0, 0))],
        out_specs=[slab, slab, BS((1, nb, 8, LANES), lambda b: (b, 0, 0, 0))],
        out_shape=[jax.ShapeDtypeStruct((Bl, N_FOX_HEADS, S, LANES), BF16),
                   jax.ShapeDtypeStruct((Bl, N_FOX_HEADS, S, LANES), BF16),
                   jax.ShapeDtypeStruct((Bl, nb, 8, LANES), F32)],
        compiler_params=_cp(("arbitrary",)),
    )(flog3, bf_row)


def _bias_lane(h):
    return HEAD if h % 2 == 0 else 0


def _sgu_pre(zu, zv, g_sgu):
    return _gelu(zu), _rms(_gelu(zv), g_sgu)


def _sgu_fwd(proj, g_sgu, ws_tril, bs_full, tm):
    T = proj.shape[0]
    nch = tm // CHUNK

    def body(zu_ref, zv_ref, g_ref, ws_ref, b_ref, ya_ref):
        lane = _iota((CHUNK, LANES), 1)
        u, vn = _sgu_pre(zu_ref[...].astype(F32), zv_ref[...].astype(F32), g_ref[...])
        vn = vn.astype(BF16)
        for c in range(nch):
            rs = slice(c * CHUNK, (c + 1) * CHUNK)
            for j in range(3):
                cs = slice(j * LANES, (j + 1) * LANES)
                vp = vn[rs, cs]
                z = jnp.where(lane < HEAD, _dot(ws_ref[2 * j], vp), _dot(ws_ref[2 * j + 1], vp)) + b_ref[:, cs]
                ya_ref[rs, cs] = u[rs, cs] * z

    return pl.pallas_call(
        body, name="sgu_fwd", grid=(T // tm,),
        in_specs=[BS((tm, A_W), lambda i: (i, 0)), BS((tm, A_W), lambda i: (i, 1)), BS((1, A_W), lambda i: (0, 0)),
                  BS((6, CHUNK, CHUNK), lambda i: (0, 0, 0)), BS((CHUNK, A_W), lambda i: (0, 0))],
        out_specs=BS((tm, A_W), lambda i: (i, 0)),
        out_shape=jax.ShapeDtypeStruct((T, A_W), F32),
        compiler_params=_cp(("arbitrary",)),
    )(proj, proj, g_sgu, ws_tril, bs_full)


def _fox_fwd(proj, bq, bk, Bl, S):
    T = Bl * S
    nq = S // Q_BLK
    qc, kc, vc = 768 // LANES, 1152 // LANES, 1536 // LANES

    def body(q_ref, k_ref, v_ref, bq_ref, bk_ref, o_ref, lse_ref, ka_ref, va_ref):
        lane_s = _iota((S, LANES), 1)
        lane = _iota((Q_BLK, LANES), 1)
        tri = _iota((Q_BLK, Q_BLK), 1) <= _iota((Q_BLK, Q_BLK), 0)
        k = k_ref[...]
        v = v_ref[...]
        for hh in range(2):
            data = (lane_s < HEAD) if hh == 0 else (lane_s >= HEAD)
            ka_ref[hh] = jnp.where(data, k, bk_ref[0, hh])
            va_ref[hh] = jnp.where(lane_s == _bias_lane(hh), jnp.ones_like(v), v)
        for i in range(nq):
            r0 = i * Q_BLK
            q = q_ref[r0:r0 + Q_BLK, :]
            o_out = jnp.zeros((Q_BLK, LANES), F32)
            lse_out = jnp.zeros((Q_BLK, LANES), F32)
            for hh in range(2):
                hmask = (lane < HEAD) if hh == 0 else (lane >= HEAD)
                qa = jnp.where(hmask, q * 0.125, bq_ref[0, hh, r0:r0 + Q_BLK, :])
                sd = jnp.where(tri, _dot_nt(qa, ka_ref[hh, r0:r0 + Q_BLK, :]), NEG)
                m = jnp.max(sd, axis=1, keepdims=True)
                if i:
                    sf = _dot_nt(qa, ka_ref[hh, 0:r0, :])
                    m = jnp.maximum(m, jnp.max(sf, axis=1, keepdims=True))
                acc = _dot(jnp.exp(sd - m), va_ref[hh, r0:r0 + Q_BLK, :])
                if i:
                    acc = acc + _dot(jnp.exp(sf - m), va_ref[hh, 0:r0, :])
                l = jnp.sum(jnp.where(lane == _bias_lane(hh), acc, 0.0), axis=1, keepdims=True)
                o_out = jnp.where(hmask, acc / l, o_out)
                lse_out = jnp.where(hmask, m + jnp.log(l), lse_out)
            o_ref[r0:r0 + Q_BLK, :] = o_out
            lse_ref[0, r0:r0 + Q_BLK, :] = lse_out

    seq = lambda c0: BS((S, LANES), lambda b, p: (b, c0 + p))
    pair = BS((1, 2, S, LANES), lambda b, p: (b, p, 0, 0))
    return pl.pallas_call(
        body, name="fox_fwd", grid=(Bl, 3),
        in_specs=[seq(qc), seq(kc), seq(vc), pair, pair],
        out_specs=[seq(0), BS((1, S, LANES), lambda b, p: (p, b, 0))],
        out_shape=[jax.ShapeDtypeStruct((T, B_W), F32), jax.ShapeDtypeStruct((3, T, LANES), F32)],
        scratch_shapes=[pltpu.VMEM((2, S, LANES), BF16), pltpu.VMEM((2, S, LANES), BF16)],
        compiler_params=_cp(("arbitrary", "arbitrary")),
    )(proj, proj, proj, bq, bk)


def _memkv_fwd(mem, g_mem, w_kv):
    Bl, Mt, D = mem.shape

    def body(m_ref, g_ref, w_ref, mn_ref, kv_ref):
        mn = _rms(m_ref[0], g_ref[...]).astype(BF16)
        mn_ref[0] = mn
        kv_ref[0] = jnp.dot(mn, w_ref[...], preferred_element_type=F32).astype(BF16)

    return pl.pallas_call(
        body, name="memkv_fwd", grid=(Bl,),
        in_specs=[BS((1, Mt, D), lambda b: (b, 0, 0)), BS((1, D), lambda b: (0, 0)), BS((D, 2 * M_W), lambda b: (0, 0))],
        out_specs=[BS((1, Mt, D), lambda b: (b, 0, 0)), BS((1, Mt, 2 * M_W), lambda b: (b, 0, 0))],
        out_shape=[jax.ShapeDtypeStruct((Bl, Mt, D), BF16), jax.ShapeDtypeStruct((Bl, Mt, 2 * M_W), BF16)],
        compiler_params=_cp(("arbitrary",)),
    )(mem, g_mem, w_kv)


def _memattn_fwd(proj, kv, Bl, S, tq):
    T = Bl * S
    nq = S // tq
    Mt = kv.shape[1]
    qc = 1920 // LANES

    def body(q_ref, km_ref, vm_ref, o_ref):
        lane = _iota((tq, LANES), 1)
        q = q_ref[...]
        out = jnp.zeros((tq, LANES), F32)
        for hh in range(2):
            hmask = (lane < HEAD) if hh == 0 else (lane >= HEAD)
            qs = jnp.where(hmask, q, jnp.zeros_like(q)) * 0.125
            s = _dot_nt(qs, km_ref[0])
            pe = jnp.exp(s - jnp.max(s, axis=1, keepdims=True))
            pn = pe / jnp.sum(pe, axis=1, keepdims=True)
            out = jnp.where(hmask, _dot(pn, vm_ref[0]), out)
        o_ref[...] = out

    return pl.pallas_call(
        body, name="memattn_fwd", grid=(Bl, 2, nq),
        in_specs=[BS((tq, LANES), lambda b, p, i: (b * nq + i, qc + p)),
                  BS((1, Mt, LANES), lambda b, p, i: (b, 0, p)),
                  BS((1, Mt, LANES), lambda b, p, i: (b, 0, 2 + p))],
        out_specs=BS((tq, LANES), lambda b, p, i: (b * nq + i, p)),
        out_shape=jax.ShapeDtypeStruct((T, M_W), F32),
        compiler_params=_cp(("arbitrary", "arbitrary", "arbitrary")),
    )(proj, kv, kv)


def _mix_norms(ya, yb, ym, ga, gb, gm):
    return _rms(ya, ga), _rms(yb, gb), _rms(ym, gm)


def _outproj_fwd(ya, yb, ym, x2d, ga, gb, gm, g_post, g_pre2, w_out, tm):
    T, D = x2d.shape

    def body(ya_ref, yb_ref, ym_ref, x_ref, ga_ref, gb_ref, gm_ref, gp_ref, g2_ref, w_ref,
             y_ref, o_ref, x1_ref, h2_ref):
        na, nb_, nm = _mix_norms(ya_ref[...], yb_ref[...], ym_ref[...], ga_ref[...], gb_ref[...], gm_ref[...])
        y_ref[:, 0:A_W] = na.astype(BF16)
        y_ref[:, A_W:A_W + B_W] = nb_.astype(BF16)
        y_ref[:, A_W + B_W:] = nm.astype(BF16)
        o = jnp.dot(y_ref[...], w_ref[...], preferred_element_type=F32)
        o_ref[...] = o
        x1 = x_ref[...] + _rms(o, gp_ref[...])
        x1_ref[...] = x1
        h2_ref[...] = _rms(x1, g2_ref[...]).astype(BF16)

    row = lambda w: BS((tm, w), lambda i: (i, 0))
    vec = lambda w: BS((1, w), lambda i: (0, 0))
    return pl.pallas_call(
        body, name="outproj_fwd", grid=(T // tm,),
        in_specs=[row(A_W), row(B_W), row(M_W), row(D), vec(A_W), vec(B_W), vec(M_W), vec(D), vec(D),
                  BS((A_W + B_W + M_W, D), lambda i: (0, 0))],
        out_specs=[row(A_W + B_W + M_W), row(D), row(D), row(D)],
        out_shape=[jax.ShapeDtypeStruct((T, A_W + B_W + M_W), BF16), jax.ShapeDtypeStruct((T, D), F32),
                   jax.ShapeDtypeStruct((T, D), F32), jax.ShapeDtypeStruct((T, D), BF16)],
        compiler_params=_cp(("arbitrary",)),
    )(ya, yb, ym, x2d, ga, gb, gm, g_post, g_pre2, w_out)


def _ffn_fwd(h2, x1, target, wg, wu, wd, g_post, tm):
    T, D = x1.shape
    ns, F, _ = wg.shape

    def body(h_ref, x1_ref, t_ref, wg_ref, wu_ref, wd_ref, gp_ref,
             gs_ref, us_ref, dff_ref, dx2_ref, dgp_ref, loss_ref, acc_ref):
        i = pl.program_id(0)
        j = pl.program_id(1)
        h = h_ref[...]
        g = _dot_nt(h, wg_ref[0])
        u = _dot_nt(h, wu_ref[0])
        gs_ref[0] = g.astype(BF16)
        us_ref[0] = u.astype(BF16)
        part = _dot(_silu_mul(g, u), wd_ref[0])
        _acc(acc_ref, part, j == 0)

        @pl.when(j == ns - 1)
        def _():
            normed, vjp = jax.vjp(_rms, acc_ref[...], gp_ref[...])
            diff = x1_ref[...] + normed - t_ref[...]
            dx2 = diff * (1.0 / D)
            dff, dgp = vjp(dx2)
            dx2_ref[...] = dx2
            dff_ref[...] = dff.astype(BF16)
            lpart = jnp.sum(_colsum(diff * diff), axis=1, keepdims=True) * (0.5 / D)
            _acc(dgp_ref, dgp, i == 0)
            _acc(loss_ref, jnp.broadcast_to(lpart, (1, LANES)), i == 0)

    row = lambda w: BS((tm, w), lambda i, j: (i, 0))
    return pl.pallas_call(
        body, name="ffn_fwd", grid=(T // tm, ns),
        in_specs=[row(D), row(D), row(D), BS((1, F, D), lambda i, j: (j, 0, 0)), BS((1, F, D), lambda i, j: (j, 0, 0)),
                  BS((1, F, D), lambda i, j: (j, 0, 0)), BS((1, D), lambda i, j: (0, 0))],
        out_specs=[BS((1, tm, F), lambda i, j: (j, i, 0)), BS((1, tm, F), lambda i, j: (j, i, 0)), row(D), row(D),
                   BS((1, D), lambda i, j: (0, 0)), BS((1, LANES), lambda i, j: (0, 0))],
        out_shape=[jax.ShapeDtypeStruct((ns, T, F), BF16), jax.ShapeDtypeStruct((ns, T, F), BF16),
                   jax.ShapeDtypeStruct((T, D), BF16), jax.ShapeDtypeStruct((T, D), F32),
                   jax.ShapeDtypeStruct((1, D), F32), jax.ShapeDtypeStruct((1, LANES), F32)],
        scratch_shapes=[pltpu.VMEM((tm, D), F32)],
        compiler_params=_cp(("arbitrary", "arbitrary")),
    )(h2, x1, target, wg, wu, wd, g_post)


def _ffn_bwd(dff, h2, gs, us, wg, wu, wd, tm):
    T, D = h2.shape
    ns, F, _ = wg.shape

    def body(dff_ref, h_ref, gs_ref, us_ref, wg_ref, wu_ref, wd_ref, dh_ref, dwg_ref, dwu_ref, dwd_ref):
        first = pl.program_id(1) == 0
        dff = dff_ref[...]
        h = h_ref[...]
        dact = _dot_nt(dff, wd_ref[0])
        g = gs_ref[0].astype(F32)
        u = us_ref[0].astype(F32)
        sig = _sigmoid(g)
        gsig = g * sig
        a = gsig * u
        dg = (dact * u * (sig + gsig * (1.0 - sig))).astype(BF16)
        du = (dact * gsig).astype(BF16)
        dh_ref[0] = (_dot(dg, wg_ref[0]) + _dot(du, wu_ref[0])).astype(BF16)
        _acc(dwd_ref, _dot_tn(a, dff)[None], first)
        _acc(dwg_ref, _dot_tn(dg, h)[None], first)
        _acc(dwu_ref, _dot_tn(du, h)[None], first)

    row = BS((tm, D), lambda j, i: (i, 0))
    sh = BS((1, tm, F), lambda j, i: (j, i, 0))
    wsh = BS((1, F, D), lambda j, i: (j, 0, 0))
    return pl.pallas_call(
        body, name="ffn_bwd", grid=(ns, T // tm),
        in_specs=[row, row, sh, sh, wsh, wsh, wsh],
        out_specs=[BS((1, tm, D), lambda j, i: (j, i, 0)), wsh, wsh, wsh],
        out_shape=[jax.ShapeDtypeStruct((ns, T, D), BF16)] + [jax.ShapeDtypeStruct((ns, F, D), F32)] * 3,
        compiler_params=_cp(("arbitrary", "arbitrary")),
    )(dff, h2, gs, us, wg, wu, wd)


def _mm_tn(a, b, name, tk, col_shards=1):
    nbatch = col_shards
    T, M = a.shape
    N = b.shape[1] // col_shards
    tk = min(tk, T)
    narrow = col_shards > 1

    def body(a_ref, b_ref, o_ref, *acc):
        t = pl.program_id(1)
        if narrow:
            _acc(acc[0], _dot_tn(b_ref[...], a_ref[...]), t == 0)

            @pl.when(t == pl.num_programs(1) - 1)
            def _():
                o_ref[0] = acc[0][...].T
        else:
            _acc(o_ref, _dot_tn(a_ref[...], b_ref[...])[None], t == 0)

    a_spec = BS((tk, M), lambda s, t: (t, 0))
    b_spec = BS((tk, N), lambda s, t: (t, s))
    return pl.pallas_call(
        body, name=name, grid=(nbatch, T // tk),
        in_specs=[a_spec, b_spec],
        out_specs=BS((1, M, N), lambda s, t: (s, 0, 0)),
        out_shape=jax.ShapeDtypeStruct((nbatch, M, N), F32),
        scratch_shapes=[pltpu.VMEM((N, M), F32)] if narrow else [],
        compiler_params=_cp(("arbitrary", "arbitrary")),
    )(a, b)


def _outproj_bwd(dh2, x1, dx2, o, ya, yb, ym, ga, gb, gm, g_post, g_pre2, w_out, tm):
    T, D = x1.shape
    ns = dh2.shape[0]

    def body(dh_ref, x1_ref, dx2_ref, o_ref, ya_ref, yb_ref, ym_ref, ga_ref, gb_ref, gm_ref, gp_ref, g2_ref, w_ref,
             dx1_ref, do_ref, dya_ref, dyb_ref, dym_ref, dga_ref, dgb_ref, dgm_ref, dgp_ref, dg2_ref):
        first = pl.program_id(0) == 0
        dh = dh_ref[0].astype(F32)
        for j in range(1, ns):
            dh = dh + dh_ref[j].astype(F32)
        _, vjp0 = jax.vjp(_rms, x1_ref[...], g2_ref[...])
        dxa, dg2 = vjp0(dh)
        dx1 = dx2_ref[...] + dxa
        dx1_ref[...] = dx1
        _acc(dg2_ref, dg2, first)
        _, vjp = jax.vjp(_rms, o_ref[...], gp_ref[...])
        do, dgp = vjp(dx1)
        do = do.astype(BF16)
        do_ref[...] = do
        dy = _dot_nt(do, w_ref[...])
        _, vjp2 = jax.vjp(_mix_norms, ya_ref[...], yb_ref[...], ym_ref[...], ga_ref[...], gb_ref[...], gm_ref[...])
        dya, dyb, dym, dga, dgb, dgm = vjp2((dy[:, 0:A_W], dy[:, A_W:A_W + B_W], dy[:, A_W + B_W:]))
        dya_ref[...] = dya
        dyb_ref[...] = dyb
        dym_ref[...] = dym
        _acc(dga_ref, dga, first)
        _acc(dgb_ref, dgb, first)
        _acc(dgm_ref, dgm, first)
        _acc(dgp_ref, dgp, first)

    row = lambda w: BS((tm, w), lambda i: (i, 0))
    vec = lambda w: BS((1, w), lambda i: (0, 0))
    sds = jax.ShapeDtypeStruct
    return pl.pallas_call(
        body, name="outproj_bwd", grid=(T // tm,),
        in_specs=[BS((ns, tm, D), lambda i: (0, i, 0)), row(D), row(D), row(D), row(A_W), row(B_W), row(M_W),
                  vec(A_W), vec(B_W), vec(M_W), vec(D), vec(D), BS((A_W + B_W + M_W, D), lambda i: (0, 0))],
        out_specs=[row(D), row(D), row(A_W), row(B_W), row(M_W), vec(A_W), vec(B_W), vec(M_W), vec(D), vec(D)],
        out_shape=[sds((T, D), F32), sds((T, D), BF16), sds((T, A_W), F32), sds((T, B_W), F32), sds((T, M_W), F32),
                   sds((1, A_W), F32), sds((1, B_W), F32), sds((1, M_W), F32), sds((1, D), F32), sds((1, D), F32)],
        compiler_params=_cp(("arbitrary",)),
    )(dh2, x1, dx2, o, ya, yb, ym, ga, gb, gm, g_post, g_pre2, w_out)


def _sgu_bwd(proj, dya, g_sgu, ws_tril, bs_full, tm):
    T = proj.shape[0]
    nch = tm // CHUNK

    def body(zu_ref, zv_ref, dy_ref, g_ref, ws_ref, b_ref, dzu_ref, dzv_ref, dws_ref, dbs_ref, dg_ref,
             du_ref, dvn_ref, dbf_ref):
        step = pl.program_id(0)
        first = step == 0
        lane = _iota((CHUNK, LANES), 1)
        tril = _iota((CHUNK, CHUNK), 0) >= _iota((CHUNK, CHUNK), 1)
        (u, vn), vjp = jax.vjp(_sgu_pre, zu_ref[...].astype(F32), zv_ref[...].astype(F32), g_ref[...])
        vnb = vn.astype(BF16)
        dy = dy_ref[...]

        @pl.when(first)
        def _():
            dws_ref[...] = jnp.zeros_like(dws_ref)
            dbf_ref[...] = jnp.zeros_like(dbf_ref)

        for c in range(nch):
            rs = slice(c * CHUNK, (c + 1) * CHUNK)
            for j in range(3):
                cs = slice(j * LANES, (j + 1) * LANES)
                vp = vnb[rs, cs]
                z = jnp.where(lane < HEAD, _dot(ws_ref[2 * j], vp), _dot(ws_ref[2 * j + 1], vp)) + b_ref[:, cs]
                du_ref[rs, cs] = dy[rs, cs] * z
                dz = dy[rs, cs] * u[rs, cs]
                dbf_ref[:, cs] += dz
                dzb = dz.astype(BF16)
                dz0 = jnp.where(lane < HEAD, dzb, jnp.zeros_like(dzb))
                dz1 = jnp.where(lane >= HEAD, dzb, jnp.zeros_like(dzb))
                dvn_ref[rs, cs] = jnp.where(lane < HEAD, _dot_tn(ws_ref[2 * j], dzb), _dot_tn(ws_ref[2 * j + 1], dzb))
                dws_ref[2 * j] += jnp.where(tril, _dot_nt(dz0, vp), 0.0)
                dws_ref[2 * j + 1] += jnp.where(tril, _dot_nt(dz1, vp), 0.0)
        dzu, dzv, dg = vjp((du_ref[...], dvn_ref[...]))
        dzu_ref[...] = dzu.astype(BF16)
        dzv_ref[...] = dzv.astype(BF16)
        _acc(dg_ref, dg, first)

        @pl.when(step == pl.num_programs(0) - 1)
        def _():
            out = jnp.zeros((CHUNK, LANES), F32)
            for j in range(3):
                slab = dbf_ref[:, j * LANES:(j + 1) * LANES]
                lo = jnp.sum(jnp.where(lane < HEAD, slab, 0.0), axis=1, keepdims=True)
                hi = jnp.sum(jnp.where(lane >= HEAD, slab, 0.0), axis=1, keepdims=True)
                out = out + jnp.where(lane == 2 * j, lo, 0.0) + jnp.where(lane == 2 * j + 1, hi, 0.0)
            dbs_ref[...] = out

    return pl.pallas_call(
        body, name="sgu_bwd", grid=(T // tm,),
        in_specs=[BS((tm, A_W), lambda i: (i, 0)), BS((tm, A_W), lambda i: (i, 1)), BS((tm, A_W), lambda i: (i, 0)),
                  BS((1, A_W), lambda i: (0, 0)), BS((6, CHUNK, CHUNK), lambda i: (0, 0, 0)),
                  BS((CHUNK, A_W), lambda i: (0, 0))],
        out_specs=[BS((tm, A_W), lambda i: (i, 0)), BS((tm, A_W), lambda i: (i, 0)),
                   BS((6, CHUNK, CHUNK), lambda i: (0, 0, 0)), BS((CHUNK, LANES), lambda i: (0, 0)),
                   BS((1, A_W), lambda i: (0, 0))],
        out_shape=[jax.ShapeDtypeStruct((T, A_W), BF16), jax.ShapeDtypeStruct((T, A_W), BF16),
                   jax.ShapeDtypeStruct((6, CHUNK, CHUNK), F32), jax.ShapeDtypeStruct((CHUNK, LANES), F32),
                   jax.ShapeDtypeStruct((1, A_W), F32)],
        scratch_shapes=[pltpu.VMEM((tm, A_W), F32), pltpu.VMEM((tm, A_W), F32), pltpu.VMEM((CHUNK, A_W), F32)],
        compiler_params=_cp(("arbitrary",)),
    )(proj, proj, dya, g_sgu, ws_tril, bs_full)


def _memattn_bwd(proj, kv, dym, Bl, S, tq):
    T = Bl * S
    nq = S // tq
    Mt = kv.shape[1]
    qc = 1920 // LANES

    def body(q_ref, km_ref, vm_ref, do_ref, dq_ref, dkm_ref, dvm_ref):
        first = pl.program_id(2) == 0
        lane = _iota((tq, LANES), 1)
        q = q_ref[...]
        do = do_ref[...]
        dq_out = jnp.zeros((tq, LANES), F32)
        dkm = jnp.zeros((Mt, LANES), F32)
        dvm = jnp.zeros((Mt, LANES), F32)
        for hh in range(2):
            hmask = (lane < HEAD) if hh == 0 else (lane >= HEAD)
            qs = jnp.where(hmask, q, jnp.zeros_like(q)) * 0.125
            dom = jnp.where(hmask, do, 0.0).astype(BF16)
            s = _dot_nt(qs, km_ref[0])
            pe = jnp.exp(s - jnp.max(s, axis=1, keepdims=True))
            pn = pe / jnp.sum(pe, axis=1, keepdims=True)
            dp = _dot_nt(dom, vm_ref[0])
            ds = (pn * (dp - jnp.sum(pn * dp, axis=1, keepdims=True))).astype(BF16)
            dq_out = jnp.where(hmask, _dot(ds, km_ref[0]) * 0.125, dq_out)
            dkm = dkm + _dot_tn(ds, qs)
            dvm = dvm + _dot_tn(pn, dom)
        dq_ref[...] = dq_out.astype(BF16)
        _acc(dkm_ref, dkm[None], first)
        _acc(dvm_ref, dvm[None], first)

    return pl.pallas_call(
        body, name="memattn_bwd", grid=(Bl, 2, nq),
        in_specs=[BS((tq, LANES), lambda b, p, i: (b * nq + i, qc + p)),
                  BS((1, Mt, LANES), lambda b, p, i: (b, 0, p)),
                  BS((1, Mt, LANES), lambda b, p, i: (b, 0, 2 + p)),
                  BS((tq, LANES), lambda b, p, i: (b * nq + i, p))],
        out_specs=[BS((tq, LANES), lambda b, p, i: (b * nq + i, p)),
                   BS((1, Mt, LANES), lambda b, p, i: (b, 0, p)),
                   BS((1, Mt, LANES), lambda b, p, i: (b, 0, p))],
        out_shape=[jax.ShapeDtypeStruct((T, M_W), BF16), jax.ShapeDtypeStruct((Bl, Mt, M_W), F32),
                   jax.ShapeDtypeStruct((Bl, Mt, M_W), F32)],
        compiler_params=_cp(("arbitrary", "arbitrary", "arbitrary")),
    )(proj, kv, kv, dym)


def _memkv_bwd(dkm, dvm, memn, mem, g_mem, w_kv):
    Bl, Mt, D = mem.shape

    def body(dk_ref, dv_ref, mn_ref, m_ref, g_ref, w_ref, dw_ref, dg_ref):
        first = pl.program_id(0) == 0
        dk = dk_ref[0].astype(BF16)
        dv = dv_ref[0].astype(BF16)
        mn = mn_ref[0]
        dmn = _dot_nt(dk, w_ref[:, 0:M_W]) + _dot_nt(dv, w_ref[:, M_W:])
        _, vjp = jax.vjp(_rms, m_ref[0], g_ref[...])
        _, dg = vjp(dmn)
        _acc(dg_ref, dg, first)

        @pl.when(first)
        def _():
            dw_ref[...] = jnp.zeros_like(dw_ref)

        dw_ref[:, 0:M_W] += _dot_tn(mn, dk)
        dw_ref[:, M_W:] += _dot_tn(mn, dv)

    return pl.pallas_call(
        body, name="memkv_bwd", grid=(Bl,),
        in_specs=[BS((1, Mt, M_W), lambda b: (b, 0, 0)), BS((1, Mt, M_W), lambda b: (b, 0, 0)),
                  BS((1, Mt, D), lambda b: (b, 0, 0)), BS((1, Mt, D), lambda b: (b, 0, 0)),
                  BS((1, D), lambda b: (0, 0)), BS((D, 2 * M_W), lambda b: (0, 0))],
        out_specs=[BS((D, 2 * M_W), lambda b: (0, 0)), BS((1, D), lambda b: (0, 0))],
        out_shape=[jax.ShapeDtypeStruct((D, 2 * M_W), F32), jax.ShapeDtypeStruct((1, D), F32)],
        compiler_params=_cp(("arbitrary",)),
    )(dkm, dvm, memn, mem, g_mem, w_kv)


def _fox_bwd(proj, dyb, lse, bq, bk, Bl, S):
    T = Bl * S
    nq = S // Q_BLK
    nb = S // LANES
    qc, kc, vc = 768 // LANES, 1152 // LANES, 1536 // LANES

    def body(q_ref, k_ref, v_ref, do_ref, lse_ref, bq_ref, bk_ref,
             dq_ref, dk_ref, dv_ref, dcr_ref, ka_ref, dka_ref, dva_ref):
        p = pl.program_id(1)
        lane_s = _iota((S, LANES), 1)
        lane = _iota((Q_BLK, LANES), 1)
        sub = _iota((8, LANES), 0)
        tri = _iota((Q_BLK, Q_BLK), 1) <= _iota((Q_BLK, Q_BLK), 0)
        k = k_ref[...]
        for hh in range(2):
            data = (lane_s < HEAD) if hh == 0 else (lane_s >= HEAD)
            ka_ref[hh] = jnp.where(data, k, bk_ref[0, hh])
        dka_ref[...] = jnp.zeros_like(dka_ref)
        dva_ref[...] = jnp.zeros_like(dva_ref)

        @pl.when(p == 0)
        def _():
            dcr_ref[...] = jnp.zeros_like(dcr_ref)

        def add_colsums(ds, first_blk, h):
            cs = _colsum(ds)
            for jb in range(ds.shape[1] // LANES):
                dcr_ref[0, first_blk + jb] += jnp.where(sub == h, cs[:, jb * LANES:(jb + 1) * LANES], 0.0)

        for i in range(nq):
            r0 = i * Q_BLK
            r1 = r0 + Q_BLK
            q = q_ref[r0:r1, :]
            do = do_ref[r0:r1, :]
            lse_b = lse_ref[0, r0:r1, :]
            dq_out = jnp.zeros((Q_BLK, LANES), F32)
            for hh in range(2):
                hmask = (lane < HEAD) if hh == 0 else (lane >= HEAD)
                h = 2 * p + hh
                qs = jnp.where(hmask, q * 0.125, jnp.zeros_like(q))
                qa = jnp.where(hmask, q * 0.125, bq_ref[0, hh, r0:r1, :])
                dob = jnp.where(hmask, do, 0.0).astype(BF16)
                lse_h = jnp.sum(jnp.where(lane == hh * HEAD, lse_b, 0.0), axis=1, keepdims=True)
                pd = jnp.where(tri, jnp.exp(_dot_nt(qa, ka_ref[hh, r0:r1, :]) - lse_h), 0.0)
                dpd = _dot_nt(dob, v_ref[r0:r1, :])
                delta = jnp.sum(pd * dpd, axis=1, keepdims=True)
                psum = jnp.sum(pd, axis=1, keepdims=True)
                if i:
                    pf = jnp.exp(_dot_nt(qa, ka_ref[hh, 0:r0, :]) - lse_h)
                    dpf = _dot_nt(dob, v_ref[0:r0, :])
                    delta = delta + jnp.sum(pf * dpf, axis=1, keepdims=True)
                    psum = psum + jnp.sum(pf, axis=1, keepdims=True)
                delta = delta / psum
                dsd = pd * (dpd - delta)
                add_colsums(dsd, r0 // LANES, h)
                dsd = dsd.astype(BF16)
                dq_h = _dot(dsd, k_ref[r0:r1, :])
                dka_ref[r0:r1, :] += _dot_tn(dsd, qs)
                dva_ref[r0:r1, :] += _dot_tn(pd, dob)
                if i:
                    dsf = pf * (dpf - delta)
                    add_colsums(dsf, 0, h)
                    dsf = dsf.astype(BF16)
                    dq_h = dq_h + _dot(dsf, k_ref[0:r0, :])
                    dka_ref[0:r0, :] += _dot_tn(dsf, qs)
                    dva_ref[0:r0, :] += _dot_tn(pf, dob)
                dq_out = jnp.where(hmask, dq_h * 0.125, dq_out)
            dq_ref[r0:r1, :] = dq_out.astype(BF16)
        dk_ref[...] = dka_ref[...].astype(BF16)
        dv_ref[...] = dva_ref[...].astype(BF16)

    seq = lambda c0: BS((S, LANES), lambda b, p: (b, c0 + p))
    pair = BS((1, 2, S, LANES), lambda b, p: (b, p, 0, 0))
    rowblk = BS((1, nb, 8, LANES), lambda b, p: (b, 0, 0, 0))
    return pl.pallas_call(
        body, name="fox_bwd", grid=(Bl, 3),
        in_specs=[seq(qc), seq(kc), seq(vc), seq(0), BS((1, S, LANES), lambda b, p: (p, b, 0)), pair, pair],
        out_specs=[seq(0), seq(0), seq(0), rowblk],
        out_shape=[jax.ShapeDtypeStruct((T, B_W), BF16)] * 3 + [jax.ShapeDtypeStruct((Bl, nb, 8, LANES), F32)],
        scratch_shapes=[pltpu.VMEM((2, S, LANES), BF16), pltpu.VMEM((S, LANES), F32), pltpu.VMEM((S, LANES), F32)],
        compiler_params=_cp(("arbitrary", "arbitrary")),
    )(proj, proj, proj, dyb, lse, bq, bk)


def _gate_bwd(dc_row, fl_row):
    Bl, nb, _, _ = dc_row.shape

    def body(dc_ref, fl_ref, o_ref):
        lane = _iota((8, LANES), 1)

        def blk(jj, carry):
            j = nb - 1 - jj
            r = -dc_ref[0, j]
            for k in (1, 2, 4, 8, 16, 32, 64):
                r = r + jnp.where(lane < LANES - k, pltpu.roll(r, LANES - k, 1), 0.0)
            r = r + carry
            dfl = r * _sigmoid(-fl_ref[0, j])
            o_ref[0, pl.ds(pl.multiple_of(j * LANES, LANES), LANES), :] = jnp.concatenate(
                [dfl, jnp.zeros((LANES - 8, LANES), F32)], axis=0).T
            return jnp.sum(jnp.where(lane == 0, r, 0.0), axis=1, keepdims=True)

        lax.fori_loop(0, nb, blk, jnp.zeros((8, 1), F32))

    rowblk = BS((1, nb, 8, LANES), lambda b: (b, 0, 0, 0))
    return pl.pallas_call(
        body, name="gate_bwd", grid=(Bl,),
        in_specs=[rowblk, rowblk],
        out_specs=BS((1, nb * LANES, LANES), lambda b: (b, 0, 0)),
        out_shape=jax.ShapeDtypeStruct((Bl, nb * LANES, LANES), F32),
        compiler_params=_cp(("arbitrary",)),
    )(dc_row, fl_row)


def _inproj_bwd(dzu, dzv, dq, dk, dv, dqm, dfl, x2d, dx1, g_pre, w_in_p, tm):
    T, D = x2d.shape
    ns, _, dsh = w_in_p.shape

    def body(dzu_ref, dzv_ref, dq_ref, dk_ref, dv_ref, dqm_ref, dfl_ref, x_ref, dx1_ref, g_ref, w_ref,
             dp_ref, gx_ref, dg_ref, dbf_ref):
        first = pl.program_id(0) == 0
        dfl = dfl_ref[...]
        dp_ref[:, 0:384] = dzu_ref[...]
        dp_ref[:, 384:768] = dzv_ref[...]
        dp_ref[:, 768:1152] = dq_ref[...]
        dp_ref[:, 1152:1536] = dk_ref[...]
        dp_ref[:, 1536:1920] = dv_ref[...]
        dp_ref[:, 1920:2048] = dfl.astype(BF16)
        dp_ref[:, 2048:2304] = dqm_ref[...]
        dh = jnp.concatenate([_dot(dp_ref[...], w_ref[s]) for s in range(ns)], axis=1)
        _, vjp = jax.vjp(_rms, x_ref[...], g_ref[...])
        dxa, dg = vjp(dh)
        gx_ref[...] = dx1_ref[...] + dxa
        _acc(dg_ref, dg, first)
        _acc(dbf_ref, _colsum(dfl), first)

    row = lambda w: BS((tm, w), lambda i: (i, 0))
    return pl.pallas_call(
        body, name="inproj_bwd", grid=(T // tm,),
        in_specs=[row(A_W), row(A_W), row(B_W), row(B_W), row(B_W), row(M_W), row(LANES), row(D), row(D),
                  BS((1, D), lambda i: (0, 0)), BS((ns, P_COLS, dsh), lambda i: (0, 0, 0))],
        out_specs=[row(P_COLS), row(D), BS((1, D), lambda i: (0, 0)), BS((1, LANES), lambda i: (0, 0))],
        out_shape=[jax.ShapeDtypeStruct((T, P_COLS), BF16), jax.ShapeDtypeStruct((T, D), F32),
                   jax.ShapeDtypeStruct((1, D), F32), jax.ShapeDtypeStruct((1, LANES), F32)],
        compiler_params=_cp(("arbitrary",)),
    )(dzu, dzv, dq, dk, dv, dqm, dfl, x2d, dx1, g_pre, w_in_p)


def _local_step(x, mem, target, W, P, reduce=None):
    Bl, S, D = x.shape
    T = Bl * S
    tm = min(512, T)
    x2d = x.reshape(T, D)
    t2d = target.reshape(T, D)
    vec = lambda a: a.reshape(1, -1)
    bf_row = jnp.pad(P["b_f"].reshape(1, -1), ((0, 0), (0, LANES - N_FOX_HEADS)))
    tril = jnp.tril(jnp.ones((CHUNK, CHUNK), bool))
    ws_tril = jnp.where(tril[None], P["w_s"][0], 0.0).astype(BF16)
    bs_full = jnp.repeat(P["b_s"][0].T, HEAD, axis=1)
    g_pre, g_sgu = vec(P["g_pre_mix"]), vec(P["g_sgu"])
    ga, gb, gm = vec(P["g_out_a"]), vec(P["g_out_b"]), vec(P["g_out_m"])
    g_mem, g_post, g_pre2, g_post2 = vec(P["g_mem"]), vec(P["g_post_mix"]), vec(P["g_pre_ffn"]), vec(P["g_post_ffn"])

    h, proj, flog = _inproj_fwd(x2d, g_pre, W["w_in"], tm)
    bq, bk, fl_row = _gate_fwd(flog.reshape(Bl, S, LANES), bf_row)
    ya = _sgu_fwd(proj, g_sgu, ws_tril, bs_full, tm)
    yb, lse = _fox_fwd(proj, bq, bk, Bl, S)
    memn, kv = _memkv_fwd(mem, g_mem, W["w_mem_kv"])
    ym = _memattn_fwd(proj, kv, Bl, S, min(512, S))
    y, o, x1, h2 = _outproj_fwd(ya, yb, ym, x2d, ga, gb, gm, g_post, g_pre2, W["w_out"], tm)
    gs, us, dff, dx2, dg_post2, loss = _ffn_fwd(h2, x1, t2d, W["w_gate"], W["w_up"], W["w_down"], g_post2, tm)

    dh2, d_w_gate, d_w_up, d_w_down = _ffn_bwd(dff, h2, gs, us, W["w_gate"], W["w_up"], W["w_down"], tm)
    ffn = [d_w_gate, d_w_up, d_w_down]
    if reduce is not None:
        pending = reduce.begin("ffn", ffn)
    dx1, do, dya, dyb, dym, dga, dgb, dgm, dg_post, dg_pre2 = _outproj_bwd(
        dh2, x1, dx2, o, ya, yb, ym, ga, gb, gm, g_post, g_pre2, W["w_out"], tm)
    if reduce is not None:
        ffn, (do, dya, dyb, dym) = reduce.finish("ffn", pending, (do, dya, dyb, dym))
    d_w_out = _mm_tn(y, do, "dw_out", 1024)[0]
    dzu, dzv, dws, dbs_cols, dg_sgu = _sgu_bwd(proj, dya, g_sgu, ws_tril, bs_full, tm)
    dqm, dkm, dvm = _memattn_bwd(proj, kv, dym, Bl, S, min(512, S))
    d_w_kv, dg_mem = _memkv_bwd(dkm, dvm, memn, mem, g_mem, W["w_mem_kv"])
    dq, dk, dv, dc_row = _fox_bwd(proj, dyb, lse, bq, bk, Bl, S)
    dfl = _gate_bwd(dc_row, fl_row).reshape(T, LANES)
    dproj, grad_x, dg_pre, dbf = _inproj_bwd(dzu, dzv, dq, dk, dv, dqm, dfl, x2d, dx1, g_pre, W["w_in"], tm)
    d_w_in = _mm_tn(dproj, h, "dw_in", 1024, col_shards=W["w_in"].shape[0])

    mix = [d_w_in, d_w_kv, d_w_out]
    if reduce is not None:
        mix, _ = reduce.finish("mix", reduce.begin("mix", mix), ())
    big = dict(zip(BIG, mix + ffn))
    small = {"g_pre_mix": dg_pre, "b_f": dbf[:, :N_FOX_HEADS], "g_sgu": dg_sgu, "w_s": dws, "b_s": dbs_cols[:, :N_FOX_HEADS].T,
             "g_out_a": dga, "g_out_b": dgb, "g_out_m": dgm, "g_mem": dg_mem, "g_post_mix": dg_post,
             "g_pre_ffn": dg_pre2, "g_post_ffn": dg_post2, "loss": loss[:, :1]}
    return grad_x.reshape(Bl, S, D), big, small


def _place():
    return lax.axis_index("x"), lax.axis_index("y"), lax.axis_index("c")


def _exchange_on_sequencer(srcs, own_full, name, collective_id):
    n = len(srcs)

    def body(*refs):
        src, dst = refs[:n], refs[n:2 * n]
        lsem, isend, irecv, dsend, drecv = refs[2 * n:]
        x, y, c = _place()
        oc = 1 - c
        s_me = 2 * x + y
        sib = (x, y, oc)
        chips = [(1 - x, y), (x, 1 - y), (1 - x, 1 - y)]
        barrier = pltpu.get_barrier_semaphore()
        for dev in [(cx, cy, c) for cx, cy in chips] + [sib]:
            pl.semaphore_signal(barrier, inc=1, device_id=dev, device_id_type=MESH)
        pl.semaphore_wait(barrier, 4)

        def remote(a, b, ssem, rsem, dev):
            return pltpu.make_async_remote_copy(src_ref=a, dst_ref=b, send_sem=ssem, recv_sem=rsem,
                                                device_id=dev, device_id_type=MESH)

        sends, local = [], []
        for w in range(n):
            for j, (cx, cy) in enumerate(chips):
                half = src[w].at[c] if own_full else src[w].at[2 * cx + cy]
                cp = remote(half, dst[w].at[s_me, c], isend.at[w, j], irecv.at[w, j], (cx, cy, c))
                cp.start()
                sends.append(cp)
            if own_full:
                cp = remote(src[w], dst[w].at[s_me], dsend.at[w, 3], drecv.at[w, 3], sib)
            else:
                cp = remote(src[w].at[s_me], dst[w].at[s_me, c], dsend.at[w, 3], drecv.at[w, 3], sib)
                loc = pltpu.make_async_copy(src[w].at[s_me], dst[w].at[s_me, c], lsem.at[w])
                loc.start()
                local.append(loc)
            cp.start()
            sends.append(cp)
        for w in range(n):
            for j, (cx, cy) in enumerate(chips):
                landed = dst[w].at[2 * cx + cy, c]
                remote(landed, landed, isend.at[w, j], irecv.at[w, j], (cx, cy, c)).wait_recv()
                cp = remote(landed, landed, dsend.at[w, j], drecv.at[w, j], sib)
                cp.start()
                sends.append(cp)
        for w in range(n):
            for j, (cx, cy) in enumerate(chips):
                landed = dst[w].at[2 * cx + cy, oc]
                remote(landed, landed, dsend.at[w, j], drecv.at[w, j], sib).wait_recv()
            landed = dst[w].at[s_me] if own_full else dst[w].at[s_me, oc]
            remote(landed, landed, dsend.at[w, 3], drecv.at[w, 3], sib).wait_recv()
        for cp in sends:
            cp.wait_send()
        for loc in local:
            loc.wait()

    return pl.kernel(
        body, out_type=[jax.ShapeDtypeStruct((4, 2) + s.shape[1:], s.dtype) for s in srcs],
        mesh=plsc.ScalarSubcoreMesh(axis_name="sequencer", num_cores=1), name=name,
        scratch_types=[pltpu.SemaphoreType.DMA((n,)), pltpu.SemaphoreType.DMA((n, 3)), pltpu.SemaphoreType.DMA((n, 3)),
                       pltpu.SemaphoreType.DMA((n, 4)), pltpu.SemaphoreType.DMA((n, 4))],
        compiler_params=pltpu.CompilerParams(collective_id=collective_id),
    )(*srcs)


def _sibling_swap(grads, name, collective_id):
    n = len(grads)

    def body(*refs):
        g, theirs = refs[:n], refs[n:2 * n]
        ssem, rsem = refs[2 * n:]
        x, y, c = _place()
        sib = (x, y, 1 - c)
        barrier = pltpu.get_barrier_semaphore()
        pl.semaphore_signal(barrier, inc=1, device_id=sib, device_id_type=MESH)
        pl.semaphore_wait(barrier, 1)
        cps = []
        for w in range(n):
            cp = pltpu.make_async_remote_copy(src_ref=g[w].at[:, 1 - c], dst_ref=theirs[w], send_sem=ssem.at[w],
                                              recv_sem=rsem.at[w], device_id=sib, device_id_type=MESH)
            cp.start()
            cps.append(cp)
        for cp in cps:
            cp.wait()

    return pl.kernel(
        body, out_type=[jax.ShapeDtypeStruct((4,) + g.shape[2:], g.dtype) for g in grads],
        mesh=plsc.ScalarSubcoreMesh(axis_name="sequencer", num_cores=1), name=name,
        scratch_types=[pltpu.SemaphoreType.DMA((n,)), pltpu.SemaphoreType.DMA((n,))],
        compiler_params=pltpu.CompilerParams(collective_id=collective_id),
    )(*grads)


def _add_pair(core, g, theirs, name):
    _, _, hr, C = g.shape

    def body(core_ref, g_ref, t_ref, o_ref):
        o_ref[0] = (g_ref[0, 0] + t_ref[0]).astype(BF16)

    blk = BS((1, hr, C), lambda s, core_ref: (s, 0, 0))
    return pl.pallas_call(
        body, name=name,
        grid_spec=pltpu.PrefetchScalarGridSpec(
            num_scalar_prefetch=1, grid=(4,),
            in_specs=[BS((1, 1, hr, C), lambda s, core_ref: (s, core_ref[0], 0, 0)), blk], out_specs=blk),
        out_shape=jax.ShapeDtypeStruct(theirs.shape, BF16), compiler_params=_cp(("arbitrary",)))(core, g, theirs)


def _sum_chips(r, name):
    _, _, hr, C = r.shape

    def body(r_ref, o_ref):
        o_ref[...] = ((r_ref[0, 0].astype(F32) + r_ref[1, 0].astype(F32)) + r_ref[2, 0].astype(F32)) + r_ref[3, 0].astype(F32)

    return pl.pallas_call(body, name=name, grid=(2,), in_specs=[BS((4, 1, hr, C), lambda h: (0, h, 0, 0))],
                          out_specs=BS((hr, C), lambda h: (h, 0)), out_shape=jax.ShapeDtypeStruct((2 * hr, C), F32),
                          compiler_params=_cp(("arbitrary",)))(r)


class _Reducer:
    IDS = {"ffn": (4, 5), "mix": (6, 7)}

    def __init__(self, core):
        self.core = core

    def begin(self, tag, grads):
        g4 = [g.reshape(4, 2, -1, g.shape[-1]) for g in grads]
        return g4, _sibling_swap(g4, "swap_" + tag, self.IDS[tag][0])

    def finish(self, tag, pending, hold):
        g4, theirs = pending
        sums = [_add_pair(self.core, g, t, "chip_sum_%s_%d" % (tag, k)) for k, (g, t) in enumerate(zip(g4, theirs))]
        sums, hold = lax.optimization_barrier((sums, hold))
        return _exchange_on_sequencer(sums, False, "scatter_" + tag, self.IDS[tag][1]), hold


def _small_allreduce(part):
    R = part.shape[0]
    rs = R // 8
    masks = [(mx, my, mc) for mx in (0, 1) for my in (0, 1) for mc in (0, 1)][1:]

    def body(p_ref, o_ref, buf_ref, s1, r1, s2, r2):
        x, y, c = _place()
        d = 4 * x + 2 * y + c
        mine = pl.ds(pl.multiple_of(d * rs, 8), rs)
        peers = [((x + mx) % 2, (y + my) % 2, (c + mc) % 2) for mx, my, mc in masks]
        first, second = [], []
        for k, (px, py, pc) in enumerate(peers):
            theirs = pl.ds(pl.multiple_of((4 * px + 2 * py + pc) * rs, 8), rs)
            cp = pltpu.make_async_remote_copy(src_ref=p_ref.at[theirs, :], dst_ref=buf_ref.at[d], send_sem=s1.at[k],
                                              recv_sem=r1.at[k], device_id=(px, py, pc), device_id_type=MESH)
            cp.start()
            first.append(cp)
        buf_ref[d] = p_ref[mine, :]
        for k, (px, py, pc) in enumerate(peers):
            slot = buf_ref.at[4 * px + 2 * py + pc]
            pltpu.make_async_remote_copy(src_ref=slot, dst_ref=slot, send_sem=s1.at[k], recv_sem=r1.at[k],
                                         device_id=(px, py, pc), device_id_type=MESH).wait_recv()
        total = buf_ref[0]
        for k in range(1, 8):
            total = total + buf_ref[k]
        o_ref[mine, :] = total
        for k, (px, py, pc) in enumerate(peers):
            cp = pltpu.make_async_remote_copy(src_ref=o_ref.at[mine, :], dst_ref=o_ref.at[mine, :], send_sem=s2.at[k],
                                              recv_sem=r2.at[k], device_id=(px, py, pc), device_id_type=MESH)
            cp.start()
            second.append(cp)
        for k, (px, py, pc) in enumerate(peers):
            rows = o_ref.at[pl.ds(pl.multiple_of((4 * px + 2 * py + pc) * rs, 8), rs), :]
            pltpu.make_async_remote_copy(src_ref=rows, dst_ref=rows, send_sem=s2.at[k], recv_sem=r2.at[k],
                                         device_id=(px, py, pc), device_id_type=MESH).wait_recv()
        for cp in first + second:
            cp.wait_send()

    vm = pl.BlockSpec(memory_space=pltpu.VMEM)
    return pl.pallas_call(
        body, name="small_allreduce", in_specs=[vm], out_specs=vm, out_shape=jax.ShapeDtypeStruct(part.shape, F32),
        scratch_shapes=[pltpu.VMEM((8, rs, LANES), F32)] + [pltpu.SemaphoreType.DMA((7,))] * 4,
    )(part)


def _adamw(w, g, m, v, name):
    R, C = w.shape
    summed = g.ndim == 4
    if summed:
        tr = R // 2
    else:
        tr = R if R * C * 4 <= (1 << 21) else R // 2
        if tr % 8:
            tr = R
    c1 = 1.0 / (1.0 - ADAM_B1 ** ADAM_STEP)
    c2 = 1.0 / (1.0 - ADAM_B2 ** ADAM_STEP)

    def body(w_ref, g_ref, m_ref, v_ref, *outs):
        if summed:
            g_ = ((g_ref[0, 0].astype(F32) + g_ref[1, 0].astype(F32)) + g_ref[2, 0].astype(F32)) + g_ref[3, 0].astype(F32)
            outs[0][...] = g_
        else:
            g_ = g_ref[...]
        d_ref, mo_ref, vo_ref = outs[-3:]
        m_ = ADAM_B1 * m_ref[...] + (1.0 - ADAM_B1) * g_
        v_ = ADAM_B2 * v_ref[...] + (1.0 - ADAM_B2) * (g_ * g_)
        mo_ref[...] = m_
        vo_ref[...] = v_
        d_ref[...] = -ADAM_LR * ((m_ * c1) / (jnp.sqrt(v_ * c2) + ADAM_EPS) + ADAM_WD * w_ref[...])

    blk = BS((tr, C), lambda i: (i, 0))
    g_blk = BS((4, 1, tr, C), lambda i: (0, i, 0, 0)) if summed else blk
    nout = 4 if summed else 3
    return pl.pallas_call(body, name=name, grid=(R // tr,), in_specs=[blk, g_blk, blk, blk], out_specs=[blk] * nout,
                          out_shape=[jax.ShapeDtypeStruct((R, C), F32)] * nout,
                          compiler_params=_cp(("arbitrary",)))(w, g, m, v)


SMALL = ("g_pre_mix", "b_f", "g_sgu", "w_s", "b_s", "g_out_a", "g_out_b", "g_out_m", "g_mem", "g_post_mix",
         "g_pre_ffn", "g_post_ffn")
BIG = ("w_in", "w_mem_kv", "w_out", "w_gate", "w_up", "w_down")
TRANSPOSED = ("w_in", "w_gate", "w_up")
WEIGHTS = ("g_pre_mix", "w_in", "b_f", "g_sgu", "w_s", "b_s", "g_out_a", "g_out_b", "g_out_m", "g_mem", "w_mem_kv",
           "w_out", "g_post_mix", "g_pre_ffn", "w_gate", "w_up", "w_down", "g_post_ffn")


def _rows_of(n):
    return -(-n // (8 * LANES)) * 8


def _pack(parts):
    tiles = []
    for a in parts:
        flat = a.reshape(-1).astype(F32)
        rows = _rows_of(flat.shape[0])
        tiles.append(jnp.pad(flat, (0, rows * LANES - flat.shape[0])).reshape(rows, LANES))
    total = sum(t.shape[0] for t in tiles)
    pad = -total % 64
    if pad:
        tiles.append(jnp.zeros((pad, LANES), F32))
    return jnp.concatenate(tiles, axis=0)


def _unpack(packed, shapes):
    out, r = [], 0
    for shp in shapes:
        n = 1
        for s in shp:
            n *= s
        rows = _rows_of(n)
        out.append(packed[r:r + rows].reshape(-1)[:n].reshape(shp))
        r += rows
    return out


def kernel(x, mem, g_pre_mix, w_in, b_f, g_sgu, w_s, b_s, g_out_a, g_out_b, g_out_m, g_mem, w_mem_kv, w_out, g_post_mix, g_pre_ffn, w_gate, w_up, w_down, g_post_ffn, loss_target, m_g_pre_mix, m_w_in, m_b_f, m_g_sgu, m_w_s, m_b_s, m_g_out_a, m_g_out_b, m_g_out_m, m_g_mem, m_w_mem_kv, m_w_out, m_g_post_mix, m_g_pre_ffn, m_w_gate, m_w_up, m_w_down, m_g_post_ffn, v_g_pre_mix, v_w_in, v_b_f, v_g_sgu, v_w_s, v_b_s, v_g_out_a, v_g_out_b, v_g_out_m, v_g_mem, v_w_mem_kv, v_w_out, v_g_post_mix, v_g_pre_ffn, v_w_gate, v_w_up, v_w_down, v_g_post_ffn):
    Wt = dict(g_pre_mix=g_pre_mix, w_in=w_in, b_f=b_f, g_sgu=g_sgu, w_s=w_s, b_s=b_s, g_out_a=g_out_a, g_out_b=g_out_b,
              g_out_m=g_out_m, g_mem=g_mem, w_mem_kv=w_mem_kv, w_out=w_out, g_post_mix=g_post_mix, g_pre_ffn=g_pre_ffn,
              w_gate=w_gate, w_up=w_up, w_down=w_down, g_post_ffn=g_post_ffn)
    Mo = dict(g_pre_mix=m_g_pre_mix, w_in=m_w_in, b_f=m_b_f, g_sgu=m_g_sgu, w_s=m_w_s, b_s=m_b_s, g_out_a=m_g_out_a,
              g_out_b=m_g_out_b, g_out_m=m_g_out_m, g_mem=m_g_mem, w_mem_kv=m_w_mem_kv, w_out=m_w_out,
              g_post_mix=m_g_post_mix, g_pre_ffn=m_g_pre_ffn, w_gate=m_w_gate, w_up=m_w_up, w_down=m_w_down,
              g_post_ffn=m_g_post_ffn)
    Vo = dict(g_pre_mix=v_g_pre_mix, w_in=v_w_in, b_f=v_b_f, g_sgu=v_g_sgu, w_s=v_w_s, b_s=v_b_s, g_out_a=v_g_out_a,
              g_out_b=v_g_out_b, g_out_m=v_g_out_m, g_mem=v_g_mem, w_mem_kv=v_w_mem_kv, w_out=v_w_out,
              g_post_mix=v_g_post_mix, g_pre_ffn=v_g_pre_ffn, w_gate=v_w_gate, w_up=v_w_up, w_down=v_w_down,
              g_post_ffn=v_g_post_ffn)

    gap = P_COLS - IN_COLS

    def to_kernel(n, w):
        if n in TRANSPOSED:
            w = w.T
        if n == "w_in":
            w = jnp.pad(w[:F_END], ((0, P_COLS - F_END), (0, 0))) + jnp.pad(w[F_END:], ((F_END + gap, 0), (0, 0)))
        return w

    def ungroup(g):
        return jnp.pad(g[:F_END], ((0, IN_COLS - F_END), (0, 0))) + jnp.pad(g[F_END + gap:], ((F_END, 0), (0, 0)))

    shards = {n: to_kernel(n, Wt[n][0]) for n in BIG}
    srcs = [shards[n].astype(BF16).reshape(2, shards[n].shape[0] // 2, shards[n].shape[1]) for n in BIG]
    fulls = (_exchange_on_sequencer(srcs[:1], True, "gather_w_in", 1)
             + _exchange_on_sequencer(srcs[1:3], True, "gather_kv_out", 2)
             + _exchange_on_sequencer(srcs[3:], True, "gather_ffn", 3))
    W = {}
    for n, f in zip(BIG, fulls):
        _, _, hr, C = f.shape
        W[n] = f.reshape(8 * hr, C) if n in ("w_mem_kv", "w_out") else f.reshape(4, 2 * hr, C)

    P = {n: Wt[n] for n in SMALL}
    core = lax.axis_index("c").astype(jnp.int32).reshape(1)
    grad_x, landed, small = _local_step(x, mem, loss_target, W, P, _Reducer(core))

    grads, deltas, new_m, new_v = {}, {}, {}, {}
    for n in BIG:
        wmv = [a[n][0].T if n in TRANSPOSED else a[n][0] for a in (Wt, Mo, Vo)]
        if n == "w_in":
            g = ungroup(_sum_chips(landed[n], "sum_chips_" + n))
            g, d, m1, v1 = (g,) + tuple(_adamw(wmv[0], g, wmv[1], wmv[2], "adamw_" + n))
        else:
            g, d, m1, v1 = _adamw(wmv[0], landed[n], wmv[1], wmv[2], "adamw_" + n)
        if n in TRANSPOSED:
            g, d, m1, v1 = g.T, d.T, m1.T, v1.T
        grads[n], deltas[n], new_m[n], new_v[n] = g[None], d[None], m1[None], v1[None]

    total = _small_allreduce(_pack([small[n] for n in SMALL] + [small["loss"]]))
    slot = [jnp.zeros((1, 1), F32)]
    shapes = [Wt[n].shape for n in SMALL] + [(1, 1)]
    d, m1, v1 = _adamw(_pack([Wt[n] for n in SMALL] + slot), total, _pack([Mo[n] for n in SMALL] + slot),
                       _pack([Vo[n] for n in SMALL] + slot), "adamw_small")
    g_s, d_s, m_s, v_s = _unpack(total, shapes), _unpack(d, shapes), _unpack(m1, shapes), _unpack(v1, shapes)
    for k, n in enumerate(SMALL):
        grads[n], deltas[n], new_m[n], new_v[n] = g_s[k], d_s[k], m_s[k], v_s[k]
    loss = g_s[-1][0, 0]

    return (loss, grad_x, *[grads[n] for n in WEIGHTS], *[deltas[n] for n in WEIGHTS],
            *[new_m[n] for n in WEIGHTS], *[new_v[n] for n in WEIGHTS])
```

```python
import functools

import jax
import jax.numpy as jnp
from jax import lax
from jax.experimental import pallas as pl
from jax.experimental.pallas import tpu as pltpu
from jax.experimental.pallas import tpu_sc as plsc

F32 = jnp.float32
BF16 = jnp.bfloat16
EPS = 1e-6
NEG = -1e30
HEAD = 64
A_W, B_W, M_W = 384, 384, 256
N_FOX_HEADS = 6
CHUNK = 128
IN_COLS = 2 * A_W + 3 * B_W + N_FOX_HEADS + M_W
P_MAIN = 2 * A_W + 3 * B_W + M_W
P_COLS = P_MAIN + 128
F_END = 2 * A_W + 3 * B_W + N_FOX_HEADS
LANES = 128
Q_BLK, K_BLK = 256, 128
ADAM_LR, ADAM_B1, ADAM_B2, ADAM_EPS, ADAM_WD, ADAM_STEP = 0.001, 0.9, 0.999, 1e-08, 0.01, 10
VMEM_LIMIT = 56 * 1024 * 1024
MESH = pl.DeviceIdType.MESH
ANY = pl.BlockSpec(memory_space=pl.ANY)
BS = pl.BlockSpec


def _cp(sem=None):
    return pltpu.CompilerParams(dimension_semantics=sem, vmem_limit_bytes=VMEM_LIMIT)


def _iota(shape, dim):
    return lax.broadcasted_iota(jnp.int32, shape, dim)


def _dot(a, b):
    return jnp.dot(a.astype(BF16), b.astype(BF16), preferred_element_type=F32)


def _dot_nt(a, b):
    return lax.dot_general(a.astype(BF16), b.astype(BF16), (((1,), (1,)), ((), ())), preferred_element_type=F32)


def _dot_tn(a, b):
    return lax.dot_general(a.astype(BF16), b.astype(BF16), (((0,), (0,)), ((), ())), preferred_element_type=F32)


def _rms(x, g):
    return x * lax.rsqrt(jnp.mean(x * x, axis=-1, keepdims=True) + EPS) * g


def _gelu(x):
    return 0.5 * x * (1.0 + jnp.tanh(0.7978845608028654 * (x + 0.044715 * (x * x * x))))


def _sigmoid(x):
    return 1.0 / (1.0 + jnp.exp(-x))


def _silu_mul(g, u):
    return g * _sigmoid(g) * u


def _logsig(x):
    return jnp.minimum(x, 0.0) - jnp.log(1.0 + jnp.exp(-jnp.abs(x)))


def _colsum(x):
    return jnp.sum(x, axis=0, keepdims=True)


def _acc(ref, val, first):
    @pl.when(first)
    def _():
        ref[...] = val

    @pl.when(jnp.logical_not(first))
    def _():
        ref[...] += val


def _inproj_fwd(x2d, g_pre, w_in_p, tm):
    T, D = x2d.shape
    nchunk = P_COLS // 384
    ns, _, dsh = w_in_p.shape

    def body(x_ref, g_ref, w_ref, h_ref, proj_ref, fl_ref):
        h = _rms(x_ref[...], g_ref[...]).astype(BF16)
        h_ref[...] = h
        for n in range(nchunk):
            r = _dot_nt(h[:, 0:dsh], w_ref[0, n * 384:(n + 1) * 384, :])
            for s in range(1, ns):
                r = r + _dot_nt(h[:, s * dsh:(s + 1) * dsh], w_ref[s, n * 384:(n + 1) * 384, :])
            if n < nchunk - 1:
                proj_ref[:, n * 384:(n + 1) * 384] = r.astype(BF16)
            else:
                fl_ref[...] = r[:, :LANES]
                proj_ref[:, n * 384:n * 384 + M_W] = r[:, LANES:].astype(BF16)

    return pl.pallas_call(
        body, name="inproj_fwd", grid=(T // tm,),
        in_specs=[BS((tm, D), lambda i: (i, 0)), BS((1, D), lambda i: (0, 0)),
                  BS((ns, P_COLS, dsh), lambda i: (0, 0, 0))],
        out_specs=[BS((tm, D), lambda i: (i, 0)), BS((tm, P_MAIN), lambda i: (i, 0)), BS((tm, LANES), lambda i: (i, 0))],
        out_shape=[jax.ShapeDtypeStruct((T, D), BF16), jax.ShapeDtypeStruct((T, P_MAIN), BF16),
                   jax.ShapeDtypeStruct((T, LANES), F32)],
        compiler_params=_cp(("arbitrary",)),
    )(x2d, g_pre, w_in_p)


def _gate_fwd(flog3, bf_row):
    Bl, S, _ = flog3.shape
    nb = S // LANES

    def body(f_ref, b_ref, bq_ref, bk_ref, fr_ref):
        row = _iota((LANES, LANES), 0)
        lane = _iota((LANES, LANES), 1)
        one = jnp.ones((LANES, LANES), BF16)
        zero = jnp.zeros((LANES, LANES), BF16)

        def blk(j, carry):
            r0 = pl.multiple_of(j * LANES, LANES)
            fl = f_ref[0, pl.ds(r0, LANES), :] + b_ref[...]
            fr_ref[0, j] = fl.T[0:8, :]
            c = _logsig(fl)
            for k in (1, 2, 4, 8, 16, 32, 64):
                c = c + jnp.where(row >= k, pltpu.roll(c, k, 0), 0.0)
            c = c + carry
            for h in range(N_FOX_HEADS):
                col = jnp.sum(jnp.where(lane == h, c, 0.0), axis=1, keepdims=True)
                hi = col.astype(BF16)
                rest = col - hi.astype(F32)
                mid = rest.astype(BF16)
                lo = (rest - mid.astype(F32)).astype(BF16)
                base = _bias_lane(h)
                bq = jnp.where(lane == base, hi, jnp.where(lane == base + 1, mid, jnp.where(lane == base + 2, lo, zero)))
                bq = jnp.where((lane >= base + 3) & (lane < base + 6), one, bq)
                bk = jnp.where(lane == base + 3, -hi, jnp.where(lane == base + 4, -mid, jnp.where(lane == base + 5, -lo, zero)))
                bk = jnp.where((lane >= base) & (lane < base + 3), one, bk)
                bq_ref[0, h, pl.ds(r0, LANES), :] = bq
                bk_ref[0, h, pl.ds(r0, LANES), :] = bk
            return _colsum(jnp.where(row == LANES - 1, c, 0.0))

        lax.fori_loop(0, nb, blk, jnp.zeros((1, LANES), F32))

    slab = BS((1, N_FOX_HEADS, S, LANES), lambda b: (b, 0, 0, 0))
    return pl.pallas_call(
        body, name="gate_fwd", grid=(Bl,),
        in_specs=[BS((1, S, LANES), lambda b: (b, 0, 0)), BS((1, LANES), lambda b: (0, 0))],
        out_specs=[slab, slab, BS((1, nb, 8, LANES), lambda b: (b, 0, 0, 0))],
        out_shape=[jax.ShapeDtypeStruct((Bl, N_FOX_HEADS, S, LANES), BF16),
                   jax.ShapeDtypeStruct((Bl, N_FOX_HEADS, S, LANES), BF16),
                   jax.ShapeDtypeStruct((Bl, nb, 8, LANES), F32)],
        compiler_params=_cp(("arbitrary",)),
    )(flog3, bf_row)


def _bias_lane(h):
    return HEAD if h % 2 == 0 else 0


def _sgu_pre(zu, zv, g_sgu):
    return _gelu(zu), _rms(_gelu(zv), g_sgu)


def _sgu_fwd(proj, g_sgu, ws_tril, bs_full, tm):
    T = proj.shape[0]
    nch = tm // CHUNK

    def body(zu_ref, zv_ref, g_ref, ws_ref, b_ref, ya_ref):
        lane = _iota((CHUNK, LANES), 1)
        u, vn = _sgu_pre(zu_ref[...].astype(F32), zv_ref[...].astype(F32), g_ref[...])
        vn = vn.astype(BF16)
        for c in range(nch):
            rs = slice(c * CHUNK, (c + 1) * CHUNK)
            for j in range(3):
                cs = slice(j * LANES, (j + 1) * LANES)
                vp = vn[rs, cs]
                z = jnp.where(lane < HEAD, _dot(ws_ref[2 * j], vp), _dot(ws_ref[2 * j + 1], vp)) + b_ref[:, cs]
                ya_ref[rs, cs] = u[rs, cs] * z

    return pl.pallas_call(
        body, name="sgu_fwd", grid=(T // tm,),
        in_specs=[BS((tm, A_W), lambda i: (i, 0)), BS((tm, A_W), lambda i: (i, 1)), BS((1, A_W), lambda i: (0, 0)),
                  BS((6, CHUNK, CHUNK), lambda i: (0, 0, 0)), BS((CHUNK, A_W), lambda i: (0, 0))],
        out_specs=BS((tm, A_W), lambda i: (i, 0)),
        out_shape=jax.ShapeDtypeStruct((T, A_W), F32),
        compiler_params=_cp(("arbitrary",)),
    )(proj, proj, g_sgu, ws_tril, bs_full)


def _fox_fwd(proj, bq, bk, Bl, S):
    T = Bl * S
    nq = S // Q_BLK
    qc, kc, vc = 768 // LANES, 1152 // LANES, 1536 // LANES

    def body(q_ref, k_ref, v_ref, bq_ref, bk_ref, o_ref, lse_ref, ka_ref, va_ref):
        lane_s = _iota((S, LANES), 1)
        lane = _iota((Q_BLK, LANES), 1)
        tri = _iota((Q_BLK, Q_BLK), 1) <= _iota((Q_BLK, Q_BLK), 0)
        k = k_ref[...]
        v = v_ref[...]
        for hh in range(2):
            data = (lane_s < HEAD) if hh == 0 else (lane_s >= HEAD)
            ka_ref[hh] = jnp.where(data, k, bk_ref[0, hh])
            va_ref[hh] = jnp.where(lane_s == _bias_lane(hh), jnp.ones_like(v), v)
        for i in range(nq):
            r0 = i * Q_BLK
            q = q_ref[r0:r0 + Q_BLK, :]
            o_out = jnp.zeros((Q_BLK, LANES), F32)
            lse_out = jnp.zeros((Q_BLK, LANES), F32)
            for hh in range(2):
                hmask = (lane < HEAD) if hh == 0 else (lane >= HEAD)
                qa = jnp.where(hmask, q * 0.125, bq_ref[0, hh, r0:r0 + Q_BLK, :])
                sd = jnp.where(tri, _dot_nt(qa, ka_ref[hh, r0:r0 + Q_BLK, :]), NEG)
                m = jnp.max(sd, axis=1, keepdims=True)
                if i:
                    sf = _dot_nt(qa, ka_ref[hh, 0:r0, :])
                    m = jnp.maximum(m, jnp.max(sf, axis=1, keepdims=True))
                acc = _dot(jnp.exp(sd - m), va_ref[hh, r0:r0 + Q_BLK, :])
                if i:
                    acc = acc + _dot(jnp.exp(sf - m), va_ref[hh, 0:r0, :])
                l = jnp.sum(jnp.where(lane == _bias_lane(hh), acc, 0.0), axis=1, keepdims=True)
                o_out = jnp.where(hmask, acc / l, o_out)
                lse_out = jnp.where(hmask, m + jnp.log(l), lse_out)
            o_ref[r0:r0 + Q_BLK, :] = o_out
            lse_ref[0, r0:r0 + Q_BLK, :] = lse_out

    seq = lambda c0: BS((S, LANES), lambda b, p: (b, c0 + p))
    pair = BS((1, 2, S, LANES), lambda b, p: (b, p, 0, 0))
    return pl.pallas_call(
        body, name="fox_fwd", grid=(Bl, 3),
        in_specs=[seq(qc), seq(kc), seq(vc), pair, pair],
        out_specs=[seq(0), BS((1, S, LANES), lambda b, p: (p, b, 0))],
        out_shape=[jax.ShapeDtypeStruct((T, B_W), F32), jax.ShapeDtypeStruct((3, T, LANES), F32)],
        scratch_shapes=[pltpu.VMEM((2, S, LANES), BF16), pltpu.VMEM((2, S, LANES), BF16)],
        compiler_params=_cp(("arbitrary", "arbitrary")),
    )(proj, proj, proj, bq, bk)


def _memkv_fwd(mem, g_mem, w_kv):
    Bl, Mt, D = mem.shape

    def body(m_ref, g_ref, w_ref, mn_ref, kv_ref):
        mn = _rms(m_ref[0], g_ref[...]).astype(BF16)
        mn_ref[0] = mn
        kv_ref[0] = jnp.dot(mn, w_ref[...], preferred_element_type=F32).astype(BF16)

    return pl.pallas_call(
        body, name="memkv_fwd", grid=(Bl,),
        in_specs=[BS((1, Mt, D), lambda b: (b, 0, 0)), BS((1, D), lambda b: (0, 0)), BS((D, 2 * M_W), lambda b: (0, 0))],
        out_specs=[BS((1, Mt, D), lambda b: (b, 0, 0)), BS((1, Mt, 2 * M_W), lambda b: (b, 0, 0))],
        out_shape=[jax.ShapeDtypeStruct((Bl, Mt, D), BF16), jax.ShapeDtypeStruct((Bl, Mt, 2 * M_W), BF16)],
        compiler_params=_cp(("arbitrary",)),
    )(mem, g_mem, w_kv)


def _memattn_fwd(proj, kv, Bl, S, tq):
    T = Bl * S
    nq = S // tq
    Mt = kv.shape[1]
    qc = 1920 // LANES

    def body(q_ref, km_ref, vm_ref, o_ref):
        lane = _iota((tq, LANES), 1)
        q = q_ref[...]
        out = jnp.zeros((tq, LANES), F32)
        for hh in range(2):
            hmask = (lane < HEAD) if hh == 0 else (lane >= HEAD)
            qs = jnp.where(hmask, q, jnp.zeros_like(q)) * 0.125
            s = _dot_nt(qs, km_ref[0])
            pe = jnp.exp(s - jnp.max(s, axis=1, keepdims=True))
            pn = pe / jnp.sum(pe, axis=1, keepdims=True)
            out = jnp.where(hmask, _dot(pn, vm_ref[0]), out)
        o_ref[...] = out

    return pl.pallas_call(
        body, name="memattn_fwd", grid=(Bl, 2, nq),
        in_specs=[BS((tq, LANES), lambda b, p, i: (b * nq + i, qc + p)),
                  BS((1, Mt, LANES), lambda b, p, i: (b, 0, p)),
                  BS((1, Mt, LANES), lambda b, p, i: (b, 0, 2 + p))],
        out_specs=BS((tq, LANES), lambda b, p, i: (b * nq + i, p)),
        out_shape=jax.ShapeDtypeStruct((T, M_W), F32),
        compiler_params=_cp(("arbitrary", "arbitrary", "arbitrary")),
    )(proj, kv, kv)


def _mix_norms(ya, yb, ym, ga, gb, gm):
    return _rms(ya, ga), _rms(yb, gb), _rms(ym, gm)


def _outproj_fwd(ya, yb, ym, x2d, ga, gb, gm, g_post, g_pre2, w_out, tm):
    T, D = x2d.shape

    def body(ya_ref, yb_ref, ym_ref, x_ref, ga_ref, gb_ref, gm_ref, gp_ref, g2_ref, w_ref,
             y_ref, o_ref, x1_ref, h2_ref):
        na, nb_, nm = _mix_norms(ya_ref[...], yb_ref[...], ym_ref[...], ga_ref[...], gb_ref[...], gm_ref[...])
        y_ref[:, 0:A_W] = na.astype(BF16)
        y_ref[:, A_W:A_W + B_W] = nb_.astype(BF16)
        y_ref[:, A_W + B_W:] = nm.astype(BF16)
        o = jnp.dot(y_ref[...], w_ref[...], preferred_element_type=F32)
        o_ref[...] = o
        x1 = x_ref[...] + _rms(o, gp_ref[...])
        x1_ref[...] = x1
        h2_ref[...] = _rms(x1, g2_ref[...]).astype(BF16)

    row = lambda w: BS((tm, w), lambda i: (i, 0))
    vec = lambda w: BS((1, w), lambda i: (0, 0))
    return pl.pallas_call(
        body, name="outproj_fwd", grid=(T // tm,),
        in_specs=[row(A_W), row(B_W), row(M_W), row(D), vec(A_W), vec(B_W), vec(M_W), vec(D), vec(D),
                  BS((A_W + B_W + M_W, D), lambda i: (0, 0))],
        out_specs=[row(A_W + B_W + M_W), row(D), row(D), row(D)],
        out_shape=[jax.ShapeDtypeStruct((T, A_W + B_W + M_W), BF16), jax.ShapeDtypeStruct((T, D), F32),
                   jax.ShapeDtypeStruct((T, D), F32), jax.ShapeDtypeStruct((T, D), BF16)],
        compiler_params=_cp(("arbitrary",)),
    )(ya, yb, ym, x2d, ga, gb, gm, g_post, g_pre2, w_out)


def _ffn_fwd(h2, x1, target, wg, wu, wd, g_post, tm):
    T, D = x1.shape
    ns, F, _ = wg.shape

    def body(h_ref, x1_ref, t_ref, wg_ref, wu_ref, wd_ref, gp_ref,
             gs_ref, us_ref, dff_ref, dx2_ref, dgp_ref, loss_ref, acc_ref):
        i = pl.program_id(0)
        j = pl.program_id(1)
        h = h_ref[...]
        g = _dot_nt(h, wg_ref[0])
        u = _dot_nt(h, wu_ref[0])
        gs_ref[0] = g.astype(BF16)
        us_ref[0] = u.astype(BF16)
        part = _dot(_silu_mul(g, u), wd_ref[0])
        _acc(acc_ref, part, j == 0)

        @pl.when(j == ns - 1)
        def _():
            normed, vjp = jax.vjp(_rms, acc_ref[...], gp_ref[...])
            diff = x1_ref[...] + normed - t_ref[...]
            dx2 = diff * (1.0 / D)
            dff, dgp = vjp(dx2)
            dx2_ref[...] = dx2
            dff_ref[...] = dff.astype(BF16)
            lpart = jnp.sum(_colsum(diff * diff), axis=1, keepdims=True) * (0.5 / D)
            _acc(dgp_ref, dgp, i == 0)
            _acc(loss_ref, jnp.broadcast_to(lpart, (1, LANES)), i == 0)

    row = lambda w: BS((tm, w), lambda i, j: (i, 0))
    return pl.pallas_call(
        body, name="ffn_fwd", grid=(T // tm, ns),
        in_specs=[row(D), row(D), row(D), BS((1, F, D), lambda i, j: (j, 0, 0)), BS((1, F, D), lambda i, j: (j, 0, 0)),
                  BS((1, F, D), lambda i, j: (j, 0, 0)), BS((1, D), lambda i, j: (0, 0))],
        out_specs=[BS((1, tm, F), lambda i, j: (j, i, 0)), BS((1, tm, F), lambda i, j: (j, i, 0)), row(D), row(D),
                   BS((1, D), lambda i, j: (0, 0)), BS((1, LANES), lambda i, j: (0, 0))],
        out_shape=[jax.ShapeDtypeStruct((ns, T, F), BF16), jax.ShapeDtypeStruct((ns, T, F), BF16),
                   jax.ShapeDtypeStruct((T, D), BF16), jax.ShapeDtypeStruct((T, D), F32),
                   jax.ShapeDtypeStruct((1, D), F32), jax.ShapeDtypeStruct((1, LANES), F32)],
        scratch_shapes=[pltpu.VMEM((tm, D), F32)],
        compiler_params=_cp(("arbitrary", "arbitrary")),
    )(h2, x1, target, wg, wu, wd, g_post)


def _ffn_bwd(dff, h2, gs, us, wg, wu, wd, tm):
    T, D = h2.shape
    ns, F, _ = wg.shape

    def body(dff_ref, h_ref, gs_ref, us_ref, wg_ref, wu_ref, wd_ref, dh_ref, dwg_ref, dwu_ref, dwd_ref):
        first = pl.program_id(1) == 0
        dff = dff_ref[...]
        h = h_ref[...]
        dact = _dot_nt(dff, wd_ref[0])
        g = gs_ref[0].astype(F32)
        u = us_ref[0].astype(F32)
        sig = _sigmoid(g)
        gsig = g * sig
        a = gsig * u
        dg = (dact * u * (sig + gsig * (1.0 - sig))).astype(BF16)
        du = (dact * gsig).astype(BF16)
        dh_ref[0] = (_dot(dg, wg_ref[0]) + _dot(du, wu_ref[0])).astype(BF16)
        _acc(dwd_ref, _dot_tn(a, dff)[None], first)
        _acc(dwg_ref, _dot_tn(dg, h)[None], first)
        _acc(dwu_ref, _dot_tn(du, h)[None], first)

    row = BS((tm, D), lambda j, i: (i, 0))
    sh = BS((1, tm, F), lambda j, i: (j, i, 0))
    wsh = BS((1, F, D), lambda j, i: (j, 0, 0))
    return pl.pallas_call(
        body, name="ffn_bwd", grid=(ns, T // tm),
        in_specs=[row, row, sh, sh, wsh, wsh, wsh],
        out_specs=[BS((1, tm, D), lambda j, i: (j, i, 0)), wsh, wsh, wsh],
        out_shape=[jax.ShapeDtypeStruct((ns, T, D), BF16)] + [jax.ShapeDtypeStruct((ns, F, D), F32)] * 3,
        compiler_params=_cp(("arbitrary", "arbitrary")),
    )(dff, h2, gs, us, wg, wu, wd)


def _mm_tn(a, b, name, tk):
    T, M = a.shape
    N = b.shape[1]
    tk = min(tk, T)

    def body(a_ref, b_ref, o_ref):
        _acc(o_ref, _dot_tn(a_ref[...], b_ref[...]), pl.program_id(0) == 0)

    return pl.pallas_call(
        body, name=name, grid=(T // tk,),
        in_specs=[BS((tk, M), lambda t: (t, 0)), BS((tk, N), lambda t: (t, 0))],
        out_specs=BS((M, N), lambda t: (0, 0)),
        out_shape=jax.ShapeDtypeStruct((M, N), F32),
        compiler_params=_cp(("arbitrary",)),
    )(a, b)


def _dw_in(dproj, h, ns, tk):
    T, M = dproj.shape
    D = h.shape[1]
    dsh = D // ns
    tk = min(tk, T)

    def body(a_ref, b_ref, o_ref, acc_ref):
        t = pl.program_id(0)
        _acc(acc_ref, _dot_tn(b_ref[...], a_ref[...]), t == 0)

        @pl.when(t == pl.num_programs(0) - 1)
        def _():
            for s in range(ns):
                o_ref[s] = acc_ref[s * dsh:(s + 1) * dsh, :].T

    return pl.pallas_call(
        body, name="dw_in", grid=(T // tk,),
        in_specs=[BS((tk, M), lambda t: (t, 0)), BS((tk, D), lambda t: (t, 0))],
        out_specs=BS((ns, M, dsh), lambda t: (0, 0, 0)),
        out_shape=jax.ShapeDtypeStruct((ns, M, dsh), F32),
        scratch_shapes=[pltpu.VMEM((D, M), F32)],
        compiler_params=_cp(("arbitrary",)),
    )(dproj, h)


def _outproj_bwd(dh2, x1, dx2, o, ya, yb, ym, ga, gb, gm, g_post, g_pre2, w_out, tm):
    T, D = x1.shape
    ns = dh2.shape[0]

    def body(dh_ref, x1_ref, dx2_ref, o_ref, ya_ref, yb_ref, ym_ref, ga_ref, gb_ref, gm_ref, gp_ref, g2_ref, w_ref,
             dx1_ref, do_ref, dya_ref, dyb_ref, dym_ref, dga_ref, dgb_ref, dgm_ref, dgp_ref, dg2_ref):
        first = pl.program_id(0) == 0
        dh = dh_ref[0].astype(F32)
        for j in range(1, ns):
            dh = dh + dh_ref[j].astype(F32)
        _, vjp0 = jax.vjp(_rms, x1_ref[...], g2_ref[...])
        dxa, dg2 = vjp0(dh)
        dx1 = dx2_ref[...] + dxa
        dx1_ref[...] = dx1
        _acc(dg2_ref, dg2, first)
        _, vjp = jax.vjp(_rms, o_ref[...], gp_ref[...])
        do, dgp = vjp(dx1)
        do = do.astype(BF16)
        do_ref[...] = do
        dy = _dot_nt(do, w_ref[...])
        _, vjp2 = jax.vjp(_mix_norms, ya_ref[...], yb_ref[...], ym_ref[...], ga_ref[...], gb_ref[...], gm_ref[...])
        dya, dyb, dym, dga, dgb, dgm = vjp2((dy[:, 0:A_W], dy[:, A_W:A_W + B_W], dy[:, A_W + B_W:]))
        dya_ref[...] = dya
        dyb_ref[...] = dyb
        dym_ref[...] = dym
        _acc(dga_ref, dga, first)
        _acc(dgb_ref, dgb, first)
        _acc(dgm_ref, dgm, first)
        _acc(dgp_ref, dgp, first)

    row = lambda w: BS((tm, w), lambda i: (i, 0))
    vec = lambda w: BS((1, w), lambda i: (0, 0))
    sds = jax.ShapeDtypeStruct
    return pl.pallas_call(
        body, name="outproj_bwd", grid=(T // tm,),
        in_specs=[BS((ns, tm, D), lambda i: (0, i, 0)), row(D), row(D), row(D), row(A_W), row(B_W), row(M_W),
                  vec(A_W), vec(B_W), vec(M_W), vec(D), vec(D), BS((A_W + B_W + M_W, D), lambda i: (0, 0))],
        out_specs=[row(D), row(D), row(A_W), row(B_W), row(M_W), vec(A_W), vec(B_W), vec(M_W), vec(D), vec(D)],
        out_shape=[sds((T, D), F32), sds((T, D), BF16), sds((T, A_W), F32), sds((T, B_W), F32), sds((T, M_W), F32),
                   sds((1, A_W), F32), sds((1, B_W), F32), sds((1, M_W), F32), sds((1, D), F32), sds((1, D), F32)],
        compiler_params=_cp(("arbitrary",)),
    )(dh2, x1, dx2, o, ya, yb, ym, ga, gb, gm, g_post, g_pre2, w_out)


def _sgu_bwd(proj, dya, g_sgu, ws_tril, bs_full, tm):
    T = proj.shape[0]
    nch = tm // CHUNK

    def body(zu_ref, zv_ref, dy_ref, g_ref, ws_ref, b_ref, dzu_ref, dzv_ref, dws_ref, dbs_ref, dg_ref,
             du_ref, dvn_ref, dbf_ref):
        step = pl.program_id(0)
        first = step == 0
        lane = _iota((CHUNK, LANES), 1)
        tril = _iota((CHUNK, CHUNK), 0) >= _iota((CHUNK, CHUNK), 1)
        (u, vn), vjp = jax.vjp(_sgu_pre, zu_ref[...].astype(F32), zv_ref[...].astype(F32), g_ref[...])
        vnb = vn.astype(BF16)
        dy = dy_ref[...]

        @pl.when(first)
        def _():
            dws_ref[...] = jnp.zeros_like(dws_ref)
            dbf_ref[...] = jnp.zeros_like(dbf_ref)

        for c in range(nch):
            rs = slice(c * CHUNK, (c + 1) * CHUNK)
            for j in range(3):
                cs = slice(j * LANES, (j + 1) * LANES)
                vp = vnb[rs, cs]
                z = jnp.where(lane < HEAD, _dot(ws_ref[2 * j], vp), _dot(ws_ref[2 * j + 1], vp)) + b_ref[:, cs]
                du_ref[rs, cs] = dy[rs, cs] * z
                dz = dy[rs, cs] * u[rs, cs]
                dbf_ref[:, cs] += dz
                dzb = dz.astype(BF16)
                dz0 = jnp.where(lane < HEAD, dzb, jnp.zeros_like(dzb))
                dz1 = jnp.where(lane >= HEAD, dzb, jnp.zeros_like(dzb))
                dvn_ref[rs, cs] = jnp.where(lane < HEAD, _dot_tn(ws_ref[2 * j], dzb), _dot_tn(ws_ref[2 * j + 1], dzb))
                dws_ref[2 * j] += jnp.where(tril, _dot_nt(dz0, vp), 0.0)
                dws_ref[2 * j + 1] += jnp.where(tril, _dot_nt(dz1, vp), 0.0)
        dzu, dzv, dg = vjp((du_ref[...], dvn_ref[...]))
        dzu_ref[...] = dzu.astype(BF16)
        dzv_ref[...] = dzv.astype(BF16)
        _acc(dg_ref, dg, first)

        @pl.when(step == pl.num_programs(0) - 1)
        def _():
            out = jnp.zeros((CHUNK, LANES), F32)
            for j in range(3):
                slab = dbf_ref[:, j * LANES:(j + 1) * LANES]
                lo = jnp.sum(jnp.where(lane < HEAD, slab, 0.0), axis=1, keepdims=True)
                hi = jnp.sum(jnp.where(lane >= HEAD, slab, 0.0), axis=1, keepdims=True)
                out = out + jnp.where(lane == 2 * j, lo, 0.0) + jnp.where(lane == 2 * j + 1, hi, 0.0)
            dbs_ref[...] = out

    return pl.pallas_call(
        body, name="sgu_bwd", grid=(T // tm,),
        in_specs=[BS((tm, A_W), lambda i: (i, 0)), BS((tm, A_W), lambda i: (i, 1)), BS((tm, A_W), lambda i: (i, 0)),
                  BS((1, A_W), lambda i: (0, 0)), BS((6, CHUNK, CHUNK), lambda i: (0, 0, 0)),
                  BS((CHUNK, A_W), lambda i: (0, 0))],
        out_specs=[BS((tm, A_W), lambda i: (i, 0)), BS((tm, A_W), lambda i: (i, 0)),
                   BS((6, CHUNK, CHUNK), lambda i: (0, 0, 0)), BS((CHUNK, LANES), lambda i: (0, 0)),
                   BS((1, A_W), lambda i: (0, 0))],
        out_shape=[jax.ShapeDtypeStruct((T, A_W), BF16), jax.ShapeDtypeStruct((T, A_W), BF16),
                   jax.ShapeDtypeStruct((6, CHUNK, CHUNK), F32), jax.ShapeDtypeStruct((CHUNK, LANES), F32),
                   jax.ShapeDtypeStruct((1, A_W), F32)],
        scratch_shapes=[pltpu.VMEM((tm, A_W), F32), pltpu.VMEM((tm, A_W), F32), pltpu.VMEM((CHUNK, A_W), F32)],
        compiler_params=_cp(("arbitrary",)),
    )(proj, proj, dya, g_sgu, ws_tril, bs_full)


def _memattn_bwd(proj, kv, dym, Bl, S, tq):
    T = Bl * S
    nq = S // tq
    Mt = kv.shape[1]
    qc = 1920 // LANES

    def body(q_ref, km_ref, vm_ref, do_ref, dq_ref, dkm_ref, dvm_ref):
        first = pl.program_id(2) == 0
        lane = _iota((tq, LANES), 1)
        q = q_ref[...]
        do = do_ref[...]
        dq_out = jnp.zeros((tq, LANES), F32)
        dkm = jnp.zeros((Mt, LANES), F32)
        dvm = jnp.zeros((Mt, LANES), F32)
        for hh in range(2):
            hmask = (lane < HEAD) if hh == 0 else (lane >= HEAD)
            qs = jnp.where(hmask, q, jnp.zeros_like(q)) * 0.125
            dom = jnp.where(hmask, do, 0.0).astype(BF16)
            s = _dot_nt(qs, km_ref[0])
            pe = jnp.exp(s - jnp.max(s, axis=1, keepdims=True))
            pn = pe / jnp.sum(pe, axis=1, keepdims=True)
            dp = _dot_nt(dom, vm_ref[0])
            ds = (pn * (dp - jnp.sum(pn * dp, axis=1, keepdims=True))).astype(BF16)
            dq_out = jnp.where(hmask, _dot(ds, km_ref[0]) * 0.125, dq_out)
            dkm = dkm + _dot_tn(ds, qs)
            dvm = dvm + _dot_tn(pn, dom)
        dq_ref[...] = dq_out.astype(BF16)
        _acc(dkm_ref, dkm[None], first)
        _acc(dvm_ref, dvm[None], first)

    return pl.pallas_call(
        body, name="memattn_bwd", grid=(Bl, 2, nq),
        in_specs=[BS((tq, LANES), lambda b, p, i: (b * nq + i, qc + p)),
                  BS((1, Mt, LANES), lambda b, p, i: (b, 0, p)),
                  BS((1, Mt, LANES), lambda b, p, i: (b, 0, 2 + p)),
                  BS((tq, LANES), lambda b, p, i: (b * nq + i, p))],
        out_specs=[BS((tq, LANES), lambda b, p, i: (b * nq + i, p)),
                   BS((1, Mt, LANES), lambda b, p, i: (b, 0, p)),
                   BS((1, Mt, LANES), lambda b, p, i: (b, 0, p))],
        out_shape=[jax.ShapeDtypeStruct((T, M_W), BF16), jax.ShapeDtypeStruct((Bl, Mt, M_W), F32),
                   jax.ShapeDtypeStruct((Bl, Mt, M_W), F32)],
        compiler_params=_cp(("arbitrary", "arbitrary", "arbitrary")),
    )(proj, kv, kv, dym)


def _memkv_bwd(dkm, dvm, memn, mem, g_mem, w_kv):
    Bl, Mt, D = mem.shape

    def body(dk_ref, dv_ref, mn_ref, m_ref, g_ref, w_ref, dw_ref, dg_ref):
        first = pl.program_id(0) == 0
        dk = dk_ref[0].astype(BF16)
        dv = dv_ref[0].astype(BF16)
        mn = mn_ref[0]
        dmn = _dot_nt(dk, w_ref[:, 0:M_W]) + _dot_nt(dv, w_ref[:, M_W:])
        _, vjp = jax.vjp(_rms, m_ref[0], g_ref[...])
        _, dg = vjp(dmn)
        _acc(dg_ref, dg, first)

        @pl.when(first)
        def _():
            dw_ref[...] = jnp.zeros_like(dw_ref)

        dw_ref[:, 0:M_W] += _dot_tn(mn, dk)
        dw_ref[:, M_W:] += _dot_tn(mn, dv)

    return pl.pallas_call(
        body, name="memkv_bwd", grid=(Bl,),
        in_specs=[BS((1, Mt, M_W), lambda b: (b, 0, 0)), BS((1, Mt, M_W), lambda b: (b, 0, 0)),
                  BS((1, Mt, D), lambda b: (b, 0, 0)), BS((1, Mt, D), lambda b: (b, 0, 0)),
                  BS((1, D), lambda b: (0, 0)), BS((D, 2 * M_W), lambda b: (0, 0))],
        out_specs=[BS((D, 2 * M_W), lambda b: (0, 0)), BS((1, D), lambda b: (0, 0))],
        out_shape=[jax.ShapeDtypeStruct((D, 2 * M_W), F32), jax.ShapeDtypeStruct((1, D), F32)],
        compiler_params=_cp(("arbitrary",)),
    )(dkm, dvm, memn, mem, g_mem, w_kv)


def _fox_bwd(proj, dyb, lse, bq, bk, Bl, S):
    T = Bl * S
    nq = S // Q_BLK
    nb = S // LANES
    qc, kc, vc = 768 // LANES, 1152 // LANES, 1536 // LANES

    def body(q_ref, k_ref, v_ref, do_ref, lse_ref, bq_ref, bk_ref,
             dq_ref, dk_ref, dv_ref, dcr_ref, ka_ref, dka_ref, dva_ref):
        p = pl.program_id(1)
        lane_s = _iota((S, LANES), 1)
        lane = _iota((Q_BLK, LANES), 1)
        sub = _iota((8, LANES), 0)
        tri = _iota((Q_BLK, Q_BLK), 1) <= _iota((Q_BLK, Q_BLK), 0)
        k = k_ref[...]
        for hh in range(2):
            data = (lane_s < HEAD) if hh == 0 else (lane_s >= HEAD)
            ka_ref[hh] = jnp.where(data, k, bk_ref[0, hh])
        dka_ref[...] = jnp.zeros_like(dka_ref)
        dva_ref[...] = jnp.zeros_like(dva_ref)

        @pl.when(p == 0)
        def _():
            dcr_ref[...] = jnp.zeros_like(dcr_ref)

        def add_colsums(ds, first_blk, h):
            cs = _colsum(ds)
            for jb in range(ds.shape[1] // LANES):
                dcr_ref[0, first_blk + jb] += jnp.where(sub == h, cs[:, jb * LANES:(jb + 1) * LANES], 0.0)

        for i in range(nq):
            r0 = i * Q_BLK
            r1 = r0 + Q_BLK
            q = q_ref[r0:r1, :]
            do = do_ref[r0:r1, :]
            lse_b = lse_ref[0, r0:r1, :]
            dq_out = jnp.zeros((Q_BLK, LANES), F32)
            for hh in range(2):
                hmask = (lane < HEAD) if hh == 0 else (lane >= HEAD)
                h = 2 * p + hh
                qs = jnp.where(hmask, q * 0.125, jnp.zeros_like(q))
                qa = jnp.where(hmask, q * 0.125, bq_ref[0, hh, r0:r1, :])
                dob = jnp.where(hmask, do, 0.0).astype(BF16)
                lse_h = jnp.sum(jnp.where(lane == hh * HEAD, lse_b, 0.0), axis=1, keepdims=True)
                pd = jnp.where(tri, jnp.exp(_dot_nt(qa, ka_ref[hh, r0:r1, :]) - lse_h), 0.0)
                dpd = _dot_nt(dob, v_ref[r0:r1, :])
                delta = jnp.sum(pd * dpd, axis=1, keepdims=True)
                psum = jnp.sum(pd, axis=1, keepdims=True)
                if i:
                    pf = jnp.exp(_dot_nt(qa, ka_ref[hh, 0:r0, :]) - lse_h)
                    dpf = _dot_nt(dob, v_ref[0:r0, :])
                    delta = delta + jnp.sum(pf * dpf, axis=1, keepdims=True)
                    psum = psum + jnp.sum(pf, axis=1, keepdims=True)
                delta = delta / psum
                dsd = pd * (dpd - delta)
                add_colsums(dsd, r0 // LANES, h)
                dsd = dsd.astype(BF16)
                dq_h = _dot(dsd, k_ref[r0:r1, :])
                dka_ref[r0:r1, :] += _dot_tn(dsd, qs)
                dva_ref[r0:r1, :] += _dot_tn(pd, dob)
                if i:
                    dsf = pf * (dpf - delta)
                    add_colsums(dsf, 0, h)
                    dsf = dsf.astype(BF16)
                    dq_h = dq_h + _dot(dsf, k_ref[0:r0, :])
                    dka_ref[0:r0, :] += _dot_tn(dsf, qs)
                    dva_ref[0:r0, :] += _dot_tn(pf, dob)
                dq_out = jnp.where(hmask, dq_h * 0.125, dq_out)
            dq_ref[r0:r1, :] = dq_out.astype(BF16)
        dk_ref[...] = dka_ref[...].astype(BF16)
        dv_ref[...] = dva_ref[...].astype(BF16)

    seq = lambda c0: BS((S, LANES), lambda b, p: (b, c0 + p))
    pair = BS((1, 2, S, LANES), lambda b, p: (b, p, 0, 0))
    rowblk = BS((1, nb, 8, LANES), lambda b, p: (b, 0, 0, 0))
    return pl.pallas_call(
        body, name="fox_bwd", grid=(Bl, 3),
        in_specs=[seq(qc), seq(kc), seq(vc), seq(0), BS((1, S, LANES), lambda b, p: (p, b, 0)), pair, pair],
        out_specs=[seq(0), seq(0), seq(0), rowblk],
        out_shape=[jax.ShapeDtypeStruct((T, B_W), BF16)] * 3 + [jax.ShapeDtypeStruct((Bl, nb, 8, LANES), F32)],
        scratch_shapes=[pltpu.VMEM((2, S, LANES), BF16), pltpu.VMEM((S, LANES), F32), pltpu.VMEM((S, LANES), F32)],
        compiler_params=_cp(("arbitrary", "arbitrary")),
    )(proj, proj, proj, dyb, lse, bq, bk)


def _gate_bwd(dc_row, fl_row):
    Bl, nb, _, _ = dc_row.shape

    def body(dc_ref, fl_ref, o_ref):
        lane = _iota((8, LANES), 1)

        def blk(jj, carry):
            j = nb - 1 - jj
            r = -dc_ref[0, j]
            for k in (1, 2, 4, 8, 16, 32, 64):
                r = r + jnp.where(lane < LANES - k, pltpu.roll(r, LANES - k, 1), 0.0)
            r = r + carry
            dfl = r * _sigmoid(-fl_ref[0, j])
            o_ref[0, pl.ds(pl.multiple_of(j * LANES, LANES), LANES), :] = jnp.concatenate(
                [dfl, jnp.zeros((LANES - 8, LANES), F32)], axis=0).T
            return jnp.sum(jnp.where(lane == 0, r, 0.0), axis=1, keepdims=True)

        lax.fori_loop(0, nb, blk, jnp.zeros((8, 1), F32))

    rowblk = BS((1, nb, 8, LANES), lambda b: (b, 0, 0, 0))
    return pl.pallas_call(
        body, name="gate_bwd", grid=(Bl,),
        in_specs=[rowblk, rowblk],
        out_specs=BS((1, nb * LANES, LANES), lambda b: (b, 0, 0)),
        out_shape=jax.ShapeDtypeStruct((Bl, nb * LANES, LANES), F32),
        compiler_params=_cp(("arbitrary",)),
    )(dc_row, fl_row)


def _inproj_bwd(dzu, dzv, dq, dk, dv, dqm, dfl, x2d, dx1, g_pre, w_in_p, tm):
    T, D = x2d.shape
    ns, _, dsh = w_in_p.shape

    def body(dzu_ref, dzv_ref, dq_ref, dk_ref, dv_ref, dqm_ref, dfl_ref, x_ref, dx1_ref, g_ref, w_ref,
             dp_ref, gx_ref, dg_ref, dbf_ref):
        first = pl.program_id(0) == 0
        dfl = dfl_ref[...]
        dp_ref[:, 0:384] = dzu_ref[...]
        dp_ref[:, 384:768] = dzv_ref[...]
        dp_ref[:, 768:1152] = dq_ref[...]
        dp_ref[:, 1152:1536] = dk_ref[...]
        dp_ref[:, 1536:1920] = dv_ref[...]
        dp_ref[:, 1920:2048] = dfl.astype(BF16)
        dp_ref[:, 2048:2304] = dqm_ref[...]
        dh = jnp.concatenate([_dot(dp_ref[...], w_ref[s]) for s in range(ns)], axis=1)
        _, vjp = jax.vjp(_rms, x_ref[...], g_ref[...])
        dxa, dg = vjp(dh)
        gx_ref[...] = dx1_ref[...] + dxa
        _acc(dg_ref, dg, first)
        _acc(dbf_ref, _colsum(dfl), first)

    row = lambda w: BS((tm, w), lambda i: (i, 0))
    return pl.pallas_call(
        body, name="inproj_bwd", grid=(T // tm,),
        in_specs=[row(A_W), row(A_W), row(B_W), row(B_W), row(B_W), row(M_W), row(LANES), row(D), row(D),
                  BS((1, D), lambda i: (0, 0)), BS((ns, P_COLS, dsh), lambda i: (0, 0, 0))],
        out_specs=[row(P_COLS), row(D), BS((1, D), lambda i: (0, 0)), BS((1, LANES), lambda i: (0, 0))],
        out_shape=[jax.ShapeDtypeStruct((T, P_COLS), BF16), jax.ShapeDtypeStruct((T, D), F32),
                   jax.ShapeDtypeStruct((1, D), F32), jax.ShapeDtypeStruct((1, LANES), F32)],
        compiler_params=_cp(("arbitrary",)),
    )(dzu, dzv, dq, dk, dv, dqm, dfl, x2d, dx1, g_pre, w_in_p)


def _local_step(x, mem, target, W, P, reduce=None):
    Bl, S, D = x.shape
    T = Bl * S
    tm = min(512, T)
    x2d = x.reshape(T, D)
    t2d = target.reshape(T, D)
    vec = lambda a: a.reshape(1, -1)
    bf_row = jnp.pad(P["b_f"].reshape(1, -1), ((0, 0), (0, LANES - N_FOX_HEADS)))
    tril = jnp.tril(jnp.ones((CHUNK, CHUNK), bool))
    ws_tril = jnp.where(tril[None], P["w_s"][0], 0.0).astype(BF16)
    bs_full = jnp.repeat(P["b_s"][0].T, HEAD, axis=1)
    g_pre, g_sgu = vec(P["g_pre_mix"]), vec(P["g_sgu"])
    ga, gb, gm = vec(P["g_out_a"]), vec(P["g_out_b"]), vec(P["g_out_m"])
    g_mem, g_post, g_pre2, g_post2 = vec(P["g_mem"]), vec(P["g_post_mix"]), vec(P["g_pre_ffn"]), vec(P["g_post_ffn"])

    h, proj, flog = _inproj_fwd(x2d, g_pre, W["w_in"], tm)
    bq, bk, fl_row = _gate_fwd(flog.reshape(Bl, S, LANES), bf_row)
    ya = _sgu_fwd(proj, g_sgu, ws_tril, bs_full, tm)
    yb, lse = _fox_fwd(proj, bq, bk, Bl, S)
    memn, kv = _memkv_fwd(mem, g_mem, W["w_mem_kv"])
    ym = _memattn_fwd(proj, kv, Bl, S, min(512, S))
    y, o, x1, h2 = _outproj_fwd(ya, yb, ym, x2d, ga, gb, gm, g_post, g_pre2, W["w_out"], tm)
    gs, us, dff, dx2, dg_post2, loss = _ffn_fwd(h2, x1, t2d, W["w_gate"], W["w_up"], W["w_down"], g_post2, tm)

    dh2, d_w_gate, d_w_up, d_w_down = _ffn_bwd(dff, h2, gs, us, W["w_gate"], W["w_up"], W["w_down"], tm)
    ffn = [d_w_gate, d_w_up, d_w_down]
    if reduce is not None:
        pending = reduce.begin("ffn", ffn)
    dx1, do, dya, dyb, dym, dga, dgb, dgm, dg_post, dg_pre2 = _outproj_bwd(
        dh2, x1, dx2, o, ya, yb, ym, ga, gb, gm, g_post, g_pre2, W["w_out"], tm)
    if reduce is not None:
        ffn, (do, dya, dyb, dym) = reduce.finish("ffn", pending, (do, dya, dyb, dym))
    d_w_out = _mm_tn(y, do, "dw_out", 1024)
    dzu, dzv, dws, dbs_cols, dg_sgu = _sgu_bwd(proj, dya, g_sgu, ws_tril, bs_full, tm)
    dqm, dkm, dvm = _memattn_bwd(proj, kv, dym, Bl, S, min(512, S))
    d_w_kv, dg_mem = _memkv_bwd(dkm, dvm, memn, mem, g_mem, W["w_mem_kv"])
    mid = [d_w_kv, d_w_out]
    if reduce is not None:
        pending = reduce.begin("mid", mid)
    dq, dk, dv, dc_row = _fox_bwd(proj, dyb, lse, bq, bk, Bl, S)
    if reduce is not None:
        mid, (dq, dk, dv, dc_row) = reduce.finish("mid", pending, (dq, dk, dv, dc_row))
    dfl = _gate_bwd(dc_row, fl_row).reshape(T, LANES)
    dproj, grad_x, dg_pre, dbf = _inproj_bwd(dzu, dzv, dq, dk, dv, dqm, dfl, x2d, dx1, g_pre, W["w_in"], tm)
    d_w_in = _dw_in(dproj, h, W["w_in"].shape[0], 1024)
    if reduce is not None:
        d_w_in = reduce.begin("in", [d_w_in])
    big = dict(zip(BIG, [d_w_in] + mid + ffn))
    small = {"g_pre_mix": dg_pre, "b_f": dbf[:, :N_FOX_HEADS], "g_sgu": dg_sgu, "w_s": dws, "b_s": dbs_cols[:, :N_FOX_HEADS].T,
             "g_out_a": dga, "g_out_b": dgb, "g_out_m": dgm, "g_mem": dg_mem, "g_post_mix": dg_post,
             "g_pre_ffn": dg_pre2, "g_post_ffn": dg_post2, "loss": loss[:, :1]}
    return grad_x.reshape(Bl, S, D), big, small


def _place():
    return lax.axis_index("x"), lax.axis_index("y"), lax.axis_index("c")


def _exchange_on_sequencer(srcs, own_full, name, collective_id):
    n = len(srcs)

    def body(*refs):
        src, dst = refs[:n], refs[n:2 * n]
        lsem, isend, irecv, dsend, drecv = refs[2 * n:]
        x, y, c = _place()
        oc = 1 - c
        s_me = 2 * x + y
        sib = (x, y, oc)
        chips = [(1 - x, y), (x, 1 - y), (1 - x, 1 - y)]
        barrier = pltpu.get_barrier_semaphore()
        for dev in [(cx, cy, c) for cx, cy in chips] + [sib]:
            pl.semaphore_signal(barrier, inc=1, device_id=dev, device_id_type=MESH)
        pl.semaphore_wait(barrier, 4)

        def remote(a, b, ssem, rsem, dev):
            return pltpu.make_async_remote_copy(src_ref=a, dst_ref=b, send_sem=ssem, recv_sem=rsem,
                                                device_id=dev, device_id_type=MESH)

        sends, local = [], []
        for w in range(n):
            for j, (cx, cy) in enumerate(chips):
                half = src[w].at[c] if own_full else src[w].at[2 * cx + cy]
                cp = remote(half, dst[w].at[s_me, c], isend.at[w, j], irecv.at[w, j], (cx, cy, c))
                cp.start()
                sends.append(cp)
            if own_full:
                cp = remote(src[w], dst[w].at[s_me], dsend.at[w, 3], drecv.at[w, 3], sib)
            else:
                cp = remote(src[w].at[s_me], dst[w].at[s_me, c], dsend.at[w, 3], drecv.at[w, 3], sib)
                loc = pltpu.make_async_copy(src[w].at[s_me], dst[w].at[s_me, c], lsem.at[w])
                loc.start()
                local.append(loc)
            cp.start()
            sends.append(cp)
        for w in range(n):
            for j, (cx, cy) in enumerate(chips):
                landed = dst[w].at[2 * cx + cy, c]
                remote(landed, landed, isend.at[w, j], irecv.at[w, j], (cx, cy, c)).wait_recv()
                cp = remote(landed, landed, dsend.at[w, j], drecv.at[w, j], sib)
                cp.start()
                sends.append(cp)
        for w in range(n):
            for j, (cx, cy) in enumerate(chips):
                landed = dst[w].at[2 * cx + cy, oc]
                remote(landed, landed, dsend.at[w, j], drecv.at[w, j], sib).wait_recv()
            landed = dst[w].at[s_me] if own_full else dst[w].at[s_me, oc]
            remote(landed, landed, dsend.at[w, 3], drecv.at[w, 3], sib).wait_recv()
        for cp in sends:
            cp.wait_send()
        for loc in local:
            loc.wait()

    return pl.kernel(
        body, out_type=[jax.ShapeDtypeStruct((4, 2) + s.shape[1:], s.dtype) for s in srcs],
        mesh=plsc.ScalarSubcoreMesh(axis_name="sequencer", num_cores=1), name=name,
        scratch_types=[pltpu.SemaphoreType.DMA((n,)), pltpu.SemaphoreType.DMA((n, 3)), pltpu.SemaphoreType.DMA((n, 3)),
                       pltpu.SemaphoreType.DMA((n, 4)), pltpu.SemaphoreType.DMA((n, 4))],
        compiler_params=pltpu.CompilerParams(collective_id=collective_id),
    )(*srcs)


def _sibling_swap(grads, name, collective_id):
    n = len(grads)

    def body(*refs):
        g, theirs = refs[:n], refs[n:2 * n]
        ssem, rsem = refs[2 * n:]
        x, y, c = _place()
        sib = (x, y, 1 - c)
        barrier = pltpu.get_barrier_semaphore()
        pl.semaphore_signal(barrier, inc=1, device_id=sib, device_id_type=MESH)
        pl.semaphore_wait(barrier, 1)
        cps = []
        for w in range(n):
            cp = pltpu.make_async_remote_copy(src_ref=g[w].at[:, 1 - c], dst_ref=theirs[w], send_sem=ssem.at[w],
                                              recv_sem=rsem.at[w], device_id=sib, device_id_type=MESH)
            cp.start()
            cps.append(cp)
        for cp in cps:
            cp.wait()

    return pl.kernel(
        body, out_type=[jax.ShapeDtypeStruct((4,) + g.shape[2:], g.dtype) for g in grads],
        mesh=plsc.ScalarSubcoreMesh(axis_name="sequencer", num_cores=1), name=name,
        scratch_types=[pltpu.SemaphoreType.DMA((n,)), pltpu.SemaphoreType.DMA((n,))],
        compiler_params=pltpu.CompilerParams(collective_id=collective_id),
    )(*grads)


def _add_pair(core, g, theirs, name):
    _, _, hr, C = g.shape

    def body(core_ref, g_ref, t_ref, o_ref):
        o_ref[0] = (g_ref[0, 0] + t_ref[0]).astype(BF16)

    blk = BS((1, hr, C), lambda s, core_ref: (s, 0, 0))
    return pl.pallas_call(
        body, name=name,
        grid_spec=pltpu.PrefetchScalarGridSpec(
            num_scalar_prefetch=1, grid=(4,),
            in_specs=[BS((1, 1, hr, C), lambda s, core_ref: (s, core_ref[0], 0, 0)), blk], out_specs=blk),
        out_shape=jax.ShapeDtypeStruct(theirs.shape, BF16), compiler_params=_cp(("arbitrary",)))(core, g, theirs)


def _sum_chips(r, name):
    _, _, hr, C = r.shape

    def body(r_ref, o_ref):
        o_ref[...] = ((r_ref[0, 0].astype(F32) + r_ref[1, 0].astype(F32)) + r_ref[2, 0].astype(F32)) + r_ref[3, 0].astype(F32)

    return pl.pallas_call(body, name=name, grid=(2,), in_specs=[BS((4, 1, hr, C), lambda h: (0, h, 0, 0))],
                          out_specs=BS((hr, C), lambda h: (h, 0)), out_shape=jax.ShapeDtypeStruct((2 * hr, C), F32),
                          compiler_params=_cp(("arbitrary",)))(r)


class _Reducer:
    IDS = {"ffn": (4, 5), "mid": (6, 7), "in": (8, 9)}

    def __init__(self, core):
        self.core = core

    def begin(self, tag, grads):
        g4 = [g.reshape(4, 2, -1, g.shape[-1]) for g in grads]
        return g4, _sibling_swap(g4, "swap_" + tag, self.IDS[tag][0])

    def finish(self, tag, pending, hold):
        g4, theirs = pending
        sums = [_add_pair(self.core, g, t, "chip_sum_%s_%d" % (tag, k)) for k, (g, t) in enumerate(zip(g4, theirs))]
        sums, hold = lax.optimization_barrier((sums, hold))
        return _exchange_on_sequencer(sums, False, "scatter_" + tag, self.IDS[tag][1]), hold


def _small_allreduce(part):
    R = part.shape[0]
    rs = R // 8
    masks = [(mx, my, mc) for mx in (0, 1) for my in (0, 1) for mc in (0, 1)][1:]

    def body(p_ref, o_ref, buf_ref, s1, r1, s2, r2):
        x, y, c = _place()
        d = 4 * x + 2 * y + c
        mine = pl.ds(pl.multiple_of(d * rs, 8), rs)
        peers = [((x + mx) % 2, (y + my) % 2, (c + mc) % 2) for mx, my, mc in masks]
        first, second = [], []
        for k, (px, py, pc) in enumerate(peers):
            theirs = pl.ds(pl.multiple_of((4 * px + 2 * py + pc) * rs, 8), rs)
            cp = pltpu.make_async_remote_copy(src_ref=p_ref.at[theirs, :], dst_ref=buf_ref.at[d], send_sem=s1.at[k],
                                              recv_sem=r1.at[k], device_id=(px, py, pc), device_id_type=MESH)
            cp.start()
            first.append(cp)
        buf_ref[d] = p_ref[mine, :]
        for k, (px, py, pc) in enumerate(peers):
            slot = buf_ref.at[4 * px + 2 * py + pc]
            pltpu.make_async_remote_copy(src_ref=slot, dst_ref=slot, send_sem=s1.at[k], recv_sem=r1.at[k],
                                         device_id=(px, py, pc), device_id_type=MESH).wait_recv()
        total = buf_ref[0]
        for k in range(1, 8):
            total = total + buf_ref[k]
        o_ref[mine, :] = total
        for k, (px, py, pc) in enumerate(peers):
            cp = pltpu.make_async_remote_copy(src_ref=o_ref.at[mine, :], dst_ref=o_ref.at[mine, :], send_sem=s2.at[k],
                                              recv_sem=r2.at[k], device_id=(px, py, pc), device_id_type=MESH)
            cp.start()
            second.append(cp)
        for k, (px, py, pc) in enumerate(peers):
            rows = o_ref.at[pl.ds(pl.multiple_of((4 * px + 2 * py + pc) * rs, 8), rs), :]
            pltpu.make_async_remote_copy(src_ref=rows, dst_ref=rows, send_sem=s2.at[k], recv_sem=r2.at[k],
                                         device_id=(px, py, pc), device_id_type=MESH).wait_recv()
        for cp in first + second:
            cp.wait_send()

    vm = pl.BlockSpec(memory_space=pltpu.VMEM)
    return pl.pallas_call(
        body, name="small_allreduce", in_specs=[vm], out_specs=vm, out_shape=jax.ShapeDtypeStruct(part.shape, F32),
        scratch_shapes=[pltpu.VMEM((8, rs, LANES), F32)] + [pltpu.SemaphoreType.DMA((7,))] * 4,
    )(part)


def _adamw(w, g, m, v, name):
    R, C = w.shape
    summed = g.ndim == 4
    if summed:
        tr = R // 2
    else:
        tr = R if R * C * 4 <= (1 << 21) else R // 2
        if tr % 8:
            tr = R
    c1 = 1.0 / (1.0 - ADAM_B1 ** ADAM_STEP)
    c2 = 1.0 / (1.0 - ADAM_B2 ** ADAM_STEP)

    def body(w_ref, g_ref, m_ref, v_ref, *outs):
        if summed:
            g_ = ((g_ref[0, 0].astype(F32) + g_ref[1, 0].astype(F32)) + g_ref[2, 0].astype(F32)) + g_ref[3, 0].astype(F32)
            outs[0][...] = g_
        else:
            g_ = g_ref[...]
        d_ref, mo_ref, vo_ref = outs[-3:]
        m_ = ADAM_B1 * m_ref[...] + (1.0 - ADAM_B1) * g_
        v_ = ADAM_B2 * v_ref[...] + (1.0 - ADAM_B2) * (g_ * g_)
        mo_ref[...] = m_
        vo_ref[...] = v_
        d_ref[...] = -ADAM_LR * ((m_ * c1) / (jnp.sqrt(v_ * c2) + ADAM_EPS) + ADAM_WD * w_ref[...])

    blk = BS((tr, C), lambda i: (i, 0))
    g_blk = BS((4, 1, tr, C), lambda i: (0, i, 0, 0)) if summed else blk
    nout = 4 if summed else 3
    return pl.pallas_call(body, name=name, grid=(R // tr,), in_specs=[blk, g_blk, blk, blk], out_specs=[blk] * nout,
                          out_shape=[jax.ShapeDtypeStruct((R, C), F32)] * nout,
                          compiler_params=_cp(("arbitrary",)))(w, g, m, v)


SMALL = ("g_pre_mix", "b_f", "g_sgu", "w_s", "b_s", "g_out_a", "g_out_b", "g_out_m", "g_mem", "g_post_mix",
         "g_pre_ffn", "g_post_ffn")
BIG = ("w_in", "w_mem_kv", "w_out", "w_gate", "w_up", "w_down")
TRANSPOSED = ("w_in", "w_gate", "w_up")
WEIGHTS = ("g_pre_mix", "w_in", "b_f", "g_sgu", "w_s", "b_s", "g_out_a", "g_out_b", "g_out_m", "g_mem", "w_mem_kv",
           "w_out", "g_post_mix", "g_pre_ffn", "w_gate", "w_up", "w_down", "g_post_ffn")


def _rows_of(n):
    return -(-n // (8 * LANES)) * 8


def _pack(parts):
    tiles = []
    for a in parts:
        flat = a.reshape(-1).astype(F32)
        rows = _rows_of(flat.shape[0])
        tiles.append(jnp.pad(flat, (0, rows * LANES - flat.shape[0])).reshape(rows, LANES))
    total = sum(t.shape[0] for t in tiles)
    pad = -total % 64
    if pad:
        tiles.append(jnp.zeros((pad, LANES), F32))
    return jnp.concatenate(tiles, axis=0)


def _unpack(packed, shapes):
    out, r = [], 0
    for shp in shapes:
        n = 1
        for s in shp:
            n *= s
        rows = _rows_of(n)
        out.append(packed[r:r + rows].reshape(-1)[:n].reshape(shp))
        r += rows
    return out


def kernel(x, mem, g_pre_mix, w_in, b_f, g_sgu, w_s, b_s, g_out_a, g_out_b, g_out_m, g_mem, w_mem_kv, w_out, g_post_mix, g_pre_ffn, w_gate, w_up, w_down, g_post_ffn, loss_target, m_g_pre_mix, m_w_in, m_b_f, m_g_sgu, m_w_s, m_b_s, m_g_out_a, m_g_out_b, m_g_out_m, m_g_mem, m_w_mem_kv, m_w_out, m_g_post_mix, m_g_pre_ffn, m_w_gate, m_w_up, m_w_down, m_g_post_ffn, v_g_pre_mix, v_w_in, v_b_f, v_g_sgu, v_w_s, v_b_s, v_g_out_a, v_g_out_b, v_g_out_m, v_g_mem, v_w_mem_kv, v_w_out, v_g_post_mix, v_g_pre_ffn, v_w_gate, v_w_up, v_w_down, v_g_post_ffn):
    Wt = dict(g_pre_mix=g_pre_mix, w_in=w_in, b_f=b_f, g_sgu=g_sgu, w_s=w_s, b_s=b_s, g_out_a=g_out_a, g_out_b=g_out_b,
              g_out_m=g_out_m, g_mem=g_mem, w_mem_kv=w_mem_kv, w_out=w_out, g_post_mix=g_post_mix, g_pre_ffn=g_pre_ffn,
              w_gate=w_gate, w_up=w_up, w_down=w_down, g_post_ffn=g_post_ffn)
    Mo = dict(g_pre_mix=m_g_pre_mix, w_in=m_w_in, b_f=m_b_f, g_sgu=m_g_sgu, w_s=m_w_s, b_s=m_b_s, g_out_a=m_g_out_a,
              g_out_b=m_g_out_b, g_out_m=m_g_out_m, g_mem=m_g_mem, w_mem_kv=m_w_mem_kv, w_out=m_w_out,
              g_post_mix=m_g_post_mix, g_pre_ffn=m_g_pre_ffn, w_gate=m_w_gate, w_up=m_w_up, w_down=m_w_down,
              g_post_ffn=m_g_post_ffn)
    Vo = dict(g_pre_mix=v_g_pre_mix, w_in=v_w_in, b_f=v_b_f, g_sgu=v_g_sgu, w_s=v_w_s, b_s=v_b_s, g_out_a=v_g_out_a,
              g_out_b=v_g_out_b, g_out_m=v_g_out_m, g_mem=v_g_mem, w_mem_kv=v_w_mem_kv, w_out=v_w_out,
              g_post_mix=v_g_post_mix, g_pre_ffn=v_g_pre_ffn, w_gate=v_w_gate, w_up=v_w_up, w_down=v_w_down,
              g_post_ffn=v_g_post_ffn)

    gap = P_COLS - IN_COLS

    def to_kernel(n, w):
        if n in TRANSPOSED:
            w = w.T
        if n == "w_in":
            w = jnp.pad(w[:F_END], ((0, P_COLS - F_END), (0, 0))) + jnp.pad(w[F_END:], ((F_END + gap, 0), (0, 0)))
        return w

    def ungroup(g):
        return jnp.pad(g[:F_END], ((0, IN_COLS - F_END), (0, 0))) + jnp.pad(g[F_END + gap:], ((F_END, 0), (0, 0)))

    shards = {n: to_kernel(n, Wt[n][0]) for n in BIG}
    srcs = [shards[n].astype(BF16).reshape(2, shards[n].shape[0] // 2, shards[n].shape[1]) for n in BIG]
    fulls = (_exchange_on_sequencer(srcs[:1], True, "gather_w_in", 1)
             + _exchange_on_sequencer(srcs[1:3], True, "gather_kv_out", 2)
             + _exchange_on_sequencer(srcs[3:], True, "gather_ffn", 3))
    W = {}
    for n, f in zip(BIG, fulls):
        _, _, hr, C = f.shape
        W[n] = f.reshape(8 * hr, C) if n in ("w_mem_kv", "w_out") else f.reshape(4, 2 * hr, C)

    P = {n: Wt[n] for n in SMALL}
    core = lax.axis_index("c").astype(jnp.int32).reshape(1)
    reducer = _Reducer(core)
    grad_x, landed, small = _local_step(x, mem, loss_target, W, P, reducer)

    total = _small_allreduce(_pack([small[n] for n in SMALL] + [small["loss"]]))
    (landed["w_in"],), (total,) = reducer.finish("in", landed["w_in"], (total,))

    grads, deltas, new_m, new_v = {}, {}, {}, {}
    for n in BIG:
        wmv = [a[n][0].T if n in TRANSPOSED else a[n][0] for a in (Wt, Mo, Vo)]
        if n == "w_in":
            g = ungroup(_sum_chips(landed[n], "sum_chips_" + n))
            g, d, m1, v1 = (g,) + tuple(_adamw(wmv[0], g, wmv[1], wmv[2], "adamw_" + n))
        else:
            g, d, m1, v1 = _adamw(wmv[0], landed[n], wmv[1], wmv[2], "adamw_" + n)
        if n in TRANSPOSED:
            g, d, m1, v1 = g.T, d.T, m1.T, v1.T
        grads[n], deltas[n], new_m[n], new_v[n] = g[None], d[None], m1[None], v1[None]

    slot = [jnp.zeros((1, 1), F32)]
    shapes = [Wt[n].shape for n in SMALL] + [(1, 1)]
    d, m1, v1 = _adamw(_pack([Wt[n] for n in SMALL] + slot), total, _pack([Mo[n] for n in SMALL] + slot),
                       _pack([Vo[n] for n in SMALL] + slot), "adamw_small")
    g_s, d_s, m_s, v_s = _unpack(total, shapes), _unpack(d, shapes), _unpack(m1, shapes), _unpack(v1, shapes)
    for k, n in enumerate(SMALL):
        grads[n], deltas[n], new_m[n], new_v[n] = g_s[k], d_s[k], m_s[k], v_s[k]
    loss = g_s[-1][0, 0]

    return (loss, grad_x, *[grads[n] for n in WEIGHTS], *[deltas[n] for n in WEIGHTS],
            *[new_m[n] for n in WEIGHTS], *[new_v[n] for n in WEIGHTS])
```

```python
import functools

import jax
import jax.numpy as jnp
from jax import lax
from jax.experimental import pallas as pl
from jax.experimental.pallas import tpu as pltpu
from jax.experimental.pallas import tpu_sc as plsc

F32 = jnp.float32
BF16 = jnp.bfloat16
EPS = 1e-6
NEG = -1e30
HEAD = 64
A_W, B_W, M_W = 384, 384, 256
N_FOX_HEADS = 6
CHUNK = 128
IN_COLS = 2 * A_W + 3 * B_W + N_FOX_HEADS + M_W
P_MAIN = 2 * A_W + 3 * B_W + M_W
P_COLS = P_MAIN + 128
F_END = 2 * A_W + 3 * B_W + N_FOX_HEADS
LANES = 128
Q_BLK, K_BLK = 256, 128
ADAM_LR, ADAM_B1, ADAM_B2, ADAM_EPS, ADAM_WD, ADAM_STEP = 0.001, 0.9, 0.999, 1e-08, 0.01, 10
VMEM_LIMIT = 56 * 1024 * 1024
MESH = pl.DeviceIdType.MESH
ANY = pl.BlockSpec(memory_space=pl.ANY)
BS = pl.BlockSpec


def _cp(sem=None):
    return pltpu.CompilerParams(dimension_semantics=sem, vmem_limit_bytes=VMEM_LIMIT)


def _iota(shape, dim):
    return lax.broadcasted_iota(jnp.int32, shape, dim)


def _dot(a, b):
    return jnp.dot(a.astype(BF16), b.astype(BF16), preferred_element_type=F32)


def _dot_nt(a, b):
    return lax.dot_general(a.astype(BF16), b.astype(BF16), (((1,), (1,)), ((), ())), preferred_element_type=F32)


def _dot_tn(a, b):
    return lax.dot_general(a.astype(BF16), b.astype(BF16), (((0,), (0,)), ((), ())), preferred_element_type=F32)


def _rms(x, g):
    return x * lax.rsqrt(jnp.mean(x * x, axis=-1, keepdims=True) + EPS) * g


def _gelu(x):
    return 0.5 * x * (1.0 + jnp.tanh(0.7978845608028654 * (x + 0.044715 * (x * x * x))))


def _sigmoid(x):
    return 1.0 / (1.0 + jnp.exp(-x))


def _silu_mul(g, u):
    return g * _sigmoid(g) * u


def _logsig(x):
    return jnp.minimum(x, 0.0) - jnp.log(1.0 + jnp.exp(-jnp.abs(x)))


def _colsum(x):
    return jnp.sum(x, axis=0, keepdims=True)


def _acc(ref, val, first):
    @pl.when(first)
    def _():
        ref[...] = val

    @pl.when(jnp.logical_not(first))
    def _():
        ref[...] += val


def _inproj_fwd(x2d, g_pre, w_in_p, tm):
    T, D = x2d.shape
    nchunk = P_COLS // 384
    ns, _, dsh = w_in_p.shape

    def body(x_ref, g_ref, w_ref, h_ref, proj_ref, fl_ref):
        h = _rms(x_ref[...], g_ref[...]).astype(BF16)
        h_ref[...] = h
        for n in range(nchunk):
            r = _dot_nt(h[:, 0:dsh], w_ref[0, n * 384:(n + 1) * 384, :])
            for s in range(1, ns):
                r = r + _dot_nt(h[:, s * dsh:(s + 1) * dsh], w_ref[s, n * 384:(n + 1) * 384, :])
            if n < nchunk - 1:
                proj_ref[:, n * 384:(n + 1) * 384] = r.astype(BF16)
            else:
                fl_ref[...] = r[:, :LANES]
                proj_ref[:, n * 384:n * 384 + M_W] = r[:, LANES:].astype(BF16)

    return pl.pallas_call(
        body, name="inproj_fwd", grid=(T // tm,),
        in_specs=[BS((tm, D), lambda i: (i, 0)), BS((1, D), lambda i: (0, 0)),
                  BS((ns, P_COLS, dsh), lambda i: (0, 0, 0))],
        out_specs=[BS((tm, D), lambda i: (i, 0)), BS((tm, P_MAIN), lambda i: (i, 0)), BS((tm, LANES), lambda i: (i, 0))],
        out_shape=[jax.ShapeDtypeStruct((T, D), BF16), jax.ShapeDtypeStruct((T, P_MAIN), BF16),
                   jax.ShapeDtypeStruct((T, LANES), F32)],
        compiler_params=_cp(("arbitrary",)),
    )(x2d, g_pre, w_in_p)


def _gate_fwd(flog3, bf_row):
    Bl, S, _ = flog3.shape
    nb = S // LANES

    def body(f_ref, b_ref, bq_ref, bk_ref, fr_ref):
        row = _iota((LANES, LANES), 0)
        lane = _iota((LANES, LANES), 1)
        one = jnp.ones((LANES, LANES), BF16)
        zero = jnp.zeros((LANES, LANES), BF16)

        def blk(j, carry):
            r0 = pl.multiple_of(j * LANES, LANES)
            fl = f_ref[0, pl.ds(r0, LANES), :] + b_ref[...]
            fr_ref[0, j] = fl.T[0:8, :]
            c = _logsig(fl)
            for k in (1, 2, 4, 8, 16, 32, 64):
                c = c + jnp.where(row >= k, pltpu.roll(c, k, 0), 0.0)
            c = c + carry
            for h in range(N_FOX_HEADS):
                col = jnp.sum(jnp.where(lane == h, c, 0.0), axis=1, keepdims=True)
                hi = col.astype(BF16)
                rest = col - hi.astype(F32)
                mid = rest.astype(BF16)
                lo = (rest - mid.astype(F32)).astype(BF16)
                base = _bias_lane(h)
                bq = jnp.where(lane == base, hi, jnp.where(lane == base + 1, mid, jnp.where(lane == base + 2, lo, zero)))
                bq = jnp.where((lane >= base + 3) & (lane < base + 6), one, bq)
                bk = jnp.where(lane == base + 3, -hi, jnp.where(lane == base + 4, -mid, jnp.where(lane == base + 5, -lo, zero)))
                bk = jnp.where((lane >= base) & (lane < base + 3), one, bk)
                bq_ref[0, h, pl.ds(r0, LANES), :] = bq
                bk_ref[0, h, pl.ds(r0, LANES), :] = bk
            return _colsum(jnp.where(row == LANES - 1, c, 0.0))

        lax.fori_loop(0, nb, blk, jnp.zeros((1, LANES), F32))

    slab = BS((1, N_FOX_HEADS, S, LANES), lambda b: (b, 0, 0, 0))
    return pl.pallas_call(
        body, name="gate_fwd", grid=(Bl,),
        in_specs=[BS((1, S, LANES), lambda b: (b, 0, 0)), BS((1, LANES), lambda b: (0, 0))],
        out_specs=[slab, slab, BS((1, nb, 8, LANES), lambda b: (b, 0, 0, 0))],
        out_shape=[jax.ShapeDtypeStruct((Bl, N_FOX_HEADS, S, LANES), BF16),
                   jax.ShapeDtypeStruct((Bl, N_FOX_HEADS, S, LANES), BF16),
                   jax.ShapeDtypeStruct((Bl, nb, 8, LANES), F32)],
        compiler_params=_cp(("arbitrary",)),
    )(flog3, bf_row)


def _bias_lane(h):
    return HEAD if h % 2 == 0 else 0


def _sgu_pre(zu, zv, g_sgu):
    return _gelu(zu), _rms(_gelu(zv), g_sgu)


def _sgu_fwd(proj, g_sgu, ws_tril, bs_full, tm):
    T = proj.shape[0]
    nch = tm // CHUNK

    def body(zu_ref, zv_ref, g_ref, ws_ref, b_ref, ya_ref):
        lane = _iota((CHUNK, LANES), 1)
        u, vn = _sgu_pre(zu_ref[...].astype(F32), zv_ref[...].astype(F32), g_ref[...])
        vn = vn.astype(BF16)
        for c in range(nch):
            rs = slice(c * CHUNK, (c + 1) * CHUNK)
            for j in range(3):
                cs = slice(j * LANES, (j + 1) * LANES)
                vp = vn[rs, cs]
                z = jnp.where(lane < HEAD, _dot(ws_ref[2 * j], vp), _dot(ws_ref[2 * j + 1], vp)) + b_ref[:, cs]
                ya_ref[rs, cs] = u[rs, cs] * z

    return pl.pallas_call(
        body, name="sgu_fwd", grid=(T // tm,),
        in_specs=[BS((tm, A_W), lambda i: (i, 0)), BS((tm, A_W), lambda i: (i, 1)), BS((1, A_W), lambda i: (0, 0)),
                  BS((6, CHUNK, CHUNK), lambda i: (0, 0, 0)), BS((CHUNK, A_W), lambda i: (0, 0))],
        out_specs=BS((tm, A_W), lambda i: (i, 0)),
        out_shape=jax.ShapeDtypeStruct((T, A_W), F32),
        compiler_params=_cp(("arbitrary",)),
    )(proj, proj, g_sgu, ws_tril, bs_full)


def _fox_fwd(proj, bq, bk, Bl, S):
    T = Bl * S
    nq = S // Q_BLK
    qc, kc, vc = 768 // LANES, 1152 // LANES, 1536 // LANES

    def body(q_ref, k_ref, v_ref, bq_ref, bk_ref, o_ref, lse_ref, ka_ref, va_ref):
        lane_s = _iota((S, LANES), 1)
        lane = _iota((Q_BLK, LANES), 1)
        tri = _iota((Q_BLK, Q_BLK), 1) <= _iota((Q_BLK, Q_BLK), 0)
        k = k_ref[...]
        v = v_ref[...]
        for hh in range(2):
            data = (lane_s < HEAD) if hh == 0 else (lane_s >= HEAD)
            ka_ref[hh] = jnp.where(data, k, bk_ref[0, hh])
            va_ref[hh] = jnp.where(lane_s == _bias_lane(hh), jnp.ones_like(v), v)
        for i in range(nq):
            r0 = i * Q_BLK
            q = q_ref[r0:r0 + Q_BLK, :]
            o_out = jnp.zeros((Q_BLK, LANES), F32)
            lse_out = jnp.zeros((Q_BLK, LANES), F32)
            for hh in range(2):
                hmask = (lane < HEAD) if hh == 0 else (lane >= HEAD)
                qa = jnp.where(hmask, q * 0.125, bq_ref[0, hh, r0:r0 + Q_BLK, :])
                sd = jnp.where(tri, _dot_nt(qa, ka_ref[hh, r0:r0 + Q_BLK, :]), NEG)
                m = jnp.max(sd, axis=1, keepdims=True)
                if i:
                    sf = _dot_nt(qa, ka_ref[hh, 0:r0, :])
                    m = jnp.maximum(m, jnp.max(sf, axis=1, keepdims=True))
                acc = _dot(jnp.exp(sd - m), va_ref[hh, r0:r0 + Q_BLK, :])
                if i:
                    acc = acc + _dot(jnp.exp(sf - m), va_ref[hh, 0:r0, :])
                l = jnp.sum(jnp.where(lane == _bias_lane(hh), acc, 0.0), axis=1, keepdims=True)
                o_out = jnp.where(hmask, acc / l, o_out)
                lse_out = jnp.where(hmask, m + jnp.log(l), lse_out)
            o_ref[r0:r0 + Q_BLK, :] = o_out
            lse_ref[0, r0:r0 + Q_BLK, :] = lse_out

    seq = lambda c0: BS((S, LANES), lambda b, p: (b, c0 + p))
    pair = BS((1, 2, S, LANES), lambda b, p: (b, p, 0, 0))
    return pl.pallas_call(
        body, name="fox_fwd", grid=(Bl, 3),
        in_specs=[seq(qc), seq(kc), seq(vc), pair, pair],
        out_specs=[seq(0), BS((1, S, LANES), lambda b, p: (p, b, 0))],
        out_shape=[jax.ShapeDtypeStruct((T, B_W), F32), jax.ShapeDtypeStruct((3, T, LANES), F32)],
        scratch_shapes=[pltpu.VMEM((2, S, LANES), BF16), pltpu.VMEM((2, S, LANES), BF16)],
        compiler_params=_cp(("arbitrary", "arbitrary")),
    )(proj, proj, proj, bq, bk)


def _memkv_fwd(mem, g_mem, w_kv):
    Bl, Mt, D = mem.shape

    def body(m_ref, g_ref, w_ref, mn_ref, kv_ref):
        mn = _rms(m_ref[0], g_ref[...]).astype(BF16)
        mn_ref[0] = mn
        kv_ref[0] = jnp.dot(mn, w_ref[...], preferred_element_type=F32).astype(BF16)

    return pl.pallas_call(
        body, name="memkv_fwd", grid=(Bl,),
        in_specs=[BS((1, Mt, D), lambda b: (b, 0, 0)), BS((1, D), lambda b: (0, 0)), BS((D, 2 * M_W), lambda b: (0, 0))],
        out_specs=[BS((1, Mt, D), lambda b: (b, 0, 0)), BS((1, Mt, 2 * M_W), lambda b: (b, 0, 0))],
        out_shape=[jax.ShapeDtypeStruct((Bl, Mt, D), BF16), jax.ShapeDtypeStruct((Bl, Mt, 2 * M_W), BF16)],
        compiler_params=_cp(("arbitrary",)),
    )(mem, g_mem, w_kv)


def _memattn_fwd(proj, kv, Bl, S, tq):
    T = Bl * S
    nq = S // tq
    Mt = kv.shape[1]
    qc = 1920 // LANES

    def body(q_ref, km_ref, vm_ref, o_ref):
        lane = _iota((tq, LANES), 1)
        q = q_ref[...]
        out = jnp.zeros((tq, LANES), F32)
        for hh in range(2):
            hmask = (lane < HEAD) if hh == 0 else (lane >= HEAD)
            qs = jnp.where(hmask, q, jnp.zeros_like(q)) * 0.125
            s = _dot_nt(qs, km_ref[0])
            pe = jnp.exp(s - jnp.max(s, axis=1, keepdims=True))
            pn = pe / jnp.sum(pe, axis=1, keepdims=True)
            out = jnp.where(hmask, _dot(pn, vm_ref[0]), out)
        o_ref[...] = out

    return pl.pallas_call(
        body, name="memattn_fwd", grid=(Bl, 2, nq),
        in_specs=[BS((tq, LANES), lambda b, p, i: (b * nq + i, qc + p)),
                  BS((1, Mt, LANES), lambda b, p, i: (b, 0, p)),
                  BS((1, Mt, LANES), lambda b, p, i: (b, 0, 2 + p))],
        out_specs=BS((tq, LANES), lambda b, p, i: (b * nq + i, p)),
        out_shape=jax.ShapeDtypeStruct((T, M_W), F32),
        compiler_params=_cp(("arbitrary", "arbitrary", "arbitrary")),
    )(proj, kv, kv)


def _mix_norms(ya, yb, ym, ga, gb, gm):
    return _rms(ya, ga), _rms(yb, gb), _rms(ym, gm)


def _outproj_fwd(ya, yb, ym, x2d, ga, gb, gm, g_post, g_pre2, w_out, tm):
    T, D = x2d.shape

    def body(ya_ref, yb_ref, ym_ref, x_ref, ga_ref, gb_ref, gm_ref, gp_ref, g2_ref, w_ref,
             y_ref, o_ref, x1_ref, h2_ref):
        na, nb_, nm = _mix_norms(ya_ref[...], yb_ref[...], ym_ref[...], ga_ref[...], gb_ref[...], gm_ref[...])
        y_ref[:, 0:A_W] = na.astype(BF16)
        y_ref[:, A_W:A_W + B_W] = nb_.astype(BF16)
        y_ref[:, A_W + B_W:] = nm.astype(BF16)
        o = jnp.dot(y_ref[...], w_ref[...], preferred_element_type=F32)
        o_ref[...] = o
        x1 = x_ref[...] + _rms(o, gp_ref[...])
        x1_ref[...] = x1
        h2_ref[...] = _rms(x1, g2_ref[...]).astype(BF16)

    row = lambda w: BS((tm, w), lambda i: (i, 0))
    vec = lambda w: BS((1, w), lambda i: (0, 0))
    return pl.pallas_call(
        body, name="outproj_fwd", grid=(T // tm,),
        in_specs=[row(A_W), row(B_W), row(M_W), row(D), vec(A_W), vec(B_W), vec(M_W), vec(D), vec(D),
                  BS((A_W + B_W + M_W, D), lambda i: (0, 0))],
        out_specs=[row(A_W + B_W + M_W), row(D), row(D), row(D)],
        out_shape=[jax.ShapeDtypeStruct((T, A_W + B_W + M_W), BF16), jax.ShapeDtypeStruct((T, D), F32),
                   jax.ShapeDtypeStruct((T, D), F32), jax.ShapeDtypeStruct((T, D), BF16)],
        compiler_params=_cp(("arbitrary",)),
    )(ya, yb, ym, x2d, ga, gb, gm, g_post, g_pre2, w_out)


def _ffn_fwd(h2, x1, target, wg, wu, wd, g_post, tm):
    T, D = x1.shape
    ns, F, _ = wg.shape

    def body(h_ref, x1_ref, t_ref, wg_ref, wu_ref, wd_ref, gp_ref,
             gs_ref, us_ref, dff_ref, dx2_ref, dgp_ref, loss_ref, acc_ref):
        i = pl.program_id(0)
        j = pl.program_id(1)
        h = h_ref[...]
        g = _dot_nt(h, wg_ref[0])
        u = _dot_nt(h, wu_ref[0])
        gs_ref[0] = g.astype(BF16)
        us_ref[0] = u.astype(BF16)
        part = _dot(_silu_mul(g, u), wd_ref[0])
        _acc(acc_ref, part, j == 0)

        @pl.when(j == ns - 1)
        def _():
            normed, vjp = jax.vjp(_rms, acc_ref[...], gp_ref[...])
            diff = x1_ref[...] + normed - t_ref[...]
            dx2 = diff * (1.0 / D)
            dff, dgp = vjp(dx2)
            dx2_ref[...] = dx2
            dff_ref[...] = dff.astype(BF16)
            lpart = jnp.sum(_colsum(diff * diff), axis=1, keepdims=True) * (0.5 / D)
            _acc(dgp_ref, dgp, i == 0)
            _acc(loss_ref, jnp.broadcast_to(lpart, (1, LANES)), i == 0)

    row = lambda w: BS((tm, w), lambda i, j: (i, 0))
    return pl.pallas_call(
        body, name="ffn_fwd", grid=(T // tm, ns),
        in_specs=[row(D), row(D), row(D), BS((1, F, D), lambda i, j: (j, 0, 0)), BS((1, F, D), lambda i, j: (j, 0, 0)),
                  BS((1, F, D), lambda i, j: (j, 0, 0)), BS((1, D), lambda i, j: (0, 0))],
        out_specs=[BS((1, tm, F), lambda i, j: (j, i, 0)), BS((1, tm, F), lambda i, j: (j, i, 0)), row(D), row(D),
                   BS((1, D), lambda i, j: (0, 0)), BS((1, LANES), lambda i, j: (0, 0))],
        out_shape=[jax.ShapeDtypeStruct((ns, T, F), BF16), jax.ShapeDtypeStruct((ns, T, F), BF16),
                   jax.ShapeDtypeStruct((T, D), BF16), jax.ShapeDtypeStruct((T, D), F32),
                   jax.ShapeDtypeStruct((1, D), F32), jax.ShapeDtypeStruct((1, LANES), F32)],
        scratch_shapes=[pltpu.VMEM((tm, D), F32)],
        compiler_params=_cp(("arbitrary", "arbitrary")),
    )(h2, x1, target, wg, wu, wd, g_post)


def _ffn_bwd(dff, h2, gs, us, wg, wu, wd, tm):
    T, D = h2.shape
    ns, F, _ = wg.shape

    def body(dff_ref, h_ref, gs_ref, us_ref, wg_ref, wu_ref, wd_ref, dh_ref, dwg_ref, dwu_ref, dwd_ref):
        first = pl.program_id(1) == 0
        dff = dff_ref[...]
        h = h_ref[...]
        dact = _dot_nt(dff, wd_ref[0])
        g = gs_ref[0].astype(F32)
        u = us_ref[0].astype(F32)
        sig = _sigmoid(g)
        gsig = g * sig
        a = gsig * u
        dg = (dact * u * (sig + gsig * (1.0 - sig))).astype(BF16)
        du = (dact * gsig).astype(BF16)
        dh_ref[0] = (_dot(dg, wg_ref[0]) + _dot(du, wu_ref[0])).astype(BF16)
        _acc(dwd_ref, _dot_tn(a, dff)[None], first)
        _acc(dwg_ref, _dot_tn(dg, h)[None], first)
        _acc(dwu_ref, _dot_tn(du, h)[None], first)

    row = BS((tm, D), lambda j, i: (i, 0))
    sh = BS((1, tm, F), lambda j, i: (j, i, 0))
    wsh = BS((1, F, D), lambda j, i: (j, 0, 0))
    return pl.pallas_call(
        body, name="ffn_bwd", grid=(ns, T // tm),
        in_specs=[row, row, sh, sh, wsh, wsh, wsh],
        out_specs=[BS((1, tm, D), lambda j, i: (j, i, 0)), wsh, wsh, wsh],
        out_shape=[jax.ShapeDtypeStruct((ns, T, D), BF16)] + [jax.ShapeDtypeStruct((ns, F, D), F32)] * 3,
        compiler_params=_cp(("arbitrary", "arbitrary")),
    )(dff, h2, gs, us, wg, wu, wd)


def _mm_tn(a, b, name, tk):
    T, M = a.shape
    N = b.shape[1]
    tk = min(tk, T)

    def body(a_ref, b_ref, o_ref):
        _acc(o_ref, _dot_tn(a_ref[...], b_ref[...]), pl.program_id(0) == 0)

    return pl.pallas_call(
        body, name=name, grid=(T // tk,),
        in_specs=[BS((tk, M), lambda t: (t, 0)), BS((tk, N), lambda t: (t, 0))],
        out_specs=BS((M, N), lambda t: (0, 0)),
        out_shape=jax.ShapeDtypeStruct((M, N), F32),
        compiler_params=_cp(("arbitrary",)),
    )(a, b)


def _dw_in(dproj, h, ns, tk):
    T, M = dproj.shape
    D = h.shape[1]
    dsh = D // ns
    tk = min(tk, T)

    def body(a_ref, b_ref, o_ref, acc_ref):
        t = pl.program_id(0)
        _acc(acc_ref, _dot_tn(b_ref[...], a_ref[...]), t == 0)

        @pl.when(t == pl.num_programs(0) - 1)
        def _():
            for s in range(ns):
                o_ref[s] = acc_ref[s * dsh:(s + 1) * dsh, :].T

    return pl.pallas_call(
        body, name="dw_in", grid=(T // tk,),
        in_specs=[BS((tk, M), lambda t: (t, 0)), BS((tk, D), lambda t: (t, 0))],
        out_specs=BS((ns, M, dsh), lambda t: (0, 0, 0)),
        out_shape=jax.ShapeDtypeStruct((ns, M, dsh), F32),
        scratch_shapes=[pltpu.VMEM((D, M), F32)],
        compiler_params=_cp(("arbitrary",)),
    )(dproj, h)


def _outproj_bwd(dh2, x1, dx2, o, ya, yb, ym, ga, gb, gm, g_post, g_pre2, w_out, tm):
    T, D = x1.shape
    ns = dh2.shape[0]

    def body(dh_ref, x1_ref, dx2_ref, o_ref, ya_ref, yb_ref, ym_ref, ga_ref, gb_ref, gm_ref, gp_ref, g2_ref, w_ref,
             dx1_ref, do_ref, dya_ref, dyb_ref, dym_ref, dga_ref, dgb_ref, dgm_ref, dgp_ref, dg2_ref):
        first = pl.program_id(0) == 0
        dh = dh_ref[0].astype(F32)
        for j in range(1, ns):
            dh = dh + dh_ref[j].astype(F32)
        _, vjp0 = jax.vjp(_rms, x1_ref[...], g2_ref[...])
        dxa, dg2 = vjp0(dh)
        dx1 = dx2_ref[...] + dxa
        dx1_ref[...] = dx1
        _acc(dg2_ref, dg2, first)
        _, vjp = jax.vjp(_rms, o_ref[...], gp_ref[...])
        do, dgp = vjp(dx1)
        do = do.astype(BF16)
        do_ref[...] = do
        dy = _dot_nt(do, w_ref[...])
        _, vjp2 = jax.vjp(_mix_norms, ya_ref[...], yb_ref[...], ym_ref[...], ga_ref[...], gb_ref[...], gm_ref[...])
        dya, dyb, dym, dga, dgb, dgm = vjp2((dy[:, 0:A_W], dy[:, A_W:A_W + B_W], dy[:, A_W + B_W:]))
        dya_ref[...] = dya
        dyb_ref[...] = dyb
        dym_ref[...] = dym
        _acc(dga_ref, dga, first)
        _acc(dgb_ref, dgb, first)
        _acc(dgm_ref, dgm, first)
        _acc(dgp_ref, dgp, first)

    row = lambda w: BS((tm, w), lambda i: (i, 0))
    vec = lambda w: BS((1, w), lambda i: (0, 0))
    sds = jax.ShapeDtypeStruct
    return pl.pallas_call(
        body, name="outproj_bwd", grid=(T // tm,),
        in_specs=[BS((ns, tm, D), lambda i: (0, i, 0)), row(D), row(D), row(D), row(A_W), row(B_W), row(M_W),
                  vec(A_W), vec(B_W), vec(M_W), vec(D), vec(D), BS((A_W + B_W + M_W, D), lambda i: (0, 0))],
        out_specs=[row(D), row(D), row(A_W), row(B_W), row(M_W), vec(A_W), vec(B_W), vec(M_W), vec(D), vec(D)],
        out_shape=[sds((T, D), F32), sds((T, D), BF16), sds((T, A_W), F32), sds((T, B_W), F32), sds((T, M_W), F32),
                   sds((1, A_W), F32), sds((1, B_W), F32), sds((1, M_W), F32), sds((1, D), F32), sds((1, D), F32)],
        compiler_params=_cp(("arbitrary",)),
    )(dh2, x1, dx2, o, ya, yb, ym, ga, gb, gm, g_post, g_pre2, w_out)


def _sgu_bwd(proj, dya, g_sgu, ws_tril, bs_full, tm):
    T = proj.shape[0]
    nch = tm // CHUNK

    def body(zu_ref, zv_ref, dy_ref, g_ref, ws_ref, b_ref, dzu_ref, dzv_ref, dws_ref, dbs_ref, dg_ref,
             du_ref, dvn_ref, dbf_ref):
        step = pl.program_id(0)
        first = step == 0
        lane = _iota((CHUNK, LANES), 1)
        tril = _iota((CHUNK, CHUNK), 0) >= _iota((CHUNK, CHUNK), 1)
        (u, vn), vjp = jax.vjp(_sgu_pre, zu_ref[...].astype(F32), zv_ref[...].astype(F32), g_ref[...])
        vnb = vn.astype(BF16)
        dy = dy_ref[...]

        @pl.when(first)
        def _():
            dws_ref[...] = jnp.zeros_like(dws_ref)
            dbf_ref[...] = jnp.zeros_like(dbf_ref)

        for c in range(nch):
            rs = slice(c * CHUNK, (c + 1) * CHUNK)
            for j in range(3):
                cs = slice(j * LANES, (j + 1) * LANES)
                vp = vnb[rs, cs]
                z = jnp.where(lane < HEAD, _dot(ws_ref[2 * j], vp), _dot(ws_ref[2 * j + 1], vp)) + b_ref[:, cs]
                du_ref[rs, cs] = dy[rs, cs] * z
                dz = dy[rs, cs] * u[rs, cs]
                dbf_ref[:, cs] += dz
                dzb = dz.astype(BF16)
                dz0 = jnp.where(lane < HEAD, dzb, jnp.zeros_like(dzb))
                dz1 = jnp.where(lane >= HEAD, dzb, jnp.zeros_like(dzb))
                dvn_ref[rs, cs] = jnp.where(lane < HEAD, _dot_tn(ws_ref[2 * j], dzb), _dot_tn(ws_ref[2 * j + 1], dzb))
                dws_ref[2 * j] += jnp.where(tril, _dot_nt(dz0, vp), 0.0)
                dws_ref[2 * j + 1] += jnp.where(tril, _dot_nt(dz1, vp), 0.0)
        dzu, dzv, dg = vjp((du_ref[...], dvn_ref[...]))
        dzu_ref[...] = dzu.astype(BF16)
        dzv_ref[...] = dzv.astype(BF16)
        _acc(dg_ref, dg, first)

        @pl.when(step == pl.num_programs(0) - 1)
        def _():
            out = jnp.zeros((CHUNK, LANES), F32)
            for j in range(3):
                slab = dbf_ref[:, j * LANES:(j + 1) * LANES]
                lo = jnp.sum(jnp.where(lane < HEAD, slab, 0.0), axis=1, keepdims=True)
                hi = jnp.sum(jnp.where(lane >= HEAD, slab, 0.0), axis=1, keepdims=True)
                out = out + jnp.where(lane == 2 * j, lo, 0.0) + jnp.where(lane == 2 * j + 1, hi, 0.0)
            dbs_ref[...] = out

    return pl.pallas_call(
        body, name="sgu_bwd", grid=(T // tm,),
        in_specs=[BS((tm, A_W), lambda i: (i, 0)), BS((tm, A_W), lambda i: (i, 1)), BS((tm, A_W), lambda i: (i, 0)),
                  BS((1, A_W), lambda i: (0, 0)), BS((6, CHUNK, CHUNK), lambda i: (0, 0, 0)),
                  BS((CHUNK, A_W), lambda i: (0, 0))],
        out_specs=[BS((tm, A_W), lambda i: (i, 0)), BS((tm, A_W), lambda i: (i, 0)),
                   BS((6, CHUNK, CHUNK), lambda i: (0, 0, 0)), BS((CHUNK, LANES), lambda i: (0, 0)),
                   BS((1, A_W), lambda i: (0, 0))],
        out_shape=[jax.ShapeDtypeStruct((T, A_W), BF16), jax.ShapeDtypeStruct((T, A_W), BF16),
                   jax.ShapeDtypeStruct((6, CHUNK, CHUNK), F32), jax.ShapeDtypeStruct((CHUNK, LANES), F32),
                   jax.ShapeDtypeStruct((1, A_W), F32)],
        scratch_shapes=[pltpu.VMEM((tm, A_W), F32), pltpu.VMEM((tm, A_W), F32), pltpu.VMEM((CHUNK, A_W), F32)],
        compiler_params=_cp(("arbitrary",)),
    )(proj, proj, dya, g_sgu, ws_tril, bs_full)


def _memattn_bwd(proj, kv, dym, Bl, S, tq):
    T = Bl * S
    nq = S // tq
    Mt = kv.shape[1]
    qc = 1920 // LANES

    def body(q_ref, km_ref, vm_ref, do_ref, dq_ref, dkm_ref, dvm_ref):
        first = pl.program_id(2) == 0
        lane = _iota((tq, LANES), 1)
        q = q_ref[...]
        do = do_ref[...]
        dq_out = jnp.zeros((tq, LANES), F32)
        dkm = jnp.zeros((Mt, LANES), F32)
        dvm = jnp.zeros((Mt, LANES), F32)
        for hh in range(2):
            hmask = (lane < HEAD) if hh == 0 else (lane >= HEAD)
            qs = jnp.where(hmask, q, jnp.zeros_like(q)) * 0.125
            dom = jnp.where(hmask, do, 0.0).astype(BF16)
            s = _dot_nt(qs, km_ref[0])
            pe = jnp.exp(s - jnp.max(s, axis=1, keepdims=True))
            pn = pe / jnp.sum(pe, axis=1, keepdims=True)
            dp = _dot_nt(dom, vm_ref[0])
            ds = (pn * (dp - jnp.sum(pn * dp, axis=1, keepdims=True))).astype(BF16)
            dq_out = jnp.where(hmask, _dot(ds, km_ref[0]) * 0.125, dq_out)
            dkm = dkm + _dot_tn(ds, qs)
            dvm = dvm + _dot_tn(pn, dom)
        dq_ref[...] = dq_out.astype(BF16)
        _acc(dkm_ref, dkm[None], first)
        _acc(dvm_ref, dvm[None], first)

    return pl.pallas_call(
        body, name="memattn_bwd", grid=(Bl, 2, nq),
        in_specs=[BS((tq, LANES), lambda b, p, i: (b * nq + i, qc + p)),
                  BS((1, Mt, LANES), lambda b, p, i: (b, 0, p)),
                  BS((1, Mt, LANES), lambda b, p, i: (b, 0, 2 + p)),
                  BS((tq, LANES), lambda b, p, i: (b * nq + i, p))],
        out_specs=[BS((tq, LANES), lambda b, p, i: (b * nq + i, p)),
                   BS((1, Mt, LANES), lambda b, p, i: (b, 0, p)),
                   BS((1, Mt, LANES), lambda b, p, i: (b, 0, p))],
        out_shape=[jax.ShapeDtypeStruct((T, M_W), BF16), jax.ShapeDtypeStruct((Bl, Mt, M_W), F32),
                   jax.ShapeDtypeStruct((Bl, Mt, M_W), F32)],
        compiler_params=_cp(("arbitrary", "arbitrary", "arbitrary")),
    )(proj, kv, kv, dym)


def _memkv_bwd(dkm, dvm, memn, mem, g_mem, w_kv):
    Bl, Mt, D = mem.shape

    def body(dk_ref, dv_ref, mn_ref, m_ref, g_ref, w_ref, dw_ref, dg_ref):
        first = pl.program_id(0) == 0
        dk = dk_ref[0].astype(BF16)
        dv = dv_ref[0].astype(BF16)
        mn = mn_ref[0]
        dmn = _dot_nt(dk, w_ref[:, 0:M_W]) + _dot_nt(dv, w_ref[:, M_W:])
        _, vjp = jax.vjp(_rms, m_ref[0], g_ref[...])
        _, dg = vjp(dmn)
        _acc(dg_ref, dg, first)

        @pl.when(first)
        def _():
            dw_ref[...] = jnp.zeros_like(dw_ref)

        dw_ref[:, 0:M_W] += _dot_tn(mn, dk)
        dw_ref[:, M_W:] += _dot_tn(mn, dv)

    return pl.pallas_call(
        body, name="memkv_bwd", grid=(Bl,),
        in_specs=[BS((1, Mt, M_W), lambda b: (b, 0, 0)), BS((1, Mt, M_W), lambda b: (b, 0, 0)),
                  BS((1, Mt, D), lambda b: (b, 0, 0)), BS((1, Mt, D), lambda b: (b, 0, 0)),
                  BS((1, D), lambda b: (0, 0)), BS((D, 2 * M_W), lambda b: (0, 0))],
        out_specs=[BS((D, 2 * M_W), lambda b: (0, 0)), BS((1, D), lambda b: (0, 0))],
        out_shape=[jax.ShapeDtypeStruct((D, 2 * M_W), F32), jax.ShapeDtypeStruct((1, D), F32)],
        compiler_params=_cp(("arbitrary",)),
    )(dkm, dvm, memn, mem, g_mem, w_kv)


def _fox_bwd(proj, dyb, lse, bq, bk, Bl, S):
    T = Bl * S
    nq = S // Q_BLK
    nb = S // LANES
    qc, kc, vc = 768 // LANES, 1152 // LANES, 1536 // LANES

    def body(q_ref, k_ref, v_ref, do_ref, lse_ref, bq_ref, bk_ref,
             dq_ref, dk_ref, dv_ref, dcr_ref, ka_ref, dka_ref, dva_ref):
        p = pl.program_id(1)
        lane_s = _iota((S, LANES), 1)
        lane = _iota((Q_BLK, LANES), 1)
        sub = _iota((8, LANES), 0)
        tri = _iota((Q_BLK, Q_BLK), 1) <= _iota((Q_BLK, Q_BLK), 0)
        k = k_ref[...]
        for hh in range(2):
            data = (lane_s < HEAD) if hh == 0 else (lane_s >= HEAD)
            ka_ref[hh] = jnp.where(data, k, bk_ref[0, hh])
        dka_ref[...] = jnp.zeros_like(dka_ref)
        dva_ref[...] = jnp.zeros_like(dva_ref)

        @pl.when(p == 0)
        def _():
            dcr_ref[...] = jnp.zeros_like(dcr_ref)

        def add_colsums(ds, first_blk, h):
            cs = _colsum(ds)
            for jb in range(ds.shape[1] // LANES):
                dcr_ref[0, first_blk + jb] += jnp.where(sub == h, cs[:, jb * LANES:(jb + 1) * LANES], 0.0)

        for i in range(nq):
            r0 = i * Q_BLK
            r1 = r0 + Q_BLK
            q = q_ref[r0:r1, :]
            do = do_ref[r0:r1, :]
            lse_b = lse_ref[0, r0:r1, :]
            dq_out = jnp.zeros((Q_BLK, LANES), F32)
            for hh in range(2):
                hmask = (lane < HEAD) if hh == 0 else (lane >= HEAD)
                h = 2 * p + hh
                qs = jnp.where(hmask, q * 0.125, jnp.zeros_like(q))
                qa = jnp.where(hmask, q * 0.125, bq_ref[0, hh, r0:r1, :])
                dob = jnp.where(hmask, do, 0.0).astype(BF16)
                lse_h = jnp.sum(jnp.where(lane == hh * HEAD, lse_b, 0.0), axis=1, keepdims=True)
                pd = jnp.where(tri, jnp.exp(_dot_nt(qa, ka_ref[hh, r0:r1, :]) - lse_h), 0.0)
                dpd = _dot_nt(dob, v_ref[r0:r1, :])
                delta = jnp.sum(pd * dpd, axis=1, keepdims=True)
                psum = jnp.sum(pd, axis=1, keepdims=True)
                if i:
                    pf = jnp.exp(_dot_nt(qa, ka_ref[hh, 0:r0, :]) - lse_h)
                    dpf = _dot_nt(dob, v_ref[0:r0, :])
                    delta = delta + jnp.sum(pf * dpf, axis=1, keepdims=True)
                    psum = psum + jnp.sum(pf, axis=1, keepdims=True)
                delta = delta / psum
                dsd = pd * (dpd - delta)
                add_colsums(dsd, r0 // LANES, h)
                dsd = dsd.astype(BF16)
                dq_h = _dot(dsd, k_ref[r0:r1, :])
                dka_ref[r0:r1, :] += _dot_tn(dsd, qs)
                dva_ref[r0:r1, :] += _dot_tn(pd, dob)
                if i:
                    dsf = pf * (dpf - delta)
                    add_colsums(dsf, 0, h)
                    dsf = dsf.astype(BF16)
                    dq_h = dq_h + _dot(dsf, k_ref[0:r0, :])
                    dka_ref[0:r0, :] += _dot_tn(dsf, qs)
                    dva_ref[0:r0, :] += _dot_tn(pf, dob)
                dq_out = jnp.where(hmask, dq_h * 0.125, dq_out)
            dq_ref[r0:r1, :] = dq_out.astype(BF16)
        dk_ref[...] = dka_ref[...].astype(BF16)
        dv_ref[...] = dva_ref[...].astype(BF16)

    seq = lambda c0: BS((S, LANES), lambda b, p: (b, c0 + p))
    pair = BS((1, 2, S, LANES), lambda b, p: (b, p, 0, 0))
    rowblk = BS((1, nb, 8, LANES), lambda b, p: (b, 0, 0, 0))
    return pl.pallas_call(
        body, name="fox_bwd", grid=(Bl, 3),
        in_specs=[seq(qc), seq(kc), seq(vc), seq(0), BS((1, S, LANES), lambda b, p: (p, b, 0)), pair, pair],
        out_specs=[seq(0), seq(0), seq(0), rowblk],
        out_shape=[jax.ShapeDtypeStruct((T, B_W), BF16)] * 3 + [jax.ShapeDtypeStruct((Bl, nb, 8, LANES), F32)],
        scratch_shapes=[pltpu.VMEM((2, S, LANES), BF16), pltpu.VMEM((S, LANES), F32), pltpu.VMEM((S, LANES), F32)],
        compiler_params=_cp(("arbitrary", "arbitrary")),
    )(proj, proj, proj, dyb, lse, bq, bk)


def _gate_bwd(dc_row, fl_row):
    Bl, nb, _, _ = dc_row.shape

    def body(dc_ref, fl_ref, o_ref):
        lane = _iota((8, LANES), 1)

        def blk(jj, carry):
            j = nb - 1 - jj
            r = -dc_ref[0, j]
            for k in (1, 2, 4, 8, 16, 32, 64):
                r = r + jnp.where(lane < LANES - k, pltpu.roll(r, LANES - k, 1), 0.0)
            r = r + carry
            dfl = r * _sigmoid(-fl_ref[0, j])
            o_ref[0, pl.ds(pl.multiple_of(j * LANES, LANES), LANES), :] = jnp.concatenate(
                [dfl, jnp.zeros((LANES - 8, LANES), F32)], axis=0).T
            return jnp.sum(jnp.where(lane == 0, r, 0.0), axis=1, keepdims=True)

        lax.fori_loop(0, nb, blk, jnp.zeros((8, 1), F32))

    rowblk = BS((1, nb, 8, LANES), lambda b: (b, 0, 0, 0))
    return pl.pallas_call(
        body, name="gate_bwd", grid=(Bl,),
        in_specs=[rowblk, rowblk],
        out_specs=BS((1, nb * LANES, LANES), lambda b: (b, 0, 0)),
        out_shape=jax.ShapeDtypeStruct((Bl, nb * LANES, LANES), F32),
        compiler_params=_cp(("arbitrary",)),
    )(dc_row, fl_row)


def _inproj_bwd(dzu, dzv, dq, dk, dv, dqm, dfl, x2d, dx1, g_pre, w_in_p, tm):
    T, D = x2d.shape
    ns, _, dsh = w_in_p.shape

    def body(dzu_ref, dzv_ref, dq_ref, dk_ref, dv_ref, dqm_ref, dfl_ref, x_ref, dx1_ref, g_ref, w_ref,
             dp_ref, gx_ref, dg_ref, dbf_ref):
        first = pl.program_id(0) == 0
        dfl = dfl_ref[...]
        dp_ref[:, 0:384] = dzu_ref[...]
        dp_ref[:, 384:768] = dzv_ref[...]
        dp_ref[:, 768:1152] = dq_ref[...]
        dp_ref[:, 1152:1536] = dk_ref[...]
        dp_ref[:, 1536:1920] = dv_ref[...]
        dp_ref[:, 1920:2048] = dfl.astype(BF16)
        dp_ref[:, 2048:2304] = dqm_ref[...]
        dh = jnp.concatenate([_dot(dp_ref[...], w_ref[s]) for s in range(ns)], axis=1)
        _, vjp = jax.vjp(_rms, x_ref[...], g_ref[...])
        dxa, dg = vjp(dh)
        gx_ref[...] = dx1_ref[...] + dxa
        _acc(dg_ref, dg, first)
        _acc(dbf_ref, _colsum(dfl), first)

    row = lambda w: BS((tm, w), lambda i: (i, 0))
    return pl.pallas_call(
        body, name="inproj_bwd", grid=(T // tm,),
        in_specs=[row(A_W), row(A_W), row(B_W), row(B_W), row(B_W), row(M_W), row(LANES), row(D), row(D),
                  BS((1, D), lambda i: (0, 0)), BS((ns, P_COLS, dsh), lambda i: (0, 0, 0))],
        out_specs=[row(P_COLS), row(D), BS((1, D), lambda i: (0, 0)), BS((1, LANES), lambda i: (0, 0))],
        out_shape=[jax.ShapeDtypeStruct((T, P_COLS), BF16), jax.ShapeDtypeStruct((T, D), F32),
                   jax.ShapeDtypeStruct((1, D), F32), jax.ShapeDtypeStruct((1, LANES), F32)],
        compiler_params=_cp(("arbitrary",)),
    )(dzu, dzv, dq, dk, dv, dqm, dfl, x2d, dx1, g_pre, w_in_p)


def _local_step(x, mem, target, W, P, reduce=None):
    Bl, S, D = x.shape
    T = Bl * S
    tm = min(512, T)
    x2d = x.reshape(T, D)
    t2d = target.reshape(T, D)
    vec = lambda a: a.reshape(1, -1)
    bf_row = jnp.pad(P["b_f"].reshape(1, -1), ((0, 0), (0, LANES - N_FOX_HEADS)))
    tril = jnp.tril(jnp.ones((CHUNK, CHUNK), bool))
    ws_tril = jnp.where(tril[None], P["w_s"][0], 0.0).astype(BF16)
    bs_full = jnp.repeat(P["b_s"][0].T, HEAD, axis=1)
    g_pre, g_sgu = vec(P["g_pre_mix"]), vec(P["g_sgu"])
    ga, gb, gm = vec(P["g_out_a"]), vec(P["g_out_b"]), vec(P["g_out_m"])
    g_mem, g_post, g_pre2, g_post2 = vec(P["g_mem"]), vec(P["g_post_mix"]), vec(P["g_pre_ffn"]), vec(P["g_post_ffn"])

    h, proj, flog = _inproj_fwd(x2d, g_pre, W["w_in"], tm)
    bq, bk, fl_row = _gate_fwd(flog.reshape(Bl, S, LANES), bf_row)
    ya = _sgu_fwd(proj, g_sgu, ws_tril, bs_full, tm)
    yb, lse = _fox_fwd(proj, bq, bk, Bl, S)
    memn, kv = _memkv_fwd(mem, g_mem, W["w_mem_kv"])
    ym = _memattn_fwd(proj, kv, Bl, S, min(512, S))
    y, o, x1, h2 = _outproj_fwd(ya, yb, ym, x2d, ga, gb, gm, g_post, g_pre2, W["w_out"], tm)
    gs, us, dff, dx2, dg_post2, loss = _ffn_fwd(h2, x1, t2d, W["w_gate"], W["w_up"], W["w_down"], g_post2, tm)

    dh2, d_w_gate, d_w_up, d_w_down = _ffn_bwd(dff, h2, gs, us, W["w_gate"], W["w_up"], W["w_down"], tm)
    ffn = [d_w_gate, d_w_up, d_w_down]
    if reduce is not None:
        pending, _ = reduce.begin("ffn", ffn)
    dx1, do, dya, dyb, dym, dga, dgb, dgm, dg_post, dg_pre2 = _outproj_bwd(
        dh2, x1, dx2, o, ya, yb, ym, ga, gb, gm, g_post, g_pre2, W["w_out"], tm)
    if reduce is not None:
        ffn, (do, dya, dyb, dym) = reduce.finish("ffn", pending, (do, dya, dyb, dym))
    d_w_out = _mm_tn(y, do, "dw_out", 1024)
    dzu, dzv, dws, dbs_cols, dg_sgu = _sgu_bwd(proj, dya, g_sgu, ws_tril, bs_full, tm)
    dqm, dkm, dvm = _memattn_bwd(proj, kv, dym, Bl, S, min(512, S))
    d_w_kv, dg_mem = _memkv_bwd(dkm, dvm, memn, mem, g_mem, W["w_mem_kv"])
    mid = [d_w_kv, d_w_out]
    dq, dk, dv, dc_row = _fox_bwd(proj, dyb, lse, bq, bk, Bl, S)
    if reduce is not None:
        pending, (dc_row,) = reduce.begin("mid", mid, (dc_row,))
    dfl = _gate_bwd(dc_row, fl_row).reshape(T, LANES)
    dproj, grad_x, dg_pre, dbf = _inproj_bwd(dzu, dzv, dq, dk, dv, dqm, dfl, x2d, dx1, g_pre, W["w_in"], tm)
    if reduce is not None:
        mid, (dproj,) = reduce.finish("mid", pending, (dproj,))
    d_w_in = _dw_in(dproj, h, W["w_in"].shape[0], 1024)
    if reduce is not None:
        d_w_in, _ = reduce.begin("in", [d_w_in])
    big = dict(zip(BIG, [d_w_in] + mid + ffn))
    small = {"g_pre_mix": dg_pre, "b_f": dbf[:, :N_FOX_HEADS], "g_sgu": dg_sgu, "w_s": dws, "b_s": dbs_cols[:, :N_FOX_HEADS].T,
             "g_out_a": dga, "g_out_b": dgb, "g_out_m": dgm, "g_mem": dg_mem, "g_post_mix": dg_post,
             "g_pre_ffn": dg_pre2, "g_post_ffn": dg_post2, "loss": loss[:, :1]}
    return grad_x.reshape(Bl, S, D), big, small


def _place():
    return lax.axis_index("x"), lax.axis_index("y"), lax.axis_index("c")


def _exchange_on_sequencer(srcs, own_full, name, collective_id):
    n = len(srcs)

    def body(*refs):
        src, dst = refs[:n], refs[n:2 * n]
        lsem, isend, irecv, dsend, drecv = refs[2 * n:]
        x, y, c = _place()
        oc = 1 - c
        s_me = 2 * x + y
        sib = (x, y, oc)
        chips = [(1 - x, y), (x, 1 - y), (1 - x, 1 - y)]
        barrier = pltpu.get_barrier_semaphore()
        for dev in [(cx, cy, c) for cx, cy in chips] + [sib]:
            pl.semaphore_signal(barrier, inc=1, device_id=dev, device_id_type=MESH)
        pl.semaphore_wait(barrier, 4)

        def remote(a, b, ssem, rsem, dev):
            return pltpu.make_async_remote_copy(src_ref=a, dst_ref=b, send_sem=ssem, recv_sem=rsem,
                                                device_id=dev, device_id_type=MESH)

        sends, local = [], []
        for w in range(n):
            for j, (cx, cy) in enumerate(chips):
                half = src[w].at[c] if own_full else src[w].at[2 * cx + cy]
                cp = remote(half, dst[w].at[s_me, c], isend.at[w, j], irecv.at[w, j], (cx, cy, c))
                cp.start()
                sends.append(cp)
            if own_full:
                cp = remote(src[w], dst[w].at[s_me], dsend.at[w, 3], drecv.at[w, 3], sib)
            else:
                cp = remote(src[w].at[s_me], dst[w].at[s_me, c], dsend.at[w, 3], drecv.at[w, 3], sib)
                loc = pltpu.make_async_copy(src[w].at[s_me], dst[w].at[s_me, c], lsem.at[w])
                loc.start()
                local.append(loc)
            cp.start()
            sends.append(cp)
        for w in range(n):
            for j, (cx, cy) in enumerate(chips):
                landed = dst[w].at[2 * cx + cy, c]
                remote(landed, landed, isend.at[w, j], irecv.at[w, j], (cx, cy, c)).wait_recv()
                cp = remote(landed, landed, dsend.at[w, j], drecv.at[w, j], sib)
                cp.start()
                sends.append(cp)
        for w in range(n):
            for j, (cx, cy) in enumerate(chips):
                landed = dst[w].at[2 * cx + cy, oc]
                remote(landed, landed, dsend.at[w, j], drecv.at[w, j], sib).wait_recv()
            landed = dst[w].at[s_me] if own_full else dst[w].at[s_me, oc]
            remote(landed, landed, dsend.at[w, 3], drecv.at[w, 3], sib).wait_recv()
        for cp in sends:
            cp.wait_send()
        for loc in local:
            loc.wait()

    return pl.kernel(
        body, out_type=[jax.ShapeDtypeStruct((4, 2) + s.shape[1:], s.dtype) for s in srcs],
        mesh=plsc.ScalarSubcoreMesh(axis_name="sequencer", num_cores=1), name=name,
        scratch_types=[pltpu.SemaphoreType.DMA((n,)), pltpu.SemaphoreType.DMA((n, 3)), pltpu.SemaphoreType.DMA((n, 3)),
                       pltpu.SemaphoreType.DMA((n, 4)), pltpu.SemaphoreType.DMA((n, 4))],
        compiler_params=pltpu.CompilerParams(collective_id=collective_id),
    )(*srcs)


def _sibling_swap(grads, name, collective_id):
    n = len(grads)

    def body(*refs):
        g, theirs = refs[:n], refs[n:2 * n]
        ssem, rsem = refs[2 * n:]
        x, y, c = _place()
        sib = (x, y, 1 - c)
        barrier = pltpu.get_barrier_semaphore()
        pl.semaphore_signal(barrier, inc=1, device_id=sib, device_id_type=MESH)
        pl.semaphore_wait(barrier, 1)
        cps = []
        for w in range(n):
            cp = pltpu.make_async_remote_copy(src_ref=g[w].at[:, 1 - c], dst_ref=theirs[w], send_sem=ssem.at[w],
                                              recv_sem=rsem.at[w], device_id=sib, device_id_type=MESH)
            cp.start()
            cps.append(cp)
        for cp in cps:
            cp.wait()

    return pl.kernel(
        body, out_type=[jax.ShapeDtypeStruct((4,) + g.shape[2:], g.dtype) for g in grads],
        mesh=plsc.ScalarSubcoreMesh(axis_name="sequencer", num_cores=1), name=name,
        scratch_types=[pltpu.SemaphoreType.DMA((n,)), pltpu.SemaphoreType.DMA((n,))],
        compiler_params=pltpu.CompilerParams(collective_id=collective_id),
    )(*grads)


def _add_pair(core, g, theirs, name):
    _, _, hr, C = g.shape

    def body(core_ref, g_ref, t_ref, o_ref):
        o_ref[0] = (g_ref[0, 0] + t_ref[0]).astype(BF16)

    blk = BS((1, hr, C), lambda s, core_ref: (s, 0, 0))
    return pl.pallas_call(
        body, name=name,
        grid_spec=pltpu.PrefetchScalarGridSpec(
            num_scalar_prefetch=1, grid=(4,),
            in_specs=[BS((1, 1, hr, C), lambda s, core_ref: (s, core_ref[0], 0, 0)), blk], out_specs=blk),
        out_shape=jax.ShapeDtypeStruct(theirs.shape, BF16), compiler_params=_cp(("arbitrary",)))(core, g, theirs)


def _sum_chips(r, name):
    _, _, hr, C = r.shape

    def body(r_ref, o_ref):
        o_ref[...] = ((r_ref[0, 0].astype(F32) + r_ref[1, 0].astype(F32)) + r_ref[2, 0].astype(F32)) + r_ref[3, 0].astype(F32)

    return pl.pallas_call(body, name=name, grid=(2,), in_specs=[BS((4, 1, hr, C), lambda h: (0, h, 0, 0))],
                          out_specs=BS((hr, C), lambda h: (h, 0)), out_shape=jax.ShapeDtypeStruct((2 * hr, C), F32),
                          compiler_params=_cp(("arbitrary",)))(r)


class _Reducer:
    IDS = {"ffn": (4, 5), "mid": (6, 7), "in": (8, 9)}

    def __init__(self, core):
        self.core = core

    def begin(self, tag, grads, after=()):
        grads, after = lax.optimization_barrier((list(grads), after))
        g4 = [g.reshape(4, 2, -1, g.shape[-1]) for g in grads]
        return (g4, _sibling_swap(g4, "swap_" + tag, self.IDS[tag][0])), after

    def finish(self, tag, pending, hold):
        g4, theirs = pending
        sums = [_add_pair(self.core, g, t, "chip_sum_%s_%d" % (tag, k)) for k, (g, t) in enumerate(zip(g4, theirs))]
        sums, hold = lax.optimization_barrier((sums, hold))
        return _exchange_on_sequencer(sums, False, "scatter_" + tag, self.IDS[tag][1]), hold


def _small_allreduce(part):
    R = part.shape[0]
    rs = R // 8
    masks = [(mx, my, mc) for mx in (0, 1) for my in (0, 1) for mc in (0, 1)][1:]

    def body(p_ref, o_ref, buf_ref, s1, r1, s2, r2):
        x, y, c = _place()
        d = 4 * x + 2 * y + c
        mine = pl.ds(pl.multiple_of(d * rs, 8), rs)
        peers = [((x + mx) % 2, (y + my) % 2, (c + mc) % 2) for mx, my, mc in masks]
        first, second = [], []
        for k, (px, py, pc) in enumerate(peers):
            theirs = pl.ds(pl.multiple_of((4 * px + 2 * py + pc) * rs, 8), rs)
            cp = pltpu.make_async_remote_copy(src_ref=p_ref.at[theirs, :], dst_ref=buf_ref.at[d], send_sem=s1.at[k],
                                              recv_sem=r1.at[k], device_id=(px, py, pc), device_id_type=MESH)
            cp.start()
            first.append(cp)
        buf_ref[d] = p_ref[mine, :]
        for k, (px, py, pc) in enumerate(peers):
            slot = buf_ref.at[4 * px + 2 * py + pc]
            pltpu.make_async_remote_copy(src_ref=slot, dst_ref=slot, send_sem=s1.at[k], recv_sem=r1.at[k],
                                         device_id=(px, py, pc), device_id_type=MESH).wait_recv()
        total = buf_ref[0]
        for k in range(1, 8):
            total = total + buf_ref[k]
        o_ref[mine, :] = total
        for k, (px, py, pc) in enumerate(peers):
            cp = pltpu.make_async_remote_copy(src_ref=o_ref.at[mine, :], dst_ref=o_ref.at[mine, :], send_sem=s2.at[k],
                                              recv_sem=r2.at[k], device_id=(px, py, pc), device_id_type=MESH)
            cp.start()
            second.append(cp)
        for k, (px, py, pc) in enumerate(peers):
            rows = o_ref.at[pl.ds(pl.multiple_of((4 * px + 2 * py + pc) * rs, 8), rs), :]
            pltpu.make_async_remote_copy(src_ref=rows, dst_ref=rows, send_sem=s2.at[k], recv_sem=r2.at[k],
                                         device_id=(px, py, pc), device_id_type=MESH).wait_recv()
        for cp in first + second:
            cp.wait_send()

    vm = pl.BlockSpec(memory_space=pltpu.VMEM)
    return pl.pallas_call(
        body, name="small_allreduce", in_specs=[vm], out_specs=vm, out_shape=jax.ShapeDtypeStruct(part.shape, F32),
        scratch_shapes=[pltpu.VMEM((8, rs, LANES), F32)] + [pltpu.SemaphoreType.DMA((7,))] * 4,
    )(part)


def _adamw(w, g, m, v, name):
    R, C = w.shape
    summed = g.ndim == 4
    if summed:
        tr = R // 2
    else:
        tr = R if R * C * 4 <= (1 << 21) else R // 2
        if tr % 8:
            tr = R
    c1 = 1.0 / (1.0 - ADAM_B1 ** ADAM_STEP)
    c2 = 1.0 / (1.0 - ADAM_B2 ** ADAM_STEP)

    def body(w_ref, g_ref, m_ref, v_ref, *outs):
        if summed:
            g_ = ((g_ref[0, 0].astype(F32) + g_ref[1, 0].astype(F32)) + g_ref[2, 0].astype(F32)) + g_ref[3, 0].astype(F32)
            outs[0][...] = g_
        else:
            g_ = g_ref[...]
        d_ref, mo_ref, vo_ref = outs[-3:]
        m_ = ADAM_B1 * m_ref[...] + (1.0 - ADAM_B1) * g_
        v_ = ADAM_B2 * v_ref[...] + (1.0 - ADAM_B2) * (g_ * g_)
        mo_ref[...] = m_
        vo_ref[...] = v_
        d_ref[...] = -ADAM_LR * ((m_ * c1) / (jnp.sqrt(v_ * c2) + ADAM_EPS) + ADAM_WD * w_ref[...])

    blk = BS((tr, C), lambda i: (i, 0))
    g_blk = BS((4, 1, tr, C), lambda i: (0, i, 0, 0)) if summed else blk
    nout = 4 if summed else 3
    return pl.pallas_call(body, name=name, grid=(R // tr,), in_specs=[blk, g_blk, blk, blk], out_specs=[blk] * nout,
                          out_shape=[jax.ShapeDtypeStruct((R, C), F32)] * nout,
                          compiler_params=_cp(("arbitrary",)))(w, g, m, v)


SMALL = ("g_pre_mix", "b_f", "g_sgu", "w_s", "b_s", "g_out_a", "g_out_b", "g_out_m", "g_mem", "g_post_mix",
         "g_pre_ffn", "g_post_ffn")
BIG = ("w_in", "w_mem_kv", "w_out", "w_gate", "w_up", "w_down")
TRANSPOSED = ("w_in", "w_gate", "w_up")
WEIGHTS = ("g_pre_mix", "w_in", "b_f", "g_sgu", "w_s", "b_s", "g_out_a", "g_out_b", "g_out_m", "g_mem", "w_mem_kv",
           "w_out", "g_post_mix", "g_pre_ffn", "w_gate", "w_up", "w_down", "g_post_ffn")


def _rows_of(n):
    return -(-n // (8 * LANES)) * 8


def _pack(parts):
    tiles = []
    for a in parts:
        flat = a.reshape(-1).astype(F32)
        rows = _rows_of(flat.shape[0])
        tiles.append(jnp.pad(flat, (0, rows * LANES - flat.shape[0])).reshape(rows, LANES))
    total = sum(t.shape[0] for t in tiles)
    pad = -total % 64
    if pad:
        tiles.append(jnp.zeros((pad, LANES), F32))
    return jnp.concatenate(tiles, axis=0)


def _unpack(packed, shapes):
    out, r = [], 0
    for shp in shapes:
        n = 1
        for s in shp:
            n *= s
        rows = _rows_of(n)
        out.append(packed[r:r + rows].reshape(-1)[:n].reshape(shp))
        r += rows
    return out


def kernel(x, mem, g_pre_mix, w_in, b_f, g_sgu, w_s, b_s, g_out_a, g_out_b, g_out_m, g_mem, w_mem_kv, w_out, g_post_mix, g_pre_ffn, w_gate, w_up, w_down, g_post_ffn, loss_target, m_g_pre_mix, m_w_in, m_b_f, m_g_sgu, m_w_s, m_b_s, m_g_out_a, m_g_out_b, m_g_out_m, m_g_mem, m_w_mem_kv, m_w_out, m_g_post_mix, m_g_pre_ffn, m_w_gate, m_w_up, m_w_down, m_g_post_ffn, v_g_pre_mix, v_w_in, v_b_f, v_g_sgu, v_w_s, v_b_s, v_g_out_a, v_g_out_b, v_g_out_m, v_g_mem, v_w_mem_kv, v_w_out, v_g_post_mix, v_g_pre_ffn, v_w_gate, v_w_up, v_w_down, v_g_post_ffn):
    Wt = dict(g_pre_mix=g_pre_mix, w_in=w_in, b_f=b_f, g_sgu=g_sgu, w_s=w_s, b_s=b_s, g_out_a=g_out_a, g_out_b=g_out_b,
              g_out_m=g_out_m, g_mem=g_mem, w_mem_kv=w_mem_kv, w_out=w_out, g_post_mix=g_post_mix, g_pre_ffn=g_pre_ffn,
              w_gate=w_gate, w_up=w_up, w_down=w_down, g_post_ffn=g_post_ffn)
    Mo = dict(g_pre_mix=m_g_pre_mix, w_in=m_w_in, b_f=m_b_f, g_sgu=m_g_sgu, w_s=m_w_s, b_s=m_b_s, g_out_a=m_g_out_a,
              g_out_b=m_g_out_b, g_out_m=m_g_out_m, g_mem=m_g_mem, w_mem_kv=m_w_mem_kv, w_out=m_w_out,
              g_post_mix=m_g_post_mix, g_pre_ffn=m_g_pre_ffn, w_gate=m_w_gate, w_up=m_w_up, w_down=m_w_down,
              g_post_ffn=m_g_post_ffn)
    Vo = dict(g_pre_mix=v_g_pre_mix, w_in=v_w_in, b_f=v_b_f, g_sgu=v_g_sgu, w_s=v_w_s, b_s=v_b_s, g_out_a=v_g_out_a,
              g_out_b=v_g_out_b, g_out_m=v_g_out_m, g_mem=v_g_mem, w_mem_kv=v_w_mem_kv, w_out=v_w_out,
              g_post_mix=v_g_post_mix, g_pre_ffn=v_g_pre_ffn, w_gate=v_w_gate, w_up=v_w_up, w_down=v_w_down,
              g_post_ffn=v_g_post_ffn)

    gap = P_COLS - IN_COLS

    def to_kernel(n, w):
        if n in TRANSPOSED:
            w = w.T
        if n == "w_in":
            w = jnp.pad(w[:F_END], ((0, P_COLS - F_END), (0, 0))) + jnp.pad(w[F_END:], ((F_END + gap, 0), (0, 0)))
        return w

    def ungroup(g):
        return jnp.pad(g[:F_END], ((0, IN_COLS - F_END), (0, 0))) + jnp.pad(g[F_END + gap:], ((F_END, 0), (0, 0)))

    shards = {n: to_kernel(n, Wt[n][0]) for n in BIG}
    srcs = [shards[n].astype(BF16).reshape(2, shards[n].shape[0] // 2, shards[n].shape[1]) for n in BIG]
    fulls = (_exchange_on_sequencer(srcs[:1], True, "gather_w_in", 1)
             + _exchange_on_sequencer(srcs[1:3], True, "gather_kv_out", 2)
             + _exchange_on_sequencer(srcs[3:], True, "gather_ffn", 3))
    W = {}
    for n, f in zip(BIG, fulls):
        _, _, hr, C = f.shape
        W[n] = f.reshape(8 * hr, C) if n in ("w_mem_kv", "w_out") else f.reshape(4, 2 * hr, C)

    P = {n: Wt[n] for n in SMALL}
    core = lax.axis_index("c").astype(jnp.int32).reshape(1)
    reducer = _Reducer(core)
    grad_x, landed, small = _local_step(x, mem, loss_target, W, P, reducer)

    total = _small_allreduce(_pack([small[n] for n in SMALL] + [small["loss"]]))
    (landed["w_in"],), (total,) = reducer.finish("in", landed["w_in"], (total,))

    grads, deltas, new_m, new_v = {}, {}, {}, {}
    for n in BIG:
        wmv = [a[n][0].T if n in TRANSPOSED else a[n][0] for a in (Wt, Mo, Vo)]
        if n == "w_in":
            g = ungroup(_sum_chips(landed[n], "sum_chips_" + n))
            g, d, m1, v1 = (g,) + tuple(_adamw(wmv[0], g, wmv[1], wmv[2], "adamw_" + n))
        else:
            g, d, m1, v1 = _adamw(wmv[0], landed[n], wmv[1], wmv[2], "adamw_" + n)
        if n in TRANSPOSED:
            g, d, m1, v1 = g.T, d.T, m1.T, v1.T
        grads[n], deltas[n], new_m[n], new_v[n] = g[None], d[None], m1[None], v1[None]

    slot = [jnp.zeros((1, 1), F32)]
    shapes = [Wt[n].shape for n in SMALL] + [(1, 1)]
    d, m1, v1 = _adamw(_pack([Wt[n] for n in SMALL] + slot), total, _pack([Mo[n] for n in SMALL] + slot),
                       _pack([Vo[n] for n in SMALL] + slot), "adamw_small")
    g_s, d_s, m_s, v_s = _unpack(total, shapes), _unpack(d, shapes), _unpack(m1, shapes), _unpack(v1, shapes)
    for k, n in enumerate(SMALL):
        grads[n], deltas[n], new_m[n], new_v[n] = g_s[k], d_s[k], m_s[k], v_s[k]
    loss = g_s[-1][0, 0]

    return (loss, grad_x, *[grads[n] for n in WEIGHTS], *[deltas[n] for n in WEIGHTS],
            *[new_m[n] for n in WEIGHTS], *[new_v[n] for n in WEIGHTS])
```

```python
import functools

import jax
import jax.numpy as jnp
from jax import lax
from jax.experimental import pallas as pl
from jax.experimental.pallas import tpu as pltpu
from jax.experimental.pallas import tpu_sc as plsc

F32 = jnp.float32
BF16 = jnp.bfloat16
EPS = 1e-6
NEG = -1e30
HEAD = 64
A_W, B_W, M_W = 384, 384, 256
N_FOX_HEADS = 6
CHUNK = 128
IN_COLS = 2 * A_W + 3 * B_W + N_FOX_HEADS + M_W
P_MAIN = 2 * A_W + 3 * B_W + M_W
P_COLS = P_MAIN + 128
F_END = 2 * A_W + 3 * B_W + N_FOX_HEADS
LANES = 128
Q_BLK, K_BLK = 256, 128
ADAM_LR, ADAM_B1, ADAM_B2, ADAM_EPS, ADAM_WD, ADAM_STEP = 0.001, 0.9, 0.999, 1e-08, 0.01, 10
VMEM_LIMIT = 56 * 1024 * 1024
MESH = pl.DeviceIdType.MESH
ANY = pl.BlockSpec(memory_space=pl.ANY)
BS = pl.BlockSpec


def _cp(sem=None):
    return pltpu.CompilerParams(dimension_semantics=sem, vmem_limit_bytes=VMEM_LIMIT)


def _iota(shape, dim):
    return lax.broadcasted_iota(jnp.int32, shape, dim)


def _dot(a, b):
    return jnp.dot(a.astype(BF16), b.astype(BF16), preferred_element_type=F32)


def _dot_nt(a, b):
    return lax.dot_general(a.astype(BF16), b.astype(BF16), (((1,), (1,)), ((), ())), preferred_element_type=F32)


def _dot_tn(a, b):
    return lax.dot_general(a.astype(BF16), b.astype(BF16), (((0,), (0,)), ((), ())), preferred_element_type=F32)


def _rms(x, g):
    return x * lax.rsqrt(jnp.mean(x * x, axis=-1, keepdims=True) + EPS) * g


def _gelu(x):
    return 0.5 * x * (1.0 + jnp.tanh(0.7978845608028654 * (x + 0.044715 * (x * x * x))))


def _sigmoid(x):
    return 1.0 / (1.0 + jnp.exp(-x))


def _silu_mul(g, u):
    return g * _sigmoid(g) * u


def _logsig(x):
    return jnp.minimum(x, 0.0) - jnp.log(1.0 + jnp.exp(-jnp.abs(x)))


def _colsum(x):
    return jnp.sum(x, axis=0, keepdims=True)


def _acc(ref, val, first):
    @pl.when(first)
    def _():
        ref[...] = val

    @pl.when(jnp.logical_not(first))
    def _():
        ref[...] += val


def _inproj_fwd(x2d, g_pre, w_in_p, tm):
    T, D = x2d.shape
    nchunk = P_COLS // 384
    ns, _, dsh = w_in_p.shape

    def body(x_ref, g_ref, w_ref, h_ref, proj_ref, fl_ref):
        h = _rms(x_ref[...], g_ref[...]).astype(BF16)
        h_ref[...] = h
        for n in range(nchunk):
            r = _dot_nt(h[:, 0:dsh], w_ref[0, n * 384:(n + 1) * 384, :])
            for s in range(1, ns):
                r = r + _dot_nt(h[:, s * dsh:(s + 1) * dsh], w_ref[s, n * 384:(n + 1) * 384, :])
            if n < nchunk - 1:
                proj_ref[:, n * 384:(n + 1) * 384] = r.astype(BF16)
            else:
                fl_ref[...] = r[:, :LANES]
                proj_ref[:, n * 384:n * 384 + M_W] = r[:, LANES:].astype(BF16)

    return pl.pallas_call(
        body, name="inproj_fwd", grid=(T // tm,),
        in_specs=[BS((tm, D), lambda i: (i, 0)), BS((1, D), lambda i: (0, 0)),
                  BS((ns, P_COLS, dsh), lambda i: (0, 0, 0))],
        out_specs=[BS((tm, D), lambda i: (i, 0)), BS((tm, P_MAIN), lambda i: (i, 0)), BS((tm, LANES), lambda i: (i, 0))],
        out_shape=[jax.ShapeDtypeStruct((T, D), BF16), jax.ShapeDtypeStruct((T, P_MAIN), BF16),
                   jax.ShapeDtypeStruct((T, LANES), F32)],
        compiler_params=_cp(("arbitrary",)),
    )(x2d, g_pre, w_in_p)


def _gate_fwd(flog3, bf_row):
    Bl, S, _ = flog3.shape
    nb = S // LANES

    def body(f_ref, b_ref, bq_ref, bk_ref, fr_ref):
        row = _iota((LANES, LANES), 0)
        lane = _iota((LANES, LANES), 1)
        one = jnp.ones((LANES, LANES), BF16)
        zero = jnp.zeros((LANES, LANES), BF16)

        def blk(j, carry):
            r0 = pl.multiple_of(j * LANES, LANES)
            fl = f_ref[0, pl.ds(r0, LANES), :] + b_ref[...]
            fr_ref[0, j] = fl.T[0:8, :]
            c = _logsig(fl)
            for k in (1, 2, 4, 8, 16, 32, 64):
                c = c + jnp.where(row >= k, pltpu.roll(c, k, 0), 0.0)
            c = c + carry
            for h in range(N_FOX_HEADS):
                col = jnp.sum(jnp.where(lane == h, c, 0.0), axis=1, keepdims=True)
                hi = col.astype(BF16)
                rest = col - hi.astype(F32)
                mid = rest.astype(BF16)
                lo = (rest - mid.astype(F32)).astype(BF16)
                base = _bias_lane(h)
                bq = jnp.where(lane == base, hi, jnp.where(lane == base + 1, mid, jnp.where(lane == base + 2, lo, zero)))
                bq = jnp.where((lane >= base + 3) & (lane < base + 6), one, bq)
                bk = jnp.where(lane == base + 3, -hi, jnp.where(lane == base + 4, -mid, jnp.where(lane == base + 5, -lo, zero)))
                bk = jnp.where((lane >= base) & (lane < base + 3), one, bk)
                bq_ref[0, h, pl.ds(r0, LANES), :] = bq
                bk_ref[0, h, pl.ds(r0, LANES), :] = bk
            return _colsum(jnp.where(row == LANES - 1, c, 0.0))

        lax.fori_loop(0, nb, blk, jnp.zeros((1, LANES), F32))

    slab = BS((1, N_FOX_HEADS, S, LANES), lambda b: (b, 0, 0, 0))
    return pl.pallas_call(
        body, name="gate_fwd", grid=(Bl,),
        in_specs=[BS((1, S, LANES), lambda b: (b, 0, 0)), BS((1, LANES), lambda b: (0, 0))],
        out_specs=[slab, slab, BS((1, nb, 8, LANES), lambda b: (b, 0, 0, 0))],
        out_shape=[jax.ShapeDtypeStruct((Bl, N_FOX_HEADS, S, LANES), BF16),
                   jax.ShapeDtypeStruct((Bl, N_FOX_HEADS, S, LANES), BF16),
                   jax.ShapeDtypeStruct((Bl, nb, 8, LANES), F32)],
        compiler_params=_cp(("arbitrary",)),
    )(flog3, bf_row)


def _bias_lane(h):
    return HEAD if h % 2 == 0 else 0


def _sgu_pre(zu, zv, g_sgu):
    return _gelu(zu), _rms(_gelu(zv), g_sgu)


def _sgu_fwd(proj, g_sgu, ws_tril, bs_full, tm):
    T = proj.shape[0]
    nch = tm // CHUNK

    def body(zu_ref, zv_ref, g_ref, ws_ref, b_ref, ya_ref):
        lane = _iota((CHUNK, LANES), 1)
        u, vn = _sgu_pre(zu_ref[...].astype(F32), zv_ref[...].astype(F32), g_ref[...])
        vn = vn.astype(BF16)
        for c in range(nch):
            rs = slice(c * CHUNK, (c + 1) * CHUNK)
            for j in range(3):
                cs = slice(j * LANES, (j + 1) * LANES)
                vp = vn[rs, cs]
                z = jnp.where(lane < HEAD, _dot(ws_ref[2 * j], vp), _dot(ws_ref[2 * j + 1], vp)) + b_ref[:, cs]
                ya_ref[rs, cs] = u[rs, cs] * z

    return pl.pallas_call(
        body, name="sgu_fwd", grid=(T // tm,),
        in_specs=[BS((tm, A_W), lambda i: (i, 0)), BS((tm, A_W), lambda i: (i, 1)), BS((1, A_W), lambda i: (0, 0)),
                  BS((6, CHUNK, CHUNK), lambda i: (0, 0, 0)), BS((CHUNK, A_W), lambda i: (0, 0))],
        out_specs=BS((tm, A_W), lambda i: (i, 0)),
        out_shape=jax.ShapeDtypeStruct((T, A_W), F32),
        compiler_params=_cp(("arbitrary",)),
    )(proj, proj, g_sgu, ws_tril, bs_full)


def _fox_fwd(proj, bq, bk, Bl, S):
    T = Bl * S
    nq = S // Q_BLK
    qc, kc, vc = 768 // LANES, 1152 // LANES, 1536 // LANES

    def body(q_ref, k_ref, v_ref, bq_ref, bk_ref, o_ref, lse_ref, ka_ref, va_ref):
        lane_s = _iota((S, LANES), 1)
        lane = _iota((Q_BLK, LANES), 1)
        tri = _iota((Q_BLK, Q_BLK), 1) <= _iota((Q_BLK, Q_BLK), 0)
        k = k_ref[...]
        v = v_ref[...]
        for hh in range(2):
            data = (lane_s < HEAD) if hh == 0 else (lane_s >= HEAD)
            ka_ref[hh] = jnp.where(data, k, bk_ref[0, hh])
            va_ref[hh] = jnp.where(lane_s == _bias_lane(hh), jnp.ones_like(v), v)
        for i in range(nq):
            r0 = i * Q_BLK
            q = q_ref[r0:r0 + Q_BLK, :]
            o_out = jnp.zeros((Q_BLK, LANES), F32)
            lse_out = jnp.zeros((Q_BLK, LANES), F32)
            for hh in range(2):
                hmask = (lane < HEAD) if hh == 0 else (lane >= HEAD)
                qa = jnp.where(hmask, q * 0.125, bq_ref[0, hh, r0:r0 + Q_BLK, :])
                sd = jnp.where(tri, _dot_nt(qa, ka_ref[hh, r0:r0 + Q_BLK, :]), NEG)
                m = jnp.max(sd, axis=1, keepdims=True)
                if i:
                    sf = _dot_nt(qa, ka_ref[hh, 0:r0, :])
                    m = jnp.maximum(m, jnp.max(sf, axis=1, keepdims=True))
                acc = _dot(jnp.exp(sd - m), va_ref[hh, r0:r0 + Q_BLK, :])
                if i:
                    acc = acc + _dot(jnp.exp(sf - m), va_ref[hh, 0:r0, :])
                l = jnp.sum(jnp.where(lane == _bias_lane(hh), acc, 0.0), axis=1, keepdims=True)
                o_out = jnp.where(hmask, acc / l, o_out)
                lse_out = jnp.where(hmask, m + jnp.log(l), lse_out)
            o_ref[r0:r0 + Q_BLK, :] = o_out
            lse_ref[0, r0:r0 + Q_BLK, :] = lse_out

    seq = lambda c0: BS((S, LANES), lambda b, p: (b, c0 + p))
    pair = BS((1, 2, S, LANES), lambda b, p: (b, p, 0, 0))
    return pl.pallas_call(
        body, name="fox_fwd", grid=(Bl, 3),
        in_specs=[seq(qc), seq(kc), seq(vc), pair, pair],
        out_specs=[seq(0), BS((1, S, LANES), lambda b, p: (p, b, 0))],
        out_shape=[jax.ShapeDtypeStruct((T, B_W), F32), jax.ShapeDtypeStruct((3, T, LANES), F32)],
        scratch_shapes=[pltpu.VMEM((2, S, LANES), BF16), pltpu.VMEM((2, S, LANES), BF16)],
        compiler_params=_cp(("arbitrary", "arbitrary")),
    )(proj, proj, proj, bq, bk)


def _memkv_fwd(mem, g_mem, w_kv):
    Bl, Mt, D = mem.shape

    def body(m_ref, g_ref, w_ref, mn_ref, kv_ref):
        mn = _rms(m_ref[0], g_ref[...]).astype(BF16)
        mn_ref[0] = mn
        kv_ref[0] = jnp.dot(mn, w_ref[...], preferred_element_type=F32).astype(BF16)

    return pl.pallas_call(
        body, name="memkv_fwd", grid=(Bl,),
        in_specs=[BS((1, Mt, D), lambda b: (b, 0, 0)), BS((1, D), lambda b: (0, 0)), BS((D, 2 * M_W), lambda b: (0, 0))],
        out_specs=[BS((1, Mt, D), lambda b: (b, 0, 0)), BS((1, Mt, 2 * M_W), lambda b: (b, 0, 0))],
        out_shape=[jax.ShapeDtypeStruct((Bl, Mt, D), BF16), jax.ShapeDtypeStruct((Bl, Mt, 2 * M_W), BF16)],
        compiler_params=_cp(("arbitrary",)),
    )(mem, g_mem, w_kv)


def _memattn_fwd(proj, kv, Bl, S, tq):
    T = Bl * S
    nq = S // tq
    Mt = kv.shape[1]
    qc = 1920 // LANES

    def body(q_ref, km_ref, vm_ref, o_ref):
        lane = _iota((tq, LANES), 1)
        q = q_ref[...]
        out = jnp.zeros((tq, LANES), F32)
        for hh in range(2):
            hmask = (lane < HEAD) if hh == 0 else (lane >= HEAD)
            qs = jnp.where(hmask, q, jnp.zeros_like(q)) * 0.125
            s = _dot_nt(qs, km_ref[0])
            pe = jnp.exp(s - jnp.max(s, axis=1, keepdims=True))
            pn = pe / jnp.sum(pe, axis=1, keepdims=True)
            out = jnp.where(hmask, _dot(pn, vm_ref[0]), out)
        o_ref[...] = out

    return pl.pallas_call(
        body, name="memattn_fwd", grid=(Bl, 2, nq),
        in_specs=[BS((tq, LANES), lambda b, p, i: (b * nq + i, qc + p)),
                  BS((1, Mt, LANES), lambda b, p, i: (b, 0, p)),
                  BS((1, Mt, LANES), lambda b, p, i: (b, 0, 2 + p))],
        out_specs=BS((tq, LANES), lambda b, p, i: (b * nq + i, p)),
        out_shape=jax.ShapeDtypeStruct((T, M_W), F32),
        compiler_params=_cp(("arbitrary", "arbitrary", "arbitrary")),
    )(proj, kv, kv)


def _mix_norms(ya, yb, ym, ga, gb, gm):
    return _rms(ya, ga), _rms(yb, gb), _rms(ym, gm)


def _outproj_fwd(ya, yb, ym, x2d, ga, gb, gm, g_post, g_pre2, w_out, tm):
    T, D = x2d.shape

    def body(ya_ref, yb_ref, ym_ref, x_ref, ga_ref, gb_ref, gm_ref, gp_ref, g2_ref, w_ref,
             y_ref, o_ref, x1_ref, h2_ref):
        na, nb_, nm = _mix_norms(ya_ref[...], yb_ref[...], ym_ref[...], ga_ref[...], gb_ref[...], gm_ref[...])
        y_ref[:, 0:A_W] = na.astype(BF16)
        y_ref[:, A_W:A_W + B_W] = nb_.astype(BF16)
        y_ref[:, A_W + B_W:] = nm.astype(BF16)
        o = jnp.dot(y_ref[...], w_ref[...], preferred_element_type=F32)
        o_ref[...] = o
        x1 = x_ref[...] + _rms(o, gp_ref[...])
        x1_ref[...] = x1
        h2_ref[...] = _rms(x1, g2_ref[...]).astype(BF16)

    row = lambda w: BS((tm, w), lambda i: (i, 0))
    vec = lambda w: BS((1, w), lambda i: (0, 0))
    return pl.pallas_call(
        body, name="outproj_fwd", grid=(T // tm,),
        in_specs=[row(A_W), row(B_W), row(M_W), row(D), vec(A_W), vec(B_W), vec(M_W), vec(D), vec(D),
                  BS((A_W + B_W + M_W, D), lambda i: (0, 0))],
        out_specs=[row(A_W + B_W + M_W), row(D), row(D), row(D)],
        out_shape=[jax.ShapeDtypeStruct((T, A_W + B_W + M_W), BF16), jax.ShapeDtypeStruct((T, D), F32),
                   jax.ShapeDtypeStruct((T, D), F32), jax.ShapeDtypeStruct((T, D), BF16)],
        compiler_params=_cp(("arbitrary",)),
    )(ya, yb, ym, x2d, ga, gb, gm, g_post, g_pre2, w_out)


def _ffn_fwd(h2, x1, target, wg, wu, wd, g_post, tm):
    T, D = x1.shape
    ns, F, _ = wg.shape

    def body(h_ref, x1_ref, t_ref, wg_ref, wu_ref, wd_ref, gp_ref,
             gs_ref, us_ref, dff_ref, dx2_ref, dgp_ref, loss_ref, acc_ref):
        i = pl.program_id(0)
        j = pl.program_id(1)
        h = h_ref[...]
        g = _dot_nt(h, wg_ref[0])
        u = _dot_nt(h, wu_ref[0])
        gs_ref[0] = g.astype(BF16)
        us_ref[0] = u.astype(BF16)
        part = _dot(_silu_mul(g, u), wd_ref[0])
        _acc(acc_ref, part, j == 0)

        @pl.when(j == ns - 1)
        def _():
            normed, vjp = jax.vjp(_rms, acc_ref[...], gp_ref[...])
            diff = x1_ref[...] + normed - t_ref[...]
            dx2 = diff * (1.0 / D)
            dff, dgp = vjp(dx2)
            dx2_ref[...] = dx2
            dff_ref[...] = dff.astype(BF16)
            lpart = jnp.sum(_colsum(diff * diff), axis=1, keepdims=True) * (0.5 / D)
            _acc(dgp_ref, dgp, i == 0)
            _acc(loss_ref, jnp.broadcast_to(lpart, (1, LANES)), i == 0)

    row = lambda w: BS((tm, w), lambda i, j: (i, 0))
    return pl.pallas_call(
        body, name="ffn_fwd", grid=(T // tm, ns),
        in_specs=[row(D), row(D), row(D), BS((1, F, D), lambda i, j: (j, 0, 0)), BS((1, F, D), lambda i, j: (j, 0, 0)),
                  BS((1, F, D), lambda i, j: (j, 0, 0)), BS((1, D), lambda i, j: (0, 0))],
        out_specs=[BS((1, tm, F), lambda i, j: (j, i, 0)), BS((1, tm, F), lambda i, j: (j, i, 0)), row(D), row(D),
                   BS((1, D), lambda i, j: (0, 0)), BS((1, LANES), lambda i, j: (0, 0))],
        out_shape=[jax.ShapeDtypeStruct((ns, T, F), BF16), jax.ShapeDtypeStruct((ns, T, F), BF16),
                   jax.ShapeDtypeStruct((T, D), BF16), jax.ShapeDtypeStruct((T, D), F32),
                   jax.ShapeDtypeStruct((1, D), F32), jax.ShapeDtypeStruct((1, LANES), F32)],
        scratch_shapes=[pltpu.VMEM((tm, D), F32)],
        compiler_params=_cp(("arbitrary", "arbitrary")),
    )(h2, x1, target, wg, wu, wd, g_post)


def _ffn_bwd(dff, h2, gs, us, wg, wu, wd, tm):
    T, D = h2.shape
    ns, F, _ = wg.shape

    def body(dff_ref, h_ref, gs_ref, us_ref, wg_ref, wu_ref, wd_ref, dh_ref, dwg_ref, dwu_ref, dwd_ref):
        first = pl.program_id(1) == 0
        dff = dff_ref[...]
        h = h_ref[...]
        dact = _dot_nt(dff, wd_ref[0])
        g = gs_ref[0].astype(F32)
        u = us_ref[0].astype(F32)
        sig = _sigmoid(g)
        gsig = g * sig
        a = gsig * u
        dg = (dact * u * (sig + gsig * (1.0 - sig))).astype(BF16)
        du = (dact * gsig).astype(BF16)
        dh_ref[0] = (_dot(dg, wg_ref[0]) + _dot(du, wu_ref[0])).astype(BF16)
        _acc(dwd_ref, _dot_tn(a, dff)[None], first)
        _acc(dwg_ref, _dot_tn(dg, h)[None], first)
        _acc(dwu_ref, _dot_tn(du, h)[None], first)

    row = BS((tm, D), lambda j, i: (i, 0))
    sh = BS((1, tm, F), lambda j, i: (j, i, 0))
    wsh = BS((1, F, D), lambda j, i: (j, 0, 0))
    return pl.pallas_call(
        body, name="ffn_bwd", grid=(ns, T // tm),
        in_specs=[row, row, sh, sh, wsh, wsh, wsh],
        out_specs=[BS((1, tm, D), lambda j, i: (j, i, 0)), wsh, wsh, wsh],
        out_shape=[jax.ShapeDtypeStruct((ns, T, D), BF16)] + [jax.ShapeDtypeStruct((ns, F, D), F32)] * 3,
        compiler_params=_cp(("arbitrary", "arbitrary")),
    )(dff, h2, gs, us, wg, wu, wd)


def _mm_tn(a, b, name, tk):
    T, M = a.shape
    N = b.shape[1]
    tk = min(tk, T)

    def body(a_ref, b_ref, o_ref):
        _acc(o_ref, _dot_tn(a_ref[...], b_ref[...]), pl.program_id(0) == 0)

    return pl.pallas_call(
        body, name=name, grid=(T // tk,),
        in_specs=[BS((tk, M), lambda t: (t, 0)), BS((tk, N), lambda t: (t, 0))],
        out_specs=BS((M, N), lambda t: (0, 0)),
        out_shape=jax.ShapeDtypeStruct((M, N), F32),
        compiler_params=_cp(("arbitrary",)),
    )(a, b)


def _dw_in(dproj, h, ns, tk):
    T, M = dproj.shape
    D = h.shape[1]
    dsh = D // ns
    tk = min(tk, T)

    def body(a_ref, b_ref, o_ref, acc_ref):
        t = pl.program_id(0)
        _acc(acc_ref, _dot_tn(b_ref[...], a_ref[...]), t == 0)

        @pl.when(t == pl.num_programs(0) - 1)
        def _():
            for s in range(ns):
                o_ref[s] = acc_ref[s * dsh:(s + 1) * dsh, :].T

    return pl.pallas_call(
        body, name="dw_in", grid=(T // tk,),
        in_specs=[BS((tk, M), lambda t: (t, 0)), BS((tk, D), lambda t: (t, 0))],
        out_specs=BS((ns, M, dsh), lambda t: (0, 0, 0)),
        out_shape=jax.ShapeDtypeStruct((ns, M, dsh), F32),
        scratch_shapes=[pltpu.VMEM((D, M), F32)],
        compiler_params=_cp(("arbitrary",)),
    )(dproj, h)


def _outproj_bwd(dh2, x1, dx2, o, ya, yb, ym, ga, gb, gm, g_post, g_pre2, w_out, tm):
    T, D = x1.shape
    ns = dh2.shape[0]

    def body(dh_ref, x1_ref, dx2_ref, o_ref, ya_ref, yb_ref, ym_ref, ga_ref, gb_ref, gm_ref, gp_ref, g2_ref, w_ref,
             dx1_ref, do_ref, dya_ref, dyb_ref, dym_ref, dga_ref, dgb_ref, dgm_ref, dgp_ref, dg2_ref):
        first = pl.program_id(0) == 0
        dh = dh_ref[0].astype(F32)
        for j in range(1, ns):
            dh = dh + dh_ref[j].astype(F32)
        _, vjp0 = jax.vjp(_rms, x1_ref[...], g2_ref[...])
        dxa, dg2 = vjp0(dh)
        dx1 = dx2_ref[...] + dxa
        dx1_ref[...] = dx1
        _acc(dg2_ref, dg2, first)
        _, vjp = jax.vjp(_rms, o_ref[...], gp_ref[...])
        do, dgp = vjp(dx1)
        do = do.astype(BF16)
        do_ref[...] = do
        dy = _dot_nt(do, w_ref[...])
        _, vjp2 = jax.vjp(_mix_norms, ya_ref[...], yb_ref[...], ym_ref[...], ga_ref[...], gb_ref[...], gm_ref[...])
        dya, dyb, dym, dga, dgb, dgm = vjp2((dy[:, 0:A_W], dy[:, A_W:A_W + B_W], dy[:, A_W + B_W:]))
        dya_ref[...] = dya
        dyb_ref[...] = dyb
        dym_ref[...] = dym
        _acc(dga_ref, dga, first)
        _acc(dgb_ref, dgb, first)
        _acc(dgm_ref, dgm, first)
        _acc(dgp_ref, dgp, first)

    row = lambda w: BS((tm, w), lambda i: (i, 0))
    vec = lambda w: BS((1, w), lambda i: (0, 0))
    sds = jax.ShapeDtypeStruct
    return pl.pallas_call(
        body, name="outproj_bwd", grid=(T // tm,),
        in_specs=[BS((ns, tm, D), lambda i: (0, i, 0)), row(D), row(D), row(D), row(A_W), row(B_W), row(M_W),
                  vec(A_W), vec(B_W), vec(M_W), vec(D), vec(D), BS((A_W + B_W + M_W, D), lambda i: (0, 0))],
        out_specs=[row(D), row(D), row(A_W), row(B_W), row(M_W), vec(A_W), vec(B_W), vec(M_W), vec(D), vec(D)],
        out_shape=[sds((T, D), F32), sds((T, D), BF16), sds((T, A_W), F32), sds((T, B_W), F32), sds((T, M_W), F32),
                   sds((1, A_W), F32), sds((1, B_W), F32), sds((1, M_W), F32), sds((1, D), F32), sds((1, D), F32)],
        compiler_params=_cp(("arbitrary",)),
    )(dh2, x1, dx2, o, ya, yb, ym, ga, gb, gm, g_post, g_pre2, w_out)


def _sgu_bwd(proj, dya, g_sgu, ws_tril, bs_full, tm):
    T = proj.shape[0]
    nch = tm // CHUNK

    def body(zu_ref, zv_ref, dy_ref, g_ref, ws_ref, b_ref, dzu_ref, dzv_ref, dws_ref, dbs_ref, dg_ref,
             du_ref, dvn_ref, dbf_ref):
        step = pl.program_id(0)
        first = step == 0
        lane = _iota((CHUNK, LANES), 1)
        tril = _iota((CHUNK, CHUNK), 0) >= _iota((CHUNK, CHUNK), 1)
        (u, vn), vjp = jax.vjp(_sgu_pre, zu_ref[...].astype(F32), zv_ref[...].astype(F32), g_ref[...])
        vnb = vn.astype(BF16)
        dy = dy_ref[...]

        @pl.when(first)
        def _():
            dws_ref[...] = jnp.zeros_like(dws_ref)
            dbf_ref[...] = jnp.zeros_like(dbf_ref)

        for c in range(nch):
            rs = slice(c * CHUNK, (c + 1) * CHUNK)
            for j in range(3):
                cs = slice(j * LANES, (j + 1) * LANES)
                vp = vnb[rs, cs]
                z = jnp.where(lane < HEAD, _dot(ws_ref[2 * j], vp), _dot(ws_ref[2 * j + 1], vp)) + b_ref[:, cs]
                du_ref[rs, cs] = dy[rs, cs] * z
                dz = dy[rs, cs] * u[rs, cs]
                dbf_ref[:, cs] += dz
                dzb = dz.astype(BF16)
                dz0 = jnp.where(lane < HEAD, dzb, jnp.zeros_like(dzb))
                dz1 = jnp.where(lane >= HEAD, dzb, jnp.zeros_like(dzb))
                dvn_ref[rs, cs] = jnp.where(lane < HEAD, _dot_tn(ws_ref[2 * j], dzb), _dot_tn(ws_ref[2 * j + 1], dzb))
                dws_ref[2 * j] += jnp.where(tril, _dot_nt(dz0, vp), 0.0)
                dws_ref[2 * j + 1] += jnp.where(tril, _dot_nt(dz1, vp), 0.0)
        dzu, dzv, dg = vjp((du_ref[...], dvn_ref[...]))
        dzu_ref[...] = dzu.astype(BF16)
        dzv_ref[...] = dzv.astype(BF16)
        _acc(dg_ref, dg, first)

        @pl.when(step == pl.num_programs(0) - 1)
        def _():
            out = jnp.zeros((CHUNK, LANES), F32)
            for j in range(3):
                slab = dbf_ref[:, j * LANES:(j + 1) * LANES]
                lo = jnp.sum(jnp.where(lane < HEAD, slab, 0.0), axis=1, keepdims=True)
                hi = jnp.sum(jnp.where(lane >= HEAD, slab, 0.0), axis=1, keepdims=True)
                out = out + jnp.where(lane == 2 * j, lo, 0.0) + jnp.where(lane == 2 * j + 1, hi, 0.0)
            dbs_ref[...] = out

    return pl.pallas_call(
        body, name="sgu_bwd", grid=(T // tm,),
        in_specs=[BS((tm, A_W), lambda i: (i, 0)), BS((tm, A_W), lambda i: (i, 1)), BS((tm, A_W), lambda i: (i, 0)),
                  BS((1, A_W), lambda i: (0, 0)), BS((6, CHUNK, CHUNK), lambda i: (0, 0, 0)),
                  BS((CHUNK, A_W), lambda i: (0, 0))],
        out_specs=[BS((tm, A_W), lambda i: (i, 0)), BS((tm, A_W), lambda i: (i, 0)),
                   BS((6, CHUNK, CHUNK), lambda i: (0, 0, 0)), BS((CHUNK, LANES), lambda i: (0, 0)),
                   BS((1, A_W), lambda i: (0, 0))],
        out_shape=[jax.ShapeDtypeStruct((T, A_W), BF16), jax.ShapeDtypeStruct((T, A_W), BF16),
                   jax.ShapeDtypeStruct((6, CHUNK, CHUNK), F32), jax.ShapeDtypeStruct((CHUNK, LANES), F32),
                   jax.ShapeDtypeStruct((1, A_W), F32)],
        scratch_shapes=[pltpu.VMEM((tm, A_W), F32), pltpu.VMEM((tm, A_W), F32), pltpu.VMEM((CHUNK, A_W), F32)],
        compiler_params=_cp(("arbitrary",)),
    )(proj, proj, dya, g_sgu, ws_tril, bs_full)


def _memattn_bwd(proj, kv, dym, Bl, S, tq):
    T = Bl * S
    nq = S // tq
    Mt = kv.shape[1]
    qc = 1920 // LANES

    def body(q_ref, km_ref, vm_ref, do_ref, dq_ref, dkm_ref, dvm_ref):
        first = pl.program_id(2) == 0
        lane = _iota((tq, LANES), 1)
        q = q_ref[...]
        do = do_ref[...]
        dq_out = jnp.zeros((tq, LANES), F32)
        dkm = jnp.zeros((Mt, LANES), F32)
        dvm = jnp.zeros((Mt, LANES), F32)
        for hh in range(2):
            hmask = (lane < HEAD) if hh == 0 else (lane >= HEAD)
            qs = jnp.where(hmask, q, jnp.zeros_like(q)) * 0.125
            dom = jnp.where(hmask, do, 0.0).astype(BF16)
            s = _dot_nt(qs, km_ref[0])
            pe = jnp.exp(s - jnp.max(s, axis=1, keepdims=True))
            pn = pe / jnp.sum(pe, axis=1, keepdims=True)
            dp = _dot_nt(dom, vm_ref[0])
            ds = (pn * (dp - jnp.sum(pn * dp, axis=1, keepdims=True))).astype(BF16)
            dq_out = jnp.where(hmask, _dot(ds, km_ref[0]) * 0.125, dq_out)
            dkm = dkm + _dot_tn(ds, qs)
            dvm = dvm + _dot_tn(pn, dom)
        dq_ref[...] = dq_out.astype(BF16)
        _acc(dkm_ref, dkm[None], first)
        _acc(dvm_ref, dvm[None], first)

    return pl.pallas_call(
        body, name="memattn_bwd", grid=(Bl, 2, nq),
        in_specs=[BS((tq, LANES), lambda b, p, i: (b * nq + i, qc + p)),
                  BS((1, Mt, LANES), lambda b, p, i: (b, 0, p)),
                  BS((1, Mt, LANES), lambda b, p, i: (b, 0, 2 + p)),
                  BS((tq, LANES), lambda b, p, i: (b * nq + i, p))],
        out_specs=[BS((tq, LANES), lambda b, p, i: (b * nq + i, p)),
                   BS((1, Mt, LANES), lambda b, p, i: (b, 0, p)),
                   BS((1, Mt, LANES), lambda b, p, i: (b, 0, p))],
        out_shape=[jax.ShapeDtypeStruct((T, M_W), BF16), jax.ShapeDtypeStruct((Bl, Mt, M_W), F32),
                   jax.ShapeDtypeStruct((Bl, Mt, M_W), F32)],
        compiler_params=_cp(("arbitrary", "arbitrary", "arbitrary")),
    )(proj, kv, kv, dym)


def _memkv_bwd(dkm, dvm, memn, mem, g_mem, w_kv):
    Bl, Mt, D = mem.shape

    def body(dk_ref, dv_ref, mn_ref, m_ref, g_ref, w_ref, dw_ref, dg_ref):
        first = pl.program_id(0) == 0
        dk = dk_ref[0].astype(BF16)
        dv = dv_ref[0].astype(BF16)
        mn = mn_ref[0]
        dmn = _dot_nt(dk, w_ref[:, 0:M_W]) + _dot_nt(dv, w_ref[:, M_W:])
        _, vjp = jax.vjp(_rms, m_ref[0], g_ref[...])
        _, dg = vjp(dmn)
        _acc(dg_ref, dg, first)

        @pl.when(first)
        def _():
            dw_ref[...] = jnp.zeros_like(dw_ref)

        dw_ref[:, 0:M_W] += _dot_tn(mn, dk)
        dw_ref[:, M_W:] += _dot_tn(mn, dv)

    return pl.pallas_call(
        body, name="memkv_bwd", grid=(Bl,),
        in_specs=[BS((1, Mt, M_W), lambda b: (b, 0, 0)), BS((1, Mt, M_W), lambda b: (b, 0, 0)),
                  BS((1, Mt, D), lambda b: (b, 0, 0)), BS((1, Mt, D), lambda b: (b, 0, 0)),
                  BS((1, D), lambda b: (0, 0)), BS((D, 2 * M_W), lambda b: (0, 0))],
        out_specs=[BS((D, 2 * M_W), lambda b: (0, 0)), BS((1, D), lambda b: (0, 0))],
        out_shape=[jax.ShapeDtypeStruct((D, 2 * M_W), F32), jax.ShapeDtypeStruct((1, D), F32)],
        compiler_params=_cp(("arbitrary",)),
    )(dkm, dvm, memn, mem, g_mem, w_kv)


def _fox_bwd(proj, dyb, lse, bq, bk, Bl, S):
    T = Bl * S
    nq = S // Q_BLK
    nb = S // LANES
    qc, kc, vc = 768 // LANES, 1152 // LANES, 1536 // LANES

    def body(q_ref, k_ref, v_ref, do_ref, lse_ref, bq_ref, bk_ref,
             dq_ref, dk_ref, dv_ref, dcr_ref, ka_ref, dka_ref, dva_ref):
        p = pl.program_id(1)
        lane_s = _iota((S, LANES), 1)
        lane = _iota((Q_BLK, LANES), 1)
        sub = _iota((8, LANES), 0)
        tri = _iota((Q_BLK, Q_BLK), 1) <= _iota((Q_BLK, Q_BLK), 0)
        k = k_ref[...]
        for hh in range(2):
            data = (lane_s < HEAD) if hh == 0 else (lane_s >= HEAD)
            ka_ref[hh] = jnp.where(data, k, bk_ref[0, hh])
        dka_ref[...] = jnp.zeros_like(dka_ref)
        dva_ref[...] = jnp.zeros_like(dva_ref)

        @pl.when(p == 0)
        def _():
            dcr_ref[...] = jnp.zeros_like(dcr_ref)

        def add_colsums(ds, first_blk, h):
            cs = _colsum(ds)
            for jb in range(ds.shape[1] // LANES):
                dcr_ref[0, first_blk + jb] += jnp.where(sub == h, cs[:, jb * LANES:(jb + 1) * LANES], 0.0)

        for i in range(nq):
            r0 = i * Q_BLK
            r1 = r0 + Q_BLK
            q = q_ref[r0:r1, :]
            do = do_ref[r0:r1, :]
            lse_b = lse_ref[0, r0:r1, :]
            dq_out = jnp.zeros((Q_BLK, LANES), F32)
            for hh in range(2):
                hmask = (lane < HEAD) if hh == 0 else (lane >= HEAD)
                h = 2 * p + hh
                qs = jnp.where(hmask, q * 0.125, jnp.zeros_like(q))
                qa = jnp.where(hmask, q * 0.125, bq_ref[0, hh, r0:r1, :])
                dob = jnp.where(hmask, do, 0.0).astype(BF16)
                lse_h = jnp.sum(jnp.where(lane == hh * HEAD, lse_b, 0.0), axis=1, keepdims=True)
                pd = jnp.where(tri, jnp.exp(_dot_nt(qa, ka_ref[hh, r0:r1, :]) - lse_h), 0.0)
                dpd = _dot_nt(dob, v_ref[r0:r1, :])
                delta = jnp.sum(pd * dpd, axis=1, keepdims=True)
                psum = jnp.sum(pd, axis=1, keepdims=True)
                if i:
                    pf = jnp.exp(_dot_nt(qa, ka_ref[hh, 0:r0, :]) - lse_h)
                    dpf = _dot_nt(dob, v_ref[0:r0, :])
                    delta = delta + jnp.sum(pf * dpf, axis=1, keepdims=True)
                    psum = psum + jnp.sum(pf, axis=1, keepdims=True)
                delta = delta / psum
                dsd = pd * (dpd - delta)
                add_colsums(dsd, r0 // LANES, h)
                dsd = dsd.astype(BF16)
                dq_h = _dot(dsd, k_ref[r0:r1, :])
                dka_ref[r0:r1, :] += _dot_tn(dsd, qs)
                dva_ref[r0:r1, :] += _dot_tn(pd, dob)
                if i:
                    dsf = pf * (dpf - delta)
                    add_colsums(dsf, 0, h)
                    dsf = dsf.astype(BF16)
                    dq_h = dq_h + _dot(dsf, k_ref[0:r0, :])
                    dka_ref[0:r0, :] += _dot_tn(dsf, qs)
                    dva_ref[0:r0, :] += _dot_tn(pf, dob)
                dq_out = jnp.where(hmask, dq_h * 0.125, dq_out)
            dq_ref[r0:r1, :] = dq_out.astype(BF16)
        dk_ref[...] = dka_ref[...].astype(BF16)
        dv_ref[...] = dva_ref[...].astype(BF16)

    seq = lambda c0: BS((S, LANES), lambda b, p: (b, c0 + p))
    pair = BS((1, 2, S, LANES), lambda b, p: (b, p, 0, 0))
    rowblk = BS((1, nb, 8, LANES), lambda b, p: (b, 0, 0, 0))
    return pl.pallas_call(
        body, name="fox_bwd", grid=(Bl, 3),
        in_specs=[seq(qc), seq(kc), seq(vc), seq(0), BS((1, S, LANES), lambda b, p: (p, b, 0)), pair, pair],
        out_specs=[seq(0), seq(0), seq(0), rowblk],
        out_shape=[jax.ShapeDtypeStruct((T, B_W), BF16)] * 3 + [jax.ShapeDtypeStruct((Bl, nb, 8, LANES), F32)],
        scratch_shapes=[pltpu.VMEM((2, S, LANES), BF16), pltpu.VMEM((S, LANES), F32), pltpu.VMEM((S, LANES), F32)],
        compiler_params=_cp(("arbitrary", "arbitrary")),
    )(proj, proj, proj, dyb, lse, bq, bk)


def _gate_bwd(dc_row, fl_row):
    Bl, nb, _, _ = dc_row.shape

    def body(dc_ref, fl_ref, o_ref):
        lane = _iota((8, LANES), 1)

        def blk(jj, carry):
            j = nb - 1 - jj
            r = -dc_ref[0, j]
            for k in (1, 2, 4, 8, 16, 32, 64):
                r = r + jnp.where(lane < LANES - k, pltpu.roll(r, LANES - k, 1), 0.0)
            r = r + carry
            dfl = r * _sigmoid(-fl_ref[0, j])
            o_ref[0, pl.ds(pl.multiple_of(j * LANES, LANES), LANES), :] = jnp.concatenate(
                [dfl, jnp.zeros((LANES - 8, LANES), F32)], axis=0).T
            return jnp.sum(jnp.where(lane == 0, r, 0.0), axis=1, keepdims=True)

        lax.fori_loop(0, nb, blk, jnp.zeros((8, 1), F32))

    rowblk = BS((1, nb, 8, LANES), lambda b: (b, 0, 0, 0))
    return pl.pallas_call(
        body, name="gate_bwd", grid=(Bl,),
        in_specs=[rowblk, rowblk],
        out_specs=BS((1, nb * LANES, LANES), lambda b: (b, 0, 0)),
        out_shape=jax.ShapeDtypeStruct((Bl, nb * LANES, LANES), F32),
        compiler_params=_cp(("arbitrary",)),
    )(dc_row, fl_row)


def _inproj_bwd(dzu, dzv, dq, dk, dv, dqm, dfl, x2d, dx1, g_pre, w_in_p, tm):
    T, D = x2d.shape
    ns, _, dsh = w_in_p.shape

    def body(dzu_ref, dzv_ref, dq_ref, dk_ref, dv_ref, dqm_ref, dfl_ref, x_ref, dx1_ref, g_ref, w_ref,
             dp_ref, gx_ref, dg_ref, dbf_ref):
        first = pl.program_id(0) == 0
        dfl = dfl_ref[...]
        dp_ref[:, 0:384] = dzu_ref[...]
        dp_ref[:, 384:768] = dzv_ref[...]
        dp_ref[:, 768:1152] = dq_ref[...]
        dp_ref[:, 1152:1536] = dk_ref[...]
        dp_ref[:, 1536:1920] = dv_ref[...]
        dp_ref[:, 1920:2048] = dfl.astype(BF16)
        dp_ref[:, 2048:2304] = dqm_ref[...]
        dh = jnp.concatenate([_dot(dp_ref[...], w_ref[s]) for s in range(ns)], axis=1)
        _, vjp = jax.vjp(_rms, x_ref[...], g_ref[...])
        dxa, dg = vjp(dh)
        gx_ref[...] = dx1_ref[...] + dxa
        _acc(dg_ref, dg, first)
        _acc(dbf_ref, _colsum(dfl), first)

    row = lambda w: BS((tm, w), lambda i: (i, 0))
    return pl.pallas_call(
        body, name="inproj_bwd", grid=(T // tm,),
        in_specs=[row(A_W), row(A_W), row(B_W), row(B_W), row(B_W), row(M_W), row(LANES), row(D), row(D),
                  BS((1, D), lambda i: (0, 0)), BS((ns, P_COLS, dsh), lambda i: (0, 0, 0))],
        out_specs=[row(P_COLS), row(D), BS((1, D), lambda i: (0, 0)), BS((1, LANES), lambda i: (0, 0))],
        out_shape=[jax.ShapeDtypeStruct((T, P_COLS), BF16), jax.ShapeDtypeStruct((T, D), F32),
                   jax.ShapeDtypeStruct((1, D), F32), jax.ShapeDtypeStruct((1, LANES), F32)],
        compiler_params=_cp(("arbitrary",)),
    )(dzu, dzv, dq, dk, dv, dqm, dfl, x2d, dx1, g_pre, w_in_p)


def _local_step(x, mem, target, W, P, reduce=None):
    Bl, S, D = x.shape
    T = Bl * S
    tm = min(512, T)
    x2d = x.reshape(T, D)
    t2d = target.reshape(T, D)
    vec = lambda a: a.reshape(1, -1)
    bf_row = jnp.pad(P["b_f"].reshape(1, -1), ((0, 0), (0, LANES - N_FOX_HEADS)))
    tril = jnp.tril(jnp.ones((CHUNK, CHUNK), bool))
    ws_tril = jnp.where(tril[None], P["w_s"][0], 0.0).astype(BF16)
    bs_full = jnp.repeat(P["b_s"][0].T, HEAD, axis=1)
    g_pre, g_sgu = vec(P["g_pre_mix"]), vec(P["g_sgu"])
    ga, gb, gm = vec(P["g_out_a"]), vec(P["g_out_b"]), vec(P["g_out_m"])
    g_mem, g_post, g_pre2, g_post2 = vec(P["g_mem"]), vec(P["g_post_mix"]), vec(P["g_pre_ffn"]), vec(P["g_post_ffn"])

    h, proj, flog = _inproj_fwd(x2d, g_pre, W["w_in"], tm)
    bq, bk, fl_row = _gate_fwd(flog.reshape(Bl, S, LANES), bf_row)
    ya = _sgu_fwd(proj, g_sgu, ws_tril, bs_full, tm)
    yb, lse = _fox_fwd(proj, bq, bk, Bl, S)
    memn, kv = _memkv_fwd(mem, g_mem, W["w_mem_kv"])
    ym = _memattn_fwd(proj, kv, Bl, S, min(512, S))
    y, o, x1, h2 = _outproj_fwd(ya, yb, ym, x2d, ga, gb, gm, g_post, g_pre2, W["w_out"], tm)
    gs, us, dff, dx2, dg_post2, loss = _ffn_fwd(h2, x1, t2d, W["w_gate"], W["w_up"], W["w_down"], g_post2, tm)

    dh2, d_w_gate, d_w_up, d_w_down = _ffn_bwd(dff, h2, gs, us, W["w_gate"], W["w_up"], W["w_down"], tm)
    ffn = [d_w_gate, d_w_up, d_w_down]
    if reduce is not None:
        pending, _ = reduce.begin("ffn", ffn)
    dx1, do, dya, dyb, dym, dga, dgb, dgm, dg_post, dg_pre2 = _outproj_bwd(
        dh2, x1, dx2, o, ya, yb, ym, ga, gb, gm, g_post, g_pre2, W["w_out"], tm)
    if reduce is not None:
        ffn, (do, dya, dyb, dym) = reduce.finish("ffn", pending, (do, dya, dyb, dym))
    d_w_out = _mm_tn(y, do, "dw_out", 1024)
    dzu, dzv, dws, dbs_cols, dg_sgu = _sgu_bwd(proj, dya, g_sgu, ws_tril, bs_full, tm)
    dqm, dkm, dvm = _memattn_bwd(proj, kv, dym, Bl, S, min(512, S))
    d_w_kv, dg_mem = _memkv_bwd(dkm, dvm, memn, mem, g_mem, W["w_mem_kv"])
    mid = [d_w_kv, d_w_out]
    dq, dk, dv, dc_row = _fox_bwd(proj, dyb, lse, bq, bk, Bl, S)
    if reduce is not None:
        done = reduce.apply(BIG[3:], ffn)
        pending, after = reduce.begin("mid", mid, (dc_row,) + done)
        dc_row = after[0]
    dfl = _gate_bwd(dc_row, fl_row).reshape(T, LANES)
    dproj, grad_x, dg_pre, dbf = _inproj_bwd(dzu, dzv, dq, dk, dv, dqm, dfl, x2d, dx1, g_pre, W["w_in"], tm)
    if reduce is not None:
        mid, (dproj,) = reduce.finish("mid", pending, (dproj,))
    d_w_in = _dw_in(dproj, h, W["w_in"].shape[0], 1024)
    if reduce is None:
        big = dict(zip(BIG, [d_w_in] + mid + ffn))
    else:
        done = reduce.apply(BIG[1:3], mid)
        big = {"w_in": reduce.begin("in", [d_w_in], done)[0]}
    small = {"g_pre_mix": dg_pre, "b_f": dbf[:, :N_FOX_HEADS], "g_sgu": dg_sgu, "w_s": dws, "b_s": dbs_cols[:, :N_FOX_HEADS].T,
             "g_out_a": dga, "g_out_b": dgb, "g_out_m": dgm, "g_mem": dg_mem, "g_post_mix": dg_post,
             "g_pre_ffn": dg_pre2, "g_post_ffn": dg_post2, "loss": loss[:, :1]}
    return grad_x.reshape(Bl, S, D), big, small


def _place():
    return lax.axis_index("x"), lax.axis_index("y"), lax.axis_index("c")


def _exchange_on_sequencer(srcs, own_full, name, collective_id):
    n = len(srcs)

    def body(*refs):
        src, dst = refs[:n], refs[n:2 * n]
        lsem, isend, irecv, dsend, drecv = refs[2 * n:]
        x, y, c = _place()
        oc = 1 - c
        s_me = 2 * x + y
        sib = (x, y, oc)
        chips = [(1 - x, y), (x, 1 - y), (1 - x, 1 - y)]
        barrier = pltpu.get_barrier_semaphore()
        for dev in [(cx, cy, c) for cx, cy in chips] + [sib]:
            pl.semaphore_signal(barrier, inc=1, device_id=dev, device_id_type=MESH)
        pl.semaphore_wait(barrier, 4)

        def remote(a, b, ssem, rsem, dev):
            return pltpu.make_async_remote_copy(src_ref=a, dst_ref=b, send_sem=ssem, recv_sem=rsem,
                                                device_id=dev, device_id_type=MESH)

        sends, local = [], []
        for w in range(n):
            for j, (cx, cy) in enumerate(chips):
                half = src[w].at[c] if own_full else src[w].at[2 * cx + cy]
                cp = remote(half, dst[w].at[s_me, c], isend.at[w, j], irecv.at[w, j], (cx, cy, c))
                cp.start()
                sends.append(cp)
            if own_full:
                cp = remote(src[w], dst[w].at[s_me], dsend.at[w, 3], drecv.at[w, 3], sib)
            else:
                cp = remote(src[w].at[s_me], dst[w].at[s_me, c], dsend.at[w, 3], drecv.at[w, 3], sib)
                loc = pltpu.make_async_copy(src[w].at[s_me], dst[w].at[s_me, c], lsem.at[w])
                loc.start()
                local.append(loc)
            cp.start()
            sends.append(cp)
        for w in range(n):
            for j, (cx, cy) in enumerate(chips):
                landed = dst[w].at[2 * cx + cy, c]
                remote(landed, landed, isend.at[w, j], irecv.at[w, j], (cx, cy, c)).wait_recv()
                cp = remote(landed, landed, dsend.at[w, j], drecv.at[w, j], sib)
                cp.start()
                sends.append(cp)
        for w in range(n):
            for j, (cx, cy) in enumerate(chips):
                landed = dst[w].at[2 * cx + cy, oc]
                remote(landed, landed, dsend.at[w, j], drecv.at[w, j], sib).wait_recv()
            landed = dst[w].at[s_me] if own_full else dst[w].at[s_me, oc]
            remote(landed, landed, dsend.at[w, 3], drecv.at[w, 3], sib).wait_recv()
        for cp in sends:
            cp.wait_send()
        for loc in local:
            loc.wait()

    return pl.kernel(
        body, out_type=[jax.ShapeDtypeStruct((4, 2) + s.shape[1:], s.dtype) for s in srcs],
        mesh=plsc.ScalarSubcoreMesh(axis_name="sequencer", num_cores=1), name=name,
        scratch_types=[pltpu.SemaphoreType.DMA((n,)), pltpu.SemaphoreType.DMA((n, 3)), pltpu.SemaphoreType.DMA((n, 3)),
                       pltpu.SemaphoreType.DMA((n, 4)), pltpu.SemaphoreType.DMA((n, 4))],
        compiler_params=pltpu.CompilerParams(collective_id=collective_id),
    )(*srcs)


def _sibling_swap(grads, name, collective_id):
    n = len(grads)

    def body(*refs):
        g, theirs = refs[:n], refs[n:2 * n]
        ssem, rsem = refs[2 * n:]
        x, y, c = _place()
        sib = (x, y, 1 - c)
        barrier = pltpu.get_barrier_semaphore()
        pl.semaphore_signal(barrier, inc=1, device_id=sib, device_id_type=MESH)
        pl.semaphore_wait(barrier, 1)
        cps = []
        for w in range(n):
            cp = pltpu.make_async_remote_copy(src_ref=g[w].at[:, 1 - c], dst_ref=theirs[w], send_sem=ssem.at[w],
                                              recv_sem=rsem.at[w], device_id=sib, device_id_type=MESH)
            cp.start()
            cps.append(cp)
        for cp in cps:
            cp.wait()

    return pl.kernel(
        body, out_type=[jax.ShapeDtypeStruct((4,) + g.shape[2:], g.dtype) for g in grads],
        mesh=plsc.ScalarSubcoreMesh(axis_name="sequencer", num_cores=1), name=name,
        scratch_types=[pltpu.SemaphoreType.DMA((n,)), pltpu.SemaphoreType.DMA((n,))],
        compiler_params=pltpu.CompilerParams(collective_id=collective_id),
    )(*grads)


def _add_pair(core, g, theirs, name):
    _, _, hr, C = g.shape

    def body(core_ref, g_ref, t_ref, o_ref):
        o_ref[0] = (g_ref[0, 0] + t_ref[0]).astype(BF16)

    blk = BS((1, hr, C), lambda s, core_ref: (s, 0, 0))
    return pl.pallas_call(
        body, name=name,
        grid_spec=pltpu.PrefetchScalarGridSpec(
            num_scalar_prefetch=1, grid=(4,),
            in_specs=[BS((1, 1, hr, C), lambda s, core_ref: (s, core_ref[0], 0, 0)), blk], out_specs=blk),
        out_shape=jax.ShapeDtypeStruct(theirs.shape, BF16), compiler_params=_cp(("arbitrary",)))(core, g, theirs)


def _sum_chips(r, name):
    _, _, hr, C = r.shape

    def body(r_ref, o_ref):
        o_ref[...] = ((r_ref[0, 0].astype(F32) + r_ref[1, 0].astype(F32)) + r_ref[2, 0].astype(F32)) + r_ref[3, 0].astype(F32)

    return pl.pallas_call(body, name=name, grid=(2,), in_specs=[BS((4, 1, hr, C), lambda h: (0, h, 0, 0))],
                          out_specs=BS((hr, C), lambda h: (h, 0)), out_shape=jax.ShapeDtypeStruct((2 * hr, C), F32),
                          compiler_params=_cp(("arbitrary",)))(r)


class _Reducer:
    IDS = {"ffn": (4, 5), "mid": (6, 7), "in": (8, 9)}

    def __init__(self, core, apply):
        self.core = core
        self.apply = apply

    def begin(self, tag, grads, after=()):
        grads, after = lax.optimization_barrier((list(grads), after))
        g4 = [g.reshape(4, 2, -1, g.shape[-1]) for g in grads]
        return (g4, _sibling_swap(g4, "swap_" + tag, self.IDS[tag][0])), after

    def finish(self, tag, pending, hold):
        g4, theirs = pending
        sums = [_add_pair(self.core, g, t, "chip_sum_%s_%d" % (tag, k)) for k, (g, t) in enumerate(zip(g4, theirs))]
        sums, hold = lax.optimization_barrier((sums, hold))
        return _exchange_on_sequencer(sums, False, "scatter_" + tag, self.IDS[tag][1]), hold


def _small_allreduce(part):
    R = part.shape[0]
    rs = R // 8
    masks = [(mx, my, mc) for mx in (0, 1) for my in (0, 1) for mc in (0, 1)][1:]

    def body(p_ref, o_ref, buf_ref, s1, r1, s2, r2):
        x, y, c = _place()
        d = 4 * x + 2 * y + c
        mine = pl.ds(pl.multiple_of(d * rs, 8), rs)
        peers = [((x + mx) % 2, (y + my) % 2, (c + mc) % 2) for mx, my, mc in masks]
        first, second = [], []
        for k, (px, py, pc) in enumerate(peers):
            theirs = pl.ds(pl.multiple_of((4 * px + 2 * py + pc) * rs, 8), rs)
            cp = pltpu.make_async_remote_copy(src_ref=p_ref.at[theirs, :], dst_ref=buf_ref.at[d], send_sem=s1.at[k],
                                              recv_sem=r1.at[k], device_id=(px, py, pc), device_id_type=MESH)
            cp.start()
            first.append(cp)
        buf_ref[d] = p_ref[mine, :]
        for k, (px, py, pc) in enumerate(peers):
            slot = buf_ref.at[4 * px + 2 * py + pc]
            pltpu.make_async_remote_copy(src_ref=slot, dst_ref=slot, send_sem=s1.at[k], recv_sem=r1.at[k],
                                         device_id=(px, py, pc), device_id_type=MESH).wait_recv()
        total = buf_ref[0]
        for k in range(1, 8):
            total = total + buf_ref[k]
        o_ref[mine, :] = total
        for k, (px, py, pc) in enumerate(peers):
            cp = pltpu.make_async_remote_copy(src_ref=o_ref.at[mine, :], dst_ref=o_ref.at[mine, :], send_sem=s2.at[k],
                                              recv_sem=r2.at[k], device_id=(px, py, pc), device_id_type=MESH)
            cp.start()
            second.append(cp)
        for k, (px, py, pc) in enumerate(peers):
            rows = o_ref.at[pl.ds(pl.multiple_of((4 * px + 2 * py + pc) * rs, 8), rs), :]
            pltpu.make_async_remote_copy(src_ref=rows, dst_ref=rows, send_sem=s2.at[k], recv_sem=r2.at[k],
                                         device_id=(px, py, pc), device_id_type=MESH).wait_recv()
        for cp in first + second:
            cp.wait_send()

    vm = pl.BlockSpec(memory_space=pltpu.VMEM)
    return pl.pallas_call(
        body, name="small_allreduce", in_specs=[vm], out_specs=vm, out_shape=jax.ShapeDtypeStruct(part.shape, F32),
        scratch_shapes=[pltpu.VMEM((8, rs, LANES), F32)] + [pltpu.SemaphoreType.DMA((7,))] * 4,
    )(part)


def _adamw(w, g, m, v, name):
    R, C = w.shape
    summed = g.ndim == 4
    if summed:
        tr = R // 2
    else:
        tr = R if R * C * 4 <= (1 << 21) else R // 2
        if tr % 8:
            tr = R
    c1 = 1.0 / (1.0 - ADAM_B1 ** ADAM_STEP)
    c2 = 1.0 / (1.0 - ADAM_B2 ** ADAM_STEP)

    def body(w_ref, g_ref, m_ref, v_ref, *outs):
        if summed:
            g_ = ((g_ref[0, 0].astype(F32) + g_ref[1, 0].astype(F32)) + g_ref[2, 0].astype(F32)) + g_ref[3, 0].astype(F32)
            outs[0][...] = g_
        else:
            g_ = g_ref[...]
        d_ref, mo_ref, vo_ref = outs[-3:]
        m_ = ADAM_B1 * m_ref[...] + (1.0 - ADAM_B1) * g_
        v_ = ADAM_B2 * v_ref[...] + (1.0 - ADAM_B2) * (g_ * g_)
        mo_ref[...] = m_
        vo_ref[...] = v_
        d_ref[...] = -ADAM_LR * ((m_ * c1) / (jnp.sqrt(v_ * c2) + ADAM_EPS) + ADAM_WD * w_ref[...])

    blk = BS((tr, C), lambda i: (i, 0))
    g_blk = BS((4, 1, tr, C), lambda i: (0, i, 0, 0)) if summed else blk
    nout = 4 if summed else 3
    return pl.pallas_call(body, name=name, grid=(R // tr,), in_specs=[blk, g_blk, blk, blk], out_specs=[blk] * nout,
                          out_shape=[jax.ShapeDtypeStruct((R, C), F32)] * nout,
                          compiler_params=_cp(("arbitrary",)))(w, g, m, v)


SMALL = ("g_pre_mix", "b_f", "g_sgu", "w_s", "b_s", "g_out_a", "g_out_b", "g_out_m", "g_mem", "g_post_mix",
         "g_pre_ffn", "g_post_ffn")
BIG = ("w_in", "w_mem_kv", "w_out", "w_gate", "w_up", "w_down")
TRANSPOSED = ("w_in", "w_gate", "w_up")
WEIGHTS = ("g_pre_mix", "w_in", "b_f", "g_sgu", "w_s", "b_s", "g_out_a", "g_out_b", "g_out_m", "g_mem", "w_mem_kv",
           "w_out", "g_post_mix", "g_pre_ffn", "w_gate", "w_up", "w_down", "g_post_ffn")


def _rows_of(n):
    return -(-n // (8 * LANES)) * 8


def _pack(parts):
    tiles = []
    for a in parts:
        flat = a.reshape(-1).astype(F32)
        rows = _rows_of(flat.shape[0])
        tiles.append(jnp.pad(flat, (0, rows * LANES - flat.shape[0])).reshape(rows, LANES))
    total = sum(t.shape[0] for t in tiles)
    pad = -total % 64
    if pad:
        tiles.append(jnp.zeros((pad, LANES), F32))
    return jnp.concatenate(tiles, axis=0)


def _unpack(packed, shapes):
    out, r = [], 0
    for shp in shapes:
        n = 1
        for s in shp:
            n *= s
        rows = _rows_of(n)
        out.append(packed[r:r + rows].reshape(-1)[:n].reshape(shp))
        r += rows
    return out


def kernel(x, mem, g_pre_mix, w_in, b_f, g_sgu, w_s, b_s, g_out_a, g_out_b, g_out_m, g_mem, w_mem_kv, w_out, g_post_mix, g_pre_ffn, w_gate, w_up, w_down, g_post_ffn, loss_target, m_g_pre_mix, m_w_in, m_b_f, m_g_sgu, m_w_s, m_b_s, m_g_out_a, m_g_out_b, m_g_out_m, m_g_mem, m_w_mem_kv, m_w_out, m_g_post_mix, m_g_pre_ffn, m_w_gate, m_w_up, m_w_down, m_g_post_ffn, v_g_pre_mix, v_w_in, v_b_f, v_g_sgu, v_w_s, v_b_s, v_g_out_a, v_g_out_b, v_g_out_m, v_g_mem, v_w_mem_kv, v_w_out, v_g_post_mix, v_g_pre_ffn, v_w_gate, v_w_up, v_w_down, v_g_post_ffn):
    Wt = dict(g_pre_mix=g_pre_mix, w_in=w_in, b_f=b_f, g_sgu=g_sgu, w_s=w_s, b_s=b_s, g_out_a=g_out_a, g_out_b=g_out_b,
              g_out_m=g_out_m, g_mem=g_mem, w_mem_kv=w_mem_kv, w_out=w_out, g_post_mix=g_post_mix, g_pre_ffn=g_pre_ffn,
              w_gate=w_gate, w_up=w_up, w_down=w_down, g_post_ffn=g_post_ffn)
    Mo = dict(g_pre_mix=m_g_pre_mix, w_in=m_w_in, b_f=m_b_f, g_sgu=m_g_sgu, w_s=m_w_s, b_s=m_b_s, g_out_a=m_g_out_a,
              g_out_b=m_g_out_b, g_out_m=m_g_out_m, g_mem=m_g_mem, w_mem_kv=m_w_mem_kv, w_out=m_w_out,
              g_post_mix=m_g_post_mix, g_pre_ffn=m_g_pre_ffn, w_gate=m_w_gate, w_up=m_w_up, w_down=m_w_down,
              g_post_ffn=m_g_post_ffn)
    Vo = dict(g_pre_mix=v_g_pre_mix, w_in=v_w_in, b_f=v_b_f, g_sgu=v_g_sgu, w_s=v_w_s, b_s=v_b_s, g_out_a=v_g_out_a,
              g_out_b=v_g_out_b, g_out_m=v_g_out_m, g_mem=v_g_mem, w_mem_kv=v_w_mem_kv, w_out=v_w_out,
              g_post_mix=v_g_post_mix, g_pre_ffn=v_g_pre_ffn, w_gate=v_w_gate, w_up=v_w_up, w_down=v_w_down,
              g_post_ffn=v_g_post_ffn)

    gap = P_COLS - IN_COLS

    def to_kernel(n, w):
        if n in TRANSPOSED:
            w = w.T
        if n == "w_in":
            w = jnp.pad(w[:F_END], ((0, P_COLS - F_END), (0, 0))) + jnp.pad(w[F_END:], ((F_END + gap, 0), (0, 0)))
        return w

    def ungroup(g):
        return jnp.pad(g[:F_END], ((0, IN_COLS - F_END), (0, 0))) + jnp.pad(g[F_END + gap:], ((F_END, 0), (0, 0)))

    shards = {n: to_kernel(n, Wt[n][0]) for n in BIG}
    srcs = [shards[n].astype(BF16).reshape(2, shards[n].shape[0] // 2, shards[n].shape[1]) for n in BIG]
    fulls = (_exchange_on_sequencer(srcs[:1], True, "gather_w_in", 1)
             + _exchange_on_sequencer(srcs[1:3], True, "gather_kv_out", 2)
             + _exchange_on_sequencer(srcs[3:], True, "gather_ffn", 3))
    W = {}
    for n, f in zip(BIG, fulls):
        _, _, hr, C = f.shape
        W[n] = f.reshape(8 * hr, C) if n in ("w_mem_kv", "w_out") else f.reshape(4, 2 * hr, C)

    P = {n: Wt[n] for n in SMALL}
    grads, deltas, new_m, new_v = {}, {}, {}, {}

    def apply(names, landed):
        for n, r in zip(names, landed):
            wmv = [a[n][0].T if n in TRANSPOSED else a[n][0] for a in (Wt, Mo, Vo)]
            if n == "w_in":
                g = ungroup(_sum_chips(r, "sum_chips_" + n))
                g, d, m1, v1 = (g,) + tuple(_adamw(wmv[0], g, wmv[1], wmv[2], "adamw_" + n))
            else:
                g, d, m1, v1 = _adamw(wmv[0], r, wmv[1], wmv[2], "adamw_" + n)
            if n in TRANSPOSED:
                g, d, m1, v1 = g.T, d.T, m1.T, v1.T
            grads[n], deltas[n], new_m[n], new_v[n] = g[None], d[None], m1[None], v1[None]
        return tuple(deltas[n] for n in names)

    core = lax.axis_index("c").astype(jnp.int32).reshape(1)
    reducer = _Reducer(core, apply)
    grad_x, pending, small = _local_step(x, mem, loss_target, W, P, reducer)

    total = _small_allreduce(_pack([small[n] for n in SMALL] + [small["loss"]]))
    landed, (total,) = reducer.finish("in", pending["w_in"], (total,))
    apply(BIG[:1], landed)

    slot = [jnp.zeros((1, 1), F32)]
    shapes = [Wt[n].shape for n in SMALL] + [(1, 1)]
    d, m1, v1 = _adamw(_pack([Wt[n] for n in SMALL] + slot), total, _pack([Mo[n] for n in SMALL] + slot),
                       _pack([Vo[n] for n in SMALL] + slot), "adamw_small")
    g_s, d_s, m_s, v_s = _unpack(total, shapes), _unpack(d, shapes), _unpack(m1, shapes), _unpack(v1, shapes)
    for k, n in enumerate(SMALL):
        grads[n], deltas[n], new_m[n], new_v[n] = g_s[k], d_s[k], m_s[k], v_s[k]
    loss = g_s[-1][0, 0]

    return (loss, grad_x, *[grads[n] for n in WEIGHTS], *[deltas[n] for n in WEIGHTS],
            *[new_m[n] for n in WEIGHTS], *[new_v[n] for n in WEIGHTS])
```

```python
import functools

import jax
import jax.numpy as jnp
from jax import lax
from jax.experimental import pallas as pl
from jax.experimental.pallas import tpu as pltpu
from jax.experimental.pallas import tpu_sc as plsc

F32 = jnp.float32
BF16 = jnp.bfloat16
EPS = 1e-6
NEG = -1e30
HEAD = 64
A_W, B_W, M_W = 384, 384, 256
N_FOX_HEADS = 6
CHUNK = 128
IN_COLS = 2 * A_W + 3 * B_W + N_FOX_HEADS + M_W
P_MAIN = 2 * A_W + 3 * B_W + M_W
P_COLS = P_MAIN + 128
F_END = 2 * A_W + 3 * B_W + N_FOX_HEADS
LANES = 128
Q_BLK, K_BLK = 256, 128
ROW_SPLIT = 2
ADAM_LR, ADAM_B1, ADAM_B2, ADAM_EPS, ADAM_WD, ADAM_STEP = 0.001, 0.9, 0.999, 1e-08, 0.01, 10
VMEM_LIMIT = 56 * 1024 * 1024
MESH = pl.DeviceIdType.MESH
ANY = pl.BlockSpec(memory_space=pl.ANY)
BS = pl.BlockSpec


def _cp(sem=None):
    return pltpu.CompilerParams(dimension_semantics=sem, vmem_limit_bytes=VMEM_LIMIT)


def _iota(shape, dim):
    return lax.broadcasted_iota(jnp.int32, shape, dim)


def _dot(a, b):
    return jnp.dot(a.astype(BF16), b.astype(BF16), preferred_element_type=F32)


def _dot_nt(a, b):
    return lax.dot_general(a.astype(BF16), b.astype(BF16), (((1,), (1,)), ((), ())), preferred_element_type=F32)


def _dot_tn(a, b):
    return lax.dot_general(a.astype(BF16), b.astype(BF16), (((0,), (0,)), ((), ())), preferred_element_type=F32)


def _rms(x, g):
    return x * lax.rsqrt(jnp.mean(x * x, axis=-1, keepdims=True) + EPS) * g


def _gelu(x):
    return 0.5 * x * (1.0 + jnp.tanh(0.7978845608028654 * (x + 0.044715 * (x * x * x))))


def _sigmoid(x):
    return 1.0 / (1.0 + jnp.exp(-x))


def _silu_mul(g, u):
    return g * _sigmoid(g) * u


def _logsig(x):
    return jnp.minimum(x, 0.0) - jnp.log(1.0 + jnp.exp(-jnp.abs(x)))


def _colsum(x):
    return jnp.sum(x, axis=0, keepdims=True)


def _acc(ref, val, first):
    @pl.when(first)
    def _():
        ref[...] = val

    @pl.when(jnp.logical_not(first))
    def _():
        ref[...] += val


def _inproj_fwd(x2d, g_pre, w_in_p, tm):
    T, D = x2d.shape
    nchunk = P_COLS // 384
    ns, _, dsh = w_in_p.shape

    def body(x_ref, g_ref, w_ref, h_ref, proj_ref, fl_ref):
        h = _rms(x_ref[...], g_ref[...]).astype(BF16)
        h_ref[...] = h
        for n in range(nchunk):
            r = _dot_nt(h[:, 0:dsh], w_ref[0, n * 384:(n + 1) * 384, :])
            for s in range(1, ns):
                r = r + _dot_nt(h[:, s * dsh:(s + 1) * dsh], w_ref[s, n * 384:(n + 1) * 384, :])
            if n < nchunk - 1:
                proj_ref[:, n * 384:(n + 1) * 384] = r.astype(BF16)
            else:
                fl_ref[...] = r[:, :LANES]
                proj_ref[:, n * 384:n * 384 + M_W] = r[:, LANES:].astype(BF16)

    return pl.pallas_call(
        body, name="inproj_fwd", grid=(T // tm,),
        in_specs=[BS((tm, D), lambda i: (i, 0)), BS((1, D), lambda i: (0, 0)),
                  BS((ns, P_COLS, dsh), lambda i: (0, 0, 0))],
        out_specs=[BS((tm, D), lambda i: (i, 0)), BS((tm, P_MAIN), lambda i: (i, 0)), BS((tm, LANES), lambda i: (i, 0))],
        out_shape=[jax.ShapeDtypeStruct((T, D), BF16), jax.ShapeDtypeStruct((T, P_MAIN), BF16),
                   jax.ShapeDtypeStruct((T, LANES), F32)],
        compiler_params=_cp(("arbitrary",)),
    )(x2d, g_pre, w_in_p)


def _gate_fwd(flog3, bf_row):
    Bl, S, _ = flog3.shape
    nb = S // LANES

    def body(f_ref, b_ref, bq_ref, bk_ref, fr_ref):
        row = _iota((LANES, LANES), 0)
        lane = _iota((LANES, LANES), 1)
        one = jnp.ones((LANES, LANES), BF16)
        zero = jnp.zeros((LANES, LANES), BF16)

        def blk(j, carry):
            r0 = pl.multiple_of(j * LANES, LANES)
            fl = f_ref[0, pl.ds(r0, LANES), :] + b_ref[...]
            fr_ref[0, j] = fl.T[0:8, :]
            c = _logsig(fl)
            for k in (1, 2, 4, 8, 16, 32, 64):
                c = c + jnp.where(row >= k, pltpu.roll(c, k, 0), 0.0)
            c = c + carry
            for h in range(N_FOX_HEADS):
                col = jnp.sum(jnp.where(lane == h, c, 0.0), axis=1, keepdims=True)
                hi = col.astype(BF16)
                rest = col - hi.astype(F32)
                mid = rest.astype(BF16)
                lo = (rest - mid.astype(F32)).astype(BF16)
                base = _bias_lane(h)
                bq = jnp.where(lane == base, hi, jnp.where(lane == base + 1, mid, jnp.where(lane == base + 2, lo, zero)))
                bq = jnp.where((lane >= base + 3) & (lane < base + 6), one, bq)
                bk = jnp.where(lane == base + 3, -hi, jnp.where(lane == base + 4, -mid, jnp.where(lane == base + 5, -lo, zero)))
                bk = jnp.where((lane >= base) & (lane < base + 3), one, bk)
                bq_ref[0, h, pl.ds(r0, LANES), :] = bq
                bk_ref[0, h, pl.ds(r0, LANES), :] = bk
            return _colsum(jnp.where(row == LANES - 1, c, 0.0))

        lax.fori_loop(0, nb, blk, jnp.zeros((1, LANES), F32))

    slab = BS((1, N_FOX_HEADS, S, LANES), lambda b: (b, 0, 0, 0))
    return pl.pallas_call(
        body, name="gate_fwd", grid=(Bl,),
        in_specs=[BS((1, S, LANES), lambda b: (b, 0, 0)), BS((1, LANES), lambda b: (0, 0))],
        out_specs=[slab, slab, BS((1, nb, 8, LANES), lambda b: (b, 0, 0, 0))],
        out_shape=[jax.ShapeDtypeStruct((Bl, N_FOX_HEADS, S, LANES), BF16),
                   jax.ShapeDtypeStruct((Bl, N_FOX_HEADS, S, LANES), BF16),
                   jax.ShapeDtypeStruct((Bl, nb, 8, LANES), F32)],
        compiler_params=_cp(("arbitrary",)),
    )(flog3, bf_row)


def _bias_lane(h):
    return HEAD if h % 2 == 0 else 0


def _sgu_pre(zu, zv, g_sgu):
    return _gelu(zu), _rms(_gelu(zv), g_sgu)


def _sgu_fwd(proj, g_sgu, ws_tril, bs_full, tm):
    T = proj.shape[0]
    nch = tm // CHUNK

    def body(zu_ref, zv_ref, g_ref, ws_ref, b_ref, ya_ref):
        lane = _iota((CHUNK, LANES), 1)
        u, vn = _sgu_pre(zu_ref[...].astype(F32), zv_ref[...].astype(F32), g_ref[...])
        vn = vn.astype(BF16)
        for c in range(nch):
            rs = slice(c * CHUNK, (c + 1) * CHUNK)
            for j in range(3):
                cs = slice(j * LANES, (j + 1) * LANES)
                vp = vn[rs, cs]
                z = jnp.where(lane < HEAD, _dot(ws_ref[2 * j], vp), _dot(ws_ref[2 * j + 1], vp)) + b_ref[:, cs]
                ya_ref[rs, cs] = u[rs, cs] * z

    return pl.pallas_call(
        body, name="sgu_fwd", grid=(T // tm,),
        in_specs=[BS((tm, A_W), lambda i: (i, 0)), BS((tm, A_W), lambda i: (i, 1)), BS((1, A_W), lambda i: (0, 0)),
                  BS((6, CHUNK, CHUNK), lambda i: (0, 0, 0)), BS((CHUNK, A_W), lambda i: (0, 0))],
        out_specs=BS((tm, A_W), lambda i: (i, 0)),
        out_shape=jax.ShapeDtypeStruct((T, A_W), F32),
        compiler_params=_cp(("arbitrary",)),
    )(proj, proj, g_sgu, ws_tril, bs_full)


def _fox_fwd(proj, bq, bk, Bl, S):
    T = Bl * S
    nq = S // Q_BLK
    qc, kc, vc = 768 // LANES, 1152 // LANES, 1536 // LANES

    def body(q_ref, k_ref, v_ref, bq_ref, bk_ref, o_ref, lse_ref, ka_ref, va_ref):
        lane_s = _iota((S, LANES), 1)
        lane = _iota((Q_BLK, LANES), 1)
        tri = _iota((Q_BLK, Q_BLK), 1) <= _iota((Q_BLK, Q_BLK), 0)
        k = k_ref[...]
        v = v_ref[...]
        for hh in range(2):
            data = (lane_s < HEAD) if hh == 0 else (lane_s >= HEAD)
            ka_ref[hh] = jnp.where(data, k, bk_ref[0, hh])
            va_ref[hh] = jnp.where(lane_s == _bias_lane(hh), jnp.ones_like(v), v)
        for i in range(nq):
            r0 = i * Q_BLK
            q = q_ref[r0:r0 + Q_BLK, :]
            o_out = jnp.zeros((Q_BLK, LANES), F32)
            lse_out = jnp.zeros((Q_BLK, LANES), F32)
            for hh in range(2):
                hmask = (lane < HEAD) if hh == 0 else (lane >= HEAD)
                qa = jnp.where(hmask, q * 0.125, bq_ref[0, hh, r0:r0 + Q_BLK, :])
                sd = jnp.where(tri, _dot_nt(qa, ka_ref[hh, r0:r0 + Q_BLK, :]), NEG)
                m = jnp.max(sd, axis=1, keepdims=True)
                if i:
                    sf = _dot_nt(qa, ka_ref[hh, 0:r0, :])
                    m = jnp.maximum(m, jnp.max(sf, axis=1, keepdims=True))
                acc = _dot(jnp.exp(sd - m), va_ref[hh, r0:r0 + Q_BLK, :])
                if i:
                    acc = acc + _dot(jnp.exp(sf - m), va_ref[hh, 0:r0, :])
                l = jnp.sum(jnp.where(lane == _bias_lane(hh), acc, 0.0), axis=1, keepdims=True)
                o_out = jnp.where(hmask, acc / l, o_out)
                lse_out = jnp.where(hmask, m + jnp.log(l), lse_out)
            o_ref[r0:r0 + Q_BLK, :] = o_out
            lse_ref[0, r0:r0 + Q_BLK, :] = lse_out

    seq = lambda c0: BS((S, LANES), lambda b, p: (b, c0 + p))
    pair = BS((1, 2, S, LANES), lambda b, p: (b, p, 0, 0))
    return pl.pallas_call(
        body, name="fox_fwd", grid=(Bl, 3),
        in_specs=[seq(qc), seq(kc), seq(vc), pair, pair],
        out_specs=[seq(0), BS((1, S, LANES), lambda b, p: (p, b, 0))],
        out_shape=[jax.ShapeDtypeStruct((T, B_W), F32), jax.ShapeDtypeStruct((3, T, LANES), F32)],
        scratch_shapes=[pltpu.VMEM((2, S, LANES), BF16), pltpu.VMEM((2, S, LANES), BF16)],
        compiler_params=_cp(("arbitrary", "arbitrary")),
    )(proj, proj, proj, bq, bk)


def _memkv_fwd(mem, g_mem, w_kv):
    Bl, Mt, D = mem.shape

    def body(m_ref, g_ref, w_ref, mn_ref, kv_ref):
        mn = _rms(m_ref[0], g_ref[...]).astype(BF16)
        mn_ref[0] = mn
        kv_ref[0] = jnp.dot(mn, w_ref[...], preferred_element_type=F32).astype(BF16)

    return pl.pallas_call(
        body, name="memkv_fwd", grid=(Bl,),
        in_specs=[BS((1, Mt, D), lambda b: (b, 0, 0)), BS((1, D), lambda b: (0, 0)), BS((D, 2 * M_W), lambda b: (0, 0))],
        out_specs=[BS((1, Mt, D), lambda b: (b, 0, 0)), BS((1, Mt, 2 * M_W), lambda b: (b, 0, 0))],
        out_shape=[jax.ShapeDtypeStruct((Bl, Mt, D), BF16), jax.ShapeDtypeStruct((Bl, Mt, 2 * M_W), BF16)],
        compiler_params=_cp(("arbitrary",)),
    )(mem, g_mem, w_kv)


def _memattn_fwd(proj, kv, Bl, S, tq):
    T = Bl * S
    nq = S // tq
    Mt = kv.shape[1]
    qc = 1920 // LANES

    def body(q_ref, km_ref, vm_ref, o_ref):
        lane = _iota((tq, LANES), 1)
        q = q_ref[...]
        out = jnp.zeros((tq, LANES), F32)
        for hh in range(2):
            hmask = (lane < HEAD) if hh == 0 else (lane >= HEAD)
            qs = jnp.where(hmask, q, jnp.zeros_like(q)) * 0.125
            s = _dot_nt(qs, km_ref[0])
            pe = jnp.exp(s - jnp.max(s, axis=1, keepdims=True))
            pn = pe / jnp.sum(pe, axis=1, keepdims=True)
            out = jnp.where(hmask, _dot(pn, vm_ref[0]), out)
        o_ref[...] = out

    return pl.pallas_call(
        body, name="memattn_fwd", grid=(Bl, 2, nq),
        in_specs=[BS((tq, LANES), lambda b, p, i: (b * nq + i, qc + p)),
                  BS((1, Mt, LANES), lambda b, p, i: (b, 0, p)),
                  BS((1, Mt, LANES), lambda b, p, i: (b, 0, 2 + p))],
        out_specs=BS((tq, LANES), lambda b, p, i: (b * nq + i, p)),
        out_shape=jax.ShapeDtypeStruct((T, M_W), F32),
        compiler_params=_cp(("arbitrary", "arbitrary", "arbitrary")),
    )(proj, kv, kv)


def _mix_norms(ya, yb, ym, ga, gb, gm):
    return _rms(ya, ga), _rms(yb, gb), _rms(ym, gm)


def _outproj_fwd(ya, yb, ym, x2d, ga, gb, gm, g_post, g_pre2, w_out, tm):
    T, D = x2d.shape

    def body(ya_ref, yb_ref, ym_ref, x_ref, ga_ref, gb_ref, gm_ref, gp_ref, g2_ref, w_ref,
             y_ref, o_ref, x1_ref, h2_ref):
        na, nb_, nm = _mix_norms(ya_ref[...], yb_ref[...], ym_ref[...], ga_ref[...], gb_ref[...], gm_ref[...])
        y_ref[:, 0:A_W] = na.astype(BF16)
        y_ref[:, A_W:A_W + B_W] = nb_.astype(BF16)
        y_ref[:, A_W + B_W:] = nm.astype(BF16)
        o = jnp.dot(y_ref[...], w_ref[...], preferred_element_type=F32)
        o_ref[...] = o
        x1 = x_ref[...] + _rms(o, gp_ref[...])
        x1_ref[...] = x1
        h2_ref[...] = _rms(x1, g2_ref[...]).astype(BF16)

    row = lambda w: BS((tm, w), lambda i: (i, 0))
    vec = lambda w: BS((1, w), lambda i: (0, 0))
    return pl.pallas_call(
        body, name="outproj_fwd", grid=(T // tm,),
        in_specs=[row(A_W), row(B_W), row(M_W), row(D), vec(A_W), vec(B_W), vec(M_W), vec(D), vec(D),
                  BS((A_W + B_W + M_W, D), lambda i: (0, 0))],
        out_specs=[row(A_W + B_W + M_W), row(D), row(D), row(D)],
        out_shape=[jax.ShapeDtypeStruct((T, A_W + B_W + M_W), BF16), jax.ShapeDtypeStruct((T, D), F32),
                   jax.ShapeDtypeStruct((T, D), F32), jax.ShapeDtypeStruct((T, D), BF16)],
        compiler_params=_cp(("arbitrary",)),
    )(ya, yb, ym, x2d, ga, gb, gm, g_post, g_pre2, w_out)


def _ffn_fwd(h2, x1, target, wg, wu, wd, g_post, tm):
    T, D = x1.shape
    ns, F, _ = wg.shape

    def body(h_ref, x1_ref, t_ref, wg_ref, wu_ref, wd_ref, gp_ref,
             gs_ref, us_ref, dff_ref, dx2_ref, dgp_ref, loss_ref, acc_ref):
        i = pl.program_id(0)
        j = pl.program_id(1)
        h = h_ref[...]
        g = _dot_nt(h, wg_ref[0])
        u = _dot_nt(h, wu_ref[0])
        gs_ref[0] = g.astype(BF16)
        us_ref[0] = u.astype(BF16)
        part = _dot(_silu_mul(g, u), wd_ref[0])
        _acc(acc_ref, part, j == 0)

        @pl.when(j == ns - 1)
        def _():
            normed, vjp = jax.vjp(_rms, acc_ref[...], gp_ref[...])
            diff = x1_ref[...] + normed - t_ref[...]
            dx2 = diff * (1.0 / D)
            dff, dgp = vjp(dx2)
            dx2_ref[...] = dx2
            dff_ref[...] = dff.astype(BF16)
            lpart = jnp.sum(_colsum(diff * diff), axis=1, keepdims=True) * (0.5 / D)
            _acc(dgp_ref, dgp, i == 0)
            _acc(loss_ref, jnp.broadcast_to(lpart, (1, LANES)), i == 0)

    row = lambda w: BS((tm, w), lambda i, j: (i, 0))
    return pl.pallas_call(
        body, name="ffn_fwd", grid=(T // tm, ns),
        in_specs=[row(D), row(D), row(D), BS((1, F, D), lambda i, j: (j, 0, 0)), BS((1, F, D), lambda i, j: (j, 0, 0)),
                  BS((1, F, D), lambda i, j: (j, 0, 0)), BS((1, D), lambda i, j: (0, 0))],
        out_specs=[BS((1, tm, F), lambda i, j: (j, i, 0)), BS((1, tm, F), lambda i, j: (j, i, 0)), row(D), row(D),
                   BS((1, D), lambda i, j: (0, 0)), BS((1, LANES), lambda i, j: (0, 0))],
        out_shape=[jax.ShapeDtypeStruct((ns, T, F), BF16), jax.ShapeDtypeStruct((ns, T, F), BF16),
                   jax.ShapeDtypeStruct((T, D), BF16), jax.ShapeDtypeStruct((T, D), F32),
                   jax.ShapeDtypeStruct((1, D), F32), jax.ShapeDtypeStruct((1, LANES), F32)],
        scratch_shapes=[pltpu.VMEM((tm, D), F32)],
        compiler_params=_cp(("arbitrary", "arbitrary")),
    )(h2, x1, target, wg, wu, wd, g_post)


def _ffn_bwd(dff, h2, gs, us, wg, wu, wd, tm):
    T, D = h2.shape
    ns, F, _ = wg.shape

    def body(dff_ref, h_ref, gs_ref, us_ref, wg_ref, wu_ref, wd_ref, dh_ref, dwg_ref, dwu_ref, dwd_ref):
        first = pl.program_id(1) == 0
        dff = dff_ref[...]
        h = h_ref[...]
        parts = []
        for r in range(ROW_SPLIT):
            rows = slice(r * (tm // ROW_SPLIT), (r + 1) * (tm // ROW_SPLIT))
            dact = _dot_nt(dff[rows], wd_ref[0])
            g = gs_ref[0, rows, :].astype(F32)
            u = us_ref[0, rows, :].astype(F32)
            sig = _sigmoid(g)
            gsig = g * sig
            dg = (dact * u * (sig + gsig * (1.0 - sig))).astype(BF16)
            du = (dact * gsig).astype(BF16)
            dh_ref[0, rows, :] = (_dot(dg, wg_ref[0]) + _dot(du, wu_ref[0])).astype(BF16)
            parts.append(((gsig * u).astype(BF16), dg, du))
        a, dg, du = [jnp.concatenate(p, axis=0) for p in zip(*parts)]
        _acc(dwd_ref, _dot_tn(a, dff)[None], first)
        _acc(dwg_ref, _dot_tn(dg, h)[None], first)
        _acc(dwu_ref, _dot_tn(du, h)[None], first)

    row = BS((tm, D), lambda j, i: (i, 0))
    sh = BS((1, tm, F), lambda j, i: (j, i, 0))
    wsh = BS((1, F, D), lambda j, i: (j, 0, 0))
    return pl.pallas_call(
        body, name="ffn_bwd", grid=(ns, T // tm),
        in_specs=[row, row, sh, sh, wsh, wsh, wsh],
        out_specs=[BS((1, tm, D), lambda j, i: (j, i, 0)), wsh, wsh, wsh],
        out_shape=[jax.ShapeDtypeStruct((ns, T, D), BF16)] + [jax.ShapeDtypeStruct((ns, F, D), F32)] * 3,
        compiler_params=_cp(("arbitrary", "arbitrary")),
    )(dff, h2, gs, us, wg, wu, wd)


def _mm_tn(a, b, name, tk):
    T, M = a.shape
    N = b.shape[1]
    tk = min(tk, T)

    def body(a_ref, b_ref, o_ref):
        _acc(o_ref, _dot_tn(a_ref[...], b_ref[...]), pl.program_id(0) == 0)

    return pl.pallas_call(
        body, name=name, grid=(T // tk,),
        in_specs=[BS((tk, M), lambda t: (t, 0)), BS((tk, N), lambda t: (t, 0))],
        out_specs=BS((M, N), lambda t: (0, 0)),
        out_shape=jax.ShapeDtypeStruct((M, N), F32),
        compiler_params=_cp(("arbitrary",)),
    )(a, b)


def _dw_in(dproj, h, ns, tk):
    T, M = dproj.shape
    D = h.shape[1]
    dsh = D // ns
    tk = min(tk, T)

    def body(a_ref, b_ref, o_ref, acc_ref):
        t = pl.program_id(0)
        _acc(acc_ref, _dot_tn(b_ref[...], a_ref[...]), t == 0)

        @pl.when(t == pl.num_programs(0) - 1)
        def _():
            for s in range(ns):
                o_ref[s] = acc_ref[s * dsh:(s + 1) * dsh, :].T

    return pl.pallas_call(
        body, name="dw_in", grid=(T // tk,),
        in_specs=[BS((tk, M), lambda t: (t, 0)), BS((tk, D), lambda t: (t, 0))],
        out_specs=BS((ns, M, dsh), lambda t: (0, 0, 0)),
        out_shape=jax.ShapeDtypeStruct((ns, M, dsh), F32),
        scratch_shapes=[pltpu.VMEM((D, M), F32)],
        compiler_params=_cp(("arbitrary",)),
    )(dproj, h)


def _outproj_bwd(dh2, x1, dx2, o, ya, yb, ym, ga, gb, gm, g_post, g_pre2, w_out, tm):
    T, D = x1.shape
    ns = dh2.shape[0]

    def body(dh_ref, x1_ref, dx2_ref, o_ref, ya_ref, yb_ref, ym_ref, ga_ref, gb_ref, gm_ref, gp_ref, g2_ref, w_ref,
             dx1_ref, do_ref, dya_ref, dyb_ref, dym_ref, dga_ref, dgb_ref, dgm_ref, dgp_ref, dg2_ref):
        first = pl.program_id(0) == 0
        dh = dh_ref[0].astype(F32)
        for j in range(1, ns):
            dh = dh + dh_ref[j].astype(F32)
        _, vjp0 = jax.vjp(_rms, x1_ref[...], g2_ref[...])
        dxa, dg2 = vjp0(dh)
        dx1 = dx2_ref[...] + dxa
        dx1_ref[...] = dx1
        _acc(dg2_ref, dg2, first)
        _, vjp = jax.vjp(_rms, o_ref[...], gp_ref[...])
        do, dgp = vjp(dx1)
        do = do.astype(BF16)
        do_ref[...] = do
        dy = _dot_nt(do, w_ref[...])
        _, vjp2 = jax.vjp(_mix_norms, ya_ref[...], yb_ref[...], ym_ref[...], ga_ref[...], gb_ref[...], gm_ref[...])
        dya, dyb, dym, dga, dgb, dgm = vjp2((dy[:, 0:A_W], dy[:, A_W:A_W + B_W], dy[:, A_W + B_W:]))
        dya_ref[...] = dya
        dyb_ref[...] = dyb
        dym_ref[...] = dym
        _acc(dga_ref, dga, first)
        _acc(dgb_ref, dgb, first)
        _acc(dgm_ref, dgm, first)
        _acc(dgp_ref, dgp, first)

    row = lambda w: BS((tm, w), lambda i: (i, 0))
    vec = lambda w: BS((1, w), lambda i: (0, 0))
    sds = jax.ShapeDtypeStruct
    return pl.pallas_call(
        body, name="outproj_bwd", grid=(T // tm,),
        in_specs=[BS((ns, tm, D), lambda i: (0, i, 0)), row(D), row(D), row(D), row(A_W), row(B_W), row(M_W),
                  vec(A_W), vec(B_W), vec(M_W), vec(D), vec(D), BS((A_W + B_W + M_W, D), lambda i: (0, 0))],
        out_specs=[row(D), row(D), row(A_W), row(B_W), row(M_W), vec(A_W), vec(B_W), vec(M_W), vec(D), vec(D)],
        out_shape=[sds((T, D), F32), sds((T, D), BF16), sds((T, A_W), F32), sds((T, B_W), F32), sds((T, M_W), F32),
                   sds((1, A_W), F32), sds((1, B_W), F32), sds((1, M_W), F32), sds((1, D), F32), sds((1, D), F32)],
        compiler_params=_cp(("arbitrary",)),
    )(dh2, x1, dx2, o, ya, yb, ym, ga, gb, gm, g_post, g_pre2, w_out)


def _sgu_bwd(proj, dya, g_sgu, ws_tril, bs_full, tm):
    T = proj.shape[0]
    nch = tm // CHUNK

    def body(zu_ref, zv_ref, dy_ref, g_ref, ws_ref, b_ref, dzu_ref, dzv_ref, dws_ref, dbs_ref, dg_ref,
             du_ref, dvn_ref, dbf_ref):
        step = pl.program_id(0)
        first = step == 0
        lane = _iota((CHUNK, LANES), 1)
        tril = _iota((CHUNK, CHUNK), 0) >= _iota((CHUNK, CHUNK), 1)
        (u, vn), vjp = jax.vjp(_sgu_pre, zu_ref[...].astype(F32), zv_ref[...].astype(F32), g_ref[...])
        vnb = vn.astype(BF16)
        dy = dy_ref[...]

        @pl.when(first)
        def _():
            dws_ref[...] = jnp.zeros_like(dws_ref)
            dbf_ref[...] = jnp.zeros_like(dbf_ref)

        for c in range(nch):
            rs = slice(c * CHUNK, (c + 1) * CHUNK)
            for j in range(3):
                cs = slice(j * LANES, (j + 1) * LANES)
                vp = vnb[rs, cs]
                z = jnp.where(lane < HEAD, _dot(ws_ref[2 * j], vp), _dot(ws_ref[2 * j + 1], vp)) + b_ref[:, cs]
                du_ref[rs, cs] = dy[rs, cs] * z
                dz = dy[rs, cs] * u[rs, cs]
                dbf_ref[:, cs] += dz
                dzb = dz.astype(BF16)
                dz0 = jnp.where(lane < HEAD, dzb, jnp.zeros_like(dzb))
                dz1 = jnp.where(lane >= HEAD, dzb, jnp.zeros_like(dzb))
                dvn_ref[rs, cs] = jnp.where(lane < HEAD, _dot_tn(ws_ref[2 * j], dzb), _dot_tn(ws_ref[2 * j + 1], dzb))
                dws_ref[2 * j] += jnp.where(tril, _dot_nt(dz0, vp), 0.0)
                dws_ref[2 * j + 1] += jnp.where(tril, _dot_nt(dz1, vp), 0.0)
        dzu, dzv, dg = vjp((du_ref[...], dvn_ref[...]))
        dzu_ref[...] = dzu.astype(BF16)
        dzv_ref[...] = dzv.astype(BF16)
        _acc(dg_ref, dg, first)

        @pl.when(step == pl.num_programs(0) - 1)
        def _():
            out = jnp.zeros((CHUNK, LANES), F32)
            for j in range(3):
                slab = dbf_ref[:, j * LANES:(j + 1) * LANES]
                lo = jnp.sum(jnp.where(lane < HEAD, slab, 0.0), axis=1, keepdims=True)
                hi = jnp.sum(jnp.where(lane >= HEAD, slab, 0.0), axis=1, keepdims=True)
                out = out + jnp.where(lane == 2 * j, lo, 0.0) + jnp.where(lane == 2 * j + 1, hi, 0.0)
            dbs_ref[...] = out

    return pl.pallas_call(
        body, name="sgu_bwd", grid=(T // tm,),
        in_specs=[BS((tm, A_W), lambda i: (i, 0)), BS((tm, A_W), lambda i: (i, 1)), BS((tm, A_W), lambda i: (i, 0)),
                  BS((1, A_W), lambda i: (0, 0)), BS((6, CHUNK, CHUNK), lambda i: (0, 0, 0)),
                  BS((CHUNK, A_W), lambda i: (0, 0))],
        out_specs=[BS((tm, A_W), lambda i: (i, 0)), BS((tm, A_W), lambda i: (i, 0)),
                   BS((6, CHUNK, CHUNK), lambda i: (0, 0, 0)), BS((CHUNK, LANES), lambda i: (0, 0)),
                   BS((1, A_W), lambda i: (0, 0))],
        out_shape=[jax.ShapeDtypeStruct((T, A_W), BF16), jax.ShapeDtypeStruct((T, A_W), BF16),
                   jax.ShapeDtypeStruct((6, CHUNK, CHUNK), F32), jax.ShapeDtypeStruct((CHUNK, LANES), F32),
                   jax.ShapeDtypeStruct((1, A_W), F32)],
        scratch_shapes=[pltpu.VMEM((tm, A_W), F32), pltpu.VMEM((tm, A_W), F32), pltpu.VMEM((CHUNK, A_W), F32)],
        compiler_params=_cp(("arbitrary",)),
    )(proj, proj, dya, g_sgu, ws_tril, bs_full)


def _memattn_bwd(proj, kv, dym, Bl, S, tq):
    T = Bl * S
    nq = S // tq
    Mt = kv.shape[1]
    qc = 1920 // LANES

    def body(q_ref, km_ref, vm_ref, do_ref, dq_ref, dkm_ref, dvm_ref):
        first = pl.program_id(2) == 0
        lane = _iota((tq, LANES), 1)
        q = q_ref[...]
        do = do_ref[...]
        dq_out = jnp.zeros((tq, LANES), F32)
        dkm = jnp.zeros((Mt, LANES), F32)
        dvm = jnp.zeros((Mt, LANES), F32)
        for hh in range(2):
            hmask = (lane < HEAD) if hh == 0 else (lane >= HEAD)
            qs = jnp.where(hmask, q, jnp.zeros_like(q)) * 0.125
            dom = jnp.where(hmask, do, 0.0).astype(BF16)
            s = _dot_nt(qs, km_ref[0])
            pe = jnp.exp(s - jnp.max(s, axis=1, keepdims=True))
            pn = pe / jnp.sum(pe, axis=1, keepdims=True)
            dp = _dot_nt(dom, vm_ref[0])
            ds = (pn * (dp - jnp.sum(pn * dp, axis=1, keepdims=True))).astype(BF16)
            dq_out = jnp.where(hmask, _dot(ds, km_ref[0]) * 0.125, dq_out)
            dkm = dkm + _dot_tn(ds, qs)
            dvm = dvm + _dot_tn(pn, dom)
        dq_ref[...] = dq_out.astype(BF16)
        _acc(dkm_ref, dkm[None], first)
        _acc(dvm_ref, dvm[None], first)

    return pl.pallas_call(
        body, name="memattn_bwd", grid=(Bl, 2, nq),
        in_specs=[BS((tq, LANES), lambda b, p, i: (b * nq + i, qc + p)),
                  BS((1, Mt, LANES), lambda b, p, i: (b, 0, p)),
                  BS((1, Mt, LANES), lambda b, p, i: (b, 0, 2 + p)),
                  BS((tq, LANES), lambda b, p, i: (b * nq + i, p))],
        out_specs=[BS((tq, LANES), lambda b, p, i: (b * nq + i, p)),
                   BS((1, Mt, LANES), lambda b, p, i: (b, 0, p)),
                   BS((1, Mt, LANES), lambda b, p, i: (b, 0, p))],
        out_shape=[jax.ShapeDtypeStruct((T, M_W), BF16), jax.ShapeDtypeStruct((Bl, Mt, M_W), F32),
                   jax.ShapeDtypeStruct((Bl, Mt, M_W), F32)],
        compiler_params=_cp(("arbitrary", "arbitrary", "arbitrary")),
    )(proj, kv, kv, dym)


def _memkv_bwd(dkm, dvm, memn, mem, g_mem, w_kv):
    Bl, Mt, D = mem.shape

    def body(dk_ref, dv_ref, mn_ref, m_ref, g_ref, w_ref, dw_ref, dg_ref):
        first = pl.program_id(0) == 0
        dk = dk_ref[0].astype(BF16)
        dv = dv_ref[0].astype(BF16)
        mn = mn_ref[0]
        dmn = _dot_nt(dk, w_ref[:, 0:M_W]) + _dot_nt(dv, w_ref[:, M_W:])
        _, vjp = jax.vjp(_rms, m_ref[0], g_ref[...])
        _, dg = vjp(dmn)
        _acc(dg_ref, dg, first)

        @pl.when(first)
        def _():
            dw_ref[...] = jnp.zeros_like(dw_ref)

        dw_ref[:, 0:M_W] += _dot_tn(mn, dk)
        dw_ref[:, M_W:] += _dot_tn(mn, dv)

    return pl.pallas_call(
        body, name="memkv_bwd", grid=(Bl,),
        in_specs=[BS((1, Mt, M_W), lambda b: (b, 0, 0)), BS((1, Mt, M_W), lambda b: (b, 0, 0)),
                  BS((1, Mt, D), lambda b: (b, 0, 0)), BS((1, Mt, D), lambda b: (b, 0, 0)),
                  BS((1, D), lambda b: (0, 0)), BS((D, 2 * M_W), lambda b: (0, 0))],
        out_specs=[BS((D, 2 * M_W), lambda b: (0, 0)), BS((1, D), lambda b: (0, 0))],
        out_shape=[jax.ShapeDtypeStruct((D, 2 * M_W), F32), jax.ShapeDtypeStruct((1, D), F32)],
        compiler_params=_cp(("arbitrary",)),
    )(dkm, dvm, memn, mem, g_mem, w_kv)


def _fox_bwd(proj, dyb, lse, bq, bk, Bl, S):
    T = Bl * S
    nq = S // Q_BLK
    nb = S // LANES
    qc, kc, vc = 768 // LANES, 1152 // LANES, 1536 // LANES

    def body(q_ref, k_ref, v_ref, do_ref, lse_ref, bq_ref, bk_ref,
             dq_ref, dk_ref, dv_ref, dcr_ref, ka_ref, dka_ref, dva_ref):
        p = pl.program_id(1)
        lane_s = _iota((S, LANES), 1)
        lane = _iota((Q_BLK, LANES), 1)
        sub = _iota((8, LANES), 0)
        tri = _iota((Q_BLK, Q_BLK), 1) <= _iota((Q_BLK, Q_BLK), 0)
        k = k_ref[...]
        for hh in range(2):
            data = (lane_s < HEAD) if hh == 0 else (lane_s >= HEAD)
            ka_ref[hh] = jnp.where(data, k, bk_ref[0, hh])
        dka_ref[...] = jnp.zeros_like(dka_ref)
        dva_ref[...] = jnp.zeros_like(dva_ref)

        @pl.when(p == 0)
        def _():
            dcr_ref[...] = jnp.zeros_like(dcr_ref)

        def add_colsums(ds, first_blk, h):
            cs = _colsum(ds)
            for jb in range(ds.shape[1] // LANES):
                dcr_ref[0, first_blk + jb] += jnp.where(sub == h, cs[:, jb * LANES:(jb + 1) * LANES], 0.0)

        for i in range(nq):
            r0 = i * Q_BLK
            r1 = r0 + Q_BLK
            q = q_ref[r0:r1, :]
            do = do_ref[r0:r1, :]
            lse_b = lse_ref[0, r0:r1, :]
            dq_out = jnp.zeros((Q_BLK, LANES), F32)
            for hh in range(2):
                hmask = (lane < HEAD) if hh == 0 else (lane >= HEAD)
                h = 2 * p + hh
                qs = jnp.where(hmask, q * 0.125, jnp.zeros_like(q))
                qa = jnp.where(hmask, q * 0.125, bq_ref[0, hh, r0:r1, :])
                dob = jnp.where(hmask, do, 0.0).astype(BF16)
                lse_h = jnp.sum(jnp.where(lane == hh * HEAD, lse_b, 0.0), axis=1, keepdims=True)
                pd = jnp.where(tri, jnp.exp(_dot_nt(qa, ka_ref[hh, r0:r1, :]) - lse_h), 0.0)
                dpd = _dot_nt(dob, v_ref[r0:r1, :])
                delta = jnp.sum(pd * dpd, axis=1, keepdims=True)
                psum = jnp.sum(pd, axis=1, keepdims=True)
                if i:
                    pf = jnp.exp(_dot_nt(qa, ka_ref[hh, 0:r0, :]) - lse_h)
                    dpf = _dot_nt(dob, v_ref[0:r0, :])
                    delta = delta + jnp.sum(pf * dpf, axis=1, keepdims=True)
                    psum = psum + jnp.sum(pf, axis=1, keepdims=True)
                delta = delta / psum
                dsd = pd * (dpd - delta)
                add_colsums(dsd, r0 // LANES, h)
                dsd = dsd.astype(BF16)
                dq_h = _dot(dsd, k_ref[r0:r1, :])
                dka_ref[r0:r1, :] += _dot_tn(dsd, qs)
                dva_ref[r0:r1, :] += _dot_tn(pd, dob)
                if i:
                    dsf = pf * (dpf - delta)
                    add_colsums(dsf, 0, h)
                    dsf = dsf.astype(BF16)
                    dq_h = dq_h + _dot(dsf, k_ref[0:r0, :])
                    dka_ref[0:r0, :] += _dot_tn(dsf, qs)
                    dva_ref[0:r0, :] += _dot_tn(pf, dob)
                dq_out = jnp.where(hmask, dq_h * 0.125, dq_out)
            dq_ref[r0:r1, :] = dq_out.astype(BF16)
        dk_ref[...] = dka_ref[...].astype(BF16)
        dv_ref[...] = dva_ref[...].astype(BF16)

    seq = lambda c0: BS((S, LANES), lambda b, p: (b, c0 + p))
    pair = BS((1, 2, S, LANES), lambda b, p: (b, p, 0, 0))
    rowblk = BS((1, nb, 8, LANES), lambda b, p: (b, 0, 0, 0))
    return pl.pallas_call(
        body, name="fox_bwd", grid=(Bl, 3),
        in_specs=[seq(qc), seq(kc), seq(vc), seq(0), BS((1, S, LANES), lambda b, p: (p, b, 0)), pair, pair],
        out_specs=[seq(0), seq(0), seq(0), rowblk],
        out_shape=[jax.ShapeDtypeStruct((T, B_W), BF16)] * 3 + [jax.ShapeDtypeStruct((Bl, nb, 8, LANES), F32)],
        scratch_shapes=[pltpu.VMEM((2, S, LANES), BF16), pltpu.VMEM((S, LANES), F32), pltpu.VMEM((S, LANES), F32)],
        compiler_params=_cp(("arbitrary", "arbitrary")),
    )(proj, proj, proj, dyb, lse, bq, bk)


def _gate_bwd(dc_row, fl_row):
    Bl, nb, _, _ = dc_row.shape

    def body(dc_ref, fl_ref, o_ref):
        lane = _iota((8, LANES), 1)

        def blk(jj, carry):
            j = nb - 1 - jj
            r = -dc_ref[0, j]
            for k in (1, 2, 4, 8, 16, 32, 64):
                r = r + jnp.where(lane < LANES - k, pltpu.roll(r, LANES - k, 1), 0.0)
            r = r + carry
            dfl = r * _sigmoid(-fl_ref[0, j])
            o_ref[0, pl.ds(pl.multiple_of(j * LANES, LANES), LANES), :] = jnp.concatenate(
                [dfl, jnp.zeros((LANES - 8, LANES), F32)], axis=0).T
            return jnp.sum(jnp.where(lane == 0, r, 0.0), axis=1, keepdims=True)

        lax.fori_loop(0, nb, blk, jnp.zeros((8, 1), F32))

    rowblk = BS((1, nb, 8, LANES), lambda b: (b, 0, 0, 0))
    return pl.pallas_call(
        body, name="gate_bwd", grid=(Bl,),
        in_specs=[rowblk, rowblk],
        out_specs=BS((1, nb * LANES, LANES), lambda b: (b, 0, 0)),
        out_shape=jax.ShapeDtypeStruct((Bl, nb * LANES, LANES), F32),
        compiler_params=_cp(("arbitrary",)),
    )(dc_row, fl_row)


def _inproj_bwd(dzu, dzv, dq, dk, dv, dqm, dfl, x2d, dx1, g_pre, w_in_p, tm):
    T, D = x2d.shape
    ns, _, dsh = w_in_p.shape

    def body(dzu_ref, dzv_ref, dq_ref, dk_ref, dv_ref, dqm_ref, dfl_ref, x_ref, dx1_ref, g_ref, w_ref,
             dp_ref, gx_ref, dg_ref, dbf_ref):
        first = pl.program_id(0) == 0
        dfl = dfl_ref[...]
        dp_ref[:, 0:384] = dzu_ref[...]
        dp_ref[:, 384:768] = dzv_ref[...]
        dp_ref[:, 768:1152] = dq_ref[...]
        dp_ref[:, 1152:1536] = dk_ref[...]
        dp_ref[:, 1536:1920] = dv_ref[...]
        dp_ref[:, 1920:2048] = dfl.astype(BF16)
        dp_ref[:, 2048:2304] = dqm_ref[...]
        dh = jnp.concatenate([_dot(dp_ref[...], w_ref[s]) for s in range(ns)], axis=1)
        _, vjp = jax.vjp(_rms, x_ref[...], g_ref[...])
        dxa, dg = vjp(dh)
        gx_ref[...] = dx1_ref[...] + dxa
        _acc(dg_ref, dg, first)
        _acc(dbf_ref, _colsum(dfl), first)

    row = lambda w: BS((tm, w), lambda i: (i, 0))
    return pl.pallas_call(
        body, name="inproj_bwd", grid=(T // tm,),
        in_specs=[row(A_W), row(A_W), row(B_W), row(B_W), row(B_W), row(M_W), row(LANES), row(D), row(D),
                  BS((1, D), lambda i: (0, 0)), BS((ns, P_COLS, dsh), lambda i: (0, 0, 0))],
        out_specs=[row(P_COLS), row(D), BS((1, D), lambda i: (0, 0)), BS((1, LANES), lambda i: (0, 0))],
        out_shape=[jax.ShapeDtypeStruct((T, P_COLS), BF16), jax.ShapeDtypeStruct((T, D), F32),
                   jax.ShapeDtypeStruct((1, D), F32), jax.ShapeDtypeStruct((1, LANES), F32)],
        compiler_params=_cp(("arbitrary",)),
    )(dzu, dzv, dq, dk, dv, dqm, dfl, x2d, dx1, g_pre, w_in_p)


def _local_step(x, mem, target, W, P, reduce=None):
    Bl, S, D = x.shape
    T = Bl * S
    tm = min(512, T)
    x2d = x.reshape(T, D)
    t2d = target.reshape(T, D)
    vec = lambda a: a.reshape(1, -1)
    bf_row = jnp.pad(P["b_f"].reshape(1, -1), ((0, 0), (0, LANES - N_FOX_HEADS)))
    tril = jnp.tril(jnp.ones((CHUNK, CHUNK), bool))
    ws_tril = jnp.where(tril[None], P["w_s"][0], 0.0).astype(BF16)
    bs_full = jnp.repeat(P["b_s"][0].T, HEAD, axis=1)
    g_pre, g_sgu = vec(P["g_pre_mix"]), vec(P["g_sgu"])
    ga, gb, gm = vec(P["g_out_a"]), vec(P["g_out_b"]), vec(P["g_out_m"])
    g_mem, g_post, g_pre2, g_post2 = vec(P["g_mem"]), vec(P["g_post_mix"]), vec(P["g_pre_ffn"]), vec(P["g_post_ffn"])

    h, proj, flog = _inproj_fwd(x2d, g_pre, W["w_in"], tm)
    bq, bk, fl_row = _gate_fwd(flog.reshape(Bl, S, LANES), bf_row)
    ya = _sgu_fwd(proj, g_sgu, ws_tril, bs_full, tm)
    yb, lse = _fox_fwd(proj, bq, bk, Bl, S)
    memn, kv = _memkv_fwd(mem, g_mem, W["w_mem_kv"])
    ym = _memattn_fwd(proj, kv, Bl, S, min(512, S))
    y, o, x1, h2 = _outproj_fwd(ya, yb, ym, x2d, ga, gb, gm, g_post, g_pre2, W["w_out"], tm)
    gs, us, dff, dx2, dg_post2, loss = _ffn_fwd(h2, x1, t2d, W["w_gate"], W["w_up"], W["w_down"], g_post2, tm)

    dh2, d_w_gate, d_w_up, d_w_down = _ffn_bwd(dff, h2, gs, us, W["w_gate"], W["w_up"], W["w_down"], tm)
    ffn = [d_w_gate, d_w_up, d_w_down]
    if reduce is not None:
        pending, _ = reduce.begin("ffn", ffn)
    dx1, do, dya, dyb, dym, dga, dgb, dgm, dg_post, dg_pre2 = _outproj_bwd(
        dh2, x1, dx2, o, ya, yb, ym, ga, gb, gm, g_post, g_pre2, W["w_out"], tm)
    if reduce is not None:
        ffn, (do, dya, dyb, dym) = reduce.finish("ffn", pending, (do, dya, dyb, dym))
    d_w_out = _mm_tn(y, do, "dw_out", 1024)
    dzu, dzv, dws, dbs_cols, dg_sgu = _sgu_bwd(proj, dya, g_sgu, ws_tril, bs_full, tm)
    dqm, dkm, dvm = _memattn_bwd(proj, kv, dym, Bl, S, min(512, S))
    d_w_kv, dg_mem = _memkv_bwd(dkm, dvm, memn, mem, g_mem, W["w_mem_kv"])
    mid = [d_w_kv, d_w_out]
    dq, dk, dv, dc_row = _fox_bwd(proj, dyb, lse, bq, bk, Bl, S)
    if reduce is not None:
        done = reduce.apply(BIG[3:], ffn)
        pending, after = reduce.begin("mid", mid, (dc_row,) + done)
        dc_row = after[0]
    dfl = _gate_bwd(dc_row, fl_row).reshape(T, LANES)
    dproj, grad_x, dg_pre, dbf = _inproj_bwd(dzu, dzv, dq, dk, dv, dqm, dfl, x2d, dx1, g_pre, W["w_in"], tm)
    if reduce is not None:
        mid, (dproj,) = reduce.finish("mid", pending, (dproj,))
    d_w_in = _dw_in(dproj, h, W["w_in"].shape[0], 1024)
    if reduce is None:
        big = dict(zip(BIG, [d_w_in] + mid + ffn))
    else:
        done = reduce.apply(BIG[1:3], mid)
        big = {"w_in": reduce.begin("in", [d_w_in], done)[0]}
    small = {"g_pre_mix": dg_pre, "b_f": dbf[:, :N_FOX_HEADS], "g_sgu": dg_sgu, "w_s": dws, "b_s": dbs_cols[:, :N_FOX_HEADS].T,
             "g_out_a": dga, "g_out_b": dgb, "g_out_m": dgm, "g_mem": dg_mem, "g_post_mix": dg_post,
             "g_pre_ffn": dg_pre2, "g_post_ffn": dg_post2, "loss": loss[:, :1]}
    return grad_x.reshape(Bl, S, D), big, small


def _place():
    return lax.axis_index("x"), lax.axis_index("y"), lax.axis_index("c")


def _exchange_on_sequencer(srcs, own_full, name, collective_id):
    n = len(srcs)

    def body(*refs):
        src, dst = refs[:n], refs[n:2 * n]
        lsem, isend, irecv, dsend, drecv = refs[2 * n:]
        x, y, c = _place()
        oc = 1 - c
        s_me = 2 * x + y
        sib = (x, y, oc)
        chips = [(1 - x, y), (x, 1 - y), (1 - x, 1 - y)]
        barrier = pltpu.get_barrier_semaphore()
        for dev in [(cx, cy, c) for cx, cy in chips] + [sib]:
            pl.semaphore_signal(barrier, inc=1, device_id=dev, device_id_type=MESH)
        pl.semaphore_wait(barrier, 4)

        def remote(a, b, ssem, rsem, dev):
            return pltpu.make_async_remote_copy(src_ref=a, dst_ref=b, send_sem=ssem, recv_sem=rsem,
                                                device_id=dev, device_id_type=MESH)

        sends, local = [], []
        for w in range(n):
            for j, (cx, cy) in enumerate(chips):
                half = src[w].at[c] if own_full else src[w].at[2 * cx + cy]
                cp = remote(half, dst[w].at[s_me, c], isend.at[w, j], irecv.at[w, j], (cx, cy, c))
                cp.start()
                sends.append(cp)
            if own_full:
                cp = remote(src[w], dst[w].at[s_me], dsend.at[w, 3], drecv.at[w, 3], sib)
            else:
                cp = remote(src[w].at[s_me], dst[w].at[s_me, c], dsend.at[w, 3], drecv.at[w, 3], sib)
                loc = pltpu.make_async_copy(src[w].at[s_me], dst[w].at[s_me, c], lsem.at[w])
                loc.start()
                local.append(loc)
            cp.start()
            sends.append(cp)
        for w in range(n):
            for j, (cx, cy) in enumerate(chips):
                landed = dst[w].at[2 * cx + cy, c]
                remote(landed, landed, isend.at[w, j], irecv.at[w, j], (cx, cy, c)).wait_recv()
                cp = remote(landed, landed, dsend.at[w, j], drecv.at[w, j], sib)
                cp.start()
                sends.append(cp)
        for w in range(n):
            for j, (cx, cy) in enumerate(chips):
                landed = dst[w].at[2 * cx + cy, oc]
                remote(landed, landed, dsend.at[w, j], drecv.at[w, j], sib).wait_recv()
            landed = dst[w].at[s_me] if own_full else dst[w].at[s_me, oc]
            remote(landed, landed, dsend.at[w, 3], drecv.at[w, 3], sib).wait_recv()
        for cp in sends:
            cp.wait_send()
        for loc in local:
            loc.wait()

    return pl.kernel(
        body, out_type=[jax.ShapeDtypeStruct((4, 2) + s.shape[1:], s.dtype) for s in srcs],
        mesh=plsc.ScalarSubcoreMesh(axis_name="sequencer", num_cores=1), name=name,
        scratch_types=[pltpu.SemaphoreType.DMA((n,)), pltpu.SemaphoreType.DMA((n, 3)), pltpu.SemaphoreType.DMA((n, 3)),
                       pltpu.SemaphoreType.DMA((n, 4)), pltpu.SemaphoreType.DMA((n, 4))],
        compiler_params=pltpu.CompilerParams(collective_id=collective_id),
    )(*srcs)


def _sibling_swap(grads, name, collective_id):
    n = len(grads)

    def body(*refs):
        g, theirs = refs[:n], refs[n:2 * n]
        ssem, rsem = refs[2 * n:]
        x, y, c = _place()
        sib = (x, y, 1 - c)
        barrier = pltpu.get_barrier_semaphore()
        pl.semaphore_signal(barrier, inc=1, device_id=sib, device_id_type=MESH)
        pl.semaphore_wait(barrier, 1)
        cps = []
        for w in range(n):
            cp = pltpu.make_async_remote_copy(src_ref=g[w].at[:, 1 - c], dst_ref=theirs[w], send_sem=ssem.at[w],
                                              recv_sem=rsem.at[w], device_id=sib, device_id_type=MESH)
            cp.start()
            cps.append(cp)
        for cp in cps:
            cp.wait()

    return pl.kernel(
        body, out_type=[jax.ShapeDtypeStruct((4,) + g.shape[2:], g.dtype) for g in grads],
        mesh=plsc.ScalarSubcoreMesh(axis_name="sequencer", num_cores=1), name=name,
        scratch_types=[pltpu.SemaphoreType.DMA((n,)), pltpu.SemaphoreType.DMA((n,))],
        compiler_params=pltpu.CompilerParams(collective_id=collective_id),
    )(*grads)


def _add_pair(core, g, theirs, name):
    _, _, hr, C = g.shape

    def body(core_ref, g_ref, t_ref, o_ref):
        o_ref[0] = (g_ref[0, 0] + t_ref[0]).astype(BF16)

    blk = BS((1, hr, C), lambda s, core_ref: (s, 0, 0))
    return pl.pallas_call(
        body, name=name,
        grid_spec=pltpu.PrefetchScalarGridSpec(
            num_scalar_prefetch=1, grid=(4,),
            in_specs=[BS((1, 1, hr, C), lambda s, core_ref: (s, core_ref[0], 0, 0)), blk], out_specs=blk),
        out_shape=jax.ShapeDtypeStruct(theirs.shape, BF16), compiler_params=_cp(("arbitrary",)))(core, g, theirs)


def _sum_chips(r, name):
    _, _, hr, C = r.shape

    def body(r_ref, o_ref):
        o_ref[...] = ((r_ref[0, 0].astype(F32) + r_ref[1, 0].astype(F32)) + r_ref[2, 0].astype(F32)) + r_ref[3, 0].astype(F32)

    return pl.pallas_call(body, name=name, grid=(2,), in_specs=[BS((4, 1, hr, C), lambda h: (0, h, 0, 0))],
                          out_specs=BS((hr, C), lambda h: (h, 0)), out_shape=jax.ShapeDtypeStruct((2 * hr, C), F32),
                          compiler_params=_cp(("arbitrary",)))(r)


class _Reducer:
    IDS = {"ffn": (4, 5), "mid": (6, 7), "in": (8, 9)}

    def __init__(self, core, apply):
        self.core = core
        self.apply = apply

    def begin(self, tag, grads, after=()):
        grads, after = lax.optimization_barrier((list(grads), after))
        g4 = [g.reshape(4, 2, -1, g.shape[-1]) for g in grads]
        return (g4, _sibling_swap(g4, "swap_" + tag, self.IDS[tag][0])), after

    def finish(self, tag, pending, hold):
        g4, theirs = pending
        sums = [_add_pair(self.core, g, t, "chip_sum_%s_%d" % (tag, k)) for k, (g, t) in enumerate(zip(g4, theirs))]
        sums, hold = lax.optimization_barrier((sums, hold))
        return _exchange_on_sequencer(sums, False, "scatter_" + tag, self.IDS[tag][1]), hold


def _small_allreduce(part):
    R = part.shape[0]
    rs = R // 8
    masks = [(mx, my, mc) for mx in (0, 1) for my in (0, 1) for mc in (0, 1)][1:]

    def body(p_ref, o_ref, buf_ref, s1, r1, s2, r2):
        x, y, c = _place()
        d = 4 * x + 2 * y + c
        mine = pl.ds(pl.multiple_of(d * rs, 8), rs)
        peers = [((x + mx) % 2, (y + my) % 2, (c + mc) % 2) for mx, my, mc in masks]
        first, second = [], []
        for k, (px, py, pc) in enumerate(peers):
            theirs = pl.ds(pl.multiple_of((4 * px + 2 * py + pc) * rs, 8), rs)
            cp = pltpu.make_async_remote_copy(src_ref=p_ref.at[theirs, :], dst_ref=buf_ref.at[d], send_sem=s1.at[k],
                                              recv_sem=r1.at[k], device_id=(px, py, pc), device_id_type=MESH)
            cp.start()
            first.append(cp)
        buf_ref[d] = p_ref[mine, :]
        for k, (px, py, pc) in enumerate(peers):
            slot = buf_ref.at[4 * px + 2 * py + pc]
            pltpu.make_async_remote_copy(src_ref=slot, dst_ref=slot, send_sem=s1.at[k], recv_sem=r1.at[k],
                                         device_id=(px, py, pc), device_id_type=MESH).wait_recv()
        total = buf_ref[0]
        for k in range(1, 8):
            total = total + buf_ref[k]
        o_ref[mine, :] = total
        for k, (px, py, pc) in enumerate(peers):
            cp = pltpu.make_async_remote_copy(src_ref=o_ref.at[mine, :], dst_ref=o_ref.at[mine, :], send_sem=s2.at[k],
                                              recv_sem=r2.at[k], device_id=(px, py, pc), device_id_type=MESH)
            cp.start()
            second.append(cp)
        for k, (px, py, pc) in enumerate(peers):
            rows = o_ref.at[pl.ds(pl.multiple_of((4 * px + 2 * py + pc) * rs, 8), rs), :]
            pltpu.make_async_remote_copy(src_ref=rows, dst_ref=rows, send_sem=s2.at[k], recv_sem=r2.at[k],
                                         device_id=(px, py, pc), device_id_type=MESH).wait_recv()
        for cp in first + second:
            cp.wait_send()

    vm = pl.BlockSpec(memory_space=pltpu.VMEM)
    return pl.pallas_call(
        body, name="small_allreduce", in_specs=[vm], out_specs=vm, out_shape=jax.ShapeDtypeStruct(part.shape, F32),
        scratch_shapes=[pltpu.VMEM((8, rs, LANES), F32)] + [pltpu.SemaphoreType.DMA((7,))] * 4,
    )(part)


def _adamw(w, g, m, v, name):
    R, C = w.shape
    summed = g.ndim == 4
    if summed:
        tr = R // 2
    else:
        tr = R if R * C * 4 <= (1 << 21) else R // 2
        if tr % 8:
            tr = R
    c1 = 1.0 / (1.0 - ADAM_B1 ** ADAM_STEP)
    c2 = 1.0 / (1.0 - ADAM_B2 ** ADAM_STEP)

    def body(w_ref, g_ref, m_ref, v_ref, *outs):
        if summed:
            g_ = ((g_ref[0, 0].astype(F32) + g_ref[1, 0].astype(F32)) + g_ref[2, 0].astype(F32)) + g_ref[3, 0].astype(F32)
            outs[0][...] = g_
        else:
            g_ = g_ref[...]
        d_ref, mo_ref, vo_ref = outs[-3:]
        m_ = ADAM_B1 * m_ref[...] + (1.0 - ADAM_B1) * g_
        v_ = ADAM_B2 * v_ref[...] + (1.0 - ADAM_B2) * (g_ * g_)
        mo_ref[...] = m_
        vo_ref[...] = v_
        d_ref[...] = -ADAM_LR * ((m_ * c1) / (jnp.sqrt(v_ * c2) + ADAM_EPS) + ADAM_WD * w_ref[...])

    blk = BS((tr, C), lambda i: (i, 0))
    g_blk = BS((4, 1, tr, C), lambda i: (0, i, 0, 0)) if summed else blk
    nout = 4 if summed else 3
    return pl.pallas_call(body, name=name, grid=(R // tr,), in_specs=[blk, g_blk, blk, blk], out_specs=[blk] * nout,
                          out_shape=[jax.ShapeDtypeStruct((R, C), F32)] * nout,
                          compiler_params=_cp(("arbitrary",)))(w, g, m, v)


SMALL = ("g_pre_mix", "b_f", "g_sgu", "w_s", "b_s", "g_out_a", "g_out_b", "g_out_m", "g_mem", "g_post_mix",
         "g_pre_ffn", "g_post_ffn")
BIG = ("w_in", "w_mem_kv", "w_out", "w_gate", "w_up", "w_down")
TRANSPOSED = ("w_in", "w_gate", "w_up")
WEIGHTS = ("g_pre_mix", "w_in", "b_f", "g_sgu", "w_s", "b_s", "g_out_a", "g_out_b", "g_out_m", "g_mem", "w_mem_kv",
           "w_out", "g_post_mix", "g_pre_ffn", "w_gate", "w_up", "w_down", "g_post_ffn")


def _rows_of(n):
    return -(-n // (8 * LANES)) * 8


def _pack(parts):
    tiles = []
    for a in parts:
        flat = a.reshape(-1).astype(F32)
        rows = _rows_of(flat.shape[0])
        tiles.append(jnp.pad(flat, (0, rows * LANES - flat.shape[0])).reshape(rows, LANES))
    total = sum(t.shape[0] for t in tiles)
    pad = -total % 64
    if pad:
        tiles.append(jnp.zeros((pad, LANES), F32))
    return jnp.concatenate(tiles, axis=0)


def _unpack(packed, shapes):
    out, r = [], 0
    for shp in shapes:
        n = 1
        for s in shp:
            n *= s
        rows = _rows_of(n)
        out.append(packed[r:r + rows].reshape(-1)[:n].reshape(shp))
        r += rows
    return out


def kernel(x, mem, g_pre_mix, w_in, b_f, g_sgu, w_s, b_s, g_out_a, g_out_b, g_out_m, g_mem, w_mem_kv, w_out, g_post_mix, g_pre_ffn, w_gate, w_up, w_down, g_post_ffn, loss_target, m_g_pre_mix, m_w_in, m_b_f, m_g_sgu, m_w_s, m_b_s, m_g_out_a, m_g_out_b, m_g_out_m, m_g_mem, m_w_mem_kv, m_w_out, m_g_post_mix, m_g_pre_ffn, m_w_gate, m_w_up, m_w_down, m_g_post_ffn, v_g_pre_mix, v_w_in, v_b_f, v_g_sgu, v_w_s, v_b_s, v_g_out_a, v_g_out_b, v_g_out_m, v_g_mem, v_w_mem_kv, v_w_out, v_g_post_mix, v_g_pre_ffn, v_w_gate, v_w_up, v_w_down, v_g_post_ffn):
    Wt = dict(g_pre_mix=g_pre_mix, w_in=w_in, b_f=b_f, g_sgu=g_sgu, w_s=w_s, b_s=b_s, g_out_a=g_out_a, g_out_b=g_out_b,
              g_out_m=g_out_m, g_mem=g_mem, w_mem_kv=w_mem_kv, w_out=w_out, g_post_mix=g_post_mix, g_pre_ffn=g_pre_ffn,
              w_gate=w_gate, w_up=w_up, w_down=w_down, g_post_ffn=g_post_ffn)
    Mo = dict(g_pre_mix=m_g_pre_mix, w_in=m_w_in, b_f=m_b_f, g_sgu=m_g_sgu, w_s=m_w_s, b_s=m_b_s, g_out_a=m_g_out_a,
              g_out_b=m_g_out_b, g_out_m=m_g_out_m, g_mem=m_g_mem, w_mem_kv=m_w_mem_kv, w_out=m_w_out,
              g_post_mix=m_g_post_mix, g_pre_ffn=m_g_pre_ffn, w_gate=m_w_gate, w_up=m_w_up, w_down=m_w_down,
              g_post_ffn=m_g_post_ffn)
    Vo = dict(g_pre_mix=v_g_pre_mix, w_in=v_w_in, b_f=v_b_f, g_sgu=v_g_sgu, w_s=v_w_s, b_s=v_b_s, g_out_a=v_g_out_a,
              g_out_b=v_g_out_b, g_out_m=v_g_out_m, g_mem=v_g_mem, w_mem_kv=v_w_mem_kv, w_out=v_w_out,
              g_post_mix=v_g_post_mix, g_pre_ffn=v_g_pre_ffn, w_gate=v_w_gate, w_up=v_w_up, w_down=v_w_down,
              g_post_ffn=v_g_post_ffn)

    gap = P_COLS - IN_COLS

    def to_kernel(n, w):
        if n in TRANSPOSED:
            w = w.T
        if n == "w_in":
            w = jnp.pad(w[:F_END], ((0, P_COLS - F_END), (0, 0))) + jnp.pad(w[F_END:], ((F_END + gap, 0), (0, 0)))
        return w

    def ungroup(g):
        return jnp.pad(g[:F_END], ((0, IN_COLS - F_END), (0, 0))) + jnp.pad(g[F_END + gap:], ((F_END, 0), (0, 0)))

    shards = {n: to_kernel(n, Wt[n][0]) for n in BIG}
    srcs = [shards[n].astype(BF16).reshape(2, shards[n].shape[0] // 2, shards[n].shape[1]) for n in BIG]
    fulls = (_exchange_on_sequencer(srcs[:1], True, "gather_w_in", 1)
             + _exchange_on_sequencer(srcs[1:3], True, "gather_kv_out", 2)
             + _exchange_on_sequencer(srcs[3:], True, "gather_ffn", 3))
    W = {}
    for n, f in zip(BIG, fulls):
        _, _, hr, C = f.shape
        W[n] = f.reshape(8 * hr, C) if n in ("w_mem_kv", "w_out") else f.reshape(4, 2 * hr, C)

    P = {n: Wt[n] for n in SMALL}
    grads, deltas, new_m, new_v = {}, {}, {}, {}

    def apply(names, landed):
        for n, r in zip(names, landed):
            wmv = [a[n][0].T if n in TRANSPOSED else a[n][0] for a in (Wt, Mo, Vo)]
            if n == "w_in":
                g = ungroup(_sum_chips(r, "sum_chips_" + n))
                g, d, m1, v1 = (g,) + tuple(_adamw(wmv[0], g, wmv[1], wmv[2], "adamw_" + n))
            else:
                g, d, m1, v1 = _adamw(wmv[0], r, wmv[1], wmv[2], "adamw_" + n)
            if n in TRANSPOSED:
                g, d, m1, v1 = g.T, d.T, m1.T, v1.T
            grads[n], deltas[n], new_m[n], new_v[n] = g[None], d[None], m1[None], v1[None]
        return tuple(deltas[n] for n in names)

    core = lax.axis_index("c").astype(jnp.int32).reshape(1)
    reducer = _Reducer(core, apply)
    grad_x, pending, small = _local_step(x, mem, loss_target, W, P, reducer)

    total = _small_allreduce(_pack([small[n] for n in SMALL] + [small["loss"]]))
    landed, (total,) = reducer.finish("in", pending["w_in"], (total,))
    apply(BIG[:1], landed)

    slot = [jnp.zeros((1, 1), F32)]
    shapes = [Wt[n].shape for n in SMALL] + [(1, 1)]
    d, m1, v1 = _adamw(_pack([Wt[n] for n in SMALL] + slot), total, _pack([Mo[n] for n in SMALL] + slot),
                       _pack([Vo[n] for n in SMALL] + slot), "adamw_small")
    g_s, d_s, m_s, v_s = _unpack(total, shapes), _unpack(d, shapes), _unpack(m1, shapes), _unpack(v1, shapes)
    for k, n in enumerate(SMALL):
        grads[n], deltas[n], new_m[n], new_v[n] = g_s[k], d_s[k], m_s[k], v_s[k]
    loss = g_s[-1][0, 0]

    return (loss, grad_x, *[grads[n] for n in WEIGHTS], *[deltas[n] for n in WEIGHTS],
            *[new_m[n] for n in WEIGHTS], *[new_v[n] for n in WEIGHTS])
```

```python
import functools

import jax
import jax.numpy as jnp
from jax import lax
from jax.experimental import pallas as pl
from jax.experimental.pallas import tpu as pltpu
from jax.experimental.pallas import tpu_sc as plsc

F32 = jnp.float32
BF16 = jnp.bfloat16
EPS = 1e-6
NEG = -1e30
HEAD = 64
A_W, B_W, M_W = 384, 384, 256
N_FOX_HEADS = 6
CHUNK = 128
IN_COLS = 2 * A_W + 3 * B_W + N_FOX_HEADS + M_W
P_MAIN = 2 * A_W + 3 * B_W + M_W
P_COLS = P_MAIN + 128
F_END = 2 * A_W + 3 * B_W + N_FOX_HEADS
LANES = 128
Q_BLK, K_BLK = 256, 128
ROW_SPLIT = 4
ADAM_LR, ADAM_B1, ADAM_B2, ADAM_EPS, ADAM_WD, ADAM_STEP = 0.001, 0.9, 0.999, 1e-08, 0.01, 10
VMEM_LIMIT = 56 * 1024 * 1024
MESH = pl.DeviceIdType.MESH
ANY = pl.BlockSpec(memory_space=pl.ANY)
BS = pl.BlockSpec


def _cp(sem=None):
    return pltpu.CompilerParams(dimension_semantics=sem, vmem_limit_bytes=VMEM_LIMIT)


def _iota(shape, dim):
    return lax.broadcasted_iota(jnp.int32, shape, dim)


def _dot(a, b):
    return jnp.dot(a.astype(BF16), b.astype(BF16), preferred_element_type=F32)


def _dot_nt(a, b):
    return lax.dot_general(a.astype(BF16), b.astype(BF16), (((1,), (1,)), ((), ())), preferred_element_type=F32)


def _dot_tn(a, b):
    return lax.dot_general(a.astype(BF16), b.astype(BF16), (((0,), (0,)), ((), ())), preferred_element_type=F32)


def _rms(x, g):
    return x * lax.rsqrt(jnp.mean(x * x, axis=-1, keepdims=True) + EPS) * g


def _rms_bwd(x, g, dy):
    r = lax.rsqrt(jnp.mean(x * x, axis=-1, keepdims=True) + EPS)
    xr = x * r
    gd = dy * g
    m = jnp.mean(gd * xr, axis=-1, keepdims=True)
    return (gd - xr * m) * r, _colsum(dy * xr)


def _gelu(x):
    return 0.5 * x * (1.0 + jnp.tanh(0.7978845608028654 * (x + 0.044715 * (x * x * x))))


def _sigmoid(x):
    return 1.0 / (1.0 + jnp.exp(-x))


def _silu_mul(g, u):
    return g * _sigmoid(g) * u


def _logsig(x):
    return jnp.minimum(x, 0.0) - jnp.log(1.0 + jnp.exp(-jnp.abs(x)))


def _colsum(x):
    return jnp.sum(x, axis=0, keepdims=True)


def _acc(ref, val, first):
    @pl.when(first)
    def _():
        ref[...] = val

    @pl.when(jnp.logical_not(first))
    def _():
        ref[...] += val


def _inproj_fwd(x2d, g_pre, w_in_p, tm):
    T, D = x2d.shape
    CH = 768
    nchunk = P_COLS // CH
    ns, _, dsh = w_in_p.shape

    def body(x_ref, g_ref, w_ref, h_ref, proj_ref, fl_ref):
        h = _rms(x_ref[...], g_ref[...]).astype(BF16)
        h_ref[...] = h
        for n in range(nchunk):
            rows = slice(n * CH, (n + 1) * CH)
            r = _dot_nt(h[:, 0:dsh], w_ref[0, rows, :])
            for s in range(1, ns):
                r = r + _dot_nt(h[:, s * dsh:(s + 1) * dsh], w_ref[s, rows, :])
            if n < nchunk - 1:
                proj_ref[:, rows] = r.astype(BF16)
            else:
                fg = 1920 - n * CH
                proj_ref[:, n * CH:1920] = r[:, :fg].astype(BF16)
                fl_ref[...] = r[:, fg:fg + LANES]
                proj_ref[:, 1920:P_MAIN] = r[:, fg + LANES:].astype(BF16)

    return pl.pallas_call(
        body, name="inproj_fwd", grid=(T // tm,),
        in_specs=[BS((tm, D), lambda i: (i, 0)), BS((1, D), lambda i: (0, 0)),
                  BS((ns, P_COLS, dsh), lambda i: (0, 0, 0))],
        out_specs=[BS((tm, D), lambda i: (i, 0)), BS((tm, P_MAIN), lambda i: (i, 0)), BS((tm, LANES), lambda i: (i, 0))],
        out_shape=[jax.ShapeDtypeStruct((T, D), BF16), jax.ShapeDtypeStruct((T, P_MAIN), BF16),
                   jax.ShapeDtypeStruct((T, LANES), F32)],
        compiler_params=_cp(("arbitrary",)),
    )(x2d, g_pre, w_in_p)


def _gate_fwd(flog3, bf_row):
    Bl, S, _ = flog3.shape
    nb = S // LANES

    def body(f_ref, b_ref, bq_ref, bk_ref, fr_ref):
        row = _iota((LANES, LANES), 0)
        lane = _iota((LANES, LANES), 1)
        one = jnp.ones((LANES, LANES), BF16)
        zero = jnp.zeros((LANES, LANES), BF16)

        def blk(j, carry):
            r0 = pl.multiple_of(j * LANES, LANES)
            fl = f_ref[0, pl.ds(r0, LANES), :] + b_ref[...]
            fr_ref[0, j] = fl.T[0:8, :]
            c = _logsig(fl)
            for k in (1, 2, 4, 8, 16, 32, 64):
                c = c + jnp.where(row >= k, pltpu.roll(c, k, 0), 0.0)
            c = c + carry
            for h in range(N_FOX_HEADS):
                col = jnp.sum(jnp.where(lane == h, c, 0.0), axis=1, keepdims=True)
                hi = col.astype(BF16)
                rest = col - hi.astype(F32)
                mid = rest.astype(BF16)
                lo = (rest - mid.astype(F32)).astype(BF16)
                base = _bias_lane(h)
                bq = jnp.where(lane == base, hi, jnp.where(lane == base + 1, mid, jnp.where(lane == base + 2, lo, zero)))
                bq = jnp.where((lane >= base + 3) & (lane < base + 6), one, bq)
                bk = jnp.where(lane == base + 3, -hi, jnp.where(lane == base + 4, -mid, jnp.where(lane == base + 5, -lo, zero)))
                bk = jnp.where((lane >= base) & (lane < base + 3), one, bk)
                bq_ref[0, h, pl.ds(r0, LANES), :] = bq
                bk_ref[0, h, pl.ds(r0, LANES), :] = bk
            return _colsum(jnp.where(row == LANES - 1, c, 0.0))

        lax.fori_loop(0, nb, blk, jnp.zeros((1, LANES), F32))

    slab = BS((1, N_FOX_HEADS, S, LANES), lambda b: (b, 0, 0, 0))
    return pl.pallas_call(
        body, name="gate_fwd", grid=(Bl,),
        in_specs=[BS((1, S, LANES), lambda b: (b, 0, 0)), BS((1, LANES), lambda b: (0, 0))],
        out_specs=[slab, slab, BS((1, nb, 8, LANES), lambda b: (b, 0, 0, 0))],
        out_shape=[jax.ShapeDtypeStruct((Bl, N_FOX_HEADS, S, LANES), BF16),
                   jax.ShapeDtypeStruct((Bl, N_FOX_HEADS, S, LANES), BF16),
                   jax.ShapeDtypeStruct((Bl, nb, 8, LANES), F32)],
        compiler_params=_cp(("arbitrary",)),
    )(flog3, bf_row)


def _bias_lane(h):
    return HEAD if h % 2 == 0 else 0


def _sgu_pre(zu, zv, g_sgu):
    return _gelu(zu), _rms(_gelu(zv), g_sgu)


def _sgu_fwd(proj, g_sgu, ws_tril, bs_full, tm):
    T = proj.shape[0]
    nch = tm // CHUNK

    def body(zu_ref, zv_ref, g_ref, ws_ref, b_ref, ya_ref):
        lane = _iota((CHUNK, LANES), 1)
        u, vn = _sgu_pre(zu_ref[...].astype(F32), zv_ref[...].astype(F32), g_ref[...])
        vn = vn.astype(BF16)
        for c in range(nch):
            rs = slice(c * CHUNK, (c + 1) * CHUNK)
            for j in range(3):
                cs = slice(j * LANES, (j + 1) * LANES)
                vp = vn[rs, cs]
                z = jnp.where(lane < HEAD, _dot(ws_ref[2 * j], vp), _dot(ws_ref[2 * j + 1], vp)) + b_ref[:, cs]
                ya_ref[rs, cs] = u[rs, cs] * z

    return pl.pallas_call(
        body, name="sgu_fwd", grid=(T // tm,),
        in_specs=[BS((tm, A_W), lambda i: (i, 0)), BS((tm, A_W), lambda i: (i, 1)), BS((1, A_W), lambda i: (0, 0)),
                  BS((6, CHUNK, CHUNK), lambda i: (0, 0, 0)), BS((CHUNK, A_W), lambda i: (0, 0))],
        out_specs=BS((tm, A_W), lambda i: (i, 0)),
        out_shape=jax.ShapeDtypeStruct((T, A_W), F32),
        compiler_params=_cp(("arbitrary",)),
    )(proj, proj, g_sgu, ws_tril, bs_full)


def _fox_fwd(proj, bq, bk, Bl, S):
    T = Bl * S
    nq = S // Q_BLK
    qc, kc, vc = 768 // LANES, 1152 // LANES, 1536 // LANES

    def body(q_ref, k_ref, v_ref, bq_ref, bk_ref, o_ref, lse_ref, ka_ref, va_ref):
        lane_s = _iota((S, LANES), 1)
        lane = _iota((Q_BLK, LANES), 1)
        tri = _iota((Q_BLK, Q_BLK), 1) <= _iota((Q_BLK, Q_BLK), 0)
        k = k_ref[...]
        v = v_ref[...]
        for hh in range(2):
            data = (lane_s < HEAD) if hh == 0 else (lane_s >= HEAD)
            ka_ref[hh] = jnp.where(data, k, bk_ref[0, hh])
            va_ref[hh] = jnp.where(lane_s == _bias_lane(hh), jnp.ones_like(v), v)
        for i in range(nq):
            r0 = i * Q_BLK
            q = q_ref[r0:r0 + Q_BLK, :]
            o_out = jnp.zeros((Q_BLK, LANES), F32)
            lse_out = jnp.zeros((Q_BLK, LANES), F32)
            for hh in range(2):
                hmask = (lane < HEAD) if hh == 0 else (lane >= HEAD)
                qa = jnp.where(hmask, q * 0.125, bq_ref[0, hh, r0:r0 + Q_BLK, :])
                sd = jnp.where(tri, _dot_nt(qa, ka_ref[hh, r0:r0 + Q_BLK, :]), NEG)
                m = jnp.max(sd, axis=1, keepdims=True)
                if i:
                    sf = _dot_nt(qa, ka_ref[hh, 0:r0, :])
                    m = jnp.maximum(m, jnp.max(sf, axis=1, keepdims=True))
                acc = _dot(jnp.exp(sd - m), va_ref[hh, r0:r0 + Q_BLK, :])
                if i:
                    acc = acc + _dot(jnp.exp(sf - m), va_ref[hh, 0:r0, :])
                l = jnp.sum(jnp.where(lane == _bias_lane(hh), acc, 0.0), axis=1, keepdims=True)
                o_out = jnp.where(hmask, acc / l, o_out)
                lse_out = jnp.where(hmask, m + jnp.log(l), lse_out)
            o_ref[r0:r0 + Q_BLK, :] = o_out
            lse_ref[0, r0:r0 + Q_BLK, :] = lse_out

    seq = lambda c0: BS((S, LANES), lambda b, p: (b, c0 + p))
    pair = BS((1, 2, S, LANES), lambda b, p: (b, p, 0, 0))
    return pl.pallas_call(
        body, name="fox_fwd", grid=(Bl, 3),
        in_specs=[seq(qc), seq(kc), seq(vc), pair, pair],
        out_specs=[seq(0), BS((1, S, LANES), lambda b, p: (p, b, 0))],
        out_shape=[jax.ShapeDtypeStruct((T, B_W), F32), jax.ShapeDtypeStruct((3, T, LANES), F32)],
        scratch_shapes=[pltpu.VMEM((2, S, LANES), BF16), pltpu.VMEM((2, S, LANES), BF16)],
        compiler_params=_cp(("arbitrary", "arbitrary")),
    )(proj, proj, proj, bq, bk)


def _memkv_fwd(mem, g_mem, w_kv):
    Bl, Mt, D = mem.shape

    def body(m_ref, g_ref, w_ref, mn_ref, kv_ref):
        mn = _rms(m_ref[0], g_ref[...]).astype(BF16)
        mn_ref[0] = mn
        kv_ref[0] = jnp.dot(mn, w_ref[...], preferred_element_type=F32).astype(BF16)

    return pl.pallas_call(
        body, name="memkv_fwd", grid=(Bl,),
        in_specs=[BS((1, Mt, D), lambda b: (b, 0, 0)), BS((1, D), lambda b: (0, 0)), BS((D, 2 * M_W), lambda b: (0, 0))],
        out_specs=[BS((1, Mt, D), lambda b: (b, 0, 0)), BS((1, Mt, 2 * M_W), lambda b: (b, 0, 0))],
        out_shape=[jax.ShapeDtypeStruct((Bl, Mt, D), BF16), jax.ShapeDtypeStruct((Bl, Mt, 2 * M_W), BF16)],
        compiler_params=_cp(("arbitrary",)),
    )(mem, g_mem, w_kv)


def _memattn_fwd(proj, kv, Bl, S, tq):
    T = Bl * S
    nq = S // tq
    Mt = kv.shape[1]
    qc = 1920 // LANES

    def body(q_ref, km_ref, vm_ref, o_ref):
        lane = _iota((tq, LANES), 1)
        q = q_ref[...]
        out = jnp.zeros((tq, LANES), F32)
        for hh in range(2):
            hmask = (lane < HEAD) if hh == 0 else (lane >= HEAD)
            qs = jnp.where(hmask, q, jnp.zeros_like(q)) * 0.125
            s = _dot_nt(qs, km_ref[0])
            pe = jnp.exp(s - jnp.max(s, axis=1, keepdims=True))
            pn = pe / jnp.sum(pe, axis=1, keepdims=True)
            out = jnp.where(hmask, _dot(pn, vm_ref[0]), out)
        o_ref[...] = out

    return pl.pallas_call(
        body, name="memattn_fwd", grid=(Bl, 2, nq),
        in_specs=[BS((tq, LANES), lambda b, p, i: (b * nq + i, qc + p)),
                  BS((1, Mt, LANES), lambda b, p, i: (b, 0, p)),
                  BS((1, Mt, LANES), lambda b, p, i: (b, 0, 2 + p))],
        out_specs=BS((tq, LANES), lambda b, p, i: (b * nq + i, p)),
        out_shape=jax.ShapeDtypeStruct((T, M_W), F32),
        compiler_params=_cp(("arbitrary", "arbitrary", "arbitrary")),
    )(proj, kv, kv)


def _mix_norms(ya, yb, ym, ga, gb, gm):
    return _rms(ya, ga), _rms(yb, gb), _rms(ym, gm)


def _outproj_fwd(ya, yb, ym, x2d, ga, gb, gm, g_post, g_pre2, w_out, tm):
    T, D = x2d.shape

    def body(ya_ref, yb_ref, ym_ref, x_ref, ga_ref, gb_ref, gm_ref, gp_ref, g2_ref, w_ref,
             y_ref, o_ref, x1_ref, h2_ref):
        na, nb_, nm = _mix_norms(ya_ref[...], yb_ref[...], ym_ref[...], ga_ref[...], gb_ref[...], gm_ref[...])
        y_ref[:, 0:A_W] = na.astype(BF16)
        y_ref[:, A_W:A_W + B_W] = nb_.astype(BF16)
        y_ref[:, A_W + B_W:] = nm.astype(BF16)
        o = jnp.dot(y_ref[...], w_ref[...], preferred_element_type=F32)
        o_ref[...] = o
        x1 = x_ref[...] + _rms(o, gp_ref[...])
        x1_ref[...] = x1
        h2_ref[...] = _rms(x1, g2_ref[...]).astype(BF16)

    row = lambda w: BS((tm, w), lambda i: (i, 0))
    vec = lambda w: BS((1, w), lambda i: (0, 0))
    return pl.pallas_call(
        body, name="outproj_fwd", grid=(T // tm,),
        in_specs=[row(A_W), row(B_W), row(M_W), row(D), vec(A_W), vec(B_W), vec(M_W), vec(D), vec(D),
                  BS((A_W + B_W + M_W, D), lambda i: (0, 0))],
        out_specs=[row(A_W + B_W + M_W), row(D), row(D), row(D)],
        out_shape=[jax.ShapeDtypeStruct((T, A_W + B_W + M_W), BF16), jax.ShapeDtypeStruct((T, D), F32),
                   jax.ShapeDtypeStruct((T, D), F32), jax.ShapeDtypeStruct((T, D), BF16)],
        compiler_params=_cp(("arbitrary",)),
    )(ya, yb, ym, x2d, ga, gb, gm, g_post, g_pre2, w_out)


def _ffn_fwd(h2, x1, target, wg, wu, wd, g_post, tm):
    T, D = x1.shape
    ns, F, _ = wg.shape

    def body(h_ref, x1_ref, t_ref, wg_ref, wu_ref, wd_ref, gp_ref,
             gs_ref, us_ref, dff_ref, dx2_ref, dgp_ref, loss_ref, acc_ref):
        i = pl.program_id(0)
        j = pl.program_id(1)
        h = h_ref[...]
        g = _dot_nt(h, wg_ref[0])
        u = _dot_nt(h, wu_ref[0])
        gs_ref[0] = g.astype(BF16)
        us_ref[0] = u.astype(BF16)
        part = _dot(_silu_mul(g, u), wd_ref[0])
        _acc(acc_ref, part, j == 0)

        @pl.when(j == ns - 1)
        def _():
            ff = acc_ref[...]
            diff = x1_ref[...] + _rms(ff, gp_ref[...]) - t_ref[...]
            dx2 = diff * (1.0 / D)
            dff, dgp = _rms_bwd(ff, gp_ref[...], dx2)
            dx2_ref[...] = dx2
            dff_ref[...] = dff.astype(BF16)
            lpart = jnp.sum(_colsum(diff * diff), axis=1, keepdims=True) * (0.5 / D)
            _acc(dgp_ref, dgp, i == 0)
            _acc(loss_ref, jnp.broadcast_to(lpart, (1, LANES)), i == 0)

    row = lambda w: BS((tm, w), lambda i, j: (i, 0))
    return pl.pallas_call(
        body, name="ffn_fwd", grid=(T // tm, ns),
        in_specs=[row(D), row(D), row(D), BS((1, F, D), lambda i, j: (j, 0, 0)), BS((1, F, D), lambda i, j: (j, 0, 0)),
                  BS((1, F, D), lambda i, j: (j, 0, 0)), BS((1, D), lambda i, j: (0, 0))],
        out_specs=[BS((1, tm, F), lambda i, j: (j, i, 0)), BS((1, tm, F), lambda i, j: (j, i, 0)), row(D), row(D),
                   BS((1, D), lambda i, j: (0, 0)), BS((1, LANES), lambda i, j: (0, 0))],
        out_shape=[jax.ShapeDtypeStruct((ns, T, F), BF16), jax.ShapeDtypeStruct((ns, T, F), BF16),
                   jax.ShapeDtypeStruct((T, D), BF16), jax.ShapeDtypeStruct((T, D), F32),
                   jax.ShapeDtypeStruct((1, D), F32), jax.ShapeDtypeStruct((1, LANES), F32)],
        scratch_shapes=[pltpu.VMEM((tm, D), F32)],
        compiler_params=_cp(("arbitrary", "arbitrary")),
    )(h2, x1, target, wg, wu, wd, g_post)


def _ffn_bwd(dff, h2, gs, us, wg, wu, wd, tm):
    T, D = h2.shape
    ns, F, _ = wg.shape

    def body(dff_ref, h_ref, gs_ref, us_ref, wg_ref, wu_ref, wd_ref, dh_ref, dwg_ref, dwu_ref, dwd_ref):
        first = pl.program_id(1) == 0
        dff = dff_ref[...]
        h = h_ref[...]
        parts = []
        for r in range(ROW_SPLIT):
            rows = slice(r * (tm // ROW_SPLIT), (r + 1) * (tm // ROW_SPLIT))
            dact = _dot_nt(dff[rows], wd_ref[0])
            g = gs_ref[0, rows, :].astype(F32)
            u = us_ref[0, rows, :].astype(F32)
            sig = _sigmoid(g)
            gsig = g * sig
            dg = (dact * u * (sig + gsig * (1.0 - sig))).astype(BF16)
            du = (dact * gsig).astype(BF16)
            dh_ref[0, rows, :] = (_dot(dg, wg_ref[0]) + _dot(du, wu_ref[0])).astype(BF16)
            parts.append(((gsig * u).astype(BF16), dg, du))
        a, dg, du = [jnp.concatenate(p, axis=0) for p in zip(*parts)]
        _acc(dwd_ref, _dot_tn(a, dff)[None], first)
        _acc(dwg_ref, _dot_tn(dg, h)[None], first)
        _acc(dwu_ref, _dot_tn(du, h)[None], first)

    row = BS((tm, D), lambda j, i: (i, 0))
    sh = BS((1, tm, F), lambda j, i: (j, i, 0))
    wsh = BS((1, F, D), lambda j, i: (j, 0, 0))
    return pl.pallas_call(
        body, name="ffn_bwd", grid=(ns, T // tm),
        in_specs=[row, row, sh, sh, wsh, wsh, wsh],
        out_specs=[BS((1, tm, D), lambda j, i: (j, i, 0)), wsh, wsh, wsh],
        out_shape=[jax.ShapeDtypeStruct((ns, T, D), BF16)] + [jax.ShapeDtypeStruct((ns, F, D), F32)] * 3,
        compiler_params=_cp(("arbitrary", "arbitrary")),
    )(dff, h2, gs, us, wg, wu, wd)


def _mm_tn(a, b, name, tk):
    T, M = a.shape
    N = b.shape[1]
    tk = min(tk, T)

    def body(a_ref, b_ref, o_ref):
        _acc(o_ref, _dot_tn(a_ref[...], b_ref[...]), pl.program_id(0) == 0)

    return pl.pallas_call(
        body, name=name, grid=(T // tk,),
        in_specs=[BS((tk, M), lambda t: (t, 0)), BS((tk, N), lambda t: (t, 0))],
        out_specs=BS((M, N), lambda t: (0, 0)),
        out_shape=jax.ShapeDtypeStruct((M, N), F32),
        compiler_params=_cp(("arbitrary",)),
    )(a, b)


def _dw_in(dproj, h, ns, tk):
    T, M = dproj.shape
    D = h.shape[1]
    dsh = D // ns
    tk = min(tk, T)

    def body(a_ref, b_ref, o_ref, acc_ref):
        t = pl.program_id(0)
        _acc(acc_ref, _dot_tn(b_ref[...], a_ref[...]), t == 0)

        @pl.when(t == pl.num_programs(0) - 1)
        def _():
            for s in range(ns):
                o_ref[s] = acc_ref[s * dsh:(s + 1) * dsh, :].T

    return pl.pallas_call(
        body, name="dw_in", grid=(T // tk,),
        in_specs=[BS((tk, M), lambda t: (t, 0)), BS((tk, D), lambda t: (t, 0))],
        out_specs=BS((ns, M, dsh), lambda t: (0, 0, 0)),
        out_shape=jax.ShapeDtypeStruct((ns, M, dsh), F32),
        scratch_shapes=[pltpu.VMEM((D, M), F32)],
        compiler_params=_cp(("arbitrary",)),
    )(dproj, h)


def _outproj_bwd(dh2, x1, dx2, o, ya, yb, ym, ga, gb, gm, g_post, g_pre2, w_out, tm):
    T, D = x1.shape
    ns = dh2.shape[0]

    def body(dh_ref, x1_ref, dx2_ref, o_ref, ya_ref, yb_ref, ym_ref, ga_ref, gb_ref, gm_ref, gp_ref, g2_ref, w_ref,
             dx1_ref, do_ref, dya_ref, dyb_ref, dym_ref, dga_ref, dgb_ref, dgm_ref, dgp_ref, dg2_ref):
        first = pl.program_id(0) == 0
        dh = dh_ref[0].astype(F32)
        for j in range(1, ns):
            dh = dh + dh_ref[j].astype(F32)
        dxa, dg2 = _rms_bwd(x1_ref[...], g2_ref[...], dh)
        dx1 = dx2_ref[...] + dxa
        dx1_ref[...] = dx1
        _acc(dg2_ref, dg2, first)
        do, dgp = _rms_bwd(o_ref[...], gp_ref[...], dx1)
        do = do.astype(BF16)
        do_ref[...] = do
        dy = _dot_nt(do, w_ref[...])
        dya_ref[...], dga = _rms_bwd(ya_ref[...], ga_ref[...], dy[:, 0:A_W])
        dyb_ref[...], dgb = _rms_bwd(yb_ref[...], gb_ref[...], dy[:, A_W:A_W + B_W])
        dym_ref[...], dgm = _rms_bwd(ym_ref[...], gm_ref[...], dy[:, A_W + B_W:])
        _acc(dga_ref, dga, first)
        _acc(dgb_ref, dgb, first)
        _acc(dgm_ref, dgm, first)
        _acc(dgp_ref, dgp, first)

    row = lambda w: BS((tm, w), lambda i: (i, 0))
    vec = lambda w: BS((1, w), lambda i: (0, 0))
    sds = jax.ShapeDtypeStruct
    return pl.pallas_call(
        body, name="outproj_bwd", grid=(T // tm,),
        in_specs=[BS((ns, tm, D), lambda i: (0, i, 0)), row(D), row(D), row(D), row(A_W), row(B_W), row(M_W),
                  vec(A_W), vec(B_W), vec(M_W), vec(D), vec(D), BS((A_W + B_W + M_W, D), lambda i: (0, 0))],
        out_specs=[row(D), row(D), row(A_W), row(B_W), row(M_W), vec(A_W), vec(B_W), vec(M_W), vec(D), vec(D)],
        out_shape=[sds((T, D), F32), sds((T, D), BF16), sds((T, A_W), F32), sds((T, B_W), F32), sds((T, M_W), F32),
                   sds((1, A_W), F32), sds((1, B_W), F32), sds((1, M_W), F32), sds((1, D), F32), sds((1, D), F32)],
        compiler_params=_cp(("arbitrary",)),
    )(dh2, x1, dx2, o, ya, yb, ym, ga, gb, gm, g_post, g_pre2, w_out)


def _sgu_bwd(proj, dya, g_sgu, ws_tril, bs_full, tm):
    T = proj.shape[0]
    nch = tm // CHUNK

    def body(zu_ref, zv_ref, dy_ref, g_ref, ws_ref, b_ref, dzu_ref, dzv_ref, dws_ref, dbs_ref, dg_ref,
             du_ref, dvn_ref, dbf_ref):
        step = pl.program_id(0)
        first = step == 0
        lane = _iota((CHUNK, LANES), 1)
        tril = _iota((CHUNK, CHUNK), 0) >= _iota((CHUNK, CHUNK), 1)
        (u, vn), vjp = jax.vjp(_sgu_pre, zu_ref[...].astype(F32), zv_ref[...].astype(F32), g_ref[...])
        vnb = vn.astype(BF16)
        dy = dy_ref[...]

        @pl.when(first)
        def _():
            dws_ref[...] = jnp.zeros_like(dws_ref)
            dbf_ref[...] = jnp.zeros_like(dbf_ref)

        for c in range(nch):
            rs = slice(c * CHUNK, (c + 1) * CHUNK)
            for j in range(3):
                cs = slice(j * LANES, (j + 1) * LANES)
                vp = vnb[rs, cs]
                z = jnp.where(lane < HEAD, _dot(ws_ref[2 * j], vp), _dot(ws_ref[2 * j + 1], vp)) + b_ref[:, cs]
                du_ref[rs, cs] = dy[rs, cs] * z
                dz = dy[rs, cs] * u[rs, cs]
                dbf_ref[:, cs] += dz
                dzb = dz.astype(BF16)
                dz0 = jnp.where(lane < HEAD, dzb, jnp.zeros_like(dzb))
                dz1 = jnp.where(lane >= HEAD, dzb, jnp.zeros_like(dzb))
                dvn_ref[rs, cs] = jnp.where(lane < HEAD, _dot_tn(ws_ref[2 * j], dzb), _dot_tn(ws_ref[2 * j + 1], dzb))
                dws_ref[2 * j] += jnp.where(tril, _dot_nt(dz0, vp), 0.0)
                dws_ref[2 * j + 1] += jnp.where(tril, _dot_nt(dz1, vp), 0.0)
        dzu, dzv, dg = vjp((du_ref[...], dvn_ref[...]))
        dzu_ref[...] = dzu.astype(BF16)
        dzv_ref[...] = dzv.astype(BF16)
        _acc(dg_ref, dg, first)

        @pl.when(step == pl.num_programs(0) - 1)
        def _():
            out = jnp.zeros((CHUNK, LANES), F32)
            for j in range(3):
                slab = dbf_ref[:, j * LANES:(j + 1) * LANES]
                lo = jnp.sum(jnp.where(lane < HEAD, slab, 0.0), axis=1, keepdims=True)
                hi = jnp.sum(jnp.where(lane >= HEAD, slab, 0.0), axis=1, keepdims=True)
                out = out + jnp.where(lane == 2 * j, lo, 0.0) + jnp.where(lane == 2 * j + 1, hi, 0.0)
            dbs_ref[...] = out

    return pl.pallas_call(
        body, name="sgu_bwd", grid=(T // tm,),
        in_specs=[BS((tm, A_W), lambda i: (i, 0)), BS((tm, A_W), lambda i: (i, 1)), BS((tm, A_W), lambda i: (i, 0)),
                  BS((1, A_W), lambda i: (0, 0)), BS((6, CHUNK, CHUNK), lambda i: (0, 0, 0)),
                  BS((CHUNK, A_W), lambda i: (0, 0))],
        out_specs=[BS((tm, A_W), lambda i: (i, 0)), BS((tm, A_W), lambda i: (i, 0)),
                   BS((6, CHUNK, CHUNK), lambda i: (0, 0, 0)), BS((CHUNK, LANES), lambda i: (0, 0)),
                   BS((1, A_W), lambda i: (0, 0))],
        out_shape=[jax.ShapeDtypeStruct((T, A_W), BF16), jax.ShapeDtypeStruct((T, A_W), BF16),
                   jax.ShapeDtypeStruct((6, CHUNK, CHUNK), F32), jax.ShapeDtypeStruct((CHUNK, LANES), F32),
                   jax.ShapeDtypeStruct((1, A_W), F32)],
        scratch_shapes=[pltpu.VMEM((tm, A_W), F32), pltpu.VMEM((tm, A_W), F32), pltpu.VMEM((CHUNK, A_W), F32)],
        compiler_params=_cp(("arbitrary",)),
    )(proj, proj, dya, g_sgu, ws_tril, bs_full)


def _memattn_bwd(proj, kv, dym, Bl, S, tq):
    T = Bl * S
    nq = S // tq
    Mt = kv.shape[1]
    qc = 1920 // LANES

    def body(q_ref, km_ref, vm_ref, do_ref, dq_ref, dkm_ref, dvm_ref):
        first = pl.program_id(2) == 0
        lane = _iota((tq, LANES), 1)
        q = q_ref[...]
        do = do_ref[...]
        dq_out = jnp.zeros((tq, LANES), F32)
        dkm = jnp.zeros((Mt, LANES), F32)
        dvm = jnp.zeros((Mt, LANES), F32)
        for hh in range(2):
            hmask = (lane < HEAD) if hh == 0 else (lane >= HEAD)
            qs = jnp.where(hmask, q, jnp.zeros_like(q)) * 0.125
            dom = jnp.where(hmask, do, 0.0).astype(BF16)
            s = _dot_nt(qs, km_ref[0])
            pe = jnp.exp(s - jnp.max(s, axis=1, keepdims=True))
            pn = pe / jnp.sum(pe, axis=1, keepdims=True)
            dp = _dot_nt(dom, vm_ref[0])
            ds = (pn * (dp - jnp.sum(pn * dp, axis=1, keepdims=True))).astype(BF16)
            dq_out = jnp.where(hmask, _dot(ds, km_ref[0]) * 0.125, dq_out)
            dkm = dkm + _dot_tn(ds, qs)
            dvm = dvm + _dot_tn(pn, dom)
        dq_ref[...] = dq_out.astype(BF16)
        _acc(dkm_ref, dkm[None], first)
        _acc(dvm_ref, dvm[None], first)

    return pl.pallas_call(
        body, name="memattn_bwd", grid=(Bl, 2, nq),
        in_specs=[BS((tq, LANES), lambda b, p, i: (b * nq + i, qc + p)),
                  BS((1, Mt, LANES), lambda b, p, i: (b, 0, p)),
                  BS((1, Mt, LANES), lambda b, p, i: (b, 0, 2 + p)),
                  BS((tq, LANES), lambda b, p, i: (b * nq + i, p))],
        out_specs=[BS((tq, LANES), lambda b, p, i: (b * nq + i, p)),
                   BS((1, Mt, LANES), lambda b, p, i: (b, 0, p)),
                   BS((1, Mt, LANES), lambda b, p, i: (b, 0, p))],
        out_shape=[jax.ShapeDtypeStruct((T, M_W), BF16), jax.ShapeDtypeStruct((Bl, Mt, M_W), F32),
                   jax.ShapeDtypeStruct((Bl, Mt, M_W), F32)],
        compiler_params=_cp(("arbitrary", "arbitrary", "arbitrary")),
    )(proj, kv, kv, dym)


def _memkv_bwd(dkm, dvm, memn, mem, g_mem, w_kv):
    Bl, Mt, D = mem.shape

    def body(dk_ref, dv_ref, mn_ref, m_ref, g_ref, w_ref, dw_ref, dg_ref):
        first = pl.program_id(0) == 0
        dk = dk_ref[0].astype(BF16)
        dv = dv_ref[0].astype(BF16)
        mn = mn_ref[0]
        dmn = _dot_nt(dk, w_ref[:, 0:M_W]) + _dot_nt(dv, w_ref[:, M_W:])
        _, dg = _rms_bwd(m_ref[0], g_ref[...], dmn)
        _acc(dg_ref, dg, first)

        @pl.when(first)
        def _():
            dw_ref[...] = jnp.zeros_like(dw_ref)

        dw_ref[:, 0:M_W] += _dot_tn(mn, dk)
        dw_ref[:, M_W:] += _dot_tn(mn, dv)

    return pl.pallas_call(
        body, name="memkv_bwd", grid=(Bl,),
        in_specs=[BS((1, Mt, M_W), lambda b: (b, 0, 0)), BS((1, Mt, M_W), lambda b: (b, 0, 0)),
                  BS((1, Mt, D), lambda b: (b, 0, 0)), BS((1, Mt, D), lambda b: (b, 0, 0)),
                  BS((1, D), lambda b: (0, 0)), BS((D, 2 * M_W), lambda b: (0, 0))],
        out_specs=[BS((D, 2 * M_W), lambda b: (0, 0)), BS((1, D), lambda b: (0, 0))],
        out_shape=[jax.ShapeDtypeStruct((D, 2 * M_W), F32), jax.ShapeDtypeStruct((1, D), F32)],
        compiler_params=_cp(("arbitrary",)),
    )(dkm, dvm, memn, mem, g_mem, w_kv)


def _fox_bwd(proj, dyb, lse, bq, bk, Bl, S):
    T = Bl * S
    nq = S // Q_BLK
    nb = S // LANES
    qc, kc, vc = 768 // LANES, 1152 // LANES, 1536 // LANES

    def body(q_ref, k_ref, v_ref, do_ref, lse_ref, bq_ref, bk_ref,
             dq_ref, dk_ref, dv_ref, dcr_ref, ka_ref, dka_ref, dva_ref):
        p = pl.program_id(1)
        lane_s = _iota((S, LANES), 1)
        lane = _iota((Q_BLK, LANES), 1)
        sub = _iota((8, LANES), 0)
        tri = _iota((Q_BLK, Q_BLK), 1) <= _iota((Q_BLK, Q_BLK), 0)
        k = k_ref[...]
        for hh in range(2):
            data = (lane_s < HEAD) if hh == 0 else (lane_s >= HEAD)
            ka_ref[hh] = jnp.where(data, k, bk_ref[0, hh])
        dka_ref[...] = jnp.zeros_like(dka_ref)
        dva_ref[...] = jnp.zeros_like(dva_ref)

        @pl.when(p == 0)
        def _():
            dcr_ref[...] = jnp.zeros_like(dcr_ref)

        def add_colsums(ds, first_blk, h):
            cs = _colsum(ds)
            for jb in range(ds.shape[1] // LANES):
                dcr_ref[0, first_blk + jb] += jnp.where(sub == h, cs[:, jb * LANES:(jb + 1) * LANES], 0.0)

        for i in range(nq):
            r0 = i * Q_BLK
            r1 = r0 + Q_BLK
            q = q_ref[r0:r1, :]
            do = do_ref[r0:r1, :]
            lse_b = lse_ref[0, r0:r1, :]
            dq_out = jnp.zeros((Q_BLK, LANES), F32)
            for hh in range(2):
                hmask = (lane < HEAD) if hh == 0 else (lane >= HEAD)
                h = 2 * p + hh
                qs = jnp.where(hmask, q * 0.125, jnp.zeros_like(q))
                qa = jnp.where(hmask, q * 0.125, bq_ref[0, hh, r0:r1, :])
                dob = jnp.where(hmask, do, 0.0).astype(BF16)
                lse_h = jnp.sum(jnp.where(lane == hh * HEAD, lse_b, 0.0), axis=1, keepdims=True)
                pd = jnp.where(tri, jnp.exp(_dot_nt(qa, ka_ref[hh, r0:r1, :]) - lse_h), 0.0)
                dpd = _dot_nt(dob, v_ref[r0:r1, :])
                delta = jnp.sum(pd * dpd, axis=1, keepdims=True)
                psum = jnp.sum(pd, axis=1, keepdims=True)
                if i:
                    pf = jnp.exp(_dot_nt(qa, ka_ref[hh, 0:r0, :]) - lse_h)
                    dpf = _dot_nt(dob, v_ref[0:r0, :])
                    delta = delta + jnp.sum(pf * dpf, axis=1, keepdims=True)
                    psum = psum + jnp.sum(pf, axis=1, keepdims=True)
                delta = delta / psum
                dsd = pd * (dpd - delta)
                add_colsums(dsd, r0 // LANES, h)
                dsd = dsd.astype(BF16)
                dq_h = _dot(dsd, k_ref[r0:r1, :])
                dka_ref[r0:r1, :] += _dot_tn(dsd, qs)
                dva_ref[r0:r1, :] += _dot_tn(pd, dob)
                if i:
                    dsf = pf * (dpf - delta)
                    add_colsums(dsf, 0, h)
                    dsf = dsf.astype(BF16)
                    dq_h = dq_h + _dot(dsf, k_ref[0:r0, :])
                    dka_ref[0:r0, :] += _dot_tn(dsf, qs)
                    dva_ref[0:r0, :] += _dot_tn(pf, dob)
                dq_out = jnp.where(hmask, dq_h * 0.125, dq_out)
            dq_ref[r0:r1, :] = dq_out.astype(BF16)
        dk_ref[...] = dka_ref[...].astype(BF16)
        dv_ref[...] = dva_ref[...].astype(BF16)

    seq = lambda c0: BS((S, LANES), lambda b, p: (b, c0 + p))
    pair = BS((1, 2, S, LANES), lambda b, p: (b, p, 0, 0))
    rowblk = BS((1, nb, 8, LANES), lambda b, p: (b, 0, 0, 0))
    return pl.pallas_call(
        body, name="fox_bwd", grid=(Bl, 3),
        in_specs=[seq(qc), seq(kc), seq(vc), seq(0), BS((1, S, LANES), lambda b, p: (p, b, 0)), pair, pair],
        out_specs=[seq(0), seq(0), seq(0), rowblk],
        out_shape=[jax.ShapeDtypeStruct((T, B_W), BF16)] * 3 + [jax.ShapeDtypeStruct((Bl, nb, 8, LANES), F32)],
        scratch_shapes=[pltpu.VMEM((2, S, LANES), BF16), pltpu.VMEM((S, LANES), F32), pltpu.VMEM((S, LANES), F32)],
        compiler_params=_cp(("arbitrary", "arbitrary")),
    )(proj, proj, proj, dyb, lse, bq, bk)


def _gate_bwd(dc_row, fl_row):
    Bl, nb, _, _ = dc_row.shape

    def body(dc_ref, fl_ref, o_ref):
        lane = _iota((8, LANES), 1)

        def blk(jj, carry):
            j = nb - 1 - jj
            r = -dc_ref[0, j]
            for k in (1, 2, 4, 8, 16, 32, 64):
                r = r + jnp.where(lane < LANES - k, pltpu.roll(r, LANES - k, 1), 0.0)
            r = r + carry
            dfl = r * _sigmoid(-fl_ref[0, j])
            o_ref[0, pl.ds(pl.multiple_of(j * LANES, LANES), LANES), :] = jnp.concatenate(
                [dfl, jnp.zeros((LANES - 8, LANES), F32)], axis=0).T
            return jnp.sum(jnp.where(lane == 0, r, 0.0), axis=1, keepdims=True)

        lax.fori_loop(0, nb, blk, jnp.zeros((8, 1), F32))

    rowblk = BS((1, nb, 8, LANES), lambda b: (b, 0, 0, 0))
    return pl.pallas_call(
        body, name="gate_bwd", grid=(Bl,),
        in_specs=[rowblk, rowblk],
        out_specs=BS((1, nb * LANES, LANES), lambda b: (b, 0, 0)),
        out_shape=jax.ShapeDtypeStruct((Bl, nb * LANES, LANES), F32),
        compiler_params=_cp(("arbitrary",)),
    )(dc_row, fl_row)


def _inproj_bwd(dzu, dzv, dq, dk, dv, dqm, dfl, x2d, dx1, g_pre, w_in_p, tm):
    T, D = x2d.shape
    ns, _, dsh = w_in_p.shape

    def body(dzu_ref, dzv_ref, dq_ref, dk_ref, dv_ref, dqm_ref, dfl_ref, x_ref, dx1_ref, g_ref, w_ref,
             dp_ref, gx_ref, dg_ref, dbf_ref):
        first = pl.program_id(0) == 0
        dfl = dfl_ref[...]
        dp_ref[:, 0:384] = dzu_ref[...]
        dp_ref[:, 384:768] = dzv_ref[...]
        dp_ref[:, 768:1152] = dq_ref[...]
        dp_ref[:, 1152:1536] = dk_ref[...]
        dp_ref[:, 1536:1920] = dv_ref[...]
        dp_ref[:, 1920:2048] = dfl.astype(BF16)
        dp_ref[:, 2048:2304] = dqm_ref[...]
        dh = jnp.concatenate([_dot(dp_ref[...], w_ref[s]) for s in range(ns)], axis=1)
        dxa, dg = _rms_bwd(x_ref[...], g_ref[...], dh)
        gx_ref[...] = dx1_ref[...] + dxa
        _acc(dg_ref, dg, first)
        _acc(dbf_ref, _colsum(dfl), first)

    row = lambda w: BS((tm, w), lambda i: (i, 0))
    return pl.pallas_call(
        body, name="inproj_bwd", grid=(T // tm,),
        in_specs=[row(A_W), row(A_W), row(B_W), row(B_W), row(B_W), row(M_W), row(LANES), row(D), row(D),
                  BS((1, D), lambda i: (0, 0)), BS((ns, P_COLS, dsh), lambda i: (0, 0, 0))],
        out_specs=[row(P_COLS), row(D), BS((1, D), lambda i: (0, 0)), BS((1, LANES), lambda i: (0, 0))],
        out_shape=[jax.ShapeDtypeStruct((T, P_COLS), BF16), jax.ShapeDtypeStruct((T, D), F32),
                   jax.ShapeDtypeStruct((1, D), F32), jax.ShapeDtypeStruct((1, LANES), F32)],
        compiler_params=_cp(("arbitrary",)),
    )(dzu, dzv, dq, dk, dv, dqm, dfl, x2d, dx1, g_pre, w_in_p)


def _local_step(x, mem, target, W, P, reduce=None):
    Bl, S, D = x.shape
    T = Bl * S
    tm = min(512, T)
    x2d = x.reshape(T, D)
    t2d = target.reshape(T, D)
    vec = lambda a: a.reshape(1, -1)
    bf_row = jnp.pad(P["b_f"].reshape(1, -1), ((0, 0), (0, LANES - N_FOX_HEADS)))
    tril = jnp.tril(jnp.ones((CHUNK, CHUNK), bool))
    ws_tril = jnp.where(tril[None], P["w_s"][0], 0.0).astype(BF16)
    bs_full = jnp.repeat(P["b_s"][0].T, HEAD, axis=1)
    g_pre, g_sgu = vec(P["g_pre_mix"]), vec(P["g_sgu"])
    ga, gb, gm = vec(P["g_out_a"]), vec(P["g_out_b"]), vec(P["g_out_m"])
    g_mem, g_post, g_pre2, g_post2 = vec(P["g_mem"]), vec(P["g_post_mix"]), vec(P["g_pre_ffn"]), vec(P["g_post_ffn"])

    h, proj, flog = _inproj_fwd(x2d, g_pre, W["w_in"], tm)
    bq, bk, fl_row = _gate_fwd(flog.reshape(Bl, S, LANES), bf_row)
    ya = _sgu_fwd(proj, g_sgu, ws_tril, bs_full, tm)
    yb, lse = _fox_fwd(proj, bq, bk, Bl, S)
    memn, kv = _memkv_fwd(mem, g_mem, W["w_mem_kv"])
    ym = _memattn_fwd(proj, kv, Bl, S, min(512, S))
    y, o, x1, h2 = _outproj_fwd(ya, yb, ym, x2d, ga, gb, gm, g_post, g_pre2, W["w_out"], tm)
    gs, us, dff, dx2, dg_post2, loss = _ffn_fwd(h2, x1, t2d, W["w_gate"], W["w_up"], W["w_down"], g_post2, tm)

    dh2, d_w_gate, d_w_up, d_w_down = _ffn_bwd(dff, h2, gs, us, W["w_gate"], W["w_up"], W["w_down"], min(1024, T))
    ffn = [d_w_gate, d_w_up, d_w_down]
    if reduce is not None:
        pending, _ = reduce.begin("ffn", ffn)
    dx1, do, dya, dyb, dym, dga, dgb, dgm, dg_post, dg_pre2 = _outproj_bwd(
        dh2, x1, dx2, o, ya, yb, ym, ga, gb, gm, g_post, g_pre2, W["w_out"], tm)
    if reduce is not None:
        ffn, (do, dya, dyb, dym) = reduce.finish("ffn", pending, (do, dya, dyb, dym))
    d_w_out = _mm_tn(y, do, "dw_out", 1024)
    dzu, dzv, dws, dbs_cols, dg_sgu = _sgu_bwd(proj, dya, g_sgu, ws_tril, bs_full, tm)
    dqm, dkm, dvm = _memattn_bwd(proj, kv, dym, Bl, S, min(512, S))
    d_w_kv, dg_mem = _memkv_bwd(dkm, dvm, memn, mem, g_mem, W["w_mem_kv"])
    mid = [d_w_kv, d_w_out]
    dq, dk, dv, dc_row = _fox_bwd(proj, dyb, lse, bq, bk, Bl, S)
    if reduce is not None:
        done = reduce.apply(BIG[3:], ffn)
        pending, after = reduce.begin("mid", mid, (dc_row,) + done)
        dc_row = after[0]
    dfl = _gate_bwd(dc_row, fl_row).reshape(T, LANES)
    dproj, grad_x, dg_pre, dbf = _inproj_bwd(dzu, dzv, dq, dk, dv, dqm, dfl, x2d, dx1, g_pre, W["w_in"], tm)
    if reduce is not None:
        mid, (dproj,) = reduce.finish("mid", pending, (dproj,))
    d_w_in = _dw_in(dproj, h, W["w_in"].shape[0], 1024)
    if reduce is None:
        big = dict(zip(BIG, [d_w_in] + mid + ffn))
    else:
        done = reduce.apply(BIG[1:3], mid)
        big = {"w_in": reduce.begin("in", [d_w_in], done)[0]}
    small = {"g_pre_mix": dg_pre, "b_f": dbf[:, :N_FOX_HEADS], "g_sgu": dg_sgu, "w_s": dws, "b_s": dbs_cols[:, :N_FOX_HEADS].T,
             "g_out_a": dga, "g_out_b": dgb, "g_out_m": dgm, "g_mem": dg_mem, "g_post_mix": dg_post,
             "g_pre_ffn": dg_pre2, "g_post_ffn": dg_post2, "loss": loss[:, :1]}
    return grad_x.reshape(Bl, S, D), big, small


def _place():
    return lax.axis_index("x"), lax.axis_index("y"), lax.axis_index("c")


def _exchange_on_sequencer(srcs, own_full, name, collective_id):
    n = len(srcs)

    def body(*refs):
        src, dst = refs[:n], refs[n:2 * n]
        lsem, isend, irecv, dsend, drecv = refs[2 * n:]
        x, y, c = _place()
        oc = 1 - c
        s_me = 2 * x + y
        sib = (x, y, oc)
        chips = [(1 - x, y), (x, 1 - y), (1 - x, 1 - y)]
        barrier = pltpu.get_barrier_semaphore()
        for dev in [(cx, cy, c) for cx, cy in chips] + [sib]:
            pl.semaphore_signal(barrier, inc=1, device_id=dev, device_id_type=MESH)
        pl.semaphore_wait(barrier, 4)

        def remote(a, b, ssem, rsem, dev):
            return pltpu.make_async_remote_copy(src_ref=a, dst_ref=b, send_sem=ssem, recv_sem=rsem,
                                                device_id=dev, device_id_type=MESH)

        sends, local = [], []
        for w in range(n):
            for j, (cx, cy) in enumerate(chips):
                half = src[w].at[c] if own_full else src[w].at[2 * cx + cy]
                cp = remote(half, dst[w].at[s_me, c], isend.at[w, j], irecv.at[w, j], (cx, cy, c))
                cp.start()
                sends.append(cp)
            if own_full:
                cp = remote(src[w], dst[w].at[s_me], dsend.at[w, 3], drecv.at[w, 3], sib)
            else:
                cp = remote(src[w].at[s_me], dst[w].at[s_me, c], dsend.at[w, 3], drecv.at[w, 3], sib)
                loc = pltpu.make_async_copy(src[w].at[s_me], dst[w].at[s_me, c], lsem.at[w])
                loc.start()
                local.append(loc)
            cp.start()
            sends.append(cp)
        for w in range(n):
            for j, (cx, cy) in enumerate(chips):
                landed = dst[w].at[2 * cx + cy, c]
                remote(landed, landed, isend.at[w, j], irecv.at[w, j], (cx, cy, c)).wait_recv()
                cp = remote(landed, landed, dsend.at[w, j], drecv.at[w, j], sib)
                cp.start()
                sends.append(cp)
        for w in range(n):
            for j, (cx, cy) in enumerate(chips):
                landed = dst[w].at[2 * cx + cy, oc]
                remote(landed, landed, dsend.at[w, j], drecv.at[w, j], sib).wait_recv()
            landed = dst[w].at[s_me] if own_full else dst[w].at[s_me, oc]
            remote(landed, landed, dsend.at[w, 3], drecv.at[w, 3], sib).wait_recv()
        for cp in sends:
            cp.wait_send()
        for loc in local:
            loc.wait()

    return pl.kernel(
        body, out_type=[jax.ShapeDtypeStruct((4, 2) + s.shape[1:], s.dtype) for s in srcs],
        mesh=plsc.ScalarSubcoreMesh(axis_name="sequencer", num_cores=1), name=name,
        scratch_types=[pltpu.SemaphoreType.DMA((n,)), pltpu.SemaphoreType.DMA((n, 3)), pltpu.SemaphoreType.DMA((n, 3)),
                       pltpu.SemaphoreType.DMA((n, 4)), pltpu.SemaphoreType.DMA((n, 4))],
        compiler_params=pltpu.CompilerParams(collective_id=collective_id),
    )(*srcs)


def _sibling_swap(grads, name, collective_id):
    n = len(grads)

    def body(*refs):
        g, theirs = refs[:n], refs[n:2 * n]
        ssem, rsem = refs[2 * n:]
        x, y, c = _place()
        sib = (x, y, 1 - c)
        barrier = pltpu.get_barrier_semaphore()
        pl.semaphore_signal(barrier, inc=1, device_id=sib, device_id_type=MESH)
        pl.semaphore_wait(barrier, 1)
        cps = []
        for w in range(n):
            cp = pltpu.make_async_remote_copy(src_ref=g[w].at[:, 1 - c], dst_ref=theirs[w], send_sem=ssem.at[w],
                                              recv_sem=rsem.at[w], device_id=sib, device_id_type=MESH)
            cp.start()
            cps.append(cp)
        for cp in cps:
            cp.wait()

    return pl.kernel(
        body, out_type=[jax.ShapeDtypeStruct((4,) + g.shape[2:], g.dtype) for g in grads],
        mesh=plsc.ScalarSubcoreMesh(axis_name="sequencer", num_cores=1), name=name,
        scratch_types=[pltpu.SemaphoreType.DMA((n,)), pltpu.SemaphoreType.DMA((n,))],
        compiler_params=pltpu.CompilerParams(collective_id=collective_id),
    )(*grads)


def _add_pair(core, g, theirs, name):
    _, _, hr, C = g.shape

    def body(core_ref, g_ref, t_ref, o_ref):
        o_ref[0] = (g_ref[0, 0] + t_ref[0]).astype(BF16)

    blk = BS((1, hr, C), lambda s, core_ref: (s, 0, 0))
    return pl.pallas_call(
        body, name=name,
        grid_spec=pltpu.PrefetchScalarGridSpec(
            num_scalar_prefetch=1, grid=(4,),
            in_specs=[BS((1, 1, hr, C), lambda s, core_ref: (s, core_ref[0], 0, 0)), blk], out_specs=blk),
        out_shape=jax.ShapeDtypeStruct(theirs.shape, BF16), compiler_params=_cp(("arbitrary",)))(core, g, theirs)


def _sum_chips(r, name):
    _, _, hr, C = r.shape

    def body(r_ref, o_ref):
        o_ref[...] = ((r_ref[0, 0].astype(F32) + r_ref[1, 0].astype(F32)) + r_ref[2, 0].astype(F32)) + r_ref[3, 0].astype(F32)

    return pl.pallas_call(body, name=name, grid=(2,), in_specs=[BS((4, 1, hr, C), lambda h: (0, h, 0, 0))],
                          out_specs=BS((hr, C), lambda h: (h, 0)), out_shape=jax.ShapeDtypeStruct((2 * hr, C), F32),
                          compiler_params=_cp(("arbitrary",)))(r)


class _Reducer:
    IDS = {"ffn": (4, 5), "mid": (6, 7), "in": (8, 9)}

    def __init__(self, core, apply):
        self.core = core
        self.apply = apply

    def begin(self, tag, grads, after=()):
        grads, after = lax.optimization_barrier((list(grads), after))
        g4 = [g.reshape(4, 2, -1, g.shape[-1]) for g in grads]
        return (g4, _sibling_swap(g4, "swap_" + tag, self.IDS[tag][0])), after

    def finish(self, tag, pending, hold):
        g4, theirs = pending
        sums = [_add_pair(self.core, g, t, "chip_sum_%s_%d" % (tag, k)) for k, (g, t) in enumerate(zip(g4, theirs))]
        sums, hold = lax.optimization_barrier((sums, hold))
        return _exchange_on_sequencer(sums, False, "scatter_" + tag, self.IDS[tag][1]), hold


def _small_allreduce(part):
    R = part.shape[0]
    rs = R // 8
    masks = [(mx, my, mc) for mx in (0, 1) for my in (0, 1) for mc in (0, 1)][1:]

    def body(p_ref, o_ref, buf_ref, s1, r1, s2, r2):
        x, y, c = _place()
        d = 4 * x + 2 * y + c
        mine = pl.ds(pl.multiple_of(d * rs, 8), rs)
        peers = [((x + mx) % 2, (y + my) % 2, (c + mc) % 2) for mx, my, mc in masks]
        first, second = [], []
        for k, (px, py, pc) in enumerate(peers):
            theirs = pl.ds(pl.multiple_of((4 * px + 2 * py + pc) * rs, 8), rs)
            cp = pltpu.make_async_remote_copy(src_ref=p_ref.at[theirs, :], dst_ref=buf_ref.at[d], send_sem=s1.at[k],
                                              recv_sem=r1.at[k], device_id=(px, py, pc), device_id_type=MESH)
            cp.start()
            first.append(cp)
        buf_ref[d] = p_ref[mine, :]
        for k, (px, py, pc) in enumerate(peers):
            slot = buf_ref.at[4 * px + 2 * py + pc]
            pltpu.make_async_remote_copy(src_ref=slot, dst_ref=slot, send_sem=s1.at[k], recv_sem=r1.at[k],
                                         device_id=(px, py, pc), device_id_type=MESH).wait_recv()
        total = buf_ref[0]
        for k in range(1, 8):
            total = total + buf_ref[k]
        o_ref[mine, :] = total
        for k, (px, py, pc) in enumerate(peers):
            cp = pltpu.make_async_remote_copy(src_ref=o_ref.at[mine, :], dst_ref=o_ref.at[mine, :], send_sem=s2.at[k],
                                              recv_sem=r2.at[k], device_id=(px, py, pc), device_id_type=MESH)
            cp.start()
            second.append(cp)
        for k, (px, py, pc) in enumerate(peers):
            rows = o_ref.at[pl.ds(pl.multiple_of((4 * px + 2 * py + pc) * rs, 8), rs), :]
            pltpu.make_async_remote_copy(src_ref=rows, dst_ref=rows, send_sem=s2.at[k], recv_sem=r2.at[k],
                                         device_id=(px, py, pc), device_id_type=MESH).wait_recv()
        for cp in first + second:
            cp.wait_send()

    vm = pl.BlockSpec(memory_space=pltpu.VMEM)
    return pl.pallas_call(
        body, name="small_allreduce", in_specs=[vm], out_specs=vm, out_shape=jax.ShapeDtypeStruct(part.shape, F32),
        scratch_shapes=[pltpu.VMEM((8, rs, LANES), F32)] + [pltpu.SemaphoreType.DMA((7,))] * 4,
    )(part)


def _adamw(w, g, m, v, name):
    R, C = w.shape
    summed = g.ndim == 4
    if summed:
        tr = R // 2
    else:
        tr = R if R * C * 4 <= (1 << 21) else R // 2
        if tr % 8:
            tr = R
    c1 = 1.0 / (1.0 - ADAM_B1 ** ADAM_STEP)
    c2 = 1.0 / (1.0 - ADAM_B2 ** ADAM_STEP)

    def body(w_ref, g_ref, m_ref, v_ref, *outs):
        if summed:
            g_ = ((g_ref[0, 0].astype(F32) + g_ref[1, 0].astype(F32)) + g_ref[2, 0].astype(F32)) + g_ref[3, 0].astype(F32)
            outs[0][...] = g_
        else:
            g_ = g_ref[...]
        d_ref, mo_ref, vo_ref = outs[-3:]
        m_ = ADAM_B1 * m_ref[...] + (1.0 - ADAM_B1) * g_
        v_ = ADAM_B2 * v_ref[...] + (1.0 - ADAM_B2) * (g_ * g_)
        mo_ref[...] = m_
        vo_ref[...] = v_
        d_ref[...] = -ADAM_LR * ((m_ * c1) / (jnp.sqrt(v_ * c2) + ADAM_EPS) + ADAM_WD * w_ref[...])

    blk = BS((tr, C), lambda i: (i, 0))
    g_blk = BS((4, 1, tr, C), lambda i: (0, i, 0, 0)) if summed else blk
    nout = 4 if summed else 3
    return pl.pallas_call(body, name=name, grid=(R // tr,), in_specs=[blk, g_blk, blk, blk], out_specs=[blk] * nout,
                          out_shape=[jax.ShapeDtypeStruct((R, C), F32)] * nout,
                          compiler_params=_cp(("arbitrary",)))(w, g, m, v)


SMALL = ("g_pre_mix", "b_f", "g_sgu", "w_s", "b_s", "g_out_a", "g_out_b", "g_out_m", "g_mem", "g_post_mix",
         "g_pre_ffn", "g_post_ffn")
BIG = ("w_in", "w_mem_kv", "w_out", "w_gate", "w_up", "w_down")
TRANSPOSED = ("w_in", "w_gate", "w_up")
WEIGHTS = ("g_pre_mix", "w_in", "b_f", "g_sgu", "w_s", "b_s", "g_out_a", "g_out_b", "g_out_m", "g_mem", "w_mem_kv",
           "w_out", "g_post_mix", "g_pre_ffn", "w_gate", "w_up", "w_down", "g_post_ffn")


def _rows_of(n):
    return -(-n // (8 * LANES)) * 8


def _pack(parts):
    tiles = []
    for a in parts:
        flat = a.reshape(-1).astype(F32)
        rows = _rows_of(flat.shape[0])
        tiles.append(jnp.pad(flat, (0, rows * LANES - flat.shape[0])).reshape(rows, LANES))
    total = sum(t.shape[0] for t in tiles)
    pad = -total % 64
    if pad:
        tiles.append(jnp.zeros((pad, LANES), F32))
    return jnp.concatenate(tiles, axis=0)


def _unpack(packed, shapes):
    out, r = [], 0
    for shp in shapes:
        n = 1
        for s in shp:
            n *= s
        rows = _rows_of(n)
        out.append(packed[r:r + rows].reshape(-1)[:n].reshape(shp))
        r += rows
    return out


def kernel(x, mem, g_pre_mix, w_in, b_f, g_sgu, w_s, b_s, g_out_a, g_out_b, g_out_m, g_mem, w_mem_kv, w_out, g_post_mix, g_pre_ffn, w_gate, w_up, w_down, g_post_ffn, loss_target, m_g_pre_mix, m_w_in, m_b_f, m_g_sgu, m_w_s, m_b_s, m_g_out_a, m_g_out_b, m_g_out_m, m_g_mem, m_w_mem_kv, m_w_out, m_g_post_mix, m_g_pre_ffn, m_w_gate, m_w_up, m_w_down, m_g_post_ffn, v_g_pre_mix, v_w_in, v_b_f, v_g_sgu, v_w_s, v_b_s, v_g_out_a, v_g_out_b, v_g_out_m, v_g_mem, v_w_mem_kv, v_w_out, v_g_post_mix, v_g_pre_ffn, v_w_gate, v_w_up, v_w_down, v_g_post_ffn):
    Wt = dict(g_pre_mix=g_pre_mix, w_in=w_in, b_f=b_f, g_sgu=g_sgu, w_s=w_s, b_s=b_s, g_out_a=g_out_a, g_out_b=g_out_b,
              g_out_m=g_out_m, g_mem=g_mem, w_mem_kv=w_mem_kv, w_out=w_out, g_post_mix=g_post_mix, g_pre_ffn=g_pre_ffn,
              w_gate=w_gate, w_up=w_up, w_down=w_down, g_post_ffn=g_post_ffn)
    Mo = dict(g_pre_mix=m_g_pre_mix, w_in=m_w_in, b_f=m_b_f, g_sgu=m_g_sgu, w_s=m_w_s, b_s=m_b_s, g_out_a=m_g_out_a,
              g_out_b=m_g_out_b, g_out_m=m_g_out_m, g_mem=m_g_mem, w_mem_kv=m_w_mem_kv, w_out=m_w_out,
              g_post_mix=m_g_post_mix, g_pre_ffn=m_g_pre_ffn, w_gate=m_w_gate, w_up=m_w_up, w_down=m_w_down,
              g_post_ffn=m_g_post_ffn)
    Vo = dict(g_pre_mix=v_g_pre_mix, w_in=v_w_in, b_f=v_b_f, g_sgu=v_g_sgu, w_s=v_w_s, b_s=v_b_s, g_out_a=v_g_out_a,
              g_out_b=v_g_out_b, g_out_m=v_g_out_m, g_mem=v_g_mem, w_mem_kv=v_w_mem_kv, w_out=v_w_out,
              g_post_mix=v_g_post_mix, g_pre_ffn=v_g_pre_ffn, w_gate=v_w_gate, w_up=v_w_up, w_down=v_w_down,
              g_post_ffn=v_g_post_ffn)

    gap = P_COLS - IN_COLS

    def to_kernel(n, w):
        if n in TRANSPOSED:
            w = w.T
        if n == "w_in":
            w = jnp.pad(w[:F_END], ((0, P_COLS - F_END), (0, 0))) + jnp.pad(w[F_END:], ((F_END + gap, 0), (0, 0)))
        return w

    def ungroup(g):
        return jnp.pad(g[:F_END], ((0, IN_COLS - F_END), (0, 0))) + jnp.pad(g[F_END + gap:], ((F_END, 0), (0, 0)))

    shards = {n: to_kernel(n, Wt[n][0]) for n in BIG}
    srcs = [shards[n].astype(BF16).reshape(2, shards[n].shape[0] // 2, shards[n].shape[1]) for n in BIG]
    fulls = (_exchange_on_sequencer(srcs[:1], True, "gather_w_in", 1)
             + _exchange_on_sequencer(srcs[1:3], True, "gather_kv_out", 2)
             + _exchange_on_sequencer(srcs[3:], True, "gather_ffn", 3))
    W = {}
    for n, f in zip(BIG, fulls):
        _, _, hr, C = f.shape
        W[n] = f.reshape(8 * hr, C) if n in ("w_mem_kv", "w_out") else f.reshape(4, 2 * hr, C)

    P = {n: Wt[n] for n in SMALL}
    grads, deltas, new_m, new_v = {}, {}, {}, {}

    def apply(names, landed):
        for n, r in zip(names, landed):
            wmv = [a[n][0].T if n in TRANSPOSED else a[n][0] for a in (Wt, Mo, Vo)]
            if n == "w_in":
                g = ungroup(_sum_chips(r, "sum_chips_" + n))
                g, d, m1, v1 = (g,) + tuple(_adamw(wmv[0], g, wmv[1], wmv[2], "adamw_" + n))
            else:
                g, d, m1, v1 = _adamw(wmv[0], r, wmv[1], wmv[2], "adamw_" + n)
            if n in TRANSPOSED:
                g, d, m1, v1 = g.T, d.T, m1.T, v1.T
            grads[n], deltas[n], new_m[n], new_v[n] = g[None], d[None], m1[None], v1[None]
        return tuple(deltas[n] for n in names)

    core = lax.axis_index("c").astype(jnp.int32).reshape(1)
    reducer = _Reducer(core, apply)
    grad_x, pending, small = _local_step(x, mem, loss_target, W, P, reducer)

    total = _small_allreduce(_pack([small[n] for n in SMALL] + [small["loss"]]))
    landed, (total,) = reducer.finish("in", pending["w_in"], (total,))
    apply(BIG[:1], landed)

    slot = [jnp.zeros((1, 1), F32)]
    shapes = [Wt[n].shape for n in SMALL] + [(1, 1)]
    d, m1, v1 = _adamw(_pack([Wt[n] for n in SMALL] + slot), total, _pack([Mo[n] for n in SMALL] + slot),
                       _pack([Vo[n] for n in SMALL] + slot), "adamw_small")
    g_s, d_s, m_s, v_s = _unpack(total, shapes), _unpack(d, shapes), _unpack(m1, shapes), _unpack(v1, shapes)
    for k, n in enumerate(SMALL):
        grads[n], deltas[n], new_m[n], new_v[n] = g_s[k], d_s[k], m_s[k], v_s[k]
    loss = g_s[-1][0, 0]

    return (loss, grad_x, *[grads[n] for n in WEIGHTS], *[deltas[n] for n in WEIGHTS],
            *[new_m[n] for n in WEIGHTS], *[new_v[n] for n in WEIGHTS])
```

```python
import functools

import jax
import jax.numpy as jnp
from jax import lax
from jax.experimental import pallas as pl
from jax.experimental.pallas import tpu as pltpu
from jax.experimental.pallas import tpu_sc as plsc

F32 = jnp.float32
BF16 = jnp.bfloat16
EPS = 1e-6
NEG = -1e30
HEAD = 64
A_W, B_W, M_W = 384, 384, 256
N_FOX_HEADS = 6
CHUNK = 128
IN_COLS = 2 * A_W + 3 * B_W + N_FOX_HEADS + M_W
P_MAIN = 2 * A_W + 3 * B_W + M_W
P_COLS = P_MAIN + 128
F_END = 2 * A_W + 3 * B_W + N_FOX_HEADS
LANES = 128
Q_BLK, K_BLK = 256, 128
ROW_SPLIT = 4
ADAM_LR, ADAM_B1, ADAM_B2, ADAM_EPS, ADAM_WD, ADAM_STEP = 0.001, 0.9, 0.999, 1e-08, 0.01, 10
VMEM_LIMIT = 56 * 1024 * 1024
MESH = pl.DeviceIdType.MESH
ANY = pl.BlockSpec(memory_space=pl.ANY)
BS = pl.BlockSpec


def _cp(sem=None):
    return pltpu.CompilerParams(dimension_semantics=sem, vmem_limit_bytes=VMEM_LIMIT)


def _iota(shape, dim):
    return lax.broadcasted_iota(jnp.int32, shape, dim)


def _dot(a, b):
    return jnp.dot(a.astype(BF16), b.astype(BF16), preferred_element_type=F32)


def _dot_nt(a, b):
    return lax.dot_general(a.astype(BF16), b.astype(BF16), (((1,), (1,)), ((), ())), preferred_element_type=F32)


def _dot_tn(a, b):
    return lax.dot_general(a.astype(BF16), b.astype(BF16), (((0,), (0,)), ((), ())), preferred_element_type=F32)


def _rms(x, g):
    return x * lax.rsqrt(jnp.mean(x * x, axis=-1, keepdims=True) + EPS) * g


def _rms_bwd(x, g, dy):
    r = lax.rsqrt(jnp.mean(x * x, axis=-1, keepdims=True) + EPS)
    xr = x * r
    gd = dy * g
    m = jnp.mean(gd * xr, axis=-1, keepdims=True)
    return (gd - xr * m) * r, _colsum(dy * xr)


def _gelu(x):
    return 0.5 * x * (1.0 + jnp.tanh(0.7978845608028654 * (x + 0.044715 * (x * x * x))))


def _sigmoid(x):
    return 1.0 / (1.0 + jnp.exp(-x))


def _silu_mul(g, u):
    return g * _sigmoid(g) * u


def _logsig(x):
    return jnp.minimum(x, 0.0) - jnp.log(1.0 + jnp.exp(-jnp.abs(x)))


def _colsum(x):
    return jnp.sum(x, axis=0, keepdims=True)


def _acc(ref, val, first):
    @pl.when(first)
    def _():
        ref[...] = val

    @pl.when(jnp.logical_not(first))
    def _():
        ref[...] += val


def _inproj_fwd(x2d, g_pre, w_in_p, tm):
    T, D = x2d.shape
    CH = 768
    nchunk = P_COLS // CH
    ns, _, dsh = w_in_p.shape

    def body(x_ref, g_ref, w_ref, h_ref, proj_ref, fl_ref):
        h = _rms(x_ref[...], g_ref[...]).astype(BF16)
        h_ref[...] = h
        for n in range(nchunk):
            rows = slice(n * CH, (n + 1) * CH)
            r = _dot_nt(h[:, 0:dsh], w_ref[0, rows, :])
            for s in range(1, ns):
                r = r + _dot_nt(h[:, s * dsh:(s + 1) * dsh], w_ref[s, rows, :])
            if n < nchunk - 1:
                proj_ref[:, rows] = r.astype(BF16)
            else:
                fg = 1920 - n * CH
                proj_ref[:, n * CH:1920] = r[:, :fg].astype(BF16)
                fl_ref[...] = r[:, fg:fg + LANES]
                proj_ref[:, 1920:P_MAIN] = r[:, fg + LANES:].astype(BF16)

    return pl.pallas_call(
        body, name="inproj_fwd", grid=(T // tm,),
        in_specs=[BS((tm, D), lambda i: (i, 0)), BS((1, D), lambda i: (0, 0)),
                  BS((ns, P_COLS, dsh), lambda i: (0, 0, 0))],
        out_specs=[BS((tm, D), lambda i: (i, 0)), BS((tm, P_MAIN), lambda i: (i, 0)), BS((tm, LANES), lambda i: (i, 0))],
        out_shape=[jax.ShapeDtypeStruct((T, D), BF16), jax.ShapeDtypeStruct((T, P_MAIN), BF16),
                   jax.ShapeDtypeStruct((T, LANES), F32)],
        compiler_params=_cp(("arbitrary",)),
    )(x2d, g_pre, w_in_p)


def _gate_fwd(flog3, bf_row):
    Bl, S, _ = flog3.shape
    nb = S // LANES

    def body(f_ref, b_ref, bq_ref, bk_ref, fr_ref):
        row = _iota((LANES, LANES), 0)
        lane = _iota((LANES, LANES), 1)
        one = jnp.ones((LANES, LANES), BF16)
        zero = jnp.zeros((LANES, LANES), BF16)

        carry = jnp.zeros((1, LANES), F32)
        for j in range(nb):
            r0 = j * LANES
            fl = f_ref[0, pl.ds(r0, LANES), :] + b_ref[...]
            fr_ref[0, j] = fl.T[0:8, :]
            c = _logsig(fl)
            for k in (1, 2, 4, 8, 16, 32, 64):
                c = c + jnp.where(row >= k, pltpu.roll(c, k, 0), 0.0)
            total = _colsum(jnp.where(row == LANES - 1, c, 0.0))
            c = c + carry
            carry = carry + total
            for h in range(N_FOX_HEADS):
                col = jnp.sum(jnp.where(lane == h, c, 0.0), axis=1, keepdims=True)
                hi = col.astype(BF16)
                rest = col - hi.astype(F32)
                mid = rest.astype(BF16)
                lo = (rest - mid.astype(F32)).astype(BF16)
                base = _bias_lane(h)
                bq = jnp.where(lane == base, hi, jnp.where(lane == base + 1, mid, jnp.where(lane == base + 2, lo, zero)))
                bq = jnp.where((lane >= base + 3) & (lane < base + 6), one, bq)
                bk = jnp.where(lane == base + 3, -hi, jnp.where(lane == base + 4, -mid, jnp.where(lane == base + 5, -lo, zero)))
                bk = jnp.where((lane >= base) & (lane < base + 3), one, bk)
                bq_ref[0, h, pl.ds(r0, LANES), :] = bq
                bk_ref[0, h, pl.ds(r0, LANES), :] = bk

    slab = BS((1, N_FOX_HEADS, S, LANES), lambda b: (b, 0, 0, 0))
    return pl.pallas_call(
        body, name="gate_fwd", grid=(Bl,),
        in_specs=[BS((1, S, LANES), lambda b: (b, 0, 0)), BS((1, LANES), lambda b: (0, 0))],
        out_specs=[slab, slab, BS((1, nb, 8, LANES), lambda b: (b, 0, 0, 0))],
        out_shape=[jax.ShapeDtypeStruct((Bl, N_FOX_HEADS, S, LANES), BF16),
                   jax.ShapeDtypeStruct((Bl, N_FOX_HEADS, S, LANES), BF16),
                   jax.ShapeDtypeStruct((Bl, nb, 8, LANES), F32)],
        compiler_params=_cp(("arbitrary",)),
    )(flog3, bf_row)


def _bias_lane(h):
    return HEAD if h % 2 == 0 else 0


def _sgu_pre(zu, zv, g_sgu):
    return _gelu(zu), _rms(_gelu(zv), g_sgu)


def _sgu_fwd(proj, g_sgu, ws_tril, bs_full, tm):
    T = proj.shape[0]
    nch = tm // CHUNK

    def body(zu_ref, zv_ref, g_ref, ws_ref, b_ref, ya_ref):
        lane = _iota((CHUNK, LANES), 1)
        u, vn = _sgu_pre(zu_ref[...].astype(F32), zv_ref[...].astype(F32), g_ref[...])
        vn = vn.astype(BF16)
        for c in range(nch):
            rs = slice(c * CHUNK, (c + 1) * CHUNK)
            for j in range(3):
                cs = slice(j * LANES, (j + 1) * LANES)
                vp = vn[rs, cs]
                z = jnp.where(lane < HEAD, _dot(ws_ref[2 * j], vp), _dot(ws_ref[2 * j + 1], vp)) + b_ref[:, cs]
                ya_ref[rs, cs] = (u[rs, cs] * z).astype(BF16)

    return pl.pallas_call(
        body, name="sgu_fwd", grid=(T // tm,),
        in_specs=[BS((tm, A_W), lambda i: (i, 0)), BS((tm, A_W), lambda i: (i, 1)), BS((1, A_W), lambda i: (0, 0)),
                  BS((6, CHUNK, CHUNK), lambda i: (0, 0, 0)), BS((CHUNK, A_W), lambda i: (0, 0))],
        out_specs=BS((tm, A_W), lambda i: (i, 0)),
        out_shape=jax.ShapeDtypeStruct((T, A_W), BF16),
        compiler_params=_cp(("arbitrary",)),
    )(proj, proj, g_sgu, ws_tril, bs_full)


def _fox_fwd(proj, bq, bk, Bl, S):
    T = Bl * S
    nq = S // Q_BLK
    qc, kc, vc = 768 // LANES, 1152 // LANES, 1536 // LANES

    def body(q_ref, k_ref, v_ref, bq_ref, bk_ref, o_ref, lse_ref, ka_ref, va_ref):
        lane_s = _iota((S, LANES), 1)
        lane = _iota((Q_BLK, LANES), 1)
        tri = _iota((Q_BLK, Q_BLK), 1) <= _iota((Q_BLK, Q_BLK), 0)
        k = k_ref[...]
        v = v_ref[...]
        for hh in range(2):
            data = (lane_s < HEAD) if hh == 0 else (lane_s >= HEAD)
            ka_ref[hh] = jnp.where(data, k, bk_ref[0, hh])
            va_ref[hh] = jnp.where(lane_s == _bias_lane(hh), jnp.ones_like(v), v)
        for i in range(nq):
            r0 = i * Q_BLK
            q = q_ref[r0:r0 + Q_BLK, :]
            o_out = jnp.zeros((Q_BLK, LANES), F32)
            lse_out = jnp.zeros((Q_BLK, LANES), F32)
            for hh in range(2):
                hmask = (lane < HEAD) if hh == 0 else (lane >= HEAD)
                qa = jnp.where(hmask, q * 0.125, bq_ref[0, hh, r0:r0 + Q_BLK, :])
                sd = jnp.where(tri, _dot_nt(qa, ka_ref[hh, r0:r0 + Q_BLK, :]), NEG)
                m = jnp.max(sd, axis=1, keepdims=True)
                if i:
                    sf = _dot_nt(qa, ka_ref[hh, 0:r0, :])
                    m = jnp.maximum(m, jnp.max(sf, axis=1, keepdims=True))
                acc = _dot(jnp.exp(sd - m), va_ref[hh, r0:r0 + Q_BLK, :])
                if i:
                    acc = acc + _dot(jnp.exp(sf - m), va_ref[hh, 0:r0, :])
                l = jnp.sum(jnp.where(lane == _bias_lane(hh), acc, 0.0), axis=1, keepdims=True)
                o_out = jnp.where(hmask, acc / l, o_out)
                lse_out = jnp.where(hmask, m + jnp.log(l), lse_out)
            o_ref[r0:r0 + Q_BLK, :] = o_out.astype(BF16)
            lse_ref[0, r0:r0 + Q_BLK, :] = lse_out

    seq = lambda c0: BS((S, LANES), lambda b, p: (b, c0 + p))
    pair = BS((1, 2, S, LANES), lambda b, p: (b, p, 0, 0))
    return pl.pallas_call(
        body, name="fox_fwd", grid=(Bl, 3),
        in_specs=[seq(qc), seq(kc), seq(vc), pair, pair],
        out_specs=[seq(0), BS((1, S, LANES), lambda b, p: (p, b, 0))],
        out_shape=[jax.ShapeDtypeStruct((T, B_W), BF16), jax.ShapeDtypeStruct((3, T, LANES), F32)],
        scratch_shapes=[pltpu.VMEM((2, S, LANES), BF16), pltpu.VMEM((2, S, LANES), BF16)],
        compiler_params=_cp(("arbitrary", "arbitrary")),
    )(proj, proj, proj, bq, bk)


def _memkv_fwd(mem, g_mem, w_kv):
    Bl, Mt, D = mem.shape

    def body(m_ref, g_ref, w_ref, mn_ref, kv_ref):
        mn = _rms(m_ref[0], g_ref[...]).astype(BF16)
        mn_ref[0] = mn
        kv_ref[0] = jnp.dot(mn, w_ref[...], preferred_element_type=F32).astype(BF16)

    return pl.pallas_call(
        body, name="memkv_fwd", grid=(Bl,),
        in_specs=[BS((1, Mt, D), lambda b: (b, 0, 0)), BS((1, D), lambda b: (0, 0)), BS((D, 2 * M_W), lambda b: (0, 0))],
        out_specs=[BS((1, Mt, D), lambda b: (b, 0, 0)), BS((1, Mt, 2 * M_W), lambda b: (b, 0, 0))],
        out_shape=[jax.ShapeDtypeStruct((Bl, Mt, D), BF16), jax.ShapeDtypeStruct((Bl, Mt, 2 * M_W), BF16)],
        compiler_params=_cp(("arbitrary",)),
    )(mem, g_mem, w_kv)


def _memattn_fwd(proj, kv, Bl, S, tq):
    T = Bl * S
    nq = S // tq
    Mt = kv.shape[1]
    qc = 1920 // LANES

    def body(q_ref, km_ref, vm_ref, o_ref):
        lane = _iota((tq, LANES), 1)
        q = q_ref[...]
        out = jnp.zeros((tq, LANES), F32)
        for hh in range(2):
            hmask = (lane < HEAD) if hh == 0 else (lane >= HEAD)
            qs = jnp.where(hmask, q, jnp.zeros_like(q)) * 0.125
            s = _dot_nt(qs, km_ref[0])
            pe = jnp.exp(s - jnp.max(s, axis=1, keepdims=True))
            pn = pe / jnp.sum(pe, axis=1, keepdims=True)
            out = jnp.where(hmask, _dot(pn, vm_ref[0]), out)
        o_ref[...] = out.astype(BF16)

    return pl.pallas_call(
        body, name="memattn_fwd", grid=(Bl, 2, nq),
        in_specs=[BS((tq, LANES), lambda b, p, i: (b * nq + i, qc + p)),
                  BS((1, Mt, LANES), lambda b, p, i: (b, 0, p)),
                  BS((1, Mt, LANES), lambda b, p, i: (b, 0, 2 + p))],
        out_specs=BS((tq, LANES), lambda b, p, i: (b * nq + i, p)),
        out_shape=jax.ShapeDtypeStruct((T, M_W), BF16),
        compiler_params=_cp(("arbitrary", "arbitrary", "arbitrary")),
    )(proj, kv, kv)


def _mix_norms(ya, yb, ym, ga, gb, gm):
    return _rms(ya, ga), _rms(yb, gb), _rms(ym, gm)


def _outproj_fwd(ya, yb, ym, x2d, ga, gb, gm, g_post, g_pre2, w_out, tm):
    T, D = x2d.shape

    def body(ya_ref, yb_ref, ym_ref, x_ref, ga_ref, gb_ref, gm_ref, gp_ref, g2_ref, w_ref,
             y_ref, o_ref, x1_ref, h2_ref):
        na, nb_, nm = _mix_norms(ya_ref[...].astype(F32), yb_ref[...].astype(F32), ym_ref[...].astype(F32),
                                 ga_ref[...], gb_ref[...], gm_ref[...])
        y_ref[:, 0:A_W] = na.astype(BF16)
        y_ref[:, A_W:A_W + B_W] = nb_.astype(BF16)
        y_ref[:, A_W + B_W:] = nm.astype(BF16)
        o = jnp.dot(y_ref[...], w_ref[...], preferred_element_type=F32)
        o_ref[...] = o
        x1 = x_ref[...] + _rms(o, gp_ref[...])
        x1_ref[...] = x1
        h2_ref[...] = _rms(x1, g2_ref[...]).astype(BF16)

    row = lambda w: BS((tm, w), lambda i: (i, 0))
    vec = lambda w: BS((1, w), lambda i: (0, 0))
    return pl.pallas_call(
        body, name="outproj_fwd", grid=(T // tm,),
        in_specs=[row(A_W), row(B_W), row(M_W), row(D), vec(A_W), vec(B_W), vec(M_W), vec(D), vec(D),
                  BS((A_W + B_W + M_W, D), lambda i: (0, 0))],
        out_specs=[row(A_W + B_W + M_W), row(D), row(D), row(D)],
        out_shape=[jax.ShapeDtypeStruct((T, A_W + B_W + M_W), BF16), jax.ShapeDtypeStruct((T, D), F32),
                   jax.ShapeDtypeStruct((T, D), F32), jax.ShapeDtypeStruct((T, D), BF16)],
        compiler_params=_cp(("arbitrary",)),
    )(ya, yb, ym, x2d, ga, gb, gm, g_post, g_pre2, w_out)


def _ffn_fwd(h2, x1, target, wg, wu, wd, g_post, tm):
    T, D = x1.shape
    ns, F, _ = wg.shape

    def body(h_ref, x1_ref, t_ref, wg_ref, wu_ref, wd_ref, gp_ref,
             gs_ref, us_ref, dff_ref, dx2_ref, dgp_ref, loss_ref, acc_ref):
        i = pl.program_id(0)
        j = pl.program_id(1)
        h = h_ref[...]
        g = _dot_nt(h, wg_ref[0])
        u = _dot_nt(h, wu_ref[0])
        gs_ref[0] = g.astype(BF16)
        us_ref[0] = u.astype(BF16)
        part = _dot(_silu_mul(g, u), wd_ref[0])
        _acc(acc_ref, part, j == 0)

        @pl.when(j == ns - 1)
        def _():
            ff = acc_ref[...]
            diff = x1_ref[...] + _rms(ff, gp_ref[...]) - t_ref[...]
            dx2 = diff * (1.0 / D)
            dff, dgp = _rms_bwd(ff, gp_ref[...], dx2)
            dx2_ref[...] = dx2
            dff_ref[...] = dff.astype(BF16)
            lpart = jnp.sum(_colsum(diff * diff), axis=1, keepdims=True) * (0.5 / D)
            _acc(dgp_ref, dgp, i == 0)
            _acc(loss_ref, jnp.broadcast_to(lpart, (1, LANES)), i == 0)

    row = lambda w: BS((tm, w), lambda i, j: (i, 0))
    return pl.pallas_call(
        body, name="ffn_fwd", grid=(T // tm, ns),
        in_specs=[row(D), row(D), row(D), BS((1, F, D), lambda i, j: (j, 0, 0)), BS((1, F, D), lambda i, j: (j, 0, 0)),
                  BS((1, F, D), lambda i, j: (j, 0, 0)), BS((1, D), lambda i, j: (0, 0))],
        out_specs=[BS((1, tm, F), lambda i, j: (j, i, 0)), BS((1, tm, F), lambda i, j: (j, i, 0)), row(D), row(D),
                   BS((1, D), lambda i, j: (0, 0)), BS((1, LANES), lambda i, j: (0, 0))],
        out_shape=[jax.ShapeDtypeStruct((ns, T, F), BF16), jax.ShapeDtypeStruct((ns, T, F), BF16),
                   jax.ShapeDtypeStruct((T, D), BF16), jax.ShapeDtypeStruct((T, D), F32),
                   jax.ShapeDtypeStruct((1, D), F32), jax.ShapeDtypeStruct((1, LANES), F32)],
        scratch_shapes=[pltpu.VMEM((tm, D), F32)],
        compiler_params=_cp(("arbitrary", "arbitrary")),
    )(h2, x1, target, wg, wu, wd, g_post)


def _ffn_bwd(dff, h2, gs, us, wg, wu, wd, tm):
    T, D = h2.shape
    ns, F, _ = wg.shape

    def body(dff_ref, h_ref, gs_ref, us_ref, wg_ref, wu_ref, wd_ref, dh_ref, dwg_ref, dwu_ref, dwd_ref):
        first = pl.program_id(1) == 0
        dff = dff_ref[...]
        h = h_ref[...]
        parts = []
        for r in range(ROW_SPLIT):
            rows = slice(r * (tm // ROW_SPLIT), (r + 1) * (tm // ROW_SPLIT))
            dact = _dot_nt(dff[rows], wd_ref[0])
            g = gs_ref[0, rows, :].astype(F32)
            u = us_ref[0, rows, :].astype(F32)
            sig = _sigmoid(g)
            gsig = g * sig
            dg = (dact * u * (sig + gsig * (1.0 - sig))).astype(BF16)
            du = (dact * gsig).astype(BF16)
            dh_ref[0, rows, :] = (_dot(dg, wg_ref[0]) + _dot(du, wu_ref[0])).astype(BF16)
            parts.append(((gsig * u).astype(BF16), dg, du))
        a, dg, du = [jnp.concatenate(p, axis=0) for p in zip(*parts)]
        _acc(dwd_ref, _dot_tn(a, dff)[None], first)
        _acc(dwg_ref, _dot_tn(dg, h)[None], first)
        _acc(dwu_ref, _dot_tn(du, h)[None], first)

    row = BS((tm, D), lambda j, i: (i, 0))
    sh = BS((1, tm, F), lambda j, i: (j, i, 0))
    wsh = BS((1, F, D), lambda j, i: (j, 0, 0))
    return pl.pallas_call(
        body, name="ffn_bwd", grid=(ns, T // tm),
        in_specs=[row, row, sh, sh, wsh, wsh, wsh],
        out_specs=[BS((1, tm, D), lambda j, i: (j, i, 0)), wsh, wsh, wsh],
        out_shape=[jax.ShapeDtypeStruct((ns, T, D), BF16)] + [jax.ShapeDtypeStruct((ns, F, D), F32)] * 3,
        compiler_params=_cp(("arbitrary", "arbitrary")),
    )(dff, h2, gs, us, wg, wu, wd)


def _mm_tn(a, b, name, tk):
    T, M = a.shape
    N = b.shape[1]
    tk = min(tk, T)

    def body(a_ref, b_ref, o_ref):
        _acc(o_ref, _dot_tn(a_ref[...], b_ref[...]), pl.program_id(0) == 0)

    return pl.pallas_call(
        body, name=name, grid=(T // tk,),
        in_specs=[BS((tk, M), lambda t: (t, 0)), BS((tk, N), lambda t: (t, 0))],
        out_specs=BS((M, N), lambda t: (0, 0)),
        out_shape=jax.ShapeDtypeStruct((M, N), F32),
        compiler_params=_cp(("arbitrary",)),
    )(a, b)


def _dw_in(dproj, h, ns, tk):
    T, M = dproj.shape
    D = h.shape[1]
    dsh = D // ns
    tk = min(tk, T)

    def body(a_ref, b_ref, o_ref, acc_ref):
        t = pl.program_id(0)
        _acc(acc_ref, _dot_tn(b_ref[...], a_ref[...]), t == 0)

        @pl.when(t == pl.num_programs(0) - 1)
        def _():
            for s in range(ns):
                o_ref[s] = acc_ref[s * dsh:(s + 1) * dsh, :].T

    return pl.pallas_call(
        body, name="dw_in", grid=(T // tk,),
        in_specs=[BS((tk, M), lambda t: (t, 0)), BS((tk, D), lambda t: (t, 0))],
        out_specs=BS((ns, M, dsh), lambda t: (0, 0, 0)),
        out_shape=jax.ShapeDtypeStruct((ns, M, dsh), F32),
        scratch_shapes=[pltpu.VMEM((D, M), F32)],
        compiler_params=_cp(("arbitrary",)),
    )(dproj, h)


def _outproj_bwd(dh2, x1, dx2, o, ya, yb, ym, ga, gb, gm, g_post, g_pre2, w_out, tm):
    T, D = x1.shape
    ns = dh2.shape[0]

    def body(dh_ref, x1_ref, dx2_ref, o_ref, ya_ref, yb_ref, ym_ref, ga_ref, gb_ref, gm_ref, gp_ref, g2_ref, w_ref,
             dx1_ref, do_ref, dya_ref, dyb_ref, dym_ref, dga_ref, dgb_ref, dgm_ref, dgp_ref, dg2_ref):
        first = pl.program_id(0) == 0
        dh = dh_ref[0].astype(F32)
        for j in range(1, ns):
            dh = dh + dh_ref[j].astype(F32)
        dxa, dg2 = _rms_bwd(x1_ref[...], g2_ref[...], dh)
        dx1 = dx2_ref[...] + dxa
        dx1_ref[...] = dx1
        _acc(dg2_ref, dg2, first)
        do, dgp = _rms_bwd(o_ref[...], gp_ref[...], dx1)
        do = do.astype(BF16)
        do_ref[...] = do
        dy = _dot_nt(do, w_ref[...])
        dya, dga = _rms_bwd(ya_ref[...].astype(F32), ga_ref[...], dy[:, 0:A_W])
        dyb, dgb = _rms_bwd(yb_ref[...].astype(F32), gb_ref[...], dy[:, A_W:A_W + B_W])
        dym, dgm = _rms_bwd(ym_ref[...].astype(F32), gm_ref[...], dy[:, A_W + B_W:])
        dya_ref[...] = dya.astype(BF16)
        dyb_ref[...] = dyb.astype(BF16)
        dym_ref[...] = dym.astype(BF16)
        _acc(dga_ref, dga, first)
        _acc(dgb_ref, dgb, first)
        _acc(dgm_ref, dgm, first)
        _acc(dgp_ref, dgp, first)

    row = lambda w: BS((tm, w), lambda i: (i, 0))
    vec = lambda w: BS((1, w), lambda i: (0, 0))
    sds = jax.ShapeDtypeStruct
    return pl.pallas_call(
        body, name="outproj_bwd", grid=(T // tm,),
        in_specs=[BS((ns, tm, D), lambda i: (0, i, 0)), row(D), row(D), row(D), row(A_W), row(B_W), row(M_W),
                  vec(A_W), vec(B_W), vec(M_W), vec(D), vec(D), BS((A_W + B_W + M_W, D), lambda i: (0, 0))],
        out_specs=[row(D), row(D), row(A_W), row(B_W), row(M_W), vec(A_W), vec(B_W), vec(M_W), vec(D), vec(D)],
        out_shape=[sds((T, D), F32), sds((T, D), BF16), sds((T, A_W), BF16), sds((T, B_W), BF16), sds((T, M_W), BF16),
                   sds((1, A_W), F32), sds((1, B_W), F32), sds((1, M_W), F32), sds((1, D), F32), sds((1, D), F32)],
        compiler_params=_cp(("arbitrary",)),
    )(dh2, x1, dx2, o, ya, yb, ym, ga, gb, gm, g_post, g_pre2, w_out)


def _sgu_bwd(proj, dya, g_sgu, ws_tril, bs_full, tm):
    T = proj.shape[0]
    nch = tm // CHUNK

    def body(zu_ref, zv_ref, dy_ref, g_ref, ws_ref, b_ref, dzu_ref, dzv_ref, dws_ref, dbs_ref, dg_ref,
             du_ref, dvn_ref, dbf_ref):
        step = pl.program_id(0)
        first = step == 0
        lane = _iota((CHUNK, LANES), 1)
        tril = _iota((CHUNK, CHUNK), 0) >= _iota((CHUNK, CHUNK), 1)
        (u, vn), vjp = jax.vjp(_sgu_pre, zu_ref[...].astype(F32), zv_ref[...].astype(F32), g_ref[...])
        vnb = vn.astype(BF16)
        dy = dy_ref[...].astype(F32)

        @pl.when(first)
        def _():
            dws_ref[...] = jnp.zeros_like(dws_ref)
            dbf_ref[...] = jnp.zeros_like(dbf_ref)

        for c in range(nch):
            rs = slice(c * CHUNK, (c + 1) * CHUNK)
            for j in range(3):
                cs = slice(j * LANES, (j + 1) * LANES)
                vp = vnb[rs, cs]
                z = jnp.where(lane < HEAD, _dot(ws_ref[2 * j], vp), _dot(ws_ref[2 * j + 1], vp)) + b_ref[:, cs]
                du_ref[rs, cs] = dy[rs, cs] * z
                dz = dy[rs, cs] * u[rs, cs]
                dbf_ref[:, cs] += dz
                dzb = dz.astype(BF16)
                dz0 = jnp.where(lane < HEAD, dzb, jnp.zeros_like(dzb))
                dz1 = jnp.where(lane >= HEAD, dzb, jnp.zeros_like(dzb))
                dvn_ref[rs, cs] = jnp.where(lane < HEAD, _dot_tn(ws_ref[2 * j], dzb), _dot_tn(ws_ref[2 * j + 1], dzb))
                dws_ref[2 * j] += jnp.where(tril, _dot_nt(dz0, vp), 0.0)
                dws_ref[2 * j + 1] += jnp.where(tril, _dot_nt(dz1, vp), 0.0)
        dzu, dzv, dg = vjp((du_ref[...], dvn_ref[...]))
        dzu_ref[...] = dzu.astype(BF16)
        dzv_ref[...] = dzv.astype(BF16)
        _acc(dg_ref, dg, first)

        @pl.when(step == pl.num_programs(0) - 1)
        def _():
            out = jnp.zeros((CHUNK, LANES), F32)
            for j in range(3):
                slab = dbf_ref[:, j * LANES:(j + 1) * LANES]
                lo = jnp.sum(jnp.where(lane < HEAD, slab, 0.0), axis=1, keepdims=True)
                hi = jnp.sum(jnp.where(lane >= HEAD, slab, 0.0), axis=1, keepdims=True)
                out = out + jnp.where(lane == 2 * j, lo, 0.0) + jnp.where(lane == 2 * j + 1, hi, 0.0)
            dbs_ref[...] = out

    return pl.pallas_call(
        body, name="sgu_bwd", grid=(T // tm,),
        in_specs=[BS((tm, A_W), lambda i: (i, 0)), BS((tm, A_W), lambda i: (i, 1)), BS((tm, A_W), lambda i: (i, 0)),
                  BS((1, A_W), lambda i: (0, 0)), BS((6, CHUNK, CHUNK), lambda i: (0, 0, 0)),
                  BS((CHUNK, A_W), lambda i: (0, 0))],
        out_specs=[BS((tm, A_W), lambda i: (i, 0)), BS((tm, A_W), lambda i: (i, 0)),
                   BS((6, CHUNK, CHUNK), lambda i: (0, 0, 0)), BS((CHUNK, LANES), lambda i: (0, 0)),
                   BS((1, A_W), lambda i: (0, 0))],
        out_shape=[jax.ShapeDtypeStruct((T, A_W), BF16), jax.ShapeDtypeStruct((T, A_W), BF16),
                   jax.ShapeDtypeStruct((6, CHUNK, CHUNK), F32), jax.ShapeDtypeStruct((CHUNK, LANES), F32),
                   jax.ShapeDtypeStruct((1, A_W), F32)],
        scratch_shapes=[pltpu.VMEM((tm, A_W), F32), pltpu.VMEM((tm, A_W), F32), pltpu.VMEM((CHUNK, A_W), F32)],
        compiler_params=_cp(("arbitrary",)),
    )(proj, proj, dya, g_sgu, ws_tril, bs_full)


def _memattn_bwd(proj, kv, dym, Bl, S, tq):
    T = Bl * S
    nq = S // tq
    Mt = kv.shape[1]
    qc = 1920 // LANES

    def body(q_ref, km_ref, vm_ref, do_ref, dq_ref, dkm_ref, dvm_ref):
        first = pl.program_id(2) == 0
        lane = _iota((tq, LANES), 1)
        q = q_ref[...]
        do = do_ref[...]
        dq_out = jnp.zeros((tq, LANES), F32)
        dkm = jnp.zeros((Mt, LANES), F32)
        dvm = jnp.zeros((Mt, LANES), F32)
        for hh in range(2):
            hmask = (lane < HEAD) if hh == 0 else (lane >= HEAD)
            qs = jnp.where(hmask, q, jnp.zeros_like(q)) * 0.125
            dom = jnp.where(hmask, do, 0.0).astype(BF16)
            s = _dot_nt(qs, km_ref[0])
            pe = jnp.exp(s - jnp.max(s, axis=1, keepdims=True))
            pn = pe / jnp.sum(pe, axis=1, keepdims=True)
            dp = _dot_nt(dom, vm_ref[0])
            ds = (pn * (dp - jnp.sum(pn * dp, axis=1, keepdims=True))).astype(BF16)
            dq_out = jnp.where(hmask, _dot(ds, km_ref[0]) * 0.125, dq_out)
            dkm = dkm + _dot_tn(ds, qs)
            dvm = dvm + _dot_tn(pn, dom)
        dq_ref[...] = dq_out.astype(BF16)
        _acc(dkm_ref, dkm[None], first)
        _acc(dvm_ref, dvm[None], first)

    return pl.pallas_call(
        body, name="memattn_bwd", grid=(Bl, 2, nq),
        in_specs=[BS((tq, LANES), lambda b, p, i: (b * nq + i, qc + p)),
                  BS((1, Mt, LANES), lambda b, p, i: (b, 0, p)),
                  BS((1, Mt, LANES), lambda b, p, i: (b, 0, 2 + p)),
                  BS((tq, LANES), lambda b, p, i: (b * nq + i, p))],
        out_specs=[BS((tq, LANES), lambda b, p, i: (b * nq + i, p)),
                   BS((1, Mt, LANES), lambda b, p, i: (b, 0, p)),
                   BS((1, Mt, LANES), lambda b, p, i: (b, 0, p))],
        out_shape=[jax.ShapeDtypeStruct((T, M_W), BF16), jax.ShapeDtypeStruct((Bl, Mt, M_W), F32),
                   jax.ShapeDtypeStruct((Bl, Mt, M_W), F32)],
        compiler_params=_cp(("arbitrary", "arbitrary", "arbitrary")),
    )(proj, kv, kv, dym)


def _memkv_bwd(dkm, dvm, memn, mem, g_mem, w_kv):
    Bl, Mt, D = mem.shape

    def body(dk_ref, dv_ref, mn_ref, m_ref, g_ref, w_ref, dw_ref, dg_ref):
        first = pl.program_id(0) == 0
        dk = dk_ref[0].astype(BF16)
        dv = dv_ref[0].astype(BF16)
        mn = mn_ref[0]
        dmn = _dot_nt(dk, w_ref[:, 0:M_W]) + _dot_nt(dv, w_ref[:, M_W:])
        _, dg = _rms_bwd(m_ref[0], g_ref[...], dmn)
        _acc(dg_ref, dg, first)

        @pl.when(first)
        def _():
            dw_ref[...] = jnp.zeros_like(dw_ref)

        dw_ref[:, 0:M_W] += _dot_tn(mn, dk)
        dw_ref[:, M_W:] += _dot_tn(mn, dv)

    return pl.pallas_call(
        body, name="memkv_bwd", grid=(Bl,),
        in_specs=[BS((1, Mt, M_W), lambda b: (b, 0, 0)), BS((1, Mt, M_W), lambda b: (b, 0, 0)),
                  BS((1, Mt, D), lambda b: (b, 0, 0)), BS((1, Mt, D), lambda b: (b, 0, 0)),
                  BS((1, D), lambda b: (0, 0)), BS((D, 2 * M_W), lambda b: (0, 0))],
        out_specs=[BS((D, 2 * M_W), lambda b: (0, 0)), BS((1, D), lambda b: (0, 0))],
        out_shape=[jax.ShapeDtypeStruct((D, 2 * M_W), F32), jax.ShapeDtypeStruct((1, D), F32)],
        compiler_params=_cp(("arbitrary",)),
    )(dkm, dvm, memn, mem, g_mem, w_kv)


def _fox_bwd(proj, dyb, lse, bq, bk, Bl, S):
    T = Bl * S
    nq = S // Q_BLK
    nb = S // LANES
    qc, kc, vc = 768 // LANES, 1152 // LANES, 1536 // LANES

    def body(q_ref, k_ref, v_ref, do_ref, lse_ref, bq_ref, bk_ref,
             dq_ref, dk_ref, dv_ref, dcr_ref, ka_ref, dka_ref, dva_ref):
        p = pl.program_id(1)
        lane_s = _iota((S, LANES), 1)
        lane = _iota((Q_BLK, LANES), 1)
        sub = _iota((8, LANES), 0)
        tri = _iota((Q_BLK, Q_BLK), 1) <= _iota((Q_BLK, Q_BLK), 0)
        k = k_ref[...]
        for hh in range(2):
            data = (lane_s < HEAD) if hh == 0 else (lane_s >= HEAD)
            ka_ref[hh] = jnp.where(data, k, bk_ref[0, hh])
        dka_ref[...] = jnp.zeros_like(dka_ref)
        dva_ref[...] = jnp.zeros_like(dva_ref)

        @pl.when(p == 0)
        def _():
            dcr_ref[...] = jnp.zeros_like(dcr_ref)

        def add_colsums(ds, first_blk, h):
            cs = _colsum(ds)
            for jb in range(ds.shape[1] // LANES):
                dcr_ref[0, first_blk + jb] += jnp.where(sub == h, cs[:, jb * LANES:(jb + 1) * LANES], 0.0)

        for i in range(nq):
            r0 = i * Q_BLK
            r1 = r0 + Q_BLK
            q = q_ref[r0:r1, :]
            do = do_ref[r0:r1, :]
            lse_b = lse_ref[0, r0:r1, :]
            dq_out = jnp.zeros((Q_BLK, LANES), F32)
            for hh in range(2):
                hmask = (lane < HEAD) if hh == 0 else (lane >= HEAD)
                h = 2 * p + hh
                qs = jnp.where(hmask, q * 0.125, jnp.zeros_like(q))
                qa = jnp.where(hmask, q * 0.125, bq_ref[0, hh, r0:r1, :])
                dob = jnp.where(hmask, do, 0.0).astype(BF16)
                lse_h = jnp.sum(jnp.where(lane == hh * HEAD, lse_b, 0.0), axis=1, keepdims=True)
                pd = jnp.where(tri, jnp.exp(_dot_nt(qa, ka_ref[hh, r0:r1, :]) - lse_h), 0.0)
                dpd = _dot_nt(dob, v_ref[r0:r1, :])
                delta = jnp.sum(pd * dpd, axis=1, keepdims=True)
                psum = jnp.sum(pd, axis=1, keepdims=True)
                if i:
                    pf = jnp.exp(_dot_nt(qa, ka_ref[hh, 0:r0, :]) - lse_h)
                    dpf = _dot_nt(dob, v_ref[0:r0, :])
                    delta = delta + jnp.sum(pf * dpf, axis=1, keepdims=True)
                    psum = psum + jnp.sum(pf, axis=1, keepdims=True)
                delta = delta / psum
                dsd = pd * (dpd - delta)
                add_colsums(dsd, r0 // LANES, h)
                dsd = dsd.astype(BF16)
                dq_h = _dot(dsd, k_ref[r0:r1, :])
                dka_ref[r0:r1, :] += _dot_tn(dsd, qs)
                dva_ref[r0:r1, :] += _dot_tn(pd, dob)
                if i:
                    dsf = pf * (dpf - delta)
                    add_colsums(dsf, 0, h)
                    dsf = dsf.astype(BF16)
                    dq_h = dq_h + _dot(dsf, k_ref[0:r0, :])
                    dka_ref[0:r0, :] += _dot_tn(dsf, qs)
                    dva_ref[0:r0, :] += _dot_tn(pf, dob)
                dq_out = jnp.where(hmask, dq_h * 0.125, dq_out)
            dq_ref[r0:r1, :] = dq_out.astype(BF16)
        dk_ref[...] = dka_ref[...].astype(BF16)
        dv_ref[...] = dva_ref[...].astype(BF16)

    seq = lambda c0: BS((S, LANES), lambda b, p: (b, c0 + p))
    pair = BS((1, 2, S, LANES), lambda b, p: (b, p, 0, 0))
    rowblk = BS((1, nb, 8, LANES), lambda b, p: (b, 0, 0, 0))
    return pl.pallas_call(
        body, name="fox_bwd", grid=(Bl, 3),
        in_specs=[seq(qc), seq(kc), seq(vc), seq(0), BS((1, S, LANES), lambda b, p: (p, b, 0)), pair, pair],
        out_specs=[seq(0), seq(0), seq(0), rowblk],
        out_shape=[jax.ShapeDtypeStruct((T, B_W), BF16)] * 3 + [jax.ShapeDtypeStruct((Bl, nb, 8, LANES), F32)],
        scratch_shapes=[pltpu.VMEM((2, S, LANES), BF16), pltpu.VMEM((S, LANES), F32), pltpu.VMEM((S, LANES), F32)],
        compiler_params=_cp(("arbitrary", "arbitrary")),
    )(proj, proj, proj, dyb, lse, bq, bk)


def _gate_bwd(dc_row, fl_row):
    Bl, nb, _, _ = dc_row.shape

    def body(dc_ref, fl_ref, o_ref):
        lane = _iota((8, LANES), 1)

        carry = jnp.zeros((8, 1), F32)
        for j in reversed(range(nb)):
            r = -dc_ref[0, j]
            for k in (1, 2, 4, 8, 16, 32, 64):
                r = r + jnp.where(lane < LANES - k, pltpu.roll(r, LANES - k, 1), 0.0)
            total = jnp.sum(jnp.where(lane == 0, r, 0.0), axis=1, keepdims=True)
            dfl = (r + carry) * _sigmoid(-fl_ref[0, j])
            carry = carry + total
            o_ref[0, j * LANES:(j + 1) * LANES, :] = jnp.concatenate(
                [dfl, jnp.zeros((LANES - 8, LANES), F32)], axis=0).T

    rowblk = BS((1, nb, 8, LANES), lambda b: (b, 0, 0, 0))
    return pl.pallas_call(
        body, name="gate_bwd", grid=(Bl,),
        in_specs=[rowblk, rowblk],
        out_specs=BS((1, nb * LANES, LANES), lambda b: (b, 0, 0)),
        out_shape=jax.ShapeDtypeStruct((Bl, nb * LANES, LANES), F32),
        compiler_params=_cp(("arbitrary",)),
    )(dc_row, fl_row)


def _inproj_bwd(dzu, dzv, dq, dk, dv, dqm, dfl, x2d, dx1, g_pre, w_in_p, tm):
    T, D = x2d.shape
    ns, _, dsh = w_in_p.shape

    def body(dzu_ref, dzv_ref, dq_ref, dk_ref, dv_ref, dqm_ref, dfl_ref, x_ref, dx1_ref, g_ref, w_ref,
             dp_ref, gx_ref, dg_ref, dbf_ref):
        first = pl.program_id(0) == 0
        dfl = dfl_ref[...]
        dp_ref[:, 0:384] = dzu_ref[...]
        dp_ref[:, 384:768] = dzv_ref[...]
        dp_ref[:, 768:1152] = dq_ref[...]
        dp_ref[:, 1152:1536] = dk_ref[...]
        dp_ref[:, 1536:1920] = dv_ref[...]
        dp_ref[:, 1920:2048] = dfl.astype(BF16)
        dp_ref[:, 2048:2304] = dqm_ref[...]
        dh = jnp.concatenate([_dot(dp_ref[...], w_ref[s]) for s in range(ns)], axis=1)
        dxa, dg = _rms_bwd(x_ref[...], g_ref[...], dh)
        gx_ref[...] = dx1_ref[...] + dxa
        _acc(dg_ref, dg, first)
        _acc(dbf_ref, _colsum(dfl), first)

    row = lambda w: BS((tm, w), lambda i: (i, 0))
    return pl.pallas_call(
        body, name="inproj_bwd", grid=(T // tm,),
        in_specs=[row(A_W), row(A_W), row(B_W), row(B_W), row(B_W), row(M_W), row(LANES), row(D), row(D),
                  BS((1, D), lambda i: (0, 0)), BS((ns, P_COLS, dsh), lambda i: (0, 0, 0))],
        out_specs=[row(P_COLS), row(D), BS((1, D), lambda i: (0, 0)), BS((1, LANES), lambda i: (0, 0))],
        out_shape=[jax.ShapeDtypeStruct((T, P_COLS), BF16), jax.ShapeDtypeStruct((T, D), F32),
                   jax.ShapeDtypeStruct((1, D), F32), jax.ShapeDtypeStruct((1, LANES), F32)],
        compiler_params=_cp(("arbitrary",)),
    )(dzu, dzv, dq, dk, dv, dqm, dfl, x2d, dx1, g_pre, w_in_p)


def _local_step(x, mem, target, W, P, reduce=None):
    Bl, S, D = x.shape
    T = Bl * S
    tm = min(512, T)
    x2d = x.reshape(T, D)
    t2d = target.reshape(T, D)
    vec = lambda a: a.reshape(1, -1)
    bf_row = jnp.pad(P["b_f"].reshape(1, -1), ((0, 0), (0, LANES - N_FOX_HEADS)))
    tril = jnp.tril(jnp.ones((CHUNK, CHUNK), bool))
    ws_tril = jnp.where(tril[None], P["w_s"][0], 0.0).astype(BF16)
    bs_full = jnp.repeat(P["b_s"][0].T, HEAD, axis=1)
    g_pre, g_sgu = vec(P["g_pre_mix"]), vec(P["g_sgu"])
    ga, gb, gm = vec(P["g_out_a"]), vec(P["g_out_b"]), vec(P["g_out_m"])
    g_mem, g_post, g_pre2, g_post2 = vec(P["g_mem"]), vec(P["g_post_mix"]), vec(P["g_pre_ffn"]), vec(P["g_post_ffn"])

    h, proj, flog = _inproj_fwd(x2d, g_pre, W["w_in"], tm)
    bq, bk, fl_row = _gate_fwd(flog.reshape(Bl, S, LANES), bf_row)
    ya = _sgu_fwd(proj, g_sgu, ws_tril, bs_full, tm)
    yb, lse = _fox_fwd(proj, bq, bk, Bl, S)
    memn, kv = _memkv_fwd(mem, g_mem, W["w_mem_kv"])
    ym = _memattn_fwd(proj, kv, Bl, S, min(512, S))
    y, o, x1, h2 = _outproj_fwd(ya, yb, ym, x2d, ga, gb, gm, g_post, g_pre2, W["w_out"], tm)
    gs, us, dff, dx2, dg_post2, loss = _ffn_fwd(h2, x1, t2d, W["w_gate"], W["w_up"], W["w_down"], g_post2, tm)

    dh2, d_w_gate, d_w_up, d_w_down = _ffn_bwd(dff, h2, gs, us, W["w_gate"], W["w_up"], W["w_down"], min(1024, T))
    ffn = [d_w_gate, d_w_up, d_w_down]
    if reduce is not None:
        pending, _ = reduce.begin("ffn", ffn)
    dx1, do, dya, dyb, dym, dga, dgb, dgm, dg_post, dg_pre2 = _outproj_bwd(
        dh2, x1, dx2, o, ya, yb, ym, ga, gb, gm, g_post, g_pre2, W["w_out"], tm)
    if reduce is not None:
        ffn, (do, dya, dyb, dym) = reduce.finish("ffn", pending, (do, dya, dyb, dym))
    d_w_out = _mm_tn(y, do, "dw_out", 1024)
    dzu, dzv, dws, dbs_cols, dg_sgu = _sgu_bwd(proj, dya, g_sgu, ws_tril, bs_full, tm)
    dqm, dkm, dvm = _memattn_bwd(proj, kv, dym, Bl, S, min(512, S))
    d_w_kv, dg_mem = _memkv_bwd(dkm, dvm, memn, mem, g_mem, W["w_mem_kv"])
    mid = [d_w_kv, d_w_out]
    dq, dk, dv, dc_row = _fox_bwd(proj, dyb, lse, bq, bk, Bl, S)
    if reduce is not None:
        done = reduce.apply(BIG[3:], ffn)
        pending, after = reduce.begin("mid", mid, (dc_row,) + done)
        dc_row = after[0]
    dfl = _gate_bwd(dc_row, fl_row).reshape(T, LANES)
    dproj, grad_x, dg_pre, dbf = _inproj_bwd(dzu, dzv, dq, dk, dv, dqm, dfl, x2d, dx1, g_pre, W["w_in"], tm)
    if reduce is not None:
        mid, (dproj,) = reduce.finish("mid", pending, (dproj,))
    d_w_in = _dw_in(dproj, h, W["w_in"].shape[0], 1024)
    if reduce is None:
        big = dict(zip(BIG, [d_w_in] + mid + ffn))
    else:
        done = reduce.apply(BIG[1:3], mid)
        big = {"w_in": reduce.begin("in", [d_w_in], done)[0]}
    small = {"g_pre_mix": dg_pre, "b_f": dbf[:, :N_FOX_HEADS], "g_sgu": dg_sgu, "w_s": dws, "b_s": dbs_cols[:, :N_FOX_HEADS].T,
             "g_out_a": dga, "g_out_b": dgb, "g_out_m": dgm, "g_mem": dg_mem, "g_post_mix": dg_post,
             "g_pre_ffn": dg_pre2, "g_post_ffn": dg_post2, "loss": loss[:, :1]}
    return grad_x.reshape(Bl, S, D), big, small


def _place():
    return lax.axis_index("x"), lax.axis_index("y"), lax.axis_index("c")


def _exchange_on_sequencer(srcs, own_full, name, collective_id):
    n = len(srcs)

    def body(*refs):
        src, dst = refs[:n], refs[n:2 * n]
        lsem, isend, irecv, dsend, drecv = refs[2 * n:]
        x, y, c = _place()
        oc = 1 - c
        s_me = 2 * x + y
        sib = (x, y, oc)
        chips = [(1 - x, y), (x, 1 - y), (1 - x, 1 - y)]
        barrier = pltpu.get_barrier_semaphore()
        for dev in [(cx, cy, c) for cx, cy in chips] + [sib]:
            pl.semaphore_signal(barrier, inc=1, device_id=dev, device_id_type=MESH)
        pl.semaphore_wait(barrier, 4)

        def remote(a, b, ssem, rsem, dev):
            return pltpu.make_async_remote_copy(src_ref=a, dst_ref=b, send_sem=ssem, recv_sem=rsem,
                                                device_id=dev, device_id_type=MESH)

        sends, local = [], []
        for w in range(n):
            for j, (cx, cy) in enumerate(chips):
                half = src[w].at[c] if own_full else src[w].at[2 * cx + cy]
                cp = remote(half, dst[w].at[s_me, c], isend.at[w, j], irecv.at[w, j], (cx, cy, c))
                cp.start()
                sends.append(cp)
            if own_full:
                cp = remote(src[w], dst[w].at[s_me], dsend.at[w, 3], drecv.at[w, 3], sib)
            else:
                cp = remote(src[w].at[s_me], dst[w].at[s_me, c], dsend.at[w, 3], drecv.at[w, 3], sib)
                loc = pltpu.make_async_copy(src[w].at[s_me], dst[w].at[s_me, c], lsem.at[w])
                loc.start()
                local.append(loc)
            cp.start()
            sends.append(cp)
        for w in range(n):
            for j, (cx, cy) in enumerate(chips):
                landed = dst[w].at[2 * cx + cy, c]
                remote(landed, landed, isend.at[w, j], irecv.at[w, j], (cx, cy, c)).wait_recv()
                cp = remote(landed, landed, dsend.at[w, j], drecv.at[w, j], sib)
                cp.start()
                sends.append(cp)
        for w in range(n):
            for j, (cx, cy) in enumerate(chips):
                landed = dst[w].at[2 * cx + cy, oc]
                remote(landed, landed, dsend.at[w, j], drecv.at[w, j], sib).wait_recv()
            landed = dst[w].at[s_me] if own_full else dst[w].at[s_me, oc]
            remote(landed, landed, dsend.at[w, 3], drecv.at[w, 3], sib).wait_recv()
        for cp in sends:
            cp.wait_send()
        for loc in local:
            loc.wait()

    return pl.kernel(
        body, out_type=[jax.ShapeDtypeStruct((4, 2) + s.shape[1:], s.dtype) for s in srcs],
        mesh=plsc.ScalarSubcoreMesh(axis_name="sequencer", num_cores=1), name=name,
        scratch_types=[pltpu.SemaphoreType.DMA((n,)), pltpu.SemaphoreType.DMA((n, 3)), pltpu.SemaphoreType.DMA((n, 3)),
                       pltpu.SemaphoreType.DMA((n, 4)), pltpu.SemaphoreType.DMA((n, 4))],
        compiler_params=pltpu.CompilerParams(collective_id=collective_id),
    )(*srcs)


def _sibling_swap(grads, name, collective_id):
    n = len(grads)

    def body(*refs):
        g, theirs = refs[:n], refs[n:2 * n]
        ssem, rsem = refs[2 * n:]
        x, y, c = _place()
        sib = (x, y, 1 - c)
        barrier = pltpu.get_barrier_semaphore()
        pl.semaphore_signal(barrier, inc=1, device_id=sib, device_id_type=MESH)
        pl.semaphore_wait(barrier, 1)
        cps = []
        for w in range(n):
            cp = pltpu.make_async_remote_copy(src_ref=g[w].at[:, 1 - c], dst_ref=theirs[w], send_sem=ssem.at[w],
                                              recv_sem=rsem.at[w], device_id=sib, device_id_type=MESH)
            cp.start()
            cps.append(cp)
        for cp in cps:
            cp.wait()

    return pl.kernel(
        body, out_type=[jax.ShapeDtypeStruct((4,) + g.shape[2:], g.dtype) for g in grads],
        mesh=plsc.ScalarSubcoreMesh(axis_name="sequencer", num_cores=1), name=name,
        scratch_types=[pltpu.SemaphoreType.DMA((n,)), pltpu.SemaphoreType.DMA((n,))],
        compiler_params=pltpu.CompilerParams(collective_id=collective_id),
    )(*grads)


def _add_pair(core, g, theirs, name):
    _, _, hr, C = g.shape

    def body(core_ref, g_ref, t_ref, o_ref):
        o_ref[0] = (g_ref[0, 0] + t_ref[0]).astype(BF16)

    blk = BS((1, hr, C), lambda s, core_ref: (s, 0, 0))
    return pl.pallas_call(
        body, name=name,
        grid_spec=pltpu.PrefetchScalarGridSpec(
            num_scalar_prefetch=1, grid=(4,),
            in_specs=[BS((1, 1, hr, C), lambda s, core_ref: (s, core_ref[0], 0, 0)), blk], out_specs=blk),
        out_shape=jax.ShapeDtypeStruct(theirs.shape, BF16), compiler_params=_cp(("arbitrary",)))(core, g, theirs)


def _sum_chips(r, name):
    _, _, hr, C = r.shape

    def body(r_ref, o_ref):
        o_ref[...] = ((r_ref[0, 0].astype(F32) + r_ref[1, 0].astype(F32)) + r_ref[2, 0].astype(F32)) + r_ref[3, 0].astype(F32)

    return pl.pallas_call(body, name=name, grid=(2,), in_specs=[BS((4, 1, hr, C), lambda h: (0, h, 0, 0))],
                          out_specs=BS((hr, C), lambda h: (h, 0)), out_shape=jax.ShapeDtypeStruct((2 * hr, C), F32),
                          compiler_params=_cp(("arbitrary",)))(r)


class _Reducer:
    IDS = {"ffn": (4, 5), "mid": (6, 7), "in": (8, 9)}

    def __init__(self, core, apply):
        self.core = core
        self.apply = apply

    def begin(self, tag, grads, after=()):
        grads, after = lax.optimization_barrier((list(grads), after))
        g4 = [g.reshape(4, 2, -1, g.shape[-1]) for g in grads]
        return (g4, _sibling_swap(g4, "swap_" + tag, self.IDS[tag][0])), after

    def finish(self, tag, pending, hold):
        g4, theirs = pending
        sums = [_add_pair(self.core, g, t, "chip_sum_%s_%d" % (tag, k)) for k, (g, t) in enumerate(zip(g4, theirs))]
        sums, hold = lax.optimization_barrier((sums, hold))
        return _exchange_on_sequencer(sums, False, "scatter_" + tag, self.IDS[tag][1]), hold


def _small_allreduce(part):
    R = part.shape[0]
    rs = R // 8
    masks = [(mx, my, mc) for mx in (0, 1) for my in (0, 1) for mc in (0, 1)][1:]

    def body(p_ref, o_ref, buf_ref, s1, r1, s2, r2):
        x, y, c = _place()
        d = 4 * x + 2 * y + c
        mine = pl.ds(pl.multiple_of(d * rs, 8), rs)
        peers = [((x + mx) % 2, (y + my) % 2, (c + mc) % 2) for mx, my, mc in masks]
        first, second = [], []
        for k, (px, py, pc) in enumerate(peers):
            theirs = pl.ds(pl.multiple_of((4 * px + 2 * py + pc) * rs, 8), rs)
            cp = pltpu.make_async_remote_copy(src_ref=p_ref.at[theirs, :], dst_ref=buf_ref.at[d], send_sem=s1.at[k],
                                              recv_sem=r1.at[k], device_id=(px, py, pc), device_id_type=MESH)
            cp.start()
            first.append(cp)
        buf_ref[d] = p_ref[mine, :]
        for k, (px, py, pc) in enumerate(peers):
            slot = buf_ref.at[4 * px + 2 * py + pc]
            pltpu.make_async_remote_copy(src_ref=slot, dst_ref=slot, send_sem=s1.at[k], recv_sem=r1.at[k],
                                         device_id=(px, py, pc), device_id_type=MESH).wait_recv()
        total = buf_ref[0]
        for k in range(1, 8):
            total = total + buf_ref[k]
        o_ref[mine, :] = total
        for k, (px, py, pc) in enumerate(peers):
            cp = pltpu.make_async_remote_copy(src_ref=o_ref.at[mine, :], dst_ref=o_ref.at[mine, :], send_sem=s2.at[k],
                                              recv_sem=r2.at[k], device_id=(px, py, pc), device_id_type=MESH)
            cp.start()
            second.append(cp)
        for k, (px, py, pc) in enumerate(peers):
            rows = o_ref.at[pl.ds(pl.multiple_of((4 * px + 2 * py + pc) * rs, 8), rs), :]
            pltpu.make_async_remote_copy(src_ref=rows, dst_ref=rows, send_sem=s2.at[k], recv_sem=r2.at[k],
                                         device_id=(px, py, pc), device_id_type=MESH).wait_recv()
        for cp in first + second:
            cp.wait_send()

    vm = pl.BlockSpec(memory_space=pltpu.VMEM)
    return pl.pallas_call(
        body, name="small_allreduce", in_specs=[vm], out_specs=vm, out_shape=jax.ShapeDtypeStruct(part.shape, F32),
        scratch_shapes=[pltpu.VMEM((8, rs, LANES), F32)] + [pltpu.SemaphoreType.DMA((7,))] * 4,
    )(part)


def _adamw(w, g, m, v, name):
    R, C = w.shape
    summed = g.ndim == 4
    if summed:
        tr = R // 2
    else:
        tr = R if R * C * 4 <= (1 << 21) else R // 2
        if tr % 8:
            tr = R
    c1 = 1.0 / (1.0 - ADAM_B1 ** ADAM_STEP)
    c2 = 1.0 / (1.0 - ADAM_B2 ** ADAM_STEP)

    def body(w_ref, g_ref, m_ref, v_ref, *outs):
        if summed:
            g_ = ((g_ref[0, 0].astype(F32) + g_ref[1, 0].astype(F32)) + g_ref[2, 0].astype(F32)) + g_ref[3, 0].astype(F32)
            outs[0][...] = g_
        else:
            g_ = g_ref[...]
        d_ref, mo_ref, vo_ref = outs[-3:]
        m_ = ADAM_B1 * m_ref[...] + (1.0 - ADAM_B1) * g_
        v_ = ADAM_B2 * v_ref[...] + (1.0 - ADAM_B2) * (g_ * g_)
        mo_ref[...] = m_
        vo_ref[...] = v_
        d_ref[...] = -ADAM_LR * ((m_ * c1) / (jnp.sqrt(v_ * c2) + ADAM_EPS) + ADAM_WD * w_ref[...])

    blk = BS((tr, C), lambda i: (i, 0))
    g_blk = BS((4, 1, tr, C), lambda i: (0, i, 0, 0)) if summed else blk
    nout = 4 if summed else 3
    return pl.pallas_call(body, name=name, grid=(R // tr,), in_specs=[blk, g_blk, blk, blk], out_specs=[blk] * nout,
                          out_shape=[jax.ShapeDtypeStruct((R, C), F32)] * nout,
                          compiler_params=_cp(("arbitrary",)))(w, g, m, v)


SMALL = ("g_pre_mix", "b_f", "g_sgu", "w_s", "b_s", "g_out_a", "g_out_b", "g_out_m", "g_mem", "g_post_mix",
         "g_pre_ffn", "g_post_ffn")
BIG = ("w_in", "w_mem_kv", "w_out", "w_gate", "w_up", "w_down")
TRANSPOSED = ("w_in", "w_gate", "w_up")
WEIGHTS = ("g_pre_mix", "w_in", "b_f", "g_sgu", "w_s", "b_s", "g_out_a", "g_out_b", "g_out_m", "g_mem", "w_mem_kv",
           "w_out", "g_post_mix", "g_pre_ffn", "w_gate", "w_up", "w_down", "g_post_ffn")


def _rows_of(n):
    return -(-n // (8 * LANES)) * 8


def _pack(parts):
    tiles = []
    for a in parts:
        flat = a.reshape(-1).astype(F32)
        rows = _rows_of(flat.shape[0])
        tiles.append(jnp.pad(flat, (0, rows * LANES - flat.shape[0])).reshape(rows, LANES))
    total = sum(t.shape[0] for t in tiles)
    pad = -total % 64
    if pad:
        tiles.append(jnp.zeros((pad, LANES), F32))
    return jnp.concatenate(tiles, axis=0)


def _unpack(packed, shapes):
    out, r = [], 0
    for shp in shapes:
        n = 1
        for s in shp:
            n *= s
        rows = _rows_of(n)
        out.append(packed[r:r + rows].reshape(-1)[:n].reshape(shp))
        r += rows
    return out


def kernel(x, mem, g_pre_mix, w_in, b_f, g_sgu, w_s, b_s, g_out_a, g_out_b, g_out_m, g_mem, w_mem_kv, w_out, g_post_mix, g_pre_ffn, w_gate, w_up, w_down, g_post_ffn, loss_target, m_g_pre_mix, m_w_in, m_b_f, m_g_sgu, m_w_s, m_b_s, m_g_out_a, m_g_out_b, m_g_out_m, m_g_mem, m_w_mem_kv, m_w_out, m_g_post_mix, m_g_pre_ffn, m_w_gate, m_w_up, m_w_down, m_g_post_ffn, v_g_pre_mix, v_w_in, v_b_f, v_g_sgu, v_w_s, v_b_s, v_g_out_a, v_g_out_b, v_g_out_m, v_g_mem, v_w_mem_kv, v_w_out, v_g_post_mix, v_g_pre_ffn, v_w_gate, v_w_up, v_w_down, v_g_post_ffn):
    Wt = dict(g_pre_mix=g_pre_mix, w_in=w_in, b_f=b_f, g_sgu=g_sgu, w_s=w_s, b_s=b_s, g_out_a=g_out_a, g_out_b=g_out_b,
              g_out_m=g_out_m, g_mem=g_mem, w_mem_kv=w_mem_kv, w_out=w_out, g_post_mix=g_post_mix, g_pre_ffn=g_pre_ffn,
              w_gate=w_gate, w_up=w_up, w_down=w_down, g_post_ffn=g_post_ffn)
    Mo = dict(g_pre_mix=m_g_pre_mix, w_in=m_w_in, b_f=m_b_f, g_sgu=m_g_sgu, w_s=m_w_s, b_s=m_b_s, g_out_a=m_g_out_a,
              g_out_b=m_g_out_b, g_out_m=m_g_out_m, g_mem=m_g_mem, w_mem_kv=m_w_mem_kv, w_out=m_w_out,
              g_post_mix=m_g_post_mix, g_pre_ffn=m_g_pre_ffn, w_gate=m_w_gate, w_up=m_w_up, w_down=m_w_down,
              g_post_ffn=m_g_post_ffn)
    Vo = dict(g_pre_mix=v_g_pre_mix, w_in=v_w_in, b_f=v_b_f, g_sgu=v_g_sgu, w_s=v_w_s, b_s=v_b_s, g_out_a=v_g_out_a,
              g_out_b=v_g_out_b, g_out_m=v_g_out_m, g_mem=v_g_mem, w_mem_kv=v_w_mem_kv, w_out=v_w_out,
              g_post_mix=v_g_post_mix, g_pre_ffn=v_g_pre_ffn, w_gate=v_w_gate, w_up=v_w_up, w_down=v_w_down,
              g_post_ffn=v_g_post_ffn)

    gap = P_COLS - IN_COLS

    def to_kernel(n, w):
        if n in TRANSPOSED:
            w = w.T
        if n == "w_in":
            w = jnp.pad(w[:F_END], ((0, P_COLS - F_END), (0, 0))) + jnp.pad(w[F_END:], ((F_END + gap, 0), (0, 0)))
        return w

    def ungroup(g):
        return jnp.pad(g[:F_END], ((0, IN_COLS - F_END), (0, 0))) + jnp.pad(g[F_END + gap:], ((F_END, 0), (0, 0)))

    shards = {n: to_kernel(n, Wt[n][0]) for n in BIG}
    srcs = [shards[n].astype(BF16).reshape(2, shards[n].shape[0] // 2, shards[n].shape[1]) for n in BIG]
    fulls = (_exchange_on_sequencer(srcs[:1], True, "gather_w_in", 1)
             + _exchange_on_sequencer(srcs[1:3], True, "gather_kv_out", 2)
             + _exchange_on_sequencer(srcs[3:], True, "gather_ffn", 3))
    W = {}
    for n, f in zip(BIG, fulls):
        _, _, hr, C = f.shape
        W[n] = f.reshape(8 * hr, C) if n in ("w_mem_kv", "w_out") else f.reshape(4, 2 * hr, C)

    P = {n: Wt[n] for n in SMALL}
    grads, deltas, new_m, new_v = {}, {}, {}, {}

    def apply(names, landed):
        for n, r in zip(names, landed):
            wmv = [a[n][0].T if n in TRANSPOSED else a[n][0] for a in (Wt, Mo, Vo)]
            if n == "w_in":
                g = ungroup(_sum_chips(r, "sum_chips_" + n))
                g, d, m1, v1 = (g,) + tuple(_adamw(wmv[0], g, wmv[1], wmv[2], "adamw_" + n))
            else:
                g, d, m1, v1 = _adamw(wmv[0], r, wmv[1], wmv[2], "adamw_" + n)
            if n in TRANSPOSED:
                g, d, m1, v1 = g.T, d.T, m1.T, v1.T
            grads[n], deltas[n], new_m[n], new_v[n] = g[None], d[None], m1[None], v1[None]
        return tuple(deltas[n] for n in names)

    core = lax.axis_index("c").astype(jnp.int32).reshape(1)
    reducer = _Reducer(core, apply)
    grad_x, pending, small = _local_step(x, mem, loss_target, W, P, reducer)

    total = _small_allreduce(_pack([small[n] for n in SMALL] + [small["loss"]]))
    landed, (total,) = reducer.finish("in", pending["w_in"], (total,))
    apply(BIG[:1], landed)

    slot = [jnp.zeros((1, 1), F32)]
    shapes = [Wt[n].shape for n in SMALL] + [(1, 1)]
    d, m1, v1 = _adamw(_pack([Wt[n] for n in SMALL] + slot), total, _pack([Mo[n] for n in SMALL] + slot),
                       _pack([Vo[n] for n in SMALL] + slot), "adamw_small")
    g_s, d_s, m_s, v_s = _unpack(total, shapes), _unpack(d, shapes), _unpack(m1, shapes), _unpack(v1, shapes)
    for k, n in enumerate(SMALL):
        grads[n], deltas[n], new_m[n], new_v[n] = g_s[k], d_s[k], m_s[k], v_s[k]
    loss = g_s[-1][0, 0]

    return (loss, grad_x, *[grads[n] for n in WEIGHTS], *[deltas[n] for n in WEIGHTS],
            *[new_m[n] for n in WEIGHTS], *[new_v[n] for n in WEIGHTS])
```

```python
import functools

import jax
import jax.numpy as jnp
from jax import lax
from jax.experimental import pallas as pl
from jax.experimental.pallas import tpu as pltpu
from jax.experimental.pallas import tpu_sc as plsc

F32 = jnp.float32
BF16 = jnp.bfloat16
EPS = 1e-6
NEG = -1e30
HEAD = 64
A_W, B_W, M_W = 384, 384, 256
N_FOX_HEADS = 6
CHUNK = 128
IN_COLS = 2 * A_W + 3 * B_W + N_FOX_HEADS + M_W
P_MAIN = 2 * A_W + 3 * B_W + M_W
P_COLS = P_MAIN + 128
F_END = 2 * A_W + 3 * B_W + N_FOX_HEADS
LANES = 128
Q_BLK, K_BLK = 256, 128
ROW_SPLIT = 4
ADAM_LR, ADAM_B1, ADAM_B2, ADAM_EPS, ADAM_WD, ADAM_STEP = 0.001, 0.9, 0.999, 1e-08, 0.01, 10
VMEM_LIMIT = 56 * 1024 * 1024
MESH = pl.DeviceIdType.MESH
ANY = pl.BlockSpec(memory_space=pl.ANY)
BS = pl.BlockSpec


def _cp(sem=None):
    return pltpu.CompilerParams(dimension_semantics=sem, vmem_limit_bytes=VMEM_LIMIT)


def _iota(shape, dim):
    return lax.broadcasted_iota(jnp.int32, shape, dim)


def _dot(a, b):
    return jnp.dot(a.astype(BF16), b.astype(BF16), preferred_element_type=F32)


def _dot_nt(a, b):
    return lax.dot_general(a.astype(BF16), b.astype(BF16), (((1,), (1,)), ((), ())), preferred_element_type=F32)


def _dot_tn(a, b):
    return lax.dot_general(a.astype(BF16), b.astype(BF16), (((0,), (0,)), ((), ())), preferred_element_type=F32)


def _rms(x, g):
    return x * lax.rsqrt(jnp.mean(x * x, axis=-1, keepdims=True) + EPS) * g


def _rms_bwd(x, g, dy):
    r = lax.rsqrt(jnp.mean(x * x, axis=-1, keepdims=True) + EPS)
    xr = x * r
    gd = dy * g
    m = jnp.mean(gd * xr, axis=-1, keepdims=True)
    return (gd - xr * m) * r, _colsum(dy * xr)


def _gelu(x):
    return 0.5 * x * (1.0 + jnp.tanh(0.7978845608028654 * (x + 0.044715 * (x * x * x))))


def _sigmoid(x):
    return 1.0 / (1.0 + jnp.exp(-x))


def _silu_mul(g, u):
    return g * _sigmoid(g) * u


def _logsig(x):
    return jnp.minimum(x, 0.0) - jnp.log(1.0 + jnp.exp(-jnp.abs(x)))


def _colsum(x):
    return jnp.sum(x, axis=0, keepdims=True)


def _acc(ref, val, first):
    @pl.when(first)
    def _():
        ref[...] = val

    @pl.when(jnp.logical_not(first))
    def _():
        ref[...] += val


def _inproj_fwd(x2d, g_pre, w_in_p, tm):
    T, D = x2d.shape
    CH = 768
    nchunk = P_COLS // CH
    ns, _, dsh = w_in_p.shape

    def body(x_ref, g_ref, w_ref, h_ref, proj_ref, fl_ref):
        h = _rms(x_ref[...], g_ref[...]).astype(BF16)
        h_ref[...] = h
        for n in range(nchunk):
            rows = slice(n * CH, (n + 1) * CH)
            r = _dot_nt(h[:, 0:dsh], w_ref[0, rows, :])
            for s in range(1, ns):
                r = r + _dot_nt(h[:, s * dsh:(s + 1) * dsh], w_ref[s, rows, :])
            if n < nchunk - 1:
                proj_ref[:, rows] = r.astype(BF16)
            else:
                fg = 1920 - n * CH
                proj_ref[:, n * CH:1920] = r[:, :fg].astype(BF16)
                fl_ref[...] = r[:, fg:fg + LANES]
                proj_ref[:, 1920:P_MAIN] = r[:, fg + LANES:].astype(BF16)

    return pl.pallas_call(
        body, name="inproj_fwd", grid=(T // tm,),
        in_specs=[BS((tm, D), lambda i: (i, 0)), BS((1, D), lambda i: (0, 0)),
                  BS((ns, P_COLS, dsh), lambda i: (0, 0, 0))],
        out_specs=[BS((tm, D), lambda i: (i, 0)), BS((tm, P_MAIN), lambda i: (i, 0)), BS((tm, LANES), lambda i: (i, 0))],
        out_shape=[jax.ShapeDtypeStruct((T, D), BF16), jax.ShapeDtypeStruct((T, P_MAIN), BF16),
                   jax.ShapeDtypeStruct((T, LANES), F32)],
        compiler_params=_cp(("arbitrary",)),
    )(x2d, g_pre, w_in_p)


def _gate_fwd(flog3, bf_row):
    Bl, S, _ = flog3.shape
    nb = S // LANES

    def body(f_ref, b_ref, bq_ref, bk_ref, fr_ref):
        row = _iota((LANES, LANES), 0)
        lane = _iota((LANES, LANES), 1)
        one = jnp.ones((LANES, LANES), BF16)
        zero = jnp.zeros((LANES, LANES), BF16)

        carry = jnp.zeros((1, LANES), F32)
        for j in range(nb):
            r0 = j * LANES
            fl = f_ref[0, pl.ds(r0, LANES), :] + b_ref[...]
            fr_ref[0, j] = fl.T[0:8, :]
            c = _logsig(fl)
            for k in (1, 2, 4, 8, 16, 32, 64):
                c = c + jnp.where(row >= k, pltpu.roll(c, k, 0), 0.0)
            total = _colsum(jnp.where(row == LANES - 1, c, 0.0))
            c = c + carry
            carry = carry + total
            for h in range(N_FOX_HEADS):
                col = jnp.sum(jnp.where(lane == h, c, 0.0), axis=1, keepdims=True)
                hi = col.astype(BF16)
                rest = col - hi.astype(F32)
                mid = rest.astype(BF16)
                lo = (rest - mid.astype(F32)).astype(BF16)
                base = _bias_lane(h)
                bq = jnp.where(lane == base, hi, jnp.where(lane == base + 1, mid, jnp.where(lane == base + 2, lo, zero)))
                bq = jnp.where((lane >= base + 3) & (lane < base + 6), one, bq)
                bk = jnp.where(lane == base + 3, -hi, jnp.where(lane == base + 4, -mid, jnp.where(lane == base + 5, -lo, zero)))
                bk = jnp.where((lane >= base) & (lane < base + 3), one, bk)
                bq_ref[0, h, pl.ds(r0, LANES), :] = bq
                bk_ref[0, h, pl.ds(r0, LANES), :] = bk

    slab = BS((1, N_FOX_HEADS, S, LANES), lambda b: (b, 0, 0, 0))
    return pl.pallas_call(
        body, name="gate_fwd", grid=(Bl,),
        in_specs=[BS((1, S, LANES), lambda b: (b, 0, 0)), BS((1, LANES), lambda b: (0, 0))],
        out_specs=[slab, slab, BS((1, nb, 8, LANES), lambda b: (b, 0, 0, 0))],
        out_shape=[jax.ShapeDtypeStruct((Bl, N_FOX_HEADS, S, LANES), BF16),
                   jax.ShapeDtypeStruct((Bl, N_FOX_HEADS, S, LANES), BF16),
                   jax.ShapeDtypeStruct((Bl, nb, 8, LANES), F32)],
        compiler_params=_cp(("arbitrary",)),
    )(flog3, bf_row)


def _bias_lane(h):
    return HEAD if h % 2 == 0 else 0


def _sgu_pre(zu, zv, g_sgu):
    return _gelu(zu), _rms(_gelu(zv), g_sgu)


def _sgu_fwd(proj, g_sgu, ws_tril, bs_full, tm):
    T = proj.shape[0]
    nch = tm // CHUNK

    def body(zu_ref, zv_ref, g_ref, ws_ref, b_ref, ya_ref):
        lane = _iota((CHUNK, LANES), 1)
        u, vn = _sgu_pre(zu_ref[...].astype(F32), zv_ref[...].astype(F32), g_ref[...])
        vn = vn.astype(BF16)
        for c in range(nch):
            rs = slice(c * CHUNK, (c + 1) * CHUNK)
            for j in range(3):
                cs = slice(j * LANES, (j + 1) * LANES)
                vp = vn[rs, cs]
                z = jnp.where(lane < HEAD, _dot(ws_ref[2 * j], vp), _dot(ws_ref[2 * j + 1], vp)) + b_ref[:, cs]
                ya_ref[rs, cs] = (u[rs, cs] * z).astype(BF16)

    return pl.pallas_call(
        body, name="sgu_fwd", grid=(T // tm,),
        in_specs=[BS((tm, A_W), lambda i: (i, 0)), BS((tm, A_W), lambda i: (i, 1)), BS((1, A_W), lambda i: (0, 0)),
                  BS((6, CHUNK, CHUNK), lambda i: (0, 0, 0)), BS((CHUNK, A_W), lambda i: (0, 0))],
        out_specs=BS((tm, A_W), lambda i: (i, 0)),
        out_shape=jax.ShapeDtypeStruct((T, A_W), BF16),
        compiler_params=_cp(("arbitrary",)),
    )(proj, proj, g_sgu, ws_tril, bs_full)


def _fox_fwd(proj, bq, bk, Bl, S):
    T = Bl * S
    nq = S // Q_BLK
    qc, kc, vc = 768 // LANES, 1152 // LANES, 1536 // LANES

    def body(q_ref, k_ref, v_ref, bq_ref, bk_ref, o_ref, lse_ref, ka_ref, va_ref):
        lane_s = _iota((S, LANES), 1)
        lane = _iota((Q_BLK, LANES), 1)
        tri = _iota((Q_BLK, Q_BLK), 1) <= _iota((Q_BLK, Q_BLK), 0)
        k = k_ref[...]
        v = v_ref[...]
        for hh in range(2):
            data = (lane_s < HEAD) if hh == 0 else (lane_s >= HEAD)
            ka_ref[hh] = jnp.where(data, k, bk_ref[0, hh])
            va_ref[hh] = jnp.where(lane_s == _bias_lane(hh), jnp.ones_like(v), v)
        for i in range(nq):
            r0 = i * Q_BLK
            q = q_ref[r0:r0 + Q_BLK, :]
            o_out = jnp.zeros((Q_BLK, LANES), F32)
            lse_out = jnp.zeros((Q_BLK, LANES), F32)
            for hh in range(2):
                hmask = (lane < HEAD) if hh == 0 else (lane >= HEAD)
                qa = jnp.where(hmask, q * 0.125, bq_ref[0, hh, r0:r0 + Q_BLK, :])
                sd = jnp.where(tri, _dot_nt(qa, ka_ref[hh, r0:r0 + Q_BLK, :]), NEG)
                m = jnp.max(sd, axis=1, keepdims=True)
                if i:
                    sf = _dot_nt(qa, ka_ref[hh, 0:r0, :])
                    m = jnp.maximum(m, jnp.max(sf, axis=1, keepdims=True))
                acc = _dot(jnp.exp(sd - m), va_ref[hh, r0:r0 + Q_BLK, :])
                if i:
                    acc = acc + _dot(jnp.exp(sf - m), va_ref[hh, 0:r0, :])
                l = jnp.sum(jnp.where(lane == _bias_lane(hh), acc, 0.0), axis=1, keepdims=True)
                o_out = jnp.where(hmask, acc / l, o_out)
                lse_out = jnp.where(hmask, m + jnp.log(l), lse_out)
            o_ref[r0:r0 + Q_BLK, :] = o_out.astype(BF16)
            lse_ref[0, r0:r0 + Q_BLK, :] = lse_out

    seq = lambda c0: BS((S, LANES), lambda b, p: (b, c0 + p))
    pair = BS((1, 2, S, LANES), lambda b, p: (b, p, 0, 0))
    return pl.pallas_call(
        body, name="fox_fwd", grid=(Bl, 3),
        in_specs=[seq(qc), seq(kc), seq(vc), pair, pair],
        out_specs=[seq(0), BS((1, S, LANES), lambda b, p: (p, b, 0))],
        out_shape=[jax.ShapeDtypeStruct((T, B_W), BF16), jax.ShapeDtypeStruct((3, T, LANES), F32)],
        scratch_shapes=[pltpu.VMEM((2, S, LANES), BF16), pltpu.VMEM((2, S, LANES), BF16)],
        compiler_params=_cp(("arbitrary", "arbitrary")),
    )(proj, proj, proj, bq, bk)


def _memkv_fwd(mem, g_mem, w_kv):
    Bl, Mt, D = mem.shape

    def body(m_ref, g_ref, w_ref, mn_ref, kv_ref):
        mn = _rms(m_ref[0], g_ref[...]).astype(BF16)
        mn_ref[0] = mn
        kv_ref[0] = jnp.dot(mn, w_ref[...], preferred_element_type=F32).astype(BF16)

    return pl.pallas_call(
        body, name="memkv_fwd", grid=(Bl,),
        in_specs=[BS((1, Mt, D), lambda b: (b, 0, 0)), BS((1, D), lambda b: (0, 0)), BS((D, 2 * M_W), lambda b: (0, 0))],
        out_specs=[BS((1, Mt, D), lambda b: (b, 0, 0)), BS((1, Mt, 2 * M_W), lambda b: (b, 0, 0))],
        out_shape=[jax.ShapeDtypeStruct((Bl, Mt, D), BF16), jax.ShapeDtypeStruct((Bl, Mt, 2 * M_W), BF16)],
        compiler_params=_cp(("arbitrary",)),
    )(mem, g_mem, w_kv)


def _memattn_fwd(proj, kv, Bl, S, tq):
    T = Bl * S
    nq = S // tq
    Mt = kv.shape[1]
    qc = 1920 // LANES

    def body(q_ref, km_ref, vm_ref, o_ref):
        lane = _iota((tq, LANES), 1)
        q = q_ref[...]
        out = jnp.zeros((tq, LANES), F32)
        for hh in range(2):
            hmask = (lane < HEAD) if hh == 0 else (lane >= HEAD)
            qs = jnp.where(hmask, q, jnp.zeros_like(q)) * 0.125
            s = _dot_nt(qs, km_ref[0])
            pe = jnp.exp(s - jnp.max(s, axis=1, keepdims=True))
            pn = pe / jnp.sum(pe, axis=1, keepdims=True)
            out = jnp.where(hmask, _dot(pn, vm_ref[0]), out)
        o_ref[...] = out.astype(BF16)

    return pl.pallas_call(
        body, name="memattn_fwd", grid=(Bl, 2, nq),
        in_specs=[BS((tq, LANES), lambda b, p, i: (b * nq + i, qc + p)),
                  BS((1, Mt, LANES), lambda b, p, i: (b, 0, p)),
                  BS((1, Mt, LANES), lambda b, p, i: (b, 0, 2 + p))],
        out_specs=BS((tq, LANES), lambda b, p, i: (b * nq + i, p)),
        out_shape=jax.ShapeDtypeStruct((T, M_W), BF16),
        compiler_params=_cp(("arbitrary", "arbitrary", "arbitrary")),
    )(proj, kv, kv)


def _mix_norms(ya, yb, ym, ga, gb, gm):
    return _rms(ya, ga), _rms(yb, gb), _rms(ym, gm)


def _outproj_fwd(ya, yb, ym, x2d, ga, gb, gm, g_post, g_pre2, w_out, tm):
    T, D = x2d.shape

    def body(ya_ref, yb_ref, ym_ref, x_ref, ga_ref, gb_ref, gm_ref, gp_ref, g2_ref, w_ref,
             y_ref, o_ref, x1_ref, h2_ref):
        na, nb_, nm = _mix_norms(ya_ref[...].astype(F32), yb_ref[...].astype(F32), ym_ref[...].astype(F32),
                                 ga_ref[...], gb_ref[...], gm_ref[...])
        y_ref[:, 0:A_W] = na.astype(BF16)
        y_ref[:, A_W:A_W + B_W] = nb_.astype(BF16)
        y_ref[:, A_W + B_W:] = nm.astype(BF16)
        o = jnp.dot(y_ref[...], w_ref[...], preferred_element_type=F32)
        o_ref[...] = o
        x1 = x_ref[...] + _rms(o, gp_ref[...])
        x1_ref[...] = x1
        h2_ref[...] = _rms(x1, g2_ref[...]).astype(BF16)

    row = lambda w: BS((tm, w), lambda i: (i, 0))
    vec = lambda w: BS((1, w), lambda i: (0, 0))
    return pl.pallas_call(
        body, name="outproj_fwd", grid=(T // tm,),
        in_specs=[row(A_W), row(B_W), row(M_W), row(D), vec(A_W), vec(B_W), vec(M_W), vec(D), vec(D),
                  BS((A_W + B_W + M_W, D), lambda i: (0, 0))],
        out_specs=[row(A_W + B_W + M_W), row(D), row(D), row(D)],
        out_shape=[jax.ShapeDtypeStruct((T, A_W + B_W + M_W), BF16), jax.ShapeDtypeStruct((T, D), F32),
                   jax.ShapeDtypeStruct((T, D), F32), jax.ShapeDtypeStruct((T, D), BF16)],
        compiler_params=_cp(("arbitrary",)),
    )(ya, yb, ym, x2d, ga, gb, gm, g_post, g_pre2, w_out)


def _ffn_fwd(h2, x1, target, wg, wu, wd, g_post, tm):
    T, D = x1.shape
    ns, F, _ = wg.shape

    def body(h_ref, x1_ref, t_ref, wg_ref, wu_ref, wd_ref, gp_ref,
             gs_ref, us_ref, dff_ref, dx2_ref, dgp_ref, loss_ref, acc_ref):
        i = pl.program_id(0)
        j = pl.program_id(1)
        h = h_ref[...]
        g = _dot_nt(h, wg_ref[0])
        u = _dot_nt(h, wu_ref[0])
        gs_ref[0] = g.astype(BF16)
        us_ref[0] = u.astype(BF16)
        part = _dot(_silu_mul(g, u), wd_ref[0])
        _acc(acc_ref, part, j == 0)

        @pl.when(j == ns - 1)
        def _():
            ff = acc_ref[...]
            diff = x1_ref[...] + _rms(ff, gp_ref[...]) - t_ref[...]
            dx2 = diff * (1.0 / D)
            dff, dgp = _rms_bwd(ff, gp_ref[...], dx2)
            dx2_ref[...] = dx2
            dff_ref[...] = dff.astype(BF16)
            lpart = jnp.sum(_colsum(diff * diff), axis=1, keepdims=True) * (0.5 / D)
            _acc(dgp_ref, dgp, i == 0)
            _acc(loss_ref, jnp.broadcast_to(lpart, (1, LANES)), i == 0)

    row = lambda w: BS((tm, w), lambda i, j: (i, 0))
    return pl.pallas_call(
        body, name="ffn_fwd", grid=(T // tm, ns),
        in_specs=[row(D), row(D), row(D), BS((1, F, D), lambda i, j: (j, 0, 0)), BS((1, F, D), lambda i, j: (j, 0, 0)),
                  BS((1, F, D), lambda i, j: (j, 0, 0)), BS((1, D), lambda i, j: (0, 0))],
        out_specs=[BS((1, tm, F), lambda i, j: (j, i, 0)), BS((1, tm, F), lambda i, j: (j, i, 0)), row(D), row(D),
                   BS((1, D), lambda i, j: (0, 0)), BS((1, LANES), lambda i, j: (0, 0))],
        out_shape=[jax.ShapeDtypeStruct((ns, T, F), BF16), jax.ShapeDtypeStruct((ns, T, F), BF16),
                   jax.ShapeDtypeStruct((T, D), BF16), jax.ShapeDtypeStruct((T, D), F32),
                   jax.ShapeDtypeStruct((1, D), F32), jax.ShapeDtypeStruct((1, LANES), F32)],
        scratch_shapes=[pltpu.VMEM((tm, D), F32)],
        compiler_params=_cp(("arbitrary", "arbitrary")),
    )(h2, x1, target, wg, wu, wd, g_post)


def _ffn_bwd(dff, h2, gs, us, wg, wu, wd, tm):
    T, D = h2.shape
    ns, F, _ = wg.shape

    def body(dff_ref, h_ref, gs_ref, us_ref, wg_ref, wu_ref, wd_ref, dh_ref, dwg_out, dwu_out, dwd_out,
             dwg_ref, dwu_ref, dwd_ref):
        first = pl.program_id(1) == 0
        dff = dff_ref[...]
        h = h_ref[...]
        parts = []
        for r in range(ROW_SPLIT):
            rows = slice(r * (tm // ROW_SPLIT), (r + 1) * (tm // ROW_SPLIT))
            dact = _dot_nt(dff[rows], wd_ref[0])
            g = gs_ref[0, rows, :].astype(F32)
            u = us_ref[0, rows, :].astype(F32)
            sig = _sigmoid(g)
            gsig = g * sig
            dg = (dact * u * (sig + gsig * (1.0 - sig))).astype(BF16)
            du = (dact * gsig).astype(BF16)
            dh_ref[0, rows, :] = (_dot(dg, wg_ref[0]) + _dot(du, wu_ref[0])).astype(BF16)
            parts.append(((gsig * u).astype(BF16), dg, du))
        a, dg, du = [jnp.concatenate(p, axis=0) for p in zip(*parts)]
        _acc(dwd_ref, _dot_tn(a, dff), first)
        _acc(dwg_ref, _dot_tn(dg, h), first)
        _acc(dwu_ref, _dot_tn(du, h), first)

        @pl.when(pl.program_id(1) == pl.num_programs(1) - 1)
        def _():
            dwg_out[0] = dwg_ref[...].astype(BF16)
            dwu_out[0] = dwu_ref[...].astype(BF16)
            dwd_out[0] = dwd_ref[...].astype(BF16)

    row = BS((tm, D), lambda j, i: (i, 0))
    sh = BS((1, tm, F), lambda j, i: (j, i, 0))
    wsh = BS((1, F, D), lambda j, i: (j, 0, 0))
    return pl.pallas_call(
        body, name="ffn_bwd", grid=(ns, T // tm),
        in_specs=[row, row, sh, sh, wsh, wsh, wsh],
        out_specs=[BS((1, tm, D), lambda j, i: (j, i, 0)), wsh, wsh, wsh],
        out_shape=[jax.ShapeDtypeStruct((ns, T, D), BF16)] + [jax.ShapeDtypeStruct((ns, F, D), BF16)] * 3,
        scratch_shapes=[pltpu.VMEM((F, D), F32)] * 3,
        compiler_params=_cp(("arbitrary", "arbitrary")),
    )(dff, h2, gs, us, wg, wu, wd)


def _mm_tn(a, b, name, tk):
    T, M = a.shape
    N = b.shape[1]
    tk = min(tk, T)

    def body(a_ref, b_ref, o_ref):
        _acc(o_ref, _dot_tn(a_ref[...], b_ref[...]), pl.program_id(0) == 0)

    return pl.pallas_call(
        body, name=name, grid=(T // tk,),
        in_specs=[BS((tk, M), lambda t: (t, 0)), BS((tk, N), lambda t: (t, 0))],
        out_specs=BS((M, N), lambda t: (0, 0)),
        out_shape=jax.ShapeDtypeStruct((M, N), F32),
        compiler_params=_cp(("arbitrary",)),
    )(a, b)


DPROJ_PIECES = ((0, A_W), (A_W, A_W), (768, B_W), (1152, B_W), (1536, B_W), (1920, LANES), (2048, M_W))


def _put_dproj(dp_ref, piece_refs):
    for (c0, w), ref in zip(DPROJ_PIECES, piece_refs):
        dp_ref[:, c0:c0 + w] = ref[...].astype(BF16)


def _dw_in(pieces, h, ns, tk):
    T, D = h.shape
    M = P_COLS
    dsh = D // ns
    tk = min(tk, T)

    def body(*refs):
        piece_refs, h_ref, o_ref, acc_ref, dp_ref = refs[:7], refs[7], refs[8], refs[9], refs[10]
        t = pl.program_id(0)
        _put_dproj(dp_ref, piece_refs)
        _acc(acc_ref, _dot_tn(h_ref[...], dp_ref[...]), t == 0)

        @pl.when(t == pl.num_programs(0) - 1)
        def _():
            for s in range(ns):
                o_ref[s] = acc_ref[s * dsh:(s + 1) * dsh, :].T.astype(BF16)

    return pl.pallas_call(
        body, name="dw_in", grid=(T // tk,),
        in_specs=[BS((tk, w), lambda t: (t, 0)) for _, w in DPROJ_PIECES] + [BS((tk, D), lambda t: (t, 0))],
        out_specs=BS((ns, M, dsh), lambda t: (0, 0, 0)),
        out_shape=jax.ShapeDtypeStruct((ns, M, dsh), BF16),
        scratch_shapes=[pltpu.VMEM((D, M), F32), pltpu.VMEM((tk, M), BF16)],
        compiler_params=_cp(("arbitrary",)),
    )(*pieces, h)


def _outproj_bwd(dh2, x1, dx2, o, ya, yb, ym, ga, gb, gm, g_post, g_pre2, w_out, tm):
    T, D = x1.shape
    ns = dh2.shape[0]

    def body(dh_ref, x1_ref, dx2_ref, o_ref, ya_ref, yb_ref, ym_ref, ga_ref, gb_ref, gm_ref, gp_ref, g2_ref, w_ref,
             dx1_ref, do_ref, dya_ref, dyb_ref, dym_ref, dga_ref, dgb_ref, dgm_ref, dgp_ref, dg2_ref):
        first = pl.program_id(0) == 0
        dh = dh_ref[0].astype(F32)
        for j in range(1, ns):
            dh = dh + dh_ref[j].astype(F32)
        dxa, dg2 = _rms_bwd(x1_ref[...], g2_ref[...], dh)
        dx1 = dx2_ref[...] + dxa
        dx1_ref[...] = dx1
        _acc(dg2_ref, dg2, first)
        do, dgp = _rms_bwd(o_ref[...], gp_ref[...], dx1)
        do = do.astype(BF16)
        do_ref[...] = do
        dy = _dot_nt(do, w_ref[...])
        dya, dga = _rms_bwd(ya_ref[...].astype(F32), ga_ref[...], dy[:, 0:A_W])
        dyb, dgb = _rms_bwd(yb_ref[...].astype(F32), gb_ref[...], dy[:, A_W:A_W + B_W])
        dym, dgm = _rms_bwd(ym_ref[...].astype(F32), gm_ref[...], dy[:, A_W + B_W:])
        dya_ref[...] = dya.astype(BF16)
        dyb_ref[...] = dyb.astype(BF16)
        dym_ref[...] = dym.astype(BF16)
        _acc(dga_ref, dga, first)
        _acc(dgb_ref, dgb, first)
        _acc(dgm_ref, dgm, first)
        _acc(dgp_ref, dgp, first)

    row = lambda w: BS((tm, w), lambda i: (i, 0))
    vec = lambda w: BS((1, w), lambda i: (0, 0))
    sds = jax.ShapeDtypeStruct
    return pl.pallas_call(
        body, name="outproj_bwd", grid=(T // tm,),
        in_specs=[BS((ns, tm, D), lambda i: (0, i, 0)), row(D), row(D), row(D), row(A_W), row(B_W), row(M_W),
                  vec(A_W), vec(B_W), vec(M_W), vec(D), vec(D), BS((A_W + B_W + M_W, D), lambda i: (0, 0))],
        out_specs=[row(D), row(D), row(A_W), row(B_W), row(M_W), vec(A_W), vec(B_W), vec(M_W), vec(D), vec(D)],
        out_shape=[sds((T, D), F32), sds((T, D), BF16), sds((T, A_W), BF16), sds((T, B_W), BF16), sds((T, M_W), BF16),
                   sds((1, A_W), F32), sds((1, B_W), F32), sds((1, M_W), F32), sds((1, D), F32), sds((1, D), F32)],
        compiler_params=_cp(("arbitrary",)),
    )(dh2, x1, dx2, o, ya, yb, ym, ga, gb, gm, g_post, g_pre2, w_out)


def _sgu_bwd(proj, dya, g_sgu, ws_tril, bs_full, tm):
    T = proj.shape[0]
    nch = tm // CHUNK

    def body(zu_ref, zv_ref, dy_ref, g_ref, ws_ref, b_ref, dzu_ref, dzv_ref, dws_ref, dbs_ref, dg_ref,
             du_ref, dvn_ref, dbf_ref):
        step = pl.program_id(0)
        first = step == 0
        lane = _iota((CHUNK, LANES), 1)
        tril = _iota((CHUNK, CHUNK), 0) >= _iota((CHUNK, CHUNK), 1)
        (u, vn), vjp = jax.vjp(_sgu_pre, zu_ref[...].astype(F32), zv_ref[...].astype(F32), g_ref[...])
        vnb = vn.astype(BF16)
        dy = dy_ref[...].astype(F32)

        @pl.when(first)
        def _():
            dws_ref[...] = jnp.zeros_like(dws_ref)
            dbf_ref[...] = jnp.zeros_like(dbf_ref)

        for c in range(nch):
            rs = slice(c * CHUNK, (c + 1) * CHUNK)
            for j in range(3):
                cs = slice(j * LANES, (j + 1) * LANES)
                vp = vnb[rs, cs]
                z = jnp.where(lane < HEAD, _dot(ws_ref[2 * j], vp), _dot(ws_ref[2 * j + 1], vp)) + b_ref[:, cs]
                du_ref[rs, cs] = dy[rs, cs] * z
                dz = dy[rs, cs] * u[rs, cs]
                dbf_ref[:, cs] += dz
                dzb = dz.astype(BF16)
                dz0 = jnp.where(lane < HEAD, dzb, jnp.zeros_like(dzb))
                dz1 = jnp.where(lane >= HEAD, dzb, jnp.zeros_like(dzb))
                dvn_ref[rs, cs] = jnp.where(lane < HEAD, _dot_tn(ws_ref[2 * j], dzb), _dot_tn(ws_ref[2 * j + 1], dzb))
                dws_ref[2 * j] += jnp.where(tril, _dot_nt(dz0, vp), 0.0)
                dws_ref[2 * j + 1] += jnp.where(tril, _dot_nt(dz1, vp), 0.0)
        dzu, dzv, dg = vjp((du_ref[...], dvn_ref[...]))
        dzu_ref[...] = dzu.astype(BF16)
        dzv_ref[...] = dzv.astype(BF16)
        _acc(dg_ref, dg, first)

        @pl.when(step == pl.num_programs(0) - 1)
        def _():
            out = jnp.zeros((CHUNK, LANES), F32)
            for j in range(3):
                slab = dbf_ref[:, j * LANES:(j + 1) * LANES]
                lo = jnp.sum(jnp.where(lane < HEAD, slab, 0.0), axis=1, keepdims=True)
                hi = jnp.sum(jnp.where(lane >= HEAD, slab, 0.0), axis=1, keepdims=True)
                out = out + jnp.where(lane == 2 * j, lo, 0.0) + jnp.where(lane == 2 * j + 1, hi, 0.0)
            dbs_ref[...] = out

    return pl.pallas_call(
        body, name="sgu_bwd", grid=(T // tm,),
        in_specs=[BS((tm, A_W), lambda i: (i, 0)), BS((tm, A_W), lambda i: (i, 1)), BS((tm, A_W), lambda i: (i, 0)),
                  BS((1, A_W), lambda i: (0, 0)), BS((6, CHUNK, CHUNK), lambda i: (0, 0, 0)),
                  BS((CHUNK, A_W), lambda i: (0, 0))],
        out_specs=[BS((tm, A_W), lambda i: (i, 0)), BS((tm, A_W), lambda i: (i, 0)),
                   BS((6, CHUNK, CHUNK), lambda i: (0, 0, 0)), BS((CHUNK, LANES), lambda i: (0, 0)),
                   BS((1, A_W), lambda i: (0, 0))],
        out_shape=[jax.ShapeDtypeStruct((T, A_W), BF16), jax.ShapeDtypeStruct((T, A_W), BF16),
                   jax.ShapeDtypeStruct((6, CHUNK, CHUNK), F32), jax.ShapeDtypeStruct((CHUNK, LANES), F32),
                   jax.ShapeDtypeStruct((1, A_W), F32)],
        scratch_shapes=[pltpu.VMEM((tm, A_W), F32), pltpu.VMEM((tm, A_W), F32), pltpu.VMEM((CHUNK, A_W), F32)],
        compiler_params=_cp(("arbitrary",)),
    )(proj, proj, dya, g_sgu, ws_tril, bs_full)


def _memattn_bwd(proj, kv, dym, Bl, S, tq):
    T = Bl * S
    nq = S // tq
    Mt = kv.shape[1]
    qc = 1920 // LANES

    def body(q_ref, km_ref, vm_ref, do_ref, dq_ref, dkm_ref, dvm_ref):
        first = pl.program_id(2) == 0
        lane = _iota((tq, LANES), 1)
        q = q_ref[...]
        do = do_ref[...]
        dq_out = jnp.zeros((tq, LANES), F32)
        dkm = jnp.zeros((Mt, LANES), F32)
        dvm = jnp.zeros((Mt, LANES), F32)
        for hh in range(2):
            hmask = (lane < HEAD) if hh == 0 else (lane >= HEAD)
            qs = jnp.where(hmask, q, jnp.zeros_like(q)) * 0.125
            dom = jnp.where(hmask, do, 0.0).astype(BF16)
            s = _dot_nt(qs, km_ref[0])
            pe = jnp.exp(s - jnp.max(s, axis=1, keepdims=True))
            pn = pe / jnp.sum(pe, axis=1, keepdims=True)
            dp = _dot_nt(dom, vm_ref[0])
            ds = (pn * (dp - jnp.sum(pn * dp, axis=1, keepdims=True))).astype(BF16)
            dq_out = jnp.where(hmask, _dot(ds, km_ref[0]) * 0.125, dq_out)
            dkm = dkm + _dot_tn(ds, qs)
            dvm = dvm + _dot_tn(pn, dom)
        dq_ref[...] = dq_out.astype(BF16)
        _acc(dkm_ref, dkm[None], first)
        _acc(dvm_ref, dvm[None], first)

    return pl.pallas_call(
        body, name="memattn_bwd", grid=(Bl, 2, nq),
        in_specs=[BS((tq, LANES), lambda b, p, i: (b * nq + i, qc + p)),
                  BS((1, Mt, LANES), lambda b, p, i: (b, 0, p)),
                  BS((1, Mt, LANES), lambda b, p, i: (b, 0, 2 + p)),
                  BS((tq, LANES), lambda b, p, i: (b * nq + i, p))],
        out_specs=[BS((tq, LANES), lambda b, p, i: (b * nq + i, p)),
                   BS((1, Mt, LANES), lambda b, p, i: (b, 0, p)),
                   BS((1, Mt, LANES), lambda b, p, i: (b, 0, p))],
        out_shape=[jax.ShapeDtypeStruct((T, M_W), BF16), jax.ShapeDtypeStruct((Bl, Mt, M_W), F32),
                   jax.ShapeDtypeStruct((Bl, Mt, M_W), F32)],
        compiler_params=_cp(("arbitrary", "arbitrary", "arbitrary")),
    )(proj, kv, kv, dym)


def _memkv_bwd(dkm, dvm, memn, mem, g_mem, w_kv):
    Bl, Mt, D = mem.shape

    def body(dk_ref, dv_ref, mn_ref, m_ref, g_ref, w_ref, dw_ref, dg_ref):
        first = pl.program_id(0) == 0
        dk = dk_ref[0].astype(BF16)
        dv = dv_ref[0].astype(BF16)
        mn = mn_ref[0]
        dmn = _dot_nt(dk, w_ref[:, 0:M_W]) + _dot_nt(dv, w_ref[:, M_W:])
        _, dg = _rms_bwd(m_ref[0], g_ref[...], dmn)
        _acc(dg_ref, dg, first)

        @pl.when(first)
        def _():
            dw_ref[...] = jnp.zeros_like(dw_ref)

        dw_ref[:, 0:M_W] += _dot_tn(mn, dk)
        dw_ref[:, M_W:] += _dot_tn(mn, dv)

    return pl.pallas_call(
        body, name="memkv_bwd", grid=(Bl,),
        in_specs=[BS((1, Mt, M_W), lambda b: (b, 0, 0)), BS((1, Mt, M_W), lambda b: (b, 0, 0)),
                  BS((1, Mt, D), lambda b: (b, 0, 0)), BS((1, Mt, D), lambda b: (b, 0, 0)),
                  BS((1, D), lambda b: (0, 0)), BS((D, 2 * M_W), lambda b: (0, 0))],
        out_specs=[BS((D, 2 * M_W), lambda b: (0, 0)), BS((1, D), lambda b: (0, 0))],
        out_shape=[jax.ShapeDtypeStruct((D, 2 * M_W), F32), jax.ShapeDtypeStruct((1, D), F32)],
        compiler_params=_cp(("arbitrary",)),
    )(dkm, dvm, memn, mem, g_mem, w_kv)


def _fox_bwd(proj, dyb, lse, bq, bk, Bl, S):
    T = Bl * S
    nq = S // Q_BLK
    nb = S // LANES
    qc, kc, vc = 768 // LANES, 1152 // LANES, 1536 // LANES

    def body(q_ref, k_ref, v_ref, do_ref, lse_ref, bq_ref, bk_ref,
             dq_ref, dk_ref, dv_ref, dcr_ref, ka_ref, dka_ref, dva_ref):
        p = pl.program_id(1)
        lane_s = _iota((S, LANES), 1)
        lane = _iota((Q_BLK, LANES), 1)
        sub = _iota((8, LANES), 0)
        tri = _iota((Q_BLK, Q_BLK), 1) <= _iota((Q_BLK, Q_BLK), 0)
        k = k_ref[...]
        for hh in range(2):
            data = (lane_s < HEAD) if hh == 0 else (lane_s >= HEAD)
            ka_ref[hh] = jnp.where(data, k, bk_ref[0, hh])
        dka_ref[...] = jnp.zeros_like(dka_ref)
        dva_ref[...] = jnp.zeros_like(dva_ref)

        @pl.when(p == 0)
        def _():
            dcr_ref[...] = jnp.zeros_like(dcr_ref)

        def add_colsums(ds, first_blk, h):
            cs = _colsum(ds)
            for jb in range(ds.shape[1] // LANES):
                dcr_ref[0, first_blk + jb] += jnp.where(sub == h, cs[:, jb * LANES:(jb + 1) * LANES], 0.0)

        for i in range(nq):
            r0 = i * Q_BLK
            r1 = r0 + Q_BLK
            q = q_ref[r0:r1, :]
            do = do_ref[r0:r1, :]
            lse_b = lse_ref[0, r0:r1, :]
            dq_out = jnp.zeros((Q_BLK, LANES), F32)
            for hh in range(2):
                hmask = (lane < HEAD) if hh == 0 else (lane >= HEAD)
                h = 2 * p + hh
                qs = jnp.where(hmask, q * 0.125, jnp.zeros_like(q))
                qa = jnp.where(hmask, q * 0.125, bq_ref[0, hh, r0:r1, :])
                dob = jnp.where(hmask, do, 0.0).astype(BF16)
                lse_h = jnp.sum(jnp.where(lane == hh * HEAD, lse_b, 0.0), axis=1, keepdims=True)
                pd = jnp.where(tri, jnp.exp(_dot_nt(qa, ka_ref[hh, r0:r1, :]) - lse_h), 0.0)
                dpd = _dot_nt(dob, v_ref[r0:r1, :])
                delta = jnp.sum(pd * dpd, axis=1, keepdims=True)
                psum = jnp.sum(pd, axis=1, keepdims=True)
                if i:
                    pf = jnp.exp(_dot_nt(qa, ka_ref[hh, 0:r0, :]) - lse_h)
                    dpf = _dot_nt(dob, v_ref[0:r0, :])
                    delta = delta + jnp.sum(pf * dpf, axis=1, keepdims=True)
                    psum = psum + jnp.sum(pf, axis=1, keepdims=True)
                delta = delta / psum
                dsd = pd * (dpd - delta)
                add_colsums(dsd, r0 // LANES, h)
                dsd = dsd.astype(BF16)
                dq_h = _dot(dsd, k_ref[r0:r1, :])
                dka_ref[r0:r1, :] += _dot_tn(dsd, qs)
                dva_ref[r0:r1, :] += _dot_tn(pd, dob)
                if i:
                    dsf = pf * (dpf - delta)
                    add_colsums(dsf, 0, h)
                    dsf = dsf.astype(BF16)
                    dq_h = dq_h + _dot(dsf, k_ref[0:r0, :])
                    dka_ref[0:r0, :] += _dot_tn(dsf, qs)
                    dva_ref[0:r0, :] += _dot_tn(pf, dob)
                dq_out = jnp.where(hmask, dq_h * 0.125, dq_out)
            dq_ref[r0:r1, :] = dq_out.astype(BF16)
        dk_ref[...] = dka_ref[...].astype(BF16)
        dv_ref[...] = dva_ref[...].astype(BF16)

    seq = lambda c0: BS((S, LANES), lambda b, p: (b, c0 + p))
    pair = BS((1, 2, S, LANES), lambda b, p: (b, p, 0, 0))
    rowblk = BS((1, nb, 8, LANES), lambda b, p: (b, 0, 0, 0))
    return pl.pallas_call(
        body, name="fox_bwd", grid=(Bl, 3),
        in_specs=[seq(qc), seq(kc), seq(vc), seq(0), BS((1, S, LANES), lambda b, p: (p, b, 0)), pair, pair],
        out_specs=[seq(0), seq(0), seq(0), rowblk],
        out_shape=[jax.ShapeDtypeStruct((T, B_W), BF16)] * 3 + [jax.ShapeDtypeStruct((Bl, nb, 8, LANES), F32)],
        scratch_shapes=[pltpu.VMEM((2, S, LANES), BF16), pltpu.VMEM((S, LANES), F32), pltpu.VMEM((S, LANES), F32)],
        compiler_params=_cp(("arbitrary", "arbitrary")),
    )(proj, proj, proj, dyb, lse, bq, bk)


def _gate_bwd(dc_row, fl_row):
    Bl, nb, _, _ = dc_row.shape

    def body(dc_ref, fl_ref, o_ref):
        lane = _iota((8, LANES), 1)

        carry = jnp.zeros((8, 1), F32)
        for j in reversed(range(nb)):
            r = -dc_ref[0, j]
            for k in (1, 2, 4, 8, 16, 32, 64):
                r = r + jnp.where(lane < LANES - k, pltpu.roll(r, LANES - k, 1), 0.0)
            total = jnp.sum(jnp.where(lane == 0, r, 0.0), axis=1, keepdims=True)
            dfl = (r + carry) * _sigmoid(-fl_ref[0, j])
            carry = carry + total
            o_ref[0, j * LANES:(j + 1) * LANES, :] = jnp.concatenate(
                [dfl, jnp.zeros((LANES - 8, LANES), F32)], axis=0).T

    rowblk = BS((1, nb, 8, LANES), lambda b: (b, 0, 0, 0))
    return pl.pallas_call(
        body, name="gate_bwd", grid=(Bl,),
        in_specs=[rowblk, rowblk],
        out_specs=BS((1, nb * LANES, LANES), lambda b: (b, 0, 0)),
        out_shape=jax.ShapeDtypeStruct((Bl, nb * LANES, LANES), F32),
        compiler_params=_cp(("arbitrary",)),
    )(dc_row, fl_row)


def _inproj_bwd(pieces, x2d, dx1, g_pre, w_in_p, tm):
    T, D = x2d.shape
    ns, _, dsh = w_in_p.shape

    def body(*refs):
        piece_refs = refs[:7]
        x_ref, dx1_ref, g_ref, w_ref, gx_ref, dg_ref, dbf_ref, dp_ref = refs[7:]
        first = pl.program_id(0) == 0
        _put_dproj(dp_ref, piece_refs)
        dh = jnp.concatenate([_dot(dp_ref[...], w_ref[s]) for s in range(ns)], axis=1)
        dxa, dg = _rms_bwd(x_ref[...], g_ref[...], dh)
        gx_ref[...] = dx1_ref[...] + dxa
        _acc(dg_ref, dg, first)
        _acc(dbf_ref, _colsum(piece_refs[5][...]), first)

    row = lambda w: BS((tm, w), lambda i: (i, 0))
    return pl.pallas_call(
        body, name="inproj_bwd", grid=(T // tm,),
        in_specs=[row(w) for _, w in DPROJ_PIECES] + [row(D), row(D), BS((1, D), lambda i: (0, 0)),
                                                      BS((ns, P_COLS, dsh), lambda i: (0, 0, 0))],
        out_specs=[row(D), BS((1, D), lambda i: (0, 0)), BS((1, LANES), lambda i: (0, 0))],
        out_shape=[jax.ShapeDtypeStruct((T, D), F32), jax.ShapeDtypeStruct((1, D), F32),
                   jax.ShapeDtypeStruct((1, LANES), F32)],
        scratch_shapes=[pltpu.VMEM((tm, P_COLS), BF16)],
        compiler_params=_cp(("arbitrary",)),
    )(*pieces, x2d, dx1, g_pre, w_in_p)


def _local_step(x, mem, target, W, P, reduce=None):
    Bl, S, D = x.shape
    T = Bl * S
    tm = min(512, T)
    x2d = x.reshape(T, D)
    t2d = target.reshape(T, D)
    vec = lambda a: a.reshape(1, -1)
    bf_row = jnp.pad(P["b_f"].reshape(1, -1), ((0, 0), (0, LANES - N_FOX_HEADS)))
    tril = jnp.tril(jnp.ones((CHUNK, CHUNK), bool))
    ws_tril = jnp.where(tril[None], P["w_s"][0], 0.0).astype(BF16)
    bs_full = jnp.repeat(P["b_s"][0].T, HEAD, axis=1)
    g_pre, g_sgu = vec(P["g_pre_mix"]), vec(P["g_sgu"])
    ga, gb, gm = vec(P["g_out_a"]), vec(P["g_out_b"]), vec(P["g_out_m"])
    g_mem, g_post, g_pre2, g_post2 = vec(P["g_mem"]), vec(P["g_post_mix"]), vec(P["g_pre_ffn"]), vec(P["g_post_ffn"])

    h, proj, flog = _inproj_fwd(x2d, g_pre, W["w_in"], tm)
    bq, bk, fl_row = _gate_fwd(flog.reshape(Bl, S, LANES), bf_row)
    ya = _sgu_fwd(proj, g_sgu, ws_tril, bs_full, tm)
    yb, lse = _fox_fwd(proj, bq, bk, Bl, S)
    memn, kv = _memkv_fwd(mem, g_mem, W["w_mem_kv"])
    ym = _memattn_fwd(proj, kv, Bl, S, min(512, S))
    y, o, x1, h2 = _outproj_fwd(ya, yb, ym, x2d, ga, gb, gm, g_post, g_pre2, W["w_out"], tm)
    gs, us, dff, dx2, dg_post2, loss = _ffn_fwd(h2, x1, t2d, W["w_gate"], W["w_up"], W["w_down"], g_post2, tm)

    dh2, d_w_gate, d_w_up, d_w_down = _ffn_bwd(dff, h2, gs, us, W["w_gate"], W["w_up"], W["w_down"], min(1024, T))
    ffn = [d_w_gate, d_w_up, d_w_down]
    if reduce is not None:
        pending, _ = reduce.begin("ffn", ffn)
    dx1, do, dya, dyb, dym, dga, dgb, dgm, dg_post, dg_pre2 = _outproj_bwd(
        dh2, x1, dx2, o, ya, yb, ym, ga, gb, gm, g_post, g_pre2, W["w_out"], tm)
    if reduce is not None:
        ffn, (do, dya, dyb, dym) = reduce.finish("ffn", pending, (do, dya, dyb, dym))
    d_w_out = _mm_tn(y, do, "dw_out", 1024)
    dzu, dzv, dws, dbs_cols, dg_sgu = _sgu_bwd(proj, dya, g_sgu, ws_tril, bs_full, tm)
    dqm, dkm, dvm = _memattn_bwd(proj, kv, dym, Bl, S, min(512, S))
    d_w_kv, dg_mem = _memkv_bwd(dkm, dvm, memn, mem, g_mem, W["w_mem_kv"])
    mid = [d_w_kv, d_w_out]
    dq, dk, dv, dc_row = _fox_bwd(proj, dyb, lse, bq, bk, Bl, S)
    if reduce is not None:
        done = reduce.apply(BIG[3:], ffn)
        pending, after = reduce.begin("mid", mid, (dc_row,) + done)
        dc_row = after[0]
    dfl = _gate_bwd(dc_row, fl_row).reshape(T, LANES)
    pieces = (dzu, dzv, dq, dk, dv, dfl, dqm)
    grad_x, dg_pre, dbf = _inproj_bwd(pieces, x2d, dx1, g_pre, W["w_in"], tm)
    if reduce is not None:
        mid, (dfl,) = reduce.finish("mid", pending, (dfl,))
        pieces = (dzu, dzv, dq, dk, dv, dfl, dqm)
    d_w_in = _dw_in(pieces, h, W["w_in"].shape[0], 1024)
    if reduce is None:
        big = dict(zip(BIG, [d_w_in] + mid + ffn))
    else:
        done = reduce.apply(BIG[1:3], mid)
        big = {"w_in": reduce.begin("in", [d_w_in], done)[0]}
    small = {"g_pre_mix": dg_pre, "b_f": dbf[:, :N_FOX_HEADS], "g_sgu": dg_sgu, "w_s": dws, "b_s": dbs_cols[:, :N_FOX_HEADS].T,
             "g_out_a": dga, "g_out_b": dgb, "g_out_m": dgm, "g_mem": dg_mem, "g_post_mix": dg_post,
             "g_pre_ffn": dg_pre2, "g_post_ffn": dg_post2, "loss": loss[:, :1]}
    return grad_x.reshape(Bl, S, D), big, small


def _place():
    return lax.axis_index("x"), lax.axis_index("y"), lax.axis_index("c")


def _exchange_on_sequencer(srcs, own_full, name, collective_id):
    n = len(srcs)

    def body(*refs):
        src, dst = refs[:n], refs[n:2 * n]
        lsem, isend, irecv, dsend, drecv = refs[2 * n:]
        x, y, c = _place()
        oc = 1 - c
        s_me = 2 * x + y
        sib = (x, y, oc)
        chips = [(1 - x, y), (x, 1 - y), (1 - x, 1 - y)]
        barrier = pltpu.get_barrier_semaphore()
        for dev in [(cx, cy, c) for cx, cy in chips] + [sib]:
            pl.semaphore_signal(barrier, inc=1, device_id=dev, device_id_type=MESH)
        pl.semaphore_wait(barrier, 4)

        def remote(a, b, ssem, rsem, dev):
            return pltpu.make_async_remote_copy(src_ref=a, dst_ref=b, send_sem=ssem, recv_sem=rsem,
                                                device_id=dev, device_id_type=MESH)

        sends, local = [], []
        for w in range(n):
            for j, (cx, cy) in enumerate(chips):
                half = src[w].at[c] if own_full else src[w].at[2 * cx + cy]
                cp = remote(half, dst[w].at[s_me, c], isend.at[w, j], irecv.at[w, j], (cx, cy, c))
                cp.start()
                sends.append(cp)
            if own_full:
                cp = remote(src[w], dst[w].at[s_me], dsend.at[w, 3], drecv.at[w, 3], sib)
            else:
                cp = remote(src[w].at[s_me], dst[w].at[s_me, c], dsend.at[w, 3], drecv.at[w, 3], sib)
                loc = pltpu.make_async_copy(src[w].at[s_me], dst[w].at[s_me, c], lsem.at[w])
                loc.start()
                local.append(loc)
            cp.start()
            sends.append(cp)
        for w in range(n):
            for j, (cx, cy) in enumerate(chips):
                landed = dst[w].at[2 * cx + cy, c]
                remote(landed, landed, isend.at[w, j], irecv.at[w, j], (cx, cy, c)).wait_recv()
                cp = remote(landed, landed, dsend.at[w, j], drecv.at[w, j], sib)
                cp.start()
                sends.append(cp)
        for w in range(n):
            for j, (cx, cy) in enumerate(chips):
                landed = dst[w].at[2 * cx + cy, oc]
                remote(landed, landed, dsend.at[w, j], drecv.at[w, j], sib).wait_recv()
            landed = dst[w].at[s_me] if own_full else dst[w].at[s_me, oc]
            remote(landed, landed, dsend.at[w, 3], drecv.at[w, 3], sib).wait_recv()
        for cp in sends:
            cp.wait_send()
        for loc in local:
            loc.wait()

    return pl.kernel(
        body, out_type=[jax.ShapeDtypeStruct((4, 2) + s.shape[1:], s.dtype) for s in srcs],
        mesh=plsc.ScalarSubcoreMesh(axis_name="sequencer", num_cores=1), name=name,
        scratch_types=[pltpu.SemaphoreType.DMA((n,)), pltpu.SemaphoreType.DMA((n, 3)), pltpu.SemaphoreType.DMA((n, 3)),
                       pltpu.SemaphoreType.DMA((n, 4)), pltpu.SemaphoreType.DMA((n, 4))],
        compiler_params=pltpu.CompilerParams(collective_id=collective_id),
    )(*srcs)


def _sibling_swap(grads, name, collective_id):
    n = len(grads)

    def body(*refs):
        g, theirs = refs[:n], refs[n:2 * n]
        ssem, rsem = refs[2 * n:]
        x, y, c = _place()
        sib = (x, y, 1 - c)
        barrier = pltpu.get_barrier_semaphore()
        pl.semaphore_signal(barrier, inc=1, device_id=sib, device_id_type=MESH)
        pl.semaphore_wait(barrier, 1)
        cps = []
        for w in range(n):
            cp = pltpu.make_async_remote_copy(src_ref=g[w].at[:, 1 - c], dst_ref=theirs[w], send_sem=ssem.at[w],
                                              recv_sem=rsem.at[w], device_id=sib, device_id_type=MESH)
            cp.start()
            cps.append(cp)
        for cp in cps:
            cp.wait()

    return pl.kernel(
        body, out_type=[jax.ShapeDtypeStruct((4,) + g.shape[2:], g.dtype) for g in grads],
        mesh=plsc.ScalarSubcoreMesh(axis_name="sequencer", num_cores=1), name=name,
        scratch_types=[pltpu.SemaphoreType.DMA((n,)), pltpu.SemaphoreType.DMA((n,))],
        compiler_params=pltpu.CompilerParams(collective_id=collective_id),
    )(*grads)


def _add_pair(core, g, theirs, name):
    _, _, hr, C = g.shape

    def body(core_ref, g_ref, t_ref, o_ref):
        o_ref[0] = (g_ref[0, 0].astype(F32) + t_ref[0].astype(F32)).astype(BF16)

    blk = BS((1, hr, C), lambda s, core_ref: (s, 0, 0))
    return pl.pallas_call(
        body, name=name,
        grid_spec=pltpu.PrefetchScalarGridSpec(
            num_scalar_prefetch=1, grid=(4,),
            in_specs=[BS((1, 1, hr, C), lambda s, core_ref: (s, core_ref[0], 0, 0)), blk], out_specs=blk),
        out_shape=jax.ShapeDtypeStruct(theirs.shape, BF16), compiler_params=_cp(("arbitrary",)))(core, g, theirs)


def _sum_chips(r, name):
    _, _, hr, C = r.shape

    def body(r_ref, o_ref):
        o_ref[...] = ((r_ref[0, 0].astype(F32) + r_ref[1, 0].astype(F32)) + r_ref[2, 0].astype(F32)) + r_ref[3, 0].astype(F32)

    return pl.pallas_call(body, name=name, grid=(2,), in_specs=[BS((4, 1, hr, C), lambda h: (0, h, 0, 0))],
                          out_specs=BS((hr, C), lambda h: (h, 0)), out_shape=jax.ShapeDtypeStruct((2 * hr, C), F32),
                          compiler_params=_cp(("arbitrary",)))(r)


class _Reducer:
    IDS = {"ffn": (4, 5), "mid": (6, 7), "in": (8, 9)}

    def __init__(self, core, apply):
        self.core = core
        self.apply = apply

    def begin(self, tag, grads, after=()):
        grads, after = lax.optimization_barrier((list(grads), after))
        g4 = [g.reshape(4, 2, -1, g.shape[-1]) for g in grads]
        return (g4, _sibling_swap(g4, "swap_" + tag, self.IDS[tag][0])), after

    def finish(self, tag, pending, hold):
        g4, theirs = pending
        sums = [_add_pair(self.core, g, t, "chip_sum_%s_%d" % (tag, k)) for k, (g, t) in enumerate(zip(g4, theirs))]
        sums, hold = lax.optimization_barrier((sums, hold))
        return _exchange_on_sequencer(sums, False, "scatter_" + tag, self.IDS[tag][1]), hold


def _small_allreduce(part):
    R = part.shape[0]
    rs = R // 8
    masks = [(mx, my, mc) for mx in (0, 1) for my in (0, 1) for mc in (0, 1)][1:]

    def body(p_ref, o_ref, buf_ref, s1, r1, s2, r2):
        x, y, c = _place()
        d = 4 * x + 2 * y + c
        mine = pl.ds(pl.multiple_of(d * rs, 8), rs)
        peers = [((x + mx) % 2, (y + my) % 2, (c + mc) % 2) for mx, my, mc in masks]
        first, second = [], []
        for k, (px, py, pc) in enumerate(peers):
            theirs = pl.ds(pl.multiple_of((4 * px + 2 * py + pc) * rs, 8), rs)
            cp = pltpu.make_async_remote_copy(src_ref=p_ref.at[theirs, :], dst_ref=buf_ref.at[d], send_sem=s1.at[k],
                                              recv_sem=r1.at[k], device_id=(px, py, pc), device_id_type=MESH)
            cp.start()
            first.append(cp)
        buf_ref[d] = p_ref[mine, :]
        for k, (px, py, pc) in enumerate(peers):
            slot = buf_ref.at[4 * px + 2 * py + pc]
            pltpu.make_async_remote_copy(src_ref=slot, dst_ref=slot, send_sem=s1.at[k], recv_sem=r1.at[k],
                                         device_id=(px, py, pc), device_id_type=MESH).wait_recv()
        total = buf_ref[0]
        for k in range(1, 8):
            total = total + buf_ref[k]
        o_ref[mine, :] = total
        for k, (px, py, pc) in enumerate(peers):
            cp = pltpu.make_async_remote_copy(src_ref=o_ref.at[mine, :], dst_ref=o_ref.at[mine, :], send_sem=s2.at[k],
                                              recv_sem=r2.at[k], device_id=(px, py, pc), device_id_type=MESH)
            cp.start()
            second.append(cp)
        for k, (px, py, pc) in enumerate(peers):
            rows = o_ref.at[pl.ds(pl.multiple_of((4 * px + 2 * py + pc) * rs, 8), rs), :]
            pltpu.make_async_remote_copy(src_ref=rows, dst_ref=rows, send_sem=s2.at[k], recv_sem=r2.at[k],
                                         device_id=(px, py, pc), device_id_type=MESH).wait_recv()
        for cp in first + second:
            cp.wait_send()

    vm = pl.BlockSpec(memory_space=pltpu.VMEM)
    return pl.pallas_call(
        body, name="small_allreduce", in_specs=[vm], out_specs=vm, out_shape=jax.ShapeDtypeStruct(part.shape, F32),
        scratch_shapes=[pltpu.VMEM((8, rs, LANES), F32)] + [pltpu.SemaphoreType.DMA((7,))] * 4,
    )(part)


def _adamw(w, g, m, v, name):
    R, C = w.shape
    summed = g.ndim == 4
    if summed:
        tr = R // 2
    else:
        tr = R if R * C * 4 <= (1 << 21) else R // 2
        if tr % 8:
            tr = R
    c1 = 1.0 / (1.0 - ADAM_B1 ** ADAM_STEP)
    c2 = 1.0 / (1.0 - ADAM_B2 ** ADAM_STEP)

    def body(w_ref, g_ref, m_ref, v_ref, *outs):
        if summed:
            g_ = ((g_ref[0, 0].astype(F32) + g_ref[1, 0].astype(F32)) + g_ref[2, 0].astype(F32)) + g_ref[3, 0].astype(F32)
            outs[0][...] = g_
        else:
            g_ = g_ref[...]
        d_ref, mo_ref, vo_ref = outs[-3:]
        m_ = ADAM_B1 * m_ref[...] + (1.0 - ADAM_B1) * g_
        v_ = ADAM_B2 * v_ref[...] + (1.0 - ADAM_B2) * (g_ * g_)
        mo_ref[...] = m_
        vo_ref[...] = v_
        d_ref[...] = -ADAM_LR * ((m_ * c1) / (jnp.sqrt(v_ * c2) + ADAM_EPS) + ADAM_WD * w_ref[...])

    blk = BS((tr, C), lambda i: (i, 0))
    g_blk = BS((4, 1, tr, C), lambda i: (0, i, 0, 0)) if summed else blk
    nout = 4 if summed else 3
    return pl.pallas_call(body, name=name, grid=(R // tr,), in_specs=[blk, g_blk, blk, blk], out_specs=[blk] * nout,
                          out_shape=[jax.ShapeDtypeStruct((R, C), F32)] * nout,
                          compiler_params=_cp(("arbitrary",)))(w, g, m, v)


SMALL = ("g_pre_mix", "b_f", "g_sgu", "w_s", "b_s", "g_out_a", "g_out_b", "g_out_m", "g_mem", "g_post_mix",
         "g_pre_ffn", "g_post_ffn")
BIG = ("w_in", "w_mem_kv", "w_out", "w_gate", "w_up", "w_down")
TRANSPOSED = ("w_in", "w_gate", "w_up")
WEIGHTS = ("g_pre_mix", "w_in", "b_f", "g_sgu", "w_s", "b_s", "g_out_a", "g_out_b", "g_out_m", "g_mem", "w_mem_kv",
           "w_out", "g_post_mix", "g_pre_ffn", "w_gate", "w_up", "w_down", "g_post_ffn")


def _rows_of(n):
    return -(-n // (8 * LANES)) * 8


def _pack(parts):
    tiles = []
    for a in parts:
        flat = a.reshape(-1).astype(F32)
        rows = _rows_of(flat.shape[0])
        tiles.append(jnp.pad(flat, (0, rows * LANES - flat.shape[0])).reshape(rows, LANES))
    total = sum(t.shape[0] for t in tiles)
    pad = -total % 64
    if pad:
        tiles.append(jnp.zeros((pad, LANES), F32))
    return jnp.concatenate(tiles, axis=0)


def _unpack(packed, shapes):
    out, r = [], 0
    for shp in shapes:
        n = 1
        for s in shp:
            n *= s
        rows = _rows_of(n)
        out.append(packed[r:r + rows].reshape(-1)[:n].reshape(shp))
        r += rows
    return out


def kernel(x, mem, g_pre_mix, w_in, b_f, g_sgu, w_s, b_s, g_out_a, g_out_b, g_out_m, g_mem, w_mem_kv, w_out, g_post_mix, g_pre_ffn, w_gate, w_up, w_down, g_post_ffn, loss_target, m_g_pre_mix, m_w_in, m_b_f, m_g_sgu, m_w_s, m_b_s, m_g_out_a, m_g_out_b, m_g_out_m, m_g_mem, m_w_mem_kv, m_w_out, m_g_post_mix, m_g_pre_ffn, m_w_gate, m_w_up, m_w_down, m_g_post_ffn, v_g_pre_mix, v_w_in, v_b_f, v_g_sgu, v_w_s, v_b_s, v_g_out_a, v_g_out_b, v_g_out_m, v_g_mem, v_w_mem_kv, v_w_out, v_g_post_mix, v_g_pre_ffn, v_w_gate, v_w_up, v_w_down, v_g_post_ffn):
    Wt = dict(g_pre_mix=g_pre_mix, w_in=w_in, b_f=b_f, g_sgu=g_sgu, w_s=w_s, b_s=b_s, g_out_a=g_out_a, g_out_b=g_out_b,
              g_out_m=g_out_m, g_mem=g_mem, w_mem_kv=w_mem_kv, w_out=w_out, g_post_mix=g_post_mix, g_pre_ffn=g_pre_ffn,
              w_gate=w_gate, w_up=w_up, w_down=w_down, g_post_ffn=g_post_ffn)
    Mo = dict(g_pre_mix=m_g_pre_mix, w_in=m_w_in, b_f=m_b_f, g_sgu=m_g_sgu, w_s=m_w_s, b_s=m_b_s, g_out_a=m_g_out_a,
              g_out_b=m_g_out_b, g_out_m=m_g_out_m, g_mem=m_g_mem, w_mem_kv=m_w_mem_kv, w_out=m_w_out,
              g_post_mix=m_g_post_mix, g_pre_ffn=m_g_pre_ffn, w_gate=m_w_gate, w_up=m_w_up, w_down=m_w_down,
              g_post_ffn=m_g_post_ffn)
    Vo = dict(g_pre_mix=v_g_pre_mix, w_in=v_w_in, b_f=v_b_f, g_sgu=v_g_sgu, w_s=v_w_s, b_s=v_b_s, g_out_a=v_g_out_a,
              g_out_b=v_g_out_b, g_out_m=v_g_out_m, g_mem=v_g_mem, w_mem_kv=v_w_mem_kv, w_out=v_w_out,
              g_post_mix=v_g_post_mix, g_pre_ffn=v_g_pre_ffn, w_gate=v_w_gate, w_up=v_w_up, w_down=v_w_down,
              g_post_ffn=v_g_post_ffn)

    gap = P_COLS - IN_COLS

    def to_kernel(n, w):
        if n in TRANSPOSED:
            w = w.T
        if n == "w_in":
            w = jnp.pad(w[:F_END], ((0, P_COLS - F_END), (0, 0))) + jnp.pad(w[F_END:], ((F_END + gap, 0), (0, 0)))
        return w

    def ungroup(g):
        return jnp.pad(g[:F_END], ((0, IN_COLS - F_END), (0, 0))) + jnp.pad(g[F_END + gap:], ((F_END, 0), (0, 0)))

    shards = {n: to_kernel(n, Wt[n][0]) for n in BIG}
    srcs = [shards[n].astype(BF16).reshape(2, shards[n].shape[0] // 2, shards[n].shape[1]) for n in BIG]
    fulls = (_exchange_on_sequencer(srcs[:1], True, "gather_w_in", 1)
             + _exchange_on_sequencer(srcs[1:3], True, "gather_kv_out", 2)
             + _exchange_on_sequencer(srcs[3:], True, "gather_ffn", 3))
    W = {}
    for n, f in zip(BIG, fulls):
        _, _, hr, C = f.shape
        W[n] = f.reshape(8 * hr, C) if n in ("w_mem_kv", "w_out") else f.reshape(4, 2 * hr, C)

    P = {n: Wt[n] for n in SMALL}
    grads, deltas, new_m, new_v = {}, {}, {}, {}

    def apply(names, landed):
        for n, r in zip(names, landed):
            wmv = [a[n][0].T if n in TRANSPOSED else a[n][0] for a in (Wt, Mo, Vo)]
            if n == "w_in":
                g = ungroup(_sum_chips(r, "sum_chips_" + n))
                g, d, m1, v1 = (g,) + tuple(_adamw(wmv[0], g, wmv[1], wmv[2], "adamw_" + n))
            else:
                g, d, m1, v1 = _adamw(wmv[0], r, wmv[1], wmv[2], "adamw_" + n)
            if n in TRANSPOSED:
                g, d, m1, v1 = g.T, d.T, m1.T, v1.T
            grads[n], deltas[n], new_m[n], new_v[n] = g[None], d[None], m1[None], v1[None]
        return tuple(deltas[n] for n in names)

    core = lax.axis_index("c").astype(jnp.int32).reshape(1)
    reducer = _Reducer(core, apply)
    grad_x, pending, small = _local_step(x, mem, loss_target, W, P, reducer)

    total = _small_allreduce(_pack([small[n] for n in SMALL] + [small["loss"]]))
    landed, (total,) = reducer.finish("in", pending["w_in"], (total,))
    apply(BIG[:1], landed)

    slot = [jnp.zeros((1, 1), F32)]
    shapes = [Wt[n].shape for n in SMALL] + [(1, 1)]
    d, m1, v1 = _adamw(_pack([Wt[n] for n in SMALL] + slot), total, _pack([Mo[n] for n in SMALL] + slot),
                       _pack([Vo[n] for n in SMALL] + slot), "adamw_small")
    g_s, d_s, m_s, v_s = _unpack(total, shapes), _unpack(d, shapes), _unpack(m1, shapes), _unpack(v1, shapes)
    for k, n in enumerate(SMALL):
        grads[n], deltas[n], new_m[n], new_v[n] = g_s[k], d_s[k], m_s[k], v_s[k]
    loss = g_s[-1][0, 0]

    return (loss, grad_x, *[grads[n] for n in WEIGHTS], *[deltas[n] for n in WEIGHTS],
            *[new_m[n] for n in WEIGHTS], *[new_v[n] for n in WEIGHTS])
```

```python
import functools

import jax
import jax.numpy as jnp
from jax import lax
from jax.experimental import pallas as pl
from jax.experimental.pallas import tpu as pltpu
from jax.experimental.pallas import tpu_sc as plsc

F32 = jnp.float32
BF16 = jnp.bfloat16
EPS = 1e-6
NEG = -1e30
HEAD = 64
A_W, B_W, M_W = 384, 384, 256
N_FOX_HEADS = 6
CHUNK = 128
IN_COLS = 2 * A_W + 3 * B_W + N_FOX_HEADS + M_W
P_MAIN = 2 * A_W + 3 * B_W + M_W
P_COLS = P_MAIN + 128
F_END = 2 * A_W + 3 * B_W + N_FOX_HEADS
LANES = 128
Q_BLK, K_BLK = 256, 128
ROW_SPLIT = 4
ADAM_LR, ADAM_B1, ADAM_B2, ADAM_EPS, ADAM_WD, ADAM_STEP = 0.001, 0.9, 0.999, 1e-08, 0.01, 10
VMEM_LIMIT = 56 * 1024 * 1024
MESH = pl.DeviceIdType.MESH
ANY = pl.BlockSpec(memory_space=pl.ANY)
BS = pl.BlockSpec


def _cp(sem=None):
    return pltpu.CompilerParams(dimension_semantics=sem, vmem_limit_bytes=VMEM_LIMIT)


def _iota(shape, dim):
    return lax.broadcasted_iota(jnp.int32, shape, dim)


def _dot(a, b):
    return jnp.dot(a.astype(BF16), b.astype(BF16), preferred_element_type=F32)


def _dot_nt(a, b):
    return lax.dot_general(a.astype(BF16), b.astype(BF16), (((1,), (1,)), ((), ())), preferred_element_type=F32)


def _dot_tn(a, b):
    return lax.dot_general(a.astype(BF16), b.astype(BF16), (((0,), (0,)), ((), ())), preferred_element_type=F32)


def _rms(x, g):
    return x * lax.rsqrt(jnp.mean(x * x, axis=-1, keepdims=True) + EPS) * g


def _rms_bwd(x, g, dy):
    r = lax.rsqrt(jnp.mean(x * x, axis=-1, keepdims=True) + EPS)
    xr = x * r
    gd = dy * g
    m = jnp.mean(gd * xr, axis=-1, keepdims=True)
    return (gd - xr * m) * r, _colsum(dy * xr)


def _gelu(x):
    return 0.5 * x * (1.0 + jnp.tanh(0.7978845608028654 * (x + 0.044715 * (x * x * x))))


def _sigmoid(x):
    return 1.0 / (1.0 + jnp.exp(-x))


def _silu_mul(g, u):
    return g * _sigmoid(g) * u


def _logsig(x):
    return jnp.minimum(x, 0.0) - jnp.log(1.0 + jnp.exp(-jnp.abs(x)))


def _colsum(x):
    return jnp.sum(x, axis=0, keepdims=True)


def _acc(ref, val, first):
    @pl.when(first)
    def _():
        ref[...] = val

    @pl.when(jnp.logical_not(first))
    def _():
        ref[...] += val


def _inproj_fwd(x2d, g_pre, w_in_p, tm):
    T, D = x2d.shape
    CH = 768
    nchunk = P_COLS // CH
    ns, _, dsh = w_in_p.shape

    def body(x_ref, g_ref, w_ref, h_ref, proj_ref, fl_ref):
        h = _rms(x_ref[...], g_ref[...]).astype(BF16)
        h_ref[...] = h
        for n in range(nchunk):
            rows = slice(n * CH, (n + 1) * CH)
            r = _dot_nt(h[:, 0:dsh], w_ref[0, rows, :])
            for s in range(1, ns):
                r = r + _dot_nt(h[:, s * dsh:(s + 1) * dsh], w_ref[s, rows, :])
            if n < nchunk - 1:
                proj_ref[:, rows] = r.astype(BF16)
            else:
                fg = 1920 - n * CH
                proj_ref[:, n * CH:1920] = r[:, :fg].astype(BF16)
                fl_ref[...] = r[:, fg:fg + LANES]
                proj_ref[:, 1920:P_MAIN] = r[:, fg + LANES:].astype(BF16)

    return pl.pallas_call(
        body, name="inproj_fwd", grid=(T // tm,),
        in_specs=[BS((tm, D), lambda i: (i, 0)), BS((1, D), lambda i: (0, 0)),
                  BS((ns, P_COLS, dsh), lambda i: (0, 0, 0))],
        out_specs=[BS((tm, D), lambda i: (i, 0)), BS((tm, P_MAIN), lambda i: (i, 0)), BS((tm, LANES), lambda i: (i, 0))],
        out_shape=[jax.ShapeDtypeStruct((T, D), BF16), jax.ShapeDtypeStruct((T, P_MAIN), BF16),
                   jax.ShapeDtypeStruct((T, LANES), F32)],
        compiler_params=_cp(("arbitrary",)),
    )(x2d, g_pre, w_in_p)


def _gate_fwd(flog3, bf_row):
    Bl, S, _ = flog3.shape
    nb = S // LANES

    def body(f_ref, b_ref, bq_ref, bk_ref, fr_ref):
        row = _iota((LANES, LANES), 0)
        lane = _iota((LANES, LANES), 1)
        one = jnp.ones((LANES, LANES), BF16)
        zero = jnp.zeros((LANES, LANES), BF16)

        carry = jnp.zeros((1, LANES), F32)
        for j in range(nb):
            r0 = j * LANES
            fl = f_ref[0, pl.ds(r0, LANES), :] + b_ref[...]
            fr_ref[0, j] = fl.T[0:8, :]
            c = _logsig(fl)
            for k in (1, 2, 4, 8, 16, 32, 64):
                c = c + jnp.where(row >= k, pltpu.roll(c, k, 0), 0.0)
            total = _colsum(jnp.where(row == LANES - 1, c, 0.0))
            c = c + carry
            carry = carry + total
            for h in range(N_FOX_HEADS):
                col = jnp.sum(jnp.where(lane == h, c, 0.0), axis=1, keepdims=True)
                hi = col.astype(BF16)
                rest = col - hi.astype(F32)
                mid = rest.astype(BF16)
                lo = (rest - mid.astype(F32)).astype(BF16)
                base = _bias_lane(h)
                bq = jnp.where(lane == base, hi, jnp.where(lane == base + 1, mid, jnp.where(lane == base + 2, lo, zero)))
                bq = jnp.where((lane >= base + 3) & (lane < base + 6), one, bq)
                bk = jnp.where(lane == base + 3, -hi, jnp.where(lane == base + 4, -mid, jnp.where(lane == base + 5, -lo, zero)))
                bk = jnp.where((lane >= base) & (lane < base + 3), one, bk)
                bq_ref[0, h, pl.ds(r0, LANES), :] = bq
                bk_ref[0, h, pl.ds(r0, LANES), :] = bk

    slab = BS((1, N_FOX_HEADS, S, LANES), lambda b: (b, 0, 0, 0))
    return pl.pallas_call(
        body, name="gate_fwd", grid=(Bl,),
        in_specs=[BS((1, S, LANES), lambda b: (b, 0, 0)), BS((1, LANES), lambda b: (0, 0))],
        out_specs=[slab, slab, BS((1, nb, 8, LANES), lambda b: (b, 0, 0, 0))],
        out_shape=[jax.ShapeDtypeStruct((Bl, N_FOX_HEADS, S, LANES), BF16),
                   jax.ShapeDtypeStruct((Bl, N_FOX_HEADS, S, LANES), BF16),
                   jax.ShapeDtypeStruct((Bl, nb, 8, LANES), F32)],
        compiler_params=_cp(("arbitrary",)),
    )(flog3, bf_row)


def _bias_lane(h):
    return HEAD if h % 2 == 0 else 0


def _sgu_pre(zu, zv, g_sgu):
    return _gelu(zu), _rms(_gelu(zv), g_sgu)


def _sgu_fwd(proj, g_sgu, ws_tril, bs_full, tm):
    T = proj.shape[0]
    nch = tm // CHUNK

    def body(zu_ref, zv_ref, g_ref, ws_ref, b_ref, ya_ref):
        lane = _iota((CHUNK, LANES), 1)
        u, vn = _sgu_pre(zu_ref[...].astype(F32), zv_ref[...].astype(F32), g_ref[...])
        vn = vn.astype(BF16)
        for c in range(nch):
            rs = slice(c * CHUNK, (c + 1) * CHUNK)
            for j in range(3):
                cs = slice(j * LANES, (j + 1) * LANES)
                vp = vn[rs, cs]
                z = jnp.where(lane < HEAD, _dot(ws_ref[2 * j], vp), _dot(ws_ref[2 * j + 1], vp)) + b_ref[:, cs]
                ya_ref[rs, cs] = (u[rs, cs] * z).astype(BF16)

    return pl.pallas_call(
        body, name="sgu_fwd", grid=(T // tm,),
        in_specs=[BS((tm, A_W), lambda i: (i, 0)), BS((tm, A_W), lambda i: (i, 1)), BS((1, A_W), lambda i: (0, 0)),
                  BS((6, CHUNK, CHUNK), lambda i: (0, 0, 0)), BS((CHUNK, A_W), lambda i: (0, 0))],
        out_specs=BS((tm, A_W), lambda i: (i, 0)),
        out_shape=jax.ShapeDtypeStruct((T, A_W), BF16),
        compiler_params=_cp(("arbitrary",)),
    )(proj, proj, g_sgu, ws_tril, bs_full)


def _fox_fwd(proj, bq, bk, Bl, S):
    T = Bl * S
    nq = S // Q_BLK
    qc, kc, vc = 768 // LANES, 1152 // LANES, 1536 // LANES

    def body(q_ref, k_ref, v_ref, bq_ref, bk_ref, o_ref, lse_ref, ka_ref, va_ref):
        lane_s = _iota((S, LANES), 1)
        lane = _iota((Q_BLK, LANES), 1)
        tri = _iota((Q_BLK, Q_BLK), 1) <= _iota((Q_BLK, Q_BLK), 0)
        k = k_ref[...]
        v = v_ref[...]
        for hh in range(2):
            data = (lane_s < HEAD) if hh == 0 else (lane_s >= HEAD)
            ka_ref[hh] = jnp.where(data, k, bk_ref[0, hh])
            va_ref[hh] = jnp.where(lane_s == _bias_lane(hh), jnp.ones_like(v), v)
        for i in range(nq):
            r0 = i * Q_BLK
            q = q_ref[r0:r0 + Q_BLK, :]
            o_out = jnp.zeros((Q_BLK, LANES), F32)
            lse_out = jnp.zeros((Q_BLK, LANES), F32)
            for hh in range(2):
                hmask = (lane < HEAD) if hh == 0 else (lane >= HEAD)
                qa = jnp.where(hmask, q * 0.125, bq_ref[0, hh, r0:r0 + Q_BLK, :])
                sd = jnp.where(tri, _dot_nt(qa, ka_ref[hh, r0:r0 + Q_BLK, :]), NEG)
                m = jnp.max(sd, axis=1, keepdims=True)
                if i:
                    sf = _dot_nt(qa, ka_ref[hh, 0:r0, :])
                    m = jnp.maximum(m, jnp.max(sf, axis=1, keepdims=True))
                acc = _dot(jnp.exp(sd - m), va_ref[hh, r0:r0 + Q_BLK, :])
                if i:
                    acc = acc + _dot(jnp.exp(sf - m), va_ref[hh, 0:r0, :])
                l = jnp.sum(jnp.where(lane == _bias_lane(hh), acc, 0.0), axis=1, keepdims=True)
                o_out = jnp.where(hmask, acc / l, o_out)
                lse_out = jnp.where(hmask, m + jnp.log(l), lse_out)
            o_ref[r0:r0 + Q_BLK, :] = o_out.astype(BF16)
            lse_ref[0, r0:r0 + Q_BLK, :] = lse_out

    seq = lambda c0: BS((S, LANES), lambda b, p: (b, c0 + p))
    pair = BS((1, 2, S, LANES), lambda b, p: (b, p, 0, 0))
    return pl.pallas_call(
        body, name="fox_fwd", grid=(Bl, 3),
        in_specs=[seq(qc), seq(kc), seq(vc), pair, pair],
        out_specs=[seq(0), BS((1, S, LANES), lambda b, p: (p, b, 0))],
        out_shape=[jax.ShapeDtypeStruct((T, B_W), BF16), jax.ShapeDtypeStruct((3, T, LANES), F32)],
        scratch_shapes=[pltpu.VMEM((2, S, LANES), BF16), pltpu.VMEM((2, S, LANES), BF16)],
        compiler_params=_cp(("arbitrary", "arbitrary")),
    )(proj, proj, proj, bq, bk)


def _memkv_fwd(mem, g_mem, w_kv):
    Bl, Mt, D = mem.shape

    def body(m_ref, g_ref, w_ref, mn_ref, kv_ref):
        mn = _rms(m_ref[0], g_ref[...]).astype(BF16)
        mn_ref[0] = mn
        kv_ref[0] = jnp.dot(mn, w_ref[...], preferred_element_type=F32).astype(BF16)

    return pl.pallas_call(
        body, name="memkv_fwd", grid=(Bl,),
        in_specs=[BS((1, Mt, D), lambda b: (b, 0, 0)), BS((1, D), lambda b: (0, 0)), BS((D, 2 * M_W), lambda b: (0, 0))],
        out_specs=[BS((1, Mt, D), lambda b: (b, 0, 0)), BS((1, Mt, 2 * M_W), lambda b: (b, 0, 0))],
        out_shape=[jax.ShapeDtypeStruct((Bl, Mt, D), BF16), jax.ShapeDtypeStruct((Bl, Mt, 2 * M_W), BF16)],
        compiler_params=_cp(("arbitrary",)),
    )(mem, g_mem, w_kv)


def _memattn_fwd(proj, kv, Bl, S, tq):
    T = Bl * S
    nq = S // tq
    Mt = kv.shape[1]
    qc = 1920 // LANES

    def body(q_ref, km_ref, vm_ref, o_ref):
        lane = _iota((tq, LANES), 1)
        q = q_ref[...]
        out = jnp.zeros((tq, LANES), F32)
        for hh in range(2):
            hmask = (lane < HEAD) if hh == 0 else (lane >= HEAD)
            qs = jnp.where(hmask, q, jnp.zeros_like(q)) * 0.125
            s = _dot_nt(qs, km_ref[0])
            pe = jnp.exp(s - jnp.max(s, axis=1, keepdims=True))
            pn = pe / jnp.sum(pe, axis=1, keepdims=True)
            out = jnp.where(hmask, _dot(pn, vm_ref[0]), out)
        o_ref[...] = out.astype(BF16)

    return pl.pallas_call(
        body, name="memattn_fwd", grid=(Bl, 2, nq),
        in_specs=[BS((tq, LANES), lambda b, p, i: (b * nq + i, qc + p)),
                  BS((1, Mt, LANES), lambda b, p, i: (b, 0, p)),
                  BS((1, Mt, LANES), lambda b, p, i: (b, 0, 2 + p))],
        out_specs=BS((tq, LANES), lambda b, p, i: (b * nq + i, p)),
        out_shape=jax.ShapeDtypeStruct((T, M_W), BF16),
        compiler_params=_cp(("arbitrary", "arbitrary", "arbitrary")),
    )(proj, kv, kv)


def _mix_norms(ya, yb, ym, ga, gb, gm):
    return _rms(ya, ga), _rms(yb, gb), _rms(ym, gm)


def _outproj_fwd(ya, yb, ym, x2d, ga, gb, gm, g_post, g_pre2, w_out, tm):
    T, D = x2d.shape

    def body(ya_ref, yb_ref, ym_ref, x_ref, ga_ref, gb_ref, gm_ref, gp_ref, g2_ref, w_ref,
             y_ref, o_ref, x1_ref, h2_ref):
        na, nb_, nm = _mix_norms(ya_ref[...].astype(F32), yb_ref[...].astype(F32), ym_ref[...].astype(F32),
                                 ga_ref[...], gb_ref[...], gm_ref[...])
        y_ref[:, 0:A_W] = na.astype(BF16)
        y_ref[:, A_W:A_W + B_W] = nb_.astype(BF16)
        y_ref[:, A_W + B_W:] = nm.astype(BF16)
        o = jnp.dot(y_ref[...], w_ref[...], preferred_element_type=F32)
        o_ref[...] = o
        x1 = x_ref[...] + _rms(o, gp_ref[...])
        x1_ref[...] = x1
        h2_ref[...] = _rms(x1, g2_ref[...]).astype(BF16)

    row = lambda w: BS((tm, w), lambda i: (i, 0))
    vec = lambda w: BS((1, w), lambda i: (0, 0))
    return pl.pallas_call(
        body, name="outproj_fwd", grid=(T // tm,),
        in_specs=[row(A_W), row(B_W), row(M_W), row(D), vec(A_W), vec(B_W), vec(M_W), vec(D), vec(D),
                  BS((A_W + B_W + M_W, D), lambda i: (0, 0))],
        out_specs=[row(A_W + B_W + M_W), row(D), row(D), row(D)],
        out_shape=[jax.ShapeDtypeStruct((T, A_W + B_W + M_W), BF16), jax.ShapeDtypeStruct((T, D), F32),
                   jax.ShapeDtypeStruct((T, D), F32), jax.ShapeDtypeStruct((T, D), BF16)],
        compiler_params=_cp(("arbitrary",)),
    )(ya, yb, ym, x2d, ga, gb, gm, g_post, g_pre2, w_out)


def _ffn_fwd(h2, x1, target, wg, wu, wd, g_post, tm):
    T, D = x1.shape
    ns, F, _ = wg.shape

    def body(h_ref, x1_ref, t_ref, wg_ref, wu_ref, wd_ref, gp_ref,
             gs_ref, us_ref, dff_ref, dx2_ref, dgp_ref, loss_ref, acc_ref):
        i = pl.program_id(0)
        j = pl.program_id(1)
        h = h_ref[...]
        g = _dot_nt(h, wg_ref[0])
        u = _dot_nt(h, wu_ref[0])
        gs_ref[0] = g.astype(BF16)
        us_ref[0] = u.astype(BF16)
        part = _dot(_silu_mul(g, u), wd_ref[0])
        _acc(acc_ref, part, j == 0)

        @pl.when(j == ns - 1)
        def _():
            ff = acc_ref[...]
            diff = x1_ref[...] + _rms(ff, gp_ref[...]) - t_ref[...]
            dx2 = diff * (1.0 / D)
            dff, dgp = _rms_bwd(ff, gp_ref[...], dx2)
            dx2_ref[...] = dx2
            dff_ref[...] = dff.astype(BF16)
            lpart = jnp.sum(_colsum(diff * diff), axis=1, keepdims=True) * (0.5 / D)
            _acc(dgp_ref, dgp, i == 0)
            _acc(loss_ref, jnp.broadcast_to(lpart, (1, LANES)), i == 0)

    row = lambda w: BS((tm, w), lambda i, j: (i, 0))
    return pl.pallas_call(
        body, name="ffn_fwd", grid=(T // tm, ns),
        in_specs=[row(D), row(D), row(D), BS((1, F, D), lambda i, j: (j, 0, 0)), BS((1, F, D), lambda i, j: (j, 0, 0)),
                  BS((1, F, D), lambda i, j: (j, 0, 0)), BS((1, D), lambda i, j: (0, 0))],
        out_specs=[BS((1, tm, F), lambda i, j: (j, i, 0)), BS((1, tm, F), lambda i, j: (j, i, 0)), row(D), row(D),
                   BS((1, D), lambda i, j: (0, 0)), BS((1, LANES), lambda i, j: (0, 0))],
        out_shape=[jax.ShapeDtypeStruct((ns, T, F), BF16), jax.ShapeDtypeStruct((ns, T, F), BF16),
                   jax.ShapeDtypeStruct((T, D), BF16), jax.ShapeDtypeStruct((T, D), F32),
                   jax.ShapeDtypeStruct((1, D), F32), jax.ShapeDtypeStruct((1, LANES), F32)],
        scratch_shapes=[pltpu.VMEM((tm, D), F32)],
        compiler_params=_cp(("arbitrary", "arbitrary")),
    )(h2, x1, target, wg, wu, wd, g_post)


def _ffn_bwd(dff, h2, gs, us, wg, wu, wd, tm):
    T, D = h2.shape
    ns, F, _ = wg.shape

    def body(dff_ref, h_ref, gs_ref, us_ref, wg_ref, wu_ref, wd_ref, dh_ref, dwg_out, dwu_out, dwd_out,
             dwg_ref, dwu_ref, dwd_ref):
        first = pl.program_id(1) == 0
        dff = dff_ref[...]
        h = h_ref[...]
        parts = []
        for r in range(ROW_SPLIT):
            rows = slice(r * (tm // ROW_SPLIT), (r + 1) * (tm // ROW_SPLIT))
            dact = _dot_nt(dff[rows], wd_ref[0])
            g = gs_ref[0, rows, :].astype(F32)
            u = us_ref[0, rows, :].astype(F32)
            sig = _sigmoid(g)
            gsig = g * sig
            dg = (dact * u * (sig + gsig * (1.0 - sig))).astype(BF16)
            du = (dact * gsig).astype(BF16)
            dh_ref[0, rows, :] = (_dot(dg, wg_ref[0]) + _dot(du, wu_ref[0])).astype(BF16)
            parts.append(((gsig * u).astype(BF16), dg, du))
        a, dg, du = [jnp.concatenate(p, axis=0) for p in zip(*parts)]
        _acc(dwd_ref, _dot_tn(a, dff), first)
        _acc(dwg_ref, _dot_tn(dg, h), first)
        _acc(dwu_ref, _dot_tn(du, h), first)

        @pl.when(pl.program_id(1) == pl.num_programs(1) - 1)
        def _():
            dwg_out[0] = dwg_ref[...].astype(BF16)
            dwu_out[0] = dwu_ref[...].astype(BF16)
            dwd_out[0] = dwd_ref[...].astype(BF16)

    row = BS((tm, D), lambda j, i: (i, 0))
    sh = BS((1, tm, F), lambda j, i: (j, i, 0))
    wsh = BS((1, F, D), lambda j, i: (j, 0, 0))
    return pl.pallas_call(
        body, name="ffn_bwd", grid=(ns, T // tm),
        in_specs=[row, row, sh, sh, wsh, wsh, wsh],
        out_specs=[BS((1, tm, D), lambda j, i: (j, i, 0)), wsh, wsh, wsh],
        out_shape=[jax.ShapeDtypeStruct((ns, T, D), BF16)] + [jax.ShapeDtypeStruct((ns, F, D), BF16)] * 3,
        scratch_shapes=[pltpu.VMEM((F, D), F32)] * 3,
        compiler_params=_cp(("arbitrary", "arbitrary")),
    )(dff, h2, gs, us, wg, wu, wd)


def _mm_tn(a, b, name, tk):
    T, M = a.shape
    N = b.shape[1]
    tk = min(tk, T)

    def body(a_ref, b_ref, o_ref):
        _acc(o_ref, _dot_tn(a_ref[...], b_ref[...]), pl.program_id(0) == 0)

    return pl.pallas_call(
        body, name=name, grid=(T // tk,),
        in_specs=[BS((tk, M), lambda t: (t, 0)), BS((tk, N), lambda t: (t, 0))],
        out_specs=BS((M, N), lambda t: (0, 0)),
        out_shape=jax.ShapeDtypeStruct((M, N), F32),
        compiler_params=_cp(("arbitrary",)),
    )(a, b)


DPROJ_PIECES = ((0, A_W), (A_W, A_W), (768, B_W), (1152, B_W), (1536, B_W), (1920, LANES), (2048, M_W))


def _put_dproj(dp_ref, piece_refs):
    for (c0, w), ref in zip(DPROJ_PIECES, piece_refs):
        dp_ref[:, c0:c0 + w] = ref[...].astype(BF16)


def _dw_in(pieces, h, ns, tk):
    T, D = h.shape
    M = P_COLS
    dsh = D // ns
    tk = min(tk, T)

    def body(*refs):
        piece_refs, h_ref, o_ref, acc_ref, dp_ref = refs[:7], refs[7], refs[8], refs[9], refs[10]
        t = pl.program_id(0)
        _put_dproj(dp_ref, piece_refs)
        _acc(acc_ref, _dot_tn(h_ref[...], dp_ref[...]), t == 0)

        @pl.when(t == pl.num_programs(0) - 1)
        def _():
            for s in range(ns):
                o_ref[s] = acc_ref[s * dsh:(s + 1) * dsh, :].T.astype(BF16)

    return pl.pallas_call(
        body, name="dw_in", grid=(T // tk,),
        in_specs=[BS((tk, w), lambda t: (t, 0)) for _, w in DPROJ_PIECES] + [BS((tk, D), lambda t: (t, 0))],
        out_specs=BS((ns, M, dsh), lambda t: (0, 0, 0)),
        out_shape=jax.ShapeDtypeStruct((ns, M, dsh), BF16),
        scratch_shapes=[pltpu.VMEM((D, M), F32), pltpu.VMEM((tk, M), BF16)],
        compiler_params=_cp(("arbitrary",)),
    )(*pieces, h)


def _outproj_bwd(dh2, x1, dx2, o, ya, yb, ym, ga, gb, gm, g_post, g_pre2, w_out, tm):
    T, D = x1.shape
    ns = dh2.shape[0]

    def body(dh_ref, x1_ref, dx2_ref, o_ref, ya_ref, yb_ref, ym_ref, ga_ref, gb_ref, gm_ref, gp_ref, g2_ref, w_ref,
             dx1_ref, do_ref, dya_ref, dyb_ref, dym_ref, dga_ref, dgb_ref, dgm_ref, dgp_ref, dg2_ref):
        first = pl.program_id(0) == 0
        dh = dh_ref[0].astype(F32)
        for j in range(1, ns):
            dh = dh + dh_ref[j].astype(F32)
        dxa, dg2 = _rms_bwd(x1_ref[...], g2_ref[...], dh)
        dx1 = dx2_ref[...] + dxa
        dx1_ref[...] = dx1
        _acc(dg2_ref, dg2, first)
        do, dgp = _rms_bwd(o_ref[...], gp_ref[...], dx1)
        do = do.astype(BF16)
        do_ref[...] = do
        dy = _dot_nt(do, w_ref[...])
        dya, dga = _rms_bwd(ya_ref[...].astype(F32), ga_ref[...], dy[:, 0:A_W])
        dyb, dgb = _rms_bwd(yb_ref[...].astype(F32), gb_ref[...], dy[:, A_W:A_W + B_W])
        dym, dgm = _rms_bwd(ym_ref[...].astype(F32), gm_ref[...], dy[:, A_W + B_W:])
        dya_ref[...] = dya.astype(BF16)
        dyb_ref[...] = dyb.astype(BF16)
        dym_ref[...] = dym.astype(BF16)
        _acc(dga_ref, dga, first)
        _acc(dgb_ref, dgb, first)
        _acc(dgm_ref, dgm, first)
        _acc(dgp_ref, dgp, first)

    row = lambda w: BS((tm, w), lambda i: (i, 0))
    vec = lambda w: BS((1, w), lambda i: (0, 0))
    sds = jax.ShapeDtypeStruct
    return pl.pallas_call(
        body, name="outproj_bwd", grid=(T // tm,),
        in_specs=[BS((ns, tm, D), lambda i: (0, i, 0)), row(D), row(D), row(D), row(A_W), row(B_W), row(M_W),
                  vec(A_W), vec(B_W), vec(M_W), vec(D), vec(D), BS((A_W + B_W + M_W, D), lambda i: (0, 0))],
        out_specs=[row(D), row(D), row(A_W), row(B_W), row(M_W), vec(A_W), vec(B_W), vec(M_W), vec(D), vec(D)],
        out_shape=[sds((T, D), F32), sds((T, D), BF16), sds((T, A_W), BF16), sds((T, B_W), BF16), sds((T, M_W), BF16),
                   sds((1, A_W), F32), sds((1, B_W), F32), sds((1, M_W), F32), sds((1, D), F32), sds((1, D), F32)],
        compiler_params=_cp(("arbitrary",)),
    )(dh2, x1, dx2, o, ya, yb, ym, ga, gb, gm, g_post, g_pre2, w_out)


def _sgu_bwd(proj, dya, g_sgu, ws_tril, bs_full, tm):
    T = proj.shape[0]
    nch = tm // CHUNK

    def body(zu_ref, zv_ref, dy_ref, g_ref, ws_ref, b_ref, dzu_ref, dzv_ref, dws_ref, dbs_ref, dg_ref,
             du_ref, dvn_ref, dbf_ref):
        step = pl.program_id(0)
        first = step == 0
        lane = _iota((CHUNK, LANES), 1)
        tril = _iota((CHUNK, CHUNK), 0) >= _iota((CHUNK, CHUNK), 1)
        (u, vn), vjp = jax.vjp(_sgu_pre, zu_ref[...].astype(F32), zv_ref[...].astype(F32), g_ref[...])
        vnb = vn.astype(BF16)
        dy = dy_ref[...].astype(F32)

        @pl.when(first)
        def _():
            dws_ref[...] = jnp.zeros_like(dws_ref)
            dbf_ref[...] = jnp.zeros_like(dbf_ref)

        for c in range(nch):
            rs = slice(c * CHUNK, (c + 1) * CHUNK)
            for j in range(3):
                cs = slice(j * LANES, (j + 1) * LANES)
                vp = vnb[rs, cs]
                z = jnp.where(lane < HEAD, _dot(ws_ref[2 * j], vp), _dot(ws_ref[2 * j + 1], vp)) + b_ref[:, cs]
                du_ref[rs, cs] = dy[rs, cs] * z
                dz = dy[rs, cs] * u[rs, cs]
                dbf_ref[:, cs] += dz
                dzb = dz.astype(BF16)
                dz0 = jnp.where(lane < HEAD, dzb, jnp.zeros_like(dzb))
                dz1 = jnp.where(lane >= HEAD, dzb, jnp.zeros_like(dzb))
                dvn_ref[rs, cs] = jnp.where(lane < HEAD, _dot_tn(ws_ref[2 * j], dzb), _dot_tn(ws_ref[2 * j + 1], dzb))
                dws_ref[2 * j] += jnp.where(tril, _dot_nt(dz0, vp), 0.0)
                dws_ref[2 * j + 1] += jnp.where(tril, _dot_nt(dz1, vp), 0.0)
        dzu, dzv, dg = vjp((du_ref[...], dvn_ref[...]))
        dzu_ref[...] = dzu.astype(BF16)
        dzv_ref[...] = dzv.astype(BF16)
        _acc(dg_ref, dg, first)

        @pl.when(step == pl.num_programs(0) - 1)
        def _():
            out = jnp.zeros((CHUNK, LANES), F32)
            for j in range(3):
                slab = dbf_ref[:, j * LANES:(j + 1) * LANES]
                lo = jnp.sum(jnp.where(lane < HEAD, slab, 0.0), axis=1, keepdims=True)
                hi = jnp.sum(jnp.where(lane >= HEAD, slab, 0.0), axis=1, keepdims=True)
                out = out + jnp.where(lane == 2 * j, lo, 0.0) + jnp.where(lane == 2 * j + 1, hi, 0.0)
            dbs_ref[...] = out

    return pl.pallas_call(
        body, name="sgu_bwd", grid=(T // tm,),
        in_specs=[BS((tm, A_W), lambda i: (i, 0)), BS((tm, A_W), lambda i: (i, 1)), BS((tm, A_W), lambda i: (i, 0)),
                  BS((1, A_W), lambda i: (0, 0)), BS((6, CHUNK, CHUNK), lambda i: (0, 0, 0)),
                  BS((CHUNK, A_W), lambda i: (0, 0))],
        out_specs=[BS((tm, A_W), lambda i: (i, 0)), BS((tm, A_W), lambda i: (i, 0)),
                   BS((6, CHUNK, CHUNK), lambda i: (0, 0, 0)), BS((CHUNK, LANES), lambda i: (0, 0)),
                   BS((1, A_W), lambda i: (0, 0))],
        out_shape=[jax.ShapeDtypeStruct((T, A_W), BF16), jax.ShapeDtypeStruct((T, A_W), BF16),
                   jax.ShapeDtypeStruct((6, CHUNK, CHUNK), F32), jax.ShapeDtypeStruct((CHUNK, LANES), F32),
                   jax.ShapeDtypeStruct((1, A_W), F32)],
        scratch_shapes=[pltpu.VMEM((tm, A_W), F32), pltpu.VMEM((tm, A_W), F32), pltpu.VMEM((CHUNK, A_W), F32)],
        compiler_params=_cp(("arbitrary",)),
    )(proj, proj, dya, g_sgu, ws_tril, bs_full)


def _memattn_bwd(proj, kv, dym, Bl, S, tq):
    T = Bl * S
    nq = S // tq
    Mt = kv.shape[1]
    qc = 1920 // LANES

    def body(q_ref, km_ref, vm_ref, do_ref, dq_ref, dkm_ref, dvm_ref):
        first = pl.program_id(2) == 0
        lane = _iota((tq, LANES), 1)
        q = q_ref[...]
        do = do_ref[...]
        dq_out = jnp.zeros((tq, LANES), F32)
        dkm = jnp.zeros((Mt, LANES), F32)
        dvm = jnp.zeros((Mt, LANES), F32)
        for hh in range(2):
            hmask = (lane < HEAD) if hh == 0 else (lane >= HEAD)
            qs = jnp.where(hmask, q, jnp.zeros_like(q)) * 0.125
            dom = jnp.where(hmask, do, 0.0).astype(BF16)
            s = _dot_nt(qs, km_ref[0])
            pe = jnp.exp(s - jnp.max(s, axis=1, keepdims=True))
            pn = pe / jnp.sum(pe, axis=1, keepdims=True)
            dp = _dot_nt(dom, vm_ref[0])
            ds = (pn * (dp - jnp.sum(pn * dp, axis=1, keepdims=True))).astype(BF16)
            dq_out = jnp.where(hmask, _dot(ds, km_ref[0]) * 0.125, dq_out)
            dkm = dkm + _dot_tn(ds, qs)
            dvm = dvm + _dot_tn(pn, dom)
        dq_ref[...] = dq_out.astype(BF16)
        _acc(dkm_ref, dkm[None], first)
        _acc(dvm_ref, dvm[None], first)

    return pl.pallas_call(
        body, name="memattn_bwd", grid=(Bl, 2, nq),
        in_specs=[BS((tq, LANES), lambda b, p, i: (b * nq + i, qc + p)),
                  BS((1, Mt, LANES), lambda b, p, i: (b, 0, p)),
                  BS((1, Mt, LANES), lambda b, p, i: (b, 0, 2 + p)),
                  BS((tq, LANES), lambda b, p, i: (b * nq + i, p))],
        out_specs=[BS((tq, LANES), lambda b, p, i: (b * nq + i, p)),
                   BS((1, Mt, LANES), lambda b, p, i: (b, 0, p)),
                   BS((1, Mt, LANES), lambda b, p, i: (b, 0, p))],
        out_shape=[jax.ShapeDtypeStruct((T, M_W), BF16), jax.ShapeDtypeStruct((Bl, Mt, M_W), F32),
                   jax.ShapeDtypeStruct((Bl, Mt, M_W), F32)],
        compiler_params=_cp(("arbitrary", "arbitrary", "arbitrary")),
    )(proj, kv, kv, dym)


def _memkv_bwd(dkm, dvm, memn, mem, g_mem, w_kv):
    Bl, Mt, D = mem.shape

    def body(dk_ref, dv_ref, mn_ref, m_ref, g_ref, w_ref, dw_ref, dg_ref):
        first = pl.program_id(0) == 0
        dk = dk_ref[0].astype(BF16)
        dv = dv_ref[0].astype(BF16)
        mn = mn_ref[0]
        dmn = _dot_nt(dk, w_ref[:, 0:M_W]) + _dot_nt(dv, w_ref[:, M_W:])
        _, dg = _rms_bwd(m_ref[0], g_ref[...], dmn)
        _acc(dg_ref, dg, first)

        @pl.when(first)
        def _():
            dw_ref[...] = jnp.zeros_like(dw_ref)

        dw_ref[:, 0:M_W] += _dot_tn(mn, dk)
        dw_ref[:, M_W:] += _dot_tn(mn, dv)

    return pl.pallas_call(
        body, name="memkv_bwd", grid=(Bl,),
        in_specs=[BS((1, Mt, M_W), lambda b: (b, 0, 0)), BS((1, Mt, M_W), lambda b: (b, 0, 0)),
                  BS((1, Mt, D), lambda b: (b, 0, 0)), BS((1, Mt, D), lambda b: (b, 0, 0)),
                  BS((1, D), lambda b: (0, 0)), BS((D, 2 * M_W), lambda b: (0, 0))],
        out_specs=[BS((D, 2 * M_W), lambda b: (0, 0)), BS((1, D), lambda b: (0, 0))],
        out_shape=[jax.ShapeDtypeStruct((D, 2 * M_W), F32), jax.ShapeDtypeStruct((1, D), F32)],
        compiler_params=_cp(("arbitrary",)),
    )(dkm, dvm, memn, mem, g_mem, w_kv)


def _fox_bwd(proj, dyb, lse, bq, bk, Bl, S):
    T = Bl * S
    nq = S // Q_BLK
    nb = S // LANES
    qc, kc, vc = 768 // LANES, 1152 // LANES, 1536 // LANES

    def body(q_ref, k_ref, v_ref, do_ref, lse_ref, bq_ref, bk_ref,
             dq_ref, dk_ref, dv_ref, dcr_ref, ka_ref, dka_ref, dva_ref):
        p = pl.program_id(1)
        lane_s = _iota((S, LANES), 1)
        lane = _iota((Q_BLK, LANES), 1)
        sub = _iota((8, LANES), 0)
        tri = _iota((Q_BLK, Q_BLK), 1) <= _iota((Q_BLK, Q_BLK), 0)
        k = k_ref[...]
        for hh in range(2):
            data = (lane_s < HEAD) if hh == 0 else (lane_s >= HEAD)
            ka_ref[hh] = jnp.where(data, k, bk_ref[0, hh])
        dka_ref[...] = jnp.zeros_like(dka_ref)
        dva_ref[...] = jnp.zeros_like(dva_ref)

        @pl.when(p == 0)
        def _():
            dcr_ref[...] = jnp.zeros_like(dcr_ref)

        def add_colsums(ds, first_blk, h):
            cs = _colsum(ds)
            for jb in range(ds.shape[1] // LANES):
                dcr_ref[0, first_blk + jb] += jnp.where(sub == h, cs[:, jb * LANES:(jb + 1) * LANES], 0.0)

        for i in range(nq):
            r0 = i * Q_BLK
            r1 = r0 + Q_BLK
            q = q_ref[r0:r1, :]
            do = do_ref[r0:r1, :]
            lse_b = lse_ref[0, r0:r1, :]
            dq_out = jnp.zeros((Q_BLK, LANES), F32)
            for hh in range(2):
                hmask = (lane < HEAD) if hh == 0 else (lane >= HEAD)
                h = 2 * p + hh
                qs = jnp.where(hmask, q * 0.125, jnp.zeros_like(q))
                qa = jnp.where(hmask, q * 0.125, bq_ref[0, hh, r0:r1, :])
                dob = jnp.where(hmask, do, 0.0).astype(BF16)
                lse_h = jnp.sum(jnp.where(lane == hh * HEAD, lse_b, 0.0), axis=1, keepdims=True)
                pd = jnp.where(tri, jnp.exp(_dot_nt(qa, ka_ref[hh, r0:r1, :]) - lse_h), 0.0)
                dpd = _dot_nt(dob, v_ref[r0:r1, :])
                delta = jnp.sum(pd * dpd, axis=1, keepdims=True)
                psum = jnp.sum(pd, axis=1, keepdims=True)
                if i:
                    pf = jnp.exp(_dot_nt(qa, ka_ref[hh, 0:r0, :]) - lse_h)
                    dpf = _dot_nt(dob, v_ref[0:r0, :])
                    delta = delta + jnp.sum(pf * dpf, axis=1, keepdims=True)
                    psum = psum + jnp.sum(pf, axis=1, keepdims=True)
                delta = delta / psum
                dsd = pd * (dpd - delta)
                add_colsums(dsd, r0 // LANES, h)
                dsd = dsd.astype(BF16)
                dq_h = _dot(dsd, k_ref[r0:r1, :])
                dka_ref[r0:r1, :] += _dot_tn(dsd, qs)
                dva_ref[r0:r1, :] += _dot_tn(pd, dob)
                if i:
                    dsf = pf * (dpf - delta)
                    add_colsums(dsf, 0, h)
                    dsf = dsf.astype(BF16)
                    dq_h = dq_h + _dot(dsf, k_ref[0:r0, :])
                    dka_ref[0:r0, :] += _dot_tn(dsf, qs)
                    dva_ref[0:r0, :] += _dot_tn(pf, dob)
                dq_out = jnp.where(hmask, dq_h * 0.125, dq_out)
            dq_ref[r0:r1, :] = dq_out.astype(BF16)
        dk_ref[...] = dka_ref[...].astype(BF16)
        dv_ref[...] = dva_ref[...].astype(BF16)

    seq = lambda c0: BS((S, LANES), lambda b, p: (b, c0 + p))
    pair = BS((1, 2, S, LANES), lambda b, p: (b, p, 0, 0))
    rowblk = BS((1, nb, 8, LANES), lambda b, p: (b, 0, 0, 0))
    return pl.pallas_call(
        body, name="fox_bwd", grid=(Bl, 3),
        in_specs=[seq(qc), seq(kc), seq(vc), seq(0), BS((1, S, LANES), lambda b, p: (p, b, 0)), pair, pair],
        out_specs=[seq(0), seq(0), seq(0), rowblk],
        out_shape=[jax.ShapeDtypeStruct((T, B_W), BF16)] * 3 + [jax.ShapeDtypeStruct((Bl, nb, 8, LANES), F32)],
        scratch_shapes=[pltpu.VMEM((2, S, LANES), BF16), pltpu.VMEM((S, LANES), F32), pltpu.VMEM((S, LANES), F32)],
        compiler_params=_cp(("arbitrary", "arbitrary")),
    )(proj, proj, proj, dyb, lse, bq, bk)


def _gate_bwd(dc_row, fl_row):
    Bl, nb, _, _ = dc_row.shape

    def body(dc_ref, fl_ref, o_ref):
        lane = _iota((8, LANES), 1)

        carry = jnp.zeros((8, 1), F32)
        for j in reversed(range(nb)):
            r = -dc_ref[0, j]
            for k in (1, 2, 4, 8, 16, 32, 64):
                r = r + jnp.where(lane < LANES - k, pltpu.roll(r, LANES - k, 1), 0.0)
            total = jnp.sum(jnp.where(lane == 0, r, 0.0), axis=1, keepdims=True)
            dfl = (r + carry) * _sigmoid(-fl_ref[0, j])
            carry = carry + total
            o_ref[0, j * LANES:(j + 1) * LANES, :] = jnp.concatenate(
                [dfl, jnp.zeros((LANES - 8, LANES), F32)], axis=0).T

    rowblk = BS((1, nb, 8, LANES), lambda b: (b, 0, 0, 0))
    return pl.pallas_call(
        body, name="gate_bwd", grid=(Bl,),
        in_specs=[rowblk, rowblk],
        out_specs=BS((1, nb * LANES, LANES), lambda b: (b, 0, 0)),
        out_shape=jax.ShapeDtypeStruct((Bl, nb * LANES, LANES), F32),
        compiler_params=_cp(("arbitrary",)),
    )(dc_row, fl_row)


def _inproj_bwd(pieces, x2d, dx1, g_pre, w_in_p, tm):
    T, D = x2d.shape
    ns, _, dsh = w_in_p.shape

    def body(*refs):
        piece_refs = refs[:7]
        x_ref, dx1_ref, g_ref, w_ref, gx_ref, dg_ref, dbf_ref, dp_ref = refs[7:]
        first = pl.program_id(0) == 0
        _put_dproj(dp_ref, piece_refs)
        dh = jnp.concatenate([_dot(dp_ref[...], w_ref[s]) for s in range(ns)], axis=1)
        dxa, dg = _rms_bwd(x_ref[...], g_ref[...], dh)
        gx_ref[...] = dx1_ref[...] + dxa
        _acc(dg_ref, dg, first)
        _acc(dbf_ref, _colsum(piece_refs[5][...]), first)

    row = lambda w: BS((tm, w), lambda i: (i, 0))
    return pl.pallas_call(
        body, name="inproj_bwd", grid=(T // tm,),
        in_specs=[row(w) for _, w in DPROJ_PIECES] + [row(D), row(D), BS((1, D), lambda i: (0, 0)),
                                                      BS((ns, P_COLS, dsh), lambda i: (0, 0, 0))],
        out_specs=[row(D), BS((1, D), lambda i: (0, 0)), BS((1, LANES), lambda i: (0, 0))],
        out_shape=[jax.ShapeDtypeStruct((T, D), F32), jax.ShapeDtypeStruct((1, D), F32),
                   jax.ShapeDtypeStruct((1, LANES), F32)],
        scratch_shapes=[pltpu.VMEM((tm, P_COLS), BF16)],
        compiler_params=_cp(("arbitrary",)),
    )(*pieces, x2d, dx1, g_pre, w_in_p)


def _local_step(x, mem, target, W, P, reduce=None):
    Bl, S, D = x.shape
    T = Bl * S
    tm = min(512, T)
    x2d = x.reshape(T, D)
    t2d = target.reshape(T, D)
    vec = lambda a: a.reshape(1, -1)
    bf_row = jnp.pad(P["b_f"].reshape(1, -1), ((0, 0), (0, LANES - N_FOX_HEADS)))
    tril = jnp.tril(jnp.ones((CHUNK, CHUNK), bool))
    ws_tril = jnp.where(tril[None], P["w_s"][0], 0.0).astype(BF16)
    bs_full = jnp.repeat(P["b_s"][0].T, HEAD, axis=1)
    g_pre, g_sgu = vec(P["g_pre_mix"]), vec(P["g_sgu"])
    ga, gb, gm = vec(P["g_out_a"]), vec(P["g_out_b"]), vec(P["g_out_m"])
    g_mem, g_post, g_pre2, g_post2 = vec(P["g_mem"]), vec(P["g_post_mix"]), vec(P["g_pre_ffn"]), vec(P["g_post_ffn"])

    h, proj, flog = _inproj_fwd(x2d, g_pre, W["w_in"], tm)
    bq, bk, fl_row = _gate_fwd(flog.reshape(Bl, S, LANES), bf_row)
    ya = _sgu_fwd(proj, g_sgu, ws_tril, bs_full, tm)
    yb, lse = _fox_fwd(proj, bq, bk, Bl, S)
    memn, kv = _memkv_fwd(mem, g_mem, W["w_mem_kv"])
    ym = _memattn_fwd(proj, kv, Bl, S, min(512, S))
    y, o, x1, h2 = _outproj_fwd(ya, yb, ym, x2d, ga, gb, gm, g_post, g_pre2, W["w_out"], tm)
    gs, us, dff, dx2, dg_post2, loss = _ffn_fwd(h2, x1, t2d, W["w_gate"], W["w_up"], W["w_down"], g_post2, tm)

    dh2, d_w_gate, d_w_up, d_w_down = _ffn_bwd(dff, h2, gs, us, W["w_gate"], W["w_up"], W["w_down"], min(1024, T))
    ffn = [d_w_gate, d_w_up, d_w_down]
    if reduce is not None:
        pending, _ = reduce.begin("ffn", ffn)
    dx1, do, dya, dyb, dym, dga, dgb, dgm, dg_post, dg_pre2 = _outproj_bwd(
        dh2, x1, dx2, o, ya, yb, ym, ga, gb, gm, g_post, g_pre2, W["w_out"], tm)
    if reduce is not None:
        ffn, (do, dya, dyb, dym) = reduce.finish("ffn", pending, (do, dya, dyb, dym))
    d_w_out = _mm_tn(y, do, "dw_out", 1024)
    dzu, dzv, dws, dbs_cols, dg_sgu = _sgu_bwd(proj, dya, g_sgu, ws_tril, bs_full, tm)
    dqm, dkm, dvm = _memattn_bwd(proj, kv, dym, Bl, S, min(512, S))
    d_w_kv, dg_mem = _memkv_bwd(dkm, dvm, memn, mem, g_mem, W["w_mem_kv"])
    mid = [d_w_kv, d_w_out]
    dq, dk, dv, dc_row = _fox_bwd(proj, dyb, lse, bq, bk, Bl, S)
    if reduce is not None:
        done = reduce.apply(BIG[3:], ffn)
        pending, after = reduce.begin("mid", mid, (dc_row,) + done)
        dc_row = after[0]
    dfl = _gate_bwd(dc_row, fl_row).reshape(T, LANES)
    pieces = (dzu, dzv, dq, dk, dv, dfl, dqm)
    grad_x, dg_pre, dbf = _inproj_bwd(pieces, x2d, dx1, g_pre, W["w_in"], tm)
    if reduce is not None:
        mid, (dfl,) = reduce.finish("mid", pending, (dfl,))
        pieces = (dzu, dzv, dq, dk, dv, dfl, dqm)
    d_w_in = _dw_in(pieces, h, W["w_in"].shape[0], 1024)
    if reduce is None:
        big = dict(zip(BIG, [d_w_in] + mid + ffn))
    else:
        done = reduce.apply(BIG[1:3], mid)
        big = {"w_in": reduce.begin("in", [d_w_in], done)[0]}
    small = {"g_pre_mix": dg_pre, "b_f": dbf[:, :N_FOX_HEADS], "g_sgu": dg_sgu, "w_s": dws, "b_s": dbs_cols[:, :N_FOX_HEADS].T,
             "g_out_a": dga, "g_out_b": dgb, "g_out_m": dgm, "g_mem": dg_mem, "g_post_mix": dg_post,
             "g_pre_ffn": dg_pre2, "g_post_ffn": dg_post2, "loss": loss[:, :1]}
    return grad_x.reshape(Bl, S, D), big, small


def _place():
    return lax.axis_index("x"), lax.axis_index("y"), lax.axis_index("c")


def _exchange_on_sequencer(srcs, own_full, name, collective_id):
    n = len(srcs)

    def body(*refs):
        src, dst = refs[:n], refs[n:2 * n]
        lsem, isend, irecv, dsend, drecv = refs[2 * n:]
        x, y, c = _place()
        oc = 1 - c
        s_me = 2 * x + y
        sib = (x, y, oc)
        chips = [(1 - x, y), (x, 1 - y), (1 - x, 1 - y)]
        barrier = pltpu.get_barrier_semaphore()
        for dev in [(cx, cy, c) for cx, cy in chips] + [sib]:
            pl.semaphore_signal(barrier, inc=1, device_id=dev, device_id_type=MESH)
        pl.semaphore_wait(barrier, 4)

        def remote(a, b, ssem, rsem, dev):
            return pltpu.make_async_remote_copy(src_ref=a, dst_ref=b, send_sem=ssem, recv_sem=rsem,
                                                device_id=dev, device_id_type=MESH)

        sends, local = [], []
        for w in range(n):
            for j, (cx, cy) in enumerate(chips):
                half = src[w].at[c] if own_full else src[w].at[2 * cx + cy]
                cp = remote(half, dst[w].at[s_me, c], isend.at[w, j], irecv.at[w, j], (cx, cy, c))
                cp.start()
                sends.append(cp)
            if own_full:
                cp = remote(src[w], dst[w].at[s_me], dsend.at[w, 3], drecv.at[w, 3], sib)
            else:
                cp = remote(src[w].at[s_me], dst[w].at[s_me, c], dsend.at[w, 3], drecv.at[w, 3], sib)
                loc = pltpu.make_async_copy(src[w].at[s_me], dst[w].at[s_me, c], lsem.at[w])
                loc.start()
                local.append(loc)
            cp.start()
            sends.append(cp)
        for w in range(n):
            for j, (cx, cy) in enumerate(chips):
                landed = dst[w].at[2 * cx + cy, c]
                remote(landed, landed, isend.at[w, j], irecv.at[w, j], (cx, cy, c)).wait_recv()
                cp = remote(landed, landed, dsend.at[w, j], drecv.at[w, j], sib)
                cp.start()
                sends.append(cp)
        for w in range(n):
            for j, (cx, cy) in enumerate(chips):
                landed = dst[w].at[2 * cx + cy, oc]
                remote(landed, landed, dsend.at[w, j], drecv.at[w, j], sib).wait_recv()
            landed = dst[w].at[s_me] if own_full else dst[w].at[s_me, oc]
            remote(landed, landed, dsend.at[w, 3], drecv.at[w, 3], sib).wait_recv()
        for cp in sends:
            cp.wait_send()
        for loc in local:
            loc.wait()

    return pl.kernel(
        body, out_type=[jax.ShapeDtypeStruct((4, 2) + s.shape[1:], s.dtype) for s in srcs],
        mesh=plsc.ScalarSubcoreMesh(axis_name="sequencer", num_cores=1), name=name,
        scratch_types=[pltpu.SemaphoreType.DMA((n,)), pltpu.SemaphoreType.DMA((n, 3)), pltpu.SemaphoreType.DMA((n, 3)),
                       pltpu.SemaphoreType.DMA((n, 4)), pltpu.SemaphoreType.DMA((n, 4))],
        compiler_params=pltpu.CompilerParams(collective_id=collective_id),
    )(*srcs)


def _sibling_swap(grads, name, collective_id):
    n = len(grads)

    def body(*refs):
        g, theirs = refs[:n], refs[n:2 * n]
        ssem, rsem = refs[2 * n:]
        x, y, c = _place()
        sib = (x, y, 1 - c)
        barrier = pltpu.get_barrier_semaphore()
        pl.semaphore_signal(barrier, inc=1, device_id=sib, device_id_type=MESH)
        pl.semaphore_wait(barrier, 1)
        cps = []
        for w in range(n):
            cp = pltpu.make_async_remote_copy(src_ref=g[w].at[:, 1 - c], dst_ref=theirs[w], send_sem=ssem.at[w],
                                              recv_sem=rsem.at[w], device_id=sib, device_id_type=MESH)
            cp.start()
            cps.append(cp)
        for cp in cps:
            cp.wait()

    return pl.kernel(
        body, out_type=[jax.ShapeDtypeStruct((4,) + g.shape[2:], g.dtype) for g in grads],
        mesh=plsc.ScalarSubcoreMesh(axis_name="sequencer", num_cores=1), name=name,
        scratch_types=[pltpu.SemaphoreType.DMA((n,)), pltpu.SemaphoreType.DMA((n,))],
        compiler_params=pltpu.CompilerParams(collective_id=collective_id),
    )(*grads)


def _add_pair(core, g, theirs, name):
    _, _, hr, C = g.shape

    def body(core_ref, g_ref, t_ref, o_ref):
        o_ref[0] = (g_ref[0, 0].astype(F32) + t_ref[0].astype(F32)).astype(BF16)

    blk = BS((1, hr, C), lambda s, core_ref: (s, 0, 0))
    return pl.pallas_call(
        body, name=name,
        grid_spec=pltpu.PrefetchScalarGridSpec(
            num_scalar_prefetch=1, grid=(4,),
            in_specs=[BS((1, 1, hr, C), lambda s, core_ref: (s, core_ref[0], 0, 0)), blk], out_specs=blk),
        out_shape=jax.ShapeDtypeStruct(theirs.shape, BF16), compiler_params=_cp(("arbitrary",)))(core, g, theirs)


def _sum_chips(r, name):
    _, _, hr, C = r.shape

    def body(r_ref, o_ref):
        o_ref[...] = ((r_ref[0, 0].astype(F32) + r_ref[1, 0].astype(F32)) + r_ref[2, 0].astype(F32)) + r_ref[3, 0].astype(F32)

    return pl.pallas_call(body, name=name, grid=(2,), in_specs=[BS((4, 1, hr, C), lambda h: (0, h, 0, 0))],
                          out_specs=BS((hr, C), lambda h: (h, 0)), out_shape=jax.ShapeDtypeStruct((2 * hr, C), F32),
                          compiler_params=_cp(("arbitrary",)))(r)


class _Reducer:
    IDS = {"ffn": (4, 5), "mid": (6, 7), "in": (8, 9)}

    def __init__(self, core, apply):
        self.core = core
        self.apply = apply

    def begin(self, tag, grads, after=()):
        grads, after = lax.optimization_barrier((list(grads), after))
        g4 = [g.reshape(4, 2, -1, g.shape[-1]) for g in grads]
        return (g4, _sibling_swap(g4, "swap_" + tag, self.IDS[tag][0])), after

    def finish(self, tag, pending, hold):
        g4, theirs = pending
        sums = [_add_pair(self.core, g, t, "chip_sum_%s_%d" % (tag, k)) for k, (g, t) in enumerate(zip(g4, theirs))]
        sums, hold = lax.optimization_barrier((sums, hold))
        return _exchange_on_sequencer(sums, False, "scatter_" + tag, self.IDS[tag][1]), hold


def _small_allreduce(part):
    R = part.shape[0]
    rs = R // 8
    masks = [(mx, my, mc) for mx in (0, 1) for my in (0, 1) for mc in (0, 1)][1:]

    def body(p_ref, o_ref, buf_ref, s1, r1, s2, r2):
        x, y, c = _place()
        d = 4 * x + 2 * y + c
        mine = pl.ds(pl.multiple_of(d * rs, 8), rs)
        peers = [((x + mx) % 2, (y + my) % 2, (c + mc) % 2) for mx, my, mc in masks]
        first, second = [], []
        for k, (px, py, pc) in enumerate(peers):
            theirs = pl.ds(pl.multiple_of((4 * px + 2 * py + pc) * rs, 8), rs)
            cp = pltpu.make_async_remote_copy(src_ref=p_ref.at[theirs, :], dst_ref=buf_ref.at[d], send_sem=s1.at[k],
                                              recv_sem=r1.at[k], device_id=(px, py, pc), device_id_type=MESH)
            cp.start()
            first.append(cp)
        buf_ref[d] = p_ref[mine, :]
        for k, (px, py, pc) in enumerate(peers):
            slot = buf_ref.at[4 * px + 2 * py + pc]
            pltpu.make_async_remote_copy(src_ref=slot, dst_ref=slot, send_sem=s1.at[k], recv_sem=r1.at[k],
                                         device_id=(px, py, pc), device_id_type=MESH).wait_recv()
        total = buf_ref[0]
        for k in range(1, 8):
            total = total + buf_ref[k]
        o_ref[mine, :] = total
        for k, (px, py, pc) in enumerate(peers):
            cp = pltpu.make_async_remote_copy(src_ref=o_ref.at[mine, :], dst_ref=o_ref.at[mine, :], send_sem=s2.at[k],
                                              recv_sem=r2.at[k], device_id=(px, py, pc), device_id_type=MESH)
            cp.start()
            second.append(cp)
        for k, (px, py, pc) in enumerate(peers):
            rows = o_ref.at[pl.ds(pl.multiple_of((4 * px + 2 * py + pc) * rs, 8), rs), :]
            pltpu.make_async_remote_copy(src_ref=rows, dst_ref=rows, send_sem=s2.at[k], recv_sem=r2.at[k],
                                         device_id=(px, py, pc), device_id_type=MESH).wait_recv()
        for cp in first + second:
            cp.wait_send()

    vm = pl.BlockSpec(memory_space=pltpu.VMEM)
    return pl.pallas_call(
        body, name="small_allreduce", in_specs=[vm], out_specs=vm, out_shape=jax.ShapeDtypeStruct(part.shape, F32),
        scratch_shapes=[pltpu.VMEM((8, rs, LANES), F32)] + [pltpu.SemaphoreType.DMA((7,))] * 4,
    )(part)


def _adamw(w, g, m, v, name):
    R, C = w.shape
    summed = g.ndim == 4
    if summed:
        tr = R // 2
    else:
        tr = R if R * C * 4 <= (1 << 21) else R // 2
        if tr % 8:
            tr = R
    c1 = 1.0 / (1.0 - ADAM_B1 ** ADAM_STEP)
    c2 = 1.0 / (1.0 - ADAM_B2 ** ADAM_STEP)

    def body(w_ref, g_ref, m_ref, v_ref, *outs):
        if summed:
            g_ = ((g_ref[0, 0].astype(F32) + g_ref[1, 0].astype(F32)) + g_ref[2, 0].astype(F32)) + g_ref[3, 0].astype(F32)
            outs[0][...] = g_
        else:
            g_ = g_ref[...]
        d_ref, mo_ref, vo_ref = outs[-3:]
        m_ = ADAM_B1 * m_ref[...] + (1.0 - ADAM_B1) * g_
        v_ = ADAM_B2 * v_ref[...] + (1.0 - ADAM_B2) * (g_ * g_)
        mo_ref[...] = m_
        vo_ref[...] = v_
        d_ref[...] = -ADAM_LR * ((m_ * c1) / (jnp.sqrt(v_ * c2) + ADAM_EPS) + ADAM_WD * w_ref[...])

    blk = BS((tr, C), lambda i: (i, 0))
    g_blk = BS((4, 1, tr, C), lambda i: (0, i, 0, 0)) if summed else blk
    nout = 4 if summed else 3
    return pl.pallas_call(body, name=name, grid=(R // tr,), in_specs=[blk, g_blk, blk, blk], out_specs=[blk] * nout,
                          out_shape=[jax.ShapeDtypeStruct((R, C), F32)] * nout,
                          compiler_params=_cp(("arbitrary",)))(w, g, m, v)


def _adamw_from_transposed(w, g_t, m, v, name):
    R, C = w.shape
    tc = 256
    c1 = 1.0 / (1.0 - ADAM_B1 ** ADAM_STEP)
    c2 = 1.0 / (1.0 - ADAM_B2 ** ADAM_STEP)

    def body(w_ref, g_ref, m_ref, v_ref, go_ref, d_ref, mo_ref, vo_ref):
        g_ = g_ref[...].T
        go_ref[...] = g_
        m_ = ADAM_B1 * m_ref[...] + (1.0 - ADAM_B1) * g_
        v_ = ADAM_B2 * v_ref[...] + (1.0 - ADAM_B2) * (g_ * g_)
        mo_ref[...] = m_
        vo_ref[...] = v_
        d_ref[...] = -ADAM_LR * ((m_ * c1) / (jnp.sqrt(v_ * c2) + ADAM_EPS) + ADAM_WD * w_ref[...])

    blk = BS((R, tc), lambda i: (0, i))
    return pl.pallas_call(body, name=name, grid=(pl.cdiv(C, tc),), in_specs=[blk, BS((tc, R), lambda i: (i, 0)), blk, blk],
                          out_specs=[blk] * 4, out_shape=[jax.ShapeDtypeStruct((R, C), F32)] * 4,
                          compiler_params=_cp(("arbitrary",)))(w, g_t, m, v)


SMALL = ("g_pre_mix", "b_f", "g_sgu", "w_s", "b_s", "g_out_a", "g_out_b", "g_out_m", "g_mem", "g_post_mix",
         "g_pre_ffn", "g_post_ffn")
BIG = ("w_in", "w_mem_kv", "w_out", "w_gate", "w_up", "w_down")
TRANSPOSED = ("w_in", "w_gate", "w_up")
WEIGHTS = ("g_pre_mix", "w_in", "b_f", "g_sgu", "w_s", "b_s", "g_out_a", "g_out_b", "g_out_m", "g_mem", "w_mem_kv",
           "w_out", "g_post_mix", "g_pre_ffn", "w_gate", "w_up", "w_down", "g_post_ffn")


def _rows_of(n):
    return -(-n // (8 * LANES)) * 8


def _pack(parts):
    tiles = []
    for a in parts:
        flat = a.reshape(-1).astype(F32)
        rows = _rows_of(flat.shape[0])
        tiles.append(jnp.pad(flat, (0, rows * LANES - flat.shape[0])).reshape(rows, LANES))
    total = sum(t.shape[0] for t in tiles)
    pad = -total % 64
    if pad:
        tiles.append(jnp.zeros((pad, LANES), F32))
    return jnp.concatenate(tiles, axis=0)


def _unpack(packed, shapes):
    out, r = [], 0
    for shp in shapes:
        n = 1
        for s in shp:
            n *= s
        rows = _rows_of(n)
        out.append(packed[r:r + rows].reshape(-1)[:n].reshape(shp))
        r += rows
    return out


def kernel(x, mem, g_pre_mix, w_in, b_f, g_sgu, w_s, b_s, g_out_a, g_out_b, g_out_m, g_mem, w_mem_kv, w_out, g_post_mix, g_pre_ffn, w_gate, w_up, w_down, g_post_ffn, loss_target, m_g_pre_mix, m_w_in, m_b_f, m_g_sgu, m_w_s, m_b_s, m_g_out_a, m_g_out_b, m_g_out_m, m_g_mem, m_w_mem_kv, m_w_out, m_g_post_mix, m_g_pre_ffn, m_w_gate, m_w_up, m_w_down, m_g_post_ffn, v_g_pre_mix, v_w_in, v_b_f, v_g_sgu, v_w_s, v_b_s, v_g_out_a, v_g_out_b, v_g_out_m, v_g_mem, v_w_mem_kv, v_w_out, v_g_post_mix, v_g_pre_ffn, v_w_gate, v_w_up, v_w_down, v_g_post_ffn):
    Wt = dict(g_pre_mix=g_pre_mix, w_in=w_in, b_f=b_f, g_sgu=g_sgu, w_s=w_s, b_s=b_s, g_out_a=g_out_a, g_out_b=g_out_b,
              g_out_m=g_out_m, g_mem=g_mem, w_mem_kv=w_mem_kv, w_out=w_out, g_post_mix=g_post_mix, g_pre_ffn=g_pre_ffn,
              w_gate=w_gate, w_up=w_up, w_down=w_down, g_post_ffn=g_post_ffn)
    Mo = dict(g_pre_mix=m_g_pre_mix, w_in=m_w_in, b_f=m_b_f, g_sgu=m_g_sgu, w_s=m_w_s, b_s=m_b_s, g_out_a=m_g_out_a,
              g_out_b=m_g_out_b, g_out_m=m_g_out_m, g_mem=m_g_mem, w_mem_kv=m_w_mem_kv, w_out=m_w_out,
              g_post_mix=m_g_post_mix, g_pre_ffn=m_g_pre_ffn, w_gate=m_w_gate, w_up=m_w_up, w_down=m_w_down,
              g_post_ffn=m_g_post_ffn)
    Vo = dict(g_pre_mix=v_g_pre_mix, w_in=v_w_in, b_f=v_b_f, g_sgu=v_g_sgu, w_s=v_w_s, b_s=v_b_s, g_out_a=v_g_out_a,
              g_out_b=v_g_out_b, g_out_m=v_g_out_m, g_mem=v_g_mem, w_mem_kv=v_w_mem_kv, w_out=v_w_out,
              g_post_mix=v_g_post_mix, g_pre_ffn=v_g_pre_ffn, w_gate=v_w_gate, w_up=v_w_up, w_down=v_w_down,
              g_post_ffn=v_g_post_ffn)

    gap = P_COLS - IN_COLS

    def to_kernel(n, w):
        if n in TRANSPOSED:
            w = w.T
        if n == "w_in":
            w = jnp.pad(w[:F_END], ((0, P_COLS - F_END), (0, 0))) + jnp.pad(w[F_END:], ((F_END + gap, 0), (0, 0)))
        return w

    def ungroup(g):
        return jnp.pad(g[:F_END], ((0, IN_COLS - F_END), (0, 0))) + jnp.pad(g[F_END + gap:], ((F_END, 0), (0, 0)))

    shards = {n: to_kernel(n, Wt[n][0]) for n in BIG}
    srcs = [shards[n].astype(BF16).reshape(2, shards[n].shape[0] // 2, shards[n].shape[1]) for n in BIG]
    fulls = (_exchange_on_sequencer(srcs[:1], True, "gather_w_in", 1)
             + _exchange_on_sequencer(srcs[1:3], True, "gather_kv_out", 2)
             + _exchange_on_sequencer(srcs[3:], True, "gather_ffn", 3))
    W = {}
    for n, f in zip(BIG, fulls):
        _, _, hr, C = f.shape
        W[n] = f.reshape(8 * hr, C) if n in ("w_mem_kv", "w_out") else f.reshape(4, 2 * hr, C)

    P = {n: Wt[n] for n in SMALL}
    grads, deltas, new_m, new_v = {}, {}, {}, {}

    def apply(names, landed):
        for n, r in zip(names, landed):
            if n == "w_in":
                g_t = ungroup(_sum_chips(r, "sum_chips_" + n))
                g, d, m1, v1 = _adamw_from_transposed(Wt[n][0], g_t, Mo[n][0], Vo[n][0], "adamw_" + n)
            elif n in TRANSPOSED:
                g, d, m1, v1 = [a.T for a in _adamw(Wt[n][0].T, r, Mo[n][0].T, Vo[n][0].T, "adamw_" + n)]
            else:
                g, d, m1, v1 = _adamw(Wt[n][0], r, Mo[n][0], Vo[n][0], "adamw_" + n)
            grads[n], deltas[n], new_m[n], new_v[n] = g[None], d[None], m1[None], v1[None]
        return tuple(deltas[n] for n in names)

    core = lax.axis_index("c").astype(jnp.int32).reshape(1)
    reducer = _Reducer(core, apply)
    grad_x, pending, small = _local_step(x, mem, loss_target, W, P, reducer)

    total = _small_allreduce(_pack([small[n] for n in SMALL] + [small["loss"]]))
    landed, (total,) = reducer.finish("in", pending["w_in"], (total,))
    apply(BIG[:1], landed)

    slot = [jnp.zeros((1, 1), F32)]
    shapes = [Wt[n].shape for n in SMALL] + [(1, 1)]
    d, m1, v1 = _adamw(_pack([Wt[n] for n in SMALL] + slot), total, _pack([Mo[n] for n in SMALL] + slot),
                       _pack([Vo[n] for n in SMALL] + slot), "adamw_small")
    g_s, d_s, m_s, v_s = _unpack(total, shapes), _unpack(d, shapes), _unpack(m1, shapes), _unpack(v1, shapes)
    for k, n in enumerate(SMALL):
        grads[n], deltas[n], new_m[n], new_v[n] = g_s[k], d_s[k], m_s[k], v_s[k]
    loss = g_s[-1][0, 0]

    return (loss, grad_x, *[grads[n] for n in WEIGHTS], *[deltas[n] for n in WEIGHTS],
            *[new_m[n] for n in WEIGHTS], *[new_v[n] for n in WEIGHTS])
```

```python
import functools

import jax
import jax.numpy as jnp
from jax import lax
from jax.experimental import pallas as pl
from jax.experimental.pallas import tpu as pltpu
from jax.experimental.pallas import tpu_sc as plsc

F32 = jnp.float32
BF16 = jnp.bfloat16
EPS = 1e-6
NEG = -1e30
HEAD = 64
A_W, B_W, M_W = 384, 384, 256
N_FOX_HEADS = 6
CHUNK = 128
IN_COLS = 2 * A_W + 3 * B_W + N_FOX_HEADS + M_W
P_MAIN = 2 * A_W + 3 * B_W + M_W
P_COLS = P_MAIN + 128
F_END = 2 * A_W + 3 * B_W + N_FOX_HEADS
LANES = 128
Q_BLK, K_BLK = 256, 128
ROW_SPLIT = 4
ADAM_LR, ADAM_B1, ADAM_B2, ADAM_EPS, ADAM_WD, ADAM_STEP = 0.001, 0.9, 0.999, 1e-08, 0.01, 10
VMEM_LIMIT = 56 * 1024 * 1024
MESH = pl.DeviceIdType.MESH
ANY = pl.BlockSpec(memory_space=pl.ANY)
BS = pl.BlockSpec


def _cp(sem=None):
    return pltpu.CompilerParams(dimension_semantics=sem, vmem_limit_bytes=VMEM_LIMIT)


def _iota(shape, dim):
    return lax.broadcasted_iota(jnp.int32, shape, dim)


def _dot(a, b):
    return jnp.dot(a.astype(BF16), b.astype(BF16), preferred_element_type=F32)


def _dot_nt(a, b):
    return lax.dot_general(a.astype(BF16), b.astype(BF16), (((1,), (1,)), ((), ())), preferred_element_type=F32)


def _dot_tn(a, b):
    return lax.dot_general(a.astype(BF16), b.astype(BF16), (((0,), (0,)), ((), ())), preferred_element_type=F32)


def _rms(x, g):
    return x * lax.rsqrt(jnp.mean(x * x, axis=-1, keepdims=True) + EPS) * g


def _rms_bwd(x, g, dy):
    r = lax.rsqrt(jnp.mean(x * x, axis=-1, keepdims=True) + EPS)
    xr = x * r
    gd = dy * g
    m = jnp.mean(gd * xr, axis=-1, keepdims=True)
    return (gd - xr * m) * r, _colsum(dy * xr)


def _gelu(x):
    return 0.5 * x * (1.0 + jnp.tanh(0.7978845608028654 * (x + 0.044715 * (x * x * x))))


def _sigmoid(x):
    return 1.0 / (1.0 + jnp.exp(-x))


def _silu_mul(g, u):
    return g * _sigmoid(g) * u


def _logsig(x):
    return jnp.minimum(x, 0.0) - jnp.log(1.0 + jnp.exp(-jnp.abs(x)))


def _colsum(x):
    return jnp.sum(x, axis=0, keepdims=True)


def _acc(ref, val, first):
    @pl.when(first)
    def _():
        ref[...] = val

    @pl.when(jnp.logical_not(first))
    def _():
        ref[...] += val


def _inproj_fwd(x2d, g_pre, w_in_p, tm):
    T, D = x2d.shape
    CH = 768
    nchunk = P_COLS // CH
    ns, _, dsh = w_in_p.shape

    def body(x_ref, g_ref, w_ref, h_ref, proj_ref, fl_ref):
        h = _rms(x_ref[...], g_ref[...]).astype(BF16)
        h_ref[...] = h
        for n in range(nchunk):
            rows = slice(n * CH, (n + 1) * CH)
            r = _dot_nt(h[:, 0:dsh], w_ref[0, rows, :])
            for s in range(1, ns):
                r = r + _dot_nt(h[:, s * dsh:(s + 1) * dsh], w_ref[s, rows, :])
            if n < nchunk - 1:
                proj_ref[:, rows] = r.astype(BF16)
            else:
                fg = 1920 - n * CH
                proj_ref[:, n * CH:1920] = r[:, :fg].astype(BF16)
                fl_ref[...] = r[:, fg:fg + LANES]
                proj_ref[:, 1920:P_MAIN] = r[:, fg + LANES:].astype(BF16)

    return pl.pallas_call(
        body, name="inproj_fwd", grid=(T // tm,),
        in_specs=[BS((tm, D), lambda i: (i, 0)), BS((1, D), lambda i: (0, 0)),
                  BS((ns, P_COLS, dsh), lambda i: (0, 0, 0))],
        out_specs=[BS((tm, D), lambda i: (i, 0)), BS((tm, P_MAIN), lambda i: (i, 0)), BS((tm, LANES), lambda i: (i, 0))],
        out_shape=[jax.ShapeDtypeStruct((T, D), BF16), jax.ShapeDtypeStruct((T, P_MAIN), BF16),
                   jax.ShapeDtypeStruct((T, LANES), F32)],
        compiler_params=_cp(("arbitrary",)),
    )(x2d, g_pre, w_in_p)


def _gate_fwd(flog3, bf_row):
    Bl, S, _ = flog3.shape
    nb = S // LANES

    def body(f_ref, b_ref, bq_ref, bk_ref, fr_ref):
        row = _iota((LANES, LANES), 0)
        lane = _iota((LANES, LANES), 1)
        one = jnp.ones((LANES, LANES), BF16)
        zero = jnp.zeros((LANES, LANES), BF16)

        carry = jnp.zeros((1, LANES), F32)
        for j in range(nb):
            r0 = j * LANES
            fl = f_ref[0, pl.ds(r0, LANES), :] + b_ref[...]
            fr_ref[0, j] = fl.T[0:8, :]
            c = _logsig(fl)
            for k in (1, 2, 4, 8, 16, 32, 64):
                c = c + jnp.where(row >= k, pltpu.roll(c, k, 0), 0.0)
            total = _colsum(jnp.where(row == LANES - 1, c, 0.0))
            c = c + carry
            carry = carry + total
            for h in range(N_FOX_HEADS):
                col = jnp.sum(jnp.where(lane == h, c, 0.0), axis=1, keepdims=True)
                hi = col.astype(BF16)
                rest = col - hi.astype(F32)
                mid = rest.astype(BF16)
                lo = (rest - mid.astype(F32)).astype(BF16)
                base = _bias_lane(h)
                bq = jnp.where(lane == base, hi, jnp.where(lane == base + 1, mid, jnp.where(lane == base + 2, lo, zero)))
                bq = jnp.where((lane >= base + 3) & (lane < base + 6), one, bq)
                bk = jnp.where(lane == base + 3, -hi, jnp.where(lane == base + 4, -mid, jnp.where(lane == base + 5, -lo, zero)))
                bk = jnp.where((lane >= base) & (lane < base + 3), one, bk)
                bq_ref[0, h, pl.ds(r0, LANES), :] = bq
                bk_ref[0, h, pl.ds(r0, LANES), :] = bk

    slab = BS((1, N_FOX_HEADS, S, LANES), lambda b: (b, 0, 0, 0))
    return pl.pallas_call(
        body, name="gate_fwd", grid=(Bl,),
        in_specs=[BS((1, S, LANES), lambda b: (b, 0, 0)), BS((1, LANES), lambda b: (0, 0))],
        out_specs=[slab, slab, BS((1, nb, 8, LANES), lambda b: (b, 0, 0, 0))],
        out_shape=[jax.ShapeDtypeStruct((Bl, N_FOX_HEADS, S, LANES), BF16),
                   jax.ShapeDtypeStruct((Bl, N_FOX_HEADS, S, LANES), BF16),
                   jax.ShapeDtypeStruct((Bl, nb, 8, LANES), F32)],
        compiler_params=_cp(("arbitrary",)),
    )(flog3, bf_row)


def _bias_lane(h):
    return HEAD if h % 2 == 0 else 0


def _sgu_pre(zu, zv, g_sgu):
    return _gelu(zu), _rms(_gelu(zv), g_sgu)


def _sgu_fwd(proj, g_sgu, ws_tril, bs_full, tm):
    T = proj.shape[0]
    nch = tm // CHUNK

    def body(zu_ref, zv_ref, g_ref, ws_ref, b_ref, ya_ref):
        lane = _iota((CHUNK, LANES), 1)
        u, vn = _sgu_pre(zu_ref[...].astype(F32), zv_ref[...].astype(F32), g_ref[...])
        vn = vn.astype(BF16)
        for c in range(nch):
            rs = slice(c * CHUNK, (c + 1) * CHUNK)
            for j in range(3):
                cs = slice(j * LANES, (j + 1) * LANES)
                vp = vn[rs, cs]
                z = jnp.where(lane < HEAD, _dot(ws_ref[2 * j], vp), _dot(ws_ref[2 * j + 1], vp)) + b_ref[:, cs]
                ya_ref[rs, cs] = (u[rs, cs] * z).astype(BF16)

    return pl.pallas_call(
        body, name="sgu_fwd", grid=(T // tm,),
        in_specs=[BS((tm, A_W), lambda i: (i, 0)), BS((tm, A_W), lambda i: (i, 1)), BS((1, A_W), lambda i: (0, 0)),
                  BS((6, CHUNK, CHUNK), lambda i: (0, 0, 0)), BS((CHUNK, A_W), lambda i: (0, 0))],
        out_specs=BS((tm, A_W), lambda i: (i, 0)),
        out_shape=jax.ShapeDtypeStruct((T, A_W), BF16),
        compiler_params=_cp(("arbitrary",)),
    )(proj, proj, g_sgu, ws_tril, bs_full)


def _fox_fwd(proj, bq, bk, Bl, S):
    T = Bl * S
    nq = S // Q_BLK
    qc, kc, vc = 768 // LANES, 1152 // LANES, 1536 // LANES

    def body(q_ref, k_ref, v_ref, bq_ref, bk_ref, o_ref, lse_ref, ka_ref, va_ref):
        lane_s = _iota((S, LANES), 1)
        lane = _iota((Q_BLK, LANES), 1)
        tri = _iota((Q_BLK, Q_BLK), 1) <= _iota((Q_BLK, Q_BLK), 0)
        k = k_ref[...]
        v = v_ref[...]
        for hh in range(2):
            data = (lane_s < HEAD) if hh == 0 else (lane_s >= HEAD)
            ka_ref[hh] = jnp.where(data, k, bk_ref[0, hh])
            va_ref[hh] = jnp.where(lane_s == _bias_lane(hh), jnp.ones_like(v), v)
        for i in range(nq):
            r0 = i * Q_BLK
            q = q_ref[r0:r0 + Q_BLK, :]
            o_out = jnp.zeros((Q_BLK, LANES), F32)
            lse_out = jnp.zeros((Q_BLK, LANES), F32)
            for hh in range(2):
                hmask = (lane < HEAD) if hh == 0 else (lane >= HEAD)
                qa = jnp.where(hmask, q * 0.125, bq_ref[0, hh, r0:r0 + Q_BLK, :])
                sd = jnp.where(tri, _dot_nt(qa, ka_ref[hh, r0:r0 + Q_BLK, :]), NEG)
                m = jnp.max(sd, axis=1, keepdims=True)
                if i:
                    sf = _dot_nt(qa, ka_ref[hh, 0:r0, :])
                    m = jnp.maximum(m, jnp.max(sf, axis=1, keepdims=True))
                acc = _dot(jnp.exp(sd - m), va_ref[hh, r0:r0 + Q_BLK, :])
                if i:
                    acc = acc + _dot(jnp.exp(sf - m), va_ref[hh, 0:r0, :])
                l = jnp.sum(jnp.where(lane == _bias_lane(hh), acc, 0.0), axis=1, keepdims=True)
                o_out = jnp.where(hmask, acc / l, o_out)
                lse_out = jnp.where(hmask, m + jnp.log(l), lse_out)
            o_ref[r0:r0 + Q_BLK, :] = o_out.astype(BF16)
            lse_ref[0, r0:r0 + Q_BLK, :] = lse_out

    seq = lambda c0: BS((S, LANES), lambda b, p: (b, c0 + p))
    pair = BS((1, 2, S, LANES), lambda b, p: (b, p, 0, 0))
    return pl.pallas_call(
        body, name="fox_fwd", grid=(Bl, 3),
        in_specs=[seq(qc), seq(kc), seq(vc), pair, pair],
        out_specs=[seq(0), BS((1, S, LANES), lambda b, p: (p, b, 0))],
        out_shape=[jax.ShapeDtypeStruct((T, B_W), BF16), jax.ShapeDtypeStruct((3, T, LANES), F32)],
        scratch_shapes=[pltpu.VMEM((2, S, LANES), BF16), pltpu.VMEM((2, S, LANES), BF16)],
        compiler_params=_cp(("arbitrary", "arbitrary")),
    )(proj, proj, proj, bq, bk)


def _memkv_fwd(mem, g_mem, w_kv):
    Bl, Mt, D = mem.shape

    def body(m_ref, g_ref, w_ref, mn_ref, kv_ref):
        mn = _rms(m_ref[0], g_ref[...]).astype(BF16)
        mn_ref[0] = mn
        kv_ref[0] = jnp.dot(mn, w_ref[...], preferred_element_type=F32).astype(BF16)

    return pl.pallas_call(
        body, name="memkv_fwd", grid=(Bl,),
        in_specs=[BS((1, Mt, D), lambda b: (b, 0, 0)), BS((1, D), lambda b: (0, 0)), BS((D, 2 * M_W), lambda b: (0, 0))],
        out_specs=[BS((1, Mt, D), lambda b: (b, 0, 0)), BS((1, Mt, 2 * M_W), lambda b: (b, 0, 0))],
        out_shape=[jax.ShapeDtypeStruct((Bl, Mt, D), BF16), jax.ShapeDtypeStruct((Bl, Mt, 2 * M_W), BF16)],
        compiler_params=_cp(("arbitrary",)),
    )(mem, g_mem, w_kv)


def _memattn_fwd(proj, kv, Bl, S, tq):
    T = Bl * S
    nq = S // tq
    Mt = kv.shape[1]
    qc = 1920 // LANES

    def body(q_ref, km_ref, vm_ref, o_ref):
        lane = _iota((tq, LANES), 1)
        q = q_ref[...]
        out = jnp.zeros((tq, LANES), F32)
        for hh in range(2):
            hmask = (lane < HEAD) if hh == 0 else (lane >= HEAD)
            qs = jnp.where(hmask, q, jnp.zeros_like(q)) * 0.125
            s = _dot_nt(qs, km_ref[0])
            pe = jnp.exp(s - jnp.max(s, axis=1, keepdims=True))
            pn = pe / jnp.sum(pe, axis=1, keepdims=True)
            out = jnp.where(hmask, _dot(pn, vm_ref[0]), out)
        o_ref[...] = out.astype(BF16)

    return pl.pallas_call(
        body, name="memattn_fwd", grid=(Bl, 2, nq),
        in_specs=[BS((tq, LANES), lambda b, p, i: (b * nq + i, qc + p)),
                  BS((1, Mt, LANES), lambda b, p, i: (b, 0, p)),
                  BS((1, Mt, LANES), lambda b, p, i: (b, 0, 2 + p))],
        out_specs=BS((tq, LANES), lambda b, p, i: (b * nq + i, p)),
        out_shape=jax.ShapeDtypeStruct((T, M_W), BF16),
        compiler_params=_cp(("arbitrary", "arbitrary", "arbitrary")),
    )(proj, kv, kv)


def _mix_norms(ya, yb, ym, ga, gb, gm):
    return _rms(ya, ga), _rms(yb, gb), _rms(ym, gm)


def _outproj_fwd(ya, yb, ym, x2d, ga, gb, gm, g_post, g_pre2, w_out, tm):
    T, D = x2d.shape

    def body(ya_ref, yb_ref, ym_ref, x_ref, ga_ref, gb_ref, gm_ref, gp_ref, g2_ref, w_ref,
             y_ref, o_ref, x1_ref, h2_ref):
        na, nb_, nm = _mix_norms(ya_ref[...].astype(F32), yb_ref[...].astype(F32), ym_ref[...].astype(F32),
                                 ga_ref[...], gb_ref[...], gm_ref[...])
        y_ref[:, 0:A_W] = na.astype(BF16)
        y_ref[:, A_W:A_W + B_W] = nb_.astype(BF16)
        y_ref[:, A_W + B_W:] = nm.astype(BF16)
        o = jnp.dot(y_ref[...], w_ref[...], preferred_element_type=F32)
        o_ref[...] = o
        x1 = x_ref[...] + _rms(o, gp_ref[...])
        x1_ref[...] = x1
        h2_ref[...] = _rms(x1, g2_ref[...]).astype(BF16)

    row = lambda w: BS((tm, w), lambda i: (i, 0))
    vec = lambda w: BS((1, w), lambda i: (0, 0))
    return pl.pallas_call(
        body, name="outproj_fwd", grid=(T // tm,),
        in_specs=[row(A_W), row(B_W), row(M_W), row(D), vec(A_W), vec(B_W), vec(M_W), vec(D), vec(D),
                  BS((A_W + B_W + M_W, D), lambda i: (0, 0))],
        out_specs=[row(A_W + B_W + M_W), row(D), row(D), row(D)],
        out_shape=[jax.ShapeDtypeStruct((T, A_W + B_W + M_W), BF16), jax.ShapeDtypeStruct((T, D), F32),
                   jax.ShapeDtypeStruct((T, D), F32), jax.ShapeDtypeStruct((T, D), BF16)],
        compiler_params=_cp(("arbitrary",)),
    )(ya, yb, ym, x2d, ga, gb, gm, g_post, g_pre2, w_out)


def _ffn_fwd(h2, x1, target, wg, wu, wd, g_post, tm):
    T, D = x1.shape
    ns, F, _ = wg.shape

    def body(h_ref, x1_ref, t_ref, wg_ref, wu_ref, wd_ref, gp_ref,
             gs_ref, us_ref, dff_ref, dx2_ref, dgp_ref, loss_ref, acc_ref):
        i = pl.program_id(0)
        j = pl.program_id(1)
        h = h_ref[...]
        g = _dot_nt(h, wg_ref[0])
        u = _dot_nt(h, wu_ref[0])
        gs_ref[0] = g.astype(BF16)
        us_ref[0] = u.astype(BF16)
        part = _dot(_silu_mul(g, u), wd_ref[0])
        _acc(acc_ref, part, j == 0)

        @pl.when(j == ns - 1)
        def _():
            ff = acc_ref[...]
            diff = x1_ref[...] + _rms(ff, gp_ref[...]) - t_ref[...]
            dx2 = diff * (1.0 / D)
            dff, dgp = _rms_bwd(ff, gp_ref[...], dx2)
            dx2_ref[...] = dx2
            dff_ref[...] = dff.astype(BF16)
            lpart = jnp.sum(_colsum(diff * diff), axis=1, keepdims=True) * (0.5 / D)
            _acc(dgp_ref, dgp, i == 0)
            _acc(loss_ref, jnp.broadcast_to(lpart, (1, LANES)), i == 0)

    row = lambda w: BS((tm, w), lambda i, j: (i, 0))
    return pl.pallas_call(
        body, name="ffn_fwd", grid=(T // tm, ns),
        in_specs=[row(D), row(D), row(D), BS((1, F, D), lambda i, j: (j, 0, 0)), BS((1, F, D), lambda i, j: (j, 0, 0)),
                  BS((1, F, D), lambda i, j: (j, 0, 0)), BS((1, D), lambda i, j: (0, 0))],
        out_specs=[BS((1, tm, F), lambda i, j: (j, i, 0)), BS((1, tm, F), lambda i, j: (j, i, 0)), row(D), row(D),
                   BS((1, D), lambda i, j: (0, 0)), BS((1, LANES), lambda i, j: (0, 0))],
        out_shape=[jax.ShapeDtypeStruct((ns, T, F), BF16), jax.ShapeDtypeStruct((ns, T, F), BF16),
                   jax.ShapeDtypeStruct((T, D), BF16), jax.ShapeDtypeStruct((T, D), F32),
                   jax.ShapeDtypeStruct((1, D), F32), jax.ShapeDtypeStruct((1, LANES), F32)],
        scratch_shapes=[pltpu.VMEM((tm, D), F32)],
        compiler_params=_cp(("arbitrary", "arbitrary")),
    )(h2, x1, target, wg, wu, wd, g_post)


def _ffn_bwd(dff, h2, gs, us, wg, wu, wd, tm):
    T, D = h2.shape
    ns, F, _ = wg.shape

    def body(dff_ref, h_ref, gs_ref, us_ref, wg_ref, wu_ref, wd_ref, dh_ref, dwg_out, dwu_out, dwd_out,
             dwg_ref, dwu_ref, dwd_ref):
        first = pl.program_id(1) == 0
        dff = dff_ref[...]
        h = h_ref[...]
        parts = []
        for r in range(ROW_SPLIT):
            rows = slice(r * (tm // ROW_SPLIT), (r + 1) * (tm // ROW_SPLIT))
            dact = _dot_nt(dff[rows], wd_ref[0])
            g = gs_ref[0, rows, :].astype(F32)
            u = us_ref[0, rows, :].astype(F32)
            sig = _sigmoid(g)
            gsig = g * sig
            dg = (dact * u * (sig + gsig * (1.0 - sig))).astype(BF16)
            du = (dact * gsig).astype(BF16)
            dh_ref[0, rows, :] = (_dot(dg, wg_ref[0]) + _dot(du, wu_ref[0])).astype(BF16)
            parts.append(((gsig * u).astype(BF16), dg, du))
        a, dg, du = [jnp.concatenate(p, axis=0) for p in zip(*parts)]
        _acc(dwd_ref, _dot_tn(a, dff), first)
        _acc(dwg_ref, _dot_tn(dg, h), first)
        _acc(dwu_ref, _dot_tn(du, h), first)

        @pl.when(pl.program_id(1) == pl.num_programs(1) - 1)
        def _():
            dwg_out[0] = dwg_ref[...].astype(BF16)
            dwu_out[0] = dwu_ref[...].astype(BF16)
            dwd_out[0] = dwd_ref[...].astype(BF16)

    row = BS((tm, D), lambda j, i: (i, 0))
    sh = BS((1, tm, F), lambda j, i: (j, i, 0))
    wsh = BS((1, F, D), lambda j, i: (j, 0, 0))
    return pl.pallas_call(
        body, name="ffn_bwd", grid=(ns, T // tm),
        in_specs=[row, row, sh, sh, wsh, wsh, wsh],
        out_specs=[BS((1, tm, D), lambda j, i: (j, i, 0)), wsh, wsh, wsh],
        out_shape=[jax.ShapeDtypeStruct((ns, T, D), BF16)] + [jax.ShapeDtypeStruct((ns, F, D), BF16)] * 3,
        scratch_shapes=[pltpu.VMEM((F, D), F32)] * 3,
        compiler_params=_cp(("arbitrary", "arbitrary")),
    )(dff, h2, gs, us, wg, wu, wd)


def _mm_tn(a, b, name, tk):
    T, M = a.shape
    N = b.shape[1]
    tk = min(tk, T)

    def body(a_ref, b_ref, o_ref):
        _acc(o_ref, _dot_tn(a_ref[...], b_ref[...]), pl.program_id(0) == 0)

    return pl.pallas_call(
        body, name=name, grid=(T // tk,),
        in_specs=[BS((tk, M), lambda t: (t, 0)), BS((tk, N), lambda t: (t, 0))],
        out_specs=BS((M, N), lambda t: (0, 0)),
        out_shape=jax.ShapeDtypeStruct((M, N), F32),
        compiler_params=_cp(("arbitrary",)),
    )(a, b)


DPROJ_PIECES = ((0, A_W), (A_W, A_W), (768, B_W), (1152, B_W), (1536, B_W), (1920, LANES), (2048, M_W))


def _put_dproj(dp_ref, piece_refs):
    for (c0, w), ref in zip(DPROJ_PIECES, piece_refs):
        dp_ref[:, c0:c0 + w] = ref[...].astype(BF16)


def _dw_in(pieces, h, ns, tk):
    T, D = h.shape
    M = P_COLS
    dsh = D // ns
    tk = min(tk, T)

    def body(*refs):
        piece_refs, h_ref, o_ref, acc_ref, dp_ref = refs[:7], refs[7], refs[8], refs[9], refs[10]
        t = pl.program_id(0)
        _put_dproj(dp_ref, piece_refs)
        _acc(acc_ref, _dot_tn(h_ref[...], dp_ref[...]), t == 0)

        @pl.when(t == pl.num_programs(0) - 1)
        def _():
            for s in range(ns):
                o_ref[s] = acc_ref[s * dsh:(s + 1) * dsh, :].T.astype(BF16)

    return pl.pallas_call(
        body, name="dw_in", grid=(T // tk,),
        in_specs=[BS((tk, w), lambda t: (t, 0)) for _, w in DPROJ_PIECES] + [BS((tk, D), lambda t: (t, 0))],
        out_specs=BS((ns, M, dsh), lambda t: (0, 0, 0)),
        out_shape=jax.ShapeDtypeStruct((ns, M, dsh), BF16),
        scratch_shapes=[pltpu.VMEM((D, M), F32), pltpu.VMEM((tk, M), BF16)],
        compiler_params=_cp(("arbitrary",)),
    )(*pieces, h)


def _outproj_bwd(dh2, x1, dx2, o, ya, yb, ym, ga, gb, gm, g_post, g_pre2, w_out, tm):
    T, D = x1.shape
    ns = dh2.shape[0]

    def body(dh_ref, x1_ref, dx2_ref, o_ref, ya_ref, yb_ref, ym_ref, ga_ref, gb_ref, gm_ref, gp_ref, g2_ref, w_ref,
             dx1_ref, do_ref, dya_ref, dyb_ref, dym_ref, dga_ref, dgb_ref, dgm_ref, dgp_ref, dg2_ref):
        first = pl.program_id(0) == 0
        dh = dh_ref[0].astype(F32)
        for j in range(1, ns):
            dh = dh + dh_ref[j].astype(F32)
        dxa, dg2 = _rms_bwd(x1_ref[...], g2_ref[...], dh)
        dx1 = dx2_ref[...] + dxa
        dx1_ref[...] = dx1
        _acc(dg2_ref, dg2, first)
        do, dgp = _rms_bwd(o_ref[...], gp_ref[...], dx1)
        do = do.astype(BF16)
        do_ref[...] = do
        dy = _dot_nt(do, w_ref[...])
        dya, dga = _rms_bwd(ya_ref[...].astype(F32), ga_ref[...], dy[:, 0:A_W])
        dyb, dgb = _rms_bwd(yb_ref[...].astype(F32), gb_ref[...], dy[:, A_W:A_W + B_W])
        dym, dgm = _rms_bwd(ym_ref[...].astype(F32), gm_ref[...], dy[:, A_W + B_W:])
        dya_ref[...] = dya.astype(BF16)
        dyb_ref[...] = dyb.astype(BF16)
        dym_ref[...] = dym.astype(BF16)
        _acc(dga_ref, dga, first)
        _acc(dgb_ref, dgb, first)
        _acc(dgm_ref, dgm, first)
        _acc(dgp_ref, dgp, first)

    row = lambda w: BS((tm, w), lambda i: (i, 0))
    vec = lambda w: BS((1, w), lambda i: (0, 0))
    sds = jax.ShapeDtypeStruct
    return pl.pallas_call(
        body, name="outproj_bwd", grid=(T // tm,),
        in_specs=[BS((ns, tm, D), lambda i: (0, i, 0)), row(D), row(D), row(D), row(A_W), row(B_W), row(M_W),
                  vec(A_W), vec(B_W), vec(M_W), vec(D), vec(D), BS((A_W + B_W + M_W, D), lambda i: (0, 0))],
        out_specs=[row(D), row(D), row(A_W), row(B_W), row(M_W), vec(A_W), vec(B_W), vec(M_W), vec(D), vec(D)],
        out_shape=[sds((T, D), F32), sds((T, D), BF16), sds((T, A_W), BF16), sds((T, B_W), BF16), sds((T, M_W), BF16),
                   sds((1, A_W), F32), sds((1, B_W), F32), sds((1, M_W), F32), sds((1, D), F32), sds((1, D), F32)],
        compiler_params=_cp(("arbitrary",)),
    )(dh2, x1, dx2, o, ya, yb, ym, ga, gb, gm, g_post, g_pre2, w_out)


def _sgu_bwd(proj, dya, g_sgu, ws_tril, bs_full, tm):
    T = proj.shape[0]
    nch = tm // CHUNK

    def body(zu_ref, zv_ref, dy_ref, g_ref, ws_ref, b_ref, dzu_ref, dzv_ref, dws_ref, dbs_ref, dg_ref,
             du_ref, dvn_ref, dbf_ref):
        step = pl.program_id(0)
        first = step == 0
        lane = _iota((CHUNK, LANES), 1)
        tril = _iota((CHUNK, CHUNK), 0) >= _iota((CHUNK, CHUNK), 1)
        (u, vn), vjp = jax.vjp(_sgu_pre, zu_ref[...].astype(F32), zv_ref[...].astype(F32), g_ref[...])
        vnb = vn.astype(BF16)
        dy = dy_ref[...].astype(F32)

        @pl.when(first)
        def _():
            dws_ref[...] = jnp.zeros_like(dws_ref)
            dbf_ref[...] = jnp.zeros_like(dbf_ref)

        for c in range(nch):
            rs = slice(c * CHUNK, (c + 1) * CHUNK)
            for j in range(3):
                cs = slice(j * LANES, (j + 1) * LANES)
                vp = vnb[rs, cs]
                z = jnp.where(lane < HEAD, _dot(ws_ref[2 * j], vp), _dot(ws_ref[2 * j + 1], vp)) + b_ref[:, cs]
                du_ref[rs, cs] = dy[rs, cs] * z
                dz = dy[rs, cs] * u[rs, cs]
                dbf_ref[:, cs] += dz
                dzb = dz.astype(BF16)
                dz0 = jnp.where(lane < HEAD, dzb, jnp.zeros_like(dzb))
                dz1 = jnp.where(lane >= HEAD, dzb, jnp.zeros_like(dzb))
                dvn_ref[rs, cs] = jnp.where(lane < HEAD, _dot_tn(ws_ref[2 * j], dzb), _dot_tn(ws_ref[2 * j + 1], dzb))
                dws_ref[2 * j] += jnp.where(tril, _dot_nt(dz0, vp), 0.0)
                dws_ref[2 * j + 1] += jnp.where(tril, _dot_nt(dz1, vp), 0.0)
        dzu, dzv, dg = vjp((du_ref[...], dvn_ref[...]))
        dzu_ref[...] = dzu.astype(BF16)
        dzv_ref[...] = dzv.astype(BF16)
        _acc(dg_ref, dg, first)

        @pl.when(step == pl.num_programs(0) - 1)
        def _():
            out = jnp.zeros((CHUNK, LANES), F32)
            for j in range(3):
                slab = dbf_ref[:, j * LANES:(j + 1) * LANES]
                lo = jnp.sum(jnp.where(lane < HEAD, slab, 0.0), axis=1, keepdims=True)
                hi = jnp.sum(jnp.where(lane >= HEAD, slab, 0.0), axis=1, keepdims=True)
                out = out + jnp.where(lane == 2 * j, lo, 0.0) + jnp.where(lane == 2 * j + 1, hi, 0.0)
            dbs_ref[...] = out

    return pl.pallas_call(
        body, name="sgu_bwd", grid=(T // tm,),
        in_specs=[BS((tm, A_W), lambda i: (i, 0)), BS((tm, A_W), lambda i: (i, 1)), BS((tm, A_W), lambda i: (i, 0)),
                  BS((1, A_W), lambda i: (0, 0)), BS((6, CHUNK, CHUNK), lambda i: (0, 0, 0)),
                  BS((CHUNK, A_W), lambda i: (0, 0))],
        out_specs=[BS((tm, A_W), lambda i: (i, 0)), BS((tm, A_W), lambda i: (i, 0)),
                   BS((6, CHUNK, CHUNK), lambda i: (0, 0, 0)), BS((CHUNK, LANES), lambda i: (0, 0)),
                   BS((1, A_W), lambda i: (0, 0))],
        out_shape=[jax.ShapeDtypeStruct((T, A_W), BF16), jax.ShapeDtypeStruct((T, A_W), BF16),
                   jax.ShapeDtypeStruct((6, CHUNK, CHUNK), F32), jax.ShapeDtypeStruct((CHUNK, LANES), F32),
                   jax.ShapeDtypeStruct((1, A_W), F32)],
        scratch_shapes=[pltpu.VMEM((tm, A_W), F32), pltpu.VMEM((tm, A_W), F32), pltpu.VMEM((CHUNK, A_W), F32)],
        compiler_params=_cp(("arbitrary",)),
    )(proj, proj, dya, g_sgu, ws_tril, bs_full)


def _memattn_bwd(proj, kv, dym, Bl, S, tq):
    T = Bl * S
    nq = S // tq
    Mt = kv.shape[1]
    qc = 1920 // LANES

    def body(q_ref, km_ref, vm_ref, do_ref, dq_ref, dkm_ref, dvm_ref):
        first = pl.program_id(2) == 0
        lane = _iota((tq, LANES), 1)
        q = q_ref[...]
        do = do_ref[...]
        dq_out = jnp.zeros((tq, LANES), F32)
        dkm = jnp.zeros((Mt, LANES), F32)
        dvm = jnp.zeros((Mt, LANES), F32)
        for hh in range(2):
            hmask = (lane < HEAD) if hh == 0 else (lane >= HEAD)
            qs = jnp.where(hmask, q, jnp.zeros_like(q)) * 0.125
            dom = jnp.where(hmask, do, 0.0).astype(BF16)
            s = _dot_nt(qs, km_ref[0])
            pe = jnp.exp(s - jnp.max(s, axis=1, keepdims=True))
            pn = pe / jnp.sum(pe, axis=1, keepdims=True)
            dp = _dot_nt(dom, vm_ref[0])
            ds = (pn * (dp - jnp.sum(pn * dp, axis=1, keepdims=True))).astype(BF16)
            dq_out = jnp.where(hmask, _dot(ds, km_ref[0]) * 0.125, dq_out)
            dkm = dkm + _dot_tn(ds, qs)
            dvm = dvm + _dot_tn(pn, dom)
        dq_ref[...] = dq_out.astype(BF16)
        _acc(dkm_ref, dkm[None], first)
        _acc(dvm_ref, dvm[None], first)

    return pl.pallas_call(
        body, name="memattn_bwd", grid=(Bl, 2, nq),
        in_specs=[BS((tq, LANES), lambda b, p, i: (b * nq + i, qc + p)),
                  BS((1, Mt, LANES), lambda b, p, i: (b, 0, p)),
                  BS((1, Mt, LANES), lambda b, p, i: (b, 0, 2 + p)),
                  BS((tq, LANES), lambda b, p, i: (b * nq + i, p))],
        out_specs=[BS((tq, LANES), lambda b, p, i: (b * nq + i, p)),
                   BS((1, Mt, LANES), lambda b, p, i: (b, 0, p)),
                   BS((1, Mt, LANES), lambda b, p, i: (b, 0, p))],
        out_shape=[jax.ShapeDtypeStruct((T, M_W), BF16), jax.ShapeDtypeStruct((Bl, Mt, M_W), F32),
                   jax.ShapeDtypeStruct((Bl, Mt, M_W), F32)],
        compiler_params=_cp(("arbitrary", "arbitrary", "arbitrary")),
    )(proj, kv, kv, dym)


def _memkv_bwd(dkm, dvm, memn, mem, g_mem, w_kv):
    Bl, Mt, D = mem.shape

    def body(dk_ref, dv_ref, mn_ref, m_ref, g_ref, w_ref, dw_ref, dg_ref):
        first = pl.program_id(0) == 0
        dk = dk_ref[0].astype(BF16)
        dv = dv_ref[0].astype(BF16)
        mn = mn_ref[0]
        dmn = _dot_nt(dk, w_ref[:, 0:M_W]) + _dot_nt(dv, w_ref[:, M_W:])
        _, dg = _rms_bwd(m_ref[0], g_ref[...], dmn)
        _acc(dg_ref, dg, first)

        @pl.when(first)
        def _():
            dw_ref[...] = jnp.zeros_like(dw_ref)

        dw_ref[:, 0:M_W] += _dot_tn(mn, dk)
        dw_ref[:, M_W:] += _dot_tn(mn, dv)

    return pl.pallas_call(
        body, name="memkv_bwd", grid=(Bl,),
        in_specs=[BS((1, Mt, M_W), lambda b: (b, 0, 0)), BS((1, Mt, M_W), lambda b: (b, 0, 0)),
                  BS((1, Mt, D), lambda b: (b, 0, 0)), BS((1, Mt, D), lambda b: (b, 0, 0)),
                  BS((1, D), lambda b: (0, 0)), BS((D, 2 * M_W), lambda b: (0, 0))],
        out_specs=[BS((D, 2 * M_W), lambda b: (0, 0)), BS((1, D), lambda b: (0, 0))],
        out_shape=[jax.ShapeDtypeStruct((D, 2 * M_W), F32), jax.ShapeDtypeStruct((1, D), F32)],
        compiler_params=_cp(("arbitrary",)),
    )(dkm, dvm, memn, mem, g_mem, w_kv)


def _fox_bwd(proj, dyb, lse, bq, bk, Bl, S):
    T = Bl * S
    nq = S // Q_BLK
    nb = S // LANES
    qc, kc, vc = 768 // LANES, 1152 // LANES, 1536 // LANES

    def body(q_ref, k_ref, v_ref, do_ref, lse_ref, bq_ref, bk_ref,
             dq_ref, dk_ref, dv_ref, dcr_ref, ka_ref, dka_ref, dva_ref):
        p = pl.program_id(1)
        lane_s = _iota((S, LANES), 1)
        lane = _iota((Q_BLK, LANES), 1)
        sub = _iota((8, LANES), 0)
        tri = _iota((Q_BLK, Q_BLK), 1) <= _iota((Q_BLK, Q_BLK), 0)
        k = k_ref[...]
        for hh in range(2):
            data = (lane_s < HEAD) if hh == 0 else (lane_s >= HEAD)
            ka_ref[hh] = jnp.where(data, k, bk_ref[0, hh])
        dka_ref[...] = jnp.zeros_like(dka_ref)
        dva_ref[...] = jnp.zeros_like(dva_ref)

        @pl.when(p == 0)
        def _():
            dcr_ref[...] = jnp.zeros_like(dcr_ref)

        def add_colsums(ds, first_blk, h):
            cs = _colsum(ds)
            for jb in range(ds.shape[1] // LANES):
                dcr_ref[0, first_blk + jb] += jnp.where(sub == h, cs[:, jb * LANES:(jb + 1) * LANES], 0.0)

        for i in range(nq):
            r0 = i * Q_BLK
            r1 = r0 + Q_BLK
            q = q_ref[r0:r1, :]
            do = do_ref[r0:r1, :]
            lse_b = lse_ref[0, r0:r1, :]
            dq_out = jnp.zeros((Q_BLK, LANES), F32)
            for hh in range(2):
                hmask = (lane < HEAD) if hh == 0 else (lane >= HEAD)
                h = 2 * p + hh
                qs = jnp.where(hmask, q * 0.125, jnp.zeros_like(q))
                qa = jnp.where(hmask, q * 0.125, bq_ref[0, hh, r0:r1, :])
                dob = jnp.where(hmask, do, 0.0).astype(BF16)
                lse_h = jnp.sum(jnp.where(lane == hh * HEAD, lse_b, 0.0), axis=1, keepdims=True)
                pd = jnp.where(tri, jnp.exp(_dot_nt(qa, ka_ref[hh, r0:r1, :]) - lse_h), 0.0)
                dpd = _dot_nt(dob, v_ref[r0:r1, :])
                delta = jnp.sum(pd * dpd, axis=1, keepdims=True)
                psum = jnp.sum(pd, axis=1, keepdims=True)
                if i:
                    pf = jnp.exp(_dot_nt(qa, ka_ref[hh, 0:r0, :]) - lse_h)
                    dpf = _dot_nt(dob, v_ref[0:r0, :])
                    delta = delta + jnp.sum(pf * dpf, axis=1, keepdims=True)
                    psum = psum + jnp.sum(pf, axis=1, keepdims=True)
                delta = delta / psum
                dsd = pd * (dpd - delta)
                add_colsums(dsd, r0 // LANES, h)
                dsd = dsd.astype(BF16)
                dq_h = _dot(dsd, k_ref[r0:r1, :])
                dka_ref[r0:r1, :] += _dot_tn(dsd, qs)
                dva_ref[r0:r1, :] += _dot_tn(pd, dob)
                if i:
                    dsf = pf * (dpf - delta)
                    add_colsums(dsf, 0, h)
                    dsf = dsf.astype(BF16)
                    dq_h = dq_h + _dot(dsf, k_ref[0:r0, :])
                    dka_ref[0:r0, :] += _dot_tn(dsf, qs)
                    dva_ref[0:r0, :] += _dot_tn(pf, dob)
                dq_out = jnp.where(hmask, dq_h * 0.125, dq_out)
            dq_ref[r0:r1, :] = dq_out.astype(BF16)
        dk_ref[...] = dka_ref[...].astype(BF16)
        dv_ref[...] = dva_ref[...].astype(BF16)

    seq = lambda c0: BS((S, LANES), lambda b, p: (b, c0 + p))
    pair = BS((1, 2, S, LANES), lambda b, p: (b, p, 0, 0))
    rowblk = BS((1, nb, 8, LANES), lambda b, p: (b, 0, 0, 0))
    return pl.pallas_call(
        body, name="fox_bwd", grid=(Bl, 3),
        in_specs=[seq(qc), seq(kc), seq(vc), seq(0), BS((1, S, LANES), lambda b, p: (p, b, 0)), pair, pair],
        out_specs=[seq(0), seq(0), seq(0), rowblk],
        out_shape=[jax.ShapeDtypeStruct((T, B_W), BF16)] * 3 + [jax.ShapeDtypeStruct((Bl, nb, 8, LANES), F32)],
        scratch_shapes=[pltpu.VMEM((2, S, LANES), BF16), pltpu.VMEM((S, LANES), F32), pltpu.VMEM((S, LANES), F32)],
        compiler_params=_cp(("arbitrary", "arbitrary")),
    )(proj, proj, proj, dyb, lse, bq, bk)


def _gate_bwd(dc_row, fl_row):
    Bl, nb, _, _ = dc_row.shape

    def body(dc_ref, fl_ref, o_ref):
        lane = _iota((8, LANES), 1)

        carry = jnp.zeros((8, 1), F32)
        for j in reversed(range(nb)):
            r = -dc_ref[0, j]
            for k in (1, 2, 4, 8, 16, 32, 64):
                r = r + jnp.where(lane < LANES - k, pltpu.roll(r, LANES - k, 1), 0.0)
            total = jnp.sum(jnp.where(lane == 0, r, 0.0), axis=1, keepdims=True)
            dfl = (r + carry) * _sigmoid(-fl_ref[0, j])
            carry = carry + total
            o_ref[0, j * LANES:(j + 1) * LANES, :] = jnp.concatenate(
                [dfl, jnp.zeros((LANES - 8, LANES), F32)], axis=0).T

    rowblk = BS((1, nb, 8, LANES), lambda b: (b, 0, 0, 0))
    return pl.pallas_call(
        body, name="gate_bwd", grid=(Bl,),
        in_specs=[rowblk, rowblk],
        out_specs=BS((1, nb * LANES, LANES), lambda b: (b, 0, 0)),
        out_shape=jax.ShapeDtypeStruct((Bl, nb * LANES, LANES), F32),
        compiler_params=_cp(("arbitrary",)),
    )(dc_row, fl_row)


def _inproj_bwd(pieces, x2d, dx1, g_pre, w_in_p, tm):
    T, D = x2d.shape
    ns, _, dsh = w_in_p.shape

    def body(*refs):
        piece_refs = refs[:7]
        x_ref, dx1_ref, g_ref, w_ref, gx_ref, dg_ref, dbf_ref, dp_ref = refs[7:]
        first = pl.program_id(0) == 0
        _put_dproj(dp_ref, piece_refs)
        dh = jnp.concatenate([_dot(dp_ref[...], w_ref[s]) for s in range(ns)], axis=1)
        dxa, dg = _rms_bwd(x_ref[...], g_ref[...], dh)
        gx_ref[...] = dx1_ref[...] + dxa
        _acc(dg_ref, dg, first)
        _acc(dbf_ref, _colsum(piece_refs[5][...]), first)

    row = lambda w: BS((tm, w), lambda i: (i, 0))
    return pl.pallas_call(
        body, name="inproj_bwd", grid=(T // tm,),
        in_specs=[row(w) for _, w in DPROJ_PIECES] + [row(D), row(D), BS((1, D), lambda i: (0, 0)),
                                                      BS((ns, P_COLS, dsh), lambda i: (0, 0, 0))],
        out_specs=[row(D), BS((1, D), lambda i: (0, 0)), BS((1, LANES), lambda i: (0, 0))],
        out_shape=[jax.ShapeDtypeStruct((T, D), F32), jax.ShapeDtypeStruct((1, D), F32),
                   jax.ShapeDtypeStruct((1, LANES), F32)],
        scratch_shapes=[pltpu.VMEM((tm, P_COLS), BF16)],
        compiler_params=_cp(("arbitrary",)),
    )(*pieces, x2d, dx1, g_pre, w_in_p)


def _local_step(x, mem, target, W, P, reduce=None):
    Bl, S, D = x.shape
    T = Bl * S
    tm = min(512, T)
    x2d = x.reshape(T, D)
    t2d = target.reshape(T, D)
    vec = lambda a: a.reshape(1, -1)
    bf_row = jnp.pad(P["b_f"].reshape(1, -1), ((0, 0), (0, LANES - N_FOX_HEADS)))
    tril = jnp.tril(jnp.ones((CHUNK, CHUNK), bool))
    ws_tril = jnp.where(tril[None], P["w_s"][0], 0.0).astype(BF16)
    bs_full = jnp.repeat(P["b_s"][0].T, HEAD, axis=1)
    g_pre, g_sgu = vec(P["g_pre_mix"]), vec(P["g_sgu"])
    ga, gb, gm = vec(P["g_out_a"]), vec(P["g_out_b"]), vec(P["g_out_m"])
    g_mem, g_post, g_pre2, g_post2 = vec(P["g_mem"]), vec(P["g_post_mix"]), vec(P["g_pre_ffn"]), vec(P["g_post_ffn"])

    h, proj, flog = _inproj_fwd(x2d, g_pre, W["w_in"], tm)
    bq, bk, fl_row = _gate_fwd(flog.reshape(Bl, S, LANES), bf_row)
    ya = _sgu_fwd(proj, g_sgu, ws_tril, bs_full, tm)
    yb, lse = _fox_fwd(proj, bq, bk, Bl, S)
    memn, kv = _memkv_fwd(mem, g_mem, W["w_mem_kv"])
    ym = _memattn_fwd(proj, kv, Bl, S, min(512, S))
    y, o, x1, h2 = _outproj_fwd(ya, yb, ym, x2d, ga, gb, gm, g_post, g_pre2, W["w_out"], tm)
    gs, us, dff, dx2, dg_post2, loss = _ffn_fwd(h2, x1, t2d, W["w_gate"], W["w_up"], W["w_down"], g_post2, tm)

    dh2, d_w_gate, d_w_up, d_w_down = _ffn_bwd(dff, h2, gs, us, W["w_gate"], W["w_up"], W["w_down"], min(1024, T))
    ffn = [d_w_gate, d_w_up, d_w_down]
    if reduce is not None:
        pending, _ = reduce.begin("ffn", ffn)
    dx1, do, dya, dyb, dym, dga, dgb, dgm, dg_post, dg_pre2 = _outproj_bwd(
        dh2, x1, dx2, o, ya, yb, ym, ga, gb, gm, g_post, g_pre2, W["w_out"], tm)
    if reduce is not None:
        ffn, (do, dya, dyb, dym) = reduce.finish("ffn", pending, (do, dya, dyb, dym))
    d_w_out = _mm_tn(y, do, "dw_out", 1024)
    dzu, dzv, dws, dbs_cols, dg_sgu = _sgu_bwd(proj, dya, g_sgu, ws_tril, bs_full, tm)
    dqm, dkm, dvm = _memattn_bwd(proj, kv, dym, Bl, S, min(512, S))
    d_w_kv, dg_mem = _memkv_bwd(dkm, dvm, memn, mem, g_mem, W["w_mem_kv"])
    mid = [d_w_kv, d_w_out]
    dq, dk, dv, dc_row = _fox_bwd(proj, dyb, lse, bq, bk, Bl, S)
    if reduce is not None:
        done = reduce.apply(BIG[3:], ffn)
        pending, after = reduce.begin("mid", mid, (dc_row,) + done)
        dc_row = after[0]
    dfl = _gate_bwd(dc_row, fl_row).reshape(T, LANES)
    pieces = (dzu, dzv, dq, dk, dv, dfl, dqm)
    grad_x, dg_pre, dbf = _inproj_bwd(pieces, x2d, dx1, g_pre, W["w_in"], tm)
    if reduce is not None:
        mid, (dfl,) = reduce.finish("mid", pending, (dfl,))
        pieces = (dzu, dzv, dq, dk, dv, dfl, dqm)
    d_w_in = _dw_in(pieces, h, W["w_in"].shape[0], 1024)
    if reduce is None:
        big = dict(zip(BIG, [d_w_in] + mid + ffn))
    else:
        done = reduce.apply(BIG[1:3], mid)
        big = {"w_in": reduce.begin("in", [d_w_in], done)[0]}
    small = {"g_pre_mix": dg_pre, "b_f": dbf[:, :N_FOX_HEADS], "g_sgu": dg_sgu, "w_s": dws, "b_s": dbs_cols[:, :N_FOX_HEADS].T,
             "g_out_a": dga, "g_out_b": dgb, "g_out_m": dgm, "g_mem": dg_mem, "g_post_mix": dg_post,
             "g_pre_ffn": dg_pre2, "g_post_ffn": dg_post2, "loss": loss[:, :1]}
    return grad_x.reshape(Bl, S, D), big, small


def _place():
    return lax.axis_index("x"), lax.axis_index("y"), lax.axis_index("c")


def _exchange_on_sequencer(srcs, own_full, name, collective_id):
    n = len(srcs)

    def body(*refs):
        src, dst = refs[:n], refs[n:2 * n]
        lsem, isend, irecv, dsend, drecv = refs[2 * n:]
        x, y, c = _place()
        oc = 1 - c
        s_me = 2 * x + y
        sib = (x, y, oc)
        chips = [(1 - x, y), (x, 1 - y), (1 - x, 1 - y)]
        barrier = pltpu.get_barrier_semaphore()
        for dev in [(cx, cy, c) for cx, cy in chips] + [sib]:
            pl.semaphore_signal(barrier, inc=1, device_id=dev, device_id_type=MESH)
        pl.semaphore_wait(barrier, 4)

        def remote(a, b, ssem, rsem, dev):
            return pltpu.make_async_remote_copy(src_ref=a, dst_ref=b, send_sem=ssem, recv_sem=rsem,
                                                device_id=dev, device_id_type=MESH)

        sends, local = [], []
        for w in range(n):
            for j, (cx, cy) in enumerate(chips):
                half = src[w].at[c] if own_full else src[w].at[2 * cx + cy]
                cp = remote(half, dst[w].at[s_me, c], isend.at[w, j], irecv.at[w, j], (cx, cy, c))
                cp.start()
                sends.append(cp)
            if own_full:
                cp = remote(src[w], dst[w].at[s_me], dsend.at[w, 3], drecv.at[w, 3], sib)
            else:
                cp = remote(src[w].at[s_me], dst[w].at[s_me, c], dsend.at[w, 3], drecv.at[w, 3], sib)
                loc = pltpu.make_async_copy(src[w].at[s_me], dst[w].at[s_me, c], lsem.at[w])
                loc.start()
                local.append(loc)
            cp.start()
            sends.append(cp)
        for w in range(n):
            for j, (cx, cy) in enumerate(chips):
                landed = dst[w].at[2 * cx + cy, c]
                remote(landed, landed, isend.at[w, j], irecv.at[w, j], (cx, cy, c)).wait_recv()
                cp = remote(landed, landed, dsend.at[w, j], drecv.at[w, j], sib)
                cp.start()
                sends.append(cp)
        for w in range(n):
            for j, (cx, cy) in enumerate(chips):
                landed = dst[w].at[2 * cx + cy, oc]
                remote(landed, landed, dsend.at[w, j], drecv.at[w, j], sib).wait_recv()
            landed = dst[w].at[s_me] if own_full else dst[w].at[s_me, oc]
            remote(landed, landed, dsend.at[w, 3], drecv.at[w, 3], sib).wait_recv()
        for cp in sends:
            cp.wait_send()
        for loc in local:
            loc.wait()

    return pl.kernel(
        body, out_type=[jax.ShapeDtypeStruct((4, 2) + s.shape[1:], s.dtype) for s in srcs],
        mesh=plsc.ScalarSubcoreMesh(axis_name="sequencer", num_cores=1), name=name,
        scratch_types=[pltpu.SemaphoreType.DMA((n,)), pltpu.SemaphoreType.DMA((n, 3)), pltpu.SemaphoreType.DMA((n, 3)),
                       pltpu.SemaphoreType.DMA((n, 4)), pltpu.SemaphoreType.DMA((n, 4))],
        compiler_params=pltpu.CompilerParams(collective_id=collective_id),
    )(*srcs)


def _sibling_swap(grads, name, collective_id):
    n = len(grads)

    def body(*refs):
        g, theirs = refs[:n], refs[n:2 * n]
        ssem, rsem = refs[2 * n:]
        x, y, c = _place()
        sib = (x, y, 1 - c)
        barrier = pltpu.get_barrier_semaphore()
        pl.semaphore_signal(barrier, inc=1, device_id=sib, device_id_type=MESH)
        pl.semaphore_wait(barrier, 1)
        cps = []
        for w in range(n):
            cp = pltpu.make_async_remote_copy(src_ref=g[w].at[:, 1 - c], dst_ref=theirs[w], send_sem=ssem.at[w],
                                              recv_sem=rsem.at[w], device_id=sib, device_id_type=MESH)
            cp.start()
            cps.append(cp)
        for cp in cps:
            cp.wait()

    return pl.kernel(
        body, out_type=[jax.ShapeDtypeStruct((4,) + g.shape[2:], g.dtype) for g in grads],
        mesh=plsc.ScalarSubcoreMesh(axis_name="sequencer", num_cores=1), name=name,
        scratch_types=[pltpu.SemaphoreType.DMA((n,)), pltpu.SemaphoreType.DMA((n,))],
        compiler_params=pltpu.CompilerParams(collective_id=collective_id),
    )(*grads)


def _add_pair(core, g, theirs, name):
    _, _, hr, C = g.shape

    def body(core_ref, g_ref, t_ref, o_ref):
        o_ref[0] = (g_ref[0, 0].astype(F32) + t_ref[0].astype(F32)).astype(BF16)

    blk = BS((1, hr, C), lambda s, core_ref: (s, 0, 0))
    return pl.pallas_call(
        body, name=name,
        grid_spec=pltpu.PrefetchScalarGridSpec(
            num_scalar_prefetch=1, grid=(4,),
            in_specs=[BS((1, 1, hr, C), lambda s, core_ref: (s, core_ref[0], 0, 0)), blk], out_specs=blk),
        out_shape=jax.ShapeDtypeStruct(theirs.shape, BF16), compiler_params=_cp(("arbitrary",)))(core, g, theirs)


def _sum_chips(r, name):
    _, _, hr, C = r.shape

    def body(r_ref, o_ref):
        o_ref[...] = ((r_ref[0, 0].astype(F32) + r_ref[1, 0].astype(F32)) + r_ref[2, 0].astype(F32)) + r_ref[3, 0].astype(F32)

    return pl.pallas_call(body, name=name, grid=(2,), in_specs=[BS((4, 1, hr, C), lambda h: (0, h, 0, 0))],
                          out_specs=BS((hr, C), lambda h: (h, 0)), out_shape=jax.ShapeDtypeStruct((2 * hr, C), F32),
                          compiler_params=_cp(("arbitrary",)))(r)


class _Reducer:
    IDS = {"ffn": (4, 5), "mid": (6, 7), "in": (8, 9)}

    def __init__(self, core, apply):
        self.core = core
        self.apply = apply

    def begin(self, tag, grads, after=()):
        grads, after = lax.optimization_barrier((list(grads), after))
        g4 = [g.reshape(4, 2, -1, g.shape[-1]) for g in grads]
        return (g4, _sibling_swap(g4, "swap_" + tag, self.IDS[tag][0])), after

    def finish(self, tag, pending, hold):
        g4, theirs = pending
        sums = [_add_pair(self.core, g, t, "chip_sum_%s_%d" % (tag, k)) for k, (g, t) in enumerate(zip(g4, theirs))]
        sums, hold = lax.optimization_barrier((sums, hold))
        return _exchange_on_sequencer(sums, False, "scatter_" + tag, self.IDS[tag][1]), hold


def _small_allreduce(part):
    R = part.shape[0]
    rs = R // 8
    masks = [(mx, my, mc) for mx in (0, 1) for my in (0, 1) for mc in (0, 1)][1:]

    def body(p_ref, o_ref, buf_ref, s1, r1, s2, r2):
        x, y, c = _place()
        d = 4 * x + 2 * y + c
        mine = pl.ds(pl.multiple_of(d * rs, 8), rs)
        peers = [((x + mx) % 2, (y + my) % 2, (c + mc) % 2) for mx, my, mc in masks]
        first, second = [], []
        for k, (px, py, pc) in enumerate(peers):
            theirs = pl.ds(pl.multiple_of((4 * px + 2 * py + pc) * rs, 8), rs)
            cp = pltpu.make_async_remote_copy(src_ref=p_ref.at[theirs, :], dst_ref=buf_ref.at[d], send_sem=s1.at[k],
                                              recv_sem=r1.at[k], device_id=(px, py, pc), device_id_type=MESH)
            cp.start()
            first.append(cp)
        buf_ref[d] = p_ref[mine, :]
        for k, (px, py, pc) in enumerate(peers):
            slot = buf_ref.at[4 * px + 2 * py + pc]
            pltpu.make_async_remote_copy(src_ref=slot, dst_ref=slot, send_sem=s1.at[k], recv_sem=r1.at[k],
                                         device_id=(px, py, pc), device_id_type=MESH).wait_recv()
        total = buf_ref[0]
        for k in range(1, 8):
            total = total + buf_ref[k]
        o_ref[mine, :] = total
        for k, (px, py, pc) in enumerate(peers):
            cp = pltpu.make_async_remote_copy(src_ref=o_ref.at[mine, :], dst_ref=o_ref.at[mine, :], send_sem=s2.at[k],
                                              recv_sem=r2.at[k], device_id=(px, py, pc), device_id_type=MESH)
            cp.start()
            second.append(cp)
        for k, (px, py, pc) in enumerate(peers):
            rows = o_ref.at[pl.ds(pl.multiple_of((4 * px + 2 * py + pc) * rs, 8), rs), :]
            pltpu.make_async_remote_copy(src_ref=rows, dst_ref=rows, send_sem=s2.at[k], recv_sem=r2.at[k],
                                         device_id=(px, py, pc), device_id_type=MESH).wait_recv()
        for cp in first + second:
            cp.wait_send()

    vm = pl.BlockSpec(memory_space=pltpu.VMEM)
    return pl.pallas_call(
        body, name="small_allreduce", in_specs=[vm], out_specs=vm, out_shape=jax.ShapeDtypeStruct(part.shape, F32),
        scratch_shapes=[pltpu.VMEM((8, rs, LANES), F32)] + [pltpu.SemaphoreType.DMA((7,))] * 4,
    )(part)


def _adamw(w, g, m, v, name):
    R, C = w.shape
    summed = g.ndim == 4
    if summed:
        tr = R // 2
    else:
        tr = R if R * C * 4 <= (1 << 21) else R // 2
        if tr % 8:
            tr = R
    c1 = 1.0 / (1.0 - ADAM_B1 ** ADAM_STEP)
    c2 = 1.0 / (1.0 - ADAM_B2 ** ADAM_STEP)

    def body(w_ref, g_ref, m_ref, v_ref, *outs):
        if summed:
            g_ = ((g_ref[0, 0].astype(F32) + g_ref[1, 0].astype(F32)) + g_ref[2, 0].astype(F32)) + g_ref[3, 0].astype(F32)
            outs[0][...] = g_
        else:
            g_ = g_ref[...]
        d_ref, mo_ref, vo_ref = outs[-3:]
        m_ = ADAM_B1 * m_ref[...] + (1.0 - ADAM_B1) * g_
        v_ = ADAM_B2 * v_ref[...] + (1.0 - ADAM_B2) * (g_ * g_)
        mo_ref[...] = m_
        vo_ref[...] = v_
        d_ref[...] = -ADAM_LR * ((m_ * c1) / (jnp.sqrt(v_ * c2) + ADAM_EPS) + ADAM_WD * w_ref[...])

    blk = BS((tr, C), lambda i: (i, 0))
    g_blk = BS((4, 1, tr, C), lambda i: (0, i, 0, 0)) if summed else blk
    nout = 4 if summed else 3
    return pl.pallas_call(body, name=name, grid=(R // tr,), in_specs=[blk, g_blk, blk, blk], out_specs=[blk] * nout,
                          out_shape=[jax.ShapeDtypeStruct((R, C), F32)] * nout,
                          compiler_params=_cp(("arbitrary",)))(w, g, m, v)


def _adamw_from_transposed(w, g_t, m, v, name):
    R, C = w.shape
    tc = 256
    c1 = 1.0 / (1.0 - ADAM_B1 ** ADAM_STEP)
    c2 = 1.0 / (1.0 - ADAM_B2 ** ADAM_STEP)

    def body(w_ref, g_ref, m_ref, v_ref, go_ref, d_ref, mo_ref, vo_ref):
        g_ = g_ref[...].T
        go_ref[...] = g_
        m_ = ADAM_B1 * m_ref[...] + (1.0 - ADAM_B1) * g_
        v_ = ADAM_B2 * v_ref[...] + (1.0 - ADAM_B2) * (g_ * g_)
        mo_ref[...] = m_
        vo_ref[...] = v_
        d_ref[...] = -ADAM_LR * ((m_ * c1) / (jnp.sqrt(v_ * c2) + ADAM_EPS) + ADAM_WD * w_ref[...])

    blk = BS((R, tc), lambda i: (0, i))
    return pl.pallas_call(body, name=name, grid=(pl.cdiv(C, tc),), in_specs=[blk, BS((tc, R), lambda i: (i, 0)), blk, blk],
                          out_specs=[blk] * 4, out_shape=[jax.ShapeDtypeStruct((R, C), F32)] * 4,
                          compiler_params=_cp(("arbitrary",)))(w, g_t, m, v)


SMALL = ("g_pre_mix", "b_f", "g_sgu", "w_s", "b_s", "g_out_a", "g_out_b", "g_out_m", "g_mem", "g_post_mix",
         "g_pre_ffn", "g_post_ffn")
BIG = ("w_in", "w_mem_kv", "w_out", "w_gate", "w_up", "w_down")
TRANSPOSED = ("w_in", "w_gate", "w_up")
WEIGHTS = ("g_pre_mix", "w_in", "b_f", "g_sgu", "w_s", "b_s", "g_out_a", "g_out_b", "g_out_m", "g_mem", "w_mem_kv",
           "w_out", "g_post_mix", "g_pre_ffn", "w_gate", "w_up", "w_down", "g_post_ffn")


def _rows_of(n):
    return -(-n // (8 * LANES)) * 8


def _pack(parts):
    tiles = []
    for a in parts:
        flat = a.reshape(-1).astype(F32)
        rows = _rows_of(flat.shape[0])
        tiles.append(jnp.pad(flat, (0, rows * LANES - flat.shape[0])).reshape(rows, LANES))
    total = sum(t.shape[0] for t in tiles)
    pad = -total % 64
    if pad:
        tiles.append(jnp.zeros((pad, LANES), F32))
    return jnp.concatenate(tiles, axis=0)


def _unpack(packed, shapes):
    out, r = [], 0
    for shp in shapes:
        n = 1
        for s in shp:
            n *= s
        rows = _rows_of(n)
        out.append(packed[r:r + rows].reshape(-1)[:n].reshape(shp))
        r += rows
    return out


def kernel(x, mem, g_pre_mix, w_in, b_f, g_sgu, w_s, b_s, g_out_a, g_out_b, g_out_m, g_mem, w_mem_kv, w_out, g_post_mix, g_pre_ffn, w_gate, w_up, w_down, g_post_ffn, loss_target, m_g_pre_mix, m_w_in, m_b_f, m_g_sgu, m_w_s, m_b_s, m_g_out_a, m_g_out_b, m_g_out_m, m_g_mem, m_w_mem_kv, m_w_out, m_g_post_mix, m_g_pre_ffn, m_w_gate, m_w_up, m_w_down, m_g_post_ffn, v_g_pre_mix, v_w_in, v_b_f, v_g_sgu, v_w_s, v_b_s, v_g_out_a, v_g_out_b, v_g_out_m, v_g_mem, v_w_mem_kv, v_w_out, v_g_post_mix, v_g_pre_ffn, v_w_gate, v_w_up, v_w_down, v_g_post_ffn):
    Wt = dict(g_pre_mix=g_pre_mix, w_in=w_in, b_f=b_f, g_sgu=g_sgu, w_s=w_s, b_s=b_s, g_out_a=g_out_a, g_out_b=g_out_b,
              g_out_m=g_out_m, g_mem=g_mem, w_mem_kv=w_mem_kv, w_out=w_out, g_post_mix=g_post_mix, g_pre_ffn=g_pre_ffn,
              w_gate=w_gate, w_up=w_up, w_down=w_down, g_post_ffn=g_post_ffn)
    Mo = dict(g_pre_mix=m_g_pre_mix, w_in=m_w_in, b_f=m_b_f, g_sgu=m_g_sgu, w_s=m_w_s, b_s=m_b_s, g_out_a=m_g_out_a,
              g_out_b=m_g_out_b, g_out_m=m_g_out_m, g_mem=m_g_mem, w_mem_kv=m_w_mem_kv, w_out=m_w_out,
              g_post_mix=m_g_post_mix, g_pre_ffn=m_g_pre_ffn, w_gate=m_w_gate, w_up=m_w_up, w_down=m_w_down,
              g_post_ffn=m_g_post_ffn)
    Vo = dict(g_pre_mix=v_g_pre_mix, w_in=v_w_in, b_f=v_b_f, g_sgu=v_g_sgu, w_s=v_w_s, b_s=v_b_s, g_out_a=v_g_out_a,
              g_out_b=v_g_out_b, g_out_m=v_g_out_m, g_mem=v_g_mem, w_mem_kv=v_w_mem_kv, w_out=v_w_out,
              g_post_mix=v_g_post_mix, g_pre_ffn=v_g_pre_ffn, w_gate=v_w_gate, w_up=v_w_up, w_down=v_w_down,
              g_post_ffn=v_g_post_ffn)

    gap = P_COLS - IN_COLS

    def to_kernel(n, w):
        if n in TRANSPOSED:
            w = w.T
        if n == "w_in":
            w = jnp.pad(w[:F_END], ((0, P_COLS - F_END), (0, 0))) + jnp.pad(w[F_END:], ((F_END + gap, 0), (0, 0)))
        return w

    def ungroup(g):
        return jnp.pad(g[:F_END], ((0, IN_COLS - F_END), (0, 0))) + jnp.pad(g[F_END + gap:], ((F_END, 0), (0, 0)))

    shards = {n: to_kernel(n, Wt[n][0]) for n in BIG}
    srcs = [shards[n].astype(BF16).reshape(2, shards[n].shape[0] // 2, shards[n].shape[1]) for n in BIG]
    fulls = (_exchange_on_sequencer(srcs[:1], True, "gather_w_in", 1)
             + _exchange_on_sequencer(srcs[1:3], True, "gather_kv_out", 2)
             + _exchange_on_sequencer(srcs[3:], True, "gather_ffn", 3))
    W = {}
    for n, f in zip(BIG, fulls):
        _, _, hr, C = f.shape
        W[n] = f.reshape(8 * hr, C) if n in ("w_mem_kv", "w_out") else f.reshape(4, 2 * hr, C)

    P = {n: Wt[n] for n in SMALL}
    grads, deltas, new_m, new_v = {}, {}, {}, {}

    def apply(names, landed):
        for n, r in zip(names, landed):
            if n == "w_in":
                g_t = ungroup(_sum_chips(r, "sum_chips_" + n))
                flat = lambda a: a.reshape(-1, LANES)
                outs = _adamw(flat(Wt[n][0].T), flat(g_t), flat(Mo[n][0].T), flat(Vo[n][0].T), "adamw_" + n)
                g, d, m1, v1 = [a.reshape(g_t.shape).T for a in (flat(g_t),) + tuple(outs)]
            elif n in TRANSPOSED:
                g, d, m1, v1 = [a.T for a in _adamw(Wt[n][0].T, r, Mo[n][0].T, Vo[n][0].T, "adamw_" + n)]
            else:
                g, d, m1, v1 = _adamw(Wt[n][0], r, Mo[n][0], Vo[n][0], "adamw_" + n)
            grads[n], deltas[n], new_m[n], new_v[n] = g[None], d[None], m1[None], v1[None]
        return tuple(deltas[n] for n in names)

    core = lax.axis_index("c").astype(jnp.int32).reshape(1)
    reducer = _Reducer(core, apply)
    grad_x, pending, small = _local_step(x, mem, loss_target, W, P, reducer)

    total = _small_allreduce(_pack([small[n] for n in SMALL] + [small["loss"]]))
    landed, (total,) = reducer.finish("in", pending["w_in"], (total,))
    apply(BIG[:1], landed)

    slot = [jnp.zeros((1, 1), F32)]
    shapes = [Wt[n].shape for n in SMALL] + [(1, 1)]
    d, m1, v1 = _adamw(_pack([Wt[n] for n in SMALL] + slot), total, _pack([Mo[n] for n in SMALL] + slot),
                       _pack([Vo[n] for n in SMALL] + slot), "adamw_small")
    g_s, d_s, m_s, v_s = _unpack(total, shapes), _unpack(d, shapes), _unpack(m1, shapes), _unpack(v1, shapes)
    for k, n in enumerate(SMALL):
        grads[n], deltas[n], new_m[n], new_v[n] = g_s[k], d_s[k], m_s[k], v_s[k]
    loss = g_s[-1][0, 0]

    return (loss, grad_x, *[grads[n] for n in WEIGHTS], *[deltas[n] for n in WEIGHTS],
            *[new_m[n] for n in WEIGHTS], *[new_v[n] for n in WEIGHTS])
```

```python
import functools

import jax
import jax.numpy as jnp
from jax import lax
from jax.experimental import pallas as pl
from jax.experimental.pallas import tpu as pltpu
from jax.experimental.pallas import tpu_sc as plsc

F32 = jnp.float32
BF16 = jnp.bfloat16
EPS = 1e-6
NEG = -1e30
HEAD = 64
A_W, B_W, M_W = 384, 384, 256
N_FOX_HEADS = 6
CHUNK = 128
IN_COLS = 2 * A_W + 3 * B_W + N_FOX_HEADS + M_W
P_MAIN = 2 * A_W + 3 * B_W + M_W
P_COLS = P_MAIN + 128
F_END = 2 * A_W + 3 * B_W + N_FOX_HEADS
LANES = 128
Q_BLK, K_BLK = 256, 128
ROW_SPLIT = 4
ADAM_LR, ADAM_B1, ADAM_B2, ADAM_EPS, ADAM_WD, ADAM_STEP = 0.001, 0.9, 0.999, 1e-08, 0.01, 10
VMEM_LIMIT = 56 * 1024 * 1024
MESH = pl.DeviceIdType.MESH
ANY = pl.BlockSpec(memory_space=pl.ANY)
BS = pl.BlockSpec


def _cp(sem=None):
    return pltpu.CompilerParams(dimension_semantics=sem, vmem_limit_bytes=VMEM_LIMIT)


def _iota(shape, dim):
    return lax.broadcasted_iota(jnp.int32, shape, dim)


def _dot(a, b):
    return jnp.dot(a.astype(BF16), b.astype(BF16), preferred_element_type=F32)


def _dot_nt(a, b):
    return lax.dot_general(a.astype(BF16), b.astype(BF16), (((1,), (1,)), ((), ())), preferred_element_type=F32)


def _dot_tn(a, b):
    return lax.dot_general(a.astype(BF16), b.astype(BF16), (((0,), (0,)), ((), ())), preferred_element_type=F32)


def _rms(x, g):
    return x * lax.rsqrt(jnp.mean(x * x, axis=-1, keepdims=True) + EPS) * g


def _rms_bwd(x, g, dy):
    r = lax.rsqrt(jnp.mean(x * x, axis=-1, keepdims=True) + EPS)
    xr = x * r
    gd = dy * g
    m = jnp.mean(gd * xr, axis=-1, keepdims=True)
    return (gd - xr * m) * r, _colsum(dy * xr)


def _gelu(x):
    return 0.5 * x * (1.0 + jnp.tanh(0.7978845608028654 * (x + 0.044715 * (x * x * x))))


def _sigmoid(x):
    return 1.0 / (1.0 + jnp.exp(-x))


def _silu_mul(g, u):
    return g * _sigmoid(g) * u


def _logsig(x):
    return jnp.minimum(x, 0.0) - jnp.log(1.0 + jnp.exp(-jnp.abs(x)))


def _colsum(x):
    return jnp.sum(x, axis=0, keepdims=True)


def _acc(ref, val, first):
    @pl.when(first)
    def _():
        ref[...] = val

    @pl.when(jnp.logical_not(first))
    def _():
        ref[...] += val


def _inproj_fwd(x2d, g_pre, w_in_p, tm):
    T, D = x2d.shape
    CH = 768
    nchunk = P_COLS // CH
    ns, _, dsh = w_in_p.shape

    def body(x_ref, g_ref, w_ref, h_ref, proj_ref, fl_ref):
        h = _rms(x_ref[...], g_ref[...]).astype(BF16)
        h_ref[...] = h
        for n in range(nchunk):
            rows = slice(n * CH, (n + 1) * CH)
            r = _dot_nt(h[:, 0:dsh], w_ref[0, rows, :])
            for s in range(1, ns):
                r = r + _dot_nt(h[:, s * dsh:(s + 1) * dsh], w_ref[s, rows, :])
            if n < nchunk - 1:
                proj_ref[:, rows] = r.astype(BF16)
            else:
                fg = 1920 - n * CH
                proj_ref[:, n * CH:1920] = r[:, :fg].astype(BF16)
                fl_ref[...] = r[:, fg:fg + LANES]
                proj_ref[:, 1920:P_MAIN] = r[:, fg + LANES:].astype(BF16)

    return pl.pallas_call(
        body, name="inproj_fwd", grid=(T // tm,),
        in_specs=[BS((tm, D), lambda i: (i, 0)), BS((1, D), lambda i: (0, 0)),
                  BS((ns, P_COLS, dsh), lambda i: (0, 0, 0))],
        out_specs=[BS((tm, D), lambda i: (i, 0)), BS((tm, P_MAIN), lambda i: (i, 0)), BS((tm, LANES), lambda i: (i, 0))],
        out_shape=[jax.ShapeDtypeStruct((T, D), BF16), jax.ShapeDtypeStruct((T, P_MAIN), BF16),
                   jax.ShapeDtypeStruct((T, LANES), F32)],
        compiler_params=_cp(("arbitrary",)),
    )(x2d, g_pre, w_in_p)


def _gate_fwd(flog3, bf_row):
    Bl, S, _ = flog3.shape
    nb = S // LANES

    def body(f_ref, b_ref, bq_ref, bk_ref, fr_ref):
        row = _iota((LANES, LANES), 0)
        lane = _iota((LANES, LANES), 1)
        one = jnp.ones((LANES, LANES), BF16)
        zero = jnp.zeros((LANES, LANES), BF16)

        carry = jnp.zeros((1, LANES), F32)
        for j in range(nb):
            r0 = j * LANES
            fl = f_ref[0, pl.ds(r0, LANES), :] + b_ref[...]
            fr_ref[0, j] = fl.T[0:8, :]
            c = _logsig(fl)
            for k in (1, 2, 4, 8, 16, 32, 64):
                c = c + jnp.where(row >= k, pltpu.roll(c, k, 0), 0.0)
            total = _colsum(jnp.where(row == LANES - 1, c, 0.0))
            c = c + carry
            carry = carry + total
            for h in range(N_FOX_HEADS):
                col = jnp.sum(jnp.where(lane == h, c, 0.0), axis=1, keepdims=True)
                hi = col.astype(BF16)
                rest = col - hi.astype(F32)
                mid = rest.astype(BF16)
                lo = (rest - mid.astype(F32)).astype(BF16)
                base = _bias_lane(h)
                bq = jnp.where(lane == base, hi, jnp.where(lane == base + 1, mid, jnp.where(lane == base + 2, lo, zero)))
                bq = jnp.where((lane >= base + 3) & (lane < base + 6), one, bq)
                bk = jnp.where(lane == base + 3, -hi, jnp.where(lane == base + 4, -mid, jnp.where(lane == base + 5, -lo, zero)))
                bk = jnp.where((lane >= base) & (lane < base + 3), one, bk)
                bq_ref[0, h, pl.ds(r0, LANES), :] = bq
                bk_ref[0, h, pl.ds(r0, LANES), :] = bk

    slab = BS((1, N_FOX_HEADS, S, LANES), lambda b: (b, 0, 0, 0))
    return pl.pallas_call(
        body, name="gate_fwd", grid=(Bl,),
        in_specs=[BS((1, S, LANES), lambda b: (b, 0, 0)), BS((1, LANES), lambda b: (0, 0))],
        out_specs=[slab, slab, BS((1, nb, 8, LANES), lambda b: (b, 0, 0, 0))],
        out_shape=[jax.ShapeDtypeStruct((Bl, N_FOX_HEADS, S, LANES), BF16),
                   jax.ShapeDtypeStruct((Bl, N_FOX_HEADS, S, LANES), BF16),
                   jax.ShapeDtypeStruct((Bl, nb, 8, LANES), F32)],
        compiler_params=_cp(("arbitrary",)),
    )(flog3, bf_row)


def _bias_lane(h):
    return HEAD if h % 2 == 0 else 0


def _sgu_pre(zu, zv, g_sgu):
    return _gelu(zu), _rms(_gelu(zv), g_sgu)


def _sgu_fwd(proj, g_sgu, ws_tril, bs_full, tm):
    T = proj.shape[0]
    nch = tm // CHUNK

    def body(zu_ref, zv_ref, g_ref, ws_ref, b_ref, ya_ref):
        lane = _iota((CHUNK, LANES), 1)
        u, vn = _sgu_pre(zu_ref[...].astype(F32), zv_ref[...].astype(F32), g_ref[...])
        vn = vn.astype(BF16)
        for c in range(nch):
            rs = slice(c * CHUNK, (c + 1) * CHUNK)
            for j in range(3):
                cs = slice(j * LANES, (j + 1) * LANES)
                vp = vn[rs, cs]
                z = jnp.where(lane < HEAD, _dot(ws_ref[2 * j], vp), _dot(ws_ref[2 * j + 1], vp)) + b_ref[:, cs]
                ya_ref[rs, cs] = (u[rs, cs] * z).astype(BF16)

    return pl.pallas_call(
        body, name="sgu_fwd", grid=(T // tm,),
        in_specs=[BS((tm, A_W), lambda i: (i, 0)), BS((tm, A_W), lambda i: (i, 1)), BS((1, A_W), lambda i: (0, 0)),
                  BS((6, CHUNK, CHUNK), lambda i: (0, 0, 0)), BS((CHUNK, A_W), lambda i: (0, 0))],
        out_specs=BS((tm, A_W), lambda i: (i, 0)),
        out_shape=jax.ShapeDtypeStruct((T, A_W), BF16),
        compiler_params=_cp(("arbitrary",)),
    )(proj, proj, g_sgu, ws_tril, bs_full)


def _fox_fwd(proj, bq, bk, Bl, S):
    T = Bl * S
    nq = S // Q_BLK
    qc, kc, vc = 768 // LANES, 1152 // LANES, 1536 // LANES

    def body(q_ref, k_ref, v_ref, bq_ref, bk_ref, o_ref, lse_ref, ka_ref, va_ref):
        lane_s = _iota((S, LANES), 1)
        lane = _iota((Q_BLK, LANES), 1)
        tri = _iota((Q_BLK, Q_BLK), 1) <= _iota((Q_BLK, Q_BLK), 0)
        k = k_ref[...]
        v = v_ref[...]
        for hh in range(2):
            data = (lane_s < HEAD) if hh == 0 else (lane_s >= HEAD)
            ka_ref[hh] = jnp.where(data, k, bk_ref[0, hh])
            va_ref[hh] = jnp.where(lane_s == _bias_lane(hh), jnp.ones_like(v), v)
        for i in range(nq):
            r0 = i * Q_BLK
            q = q_ref[r0:r0 + Q_BLK, :]
            o_out = jnp.zeros((Q_BLK, LANES), F32)
            lse_out = jnp.zeros((Q_BLK, LANES), F32)
            for hh in range(2):
                hmask = (lane < HEAD) if hh == 0 else (lane >= HEAD)
                qa = jnp.where(hmask, q * 0.125, bq_ref[0, hh, r0:r0 + Q_BLK, :])
                sd = jnp.where(tri, _dot_nt(qa, ka_ref[hh, r0:r0 + Q_BLK, :]), NEG)
                m = jnp.max(sd, axis=1, keepdims=True)
                if i:
                    sf = _dot_nt(qa, ka_ref[hh, 0:r0, :])
                    m = jnp.maximum(m, jnp.max(sf, axis=1, keepdims=True))
                acc = _dot(jnp.exp(sd - m), va_ref[hh, r0:r0 + Q_BLK, :])
                if i:
                    acc = acc + _dot(jnp.exp(sf - m), va_ref[hh, 0:r0, :])
                l = jnp.sum(jnp.where(lane == _bias_lane(hh), acc, 0.0), axis=1, keepdims=True)
                o_out = jnp.where(hmask, acc / l, o_out)
                lse_out = jnp.where(hmask, m + jnp.log(l), lse_out)
            o_ref[r0:r0 + Q_BLK, :] = o_out.astype(BF16)
            lse_ref[0, r0:r0 + Q_BLK, :] = lse_out

    seq = lambda c0: BS((S, LANES), lambda b, p: (b, c0 + p))
    pair = BS((1, 2, S, LANES), lambda b, p: (b, p, 0, 0))
    return pl.pallas_call(
        body, name="fox_fwd", grid=(Bl, 3),
        in_specs=[seq(qc), seq(kc), seq(vc), pair, pair],
        out_specs=[seq(0), BS((1, S, LANES), lambda b, p: (p, b, 0))],
        out_shape=[jax.ShapeDtypeStruct((T, B_W), BF16), jax.ShapeDtypeStruct((3, T, LANES), F32)],
        scratch_shapes=[pltpu.VMEM((2, S, LANES), BF16), pltpu.VMEM((2, S, LANES), BF16)],
        compiler_params=_cp(("arbitrary", "arbitrary")),
    )(proj, proj, proj, bq, bk)


def _memkv_fwd(mem, g_mem, w_kv):
    Bl, Mt, D = mem.shape

    def body(m_ref, g_ref, w_ref, mn_ref, kv_ref):
        mn = _rms(m_ref[0], g_ref[...]).astype(BF16)
        mn_ref[0] = mn
        kv_ref[0] = jnp.dot(mn, w_ref[...], preferred_element_type=F32).astype(BF16)

    return pl.pallas_call(
        body, name="memkv_fwd", grid=(Bl,),
        in_specs=[BS((1, Mt, D), lambda b: (b, 0, 0)), BS((1, D), lambda b: (0, 0)), BS((D, 2 * M_W), lambda b: (0, 0))],
        out_specs=[BS((1, Mt, D), lambda b: (b, 0, 0)), BS((1, Mt, 2 * M_W), lambda b: (b, 0, 0))],
        out_shape=[jax.ShapeDtypeStruct((Bl, Mt, D), BF16), jax.ShapeDtypeStruct((Bl, Mt, 2 * M_W), BF16)],
        compiler_params=_cp(("arbitrary",)),
    )(mem, g_mem, w_kv)


def _memattn_fwd(proj, kv, Bl, S, tq):
    T = Bl * S
    nq = S // tq
    Mt = kv.shape[1]
    qc = 1920 // LANES

    def body(q_ref, km_ref, vm_ref, o_ref):
        lane = _iota((tq, LANES), 1)
        q = q_ref[...]
        out = jnp.zeros((tq, LANES), F32)
        for hh in range(2):
            hmask = (lane < HEAD) if hh == 0 else (lane >= HEAD)
            qs = jnp.where(hmask, q, jnp.zeros_like(q)) * 0.125
            s = _dot_nt(qs, km_ref[0])
            pe = jnp.exp(s - jnp.max(s, axis=1, keepdims=True))
            pn = pe / jnp.sum(pe, axis=1, keepdims=True)
            out = jnp.where(hmask, _dot(pn, vm_ref[0]), out)
        o_ref[...] = out.astype(BF16)

    return pl.pallas_call(
        body, name="memattn_fwd", grid=(Bl, 2, nq),
        in_specs=[BS((tq, LANES), lambda b, p, i: (b * nq + i, qc + p)),
                  BS((1, Mt, LANES), lambda b, p, i: (b, 0, p)),
                  BS((1, Mt, LANES), lambda b, p, i: (b, 0, 2 + p))],
        out_specs=BS((tq, LANES), lambda b, p, i: (b * nq + i, p)),
        out_shape=jax.ShapeDtypeStruct((T, M_W), BF16),
        compiler_params=_cp(("arbitrary", "arbitrary", "arbitrary")),
    )(proj, kv, kv)


def _mix_norms(ya, yb, ym, ga, gb, gm):
    return _rms(ya, ga), _rms(yb, gb), _rms(ym, gm)


def _outproj_fwd(ya, yb, ym, x2d, ga, gb, gm, g_post, g_pre2, w_out, tm):
    T, D = x2d.shape

    def body(ya_ref, yb_ref, ym_ref, x_ref, ga_ref, gb_ref, gm_ref, gp_ref, g2_ref, w_ref,
             y_ref, o_ref, x1_ref, h2_ref):
        na, nb_, nm = _mix_norms(ya_ref[...].astype(F32), yb_ref[...].astype(F32), ym_ref[...].astype(F32),
                                 ga_ref[...], gb_ref[...], gm_ref[...])
        y_ref[:, 0:A_W] = na.astype(BF16)
        y_ref[:, A_W:A_W + B_W] = nb_.astype(BF16)
        y_ref[:, A_W + B_W:] = nm.astype(BF16)
        o = jnp.dot(y_ref[...], w_ref[...], preferred_element_type=F32)
        o_ref[...] = o
        x1 = x_ref[...] + _rms(o, gp_ref[...])
        x1_ref[...] = x1
        h2_ref[...] = _rms(x1, g2_ref[...]).astype(BF16)

    row = lambda w: BS((tm, w), lambda i: (i, 0))
    vec = lambda w: BS((1, w), lambda i: (0, 0))
    return pl.pallas_call(
        body, name="outproj_fwd", grid=(T // tm,),
        in_specs=[row(A_W), row(B_W), row(M_W), row(D), vec(A_W), vec(B_W), vec(M_W), vec(D), vec(D),
                  BS((A_W + B_W + M_W, D), lambda i: (0, 0))],
        out_specs=[row(A_W + B_W + M_W), row(D), row(D), row(D)],
        out_shape=[jax.ShapeDtypeStruct((T, A_W + B_W + M_W), BF16), jax.ShapeDtypeStruct((T, D), F32),
                   jax.ShapeDtypeStruct((T, D), F32), jax.ShapeDtypeStruct((T, D), BF16)],
        compiler_params=_cp(("arbitrary",)),
    )(ya, yb, ym, x2d, ga, gb, gm, g_post, g_pre2, w_out)


def _ffn_fwd(h2, x1, target, wg, wu, wd, g_post, tm):
    T, D = x1.shape
    ns, F, _ = wg.shape

    def body(h_ref, x1_ref, t_ref, wg_ref, wu_ref, wd_ref, gp_ref,
             gs_ref, us_ref, dff_ref, dx2_ref, dgp_ref, loss_ref, acc_ref):
        j = pl.program_id(0)
        i = pl.program_id(1)
        rows = pl.ds(pl.multiple_of(i * tm, tm), tm)
        h = h_ref[...]
        g = _dot_nt(h, wg_ref[0])
        u = _dot_nt(h, wu_ref[0])
        gs_ref[0] = g.astype(BF16)
        us_ref[0] = u.astype(BF16)
        part = _dot(_silu_mul(g, u), wd_ref[0])

        @pl.when(j == 0)
        def _():
            acc_ref[rows, :] = part

        @pl.when(j != 0)
        def _():
            acc_ref[rows, :] += part

        @pl.when(j == ns - 1)
        def _():
            ff = acc_ref[rows, :]
            diff = x1_ref[...] + _rms(ff, gp_ref[...]) - t_ref[...]
            dx2 = diff * (1.0 / D)
            dff, dgp = _rms_bwd(ff, gp_ref[...], dx2)
            dx2_ref[...] = dx2
            dff_ref[...] = dff.astype(BF16)
            lpart = jnp.sum(_colsum(diff * diff), axis=1, keepdims=True) * (0.5 / D)
            _acc(dgp_ref, dgp, i == 0)
            _acc(loss_ref, jnp.broadcast_to(lpart, (1, LANES)), i == 0)

    last = lambda j, i: (jnp.where(j == ns - 1, i, 0), 0)
    wsh = BS((1, F, D), lambda j, i: (j, 0, 0))
    sh = BS((1, tm, F), lambda j, i: (j, i, 0))
    return pl.pallas_call(
        body, name="ffn_fwd", grid=(ns, T // tm),
        in_specs=[BS((tm, D), lambda j, i: (i, 0)), BS((tm, D), last), BS((tm, D), last), wsh, wsh, wsh,
                  BS((1, D), lambda j, i: (0, 0))],
        out_specs=[sh, sh, BS((tm, D), last), BS((tm, D), last),
                   BS((1, D), lambda j, i: (0, 0)), BS((1, LANES), lambda j, i: (0, 0))],
        out_shape=[jax.ShapeDtypeStruct((ns, T, F), BF16), jax.ShapeDtypeStruct((ns, T, F), BF16),
                   jax.ShapeDtypeStruct((T, D), BF16), jax.ShapeDtypeStruct((T, D), F32),
                   jax.ShapeDtypeStruct((1, D), F32), jax.ShapeDtypeStruct((1, LANES), F32)],
        scratch_shapes=[pltpu.VMEM((T, D), F32)],
        compiler_params=_cp(("arbitrary", "arbitrary")),
    )(h2, x1, target, wg, wu, wd, g_post)


def _ffn_bwd(dff, h2, gs, us, wg, wu, wd, tm):
    T, D = h2.shape
    ns, F, _ = wg.shape

    def body(dff_ref, h_ref, gs_ref, us_ref, wg_ref, wu_ref, wd_ref, dh_ref, dwg_out, dwu_out, dwd_out,
             dwg_ref, dwu_ref, dwd_ref):
        first = pl.program_id(1) == 0
        dff = dff_ref[...]
        h = h_ref[...]
        parts = []
        for r in range(ROW_SPLIT):
            rows = slice(r * (tm // ROW_SPLIT), (r + 1) * (tm // ROW_SPLIT))
            dact = _dot_nt(dff[rows], wd_ref[0])
            g = gs_ref[0, rows, :].astype(F32)
            u = us_ref[0, rows, :].astype(F32)
            sig = _sigmoid(g)
            gsig = g * sig
            dg = (dact * u * (sig + gsig * (1.0 - sig))).astype(BF16)
            du = (dact * gsig).astype(BF16)
            dh_ref[0, rows, :] = (_dot(dg, wg_ref[0]) + _dot(du, wu_ref[0])).astype(BF16)
            parts.append(((gsig * u).astype(BF16), dg, du))
        a, dg, du = [jnp.concatenate(p, axis=0) for p in zip(*parts)]
        _acc(dwd_ref, _dot_tn(a, dff), first)
        _acc(dwg_ref, _dot_tn(dg, h), first)
        _acc(dwu_ref, _dot_tn(du, h), first)

        @pl.when(pl.program_id(1) == pl.num_programs(1) - 1)
        def _():
            dwg_out[0] = dwg_ref[...].astype(BF16)
            dwu_out[0] = dwu_ref[...].astype(BF16)
            dwd_out[0] = dwd_ref[...].astype(BF16)

    row = BS((tm, D), lambda j, i: (i, 0))
    sh = BS((1, tm, F), lambda j, i: (j, i, 0))
    wsh = BS((1, F, D), lambda j, i: (j, 0, 0))
    return pl.pallas_call(
        body, name="ffn_bwd", grid=(ns, T // tm),
        in_specs=[row, row, sh, sh, wsh, wsh, wsh],
        out_specs=[BS((1, tm, D), lambda j, i: (j, i, 0)), wsh, wsh, wsh],
        out_shape=[jax.ShapeDtypeStruct((ns, T, D), BF16)] + [jax.ShapeDtypeStruct((ns, F, D), BF16)] * 3,
        scratch_shapes=[pltpu.VMEM((F, D), F32)] * 3,
        compiler_params=_cp(("arbitrary", "arbitrary")),
    )(dff, h2, gs, us, wg, wu, wd)


def _mm_tn(a, b, name, tk):
    T, M = a.shape
    N = b.shape[1]
    tk = min(tk, T)

    def body(a_ref, b_ref, o_ref):
        _acc(o_ref, _dot_tn(a_ref[...], b_ref[...]), pl.program_id(0) == 0)

    return pl.pallas_call(
        body, name=name, grid=(T // tk,),
        in_specs=[BS((tk, M), lambda t: (t, 0)), BS((tk, N), lambda t: (t, 0))],
        out_specs=BS((M, N), lambda t: (0, 0)),
        out_shape=jax.ShapeDtypeStruct((M, N), F32),
        compiler_params=_cp(("arbitrary",)),
    )(a, b)


DPROJ_PIECES = ((0, A_W), (A_W, A_W), (768, B_W), (1152, B_W), (1536, B_W), (1920, LANES), (2048, M_W))


def _put_dproj(dp_ref, piece_refs):
    for (c0, w), ref in zip(DPROJ_PIECES, piece_refs):
        dp_ref[:, c0:c0 + w] = ref[...].astype(BF16)


def _dw_in(pieces, h, ns, tk):
    T, D = h.shape
    M = P_COLS
    dsh = D // ns
    tk = min(tk, T)

    def body(*refs):
        piece_refs, h_ref, o_ref, acc_ref, dp_ref = refs[:7], refs[7], refs[8], refs[9], refs[10]
        t = pl.program_id(0)
        _put_dproj(dp_ref, piece_refs)
        _acc(acc_ref, _dot_tn(h_ref[...], dp_ref[...]), t == 0)

        @pl.when(t == pl.num_programs(0) - 1)
        def _():
            for s in range(ns):
                o_ref[s] = acc_ref[s * dsh:(s + 1) * dsh, :].T.astype(BF16)

    return pl.pallas_call(
        body, name="dw_in", grid=(T // tk,),
        in_specs=[BS((tk, w), lambda t: (t, 0)) for _, w in DPROJ_PIECES] + [BS((tk, D), lambda t: (t, 0))],
        out_specs=BS((ns, M, dsh), lambda t: (0, 0, 0)),
        out_shape=jax.ShapeDtypeStruct((ns, M, dsh), BF16),
        scratch_shapes=[pltpu.VMEM((D, M), F32), pltpu.VMEM((tk, M), BF16)],
        compiler_params=_cp(("arbitrary",)),
    )(*pieces, h)


def _outproj_bwd(dh2, x1, dx2, o, ya, yb, ym, ga, gb, gm, g_post, g_pre2, w_out, tm):
    T, D = x1.shape
    ns = dh2.shape[0]

    def body(dh_ref, x1_ref, dx2_ref, o_ref, ya_ref, yb_ref, ym_ref, ga_ref, gb_ref, gm_ref, gp_ref, g2_ref, w_ref,
             dx1_ref, do_ref, dya_ref, dyb_ref, dym_ref, dga_ref, dgb_ref, dgm_ref, dgp_ref, dg2_ref):
        first = pl.program_id(0) == 0
        dh = dh_ref[0].astype(F32)
        for j in range(1, ns):
            dh = dh + dh_ref[j].astype(F32)
        dxa, dg2 = _rms_bwd(x1_ref[...], g2_ref[...], dh)
        dx1 = dx2_ref[...] + dxa
        dx1_ref[...] = dx1
        _acc(dg2_ref, dg2, first)
        do, dgp = _rms_bwd(o_ref[...], gp_ref[...], dx1)
        do = do.astype(BF16)
        do_ref[...] = do
        dy = _dot_nt(do, w_ref[...])
        dya, dga = _rms_bwd(ya_ref[...].astype(F32), ga_ref[...], dy[:, 0:A_W])
        dyb, dgb = _rms_bwd(yb_ref[...].astype(F32), gb_ref[...], dy[:, A_W:A_W + B_W])
        dym, dgm = _rms_bwd(ym_ref[...].astype(F32), gm_ref[...], dy[:, A_W + B_W:])
        dya_ref[...] = dya.astype(BF16)
        dyb_ref[...] = dyb.astype(BF16)
        dym_ref[...] = dym.astype(BF16)
        _acc(dga_ref, dga, first)
        _acc(dgb_ref, dgb, first)
        _acc(dgm_ref, dgm, first)
        _acc(dgp_ref, dgp, first)

    row = lambda w: BS((tm, w), lambda i: (i, 0))
    vec = lambda w: BS((1, w), lambda i: (0, 0))
    sds = jax.ShapeDtypeStruct
    return pl.pallas_call(
        body, name="outproj_bwd", grid=(T // tm,),
        in_specs=[BS((ns, tm, D), lambda i: (0, i, 0)), row(D), row(D), row(D), row(A_W), row(B_W), row(M_W),
                  vec(A_W), vec(B_W), vec(M_W), vec(D), vec(D), BS((A_W + B_W + M_W, D), lambda i: (0, 0))],
        out_specs=[row(D), row(D), row(A_W), row(B_W), row(M_W), vec(A_W), vec(B_W), vec(M_W), vec(D), vec(D)],
        out_shape=[sds((T, D), F32), sds((T, D), BF16), sds((T, A_W), BF16), sds((T, B_W), BF16), sds((T, M_W), BF16),
                   sds((1, A_W), F32), sds((1, B_W), F32), sds((1, M_W), F32), sds((1, D), F32), sds((1, D), F32)],
        compiler_params=_cp(("arbitrary",)),
    )(dh2, x1, dx2, o, ya, yb, ym, ga, gb, gm, g_post, g_pre2, w_out)


def _sgu_bwd(proj, dya, g_sgu, ws_tril, bs_full, tm):
    T = proj.shape[0]
    nch = tm // CHUNK

    def body(zu_ref, zv_ref, dy_ref, g_ref, ws_ref, b_ref, dzu_ref, dzv_ref, dws_ref, dbs_ref, dg_ref,
             du_ref, dvn_ref, dbf_ref):
        step = pl.program_id(0)
        first = step == 0
        lane = _iota((CHUNK, LANES), 1)
        tril = _iota((CHUNK, CHUNK), 0) >= _iota((CHUNK, CHUNK), 1)
        (u, vn), vjp = jax.vjp(_sgu_pre, zu_ref[...].astype(F32), zv_ref[...].astype(F32), g_ref[...])
        vnb = vn.astype(BF16)
        dy = dy_ref[...].astype(F32)

        @pl.when(first)
        def _():
            dws_ref[...] = jnp.zeros_like(dws_ref)
            dbf_ref[...] = jnp.zeros_like(dbf_ref)

        for c in range(nch):
            rs = slice(c * CHUNK, (c + 1) * CHUNK)
            for j in range(3):
                cs = slice(j * LANES, (j + 1) * LANES)
                vp = vnb[rs, cs]
                z = jnp.where(lane < HEAD, _dot(ws_ref[2 * j], vp), _dot(ws_ref[2 * j + 1], vp)) + b_ref[:, cs]
                du_ref[rs, cs] = dy[rs, cs] * z
                dz = dy[rs, cs] * u[rs, cs]
                dbf_ref[:, cs] += dz
                dzb = dz.astype(BF16)
                dz0 = jnp.where(lane < HEAD, dzb, jnp.zeros_like(dzb))
                dz1 = jnp.where(lane >= HEAD, dzb, jnp.zeros_like(dzb))
                dvn_ref[rs, cs] = jnp.where(lane < HEAD, _dot_tn(ws_ref[2 * j], dzb), _dot_tn(ws_ref[2 * j + 1], dzb))
                dws_ref[2 * j] += jnp.where(tril, _dot_nt(dz0, vp), 0.0)
                dws_ref[2 * j + 1] += jnp.where(tril, _dot_nt(dz1, vp), 0.0)
        dzu, dzv, dg = vjp((du_ref[...], dvn_ref[...]))
        dzu_ref[...] = dzu.astype(BF16)
        dzv_ref[...] = dzv.astype(BF16)
        _acc(dg_ref, dg, first)

        @pl.when(step == pl.num_programs(0) - 1)
        def _():
            out = jnp.zeros((CHUNK, LANES), F32)
            for j in range(3):
                slab = dbf_ref[:, j * LANES:(j + 1) * LANES]
                lo = jnp.sum(jnp.where(lane < HEAD, slab, 0.0), axis=1, keepdims=True)
                hi = jnp.sum(jnp.where(lane >= HEAD, slab, 0.0), axis=1, keepdims=True)
                out = out + jnp.where(lane == 2 * j, lo, 0.0) + jnp.where(lane == 2 * j + 1, hi, 0.0)
            dbs_ref[...] = out

    return pl.pallas_call(
        body, name="sgu_bwd", grid=(T // tm,),
        in_specs=[BS((tm, A_W), lambda i: (i, 0)), BS((tm, A_W), lambda i: (i, 1)), BS((tm, A_W), lambda i: (i, 0)),
                  BS((1, A_W), lambda i: (0, 0)), BS((6, CHUNK, CHUNK), lambda i: (0, 0, 0)),
                  BS((CHUNK, A_W), lambda i: (0, 0))],
        out_specs=[BS((tm, A_W), lambda i: (i, 0)), BS((tm, A_W), lambda i: (i, 0)),
                   BS((6, CHUNK, CHUNK), lambda i: (0, 0, 0)), BS((CHUNK, LANES), lambda i: (0, 0)),
                   BS((1, A_W), lambda i: (0, 0))],
        out_shape=[jax.ShapeDtypeStruct((T, A_W), BF16), jax.ShapeDtypeStruct((T, A_W), BF16),
                   jax.ShapeDtypeStruct((6, CHUNK, CHUNK), F32), jax.ShapeDtypeStruct((CHUNK, LANES), F32),
                   jax.ShapeDtypeStruct((1, A_W), F32)],
        scratch_shapes=[pltpu.VMEM((tm, A_W), F32), pltpu.VMEM((tm, A_W), F32), pltpu.VMEM((CHUNK, A_W), F32)],
        compiler_params=_cp(("arbitrary",)),
    )(proj, proj, dya, g_sgu, ws_tril, bs_full)


def _memattn_bwd(proj, kv, dym, Bl, S, tq):
    T = Bl * S
    nq = S // tq
    Mt = kv.shape[1]
    qc = 1920 // LANES

    def body(q_ref, km_ref, vm_ref, do_ref, dq_ref, dkm_ref, dvm_ref):
        first = pl.program_id(2) == 0
        lane = _iota((tq, LANES), 1)
        q = q_ref[...]
        do = do_ref[...]
        dq_out = jnp.zeros((tq, LANES), F32)
        dkm = jnp.zeros((Mt, LANES), F32)
        dvm = jnp.zeros((Mt, LANES), F32)
        for hh in range(2):
            hmask = (lane < HEAD) if hh == 0 else (lane >= HEAD)
            qs = jnp.where(hmask, q, jnp.zeros_like(q)) * 0.125
            dom = jnp.where(hmask, do, 0.0).astype(BF16)
            s = _dot_nt(qs, km_ref[0])
            pe = jnp.exp(s - jnp.max(s, axis=1, keepdims=True))
            pn = pe / jnp.sum(pe, axis=1, keepdims=True)
            dp = _dot_nt(dom, vm_ref[0])
            ds = (pn * (dp - jnp.sum(pn * dp, axis=1, keepdims=True))).astype(BF16)
            dq_out = jnp.where(hmask, _dot(ds, km_ref[0]) * 0.125, dq_out)
            dkm = dkm + _dot_tn(ds, qs)
            dvm = dvm + _dot_tn(pn, dom)
        dq_ref[...] = dq_out.astype(BF16)
        _acc(dkm_ref, dkm[None], first)
        _acc(dvm_ref, dvm[None], first)

    return pl.pallas_call(
        body, name="memattn_bwd", grid=(Bl, 2, nq),
        in_specs=[BS((tq, LANES), lambda b, p, i: (b * nq + i, qc + p)),
                  BS((1, Mt, LANES), lambda b, p, i: (b, 0, p)),
                  BS((1, Mt, LANES), lambda b, p, i: (b, 0, 2 + p)),
                  BS((tq, LANES), lambda b, p, i: (b * nq + i, p))],
        out_specs=[BS((tq, LANES), lambda b, p, i: (b * nq + i, p)),
                   BS((1, Mt, LANES), lambda b, p, i: (b, 0, p)),
                   BS((1, Mt, LANES), lambda b, p, i: (b, 0, p))],
        out_shape=[jax.ShapeDtypeStruct((T, M_W), BF16), jax.ShapeDtypeStruct((Bl, Mt, M_W), F32),
                   jax.ShapeDtypeStruct((Bl, Mt, M_W), F32)],
        compiler_params=_cp(("arbitrary", "arbitrary", "arbitrary")),
    )(proj, kv, kv, dym)


def _memkv_bwd(dkm, dvm, memn, mem, g_mem, w_kv):
    Bl, Mt, D = mem.shape

    def body(dk_ref, dv_ref, mn_ref, m_ref, g_ref, w_ref, dw_ref, dg_ref):
        first = pl.program_id(0) == 0
        dk = dk_ref[0].astype(BF16)
        dv = dv_ref[0].astype(BF16)
        mn = mn_ref[0]
        dmn = _dot_nt(dk, w_ref[:, 0:M_W]) + _dot_nt(dv, w_ref[:, M_W:])
        _, dg = _rms_bwd(m_ref[0], g_ref[...], dmn)
        _acc(dg_ref, dg, first)

        @pl.when(first)
        def _():
            dw_ref[...] = jnp.zeros_like(dw_ref)

        dw_ref[:, 0:M_W] += _dot_tn(mn, dk)
        dw_ref[:, M_W:] += _dot_tn(mn, dv)

    return pl.pallas_call(
        body, name="memkv_bwd", grid=(Bl,),
        in_specs=[BS((1, Mt, M_W), lambda b: (b, 0, 0)), BS((1, Mt, M_W), lambda b: (b, 0, 0)),
                  BS((1, Mt, D), lambda b: (b, 0, 0)), BS((1, Mt, D), lambda b: (b, 0, 0)),
                  BS((1, D), lambda b: (0, 0)), BS((D, 2 * M_W), lambda b: (0, 0))],
        out_specs=[BS((D, 2 * M_W), lambda b: (0, 0)), BS((1, D), lambda b: (0, 0))],
        out_shape=[jax.ShapeDtypeStruct((D, 2 * M_W), F32), jax.ShapeDtypeStruct((1, D), F32)],
        compiler_params=_cp(("arbitrary",)),
    )(dkm, dvm, memn, mem, g_mem, w_kv)


def _fox_bwd(proj, dyb, lse, bq, bk, Bl, S):
    T = Bl * S
    nq = S // Q_BLK
    nb = S // LANES
    qc, kc, vc = 768 // LANES, 1152 // LANES, 1536 // LANES

    def body(q_ref, k_ref, v_ref, do_ref, lse_ref, bq_ref, bk_ref,
             dq_ref, dk_ref, dv_ref, dcr_ref, ka_ref, dka_ref, dva_ref):
        p = pl.program_id(1)
        lane_s = _iota((S, LANES), 1)
        lane = _iota((Q_BLK, LANES), 1)
        sub = _iota((8, LANES), 0)
        tri = _iota((Q_BLK, Q_BLK), 1) <= _iota((Q_BLK, Q_BLK), 0)
        k = k_ref[...]
        for hh in range(2):
            data = (lane_s < HEAD) if hh == 0 else (lane_s >= HEAD)
            ka_ref[hh] = jnp.where(data, k, bk_ref[0, hh])
        dka_ref[...] = jnp.zeros_like(dka_ref)
        dva_ref[...] = jnp.zeros_like(dva_ref)

        @pl.when(p == 0)
        def _():
            dcr_ref[...] = jnp.zeros_like(dcr_ref)

        def add_colsums(ds, first_blk, h):
            cs = _colsum(ds)
            for jb in range(ds.shape[1] // LANES):
                dcr_ref[0, first_blk + jb] += jnp.where(sub == h, cs[:, jb * LANES:(jb + 1) * LANES], 0.0)

        for i in range(nq):
            r0 = i * Q_BLK
            r1 = r0 + Q_BLK
            q = q_ref[r0:r1, :]
            do = do_ref[r0:r1, :]
            lse_b = lse_ref[0, r0:r1, :]
            dq_out = jnp.zeros((Q_BLK, LANES), F32)
            for hh in range(2):
                hmask = (lane < HEAD) if hh == 0 else (lane >= HEAD)
                h = 2 * p + hh
                qs = jnp.where(hmask, q * 0.125, jnp.zeros_like(q))
                qa = jnp.where(hmask, q * 0.125, bq_ref[0, hh, r0:r1, :])
                dob = jnp.where(hmask, do, 0.0).astype(BF16)
                lse_h = jnp.sum(jnp.where(lane == hh * HEAD, lse_b, 0.0), axis=1, keepdims=True)
                pd = jnp.where(tri, jnp.exp(_dot_nt(qa, ka_ref[hh, r0:r1, :]) - lse_h), 0.0)
                dpd = _dot_nt(dob, v_ref[r0:r1, :])
                delta = jnp.sum(pd * dpd, axis=1, keepdims=True)
                psum = jnp.sum(pd, axis=1, keepdims=True)
                if i:
                    pf = jnp.exp(_dot_nt(qa, ka_ref[hh, 0:r0, :]) - lse_h)
                    dpf = _dot_nt(dob, v_ref[0:r0, :])
                    delta = delta + jnp.sum(pf * dpf, axis=1, keepdims=True)
                    psum = psum + jnp.sum(pf, axis=1, keepdims=True)
                delta = delta / psum
                dsd = pd * (dpd - delta)
                add_colsums(dsd, r0 // LANES, h)
                dsd = dsd.astype(BF16)
                dq_h = _dot(dsd, k_ref[r0:r1, :])
                dka_ref[r0:r1, :] += _dot_tn(dsd, qs)
                dva_ref[r0:r1, :] += _dot_tn(pd, dob)
                if i:
                    dsf = pf * (dpf - delta)
                    add_colsums(dsf, 0, h)
                    dsf = dsf.astype(BF16)
                    dq_h = dq_h + _dot(dsf, k_ref[0:r0, :])
                    dka_ref[0:r0, :] += _dot_tn(dsf, qs)
                    dva_ref[0:r0, :] += _dot_tn(pf, dob)
                dq_out = jnp.where(hmask, dq_h * 0.125, dq_out)
            dq_ref[r0:r1, :] = dq_out.astype(BF16)
        dk_ref[...] = dka_ref[...].astype(BF16)
        dv_ref[...] = dva_ref[...].astype(BF16)

    seq = lambda c0: BS((S, LANES), lambda b, p: (b, c0 + p))
    pair = BS((1, 2, S, LANES), lambda b, p: (b, p, 0, 0))
    rowblk = BS((1, nb, 8, LANES), lambda b, p: (b, 0, 0, 0))
    return pl.pallas_call(
        body, name="fox_bwd", grid=(Bl, 3),
        in_specs=[seq(qc), seq(kc), seq(vc), seq(0), BS((1, S, LANES), lambda b, p: (p, b, 0)), pair, pair],
        out_specs=[seq(0), seq(0), seq(0), rowblk],
        out_shape=[jax.ShapeDtypeStruct((T, B_W), BF16)] * 3 + [jax.ShapeDtypeStruct((Bl, nb, 8, LANES), F32)],
        scratch_shapes=[pltpu.VMEM((2, S, LANES), BF16), pltpu.VMEM((S, LANES), F32), pltpu.VMEM((S, LANES), F32)],
        compiler_params=_cp(("arbitrary", "arbitrary")),
    )(proj, proj, proj, dyb, lse, bq, bk)


def _gate_bwd(dc_row, fl_row):
    Bl, nb, _, _ = dc_row.shape

    def body(dc_ref, fl_ref, o_ref):
        lane = _iota((8, LANES), 1)

        carry = jnp.zeros((8, 1), F32)
        for j in reversed(range(nb)):
            r = -dc_ref[0, j]
            for k in (1, 2, 4, 8, 16, 32, 64):
                r = r + jnp.where(lane < LANES - k, pltpu.roll(r, LANES - k, 1), 0.0)
            total = jnp.sum(jnp.where(lane == 0, r, 0.0), axis=1, keepdims=True)
            dfl = (r + carry) * _sigmoid(-fl_ref[0, j])
            carry = carry + total
            o_ref[0, j * LANES:(j + 1) * LANES, :] = jnp.concatenate(
                [dfl, jnp.zeros((LANES - 8, LANES), F32)], axis=0).T

    rowblk = BS((1, nb, 8, LANES), lambda b: (b, 0, 0, 0))
    return pl.pallas_call(
        body, name="gate_bwd", grid=(Bl,),
        in_specs=[rowblk, rowblk],
        out_specs=BS((1, nb * LANES, LANES), lambda b: (b, 0, 0)),
        out_shape=jax.ShapeDtypeStruct((Bl, nb * LANES, LANES), F32),
        compiler_params=_cp(("arbitrary",)),
    )(dc_row, fl_row)


def _inproj_bwd(pieces, x2d, dx1, g_pre, w_in_p, tm):
    T, D = x2d.shape
    ns, _, dsh = w_in_p.shape

    def body(*refs):
        piece_refs = refs[:7]
        x_ref, dx1_ref, g_ref, w_ref, gx_ref, dg_ref, dbf_ref, dp_ref = refs[7:]
        first = pl.program_id(0) == 0
        _put_dproj(dp_ref, piece_refs)
        dh = jnp.concatenate([_dot(dp_ref[...], w_ref[s]) for s in range(ns)], axis=1)
        dxa, dg = _rms_bwd(x_ref[...], g_ref[...], dh)
        gx_ref[...] = dx1_ref[...] + dxa
        _acc(dg_ref, dg, first)
        _acc(dbf_ref, _colsum(piece_refs[5][...]), first)

    row = lambda w: BS((tm, w), lambda i: (i, 0))
    return pl.pallas_call(
        body, name="inproj_bwd", grid=(T // tm,),
        in_specs=[row(w) for _, w in DPROJ_PIECES] + [row(D), row(D), BS((1, D), lambda i: (0, 0)),
                                                      BS((ns, P_COLS, dsh), lambda i: (0, 0, 0))],
        out_specs=[row(D), BS((1, D), lambda i: (0, 0)), BS((1, LANES), lambda i: (0, 0))],
        out_shape=[jax.ShapeDtypeStruct((T, D), F32), jax.ShapeDtypeStruct((1, D), F32),
                   jax.ShapeDtypeStruct((1, LANES), F32)],
        scratch_shapes=[pltpu.VMEM((tm, P_COLS), BF16)],
        compiler_params=_cp(("arbitrary",)),
    )(*pieces, x2d, dx1, g_pre, w_in_p)


def _local_step(x, mem, target, W, P, reduce=None):
    Bl, S, D = x.shape
    T = Bl * S
    tm = min(512, T)
    x2d = x.reshape(T, D)
    t2d = target.reshape(T, D)
    vec = lambda a: a.reshape(1, -1)
    bf_row = jnp.pad(P["b_f"].reshape(1, -1), ((0, 0), (0, LANES - N_FOX_HEADS)))
    tril = jnp.tril(jnp.ones((CHUNK, CHUNK), bool))
    ws_tril = jnp.where(tril[None], P["w_s"][0], 0.0).astype(BF16)
    bs_full = jnp.repeat(P["b_s"][0].T, HEAD, axis=1)
    g_pre, g_sgu = vec(P["g_pre_mix"]), vec(P["g_sgu"])
    ga, gb, gm = vec(P["g_out_a"]), vec(P["g_out_b"]), vec(P["g_out_m"])
    g_mem, g_post, g_pre2, g_post2 = vec(P["g_mem"]), vec(P["g_post_mix"]), vec(P["g_pre_ffn"]), vec(P["g_post_ffn"])

    h, proj, flog = _inproj_fwd(x2d, g_pre, W["w_in"], tm)
    bq, bk, fl_row = _gate_fwd(flog.reshape(Bl, S, LANES), bf_row)
    ya = _sgu_fwd(proj, g_sgu, ws_tril, bs_full, tm)
    yb, lse = _fox_fwd(proj, bq, bk, Bl, S)
    memn, kv = _memkv_fwd(mem, g_mem, W["w_mem_kv"])
    ym = _memattn_fwd(proj, kv, Bl, S, min(512, S))
    y, o, x1, h2 = _outproj_fwd(ya, yb, ym, x2d, ga, gb, gm, g_post, g_pre2, W["w_out"], tm)
    gs, us, dff, dx2, dg_post2, loss = _ffn_fwd(h2, x1, t2d, W["w_gate"], W["w_up"], W["w_down"], g_post2, tm)

    dh2, d_w_gate, d_w_up, d_w_down = _ffn_bwd(dff, h2, gs, us, W["w_gate"], W["w_up"], W["w_down"], min(1024, T))
    ffn = [d_w_gate, d_w_up, d_w_down]
    if reduce is not None:
        pending, _ = reduce.begin("ffn", ffn)
    dx1, do, dya, dyb, dym, dga, dgb, dgm, dg_post, dg_pre2 = _outproj_bwd(
        dh2, x1, dx2, o, ya, yb, ym, ga, gb, gm, g_post, g_pre2, W["w_out"], tm)
    if reduce is not None:
        ffn, (do, dya, dyb, dym) = reduce.finish("ffn", pending, (do, dya, dyb, dym))
    d_w_out = _mm_tn(y, do, "dw_out", 1024)
    dzu, dzv, dws, dbs_cols, dg_sgu = _sgu_bwd(proj, dya, g_sgu, ws_tril, bs_full, tm)
    dqm, dkm, dvm = _memattn_bwd(proj, kv, dym, Bl, S, min(512, S))
    d_w_kv, dg_mem = _memkv_bwd(dkm, dvm, memn, mem, g_mem, W["w_mem_kv"])
    mid = [d_w_kv, d_w_out]
    dq, dk, dv, dc_row = _fox_bwd(proj, dyb, lse, bq, bk, Bl, S)
    if reduce is not None:
        done = reduce.apply(BIG[3:], ffn)
        pending, after = reduce.begin("mid", mid, (dc_row,) + done)
        dc_row = after[0]
    dfl = _gate_bwd(dc_row, fl_row).reshape(T, LANES)
    pieces = (dzu, dzv, dq, dk, dv, dfl, dqm)
    grad_x, dg_pre, dbf = _inproj_bwd(pieces, x2d, dx1, g_pre, W["w_in"], tm)
    if reduce is not None:
        mid, (dfl,) = reduce.finish("mid", pending, (dfl,))
        pieces = (dzu, dzv, dq, dk, dv, dfl, dqm)
    d_w_in = _dw_in(pieces, h, W["w_in"].shape[0], 1024)
    if reduce is None:
        big = dict(zip(BIG, [d_w_in] + mid + ffn))
    else:
        done = reduce.apply(BIG[1:3], mid)
        big = {"w_in": reduce.begin("in", [d_w_in], done)[0]}
    small = {"g_pre_mix": dg_pre, "b_f": dbf[:, :N_FOX_HEADS], "g_sgu": dg_sgu, "w_s": dws, "b_s": dbs_cols[:, :N_FOX_HEADS].T,
             "g_out_a": dga, "g_out_b": dgb, "g_out_m": dgm, "g_mem": dg_mem, "g_post_mix": dg_post,
             "g_pre_ffn": dg_pre2, "g_post_ffn": dg_post2, "loss": loss[:, :1]}
    return grad_x.reshape(Bl, S, D), big, small


def _place():
    return lax.axis_index("x"), lax.axis_index("y"), lax.axis_index("c")


def _exchange_on_sequencer(srcs, own_full, name, collective_id):
    n = len(srcs)

    def body(*refs):
        src, dst = refs[:n], refs[n:2 * n]
        lsem, isend, irecv, dsend, drecv = refs[2 * n:]
        x, y, c = _place()
        oc = 1 - c
        s_me = 2 * x + y
        sib = (x, y, oc)
        chips = [(1 - x, y), (x, 1 - y), (1 - x, 1 - y)]
        barrier = pltpu.get_barrier_semaphore()
        for dev in [(cx, cy, c) for cx, cy in chips] + [sib]:
            pl.semaphore_signal(barrier, inc=1, device_id=dev, device_id_type=MESH)
        pl.semaphore_wait(barrier, 4)

        def remote(a, b, ssem, rsem, dev):
            return pltpu.make_async_remote_copy(src_ref=a, dst_ref=b, send_sem=ssem, recv_sem=rsem,
                                                device_id=dev, device_id_type=MESH)

        sends, local = [], []
        for w in range(n):
            for j, (cx, cy) in enumerate(chips):
                half = src[w].at[c] if own_full else src[w].at[2 * cx + cy]
                cp = remote(half, dst[w].at[s_me, c], isend.at[w, j], irecv.at[w, j], (cx, cy, c))
                cp.start()
                sends.append(cp)
            if own_full:
                cp = remote(src[w], dst[w].at[s_me], dsend.at[w, 3], drecv.at[w, 3], sib)
            else:
                cp = remote(src[w].at[s_me], dst[w].at[s_me, c], dsend.at[w, 3], drecv.at[w, 3], sib)
                loc = pltpu.make_async_copy(src[w].at[s_me], dst[w].at[s_me, c], lsem.at[w])
                loc.start()
                local.append(loc)
            cp.start()
            sends.append(cp)
        for w in range(n):
            for j, (cx, cy) in enumerate(chips):
                landed = dst[w].at[2 * cx + cy, c]
                remote(landed, landed, isend.at[w, j], irecv.at[w, j], (cx, cy, c)).wait_recv()
                cp = remote(landed, landed, dsend.at[w, j], drecv.at[w, j], sib)
                cp.start()
                sends.append(cp)
        for w in range(n):
            for j, (cx, cy) in enumerate(chips):
                landed = dst[w].at[2 * cx + cy, oc]
                remote(landed, landed, dsend.at[w, j], drecv.at[w, j], sib).wait_recv()
            landed = dst[w].at[s_me] if own_full else dst[w].at[s_me, oc]
            remote(landed, landed, dsend.at[w, 3], drecv.at[w, 3], sib).wait_recv()
        for cp in sends:
            cp.wait_send()
        for loc in local:
            loc.wait()

    return pl.kernel(
        body, out_type=[jax.ShapeDtypeStruct((4, 2) + s.shape[1:], s.dtype) for s in srcs],
        mesh=plsc.ScalarSubcoreMesh(axis_name="sequencer", num_cores=1), name=name,
        scratch_types=[pltpu.SemaphoreType.DMA((n,)), pltpu.SemaphoreType.DMA((n, 3)), pltpu.SemaphoreType.DMA((n, 3)),
                       pltpu.SemaphoreType.DMA((n, 4)), pltpu.SemaphoreType.DMA((n, 4))],
        compiler_params=pltpu.CompilerParams(collective_id=collective_id),
    )(*srcs)


def _sibling_swap(grads, name, collective_id):
    n = len(grads)

    def body(*refs):
        g, theirs = refs[:n], refs[n:2 * n]
        ssem, rsem = refs[2 * n:]
        x, y, c = _place()
        sib = (x, y, 1 - c)
        barrier = pltpu.get_barrier_semaphore()
        pl.semaphore_signal(barrier, inc=1, device_id=sib, device_id_type=MESH)
        pl.semaphore_wait(barrier, 1)
        cps = []
        for w in range(n):
            cp = pltpu.make_async_remote_copy(src_ref=g[w].at[:, 1 - c], dst_ref=theirs[w], send_sem=ssem.at[w],
                                              recv_sem=rsem.at[w], device_id=sib, device_id_type=MESH)
            cp.start()
            cps.append(cp)
        for cp in cps:
            cp.wait()

    return pl.kernel(
        body, out_type=[jax.ShapeDtypeStruct((4,) + g.shape[2:], g.dtype) for g in grads],
        mesh=plsc.ScalarSubcoreMesh(axis_name="sequencer", num_cores=1), name=name,
        scratch_types=[pltpu.SemaphoreType.DMA((n,)), pltpu.SemaphoreType.DMA((n,))],
        compiler_params=pltpu.CompilerParams(collective_id=collective_id),
    )(*grads)


def _add_pair(core, g, theirs, name):
    _, _, hr, C = g.shape

    def body(core_ref, g_ref, t_ref, o_ref):
        o_ref[0] = (g_ref[0, 0].astype(F32) + t_ref[0].astype(F32)).astype(BF16)

    blk = BS((1, hr, C), lambda s, core_ref: (s, 0, 0))
    return pl.pallas_call(
        body, name=name,
        grid_spec=pltpu.PrefetchScalarGridSpec(
            num_scalar_prefetch=1, grid=(4,),
            in_specs=[BS((1, 1, hr, C), lambda s, core_ref: (s, core_ref[0], 0, 0)), blk], out_specs=blk),
        out_shape=jax.ShapeDtypeStruct(theirs.shape, BF16), compiler_params=_cp(("arbitrary",)))(core, g, theirs)


def _sum_chips(r, name):
    _, _, hr, C = r.shape

    def body(r_ref, o_ref):
        o_ref[...] = ((r_ref[0, 0].astype(F32) + r_ref[1, 0].astype(F32)) + r_ref[2, 0].astype(F32)) + r_ref[3, 0].astype(F32)

    return pl.pallas_call(body, name=name, grid=(2,), in_specs=[BS((4, 1, hr, C), lambda h: (0, h, 0, 0))],
                          out_specs=BS((hr, C), lambda h: (h, 0)), out_shape=jax.ShapeDtypeStruct((2 * hr, C), F32),
                          compiler_params=_cp(("arbitrary",)))(r)


class _Reducer:
    IDS = {"ffn": (4, 5), "mid": (6, 7), "in": (8, 9)}

    def __init__(self, core, apply):
        self.core = core
        self.apply = apply

    def begin(self, tag, grads, after=()):
        grads, after = lax.optimization_barrier((list(grads), after))
        g4 = [g.reshape(4, 2, -1, g.shape[-1]) for g in grads]
        return (g4, _sibling_swap(g4, "swap_" + tag, self.IDS[tag][0])), after

    def finish(self, tag, pending, hold):
        g4, theirs = pending
        sums = [_add_pair(self.core, g, t, "chip_sum_%s_%d" % (tag, k)) for k, (g, t) in enumerate(zip(g4, theirs))]
        sums, hold = lax.optimization_barrier((sums, hold))
        return _exchange_on_sequencer(sums, False, "scatter_" + tag, self.IDS[tag][1]), hold


def _small_allreduce(part):
    R = part.shape[0]
    rs = R // 8
    masks = [(mx, my, mc) for mx in (0, 1) for my in (0, 1) for mc in (0, 1)][1:]

    def body(p_ref, o_ref, buf_ref, s1, r1, s2, r2):
        x, y, c = _place()
        d = 4 * x + 2 * y + c
        mine = pl.ds(pl.multiple_of(d * rs, 8), rs)
        peers = [((x + mx) % 2, (y + my) % 2, (c + mc) % 2) for mx, my, mc in masks]
        first, second = [], []
        for k, (px, py, pc) in enumerate(peers):
            theirs = pl.ds(pl.multiple_of((4 * px + 2 * py + pc) * rs, 8), rs)
            cp = pltpu.make_async_remote_copy(src_ref=p_ref.at[theirs, :], dst_ref=buf_ref.at[d], send_sem=s1.at[k],
                                              recv_sem=r1.at[k], device_id=(px, py, pc), device_id_type=MESH)
            cp.start()
            first.append(cp)
        buf_ref[d] = p_ref[mine, :]
        for k, (px, py, pc) in enumerate(peers):
            slot = buf_ref.at[4 * px + 2 * py + pc]
            pltpu.make_async_remote_copy(src_ref=slot, dst_ref=slot, send_sem=s1.at[k], recv_sem=r1.at[k],
                                         device_id=(px, py, pc), device_id_type=MESH).wait_recv()
        total = buf_ref[0]
        for k in range(1, 8):
            total = total + buf_ref[k]
        o_ref[mine, :] = total
        for k, (px, py, pc) in enumerate(peers):
            cp = pltpu.make_async_remote_copy(src_ref=o_ref.at[mine, :], dst_ref=o_ref.at[mine, :], send_sem=s2.at[k],
                                              recv_sem=r2.at[k], device_id=(px, py, pc), device_id_type=MESH)
            cp.start()
            second.append(cp)
        for k, (px, py, pc) in enumerate(peers):
            rows = o_ref.at[pl.ds(pl.multiple_of((4 * px + 2 * py + pc) * rs, 8), rs), :]
            pltpu.make_async_remote_copy(src_ref=rows, dst_ref=rows, send_sem=s2.at[k], recv_sem=r2.at[k],
                                         device_id=(px, py, pc), device_id_type=MESH).wait_recv()
        for cp in first + second:
            cp.wait_send()

    vm = pl.BlockSpec(memory_space=pltpu.VMEM)
    return pl.pallas_call(
        body, name="small_allreduce", in_specs=[vm], out_specs=vm, out_shape=jax.ShapeDtypeStruct(part.shape, F32),
        scratch_shapes=[pltpu.VMEM((8, rs, LANES), F32)] + [pltpu.SemaphoreType.DMA((7,))] * 4,
    )(part)


def _adamw(w, g, m, v, name):
    R, C = w.shape
    summed = g.ndim == 4
    if summed:
        tr = R // 2
    else:
        tr = R if R * C * 4 <= (1 << 21) else R // 2
        if tr % 8:
            tr = R
    c1 = 1.0 / (1.0 - ADAM_B1 ** ADAM_STEP)
    c2 = 1.0 / (1.0 - ADAM_B2 ** ADAM_STEP)

    def body(w_ref, g_ref, m_ref, v_ref, *outs):
        if summed:
            g_ = ((g_ref[0, 0].astype(F32) + g_ref[1, 0].astype(F32)) + g_ref[2, 0].astype(F32)) + g_ref[3, 0].astype(F32)
            outs[0][...] = g_
        else:
            g_ = g_ref[...]
        d_ref, mo_ref, vo_ref = outs[-3:]
        m_ = ADAM_B1 * m_ref[...] + (1.0 - ADAM_B1) * g_
        v_ = ADAM_B2 * v_ref[...] + (1.0 - ADAM_B2) * (g_ * g_)
        mo_ref[...] = m_
        vo_ref[...] = v_
        d_ref[...] = -ADAM_LR * ((m_ * c1) / (jnp.sqrt(v_ * c2) + ADAM_EPS) + ADAM_WD * w_ref[...])

    blk = BS((tr, C), lambda i: (i, 0))
    g_blk = BS((4, 1, tr, C), lambda i: (0, i, 0, 0)) if summed else blk
    nout = 4 if summed else 3
    return pl.pallas_call(body, name=name, grid=(R // tr,), in_specs=[blk, g_blk, blk, blk], out_specs=[blk] * nout,
                          out_shape=[jax.ShapeDtypeStruct((R, C), F32)] * nout,
                          compiler_params=_cp(("arbitrary",)))(w, g, m, v)


def _adamw_from_transposed(w, g_t, m, v, name):
    R, C = w.shape
    tc = 256
    c1 = 1.0 / (1.0 - ADAM_B1 ** ADAM_STEP)
    c2 = 1.0 / (1.0 - ADAM_B2 ** ADAM_STEP)

    def body(w_ref, g_ref, m_ref, v_ref, go_ref, d_ref, mo_ref, vo_ref):
        g_ = g_ref[...].T
        go_ref[...] = g_
        m_ = ADAM_B1 * m_ref[...] + (1.0 - ADAM_B1) * g_
        v_ = ADAM_B2 * v_ref[...] + (1.0 - ADAM_B2) * (g_ * g_)
        mo_ref[...] = m_
        vo_ref[...] = v_
        d_ref[...] = -ADAM_LR * ((m_ * c1) / (jnp.sqrt(v_ * c2) + ADAM_EPS) + ADAM_WD * w_ref[...])

    blk = BS((R, tc), lambda i: (0, i))
    return pl.pallas_call(body, name=name, grid=(pl.cdiv(C, tc),), in_specs=[blk, BS((tc, R), lambda i: (i, 0)), blk, blk],
                          out_specs=[blk] * 4, out_shape=[jax.ShapeDtypeStruct((R, C), F32)] * 4,
                          compiler_params=_cp(("arbitrary",)))(w, g_t, m, v)


SMALL = ("g_pre_mix", "b_f", "g_sgu", "w_s", "b_s", "g_out_a", "g_out_b", "g_out_m", "g_mem", "g_post_mix",
         "g_pre_ffn", "g_post_ffn")
BIG = ("w_in", "w_mem_kv", "w_out", "w_gate", "w_up", "w_down")
TRANSPOSED = ("w_in", "w_gate", "w_up")
WEIGHTS = ("g_pre_mix", "w_in", "b_f", "g_sgu", "w_s", "b_s", "g_out_a", "g_out_b", "g_out_m", "g_mem", "w_mem_kv",
           "w_out", "g_post_mix", "g_pre_ffn", "w_gate", "w_up", "w_down", "g_post_ffn")


def _rows_of(n):
    return -(-n // (8 * LANES)) * 8


def _pack(parts):
    tiles = []
    for a in parts:
        flat = a.reshape(-1).astype(F32)
        rows = _rows_of(flat.shape[0])
        tiles.append(jnp.pad(flat, (0, rows * LANES - flat.shape[0])).reshape(rows, LANES))
    total = sum(t.shape[0] for t in tiles)
    pad = -total % 64
    if pad:
        tiles.append(jnp.zeros((pad, LANES), F32))
    return jnp.concatenate(tiles, axis=0)


def _unpack(packed, shapes):
    out, r = [], 0
    for shp in shapes:
        n = 1
        for s in shp:
            n *= s
        rows = _rows_of(n)
        out.append(packed[r:r + rows].reshape(-1)[:n].reshape(shp))
        r += rows
    return out


def kernel(x, mem, g_pre_mix, w_in, b_f, g_sgu, w_s, b_s, g_out_a, g_out_b, g_out_m, g_mem, w_mem_kv, w_out, g_post_mix, g_pre_ffn, w_gate, w_up, w_down, g_post_ffn, loss_target, m_g_pre_mix, m_w_in, m_b_f, m_g_sgu, m_w_s, m_b_s, m_g_out_a, m_g_out_b, m_g_out_m, m_g_mem, m_w_mem_kv, m_w_out, m_g_post_mix, m_g_pre_ffn, m_w_gate, m_w_up, m_w_down, m_g_post_ffn, v_g_pre_mix, v_w_in, v_b_f, v_g_sgu, v_w_s, v_b_s, v_g_out_a, v_g_out_b, v_g_out_m, v_g_mem, v_w_mem_kv, v_w_out, v_g_post_mix, v_g_pre_ffn, v_w_gate, v_w_up, v_w_down, v_g_post_ffn):
    Wt = dict(g_pre_mix=g_pre_mix, w_in=w_in, b_f=b_f, g_sgu=g_sgu, w_s=w_s, b_s=b_s, g_out_a=g_out_a, g_out_b=g_out_b,
              g_out_m=g_out_m, g_mem=g_mem, w_mem_kv=w_mem_kv, w_out=w_out, g_post_mix=g_post_mix, g_pre_ffn=g_pre_ffn,
              w_gate=w_gate, w_up=w_up, w_down=w_down, g_post_ffn=g_post_ffn)
    Mo = dict(g_pre_mix=m_g_pre_mix, w_in=m_w_in, b_f=m_b_f, g_sgu=m_g_sgu, w_s=m_w_s, b_s=m_b_s, g_out_a=m_g_out_a,
              g_out_b=m_g_out_b, g_out_m=m_g_out_m, g_mem=m_g_mem, w_mem_kv=m_w_mem_kv, w_out=m_w_out,
              g_post_mix=m_g_post_mix, g_pre_ffn=m_g_pre_ffn, w_gate=m_w_gate, w_up=m_w_up, w_down=m_w_down,
              g_post_ffn=m_g_post_ffn)
    Vo = dict(g_pre_mix=v_g_pre_mix, w_in=v_w_in, b_f=v_b_f, g_sgu=v_g_sgu, w_s=v_w_s, b_s=v_b_s, g_out_a=v_g_out_a,
              g_out_b=v_g_out_b, g_out_m=v_g_out_m, g_mem=v_g_mem, w_mem_kv=v_w_mem_kv, w_out=v_w_out,
              g_post_mix=v_g_post_mix, g_pre_ffn=v_g_pre_ffn, w_gate=v_w_gate, w_up=v_w_up, w_down=v_w_down,
              g_post_ffn=v_g_post_ffn)

    gap = P_COLS - IN_COLS

    def to_kernel(n, w):
        if n in TRANSPOSED:
            w = w.T
        if n == "w_in":
            w = jnp.pad(w[:F_END], ((0, P_COLS - F_END), (0, 0))) + jnp.pad(w[F_END:], ((F_END + gap, 0), (0, 0)))
        return w

    def ungroup(g):
        return jnp.pad(g[:F_END], ((0, IN_COLS - F_END), (0, 0))) + jnp.pad(g[F_END + gap:], ((F_END, 0), (0, 0)))

    shards = {n: to_kernel(n, Wt[n][0]) for n in BIG}
    srcs = [shards[n].astype(BF16).reshape(2, shards[n].shape[0] // 2, shards[n].shape[1]) for n in BIG]
    fulls = (_exchange_on_sequencer(srcs[:1], True, "gather_w_in", 1)
             + _exchange_on_sequencer(srcs[1:3], True, "gather_kv_out", 2)
             + _exchange_on_sequencer(srcs[3:], True, "gather_ffn", 3))
    W = {}
    for n, f in zip(BIG, fulls):
        _, _, hr, C = f.shape
        W[n] = f.reshape(8 * hr, C) if n in ("w_mem_kv", "w_out") else f.reshape(4, 2 * hr, C)

    P = {n: Wt[n] for n in SMALL}
    grads, deltas, new_m, new_v = {}, {}, {}, {}

    def apply(names, landed):
        for n, r in zip(names, landed):
            if n == "w_in":
                g_t = ungroup(_sum_chips(r, "sum_chips_" + n))
                g, d, m1, v1 = _adamw_from_transposed(Wt[n][0], g_t, Mo[n][0], Vo[n][0], "adamw_" + n)
            elif n in TRANSPOSED:
                g, d, m1, v1 = [a.T for a in _adamw(Wt[n][0].T, r, Mo[n][0].T, Vo[n][0].T, "adamw_" + n)]
            else:
                g, d, m1, v1 = _adamw(Wt[n][0], r, Mo[n][0], Vo[n][0], "adamw_" + n)
            grads[n], deltas[n], new_m[n], new_v[n] = g[None], d[None], m1[None], v1[None]
        return tuple(deltas[n] for n in names)

    core = lax.axis_index("c").astype(jnp.int32).reshape(1)
    reducer = _Reducer(core, apply)
    grad_x, pending, small = _local_step(x, mem, loss_target, W, P, reducer)

    total = _small_allreduce(_pack([small[n] for n in SMALL] + [small["loss"]]))
    landed, (total,) = reducer.finish("in", pending["w_in"], (total,))
    apply(BIG[:1], landed)

    slot = [jnp.zeros((1, 1), F32)]
    shapes = [Wt[n].shape for n in SMALL] + [(1, 1)]
    d, m1, v1 = _adamw(_pack([Wt[n] for n in SMALL] + slot), total, _pack([Mo[n] for n in SMALL] + slot),
                       _pack([Vo[n] for n in SMALL] + slot), "adamw_small")
    g_s, d_s, m_s, v_s = _unpack(total, shapes), _unpack(d, shapes), _unpack(m1, shapes), _unpack(v1, shapes)
    for k, n in enumerate(SMALL):
        grads[n], deltas[n], new_m[n], new_v[n] = g_s[k], d_s[k], m_s[k], v_s[k]
    loss = g_s[-1][0, 0]

    return (loss, grad_x, *[grads[n] for n in WEIGHTS], *[deltas[n] for n in WEIGHTS],
            *[new_m[n] for n in WEIGHTS], *[new_v[n] for n in WEIGHTS])
```

```python
import functools

import jax
import jax.numpy as jnp
from jax import lax
from jax.experimental import pallas as pl
from jax.experimental.pallas import tpu as pltpu
from jax.experimental.pallas import tpu_sc as plsc

F32 = jnp.float32
BF16 = jnp.bfloat16
EPS = 1e-6
NEG = -1e30
HEAD = 64
A_W, B_W, M_W = 384, 384, 256
N_FOX_HEADS = 6
CHUNK = 128
IN_COLS = 2 * A_W + 3 * B_W + N_FOX_HEADS + M_W
P_MAIN = 2 * A_W + 3 * B_W + M_W
P_COLS = P_MAIN + 128
F_END = 2 * A_W + 3 * B_W + N_FOX_HEADS
LANES = 128
Q_BLK, K_BLK = 512, 128
ROW_SPLIT = 4
ADAM_LR, ADAM_B1, ADAM_B2, ADAM_EPS, ADAM_WD, ADAM_STEP = 0.001, 0.9, 0.999, 1e-08, 0.01, 10
VMEM_LIMIT = 56 * 1024 * 1024
MESH = pl.DeviceIdType.MESH
ANY = pl.BlockSpec(memory_space=pl.ANY)
BS = pl.BlockSpec


def _cp(sem=None):
    return pltpu.CompilerParams(dimension_semantics=sem, vmem_limit_bytes=VMEM_LIMIT)


def _iota(shape, dim):
    return lax.broadcasted_iota(jnp.int32, shape, dim)


def _dot(a, b):
    return jnp.dot(a.astype(BF16), b.astype(BF16), preferred_element_type=F32)


def _dot_nt(a, b):
    return lax.dot_general(a.astype(BF16), b.astype(BF16), (((1,), (1,)), ((), ())), preferred_element_type=F32)


def _dot_tn(a, b):
    return lax.dot_general(a.astype(BF16), b.astype(BF16), (((0,), (0,)), ((), ())), preferred_element_type=F32)


def _rms(x, g):
    return x * lax.rsqrt(jnp.mean(x * x, axis=-1, keepdims=True) + EPS) * g


def _rms_bwd(x, g, dy):
    r = lax.rsqrt(jnp.mean(x * x, axis=-1, keepdims=True) + EPS)
    xr = x * r
    gd = dy * g
    m = jnp.mean(gd * xr, axis=-1, keepdims=True)
    return (gd - xr * m) * r, _colsum(dy * xr)


def _gelu(x):
    return 0.5 * x * (1.0 + jnp.tanh(0.7978845608028654 * (x + 0.044715 * (x * x * x))))


def _sigmoid(x):
    return 1.0 / (1.0 + jnp.exp(-x))


def _silu_mul(g, u):
    return g * _sigmoid(g) * u


def _logsig(x):
    return jnp.minimum(x, 0.0) - jnp.log(1.0 + jnp.exp(-jnp.abs(x)))


def _colsum(x):
    return jnp.sum(x, axis=0, keepdims=True)


def _acc(ref, val, first):
    @pl.when(first)
    def _():
        ref[...] = val

    @pl.when(jnp.logical_not(first))
    def _():
        ref[...] += val


def _inproj_fwd(x2d, g_pre, w_in_p, tm):
    T, D = x2d.shape
    CH = 768
    nchunk = P_COLS // CH
    ns, _, dsh = w_in_p.shape

    def body(x_ref, g_ref, w_ref, h_ref, proj_ref, fl_ref):
        h = _rms(x_ref[...], g_ref[...]).astype(BF16)
        h_ref[...] = h
        for n in range(nchunk):
            rows = slice(n * CH, (n + 1) * CH)
            r = _dot_nt(h[:, 0:dsh], w_ref[0, rows, :])
            for s in range(1, ns):
                r = r + _dot_nt(h[:, s * dsh:(s + 1) * dsh], w_ref[s, rows, :])
            if n < nchunk - 1:
                proj_ref[:, rows] = r.astype(BF16)
            else:
                fg = 1920 - n * CH
                proj_ref[:, n * CH:1920] = r[:, :fg].astype(BF16)
                fl_ref[...] = r[:, fg:fg + LANES]
                proj_ref[:, 1920:P_MAIN] = r[:, fg + LANES:].astype(BF16)

    return pl.pallas_call(
        body, name="inproj_fwd", grid=(T // tm,),
        in_specs=[BS((tm, D), lambda i: (i, 0)), BS((1, D), lambda i: (0, 0)),
                  BS((ns, P_COLS, dsh), lambda i: (0, 0, 0))],
        out_specs=[BS((tm, D), lambda i: (i, 0)), BS((tm, P_MAIN), lambda i: (i, 0)), BS((tm, LANES), lambda i: (i, 0))],
        out_shape=[jax.ShapeDtypeStruct((T, D), BF16), jax.ShapeDtypeStruct((T, P_MAIN), BF16),
                   jax.ShapeDtypeStruct((T, LANES), F32)],
        compiler_params=_cp(("arbitrary",)),
    )(x2d, g_pre, w_in_p)


def _gate_fwd(flog3, bf_row):
    Bl, S, _ = flog3.shape
    nb = S // LANES

    def body(f_ref, b_ref, bq_ref, bk_ref, fr_ref):
        row = _iota((LANES, LANES), 0)
        lane = _iota((LANES, LANES), 1)
        one = jnp.ones((LANES, LANES), BF16)
        zero = jnp.zeros((LANES, LANES), BF16)

        carry = jnp.zeros((1, LANES), F32)
        for j in range(nb):
            r0 = j * LANES
            fl = f_ref[0, pl.ds(r0, LANES), :] + b_ref[...]
            fr_ref[0, j] = fl.T[0:8, :]
            c = _logsig(fl)
            for k in (1, 2, 4, 8, 16, 32, 64):
                c = c + jnp.where(row >= k, pltpu.roll(c, k, 0), 0.0)
            total = _colsum(jnp.where(row == LANES - 1, c, 0.0))
            c = c + carry
            carry = carry + total
            for h in range(N_FOX_HEADS):
                col = jnp.sum(jnp.where(lane == h, c, 0.0), axis=1, keepdims=True)
                hi = col.astype(BF16)
                rest = col - hi.astype(F32)
                mid = rest.astype(BF16)
                lo = (rest - mid.astype(F32)).astype(BF16)
                base = _bias_lane(h)
                bq = jnp.where(lane == base, hi, jnp.where(lane == base + 1, mid, jnp.where(lane == base + 2, lo, zero)))
                bq = jnp.where((lane >= base + 3) & (lane < base + 6), one, bq)
                bk = jnp.where(lane == base + 3, -hi, jnp.where(lane == base + 4, -mid, jnp.where(lane == base + 5, -lo, zero)))
                bk = jnp.where((lane >= base) & (lane < base + 3), one, bk)
                bq_ref[0, h, pl.ds(r0, LANES), :] = bq
                bk_ref[0, h, pl.ds(r0, LANES), :] = bk

    slab = BS((1, N_FOX_HEADS, S, LANES), lambda b: (b, 0, 0, 0))
    return pl.pallas_call(
        body, name="gate_fwd", grid=(Bl,),
        in_specs=[BS((1, S, LANES), lambda b: (b, 0, 0)), BS((1, LANES), lambda b: (0, 0))],
        out_specs=[slab, slab, BS((1, nb, 8, LANES), lambda b: (b, 0, 0, 0))],
        out_shape=[jax.ShapeDtypeStruct((Bl, N_FOX_HEADS, S, LANES), BF16),
                   jax.ShapeDtypeStruct((Bl, N_FOX_HEADS, S, LANES), BF16),
                   jax.ShapeDtypeStruct((Bl, nb, 8, LANES), F32)],
        compiler_params=_cp(("arbitrary",)),
    )(flog3, bf_row)


def _bias_lane(h):
    return HEAD if h % 2 == 0 else 0


def _sgu_pre(zu, zv, g_sgu):
    return _gelu(zu), _rms(_gelu(zv), g_sgu)


def _sgu_fwd(proj, g_sgu, ws_tril, bs_full, tm):
    T = proj.shape[0]
    nch = tm // CHUNK

    def body(zu_ref, zv_ref, g_ref, ws_ref, b_ref, ya_ref):
        lane = _iota((CHUNK, LANES), 1)
        u, vn = _sgu_pre(zu_ref[...].astype(F32), zv_ref[...].astype(F32), g_ref[...])
        vn = vn.astype(BF16)
        for c in range(nch):
            rs = slice(c * CHUNK, (c + 1) * CHUNK)
            for j in range(3):
                cs = slice(j * LANES, (j + 1) * LANES)
                vp = vn[rs, cs]
                z = jnp.where(lane < HEAD, _dot(ws_ref[2 * j], vp), _dot(ws_ref[2 * j + 1], vp)) + b_ref[:, cs]
                ya_ref[rs, cs] = (u[rs, cs] * z).astype(BF16)

    return pl.pallas_call(
        body, name="sgu_fwd", grid=(T // tm,),
        in_specs=[BS((tm, A_W), lambda i: (i, 0)), BS((tm, A_W), lambda i: (i, 1)), BS((1, A_W), lambda i: (0, 0)),
                  BS((6, CHUNK, CHUNK), lambda i: (0, 0, 0)), BS((CHUNK, A_W), lambda i: (0, 0))],
        out_specs=BS((tm, A_W), lambda i: (i, 0)),
        out_shape=jax.ShapeDtypeStruct((T, A_W), BF16),
        compiler_params=_cp(("arbitrary",)),
    )(proj, proj, g_sgu, ws_tril, bs_full)


def _fox_fwd(proj, bq, bk, Bl, S):
    T = Bl * S
    nq = S // Q_BLK
    qc, kc, vc = 768 // LANES, 1152 // LANES, 1536 // LANES

    def body(q_ref, k_ref, v_ref, bq_ref, bk_ref, o_ref, lse_ref, ka_ref, va_ref):
        lane_s = _iota((S, LANES), 1)
        lane = _iota((Q_BLK, LANES), 1)
        tri = _iota((Q_BLK, Q_BLK), 1) <= _iota((Q_BLK, Q_BLK), 0)
        k = k_ref[...]
        v = v_ref[...]
        for hh in range(2):
            data = (lane_s < HEAD) if hh == 0 else (lane_s >= HEAD)
            ka_ref[hh] = jnp.where(data, k, bk_ref[0, hh])
            va_ref[hh] = jnp.where(lane_s == _bias_lane(hh), jnp.ones_like(v), v)
        for i in range(nq):
            r0 = i * Q_BLK
            q = q_ref[r0:r0 + Q_BLK, :]
            o_out = jnp.zeros((Q_BLK, LANES), F32)
            lse_out = jnp.zeros((Q_BLK, LANES), F32)
            for hh in range(2):
                hmask = (lane < HEAD) if hh == 0 else (lane >= HEAD)
                qa = jnp.where(hmask, q * 0.125, bq_ref[0, hh, r0:r0 + Q_BLK, :])
                sd = jnp.where(tri, _dot_nt(qa, ka_ref[hh, r0:r0 + Q_BLK, :]), NEG)
                m = jnp.max(sd, axis=1, keepdims=True)
                if i:
                    sf = _dot_nt(qa, ka_ref[hh, 0:r0, :])
                    m = jnp.maximum(m, jnp.max(sf, axis=1, keepdims=True))
                acc = _dot(jnp.exp(sd - m), va_ref[hh, r0:r0 + Q_BLK, :])
                if i:
                    acc = acc + _dot(jnp.exp(sf - m), va_ref[hh, 0:r0, :])
                l = jnp.sum(jnp.where(lane == _bias_lane(hh), acc, 0.0), axis=1, keepdims=True)
                o_out = jnp.where(hmask, acc / l, o_out)
                lse_out = jnp.where(hmask, m + jnp.log(l), lse_out)
            o_ref[r0:r0 + Q_BLK, :] = o_out.astype(BF16)
            lse_ref[0, r0:r0 + Q_BLK, :] = lse_out

    seq = lambda c0: BS((S, LANES), lambda b, p: (b, c0 + p))
    pair = BS((1, 2, S, LANES), lambda b, p: (b, p, 0, 0))
    return pl.pallas_call(
        body, name="fox_fwd", grid=(Bl, 3),
        in_specs=[seq(qc), seq(kc), seq(vc), pair, pair],
        out_specs=[seq(0), BS((1, S, LANES), lambda b, p: (p, b, 0))],
        out_shape=[jax.ShapeDtypeStruct((T, B_W), BF16), jax.ShapeDtypeStruct((3, T, LANES), F32)],
        scratch_shapes=[pltpu.VMEM((2, S, LANES), BF16), pltpu.VMEM((2, S, LANES), BF16)],
        compiler_params=_cp(("arbitrary", "arbitrary")),
    )(proj, proj, proj, bq, bk)


def _memkv_fwd(mem, g_mem, w_kv):
    Bl, Mt, D = mem.shape

    def body(m_ref, g_ref, w_ref, mn_ref, kv_ref):
        mn = _rms(m_ref[0], g_ref[...]).astype(BF16)
        mn_ref[0] = mn
        kv_ref[0] = jnp.dot(mn, w_ref[...], preferred_element_type=F32).astype(BF16)

    return pl.pallas_call(
        body, name="memkv_fwd", grid=(Bl,),
        in_specs=[BS((1, Mt, D), lambda b: (b, 0, 0)), BS((1, D), lambda b: (0, 0)), BS((D, 2 * M_W), lambda b: (0, 0))],
        out_specs=[BS((1, Mt, D), lambda b: (b, 0, 0)), BS((1, Mt, 2 * M_W), lambda b: (b, 0, 0))],
        out_shape=[jax.ShapeDtypeStruct((Bl, Mt, D), BF16), jax.ShapeDtypeStruct((Bl, Mt, 2 * M_W), BF16)],
        compiler_params=_cp(("arbitrary",)),
    )(mem, g_mem, w_kv)


def _memattn_fwd(proj, kv, Bl, S, tq):
    T = Bl * S
    nq = S // tq
    Mt = kv.shape[1]
    qc = 1920 // LANES

    def body(q_ref, km_ref, vm_ref, o_ref):
        lane = _iota((tq, LANES), 1)
        q = q_ref[...]
        out = jnp.zeros((tq, LANES), F32)
        for hh in range(2):
            hmask = (lane < HEAD) if hh == 0 else (lane >= HEAD)
            qs = jnp.where(hmask, q, jnp.zeros_like(q)) * 0.125
            s = _dot_nt(qs, km_ref[0])
            pe = jnp.exp(s - jnp.max(s, axis=1, keepdims=True))
            pn = pe / jnp.sum(pe, axis=1, keepdims=True)
            out = jnp.where(hmask, _dot(pn, vm_ref[0]), out)
        o_ref[...] = out.astype(BF16)

    return pl.pallas_call(
        body, name="memattn_fwd", grid=(Bl, 2, nq),
        in_specs=[BS((tq, LANES), lambda b, p, i: (b * nq + i, qc + p)),
                  BS((1, Mt, LANES), lambda b, p, i: (b, 0, p)),
                  BS((1, Mt, LANES), lambda b, p, i: (b, 0, 2 + p))],
        out_specs=BS((tq, LANES), lambda b, p, i: (b * nq + i, p)),
        out_shape=jax.ShapeDtypeStruct((T, M_W), BF16),
        compiler_params=_cp(("arbitrary", "arbitrary", "arbitrary")),
    )(proj, kv, kv)


def _mix_norms(ya, yb, ym, ga, gb, gm):
    return _rms(ya, ga), _rms(yb, gb), _rms(ym, gm)


def _outproj_fwd(ya, yb, ym, x2d, ga, gb, gm, g_post, g_pre2, w_out, tm):
    T, D = x2d.shape

    def body(ya_ref, yb_ref, ym_ref, x_ref, ga_ref, gb_ref, gm_ref, gp_ref, g2_ref, w_ref,
             y_ref, o_ref, x1_ref, h2_ref):
        na, nb_, nm = _mix_norms(ya_ref[...].astype(F32), yb_ref[...].astype(F32), ym_ref[...].astype(F32),
                                 ga_ref[...], gb_ref[...], gm_ref[...])
        y_ref[:, 0:A_W] = na.astype(BF16)
        y_ref[:, A_W:A_W + B_W] = nb_.astype(BF16)
        y_ref[:, A_W + B_W:] = nm.astype(BF16)
        o = jnp.dot(y_ref[...], w_ref[...], preferred_element_type=F32).astype(BF16)
        o_ref[...] = o
        x1 = x_ref[...] + _rms(o.astype(F32), gp_ref[...])
        x1_ref[...] = x1
        h2_ref[...] = _rms(x1, g2_ref[...]).astype(BF16)

    row = lambda w: BS((tm, w), lambda i: (i, 0))
    vec = lambda w: BS((1, w), lambda i: (0, 0))
    return pl.pallas_call(
        body, name="outproj_fwd", grid=(T // tm,),
        in_specs=[row(A_W), row(B_W), row(M_W), row(D), vec(A_W), vec(B_W), vec(M_W), vec(D), vec(D),
                  BS((A_W + B_W + M_W, D), lambda i: (0, 0))],
        out_specs=[row(A_W + B_W + M_W), row(D), row(D), row(D)],
        out_shape=[jax.ShapeDtypeStruct((T, A_W + B_W + M_W), BF16), jax.ShapeDtypeStruct((T, D), BF16),
                   jax.ShapeDtypeStruct((T, D), F32), jax.ShapeDtypeStruct((T, D), BF16)],
        compiler_params=_cp(("arbitrary",)),
    )(ya, yb, ym, x2d, ga, gb, gm, g_post, g_pre2, w_out)


def _ffn_fwd(h2, x1, target, wg, wu, wd, g_post, tm):
    T, D = x1.shape
    ns, F, _ = wg.shape

    def body(h_ref, x1_ref, t_ref, wg_ref, wu_ref, wd_ref, gp_ref,
             gs_ref, us_ref, dff_ref, dx2_ref, dgp_ref, loss_ref, acc_ref):
        j = pl.program_id(0)
        i = pl.program_id(1)
        rows = pl.ds(pl.multiple_of(i * tm, tm), tm)
        h = h_ref[...]
        g = _dot_nt(h, wg_ref[0])
        u = _dot_nt(h, wu_ref[0])
        gs_ref[0] = g.astype(BF16)
        us_ref[0] = u.astype(BF16)
        part = _dot(_silu_mul(g, u), wd_ref[0])

        @pl.when(j == 0)
        def _():
            acc_ref[rows, :] = part

        @pl.when(j != 0)
        def _():
            acc_ref[rows, :] += part

        @pl.when(j == ns - 1)
        def _():
            ff = acc_ref[rows, :]
            diff = x1_ref[...] + _rms(ff, gp_ref[...]) - t_ref[...]
            dx2 = diff * (1.0 / D)
            dff, dgp = _rms_bwd(ff, gp_ref[...], dx2)
            dx2_ref[...] = dx2
            dff_ref[...] = dff.astype(BF16)
            lpart = jnp.sum(_colsum(diff * diff), axis=1, keepdims=True) * (0.5 / D)
            _acc(dgp_ref, dgp, i == 0)
            _acc(loss_ref, jnp.broadcast_to(lpart, (1, LANES)), i == 0)

    last = lambda j, i: (jnp.where(j == ns - 1, i, 0), 0)
    wsh = BS((1, F, D), lambda j, i: (j, 0, 0))
    sh = BS((1, tm, F), lambda j, i: (j, i, 0))
    return pl.pallas_call(
        body, name="ffn_fwd", grid=(ns, T // tm),
        in_specs=[BS((tm, D), lambda j, i: (i, 0)), BS((tm, D), last), BS((tm, D), last), wsh, wsh, wsh,
                  BS((1, D), lambda j, i: (0, 0))],
        out_specs=[sh, sh, BS((tm, D), last), BS((tm, D), last),
                   BS((1, D), lambda j, i: (0, 0)), BS((1, LANES), lambda j, i: (0, 0))],
        out_shape=[jax.ShapeDtypeStruct((ns, T, F), BF16), jax.ShapeDtypeStruct((ns, T, F), BF16),
                   jax.ShapeDtypeStruct((T, D), BF16), jax.ShapeDtypeStruct((T, D), F32),
                   jax.ShapeDtypeStruct((1, D), F32), jax.ShapeDtypeStruct((1, LANES), F32)],
        scratch_shapes=[pltpu.VMEM((T, D), F32)],
        compiler_params=_cp(("arbitrary", "arbitrary")),
    )(h2, x1, target, wg, wu, wd, g_post)


def _ffn_bwd(dff, h2, gs, us, wg, wu, wd, tm):
    T, D = h2.shape
    ns, F, _ = wg.shape

    def body(dff_ref, h_ref, gs_ref, us_ref, wg_ref, wu_ref, wd_ref, dh_ref, dwg_out, dwu_out, dwd_out,
             dwg_ref, dwu_ref, dwd_ref):
        first = pl.program_id(1) == 0
        dff = dff_ref[...]
        h = h_ref[...]
        parts = []
        for r in range(ROW_SPLIT):
            rows = slice(r * (tm // ROW_SPLIT), (r + 1) * (tm // ROW_SPLIT))
            dact = _dot_nt(dff[rows], wd_ref[0])
            g = gs_ref[0, rows, :].astype(F32)
            u = us_ref[0, rows, :].astype(F32)
            sig = _sigmoid(g)
            gsig = g * sig
            dg = (dact * u * (sig + gsig * (1.0 - sig))).astype(BF16)
            du = (dact * gsig).astype(BF16)
            dh_ref[0, rows, :] = (_dot(dg, wg_ref[0]) + _dot(du, wu_ref[0])).astype(BF16)
            parts.append(((gsig * u).astype(BF16), dg, du))
        a, dg, du = [jnp.concatenate(p, axis=0) for p in zip(*parts)]
        _acc(dwd_ref, _dot_tn(a, dff), first)
        _acc(dwg_ref, _dot_tn(dg, h), first)
        _acc(dwu_ref, _dot_tn(du, h), first)

        @pl.when(pl.program_id(1) == pl.num_programs(1) - 1)
        def _():
            dwg_out[0] = dwg_ref[...].astype(BF16)
            dwu_out[0] = dwu_ref[...].astype(BF16)
            dwd_out[0] = dwd_ref[...].astype(BF16)

    row = BS((tm, D), lambda j, i: (i, 0))
    sh = BS((1, tm, F), lambda j, i: (j, i, 0))
    wsh = BS((1, F, D), lambda j, i: (j, 0, 0))
    return pl.pallas_call(
        body, name="ffn_bwd", grid=(ns, T // tm),
        in_specs=[row, row, sh, sh, wsh, wsh, wsh],
        out_specs=[BS((1, tm, D), lambda j, i: (j, i, 0)), wsh, wsh, wsh],
        out_shape=[jax.ShapeDtypeStruct((ns, T, D), BF16)] + [jax.ShapeDtypeStruct((ns, F, D), BF16)] * 3,
        scratch_shapes=[pltpu.VMEM((F, D), F32)] * 3,
        compiler_params=_cp(("arbitrary", "arbitrary")),
    )(dff, h2, gs, us, wg, wu, wd)


def _mm_tn(a, b, name, tk):
    T, M = a.shape
    N = b.shape[1]
    tk = min(tk, T)

    def body(a_ref, b_ref, o_ref):
        _acc(o_ref, _dot_tn(a_ref[...], b_ref[...]), pl.program_id(0) == 0)

    return pl.pallas_call(
        body, name=name, grid=(T // tk,),
        in_specs=[BS((tk, M), lambda t: (t, 0)), BS((tk, N), lambda t: (t, 0))],
        out_specs=BS((M, N), lambda t: (0, 0)),
        out_shape=jax.ShapeDtypeStruct((M, N), F32),
        compiler_params=_cp(("arbitrary",)),
    )(a, b)


DPROJ_PIECES = ((0, A_W), (A_W, A_W), (768, B_W), (1152, B_W), (1536, B_W), (1920, LANES), (2048, M_W))


def _put_dproj(dp_ref, piece_refs):
    for (c0, w), ref in zip(DPROJ_PIECES, piece_refs):
        dp_ref[:, c0:c0 + w] = ref[...].astype(BF16)


def _dw_in(pieces, h, ns, tk):
    T, D = h.shape
    M = P_COLS
    dsh = D // ns
    tk = min(tk, T)

    def body(*refs):
        piece_refs, h_ref, o_ref, acc_ref, dp_ref = refs[:7], refs[7], refs[8], refs[9], refs[10]
        t = pl.program_id(0)
        _put_dproj(dp_ref, piece_refs)
        _acc(acc_ref, _dot_tn(h_ref[...], dp_ref[...]), t == 0)

        @pl.when(t == pl.num_programs(0) - 1)
        def _():
            for s in range(ns):
                o_ref[s] = acc_ref[s * dsh:(s + 1) * dsh, :].T.astype(BF16)

    return pl.pallas_call(
        body, name="dw_in", grid=(T // tk,),
        in_specs=[BS((tk, w), lambda t: (t, 0)) for _, w in DPROJ_PIECES] + [BS((tk, D), lambda t: (t, 0))],
        out_specs=BS((ns, M, dsh), lambda t: (0, 0, 0)),
        out_shape=jax.ShapeDtypeStruct((ns, M, dsh), BF16),
        scratch_shapes=[pltpu.VMEM((D, M), F32), pltpu.VMEM((tk, M), BF16)],
        compiler_params=_cp(("arbitrary",)),
    )(*pieces, h)


def _outproj_bwd(dh2, x1, dx2, o, ya, yb, ym, ga, gb, gm, g_post, g_pre2, w_out, tm):
    T, D = x1.shape
    ns = dh2.shape[0]

    def body(dh_ref, x1_ref, dx2_ref, o_ref, ya_ref, yb_ref, ym_ref, ga_ref, gb_ref, gm_ref, gp_ref, g2_ref, w_ref,
             dx1_ref, do_ref, dya_ref, dyb_ref, dym_ref, dga_ref, dgb_ref, dgm_ref, dgp_ref, dg2_ref):
        first = pl.program_id(0) == 0
        dh = dh_ref[0].astype(F32)
        for j in range(1, ns):
            dh = dh + dh_ref[j].astype(F32)
        dxa, dg2 = _rms_bwd(x1_ref[...], g2_ref[...], dh)
        dx1 = dx2_ref[...] + dxa
        dx1_ref[...] = dx1
        _acc(dg2_ref, dg2, first)
        do, dgp = _rms_bwd(o_ref[...].astype(F32), gp_ref[...], dx1)
        do = do.astype(BF16)
        do_ref[...] = do
        dy = _dot_nt(do, w_ref[...])
        dya, dga = _rms_bwd(ya_ref[...].astype(F32), ga_ref[...], dy[:, 0:A_W])
        dyb, dgb = _rms_bwd(yb_ref[...].astype(F32), gb_ref[...], dy[:, A_W:A_W + B_W])
        dym, dgm = _rms_bwd(ym_ref[...].astype(F32), gm_ref[...], dy[:, A_W + B_W:])
        dya_ref[...] = dya.astype(BF16)
        dyb_ref[...] = dyb.astype(BF16)
        dym_ref[...] = dym.astype(BF16)
        _acc(dga_ref, dga, first)
        _acc(dgb_ref, dgb, first)
        _acc(dgm_ref, dgm, first)
        _acc(dgp_ref, dgp, first)

    row = lambda w: BS((tm, w), lambda i: (i, 0))
    vec = lambda w: BS((1, w), lambda i: (0, 0))
    sds = jax.ShapeDtypeStruct
    return pl.pallas_call(
        body, name="outproj_bwd", grid=(T // tm,),
        in_specs=[BS((ns, tm, D), lambda i: (0, i, 0)), row(D), row(D), row(D), row(A_W), row(B_W), row(M_W),
                  vec(A_W), vec(B_W), vec(M_W), vec(D), vec(D), BS((A_W + B_W + M_W, D), lambda i: (0, 0))],
        out_specs=[row(D), row(D), row(A_W), row(B_W), row(M_W), vec(A_W), vec(B_W), vec(M_W), vec(D), vec(D)],
        out_shape=[sds((T, D), F32), sds((T, D), BF16), sds((T, A_W), BF16), sds((T, B_W), BF16), sds((T, M_W), BF16),
                   sds((1, A_W), F32), sds((1, B_W), F32), sds((1, M_W), F32), sds((1, D), F32), sds((1, D), F32)],
        compiler_params=_cp(("arbitrary",)),
    )(dh2, x1, dx2, o, ya, yb, ym, ga, gb, gm, g_post, g_pre2, w_out)


def _sgu_bwd(proj, dya, g_sgu, ws_tril, bs_full, tm):
    T = proj.shape[0]
    nch = tm // CHUNK

    def body(zu_ref, zv_ref, dy_ref, g_ref, ws_ref, b_ref, dzu_ref, dzv_ref, dws_ref, dbs_ref, dg_ref,
             du_ref, dvn_ref, dbf_ref):
        step = pl.program_id(0)
        first = step == 0
        lane = _iota((CHUNK, LANES), 1)
        tril = _iota((CHUNK, CHUNK), 0) >= _iota((CHUNK, CHUNK), 1)
        (u, vn), vjp = jax.vjp(_sgu_pre, zu_ref[...].astype(F32), zv_ref[...].astype(F32), g_ref[...])
        vnb = vn.astype(BF16)
        dy = dy_ref[...].astype(F32)

        @pl.when(first)
        def _():
            dws_ref[...] = jnp.zeros_like(dws_ref)
            dbf_ref[...] = jnp.zeros_like(dbf_ref)

        for c in range(nch):
            rs = slice(c * CHUNK, (c + 1) * CHUNK)
            for j in range(3):
                cs = slice(j * LANES, (j + 1) * LANES)
                vp = vnb[rs, cs]
                z = jnp.where(lane < HEAD, _dot(ws_ref[2 * j], vp), _dot(ws_ref[2 * j + 1], vp)) + b_ref[:, cs]
                du_ref[rs, cs] = dy[rs, cs] * z
                dz = dy[rs, cs] * u[rs, cs]
                dbf_ref[:, cs] += dz
                dzb = dz.astype(BF16)
                dz0 = jnp.where(lane < HEAD, dzb, jnp.zeros_like(dzb))
                dz1 = jnp.where(lane >= HEAD, dzb, jnp.zeros_like(dzb))
                dvn_ref[rs, cs] = jnp.where(lane < HEAD, _dot_tn(ws_ref[2 * j], dzb), _dot_tn(ws_ref[2 * j + 1], dzb))
                dws_ref[2 * j] += jnp.where(tril, _dot_nt(dz0, vp), 0.0)
                dws_ref[2 * j + 1] += jnp.where(tril, _dot_nt(dz1, vp), 0.0)
        dzu, dzv, dg = vjp((du_ref[...], dvn_ref[...]))
        dzu_ref[...] = dzu.astype(BF16)
        dzv_ref[...] = dzv.astype(BF16)
        _acc(dg_ref, dg, first)

        @pl.when(step == pl.num_programs(0) - 1)
        def _():
            out = jnp.zeros((CHUNK, LANES), F32)
            for j in range(3):
                slab = dbf_ref[:, j * LANES:(j + 1) * LANES]
                lo = jnp.sum(jnp.where(lane < HEAD, slab, 0.0), axis=1, keepdims=True)
                hi = jnp.sum(jnp.where(lane >= HEAD, slab, 0.0), axis=1, keepdims=True)
                out = out + jnp.where(lane == 2 * j, lo, 0.0) + jnp.where(lane == 2 * j + 1, hi, 0.0)
            dbs_ref[...] = out

    return pl.pallas_call(
        body, name="sgu_bwd", grid=(T // tm,),
        in_specs=[BS((tm, A_W), lambda i: (i, 0)), BS((tm, A_W), lambda i: (i, 1)), BS((tm, A_W), lambda i: (i, 0)),
                  BS((1, A_W), lambda i: (0, 0)), BS((6, CHUNK, CHUNK), lambda i: (0, 0, 0)),
                  BS((CHUNK, A_W), lambda i: (0, 0))],
        out_specs=[BS((tm, A_W), lambda i: (i, 0)), BS((tm, A_W), lambda i: (i, 0)),
                   BS((6, CHUNK, CHUNK), lambda i: (0, 0, 0)), BS((CHUNK, LANES), lambda i: (0, 0)),
                   BS((1, A_W), lambda i: (0, 0))],
        out_shape=[jax.ShapeDtypeStruct((T, A_W), BF16), jax.ShapeDtypeStruct((T, A_W), BF16),
                   jax.ShapeDtypeStruct((6, CHUNK, CHUNK), F32), jax.ShapeDtypeStruct((CHUNK, LANES), F32),
                   jax.ShapeDtypeStruct((1, A_W), F32)],
        scratch_shapes=[pltpu.VMEM((tm, A_W), F32), pltpu.VMEM((tm, A_W), F32), pltpu.VMEM((CHUNK, A_W), F32)],
        compiler_params=_cp(("arbitrary",)),
    )(proj, proj, dya, g_sgu, ws_tril, bs_full)


def _memattn_bwd(proj, kv, dym, Bl, S, tq):
    T = Bl * S
    nq = S // tq
    Mt = kv.shape[1]
    qc = 1920 // LANES

    def body(q_ref, km_ref, vm_ref, do_ref, dq_ref, dkm_ref, dvm_ref):
        first = pl.program_id(2) == 0
        lane = _iota((tq, LANES), 1)
        q = q_ref[...]
        do = do_ref[...]
        dq_out = jnp.zeros((tq, LANES), F32)
        dkm = jnp.zeros((Mt, LANES), F32)
        dvm = jnp.zeros((Mt, LANES), F32)
        for hh in range(2):
            hmask = (lane < HEAD) if hh == 0 else (lane >= HEAD)
            qs = jnp.where(hmask, q, jnp.zeros_like(q)) * 0.125
            dom = jnp.where(hmask, do, 0.0).astype(BF16)
            s = _dot_nt(qs, km_ref[0])
            pe = jnp.exp(s - jnp.max(s, axis=1, keepdims=True))
            pn = pe / jnp.sum(pe, axis=1, keepdims=True)
            dp = _dot_nt(dom, vm_ref[0])
            ds = (pn * (dp - jnp.sum(pn * dp, axis=1, keepdims=True))).astype(BF16)
            dq_out = jnp.where(hmask, _dot(ds, km_ref[0]) * 0.125, dq_out)
            dkm = dkm + _dot_tn(ds, qs)
            dvm = dvm + _dot_tn(pn, dom)
        dq_ref[...] = dq_out.astype(BF16)
        _acc(dkm_ref, dkm[None], first)
        _acc(dvm_ref, dvm[None], first)

    return pl.pallas_call(
        body, name="memattn_bwd", grid=(Bl, 2, nq),
        in_specs=[BS((tq, LANES), lambda b, p, i: (b * nq + i, qc + p)),
                  BS((1, Mt, LANES), lambda b, p, i: (b, 0, p)),
                  BS((1, Mt, LANES), lambda b, p, i: (b, 0, 2 + p)),
                  BS((tq, LANES), lambda b, p, i: (b * nq + i, p))],
        out_specs=[BS((tq, LANES), lambda b, p, i: (b * nq + i, p)),
                   BS((1, Mt, LANES), lambda b, p, i: (b, 0, p)),
                   BS((1, Mt, LANES), lambda b, p, i: (b, 0, p))],
        out_shape=[jax.ShapeDtypeStruct((T, M_W), BF16), jax.ShapeDtypeStruct((Bl, Mt, M_W), F32),
                   jax.ShapeDtypeStruct((Bl, Mt, M_W), F32)],
        compiler_params=_cp(("arbitrary", "arbitrary", "arbitrary")),
    )(proj, kv, kv, dym)


def _memkv_bwd(dkm, dvm, memn, mem, g_mem, w_kv):
    Bl, Mt, D = mem.shape

    def body(dk_ref, dv_ref, mn_ref, m_ref, g_ref, w_ref, dw_ref, dg_ref):
        first = pl.program_id(0) == 0
        dk = dk_ref[0].astype(BF16)
        dv = dv_ref[0].astype(BF16)
        mn = mn_ref[0]
        dmn = _dot_nt(dk, w_ref[:, 0:M_W]) + _dot_nt(dv, w_ref[:, M_W:])
        _, dg = _rms_bwd(m_ref[0], g_ref[...], dmn)
        _acc(dg_ref, dg, first)

        @pl.when(first)
        def _():
            dw_ref[...] = jnp.zeros_like(dw_ref)

        dw_ref[:, 0:M_W] += _dot_tn(mn, dk)
        dw_ref[:, M_W:] += _dot_tn(mn, dv)

    return pl.pallas_call(
        body, name="memkv_bwd", grid=(Bl,),
        in_specs=[BS((1, Mt, M_W), lambda b: (b, 0, 0)), BS((1, Mt, M_W), lambda b: (b, 0, 0)),
                  BS((1, Mt, D), lambda b: (b, 0, 0)), BS((1, Mt, D), lambda b: (b, 0, 0)),
                  BS((1, D), lambda b: (0, 0)), BS((D, 2 * M_W), lambda b: (0, 0))],
        out_specs=[BS((D, 2 * M_W), lambda b: (0, 0)), BS((1, D), lambda b: (0, 0))],
        out_shape=[jax.ShapeDtypeStruct((D, 2 * M_W), F32), jax.ShapeDtypeStruct((1, D), F32)],
        compiler_params=_cp(("arbitrary",)),
    )(dkm, dvm, memn, mem, g_mem, w_kv)


def _fox_bwd(proj, dyb, lse, bq, bk, Bl, S):
    T = Bl * S
    nq = S // Q_BLK
    nb = S // LANES
    qc, kc, vc = 768 // LANES, 1152 // LANES, 1536 // LANES

    def body(q_ref, k_ref, v_ref, do_ref, lse_ref, bq_ref, bk_ref,
             dq_ref, dk_ref, dv_ref, dcr_ref, ka_ref, dka_ref, dva_ref):
        p = pl.program_id(1)
        lane_s = _iota((S, LANES), 1)
        lane = _iota((Q_BLK, LANES), 1)
        sub = _iota((8, LANES), 0)
        tri = _iota((Q_BLK, Q_BLK), 1) <= _iota((Q_BLK, Q_BLK), 0)
        k = k_ref[...]
        for hh in range(2):
            data = (lane_s < HEAD) if hh == 0 else (lane_s >= HEAD)
            ka_ref[hh] = jnp.where(data, k, bk_ref[0, hh])
        dka_ref[...] = jnp.zeros_like(dka_ref)
        dva_ref[...] = jnp.zeros_like(dva_ref)

        @pl.when(p == 0)
        def _():
            dcr_ref[...] = jnp.zeros_like(dcr_ref)

        def add_colsums(ds, first_blk, h):
            cs = _colsum(ds)
            for jb in range(ds.shape[1] // LANES):
                dcr_ref[0, first_blk + jb] += jnp.where(sub == h, cs[:, jb * LANES:(jb + 1) * LANES], 0.0)

        for i in range(nq):
            r0 = i * Q_BLK
            r1 = r0 + Q_BLK
            q = q_ref[r0:r1, :]
            do = do_ref[r0:r1, :]
            lse_b = lse_ref[0, r0:r1, :]
            dq_out = jnp.zeros((Q_BLK, LANES), F32)
            for hh in range(2):
                hmask = (lane < HEAD) if hh == 0 else (lane >= HEAD)
                h = 2 * p + hh
                qs = jnp.where(hmask, q * 0.125, jnp.zeros_like(q))
                qa = jnp.where(hmask, q * 0.125, bq_ref[0, hh, r0:r1, :])
                dob = jnp.where(hmask, do, 0.0).astype(BF16)
                lse_h = jnp.sum(jnp.where(lane == hh * HEAD, lse_b, 0.0), axis=1, keepdims=True)
                pd = jnp.where(tri, jnp.exp(_dot_nt(qa, ka_ref[hh, r0:r1, :]) - lse_h), 0.0)
                dpd = _dot_nt(dob, v_ref[r0:r1, :])
                delta = jnp.sum(pd * dpd, axis=1, keepdims=True)
                psum = jnp.sum(pd, axis=1, keepdims=True)
                if i:
                    pf = jnp.exp(_dot_nt(qa, ka_ref[hh, 0:r0, :]) - lse_h)
                    dpf = _dot_nt(dob, v_ref[0:r0, :])
                    delta = delta + jnp.sum(pf * dpf, axis=1, keepdims=True)
                    psum = psum + jnp.sum(pf, axis=1, keepdims=True)
                delta = delta / psum
                dsd = pd * (dpd - delta)
                add_colsums(dsd, r0 // LANES, h)
                dsd = dsd.astype(BF16)
                dq_h = _dot(dsd, k_ref[r0:r1, :])
                dka_ref[r0:r1, :] += _dot_tn(dsd, qs)
                dva_ref[r0:r1, :] += _dot_tn(pd, dob)
                if i:
                    dsf = pf * (dpf - delta)
                    add_colsums(dsf, 0, h)
                    dsf = dsf.astype(BF16)
                    dq_h = dq_h + _dot(dsf, k_ref[0:r0, :])
                    dka_ref[0:r0, :] += _dot_tn(dsf, qs)
                    dva_ref[0:r0, :] += _dot_tn(pf, dob)
                dq_out = jnp.where(hmask, dq_h * 0.125, dq_out)
            dq_ref[r0:r1, :] = dq_out.astype(BF16)
        dk_ref[...] = dka_ref[...].astype(BF16)
        dv_ref[...] = dva_ref[...].astype(BF16)

    seq = lambda c0: BS((S, LANES), lambda b, p: (b, c0 + p))
    pair = BS((1, 2, S, LANES), lambda b, p: (b, p, 0, 0))
    rowblk = BS((1, nb, 8, LANES), lambda b, p: (b, 0, 0, 0))
    return pl.pallas_call(
        body, name="fox_bwd", grid=(Bl, 3),
        in_specs=[seq(qc), seq(kc), seq(vc), seq(0), BS((1, S, LANES), lambda b, p: (p, b, 0)), pair, pair],
        out_specs=[seq(0), seq(0), seq(0), rowblk],
        out_shape=[jax.ShapeDtypeStruct((T, B_W), BF16)] * 3 + [jax.ShapeDtypeStruct((Bl, nb, 8, LANES), F32)],
        scratch_shapes=[pltpu.VMEM((2, S, LANES), BF16), pltpu.VMEM((S, LANES), F32), pltpu.VMEM((S, LANES), F32)],
        compiler_params=_cp(("arbitrary", "arbitrary")),
    )(proj, proj, proj, dyb, lse, bq, bk)


def _gate_bwd(dc_row, fl_row):
    Bl, nb, _, _ = dc_row.shape

    def body(dc_ref, fl_ref, o_ref):
        lane = _iota((8, LANES), 1)

        carry = jnp.zeros((8, 1), F32)
        for j in reversed(range(nb)):
            r = -dc_ref[0, j]
            for k in (1, 2, 4, 8, 16, 32, 64):
                r = r + jnp.where(lane < LANES - k, pltpu.roll(r, LANES - k, 1), 0.0)
            total = jnp.sum(jnp.where(lane == 0, r, 0.0), axis=1, keepdims=True)
            dfl = (r + carry) * _sigmoid(-fl_ref[0, j])
            carry = carry + total
            o_ref[0, j * LANES:(j + 1) * LANES, :] = jnp.concatenate(
                [dfl, jnp.zeros((LANES - 8, LANES), F32)], axis=0).T

    rowblk = BS((1, nb, 8, LANES), lambda b: (b, 0, 0, 0))
    return pl.pallas_call(
        body, name="gate_bwd", grid=(Bl,),
        in_specs=[rowblk, rowblk],
        out_specs=BS((1, nb * LANES, LANES), lambda b: (b, 0, 0)),
        out_shape=jax.ShapeDtypeStruct((Bl, nb * LANES, LANES), F32),
        compiler_params=_cp(("arbitrary",)),
    )(dc_row, fl_row)


def _inproj_bwd(pieces, x2d, dx1, g_pre, w_in_p, tm):
    T, D = x2d.shape
    ns, _, dsh = w_in_p.shape

    def body(*refs):
        piece_refs = refs[:7]
        x_ref, dx1_ref, g_ref, w_ref, gx_ref, dg_ref, dbf_ref, dp_ref = refs[7:]
        first = pl.program_id(0) == 0
        _put_dproj(dp_ref, piece_refs)
        dh = jnp.concatenate([_dot(dp_ref[...], w_ref[s]) for s in range(ns)], axis=1)
        dxa, dg = _rms_bwd(x_ref[...], g_ref[...], dh)
        gx_ref[...] = dx1_ref[...] + dxa
        _acc(dg_ref, dg, first)
        _acc(dbf_ref, _colsum(piece_refs[5][...]), first)

    row = lambda w: BS((tm, w), lambda i: (i, 0))
    return pl.pallas_call(
        body, name="inproj_bwd", grid=(T // tm,),
        in_specs=[row(w) for _, w in DPROJ_PIECES] + [row(D), row(D), BS((1, D), lambda i: (0, 0)),
                                                      BS((ns, P_COLS, dsh), lambda i: (0, 0, 0))],
        out_specs=[row(D), BS((1, D), lambda i: (0, 0)), BS((1, LANES), lambda i: (0, 0))],
        out_shape=[jax.ShapeDtypeStruct((T, D), F32), jax.ShapeDtypeStruct((1, D), F32),
                   jax.ShapeDtypeStruct((1, LANES), F32)],
        scratch_shapes=[pltpu.VMEM((tm, P_COLS), BF16)],
        compiler_params=_cp(("arbitrary",)),
    )(*pieces, x2d, dx1, g_pre, w_in_p)


def _local_step(x, mem, target, W, P, reduce=None):
    Bl, S, D = x.shape
    T = Bl * S
    tm = min(512, T)
    x2d = x.reshape(T, D)
    t2d = target.reshape(T, D)
    vec = lambda a: a.reshape(1, -1)
    bf_row = jnp.pad(P["b_f"].reshape(1, -1), ((0, 0), (0, LANES - N_FOX_HEADS)))
    tril = jnp.tril(jnp.ones((CHUNK, CHUNK), bool))
    ws_tril = jnp.where(tril[None], P["w_s"][0], 0.0).astype(BF16)
    bs_full = jnp.repeat(P["b_s"][0].T, HEAD, axis=1)
    g_pre, g_sgu = vec(P["g_pre_mix"]), vec(P["g_sgu"])
    ga, gb, gm = vec(P["g_out_a"]), vec(P["g_out_b"]), vec(P["g_out_m"])
    g_mem, g_post, g_pre2, g_post2 = vec(P["g_mem"]), vec(P["g_post_mix"]), vec(P["g_pre_ffn"]), vec(P["g_post_ffn"])

    h, proj, flog = _inproj_fwd(x2d, g_pre, W["w_in"], tm)
    bq, bk, fl_row = _gate_fwd(flog.reshape(Bl, S, LANES), bf_row)
    ya = _sgu_fwd(proj, g_sgu, ws_tril, bs_full, tm)
    yb, lse = _fox_fwd(proj, bq, bk, Bl, S)
    memn, kv = _memkv_fwd(mem, g_mem, W["w_mem_kv"])
    ym = _memattn_fwd(proj, kv, Bl, S, min(2048, S))
    y, o, x1, h2 = _outproj_fwd(ya, yb, ym, x2d, ga, gb, gm, g_post, g_pre2, W["w_out"], tm)
    gs, us, dff, dx2, dg_post2, loss = _ffn_fwd(h2, x1, t2d, W["w_gate"], W["w_up"], W["w_down"], g_post2, tm)

    dh2, d_w_gate, d_w_up, d_w_down = _ffn_bwd(dff, h2, gs, us, W["w_gate"], W["w_up"], W["w_down"], min(1024, T))
    ffn = [d_w_gate, d_w_up, d_w_down]
    if reduce is not None:
        pending, _ = reduce.begin("ffn", ffn)
    dx1, do, dya, dyb, dym, dga, dgb, dgm, dg_post, dg_pre2 = _outproj_bwd(
        dh2, x1, dx2, o, ya, yb, ym, ga, gb, gm, g_post, g_pre2, W["w_out"], tm)
    if reduce is not None:
        ffn, (do, dya, dyb, dym) = reduce.finish("ffn", pending, (do, dya, dyb, dym))
    d_w_out = _mm_tn(y, do, "dw_out", 1024)
    dzu, dzv, dws, dbs_cols, dg_sgu = _sgu_bwd(proj, dya, g_sgu, ws_tril, bs_full, tm)
    dqm, dkm, dvm = _memattn_bwd(proj, kv, dym, Bl, S, min(2048, S))
    d_w_kv, dg_mem = _memkv_bwd(dkm, dvm, memn, mem, g_mem, W["w_mem_kv"])
    mid = [d_w_kv, d_w_out]
    dq, dk, dv, dc_row = _fox_bwd(proj, dyb, lse, bq, bk, Bl, S)
    if reduce is not None:
        done = reduce.apply(BIG[3:], ffn)
        pending, after = reduce.begin("mid", mid, (dc_row,) + done)
        dc_row = after[0]
    dfl = _gate_bwd(dc_row, fl_row).reshape(T, LANES)
    pieces = (dzu, dzv, dq, dk, dv, dfl, dqm)
    grad_x, dg_pre, dbf = _inproj_bwd(pieces, x2d, dx1, g_pre, W["w_in"], tm)
    if reduce is not None:
        mid, (dfl,) = reduce.finish("mid", pending, (dfl,))
        pieces = (dzu, dzv, dq, dk, dv, dfl, dqm)
    d_w_in = _dw_in(pieces, h, W["w_in"].shape[0], 1024)
    if reduce is None:
        big = dict(zip(BIG, [d_w_in] + mid + ffn))
    else:
        done = reduce.apply(BIG[1:3], mid)
        big = {"w_in": reduce.begin("in", [d_w_in], done)[0]}
    small = {"g_pre_mix": dg_pre, "b_f": dbf[:, :N_FOX_HEADS], "g_sgu": dg_sgu, "w_s": dws, "b_s": dbs_cols[:, :N_FOX_HEADS].T,
             "g_out_a": dga, "g_out_b": dgb, "g_out_m": dgm, "g_mem": dg_mem, "g_post_mix": dg_post,
             "g_pre_ffn": dg_pre2, "g_post_ffn": dg_post2, "loss": loss[:, :1]}
    return grad_x.reshape(Bl, S, D), big, small


def _place():
    return lax.axis_index("x"), lax.axis_index("y"), lax.axis_index("c")


def _exchange_on_sequencer(srcs, own_full, name, collective_id):
    n = len(srcs)

    def body(*refs):
        src, dst = refs[:n], refs[n:2 * n]
        lsem, isend, irecv, dsend, drecv = refs[2 * n:]
        x, y, c = _place()
        oc = 1 - c
        s_me = 2 * x + y
        sib = (x, y, oc)
        chips = [(1 - x, y), (x, 1 - y), (1 - x, 1 - y)]
        barrier = pltpu.get_barrier_semaphore()
        for dev in [(cx, cy, c) for cx, cy in chips] + [sib]:
            pl.semaphore_signal(barrier, inc=1, device_id=dev, device_id_type=MESH)
        pl.semaphore_wait(barrier, 4)

        def remote(a, b, ssem, rsem, dev):
            return pltpu.make_async_remote_copy(src_ref=a, dst_ref=b, send_sem=ssem, recv_sem=rsem,
                                                device_id=dev, device_id_type=MESH)

        sends, local = [], []
        for w in range(n):
            for j, (cx, cy) in enumerate(chips):
                half = src[w].at[c] if own_full else src[w].at[2 * cx + cy]
                cp = remote(half, dst[w].at[s_me, c], isend.at[w, j], irecv.at[w, j], (cx, cy, c))
                cp.start()
                sends.append(cp)
            if own_full:
                cp = remote(src[w], dst[w].at[s_me], dsend.at[w, 3], drecv.at[w, 3], sib)
            else:
                cp = remote(src[w].at[s_me], dst[w].at[s_me, c], dsend.at[w, 3], drecv.at[w, 3], sib)
                loc = pltpu.make_async_copy(src[w].at[s_me], dst[w].at[s_me, c], lsem.at[w])
                loc.start()
                local.append(loc)
            cp.start()
            sends.append(cp)
        for w in range(n):
            for j, (cx, cy) in enumerate(chips):
                landed = dst[w].at[2 * cx + cy, c]
                remote(landed, landed, isend.at[w, j], irecv.at[w, j], (cx, cy, c)).wait_recv()
                cp = remote(landed, landed, dsend.at[w, j], drecv.at[w, j], sib)
                cp.start()
                sends.append(cp)
        for w in range(n):
            for j, (cx, cy) in enumerate(chips):
                landed = dst[w].at[2 * cx + cy, oc]
                remote(landed, landed, dsend.at[w, j], drecv.at[w, j], sib).wait_recv()
            landed = dst[w].at[s_me] if own_full else dst[w].at[s_me, oc]
            remote(landed, landed, dsend.at[w, 3], drecv.at[w, 3], sib).wait_recv()
        for cp in sends:
            cp.wait_send()
        for loc in local:
            loc.wait()

    return pl.kernel(
        body, out_type=[jax.ShapeDtypeStruct((4, 2) + s.shape[1:], s.dtype) for s in srcs],
        mesh=plsc.ScalarSubcoreMesh(axis_name="sequencer", num_cores=1), name=name,
        scratch_types=[pltpu.SemaphoreType.DMA((n,)), pltpu.SemaphoreType.DMA((n, 3)), pltpu.SemaphoreType.DMA((n, 3)),
                       pltpu.SemaphoreType.DMA((n, 4)), pltpu.SemaphoreType.DMA((n, 4))],
        compiler_params=pltpu.CompilerParams(collective_id=collective_id),
    )(*srcs)


def _sibling_swap(grads, name, collective_id):
    n = len(grads)

    def body(*refs):
        g, theirs = refs[:n], refs[n:2 * n]
        ssem, rsem = refs[2 * n:]
        x, y, c = _place()
        sib = (x, y, 1 - c)
        barrier = pltpu.get_barrier_semaphore()
        pl.semaphore_signal(barrier, inc=1, device_id=sib, device_id_type=MESH)
        pl.semaphore_wait(barrier, 1)
        cps = []
        for w in range(n):
            cp = pltpu.make_async_remote_copy(src_ref=g[w].at[:, 1 - c], dst_ref=theirs[w], send_sem=ssem.at[w],
                                              recv_sem=rsem.at[w], device_id=sib, device_id_type=MESH)
            cp.start()
            cps.append(cp)
        for cp in cps:
            cp.wait()

    return pl.kernel(
        body, out_type=[jax.ShapeDtypeStruct((4,) + g.shape[2:], g.dtype) for g in grads],
        mesh=plsc.ScalarSubcoreMesh(axis_name="sequencer", num_cores=1), name=name,
        scratch_types=[pltpu.SemaphoreType.DMA((n,)), pltpu.SemaphoreType.DMA((n,))],
        compiler_params=pltpu.CompilerParams(collective_id=collective_id),
    )(*grads)


def _add_pair(core, g, theirs, name):
    _, _, hr, C = g.shape

    def body(core_ref, g_ref, t_ref, o_ref):
        o_ref[0] = (g_ref[0, 0].astype(F32) + t_ref[0].astype(F32)).astype(BF16)

    blk = BS((1, hr, C), lambda s, core_ref: (s, 0, 0))
    return pl.pallas_call(
        body, name=name,
        grid_spec=pltpu.PrefetchScalarGridSpec(
            num_scalar_prefetch=1, grid=(4,),
            in_specs=[BS((1, 1, hr, C), lambda s, core_ref: (s, core_ref[0], 0, 0)), blk], out_specs=blk),
        out_shape=jax.ShapeDtypeStruct(theirs.shape, BF16), compiler_params=_cp(("arbitrary",)))(core, g, theirs)


def _sum_chips(r, name):
    _, _, hr, C = r.shape

    def body(r_ref, o_ref):
        o_ref[...] = ((r_ref[0, 0].astype(F32) + r_ref[1, 0].astype(F32)) + r_ref[2, 0].astype(F32)) + r_ref[3, 0].astype(F32)

    return pl.pallas_call(body, name=name, grid=(2,), in_specs=[BS((4, 1, hr, C), lambda h: (0, h, 0, 0))],
                          out_specs=BS((hr, C), lambda h: (h, 0)), out_shape=jax.ShapeDtypeStruct((2 * hr, C), F32),
                          compiler_params=_cp(("arbitrary",)))(r)


class _Reducer:
    IDS = {"ffn": (4, 5), "mid": (6, 7), "in": (8, 9)}

    def __init__(self, core, apply):
        self.core = core
        self.apply = apply

    def begin(self, tag, grads, after=()):
        grads, after = lax.optimization_barrier((list(grads), after))
        g4 = [g.reshape(4, 2, -1, g.shape[-1]) for g in grads]
        return (g4, _sibling_swap(g4, "swap_" + tag, self.IDS[tag][0])), after

    def finish(self, tag, pending, hold):
        g4, theirs = pending
        sums = [_add_pair(self.core, g, t, "chip_sum_%s_%d" % (tag, k)) for k, (g, t) in enumerate(zip(g4, theirs))]
        sums, hold = lax.optimization_barrier((sums, hold))
        return _exchange_on_sequencer(sums, False, "scatter_" + tag, self.IDS[tag][1]), hold


def _small_allreduce(part):
    R = part.shape[0]
    rs = R // 8
    masks = [(mx, my, mc) for mx in (0, 1) for my in (0, 1) for mc in (0, 1)][1:]

    def body(p_ref, o_ref, buf_ref, s1, r1, s2, r2):
        x, y, c = _place()
        d = 4 * x + 2 * y + c
        mine = pl.ds(pl.multiple_of(d * rs, 8), rs)
        peers = [((x + mx) % 2, (y + my) % 2, (c + mc) % 2) for mx, my, mc in masks]
        first, second = [], []
        for k, (px, py, pc) in enumerate(peers):
            theirs = pl.ds(pl.multiple_of((4 * px + 2 * py + pc) * rs, 8), rs)
            cp = pltpu.make_async_remote_copy(src_ref=p_ref.at[theirs, :], dst_ref=buf_ref.at[d], send_sem=s1.at[k],
                                              recv_sem=r1.at[k], device_id=(px, py, pc), device_id_type=MESH)
            cp.start()
            first.append(cp)
        buf_ref[d] = p_ref[mine, :]
        for k, (px, py, pc) in enumerate(peers):
            slot = buf_ref.at[4 * px + 2 * py + pc]
            pltpu.make_async_remote_copy(src_ref=slot, dst_ref=slot, send_sem=s1.at[k], recv_sem=r1.at[k],
                                         device_id=(px, py, pc), device_id_type=MESH).wait_recv()
        total = buf_ref[0]
        for k in range(1, 8):
            total = total + buf_ref[k]
        o_ref[mine, :] = total
        for k, (px, py, pc) in enumerate(peers):
            cp = pltpu.make_async_remote_copy(src_ref=o_ref.at[mine, :], dst_ref=o_ref.at[mine, :], send_sem=s2.at[k],
                                              recv_sem=r2.at[k], device_id=(px, py, pc), device_id_type=MESH)
            cp.start()
            second.append(cp)
        for k, (px, py, pc) in enumerate(peers):
            rows = o_ref.at[pl.ds(pl.multiple_of((4 * px + 2 * py + pc) * rs, 8), rs), :]
            pltpu.make_async_remote_copy(src_ref=rows, dst_ref=rows, send_sem=s2.at[k], recv_sem=r2.at[k],
                                         device_id=(px, py, pc), device_id_type=MESH).wait_recv()
        for cp in first + second:
            cp.wait_send()

    vm = pl.BlockSpec(memory_space=pltpu.VMEM)
    return pl.pallas_call(
        body, name="small_allreduce", in_specs=[vm], out_specs=vm, out_shape=jax.ShapeDtypeStruct(part.shape, F32),
        scratch_shapes=[pltpu.VMEM((8, rs, LANES), F32)] + [pltpu.SemaphoreType.DMA((7,))] * 4,
    )(part)


def _adamw(w, g, m, v, name):
    R, C = w.shape
    summed = g.ndim == 4
    if summed:
        tr = R // 2
    else:
        tr = R if R * C * 4 <= (1 << 21) else R // 2
        if tr % 8:
            tr = R
    c1 = 1.0 / (1.0 - ADAM_B1 ** ADAM_STEP)
    c2 = 1.0 / (1.0 - ADAM_B2 ** ADAM_STEP)

    def body(w_ref, g_ref, m_ref, v_ref, *outs):
        if summed:
            g_ = ((g_ref[0, 0].astype(F32) + g_ref[1, 0].astype(F32)) + g_ref[2, 0].astype(F32)) + g_ref[3, 0].astype(F32)
            outs[0][...] = g_
        else:
            g_ = g_ref[...]
        d_ref, mo_ref, vo_ref = outs[-3:]
        m_ = ADAM_B1 * m_ref[...] + (1.0 - ADAM_B1) * g_
        v_ = ADAM_B2 * v_ref[...] + (1.0 - ADAM_B2) * (g_ * g_)
        mo_ref[...] = m_
        vo_ref[...] = v_
        d_ref[...] = -ADAM_LR * ((m_ * c1) / (jnp.sqrt(v_ * c2) + ADAM_EPS) + ADAM_WD * w_ref[...])

    blk = BS((tr, C), lambda i: (i, 0))
    g_blk = BS((4, 1, tr, C), lambda i: (0, i, 0, 0)) if summed else blk
    nout = 4 if summed else 3
    return pl.pallas_call(body, name=name, grid=(R // tr,), in_specs=[blk, g_blk, blk, blk], out_specs=[blk] * nout,
                          out_shape=[jax.ShapeDtypeStruct((R, C), F32)] * nout,
                          compiler_params=_cp(("arbitrary",)))(w, g, m, v)


def _adamw_from_transposed(w, g_t, m, v, name):
    R, C = w.shape
    tc = 256
    c1 = 1.0 / (1.0 - ADAM_B1 ** ADAM_STEP)
    c2 = 1.0 / (1.0 - ADAM_B2 ** ADAM_STEP)

    def body(w_ref, g_ref, m_ref, v_ref, go_ref, d_ref, mo_ref, vo_ref):
        g_ = g_ref[...].T
        go_ref[...] = g_
        m_ = ADAM_B1 * m_ref[...] + (1.0 - ADAM_B1) * g_
        v_ = ADAM_B2 * v_ref[...] + (1.0 - ADAM_B2) * (g_ * g_)
        mo_ref[...] = m_
        vo_ref[...] = v_
        d_ref[...] = -ADAM_LR * ((m_ * c1) / (jnp.sqrt(v_ * c2) + ADAM_EPS) + ADAM_WD * w_ref[...])

    blk = BS((R, tc), lambda i: (0, i))
    return pl.pallas_call(body, name=name, grid=(pl.cdiv(C, tc),), in_specs=[blk, BS((tc, R), lambda i: (i, 0)), blk, blk],
                          out_specs=[blk] * 4, out_shape=[jax.ShapeDtypeStruct((R, C), F32)] * 4,
                          compiler_params=_cp(("arbitrary",)))(w, g_t, m, v)


SMALL = ("g_pre_mix", "b_f", "g_sgu", "w_s", "b_s", "g_out_a", "g_out_b", "g_out_m", "g_mem", "g_post_mix",
         "g_pre_ffn", "g_post_ffn")
BIG = ("w_in", "w_mem_kv", "w_out", "w_gate", "w_up", "w_down")
TRANSPOSED = ("w_in", "w_gate", "w_up")
WEIGHTS = ("g_pre_mix", "w_in", "b_f", "g_sgu", "w_s", "b_s", "g_out_a", "g_out_b", "g_out_m", "g_mem", "w_mem_kv",
           "w_out", "g_post_mix", "g_pre_ffn", "w_gate", "w_up", "w_down", "g_post_ffn")


def _rows_of(n):
    return -(-n // (8 * LANES)) * 8


def _pack(parts):
    tiles = []
    for a in parts:
        flat = a.reshape(-1).astype(F32)
        rows = _rows_of(flat.shape[0])
        tiles.append(jnp.pad(flat, (0, rows * LANES - flat.shape[0])).reshape(rows, LANES))
    total = sum(t.shape[0] for t in tiles)
    pad = -total % 64
    if pad:
        tiles.append(jnp.zeros((pad, LANES), F32))
    return jnp.concatenate(tiles, axis=0)


def _unpack(packed, shapes):
    out, r = [], 0
    for shp in shapes:
        n = 1
        for s in shp:
            n *= s
        rows = _rows_of(n)
        out.append(packed[r:r + rows].reshape(-1)[:n].reshape(shp))
        r += rows
    return out


def kernel(x, mem, g_pre_mix, w_in, b_f, g_sgu, w_s, b_s, g_out_a, g_out_b, g_out_m, g_mem, w_mem_kv, w_out, g_post_mix, g_pre_ffn, w_gate, w_up, w_down, g_post_ffn, loss_target, m_g_pre_mix, m_w_in, m_b_f, m_g_sgu, m_w_s, m_b_s, m_g_out_a, m_g_out_b, m_g_out_m, m_g_mem, m_w_mem_kv, m_w_out, m_g_post_mix, m_g_pre_ffn, m_w_gate, m_w_up, m_w_down, m_g_post_ffn, v_g_pre_mix, v_w_in, v_b_f, v_g_sgu, v_w_s, v_b_s, v_g_out_a, v_g_out_b, v_g_out_m, v_g_mem, v_w_mem_kv, v_w_out, v_g_post_mix, v_g_pre_ffn, v_w_gate, v_w_up, v_w_down, v_g_post_ffn):
    Wt = dict(g_pre_mix=g_pre_mix, w_in=w_in, b_f=b_f, g_sgu=g_sgu, w_s=w_s, b_s=b_s, g_out_a=g_out_a, g_out_b=g_out_b,
              g_out_m=g_out_m, g_mem=g_mem, w_mem_kv=w_mem_kv, w_out=w_out, g_post_mix=g_post_mix, g_pre_ffn=g_pre_ffn,
              w_gate=w_gate, w_up=w_up, w_down=w_down, g_post_ffn=g_post_ffn)
    Mo = dict(g_pre_mix=m_g_pre_mix, w_in=m_w_in, b_f=m_b_f, g_sgu=m_g_sgu, w_s=m_w_s, b_s=m_b_s, g_out_a=m_g_out_a,
              g_out_b=m_g_out_b, g_out_m=m_g_out_m, g_mem=m_g_mem, w_mem_kv=m_w_mem_kv, w_out=m_w_out,
              g_post_mix=m_g_post_mix, g_pre_ffn=m_g_pre_ffn, w_gate=m_w_gate, w_up=m_w_up, w_down=m_w_down,
              g_post_ffn=m_g_post_ffn)
    Vo = dict(g_pre_mix=v_g_pre_mix, w_in=v_w_in, b_f=v_b_f, g_sgu=v_g_sgu, w_s=v_w_s, b_s=v_b_s, g_out_a=v_g_out_a,
              g_out_b=v_g_out_b, g_out_m=v_g_out_m, g_mem=v_g_mem, w_mem_kv=v_w_mem_kv, w_out=v_w_out,
              g_post_mix=v_g_post_mix, g_pre_ffn=v_g_pre_ffn, w_gate=v_w_gate, w_up=v_w_up, w_down=v_w_down,
              g_post_ffn=v_g_post_ffn)

    gap = P_COLS - IN_COLS

    def to_kernel(n, w):
        if n in TRANSPOSED:
            w = w.T
        if n == "w_in":
            w = jnp.pad(w[:F_END], ((0, P_COLS - F_END), (0, 0))) + jnp.pad(w[F_END:], ((F_END + gap, 0), (0, 0)))
        return w

    def ungroup(g):
        return jnp.pad(g[:F_END], ((0, IN_COLS - F_END), (0, 0))) + jnp.pad(g[F_END + gap:], ((F_END, 0), (0, 0)))

    shards = {n: to_kernel(n, Wt[n][0]) for n in BIG}
    srcs = [shards[n].astype(BF16).reshape(2, shards[n].shape[0] // 2, shards[n].shape[1]) for n in BIG]
    fulls = (_exchange_on_sequencer(srcs[:1], True, "gather_w_in", 1)
             + _exchange_on_sequencer(srcs[1:3], True, "gather_kv_out", 2)
             + _exchange_on_sequencer(srcs[3:], True, "gather_ffn", 3))
    W = {}
    for n, f in zip(BIG, fulls):
        _, _, hr, C = f.shape
        W[n] = f.reshape(8 * hr, C) if n in ("w_mem_kv", "w_out") else f.reshape(4, 2 * hr, C)

    P = {n: Wt[n] for n in SMALL}
    grads, deltas, new_m, new_v = {}, {}, {}, {}

    def apply(names, landed):
        for n, r in zip(names, landed):
            if n == "w_in":
                g_t = ungroup(_sum_chips(r, "sum_chips_" + n))
                g, d, m1, v1 = _adamw_from_transposed(Wt[n][0], g_t, Mo[n][0], Vo[n][0], "adamw_" + n)
            elif n in TRANSPOSED:
                g, d, m1, v1 = [a.T for a in _adamw(Wt[n][0].T, r, Mo[n][0].T, Vo[n][0].T, "adamw_" + n)]
            else:
                g, d, m1, v1 = _adamw(Wt[n][0], r, Mo[n][0], Vo[n][0], "adamw_" + n)
            grads[n], deltas[n], new_m[n], new_v[n] = g[None], d[None], m1[None], v1[None]
        return tuple(deltas[n] for n in names)

    core = lax.axis_index("c").astype(jnp.int32).reshape(1)
    reducer = _Reducer(core, apply)
    grad_x, pending, small = _local_step(x, mem, loss_target, W, P, reducer)

    total = _small_allreduce(_pack([small[n] for n in SMALL] + [small["loss"]]))
    landed, (total,) = reducer.finish("in", pending["w_in"], (total,))
    apply(BIG[:1], landed)

    slot = [jnp.zeros((1, 1), F32)]
    shapes = [Wt[n].shape for n in SMALL] + [(1, 1)]
    d, m1, v1 = _adamw(_pack([Wt[n] for n in SMALL] + slot), total, _pack([Mo[n] for n in SMALL] + slot),
                       _pack([Vo[n] for n in SMALL] + slot), "adamw_small")
    g_s, d_s, m_s, v_s = _unpack(total, shapes), _unpack(d, shapes), _unpack(m1, shapes), _unpack(v1, shapes)
    for k, n in enumerate(SMALL):
        grads[n], deltas[n], new_m[n], new_v[n] = g_s[k], d_s[k], m_s[k], v_s[k]
    loss = g_s[-1][0, 0]

    return (loss, grad_x, *[grads[n] for n in WEIGHTS], *[deltas[n] for n in WEIGHTS],
            *[new_m[n] for n in WEIGHTS], *[new_v[n] for n in WEIGHTS])
```

```python
import functools

import jax
import jax.numpy as jnp
from jax import lax
from jax.experimental import pallas as pl
from jax.experimental.pallas import tpu as pltpu
from jax.experimental.pallas import tpu_sc as plsc

F32 = jnp.float32
BF16 = jnp.bfloat16
EPS = 1e-6
NEG = -1e30
HEAD = 64
A_W, B_W, M_W = 384, 384, 256
N_FOX_HEADS = 6
CHUNK = 128
IN_COLS = 2 * A_W + 3 * B_W + N_FOX_HEADS + M_W
P_MAIN = 2 * A_W + 3 * B_W + M_W
P_COLS = P_MAIN + 128
F_END = 2 * A_W + 3 * B_W + N_FOX_HEADS
LANES = 128
Q_BLK, K_BLK = 512, 128
ROW_SPLIT = 4
ADAM_LR, ADAM_B1, ADAM_B2, ADAM_EPS, ADAM_WD, ADAM_STEP = 0.001, 0.9, 0.999, 1e-08, 0.01, 10
VMEM_LIMIT = 56 * 1024 * 1024
MESH = pl.DeviceIdType.MESH
ANY = pl.BlockSpec(memory_space=pl.ANY)
BS = pl.BlockSpec


def _cp(sem=None):
    return pltpu.CompilerParams(dimension_semantics=sem, vmem_limit_bytes=VMEM_LIMIT)


def _iota(shape, dim):
    return lax.broadcasted_iota(jnp.int32, shape, dim)


def _dot(a, b):
    return jnp.dot(a.astype(BF16), b.astype(BF16), preferred_element_type=F32)


def _dot_nt(a, b):
    return lax.dot_general(a.astype(BF16), b.astype(BF16), (((1,), (1,)), ((), ())), preferred_element_type=F32)


def _dot_tn(a, b):
    return lax.dot_general(a.astype(BF16), b.astype(BF16), (((0,), (0,)), ((), ())), preferred_element_type=F32)


def _rms(x, g):
    return x * lax.rsqrt(jnp.mean(x * x, axis=-1, keepdims=True) + EPS) * g


def _rms_bwd(x, g, dy):
    r = lax.rsqrt(jnp.mean(x * x, axis=-1, keepdims=True) + EPS)
    xr = x * r
    gd = dy * g
    m = jnp.mean(gd * xr, axis=-1, keepdims=True)
    return (gd - xr * m) * r, _colsum(dy * xr)


def _gelu(x):
    return 0.5 * x * (1.0 + jnp.tanh(0.7978845608028654 * (x + 0.044715 * (x * x * x))))


def _sigmoid(x):
    return 1.0 / (1.0 + jnp.exp(-x))


def _silu_mul(g, u):
    return g * _sigmoid(g) * u


def _logsig(x):
    return jnp.minimum(x, 0.0) - jnp.log(1.0 + jnp.exp(-jnp.abs(x)))


def _colsum(x):
    return jnp.sum(x, axis=0, keepdims=True)


def _acc(ref, val, first):
    @pl.when(first)
    def _():
        ref[...] = val

    @pl.when(jnp.logical_not(first))
    def _():
        ref[...] += val


def _inproj_fwd(x2d, g_pre, w_in_p, tm):
    T, D = x2d.shape
    CH = 768
    nchunk = P_COLS // CH
    ns, _, dsh = w_in_p.shape

    def body(x_ref, g_ref, w_ref, h_ref, proj_ref, fl_ref):
        h = _rms(x_ref[...], g_ref[...]).astype(BF16)
        h_ref[...] = h
        for n in range(nchunk):
            rows = slice(n * CH, (n + 1) * CH)
            r = _dot_nt(h[:, 0:dsh], w_ref[0, rows, :])
            for s in range(1, ns):
                r = r + _dot_nt(h[:, s * dsh:(s + 1) * dsh], w_ref[s, rows, :])
            if n < nchunk - 1:
                proj_ref[:, rows] = r.astype(BF16)
            else:
                fg = 1920 - n * CH
                proj_ref[:, n * CH:1920] = r[:, :fg].astype(BF16)
                fl_ref[...] = r[:, fg:fg + LANES]
                proj_ref[:, 1920:P_MAIN] = r[:, fg + LANES:].astype(BF16)

    return pl.pallas_call(
        body, name="inproj_fwd", grid=(T // tm,),
        in_specs=[BS((tm, D), lambda i: (i, 0)), BS((1, D), lambda i: (0, 0)),
                  BS((ns, P_COLS, dsh), lambda i: (0, 0, 0))],
        out_specs=[BS((tm, D), lambda i: (i, 0)), BS((tm, P_MAIN), lambda i: (i, 0)), BS((tm, LANES), lambda i: (i, 0))],
        out_shape=[jax.ShapeDtypeStruct((T, D), BF16), jax.ShapeDtypeStruct((T, P_MAIN), BF16),
                   jax.ShapeDtypeStruct((T, LANES), F32)],
        compiler_params=_cp(("arbitrary",)),
    )(x2d, g_pre, w_in_p)


def _gate_fwd(flog3, bf_row):
    Bl, S, _ = flog3.shape
    nb = S // LANES

    def body(f_ref, b_ref, bq_ref, bk_ref, fr_ref):
        row = _iota((LANES, LANES), 0)
        lane = _iota((LANES, LANES), 1)
        one = jnp.ones((LANES, LANES), BF16)
        zero = jnp.zeros((LANES, LANES), BF16)

        carry = jnp.zeros((1, LANES), F32)
        for j in range(nb):
            r0 = j * LANES
            fl = f_ref[0, pl.ds(r0, LANES), :] + b_ref[...]
            fr_ref[0, j] = fl.T[0:8, :]
            c = _logsig(fl)
            for k in (1, 2, 4, 8, 16, 32, 64):
                c = c + jnp.where(row >= k, pltpu.roll(c, k, 0), 0.0)
            total = _colsum(jnp.where(row == LANES - 1, c, 0.0))
            c = c + carry
            carry = carry + total
            for h in range(N_FOX_HEADS):
                col = jnp.sum(jnp.where(lane == h, c, 0.0), axis=1, keepdims=True)
                hi = col.astype(BF16)
                rest = col - hi.astype(F32)
                mid = rest.astype(BF16)
                lo = (rest - mid.astype(F32)).astype(BF16)
                base = _bias_lane(h)
                bq = jnp.where(lane == base, hi, jnp.where(lane == base + 1, mid, jnp.where(lane == base + 2, lo, zero)))
                bq = jnp.where((lane >= base + 3) & (lane < base + 6), one, bq)
                bk = jnp.where(lane == base + 3, -hi, jnp.where(lane == base + 4, -mid, jnp.where(lane == base + 5, -lo, zero)))
                bk = jnp.where((lane >= base) & (lane < base + 3), one, bk)
                bq_ref[0, h, pl.ds(r0, LANES), :] = bq
                bk_ref[0, h, pl.ds(r0, LANES), :] = bk

    slab = BS((1, N_FOX_HEADS, S, LANES), lambda b: (b, 0, 0, 0))
    return pl.pallas_call(
        body, name="gate_fwd", grid=(Bl,),
        in_specs=[BS((1, S, LANES), lambda b: (b, 0, 0)), BS((1, LANES), lambda b: (0, 0))],
        out_specs=[slab, slab, BS((1, nb, 8, LANES), lambda b: (b, 0, 0, 0))],
        out_shape=[jax.ShapeDtypeStruct((Bl, N_FOX_HEADS, S, LANES), BF16),
                   jax.ShapeDtypeStruct((Bl, N_FOX_HEADS, S, LANES), BF16),
                   jax.ShapeDtypeStruct((Bl, nb, 8, LANES), F32)],
        compiler_params=_cp(("arbitrary",)),
    )(flog3, bf_row)


def _bias_lane(h):
    return HEAD if h % 2 == 0 else 0


def _sgu_pre(zu, zv, g_sgu):
    return _gelu(zu), _rms(_gelu(zv), g_sgu)


def _sgu_fwd(proj, g_sgu, ws_tril, bs_full, tm):
    T = proj.shape[0]
    nch = tm // CHUNK

    def body(zu_ref, zv_ref, g_ref, ws_ref, b_ref, ya_ref):
        lane = _iota((CHUNK, LANES), 1)
        u, vn = _sgu_pre(zu_ref[...].astype(F32), zv_ref[...].astype(F32), g_ref[...])
        vn = vn.astype(BF16)
        for c in range(nch):
            rs = slice(c * CHUNK, (c + 1) * CHUNK)
            for j in range(3):
                cs = slice(j * LANES, (j + 1) * LANES)
                vp = vn[rs, cs]
                z = jnp.where(lane < HEAD, _dot(ws_ref[2 * j], vp), _dot(ws_ref[2 * j + 1], vp)) + b_ref[:, cs]
                ya_ref[rs, cs] = (u[rs, cs] * z).astype(BF16)

    return pl.pallas_call(
        body, name="sgu_fwd", grid=(T // tm,),
        in_specs=[BS((tm, A_W), lambda i: (i, 0)), BS((tm, A_W), lambda i: (i, 1)), BS((1, A_W), lambda i: (0, 0)),
                  BS((6, CHUNK, CHUNK), lambda i: (0, 0, 0)), BS((CHUNK, A_W), lambda i: (0, 0))],
        out_specs=BS((tm, A_W), lambda i: (i, 0)),
        out_shape=jax.ShapeDtypeStruct((T, A_W), BF16),
        compiler_params=_cp(("arbitrary",)),
    )(proj, proj, g_sgu, ws_tril, bs_full)


def _fox_fwd(proj, bq, bk, Bl, S):
    T = Bl * S
    nq = S // Q_BLK
    qc, kc, vc = 768 // LANES, 1152 // LANES, 1536 // LANES

    def body(q_ref, k_ref, v_ref, bq_ref, bk_ref, o_ref, lse_ref, ka_ref, va_ref):
        lane_s = _iota((S, LANES), 1)
        lane = _iota((Q_BLK, LANES), 1)
        tri = _iota((Q_BLK, Q_BLK), 1) <= _iota((Q_BLK, Q_BLK), 0)
        k = k_ref[...]
        v = v_ref[...]
        for hh in range(2):
            data = (lane_s < HEAD) if hh == 0 else (lane_s >= HEAD)
            ka_ref[hh] = jnp.where(data, k, bk_ref[0, hh])
            va_ref[hh] = jnp.where(lane_s == _bias_lane(hh), jnp.ones_like(v), v)
        for i in range(nq):
            r0 = i * Q_BLK
            q = q_ref[r0:r0 + Q_BLK, :]
            o_out = jnp.zeros((Q_BLK, LANES), F32)
            lse_out = jnp.zeros((Q_BLK, LANES), F32)
            for hh in range(2):
                hmask = (lane < HEAD) if hh == 0 else (lane >= HEAD)
                qa = jnp.where(hmask, q * 0.125, bq_ref[0, hh, r0:r0 + Q_BLK, :])
                sd = jnp.where(tri, _dot_nt(qa, ka_ref[hh, r0:r0 + Q_BLK, :]), NEG)
                m = jnp.max(sd, axis=1, keepdims=True)
                if i:
                    sf = _dot_nt(qa, ka_ref[hh, 0:r0, :])
                    m = jnp.maximum(m, jnp.max(sf, axis=1, keepdims=True))
                acc = _dot(jnp.exp(sd - m), va_ref[hh, r0:r0 + Q_BLK, :])
                if i:
                    acc = acc + _dot(jnp.exp(sf - m), va_ref[hh, 0:r0, :])
                l = jnp.sum(jnp.where(lane == _bias_lane(hh), acc, 0.0), axis=1, keepdims=True)
                o_out = jnp.where(hmask, acc / l, o_out)
                lse_out = jnp.where(hmask, m + jnp.log(l), lse_out)
            o_ref[r0:r0 + Q_BLK, :] = o_out.astype(BF16)
            lse_ref[0, r0:r0 + Q_BLK, :] = lse_out

    seq = lambda c0: BS((S, LANES), lambda b, p: (b, c0 + p))
    pair = BS((1, 2, S, LANES), lambda b, p: (b, p, 0, 0))
    return pl.pallas_call(
        body, name="fox_fwd", grid=(Bl, 3),
        in_specs=[seq(qc), seq(kc), seq(vc), pair, pair],
        out_specs=[seq(0), BS((1, S, LANES), lambda b, p: (p, b, 0))],
        out_shape=[jax.ShapeDtypeStruct((T, B_W), BF16), jax.ShapeDtypeStruct((3, T, LANES), F32)],
        scratch_shapes=[pltpu.VMEM((2, S, LANES), BF16), pltpu.VMEM((2, S, LANES), BF16)],
        compiler_params=_cp(("arbitrary", "arbitrary")),
    )(proj, proj, proj, bq, bk)


def _memkv_fwd(mem, g_mem, w_kv):
    Bl, Mt, D = mem.shape

    def body(m_ref, g_ref, w_ref, mn_ref, kv_ref):
        mn = _rms(m_ref[0], g_ref[...]).astype(BF16)
        mn_ref[0] = mn
        kv_ref[0] = jnp.dot(mn, w_ref[...], preferred_element_type=F32).astype(BF16)

    return pl.pallas_call(
        body, name="memkv_fwd", grid=(Bl,),
        in_specs=[BS((1, Mt, D), lambda b: (b, 0, 0)), BS((1, D), lambda b: (0, 0)), BS((D, 2 * M_W), lambda b: (0, 0))],
        out_specs=[BS((1, Mt, D), lambda b: (b, 0, 0)), BS((1, Mt, 2 * M_W), lambda b: (b, 0, 0))],
        out_shape=[jax.ShapeDtypeStruct((Bl, Mt, D), BF16), jax.ShapeDtypeStruct((Bl, Mt, 2 * M_W), BF16)],
        compiler_params=_cp(("arbitrary",)),
    )(mem, g_mem, w_kv)


def _memattn_fwd(proj, kv, Bl, S, tq):
    T = Bl * S
    nq = S // tq
    Mt = kv.shape[1]
    qc = 1920 // LANES

    def body(q_ref, km_ref, vm_ref, o_ref):
        lane = _iota((tq, LANES), 1)
        q = q_ref[...]
        out = jnp.zeros((tq, LANES), F32)
        for hh in range(2):
            hmask = (lane < HEAD) if hh == 0 else (lane >= HEAD)
            qs = jnp.where(hmask, q, jnp.zeros_like(q)) * 0.125
            s = _dot_nt(qs, km_ref[0])
            pe = jnp.exp(s - jnp.max(s, axis=1, keepdims=True))
            pn = pe / jnp.sum(pe, axis=1, keepdims=True)
            out = jnp.where(hmask, _dot(pn, vm_ref[0]), out)
        o_ref[...] = out.astype(BF16)

    return pl.pallas_call(
        body, name="memattn_fwd", grid=(Bl, 2, nq),
        in_specs=[BS((tq, LANES), lambda b, p, i: (b * nq + i, qc + p)),
                  BS((1, Mt, LANES), lambda b, p, i: (b, 0, p)),
                  BS((1, Mt, LANES), lambda b, p, i: (b, 0, 2 + p))],
        out_specs=BS((tq, LANES), lambda b, p, i: (b * nq + i, p)),
        out_shape=jax.ShapeDtypeStruct((T, M_W), BF16),
        compiler_params=_cp(("arbitrary", "arbitrary", "arbitrary")),
    )(proj, kv, kv)


def _mix_norms(ya, yb, ym, ga, gb, gm):
    return _rms(ya, ga), _rms(yb, gb), _rms(ym, gm)


def _outproj_fwd(ya, yb, ym, x2d, ga, gb, gm, g_post, g_pre2, w_out, tm):
    T, D = x2d.shape

    def body(ya_ref, yb_ref, ym_ref, x_ref, ga_ref, gb_ref, gm_ref, gp_ref, g2_ref, w_ref,
             y_ref, o_ref, x1_ref, h2_ref):
        na, nb_, nm = _mix_norms(ya_ref[...].astype(F32), yb_ref[...].astype(F32), ym_ref[...].astype(F32),
                                 ga_ref[...], gb_ref[...], gm_ref[...])
        y_ref[:, 0:A_W] = na.astype(BF16)
        y_ref[:, A_W:A_W + B_W] = nb_.astype(BF16)
        y_ref[:, A_W + B_W:] = nm.astype(BF16)
        o = jnp.dot(y_ref[...], w_ref[...], preferred_element_type=F32).astype(BF16)
        o_ref[...] = o
        x1 = x_ref[...] + _rms(o.astype(F32), gp_ref[...])
        x1_ref[...] = x1
        h2_ref[...] = _rms(x1, g2_ref[...]).astype(BF16)

    row = lambda w: BS((tm, w), lambda i: (i, 0))
    vec = lambda w: BS((1, w), lambda i: (0, 0))
    return pl.pallas_call(
        body, name="outproj_fwd", grid=(T // tm,),
        in_specs=[row(A_W), row(B_W), row(M_W), row(D), vec(A_W), vec(B_W), vec(M_W), vec(D), vec(D),
                  BS((A_W + B_W + M_W, D), lambda i: (0, 0))],
        out_specs=[row(A_W + B_W + M_W), row(D), row(D), row(D)],
        out_shape=[jax.ShapeDtypeStruct((T, A_W + B_W + M_W), BF16), jax.ShapeDtypeStruct((T, D), BF16),
                   jax.ShapeDtypeStruct((T, D), F32), jax.ShapeDtypeStruct((T, D), BF16)],
        compiler_params=_cp(("arbitrary",)),
    )(ya, yb, ym, x2d, ga, gb, gm, g_post, g_pre2, w_out)


def _ffn_fwd(h2, x1, target, wg, wu, wd, g_post, tm):
    T, D = x1.shape
    ns, F, _ = wg.shape

    def body(h_ref, x1_ref, t_ref, wg_ref, wu_ref, wd_ref, gp_ref,
             gs_ref, us_ref, dff_ref, dx2_ref, dgp_ref, loss_ref, acc_ref):
        j = pl.program_id(0)
        i = pl.program_id(1)
        rows = pl.ds(pl.multiple_of(i * tm, tm), tm)
        h = h_ref[...]
        g = _dot_nt(h, wg_ref[0])
        u = _dot_nt(h, wu_ref[0])
        gs_ref[0] = g.astype(BF16)
        us_ref[0] = u.astype(BF16)
        part = _dot(_silu_mul(g, u), wd_ref[0])

        @pl.when(j == 0)
        def _():
            acc_ref[rows, :] = part

        @pl.when(j != 0)
        def _():
            acc_ref[rows, :] += part

        @pl.when(j == ns - 1)
        def _():
            ff = acc_ref[rows, :]
            diff = x1_ref[...] + _rms(ff, gp_ref[...]) - t_ref[...]
            dx2 = diff * (1.0 / D)
            dff, dgp = _rms_bwd(ff, gp_ref[...], dx2)
            dx2_ref[...] = dx2
            dff_ref[...] = dff.astype(BF16)
            lpart = jnp.sum(_colsum(diff * diff), axis=1, keepdims=True) * (0.5 / D)
            _acc(dgp_ref, dgp, i == 0)
            _acc(loss_ref, jnp.broadcast_to(lpart, (1, LANES)), i == 0)

    last = lambda j, i: (jnp.where(j == ns - 1, i, 0), 0)
    wsh = BS((1, F, D), lambda j, i: (j, 0, 0))
    sh = BS((1, tm, F), lambda j, i: (j, i, 0))
    return pl.pallas_call(
        body, name="ffn_fwd", grid=(ns, T // tm),
        in_specs=[BS((tm, D), lambda j, i: (i, 0)), BS((tm, D), last), BS((tm, D), last), wsh, wsh, wsh,
                  BS((1, D), lambda j, i: (0, 0))],
        out_specs=[sh, sh, BS((tm, D), last), BS((tm, D), last),
                   BS((1, D), lambda j, i: (0, 0)), BS((1, LANES), lambda j, i: (0, 0))],
        out_shape=[jax.ShapeDtypeStruct((ns, T, F), BF16), jax.ShapeDtypeStruct((ns, T, F), BF16),
                   jax.ShapeDtypeStruct((T, D), BF16), jax.ShapeDtypeStruct((T, D), F32),
                   jax.ShapeDtypeStruct((1, D), F32), jax.ShapeDtypeStruct((1, LANES), F32)],
        scratch_shapes=[pltpu.VMEM((T, D), F32)],
        compiler_params=_cp(("arbitrary", "arbitrary")),
    )(h2, x1, target, wg, wu, wd, g_post)


def _ffn_bwd(dff, h2, gs, us, wg, wu, wd, tm):
    T, D = h2.shape
    ns, F, _ = wg.shape

    def body(dff_ref, h_ref, gs_ref, us_ref, wg_ref, wu_ref, wd_ref, dh_ref, dwg_out, dwu_out, dwd_out,
             dwg_ref, dwu_ref, dwd_ref):
        first = pl.program_id(1) == 0
        dff = dff_ref[...]
        h = h_ref[...]
        parts = []
        for r in range(ROW_SPLIT):
            rows = slice(r * (tm // ROW_SPLIT), (r + 1) * (tm // ROW_SPLIT))
            dact = _dot_nt(dff[rows], wd_ref[0])
            g = gs_ref[0, rows, :].astype(F32)
            u = us_ref[0, rows, :].astype(F32)
            sig = _sigmoid(g)
            gsig = g * sig
            dg = (dact * u * (sig + gsig * (1.0 - sig))).astype(BF16)
            du = (dact * gsig).astype(BF16)
            dh_ref[0, rows, :] = (_dot(dg, wg_ref[0]) + _dot(du, wu_ref[0])).astype(BF16)
            parts.append(((gsig * u).astype(BF16), dg, du))
        a, dg, du = [jnp.concatenate(p, axis=0) for p in zip(*parts)]
        _acc(dwd_ref, _dot_tn(a, dff), first)
        _acc(dwg_ref, _dot_tn(dg, h), first)
        _acc(dwu_ref, _dot_tn(du, h), first)

        @pl.when(pl.program_id(1) == pl.num_programs(1) - 1)
        def _():
            dwg_out[0] = dwg_ref[...].astype(BF16)
            dwu_out[0] = dwu_ref[...].astype(BF16)
            dwd_out[0] = dwd_ref[...].astype(BF16)

    row = BS((tm, D), lambda j, i: (i, 0))
    sh = BS((1, tm, F), lambda j, i: (j, i, 0))
    wsh = BS((1, F, D), lambda j, i: (j, 0, 0))
    return pl.pallas_call(
        body, name="ffn_bwd", grid=(ns, T // tm),
        in_specs=[row, row, sh, sh, wsh, wsh, wsh],
        out_specs=[BS((1, tm, D), lambda j, i: (j, i, 0)), wsh, wsh, wsh],
        out_shape=[jax.ShapeDtypeStruct((ns, T, D), BF16)] + [jax.ShapeDtypeStruct((ns, F, D), BF16)] * 3,
        scratch_shapes=[pltpu.VMEM((F, D), F32)] * 3,
        compiler_params=_cp(("arbitrary", "arbitrary")),
    )(dff, h2, gs, us, wg, wu, wd)


def _mm_tn(a, b, name, tk):
    T, M = a.shape
    N = b.shape[1]
    tk = min(tk, T)

    def body(a_ref, b_ref, o_ref):
        _acc(o_ref, _dot_tn(a_ref[...], b_ref[...]), pl.program_id(0) == 0)

    return pl.pallas_call(
        body, name=name, grid=(T // tk,),
        in_specs=[BS((tk, M), lambda t: (t, 0)), BS((tk, N), lambda t: (t, 0))],
        out_specs=BS((M, N), lambda t: (0, 0)),
        out_shape=jax.ShapeDtypeStruct((M, N), F32),
        compiler_params=_cp(("arbitrary",)),
    )(a, b)


DPROJ_PIECES = ((0, A_W), (A_W, A_W), (768, B_W), (1152, B_W), (1536, B_W), (1920, LANES), (2048, M_W))


def _put_dproj(dp_ref, piece_refs):
    for (c0, w), ref in zip(DPROJ_PIECES, piece_refs):
        dp_ref[:, c0:c0 + w] = ref[...].astype(BF16)


def _dw_in(pieces, h, ns, tk):
    T, D = h.shape
    M = P_COLS
    dsh = D // ns
    tk = min(tk, T)

    def body(*refs):
        piece_refs, h_ref, o_ref, acc_ref, dp_ref = refs[:7], refs[7], refs[8], refs[9], refs[10]
        t = pl.program_id(0)
        _put_dproj(dp_ref, piece_refs)
        _acc(acc_ref, _dot_tn(h_ref[...], dp_ref[...]), t == 0)

        @pl.when(t == pl.num_programs(0) - 1)
        def _():
            for s in range(ns):
                o_ref[s] = acc_ref[s * dsh:(s + 1) * dsh, :].T.astype(BF16)

    return pl.pallas_call(
        body, name="dw_in", grid=(T // tk,),
        in_specs=[BS((tk, w), lambda t: (t, 0)) for _, w in DPROJ_PIECES] + [BS((tk, D), lambda t: (t, 0))],
        out_specs=BS((ns, M, dsh), lambda t: (0, 0, 0)),
        out_shape=jax.ShapeDtypeStruct((ns, M, dsh), BF16),
        scratch_shapes=[pltpu.VMEM((D, M), F32), pltpu.VMEM((tk, M), BF16)],
        compiler_params=_cp(("arbitrary",)),
    )(*pieces, h)


def _outproj_bwd(dh2, x1, dx2, o, ya, yb, ym, ga, gb, gm, g_post, g_pre2, w_out, tm):
    T, D = x1.shape
    ns = dh2.shape[0]

    def body(dh_ref, x1_ref, dx2_ref, o_ref, ya_ref, yb_ref, ym_ref, ga_ref, gb_ref, gm_ref, gp_ref, g2_ref, w_ref,
             dx1_ref, do_ref, dya_ref, dyb_ref, dym_ref, dga_ref, dgb_ref, dgm_ref, dgp_ref, dg2_ref):
        first = pl.program_id(0) == 0
        dh = dh_ref[0].astype(F32)
        for j in range(1, ns):
            dh = dh + dh_ref[j].astype(F32)
        dxa, dg2 = _rms_bwd(x1_ref[...], g2_ref[...], dh)
        dx1 = dx2_ref[...] + dxa
        dx1_ref[...] = dx1
        _acc(dg2_ref, dg2, first)
        do, dgp = _rms_bwd(o_ref[...].astype(F32), gp_ref[...], dx1)
        do = do.astype(BF16)
        do_ref[...] = do
        dy = _dot_nt(do, w_ref[...])
        dya, dga = _rms_bwd(ya_ref[...].astype(F32), ga_ref[...], dy[:, 0:A_W])
        dyb, dgb = _rms_bwd(yb_ref[...].astype(F32), gb_ref[...], dy[:, A_W:A_W + B_W])
        dym, dgm = _rms_bwd(ym_ref[...].astype(F32), gm_ref[...], dy[:, A_W + B_W:])
        dya_ref[...] = dya.astype(BF16)
        dyb_ref[...] = dyb.astype(BF16)
        dym_ref[...] = dym.astype(BF16)
        _acc(dga_ref, dga, first)
        _acc(dgb_ref, dgb, first)
        _acc(dgm_ref, dgm, first)
        _acc(dgp_ref, dgp, first)

    row = lambda w: BS((tm, w), lambda i: (i, 0))
    vec = lambda w: BS((1, w), lambda i: (0, 0))
    sds = jax.ShapeDtypeStruct
    return pl.pallas_call(
        body, name="outproj_bwd", grid=(T // tm,),
        in_specs=[BS((ns, tm, D), lambda i: (0, i, 0)), row(D), row(D), row(D), row(A_W), row(B_W), row(M_W),
                  vec(A_W), vec(B_W), vec(M_W), vec(D), vec(D), BS((A_W + B_W + M_W, D), lambda i: (0, 0))],
        out_specs=[row(D), row(D), row(A_W), row(B_W), row(M_W), vec(A_W), vec(B_W), vec(M_W), vec(D), vec(D)],
        out_shape=[sds((T, D), F32), sds((T, D), BF16), sds((T, A_W), BF16), sds((T, B_W), BF16), sds((T, M_W), BF16),
                   sds((1, A_W), F32), sds((1, B_W), F32), sds((1, M_W), F32), sds((1, D), F32), sds((1, D), F32)],
        compiler_params=_cp(("arbitrary",)),
    )(dh2, x1, dx2, o, ya, yb, ym, ga, gb, gm, g_post, g_pre2, w_out)


def _sgu_bwd(proj, dya, g_sgu, ws_tril, bs_full, tm):
    T = proj.shape[0]
    nch = tm // CHUNK

    def body(zu_ref, zv_ref, dy_ref, g_ref, ws_ref, b_ref, dzu_ref, dzv_ref, dws_ref, dbs_ref, dg_ref,
             du_ref, dvn_ref, dbf_ref):
        step = pl.program_id(0)
        first = step == 0
        lane = _iota((CHUNK, LANES), 1)
        tril = _iota((CHUNK, CHUNK), 0) >= _iota((CHUNK, CHUNK), 1)
        (u, vn), vjp = jax.vjp(_sgu_pre, zu_ref[...].astype(F32), zv_ref[...].astype(F32), g_ref[...])
        vnb = vn.astype(BF16)
        dy = dy_ref[...].astype(F32)

        @pl.when(first)
        def _():
            dws_ref[...] = jnp.zeros_like(dws_ref)
            dbf_ref[...] = jnp.zeros_like(dbf_ref)

        for c in range(nch):
            rs = slice(c * CHUNK, (c + 1) * CHUNK)
            for j in range(3):
                cs = slice(j * LANES, (j + 1) * LANES)
                vp = vnb[rs, cs]
                z = jnp.where(lane < HEAD, _dot(ws_ref[2 * j], vp), _dot(ws_ref[2 * j + 1], vp)) + b_ref[:, cs]
                du_ref[rs, cs] = dy[rs, cs] * z
                dz = dy[rs, cs] * u[rs, cs]
                dbf_ref[:, cs] += dz
                dzb = dz.astype(BF16)
                dz0 = jnp.where(lane < HEAD, dzb, jnp.zeros_like(dzb))
                dz1 = jnp.where(lane >= HEAD, dzb, jnp.zeros_like(dzb))
                dvn_ref[rs, cs] = jnp.where(lane < HEAD, _dot_tn(ws_ref[2 * j], dzb), _dot_tn(ws_ref[2 * j + 1], dzb))
                dws_ref[2 * j] += jnp.where(tril, _dot_nt(dz0, vp), 0.0)
                dws_ref[2 * j + 1] += jnp.where(tril, _dot_nt(dz1, vp), 0.0)
        dzu, dzv, dg = vjp((du_ref[...], dvn_ref[...]))
        dzu_ref[...] = dzu.astype(BF16)
        dzv_ref[...] = dzv.astype(BF16)
        _acc(dg_ref, dg, first)

        @pl.when(step == pl.num_programs(0) - 1)
        def _():
            out = jnp.zeros((CHUNK, LANES), F32)
            for j in range(3):
                slab = dbf_ref[:, j * LANES:(j + 1) * LANES]
                lo = jnp.sum(jnp.where(lane < HEAD, slab, 0.0), axis=1, keepdims=True)
                hi = jnp.sum(jnp.where(lane >= HEAD, slab, 0.0), axis=1, keepdims=True)
                out = out + jnp.where(lane == 2 * j, lo, 0.0) + jnp.where(lane == 2 * j + 1, hi, 0.0)
            dbs_ref[...] = out

    return pl.pallas_call(
        body, name="sgu_bwd", grid=(T // tm,),
        in_specs=[BS((tm, A_W), lambda i: (i, 0)), BS((tm, A_W), lambda i: (i, 1)), BS((tm, A_W), lambda i: (i, 0)),
                  BS((1, A_W), lambda i: (0, 0)), BS((6, CHUNK, CHUNK), lambda i: (0, 0, 0)),
                  BS((CHUNK, A_W), lambda i: (0, 0))],
        out_specs=[BS((tm, A_W), lambda i: (i, 0)), BS((tm, A_W), lambda i: (i, 0)),
                   BS((6, CHUNK, CHUNK), lambda i: (0, 0, 0)), BS((CHUNK, LANES), lambda i: (0, 0)),
                   BS((1, A_W), lambda i: (0, 0))],
        out_shape=[jax.ShapeDtypeStruct((T, A_W), BF16), jax.ShapeDtypeStruct((T, A_W), BF16),
                   jax.ShapeDtypeStruct((6, CHUNK, CHUNK), F32), jax.ShapeDtypeStruct((CHUNK, LANES), F32),
                   jax.ShapeDtypeStruct((1, A_W), F32)],
        scratch_shapes=[pltpu.VMEM((tm, A_W), F32), pltpu.VMEM((tm, A_W), F32), pltpu.VMEM((CHUNK, A_W), F32)],
        compiler_params=_cp(("arbitrary",)),
    )(proj, proj, dya, g_sgu, ws_tril, bs_full)


def _memattn_bwd(proj, kv, dym, Bl, S, tq):
    T = Bl * S
    nq = S // tq
    Mt = kv.shape[1]
    qc = 1920 // LANES

    def body(q_ref, km_ref, vm_ref, do_ref, dq_ref, dkm_ref, dvm_ref):
        first = pl.program_id(2) == 0
        lane = _iota((tq, LANES), 1)
        q = q_ref[...]
        do = do_ref[...]
        dq_out = jnp.zeros((tq, LANES), F32)
        dkm = jnp.zeros((Mt, LANES), F32)
        dvm = jnp.zeros((Mt, LANES), F32)
        for hh in range(2):
            hmask = (lane < HEAD) if hh == 0 else (lane >= HEAD)
            qs = jnp.where(hmask, q, jnp.zeros_like(q)) * 0.125
            dom = jnp.where(hmask, do, 0.0).astype(BF16)
            s = _dot_nt(qs, km_ref[0])
            pe = jnp.exp(s - jnp.max(s, axis=1, keepdims=True))
            pn = pe / jnp.sum(pe, axis=1, keepdims=True)
            dp = _dot_nt(dom, vm_ref[0])
            ds = (pn * (dp - jnp.sum(pn * dp, axis=1, keepdims=True))).astype(BF16)
            dq_out = jnp.where(hmask, _dot(ds, km_ref[0]) * 0.125, dq_out)
            dkm = dkm + _dot_tn(ds, qs)
            dvm = dvm + _dot_tn(pn, dom)
        dq_ref[...] = dq_out.astype(BF16)
        _acc(dkm_ref, dkm[None], first)
        _acc(dvm_ref, dvm[None], first)

    return pl.pallas_call(
        body, name="memattn_bwd", grid=(Bl, 2, nq),
        in_specs=[BS((tq, LANES), lambda b, p, i: (b * nq + i, qc + p)),
                  BS((1, Mt, LANES), lambda b, p, i: (b, 0, p)),
                  BS((1, Mt, LANES), lambda b, p, i: (b, 0, 2 + p)),
                  BS((tq, LANES), lambda b, p, i: (b * nq + i, p))],
        out_specs=[BS((tq, LANES), lambda b, p, i: (b * nq + i, p)),
                   BS((1, Mt, LANES), lambda b, p, i: (b, 0, p)),
                   BS((1, Mt, LANES), lambda b, p, i: (b, 0, p))],
        out_shape=[jax.ShapeDtypeStruct((T, M_W), BF16), jax.ShapeDtypeStruct((Bl, Mt, M_W), F32),
                   jax.ShapeDtypeStruct((Bl, Mt, M_W), F32)],
        compiler_params=_cp(("arbitrary", "arbitrary", "arbitrary")),
    )(proj, kv, kv, dym)


def _memkv_bwd(dkm, dvm, memn, mem, g_mem, w_kv):
    Bl, Mt, D = mem.shape

    def body(dk_ref, dv_ref, mn_ref, m_ref, g_ref, w_ref, dw_ref, dg_ref):
        first = pl.program_id(0) == 0
        dk = dk_ref[0].astype(BF16)
        dv = dv_ref[0].astype(BF16)
        mn = mn_ref[0]
        dmn = _dot_nt(dk, w_ref[:, 0:M_W]) + _dot_nt(dv, w_ref[:, M_W:])
        _, dg = _rms_bwd(m_ref[0], g_ref[...], dmn)
        _acc(dg_ref, dg, first)

        @pl.when(first)
        def _():
            dw_ref[...] = jnp.zeros_like(dw_ref)

        dw_ref[:, 0:M_W] += _dot_tn(mn, dk)
        dw_ref[:, M_W:] += _dot_tn(mn, dv)

    return pl.pallas_call(
        body, name="memkv_bwd", grid=(Bl,),
        in_specs=[BS((1, Mt, M_W), lambda b: (b, 0, 0)), BS((1, Mt, M_W), lambda b: (b, 0, 0)),
                  BS((1, Mt, D), lambda b: (b, 0, 0)), BS((1, Mt, D), lambda b: (b, 0, 0)),
                  BS((1, D), lambda b: (0, 0)), BS((D, 2 * M_W), lambda b: (0, 0))],
        out_specs=[BS((D, 2 * M_W), lambda b: (0, 0)), BS((1, D), lambda b: (0, 0))],
        out_shape=[jax.ShapeDtypeStruct((D, 2 * M_W), F32), jax.ShapeDtypeStruct((1, D), F32)],
        compiler_params=_cp(("arbitrary",)),
    )(dkm, dvm, memn, mem, g_mem, w_kv)


def _fox_bwd(proj, dyb, lse, bq, bk, Bl, S):
    T = Bl * S
    nq = S // Q_BLK
    nb = S // LANES
    qc, kc, vc = 768 // LANES, 1152 // LANES, 1536 // LANES

    def body(q_ref, k_ref, v_ref, do_ref, lse_ref, bq_ref, bk_ref,
             dq_ref, dk_ref, dv_ref, dcr_ref, ka_ref, dka_ref, dva_ref):
        p = pl.program_id(1)
        lane_s = _iota((S, LANES), 1)
        lane = _iota((Q_BLK, LANES), 1)
        sub = _iota((8, LANES), 0)
        tri = _iota((Q_BLK, Q_BLK), 1) <= _iota((Q_BLK, Q_BLK), 0)
        k = k_ref[...]
        for hh in range(2):
            data = (lane_s < HEAD) if hh == 0 else (lane_s >= HEAD)
            ka_ref[hh] = jnp.where(data, k, bk_ref[0, hh])
        dka_ref[...] = jnp.zeros_like(dka_ref)
        dva_ref[...] = jnp.zeros_like(dva_ref)

        @pl.when(p == 0)
        def _():
            dcr_ref[...] = jnp.zeros_like(dcr_ref)

        def add_colsums(ds, first_blk, h):
            cs = _colsum(ds)
            for jb in range(ds.shape[1] // LANES):
                dcr_ref[0, first_blk + jb] += jnp.where(sub == h, cs[:, jb * LANES:(jb + 1) * LANES], 0.0)

        for i in range(nq):
            r0 = i * Q_BLK
            r1 = r0 + Q_BLK
            q = q_ref[r0:r1, :]
            do = do_ref[r0:r1, :]
            lse_b = lse_ref[0, r0:r1, :]
            dq_out = jnp.zeros((Q_BLK, LANES), F32)
            for hh in range(2):
                hmask = (lane < HEAD) if hh == 0 else (lane >= HEAD)
                h = 2 * p + hh
                qs = jnp.where(hmask, q * 0.125, jnp.zeros_like(q))
                qa = jnp.where(hmask, q * 0.125, bq_ref[0, hh, r0:r1, :])
                dob = jnp.where(hmask, do, 0.0).astype(BF16)
                lse_h = jnp.sum(jnp.where(lane == hh * HEAD, lse_b, 0.0), axis=1, keepdims=True)
                pd = jnp.where(tri, jnp.exp(_dot_nt(qa, ka_ref[hh, r0:r1, :]) - lse_h), 0.0)
                dpd = _dot_nt(dob, v_ref[r0:r1, :])
                delta = jnp.sum(pd * dpd, axis=1, keepdims=True)
                psum = jnp.sum(pd, axis=1, keepdims=True)
                if i:
                    pf = jnp.exp(_dot_nt(qa, ka_ref[hh, 0:r0, :]) - lse_h)
                    dpf = _dot_nt(dob, v_ref[0:r0, :])
                    delta = delta + jnp.sum(pf * dpf, axis=1, keepdims=True)
                    psum = psum + jnp.sum(pf, axis=1, keepdims=True)
                delta = delta / psum
                dsd = pd * (dpd - delta)
                add_colsums(dsd, r0 // LANES, h)
                dsd = dsd.astype(BF16)
                dq_h = _dot(dsd, k_ref[r0:r1, :])
                dka_ref[r0:r1, :] += _dot_tn(dsd, qs)
                dva_ref[r0:r1, :] += _dot_tn(pd, dob)
                if i:
                    dsf = pf * (dpf - delta)
                    add_colsums(dsf, 0, h)
                    dsf = dsf.astype(BF16)
                    dq_h = dq_h + _dot(dsf, k_ref[0:r0, :])
                    dka_ref[0:r0, :] += _dot_tn(dsf, qs)
                    dva_ref[0:r0, :] += _dot_tn(pf, dob)
                dq_out = jnp.where(hmask, dq_h * 0.125, dq_out)
            dq_ref[r0:r1, :] = dq_out.astype(BF16)
        dk_ref[...] = dka_ref[...].astype(BF16)
        dv_ref[...] = dva_ref[...].astype(BF16)

    seq = lambda c0: BS((S, LANES), lambda b, p: (b, c0 + p))
    pair = BS((1, 2, S, LANES), lambda b, p: (b, p, 0, 0))
    rowblk = BS((1, nb, 8, LANES), lambda b, p: (b, 0, 0, 0))
    return pl.pallas_call(
        body, name="fox_bwd", grid=(Bl, 3),
        in_specs=[seq(qc), seq(kc), seq(vc), seq(0), BS((1, S, LANES), lambda b, p: (p, b, 0)), pair, pair],
        out_specs=[seq(0), seq(0), seq(0), rowblk],
        out_shape=[jax.ShapeDtypeStruct((T, B_W), BF16)] * 3 + [jax.ShapeDtypeStruct((Bl, nb, 8, LANES), F32)],
        scratch_shapes=[pltpu.VMEM((2, S, LANES), BF16), pltpu.VMEM((S, LANES), F32), pltpu.VMEM((S, LANES), F32)],
        compiler_params=_cp(("arbitrary", "arbitrary")),
    )(proj, proj, proj, dyb, lse, bq, bk)


def _gate_bwd(dc_row, fl_row):
    Bl, nb, _, _ = dc_row.shape

    def body(dc_ref, fl_ref, o_ref):
        lane = _iota((8, LANES), 1)

        carry = jnp.zeros((8, 1), F32)
        for j in reversed(range(nb)):
            r = -dc_ref[0, j]
            for k in (1, 2, 4, 8, 16, 32, 64):
                r = r + jnp.where(lane < LANES - k, pltpu.roll(r, LANES - k, 1), 0.0)
            total = jnp.sum(jnp.where(lane == 0, r, 0.0), axis=1, keepdims=True)
            dfl = (r + carry) * _sigmoid(-fl_ref[0, j])
            carry = carry + total
            o_ref[0, j * LANES:(j + 1) * LANES, :] = jnp.concatenate(
                [dfl, jnp.zeros((LANES - 8, LANES), F32)], axis=0).T

    rowblk = BS((1, nb, 8, LANES), lambda b: (b, 0, 0, 0))
    return pl.pallas_call(
        body, name="gate_bwd", grid=(Bl,),
        in_specs=[rowblk, rowblk],
        out_specs=BS((1, nb * LANES, LANES), lambda b: (b, 0, 0)),
        out_shape=jax.ShapeDtypeStruct((Bl, nb * LANES, LANES), F32),
        compiler_params=_cp(("arbitrary",)),
    )(dc_row, fl_row)


def _inproj_bwd(pieces, x2d, dx1, g_pre, w_in_p, tm):
    T, D = x2d.shape
    ns, _, dsh = w_in_p.shape

    def body(*refs):
        piece_refs = refs[:7]
        x_ref, dx1_ref, g_ref, w_ref, gx_ref, dg_ref, dbf_ref, dp_ref = refs[7:]
        first = pl.program_id(0) == 0
        _put_dproj(dp_ref, piece_refs)
        dh = jnp.concatenate([_dot(dp_ref[...], w_ref[s]) for s in range(ns)], axis=1)
        dxa, dg = _rms_bwd(x_ref[...], g_ref[...], dh)
        gx_ref[...] = dx1_ref[...] + dxa
        _acc(dg_ref, dg, first)
        _acc(dbf_ref, _colsum(piece_refs[5][...]), first)

    row = lambda w: BS((tm, w), lambda i: (i, 0))
    return pl.pallas_call(
        body, name="inproj_bwd", grid=(T // tm,),
        in_specs=[row(w) for _, w in DPROJ_PIECES] + [row(D), row(D), BS((1, D), lambda i: (0, 0)),
                                                      BS((ns, P_COLS, dsh), lambda i: (0, 0, 0))],
        out_specs=[row(D), BS((1, D), lambda i: (0, 0)), BS((1, LANES), lambda i: (0, 0))],
        out_shape=[jax.ShapeDtypeStruct((T, D), F32), jax.ShapeDtypeStruct((1, D), F32),
                   jax.ShapeDtypeStruct((1, LANES), F32)],
        scratch_shapes=[pltpu.VMEM((tm, P_COLS), BF16)],
        compiler_params=_cp(("arbitrary",)),
    )(*pieces, x2d, dx1, g_pre, w_in_p)


def _local_step(x, mem, target, W, P, reduce=None):
    Bl, S, D = x.shape
    T = Bl * S
    tm = min(512, T)
    x2d = x.reshape(T, D)
    t2d = target.reshape(T, D)
    vec = lambda a: a.reshape(1, -1)
    bf_row = jnp.pad(P["b_f"].reshape(1, -1), ((0, 0), (0, LANES - N_FOX_HEADS)))
    tril = jnp.tril(jnp.ones((CHUNK, CHUNK), bool))
    ws_tril = jnp.where(tril[None], P["w_s"][0], 0.0).astype(BF16)
    bs_full = jnp.repeat(P["b_s"][0].T, HEAD, axis=1)
    g_pre, g_sgu = vec(P["g_pre_mix"]), vec(P["g_sgu"])
    ga, gb, gm = vec(P["g_out_a"]), vec(P["g_out_b"]), vec(P["g_out_m"])
    g_mem, g_post, g_pre2, g_post2 = vec(P["g_mem"]), vec(P["g_post_mix"]), vec(P["g_pre_ffn"]), vec(P["g_post_ffn"])

    h, proj, flog = _inproj_fwd(x2d, g_pre, W["w_in"], tm)
    bq, bk, fl_row = _gate_fwd(flog.reshape(Bl, S, LANES), bf_row)
    ya = _sgu_fwd(proj, g_sgu, ws_tril, bs_full, tm)
    yb, lse = _fox_fwd(proj, bq, bk, Bl, S)
    memn, kv = _memkv_fwd(mem, g_mem, W["w_mem_kv"])
    ym = _memattn_fwd(proj, kv, Bl, S, min(2048, S))
    y, o, x1, h2 = _outproj_fwd(ya, yb, ym, x2d, ga, gb, gm, g_post, g_pre2, W["w_out"], tm)
    gs, us, dff, dx2, dg_post2, loss = _ffn_fwd(h2, x1, t2d, W["w_gate"], W["w_up"], W["w_down"], g_post2, tm)

    dh2, d_w_gate, d_w_up, d_w_down = _ffn_bwd(dff, h2, gs, us, W["w_gate"], W["w_up"], W["w_down"], min(1024, T))
    ffn = [d_w_gate, d_w_up, d_w_down]
    if reduce is not None:
        pending, _ = reduce.begin("ffn", ffn)
    dx1, do, dya, dyb, dym, dga, dgb, dgm, dg_post, dg_pre2 = _outproj_bwd(
        dh2, x1, dx2, o, ya, yb, ym, ga, gb, gm, g_post, g_pre2, W["w_out"], tm)
    if reduce is not None:
        ffn, (do, dya, dyb, dym) = reduce.finish("ffn", pending, (do, dya, dyb, dym))
    d_w_out = _mm_tn(y, do, "dw_out", 1024)
    dzu, dzv, dws, dbs_cols, dg_sgu = _sgu_bwd(proj, dya, g_sgu, ws_tril, bs_full, tm)
    dqm, dkm, dvm = _memattn_bwd(proj, kv, dym, Bl, S, min(2048, S))
    d_w_kv, dg_mem = _memkv_bwd(dkm, dvm, memn, mem, g_mem, W["w_mem_kv"])
    mid = [d_w_kv, d_w_out]
    dq, dk, dv, dc_row = _fox_bwd(proj, dyb, lse, bq, bk, Bl, S)
    if reduce is not None:
        done = reduce.apply(BIG[3:], ffn)
        pending, after = reduce.begin("mid", mid, (dc_row,) + done)
        dc_row = after[0]
    dfl = _gate_bwd(dc_row, fl_row).reshape(T, LANES)
    pieces = (dzu, dzv, dq, dk, dv, dfl, dqm)
    grad_x, dg_pre, dbf = _inproj_bwd(pieces, x2d, dx1, g_pre, W["w_in"], tm)
    if reduce is not None:
        mid, (dfl,) = reduce.finish("mid", pending, (dfl,))
        pieces = (dzu, dzv, dq, dk, dv, dfl, dqm)
    d_w_in = _dw_in(pieces, h, W["w_in"].shape[0], 1024)
    if reduce is None:
        big = dict(zip(BIG, [d_w_in] + mid + ffn))
    else:
        done = reduce.apply(BIG[1:3], mid)
        big = {"w_in": reduce.begin("in", [d_w_in], done)[0]}
    small = {"g_pre_mix": dg_pre, "b_f": dbf[:, :N_FOX_HEADS], "g_sgu": dg_sgu, "w_s": dws, "b_s": dbs_cols[:, :N_FOX_HEADS].T,
             "g_out_a": dga, "g_out_b": dgb, "g_out_m": dgm, "g_mem": dg_mem, "g_post_mix": dg_post,
             "g_pre_ffn": dg_pre2, "g_post_ffn": dg_post2, "loss": loss[:, :1]}
    return grad_x.reshape(Bl, S, D), big, small


def _place():
    return lax.axis_index("x"), lax.axis_index("y"), lax.axis_index("c")


def _exchange_on_sequencer(srcs, own_full, name, collective_id):
    n = len(srcs)

    def body(*refs):
        src, dst = refs[:n], refs[n:2 * n]
        lsem, isend, irecv, dsend, drecv = refs[2 * n:]
        x, y, c = _place()
        oc = 1 - c
        s_me = 2 * x + y
        sib = (x, y, oc)
        chips = [(1 - x, y), (x, 1 - y), (1 - x, 1 - y)]
        barrier = pltpu.get_barrier_semaphore()
        for dev in [(cx, cy, c) for cx, cy in chips] + [sib]:
            pl.semaphore_signal(barrier, inc=1, device_id=dev, device_id_type=MESH)
        pl.semaphore_wait(barrier, 4)

        def remote(a, b, ssem, rsem, dev):
            return pltpu.make_async_remote_copy(src_ref=a, dst_ref=b, send_sem=ssem, recv_sem=rsem,
                                                device_id=dev, device_id_type=MESH)

        sends, local = [], []
        for w in range(n):
            for j, (cx, cy) in enumerate(chips):
                half = src[w].at[c] if own_full else src[w].at[2 * cx + cy]
                cp = remote(half, dst[w].at[s_me, c], isend.at[w, j], irecv.at[w, j], (cx, cy, c))
                cp.start()
                sends.append(cp)
            if own_full:
                cp = remote(src[w], dst[w].at[s_me], dsend.at[w, 3], drecv.at[w, 3], sib)
            else:
                cp = remote(src[w].at[s_me], dst[w].at[s_me, c], dsend.at[w, 3], drecv.at[w, 3], sib)
                loc = pltpu.make_async_copy(src[w].at[s_me], dst[w].at[s_me, c], lsem.at[w])
                loc.start()
                local.append(loc)
            cp.start()
            sends.append(cp)
        for w in range(n):
            for j, (cx, cy) in enumerate(chips):
                landed = dst[w].at[2 * cx + cy, c]
                remote(landed, landed, isend.at[w, j], irecv.at[w, j], (cx, cy, c)).wait_recv()
                cp = remote(landed, landed, dsend.at[w, j], drecv.at[w, j], sib)
                cp.start()
                sends.append(cp)
        for w in range(n):
            for j, (cx, cy) in enumerate(chips):
                landed = dst[w].at[2 * cx + cy, oc]
                remote(landed, landed, dsend.at[w, j], drecv.at[w, j], sib).wait_recv()
            landed = dst[w].at[s_me] if own_full else dst[w].at[s_me, oc]
            remote(landed, landed, dsend.at[w, 3], drecv.at[w, 3], sib).wait_recv()
        for cp in sends:
            cp.wait_send()
        for loc in local:
            loc.wait()

    return pl.kernel(
        body, out_type=[jax.ShapeDtypeStruct((4, 2) + s.shape[1:], s.dtype) for s in srcs],
        mesh=plsc.ScalarSubcoreMesh(axis_name="sequencer", num_cores=1), name=name,
        scratch_types=[pltpu.SemaphoreType.DMA((n,)), pltpu.SemaphoreType.DMA((n, 3)), pltpu.SemaphoreType.DMA((n, 3)),
                       pltpu.SemaphoreType.DMA((n, 4)), pltpu.SemaphoreType.DMA((n, 4))],
        compiler_params=pltpu.CompilerParams(collective_id=collective_id),
    )(*srcs)


def _sibling_swap(grads, name, collective_id):
    n = len(grads)

    def body(*refs):
        g, theirs = refs[:n], refs[n:2 * n]
        ssem, rsem = refs[2 * n:]
        x, y, c = _place()
        sib = (x, y, 1 - c)
        barrier = pltpu.get_barrier_semaphore()
        pl.semaphore_signal(barrier, inc=1, device_id=sib, device_id_type=MESH)
        pl.semaphore_wait(barrier, 1)
        cps = []
        for w in range(n):
            cp = pltpu.make_async_remote_copy(src_ref=g[w].at[:, 1 - c], dst_ref=theirs[w], send_sem=ssem.at[w],
                                              recv_sem=rsem.at[w], device_id=sib, device_id_type=MESH)
            cp.start()
            cps.append(cp)
        for cp in cps:
            cp.wait()

    return pl.kernel(
        body, out_type=[jax.ShapeDtypeStruct((4,) + g.shape[2:], g.dtype) for g in grads],
        mesh=plsc.ScalarSubcoreMesh(axis_name="sequencer", num_cores=1), name=name,
        scratch_types=[pltpu.SemaphoreType.DMA((n,)), pltpu.SemaphoreType.DMA((n,))],
        compiler_params=pltpu.CompilerParams(collective_id=collective_id),
    )(*grads)


def _add_pair(core, g, theirs, name):
    _, _, hr, C = g.shape

    def body(core_ref, g_ref, t_ref, o_ref):
        o_ref[0] = (g_ref[0, 0].astype(F32) + t_ref[0].astype(F32)).astype(BF16)

    blk = BS((1, hr, C), lambda s, core_ref: (s, 0, 0))
    return pl.pallas_call(
        body, name=name,
        grid_spec=pltpu.PrefetchScalarGridSpec(
            num_scalar_prefetch=1, grid=(4,),
            in_specs=[BS((1, 1, hr, C), lambda s, core_ref: (s, core_ref[0], 0, 0)), blk], out_specs=blk),
        out_shape=jax.ShapeDtypeStruct(theirs.shape, BF16), compiler_params=_cp(("arbitrary",)))(core, g, theirs)


def _sum_chips(r, name):
    _, _, hr, C = r.shape

    def body(r_ref, o_ref):
        o_ref[...] = ((r_ref[0, 0].astype(F32) + r_ref[1, 0].astype(F32)) + r_ref[2, 0].astype(F32)) + r_ref[3, 0].astype(F32)

    return pl.pallas_call(body, name=name, grid=(2,), in_specs=[BS((4, 1, hr, C), lambda h: (0, h, 0, 0))],
                          out_specs=BS((hr, C), lambda h: (h, 0)), out_shape=jax.ShapeDtypeStruct((2 * hr, C), F32),
                          compiler_params=_cp(("arbitrary",)))(r)


class _Reducer:
    IDS = {"ffn": (4, 5), "mid": (6, 7), "in": (8, 9)}

    def __init__(self, core, apply):
        self.core = core
        self.apply = apply

    def begin(self, tag, grads, after=()):
        grads, after = lax.optimization_barrier((list(grads), after))
        g4 = [g.reshape(4, 2, -1, g.shape[-1]) for g in grads]
        return (g4, _sibling_swap(g4, "swap_" + tag, self.IDS[tag][0])), after

    def finish(self, tag, pending, hold):
        g4, theirs = pending
        sums = [_add_pair(self.core, g, t, "chip_sum_%s_%d" % (tag, k)) for k, (g, t) in enumerate(zip(g4, theirs))]
        sums, hold = lax.optimization_barrier((sums, hold))
        return _exchange_on_sequencer(sums, False, "scatter_" + tag, self.IDS[tag][1]), hold


def _small_allreduce(part):
    R = part.shape[0]
    rs = R // 8
    masks = [(mx, my, mc) for mx in (0, 1) for my in (0, 1) for mc in (0, 1)][1:]

    def body(p_ref, o_ref, buf_ref, s1, r1, s2, r2):
        x, y, c = _place()
        d = 4 * x + 2 * y + c
        mine = pl.ds(pl.multiple_of(d * rs, 8), rs)
        peers = [((x + mx) % 2, (y + my) % 2, (c + mc) % 2) for mx, my, mc in masks]
        first, second = [], []
        for k, (px, py, pc) in enumerate(peers):
            theirs = pl.ds(pl.multiple_of((4 * px + 2 * py + pc) * rs, 8), rs)
            cp = pltpu.make_async_remote_copy(src_ref=p_ref.at[theirs, :], dst_ref=buf_ref.at[d], send_sem=s1.at[k],
                                              recv_sem=r1.at[k], device_id=(px, py, pc), device_id_type=MESH)
            cp.start()
            first.append(cp)
        buf_ref[d] = p_ref[mine, :]
        for k, (px, py, pc) in enumerate(peers):
            slot = buf_ref.at[4 * px + 2 * py + pc]
            pltpu.make_async_remote_copy(src_ref=slot, dst_ref=slot, send_sem=s1.at[k], recv_sem=r1.at[k],
                                         device_id=(px, py, pc), device_id_type=MESH).wait_recv()
        total = buf_ref[0]
        for k in range(1, 8):
            total = total + buf_ref[k]
        o_ref[mine, :] = total
        for k, (px, py, pc) in enumerate(peers):
            cp = pltpu.make_async_remote_copy(src_ref=o_ref.at[mine, :], dst_ref=o_ref.at[mine, :], send_sem=s2.at[k],
                                              recv_sem=r2.at[k], device_id=(px, py, pc), device_id_type=MESH)
            cp.start()
            second.append(cp)
        for k, (px, py, pc) in enumerate(peers):
            rows = o_ref.at[pl.ds(pl.multiple_of((4 * px + 2 * py + pc) * rs, 8), rs), :]
            pltpu.make_async_remote_copy(src_ref=rows, dst_ref=rows, send_sem=s2.at[k], recv_sem=r2.at[k],
                                         device_id=(px, py, pc), device_id_type=MESH).wait_recv()
        for cp in first + second:
            cp.wait_send()

    vm = pl.BlockSpec(memory_space=pltpu.VMEM)
    return pl.pallas_call(
        body, name="small_allreduce", in_specs=[vm], out_specs=vm, out_shape=jax.ShapeDtypeStruct(part.shape, F32),
        scratch_shapes=[pltpu.VMEM((8, rs, LANES), F32)] + [pltpu.SemaphoreType.DMA((7,))] * 4,
    )(part)


def _adamw(w, g, m, v, name):
    R, C = w.shape
    summed = g.ndim == 4
    if summed:
        tr = R // 2
    else:
        tr = R if R * C * 4 <= (1 << 21) else R // 2
        if tr % 8:
            tr = R
    c1 = 1.0 / (1.0 - ADAM_B1 ** ADAM_STEP)
    c2 = 1.0 / (1.0 - ADAM_B2 ** ADAM_STEP)

    def body(w_ref, g_ref, m_ref, v_ref, *outs):
        if summed:
            g_ = ((g_ref[0, 0].astype(F32) + g_ref[1, 0].astype(F32)) + g_ref[2, 0].astype(F32)) + g_ref[3, 0].astype(F32)
            outs[0][...] = g_
        else:
            g_ = g_ref[...]
        d_ref, mo_ref, vo_ref = outs[-3:]
        m_ = ADAM_B1 * m_ref[...] + (1.0 - ADAM_B1) * g_
        v_ = ADAM_B2 * v_ref[...] + (1.0 - ADAM_B2) * (g_ * g_)
        mo_ref[...] = m_
        vo_ref[...] = v_
        d_ref[...] = -ADAM_LR * ((m_ * c1) / (jnp.sqrt(v_ * c2) + ADAM_EPS) + ADAM_WD * w_ref[...])

    blk = BS((tr, C), lambda i: (i, 0))
    g_blk = BS((4, 1, tr, C), lambda i: (0, i, 0, 0)) if summed else blk
    nout = 4 if summed else 3
    return pl.pallas_call(body, name=name, grid=(R // tr,), in_specs=[blk, g_blk, blk, blk], out_specs=[blk] * nout,
                          out_shape=[jax.ShapeDtypeStruct((R, C), F32)] * nout,
                          compiler_params=_cp(("arbitrary",)))(w, g, m, v)


def _adamw_from_transposed(w, g_t, m, v, name):
    R, C = w.shape
    tc = 256
    c1 = 1.0 / (1.0 - ADAM_B1 ** ADAM_STEP)
    c2 = 1.0 / (1.0 - ADAM_B2 ** ADAM_STEP)

    def body(w_ref, g_ref, m_ref, v_ref, go_ref, d_ref, mo_ref, vo_ref):
        g_ = g_ref[...].T
        go_ref[...] = g_
        m_ = ADAM_B1 * m_ref[...] + (1.0 - ADAM_B1) * g_
        v_ = ADAM_B2 * v_ref[...] + (1.0 - ADAM_B2) * (g_ * g_)
        mo_ref[...] = m_
        vo_ref[...] = v_
        d_ref[...] = -ADAM_LR * ((m_ * c1) / (jnp.sqrt(v_ * c2) + ADAM_EPS) + ADAM_WD * w_ref[...])

    blk = BS((R, tc), lambda i: (0, i))
    return pl.pallas_call(body, name=name, grid=(pl.cdiv(C, tc),), in_specs=[blk, BS((tc, R), lambda i: (i, 0)), blk, blk],
                          out_specs=[blk] * 4, out_shape=[jax.ShapeDtypeStruct((R, C), F32)] * 4,
                          compiler_params=_cp(("arbitrary",)))(w, g_t, m, v)


SMALL = ("g_pre_mix", "b_f", "g_sgu", "w_s", "b_s", "g_out_a", "g_out_b", "g_out_m", "g_mem", "g_post_mix",
         "g_pre_ffn", "g_post_ffn")
BIG = ("w_in", "w_mem_kv", "w_out", "w_gate", "w_up", "w_down")
TRANSPOSED = ("w_in", "w_gate", "w_up")
WEIGHTS = ("g_pre_mix", "w_in", "b_f", "g_sgu", "w_s", "b_s", "g_out_a", "g_out_b", "g_out_m", "g_mem", "w_mem_kv",
           "w_out", "g_post_mix", "g_pre_ffn", "w_gate", "w_up", "w_down", "g_post_ffn")


VECTORS = ("g_pre_mix", "b_f", "g_sgu", "g_out_a", "g_out_b", "g_out_m", "g_mem", "g_post_mix", "g_pre_ffn", "g_post_ffn")
VEC_ROWS = 16
WS_ROWS = N_FOX_HEADS * CHUNK
BS_ROWS = 8


def _pack_small(small, vw):
    stack = jnp.zeros((VEC_ROWS, vw), F32)
    for k, n in enumerate(VECTORS + ("loss",)):
        row = small[n].reshape(1, -1)
        stack = stack + jnp.pad(row, ((k, VEC_ROWS - 1 - k), (0, vw - row.shape[1])))
    parts = [small["w_s"].reshape(WS_ROWS, LANES), jnp.pad(small["b_s"], ((0, BS_ROWS - N_FOX_HEADS), (0, 0))),
             stack.reshape(-1, LANES)]
    rows = sum(p.shape[0] for p in parts)
    return jnp.concatenate(parts + [jnp.zeros((-rows % 64, LANES), F32)], axis=0)


def _adamw_small(vec_g, vec_wmv, ws, bs):
    c1 = 1.0 / (1.0 - ADAM_B1 ** ADAM_STEP)
    c2 = 1.0 / (1.0 - ADAM_B2 ** ADAM_STEP)
    nv = len(vec_wmv)

    def adam(g, w, m, v):
        m_ = ADAM_B1 * m + (1.0 - ADAM_B1) * g
        v_ = ADAM_B2 * v + (1.0 - ADAM_B2) * (g * g)
        return -ADAM_LR * ((m_ * c1) / (jnp.sqrt(v_ * c2) + ADAM_EPS) + ADAM_WD * w), m_, v_

    def body(*refs):
        vg_ref, ins, outs = refs[0], refs[1:1 + 3 * nv + 8], refs[1 + 3 * nv + 8:]
        for k in range(nv):
            w_ref, m_ref, v_ref = ins[3 * k:3 * k + 3]
            g = vg_ref[k:k + 1, 0:w_ref.shape[1]]
            d, m_, v_ = adam(g, w_ref[...], m_ref[...], v_ref[...])
            for o_ref, val in zip(outs[4 * k:4 * k + 4], (g, d, m_, v_)):
                o_ref[...] = val
        for j in range(2):
            g_ref, w_ref, m_ref, v_ref = ins[3 * nv + 4 * j:3 * nv + 4 * j + 4]
            for o_ref, val in zip(outs[4 * nv + 3 * j:4 * nv + 3 * j + 3], adam(g_ref[...], w_ref[...], m_ref[...], v_ref[...])):
                o_ref[...] = val

    vm = pl.BlockSpec(memory_space=pltpu.VMEM)
    operands = [vec_g] + [a for wmv in vec_wmv for a in wmv] + list(ws) + list(bs)
    out_shape = ([jax.ShapeDtypeStruct(wmv[0].shape, F32) for wmv in vec_wmv for _ in range(4)]
                 + [jax.ShapeDtypeStruct(ws[1].shape, F32)] * 3 + [jax.ShapeDtypeStruct(bs[1].shape, F32)] * 3)
    outs = pl.pallas_call(body, name="adamw_small", in_specs=[vm] * len(operands), out_specs=[vm] * len(out_shape),
                          out_shape=out_shape)(*operands)
    return [outs[4 * k:4 * k + 4] for k in range(nv)], outs[4 * nv:4 * nv + 3], outs[4 * nv + 3:]


def kernel(x, mem, g_pre_mix, w_in, b_f, g_sgu, w_s, b_s, g_out_a, g_out_b, g_out_m, g_mem, w_mem_kv, w_out, g_post_mix, g_pre_ffn, w_gate, w_up, w_down, g_post_ffn, loss_target, m_g_pre_mix, m_w_in, m_b_f, m_g_sgu, m_w_s, m_b_s, m_g_out_a, m_g_out_b, m_g_out_m, m_g_mem, m_w_mem_kv, m_w_out, m_g_post_mix, m_g_pre_ffn, m_w_gate, m_w_up, m_w_down, m_g_post_ffn, v_g_pre_mix, v_w_in, v_b_f, v_g_sgu, v_w_s, v_b_s, v_g_out_a, v_g_out_b, v_g_out_m, v_g_mem, v_w_mem_kv, v_w_out, v_g_post_mix, v_g_pre_ffn, v_w_gate, v_w_up, v_w_down, v_g_post_ffn):
    Wt = dict(g_pre_mix=g_pre_mix, w_in=w_in, b_f=b_f, g_sgu=g_sgu, w_s=w_s, b_s=b_s, g_out_a=g_out_a, g_out_b=g_out_b,
              g_out_m=g_out_m, g_mem=g_mem, w_mem_kv=w_mem_kv, w_out=w_out, g_post_mix=g_post_mix, g_pre_ffn=g_pre_ffn,
              w_gate=w_gate, w_up=w_up, w_down=w_down, g_post_ffn=g_post_ffn)
    Mo = dict(g_pre_mix=m_g_pre_mix, w_in=m_w_in, b_f=m_b_f, g_sgu=m_g_sgu, w_s=m_w_s, b_s=m_b_s, g_out_a=m_g_out_a,
              g_out_b=m_g_out_b, g_out_m=m_g_out_m, g_mem=m_g_mem, w_mem_kv=m_w_mem_kv, w_out=m_w_out,
              g_post_mix=m_g_post_mix, g_pre_ffn=m_g_pre_ffn, w_gate=m_w_gate, w_up=m_w_up, w_down=m_w_down,
              g_post_ffn=m_g_post_ffn)
    Vo = dict(g_pre_mix=v_g_pre_mix, w_in=v_w_in, b_f=v_b_f, g_sgu=v_g_sgu, w_s=v_w_s, b_s=v_b_s, g_out_a=v_g_out_a,
              g_out_b=v_g_out_b, g_out_m=v_g_out_m, g_mem=v_g_mem, w_mem_kv=v_w_mem_kv, w_out=v_w_out,
              g_post_mix=v_g_post_mix, g_pre_ffn=v_g_pre_ffn, w_gate=v_w_gate, w_up=v_w_up, w_down=v_w_down,
              g_post_ffn=v_g_post_ffn)

    gap = P_COLS - IN_COLS

    def to_kernel(n, w):
        if n in TRANSPOSED:
            w = w.T
        if n == "w_in":
            w = jnp.pad(w[:F_END], ((0, P_COLS - F_END), (0, 0))) + jnp.pad(w[F_END:], ((F_END + gap, 0), (0, 0)))
        return w

    def ungroup(g):
        return jnp.pad(g[:F_END], ((0, IN_COLS - F_END), (0, 0))) + jnp.pad(g[F_END + gap:], ((F_END, 0), (0, 0)))

    shards = {n: to_kernel(n, Wt[n][0]) for n in BIG}
    srcs = [shards[n].astype(BF16).reshape(2, shards[n].shape[0] // 2, shards[n].shape[1]) for n in BIG]
    fulls = (_exchange_on_sequencer(srcs[:1], True, "gather_w_in", 1)
             + _exchange_on_sequencer(srcs[1:3], True, "gather_kv_out", 2)
             + _exchange_on_sequencer(srcs[3:], True, "gather_ffn", 3))
    W = {}
    for n, f in zip(BIG, fulls):
        _, _, hr, C = f.shape
        W[n] = f.reshape(8 * hr, C) if n in ("w_mem_kv", "w_out") else f.reshape(4, 2 * hr, C)

    P = {n: Wt[n] for n in SMALL}
    grads, deltas, new_m, new_v = {}, {}, {}, {}

    def apply(names, landed):
        for n, r in zip(names, landed):
            if n == "w_in":
                g_t = ungroup(_sum_chips(r, "sum_chips_" + n))
                g, d, m1, v1 = _adamw_from_transposed(Wt[n][0], g_t, Mo[n][0], Vo[n][0], "adamw_" + n)
            elif n in TRANSPOSED:
                g, d, m1, v1 = [a.T for a in _adamw(Wt[n][0].T, r, Mo[n][0].T, Vo[n][0].T, "adamw_" + n)]
            else:
                g, d, m1, v1 = _adamw(Wt[n][0], r, Mo[n][0], Vo[n][0], "adamw_" + n)
            grads[n], deltas[n], new_m[n], new_v[n] = g[None], d[None], m1[None], v1[None]
        return tuple(deltas[n] for n in names)

    core = lax.axis_index("c").astype(jnp.int32).reshape(1)
    reducer = _Reducer(core, apply)
    grad_x, pending, small = _local_step(x, mem, loss_target, W, P, reducer)

    vw = -(-max(x.shape[-1], A_W) // LANES) * LANES
    total = _small_allreduce(_pack_small(small, vw))
    landed, (total,) = reducer.finish("in", pending["w_in"], (total,))
    apply(BIG[:1], landed)

    lane_row = lambda a: jnp.pad(a, ((0, 0), (0, -a.shape[1] % LANES)))
    vec_g = total[WS_ROWS + BS_ROWS:WS_ROWS + BS_ROWS + VEC_ROWS * vw // LANES].reshape(VEC_ROWS, vw)
    ws_g = total[:WS_ROWS]
    bs_g = total[WS_ROWS:WS_ROWS + N_FOX_HEADS]
    rows = lambda a, r: a.reshape(r, LANES)
    per_vec, ws_out, bs_out = _adamw_small(
        vec_g, [tuple(lane_row(a[n]) for a in (Wt, Mo, Vo)) for n in VECTORS],
        (ws_g,) + tuple(rows(a["w_s"], WS_ROWS) for a in (Wt, Mo, Vo)),
        (bs_g,) + tuple(rows(a["b_s"], N_FOX_HEADS) for a in (Wt, Mo, Vo)))
    for n, outs in zip(VECTORS, per_vec):
        grads[n], deltas[n], new_m[n], new_v[n] = [o[:, :Wt[n].shape[1]] for o in outs]
    for n, g, outs in (("w_s", ws_g, ws_out), ("b_s", bs_g, bs_out)):
        grads[n], deltas[n], new_m[n], new_v[n] = [o.reshape(Wt[n].shape) for o in (g,) + tuple(outs)]
    loss = vec_g[len(VECTORS), 0]

    return (loss, grad_x, *[grads[n] for n in WEIGHTS], *[deltas[n] for n in WEIGHTS],
            *[new_m[n] for n in WEIGHTS], *[new_v[n] for n in WEIGHTS])
```

```python
import functools

import jax
import jax.numpy as jnp
from jax import lax
from jax.experimental import pallas as pl
from jax.experimental.pallas import tpu as pltpu
from jax.experimental.pallas import tpu_sc as plsc

F32 = jnp.float32
BF16 = jnp.bfloat16
EPS = 1e-6
NEG = -1e30
HEAD = 64
A_W, B_W, M_W = 384, 384, 256
N_FOX_HEADS = 6
CHUNK = 128
IN_COLS = 2 * A_W + 3 * B_W + N_FOX_HEADS + M_W
P_MAIN = 2 * A_W + 3 * B_W + M_W
P_COLS = P_MAIN + 128
F_END = 2 * A_W + 3 * B_W + N_FOX_HEADS
LANES = 128
Q_BLK, K_BLK = 512, 128
ROW_SPLIT = 4
ADAM_LR, ADAM_B1, ADAM_B2, ADAM_EPS, ADAM_WD, ADAM_STEP = 0.001, 0.9, 0.999, 1e-08, 0.01, 10
VMEM_LIMIT = 56 * 1024 * 1024
MESH = pl.DeviceIdType.MESH
ANY = pl.BlockSpec(memory_space=pl.ANY)
BS = pl.BlockSpec


def _cp(sem=None):
    return pltpu.CompilerParams(dimension_semantics=sem, vmem_limit_bytes=VMEM_LIMIT)


def _iota(shape, dim):
    return lax.broadcasted_iota(jnp.int32, shape, dim)


def _dot(a, b):
    return jnp.dot(a.astype(BF16), b.astype(BF16), preferred_element_type=F32)


def _dot_nt(a, b):
    return lax.dot_general(a.astype(BF16), b.astype(BF16), (((1,), (1,)), ((), ())), preferred_element_type=F32)


def _dot_tn(a, b):
    return lax.dot_general(a.astype(BF16), b.astype(BF16), (((0,), (0,)), ((), ())), preferred_element_type=F32)


def _rms(x, g):
    return x * lax.rsqrt(jnp.mean(x * x, axis=-1, keepdims=True) + EPS) * g


def _rms_bwd(x, g, dy):
    r = lax.rsqrt(jnp.mean(x * x, axis=-1, keepdims=True) + EPS)
    xr = x * r
    gd = dy * g
    m = jnp.mean(gd * xr, axis=-1, keepdims=True)
    return (gd - xr * m) * r, _colsum(dy * xr)


def _gelu(x):
    return 0.5 * x * (1.0 + jnp.tanh(0.7978845608028654 * (x + 0.044715 * (x * x * x))))


def _sigmoid(x):
    return 1.0 / (1.0 + jnp.exp(-x))


def _silu_mul(g, u):
    return g * _sigmoid(g) * u


def _logsig(x):
    return jnp.minimum(x, 0.0) - jnp.log(1.0 + jnp.exp(-jnp.abs(x)))


def _colsum(x):
    return jnp.sum(x, axis=0, keepdims=True)


def _acc(ref, val, first):
    @pl.when(first)
    def _():
        ref[...] = val

    @pl.when(jnp.logical_not(first))
    def _():
        ref[...] += val


def _inproj_fwd(x2d, g_pre, w_in_p, tm):
    T, D = x2d.shape
    CH = 768
    nchunk = P_COLS // CH
    ns, _, dsh = w_in_p.shape

    def body(x_ref, g_ref, w_ref, h_ref, proj_ref, fl_ref):
        h = _rms(x_ref[...], g_ref[...]).astype(BF16)
        h_ref[...] = h
        for n in range(nchunk):
            rows = slice(n * CH, (n + 1) * CH)
            r = _dot_nt(h[:, 0:dsh], w_ref[0, rows, :])
            for s in range(1, ns):
                r = r + _dot_nt(h[:, s * dsh:(s + 1) * dsh], w_ref[s, rows, :])
            if n < nchunk - 1:
                proj_ref[:, rows] = r.astype(BF16)
            else:
                fg = 1920 - n * CH
                proj_ref[:, n * CH:1920] = r[:, :fg].astype(BF16)
                fl_ref[...] = r[:, fg:fg + LANES]
                proj_ref[:, 1920:P_MAIN] = r[:, fg + LANES:].astype(BF16)

    return pl.pallas_call(
        body, name="inproj_fwd", grid=(T // tm,),
        in_specs=[BS((tm, D), lambda i: (i, 0)), BS((1, D), lambda i: (0, 0)),
                  BS((ns, P_COLS, dsh), lambda i: (0, 0, 0))],
        out_specs=[BS((tm, D), lambda i: (i, 0)), BS((tm, P_MAIN), lambda i: (i, 0)), BS((tm, LANES), lambda i: (i, 0))],
        out_shape=[jax.ShapeDtypeStruct((T, D), BF16), jax.ShapeDtypeStruct((T, P_MAIN), BF16),
                   jax.ShapeDtypeStruct((T, LANES), F32)],
        compiler_params=_cp(("arbitrary",)),
    )(x2d, g_pre, w_in_p)


def _gate_fwd(flog3, bf_row):
    Bl, S, _ = flog3.shape
    nb = S // LANES

    def body(f_ref, b_ref, bq_ref, bk_ref, fr_ref):
        row = _iota((LANES, LANES), 0)
        lane = _iota((LANES, LANES), 1)
        one = jnp.ones((LANES, LANES), BF16)
        zero = jnp.zeros((LANES, LANES), BF16)

        carry = jnp.zeros((1, LANES), F32)
        for j in range(nb):
            r0 = j * LANES
            fl = f_ref[0, pl.ds(r0, LANES), :] + b_ref[...]
            fr_ref[0, j] = fl.T[0:8, :]
            c = _logsig(fl)
            for k in (1, 2, 4, 8, 16, 32, 64):
                c = c + jnp.where(row >= k, pltpu.roll(c, k, 0), 0.0)
            total = _colsum(jnp.where(row == LANES - 1, c, 0.0))
            c = c + carry
            carry = carry + total
            for h in range(N_FOX_HEADS):
                col = jnp.sum(jnp.where(lane == h, c, 0.0), axis=1, keepdims=True)
                hi = col.astype(BF16)
                rest = col - hi.astype(F32)
                mid = rest.astype(BF16)
                lo = (rest - mid.astype(F32)).astype(BF16)
                base = _bias_lane(h)
                bq = jnp.where(lane == base, hi, jnp.where(lane == base + 1, mid, jnp.where(lane == base + 2, lo, zero)))
                bq = jnp.where((lane >= base + 3) & (lane < base + 6), one, bq)
                bk = jnp.where(lane == base + 3, -hi, jnp.where(lane == base + 4, -mid, jnp.where(lane == base + 5, -lo, zero)))
                bk = jnp.where((lane >= base) & (lane < base + 3), one, bk)
                bq_ref[0, h, pl.ds(r0, LANES), :] = bq
                bk_ref[0, h, pl.ds(r0, LANES), :] = bk

    slab = BS((1, N_FOX_HEADS, S, LANES), lambda b: (b, 0, 0, 0))
    return pl.pallas_call(
        body, name="gate_fwd", grid=(Bl,),
        in_specs=[BS((1, S, LANES), lambda b: (b, 0, 0)), BS((1, LANES), lambda b: (0, 0))],
        out_specs=[slab, slab, BS((1, nb, 8, LANES), lambda b: (b, 0, 0, 0))],
        out_shape=[jax.ShapeDtypeStruct((Bl, N_FOX_HEADS, S, LANES), BF16),
                   jax.ShapeDtypeStruct((Bl, N_FOX_HEADS, S, LANES), BF16),
                   jax.ShapeDtypeStruct((Bl, nb, 8, LANES), F32)],
        compiler_params=_cp(("arbitrary",)),
    )(flog3, bf_row)


def _bias_lane(h):
    return HEAD if h % 2 == 0 else 0


def _sgu_pre(zu, zv, g_sgu):
    return _gelu(zu), _rms(_gelu(zv), g_sgu)


def _sgu_fwd(proj, g_sgu, ws_tril, bs_full, tm):
    T = proj.shape[0]
    nch = tm // CHUNK

    def body(zu_ref, zv_ref, g_ref, ws_ref, b_ref, ya_ref):
        lane = _iota((CHUNK, LANES), 1)
        u, vn = _sgu_pre(zu_ref[...].astype(F32), zv_ref[...].astype(F32), g_ref[...])
        vn = vn.astype(BF16)
        for c in range(nch):
            rs = slice(c * CHUNK, (c + 1) * CHUNK)
            for j in range(3):
                cs = slice(j * LANES, (j + 1) * LANES)
                vp = vn[rs, cs]
                z = jnp.where(lane < HEAD, _dot(ws_ref[2 * j], vp), _dot(ws_ref[2 * j + 1], vp)) + b_ref[:, cs]
                ya_ref[rs, cs] = (u[rs, cs] * z).astype(BF16)

    return pl.pallas_call(
        body, name="sgu_fwd", grid=(T // tm,),
        in_specs=[BS((tm, A_W), lambda i: (i, 0)), BS((tm, A_W), lambda i: (i, 1)), BS((1, A_W), lambda i: (0, 0)),
                  BS((6, CHUNK, CHUNK), lambda i: (0, 0, 0)), BS((CHUNK, A_W), lambda i: (0, 0))],
        out_specs=BS((tm, A_W), lambda i: (i, 0)),
        out_shape=jax.ShapeDtypeStruct((T, A_W), BF16),
        compiler_params=_cp(("arbitrary",)),
    )(proj, proj, g_sgu, ws_tril, bs_full)


def _fox_fwd(proj, bq, bk, Bl, S):
    T = Bl * S
    nq = S // Q_BLK
    qc, kc, vc = 768 // LANES, 1152 // LANES, 1536 // LANES

    def body(q_ref, k_ref, v_ref, bq_ref, bk_ref, o_ref, lse_ref, ka_ref, va_ref):
        lane_s = _iota((S, LANES), 1)
        lane = _iota((Q_BLK, LANES), 1)
        tri = _iota((Q_BLK, Q_BLK), 1) <= _iota((Q_BLK, Q_BLK), 0)
        k = k_ref[...]
        v = v_ref[...]
        for hh in range(2):
            data = (lane_s < HEAD) if hh == 0 else (lane_s >= HEAD)
            ka_ref[hh] = jnp.where(data, k, bk_ref[0, hh])
            va_ref[hh] = jnp.where(lane_s == _bias_lane(hh), jnp.ones_like(v), v)
        for i in range(nq):
            r0 = i * Q_BLK
            q = q_ref[r0:r0 + Q_BLK, :]
            o_out = jnp.zeros((Q_BLK, LANES), F32)
            lse_out = jnp.zeros((Q_BLK, LANES), F32)
            for hh in range(2):
                hmask = (lane < HEAD) if hh == 0 else (lane >= HEAD)
                qa = jnp.where(hmask, q * 0.125, bq_ref[0, hh, r0:r0 + Q_BLK, :])
                sd = jnp.where(tri, _dot_nt(qa, ka_ref[hh, r0:r0 + Q_BLK, :]), NEG)
                m = jnp.max(sd, axis=1, keepdims=True)
                if i:
                    sf = _dot_nt(qa, ka_ref[hh, 0:r0, :])
                    m = jnp.maximum(m, jnp.max(sf, axis=1, keepdims=True))
                acc = _dot(jnp.exp(sd - m), va_ref[hh, r0:r0 + Q_BLK, :])
                if i:
                    acc = acc + _dot(jnp.exp(sf - m), va_ref[hh, 0:r0, :])
                l = jnp.sum(jnp.where(lane == _bias_lane(hh), acc, 0.0), axis=1, keepdims=True)
                o_out = jnp.where(hmask, acc / l, o_out)
                lse_out = jnp.where(hmask, m + jnp.log(l), lse_out)
            o_ref[r0:r0 + Q_BLK, :] = o_out.astype(BF16)
            lse_ref[0, r0:r0 + Q_BLK, :] = lse_out

    seq = lambda c0: BS((S, LANES), lambda b, p: (b, c0 + p))
    pair = BS((1, 2, S, LANES), lambda b, p: (b, p, 0, 0))
    return pl.pallas_call(
        body, name="fox_fwd", grid=(Bl, 3),
        in_specs=[seq(qc), seq(kc), seq(vc), pair, pair],
        out_specs=[seq(0), BS((1, S, LANES), lambda b, p: (p, b, 0))],
        out_shape=[jax.ShapeDtypeStruct((T, B_W), BF16), jax.ShapeDtypeStruct((3, T, LANES), F32)],
        scratch_shapes=[pltpu.VMEM((2, S, LANES), BF16), pltpu.VMEM((2, S, LANES), BF16)],
        compiler_params=_cp(("arbitrary", "arbitrary")),
    )(proj, proj, proj, bq, bk)


def _memkv_fwd(mem, g_mem, w_kv):
    Bl, Mt, D = mem.shape

    def body(m_ref, g_ref, w_ref, mn_ref, kv_ref):
        mn = _rms(m_ref[0], g_ref[...]).astype(BF16)
        mn_ref[0] = mn
        kv_ref[0] = jnp.dot(mn, w_ref[...], preferred_element_type=F32).astype(BF16)

    return pl.pallas_call(
        body, name="memkv_fwd", grid=(Bl,),
        in_specs=[BS((1, Mt, D), lambda b: (b, 0, 0)), BS((1, D), lambda b: (0, 0)), BS((D, 2 * M_W), lambda b: (0, 0))],
        out_specs=[BS((1, Mt, D), lambda b: (b, 0, 0)), BS((1, Mt, 2 * M_W), lambda b: (b, 0, 0))],
        out_shape=[jax.ShapeDtypeStruct((Bl, Mt, D), BF16), jax.ShapeDtypeStruct((Bl, Mt, 2 * M_W), BF16)],
        compiler_params=_cp(("arbitrary",)),
    )(mem, g_mem, w_kv)


def _memattn_fwd(proj, kv, Bl, S, tq):
    T = Bl * S
    nq = S // tq
    Mt = kv.shape[1]
    qc = 1920 // LANES

    def body(q_ref, km_ref, vm_ref, o_ref):
        lane = _iota((tq, LANES), 1)
        q = q_ref[...]
        out = jnp.zeros((tq, LANES), F32)
        for hh in range(2):
            hmask = (lane < HEAD) if hh == 0 else (lane >= HEAD)
            qs = jnp.where(hmask, q, jnp.zeros_like(q)) * 0.125
            s = _dot_nt(qs, km_ref[0])
            pe = jnp.exp(s - jnp.max(s, axis=1, keepdims=True))
            pn = pe / jnp.sum(pe, axis=1, keepdims=True)
            out = jnp.where(hmask, _dot(pn, vm_ref[0]), out)
        o_ref[...] = out.astype(BF16)

    return pl.pallas_call(
        body, name="memattn_fwd", grid=(Bl, 2, nq),
        in_specs=[BS((tq, LANES), lambda b, p, i: (b * nq + i, qc + p)),
                  BS((1, Mt, LANES), lambda b, p, i: (b, 0, p)),
                  BS((1, Mt, LANES), lambda b, p, i: (b, 0, 2 + p))],
        out_specs=BS((tq, LANES), lambda b, p, i: (b * nq + i, p)),
        out_shape=jax.ShapeDtypeStruct((T, M_W), BF16),
        compiler_params=_cp(("arbitrary", "arbitrary", "arbitrary")),
    )(proj, kv, kv)


def _mix_norms(ya, yb, ym, ga, gb, gm):
    return _rms(ya, ga), _rms(yb, gb), _rms(ym, gm)


def _outproj_fwd(ya, yb, ym, x2d, ga, gb, gm, g_post, g_pre2, w_out, tm):
    T, D = x2d.shape

    def body(ya_ref, yb_ref, ym_ref, x_ref, ga_ref, gb_ref, gm_ref, gp_ref, g2_ref, w_ref,
             y_ref, o_ref, x1_ref, h2_ref):
        na, nb_, nm = _mix_norms(ya_ref[...].astype(F32), yb_ref[...].astype(F32), ym_ref[...].astype(F32),
                                 ga_ref[...], gb_ref[...], gm_ref[...])
        y_ref[:, 0:A_W] = na.astype(BF16)
        y_ref[:, A_W:A_W + B_W] = nb_.astype(BF16)
        y_ref[:, A_W + B_W:] = nm.astype(BF16)
        o = jnp.dot(y_ref[...], w_ref[...], preferred_element_type=F32).astype(BF16)
        o_ref[...] = o
        x1 = x_ref[...] + _rms(o.astype(F32), gp_ref[...])
        x1_ref[...] = x1
        h2_ref[...] = _rms(x1, g2_ref[...]).astype(BF16)

    row = lambda w: BS((tm, w), lambda i: (i, 0))
    vec = lambda w: BS((1, w), lambda i: (0, 0))
    return pl.pallas_call(
        body, name="outproj_fwd", grid=(T // tm,),
        in_specs=[row(A_W), row(B_W), row(M_W), row(D), vec(A_W), vec(B_W), vec(M_W), vec(D), vec(D),
                  BS((A_W + B_W + M_W, D), lambda i: (0, 0))],
        out_specs=[row(A_W + B_W + M_W), row(D), row(D), row(D)],
        out_shape=[jax.ShapeDtypeStruct((T, A_W + B_W + M_W), BF16), jax.ShapeDtypeStruct((T, D), BF16),
                   jax.ShapeDtypeStruct((T, D), F32), jax.ShapeDtypeStruct((T, D), BF16)],
        compiler_params=_cp(("arbitrary",)),
    )(ya, yb, ym, x2d, ga, gb, gm, g_post, g_pre2, w_out)


def _ffn_fwd(h2, x1, target, wg, wu, wd, g_post, tm):
    T, D = x1.shape
    ns, F, _ = wg.shape

    def body(h_ref, x1_ref, t_ref, wg_ref, wu_ref, wd_ref, gp_ref,
             gs_ref, us_ref, dff_ref, dx2_ref, dgp_ref, loss_ref, acc_ref):
        j = pl.program_id(0)
        i = pl.program_id(1)
        rows = pl.ds(pl.multiple_of(i * tm, tm), tm)
        h = h_ref[...]
        g = _dot_nt(h, wg_ref[0])
        u = _dot_nt(h, wu_ref[0])
        gs_ref[0] = g.astype(BF16)
        us_ref[0] = u.astype(BF16)
        part = _dot(_silu_mul(g, u), wd_ref[0])

        @pl.when(j == 0)
        def _():
            acc_ref[rows, :] = part

        @pl.when(j != 0)
        def _():
            acc_ref[rows, :] += part

        @pl.when(j == ns - 1)
        def _():
            ff = acc_ref[rows, :]
            diff = x1_ref[...] + _rms(ff, gp_ref[...]) - t_ref[...]
            dx2 = diff * (1.0 / D)
            dff, dgp = _rms_bwd(ff, gp_ref[...], dx2)
            dx2_ref[...] = dx2
            dff_ref[...] = dff.astype(BF16)
            lpart = jnp.sum(_colsum(diff * diff), axis=1, keepdims=True) * (0.5 / D)
            _acc(dgp_ref, dgp, i == 0)
            _acc(loss_ref, jnp.broadcast_to(lpart, (1, LANES)), i == 0)

    last = lambda j, i: (jnp.where(j == ns - 1, i, 0), 0)
    wsh = BS((1, F, D), lambda j, i: (j, 0, 0))
    sh = BS((1, tm, F), lambda j, i: (j, i, 0))
    return pl.pallas_call(
        body, name="ffn_fwd", grid=(ns, T // tm),
        in_specs=[BS((tm, D), lambda j, i: (i, 0)), BS((tm, D), last), BS((tm, D), last), wsh, wsh, wsh,
                  BS((1, D), lambda j, i: (0, 0))],
        out_specs=[sh, sh, BS((tm, D), last), BS((tm, D), last),
                   BS((1, D), lambda j, i: (0, 0)), BS((1, LANES), lambda j, i: (0, 0))],
        out_shape=[jax.ShapeDtypeStruct((ns, T, F), BF16), jax.ShapeDtypeStruct((ns, T, F), BF16),
                   jax.ShapeDtypeStruct((T, D), BF16), jax.ShapeDtypeStruct((T, D), F32),
                   jax.ShapeDtypeStruct((1, D), F32), jax.ShapeDtypeStruct((1, LANES), F32)],
        scratch_shapes=[pltpu.VMEM((T, D), F32)],
        compiler_params=_cp(("arbitrary", "arbitrary")),
    )(h2, x1, target, wg, wu, wd, g_post)


def _ffn_bwd(dff, h2, gs, us, wg, wu, wd, tm):
    T, D = h2.shape
    ns, F, _ = wg.shape

    def body(dff_ref, h_ref, gs_ref, us_ref, wg_ref, wu_ref, wd_ref, dh_ref, dwg_out, dwu_out, dwd_out,
             dwg_ref, dwu_ref, dwd_ref):
        first = pl.program_id(1) == 0
        dff = dff_ref[...]
        h = h_ref[...]
        parts = []
        for r in range(ROW_SPLIT):
            rows = slice(r * (tm // ROW_SPLIT), (r + 1) * (tm // ROW_SPLIT))
            dact = _dot_nt(dff[rows], wd_ref[0])
            g = gs_ref[0, rows, :].astype(F32)
            u = us_ref[0, rows, :].astype(F32)
            sig = _sigmoid(g)
            gsig = g * sig
            dg = (dact * u * (sig + gsig * (1.0 - sig))).astype(BF16)
            du = (dact * gsig).astype(BF16)
            dh_ref[0, rows, :] = (_dot(dg, wg_ref[0]) + _dot(du, wu_ref[0])).astype(BF16)
            parts.append(((gsig * u).astype(BF16), dg, du))
        a, dg, du = [jnp.concatenate(p, axis=0) for p in zip(*parts)]
        _acc(dwd_ref, _dot_tn(a, dff), first)
        _acc(dwg_ref, _dot_tn(dg, h), first)
        _acc(dwu_ref, _dot_tn(du, h), first)

        @pl.when(pl.program_id(1) == pl.num_programs(1) - 1)
        def _():
            dwg_out[0] = dwg_ref[...].astype(BF16)
            dwu_out[0] = dwu_ref[...].astype(BF16)
            dwd_out[0] = dwd_ref[...].astype(BF16)

    row = BS((tm, D), lambda j, i: (i, 0))
    sh = BS((1, tm, F), lambda j, i: (j, i, 0))
    wsh = BS((1, F, D), lambda j, i: (j, 0, 0))
    return pl.pallas_call(
        body, name="ffn_bwd", grid=(ns, T // tm),
        in_specs=[row, row, sh, sh, wsh, wsh, wsh],
        out_specs=[BS((1, tm, D), lambda j, i: (j, i, 0)), wsh, wsh, wsh],
        out_shape=[jax.ShapeDtypeStruct((ns, T, D), BF16)] + [jax.ShapeDtypeStruct((ns, F, D), BF16)] * 3,
        scratch_shapes=[pltpu.VMEM((F, D), F32)] * 3,
        compiler_params=_cp(("arbitrary", "arbitrary")),
    )(dff, h2, gs, us, wg, wu, wd)


def _mm_tn(a, b, name, tk):
    T, M = a.shape
    N = b.shape[1]
    tk = min(tk, T)

    def body(a_ref, b_ref, o_ref):
        _acc(o_ref, _dot_tn(a_ref[...], b_ref[...]), pl.program_id(0) == 0)

    return pl.pallas_call(
        body, name=name, grid=(T // tk,),
        in_specs=[BS((tk, M), lambda t: (t, 0)), BS((tk, N), lambda t: (t, 0))],
        out_specs=BS((M, N), lambda t: (0, 0)),
        out_shape=jax.ShapeDtypeStruct((M, N), F32),
        compiler_params=_cp(("arbitrary",)),
    )(a, b)


DPROJ_PIECES = ((0, A_W), (A_W, A_W), (768, B_W), (1152, B_W), (1536, B_W), (1920, LANES), (2048, M_W))


def _put_dproj(dp_ref, piece_refs):
    for (c0, w), ref in zip(DPROJ_PIECES, piece_refs):
        dp_ref[:, c0:c0 + w] = ref[...].astype(BF16)


def _dw_in(pieces, h, ns, tk):
    T, D = h.shape
    M = P_COLS
    dsh = D // ns
    tk = min(tk, T)

    def body(*refs):
        piece_refs, h_ref, o_ref, acc_ref, dp_ref = refs[:7], refs[7], refs[8], refs[9], refs[10]
        t = pl.program_id(0)
        _put_dproj(dp_ref, piece_refs)
        _acc(acc_ref, _dot_tn(h_ref[...], dp_ref[...]), t == 0)

        @pl.when(t == pl.num_programs(0) - 1)
        def _():
            for s in range(ns):
                o_ref[s] = acc_ref[s * dsh:(s + 1) * dsh, :].T.astype(BF16)

    return pl.pallas_call(
        body, name="dw_in", grid=(T // tk,),
        in_specs=[BS((tk, w), lambda t: (t, 0)) for _, w in DPROJ_PIECES] + [BS((tk, D), lambda t: (t, 0))],
        out_specs=BS((ns, M, dsh), lambda t: (0, 0, 0)),
        out_shape=jax.ShapeDtypeStruct((ns, M, dsh), BF16),
        scratch_shapes=[pltpu.VMEM((D, M), F32), pltpu.VMEM((tk, M), BF16)],
        compiler_params=_cp(("arbitrary",)),
    )(*pieces, h)


def _outproj_bwd(dh2, x1, dx2, o, ya, yb, ym, ga, gb, gm, g_post, g_pre2, w_out, tm):
    T, D = x1.shape
    ns = dh2.shape[0]

    def body(dh_ref, x1_ref, dx2_ref, o_ref, ya_ref, yb_ref, ym_ref, ga_ref, gb_ref, gm_ref, gp_ref, g2_ref, w_ref,
             dx1_ref, do_ref, dya_ref, dyb_ref, dym_ref, dga_ref, dgb_ref, dgm_ref, dgp_ref, dg2_ref):
        first = pl.program_id(0) == 0
        dh = dh_ref[0].astype(F32)
        for j in range(1, ns):
            dh = dh + dh_ref[j].astype(F32)
        dxa, dg2 = _rms_bwd(x1_ref[...], g2_ref[...], dh)
        dx1 = dx2_ref[...] + dxa
        dx1_ref[...] = dx1
        _acc(dg2_ref, dg2, first)
        do, dgp = _rms_bwd(o_ref[...].astype(F32), gp_ref[...], dx1)
        do = do.astype(BF16)
        do_ref[...] = do
        dy = _dot_nt(do, w_ref[...])
        dya, dga = _rms_bwd(ya_ref[...].astype(F32), ga_ref[...], dy[:, 0:A_W])
        dyb, dgb = _rms_bwd(yb_ref[...].astype(F32), gb_ref[...], dy[:, A_W:A_W + B_W])
        dym, dgm = _rms_bwd(ym_ref[...].astype(F32), gm_ref[...], dy[:, A_W + B_W:])
        dya_ref[...] = dya.astype(BF16)
        dyb_ref[...] = dyb.astype(BF16)
        dym_ref[...] = dym.astype(BF16)
        _acc(dga_ref, dga, first)
        _acc(dgb_ref, dgb, first)
        _acc(dgm_ref, dgm, first)
        _acc(dgp_ref, dgp, first)

    row = lambda w: BS((tm, w), lambda i: (i, 0))
    vec = lambda w: BS((1, w), lambda i: (0, 0))
    sds = jax.ShapeDtypeStruct
    return pl.pallas_call(
        body, name="outproj_bwd", grid=(T // tm,),
        in_specs=[BS((ns, tm, D), lambda i: (0, i, 0)), row(D), row(D), row(D), row(A_W), row(B_W), row(M_W),
                  vec(A_W), vec(B_W), vec(M_W), vec(D), vec(D), BS((A_W + B_W + M_W, D), lambda i: (0, 0))],
        out_specs=[row(D), row(D), row(A_W), row(B_W), row(M_W), vec(A_W), vec(B_W), vec(M_W), vec(D), vec(D)],
        out_shape=[sds((T, D), F32), sds((T, D), BF16), sds((T, A_W), BF16), sds((T, B_W), BF16), sds((T, M_W), BF16),
                   sds((1, A_W), F32), sds((1, B_W), F32), sds((1, M_W), F32), sds((1, D), F32), sds((1, D), F32)],
        compiler_params=_cp(("arbitrary",)),
    )(dh2, x1, dx2, o, ya, yb, ym, ga, gb, gm, g_post, g_pre2, w_out)


def _sgu_bwd(proj, dya, g_sgu, ws_tril, bs_full, tm):
    T = proj.shape[0]
    nch = tm // CHUNK

    def body(zu_ref, zv_ref, dy_ref, g_ref, ws_ref, b_ref, dzu_ref, dzv_ref, dws_ref, dbs_ref, dg_ref,
             du_ref, dvn_ref, dbf_ref):
        step = pl.program_id(0)
        first = step == 0
        lane = _iota((CHUNK, LANES), 1)
        tril = _iota((CHUNK, CHUNK), 0) >= _iota((CHUNK, CHUNK), 1)
        (u, vn), vjp = jax.vjp(_sgu_pre, zu_ref[...].astype(F32), zv_ref[...].astype(F32), g_ref[...])
        vnb = vn.astype(BF16)
        dy = dy_ref[...].astype(F32)

        @pl.when(first)
        def _():
            dws_ref[...] = jnp.zeros_like(dws_ref)
            dbf_ref[...] = jnp.zeros_like(dbf_ref)

        for c in range(nch):
            rs = slice(c * CHUNK, (c + 1) * CHUNK)
            for j in range(3):
                cs = slice(j * LANES, (j + 1) * LANES)
                vp = vnb[rs, cs]
                z = jnp.where(lane < HEAD, _dot(ws_ref[2 * j], vp), _dot(ws_ref[2 * j + 1], vp)) + b_ref[:, cs]
                du_ref[rs, cs] = dy[rs, cs] * z
                dz = dy[rs, cs] * u[rs, cs]
                dbf_ref[:, cs] += dz
                dzb = dz.astype(BF16)
                dz0 = jnp.where(lane < HEAD, dzb, jnp.zeros_like(dzb))
                dz1 = jnp.where(lane >= HEAD, dzb, jnp.zeros_like(dzb))
                dvn_ref[rs, cs] = jnp.where(lane < HEAD, _dot_tn(ws_ref[2 * j], dzb), _dot_tn(ws_ref[2 * j + 1], dzb))
                dws_ref[2 * j] += jnp.where(tril, _dot_nt(dz0, vp), 0.0)
                dws_ref[2 * j + 1] += jnp.where(tril, _dot_nt(dz1, vp), 0.0)
        dzu, dzv, dg = vjp((du_ref[...], dvn_ref[...]))
        dzu_ref[...] = dzu.astype(BF16)
        dzv_ref[...] = dzv.astype(BF16)
        _acc(dg_ref, dg, first)

        @pl.when(step == pl.num_programs(0) - 1)
        def _():
            out = jnp.zeros((CHUNK, LANES), F32)
            for j in range(3):
                slab = dbf_ref[:, j * LANES:(j + 1) * LANES]
                lo = jnp.sum(jnp.where(lane < HEAD, slab, 0.0), axis=1, keepdims=True)
                hi = jnp.sum(jnp.where(lane >= HEAD, slab, 0.0), axis=1, keepdims=True)
                out = out + jnp.where(lane == 2 * j, lo, 0.0) + jnp.where(lane == 2 * j + 1, hi, 0.0)
            dbs_ref[...] = out

    return pl.pallas_call(
        body, name="sgu_bwd", grid=(T // tm,),
        in_specs=[BS((tm, A_W), lambda i: (i, 0)), BS((tm, A_W), lambda i: (i, 1)), BS((tm, A_W), lambda i: (i, 0)),
                  BS((1, A_W), lambda i: (0, 0)), BS((6, CHUNK, CHUNK), lambda i: (0, 0, 0)),
                  BS((CHUNK, A_W), lambda i: (0, 0))],
        out_specs=[BS((tm, A_W), lambda i: (i, 0)), BS((tm, A_W), lambda i: (i, 0)),
                   BS((6, CHUNK, CHUNK), lambda i: (0, 0, 0)), BS((CHUNK, LANES), lambda i: (0, 0)),
                   BS((1, A_W), lambda i: (0, 0))],
        out_shape=[jax.ShapeDtypeStruct((T, A_W), BF16), jax.ShapeDtypeStruct((T, A_W), BF16),
                   jax.ShapeDtypeStruct((6, CHUNK, CHUNK), F32), jax.ShapeDtypeStruct((CHUNK, LANES), F32),
                   jax.ShapeDtypeStruct((1, A_W), F32)],
        scratch_shapes=[pltpu.VMEM((tm, A_W), F32), pltpu.VMEM((tm, A_W), F32), pltpu.VMEM((CHUNK, A_W), F32)],
        compiler_params=_cp(("arbitrary",)),
    )(proj, proj, dya, g_sgu, ws_tril, bs_full)


def _memattn_bwd(proj, kv, dym, Bl, S, tq):
    T = Bl * S
    nq = S // tq
    Mt = kv.shape[1]
    qc = 1920 // LANES

    def body(q_ref, km_ref, vm_ref, do_ref, dq_ref, dkm_ref, dvm_ref):
        first = pl.program_id(2) == 0
        lane = _iota((tq, LANES), 1)
        q = q_ref[...]
        do = do_ref[...]
        dq_out = jnp.zeros((tq, LANES), F32)
        dkm = jnp.zeros((Mt, LANES), F32)
        dvm = jnp.zeros((Mt, LANES), F32)
        for hh in range(2):
            hmask = (lane < HEAD) if hh == 0 else (lane >= HEAD)
            qs = jnp.where(hmask, q, jnp.zeros_like(q)) * 0.125
            dom = jnp.where(hmask, do, 0.0).astype(BF16)
            s = _dot_nt(qs, km_ref[0])
            pe = jnp.exp(s - jnp.max(s, axis=1, keepdims=True))
            pn = pe / jnp.sum(pe, axis=1, keepdims=True)
            dp = _dot_nt(dom, vm_ref[0])
            ds = (pn * (dp - jnp.sum(pn * dp, axis=1, keepdims=True))).astype(BF16)
            dq_out = jnp.where(hmask, _dot(ds, km_ref[0]) * 0.125, dq_out)
            dkm = dkm + _dot_tn(ds, qs)
            dvm = dvm + _dot_tn(pn, dom)
        dq_ref[...] = dq_out.astype(BF16)
        _acc(dkm_ref, dkm[None], first)
        _acc(dvm_ref, dvm[None], first)

    return pl.pallas_call(
        body, name="memattn_bwd", grid=(Bl, 2, nq),
        in_specs=[BS((tq, LANES), lambda b, p, i: (b * nq + i, qc + p)),
                  BS((1, Mt, LANES), lambda b, p, i: (b, 0, p)),
                  BS((1, Mt, LANES), lambda b, p, i: (b, 0, 2 + p)),
                  BS((tq, LANES), lambda b, p, i: (b * nq + i, p))],
        out_specs=[BS((tq, LANES), lambda b, p, i: (b * nq + i, p)),
                   BS((1, Mt, LANES), lambda b, p, i: (b, 0, p)),
                   BS((1, Mt, LANES), lambda b, p, i: (b, 0, p))],
        out_shape=[jax.ShapeDtypeStruct((T, M_W), BF16), jax.ShapeDtypeStruct((Bl, Mt, M_W), F32),
                   jax.ShapeDtypeStruct((Bl, Mt, M_W), F32)],
        compiler_params=_cp(("arbitrary", "arbitrary", "arbitrary")),
    )(proj, kv, kv, dym)


def _memkv_bwd(dkm, dvm, memn, mem, g_mem, w_kv):
    Bl, Mt, D = mem.shape

    def body(dk_ref, dv_ref, mn_ref, m_ref, g_ref, w_ref, dw_ref, dg_ref):
        first = pl.program_id(0) == 0
        dk = dk_ref[0].astype(BF16)
        dv = dv_ref[0].astype(BF16)
        mn = mn_ref[0]
        dmn = _dot_nt(dk, w_ref[:, 0:M_W]) + _dot_nt(dv, w_ref[:, M_W:])
        _, dg = _rms_bwd(m_ref[0], g_ref[...], dmn)
        _acc(dg_ref, dg, first)

        @pl.when(first)
        def _():
            dw_ref[...] = jnp.zeros_like(dw_ref)

        dw_ref[:, 0:M_W] += _dot_tn(mn, dk)
        dw_ref[:, M_W:] += _dot_tn(mn, dv)

    return pl.pallas_call(
        body, name="memkv_bwd", grid=(Bl,),
        in_specs=[BS((1, Mt, M_W), lambda b: (b, 0, 0)), BS((1, Mt, M_W), lambda b: (b, 0, 0)),
                  BS((1, Mt, D), lambda b: (b, 0, 0)), BS((1, Mt, D), lambda b: (b, 0, 0)),
                  BS((1, D), lambda b: (0, 0)), BS((D, 2 * M_W), lambda b: (0, 0))],
        out_specs=[BS((D, 2 * M_W), lambda b: (0, 0)), BS((1, D), lambda b: (0, 0))],
        out_shape=[jax.ShapeDtypeStruct((D, 2 * M_W), F32), jax.ShapeDtypeStruct((1, D), F32)],
        compiler_params=_cp(("arbitrary",)),
    )(dkm, dvm, memn, mem, g_mem, w_kv)


def _fox_bwd(proj, dyb, lse, bq, bk, Bl, S):
    T = Bl * S
    nq = S // Q_BLK
    nb = S // LANES
    qc, kc, vc = 768 // LANES, 1152 // LANES, 1536 // LANES

    def body(q_ref, k_ref, v_ref, do_ref, lse_ref, bq_ref, bk_ref,
             dq_ref, dk_ref, dv_ref, dcr_ref, ka_ref, dka_ref, dva_ref):
        p = pl.program_id(1)
        lane_s = _iota((S, LANES), 1)
        lane = _iota((Q_BLK, LANES), 1)
        sub = _iota((8, LANES), 0)
        tri = _iota((Q_BLK, Q_BLK), 1) <= _iota((Q_BLK, Q_BLK), 0)
        k = k_ref[...]
        for hh in range(2):
            data = (lane_s < HEAD) if hh == 0 else (lane_s >= HEAD)
            ka_ref[hh] = jnp.where(data, k, bk_ref[0, hh])
        dka_ref[...] = jnp.zeros_like(dka_ref)
        dva_ref[...] = jnp.zeros_like(dva_ref)

        @pl.when(p == 0)
        def _():
            dcr_ref[...] = jnp.zeros_like(dcr_ref)

        def add_colsums(ds, first_blk, h):
            cs = _colsum(ds)
            for jb in range(ds.shape[1] // LANES):
                dcr_ref[0, first_blk + jb] += jnp.where(sub == h, cs[:, jb * LANES:(jb + 1) * LANES], 0.0)

        for i in range(nq):
            r0 = i * Q_BLK
            r1 = r0 + Q_BLK
            q = q_ref[r0:r1, :]
            do = do_ref[r0:r1, :]
            lse_b = lse_ref[0, r0:r1, :]
            dq_out = jnp.zeros((Q_BLK, LANES), F32)
            for hh in range(2):
                hmask = (lane < HEAD) if hh == 0 else (lane >= HEAD)
                h = 2 * p + hh
                qs = jnp.where(hmask, q * 0.125, jnp.zeros_like(q))
                qa = jnp.where(hmask, q * 0.125, bq_ref[0, hh, r0:r1, :])
                dob = jnp.where(hmask, do, 0.0).astype(BF16)
                lse_h = jnp.sum(jnp.where(lane == hh * HEAD, lse_b, 0.0), axis=1, keepdims=True)
                pd = jnp.where(tri, jnp.exp(_dot_nt(qa, ka_ref[hh, r0:r1, :]) - lse_h), 0.0)
                dpd = _dot_nt(dob, v_ref[r0:r1, :])
                delta = jnp.sum(pd * dpd, axis=1, keepdims=True)
                psum = jnp.sum(pd, axis=1, keepdims=True)
                if i:
                    pf = jnp.exp(_dot_nt(qa, ka_ref[hh, 0:r0, :]) - lse_h)
                    dpf = _dot_nt(dob, v_ref[0:r0, :])
                    delta = delta + jnp.sum(pf * dpf, axis=1, keepdims=True)
                    psum = psum + jnp.sum(pf, axis=1, keepdims=True)
                delta = delta / psum
                dsd = pd * (dpd - delta)
                add_colsums(dsd, r0 // LANES, h)
                dsd = dsd.astype(BF16)
                dq_h = _dot(dsd, k_ref[r0:r1, :])
                dka_ref[r0:r1, :] += _dot_tn(dsd, qs)
                dva_ref[r0:r1, :] += _dot_tn(pd, dob)
                if i:
                    dsf = pf * (dpf - delta)
                    add_colsums(dsf, 0, h)
                    dsf = dsf.astype(BF16)
                    dq_h = dq_h + _dot(dsf, k_ref[0:r0, :])
                    dka_ref[0:r0, :] += _dot_tn(dsf, qs)
                    dva_ref[0:r0, :] += _dot_tn(pf, dob)
                dq_out = jnp.where(hmask, dq_h * 0.125, dq_out)
            dq_ref[r0:r1, :] = dq_out.astype(BF16)
        dk_ref[...] = dka_ref[...].astype(BF16)
        dv_ref[...] = dva_ref[...].astype(BF16)

    seq = lambda c0: BS((S, LANES), lambda b, p: (b, c0 + p))
    pair = BS((1, 2, S, LANES), lambda b, p: (b, p, 0, 0))
    rowblk = BS((1, nb, 8, LANES), lambda b, p: (b, 0, 0, 0))
    return pl.pallas_call(
        body, name="fox_bwd", grid=(Bl, 3),
        in_specs=[seq(qc), seq(kc), seq(vc), seq(0), BS((1, S, LANES), lambda b, p: (p, b, 0)), pair, pair],
        out_specs=[seq(0), seq(0), seq(0), rowblk],
        out_shape=[jax.ShapeDtypeStruct((T, B_W), BF16)] * 3 + [jax.ShapeDtypeStruct((Bl, nb, 8, LANES), F32)],
        scratch_shapes=[pltpu.VMEM((2, S, LANES), BF16), pltpu.VMEM((S, LANES), F32), pltpu.VMEM((S, LANES), F32)],
        compiler_params=_cp(("arbitrary", "arbitrary")),
    )(proj, proj, proj, dyb, lse, bq, bk)


def _gate_bwd(dc_row, fl_row):
    Bl, nb, _, _ = dc_row.shape

    def body(dc_ref, fl_ref, o_ref):
        lane = _iota((8, LANES), 1)

        carry = jnp.zeros((8, 1), F32)
        for j in reversed(range(nb)):
            r = -dc_ref[0, j]
            for k in (1, 2, 4, 8, 16, 32, 64):
                r = r + jnp.where(lane < LANES - k, pltpu.roll(r, LANES - k, 1), 0.0)
            total = jnp.sum(jnp.where(lane == 0, r, 0.0), axis=1, keepdims=True)
            dfl = (r + carry) * _sigmoid(-fl_ref[0, j])
            carry = carry + total
            o_ref[0, j * LANES:(j + 1) * LANES, :] = jnp.concatenate(
                [dfl, jnp.zeros((LANES - 8, LANES), F32)], axis=0).T

    rowblk = BS((1, nb, 8, LANES), lambda b: (b, 0, 0, 0))
    return pl.pallas_call(
        body, name="gate_bwd", grid=(Bl,),
        in_specs=[rowblk, rowblk],
        out_specs=BS((1, nb * LANES, LANES), lambda b: (b, 0, 0)),
        out_shape=jax.ShapeDtypeStruct((Bl, nb * LANES, LANES), F32),
        compiler_params=_cp(("arbitrary",)),
    )(dc_row, fl_row)


def _inproj_bwd(pieces, x2d, dx1, g_pre, w_in_p, tm):
    T, D = x2d.shape
    ns, _, dsh = w_in_p.shape

    def body(*refs):
        piece_refs = refs[:7]
        x_ref, dx1_ref, g_ref, w_ref, gx_ref, dg_ref, dbf_ref, dp_ref = refs[7:]
        first = pl.program_id(0) == 0
        _put_dproj(dp_ref, piece_refs)
        dh = jnp.concatenate([_dot(dp_ref[...], w_ref[s]) for s in range(ns)], axis=1)
        dxa, dg = _rms_bwd(x_ref[...], g_ref[...], dh)
        gx_ref[...] = dx1_ref[...] + dxa
        _acc(dg_ref, dg, first)
        _acc(dbf_ref, _colsum(piece_refs[5][...]), first)

    row = lambda w: BS((tm, w), lambda i: (i, 0))
    return pl.pallas_call(
        body, name="inproj_bwd", grid=(T // tm,),
        in_specs=[row(w) for _, w in DPROJ_PIECES] + [row(D), row(D), BS((1, D), lambda i: (0, 0)),
                                                      BS((ns, P_COLS, dsh), lambda i: (0, 0, 0))],
        out_specs=[row(D), BS((1, D), lambda i: (0, 0)), BS((1, LANES), lambda i: (0, 0))],
        out_shape=[jax.ShapeDtypeStruct((T, D), F32), jax.ShapeDtypeStruct((1, D), F32),
                   jax.ShapeDtypeStruct((1, LANES), F32)],
        scratch_shapes=[pltpu.VMEM((tm, P_COLS), BF16)],
        compiler_params=_cp(("arbitrary",)),
    )(*pieces, x2d, dx1, g_pre, w_in_p)


def _local_step(x, mem, target, W, P, reduce=None):
    Bl, S, D = x.shape
    T = Bl * S
    tm = min(512, T)
    x2d = x.reshape(T, D)
    t2d = target.reshape(T, D)
    vec = lambda a: a.reshape(1, -1)
    bf_row = jnp.pad(P["b_f"].reshape(1, -1), ((0, 0), (0, LANES - N_FOX_HEADS)))
    tril = jnp.tril(jnp.ones((CHUNK, CHUNK), bool))
    ws_tril = jnp.where(tril[None], P["w_s"][0], 0.0).astype(BF16)
    bs_full = jnp.repeat(P["b_s"][0].T, HEAD, axis=1)
    g_pre, g_sgu = vec(P["g_pre_mix"]), vec(P["g_sgu"])
    ga, gb, gm = vec(P["g_out_a"]), vec(P["g_out_b"]), vec(P["g_out_m"])
    g_mem, g_post, g_pre2, g_post2 = vec(P["g_mem"]), vec(P["g_post_mix"]), vec(P["g_pre_ffn"]), vec(P["g_post_ffn"])

    h, proj, flog = _inproj_fwd(x2d, g_pre, W["w_in"], tm)
    bq, bk, fl_row = _gate_fwd(flog.reshape(Bl, S, LANES), bf_row)
    ya = _sgu_fwd(proj, g_sgu, ws_tril, bs_full, tm)
    yb, lse = _fox_fwd(proj, bq, bk, Bl, S)
    memn, kv = _memkv_fwd(mem, g_mem, W["w_mem_kv"])
    ym = _memattn_fwd(proj, kv, Bl, S, min(2048, S))
    y, o, x1, h2 = _outproj_fwd(ya, yb, ym, x2d, ga, gb, gm, g_post, g_pre2, W["w_out"], tm)
    gs, us, dff, dx2, dg_post2, loss = _ffn_fwd(h2, x1, t2d, W["w_gate"], W["w_up"], W["w_down"], g_post2, tm)

    dh2, d_w_gate, d_w_up, d_w_down = _ffn_bwd(dff, h2, gs, us, W["w_gate"], W["w_up"], W["w_down"], min(1024, T))
    ffn = [d_w_gate, d_w_up, d_w_down]
    if reduce is not None:
        pending, _ = reduce.begin("ffn", ffn)
    dx1, do, dya, dyb, dym, dga, dgb, dgm, dg_post, dg_pre2 = _outproj_bwd(
        dh2, x1, dx2, o, ya, yb, ym, ga, gb, gm, g_post, g_pre2, W["w_out"], tm)
    if reduce is not None:
        ffn, (do, dya, dyb, dym) = reduce.finish("ffn", pending, (do, dya, dyb, dym))
    d_w_out = _mm_tn(y, do, "dw_out", 1024)
    dzu, dzv, dws, dbs_cols, dg_sgu = _sgu_bwd(proj, dya, g_sgu, ws_tril, bs_full, tm)
    dqm, dkm, dvm = _memattn_bwd(proj, kv, dym, Bl, S, min(2048, S))
    d_w_kv, dg_mem = _memkv_bwd(dkm, dvm, memn, mem, g_mem, W["w_mem_kv"])
    mid = [d_w_kv, d_w_out]
    dq, dk, dv, dc_row = _fox_bwd(proj, dyb, lse, bq, bk, Bl, S)
    if reduce is not None:
        done = reduce.apply(BIG[3:], ffn)
        pending, after = reduce.begin("mid", mid, (dc_row,) + done)
        dc_row = after[0]
    dfl = _gate_bwd(dc_row, fl_row).reshape(T, LANES)
    pieces = (dzu, dzv, dq, dk, dv, dfl, dqm)
    grad_x, dg_pre, dbf = _inproj_bwd(pieces, x2d, dx1, g_pre, W["w_in"], tm)
    if reduce is not None:
        mid, (dfl,) = reduce.finish("mid", pending, (dfl,))
        pieces = (dzu, dzv, dq, dk, dv, dfl, dqm)
    d_w_in = _dw_in(pieces, h, W["w_in"].shape[0], 1024)
    if reduce is None:
        big = dict(zip(BIG, [d_w_in] + mid + ffn))
    else:
        done = reduce.apply(BIG[1:3], mid)
        big = {"w_in": reduce.begin("in", [d_w_in], done)[0]}
    small = {"g_pre_mix": dg_pre, "b_f": dbf[:, :N_FOX_HEADS], "g_sgu": dg_sgu, "w_s": dws, "b_s": dbs_cols[:, :N_FOX_HEADS].T,
             "g_out_a": dga, "g_out_b": dgb, "g_out_m": dgm, "g_mem": dg_mem, "g_post_mix": dg_post,
             "g_pre_ffn": dg_pre2, "g_post_ffn": dg_post2, "loss": loss[:, :1]}
    return grad_x.reshape(Bl, S, D), big, small


def _place():
    return lax.axis_index("x"), lax.axis_index("y"), lax.axis_index("c")


def _exchange_on_sequencer(srcs, own_full, name, collective_id):
    n = len(srcs)

    def body(*refs):
        src, dst = refs[:n], refs[n:2 * n]
        lsem, isend, irecv, dsend, drecv = refs[2 * n:]
        x, y, c = _place()
        oc = 1 - c
        s_me = 2 * x + y
        sib = (x, y, oc)
        chips = [(1 - x, y), (x, 1 - y), (1 - x, 1 - y)]
        barrier = pltpu.get_barrier_semaphore()
        for dev in [(cx, cy, c) for cx, cy in chips] + [sib]:
            pl.semaphore_signal(barrier, inc=1, device_id=dev, device_id_type=MESH)
        pl.semaphore_wait(barrier, 4)

        def remote(a, b, ssem, rsem, dev):
            return pltpu.make_async_remote_copy(src_ref=a, dst_ref=b, send_sem=ssem, recv_sem=rsem,
                                                device_id=dev, device_id_type=MESH)

        sends, local = [], []
        for w in range(n):
            for j, (cx, cy) in enumerate(chips):
                half = src[w].at[c] if own_full else src[w].at[2 * cx + cy]
                cp = remote(half, dst[w].at[s_me, c], isend.at[w, j], irecv.at[w, j], (cx, cy, c))
                cp.start()
                sends.append(cp)
            if own_full:
                cp = remote(src[w], dst[w].at[s_me], dsend.at[w, 3], drecv.at[w, 3], sib)
            else:
                cp = remote(src[w].at[s_me], dst[w].at[s_me, c], dsend.at[w, 3], drecv.at[w, 3], sib)
                loc = pltpu.make_async_copy(src[w].at[s_me], dst[w].at[s_me, c], lsem.at[w])
                loc.start()
                local.append(loc)
            cp.start()
            sends.append(cp)
        for w in range(n):
            for j, (cx, cy) in enumerate(chips):
                landed = dst[w].at[2 * cx + cy, c]
                remote(landed, landed, isend.at[w, j], irecv.at[w, j], (cx, cy, c)).wait_recv()
                cp = remote(landed, landed, dsend.at[w, j], drecv.at[w, j], sib)
                cp.start()
                sends.append(cp)
        for w in range(n):
            for j, (cx, cy) in enumerate(chips):
                landed = dst[w].at[2 * cx + cy, oc]
                remote(landed, landed, dsend.at[w, j], drecv.at[w, j], sib).wait_recv()
            landed = dst[w].at[s_me] if own_full else dst[w].at[s_me, oc]
            remote(landed, landed, dsend.at[w, 3], drecv.at[w, 3], sib).wait_recv()
        for cp in sends:
            cp.wait_send()
        for loc in local:
            loc.wait()

    return pl.kernel(
        body, out_type=[jax.ShapeDtypeStruct((4, 2) + s.shape[1:], s.dtype) for s in srcs],
        mesh=plsc.ScalarSubcoreMesh(axis_name="sequencer", num_cores=1), name=name,
        scratch_types=[pltpu.SemaphoreType.DMA((n,)), pltpu.SemaphoreType.DMA((n, 3)), pltpu.SemaphoreType.DMA((n, 3)),
                       pltpu.SemaphoreType.DMA((n, 4)), pltpu.SemaphoreType.DMA((n, 4))],
        compiler_params=pltpu.CompilerParams(collective_id=collective_id),
    )(*srcs)


def _sibling_swap(grads, name, collective_id):
    n = len(grads)

    def body(*refs):
        g, theirs = refs[:n], refs[n:2 * n]
        ssem, rsem = refs[2 * n:]
        x, y, c = _place()
        sib = (x, y, 1 - c)
        barrier = pltpu.get_barrier_semaphore()
        pl.semaphore_signal(barrier, inc=1, device_id=sib, device_id_type=MESH)
        pl.semaphore_wait(barrier, 1)
        cps = []
        for w in range(n):
            cp = pltpu.make_async_remote_copy(src_ref=g[w].at[:, 1 - c], dst_ref=theirs[w], send_sem=ssem.at[w],
                                              recv_sem=rsem.at[w], device_id=sib, device_id_type=MESH)
            cp.start()
            cps.append(cp)
        for cp in cps:
            cp.wait()

    return pl.kernel(
        body, out_type=[jax.ShapeDtypeStruct((4,) + g.shape[2:], g.dtype) for g in grads],
        mesh=plsc.ScalarSubcoreMesh(axis_name="sequencer", num_cores=1), name=name,
        scratch_types=[pltpu.SemaphoreType.DMA((n,)), pltpu.SemaphoreType.DMA((n,))],
        compiler_params=pltpu.CompilerParams(collective_id=collective_id),
    )(*grads)


def _add_pair(core, g, theirs, name):
    _, _, hr, C = g.shape

    def body(core_ref, g_ref, t_ref, o_ref):
        o_ref[0] = (g_ref[0, 0].astype(F32) + t_ref[0].astype(F32)).astype(BF16)

    blk = BS((1, hr, C), lambda s, core_ref: (s, 0, 0))
    return pl.pallas_call(
        body, name=name,
        grid_spec=pltpu.PrefetchScalarGridSpec(
            num_scalar_prefetch=1, grid=(4,),
            in_specs=[BS((1, 1, hr, C), lambda s, core_ref: (s, core_ref[0], 0, 0)), blk], out_specs=blk),
        out_shape=jax.ShapeDtypeStruct(theirs.shape, BF16), compiler_params=_cp(("arbitrary",)))(core, g, theirs)


def _sum_chips(r, name):
    _, _, hr, C = r.shape

    def body(r_ref, o_ref):
        o_ref[...] = ((r_ref[0, 0].astype(F32) + r_ref[1, 0].astype(F32)) + r_ref[2, 0].astype(F32)) + r_ref[3, 0].astype(F32)

    return pl.pallas_call(body, name=name, grid=(2,), in_specs=[BS((4, 1, hr, C), lambda h: (0, h, 0, 0))],
                          out_specs=BS((hr, C), lambda h: (h, 0)), out_shape=jax.ShapeDtypeStruct((2 * hr, C), F32),
                          compiler_params=_cp(("arbitrary",)))(r)


class _Reducer:
    IDS = {"ffn": (4, 5), "mid": (6, 7), "in": (8, 9)}

    def __init__(self, core, apply):
        self.core = core
        self.apply = apply

    def begin(self, tag, grads, after=()):
        grads, after = lax.optimization_barrier((list(grads), after))
        g4 = [g.reshape(4, 2, -1, g.shape[-1]) for g in grads]
        return (g4, _sibling_swap(g4, "swap_" + tag, self.IDS[tag][0])), after

    def finish(self, tag, pending, hold):
        g4, theirs = pending
        sums = [_add_pair(self.core, g, t, "chip_sum_%s_%d" % (tag, k)) for k, (g, t) in enumerate(zip(g4, theirs))]
        sums, hold = lax.optimization_barrier((sums, hold))
        return _exchange_on_sequencer(sums, False, "scatter_" + tag, self.IDS[tag][1]), hold


def _small_allreduce(part):
    R = part.shape[0]
    rs = R // 8
    masks = [(mx, my, mc) for mx in (0, 1) for my in (0, 1) for mc in (0, 1)][1:]

    def body(p_ref, o_ref, buf_ref, s1, r1, s2, r2):
        x, y, c = _place()
        d = 4 * x + 2 * y + c
        mine = pl.ds(pl.multiple_of(d * rs, 8), rs)
        peers = [((x + mx) % 2, (y + my) % 2, (c + mc) % 2) for mx, my, mc in masks]
        first, second = [], []
        for k, (px, py, pc) in enumerate(peers):
            theirs = pl.ds(pl.multiple_of((4 * px + 2 * py + pc) * rs, 8), rs)
            cp = pltpu.make_async_remote_copy(src_ref=p_ref.at[theirs, :], dst_ref=buf_ref.at[d], send_sem=s1.at[k],
                                              recv_sem=r1.at[k], device_id=(px, py, pc), device_id_type=MESH)
            cp.start()
            first.append(cp)
        buf_ref[d] = p_ref[mine, :]
        for k, (px, py, pc) in enumerate(peers):
            slot = buf_ref.at[4 * px + 2 * py + pc]
            pltpu.make_async_remote_copy(src_ref=slot, dst_ref=slot, send_sem=s1.at[k], recv_sem=r1.at[k],
                                         device_id=(px, py, pc), device_id_type=MESH).wait_recv()
        total = buf_ref[0]
        for k in range(1, 8):
            total = total + buf_ref[k]
        o_ref[mine, :] = total
        for k, (px, py, pc) in enumerate(peers):
            cp = pltpu.make_async_remote_copy(src_ref=o_ref.at[mine, :], dst_ref=o_ref.at[mine, :], send_sem=s2.at[k],
                                              recv_sem=r2.at[k], device_id=(px, py, pc), device_id_type=MESH)
            cp.start()
            second.append(cp)
        for k, (px, py, pc) in enumerate(peers):
            rows = o_ref.at[pl.ds(pl.multiple_of((4 * px + 2 * py + pc) * rs, 8), rs), :]
            pltpu.make_async_remote_copy(src_ref=rows, dst_ref=rows, send_sem=s2.at[k], recv_sem=r2.at[k],
                                         device_id=(px, py, pc), device_id_type=MESH).wait_recv()
        for cp in first + second:
            cp.wait_send()

    vm = pl.BlockSpec(memory_space=pltpu.VMEM)
    return pl.pallas_call(
        body, name="small_allreduce", in_specs=[vm], out_specs=vm, out_shape=jax.ShapeDtypeStruct(part.shape, F32),
        scratch_shapes=[pltpu.VMEM((8, rs, LANES), F32)] + [pltpu.SemaphoreType.DMA((7,))] * 4,
    )(part)


def _adamw(w, g, m, v, name):
    R, C = w.shape
    summed = g.ndim == 4
    if summed:
        tr = R // 2
    else:
        tr = R if R * C * 4 <= (1 << 21) else R // 2
        if tr % 8:
            tr = R
    c1 = 1.0 / (1.0 - ADAM_B1 ** ADAM_STEP)
    c2 = 1.0 / (1.0 - ADAM_B2 ** ADAM_STEP)

    def body(w_ref, g_ref, m_ref, v_ref, *outs):
        if summed:
            g_ = ((g_ref[0, 0].astype(F32) + g_ref[1, 0].astype(F32)) + g_ref[2, 0].astype(F32)) + g_ref[3, 0].astype(F32)
            outs[0][...] = g_
        else:
            g_ = g_ref[...]
        d_ref, mo_ref, vo_ref = outs[-3:]
        m_ = ADAM_B1 * m_ref[...] + (1.0 - ADAM_B1) * g_
        v_ = ADAM_B2 * v_ref[...] + (1.0 - ADAM_B2) * (g_ * g_)
        mo_ref[...] = m_
        vo_ref[...] = v_
        d_ref[...] = -ADAM_LR * ((m_ * c1) / (jnp.sqrt(v_ * c2) + ADAM_EPS) + ADAM_WD * w_ref[...])

    blk = BS((tr, C), lambda i: (i, 0))
    g_blk = BS((4, 1, tr, C), lambda i: (0, i, 0, 0)) if summed else blk
    nout = 4 if summed else 3
    return pl.pallas_call(body, name=name, grid=(R // tr,), in_specs=[blk, g_blk, blk, blk], out_specs=[blk] * nout,
                          out_shape=[jax.ShapeDtypeStruct((R, C), F32)] * nout,
                          compiler_params=_cp(("arbitrary",)))(w, g, m, v)


def _adamw_from_transposed(w, g_t, m, v, name):
    R, C = w.shape
    tc = 256
    c1 = 1.0 / (1.0 - ADAM_B1 ** ADAM_STEP)
    c2 = 1.0 / (1.0 - ADAM_B2 ** ADAM_STEP)

    def body(w_ref, g_ref, m_ref, v_ref, go_ref, d_ref, mo_ref, vo_ref):
        g_ = g_ref[...].T
        go_ref[...] = g_
        m_ = ADAM_B1 * m_ref[...] + (1.0 - ADAM_B1) * g_
        v_ = ADAM_B2 * v_ref[...] + (1.0 - ADAM_B2) * (g_ * g_)
        mo_ref[...] = m_
        vo_ref[...] = v_
        d_ref[...] = -ADAM_LR * ((m_ * c1) / (jnp.sqrt(v_ * c2) + ADAM_EPS) + ADAM_WD * w_ref[...])

    blk = BS((R, tc), lambda i: (0, i))
    return pl.pallas_call(body, name=name, grid=(pl.cdiv(C, tc),), in_specs=[blk, BS((tc, R), lambda i: (i, 0)), blk, blk],
                          out_specs=[blk] * 4, out_shape=[jax.ShapeDtypeStruct((R, C), F32)] * 4,
                          compiler_params=_cp(("arbitrary",)))(w, g_t, m, v)


def _adamw_unit_rows(w, g, m, v, name):
    C, _, R = w.shape
    tc = C // 2 if C % 2 == 0 else C
    c1 = 1.0 / (1.0 - ADAM_B1 ** ADAM_STEP)
    c2 = 1.0 / (1.0 - ADAM_B2 ** ADAM_STEP)

    def body(w_ref, g_ref, m_ref, v_ref, go_ref, d_ref, mo_ref, vo_ref):
        g_ = g_ref[...]
        go_ref[...] = g_
        m_ = ADAM_B1 * m_ref[...] + (1.0 - ADAM_B1) * g_
        v_ = ADAM_B2 * v_ref[...] + (1.0 - ADAM_B2) * (g_ * g_)
        mo_ref[...] = m_
        vo_ref[...] = v_
        d_ref[...] = -ADAM_LR * ((m_ * c1) / (jnp.sqrt(v_ * c2) + ADAM_EPS) + ADAM_WD * w_ref[...])

    blk = BS((tc, 1, R), lambda i: (i, 0, 0))
    return pl.pallas_call(body, name=name, grid=(C // tc,), in_specs=[blk] * 4, out_specs=[blk] * 4,
                          out_shape=[jax.ShapeDtypeStruct((C, 1, R), F32)] * 4,
                          compiler_params=_cp(("arbitrary",)))(w, g, m, v)


SMALL = ("g_pre_mix", "b_f", "g_sgu", "w_s", "b_s", "g_out_a", "g_out_b", "g_out_m", "g_mem", "g_post_mix",
         "g_pre_ffn", "g_post_ffn")
BIG = ("w_in", "w_mem_kv", "w_out", "w_gate", "w_up", "w_down")
TRANSPOSED = ("w_in", "w_gate", "w_up")
WEIGHTS = ("g_pre_mix", "w_in", "b_f", "g_sgu", "w_s", "b_s", "g_out_a", "g_out_b", "g_out_m", "g_mem", "w_mem_kv",
           "w_out", "g_post_mix", "g_pre_ffn", "w_gate", "w_up", "w_down", "g_post_ffn")


VECTORS = ("g_pre_mix", "b_f", "g_sgu", "g_out_a", "g_out_b", "g_out_m", "g_mem", "g_post_mix", "g_pre_ffn", "g_post_ffn")
VEC_ROWS = 16
WS_ROWS = N_FOX_HEADS * CHUNK
BS_ROWS = 8


def _pack_small(small, vw):
    stack = jnp.zeros((VEC_ROWS, vw), F32)
    for k, n in enumerate(VECTORS + ("loss",)):
        row = small[n].reshape(1, -1)
        stack = stack + jnp.pad(row, ((k, VEC_ROWS - 1 - k), (0, vw - row.shape[1])))
    parts = [small["w_s"].reshape(WS_ROWS, LANES), jnp.pad(small["b_s"], ((0, BS_ROWS - N_FOX_HEADS), (0, 0))),
             stack.reshape(-1, LANES)]
    rows = sum(p.shape[0] for p in parts)
    return jnp.concatenate(parts + [jnp.zeros((-rows % 64, LANES), F32)], axis=0)


def _adamw_small(vec_g, vec_wmv, ws, bs):
    c1 = 1.0 / (1.0 - ADAM_B1 ** ADAM_STEP)
    c2 = 1.0 / (1.0 - ADAM_B2 ** ADAM_STEP)
    nv = len(vec_wmv)

    def adam(g, w, m, v):
        m_ = ADAM_B1 * m + (1.0 - ADAM_B1) * g
        v_ = ADAM_B2 * v + (1.0 - ADAM_B2) * (g * g)
        return -ADAM_LR * ((m_ * c1) / (jnp.sqrt(v_ * c2) + ADAM_EPS) + ADAM_WD * w), m_, v_

    def body(*refs):
        vg_ref, ins, outs = refs[0], refs[1:1 + 3 * nv + 8], refs[1 + 3 * nv + 8:]
        for k in range(nv):
            w_ref, m_ref, v_ref = ins[3 * k:3 * k + 3]
            g = vg_ref[k:k + 1, 0:w_ref.shape[1]]
            d, m_, v_ = adam(g, w_ref[...], m_ref[...], v_ref[...])
            for o_ref, val in zip(outs[4 * k:4 * k + 4], (g, d, m_, v_)):
                o_ref[...] = val
        for j in range(2):
            g_ref, w_ref, m_ref, v_ref = ins[3 * nv + 4 * j:3 * nv + 4 * j + 4]
            for o_ref, val in zip(outs[4 * nv + 3 * j:4 * nv + 3 * j + 3], adam(g_ref[...], w_ref[...], m_ref[...], v_ref[...])):
                o_ref[...] = val

    vm = pl.BlockSpec(memory_space=pltpu.VMEM)
    operands = [vec_g] + [a for wmv in vec_wmv for a in wmv] + list(ws) + list(bs)
    out_shape = ([jax.ShapeDtypeStruct(wmv[0].shape, F32) for wmv in vec_wmv for _ in range(4)]
                 + [jax.ShapeDtypeStruct(ws[1].shape, F32)] * 3 + [jax.ShapeDtypeStruct(bs[1].shape, F32)] * 3)
    outs = pl.pallas_call(body, name="adamw_small", in_specs=[vm] * len(operands), out_specs=[vm] * len(out_shape),
                          out_shape=out_shape)(*operands)
    return [outs[4 * k:4 * k + 4] for k in range(nv)], outs[4 * nv:4 * nv + 3], outs[4 * nv + 3:]


def kernel(x, mem, g_pre_mix, w_in, b_f, g_sgu, w_s, b_s, g_out_a, g_out_b, g_out_m, g_mem, w_mem_kv, w_out, g_post_mix, g_pre_ffn, w_gate, w_up, w_down, g_post_ffn, loss_target, m_g_pre_mix, m_w_in, m_b_f, m_g_sgu, m_w_s, m_b_s, m_g_out_a, m_g_out_b, m_g_out_m, m_g_mem, m_w_mem_kv, m_w_out, m_g_post_mix, m_g_pre_ffn, m_w_gate, m_w_up, m_w_down, m_g_post_ffn, v_g_pre_mix, v_w_in, v_b_f, v_g_sgu, v_w_s, v_b_s, v_g_out_a, v_g_out_b, v_g_out_m, v_g_mem, v_w_mem_kv, v_w_out, v_g_post_mix, v_g_pre_ffn, v_w_gate, v_w_up, v_w_down, v_g_post_ffn):
    Wt = dict(g_pre_mix=g_pre_mix, w_in=w_in, b_f=b_f, g_sgu=g_sgu, w_s=w_s, b_s=b_s, g_out_a=g_out_a, g_out_b=g_out_b,
              g_out_m=g_out_m, g_mem=g_mem, w_mem_kv=w_mem_kv, w_out=w_out, g_post_mix=g_post_mix, g_pre_ffn=g_pre_ffn,
              w_gate=w_gate, w_up=w_up, w_down=w_down, g_post_ffn=g_post_ffn)
    Mo = dict(g_pre_mix=m_g_pre_mix, w_in=m_w_in, b_f=m_b_f, g_sgu=m_g_sgu, w_s=m_w_s, b_s=m_b_s, g_out_a=m_g_out_a,
              g_out_b=m_g_out_b, g_out_m=m_g_out_m, g_mem=m_g_mem, w_mem_kv=m_w_mem_kv, w_out=m_w_out,
              g_post_mix=m_g_post_mix, g_pre_ffn=m_g_pre_ffn, w_gate=m_w_gate, w_up=m_w_up, w_down=m_w_down,
              g_post_ffn=m_g_post_ffn)
    Vo = dict(g_pre_mix=v_g_pre_mix, w_in=v_w_in, b_f=v_b_f, g_sgu=v_g_sgu, w_s=v_w_s, b_s=v_b_s, g_out_a=v_g_out_a,
              g_out_b=v_g_out_b, g_out_m=v_g_out_m, g_mem=v_g_mem, w_mem_kv=v_w_mem_kv, w_out=v_w_out,
              g_post_mix=v_g_post_mix, g_pre_ffn=v_g_pre_ffn, w_gate=v_w_gate, w_up=v_w_up, w_down=v_w_down,
              g_post_ffn=v_g_post_ffn)

    gap = P_COLS - IN_COLS

    def to_kernel(n, w):
        if n in TRANSPOSED:
            w = w.T
        if n == "w_in":
            w = jnp.pad(w[:F_END], ((0, P_COLS - F_END), (0, 0))) + jnp.pad(w[F_END:], ((F_END + gap, 0), (0, 0)))
        return w

    def ungroup(g):
        return jnp.pad(g[:F_END], ((0, IN_COLS - F_END), (0, 0))) + jnp.pad(g[F_END + gap:], ((F_END, 0), (0, 0)))

    shards = {n: to_kernel(n, Wt[n][0]) for n in BIG}
    srcs = [shards[n].astype(BF16).reshape(2, shards[n].shape[0] // 2, shards[n].shape[1]) for n in BIG]
    fulls = (_exchange_on_sequencer(srcs[:1], True, "gather_w_in", 1)
             + _exchange_on_sequencer(srcs[1:3], True, "gather_kv_out", 2)
             + _exchange_on_sequencer(srcs[3:], True, "gather_ffn", 3))
    W = {}
    for n, f in zip(BIG, fulls):
        _, _, hr, C = f.shape
        W[n] = f.reshape(8 * hr, C) if n in ("w_mem_kv", "w_out") else f.reshape(4, 2 * hr, C)

    P = {n: Wt[n] for n in SMALL}
    grads, deltas, new_m, new_v = {}, {}, {}, {}

    def apply(names, landed):
        for n, r in zip(names, landed):
            if n == "w_in":
                g_t = ungroup(_sum_chips(r, "sum_chips_" + n))
                lift = lambda a: jnp.transpose(a, (2, 0, 1))
                outs = _adamw_unit_rows(lift(Wt[n]), g_t[:, None, :], lift(Mo[n]), lift(Vo[n]), "adamw_" + n)
                g, d, m1, v1 = [jnp.transpose(a, (1, 2, 0))[0] for a in outs]
            elif n in TRANSPOSED:
                g, d, m1, v1 = [a.T for a in _adamw(Wt[n][0].T, r, Mo[n][0].T, Vo[n][0].T, "adamw_" + n)]
            else:
                g, d, m1, v1 = _adamw(Wt[n][0], r, Mo[n][0], Vo[n][0], "adamw_" + n)
            grads[n], deltas[n], new_m[n], new_v[n] = g[None], d[None], m1[None], v1[None]
        return tuple(deltas[n] for n in names)

    core = lax.axis_index("c").astype(jnp.int32).reshape(1)
    reducer = _Reducer(core, apply)
    grad_x, pending, small = _local_step(x, mem, loss_target, W, P, reducer)

    vw = -(-max(x.shape[-1], A_W) // LANES) * LANES
    total = _small_allreduce(_pack_small(small, vw))
    landed, (total,) = reducer.finish("in", pending["w_in"], (total,))
    apply(BIG[:1], landed)

    lane_row = lambda a: jnp.pad(a, ((0, 0), (0, -a.shape[1] % LANES)))
    vec_g = total[WS_ROWS + BS_ROWS:WS_ROWS + BS_ROWS + VEC_ROWS * vw // LANES].reshape(VEC_ROWS, vw)
    ws_g = total[:WS_ROWS]
    bs_g = total[WS_ROWS:WS_ROWS + N_FOX_HEADS]
    rows = lambda a, r: a.reshape(r, LANES)
    per_vec, ws_out, bs_out = _adamw_small(
        vec_g, [tuple(lane_row(a[n]) for a in (Wt, Mo, Vo)) for n in VECTORS],
        (ws_g,) + tuple(rows(a["w_s"], WS_ROWS) for a in (Wt, Mo, Vo)),
        (bs_g,) + tuple(rows(a["b_s"], N_FOX_HEADS) for a in (Wt, Mo, Vo)))
    for n, outs in zip(VECTORS, per_vec):
        grads[n], deltas[n], new_m[n], new_v[n] = [o[:, :Wt[n].shape[1]] for o in outs]
    for n, g, outs in (("w_s", ws_g, ws_out), ("b_s", bs_g, bs_out)):
        grads[n], deltas[n], new_m[n], new_v[n] = [o.reshape(Wt[n].shape) for o in (g,) + tuple(outs)]
    loss = vec_g[len(VECTORS), 0]

    return (loss, grad_x, *[grads[n] for n in WEIGHTS], *[deltas[n] for n in WEIGHTS],
            *[new_m[n] for n in WEIGHTS], *[new_v[n] for n in WEIGHTS])
```

```python
import functools

import jax
import jax.numpy as jnp
from jax import lax
from jax.experimental import pallas as pl
from jax.experimental.pallas import tpu as pltpu
from jax.experimental.pallas import tpu_sc as plsc

F32 = jnp.float32
BF16 = jnp.bfloat16
EPS = 1e-6
NEG = -1e30
HEAD = 64
A_W, B_W, M_W = 384, 384, 256
N_FOX_HEADS = 6
CHUNK = 128
IN_COLS = 2 * A_W + 3 * B_W + N_FOX_HEADS + M_W
P_MAIN = 2 * A_W + 3 * B_W + M_W
P_COLS = P_MAIN + 128
F_END = 2 * A_W + 3 * B_W + N_FOX_HEADS
LANES = 128
Q_BLK, K_BLK = 512, 128
ROW_SPLIT = 4
ADAM_LR, ADAM_B1, ADAM_B2, ADAM_EPS, ADAM_WD, ADAM_STEP = 0.001, 0.9, 0.999, 1e-08, 0.01, 10
VMEM_LIMIT = 56 * 1024 * 1024
MESH = pl.DeviceIdType.MESH
ANY = pl.BlockSpec(memory_space=pl.ANY)
BS = pl.BlockSpec


def _cp(sem=None):
    return pltpu.CompilerParams(dimension_semantics=sem, vmem_limit_bytes=VMEM_LIMIT)


def _iota(shape, dim):
    return lax.broadcasted_iota(jnp.int32, shape, dim)


def _dot(a, b):
    return jnp.dot(a.astype(BF16), b.astype(BF16), preferred_element_type=F32)


def _dot_nt(a, b):
    return lax.dot_general(a.astype(BF16), b.astype(BF16), (((1,), (1,)), ((), ())), preferred_element_type=F32)


def _dot_tn(a, b):
    return lax.dot_general(a.astype(BF16), b.astype(BF16), (((0,), (0,)), ((), ())), preferred_element_type=F32)


def _rms(x, g):
    return x * lax.rsqrt(jnp.mean(x * x, axis=-1, keepdims=True) + EPS) * g


def _rms_bwd(x, g, dy):
    r = lax.rsqrt(jnp.mean(x * x, axis=-1, keepdims=True) + EPS)
    xr = x * r
    gd = dy * g
    m = jnp.mean(gd * xr, axis=-1, keepdims=True)
    return (gd - xr * m) * r, _colsum(dy * xr)


def _gelu(x):
    return 0.5 * x * (1.0 + jnp.tanh(0.7978845608028654 * (x + 0.044715 * (x * x * x))))


def _sigmoid(x):
    return 1.0 / (1.0 + jnp.exp(-x))


def _silu_mul(g, u):
    return g * _sigmoid(g) * u


def _logsig(x):
    return jnp.minimum(x, 0.0) - jnp.log(1.0 + jnp.exp(-jnp.abs(x)))


def _colsum(x):
    return jnp.sum(x, axis=0, keepdims=True)


def _acc(ref, val, first):
    @pl.when(first)
    def _():
        ref[...] = val

    @pl.when(jnp.logical_not(first))
    def _():
        ref[...] += val


def _inproj_fwd(x2d, g_pre, w_in_p, tm):
    T, D = x2d.shape
    CH = 768
    nchunk = P_COLS // CH
    ns, _, dsh = w_in_p.shape

    def body(x_ref, g_ref, w_ref, h_ref, proj_ref, fl_ref):
        h = _rms(x_ref[...], g_ref[...]).astype(BF16)
        h_ref[...] = h
        for n in range(nchunk):
            rows = slice(n * CH, (n + 1) * CH)
            r = _dot_nt(h[:, 0:dsh], w_ref[0, rows, :])
            for s in range(1, ns):
                r = r + _dot_nt(h[:, s * dsh:(s + 1) * dsh], w_ref[s, rows, :])
            if n < nchunk - 1:
                proj_ref[:, rows] = r.astype(BF16)
            else:
                fg = 1920 - n * CH
                proj_ref[:, n * CH:1920] = r[:, :fg].astype(BF16)
                fl_ref[...] = r[:, fg:fg + LANES]
                proj_ref[:, 1920:P_MAIN] = r[:, fg + LANES:].astype(BF16)

    return pl.pallas_call(
        body, name="inproj_fwd", grid=(T // tm,),
        in_specs=[BS((tm, D), lambda i: (i, 0)), BS((1, D), lambda i: (0, 0)),
                  BS((ns, P_COLS, dsh), lambda i: (0, 0, 0))],
        out_specs=[BS((tm, D), lambda i: (i, 0)), BS((tm, P_MAIN), lambda i: (i, 0)), BS((tm, LANES), lambda i: (i, 0))],
        out_shape=[jax.ShapeDtypeStruct((T, D), BF16), jax.ShapeDtypeStruct((T, P_MAIN), BF16),
                   jax.ShapeDtypeStruct((T, LANES), F32)],
        compiler_params=_cp(("arbitrary",)),
    )(x2d, g_pre, w_in_p)


def _gate_fwd(flog3, bf_row):
    Bl, S, _ = flog3.shape
    nb = S // LANES

    def body(f_ref, b_ref, bq_ref, bk_ref, fr_ref):
        row = _iota((LANES, LANES), 0)
        lane = _iota((LANES, LANES), 1)
        one = jnp.ones((LANES, LANES), BF16)
        zero = jnp.zeros((LANES, LANES), BF16)

        carry = jnp.zeros((1, LANES), F32)
        for j in range(nb):
            r0 = j * LANES
            fl = f_ref[0, pl.ds(r0, LANES), :] + b_ref[...]
            fr_ref[0, j] = fl.T[0:8, :]
            c = _logsig(fl)
            for k in (1, 2, 4, 8, 16, 32, 64):
                c = c + jnp.where(row >= k, pltpu.roll(c, k, 0), 0.0)
            total = _colsum(jnp.where(row == LANES - 1, c, 0.0))
            c = c + carry
            carry = carry + total
            for h in range(N_FOX_HEADS):
                col = jnp.sum(jnp.where(lane == h, c, 0.0), axis=1, keepdims=True)
                hi = col.astype(BF16)
                rest = col - hi.astype(F32)
                mid = rest.astype(BF16)
                lo = (rest - mid.astype(F32)).astype(BF16)
                base = _bias_lane(h)
                bq = jnp.where(lane == base, hi, jnp.where(lane == base + 1, mid, jnp.where(lane == base + 2, lo, zero)))
                bq = jnp.where((lane >= base + 3) & (lane < base + 6), one, bq)
                bk = jnp.where(lane == base + 3, -hi, jnp.where(lane == base + 4, -mid, jnp.where(lane == base + 5, -lo, zero)))
                bk = jnp.where((lane >= base) & (lane < base + 3), one, bk)
                bq_ref[0, h, pl.ds(r0, LANES), :] = bq
                bk_ref[0, h, pl.ds(r0, LANES), :] = bk

    slab = BS((1, N_FOX_HEADS, S, LANES), lambda b: (b, 0, 0, 0))
    return pl.pallas_call(
        body, name="gate_fwd", grid=(Bl,),
        in_specs=[BS((1, S, LANES), lambda b: (b, 0, 0)), BS((1, LANES), lambda b: (0, 0))],
        out_specs=[slab, slab, BS((1, nb, 8, LANES), lambda b: (b, 0, 0, 0))],
        out_shape=[jax.ShapeDtypeStruct((Bl, N_FOX_HEADS, S, LANES), BF16),
                   jax.ShapeDtypeStruct((Bl, N_FOX_HEADS, S, LANES), BF16),
                   jax.ShapeDtypeStruct((Bl, nb, 8, LANES), F32)],
        compiler_params=_cp(("arbitrary",)),
    )(flog3, bf_row)


def _bias_lane(h):
    return HEAD if h % 2 == 0 else 0


def _sgu_pre(zu, zv, g_sgu):
    return _gelu(zu), _rms(_gelu(zv), g_sgu)


def _sgu_fwd(proj, g_sgu, ws_tril, bs_full, tm):
    T = proj.shape[0]
    nch = tm // CHUNK

    def body(zu_ref, zv_ref, g_ref, ws_ref, b_ref, ya_ref):
        lane = _iota((CHUNK, LANES), 1)
        u, vn = _sgu_pre(zu_ref[...].astype(F32), zv_ref[...].astype(F32), g_ref[...])
        vn = vn.astype(BF16)
        for c in range(nch):
            rs = slice(c * CHUNK, (c + 1) * CHUNK)
            for j in range(3):
                cs = slice(j * LANES, (j + 1) * LANES)
                vp = vn[rs, cs]
                z = jnp.where(lane < HEAD, _dot(ws_ref[2 * j], vp), _dot(ws_ref[2 * j + 1], vp)) + b_ref[:, cs]
                ya_ref[rs, cs] = (u[rs, cs] * z).astype(BF16)

    return pl.pallas_call(
        body, name="sgu_fwd", grid=(T // tm,),
        in_specs=[BS((tm, A_W), lambda i: (i, 0)), BS((tm, A_W), lambda i: (i, 1)), BS((1, A_W), lambda i: (0, 0)),
                  BS((6, CHUNK, CHUNK), lambda i: (0, 0, 0)), BS((CHUNK, A_W), lambda i: (0, 0))],
        out_specs=BS((tm, A_W), lambda i: (i, 0)),
        out_shape=jax.ShapeDtypeStruct((T, A_W), BF16),
        compiler_params=_cp(("arbitrary",)),
    )(proj, proj, g_sgu, ws_tril, bs_full)


def _fox_fwd(proj, bq, bk, Bl, S):
    T = Bl * S
    nq = S // Q_BLK
    qc, kc, vc = 768 // LANES, 1152 // LANES, 1536 // LANES

    def body(q_ref, k_ref, v_ref, bq_ref, bk_ref, o_ref, lse_ref, ka_ref, va_ref):
        lane_s = _iota((S, LANES), 1)
        lane = _iota((Q_BLK, LANES), 1)
        tri = _iota((Q_BLK, Q_BLK), 1) <= _iota((Q_BLK, Q_BLK), 0)
        k = k_ref[...]
        v = v_ref[...]
        for hh in range(2):
            data = (lane_s < HEAD) if hh == 0 else (lane_s >= HEAD)
            ka_ref[hh] = jnp.where(data, k, bk_ref[0, hh])
            va_ref[hh] = jnp.where(lane_s == _bias_lane(hh), jnp.ones_like(v), v)
        for i in range(nq):
            r0 = i * Q_BLK
            q = q_ref[r0:r0 + Q_BLK, :]
            o_out = jnp.zeros((Q_BLK, LANES), F32)
            lse_out = jnp.zeros((Q_BLK, LANES), F32)
            for hh in range(2):
                hmask = (lane < HEAD) if hh == 0 else (lane >= HEAD)
                qa = jnp.where(hmask, q * 0.125, bq_ref[0, hh, r0:r0 + Q_BLK, :])
                sd = jnp.where(tri, _dot_nt(qa, ka_ref[hh, r0:r0 + Q_BLK, :]), NEG)
                m = jnp.max(sd, axis=1, keepdims=True)
                if i:
                    sf = _dot_nt(qa, ka_ref[hh, 0:r0, :])
                    m = jnp.maximum(m, jnp.max(sf, axis=1, keepdims=True))
                acc = _dot(jnp.exp(sd - m), va_ref[hh, r0:r0 + Q_BLK, :])
                if i:
                    acc = acc + _dot(jnp.exp(sf - m), va_ref[hh, 0:r0, :])
                l = jnp.sum(jnp.where(lane == _bias_lane(hh), acc, 0.0), axis=1, keepdims=True)
                o_out = jnp.where(hmask, acc / l, o_out)
                lse_out = jnp.where(hmask, m + jnp.log(l), lse_out)
            o_ref[r0:r0 + Q_BLK, :] = o_out.astype(BF16)
            lse_ref[0, r0:r0 + Q_BLK, :] = lse_out

    seq = lambda c0: BS((S, LANES), lambda b, p: (b, c0 + p))
    pair = BS((1, 2, S, LANES), lambda b, p: (b, p, 0, 0))
    return pl.pallas_call(
        body, name="fox_fwd", grid=(Bl, 3),
        in_specs=[seq(qc), seq(kc), seq(vc), pair, pair],
        out_specs=[seq(0), BS((1, S, LANES), lambda b, p: (p, b, 0))],
        out_shape=[jax.ShapeDtypeStruct((T, B_W), BF16), jax.ShapeDtypeStruct((3, T, LANES), F32)],
        scratch_shapes=[pltpu.VMEM((2, S, LANES), BF16), pltpu.VMEM((2, S, LANES), BF16)],
        compiler_params=_cp(("arbitrary", "arbitrary")),
    )(proj, proj, proj, bq, bk)


def _memkv_fwd(mem, g_mem, w_kv):
    Bl, Mt, D = mem.shape

    def body(m_ref, g_ref, w_ref, mn_ref, kv_ref):
        mn = _rms(m_ref[0], g_ref[...]).astype(BF16)
        mn_ref[0] = mn
        kv_ref[0] = jnp.dot(mn, w_ref[...], preferred_element_type=F32).astype(BF16)

    return pl.pallas_call(
        body, name="memkv_fwd", grid=(Bl,),
        in_specs=[BS((1, Mt, D), lambda b: (b, 0, 0)), BS((1, D), lambda b: (0, 0)), BS((D, 2 * M_W), lambda b: (0, 0))],
        out_specs=[BS((1, Mt, D), lambda b: (b, 0, 0)), BS((1, Mt, 2 * M_W), lambda b: (b, 0, 0))],
        out_shape=[jax.ShapeDtypeStruct((Bl, Mt, D), BF16), jax.ShapeDtypeStruct((Bl, Mt, 2 * M_W), BF16)],
        compiler_params=_cp(("arbitrary",)),
    )(mem, g_mem, w_kv)


def _memattn_fwd(proj, kv, Bl, S, tq):
    T = Bl * S
    nq = S // tq
    Mt = kv.shape[1]
    qc = 1920 // LANES

    def body(q_ref, km_ref, vm_ref, o_ref):
        lane = _iota((tq, LANES), 1)
        q = q_ref[...]
        out = jnp.zeros((tq, LANES), F32)
        for hh in range(2):
            hmask = (lane < HEAD) if hh == 0 else (lane >= HEAD)
            qs = jnp.where(hmask, q, jnp.zeros_like(q)) * 0.125
            s = _dot_nt(qs, km_ref[0])
            pe = jnp.exp(s - jnp.max(s, axis=1, keepdims=True))
            pn = pe / jnp.sum(pe, axis=1, keepdims=True)
            out = jnp.where(hmask, _dot(pn, vm_ref[0]), out)
        o_ref[...] = out.astype(BF16)

    return pl.pallas_call(
        body, name="memattn_fwd", grid=(Bl, 2, nq),
        in_specs=[BS((tq, LANES), lambda b, p, i: (b * nq + i, qc + p)),
                  BS((1, Mt, LANES), lambda b, p, i: (b, 0, p)),
                  BS((1, Mt, LANES), lambda b, p, i: (b, 0, 2 + p))],
        out_specs=BS((tq, LANES), lambda b, p, i: (b * nq + i, p)),
        out_shape=jax.ShapeDtypeStruct((T, M_W), BF16),
        compiler_params=_cp(("arbitrary", "arbitrary", "arbitrary")),
    )(proj, kv, kv)


def _mix_norms(ya, yb, ym, ga, gb, gm):
    return _rms(ya, ga), _rms(yb, gb), _rms(ym, gm)


def _outproj_fwd(ya, yb, ym, x2d, ga, gb, gm, g_post, g_pre2, w_out, tm):
    T, D = x2d.shape

    def body(ya_ref, yb_ref, ym_ref, x_ref, ga_ref, gb_ref, gm_ref, gp_ref, g2_ref, w_ref,
             y_ref, o_ref, x1_ref, h2_ref):
        na, nb_, nm = _mix_norms(ya_ref[...].astype(F32), yb_ref[...].astype(F32), ym_ref[...].astype(F32),
                                 ga_ref[...], gb_ref[...], gm_ref[...])
        y_ref[:, 0:A_W] = na.astype(BF16)
        y_ref[:, A_W:A_W + B_W] = nb_.astype(BF16)
        y_ref[:, A_W + B_W:] = nm.astype(BF16)
        o = jnp.dot(y_ref[...], w_ref[...], preferred_element_type=F32).astype(BF16)
        o_ref[...] = o
        x1 = x_ref[...] + _rms(o.astype(F32), gp_ref[...])
        x1_ref[...] = x1
        h2_ref[...] = _rms(x1, g2_ref[...]).astype(BF16)

    row = lambda w: BS((tm, w), lambda i: (i, 0))
    vec = lambda w: BS((1, w), lambda i: (0, 0))
    return pl.pallas_call(
        body, name="outproj_fwd", grid=(T // tm,),
        in_specs=[row(A_W), row(B_W), row(M_W), row(D), vec(A_W), vec(B_W), vec(M_W), vec(D), vec(D),
                  BS((A_W + B_W + M_W, D), lambda i: (0, 0))],
        out_specs=[row(A_W + B_W + M_W), row(D), row(D), row(D)],
        out_shape=[jax.ShapeDtypeStruct((T, A_W + B_W + M_W), BF16), jax.ShapeDtypeStruct((T, D), BF16),
                   jax.ShapeDtypeStruct((T, D), F32), jax.ShapeDtypeStruct((T, D), BF16)],
        compiler_params=_cp(("arbitrary",)),
    )(ya, yb, ym, x2d, ga, gb, gm, g_post, g_pre2, w_out)


def _ffn_fwd(h2, x1, target, wg, wu, wd, g_post, tm):
    T, D = x1.shape
    ns, F, _ = wg.shape

    def body(h_ref, x1_ref, t_ref, wg_ref, wu_ref, wd_ref, gp_ref,
             gs_ref, us_ref, dff_ref, dx2_ref, dgp_ref, loss_ref, acc_ref):
        j = pl.program_id(0)
        i = pl.program_id(1)
        rows = pl.ds(pl.multiple_of(i * tm, tm), tm)
        h = h_ref[...]
        g = _dot_nt(h, wg_ref[0])
        u = _dot_nt(h, wu_ref[0])
        gs_ref[0] = g.astype(BF16)
        us_ref[0] = u.astype(BF16)
        part = _dot(_silu_mul(g, u), wd_ref[0])

        @pl.when(j == 0)
        def _():
            acc_ref[rows, :] = part

        @pl.when(j != 0)
        def _():
            acc_ref[rows, :] += part

        @pl.when(j == ns - 1)
        def _():
            ff = acc_ref[rows, :]
            diff = x1_ref[...] + _rms(ff, gp_ref[...]) - t_ref[...]
            dx2 = diff * (1.0 / D)
            dff, dgp = _rms_bwd(ff, gp_ref[...], dx2)
            dx2_ref[...] = dx2
            dff_ref[...] = dff.astype(BF16)
            lpart = jnp.sum(_colsum(diff * diff), axis=1, keepdims=True) * (0.5 / D)
            _acc(dgp_ref, dgp, i == 0)
            _acc(loss_ref, jnp.broadcast_to(lpart, (1, LANES)), i == 0)

    last = lambda j, i: (jnp.where(j == ns - 1, i, 0), 0)
    wsh = BS((1, F, D), lambda j, i: (j, 0, 0))
    sh = BS((1, tm, F), lambda j, i: (j, i, 0))
    return pl.pallas_call(
        body, name="ffn_fwd", grid=(ns, T // tm),
        in_specs=[BS((tm, D), lambda j, i: (i, 0)), BS((tm, D), last), BS((tm, D), last), wsh, wsh, wsh,
                  BS((1, D), lambda j, i: (0, 0))],
        out_specs=[sh, sh, BS((tm, D), last), BS((tm, D), last),
                   BS((1, D), lambda j, i: (0, 0)), BS((1, LANES), lambda j, i: (0, 0))],
        out_shape=[jax.ShapeDtypeStruct((ns, T, F), BF16), jax.ShapeDtypeStruct((ns, T, F), BF16),
                   jax.ShapeDtypeStruct((T, D), BF16), jax.ShapeDtypeStruct((T, D), F32),
                   jax.ShapeDtypeStruct((1, D), F32), jax.ShapeDtypeStruct((1, LANES), F32)],
        scratch_shapes=[pltpu.VMEM((T, D), F32)],
        compiler_params=_cp(("arbitrary", "arbitrary")),
    )(h2, x1, target, wg, wu, wd, g_post)


def _ffn_bwd(dff, h2, gs, us, wg, wu, wd, tm):
    T, D = h2.shape
    ns, F, _ = wg.shape

    def body(dff_ref, h_ref, gs_ref, us_ref, wg_ref, wu_ref, wd_ref, dh_ref, dwg_out, dwu_out, dwd_out,
             dwg_ref, dwu_ref, dwd_ref):
        first = pl.program_id(1) == 0
        dff = dff_ref[...]
        h = h_ref[...]
        parts = []
        for r in range(ROW_SPLIT):
            rows = slice(r * (tm // ROW_SPLIT), (r + 1) * (tm // ROW_SPLIT))
            dact = _dot_nt(dff[rows], wd_ref[0])
            g = gs_ref[0, rows, :].astype(F32)
            u = us_ref[0, rows, :].astype(F32)
            sig = _sigmoid(g)
            gsig = g * sig
            dg = (dact * u * (sig + gsig * (1.0 - sig))).astype(BF16)
            du = (dact * gsig).astype(BF16)
            dh_ref[0, rows, :] = (_dot(dg, wg_ref[0]) + _dot(du, wu_ref[0])).astype(BF16)
            parts.append(((gsig * u).astype(BF16), dg, du))
        a, dg, du = [jnp.concatenate(p, axis=0) for p in zip(*parts)]
        _acc(dwd_ref, _dot_tn(a, dff), first)
        _acc(dwg_ref, _dot_tn(dg, h), first)
        _acc(dwu_ref, _dot_tn(du, h), first)

        @pl.when(pl.program_id(1) == pl.num_programs(1) - 1)
        def _():
            dwg_out[0] = dwg_ref[...].astype(BF16)
            dwu_out[0] = dwu_ref[...].astype(BF16)
            dwd_out[0] = dwd_ref[...].astype(BF16)

    row = BS((tm, D), lambda j, i: (i, 0))
    sh = BS((1, tm, F), lambda j, i: (j, i, 0))
    wsh = BS((1, F, D), lambda j, i: (j, 0, 0))
    return pl.pallas_call(
        body, name="ffn_bwd", grid=(ns, T // tm),
        in_specs=[row, row, sh, sh, wsh, wsh, wsh],
        out_specs=[BS((1, tm, D), lambda j, i: (j, i, 0)), wsh, wsh, wsh],
        out_shape=[jax.ShapeDtypeStruct((ns, T, D), BF16)] + [jax.ShapeDtypeStruct((ns, F, D), BF16)] * 3,
        scratch_shapes=[pltpu.VMEM((F, D), F32)] * 3,
        compiler_params=_cp(("arbitrary", "arbitrary")),
    )(dff, h2, gs, us, wg, wu, wd)


def _mm_tn(a, b, name, tk):
    T, M = a.shape
    N = b.shape[1]
    tk = min(tk, T)

    def body(a_ref, b_ref, o_ref):
        _acc(o_ref, _dot_tn(a_ref[...], b_ref[...]), pl.program_id(0) == 0)

    return pl.pallas_call(
        body, name=name, grid=(T // tk,),
        in_specs=[BS((tk, M), lambda t: (t, 0)), BS((tk, N), lambda t: (t, 0))],
        out_specs=BS((M, N), lambda t: (0, 0)),
        out_shape=jax.ShapeDtypeStruct((M, N), F32),
        compiler_params=_cp(("arbitrary",)),
    )(a, b)


DPROJ_PIECES = ((0, A_W), (A_W, A_W), (768, B_W), (1152, B_W), (1536, B_W), (1920, LANES), (2048, M_W))


def _put_dproj(dp_ref, piece_refs):
    for (c0, w), ref in zip(DPROJ_PIECES, piece_refs):
        dp_ref[:, c0:c0 + w] = ref[...].astype(BF16)


def _dw_in(pieces, h, ns, tk):
    T, D = h.shape
    M = P_COLS
    dsh = D // ns
    tk = min(tk, T)

    def body(*refs):
        piece_refs, h_ref, o_ref, acc_ref, dp_ref = refs[:7], refs[7], refs[8], refs[9], refs[10]
        t = pl.program_id(0)
        _put_dproj(dp_ref, piece_refs)
        _acc(acc_ref, _dot_tn(h_ref[...], dp_ref[...]), t == 0)

        @pl.when(t == pl.num_programs(0) - 1)
        def _():
            for s in range(ns):
                o_ref[s] = acc_ref[s * dsh:(s + 1) * dsh, :].T.astype(BF16)

    return pl.pallas_call(
        body, name="dw_in", grid=(T // tk,),
        in_specs=[BS((tk, w), lambda t: (t, 0)) for _, w in DPROJ_PIECES] + [BS((tk, D), lambda t: (t, 0))],
        out_specs=BS((ns, M, dsh), lambda t: (0, 0, 0)),
        out_shape=jax.ShapeDtypeStruct((ns, M, dsh), BF16),
        scratch_shapes=[pltpu.VMEM((D, M), F32), pltpu.VMEM((tk, M), BF16)],
        compiler_params=_cp(("arbitrary",)),
    )(*pieces, h)


def _outproj_bwd(dh2, x1, dx2, o, ya, yb, ym, ga, gb, gm, g_post, g_pre2, w_out, tm):
    T, D = x1.shape
    ns = dh2.shape[0]

    def body(dh_ref, x1_ref, dx2_ref, o_ref, ya_ref, yb_ref, ym_ref, ga_ref, gb_ref, gm_ref, gp_ref, g2_ref, w_ref,
             dx1_ref, do_ref, dya_ref, dyb_ref, dym_ref, dga_ref, dgb_ref, dgm_ref, dgp_ref, dg2_ref):
        first = pl.program_id(0) == 0
        dh = dh_ref[0].astype(F32)
        for j in range(1, ns):
            dh = dh + dh_ref[j].astype(F32)
        dxa, dg2 = _rms_bwd(x1_ref[...], g2_ref[...], dh)
        dx1 = dx2_ref[...] + dxa
        dx1_ref[...] = dx1
        _acc(dg2_ref, dg2, first)
        do, dgp = _rms_bwd(o_ref[...].astype(F32), gp_ref[...], dx1)
        do = do.astype(BF16)
        do_ref[...] = do
        dy = _dot_nt(do, w_ref[...])
        dya, dga = _rms_bwd(ya_ref[...].astype(F32), ga_ref[...], dy[:, 0:A_W])
        dyb, dgb = _rms_bwd(yb_ref[...].astype(F32), gb_ref[...], dy[:, A_W:A_W + B_W])
        dym, dgm = _rms_bwd(ym_ref[...].astype(F32), gm_ref[...], dy[:, A_W + B_W:])
        dya_ref[...] = dya.astype(BF16)
        dyb_ref[...] = dyb.astype(BF16)
        dym_ref[...] = dym.astype(BF16)
        _acc(dga_ref, dga, first)
        _acc(dgb_ref, dgb, first)
        _acc(dgm_ref, dgm, first)
        _acc(dgp_ref, dgp, first)

    row = lambda w: BS((tm, w), lambda i: (i, 0))
    vec = lambda w: BS((1, w), lambda i: (0, 0))
    sds = jax.ShapeDtypeStruct
    return pl.pallas_call(
        body, name="outproj_bwd", grid=(T // tm,),
        in_specs=[BS((ns, tm, D), lambda i: (0, i, 0)), row(D), row(D), row(D), row(A_W), row(B_W), row(M_W),
                  vec(A_W), vec(B_W), vec(M_W), vec(D), vec(D), BS((A_W + B_W + M_W, D), lambda i: (0, 0))],
        out_specs=[row(D), row(D), row(A_W), row(B_W), row(M_W), vec(A_W), vec(B_W), vec(M_W), vec(D), vec(D)],
        out_shape=[sds((T, D), F32), sds((T, D), BF16), sds((T, A_W), BF16), sds((T, B_W), BF16), sds((T, M_W), BF16),
                   sds((1, A_W), F32), sds((1, B_W), F32), sds((1, M_W), F32), sds((1, D), F32), sds((1, D), F32)],
        compiler_params=_cp(("arbitrary",)),
    )(dh2, x1, dx2, o, ya, yb, ym, ga, gb, gm, g_post, g_pre2, w_out)


def _sgu_bwd(proj, dya, g_sgu, ws_tril, bs_full, tm):
    T = proj.shape[0]
    nch = tm // CHUNK

    def body(zu_ref, zv_ref, dy_ref, g_ref, ws_ref, b_ref, dzu_ref, dzv_ref, dws_ref, dbs_ref, dg_ref,
             du_ref, dvn_ref, dbf_ref):
        step = pl.program_id(0)
        first = step == 0
        lane = _iota((CHUNK, LANES), 1)
        tril = _iota((CHUNK, CHUNK), 0) >= _iota((CHUNK, CHUNK), 1)
        (u, vn), vjp = jax.vjp(_sgu_pre, zu_ref[...].astype(F32), zv_ref[...].astype(F32), g_ref[...])
        vnb = vn.astype(BF16)
        dy = dy_ref[...].astype(F32)

        @pl.when(first)
        def _():
            dws_ref[...] = jnp.zeros_like(dws_ref)
            dbf_ref[...] = jnp.zeros_like(dbf_ref)

        for c in range(nch):
            rs = slice(c * CHUNK, (c + 1) * CHUNK)
            for j in range(3):
                cs = slice(j * LANES, (j + 1) * LANES)
                vp = vnb[rs, cs]
                z = jnp.where(lane < HEAD, _dot(ws_ref[2 * j], vp), _dot(ws_ref[2 * j + 1], vp)) + b_ref[:, cs]
                du_ref[rs, cs] = dy[rs, cs] * z
                dz = dy[rs, cs] * u[rs, cs]
                dbf_ref[:, cs] += dz
                dzb = dz.astype(BF16)
                dz0 = jnp.where(lane < HEAD, dzb, jnp.zeros_like(dzb))
                dz1 = jnp.where(lane >= HEAD, dzb, jnp.zeros_like(dzb))
                dvn_ref[rs, cs] = jnp.where(lane < HEAD, _dot_tn(ws_ref[2 * j], dzb), _dot_tn(ws_ref[2 * j + 1], dzb))
                dws_ref[2 * j] += jnp.where(tril, _dot_nt(dz0, vp), 0.0)
                dws_ref[2 * j + 1] += jnp.where(tril, _dot_nt(dz1, vp), 0.0)
        dzu, dzv, dg = vjp((du_ref[...], dvn_ref[...]))
        dzu_ref[...] = dzu.astype(BF16)
        dzv_ref[...] = dzv.astype(BF16)
        _acc(dg_ref, dg, first)

        @pl.when(step == pl.num_programs(0) - 1)
        def _():
            out = jnp.zeros((CHUNK, LANES), F32)
            for j in range(3):
                slab = dbf_ref[:, j * LANES:(j + 1) * LANES]
                lo = jnp.sum(jnp.where(lane < HEAD, slab, 0.0), axis=1, keepdims=True)
                hi = jnp.sum(jnp.where(lane >= HEAD, slab, 0.0), axis=1, keepdims=True)
                out = out + jnp.where(lane == 2 * j, lo, 0.0) + jnp.where(lane == 2 * j + 1, hi, 0.0)
            dbs_ref[...] = out

    return pl.pallas_call(
        body, name="sgu_bwd", grid=(T // tm,),
        in_specs=[BS((tm, A_W), lambda i: (i, 0)), BS((tm, A_W), lambda i: (i, 1)), BS((tm, A_W), lambda i: (i, 0)),
                  BS((1, A_W), lambda i: (0, 0)), BS((6, CHUNK, CHUNK), lambda i: (0, 0, 0)),
                  BS((CHUNK, A_W), lambda i: (0, 0))],
        out_specs=[BS((tm, A_W), lambda i: (i, 0)), BS((tm, A_W), lambda i: (i, 0)),
                   BS((6, CHUNK, CHUNK), lambda i: (0, 0, 0)), BS((CHUNK, LANES), lambda i: (0, 0)),
                   BS((1, A_W), lambda i: (0, 0))],
        out_shape=[jax.ShapeDtypeStruct((T, A_W), BF16), jax.ShapeDtypeStruct((T, A_W), BF16),
                   jax.ShapeDtypeStruct((6, CHUNK, CHUNK), F32), jax.ShapeDtypeStruct((CHUNK, LANES), F32),
                   jax.ShapeDtypeStruct((1, A_W), F32)],
        scratch_shapes=[pltpu.VMEM((tm, A_W), F32), pltpu.VMEM((tm, A_W), F32), pltpu.VMEM((CHUNK, A_W), F32)],
        compiler_params=_cp(("arbitrary",)),
    )(proj, proj, dya, g_sgu, ws_tril, bs_full)


def _memattn_bwd(proj, kv, dym, Bl, S, tq):
    T = Bl * S
    nq = S // tq
    Mt = kv.shape[1]
    qc = 1920 // LANES

    def body(q_ref, km_ref, vm_ref, do_ref, dq_ref, dkm_ref, dvm_ref):
        first = pl.program_id(2) == 0
        lane = _iota((tq, LANES), 1)
        q = q_ref[...]
        do = do_ref[...]
        dq_out = jnp.zeros((tq, LANES), F32)
        dkm = jnp.zeros((Mt, LANES), F32)
        dvm = jnp.zeros((Mt, LANES), F32)
        for hh in range(2):
            hmask = (lane < HEAD) if hh == 0 else (lane >= HEAD)
            qs = jnp.where(hmask, q, jnp.zeros_like(q)) * 0.125
            dom = jnp.where(hmask, do, 0.0).astype(BF16)
            s = _dot_nt(qs, km_ref[0])
            pe = jnp.exp(s - jnp.max(s, axis=1, keepdims=True))
            pn = pe / jnp.sum(pe, axis=1, keepdims=True)
            dp = _dot_nt(dom, vm_ref[0])
            ds = (pn * (dp - jnp.sum(pn * dp, axis=1, keepdims=True))).astype(BF16)
            dq_out = jnp.where(hmask, _dot(ds, km_ref[0]) * 0.125, dq_out)
            dkm = dkm + _dot_tn(ds, qs)
            dvm = dvm + _dot_tn(pn, dom)
        dq_ref[...] = dq_out.astype(BF16)
        _acc(dkm_ref, dkm[None], first)
        _acc(dvm_ref, dvm[None], first)

    return pl.pallas_call(
        body, name="memattn_bwd", grid=(Bl, 2, nq),
        in_specs=[BS((tq, LANES), lambda b, p, i: (b * nq + i, qc + p)),
                  BS((1, Mt, LANES), lambda b, p, i: (b, 0, p)),
                  BS((1, Mt, LANES), lambda b, p, i: (b, 0, 2 + p)),
                  BS((tq, LANES), lambda b, p, i: (b * nq + i, p))],
        out_specs=[BS((tq, LANES), lambda b, p, i: (b * nq + i, p)),
                   BS((1, Mt, LANES), lambda b, p, i: (b, 0, p)),
                   BS((1, Mt, LANES), lambda b, p, i: (b, 0, p))],
        out_shape=[jax.ShapeDtypeStruct((T, M_W), BF16), jax.ShapeDtypeStruct((Bl, Mt, M_W), F32),
                   jax.ShapeDtypeStruct((Bl, Mt, M_W), F32)],
        compiler_params=_cp(("arbitrary", "arbitrary", "arbitrary")),
    )(proj, kv, kv, dym)


def _memkv_bwd(dkm, dvm, memn, mem, g_mem, w_kv):
    Bl, Mt, D = mem.shape

    def body(dk_ref, dv_ref, mn_ref, m_ref, g_ref, w_ref, dw_ref, dg_ref):
        first = pl.program_id(0) == 0
        dk = dk_ref[0].astype(BF16)
        dv = dv_ref[0].astype(BF16)
        mn = mn_ref[0]
        dmn = _dot_nt(dk, w_ref[:, 0:M_W]) + _dot_nt(dv, w_ref[:, M_W:])
        _, dg = _rms_bwd(m_ref[0], g_ref[...], dmn)
        _acc(dg_ref, dg, first)

        @pl.when(first)
        def _():
            dw_ref[...] = jnp.zeros_like(dw_ref)

        dw_ref[:, 0:M_W] += _dot_tn(mn, dk)
        dw_ref[:, M_W:] += _dot_tn(mn, dv)

    return pl.pallas_call(
        body, name="memkv_bwd", grid=(Bl,),
        in_specs=[BS((1, Mt, M_W), lambda b: (b, 0, 0)), BS((1, Mt, M_W), lambda b: (b, 0, 0)),
                  BS((1, Mt, D), lambda b: (b, 0, 0)), BS((1, Mt, D), lambda b: (b, 0, 0)),
                  BS((1, D), lambda b: (0, 0)), BS((D, 2 * M_W), lambda b: (0, 0))],
        out_specs=[BS((D, 2 * M_W), lambda b: (0, 0)), BS((1, D), lambda b: (0, 0))],
        out_shape=[jax.ShapeDtypeStruct((D, 2 * M_W), F32), jax.ShapeDtypeStruct((1, D), F32)],
        compiler_params=_cp(("arbitrary",)),
    )(dkm, dvm, memn, mem, g_mem, w_kv)


def _fox_bwd(proj, dyb, lse, bq, bk, Bl, S):
    T = Bl * S
    nq = S // Q_BLK
    nb = S // LANES
    qc, kc, vc = 768 // LANES, 1152 // LANES, 1536 // LANES

    def body(q_ref, k_ref, v_ref, do_ref, lse_ref, bq_ref, bk_ref,
             dq_ref, dk_ref, dv_ref, dcr_ref, ka_ref, dka_ref, dva_ref):
        p = pl.program_id(1)
        lane_s = _iota((S, LANES), 1)
        lane = _iota((Q_BLK, LANES), 1)
        sub = _iota((8, LANES), 0)
        tri = _iota((Q_BLK, Q_BLK), 1) <= _iota((Q_BLK, Q_BLK), 0)
        k = k_ref[...]
        for hh in range(2):
            data = (lane_s < HEAD) if hh == 0 else (lane_s >= HEAD)
            ka_ref[hh] = jnp.where(data, k, bk_ref[0, hh])
        dka_ref[...] = jnp.zeros_like(dka_ref)
        dva_ref[...] = jnp.zeros_like(dva_ref)

        @pl.when(p == 0)
        def _():
            dcr_ref[...] = jnp.zeros_like(dcr_ref)

        def add_colsums(ds, first_blk, h):
            cs = _colsum(ds)
            for jb in range(ds.shape[1] // LANES):
                dcr_ref[0, first_blk + jb] += jnp.where(sub == h, cs[:, jb * LANES:(jb + 1) * LANES], 0.0)

        for i in range(nq):
            r0 = i * Q_BLK
            r1 = r0 + Q_BLK
            q = q_ref[r0:r1, :]
            do = do_ref[r0:r1, :]
            lse_b = lse_ref[0, r0:r1, :]
            dq_out = jnp.zeros((Q_BLK, LANES), F32)
            for hh in range(2):
                hmask = (lane < HEAD) if hh == 0 else (lane >= HEAD)
                h = 2 * p + hh
                qs = jnp.where(hmask, q * 0.125, jnp.zeros_like(q))
                qa = jnp.where(hmask, q * 0.125, bq_ref[0, hh, r0:r1, :])
                dob = jnp.where(hmask, do, 0.0).astype(BF16)
                lse_h = jnp.sum(jnp.where(lane == hh * HEAD, lse_b, 0.0), axis=1, keepdims=True)
                pd = jnp.where(tri, jnp.exp(_dot_nt(qa, ka_ref[hh, r0:r1, :]) - lse_h), 0.0)
                dpd = _dot_nt(dob, v_ref[r0:r1, :])
                delta = jnp.sum(pd * dpd, axis=1, keepdims=True)
                psum = jnp.sum(pd, axis=1, keepdims=True)
                if i:
                    pf = jnp.exp(_dot_nt(qa, ka_ref[hh, 0:r0, :]) - lse_h)
                    dpf = _dot_nt(dob, v_ref[0:r0, :])
                    delta = delta + jnp.sum(pf * dpf, axis=1, keepdims=True)
                    psum = psum + jnp.sum(pf, axis=1, keepdims=True)
                delta = delta / psum
                dsd = pd * (dpd - delta)
                add_colsums(dsd, r0 // LANES, h)
                dsd = dsd.astype(BF16)
                dq_h = _dot(dsd, k_ref[r0:r1, :])
                dka_ref[r0:r1, :] += _dot_tn(dsd, qs)
                dva_ref[r0:r1, :] += _dot_tn(pd, dob)
                if i:
                    dsf = pf * (dpf - delta)
                    add_colsums(dsf, 0, h)
                    dsf = dsf.astype(BF16)
                    dq_h = dq_h + _dot(dsf, k_ref[0:r0, :])
                    dka_ref[0:r0, :] += _dot_tn(dsf, qs)
                    dva_ref[0:r0, :] += _dot_tn(pf, dob)
                dq_out = jnp.where(hmask, dq_h * 0.125, dq_out)
            dq_ref[r0:r1, :] = dq_out.astype(BF16)
        dk_ref[...] = dka_ref[...].astype(BF16)
        dv_ref[...] = dva_ref[...].astype(BF16)

    seq = lambda c0: BS((S, LANES), lambda b, p: (b, c0 + p))
    pair = BS((1, 2, S, LANES), lambda b, p: (b, p, 0, 0))
    rowblk = BS((1, nb, 8, LANES), lambda b, p: (b, 0, 0, 0))
    return pl.pallas_call(
        body, name="fox_bwd", grid=(Bl, 3),
        in_specs=[seq(qc), seq(kc), seq(vc), seq(0), BS((1, S, LANES), lambda b, p: (p, b, 0)), pair, pair],
        out_specs=[seq(0), seq(0), seq(0), rowblk],
        out_shape=[jax.ShapeDtypeStruct((T, B_W), BF16)] * 3 + [jax.ShapeDtypeStruct((Bl, nb, 8, LANES), F32)],
        scratch_shapes=[pltpu.VMEM((2, S, LANES), BF16), pltpu.VMEM((S, LANES), F32), pltpu.VMEM((S, LANES), F32)],
        compiler_params=_cp(("arbitrary", "arbitrary")),
    )(proj, proj, proj, dyb, lse, bq, bk)


def _gate_bwd(dc_row, fl_row):
    Bl, nb, _, _ = dc_row.shape

    def body(dc_ref, fl_ref, o_ref):
        lane = _iota((8, LANES), 1)

        carry = jnp.zeros((8, 1), F32)
        for j in reversed(range(nb)):
            r = -dc_ref[0, j]
            for k in (1, 2, 4, 8, 16, 32, 64):
                r = r + jnp.where(lane < LANES - k, pltpu.roll(r, LANES - k, 1), 0.0)
            total = jnp.sum(jnp.where(lane == 0, r, 0.0), axis=1, keepdims=True)
            dfl = (r + carry) * _sigmoid(-fl_ref[0, j])
            carry = carry + total
            o_ref[0, j * LANES:(j + 1) * LANES, :] = jnp.concatenate(
                [dfl, jnp.zeros((LANES - 8, LANES), F32)], axis=0).T

    rowblk = BS((1, nb, 8, LANES), lambda b: (b, 0, 0, 0))
    return pl.pallas_call(
        body, name="gate_bwd", grid=(Bl,),
        in_specs=[rowblk, rowblk],
        out_specs=BS((1, nb * LANES, LANES), lambda b: (b, 0, 0)),
        out_shape=jax.ShapeDtypeStruct((Bl, nb * LANES, LANES), F32),
        compiler_params=_cp(("arbitrary",)),
    )(dc_row, fl_row)


def _inproj_bwd(pieces, x2d, dx1, g_pre, w_in_p, tm):
    T, D = x2d.shape
    ns, _, dsh = w_in_p.shape

    def body(*refs):
        piece_refs = refs[:7]
        x_ref, dx1_ref, g_ref, w_ref, gx_ref, dg_ref, dbf_ref, dp_ref = refs[7:]
        first = pl.program_id(0) == 0
        _put_dproj(dp_ref, piece_refs)
        dh = jnp.concatenate([_dot(dp_ref[...], w_ref[s]) for s in range(ns)], axis=1)
        dxa, dg = _rms_bwd(x_ref[...], g_ref[...], dh)
        gx_ref[...] = dx1_ref[...] + dxa
        _acc(dg_ref, dg, first)
        _acc(dbf_ref, _colsum(piece_refs[5][...]), first)

    row = lambda w: BS((tm, w), lambda i: (i, 0))
    return pl.pallas_call(
        body, name="inproj_bwd", grid=(T // tm,),
        in_specs=[row(w) for _, w in DPROJ_PIECES] + [row(D), row(D), BS((1, D), lambda i: (0, 0)),
                                                      BS((ns, P_COLS, dsh), lambda i: (0, 0, 0))],
        out_specs=[row(D), BS((1, D), lambda i: (0, 0)), BS((1, LANES), lambda i: (0, 0))],
        out_shape=[jax.ShapeDtypeStruct((T, D), F32), jax.ShapeDtypeStruct((1, D), F32),
                   jax.ShapeDtypeStruct((1, LANES), F32)],
        scratch_shapes=[pltpu.VMEM((tm, P_COLS), BF16)],
        compiler_params=_cp(("arbitrary",)),
    )(*pieces, x2d, dx1, g_pre, w_in_p)


def _local_step(x, mem, target, W, P, reduce=None):
    Bl, S, D = x.shape
    T = Bl * S
    tm = min(512, T)
    x2d = x.reshape(T, D)
    t2d = target.reshape(T, D)
    vec = lambda a: a.reshape(1, -1)
    bf_row = jnp.pad(P["b_f"].reshape(1, -1), ((0, 0), (0, LANES - N_FOX_HEADS)))
    tril = jnp.tril(jnp.ones((CHUNK, CHUNK), bool))
    ws_tril = jnp.where(tril[None], P["w_s"][0], 0.0).astype(BF16)
    bs_full = jnp.repeat(P["b_s"][0].T, HEAD, axis=1)
    g_pre, g_sgu = vec(P["g_pre_mix"]), vec(P["g_sgu"])
    ga, gb, gm = vec(P["g_out_a"]), vec(P["g_out_b"]), vec(P["g_out_m"])
    g_mem, g_post, g_pre2, g_post2 = vec(P["g_mem"]), vec(P["g_post_mix"]), vec(P["g_pre_ffn"]), vec(P["g_post_ffn"])

    h, proj, flog = _inproj_fwd(x2d, g_pre, W["w_in"], tm)
    bq, bk, fl_row = _gate_fwd(flog.reshape(Bl, S, LANES), bf_row)
    ya = _sgu_fwd(proj, g_sgu, ws_tril, bs_full, tm)
    yb, lse = _fox_fwd(proj, bq, bk, Bl, S)
    memn, kv = _memkv_fwd(mem, g_mem, W["w_mem_kv"])
    ym = _memattn_fwd(proj, kv, Bl, S, min(2048, S))
    y, o, x1, h2 = _outproj_fwd(ya, yb, ym, x2d, ga, gb, gm, g_post, g_pre2, W["w_out"], tm)
    gs, us, dff, dx2, dg_post2, loss = _ffn_fwd(h2, x1, t2d, W["w_gate"], W["w_up"], W["w_down"], g_post2, tm)

    dh2, d_w_gate, d_w_up, d_w_down = _ffn_bwd(dff, h2, gs, us, W["w_gate"], W["w_up"], W["w_down"], min(1024, T))
    ffn = [d_w_gate, d_w_up, d_w_down]
    if reduce is not None:
        pending, _ = reduce.begin("ffn", ffn)
    dx1, do, dya, dyb, dym, dga, dgb, dgm, dg_post, dg_pre2 = _outproj_bwd(
        dh2, x1, dx2, o, ya, yb, ym, ga, gb, gm, g_post, g_pre2, W["w_out"], tm)
    if reduce is not None:
        ffn, (do, dya, dyb, dym) = reduce.finish("ffn", pending, (do, dya, dyb, dym))
    d_w_out = _mm_tn(y, do, "dw_out", 1024)
    dzu, dzv, dws, dbs_cols, dg_sgu = _sgu_bwd(proj, dya, g_sgu, ws_tril, bs_full, tm)
    dqm, dkm, dvm = _memattn_bwd(proj, kv, dym, Bl, S, min(2048, S))
    d_w_kv, dg_mem = _memkv_bwd(dkm, dvm, memn, mem, g_mem, W["w_mem_kv"])
    mid = [d_w_kv, d_w_out]
    dq, dk, dv, dc_row = _fox_bwd(proj, dyb, lse, bq, bk, Bl, S)
    if reduce is not None:
        done = reduce.apply(BIG[3:], ffn)
        pending, after = reduce.begin("mid", mid, (dc_row,) + done)
        dc_row = after[0]
    dfl = _gate_bwd(dc_row, fl_row).reshape(T, LANES)
    pieces = (dzu, dzv, dq, dk, dv, dfl, dqm)
    grad_x, dg_pre, dbf = _inproj_bwd(pieces, x2d, dx1, g_pre, W["w_in"], tm)
    if reduce is not None:
        mid, (dfl,) = reduce.finish("mid", pending, (dfl,))
        pieces = (dzu, dzv, dq, dk, dv, dfl, dqm)
    d_w_in = _dw_in(pieces, h, W["w_in"].shape[0], 1024)
    if reduce is None:
        big = dict(zip(BIG, [d_w_in] + mid + ffn))
    else:
        done = reduce.apply(BIG[1:3], mid)
        big = {"w_in": reduce.begin("in", [d_w_in], done)[0]}
    small = {"g_pre_mix": dg_pre, "b_f": dbf[:, :N_FOX_HEADS], "g_sgu": dg_sgu, "w_s": dws, "b_s": dbs_cols[:, :N_FOX_HEADS].T,
             "g_out_a": dga, "g_out_b": dgb, "g_out_m": dgm, "g_mem": dg_mem, "g_post_mix": dg_post,
             "g_pre_ffn": dg_pre2, "g_post_ffn": dg_post2, "loss": loss[:, :1]}
    return grad_x.reshape(Bl, S, D), big, small


def _place():
    return lax.axis_index("x"), lax.axis_index("y"), lax.axis_index("c")


def _exchange_on_sequencer(srcs, own_full, name, collective_id):
    n = len(srcs)

    def body(*refs):
        src, dst = refs[:n], refs[n:2 * n]
        lsem, isend, irecv, dsend, drecv = refs[2 * n:]
        x, y, c = _place()
        oc = 1 - c
        s_me = 2 * x + y
        sib = (x, y, oc)
        chips = [(1 - x, y), (x, 1 - y), (1 - x, 1 - y)]
        barrier = pltpu.get_barrier_semaphore()
        for dev in [(cx, cy, c) for cx, cy in chips] + [sib]:
            pl.semaphore_signal(barrier, inc=1, device_id=dev, device_id_type=MESH)
        pl.semaphore_wait(barrier, 4)

        def remote(a, b, ssem, rsem, dev):
            return pltpu.make_async_remote_copy(src_ref=a, dst_ref=b, send_sem=ssem, recv_sem=rsem,
                                                device_id=dev, device_id_type=MESH)

        sends, local = [], []
        for w in range(n):
            for j, (cx, cy) in enumerate(chips):
                half = src[w].at[c] if own_full else src[w].at[2 * cx + cy]
                cp = remote(half, dst[w].at[s_me, c], isend.at[w, j], irecv.at[w, j], (cx, cy, c))
                cp.start()
                sends.append(cp)
            if own_full:
                cp = remote(src[w], dst[w].at[s_me], dsend.at[w, 3], drecv.at[w, 3], sib)
            else:
                cp = remote(src[w].at[s_me], dst[w].at[s_me, c], dsend.at[w, 3], drecv.at[w, 3], sib)
                loc = pltpu.make_async_copy(src[w].at[s_me], dst[w].at[s_me, c], lsem.at[w])
                loc.start()
                local.append(loc)
            cp.start()
            sends.append(cp)
        for w in range(n):
            for j, (cx, cy) in enumerate(chips):
                landed = dst[w].at[2 * cx + cy, c]
                remote(landed, landed, isend.at[w, j], irecv.at[w, j], (cx, cy, c)).wait_recv()
                cp = remote(landed, landed, dsend.at[w, j], drecv.at[w, j], sib)
                cp.start()
                sends.append(cp)
        for w in range(n):
            for j, (cx, cy) in enumerate(chips):
                landed = dst[w].at[2 * cx + cy, oc]
                remote(landed, landed, dsend.at[w, j], drecv.at[w, j], sib).wait_recv()
            landed = dst[w].at[s_me] if own_full else dst[w].at[s_me, oc]
            remote(landed, landed, dsend.at[w, 3], drecv.at[w, 3], sib).wait_recv()
        for cp in sends:
            cp.wait_send()
        for loc in local:
            loc.wait()

    return pl.kernel(
        body, out_type=[jax.ShapeDtypeStruct((4, 2) + s.shape[1:], s.dtype) for s in srcs],
        mesh=plsc.ScalarSubcoreMesh(axis_name="sequencer", num_cores=1), name=name,
        scratch_types=[pltpu.SemaphoreType.DMA((n,)), pltpu.SemaphoreType.DMA((n, 3)), pltpu.SemaphoreType.DMA((n, 3)),
                       pltpu.SemaphoreType.DMA((n, 4)), pltpu.SemaphoreType.DMA((n, 4))],
        compiler_params=pltpu.CompilerParams(collective_id=collective_id),
    )(*srcs)


def _sibling_swap(grads, name, collective_id):
    n = len(grads)

    def body(*refs):
        g, theirs = refs[:n], refs[n:2 * n]
        ssem, rsem = refs[2 * n:]
        x, y, c = _place()
        sib = (x, y, 1 - c)
        barrier = pltpu.get_barrier_semaphore()
        pl.semaphore_signal(barrier, inc=1, device_id=sib, device_id_type=MESH)
        pl.semaphore_wait(barrier, 1)
        cps = []
        for w in range(n):
            cp = pltpu.make_async_remote_copy(src_ref=g[w].at[:, 1 - c], dst_ref=theirs[w], send_sem=ssem.at[w],
                                              recv_sem=rsem.at[w], device_id=sib, device_id_type=MESH)
            cp.start()
            cps.append(cp)
        for cp in cps:
            cp.wait()

    return pl.kernel(
        body, out_type=[jax.ShapeDtypeStruct((4,) + g.shape[2:], g.dtype) for g in grads],
        mesh=plsc.ScalarSubcoreMesh(axis_name="sequencer", num_cores=1), name=name,
        scratch_types=[pltpu.SemaphoreType.DMA((n,)), pltpu.SemaphoreType.DMA((n,))],
        compiler_params=pltpu.CompilerParams(collective_id=collective_id),
    )(*grads)


def _add_pair(core, g, theirs, name):
    _, _, hr, C = g.shape

    def body(core_ref, g_ref, t_ref, o_ref):
        o_ref[0] = (g_ref[0, 0].astype(F32) + t_ref[0].astype(F32)).astype(BF16)

    blk = BS((1, hr, C), lambda s, core_ref: (s, 0, 0))
    return pl.pallas_call(
        body, name=name,
        grid_spec=pltpu.PrefetchScalarGridSpec(
            num_scalar_prefetch=1, grid=(4,),
            in_specs=[BS((1, 1, hr, C), lambda s, core_ref: (s, core_ref[0], 0, 0)), blk], out_specs=blk),
        out_shape=jax.ShapeDtypeStruct(theirs.shape, BF16), compiler_params=_cp(("arbitrary",)))(core, g, theirs)


def _sum_chips(r, name):
    _, _, hr, C = r.shape

    def body(r_ref, o_ref):
        o_ref[...] = ((r_ref[0, 0].astype(F32) + r_ref[1, 0].astype(F32)) + r_ref[2, 0].astype(F32)) + r_ref[3, 0].astype(F32)

    return pl.pallas_call(body, name=name, grid=(2,), in_specs=[BS((4, 1, hr, C), lambda h: (0, h, 0, 0))],
                          out_specs=BS((hr, C), lambda h: (h, 0)), out_shape=jax.ShapeDtypeStruct((2 * hr, C), F32),
                          compiler_params=_cp(("arbitrary",)))(r)


class _Reducer:
    IDS = {"ffn": (4, 5), "mid": (6, 7), "in": (8, 9)}

    def __init__(self, core, apply):
        self.core = core
        self.apply = apply

    def begin(self, tag, grads, after=()):
        grads, after = lax.optimization_barrier((list(grads), after))
        g4 = [g.reshape(4, 2, -1, g.shape[-1]) for g in grads]
        return (g4, _sibling_swap(g4, "swap_" + tag, self.IDS[tag][0])), after

    def finish(self, tag, pending, hold):
        g4, theirs = pending
        sums = [_add_pair(self.core, g, t, "chip_sum_%s_%d" % (tag, k)) for k, (g, t) in enumerate(zip(g4, theirs))]
        sums, hold = lax.optimization_barrier((sums, hold))
        return _exchange_on_sequencer(sums, False, "scatter_" + tag, self.IDS[tag][1]), hold


def _small_allreduce(part):
    R = part.shape[0]
    rs = R // 8
    masks = [(mx, my, mc) for mx in (0, 1) for my in (0, 1) for mc in (0, 1)][1:]

    def body(p_ref, o_ref, buf_ref, s1, r1, s2, r2):
        x, y, c = _place()
        d = 4 * x + 2 * y + c
        mine = pl.ds(pl.multiple_of(d * rs, 8), rs)
        peers = [((x + mx) % 2, (y + my) % 2, (c + mc) % 2) for mx, my, mc in masks]
        first, second = [], []
        for k, (px, py, pc) in enumerate(peers):
            theirs = pl.ds(pl.multiple_of((4 * px + 2 * py + pc) * rs, 8), rs)
            cp = pltpu.make_async_remote_copy(src_ref=p_ref.at[theirs, :], dst_ref=buf_ref.at[d], send_sem=s1.at[k],
                                              recv_sem=r1.at[k], device_id=(px, py, pc), device_id_type=MESH)
            cp.start()
            first.append(cp)
        buf_ref[d] = p_ref[mine, :]
        for k, (px, py, pc) in enumerate(peers):
            slot = buf_ref.at[4 * px + 2 * py + pc]
            pltpu.make_async_remote_copy(src_ref=slot, dst_ref=slot, send_sem=s1.at[k], recv_sem=r1.at[k],
                                         device_id=(px, py, pc), device_id_type=MESH).wait_recv()
        total = buf_ref[0]
        for k in range(1, 8):
            total = total + buf_ref[k]
        o_ref[mine, :] = total
        for k, (px, py, pc) in enumerate(peers):
            cp = pltpu.make_async_remote_copy(src_ref=o_ref.at[mine, :], dst_ref=o_ref.at[mine, :], send_sem=s2.at[k],
                                              recv_sem=r2.at[k], device_id=(px, py, pc), device_id_type=MESH)
            cp.start()
            second.append(cp)
        for k, (px, py, pc) in enumerate(peers):
            rows = o_ref.at[pl.ds(pl.multiple_of((4 * px + 2 * py + pc) * rs, 8), rs), :]
            pltpu.make_async_remote_copy(src_ref=rows, dst_ref=rows, send_sem=s2.at[k], recv_sem=r2.at[k],
                                         device_id=(px, py, pc), device_id_type=MESH).wait_recv()
        for cp in first + second:
            cp.wait_send()

    vm = pl.BlockSpec(memory_space=pltpu.VMEM)
    return pl.pallas_call(
        body, name="small_allreduce", in_specs=[vm], out_specs=vm, out_shape=jax.ShapeDtypeStruct(part.shape, F32),
        scratch_shapes=[pltpu.VMEM((8, rs, LANES), F32)] + [pltpu.SemaphoreType.DMA((7,))] * 4,
    )(part)


def _adamw(w, g, m, v, name):
    R, C = w.shape
    summed = g.ndim == 4
    if summed:
        tr = R // 2
    else:
        tr = R if R * C * 4 <= (1 << 21) else R // 2
        if tr % 8:
            tr = R
    c1 = 1.0 / (1.0 - ADAM_B1 ** ADAM_STEP)
    c2 = 1.0 / (1.0 - ADAM_B2 ** ADAM_STEP)

    def body(w_ref, g_ref, m_ref, v_ref, *outs):
        if summed:
            g_ = ((g_ref[0, 0].astype(F32) + g_ref[1, 0].astype(F32)) + g_ref[2, 0].astype(F32)) + g_ref[3, 0].astype(F32)
            outs[0][...] = g_
        else:
            g_ = g_ref[...]
        d_ref, mo_ref, vo_ref = outs[-3:]
        m_ = ADAM_B1 * m_ref[...] + (1.0 - ADAM_B1) * g_
        v_ = ADAM_B2 * v_ref[...] + (1.0 - ADAM_B2) * (g_ * g_)
        mo_ref[...] = m_
        vo_ref[...] = v_
        d_ref[...] = -ADAM_LR * ((m_ * c1) / (jnp.sqrt(v_ * c2) + ADAM_EPS) + ADAM_WD * w_ref[...])

    blk = BS((tr, C), lambda i: (i, 0))
    g_blk = BS((4, 1, tr, C), lambda i: (0, i, 0, 0)) if summed else blk
    nout = 4 if summed else 3
    return pl.pallas_call(body, name=name, grid=(R // tr,), in_specs=[blk, g_blk, blk, blk], out_specs=[blk] * nout,
                          out_shape=[jax.ShapeDtypeStruct((R, C), F32)] * nout,
                          compiler_params=_cp(("arbitrary",)))(w, g, m, v)


def _adamw_unit_rows(w, g, m, v, name):
    C, _, R = w.shape
    tc = C // 2 if C % 2 == 0 else C
    c1 = 1.0 / (1.0 - ADAM_B1 ** ADAM_STEP)
    c2 = 1.0 / (1.0 - ADAM_B2 ** ADAM_STEP)

    def body(w_ref, g_ref, m_ref, v_ref, go_ref, d_ref, mo_ref, vo_ref):
        g_ = g_ref[...]
        go_ref[...] = g_
        m_ = ADAM_B1 * m_ref[...] + (1.0 - ADAM_B1) * g_
        v_ = ADAM_B2 * v_ref[...] + (1.0 - ADAM_B2) * (g_ * g_)
        mo_ref[...] = m_
        vo_ref[...] = v_
        d_ref[...] = -ADAM_LR * ((m_ * c1) / (jnp.sqrt(v_ * c2) + ADAM_EPS) + ADAM_WD * w_ref[...])

    blk = BS((tc, 1, R), lambda i: (i, 0, 0))
    return pl.pallas_call(body, name=name, grid=(C // tc,), in_specs=[blk] * 4, out_specs=[blk] * 4,
                          out_shape=[jax.ShapeDtypeStruct((C, 1, R), F32)] * 4,
                          compiler_params=_cp(("arbitrary",)))(w, g, m, v)


SMALL = ("g_pre_mix", "b_f", "g_sgu", "w_s", "b_s", "g_out_a", "g_out_b", "g_out_m", "g_mem", "g_post_mix",
         "g_pre_ffn", "g_post_ffn")
BIG = ("w_in", "w_mem_kv", "w_out", "w_gate", "w_up", "w_down")
TRANSPOSED = ("w_in", "w_gate", "w_up")
WEIGHTS = ("g_pre_mix", "w_in", "b_f", "g_sgu", "w_s", "b_s", "g_out_a", "g_out_b", "g_out_m", "g_mem", "w_mem_kv",
           "w_out", "g_post_mix", "g_pre_ffn", "w_gate", "w_up", "w_down", "g_post_ffn")


VECTORS = ("g_pre_mix", "b_f", "g_sgu", "g_out_a", "g_out_b", "g_out_m", "g_mem", "g_post_mix", "g_pre_ffn", "g_post_ffn")
VEC_ROWS = 16
WS_ROWS = N_FOX_HEADS * CHUNK
BS_ROWS = 8


def _pack_small(small, vw):
    stack = jnp.zeros((VEC_ROWS, vw), F32)
    for k, n in enumerate(VECTORS + ("loss",)):
        row = small[n].reshape(1, -1)
        stack = stack + jnp.pad(row, ((k, VEC_ROWS - 1 - k), (0, vw - row.shape[1])))
    parts = [small["w_s"].reshape(WS_ROWS, LANES), jnp.pad(small["b_s"], ((0, BS_ROWS - N_FOX_HEADS), (0, 0))),
             stack.reshape(-1, LANES)]
    rows = sum(p.shape[0] for p in parts)
    return jnp.concatenate(parts + [jnp.zeros((-rows % 64, LANES), F32)], axis=0)


def _adamw_small(vec_g, vec_wmv, ws, bs):
    c1 = 1.0 / (1.0 - ADAM_B1 ** ADAM_STEP)
    c2 = 1.0 / (1.0 - ADAM_B2 ** ADAM_STEP)
    nv = len(vec_wmv)

    def adam(g, w, m, v):
        m_ = ADAM_B1 * m + (1.0 - ADAM_B1) * g
        v_ = ADAM_B2 * v + (1.0 - ADAM_B2) * (g * g)
        return -ADAM_LR * ((m_ * c1) / (jnp.sqrt(v_ * c2) + ADAM_EPS) + ADAM_WD * w), m_, v_

    def body(*refs):
        vg_ref, ins, outs = refs[0], refs[1:1 + 3 * nv + 8], refs[1 + 3 * nv + 8:]
        for k in range(nv):
            w_ref, m_ref, v_ref = ins[3 * k:3 * k + 3]
            g = vg_ref[k:k + 1, 0:w_ref.shape[1]]
            d, m_, v_ = adam(g, w_ref[...], m_ref[...], v_ref[...])
            for o_ref, val in zip(outs[4 * k:4 * k + 4], (g, d, m_, v_)):
                o_ref[...] = val
        for j in range(2):
            g_ref, w_ref, m_ref, v_ref = ins[3 * nv + 4 * j:3 * nv + 4 * j + 4]
            for o_ref, val in zip(outs[4 * nv + 3 * j:4 * nv + 3 * j + 3], adam(g_ref[...], w_ref[...], m_ref[...], v_ref[...])):
                o_ref[...] = val

    vm = pl.BlockSpec(memory_space=pltpu.VMEM)
    operands = [vec_g] + [a for wmv in vec_wmv for a in wmv] + list(ws) + list(bs)
    out_shape = ([jax.ShapeDtypeStruct(wmv[0].shape, F32) for wmv in vec_wmv for _ in range(4)]
                 + [jax.ShapeDtypeStruct(ws[1].shape, F32)] * 3 + [jax.ShapeDtypeStruct(bs[1].shape, F32)] * 3)
    outs = pl.pallas_call(body, name="adamw_small", in_specs=[vm] * len(operands), out_specs=[vm] * len(out_shape),
                          out_shape=out_shape)(*operands)
    return [outs[4 * k:4 * k + 4] for k in range(nv)], outs[4 * nv:4 * nv + 3], outs[4 * nv + 3:]


def kernel(x, mem, g_pre_mix, w_in, b_f, g_sgu, w_s, b_s, g_out_a, g_out_b, g_out_m, g_mem, w_mem_kv, w_out, g_post_mix, g_pre_ffn, w_gate, w_up, w_down, g_post_ffn, loss_target, m_g_pre_mix, m_w_in, m_b_f, m_g_sgu, m_w_s, m_b_s, m_g_out_a, m_g_out_b, m_g_out_m, m_g_mem, m_w_mem_kv, m_w_out, m_g_post_mix, m_g_pre_ffn, m_w_gate, m_w_up, m_w_down, m_g_post_ffn, v_g_pre_mix, v_w_in, v_b_f, v_g_sgu, v_w_s, v_b_s, v_g_out_a, v_g_out_b, v_g_out_m, v_g_mem, v_w_mem_kv, v_w_out, v_g_post_mix, v_g_pre_ffn, v_w_gate, v_w_up, v_w_down, v_g_post_ffn):
    Wt = dict(g_pre_mix=g_pre_mix, w_in=w_in, b_f=b_f, g_sgu=g_sgu, w_s=w_s, b_s=b_s, g_out_a=g_out_a, g_out_b=g_out_b,
              g_out_m=g_out_m, g_mem=g_mem, w_mem_kv=w_mem_kv, w_out=w_out, g_post_mix=g_post_mix, g_pre_ffn=g_pre_ffn,
              w_gate=w_gate, w_up=w_up, w_down=w_down, g_post_ffn=g_post_ffn)
    Mo = dict(g_pre_mix=m_g_pre_mix, w_in=m_w_in, b_f=m_b_f, g_sgu=m_g_sgu, w_s=m_w_s, b_s=m_b_s, g_out_a=m_g_out_a,
              g_out_b=m_g_out_b, g_out_m=m_g_out_m, g_mem=m_g_mem, w_mem_kv=m_w_mem_kv, w_out=m_w_out,
              g_post_mix=m_g_post_mix, g_pre_ffn=m_g_pre_ffn, w_gate=m_w_gate, w_up=m_w_up, w_down=m_w_down,
              g_post_ffn=m_g_post_ffn)
    Vo = dict(g_pre_mix=v_g_pre_mix, w_in=v_w_in, b_f=v_b_f, g_sgu=v_g_sgu, w_s=v_w_s, b_s=v_b_s, g_out_a=v_g_out_a,
              g_out_b=v_g_out_b, g_out_m=v_g_out_m, g_mem=v_g_mem, w_mem_kv=v_w_mem_kv, w_out=v_w_out,
              g_post_mix=v_g_post_mix, g_pre_ffn=v_g_pre_ffn, w_gate=v_w_gate, w_up=v_w_up, w_down=v_w_down,
              g_post_ffn=v_g_post_ffn)

    gap = P_COLS - IN_COLS

    def to_kernel(n, w):
        if n in TRANSPOSED:
            w = w.T
        if n == "w_in":
            w = jnp.pad(w[:F_END], ((0, P_COLS - F_END), (0, 0))) + jnp.pad(w[F_END:], ((F_END + gap, 0), (0, 0)))
        return w

    def ungroup(g):
        return jnp.pad(g[:F_END], ((0, IN_COLS - F_END), (0, 0))) + jnp.pad(g[F_END + gap:], ((F_END, 0), (0, 0)))

    shards = {n: to_kernel(n, Wt[n][0]) for n in BIG}
    srcs = [shards[n].astype(BF16).reshape(2, shards[n].shape[0] // 2, shards[n].shape[1]) for n in BIG]
    fulls = (_exchange_on_sequencer(srcs[:1], True, "gather_w_in", 1)
             + _exchange_on_sequencer(srcs[1:3], True, "gather_kv_out", 2)
             + _exchange_on_sequencer(srcs[3:], True, "gather_ffn", 3))
    W = {}
    for n, f in zip(BIG, fulls):
        _, _, hr, C = f.shape
        W[n] = f.reshape(8 * hr, C) if n in ("w_mem_kv", "w_out") else f.reshape(4, 2 * hr, C)

    P = {n: Wt[n] for n in SMALL}
    grads, deltas, new_m, new_v = {}, {}, {}, {}

    def apply(names, landed):
        for n, r in zip(names, landed):
            if n == "w_in":
                g_t = ungroup(_sum_chips(r, "sum_chips_" + n))
                lift = lambda a: jnp.transpose(a, (2, 0, 1))
                outs = _adamw_unit_rows(lift(Wt[n]), g_t[:, None, :], lift(Mo[n]), lift(Vo[n]), "adamw_" + n)
                g, d, m1, v1 = [jnp.transpose(a, (1, 2, 0))[0] for a in outs]
            elif n in TRANSPOSED:
                g, d, m1, v1 = [a.T for a in _adamw(Wt[n][0].T, r, Mo[n][0].T, Vo[n][0].T, "adamw_" + n)]
            else:
                g, d, m1, v1 = _adamw(Wt[n][0], r, Mo[n][0], Vo[n][0], "adamw_" + n)
            grads[n], deltas[n], new_m[n], new_v[n] = g[None], d[None], m1[None], v1[None]
        return tuple(deltas[n] for n in names)

    core = lax.axis_index("c").astype(jnp.int32).reshape(1)
    reducer = _Reducer(core, apply)
    grad_x, pending, small = _local_step(x, mem, loss_target, W, P, reducer)

    vw = -(-max(x.shape[-1], A_W) // LANES) * LANES
    landed, (packed,) = reducer.finish("in", pending["w_in"], (_pack_small(small, vw),))
    total = _small_allreduce(packed)
    apply(BIG[:1], landed)

    lane_row = lambda a: jnp.pad(a, ((0, 0), (0, -a.shape[1] % LANES)))
    vec_g = total[WS_ROWS + BS_ROWS:WS_ROWS + BS_ROWS + VEC_ROWS * vw // LANES].reshape(VEC_ROWS, vw)
    ws_g = total[:WS_ROWS]
    bs_g = total[WS_ROWS:WS_ROWS + N_FOX_HEADS]
    rows = lambda a, r: a.reshape(r, LANES)
    per_vec, ws_out, bs_out = _adamw_small(
        vec_g, [tuple(lane_row(a[n]) for a in (Wt, Mo, Vo)) for n in VECTORS],
        (ws_g,) + tuple(rows(a["w_s"], WS_ROWS) for a in (Wt, Mo, Vo)),
        (bs_g,) + tuple(rows(a["b_s"], N_FOX_HEADS) for a in (Wt, Mo, Vo)))
    for n, outs in zip(VECTORS, per_vec):
        grads[n], deltas[n], new_m[n], new_v[n] = [o[:, :Wt[n].shape[1]] for o in outs]
    for n, g, outs in (("w_s", ws_g, ws_out), ("b_s", bs_g, bs_out)):
        grads[n], deltas[n], new_m[n], new_v[n] = [o.reshape(Wt[n].shape) for o in (g,) + tuple(outs)]
    loss = vec_g[len(VECTORS), 0]

    return (loss, grad_x, *[grads[n] for n in WEIGHTS], *[deltas[n] for n in WEIGHTS],
            *[new_m[n] for n in WEIGHTS], *[new_v[n] for n in WEIGHTS])
```

```python
import functools

import jax
import jax.numpy as jnp
from jax import lax
from jax.experimental import pallas as pl
from jax.experimental.pallas import tpu as pltpu
from jax.experimental.pallas import tpu_sc as plsc

F32 = jnp.float32
BF16 = jnp.bfloat16
EPS = 1e-6
NEG = -1e30
HEAD = 64
A_W, B_W, M_W = 384, 384, 256
N_FOX_HEADS = 6
CHUNK = 128
IN_COLS = 2 * A_W + 3 * B_W + N_FOX_HEADS + M_W
P_MAIN = 2 * A_W + 3 * B_W + M_W
P_COLS = P_MAIN + 128
F_END = 2 * A_W + 3 * B_W + N_FOX_HEADS
LANES = 128
Q_BLK, K_BLK = 512, 128
ROW_SPLIT = 4
ADAM_LR, ADAM_B1, ADAM_B2, ADAM_EPS, ADAM_WD, ADAM_STEP = 0.001, 0.9, 0.999, 1e-08, 0.01, 10
VMEM_LIMIT = 56 * 1024 * 1024
MESH = pl.DeviceIdType.MESH
BS = pl.BlockSpec


def _cp(sem=None):
    return pltpu.CompilerParams(dimension_semantics=sem, vmem_limit_bytes=VMEM_LIMIT)


def _iota(shape, dim):
    return lax.broadcasted_iota(jnp.int32, shape, dim)


def _dot(a, b):
    return jnp.dot(a.astype(BF16), b.astype(BF16), preferred_element_type=F32)


def _dot_nt(a, b):
    return lax.dot_general(a.astype(BF16), b.astype(BF16), (((1,), (1,)), ((), ())), preferred_element_type=F32)


def _dot_tn(a, b):
    return lax.dot_general(a.astype(BF16), b.astype(BF16), (((0,), (0,)), ((), ())), preferred_element_type=F32)


def _rms(x, g):
    return x * lax.rsqrt(jnp.mean(x * x, axis=-1, keepdims=True) + EPS) * g


def _rms_bwd(x, g, dy):
    r = lax.rsqrt(jnp.mean(x * x, axis=-1, keepdims=True) + EPS)
    xr = x * r
    gd = dy * g
    m = jnp.mean(gd * xr, axis=-1, keepdims=True)
    return (gd - xr * m) * r, _colsum(dy * xr)


def _gelu(x):
    return 0.5 * x * (1.0 + jnp.tanh(0.7978845608028654 * (x + 0.044715 * (x * x * x))))


def _sigmoid(x):
    return 1.0 / (1.0 + jnp.exp(-x))


def _silu_mul(g, u):
    return g * _sigmoid(g) * u


def _logsig(x):
    return jnp.minimum(x, 0.0) - jnp.log(1.0 + jnp.exp(-jnp.abs(x)))


def _colsum(x):
    return jnp.sum(x, axis=0, keepdims=True)


def _acc(ref, val, first):
    @pl.when(first)
    def _():
        ref[...] = val

    @pl.when(jnp.logical_not(first))
    def _():
        ref[...] += val


def _inproj_fwd(x2d, g_pre, w_in_p, tm):
    T, D = x2d.shape
    CH = 768
    nchunk = P_COLS // CH
    ns, _, dsh = w_in_p.shape

    def body(x_ref, g_ref, w_ref, h_ref, proj_ref, fl_ref):
        h = _rms(x_ref[...], g_ref[...]).astype(BF16)
        h_ref[...] = h
        for n in range(nchunk):
            rows = slice(n * CH, (n + 1) * CH)
            r = _dot_nt(h[:, 0:dsh], w_ref[0, rows, :])
            for s in range(1, ns):
                r = r + _dot_nt(h[:, s * dsh:(s + 1) * dsh], w_ref[s, rows, :])
            if n < nchunk - 1:
                proj_ref[:, rows] = r.astype(BF16)
            else:
                fg = 1920 - n * CH
                proj_ref[:, n * CH:1920] = r[:, :fg].astype(BF16)
                fl_ref[...] = r[:, fg:fg + LANES]
                proj_ref[:, 1920:P_MAIN] = r[:, fg + LANES:].astype(BF16)

    return pl.pallas_call(
        body, name="inproj_fwd", grid=(T // tm,),
        in_specs=[BS((tm, D), lambda i: (i, 0)), BS((1, D), lambda i: (0, 0)),
                  BS((ns, P_COLS, dsh), lambda i: (0, 0, 0))],
        out_specs=[BS((tm, D), lambda i: (i, 0)), BS((tm, P_MAIN), lambda i: (i, 0)), BS((tm, LANES), lambda i: (i, 0))],
        out_shape=[jax.ShapeDtypeStruct((T, D), BF16), jax.ShapeDtypeStruct((T, P_MAIN), BF16),
                   jax.ShapeDtypeStruct((T, LANES), F32)],
        compiler_params=_cp(("arbitrary",)),
    )(x2d, g_pre, w_in_p)


def _gate_fwd(flog3, bf_row):
    Bl, S, _ = flog3.shape
    nb = S // LANES

    def body(f_ref, b_ref, bq_ref, bk_ref, fr_ref):
        row = _iota((LANES, LANES), 0)
        lane = _iota((LANES, LANES), 1)
        one = jnp.ones((LANES, LANES), BF16)
        zero = jnp.zeros((LANES, LANES), BF16)

        carry = jnp.zeros((1, LANES), F32)
        for j in range(nb):
            r0 = j * LANES
            fl = f_ref[0, pl.ds(r0, LANES), :] + b_ref[...]
            fr_ref[0, j] = fl.T[0:8, :]
            c = _logsig(fl)
            for k in (1, 2, 4, 8, 16, 32, 64):
                c = c + jnp.where(row >= k, pltpu.roll(c, k, 0), 0.0)
            total = _colsum(jnp.where(row == LANES - 1, c, 0.0))
            c = c + carry
            carry = carry + total
            for h in range(N_FOX_HEADS):
                col = jnp.sum(jnp.where(lane == h, c, 0.0), axis=1, keepdims=True)
                hi = col.astype(BF16)
                rest = col - hi.astype(F32)
                mid = rest.astype(BF16)
                lo = (rest - mid.astype(F32)).astype(BF16)
                base = _bias_lane(h)
                bq = jnp.where(lane == base, hi, jnp.where(lane == base + 1, mid, jnp.where(lane == base + 2, lo, zero)))
                bq = jnp.where((lane >= base + 3) & (lane < base + 6), one, bq)
                bk = jnp.where(lane == base + 3, -hi, jnp.where(lane == base + 4, -mid, jnp.where(lane == base + 5, -lo, zero)))
                bk = jnp.where((lane >= base) & (lane < base + 3), one, bk)
                bq_ref[0, h, pl.ds(r0, LANES), :] = bq
                bk_ref[0, h, pl.ds(r0, LANES), :] = bk

    slab = BS((1, N_FOX_HEADS, S, LANES), lambda b: (b, 0, 0, 0))
    return pl.pallas_call(
        body, name="gate_fwd", grid=(Bl,),
        in_specs=[BS((1, S, LANES), lambda b: (b, 0, 0)), BS((1, LANES), lambda b: (0, 0))],
        out_specs=[slab, slab, BS((1, nb, 8, LANES), lambda b: (b, 0, 0, 0))],
        out_shape=[jax.ShapeDtypeStruct((Bl, N_FOX_HEADS, S, LANES), BF16),
                   jax.ShapeDtypeStruct((Bl, N_FOX_HEADS, S, LANES), BF16),
                   jax.ShapeDtypeStruct((Bl, nb, 8, LANES), F32)],
        compiler_params=_cp(("arbitrary",)),
    )(flog3, bf_row)


def _bias_lane(h):
    return HEAD if h % 2 == 0 else 0


def _sgu_pre(zu, zv, g_sgu):
    return _gelu(zu), _rms(_gelu(zv), g_sgu)


def _sgu_fwd(proj, g_sgu, ws_tril, bs_full, tm):
    T = proj.shape[0]
    nch = tm // CHUNK

    def body(zu_ref, zv_ref, g_ref, ws_ref, b_ref, ya_ref):
        lane = _iota((CHUNK, LANES), 1)
        u, vn = _sgu_pre(zu_ref[...].astype(F32), zv_ref[...].astype(F32), g_ref[...])
        vn = vn.astype(BF16)
        for c in range(nch):
            rs = slice(c * CHUNK, (c + 1) * CHUNK)
            for j in range(3):
                cs = slice(j * LANES, (j + 1) * LANES)
                vp = vn[rs, cs]
                z = jnp.where(lane < HEAD, _dot(ws_ref[2 * j], vp), _dot(ws_ref[2 * j + 1], vp)) + b_ref[:, cs]
                ya_ref[rs, cs] = (u[rs, cs] * z).astype(BF16)

    return pl.pallas_call(
        body, name="sgu_fwd", grid=(T // tm,),
        in_specs=[BS((tm, A_W), lambda i: (i, 0)), BS((tm, A_W), lambda i: (i, 1)), BS((1, A_W), lambda i: (0, 0)),
                  BS((6, CHUNK, CHUNK), lambda i: (0, 0, 0)), BS((CHUNK, A_W), lambda i: (0, 0))],
        out_specs=BS((tm, A_W), lambda i: (i, 0)),
        out_shape=jax.ShapeDtypeStruct((T, A_W), BF16),
        compiler_params=_cp(("arbitrary",)),
    )(proj, proj, g_sgu, ws_tril, bs_full)


def _fox_fwd(proj, bq, bk, Bl, S):
    T = Bl * S
    nq = S // Q_BLK
    qc, kc, vc = 768 // LANES, 1152 // LANES, 1536 // LANES

    def body(q_ref, k_ref, v_ref, bq_ref, bk_ref, o_ref, lse_ref, ka_ref, va_ref):
        lane_s = _iota((S, LANES), 1)
        lane = _iota((Q_BLK, LANES), 1)
        tri = _iota((Q_BLK, Q_BLK), 1) <= _iota((Q_BLK, Q_BLK), 0)
        k = k_ref[...]
        v = v_ref[...]
        for hh in range(2):
            data = (lane_s < HEAD) if hh == 0 else (lane_s >= HEAD)
            ka_ref[hh] = jnp.where(data, k, bk_ref[0, hh])
            va_ref[hh] = jnp.where(lane_s == _bias_lane(hh), jnp.ones_like(v), v)
        for i in range(nq):
            r0 = i * Q_BLK
            q = q_ref[r0:r0 + Q_BLK, :]
            o_out = jnp.zeros((Q_BLK, LANES), F32)
            lse_out = jnp.zeros((Q_BLK, LANES), F32)
            for hh in range(2):
                hmask = (lane < HEAD) if hh == 0 else (lane >= HEAD)
                qa = jnp.where(hmask, q * 0.125, bq_ref[0, hh, r0:r0 + Q_BLK, :])
                sd = jnp.where(tri, _dot_nt(qa, ka_ref[hh, r0:r0 + Q_BLK, :]), NEG)
                m = jnp.max(sd, axis=1, keepdims=True)
                if i:
                    sf = _dot_nt(qa, ka_ref[hh, 0:r0, :])
                    m = jnp.maximum(m, jnp.max(sf, axis=1, keepdims=True))
                acc = _dot(jnp.exp(sd - m), va_ref[hh, r0:r0 + Q_BLK, :])
                if i:
                    acc = acc + _dot(jnp.exp(sf - m), va_ref[hh, 0:r0, :])
                l = jnp.sum(jnp.where(lane == _bias_lane(hh), acc, 0.0), axis=1, keepdims=True)
                o_out = jnp.where(hmask, acc / l, o_out)
                lse_out = jnp.where(hmask, m + jnp.log(l), lse_out)
            o_ref[r0:r0 + Q_BLK, :] = o_out.astype(BF16)
            lse_ref[0, r0:r0 + Q_BLK, :] = lse_out

    seq = lambda c0: BS((S, LANES), lambda b, p: (b, c0 + p))
    pair = BS((1, 2, S, LANES), lambda b, p: (b, p, 0, 0))
    return pl.pallas_call(
        body, name="fox_fwd", grid=(Bl, 3),
        in_specs=[seq(qc), seq(kc), seq(vc), pair, pair],
        out_specs=[seq(0), BS((1, S, LANES), lambda b, p: (p, b, 0))],
        out_shape=[jax.ShapeDtypeStruct((T, B_W), BF16), jax.ShapeDtypeStruct((3, T, LANES), F32)],
        scratch_shapes=[pltpu.VMEM((2, S, LANES), BF16), pltpu.VMEM((2, S, LANES), BF16)],
        compiler_params=_cp(("arbitrary", "arbitrary")),
    )(proj, proj, proj, bq, bk)


def _memkv_fwd(mem, g_mem, w_kv):
    Bl, Mt, D = mem.shape

    def body(m_ref, g_ref, w_ref, mn_ref, kv_ref):
        mn = _rms(m_ref[0], g_ref[...]).astype(BF16)
        mn_ref[0] = mn
        kv_ref[0] = jnp.dot(mn, w_ref[...], preferred_element_type=F32).astype(BF16)

    return pl.pallas_call(
        body, name="memkv_fwd", grid=(Bl,),
        in_specs=[BS((1, Mt, D), lambda b: (b, 0, 0)), BS((1, D), lambda b: (0, 0)), BS((D, 2 * M_W), lambda b: (0, 0))],
        out_specs=[BS((1, Mt, D), lambda b: (b, 0, 0)), BS((1, Mt, 2 * M_W), lambda b: (b, 0, 0))],
        out_shape=[jax.ShapeDtypeStruct((Bl, Mt, D), BF16), jax.ShapeDtypeStruct((Bl, Mt, 2 * M_W), BF16)],
        compiler_params=_cp(("arbitrary",)),
    )(mem, g_mem, w_kv)


def _memattn_fwd(proj, kv, Bl, S, tq):
    T = Bl * S
    nq = S // tq
    Mt = kv.shape[1]
    qc = 1920 // LANES

    def body(q_ref, km_ref, vm_ref, o_ref):
        lane = _iota((tq, LANES), 1)
        q = q_ref[...]
        out = jnp.zeros((tq, LANES), F32)
        for hh in range(2):
            hmask = (lane < HEAD) if hh == 0 else (lane >= HEAD)
            qs = jnp.where(hmask, q, jnp.zeros_like(q)) * 0.125
            s = _dot_nt(qs, km_ref[0])
            pe = jnp.exp(s - jnp.max(s, axis=1, keepdims=True))
            pn = pe / jnp.sum(pe, axis=1, keepdims=True)
            out = jnp.where(hmask, _dot(pn, vm_ref[0]), out)
        o_ref[...] = out.astype(BF16)

    return pl.pallas_call(
        body, name="memattn_fwd", grid=(Bl, 2, nq),
        in_specs=[BS((tq, LANES), lambda b, p, i: (b * nq + i, qc + p)),
                  BS((1, Mt, LANES), lambda b, p, i: (b, 0, p)),
                  BS((1, Mt, LANES), lambda b, p, i: (b, 0, 2 + p))],
        out_specs=BS((tq, LANES), lambda b, p, i: (b * nq + i, p)),
        out_shape=jax.ShapeDtypeStruct((T, M_W), BF16),
        compiler_params=_cp(("arbitrary", "arbitrary", "arbitrary")),
    )(proj, kv, kv)


def _mix_norms(ya, yb, ym, ga, gb, gm):
    return _rms(ya, ga), _rms(yb, gb), _rms(ym, gm)


def _outproj_fwd(ya, yb, ym, x2d, ga, gb, gm, g_post, g_pre2, w_out, tm):
    T, D = x2d.shape

    def body(ya_ref, yb_ref, ym_ref, x_ref, ga_ref, gb_ref, gm_ref, gp_ref, g2_ref, w_ref,
             y_ref, o_ref, x1_ref, h2_ref):
        na, nb_, nm = _mix_norms(ya_ref[...].astype(F32), yb_ref[...].astype(F32), ym_ref[...].astype(F32),
                                 ga_ref[...], gb_ref[...], gm_ref[...])
        y_ref[:, 0:A_W] = na.astype(BF16)
        y_ref[:, A_W:A_W + B_W] = nb_.astype(BF16)
        y_ref[:, A_W + B_W:] = nm.astype(BF16)
        o = jnp.dot(y_ref[...], w_ref[...], preferred_element_type=F32).astype(BF16)
        o_ref[...] = o
        x1 = x_ref[...] + _rms(o.astype(F32), gp_ref[...])
        x1_ref[...] = x1
        h2_ref[...] = _rms(x1, g2_ref[...]).astype(BF16)

    row = lambda w: BS((tm, w), lambda i: (i, 0))
    vec = lambda w: BS((1, w), lambda i: (0, 0))
    return pl.pallas_call(
        body, name="outproj_fwd", grid=(T // tm,),
        in_specs=[row(A_W), row(B_W), row(M_W), row(D), vec(A_W), vec(B_W), vec(M_W), vec(D), vec(D),
                  BS((A_W + B_W + M_W, D), lambda i: (0, 0))],
        out_specs=[row(A_W + B_W + M_W), row(D), row(D), row(D)],
        out_shape=[jax.ShapeDtypeStruct((T, A_W + B_W + M_W), BF16), jax.ShapeDtypeStruct((T, D), BF16),
                   jax.ShapeDtypeStruct((T, D), F32), jax.ShapeDtypeStruct((T, D), BF16)],
        compiler_params=_cp(("arbitrary",)),
    )(ya, yb, ym, x2d, ga, gb, gm, g_post, g_pre2, w_out)


def _ffn_fwd(h2, x1, target, wg, wu, wd, g_post, tm):
    T, D = x1.shape
    ns, F, _ = wg.shape

    def body(h_ref, x1_ref, t_ref, wg_ref, wu_ref, wd_ref, gp_ref,
             gs_ref, us_ref, dff_ref, dx2_ref, dgp_ref, loss_ref, acc_ref):
        j = pl.program_id(0)
        i = pl.program_id(1)
        rows = pl.ds(pl.multiple_of(i * tm, tm), tm)
        h = h_ref[...]
        g = _dot_nt(h, wg_ref[0])
        u = _dot_nt(h, wu_ref[0])
        gs_ref[0] = g.astype(BF16)
        us_ref[0] = u.astype(BF16)
        part = _dot(_silu_mul(g, u), wd_ref[0])

        @pl.when(j == 0)
        def _():
            acc_ref[rows, :] = part

        @pl.when(j != 0)
        def _():
            acc_ref[rows, :] += part

        @pl.when(j == ns - 1)
        def _():
            ff = acc_ref[rows, :]
            diff = x1_ref[...] + _rms(ff, gp_ref[...]) - t_ref[...]
            dx2 = diff * (1.0 / D)
            dff, dgp = _rms_bwd(ff, gp_ref[...], dx2)
            dx2_ref[...] = dx2
            dff_ref[...] = dff.astype(BF16)
            lpart = jnp.sum(_colsum(diff * diff), axis=1, keepdims=True) * (0.5 / D)
            _acc(dgp_ref, dgp, i == 0)
            _acc(loss_ref, jnp.broadcast_to(lpart, (1, LANES)), i == 0)

    last = lambda j, i: (jnp.where(j == ns - 1, i, 0), 0)
    wsh = BS((1, F, D), lambda j, i: (j, 0, 0))
    sh = BS((1, tm, F), lambda j, i: (j, i, 0))
    return pl.pallas_call(
        body, name="ffn_fwd", grid=(ns, T // tm),
        in_specs=[BS((tm, D), lambda j, i: (i, 0)), BS((tm, D), last), BS((tm, D), last), wsh, wsh, wsh,
                  BS((1, D), lambda j, i: (0, 0))],
        out_specs=[sh, sh, BS((tm, D), last), BS((tm, D), last),
                   BS((1, D), lambda j, i: (0, 0)), BS((1, LANES), lambda j, i: (0, 0))],
        out_shape=[jax.ShapeDtypeStruct((ns, T, F), BF16), jax.ShapeDtypeStruct((ns, T, F), BF16),
                   jax.ShapeDtypeStruct((T, D), BF16), jax.ShapeDtypeStruct((T, D), F32),
                   jax.ShapeDtypeStruct((1, D), F32), jax.ShapeDtypeStruct((1, LANES), F32)],
        scratch_shapes=[pltpu.VMEM((T, D), F32)],
        compiler_params=_cp(("arbitrary", "arbitrary")),
    )(h2, x1, target, wg, wu, wd, g_post)


def _ffn_bwd(dff, h2, gs, us, wg, wu, wd, tm):
    T, D = h2.shape
    ns, F, _ = wg.shape

    def body(dff_ref, h_ref, gs_ref, us_ref, wg_ref, wu_ref, wd_ref, dh_ref, dwg_out, dwu_out, dwd_out,
             dwg_ref, dwu_ref, dwd_ref):
        first = pl.program_id(1) == 0
        dff = dff_ref[...]
        h = h_ref[...]
        parts = []
        for r in range(ROW_SPLIT):
            rows = slice(r * (tm // ROW_SPLIT), (r + 1) * (tm // ROW_SPLIT))
            dact = _dot_nt(dff[rows], wd_ref[0])
            g = gs_ref[0, rows, :].astype(F32)
            u = us_ref[0, rows, :].astype(F32)
            sig = _sigmoid(g)
            gsig = g * sig
            dg = (dact * u * (sig + gsig * (1.0 - sig))).astype(BF16)
            du = (dact * gsig).astype(BF16)
            dh_ref[0, rows, :] = (_dot(dg, wg_ref[0]) + _dot(du, wu_ref[0])).astype(BF16)
            parts.append(((gsig * u).astype(BF16), dg, du))
        a, dg, du = [jnp.concatenate(p, axis=0) for p in zip(*parts)]
        _acc(dwd_ref, _dot_tn(a, dff), first)
        _acc(dwg_ref, _dot_tn(dg, h), first)
        _acc(dwu_ref, _dot_tn(du, h), first)

        @pl.when(pl.program_id(1) == pl.num_programs(1) - 1)
        def _():
            dwg_out[0] = dwg_ref[...].astype(BF16)
            dwu_out[0] = dwu_ref[...].astype(BF16)
            dwd_out[0] = dwd_ref[...].astype(BF16)

    row = BS((tm, D), lambda j, i: (i, 0))
    sh = BS((1, tm, F), lambda j, i: (j, i, 0))
    wsh = BS((1, F, D), lambda j, i: (j, 0, 0))
    return pl.pallas_call(
        body, name="ffn_bwd", grid=(ns, T // tm),
        in_specs=[row, row, sh, sh, wsh, wsh, wsh],
        out_specs=[BS((1, tm, D), lambda j, i: (j, i, 0)), wsh, wsh, wsh],
        out_shape=[jax.ShapeDtypeStruct((ns, T, D), BF16)] + [jax.ShapeDtypeStruct((ns, F, D), BF16)] * 3,
        scratch_shapes=[pltpu.VMEM((F, D), F32)] * 3,
        compiler_params=_cp(("arbitrary", "arbitrary")),
    )(dff, h2, gs, us, wg, wu, wd)


DPROJ_PIECES = ((0, A_W), (A_W, A_W), (768, B_W), (1152, B_W), (1536, B_W), (1920, LANES), (2048, M_W))


def _put_dproj(dp_ref, piece_refs):
    for (c0, w), ref in zip(DPROJ_PIECES, piece_refs):
        dp_ref[:, c0:c0 + w] = ref[...].astype(BF16)


def _dw_in(pieces, h, ns, tk):
    T, D = h.shape
    M = P_COLS
    dsh = D // ns
    tk = min(tk, T)

    def body(*refs):
        piece_refs, h_ref, o_ref, acc_ref, dp_ref = refs[:7], refs[7], refs[8], refs[9], refs[10]
        t = pl.program_id(0)
        _put_dproj(dp_ref, piece_refs)
        _acc(acc_ref, _dot_tn(h_ref[...], dp_ref[...]), t == 0)

        @pl.when(t == pl.num_programs(0) - 1)
        def _():
            for s in range(ns):
                o_ref[s] = acc_ref[s * dsh:(s + 1) * dsh, :].T.astype(BF16)

    return pl.pallas_call(
        body, name="dw_in", grid=(T // tk,),
        in_specs=[BS((tk, w), lambda t: (t, 0)) for _, w in DPROJ_PIECES] + [BS((tk, D), lambda t: (t, 0))],
        out_specs=BS((ns, M, dsh), lambda t: (0, 0, 0)),
        out_shape=jax.ShapeDtypeStruct((ns, M, dsh), BF16),
        scratch_shapes=[pltpu.VMEM((D, M), F32), pltpu.VMEM((tk, M), BF16)],
        compiler_params=_cp(("arbitrary",)),
    )(*pieces, h)


def _outproj_bwd(dh2, x1, dx2, o, y, ya, yb, ym, ga, gb, gm, g_post, g_pre2, w_out, tm):
    T, D = x1.shape
    ns = dh2.shape[0]

    def body(dh_ref, x1_ref, dx2_ref, o_ref, y_ref, ya_ref, yb_ref, ym_ref, ga_ref, gb_ref, gm_ref, gp_ref, g2_ref, w_ref,
             dx1_ref, dw_ref, dya_ref, dyb_ref, dym_ref, dga_ref, dgb_ref, dgm_ref, dgp_ref, dg2_ref):
        first = pl.program_id(0) == 0
        dh = dh_ref[0].astype(F32)
        for j in range(1, ns):
            dh = dh + dh_ref[j].astype(F32)
        dxa, dg2 = _rms_bwd(x1_ref[...], g2_ref[...], dh)
        dx1 = dx2_ref[...] + dxa
        dx1_ref[...] = dx1
        _acc(dg2_ref, dg2, first)
        do, dgp = _rms_bwd(o_ref[...].astype(F32), gp_ref[...], dx1)
        do = do.astype(BF16)
        _acc(dw_ref, _dot_tn(y_ref[...], do), first)
        dy = _dot_nt(do, w_ref[...])
        dya, dga = _rms_bwd(ya_ref[...].astype(F32), ga_ref[...], dy[:, 0:A_W])
        dyb, dgb = _rms_bwd(yb_ref[...].astype(F32), gb_ref[...], dy[:, A_W:A_W + B_W])
        dym, dgm = _rms_bwd(ym_ref[...].astype(F32), gm_ref[...], dy[:, A_W + B_W:])
        dya_ref[...] = dya.astype(BF16)
        dyb_ref[...] = dyb.astype(BF16)
        dym_ref[...] = dym.astype(BF16)
        _acc(dga_ref, dga, first)
        _acc(dgb_ref, dgb, first)
        _acc(dgm_ref, dgm, first)
        _acc(dgp_ref, dgp, first)

    row = lambda w: BS((tm, w), lambda i: (i, 0))
    vec = lambda w: BS((1, w), lambda i: (0, 0))
    sds = jax.ShapeDtypeStruct
    return pl.pallas_call(
        body, name="outproj_bwd", grid=(T // tm,),
        in_specs=[BS((ns, tm, D), lambda i: (0, i, 0)), row(D), row(D), row(D), row(A_W + B_W + M_W), row(A_W), row(B_W),
                  row(M_W), vec(A_W), vec(B_W), vec(M_W), vec(D), vec(D), BS((A_W + B_W + M_W, D), lambda i: (0, 0))],
        out_specs=[row(D), BS((A_W + B_W + M_W, D), lambda i: (0, 0)), row(A_W), row(B_W), row(M_W),
                   vec(A_W), vec(B_W), vec(M_W), vec(D), vec(D)],
        out_shape=[sds((T, D), F32), sds((A_W + B_W + M_W, D), F32), sds((T, A_W), BF16), sds((T, B_W), BF16),
                   sds((T, M_W), BF16), sds((1, A_W), F32), sds((1, B_W), F32), sds((1, M_W), F32), sds((1, D), F32),
                   sds((1, D), F32)],
        compiler_params=_cp(("arbitrary",)),
    )(dh2, x1, dx2, o, y, ya, yb, ym, ga, gb, gm, g_post, g_pre2, w_out)


def _sgu_bwd(proj, dya, g_sgu, ws_tril, bs_full, tm):
    T = proj.shape[0]
    nch = tm // CHUNK

    def body(zu_ref, zv_ref, dy_ref, g_ref, ws_ref, b_ref, dzu_ref, dzv_ref, dws_ref, dbs_ref, dg_ref,
             du_ref, dvn_ref, dbf_ref):
        step = pl.program_id(0)
        first = step == 0
        lane = _iota((CHUNK, LANES), 1)
        tril = _iota((CHUNK, CHUNK), 0) >= _iota((CHUNK, CHUNK), 1)
        (u, vn), vjp = jax.vjp(_sgu_pre, zu_ref[...].astype(F32), zv_ref[...].astype(F32), g_ref[...])
        vnb = vn.astype(BF16)
        dy = dy_ref[...].astype(F32)

        @pl.when(first)
        def _():
            dws_ref[...] = jnp.zeros_like(dws_ref)
            dbf_ref[...] = jnp.zeros_like(dbf_ref)

        for c in range(nch):
            rs = slice(c * CHUNK, (c + 1) * CHUNK)
            for j in range(3):
                cs = slice(j * LANES, (j + 1) * LANES)
                vp = vnb[rs, cs]
                z = jnp.where(lane < HEAD, _dot(ws_ref[2 * j], vp), _dot(ws_ref[2 * j + 1], vp)) + b_ref[:, cs]
                du_ref[rs, cs] = dy[rs, cs] * z
                dz = dy[rs, cs] * u[rs, cs]
                dbf_ref[:, cs] += dz
                dzb = dz.astype(BF16)
                dz0 = jnp.where(lane < HEAD, dzb, jnp.zeros_like(dzb))
                dz1 = jnp.where(lane >= HEAD, dzb, jnp.zeros_like(dzb))
                dvn_ref[rs, cs] = jnp.where(lane < HEAD, _dot_tn(ws_ref[2 * j], dzb), _dot_tn(ws_ref[2 * j + 1], dzb))
                dws_ref[2 * j] += jnp.where(tril, _dot_nt(dz0, vp), 0.0)
                dws_ref[2 * j + 1] += jnp.where(tril, _dot_nt(dz1, vp), 0.0)
        dzu, dzv, dg = vjp((du_ref[...], dvn_ref[...]))
        dzu_ref[...] = dzu.astype(BF16)
        dzv_ref[...] = dzv.astype(BF16)
        _acc(dg_ref, dg, first)

        @pl.when(step == pl.num_programs(0) - 1)
        def _():
            out = jnp.zeros((CHUNK, LANES), F32)
            for j in range(3):
                slab = dbf_ref[:, j * LANES:(j + 1) * LANES]
                lo = jnp.sum(jnp.where(lane < HEAD, slab, 0.0), axis=1, keepdims=True)
                hi = jnp.sum(jnp.where(lane >= HEAD, slab, 0.0), axis=1, keepdims=True)
                out = out + jnp.where(lane == 2 * j, lo, 0.0) + jnp.where(lane == 2 * j + 1, hi, 0.0)
            dbs_ref[...] = out

    return pl.pallas_call(
        body, name="sgu_bwd", grid=(T // tm,),
        in_specs=[BS((tm, A_W), lambda i: (i, 0)), BS((tm, A_W), lambda i: (i, 1)), BS((tm, A_W), lambda i: (i, 0)),
                  BS((1, A_W), lambda i: (0, 0)), BS((6, CHUNK, CHUNK), lambda i: (0, 0, 0)),
                  BS((CHUNK, A_W), lambda i: (0, 0))],
        out_specs=[BS((tm, A_W), lambda i: (i, 0)), BS((tm, A_W), lambda i: (i, 0)),
                   BS((6, CHUNK, CHUNK), lambda i: (0, 0, 0)), BS((CHUNK, LANES), lambda i: (0, 0)),
                   BS((1, A_W), lambda i: (0, 0))],
        out_shape=[jax.ShapeDtypeStruct((T, A_W), BF16), jax.ShapeDtypeStruct((T, A_W), BF16),
                   jax.ShapeDtypeStruct((6, CHUNK, CHUNK), F32), jax.ShapeDtypeStruct((CHUNK, LANES), F32),
                   jax.ShapeDtypeStruct((1, A_W), F32)],
        scratch_shapes=[pltpu.VMEM((tm, A_W), F32), pltpu.VMEM((tm, A_W), F32), pltpu.VMEM((CHUNK, A_W), F32)],
        compiler_params=_cp(("arbitrary",)),
    )(proj, proj, dya, g_sgu, ws_tril, bs_full)


def _memattn_bwd(proj, kv, dym, Bl, S, tq):
    T = Bl * S
    nq = S // tq
    Mt = kv.shape[1]
    qc = 1920 // LANES

    def body(q_ref, km_ref, vm_ref, do_ref, dq_ref, dkm_ref, dvm_ref):
        first = pl.program_id(2) == 0
        lane = _iota((tq, LANES), 1)
        q = q_ref[...]
        do = do_ref[...]
        dq_out = jnp.zeros((tq, LANES), F32)
        dkm = jnp.zeros((Mt, LANES), F32)
        dvm = jnp.zeros((Mt, LANES), F32)
        for hh in range(2):
            hmask = (lane < HEAD) if hh == 0 else (lane >= HEAD)
            qs = jnp.where(hmask, q, jnp.zeros_like(q)) * 0.125
            dom = jnp.where(hmask, do, 0.0).astype(BF16)
            s = _dot_nt(qs, km_ref[0])
            pe = jnp.exp(s - jnp.max(s, axis=1, keepdims=True))
            pn = pe / jnp.sum(pe, axis=1, keepdims=True)
            dp = _dot_nt(dom, vm_ref[0])
            ds = (pn * (dp - jnp.sum(pn * dp, axis=1, keepdims=True))).astype(BF16)
            dq_out = jnp.where(hmask, _dot(ds, km_ref[0]) * 0.125, dq_out)
            dkm = dkm + _dot_tn(ds, qs)
            dvm = dvm + _dot_tn(pn, dom)
        dq_ref[...] = dq_out.astype(BF16)
        _acc(dkm_ref, dkm[None], first)
        _acc(dvm_ref, dvm[None], first)

    return pl.pallas_call(
        body, name="memattn_bwd", grid=(Bl, 2, nq),
        in_specs=[BS((tq, LANES), lambda b, p, i: (b * nq + i, qc + p)),
                  BS((1, Mt, LANES), lambda b, p, i: (b, 0, p)),
                  BS((1, Mt, LANES), lambda b, p, i: (b, 0, 2 + p)),
                  BS((tq, LANES), lambda b, p, i: (b * nq + i, p))],
        out_specs=[BS((tq, LANES), lambda b, p, i: (b * nq + i, p)),
                   BS((1, Mt, LANES), lambda b, p, i: (b, 0, p)),
                   BS((1, Mt, LANES), lambda b, p, i: (b, 0, p))],
        out_shape=[jax.ShapeDtypeStruct((T, M_W), BF16), jax.ShapeDtypeStruct((Bl, Mt, M_W), F32),
                   jax.ShapeDtypeStruct((Bl, Mt, M_W), F32)],
        compiler_params=_cp(("arbitrary", "arbitrary", "arbitrary")),
    )(proj, kv, kv, dym)


def _memkv_bwd(dkm, dvm, memn, mem, g_mem, w_kv):
    Bl, Mt, D = mem.shape

    def body(dk_ref, dv_ref, mn_ref, m_ref, g_ref, w_ref, dw_ref, dg_ref):
        first = pl.program_id(0) == 0
        dk = dk_ref[0].astype(BF16)
        dv = dv_ref[0].astype(BF16)
        mn = mn_ref[0]
        dmn = _dot_nt(dk, w_ref[:, 0:M_W]) + _dot_nt(dv, w_ref[:, M_W:])
        _, dg = _rms_bwd(m_ref[0], g_ref[...], dmn)
        _acc(dg_ref, dg, first)

        @pl.when(first)
        def _():
            dw_ref[...] = jnp.zeros_like(dw_ref)

        dw_ref[:, 0:M_W] += _dot_tn(mn, dk)
        dw_ref[:, M_W:] += _dot_tn(mn, dv)

    return pl.pallas_call(
        body, name="memkv_bwd", grid=(Bl,),
        in_specs=[BS((1, Mt, M_W), lambda b: (b, 0, 0)), BS((1, Mt, M_W), lambda b: (b, 0, 0)),
                  BS((1, Mt, D), lambda b: (b, 0, 0)), BS((1, Mt, D), lambda b: (b, 0, 0)),
                  BS((1, D), lambda b: (0, 0)), BS((D, 2 * M_W), lambda b: (0, 0))],
        out_specs=[BS((D, 2 * M_W), lambda b: (0, 0)), BS((1, D), lambda b: (0, 0))],
        out_shape=[jax.ShapeDtypeStruct((D, 2 * M_W), F32), jax.ShapeDtypeStruct((1, D), F32)],
        compiler_params=_cp(("arbitrary",)),
    )(dkm, dvm, memn, mem, g_mem, w_kv)


def _fox_bwd(proj, dyb, lse, bq, bk, Bl, S):
    T = Bl * S
    nq = S // Q_BLK
    nb = S // LANES
    qc, kc, vc = 768 // LANES, 1152 // LANES, 1536 // LANES

    def body(q_ref, k_ref, v_ref, do_ref, lse_ref, bq_ref, bk_ref,
             dq_ref, dk_ref, dv_ref, dcr_ref, ka_ref, dka_ref, dva_ref):
        p = pl.program_id(1)
        lane_s = _iota((S, LANES), 1)
        lane = _iota((Q_BLK, LANES), 1)
        sub = _iota((8, LANES), 0)
        tri = _iota((Q_BLK, Q_BLK), 1) <= _iota((Q_BLK, Q_BLK), 0)
        k = k_ref[...]
        for hh in range(2):
            data = (lane_s < HEAD) if hh == 0 else (lane_s >= HEAD)
            ka_ref[hh] = jnp.where(data, k, bk_ref[0, hh])
        dka_ref[...] = jnp.zeros_like(dka_ref)
        dva_ref[...] = jnp.zeros_like(dva_ref)

        @pl.when(p == 0)
        def _():
            dcr_ref[...] = jnp.zeros_like(dcr_ref)

        def add_colsums(ds, first_blk, h):
            cs = _colsum(ds)
            for jb in range(ds.shape[1] // LANES):
                dcr_ref[0, first_blk + jb] += jnp.where(sub == h, cs[:, jb * LANES:(jb + 1) * LANES], 0.0)

        for i in range(nq):
            r0 = i * Q_BLK
            r1 = r0 + Q_BLK
            q = q_ref[r0:r1, :]
            do = do_ref[r0:r1, :]
            lse_b = lse_ref[0, r0:r1, :]
            dq_out = jnp.zeros((Q_BLK, LANES), F32)
            for hh in range(2):
                hmask = (lane < HEAD) if hh == 0 else (lane >= HEAD)
                h = 2 * p + hh
                qs = jnp.where(hmask, q * 0.125, jnp.zeros_like(q))
                qa = jnp.where(hmask, q * 0.125, bq_ref[0, hh, r0:r1, :])
                dob = jnp.where(hmask, do, 0.0).astype(BF16)
                lse_h = jnp.sum(jnp.where(lane == hh * HEAD, lse_b, 0.0), axis=1, keepdims=True)
                pd = jnp.where(tri, jnp.exp(_dot_nt(qa, ka_ref[hh, r0:r1, :]) - lse_h), 0.0)
                dpd = _dot_nt(dob, v_ref[r0:r1, :])
                delta = jnp.sum(pd * dpd, axis=1, keepdims=True)
                psum = jnp.sum(pd, axis=1, keepdims=True)
                if i:
                    pf = jnp.exp(_dot_nt(qa, ka_ref[hh, 0:r0, :]) - lse_h)
                    dpf = _dot_nt(dob, v_ref[0:r0, :])
                    delta = delta + jnp.sum(pf * dpf, axis=1, keepdims=True)
                    psum = psum + jnp.sum(pf, axis=1, keepdims=True)
                delta = delta / psum
                dsd = pd * (dpd - delta)
                add_colsums(dsd, r0 // LANES, h)
                dsd = dsd.astype(BF16)
                dq_h = _dot(dsd, k_ref[r0:r1, :])
                dka_ref[r0:r1, :] += _dot_tn(dsd, qs)
                dva_ref[r0:r1, :] += _dot_tn(pd, dob)
                if i:
                    dsf = pf * (dpf - delta)
                    add_colsums(dsf, 0, h)
                    dsf = dsf.astype(BF16)
                    dq_h = dq_h + _dot(dsf, k_ref[0:r0, :])
                    dka_ref[0:r0, :] += _dot_tn(dsf, qs)
                    dva_ref[0:r0, :] += _dot_tn(pf, dob)
                dq_out = jnp.where(hmask, dq_h * 0.125, dq_out)
            dq_ref[r0:r1, :] = dq_out.astype(BF16)
        dk_ref[...] = dka_ref[...].astype(BF16)
        dv_ref[...] = dva_ref[...].astype(BF16)

    seq = lambda c0: BS((S, LANES), lambda b, p: (b, c0 + p))
    pair = BS((1, 2, S, LANES), lambda b, p: (b, p, 0, 0))
    rowblk = BS((1, nb, 8, LANES), lambda b, p: (b, 0, 0, 0))
    return pl.pallas_call(
        body, name="fox_bwd", grid=(Bl, 3),
        in_specs=[seq(qc), seq(kc), seq(vc), seq(0), BS((1, S, LANES), lambda b, p: (p, b, 0)), pair, pair],
        out_specs=[seq(0), seq(0), seq(0), rowblk],
        out_shape=[jax.ShapeDtypeStruct((T, B_W), BF16)] * 3 + [jax.ShapeDtypeStruct((Bl, nb, 8, LANES), F32)],
        scratch_shapes=[pltpu.VMEM((2, S, LANES), BF16), pltpu.VMEM((S, LANES), F32), pltpu.VMEM((S, LANES), F32)],
        compiler_params=_cp(("arbitrary", "arbitrary")),
    )(proj, proj, proj, dyb, lse, bq, bk)


def _gate_bwd(dc_row, fl_row):
    Bl, nb, _, _ = dc_row.shape

    def body(dc_ref, fl_ref, o_ref):
        lane = _iota((8, LANES), 1)

        carry = jnp.zeros((8, 1), F32)
        for j in reversed(range(nb)):
            r = -dc_ref[0, j]
            for k in (1, 2, 4, 8, 16, 32, 64):
                r = r + jnp.where(lane < LANES - k, pltpu.roll(r, LANES - k, 1), 0.0)
            total = jnp.sum(jnp.where(lane == 0, r, 0.0), axis=1, keepdims=True)
            dfl = (r + carry) * _sigmoid(-fl_ref[0, j])
            carry = carry + total
            o_ref[0, j * LANES:(j + 1) * LANES, :] = jnp.concatenate(
                [dfl, jnp.zeros((LANES - 8, LANES), F32)], axis=0).T

    rowblk = BS((1, nb, 8, LANES), lambda b: (b, 0, 0, 0))
    return pl.pallas_call(
        body, name="gate_bwd", grid=(Bl,),
        in_specs=[rowblk, rowblk],
        out_specs=BS((1, nb * LANES, LANES), lambda b: (b, 0, 0)),
        out_shape=jax.ShapeDtypeStruct((Bl, nb * LANES, LANES), F32),
        compiler_params=_cp(("arbitrary",)),
    )(dc_row, fl_row)


def _inproj_bwd(pieces, x2d, dx1, g_pre, w_in_p, tm):
    T, D = x2d.shape
    ns, _, dsh = w_in_p.shape

    def body(*refs):
        piece_refs = refs[:7]
        x_ref, dx1_ref, g_ref, w_ref, gx_ref, dg_ref, dbf_ref, dp_ref = refs[7:]
        first = pl.program_id(0) == 0
        _put_dproj(dp_ref, piece_refs)
        dh = jnp.concatenate([_dot(dp_ref[...], w_ref[s]) for s in range(ns)], axis=1)
        dxa, dg = _rms_bwd(x_ref[...], g_ref[...], dh)
        gx_ref[...] = dx1_ref[...] + dxa
        _acc(dg_ref, dg, first)
        _acc(dbf_ref, _colsum(piece_refs[5][...]), first)

    row = lambda w: BS((tm, w), lambda i: (i, 0))
    return pl.pallas_call(
        body, name="inproj_bwd", grid=(T // tm,),
        in_specs=[row(w) for _, w in DPROJ_PIECES] + [row(D), row(D), BS((1, D), lambda i: (0, 0)),
                                                      BS((ns, P_COLS, dsh), lambda i: (0, 0, 0))],
        out_specs=[row(D), BS((1, D), lambda i: (0, 0)), BS((1, LANES), lambda i: (0, 0))],
        out_shape=[jax.ShapeDtypeStruct((T, D), F32), jax.ShapeDtypeStruct((1, D), F32),
                   jax.ShapeDtypeStruct((1, LANES), F32)],
        scratch_shapes=[pltpu.VMEM((tm, P_COLS), BF16)],
        compiler_params=_cp(("arbitrary",)),
    )(*pieces, x2d, dx1, g_pre, w_in_p)


def _local_step(x, mem, target, W, P, reduce=None):
    Bl, S, D = x.shape
    T = Bl * S
    tm = min(512, T)
    x2d = x.reshape(T, D)
    t2d = target.reshape(T, D)
    vec = lambda a: a.reshape(1, -1)
    bf_row = jnp.pad(P["b_f"].reshape(1, -1), ((0, 0), (0, LANES - N_FOX_HEADS)))
    tril = jnp.tril(jnp.ones((CHUNK, CHUNK), bool))
    ws_tril = jnp.where(tril[None], P["w_s"][0], 0.0).astype(BF16)
    bs_full = jnp.repeat(P["b_s"][0].T, HEAD, axis=1)
    g_pre, g_sgu = vec(P["g_pre_mix"]), vec(P["g_sgu"])
    ga, gb, gm = vec(P["g_out_a"]), vec(P["g_out_b"]), vec(P["g_out_m"])
    g_mem, g_post, g_pre2, g_post2 = vec(P["g_mem"]), vec(P["g_post_mix"]), vec(P["g_pre_ffn"]), vec(P["g_post_ffn"])

    h, proj, flog = _inproj_fwd(x2d, g_pre, W["w_in"], tm)
    bq, bk, fl_row = _gate_fwd(flog.reshape(Bl, S, LANES), bf_row)
    ya = _sgu_fwd(proj, g_sgu, ws_tril, bs_full, tm)
    yb, lse = _fox_fwd(proj, bq, bk, Bl, S)
    memn, kv = _memkv_fwd(mem, g_mem, W["w_mem_kv"])
    ym = _memattn_fwd(proj, kv, Bl, S, min(2048, S))
    y, o, x1, h2 = _outproj_fwd(ya, yb, ym, x2d, ga, gb, gm, g_post, g_pre2, W["w_out"], tm)
    gs, us, dff, dx2, dg_post2, loss = _ffn_fwd(h2, x1, t2d, W["w_gate"], W["w_up"], W["w_down"], g_post2, tm)

    dh2, d_w_gate, d_w_up, d_w_down = _ffn_bwd(dff, h2, gs, us, W["w_gate"], W["w_up"], W["w_down"], min(1024, T))
    ffn = [d_w_gate, d_w_up, d_w_down]
    if reduce is not None:
        pending, _ = reduce.begin("ffn", ffn)
    dx1, d_w_out, dya, dyb, dym, dga, dgb, dgm, dg_post, dg_pre2 = _outproj_bwd(
        dh2, x1, dx2, o, y, ya, yb, ym, ga, gb, gm, g_post, g_pre2, W["w_out"], tm)
    if reduce is not None:
        ffn, (dya, dyb, dym) = reduce.finish("ffn", pending, (dya, dyb, dym))
    dzu, dzv, dws, dbs_cols, dg_sgu = _sgu_bwd(proj, dya, g_sgu, ws_tril, bs_full, tm)
    dqm, dkm, dvm = _memattn_bwd(proj, kv, dym, Bl, S, min(2048, S))
    d_w_kv, dg_mem = _memkv_bwd(dkm, dvm, memn, mem, g_mem, W["w_mem_kv"])
    mid = [d_w_kv, d_w_out]
    dq, dk, dv, dc_row = _fox_bwd(proj, dyb, lse, bq, bk, Bl, S)
    if reduce is not None:
        done = reduce.apply(BIG[3:], ffn)
        pending, after = reduce.begin("mid", mid, (dc_row,) + done)
        dc_row = after[0]
    dfl = _gate_bwd(dc_row, fl_row).reshape(T, LANES)
    pieces = (dzu, dzv, dq, dk, dv, dfl, dqm)
    grad_x, dg_pre, dbf = _inproj_bwd(pieces, x2d, dx1, g_pre, W["w_in"], tm)
    if reduce is not None:
        mid, (dfl,) = reduce.finish("mid", pending, (dfl,))
        pieces = (dzu, dzv, dq, dk, dv, dfl, dqm)
    d_w_in = _dw_in(pieces, h, W["w_in"].shape[0], 1024)
    if reduce is None:
        big = dict(zip(BIG, [d_w_in] + mid + ffn))
    else:
        done = reduce.apply(BIG[1:3], mid)
        big = {"w_in": reduce.begin("in", [d_w_in], done)[0]}
    small = {"g_pre_mix": dg_pre, "b_f": dbf[:, :N_FOX_HEADS], "g_sgu": dg_sgu, "w_s": dws, "b_s": dbs_cols[:, :N_FOX_HEADS].T,
             "g_out_a": dga, "g_out_b": dgb, "g_out_m": dgm, "g_mem": dg_mem, "g_post_mix": dg_post,
             "g_pre_ffn": dg_pre2, "g_post_ffn": dg_post2, "loss": loss[:, :1]}
    return grad_x.reshape(Bl, S, D), big, small


def _place():
    return lax.axis_index("x"), lax.axis_index("y"), lax.axis_index("c")


def _exchange_on_sequencer(srcs, own_full, name, collective_id):
    n = len(srcs)

    def body(*refs):
        src, dst = refs[:n], refs[n:2 * n]
        lsem, isend, irecv, dsend, drecv = refs[2 * n:]
        x, y, c = _place()
        oc = 1 - c
        s_me = 2 * x + y
        sib = (x, y, oc)
        chips = [(1 - x, y), (x, 1 - y), (1 - x, 1 - y)]
        barrier = pltpu.get_barrier_semaphore()
        for dev in [(cx, cy, c) for cx, cy in chips] + [sib]:
            pl.semaphore_signal(barrier, inc=1, device_id=dev, device_id_type=MESH)
        pl.semaphore_wait(barrier, 4)

        def remote(a, b, ssem, rsem, dev):
            return pltpu.make_async_remote_copy(src_ref=a, dst_ref=b, send_sem=ssem, recv_sem=rsem,
                                                device_id=dev, device_id_type=MESH)

        sends, local = [], []
        for w in range(n):
            for j, (cx, cy) in enumerate(chips):
                half = src[w].at[c] if own_full else src[w].at[2 * cx + cy]
                cp = remote(half, dst[w].at[s_me, c], isend.at[w, j], irecv.at[w, j], (cx, cy, c))
                cp.start()
                sends.append(cp)
            if own_full:
                cp = remote(src[w], dst[w].at[s_me], dsend.at[w, 3], drecv.at[w, 3], sib)
            else:
                cp = remote(src[w].at[s_me], dst[w].at[s_me, c], dsend.at[w, 3], drecv.at[w, 3], sib)
                loc = pltpu.make_async_copy(src[w].at[s_me], dst[w].at[s_me, c], lsem.at[w])
                loc.start()
                local.append(loc)
            cp.start()
            sends.append(cp)
        for w in range(n):
            for j, (cx, cy) in enumerate(chips):
                landed = dst[w].at[2 * cx + cy, c]
                remote(landed, landed, isend.at[w, j], irecv.at[w, j], (cx, cy, c)).wait_recv()
                cp = remote(landed, landed, dsend.at[w, j], drecv.at[w, j], sib)
                cp.start()
                sends.append(cp)
        for w in range(n):
            for j, (cx, cy) in enumerate(chips):
                landed = dst[w].at[2 * cx + cy, oc]
                remote(landed, landed, dsend.at[w, j], drecv.at[w, j], sib).wait_recv()
            landed = dst[w].at[s_me] if own_full else dst[w].at[s_me, oc]
            remote(landed, landed, dsend.at[w, 3], drecv.at[w, 3], sib).wait_recv()
        for cp in sends:
            cp.wait_send()
        for loc in local:
            loc.wait()

    return pl.kernel(
        body, out_type=[jax.ShapeDtypeStruct((4, 2) + s.shape[1:], s.dtype) for s in srcs],
        mesh=plsc.ScalarSubcoreMesh(axis_name="sequencer", num_cores=1), name=name,
        scratch_types=[pltpu.SemaphoreType.DMA((n,)), pltpu.SemaphoreType.DMA((n, 3)), pltpu.SemaphoreType.DMA((n, 3)),
                       pltpu.SemaphoreType.DMA((n, 4)), pltpu.SemaphoreType.DMA((n, 4))],
        compiler_params=pltpu.CompilerParams(collective_id=collective_id),
    )(*srcs)


def _sibling_swap(grads, name, collective_id):
    n = len(grads)

    def body(*refs):
        g, theirs = refs[:n], refs[n:2 * n]
        ssem, rsem = refs[2 * n:]
        x, y, c = _place()
        sib = (x, y, 1 - c)
        barrier = pltpu.get_barrier_semaphore()
        pl.semaphore_signal(barrier, inc=1, device_id=sib, device_id_type=MESH)
        pl.semaphore_wait(barrier, 1)
        cps = []
        for w in range(n):
            cp = pltpu.make_async_remote_copy(src_ref=g[w].at[:, 1 - c], dst_ref=theirs[w], send_sem=ssem.at[w],
                                              recv_sem=rsem.at[w], device_id=sib, device_id_type=MESH)
            cp.start()
            cps.append(cp)
        for cp in cps:
            cp.wait()

    return pl.kernel(
        body, out_type=[jax.ShapeDtypeStruct((4,) + g.shape[2:], g.dtype) for g in grads],
        mesh=plsc.ScalarSubcoreMesh(axis_name="sequencer", num_cores=1), name=name,
        scratch_types=[pltpu.SemaphoreType.DMA((n,)), pltpu.SemaphoreType.DMA((n,))],
        compiler_params=pltpu.CompilerParams(collective_id=collective_id),
    )(*grads)


def _add_pair(core, g, theirs, name):
    _, _, hr, C = g.shape

    def body(core_ref, g_ref, t_ref, o_ref):
        o_ref[0] = (g_ref[0, 0].astype(F32) + t_ref[0].astype(F32)).astype(BF16)

    blk = BS((1, hr, C), lambda s, core_ref: (s, 0, 0))
    return pl.pallas_call(
        body, name=name,
        grid_spec=pltpu.PrefetchScalarGridSpec(
            num_scalar_prefetch=1, grid=(4,),
            in_specs=[BS((1, 1, hr, C), lambda s, core_ref: (s, core_ref[0], 0, 0)), blk], out_specs=blk),
        out_shape=jax.ShapeDtypeStruct(theirs.shape, BF16), compiler_params=_cp(("arbitrary",)))(core, g, theirs)


def _sum_chips(r, name):
    _, _, hr, C = r.shape

    def body(r_ref, o_ref):
        o_ref[...] = ((r_ref[0, 0].astype(F32) + r_ref[1, 0].astype(F32)) + r_ref[2, 0].astype(F32)) + r_ref[3, 0].astype(F32)

    return pl.pallas_call(body, name=name, grid=(2,), in_specs=[BS((4, 1, hr, C), lambda h: (0, h, 0, 0))],
                          out_specs=BS((hr, C), lambda h: (h, 0)), out_shape=jax.ShapeDtypeStruct((2 * hr, C), F32),
                          compiler_params=_cp(("arbitrary",)))(r)


class _Reducer:
    IDS = {"ffn": (4, 5), "mid": (6, 7), "in": (8, 9)}

    def __init__(self, core, apply):
        self.core = core
        self.apply = apply

    def begin(self, tag, grads, after=()):
        grads, after = lax.optimization_barrier((list(grads), after))
        g4 = [g.reshape(4, 2, -1, g.shape[-1]) for g in grads]
        return (g4, _sibling_swap(g4, "swap_" + tag, self.IDS[tag][0])), after

    def finish(self, tag, pending, hold):
        g4, theirs = pending
        sums = [_add_pair(self.core, g, t, "chip_sum_%s_%d" % (tag, k)) for k, (g, t) in enumerate(zip(g4, theirs))]
        sums, hold = lax.optimization_barrier((sums, hold))
        return _exchange_on_sequencer(sums, False, "scatter_" + tag, self.IDS[tag][1]), hold


def _small_allreduce(part):
    R = part.shape[0]
    rs = R // 8
    masks = [(mx, my, mc) for mx in (0, 1) for my in (0, 1) for mc in (0, 1)][1:]

    def body(p_ref, o_ref, buf_ref, s1, r1, s2, r2):
        x, y, c = _place()
        d = 4 * x + 2 * y + c
        mine = pl.ds(pl.multiple_of(d * rs, 8), rs)
        peers = [((x + mx) % 2, (y + my) % 2, (c + mc) % 2) for mx, my, mc in masks]
        first, second = [], []
        for k, (px, py, pc) in enumerate(peers):
            theirs = pl.ds(pl.multiple_of((4 * px + 2 * py + pc) * rs, 8), rs)
            cp = pltpu.make_async_remote_copy(src_ref=p_ref.at[theirs, :], dst_ref=buf_ref.at[d], send_sem=s1.at[k],
                                              recv_sem=r1.at[k], device_id=(px, py, pc), device_id_type=MESH)
            cp.start()
            first.append(cp)
        buf_ref[d] = p_ref[mine, :]
        for k, (px, py, pc) in enumerate(peers):
            slot = buf_ref.at[4 * px + 2 * py + pc]
            pltpu.make_async_remote_copy(src_ref=slot, dst_ref=slot, send_sem=s1.at[k], recv_sem=r1.at[k],
                                         device_id=(px, py, pc), device_id_type=MESH).wait_recv()
        total = buf_ref[0]
        for k in range(1, 8):
            total = total + buf_ref[k]
        o_ref[mine, :] = total
        for k, (px, py, pc) in enumerate(peers):
            cp = pltpu.make_async_remote_copy(src_ref=o_ref.at[mine, :], dst_ref=o_ref.at[mine, :], send_sem=s2.at[k],
                                              recv_sem=r2.at[k], device_id=(px, py, pc), device_id_type=MESH)
            cp.start()
            second.append(cp)
        for k, (px, py, pc) in enumerate(peers):
            rows = o_ref.at[pl.ds(pl.multiple_of((4 * px + 2 * py + pc) * rs, 8), rs), :]
            pltpu.make_async_remote_copy(src_ref=rows, dst_ref=rows, send_sem=s2.at[k], recv_sem=r2.at[k],
                                         device_id=(px, py, pc), device_id_type=MESH).wait_recv()
        for cp in first + second:
            cp.wait_send()

    vm = pl.BlockSpec(memory_space=pltpu.VMEM)
    return pl.pallas_call(
        body, name="small_allreduce", in_specs=[vm], out_specs=vm, out_shape=jax.ShapeDtypeStruct(part.shape, F32),
        scratch_shapes=[pltpu.VMEM((8, rs, LANES), F32)] + [pltpu.SemaphoreType.DMA((7,))] * 4,
    )(part)


def _adamw(w, g, m, v, name):
    R, C = w.shape
    summed = g.ndim == 4
    if summed:
        tr = R // 2
    else:
        tr = R if R * C * 4 <= (1 << 21) else R // 2
        if tr % 8:
            tr = R
    c1 = 1.0 / (1.0 - ADAM_B1 ** ADAM_STEP)
    c2 = 1.0 / (1.0 - ADAM_B2 ** ADAM_STEP)

    def body(w_ref, g_ref, m_ref, v_ref, *outs):
        if summed:
            g_ = ((g_ref[0, 0].astype(F32) + g_ref[1, 0].astype(F32)) + g_ref[2, 0].astype(F32)) + g_ref[3, 0].astype(F32)
            outs[0][...] = g_
        else:
            g_ = g_ref[...]
        d_ref, mo_ref, vo_ref = outs[-3:]
        m_ = ADAM_B1 * m_ref[...] + (1.0 - ADAM_B1) * g_
        v_ = ADAM_B2 * v_ref[...] + (1.0 - ADAM_B2) * (g_ * g_)
        mo_ref[...] = m_
        vo_ref[...] = v_
        d_ref[...] = -ADAM_LR * ((m_ * c1) / (jnp.sqrt(v_ * c2) + ADAM_EPS) + ADAM_WD * w_ref[...])

    blk = BS((tr, C), lambda i: (i, 0))
    g_blk = BS((4, 1, tr, C), lambda i: (0, i, 0, 0)) if summed else blk
    nout = 4 if summed else 3
    return pl.pallas_call(body, name=name, grid=(R // tr,), in_specs=[blk, g_blk, blk, blk], out_specs=[blk] * nout,
                          out_shape=[jax.ShapeDtypeStruct((R, C), F32)] * nout,
                          compiler_params=_cp(("arbitrary",)))(w, g, m, v)


def _adamw_unit_rows(w, g, m, v, name):
    C, _, R = w.shape
    tc = C // 2 if C % 2 == 0 else C
    c1 = 1.0 / (1.0 - ADAM_B1 ** ADAM_STEP)
    c2 = 1.0 / (1.0 - ADAM_B2 ** ADAM_STEP)

    def body(w_ref, g_ref, m_ref, v_ref, go_ref, d_ref, mo_ref, vo_ref):
        g_ = g_ref[...]
        go_ref[...] = g_
        m_ = ADAM_B1 * m_ref[...] + (1.0 - ADAM_B1) * g_
        v_ = ADAM_B2 * v_ref[...] + (1.0 - ADAM_B2) * (g_ * g_)
        mo_ref[...] = m_
        vo_ref[...] = v_
        d_ref[...] = -ADAM_LR * ((m_ * c1) / (jnp.sqrt(v_ * c2) + ADAM_EPS) + ADAM_WD * w_ref[...])

    blk = BS((tc, 1, R), lambda i: (i, 0, 0))
    return pl.pallas_call(body, name=name, grid=(C // tc,), in_specs=[blk] * 4, out_specs=[blk] * 4,
                          out_shape=[jax.ShapeDtypeStruct((C, 1, R), F32)] * 4,
                          compiler_params=_cp(("arbitrary",)))(w, g, m, v)


SMALL = ("g_pre_mix", "b_f", "g_sgu", "w_s", "b_s", "g_out_a", "g_out_b", "g_out_m", "g_mem", "g_post_mix",
         "g_pre_ffn", "g_post_ffn")
BIG = ("w_in", "w_mem_kv", "w_out", "w_gate", "w_up", "w_down")
TRANSPOSED = ("w_in", "w_gate", "w_up")
WEIGHTS = ("g_pre_mix", "w_in", "b_f", "g_sgu", "w_s", "b_s", "g_out_a", "g_out_b", "g_out_m", "g_mem", "w_mem_kv",
           "w_out", "g_post_mix", "g_pre_ffn", "w_gate", "w_up", "w_down", "g_post_ffn")


VECTORS = ("g_pre_mix", "b_f", "g_sgu", "g_out_a", "g_out_b", "g_out_m", "g_mem", "g_post_mix", "g_pre_ffn", "g_post_ffn")
VEC_ROWS = 16
WS_ROWS = N_FOX_HEADS * CHUNK
BS_ROWS = 8


def _pack_small(small, vw):
    stack = jnp.zeros((VEC_ROWS, vw), F32)
    for k, n in enumerate(VECTORS + ("loss",)):
        row = small[n].reshape(1, -1)
        stack = stack + jnp.pad(row, ((k, VEC_ROWS - 1 - k), (0, vw - row.shape[1])))
    parts = [small["w_s"].reshape(WS_ROWS, LANES), jnp.pad(small["b_s"], ((0, BS_ROWS - N_FOX_HEADS), (0, 0))),
             stack.reshape(-1, LANES)]
    rows = sum(p.shape[0] for p in parts)
    return jnp.concatenate(parts + [jnp.zeros((-rows % 64, LANES), F32)], axis=0)


def _adamw_small(vec_g, vec_wmv, ws, bs):
    c1 = 1.0 / (1.0 - ADAM_B1 ** ADAM_STEP)
    c2 = 1.0 / (1.0 - ADAM_B2 ** ADAM_STEP)
    nv = len(vec_wmv)

    def adam(g, w, m, v):
        m_ = ADAM_B1 * m + (1.0 - ADAM_B1) * g
        v_ = ADAM_B2 * v + (1.0 - ADAM_B2) * (g * g)
        return -ADAM_LR * ((m_ * c1) / (jnp.sqrt(v_ * c2) + ADAM_EPS) + ADAM_WD * w), m_, v_

    def body(*refs):
        vg_ref, ins, outs = refs[0], refs[1:1 + 3 * nv + 8], refs[1 + 3 * nv + 8:]
        for k in range(nv):
            w_ref, m_ref, v_ref = ins[3 * k:3 * k + 3]
            g = vg_ref[k:k + 1, 0:w_ref.shape[1]]
            d, m_, v_ = adam(g, w_ref[...], m_ref[...], v_ref[...])
            for o_ref, val in zip(outs[4 * k:4 * k + 4], (g, d, m_, v_)):
                o_ref[...] = val
        for j in range(2):
            g_ref, w_ref, m_ref, v_ref = ins[3 * nv + 4 * j:3 * nv + 4 * j + 4]
            for o_ref, val in zip(outs[4 * nv + 3 * j:4 * nv + 3 * j + 3], adam(g_ref[...], w_ref[...], m_ref[...], v_ref[...])):
                o_ref[...] = val

    vm = pl.BlockSpec(memory_space=pltpu.VMEM)
    operands = [vec_g] + [a for wmv in vec_wmv for a in wmv] + list(ws) + list(bs)
    out_shape = ([jax.ShapeDtypeStruct(wmv[0].shape, F32) for wmv in vec_wmv for _ in range(4)]
                 + [jax.ShapeDtypeStruct(ws[1].shape, F32)] * 3 + [jax.ShapeDtypeStruct(bs[1].shape, F32)] * 3)
    outs = pl.pallas_call(body, name="adamw_small", in_specs=[vm] * len(operands), out_specs=[vm] * len(out_shape),
                          out_shape=out_shape)(*operands)
    return [outs[4 * k:4 * k + 4] for k in range(nv)], outs[4 * nv:4 * nv + 3], outs[4 * nv + 3:]


def kernel(x, mem, g_pre_mix, w_in, b_f, g_sgu, w_s, b_s, g_out_a, g_out_b, g_out_m, g_mem, w_mem_kv, w_out, g_post_mix, g_pre_ffn, w_gate, w_up, w_down, g_post_ffn, loss_target, m_g_pre_mix, m_w_in, m_b_f, m_g_sgu, m_w_s, m_b_s, m_g_out_a, m_g_out_b, m_g_out_m, m_g_mem, m_w_mem_kv, m_w_out, m_g_post_mix, m_g_pre_ffn, m_w_gate, m_w_up, m_w_down, m_g_post_ffn, v_g_pre_mix, v_w_in, v_b_f, v_g_sgu, v_w_s, v_b_s, v_g_out_a, v_g_out_b, v_g_out_m, v_g_mem, v_w_mem_kv, v_w_out, v_g_post_mix, v_g_pre_ffn, v_w_gate, v_w_up, v_w_down, v_g_post_ffn):
    Wt = dict(g_pre_mix=g_pre_mix, w_in=w_in, b_f=b_f, g_sgu=g_sgu, w_s=w_s, b_s=b_s, g_out_a=g_out_a, g_out_b=g_out_b,
              g_out_m=g_out_m, g_mem=g_mem, w_mem_kv=w_mem_kv, w_out=w_out, g_post_mix=g_post_mix, g_pre_ffn=g_pre_ffn,
              w_gate=w_gate, w_up=w_up, w_down=w_down, g_post_ffn=g_post_ffn)
    Mo = dict(g_pre_mix=m_g_pre_mix, w_in=m_w_in, b_f=m_b_f, g_sgu=m_g_sgu, w_s=m_w_s, b_s=m_b_s, g_out_a=m_g_out_a,
              g_out_b=m_g_out_b, g_out_m=m_g_out_m, g_mem=m_g_mem, w_mem_kv=m_w_mem_kv, w_out=m_w_out,
              g_post_mix=m_g_post_mix, g_pre_ffn=m_g_pre_ffn, w_gate=m_w_gate, w_up=m_w_up, w_down=m_w_down,
              g_post_ffn=m_g_post_ffn)
    Vo = dict(g_pre_mix=v_g_pre_mix, w_in=v_w_in, b_f=v_b_f, g_sgu=v_g_sgu, w_s=v_w_s, b_s=v_b_s, g_out_a=v_g_out_a,
              g_out_b=v_g_out_b, g_out_m=v_g_out_m, g_mem=v_g_mem, w_mem_kv=v_w_mem_kv, w_out=v_w_out,
              g_post_mix=v_g_post_mix, g_pre_ffn=v_g_pre_ffn, w_gate=v_w_gate, w_up=v_w_up, w_down=v_w_down,
              g_post_ffn=v_g_post_ffn)

    gap = P_COLS - IN_COLS

    def to_kernel(n, w):
        if n in TRANSPOSED:
            w = w.T
        if n == "w_in":
            w = jnp.pad(w[:F_END], ((0, P_COLS - F_END), (0, 0))) + jnp.pad(w[F_END:], ((F_END + gap, 0), (0, 0)))
        return w

    def ungroup(g):
        return jnp.pad(g[:F_END], ((0, IN_COLS - F_END), (0, 0))) + jnp.pad(g[F_END + gap:], ((F_END, 0), (0, 0)))

    shards = {n: to_kernel(n, Wt[n][0]) for n in BIG}
    srcs = [shards[n].astype(BF16).reshape(2, shards[n].shape[0] // 2, shards[n].shape[1]) for n in BIG]
    fulls = (_exchange_on_sequencer(srcs[:1], True, "gather_w_in", 1)
             + _exchange_on_sequencer(srcs[1:3], True, "gather_kv_out", 2)
             + _exchange_on_sequencer(srcs[3:], True, "gather_ffn", 3))
    W = {}
    for n, f in zip(BIG, fulls):
        _, _, hr, C = f.shape
        W[n] = f.reshape(8 * hr, C) if n in ("w_mem_kv", "w_out") else f.reshape(4, 2 * hr, C)

    P = {n: Wt[n] for n in SMALL}
    grads, deltas, new_m, new_v = {}, {}, {}, {}

    def apply(names, landed):
        for n, r in zip(names, landed):
            if n == "w_in":
                g_t = ungroup(_sum_chips(r, "sum_chips_" + n))
                lift = lambda a: jnp.transpose(a, (2, 0, 1))
                outs = _adamw_unit_rows(lift(Wt[n]), g_t[:, None, :], lift(Mo[n]), lift(Vo[n]), "adamw_" + n)
                g, d, m1, v1 = [jnp.transpose(a, (1, 2, 0))[0] for a in outs]
            elif n in TRANSPOSED:
                g, d, m1, v1 = [a.T for a in _adamw(Wt[n][0].T, r, Mo[n][0].T, Vo[n][0].T, "adamw_" + n)]
            else:
                g, d, m1, v1 = _adamw(Wt[n][0], r, Mo[n][0], Vo[n][0], "adamw_" + n)
            grads[n], deltas[n], new_m[n], new_v[n] = g[None], d[None], m1[None], v1[None]
        return tuple(deltas[n] for n in names)

    core = lax.axis_index("c").astype(jnp.int32).reshape(1)
    reducer = _Reducer(core, apply)
    grad_x, pending, small = _local_step(x, mem, loss_target, W, P, reducer)

    vw = -(-max(x.shape[-1], A_W) // LANES) * LANES
    landed, (packed,) = reducer.finish("in", pending["w_in"], (_pack_small(small, vw),))
    total = _small_allreduce(packed)
    apply(BIG[:1], landed)

    lane_row = lambda a: jnp.pad(a, ((0, 0), (0, -a.shape[1] % LANES)))
    vec_g = total[WS_ROWS + BS_ROWS:WS_ROWS + BS_ROWS + VEC_ROWS * vw // LANES].reshape(VEC_ROWS, vw)
    ws_g = total[:WS_ROWS]
    bs_g = total[WS_ROWS:WS_ROWS + N_FOX_HEADS]
    rows = lambda a, r: a.reshape(r, LANES)
    per_vec, ws_out, bs_out = _adamw_small(
        vec_g, [tuple(lane_row(a[n]) for a in (Wt, Mo, Vo)) for n in VECTORS],
        (ws_g,) + tuple(rows(a["w_s"], WS_ROWS) for a in (Wt, Mo, Vo)),
        (bs_g,) + tuple(rows(a["b_s"], N_FOX_HEADS) for a in (Wt, Mo, Vo)))
    for n, outs in zip(VECTORS, per_vec):
        grads[n], deltas[n], new_m[n], new_v[n] = [o[:, :Wt[n].shape[1]] for o in outs]
    for n, g, outs in (("w_s", ws_g, ws_out), ("b_s", bs_g, bs_out)):
        grads[n], deltas[n], new_m[n], new_v[n] = [o.reshape(Wt[n].shape) for o in (g,) + tuple(outs)]
    loss = vec_g[len(VECTORS), 0]

    return (loss, grad_x, *[grads[n] for n in WEIGHTS], *[deltas[n] for n in WEIGHTS],
            *[new_m[n] for n in WEIGHTS], *[new_v[n] for n in WEIGHTS])
```

```python
import functools

import jax
import jax.numpy as jnp
from jax import lax
from jax.experimental import pallas as pl
from jax.experimental.pallas import tpu as pltpu
from jax.experimental.pallas import tpu_sc as plsc

F32 = jnp.float32
BF16 = jnp.bfloat16
EPS = 1e-6
NEG = -1e30
HEAD = 64
A_W, B_W, M_W = 384, 384, 256
N_FOX_HEADS = 6
CHUNK = 128
IN_COLS = 2 * A_W + 3 * B_W + N_FOX_HEADS + M_W
P_MAIN = 2 * A_W + 3 * B_W + M_W
P_COLS = P_MAIN + 128
F_END = 2 * A_W + 3 * B_W + N_FOX_HEADS
LANES = 128
Q_BLK, K_BLK = 512, 128
ROW_SPLIT = 4
ADAM_LR, ADAM_B1, ADAM_B2, ADAM_EPS, ADAM_WD, ADAM_STEP = 0.001, 0.9, 0.999, 1e-08, 0.01, 10
VMEM_LIMIT = 56 * 1024 * 1024
MESH = pl.DeviceIdType.MESH
BS = pl.BlockSpec


def _cp(sem=None):
    return pltpu.CompilerParams(dimension_semantics=sem, vmem_limit_bytes=VMEM_LIMIT)


def _iota(shape, dim):
    return lax.broadcasted_iota(jnp.int32, shape, dim)


def _dot(a, b):
    return jnp.dot(a.astype(BF16), b.astype(BF16), preferred_element_type=F32)


def _dot_nt(a, b):
    return lax.dot_general(a.astype(BF16), b.astype(BF16), (((1,), (1,)), ((), ())), preferred_element_type=F32)


def _dot_tn(a, b):
    return lax.dot_general(a.astype(BF16), b.astype(BF16), (((0,), (0,)), ((), ())), preferred_element_type=F32)


def _rms(x, g):
    return x * lax.rsqrt(jnp.mean(x * x, axis=-1, keepdims=True) + EPS) * g


def _rms_bwd(x, g, dy):
    r = lax.rsqrt(jnp.mean(x * x, axis=-1, keepdims=True) + EPS)
    xr = x * r
    gd = dy * g
    m = jnp.mean(gd * xr, axis=-1, keepdims=True)
    return (gd - xr * m) * r, _colsum(dy * xr)


def _gelu(x):
    return 0.5 * x * (1.0 + jnp.tanh(0.7978845608028654 * (x + 0.044715 * (x * x * x))))


def _sigmoid(x):
    return 1.0 / (1.0 + jnp.exp(-x))


def _silu_mul(g, u):
    return g * _sigmoid(g) * u


def _logsig(x):
    return jnp.minimum(x, 0.0) - jnp.log(1.0 + jnp.exp(-jnp.abs(x)))


def _colsum(x):
    return jnp.sum(x, axis=0, keepdims=True)


def _acc(ref, val, first):
    @pl.when(first)
    def _():
        ref[...] = val

    @pl.when(jnp.logical_not(first))
    def _():
        ref[...] += val


def _inproj_fwd(x2d, g_pre, w_in_p, tm):
    T, D = x2d.shape
    CH = 768
    nchunk = P_COLS // CH
    ns, _, dsh = w_in_p.shape

    def body(x_ref, g_ref, w_ref, h_ref, proj_ref, fl_ref):
        h = _rms(x_ref[...], g_ref[...]).astype(BF16)
        h_ref[...] = h
        for n in range(nchunk):
            rows = slice(n * CH, (n + 1) * CH)
            r = _dot_nt(h[:, 0:dsh], w_ref[0, rows, :])
            for s in range(1, ns):
                r = r + _dot_nt(h[:, s * dsh:(s + 1) * dsh], w_ref[s, rows, :])
            if n < nchunk - 1:
                proj_ref[:, rows] = r.astype(BF16)
            else:
                fg = 1920 - n * CH
                proj_ref[:, n * CH:1920] = r[:, :fg].astype(BF16)
                fl_ref[...] = r[:, fg:fg + LANES]
                proj_ref[:, 1920:P_MAIN] = r[:, fg + LANES:].astype(BF16)

    return pl.pallas_call(
        body, name="inproj_fwd", grid=(T // tm,),
        in_specs=[BS((tm, D), lambda i: (i, 0)), BS((1, D), lambda i: (0, 0)),
                  BS((ns, P_COLS, dsh), lambda i: (0, 0, 0))],
        out_specs=[BS((tm, D), lambda i: (i, 0)), BS((tm, P_MAIN), lambda i: (i, 0)), BS((tm, LANES), lambda i: (i, 0))],
        out_shape=[jax.ShapeDtypeStruct((T, D), BF16), jax.ShapeDtypeStruct((T, P_MAIN), BF16),
                   jax.ShapeDtypeStruct((T, LANES), F32)],
        compiler_params=_cp(("arbitrary",)),
    )(x2d, g_pre, w_in_p)


def _gate_fwd(flog3, bf_row):
    Bl, S, _ = flog3.shape
    nb = S // LANES

    def body(f_ref, b_ref, bq_ref, bk_ref, fr_ref):
        row = _iota((LANES, LANES), 0)
        lane = _iota((LANES, LANES), 1)
        one = jnp.ones((LANES, LANES), BF16)
        zero = jnp.zeros((LANES, LANES), BF16)

        carry = jnp.zeros((1, LANES), F32)
        for j in range(nb):
            r0 = j * LANES
            fl = f_ref[0, pl.ds(r0, LANES), :] + b_ref[...]
            fr_ref[0, j] = fl.T[0:8, :]
            c = _logsig(fl)
            for k in (1, 2, 4, 8, 16, 32, 64):
                c = c + jnp.where(row >= k, pltpu.roll(c, k, 0), 0.0)
            total = _colsum(jnp.where(row == LANES - 1, c, 0.0))
            c = c + carry
            carry = carry + total
            for h in range(N_FOX_HEADS):
                col = jnp.sum(jnp.where(lane == h, c, 0.0), axis=1, keepdims=True)
                hi = col.astype(BF16)
                rest = col - hi.astype(F32)
                mid = rest.astype(BF16)
                lo = (rest - mid.astype(F32)).astype(BF16)
                base = _bias_lane(h)
                bq = jnp.where(lane == base, hi, jnp.where(lane == base + 1, mid, jnp.where(lane == base + 2, lo, zero)))
                bq = jnp.where((lane >= base + 3) & (lane < base + 6), one, bq)
                bk = jnp.where(lane == base + 3, -hi, jnp.where(lane == base + 4, -mid, jnp.where(lane == base + 5, -lo, zero)))
                bk = jnp.where((lane >= base) & (lane < base + 3), one, bk)
                bq_ref[0, h, pl.ds(r0, LANES), :] = bq
                bk_ref[0, h, pl.ds(r0, LANES), :] = bk

    slab = BS((1, N_FOX_HEADS, S, LANES), lambda b: (b, 0, 0, 0))
    return pl.pallas_call(
        body, name="gate_fwd", grid=(Bl,),
        in_specs=[BS((1, S, LANES), lambda b: (b, 0, 0)), BS((1, LANES), lambda b: (0, 0))],
        out_specs=[slab, slab, BS((1, nb, 8, LANES), lambda b: (b, 0, 0, 0))],
        out_shape=[jax.ShapeDtypeStruct((Bl, N_FOX_HEADS, S, LANES), BF16),
                   jax.ShapeDtypeStruct((Bl, N_FOX_HEADS, S, LANES), BF16),
                   jax.ShapeDtypeStruct((Bl, nb, 8, LANES), F32)],
        compiler_params=_cp(("arbitrary",)),
    )(flog3, bf_row)


def _bias_lane(h):
    return HEAD if h % 2 == 0 else 0


def _sgu_pre(zu, zv, g_sgu):
    return _gelu(zu), _rms(_gelu(zv), g_sgu)


def _sgu_fwd(proj, g_sgu, ws_tril, bs_full, tm):
    T = proj.shape[0]
    nch = tm // CHUNK

    def body(zu_ref, zv_ref, g_ref, ws_ref, b_ref, ya_ref):
        lane = _iota((CHUNK, LANES), 1)
        u, vn = _sgu_pre(zu_ref[...].astype(F32), zv_ref[...].astype(F32), g_ref[...])
        vn = vn.astype(BF16)
        for c in range(nch):
            rs = slice(c * CHUNK, (c + 1) * CHUNK)
            for j in range(3):
                cs = slice(j * LANES, (j + 1) * LANES)
                vp = vn[rs, cs]
                z = jnp.where(lane < HEAD, _dot(ws_ref[2 * j], vp), _dot(ws_ref[2 * j + 1], vp)) + b_ref[:, cs]
                ya_ref[rs, cs] = (u[rs, cs] * z).astype(BF16)

    return pl.pallas_call(
        body, name="sgu_fwd", grid=(T // tm,),
        in_specs=[BS((tm, A_W), lambda i: (i, 0)), BS((tm, A_W), lambda i: (i, 1)), BS((1, A_W), lambda i: (0, 0)),
                  BS((6, CHUNK, CHUNK), lambda i: (0, 0, 0)), BS((CHUNK, A_W), lambda i: (0, 0))],
        out_specs=BS((tm, A_W), lambda i: (i, 0)),
        out_shape=jax.ShapeDtypeStruct((T, A_W), BF16),
        compiler_params=_cp(("arbitrary",)),
    )(proj, proj, g_sgu, ws_tril, bs_full)


def _fox_fwd(proj, bq, bk, Bl, S):
    T = Bl * S
    nq = S // Q_BLK
    qc, kc, vc = 768 // LANES, 1152 // LANES, 1536 // LANES

    def body(q_ref, k_ref, v_ref, bq_ref, bk_ref, o_ref, lse_ref, ka_ref, va_ref):
        lane_s = _iota((S, LANES), 1)
        lane = _iota((Q_BLK, LANES), 1)
        tri = _iota((Q_BLK, Q_BLK), 1) <= _iota((Q_BLK, Q_BLK), 0)
        k = k_ref[...]
        v = v_ref[...]
        for hh in range(2):
            data = (lane_s < HEAD) if hh == 0 else (lane_s >= HEAD)
            ka_ref[hh] = jnp.where(data, k, bk_ref[0, hh])
            va_ref[hh] = jnp.where(lane_s == _bias_lane(hh), jnp.ones_like(v), v)
        for i in range(nq):
            r0 = i * Q_BLK
            q = q_ref[r0:r0 + Q_BLK, :]
            o_out = jnp.zeros((Q_BLK, LANES), F32)
            lse_out = jnp.zeros((Q_BLK, LANES), F32)
            for hh in range(2):
                hmask = (lane < HEAD) if hh == 0 else (lane >= HEAD)
                qa = jnp.where(hmask, q * 0.125, bq_ref[0, hh, r0:r0 + Q_BLK, :])
                sd = jnp.where(tri, _dot_nt(qa, ka_ref[hh, r0:r0 + Q_BLK, :]), NEG)
                m = jnp.max(sd, axis=1, keepdims=True)
                if i:
                    sf = _dot_nt(qa, ka_ref[hh, 0:r0, :])
                    m = jnp.maximum(m, jnp.max(sf, axis=1, keepdims=True))
                acc = _dot(jnp.exp(sd - m), va_ref[hh, r0:r0 + Q_BLK, :])
                if i:
                    acc = acc + _dot(jnp.exp(sf - m), va_ref[hh, 0:r0, :])
                l = jnp.sum(jnp.where(lane == _bias_lane(hh), acc, 0.0), axis=1, keepdims=True)
                o_out = jnp.where(hmask, acc / l, o_out)
                lse_out = jnp.where(hmask, m + jnp.log(l), lse_out)
            o_ref[r0:r0 + Q_BLK, :] = o_out.astype(BF16)
            lse_ref[0, r0:r0 + Q_BLK, :] = lse_out

    seq = lambda c0: BS((S, LANES), lambda b, p: (b, c0 + p))
    pair = BS((1, 2, S, LANES), lambda b, p: (b, p, 0, 0))
    return pl.pallas_call(
        body, name="fox_fwd", grid=(Bl, 3),
        in_specs=[seq(qc), seq(kc), seq(vc), pair, pair],
        out_specs=[seq(0), BS((1, S, LANES), lambda b, p: (p, b, 0))],
        out_shape=[jax.ShapeDtypeStruct((T, B_W), BF16), jax.ShapeDtypeStruct((3, T, LANES), F32)],
        scratch_shapes=[pltpu.VMEM((2, S, LANES), BF16), pltpu.VMEM((2, S, LANES), BF16)],
        compiler_params=_cp(("arbitrary", "arbitrary")),
    )(proj, proj, proj, bq, bk)


def _memkv_fwd(mem, g_mem, w_kv):
    Bl, Mt, D = mem.shape

    def body(m_ref, g_ref, w_ref, mn_ref, kv_ref):
        mn = _rms(m_ref[0], g_ref[...]).astype(BF16)
        mn_ref[0] = mn
        kv_ref[0] = jnp.dot(mn, w_ref[...], preferred_element_type=F32).astype(BF16)

    return pl.pallas_call(
        body, name="memkv_fwd", grid=(Bl,),
        in_specs=[BS((1, Mt, D), lambda b: (b, 0, 0)), BS((1, D), lambda b: (0, 0)), BS((D, 2 * M_W), lambda b: (0, 0))],
        out_specs=[BS((1, Mt, D), lambda b: (b, 0, 0)), BS((1, Mt, 2 * M_W), lambda b: (b, 0, 0))],
        out_shape=[jax.ShapeDtypeStruct((Bl, Mt, D), BF16), jax.ShapeDtypeStruct((Bl, Mt, 2 * M_W), BF16)],
        compiler_params=_cp(("arbitrary",)),
    )(mem, g_mem, w_kv)


def _memattn_fwd(proj, kv, Bl, S, tq):
    T = Bl * S
    nq = S // tq
    Mt = kv.shape[1]
    qc = 1920 // LANES

    def body(q_ref, km_ref, vm_ref, o_ref):
        lane = _iota((tq, LANES), 1)
        q = q_ref[...]
        out = jnp.zeros((tq, LANES), F32)
        for hh in range(2):
            hmask = (lane < HEAD) if hh == 0 else (lane >= HEAD)
            qs = jnp.where(hmask, q, jnp.zeros_like(q)) * 0.125
            s = _dot_nt(qs, km_ref[0])
            pe = jnp.exp(s - jnp.max(s, axis=1, keepdims=True))
            pn = pe / jnp.sum(pe, axis=1, keepdims=True)
            out = jnp.where(hmask, _dot(pn, vm_ref[0]), out)
        o_ref[...] = out.astype(BF16)

    return pl.pallas_call(
        body, name="memattn_fwd", grid=(Bl, 2, nq),
        in_specs=[BS((tq, LANES), lambda b, p, i: (b * nq + i, qc + p)),
                  BS((1, Mt, LANES), lambda b, p, i: (b, 0, p)),
                  BS((1, Mt, LANES), lambda b, p, i: (b, 0, 2 + p))],
        out_specs=BS((tq, LANES), lambda b, p, i: (b * nq + i, p)),
        out_shape=jax.ShapeDtypeStruct((T, M_W), BF16),
        compiler_params=_cp(("arbitrary", "arbitrary", "arbitrary")),
    )(proj, kv, kv)


def _mix_norms(ya, yb, ym, ga, gb, gm):
    return _rms(ya, ga), _rms(yb, gb), _rms(ym, gm)


def _outproj_fwd(ya, yb, ym, x2d, ga, gb, gm, g_post, g_pre2, w_out, tm):
    T, D = x2d.shape

    def body(ya_ref, yb_ref, ym_ref, x_ref, ga_ref, gb_ref, gm_ref, gp_ref, g2_ref, w_ref,
             y_ref, o_ref, x1_ref, h2_ref):
        na, nb_, nm = _mix_norms(ya_ref[...].astype(F32), yb_ref[...].astype(F32), ym_ref[...].astype(F32),
                                 ga_ref[...], gb_ref[...], gm_ref[...])
        y_ref[:, 0:A_W] = na.astype(BF16)
        y_ref[:, A_W:A_W + B_W] = nb_.astype(BF16)
        y_ref[:, A_W + B_W:] = nm.astype(BF16)
        o = jnp.dot(y_ref[...], w_ref[...], preferred_element_type=F32).astype(BF16)
        o_ref[...] = o
        x1 = x_ref[...] + _rms(o.astype(F32), gp_ref[...])
        x1_ref[...] = x1
        h2_ref[...] = _rms(x1, g2_ref[...]).astype(BF16)

    row = lambda w: BS((tm, w), lambda i: (i, 0))
    vec = lambda w: BS((1, w), lambda i: (0, 0))
    return pl.pallas_call(
        body, name="outproj_fwd", grid=(T // tm,),
        in_specs=[row(A_W), row(B_W), row(M_W), row(D), vec(A_W), vec(B_W), vec(M_W), vec(D), vec(D),
                  BS((A_W + B_W + M_W, D), lambda i: (0, 0))],
        out_specs=[row(A_W + B_W + M_W), row(D), row(D), row(D)],
        out_shape=[jax.ShapeDtypeStruct((T, A_W + B_W + M_W), BF16), jax.ShapeDtypeStruct((T, D), BF16),
                   jax.ShapeDtypeStruct((T, D), F32), jax.ShapeDtypeStruct((T, D), BF16)],
        compiler_params=_cp(("arbitrary",)),
    )(ya, yb, ym, x2d, ga, gb, gm, g_post, g_pre2, w_out)


def _ffn_fwd(h2, x1, target, wg, wu, wd, g_post, tm):
    T, D = x1.shape
    ns, F, _ = wg.shape

    def body(h_ref, x1_ref, t_ref, wg_ref, wu_ref, wd_ref, gp_ref,
             gs_ref, us_ref, dff_ref, dx2_ref, dgp_ref, loss_ref, acc_ref):
        j = pl.program_id(0)
        i = pl.program_id(1)
        rows = pl.ds(pl.multiple_of(i * tm, tm), tm)
        h = h_ref[...]
        g = _dot_nt(h, wg_ref[0])
        u = _dot_nt(h, wu_ref[0])
        gs_ref[0] = g.astype(BF16)
        us_ref[0] = u.astype(BF16)
        part = _dot(_silu_mul(g, u), wd_ref[0])

        @pl.when(j == 0)
        def _():
            acc_ref[rows, :] = part

        @pl.when(j != 0)
        def _():
            acc_ref[rows, :] += part

        @pl.when(j == ns - 1)
        def _():
            ff = acc_ref[rows, :]
            diff = x1_ref[...] + _rms(ff, gp_ref[...]) - t_ref[...]
            dx2 = diff * (1.0 / D)
            dff, dgp = _rms_bwd(ff, gp_ref[...], dx2)
            dx2_ref[...] = dx2
            dff_ref[...] = dff.astype(BF16)
            lpart = jnp.sum(_colsum(diff * diff), axis=1, keepdims=True) * (0.5 / D)
            _acc(dgp_ref, dgp, i == 0)
            _acc(loss_ref, jnp.broadcast_to(lpart, (1, LANES)), i == 0)

    last = lambda j, i: (jnp.where(j == ns - 1, i, 0), 0)
    wsh = BS((1, F, D), lambda j, i: (j, 0, 0))
    sh = BS((1, tm, F), lambda j, i: (j, i, 0))
    return pl.pallas_call(
        body, name="ffn_fwd", grid=(ns, T // tm),
        in_specs=[BS((tm, D), lambda j, i: (i, 0)), BS((tm, D), last), BS((tm, D), last), wsh, wsh, wsh,
                  BS((1, D), lambda j, i: (0, 0))],
        out_specs=[sh, sh, BS((tm, D), last), BS((tm, D), last),
                   BS((1, D), lambda j, i: (0, 0)), BS((1, LANES), lambda j, i: (0, 0))],
        out_shape=[jax.ShapeDtypeStruct((ns, T, F), BF16), jax.ShapeDtypeStruct((ns, T, F), BF16),
                   jax.ShapeDtypeStruct((T, D), BF16), jax.ShapeDtypeStruct((T, D), F32),
                   jax.ShapeDtypeStruct((1, D), F32), jax.ShapeDtypeStruct((1, LANES), F32)],
        scratch_shapes=[pltpu.VMEM((T, D), F32)],
        compiler_params=_cp(("arbitrary", "arbitrary")),
    )(h2, x1, target, wg, wu, wd, g_post)


def _ffn_bwd(dff, h2, gs, us, wg, wu, wd, tm):
    T, D = h2.shape
    ns, F, _ = wg.shape

    def body(dff_ref, h_ref, gs_ref, us_ref, wg_ref, wu_ref, wd_ref, dh_ref, dwg_out, dwu_out, dwd_out,
             dwg_ref, dwu_ref, dwd_ref):
        first = pl.program_id(1) == 0
        dff = dff_ref[...]
        h = h_ref[...]
        parts = []
        for r in range(ROW_SPLIT):
            rows = slice(r * (tm // ROW_SPLIT), (r + 1) * (tm // ROW_SPLIT))
            dact = _dot_nt(dff[rows], wd_ref[0])
            g = gs_ref[0, rows, :].astype(F32)
            u = us_ref[0, rows, :].astype(F32)
            sig = _sigmoid(g)
            gsig = g * sig
            dg = (dact * u * (sig + gsig * (1.0 - sig))).astype(BF16)
            du = (dact * gsig).astype(BF16)
            dh_ref[0, rows, :] = (_dot(dg, wg_ref[0]) + _dot(du, wu_ref[0])).astype(BF16)
            parts.append(((gsig * u).astype(BF16), dg, du))
        a, dg, du = [jnp.concatenate(p, axis=0) for p in zip(*parts)]
        _acc(dwd_ref, _dot_tn(a, dff), first)
        _acc(dwg_ref, _dot_tn(dg, h), first)
        _acc(dwu_ref, _dot_tn(du, h), first)

        @pl.when(pl.program_id(1) == pl.num_programs(1) - 1)
        def _():
            dwg_out[0] = dwg_ref[...].astype(BF16)
            dwu_out[0] = dwu_ref[...].astype(BF16)
            dwd_out[0] = dwd_ref[...].astype(BF16)

    row = BS((tm, D), lambda j, i: (i, 0))
    sh = BS((1, tm, F), lambda j, i: (j, i, 0))
    wsh = BS((1, F, D), lambda j, i: (j, 0, 0))
    return pl.pallas_call(
        body, name="ffn_bwd", grid=(ns, T // tm),
        in_specs=[row, row, sh, sh, wsh, wsh, wsh],
        out_specs=[BS((1, tm, D), lambda j, i: (j, i, 0)), wsh, wsh, wsh],
        out_shape=[jax.ShapeDtypeStruct((ns, T, D), BF16)] + [jax.ShapeDtypeStruct((ns, F, D), BF16)] * 3,
        scratch_shapes=[pltpu.VMEM((F, D), F32)] * 3,
        compiler_params=_cp(("arbitrary", "arbitrary")),
    )(dff, h2, gs, us, wg, wu, wd)


DPROJ_PIECES = ((0, A_W), (A_W, A_W), (768, B_W), (1152, B_W), (1536, B_W), (1920, LANES), (2048, M_W))


def _put_dproj(dp_ref, piece_refs):
    for (c0, w), ref in zip(DPROJ_PIECES, piece_refs):
        dp_ref[:, c0:c0 + w] = ref[...].astype(BF16)


def _dw_in(pieces, h, ns, tk):
    T, D = h.shape
    M = P_COLS
    dsh = D // ns
    tk = min(tk, T)

    def body(*refs):
        piece_refs, h_ref, o_ref, acc_ref, dp_ref = refs[:7], refs[7], refs[8], refs[9], refs[10]
        t = pl.program_id(0)
        _put_dproj(dp_ref, piece_refs)
        _acc(acc_ref, _dot_tn(h_ref[...], dp_ref[...]), t == 0)

        @pl.when(t == pl.num_programs(0) - 1)
        def _():
            for s in range(ns):
                o_ref[s] = acc_ref[s * dsh:(s + 1) * dsh, :].T.astype(BF16)

    return pl.pallas_call(
        body, name="dw_in", grid=(T // tk,),
        in_specs=[BS((tk, w), lambda t: (t, 0)) for _, w in DPROJ_PIECES] + [BS((tk, D), lambda t: (t, 0))],
        out_specs=BS((ns, M, dsh), lambda t: (0, 0, 0)),
        out_shape=jax.ShapeDtypeStruct((ns, M, dsh), BF16),
        scratch_shapes=[pltpu.VMEM((D, M), F32), pltpu.VMEM((tk, M), BF16)],
        compiler_params=_cp(("arbitrary",)),
    )(*pieces, h)


def _outproj_bwd(dh2, x1, dx2, o, y, ya, yb, ym, ga, gb, gm, g_post, g_pre2, w_out, tm):
    T, D = x1.shape
    ns = dh2.shape[0]

    def body(dh_ref, x1_ref, dx2_ref, o_ref, y_ref, ya_ref, yb_ref, ym_ref, ga_ref, gb_ref, gm_ref, gp_ref, g2_ref, w_ref,
             dx1_ref, dw_ref, dya_ref, dyb_ref, dym_ref, dga_ref, dgb_ref, dgm_ref, dgp_ref, dg2_ref):
        first = pl.program_id(0) == 0
        dh = dh_ref[0].astype(F32)
        for j in range(1, ns):
            dh = dh + dh_ref[j].astype(F32)
        dxa, dg2 = _rms_bwd(x1_ref[...], g2_ref[...], dh)
        dx1 = dx2_ref[...] + dxa
        dx1_ref[...] = dx1
        _acc(dg2_ref, dg2, first)
        do, dgp = _rms_bwd(o_ref[...].astype(F32), gp_ref[...], dx1)
        do = do.astype(BF16)
        _acc(dw_ref, _dot_tn(y_ref[...], do), first)
        dy = _dot_nt(do, w_ref[...])
        dya, dga = _rms_bwd(ya_ref[...].astype(F32), ga_ref[...], dy[:, 0:A_W])
        dyb, dgb = _rms_bwd(yb_ref[...].astype(F32), gb_ref[...], dy[:, A_W:A_W + B_W])
        dym, dgm = _rms_bwd(ym_ref[...].astype(F32), gm_ref[...], dy[:, A_W + B_W:])
        dya_ref[...] = dya.astype(BF16)
        dyb_ref[...] = dyb.astype(BF16)
        dym_ref[...] = dym.astype(BF16)
        _acc(dga_ref, dga, first)
        _acc(dgb_ref, dgb, first)
        _acc(dgm_ref, dgm, first)
        _acc(dgp_ref, dgp, first)

    row = lambda w: BS((tm, w), lambda i: (i, 0))
    vec = lambda w: BS((1, w), lambda i: (0, 0))
    sds = jax.ShapeDtypeStruct
    return pl.pallas_call(
        body, name="outproj_bwd", grid=(T // tm,),
        in_specs=[BS((ns, tm, D), lambda i: (0, i, 0)), row(D), row(D), row(D), row(A_W + B_W + M_W), row(A_W), row(B_W),
                  row(M_W), vec(A_W), vec(B_W), vec(M_W), vec(D), vec(D), BS((A_W + B_W + M_W, D), lambda i: (0, 0))],
        out_specs=[row(D), BS((A_W + B_W + M_W, D), lambda i: (0, 0)), row(A_W), row(B_W), row(M_W),
                   vec(A_W), vec(B_W), vec(M_W), vec(D), vec(D)],
        out_shape=[sds((T, D), F32), sds((A_W + B_W + M_W, D), F32), sds((T, A_W), BF16), sds((T, B_W), BF16),
                   sds((T, M_W), BF16), sds((1, A_W), F32), sds((1, B_W), F32), sds((1, M_W), F32), sds((1, D), F32),
                   sds((1, D), F32)],
        compiler_params=_cp(("arbitrary",)),
    )(dh2, x1, dx2, o, y, ya, yb, ym, ga, gb, gm, g_post, g_pre2, w_out)


def _sgu_bwd(proj, dya, g_sgu, ws_tril, bs_full, tm):
    T = proj.shape[0]
    nch = tm // CHUNK

    def body(zu_ref, zv_ref, dy_ref, g_ref, ws_ref, b_ref, dzu_ref, dzv_ref, dws_ref, dbs_ref, dg_ref,
             du_ref, dvn_ref, dbf_ref):
        step = pl.program_id(0)
        first = step == 0
        lane = _iota((CHUNK, LANES), 1)
        tril = _iota((CHUNK, CHUNK), 0) >= _iota((CHUNK, CHUNK), 1)
        (u, vn), vjp = jax.vjp(_sgu_pre, zu_ref[...].astype(F32), zv_ref[...].astype(F32), g_ref[...])
        vnb = vn.astype(BF16)
        dy = dy_ref[...].astype(F32)

        @pl.when(first)
        def _():
            dws_ref[...] = jnp.zeros_like(dws_ref)
            dbf_ref[...] = jnp.zeros_like(dbf_ref)

        for c in range(nch):
            rs = slice(c * CHUNK, (c + 1) * CHUNK)
            for j in range(3):
                cs = slice(j * LANES, (j + 1) * LANES)
                vp = vnb[rs, cs]
                z = jnp.where(lane < HEAD, _dot(ws_ref[2 * j], vp), _dot(ws_ref[2 * j + 1], vp)) + b_ref[:, cs]
                du_ref[rs, cs] = dy[rs, cs] * z
                dz = dy[rs, cs] * u[rs, cs]
                dbf_ref[:, cs] += dz
                dzb = dz.astype(BF16)
                dz0 = jnp.where(lane < HEAD, dzb, jnp.zeros_like(dzb))
                dz1 = jnp.where(lane >= HEAD, dzb, jnp.zeros_like(dzb))
                dvn_ref[rs, cs] = jnp.where(lane < HEAD, _dot_tn(ws_ref[2 * j], dzb), _dot_tn(ws_ref[2 * j + 1], dzb))
                dws_ref[2 * j] += jnp.where(tril, _dot_nt(dz0, vp), 0.0)
                dws_ref[2 * j + 1] += jnp.where(tril, _dot_nt(dz1, vp), 0.0)
        dzu, dzv, dg = vjp((du_ref[...], dvn_ref[...]))
        dzu_ref[...] = dzu.astype(BF16)
        dzv_ref[...] = dzv.astype(BF16)
        _acc(dg_ref, dg, first)

        @pl.when(step == pl.num_programs(0) - 1)
        def _():
            out = jnp.zeros((CHUNK, LANES), F32)
            for j in range(3):
                slab = dbf_ref[:, j * LANES:(j + 1) * LANES]
                lo = jnp.sum(jnp.where(lane < HEAD, slab, 0.0), axis=1, keepdims=True)
                hi = jnp.sum(jnp.where(lane >= HEAD, slab, 0.0), axis=1, keepdims=True)
                out = out + jnp.where(lane == 2 * j, lo, 0.0) + jnp.where(lane == 2 * j + 1, hi, 0.0)
            dbs_ref[...] = out

    return pl.pallas_call(
        body, name="sgu_bwd", grid=(T // tm,),
        in_specs=[BS((tm, A_W), lambda i: (i, 0)), BS((tm, A_W), lambda i: (i, 1)), BS((tm, A_W), lambda i: (i, 0)),
                  BS((1, A_W), lambda i: (0, 0)), BS((6, CHUNK, CHUNK), lambda i: (0, 0, 0)),
                  BS((CHUNK, A_W), lambda i: (0, 0))],
        out_specs=[BS((tm, A_W), lambda i: (i, 0)), BS((tm, A_W), lambda i: (i, 0)),
                   BS((6, CHUNK, CHUNK), lambda i: (0, 0, 0)), BS((CHUNK, LANES), lambda i: (0, 0)),
                   BS((1, A_W), lambda i: (0, 0))],
        out_shape=[jax.ShapeDtypeStruct((T, A_W), BF16), jax.ShapeDtypeStruct((T, A_W), BF16),
                   jax.ShapeDtypeStruct((6, CHUNK, CHUNK), F32), jax.ShapeDtypeStruct((CHUNK, LANES), F32),
                   jax.ShapeDtypeStruct((1, A_W), F32)],
        scratch_shapes=[pltpu.VMEM((tm, A_W), F32), pltpu.VMEM((tm, A_W), F32), pltpu.VMEM((CHUNK, A_W), F32)],
        compiler_params=_cp(("arbitrary",)),
    )(proj, proj, dya, g_sgu, ws_tril, bs_full)


def _memattn_bwd(proj, kv, dym, Bl, S, tq):
    T = Bl * S
    nq = S // tq
    Mt = kv.shape[1]
    qc = 1920 // LANES

    def body(q_ref, km_ref, vm_ref, do_ref, dq_ref, dkm_ref, dvm_ref):
        first = pl.program_id(2) == 0
        lane = _iota((tq, LANES), 1)
        q = q_ref[...]
        do = do_ref[...]
        dq_out = jnp.zeros((tq, LANES), F32)
        dkm = jnp.zeros((Mt, LANES), F32)
        dvm = jnp.zeros((Mt, LANES), F32)
        for hh in range(2):
            hmask = (lane < HEAD) if hh == 0 else (lane >= HEAD)
            qs = jnp.where(hmask, q, jnp.zeros_like(q)) * 0.125
            dom = jnp.where(hmask, do, 0.0).astype(BF16)
            s = _dot_nt(qs, km_ref[0])
            pe = jnp.exp(s - jnp.max(s, axis=1, keepdims=True))
            pn = pe / jnp.sum(pe, axis=1, keepdims=True)
            dp = _dot_nt(dom, vm_ref[0])
            ds = (pn * (dp - jnp.sum(pn * dp, axis=1, keepdims=True))).astype(BF16)
            dq_out = jnp.where(hmask, _dot(ds, km_ref[0]) * 0.125, dq_out)
            dkm = dkm + _dot_tn(ds, qs)
            dvm = dvm + _dot_tn(pn, dom)
        dq_ref[...] = dq_out.astype(BF16)
        _acc(dkm_ref, dkm[None], first)
        _acc(dvm_ref, dvm[None], first)

    return pl.pallas_call(
        body, name="memattn_bwd", grid=(Bl, 2, nq),
        in_specs=[BS((tq, LANES), lambda b, p, i: (b * nq + i, qc + p)),
                  BS((1, Mt, LANES), lambda b, p, i: (b, 0, p)),
                  BS((1, Mt, LANES), lambda b, p, i: (b, 0, 2 + p)),
                  BS((tq, LANES), lambda b, p, i: (b * nq + i, p))],
        out_specs=[BS((tq, LANES), lambda b, p, i: (b * nq + i, p)),
                   BS((1, Mt, LANES), lambda b, p, i: (b, 0, p)),
                   BS((1, Mt, LANES), lambda b, p, i: (b, 0, p))],
        out_shape=[jax.ShapeDtypeStruct((T, M_W), BF16), jax.ShapeDtypeStruct((Bl, Mt, M_W), F32),
                   jax.ShapeDtypeStruct((Bl, Mt, M_W), F32)],
        compiler_params=_cp(("arbitrary", "arbitrary", "arbitrary")),
    )(proj, kv, kv, dym)


def _memkv_bwd(dkm, dvm, memn, mem, g_mem, w_kv):
    Bl, Mt, D = mem.shape

    def body(dk_ref, dv_ref, mn_ref, m_ref, g_ref, w_ref, dw_ref, dg_ref):
        first = pl.program_id(0) == 0
        dk = dk_ref[0].astype(BF16)
        dv = dv_ref[0].astype(BF16)
        mn = mn_ref[0]
        dmn = _dot_nt(dk, w_ref[:, 0:M_W]) + _dot_nt(dv, w_ref[:, M_W:])
        _, dg = _rms_bwd(m_ref[0], g_ref[...], dmn)
        _acc(dg_ref, dg, first)

        @pl.when(first)
        def _():
            dw_ref[...] = jnp.zeros_like(dw_ref)

        dw_ref[:, 0:M_W] += _dot_tn(mn, dk)
        dw_ref[:, M_W:] += _dot_tn(mn, dv)

    return pl.pallas_call(
        body, name="memkv_bwd", grid=(Bl,),
        in_specs=[BS((1, Mt, M_W), lambda b: (b, 0, 0)), BS((1, Mt, M_W), lambda b: (b, 0, 0)),
                  BS((1, Mt, D), lambda b: (b, 0, 0)), BS((1, Mt, D), lambda b: (b, 0, 0)),
                  BS((1, D), lambda b: (0, 0)), BS((D, 2 * M_W), lambda b: (0, 0))],
        out_specs=[BS((D, 2 * M_W), lambda b: (0, 0)), BS((1, D), lambda b: (0, 0))],
        out_shape=[jax.ShapeDtypeStruct((D, 2 * M_W), F32), jax.ShapeDtypeStruct((1, D), F32)],
        compiler_params=_cp(("arbitrary",)),
    )(dkm, dvm, memn, mem, g_mem, w_kv)


def _fox_bwd(proj, dyb, lse, bq, bk, Bl, S):
    T = Bl * S
    nq = S // Q_BLK
    nb = S // LANES
    qc, kc, vc = 768 // LANES, 1152 // LANES, 1536 // LANES

    def body(q_ref, k_ref, v_ref, do_ref, lse_ref, bq_ref, bk_ref,
             dq_ref, dk_ref, dv_ref, dcr_ref, ka_ref, dka_ref, dva_ref):
        p = pl.program_id(1)
        lane_s = _iota((S, LANES), 1)
        lane = _iota((Q_BLK, LANES), 1)
        sub = _iota((8, LANES), 0)
        tri = _iota((Q_BLK, Q_BLK), 1) <= _iota((Q_BLK, Q_BLK), 0)
        k = k_ref[...]
        for hh in range(2):
            data = (lane_s < HEAD) if hh == 0 else (lane_s >= HEAD)
            ka_ref[hh] = jnp.where(data, k, bk_ref[0, hh])
        dka_ref[...] = jnp.zeros_like(dka_ref)
        dva_ref[...] = jnp.zeros_like(dva_ref)

        @pl.when(p == 0)
        def _():
            dcr_ref[...] = jnp.zeros_like(dcr_ref)

        def add_colsums(ds, first_blk, h):
            cs = _colsum(ds)
            for jb in range(ds.shape[1] // LANES):
                dcr_ref[0, first_blk + jb] += jnp.where(sub == h, cs[:, jb * LANES:(jb + 1) * LANES], 0.0)

        for i in range(nq):
            r0 = i * Q_BLK
            r1 = r0 + Q_BLK
            q = q_ref[r0:r1, :]
            do = do_ref[r0:r1, :]
            lse_b = lse_ref[0, r0:r1, :]
            dq_out = jnp.zeros((Q_BLK, LANES), F32)
            for hh in range(2):
                hmask = (lane < HEAD) if hh == 0 else (lane >= HEAD)
                h = 2 * p + hh
                qs = jnp.where(hmask, q * 0.125, jnp.zeros_like(q))
                qa = jnp.where(hmask, q * 0.125, bq_ref[0, hh, r0:r1, :])
                dob = jnp.where(hmask, do, 0.0).astype(BF16)
                lse_h = jnp.sum(jnp.where(lane == hh * HEAD, lse_b, 0.0), axis=1, keepdims=True)
                pd = jnp.where(tri, jnp.exp(_dot_nt(qa, ka_ref[hh, r0:r1, :]) - lse_h), 0.0)
                dpd = _dot_nt(dob, v_ref[r0:r1, :])
                delta = jnp.sum(pd * dpd, axis=1, keepdims=True)
                psum = jnp.sum(pd, axis=1, keepdims=True)
                if i:
                    pf = jnp.exp(_dot_nt(qa, ka_ref[hh, 0:r0, :]) - lse_h)
                    dpf = _dot_nt(dob, v_ref[0:r0, :])
                    delta = delta + jnp.sum(pf * dpf, axis=1, keepdims=True)
                    psum = psum + jnp.sum(pf, axis=1, keepdims=True)
                delta = delta / psum
                dsd = pd * (dpd - delta)
                add_colsums(dsd, r0 // LANES, h)
                dsd = dsd.astype(BF16)
                dq_h = _dot(dsd, k_ref[r0:r1, :])
                dka_ref[r0:r1, :] += _dot_tn(dsd, qs)
                dva_ref[r0:r1, :] += _dot_tn(pd, dob)
                if i:
                    dsf = pf * (dpf - delta)
                    add_colsums(dsf, 0, h)
                    dsf = dsf.astype(BF16)
                    dq_h = dq_h + _dot(dsf, k_ref[0:r0, :])
                    dka_ref[0:r0, :] += _dot_tn(dsf, qs)
                    dva_ref[0:r0, :] += _dot_tn(pf, dob)
                dq_out = jnp.where(hmask, dq_h * 0.125, dq_out)
            dq_ref[r0:r1, :] = dq_out.astype(BF16)
        dk_ref[...] = dka_ref[...].astype(BF16)
        dv_ref[...] = dva_ref[...].astype(BF16)

    seq = lambda c0: BS((S, LANES), lambda b, p: (b, c0 + p))
    pair = BS((1, 2, S, LANES), lambda b, p: (b, p, 0, 0))
    rowblk = BS((1, nb, 8, LANES), lambda b, p: (b, 0, 0, 0))
    return pl.pallas_call(
        body, name="fox_bwd", grid=(Bl, 3),
        in_specs=[seq(qc), seq(kc), seq(vc), seq(0), BS((1, S, LANES), lambda b, p: (p, b, 0)), pair, pair],
        out_specs=[seq(0), seq(0), seq(0), rowblk],
        out_shape=[jax.ShapeDtypeStruct((T, B_W), BF16)] * 3 + [jax.ShapeDtypeStruct((Bl, nb, 8, LANES), F32)],
        scratch_shapes=[pltpu.VMEM((2, S, LANES), BF16), pltpu.VMEM((S, LANES), F32), pltpu.VMEM((S, LANES), F32)],
        compiler_params=_cp(("arbitrary", "arbitrary")),
    )(proj, proj, proj, dyb, lse, bq, bk)


def _gate_bwd(dc_row, fl_row):
    Bl, nb, _, _ = dc_row.shape

    def body(dc_ref, fl_ref, o_ref):
        lane = _iota((8, LANES), 1)

        carry = jnp.zeros((8, 1), F32)
        for j in reversed(range(nb)):
            r = -dc_ref[0, j]
            for k in (1, 2, 4, 8, 16, 32, 64):
                r = r + jnp.where(lane < LANES - k, pltpu.roll(r, LANES - k, 1), 0.0)
            total = jnp.sum(jnp.where(lane == 0, r, 0.0), axis=1, keepdims=True)
            dfl = (r + carry) * _sigmoid(-fl_ref[0, j])
            carry = carry + total
            o_ref[0, j * LANES:(j + 1) * LANES, :] = jnp.concatenate(
                [dfl, jnp.zeros((LANES - 8, LANES), F32)], axis=0).T

    rowblk = BS((1, nb, 8, LANES), lambda b: (b, 0, 0, 0))
    return pl.pallas_call(
        body, name="gate_bwd", grid=(Bl,),
        in_specs=[rowblk, rowblk],
        out_specs=BS((1, nb * LANES, LANES), lambda b: (b, 0, 0)),
        out_shape=jax.ShapeDtypeStruct((Bl, nb * LANES, LANES), F32),
        compiler_params=_cp(("arbitrary",)),
    )(dc_row, fl_row)


def _inproj_bwd(pieces, x2d, dx1, g_pre, w_in_p, tm):
    T, D = x2d.shape
    ns, _, dsh = w_in_p.shape

    def body(*refs):
        piece_refs = refs[:7]
        x_ref, dx1_ref, g_ref, w_ref, gx_ref, dg_ref, dbf_ref, dp_ref = refs[7:]
        first = pl.program_id(0) == 0
        _put_dproj(dp_ref, piece_refs)
        dh = jnp.concatenate([_dot(dp_ref[...], w_ref[s]) for s in range(ns)], axis=1)
        dxa, dg = _rms_bwd(x_ref[...], g_ref[...], dh)
        gx_ref[...] = dx1_ref[...] + dxa
        _acc(dg_ref, dg, first)
        _acc(dbf_ref, _colsum(piece_refs[5][...]), first)

    row = lambda w: BS((tm, w), lambda i: (i, 0))
    return pl.pallas_call(
        body, name="inproj_bwd", grid=(T // tm,),
        in_specs=[row(w) for _, w in DPROJ_PIECES] + [row(D), row(D), BS((1, D), lambda i: (0, 0)),
                                                      BS((ns, P_COLS, dsh), lambda i: (0, 0, 0))],
        out_specs=[row(D), BS((1, D), lambda i: (0, 0)), BS((1, LANES), lambda i: (0, 0))],
        out_shape=[jax.ShapeDtypeStruct((T, D), F32), jax.ShapeDtypeStruct((1, D), F32),
                   jax.ShapeDtypeStruct((1, LANES), F32)],
        scratch_shapes=[pltpu.VMEM((tm, P_COLS), BF16)],
        compiler_params=_cp(("arbitrary",)),
    )(*pieces, x2d, dx1, g_pre, w_in_p)


def _local_step(x, mem, target, W, P, reduce=None):
    Bl, S, D = x.shape
    T = Bl * S
    tm = min(512, T)
    x2d = x.reshape(T, D)
    t2d = target.reshape(T, D)
    vec = lambda a: a.reshape(1, -1)
    bf_row = jnp.pad(P["b_f"].reshape(1, -1), ((0, 0), (0, LANES - N_FOX_HEADS)))
    tril = jnp.tril(jnp.ones((CHUNK, CHUNK), bool))
    ws_tril = jnp.where(tril[None], P["w_s"][0], 0.0).astype(BF16)
    bs_full = jnp.repeat(P["b_s"][0].T, HEAD, axis=1)
    g_pre, g_sgu = vec(P["g_pre_mix"]), vec(P["g_sgu"])
    ga, gb, gm = vec(P["g_out_a"]), vec(P["g_out_b"]), vec(P["g_out_m"])
    g_mem, g_post, g_pre2, g_post2 = vec(P["g_mem"]), vec(P["g_post_mix"]), vec(P["g_pre_ffn"]), vec(P["g_post_ffn"])

    h, proj, flog = _inproj_fwd(x2d, g_pre, W["w_in"], tm)
    bq, bk, fl_row = _gate_fwd(flog.reshape(Bl, S, LANES), bf_row)
    ya = _sgu_fwd(proj, g_sgu, ws_tril, bs_full, tm)
    yb, lse = _fox_fwd(proj, bq, bk, Bl, S)
    memn, kv = _memkv_fwd(mem, g_mem, W["w_mem_kv"])
    ym = _memattn_fwd(proj, kv, Bl, S, min(2048, S))
    y, o, x1, h2 = _outproj_fwd(ya, yb, ym, x2d, ga, gb, gm, g_post, g_pre2, W["w_out"], tm)
    gs, us, dff, dx2, dg_post2, loss = _ffn_fwd(h2, x1, t2d, W["w_gate"], W["w_up"], W["w_down"], g_post2, tm)

    dh2, d_w_gate, d_w_up, d_w_down = _ffn_bwd(dff, h2, gs, us, W["w_gate"], W["w_up"], W["w_down"], min(1024, T))
    ffn = [d_w_gate, d_w_up, d_w_down]
    if reduce is not None:
        pending, _ = reduce.begin("ffn", ffn)
    dx1, d_w_out, dya, dyb, dym, dga, dgb, dgm, dg_post, dg_pre2 = _outproj_bwd(
        dh2, x1, dx2, o, y, ya, yb, ym, ga, gb, gm, g_post, g_pre2, W["w_out"], tm)
    if reduce is not None:
        ffn, (dya, dyb, dym) = reduce.finish("ffn", pending, (dya, dyb, dym))
    dzu, dzv, dws, dbs_cols, dg_sgu = _sgu_bwd(proj, dya, g_sgu, ws_tril, bs_full, tm)
    dqm, dkm, dvm = _memattn_bwd(proj, kv, dym, Bl, S, min(2048, S))
    d_w_kv, dg_mem = _memkv_bwd(dkm, dvm, memn, mem, g_mem, W["w_mem_kv"])
    mid = [d_w_kv, d_w_out]
    dq, dk, dv, dc_row = _fox_bwd(proj, dyb, lse, bq, bk, Bl, S)
    if reduce is not None:
        done = reduce.apply(BIG[3:], ffn)
        pending, after = reduce.begin("mid", mid, (dc_row,) + done)
        dc_row = after[0]
    dfl = _gate_bwd(dc_row, fl_row).reshape(T, LANES)
    if reduce is not None:
        mid, (dfl,) = reduce.finish("mid", pending, (dfl,), first=(dzu, dzv))
    pieces = (dzu, dzv, dq, dk, dv, dfl, dqm)
    d_w_in = _dw_in(pieces, h, W["w_in"].shape[0], 1024)
    if reduce is None:
        big = dict(zip(BIG, [d_w_in] + mid + ffn))
    else:
        done = reduce.apply(BIG[1:3], mid)
        pending, after = reduce.begin("in", [d_w_in], done + (dx1,))
        dx1 = after[-1]
        big = {"w_in": pending}
    grad_x, dg_pre, dbf = _inproj_bwd(pieces, x2d, dx1, g_pre, W["w_in"], tm)
    small = {"g_pre_mix": dg_pre, "b_f": dbf[:, :N_FOX_HEADS], "g_sgu": dg_sgu, "w_s": dws, "b_s": dbs_cols[:, :N_FOX_HEADS].T,
             "g_out_a": dga, "g_out_b": dgb, "g_out_m": dgm, "g_mem": dg_mem, "g_post_mix": dg_post,
             "g_pre_ffn": dg_pre2, "g_post_ffn": dg_post2, "loss": loss[:, :1]}
    return grad_x.reshape(Bl, S, D), big, small


def _place():
    return lax.axis_index("x"), lax.axis_index("y"), lax.axis_index("c")


def _exchange_on_sequencer(srcs, own_full, name, collective_id):
    n = len(srcs)

    def body(*refs):
        src, dst = refs[:n], refs[n:2 * n]
        lsem, isend, irecv, dsend, drecv = refs[2 * n:]
        x, y, c = _place()
        oc = 1 - c
        s_me = 2 * x + y
        sib = (x, y, oc)
        chips = [(1 - x, y), (x, 1 - y), (1 - x, 1 - y)]
        barrier = pltpu.get_barrier_semaphore()
        for dev in [(cx, cy, c) for cx, cy in chips] + [sib]:
            pl.semaphore_signal(barrier, inc=1, device_id=dev, device_id_type=MESH)
        pl.semaphore_wait(barrier, 4)

        def remote(a, b, ssem, rsem, dev):
            return pltpu.make_async_remote_copy(src_ref=a, dst_ref=b, send_sem=ssem, recv_sem=rsem,
                                                device_id=dev, device_id_type=MESH)

        sends, local = [], []
        for w in range(n):
            for j, (cx, cy) in enumerate(chips):
                half = src[w].at[c] if own_full else src[w].at[2 * cx + cy]
                cp = remote(half, dst[w].at[s_me, c], isend.at[w, j], irecv.at[w, j], (cx, cy, c))
                cp.start()
                sends.append(cp)
            if own_full:
                cp = remote(src[w], dst[w].at[s_me], dsend.at[w, 3], drecv.at[w, 3], sib)
            else:
                cp = remote(src[w].at[s_me], dst[w].at[s_me, c], dsend.at[w, 3], drecv.at[w, 3], sib)
                loc = pltpu.make_async_copy(src[w].at[s_me], dst[w].at[s_me, c], lsem.at[w])
                loc.start()
                local.append(loc)
            cp.start()
            sends.append(cp)
        for w in range(n):
            for j, (cx, cy) in enumerate(chips):
                landed = dst[w].at[2 * cx + cy, c]
                remote(landed, landed, isend.at[w, j], irecv.at[w, j], (cx, cy, c)).wait_recv()
                cp = remote(landed, landed, dsend.at[w, j], drecv.at[w, j], sib)
                cp.start()
                sends.append(cp)
        for w in range(n):
            for j, (cx, cy) in enumerate(chips):
                landed = dst[w].at[2 * cx + cy, oc]
                remote(landed, landed, dsend.at[w, j], drecv.at[w, j], sib).wait_recv()
            landed = dst[w].at[s_me] if own_full else dst[w].at[s_me, oc]
            remote(landed, landed, dsend.at[w, 3], drecv.at[w, 3], sib).wait_recv()
        for cp in sends:
            cp.wait_send()
        for loc in local:
            loc.wait()

    return pl.kernel(
        body, out_type=[jax.ShapeDtypeStruct((4, 2) + s.shape[1:], s.dtype) for s in srcs],
        mesh=plsc.ScalarSubcoreMesh(axis_name="sequencer", num_cores=1), name=name,
        scratch_types=[pltpu.SemaphoreType.DMA((n,)), pltpu.SemaphoreType.DMA((n, 3)), pltpu.SemaphoreType.DMA((n, 3)),
                       pltpu.SemaphoreType.DMA((n, 4)), pltpu.SemaphoreType.DMA((n, 4))],
        compiler_params=pltpu.CompilerParams(collective_id=collective_id),
    )(*srcs)


def _sibling_swap(grads, name, collective_id):
    n = len(grads)

    def body(*refs):
        g, theirs = refs[:n], refs[n:2 * n]
        ssem, rsem = refs[2 * n:]
        x, y, c = _place()
        sib = (x, y, 1 - c)
        barrier = pltpu.get_barrier_semaphore()
        pl.semaphore_signal(barrier, inc=1, device_id=sib, device_id_type=MESH)
        pl.semaphore_wait(barrier, 1)
        cps = []
        for w in range(n):
            cp = pltpu.make_async_remote_copy(src_ref=g[w].at[:, 1 - c], dst_ref=theirs[w], send_sem=ssem.at[w],
                                              recv_sem=rsem.at[w], device_id=sib, device_id_type=MESH)
            cp.start()
            cps.append(cp)
        for cp in cps:
            cp.wait()

    return pl.kernel(
        body, out_type=[jax.ShapeDtypeStruct((4,) + g.shape[2:], g.dtype) for g in grads],
        mesh=plsc.ScalarSubcoreMesh(axis_name="sequencer", num_cores=1), name=name,
        scratch_types=[pltpu.SemaphoreType.DMA((n,)), pltpu.SemaphoreType.DMA((n,))],
        compiler_params=pltpu.CompilerParams(collective_id=collective_id),
    )(*grads)


def _add_pair(core, g, theirs, name):
    _, _, hr, C = g.shape

    def body(core_ref, g_ref, t_ref, o_ref):
        o_ref[0] = (g_ref[0, 0].astype(F32) + t_ref[0].astype(F32)).astype(BF16)

    blk = BS((1, hr, C), lambda s, core_ref: (s, 0, 0))
    return pl.pallas_call(
        body, name=name,
        grid_spec=pltpu.PrefetchScalarGridSpec(
            num_scalar_prefetch=1, grid=(4,),
            in_specs=[BS((1, 1, hr, C), lambda s, core_ref: (s, core_ref[0], 0, 0)), blk], out_specs=blk),
        out_shape=jax.ShapeDtypeStruct(theirs.shape, BF16), compiler_params=_cp(("arbitrary",)))(core, g, theirs)


def _sum_chips(r, name):
    _, _, hr, C = r.shape

    def body(r_ref, o_ref):
        o_ref[...] = ((r_ref[0, 0].astype(F32) + r_ref[1, 0].astype(F32)) + r_ref[2, 0].astype(F32)) + r_ref[3, 0].astype(F32)

    return pl.pallas_call(body, name=name, grid=(2,), in_specs=[BS((4, 1, hr, C), lambda h: (0, h, 0, 0))],
                          out_specs=BS((hr, C), lambda h: (h, 0)), out_shape=jax.ShapeDtypeStruct((2 * hr, C), F32),
                          compiler_params=_cp(("arbitrary",)))(r)


class _Reducer:
    IDS = {"ffn": (4, 5), "mid": (6, 7), "in": (8, 9)}

    def __init__(self, core, apply):
        self.core = core
        self.apply = apply

    def begin(self, tag, grads, after=()):
        grads, after = lax.optimization_barrier((list(grads), after))
        g4 = [g.reshape(4, 2, -1, g.shape[-1]) for g in grads]
        return (g4, _sibling_swap(g4, "swap_" + tag, self.IDS[tag][0])), after

    def finish(self, tag, pending, hold, first=()):
        pending, first = lax.optimization_barrier((pending, first))
        g4, theirs = pending
        sums = [_add_pair(self.core, g, t, "chip_sum_%s_%d" % (tag, k)) for k, (g, t) in enumerate(zip(g4, theirs))]
        sums, hold = lax.optimization_barrier((sums, hold))
        return _exchange_on_sequencer(sums, False, "scatter_" + tag, self.IDS[tag][1]), hold


def _small_allreduce(part):
    R = part.shape[0]
    rs = R // 8
    masks = [(mx, my, mc) for mx in (0, 1) for my in (0, 1) for mc in (0, 1)][1:]

    def body(p_ref, o_ref, buf_ref, s1, r1, s2, r2):
        x, y, c = _place()
        d = 4 * x + 2 * y + c
        mine = pl.ds(pl.multiple_of(d * rs, 8), rs)
        peers = [((x + mx) % 2, (y + my) % 2, (c + mc) % 2) for mx, my, mc in masks]
        first, second = [], []
        for k, (px, py, pc) in enumerate(peers):
            theirs = pl.ds(pl.multiple_of((4 * px + 2 * py + pc) * rs, 8), rs)
            cp = pltpu.make_async_remote_copy(src_ref=p_ref.at[theirs, :], dst_ref=buf_ref.at[d], send_sem=s1.at[k],
                                              recv_sem=r1.at[k], device_id=(px, py, pc), device_id_type=MESH)
            cp.start()
            first.append(cp)
        buf_ref[d] = p_ref[mine, :]
        for k, (px, py, pc) in enumerate(peers):
            slot = buf_ref.at[4 * px + 2 * py + pc]
            pltpu.make_async_remote_copy(src_ref=slot, dst_ref=slot, send_sem=s1.at[k], recv_sem=r1.at[k],
                                         device_id=(px, py, pc), device_id_type=MESH).wait_recv()
        total = buf_ref[0]
        for k in range(1, 8):
            total = total + buf_ref[k]
        o_ref[mine, :] = total
        for k, (px, py, pc) in enumerate(peers):
            cp = pltpu.make_async_remote_copy(src_ref=o_ref.at[mine, :], dst_ref=o_ref.at[mine, :], send_sem=s2.at[k],
                                              recv_sem=r2.at[k], device_id=(px, py, pc), device_id_type=MESH)
            cp.start()
            second.append(cp)
        for k, (px, py, pc) in enumerate(peers):
            rows = o_ref.at[pl.ds(pl.multiple_of((4 * px + 2 * py + pc) * rs, 8), rs), :]
            pltpu.make_async_remote_copy(src_ref=rows, dst_ref=rows, send_sem=s2.at[k], recv_sem=r2.at[k],
                                         device_id=(px, py, pc), device_id_type=MESH).wait_recv()
        for cp in first + second:
            cp.wait_send()

    vm = pl.BlockSpec(memory_space=pltpu.VMEM)
    return pl.pallas_call(
        body, name="small_allreduce", in_specs=[vm], out_specs=vm, out_shape=jax.ShapeDtypeStruct(part.shape, F32),
        scratch_shapes=[pltpu.VMEM((8, rs, LANES), F32)] + [pltpu.SemaphoreType.DMA((7,))] * 4,
    )(part)


def _adamw(w, g, m, v, name):
    R, C = w.shape
    summed = g.ndim == 4
    if summed:
        tr = R // 2
    else:
        tr = R if R * C * 4 <= (1 << 21) else R // 2
        if tr % 8:
            tr = R
    c1 = 1.0 / (1.0 - ADAM_B1 ** ADAM_STEP)
    c2 = 1.0 / (1.0 - ADAM_B2 ** ADAM_STEP)

    def body(w_ref, g_ref, m_ref, v_ref, *outs):
        if summed:
            g_ = ((g_ref[0, 0].astype(F32) + g_ref[1, 0].astype(F32)) + g_ref[2, 0].astype(F32)) + g_ref[3, 0].astype(F32)
            outs[0][...] = g_
        else:
            g_ = g_ref[...]
        d_ref, mo_ref, vo_ref = outs[-3:]
        m_ = ADAM_B1 * m_ref[...] + (1.0 - ADAM_B1) * g_
        v_ = ADAM_B2 * v_ref[...] + (1.0 - ADAM_B2) * (g_ * g_)
        mo_ref[...] = m_
        vo_ref[...] = v_
        d_ref[...] = -ADAM_LR * ((m_ * c1) / (jnp.sqrt(v_ * c2) + ADAM_EPS) + ADAM_WD * w_ref[...])

    blk = BS((tr, C), lambda i: (i, 0))
    g_blk = BS((4, 1, tr, C), lambda i: (0, i, 0, 0)) if summed else blk
    nout = 4 if summed else 3
    return pl.pallas_call(body, name=name, grid=(R // tr,), in_specs=[blk, g_blk, blk, blk], out_specs=[blk] * nout,
                          out_shape=[jax.ShapeDtypeStruct((R, C), F32)] * nout,
                          compiler_params=_cp(("arbitrary",)))(w, g, m, v)


def _adamw_unit_rows(w, g, m, v, name):
    C, _, R = w.shape
    tc = C // 2 if C % 2 == 0 else C
    c1 = 1.0 / (1.0 - ADAM_B1 ** ADAM_STEP)
    c2 = 1.0 / (1.0 - ADAM_B2 ** ADAM_STEP)

    def body(w_ref, g_ref, m_ref, v_ref, go_ref, d_ref, mo_ref, vo_ref):
        g_ = g_ref[...]
        go_ref[...] = g_
        m_ = ADAM_B1 * m_ref[...] + (1.0 - ADAM_B1) * g_
        v_ = ADAM_B2 * v_ref[...] + (1.0 - ADAM_B2) * (g_ * g_)
        mo_ref[...] = m_
        vo_ref[...] = v_
        d_ref[...] = -ADAM_LR * ((m_ * c1) / (jnp.sqrt(v_ * c2) + ADAM_EPS) + ADAM_WD * w_ref[...])

    blk = BS((tc, 1, R), lambda i: (i, 0, 0))
    return pl.pallas_call(body, name=name, grid=(C // tc,), in_specs=[blk] * 4, out_specs=[blk] * 4,
                          out_shape=[jax.ShapeDtypeStruct((C, 1, R), F32)] * 4,
                          compiler_params=_cp(("arbitrary",)))(w, g, m, v)


SMALL = ("g_pre_mix", "b_f", "g_sgu", "w_s", "b_s", "g_out_a", "g_out_b", "g_out_m", "g_mem", "g_post_mix",
         "g_pre_ffn", "g_post_ffn")
BIG = ("w_in", "w_mem_kv", "w_out", "w_gate", "w_up", "w_down")
TRANSPOSED = ("w_in", "w_gate", "w_up")
WEIGHTS = ("g_pre_mix", "w_in", "b_f", "g_sgu", "w_s", "b_s", "g_out_a", "g_out_b", "g_out_m", "g_mem", "w_mem_kv",
           "w_out", "g_post_mix", "g_pre_ffn", "w_gate", "w_up", "w_down", "g_post_ffn")


VECTORS = ("g_pre_mix", "b_f", "g_sgu", "g_out_a", "g_out_b", "g_out_m", "g_mem", "g_post_mix", "g_pre_ffn", "g_post_ffn")
VEC_ROWS = 16
WS_ROWS = N_FOX_HEADS * CHUNK
BS_ROWS = 8


def _pack_small(small, vw):
    stack = jnp.zeros((VEC_ROWS, vw), F32)
    for k, n in enumerate(VECTORS + ("loss",)):
        row = small[n].reshape(1, -1)
        stack = stack + jnp.pad(row, ((k, VEC_ROWS - 1 - k), (0, vw - row.shape[1])))
    parts = [small["w_s"].reshape(WS_ROWS, LANES), jnp.pad(small["b_s"], ((0, BS_ROWS - N_FOX_HEADS), (0, 0))),
             stack.reshape(-1, LANES)]
    rows = sum(p.shape[0] for p in parts)
    return jnp.concatenate(parts + [jnp.zeros((-rows % 64, LANES), F32)], axis=0)


def _adamw_small(vec_g, vec_wmv, ws, bs):
    c1 = 1.0 / (1.0 - ADAM_B1 ** ADAM_STEP)
    c2 = 1.0 / (1.0 - ADAM_B2 ** ADAM_STEP)
    nv = len(vec_wmv)

    def adam(g, w, m, v):
        m_ = ADAM_B1 * m + (1.0 - ADAM_B1) * g
        v_ = ADAM_B2 * v + (1.0 - ADAM_B2) * (g * g)
        return -ADAM_LR * ((m_ * c1) / (jnp.sqrt(v_ * c2) + ADAM_EPS) + ADAM_WD * w), m_, v_

    def body(*refs):
        vg_ref, ins, outs = refs[0], refs[1:1 + 3 * nv + 8], refs[1 + 3 * nv + 8:]
        for k in range(nv):
            w_ref, m_ref, v_ref = ins[3 * k:3 * k + 3]
            g = vg_ref[k:k + 1, 0:w_ref.shape[1]]
            d, m_, v_ = adam(g, w_ref[...], m_ref[...], v_ref[...])
            for o_ref, val in zip(outs[4 * k:4 * k + 4], (g, d, m_, v_)):
                o_ref[...] = val
        for j in range(2):
            g_ref, w_ref, m_ref, v_ref = ins[3 * nv + 4 * j:3 * nv + 4 * j + 4]
            for o_ref, val in zip(outs[4 * nv + 3 * j:4 * nv + 3 * j + 3], adam(g_ref[...], w_ref[...], m_ref[...], v_ref[...])):
                o_ref[...] = val

    vm = pl.BlockSpec(memory_space=pltpu.VMEM)
    operands = [vec_g] + [a for wmv in vec_wmv for a in wmv] + list(ws) + list(bs)
    out_shape = ([jax.ShapeDtypeStruct(wmv[0].shape, F32) for wmv in vec_wmv for _ in range(4)]
                 + [jax.ShapeDtypeStruct(ws[1].shape, F32)] * 3 + [jax.ShapeDtypeStruct(bs[1].shape, F32)] * 3)
    outs = pl.pallas_call(body, name="adamw_small", in_specs=[vm] * len(operands), out_specs=[vm] * len(out_shape),
                          out_shape=out_shape)(*operands)
    return [outs[4 * k:4 * k + 4] for k in range(nv)], outs[4 * nv:4 * nv + 3], outs[4 * nv + 3:]


def kernel(x, mem, g_pre_mix, w_in, b_f, g_sgu, w_s, b_s, g_out_a, g_out_b, g_out_m, g_mem, w_mem_kv, w_out, g_post_mix, g_pre_ffn, w_gate, w_up, w_down, g_post_ffn, loss_target, m_g_pre_mix, m_w_in, m_b_f, m_g_sgu, m_w_s, m_b_s, m_g_out_a, m_g_out_b, m_g_out_m, m_g_mem, m_w_mem_kv, m_w_out, m_g_post_mix, m_g_pre_ffn, m_w_gate, m_w_up, m_w_down, m_g_post_ffn, v_g_pre_mix, v_w_in, v_b_f, v_g_sgu, v_w_s, v_b_s, v_g_out_a, v_g_out_b, v_g_out_m, v_g_mem, v_w_mem_kv, v_w_out, v_g_post_mix, v_g_pre_ffn, v_w_gate, v_w_up, v_w_down, v_g_post_ffn):
    Wt = dict(g_pre_mix=g_pre_mix, w_in=w_in, b_f=b_f, g_sgu=g_sgu, w_s=w_s, b_s=b_s, g_out_a=g_out_a, g_out_b=g_out_b,
              g_out_m=g_out_m, g_mem=g_mem, w_mem_kv=w_mem_kv, w_out=w_out, g_post_mix=g_post_mix, g_pre_ffn=g_pre_ffn,
              w_gate=w_gate, w_up=w_up, w_down=w_down, g_post_ffn=g_post_ffn)
    Mo = dict(g_pre_mix=m_g_pre_mix, w_in=m_w_in, b_f=m_b_f, g_sgu=m_g_sgu, w_s=m_w_s, b_s=m_b_s, g_out_a=m_g_out_a,
              g_out_b=m_g_out_b, g_out_m=m_g_out_m, g_mem=m_g_mem, w_mem_kv=m_w_mem_kv, w_out=m_w_out,
              g_post_mix=m_g_post_mix, g_pre_ffn=m_g_pre_ffn, w_gate=m_w_gate, w_up=m_w_up, w_down=m_w_down,
              g_post_ffn=m_g_post_ffn)
    Vo = dict(g_pre_mix=v_g_pre_mix, w_in=v_w_in, b_f=v_b_f, g_sgu=v_g_sgu, w_s=v_w_s, b_s=v_b_s, g_out_a=v_g_out_a,
              g_out_b=v_g_out_b, g_out_m=v_g_out_m, g_mem=v_g_mem, w_mem_kv=v_w_mem_kv, w_out=v_w_out,
              g_post_mix=v_g_post_mix, g_pre_ffn=v_g_pre_ffn, w_gate=v_w_gate, w_up=v_w_up, w_down=v_w_down,
              g_post_ffn=v_g_post_ffn)

    gap = P_COLS - IN_COLS

    def to_kernel(n, w):
        if n in TRANSPOSED:
            w = w.T
        if n == "w_in":
            w = jnp.pad(w[:F_END], ((0, P_COLS - F_END), (0, 0))) + jnp.pad(w[F_END:], ((F_END + gap, 0), (0, 0)))
        return w

    def ungroup(g):
        return jnp.pad(g[:F_END], ((0, IN_COLS - F_END), (0, 0))) + jnp.pad(g[F_END + gap:], ((F_END, 0), (0, 0)))

    shards = {n: to_kernel(n, Wt[n][0]) for n in BIG}
    srcs = [shards[n].astype(BF16).reshape(2, shards[n].shape[0] // 2, shards[n].shape[1]) for n in BIG]
    fulls = (_exchange_on_sequencer(srcs[:1], True, "gather_w_in", 1)
             + _exchange_on_sequencer(srcs[1:3], True, "gather_kv_out", 2)
             + _exchange_on_sequencer(srcs[3:], True, "gather_ffn", 3))
    W = {}
    for n, f in zip(BIG, fulls):
        _, _, hr, C = f.shape
        W[n] = f.reshape(8 * hr, C) if n in ("w_mem_kv", "w_out") else f.reshape(4, 2 * hr, C)

    P = {n: Wt[n] for n in SMALL}
    grads, deltas, new_m, new_v = {}, {}, {}, {}

    def apply(names, landed):
        for n, r in zip(names, landed):
            if n == "w_in":
                g_t = ungroup(_sum_chips(r, "sum_chips_" + n))
                lift = lambda a: jnp.transpose(a, (2, 0, 1))
                outs = _adamw_unit_rows(lift(Wt[n]), g_t[:, None, :], lift(Mo[n]), lift(Vo[n]), "adamw_" + n)
                g, d, m1, v1 = [jnp.transpose(a, (1, 2, 0))[0] for a in outs]
            elif n in TRANSPOSED:
                g, d, m1, v1 = [a.T for a in _adamw(Wt[n][0].T, r, Mo[n][0].T, Vo[n][0].T, "adamw_" + n)]
            else:
                g, d, m1, v1 = _adamw(Wt[n][0], r, Mo[n][0], Vo[n][0], "adamw_" + n)
            grads[n], deltas[n], new_m[n], new_v[n] = g[None], d[None], m1[None], v1[None]
        return tuple(deltas[n] for n in names)

    core = lax.axis_index("c").astype(jnp.int32).reshape(1)
    reducer = _Reducer(core, apply)
    grad_x, pending, small = _local_step(x, mem, loss_target, W, P, reducer)

    vw = -(-max(x.shape[-1], A_W) // LANES) * LANES
    landed, (packed,) = reducer.finish("in", pending["w_in"], (_pack_small(small, vw),))
    total = _small_allreduce(packed)
    apply(BIG[:1], landed)

    lane_row = lambda a: jnp.pad(a, ((0, 0), (0, -a.shape[1] % LANES)))
    vec_g = total[WS_ROWS + BS_ROWS:WS_ROWS + BS_ROWS + VEC_ROWS * vw // LANES].reshape(VEC_ROWS, vw)
    ws_g = total[:WS_ROWS]
    bs_g = total[WS_ROWS:WS_ROWS + N_FOX_HEADS]
    rows = lambda a, r: a.reshape(r, LANES)
    per_vec, ws_out, bs_out = _adamw_small(
        vec_g, [tuple(lane_row(a[n]) for a in (Wt, Mo, Vo)) for n in VECTORS],
        (ws_g,) + tuple(rows(a["w_s"], WS_ROWS) for a in (Wt, Mo, Vo)),
        (bs_g,) + tuple(rows(a["b_s"], N_FOX_HEADS) for a in (Wt, Mo, Vo)))
    for n, outs in zip(VECTORS, per_vec):
        grads[n], deltas[n], new_m[n], new_v[n] = [o[:, :Wt[n].shape[1]] for o in outs]
    for n, g, outs in (("w_s", ws_g, ws_out), ("b_s", bs_g, bs_out)):
        grads[n], deltas[n], new_m[n], new_v[n] = [o.reshape(Wt[n].shape) for o in (g,) + tuple(outs)]
    loss = vec_g[len(VECTORS), 0]

    return (loss, grad_x, *[grads[n] for n in WEIGHTS], *[deltas[n] for n in WEIGHTS],
            *[new_m[n] for n in WEIGHTS], *[new_v[n] for n in WEIGHTS])
```

```python
import jax
import jax.numpy as jnp
from jax import lax
from jax.experimental import pallas as pl
from jax.experimental.pallas import tpu as pltpu
from jax.experimental.pallas import tpu_sc as plsc

F32 = jnp.float32
BF16 = jnp.bfloat16
EPS = 1e-6
NEG = -1e30
HEAD = 64
A_W, B_W, M_W = 384, 384, 256
N_FOX_HEADS = 6
CHUNK = 128
IN_COLS = 2 * A_W + 3 * B_W + N_FOX_HEADS + M_W
P_MAIN = 2 * A_W + 3 * B_W + M_W
LANES = 128
P_COLS = P_MAIN + LANES
F_END = 2 * A_W + 3 * B_W + N_FOX_HEADS
Q_BLK = 512
ROW_SPLIT = 4
ADAM_LR, ADAM_B1, ADAM_B2, ADAM_EPS, ADAM_WD, ADAM_STEP = 0.001, 0.9, 0.999, 1e-08, 0.01, 10
VMEM_LIMIT = 56 * 1024 * 1024
MESH = pl.DeviceIdType.MESH
BS = pl.BlockSpec


def _cp(sem=None):
    return pltpu.CompilerParams(dimension_semantics=sem, vmem_limit_bytes=VMEM_LIMIT)


def _iota(shape, dim):
    return lax.broadcasted_iota(jnp.int32, shape, dim)


def _dot(a, b):
    return jnp.dot(a.astype(BF16), b.astype(BF16), preferred_element_type=F32)


def _dot_nt(a, b):
    return lax.dot_general(a.astype(BF16), b.astype(BF16), (((1,), (1,)), ((), ())), preferred_element_type=F32)


def _dot_tn(a, b):
    return lax.dot_general(a.astype(BF16), b.astype(BF16), (((0,), (0,)), ((), ())), preferred_element_type=F32)


def _rms(x, g):
    return x * lax.rsqrt(jnp.mean(x * x, axis=-1, keepdims=True) + EPS) * g


def _rms_bwd(x, g, dy):
    r = lax.rsqrt(jnp.mean(x * x, axis=-1, keepdims=True) + EPS)
    xr = x * r
    gd = dy * g
    m = jnp.mean(gd * xr, axis=-1, keepdims=True)
    return (gd - xr * m) * r, _colsum(dy * xr)


def _gelu(x):
    return 0.5 * x * (1.0 + jnp.tanh(0.7978845608028654 * (x + 0.044715 * (x * x * x))))


def _sigmoid(x):
    return 1.0 / (1.0 + jnp.exp(-x))


def _silu_mul(g, u):
    return g * _sigmoid(g) * u


def _logsig(x):
    return jnp.minimum(x, 0.0) - jnp.log(1.0 + jnp.exp(-jnp.abs(x)))


def _colsum(x):
    return jnp.sum(x, axis=0, keepdims=True)


def _acc(ref, val, first):
    @pl.when(first)
    def _():
        ref[...] = val

    @pl.when(jnp.logical_not(first))
    def _():
        ref[...] += val


def _inproj_fwd(x2d, g_pre, w_in_p, tm):
    T, D = x2d.shape
    CH = 768
    nchunk = P_COLS // CH
    ns, _, dsh = w_in_p.shape

    def body(x_ref, g_ref, w_ref, h_ref, proj_ref, fl_ref):
        h = _rms(x_ref[...], g_ref[...]).astype(BF16)
        h_ref[...] = h
        for n in range(nchunk):
            rows = slice(n * CH, (n + 1) * CH)
            r = _dot_nt(h[:, 0:dsh], w_ref[0, rows, :])
            for s in range(1, ns):
                r = r + _dot_nt(h[:, s * dsh:(s + 1) * dsh], w_ref[s, rows, :])
            if n < nchunk - 1:
                proj_ref[:, rows] = r.astype(BF16)
            else:
                fg = 1920 - n * CH
                proj_ref[:, n * CH:1920] = r[:, :fg].astype(BF16)
                fl_ref[...] = r[:, fg:fg + LANES]
                proj_ref[:, 1920:P_MAIN] = r[:, fg + LANES:].astype(BF16)

    return pl.pallas_call(
        body, name="inproj_fwd", grid=(T // tm,),
        in_specs=[BS((tm, D), lambda i: (i, 0)), BS((1, D), lambda i: (0, 0)),
                  BS((ns, P_COLS, dsh), lambda i: (0, 0, 0))],
        out_specs=[BS((tm, D), lambda i: (i, 0)), BS((tm, P_MAIN), lambda i: (i, 0)), BS((tm, LANES), lambda i: (i, 0))],
        out_shape=[jax.ShapeDtypeStruct((T, D), BF16), jax.ShapeDtypeStruct((T, P_MAIN), BF16),
                   jax.ShapeDtypeStruct((T, LANES), F32)],
        compiler_params=_cp(("arbitrary",)),
    )(x2d, g_pre, w_in_p)


def _gate_fwd(flog3, bf_row):
    Bl, S, _ = flog3.shape
    nb = S // LANES

    def body(f_ref, b_ref, bq_ref, bk_ref, fr_ref):
        row = _iota((LANES, LANES), 0)
        lane = _iota((LANES, LANES), 1)
        one = jnp.ones((LANES, LANES), BF16)
        zero = jnp.zeros((LANES, LANES), BF16)

        carry = jnp.zeros((1, LANES), F32)
        for j in range(nb):
            r0 = j * LANES
            fl = f_ref[0, pl.ds(r0, LANES), :] + b_ref[...]
            fr_ref[0, j] = fl.T[0:8, :]
            c = _logsig(fl)
            for k in (1, 2, 4, 8, 16, 32, 64):
                c = c + jnp.where(row >= k, pltpu.roll(c, k, 0), 0.0)
            total = _colsum(jnp.where(row == LANES - 1, c, 0.0))
            c = c + carry
            carry = carry + total
            for h in range(N_FOX_HEADS):
                col = jnp.sum(jnp.where(lane == h, c, 0.0), axis=1, keepdims=True)
                hi = col.astype(BF16)
                rest = col - hi.astype(F32)
                mid = rest.astype(BF16)
                lo = (rest - mid.astype(F32)).astype(BF16)
                base = _bias_lane(h)
                bq = jnp.where(lane == base, hi, jnp.where(lane == base + 1, mid, jnp.where(lane == base + 2, lo, zero)))
                bq = jnp.where((lane >= base + 3) & (lane < base + 6), one, bq)
                bk = jnp.where(lane == base + 3, -hi, jnp.where(lane == base + 4, -mid, jnp.where(lane == base + 5, -lo, zero)))
                bk = jnp.where((lane >= base) & (lane < base + 3), one, bk)
                bq_ref[0, h, pl.ds(r0, LANES), :] = bq
                bk_ref[0, h, pl.ds(r0, LANES), :] = bk

    slab = BS((1, N_FOX_HEADS, S, LANES), lambda b: (b, 0, 0, 0))
    return pl.pallas_call(
        body, name="gate_fwd", grid=(Bl,),
        in_specs=[BS((1, S, LANES), lambda b: (b, 0, 0)), BS((1, LANES), lambda b: (0, 0))],
        out_specs=[slab, slab, BS((1, nb, 8, LANES), lambda b: (b, 0, 0, 0))],
        out_shape=[jax.ShapeDtypeStruct((Bl, N_FOX_HEADS, S, LANES), BF16),
                   jax.ShapeDtypeStruct((Bl, N_FOX_HEADS, S, LANES), BF16),
                   jax.ShapeDtypeStruct((Bl, nb, 8, LANES), F32)],
        compiler_params=_cp(("arbitrary",)),
    )(flog3, bf_row)


def _bias_lane(h):
    return HEAD if h % 2 == 0 else 0


def _sgu_pre(zu, zv, g_sgu):
    return _gelu(zu), _rms(_gelu(zv), g_sgu)


def _sgu_fwd(proj, g_sgu, ws_tril, bs_full, tm):
    T = proj.shape[0]
    nch = tm // CHUNK

    def body(zu_ref, zv_ref, g_ref, ws_ref, b_ref, ya_ref):
        lane = _iota((CHUNK, LANES), 1)
        u, vn = _sgu_pre(zu_ref[...].astype(F32), zv_ref[...].astype(F32), g_ref[...])
        vn = vn.astype(BF16)
        for c in range(nch):
            rs = slice(c * CHUNK, (c + 1) * CHUNK)
            for j in range(3):
                cs = slice(j * LANES, (j + 1) * LANES)
                vp = vn[rs, cs]
                z = jnp.where(lane < HEAD, _dot(ws_ref[2 * j], vp), _dot(ws_ref[2 * j + 1], vp)) + b_ref[:, cs]
                ya_ref[rs, cs] = (u[rs, cs] * z).astype(BF16)

    return pl.pallas_call(
        body, name="sgu_fwd", grid=(T // tm,),
        in_specs=[BS((tm, A_W), lambda i: (i, 0)), BS((tm, A_W), lambda i: (i, 1)), BS((1, A_W), lambda i: (0, 0)),
                  BS((6, CHUNK, CHUNK), lambda i: (0, 0, 0)), BS((CHUNK, A_W), lambda i: (0, 0))],
        out_specs=BS((tm, A_W), lambda i: (i, 0)),
        out_shape=jax.ShapeDtypeStruct((T, A_W), BF16),
        compiler_params=_cp(("arbitrary",)),
    )(proj, proj, g_sgu, ws_tril, bs_full)


def _fox_fwd(proj, bq, bk, Bl, S):
    T = Bl * S
    nq = S // Q_BLK
    qc, kc, vc = 768 // LANES, 1152 // LANES, 1536 // LANES

    def body(q_ref, k_ref, v_ref, bq_ref, bk_ref, o_ref, lse_ref, ka_ref, va_ref):
        lane_s = _iota((S, LANES), 1)
        lane = _iota((Q_BLK, LANES), 1)
        tri = _iota((Q_BLK, Q_BLK), 1) <= _iota((Q_BLK, Q_BLK), 0)
        k = k_ref[...]
        v = v_ref[...]
        for hh in range(2):
            data = (lane_s < HEAD) if hh == 0 else (lane_s >= HEAD)
            ka_ref[hh] = jnp.where(data, k, bk_ref[0, hh])
            va_ref[hh] = jnp.where(lane_s == _bias_lane(hh), jnp.ones_like(v), v)
        for i in range(nq):
            r0 = i * Q_BLK
            q = q_ref[r0:r0 + Q_BLK, :]
            o_out = jnp.zeros((Q_BLK, LANES), F32)
            lse_out = jnp.zeros((Q_BLK, LANES), F32)
            for hh in range(2):
                hmask = (lane < HEAD) if hh == 0 else (lane >= HEAD)
                qa = jnp.where(hmask, q * 0.125, bq_ref[0, hh, r0:r0 + Q_BLK, :])
                sd = jnp.where(tri, _dot_nt(qa, ka_ref[hh, r0:r0 + Q_BLK, :]), NEG)
                m = jnp.max(sd, axis=1, keepdims=True)
                if i:
                    sf = _dot_nt(qa, ka_ref[hh, 0:r0, :])
                    m = jnp.maximum(m, jnp.max(sf, axis=1, keepdims=True))
                acc = _dot(jnp.exp(sd - m), va_ref[hh, r0:r0 + Q_BLK, :])
                if i:
                    acc = acc + _dot(jnp.exp(sf - m), va_ref[hh, 0:r0, :])
                l = jnp.sum(jnp.where(lane == _bias_lane(hh), acc, 0.0), axis=1, keepdims=True)
                o_out = jnp.where(hmask, acc / l, o_out)
                lse_out = jnp.where(hmask, m + jnp.log(l), lse_out)
            o_ref[r0:r0 + Q_BLK, :] = o_out.astype(BF16)
            lse_ref[0, r0:r0 + Q_BLK, :] = lse_out

    seq = lambda c0: BS((S, LANES), lambda b, p: (b, c0 + p))
    pair = BS((1, 2, S, LANES), lambda b, p: (b, p, 0, 0))
    return pl.pallas_call(
        body, name="fox_fwd", grid=(Bl, 3),
        in_specs=[seq(qc), seq(kc), seq(vc), pair, pair],
        out_specs=[seq(0), BS((1, S, LANES), lambda b, p: (p, b, 0))],
        out_shape=[jax.ShapeDtypeStruct((T, B_W), BF16), jax.ShapeDtypeStruct((3, T, LANES), F32)],
        scratch_shapes=[pltpu.VMEM((2, S, LANES), BF16), pltpu.VMEM((2, S, LANES), BF16)],
        compiler_params=_cp(("arbitrary", "arbitrary")),
    )(proj, proj, proj, bq, bk)


def _memkv_fwd(mem, g_mem, w_kv):
    Bl, Mt, D = mem.shape

    def body(m_ref, g_ref, w_ref, mn_ref, kv_ref):
        mn = _rms(m_ref[0], g_ref[...]).astype(BF16)
        mn_ref[0] = mn
        kv_ref[0] = jnp.dot(mn, w_ref[...], preferred_element_type=F32).astype(BF16)

    return pl.pallas_call(
        body, name="memkv_fwd", grid=(Bl,),
        in_specs=[BS((1, Mt, D), lambda b: (b, 0, 0)), BS((1, D), lambda b: (0, 0)), BS((D, 2 * M_W), lambda b: (0, 0))],
        out_specs=[BS((1, Mt, D), lambda b: (b, 0, 0)), BS((1, Mt, 2 * M_W), lambda b: (b, 0, 0))],
        out_shape=[jax.ShapeDtypeStruct((Bl, Mt, D), BF16), jax.ShapeDtypeStruct((Bl, Mt, 2 * M_W), BF16)],
        compiler_params=_cp(("arbitrary",)),
    )(mem, g_mem, w_kv)


def _memattn_fwd(proj, kv, Bl, S, tq):
    T = Bl * S
    nq = S // tq
    Mt = kv.shape[1]
    qc = 1920 // LANES

    def body(q_ref, km_ref, vm_ref, o_ref):
        lane = _iota((tq, LANES), 1)
        q = q_ref[...]
        out = jnp.zeros((tq, LANES), F32)
        for hh in range(2):
            hmask = (lane < HEAD) if hh == 0 else (lane >= HEAD)
            qs = jnp.where(hmask, q, jnp.zeros_like(q)) * 0.125
            s = _dot_nt(qs, km_ref[0])
            pe = jnp.exp(s - jnp.max(s, axis=1, keepdims=True))
            pn = pe / jnp.sum(pe, axis=1, keepdims=True)
            out = jnp.where(hmask, _dot(pn, vm_ref[0]), out)
        o_ref[...] = out.astype(BF16)

    return pl.pallas_call(
        body, name="memattn_fwd", grid=(Bl, 2, nq),
        in_specs=[BS((tq, LANES), lambda b, p, i: (b * nq + i, qc + p)),
                  BS((1, Mt, LANES), lambda b, p, i: (b, 0, p)),
                  BS((1, Mt, LANES), lambda b, p, i: (b, 0, 2 + p))],
        out_specs=BS((tq, LANES), lambda b, p, i: (b * nq + i, p)),
        out_shape=jax.ShapeDtypeStruct((T, M_W), BF16),
        compiler_params=_cp(("arbitrary", "arbitrary", "arbitrary")),
    )(proj, kv, kv)


def _mix_norms(ya, yb, ym, ga, gb, gm):
    return _rms(ya, ga), _rms(yb, gb), _rms(ym, gm)


def _outproj_fwd(ya, yb, ym, x2d, ga, gb, gm, g_post, g_pre2, w_out, tm):
    T, D = x2d.shape

    def body(ya_ref, yb_ref, ym_ref, x_ref, ga_ref, gb_ref, gm_ref, gp_ref, g2_ref, w_ref,
             y_ref, o_ref, x1_ref, h2_ref):
        na, nb_, nm = _mix_norms(ya_ref[...].astype(F32), yb_ref[...].astype(F32), ym_ref[...].astype(F32),
                                 ga_ref[...], gb_ref[...], gm_ref[...])
        y_ref[:, 0:A_W] = na.astype(BF16)
        y_ref[:, A_W:A_W + B_W] = nb_.astype(BF16)
        y_ref[:, A_W + B_W:] = nm.astype(BF16)
        o = jnp.dot(y_ref[...], w_ref[...], preferred_element_type=F32).astype(BF16)
        o_ref[...] = o
        x1 = x_ref[...] + _rms(o.astype(F32), gp_ref[...])
        x1_ref[...] = x1
        h2_ref[...] = _rms(x1, g2_ref[...]).astype(BF16)

    row = lambda w: BS((tm, w), lambda i: (i, 0))
    vec = lambda w: BS((1, w), lambda i: (0, 0))
    return pl.pallas_call(
        body, name="outproj_fwd", grid=(T // tm,),
        in_specs=[row(A_W), row(B_W), row(M_W), row(D), vec(A_W), vec(B_W), vec(M_W), vec(D), vec(D),
                  BS((A_W + B_W + M_W, D), lambda i: (0, 0))],
        out_specs=[row(A_W + B_W + M_W), row(D), row(D), row(D)],
        out_shape=[jax.ShapeDtypeStruct((T, A_W + B_W + M_W), BF16), jax.ShapeDtypeStruct((T, D), BF16),
                   jax.ShapeDtypeStruct((T, D), F32), jax.ShapeDtypeStruct((T, D), BF16)],
        compiler_params=_cp(("arbitrary",)),
    )(ya, yb, ym, x2d, ga, gb, gm, g_post, g_pre2, w_out)


def _ffn_fwd(h2, x1, target, wg, wu, wd, g_post, tm):
    T, D = x1.shape
    ns, F, _ = wg.shape

    def body(h_ref, x1_ref, t_ref, wg_ref, wu_ref, wd_ref, gp_ref,
             gs_ref, us_ref, dff_ref, dx2_ref, dgp_ref, loss_ref, acc_ref):
        j = pl.program_id(0)
        i = pl.program_id(1)
        rows = pl.ds(pl.multiple_of(i * tm, tm), tm)
        h = h_ref[...]
        g = _dot_nt(h, wg_ref[0])
        u = _dot_nt(h, wu_ref[0])
        gs_ref[0] = g.astype(BF16)
        us_ref[0] = u.astype(BF16)
        part = _dot(_silu_mul(g, u), wd_ref[0])

        @pl.when(j == 0)
        def _():
            acc_ref[rows, :] = part

        @pl.when(j != 0)
        def _():
            acc_ref[rows, :] += part

        @pl.when(j == ns - 1)
        def _():
            ff = acc_ref[rows, :]
            diff = x1_ref[...] + _rms(ff, gp_ref[...]) - t_ref[...]
            dx2 = diff * (1.0 / D)
            dff, dgp = _rms_bwd(ff, gp_ref[...], dx2)
            dx2_ref[...] = dx2
            dff_ref[...] = dff.astype(BF16)
            lpart = jnp.sum(_colsum(diff * diff), axis=1, keepdims=True) * (0.5 / D)
            _acc(dgp_ref, dgp, i == 0)
            _acc(loss_ref, jnp.broadcast_to(lpart, (1, LANES)), i == 0)

    last = lambda j, i: (jnp.where(j == ns - 1, i, 0), 0)
    wsh = BS((1, F, D), lambda j, i: (j, 0, 0))
    sh = BS((1, tm, F), lambda j, i: (j, i, 0))
    return pl.pallas_call(
        body, name="ffn_fwd", grid=(ns, T // tm),
        in_specs=[BS((tm, D), lambda j, i: (i, 0)), BS((tm, D), last), BS((tm, D), last), wsh, wsh, wsh,
                  BS((1, D), lambda j, i: (0, 0))],
        out_specs=[sh, sh, BS((tm, D), last), BS((tm, D), last),
                   BS((1, D), lambda j, i: (0, 0)), BS((1, LANES), lambda j, i: (0, 0))],
        out_shape=[jax.ShapeDtypeStruct((ns, T, F), BF16), jax.ShapeDtypeStruct((ns, T, F), BF16),
                   jax.ShapeDtypeStruct((T, D), BF16), jax.ShapeDtypeStruct((T, D), F32),
                   jax.ShapeDtypeStruct((1, D), F32), jax.ShapeDtypeStruct((1, LANES), F32)],
        scratch_shapes=[pltpu.VMEM((T, D), F32)],
        compiler_params=_cp(("arbitrary", "arbitrary")),
    )(h2, x1, target, wg, wu, wd, g_post)


def _ffn_bwd(dff, h2, gs, us, wg, wu, wd, tm):
    T, D = h2.shape
    ns, F, _ = wg.shape

    def body(dff_ref, h_ref, gs_ref, us_ref, wg_ref, wu_ref, wd_ref, dh_ref, dwg_out, dwu_out, dwd_out,
             dwg_ref, dwu_ref, dwd_ref):
        first = pl.program_id(1) == 0
        dff = dff_ref[...]
        h = h_ref[...]
        parts = []
        for r in range(ROW_SPLIT):
            rows = slice(r * (tm // ROW_SPLIT), (r + 1) * (tm // ROW_SPLIT))
            dact = _dot_nt(dff[rows], wd_ref[0])
            g = gs_ref[0, rows, :].astype(F32)
            u = us_ref[0, rows, :].astype(F32)
            sig = _sigmoid(g)
            gsig = g * sig
            dg = (dact * u * (sig + gsig * (1.0 - sig))).astype(BF16)
            du = (dact * gsig).astype(BF16)
            dh_ref[0, rows, :] = (_dot(dg, wg_ref[0]) + _dot(du, wu_ref[0])).astype(BF16)
            parts.append(((gsig * u).astype(BF16), dg, du))
        a, dg, du = [jnp.concatenate(p, axis=0) for p in zip(*parts)]
        _acc(dwd_ref, _dot_tn(a, dff), first)
        _acc(dwg_ref, _dot_tn(dg, h), first)
        _acc(dwu_ref, _dot_tn(du, h), first)

        @pl.when(pl.program_id(1) == pl.num_programs(1) - 1)
        def _():
            dwg_out[0] = dwg_ref[...].astype(BF16)
            dwu_out[0] = dwu_ref[...].astype(BF16)
            dwd_out[0] = dwd_ref[...].astype(BF16)

    row = BS((tm, D), lambda j, i: (i, 0))
    sh = BS((1, tm, F), lambda j, i: (j, i, 0))
    wsh = BS((1, F, D), lambda j, i: (j, 0, 0))
    return pl.pallas_call(
        body, name="ffn_bwd", grid=(ns, T // tm),
        in_specs=[row, row, sh, sh, wsh, wsh, wsh],
        out_specs=[BS((1, tm, D), lambda j, i: (j, i, 0)), wsh, wsh, wsh],
        out_shape=[jax.ShapeDtypeStruct((ns, T, D), BF16)] + [jax.ShapeDtypeStruct((ns, F, D), BF16)] * 3,
        scratch_shapes=[pltpu.VMEM((F, D), F32)] * 3,
        compiler_params=_cp(("arbitrary", "arbitrary")),
    )(dff, h2, gs, us, wg, wu, wd)


DPROJ_PIECES = ((0, A_W), (A_W, A_W), (768, B_W), (1152, B_W), (1536, B_W), (1920, LANES), (2048, M_W))


def _put_dproj(dp_ref, piece_refs):
    for (c0, w), ref in zip(DPROJ_PIECES, piece_refs):
        dp_ref[:, c0:c0 + w] = ref[...].astype(BF16)


def _dw_in(pieces, h, ns, tk):
    T, D = h.shape
    M = P_COLS
    dsh = D // ns
    tk = min(tk, T)

    def body(*refs):
        piece_refs, h_ref, o_ref, acc_ref, dp_ref = refs[:7], refs[7], refs[8], refs[9], refs[10]
        t = pl.program_id(0)
        _put_dproj(dp_ref, piece_refs)
        _acc(acc_ref, _dot_tn(h_ref[...], dp_ref[...]), t == 0)

        @pl.when(t == pl.num_programs(0) - 1)
        def _():
            for s in range(ns):
                o_ref[s] = acc_ref[s * dsh:(s + 1) * dsh, :].T.astype(BF16)

    return pl.pallas_call(
        body, name="dw_in", grid=(T // tk,),
        in_specs=[BS((tk, w), lambda t: (t, 0)) for _, w in DPROJ_PIECES] + [BS((tk, D), lambda t: (t, 0))],
        out_specs=BS((ns, M, dsh), lambda t: (0, 0, 0)),
        out_shape=jax.ShapeDtypeStruct((ns, M, dsh), BF16),
        scratch_shapes=[pltpu.VMEM((D, M), F32), pltpu.VMEM((tk, M), BF16)],
        compiler_params=_cp(("arbitrary",)),
    )(*pieces, h)


def _outproj_bwd(dh2, x1, dx2, o, y, ya, yb, ym, ga, gb, gm, g_post, g_pre2, w_out, tm):
    T, D = x1.shape
    ns = dh2.shape[0]

    def body(dh_ref, x1_ref, dx2_ref, o_ref, y_ref, ya_ref, yb_ref, ym_ref, ga_ref, gb_ref, gm_ref, gp_ref, g2_ref, w_ref,
             dx1_ref, dw_ref, dya_ref, dyb_ref, dym_ref, dga_ref, dgb_ref, dgm_ref, dgp_ref, dg2_ref):
        first = pl.program_id(0) == 0
        dh = dh_ref[0].astype(F32)
        for j in range(1, ns):
            dh = dh + dh_ref[j].astype(F32)
        dxa, dg2 = _rms_bwd(x1_ref[...], g2_ref[...], dh)
        dx1 = dx2_ref[...] + dxa
        dx1_ref[...] = dx1
        _acc(dg2_ref, dg2, first)
        do, dgp = _rms_bwd(o_ref[...].astype(F32), gp_ref[...], dx1)
        do = do.astype(BF16)
        _acc(dw_ref, _dot_tn(y_ref[...], do), first)
        dy = _dot_nt(do, w_ref[...])
        dya, dga = _rms_bwd(ya_ref[...].astype(F32), ga_ref[...], dy[:, 0:A_W])
        dyb, dgb = _rms_bwd(yb_ref[...].astype(F32), gb_ref[...], dy[:, A_W:A_W + B_W])
        dym, dgm = _rms_bwd(ym_ref[...].astype(F32), gm_ref[...], dy[:, A_W + B_W:])
        dya_ref[...] = dya.astype(BF16)
        dyb_ref[...] = dyb.astype(BF16)
        dym_ref[...] = dym.astype(BF16)
        _acc(dga_ref, dga, first)
        _acc(dgb_ref, dgb, first)
        _acc(dgm_ref, dgm, first)
        _acc(dgp_ref, dgp, first)

    row = lambda w: BS((tm, w), lambda i: (i, 0))
    vec = lambda w: BS((1, w), lambda i: (0, 0))
    sds = jax.ShapeDtypeStruct
    return pl.pallas_call(
        body, name="outproj_bwd", grid=(T // tm,),
        in_specs=[BS((ns, tm, D), lambda i: (0, i, 0)), row(D), row(D), row(D), row(A_W + B_W + M_W), row(A_W), row(B_W),
                  row(M_W), vec(A_W), vec(B_W), vec(M_W), vec(D), vec(D), BS((A_W + B_W + M_W, D), lambda i: (0, 0))],
        out_specs=[row(D), BS((A_W + B_W + M_W, D), lambda i: (0, 0)), row(A_W), row(B_W), row(M_W),
                   vec(A_W), vec(B_W), vec(M_W), vec(D), vec(D)],
        out_shape=[sds((T, D), F32), sds((A_W + B_W + M_W, D), F32), sds((T, A_W), BF16), sds((T, B_W), BF16),
                   sds((T, M_W), BF16), sds((1, A_W), F32), sds((1, B_W), F32), sds((1, M_W), F32), sds((1, D), F32),
                   sds((1, D), F32)],
        compiler_params=_cp(("arbitrary",)),
    )(dh2, x1, dx2, o, y, ya, yb, ym, ga, gb, gm, g_post, g_pre2, w_out)


def _sgu_bwd(proj, dya, g_sgu, ws_tril, bs_full, tm):
    T = proj.shape[0]
    nch = tm // CHUNK

    def body(zu_ref, zv_ref, dy_ref, g_ref, ws_ref, b_ref, dzu_ref, dzv_ref, dws_ref, dbs_ref, dg_ref,
             du_ref, dvn_ref, dbf_ref):
        step = pl.program_id(0)
        first = step == 0
        lane = _iota((CHUNK, LANES), 1)
        tril = _iota((CHUNK, CHUNK), 0) >= _iota((CHUNK, CHUNK), 1)
        (u, vn), vjp = jax.vjp(_sgu_pre, zu_ref[...].astype(F32), zv_ref[...].astype(F32), g_ref[...])
        vnb = vn.astype(BF16)
        dy = dy_ref[...].astype(F32)

        @pl.when(first)
        def _():
            dws_ref[...] = jnp.zeros_like(dws_ref)
            dbf_ref[...] = jnp.zeros_like(dbf_ref)

        for c in range(nch):
            rs = slice(c * CHUNK, (c + 1) * CHUNK)
            for j in range(3):
                cs = slice(j * LANES, (j + 1) * LANES)
                vp = vnb[rs, cs]
                z = jnp.where(lane < HEAD, _dot(ws_ref[2 * j], vp), _dot(ws_ref[2 * j + 1], vp)) + b_ref[:, cs]
                du_ref[rs, cs] = dy[rs, cs] * z
                dz = dy[rs, cs] * u[rs, cs]
                dbf_ref[:, cs] += dz
                dzb = dz.astype(BF16)
                dz0 = jnp.where(lane < HEAD, dzb, jnp.zeros_like(dzb))
                dz1 = jnp.where(lane >= HEAD, dzb, jnp.zeros_like(dzb))
                dvn_ref[rs, cs] = jnp.where(lane < HEAD, _dot_tn(ws_ref[2 * j], dzb), _dot_tn(ws_ref[2 * j + 1], dzb))
                dws_ref[2 * j] += jnp.where(tril, _dot_nt(dz0, vp), 0.0)
                dws_ref[2 * j + 1] += jnp.where(tril, _dot_nt(dz1, vp), 0.0)
        dzu, dzv, dg = vjp((du_ref[...], dvn_ref[...]))
        dzu_ref[...] = dzu.astype(BF16)
        dzv_ref[...] = dzv.astype(BF16)
        _acc(dg_ref, dg, first)

        @pl.when(step == pl.num_programs(0) - 1)
        def _():
            out = jnp.zeros((CHUNK, LANES), F32)
            for j in range(3):
                slab = dbf_ref[:, j * LANES:(j + 1) * LANES]
                lo = jnp.sum(jnp.where(lane < HEAD, slab, 0.0), axis=1, keepdims=True)
                hi = jnp.sum(jnp.where(lane >= HEAD, slab, 0.0), axis=1, keepdims=True)
                out = out + jnp.where(lane == 2 * j, lo, 0.0) + jnp.where(lane == 2 * j + 1, hi, 0.0)
            dbs_ref[...] = out

    return pl.pallas_call(
        body, name="sgu_bwd", grid=(T // tm,),
        in_specs=[BS((tm, A_W), lambda i: (i, 0)), BS((tm, A_W), lambda i: (i, 1)), BS((tm, A_W), lambda i: (i, 0)),
                  BS((1, A_W), lambda i: (0, 0)), BS((6, CHUNK, CHUNK), lambda i: (0, 0, 0)),
                  BS((CHUNK, A_W), lambda i: (0, 0))],
        out_specs=[BS((tm, A_W), lambda i: (i, 0)), BS((tm, A_W), lambda i: (i, 0)),
                   BS((6, CHUNK, CHUNK), lambda i: (0, 0, 0)), BS((CHUNK, LANES), lambda i: (0, 0)),
                   BS((1, A_W), lambda i: (0, 0))],
        out_shape=[jax.ShapeDtypeStruct((T, A_W), BF16), jax.ShapeDtypeStruct((T, A_W), BF16),
                   jax.ShapeDtypeStruct((6, CHUNK, CHUNK), F32), jax.ShapeDtypeStruct((CHUNK, LANES), F32),
                   jax.ShapeDtypeStruct((1, A_W), F32)],
        scratch_shapes=[pltpu.VMEM((tm, A_W), F32), pltpu.VMEM((tm, A_W), F32), pltpu.VMEM((CHUNK, A_W), F32)],
        compiler_params=_cp(("arbitrary",)),
    )(proj, proj, dya, g_sgu, ws_tril, bs_full)


def _memattn_bwd(proj, kv, dym, Bl, S, tq):
    T = Bl * S
    nq = S // tq
    Mt = kv.shape[1]
    qc = 1920 // LANES

    def body(q_ref, km_ref, vm_ref, do_ref, dq_ref, dkm_ref, dvm_ref):
        first = pl.program_id(2) == 0
        lane = _iota((tq, LANES), 1)
        q = q_ref[...]
        do = do_ref[...]
        dq_out = jnp.zeros((tq, LANES), F32)
        dkm = jnp.zeros((Mt, LANES), F32)
        dvm = jnp.zeros((Mt, LANES), F32)
        for hh in range(2):
            hmask = (lane < HEAD) if hh == 0 else (lane >= HEAD)
            qs = jnp.where(hmask, q, jnp.zeros_like(q)) * 0.125
            dom = jnp.where(hmask, do, 0.0).astype(BF16)
            s = _dot_nt(qs, km_ref[0])
            pe = jnp.exp(s - jnp.max(s, axis=1, keepdims=True))
            pn = pe / jnp.sum(pe, axis=1, keepdims=True)
            dp = _dot_nt(dom, vm_ref[0])
            ds = (pn * (dp - jnp.sum(pn * dp, axis=1, keepdims=True))).astype(BF16)
            dq_out = jnp.where(hmask, _dot(ds, km_ref[0]) * 0.125, dq_out)
            dkm = dkm + _dot_tn(ds, qs)
            dvm = dvm + _dot_tn(pn, dom)
        dq_ref[...] = dq_out.astype(BF16)
        _acc(dkm_ref, dkm[None], first)
        _acc(dvm_ref, dvm[None], first)

    return pl.pallas_call(
        body, name="memattn_bwd", grid=(Bl, 2, nq),
        in_specs=[BS((tq, LANES), lambda b, p, i: (b * nq + i, qc + p)),
                  BS((1, Mt, LANES), lambda b, p, i: (b, 0, p)),
                  BS((1, Mt, LANES), lambda b, p, i: (b, 0, 2 + p)),
                  BS((tq, LANES), lambda b, p, i: (b * nq + i, p))],
        out_specs=[BS((tq, LANES), lambda b, p, i: (b * nq + i, p)),
                   BS((1, Mt, LANES), lambda b, p, i: (b, 0, p)),
                   BS((1, Mt, LANES), lambda b, p, i: (b, 0, p))],
        out_shape=[jax.ShapeDtypeStruct((T, M_W), BF16), jax.ShapeDtypeStruct((Bl, Mt, M_W), F32),
                   jax.ShapeDtypeStruct((Bl, Mt, M_W), F32)],
        compiler_params=_cp(("arbitrary", "arbitrary", "arbitrary")),
    )(proj, kv, kv, dym)


def _memkv_bwd(dkm, dvm, memn, mem, g_mem, w_kv):
    Bl, Mt, D = mem.shape

    def body(dk_ref, dv_ref, mn_ref, m_ref, g_ref, w_ref, dw_ref, dg_ref):
        first = pl.program_id(0) == 0
        dk = dk_ref[0].astype(BF16)
        dv = dv_ref[0].astype(BF16)
        mn = mn_ref[0]
        dmn = _dot_nt(dk, w_ref[:, 0:M_W]) + _dot_nt(dv, w_ref[:, M_W:])
        _, dg = _rms_bwd(m_ref[0], g_ref[...], dmn)
        _acc(dg_ref, dg, first)

        @pl.when(first)
        def _():
            dw_ref[...] = jnp.zeros_like(dw_ref)

        dw_ref[:, 0:M_W] += _dot_tn(mn, dk)
        dw_ref[:, M_W:] += _dot_tn(mn, dv)

    return pl.pallas_call(
        body, name="memkv_bwd", grid=(Bl,),
        in_specs=[BS((1, Mt, M_W), lambda b: (b, 0, 0)), BS((1, Mt, M_W), lambda b: (b, 0, 0)),
                  BS((1, Mt, D), lambda b: (b, 0, 0)), BS((1, Mt, D), lambda b: (b, 0, 0)),
                  BS((1, D), lambda b: (0, 0)), BS((D, 2 * M_W), lambda b: (0, 0))],
        out_specs=[BS((D, 2 * M_W), lambda b: (0, 0)), BS((1, D), lambda b: (0, 0))],
        out_shape=[jax.ShapeDtypeStruct((D, 2 * M_W), F32), jax.ShapeDtypeStruct((1, D), F32)],
        compiler_params=_cp(("arbitrary",)),
    )(dkm, dvm, memn, mem, g_mem, w_kv)


def _fox_bwd(proj, dyb, lse, bq, bk, Bl, S):
    T = Bl * S
    nq = S // Q_BLK
    nb = S // LANES
    qc, kc, vc = 768 // LANES, 1152 // LANES, 1536 // LANES

    def body(q_ref, k_ref, v_ref, do_ref, lse_ref, bq_ref, bk_ref,
             dq_ref, dk_ref, dv_ref, dcr_ref, ka_ref, dka_ref, dva_ref):
        p = pl.program_id(1)
        lane_s = _iota((S, LANES), 1)
        lane = _iota((Q_BLK, LANES), 1)
        sub = _iota((8, LANES), 0)
        tri = _iota((Q_BLK, Q_BLK), 1) <= _iota((Q_BLK, Q_BLK), 0)
        k = k_ref[...]
        for hh in range(2):
            data = (lane_s < HEAD) if hh == 0 else (lane_s >= HEAD)
            ka_ref[hh] = jnp.where(data, k, bk_ref[0, hh])
        dka_ref[...] = jnp.zeros_like(dka_ref)
        dva_ref[...] = jnp.zeros_like(dva_ref)

        @pl.when(p == 0)
        def _():
            dcr_ref[...] = jnp.zeros_like(dcr_ref)

        def add_colsums(ds, first_blk, h):
            cs = _colsum(ds)
            for jb in range(ds.shape[1] // LANES):
                dcr_ref[0, first_blk + jb] += jnp.where(sub == h, cs[:, jb * LANES:(jb + 1) * LANES], 0.0)

        for i in range(nq):
            r0 = i * Q_BLK
            r1 = r0 + Q_BLK
            q = q_ref[r0:r1, :]
            do = do_ref[r0:r1, :]
            lse_b = lse_ref[0, r0:r1, :]
            dq_out = jnp.zeros((Q_BLK, LANES), F32)
            for hh in range(2):
                hmask = (lane < HEAD) if hh == 0 else (lane >= HEAD)
                h = 2 * p + hh
                qs = jnp.where(hmask, q * 0.125, jnp.zeros_like(q))
                qa = jnp.where(hmask, q * 0.125, bq_ref[0, hh, r0:r1, :])
                dob = jnp.where(hmask, do, 0.0).astype(BF16)
                lse_h = jnp.sum(jnp.where(lane == hh * HEAD, lse_b, 0.0), axis=1, keepdims=True)
                pd = jnp.where(tri, jnp.exp(_dot_nt(qa, ka_ref[hh, r0:r1, :]) - lse_h), 0.0)
                dpd = _dot_nt(dob, v_ref[r0:r1, :])
                delta = jnp.sum(pd * dpd, axis=1, keepdims=True)
                psum = jnp.sum(pd, axis=1, keepdims=True)
                if i:
                    pf = jnp.exp(_dot_nt(qa, ka_ref[hh, 0:r0, :]) - lse_h)
                    dpf = _dot_nt(dob, v_ref[0:r0, :])
                    delta = delta + jnp.sum(pf * dpf, axis=1, keepdims=True)
                    psum = psum + jnp.sum(pf, axis=1, keepdims=True)
                delta = delta / psum
                dsd = pd * (dpd - delta)
                add_colsums(dsd, r0 // LANES, h)
                dsd = dsd.astype(BF16)
                dq_h = _dot(dsd, k_ref[r0:r1, :])
                dka_ref[r0:r1, :] += _dot_tn(dsd, qs)
                dva_ref[r0:r1, :] += _dot_tn(pd, dob)
                if i:
                    dsf = pf * (dpf - delta)
                    add_colsums(dsf, 0, h)
                    dsf = dsf.astype(BF16)
                    dq_h = dq_h + _dot(dsf, k_ref[0:r0, :])
                    dka_ref[0:r0, :] += _dot_tn(dsf, qs)
                    dva_ref[0:r0, :] += _dot_tn(pf, dob)
                dq_out = jnp.where(hmask, dq_h * 0.125, dq_out)
            dq_ref[r0:r1, :] = dq_out.astype(BF16)
        dk_ref[...] = dka_ref[...].astype(BF16)
        dv_ref[...] = dva_ref[...].astype(BF16)

    seq = lambda c0: BS((S, LANES), lambda b, p: (b, c0 + p))
    pair = BS((1, 2, S, LANES), lambda b, p: (b, p, 0, 0))
    rowblk = BS((1, nb, 8, LANES), lambda b, p: (b, 0, 0, 0))
    return pl.pallas_call(
        body, name="fox_bwd", grid=(Bl, 3),
        in_specs=[seq(qc), seq(kc), seq(vc), seq(0), BS((1, S, LANES), lambda b, p: (p, b, 0)), pair, pair],
        out_specs=[seq(0), seq(0), seq(0), rowblk],
        out_shape=[jax.ShapeDtypeStruct((T, B_W), BF16)] * 3 + [jax.ShapeDtypeStruct((Bl, nb, 8, LANES), F32)],
        scratch_shapes=[pltpu.VMEM((2, S, LANES), BF16), pltpu.VMEM((S, LANES), F32), pltpu.VMEM((S, LANES), F32)],
        compiler_params=_cp(("arbitrary", "arbitrary")),
    )(proj, proj, proj, dyb, lse, bq, bk)


def _gate_bwd(dc_row, fl_row):
    Bl, nb, _, _ = dc_row.shape

    def body(dc_ref, fl_ref, o_ref):
        lane = _iota((8, LANES), 1)

        carry = jnp.zeros((8, 1), F32)
        for j in reversed(range(nb)):
            r = -dc_ref[0, j]
            for k in (1, 2, 4, 8, 16, 32, 64):
                r = r + jnp.where(lane < LANES - k, pltpu.roll(r, LANES - k, 1), 0.0)
            total = jnp.sum(jnp.where(lane == 0, r, 0.0), axis=1, keepdims=True)
            dfl = (r + carry) * _sigmoid(-fl_ref[0, j])
            carry = carry + total
            o_ref[0, j * LANES:(j + 1) * LANES, :] = jnp.concatenate(
                [dfl, jnp.zeros((LANES - 8, LANES), F32)], axis=0).T

    rowblk = BS((1, nb, 8, LANES), lambda b: (b, 0, 0, 0))
    return pl.pallas_call(
        body, name="gate_bwd", grid=(Bl,),
        in_specs=[rowblk, rowblk],
        out_specs=BS((1, nb * LANES, LANES), lambda b: (b, 0, 0)),
        out_shape=jax.ShapeDtypeStruct((Bl, nb * LANES, LANES), F32),
        compiler_params=_cp(("arbitrary",)),
    )(dc_row, fl_row)


def _inproj_bwd(pieces, x2d, dx1, g_pre, w_in_p, tm):
    T, D = x2d.shape
    ns, _, dsh = w_in_p.shape

    def body(*refs):
        piece_refs = refs[:7]
        x_ref, dx1_ref, g_ref, w_ref, gx_ref, dg_ref, dbf_ref, dp_ref = refs[7:]
        first = pl.program_id(0) == 0
        _put_dproj(dp_ref, piece_refs)
        dh = jnp.concatenate([_dot(dp_ref[...], w_ref[s]) for s in range(ns)], axis=1)
        dxa, dg = _rms_bwd(x_ref[...], g_ref[...], dh)
        gx_ref[...] = dx1_ref[...] + dxa
        _acc(dg_ref, dg, first)
        _acc(dbf_ref, _colsum(piece_refs[5][...]), first)

    row = lambda w: BS((tm, w), lambda i: (i, 0))
    return pl.pallas_call(
        body, name="inproj_bwd", grid=(T // tm,),
        in_specs=[row(w) for _, w in DPROJ_PIECES] + [row(D), row(D), BS((1, D), lambda i: (0, 0)),
                                                      BS((ns, P_COLS, dsh), lambda i: (0, 0, 0))],
        out_specs=[row(D), BS((1, D), lambda i: (0, 0)), BS((1, LANES), lambda i: (0, 0))],
        out_shape=[jax.ShapeDtypeStruct((T, D), F32), jax.ShapeDtypeStruct((1, D), F32),
                   jax.ShapeDtypeStruct((1, LANES), F32)],
        scratch_shapes=[pltpu.VMEM((tm, P_COLS), BF16)],
        compiler_params=_cp(("arbitrary",)),
    )(*pieces, x2d, dx1, g_pre, w_in_p)


def _local_step(x, mem, target, W, P, reduce=None):
    Bl, S, D = x.shape
    T = Bl * S
    tm = min(512, T)
    x2d = x.reshape(T, D)
    t2d = target.reshape(T, D)
    vec = lambda a: a.reshape(1, -1)
    bf_row = jnp.pad(P["b_f"].reshape(1, -1), ((0, 0), (0, LANES - N_FOX_HEADS)))
    tril = jnp.tril(jnp.ones((CHUNK, CHUNK), bool))
    ws_tril = jnp.where(tril[None], P["w_s"][0], 0.0).astype(BF16)
    bs_full = jnp.repeat(P["b_s"][0].T, HEAD, axis=1)
    g_pre, g_sgu = vec(P["g_pre_mix"]), vec(P["g_sgu"])
    ga, gb, gm = vec(P["g_out_a"]), vec(P["g_out_b"]), vec(P["g_out_m"])
    g_mem, g_post, g_pre2, g_post2 = vec(P["g_mem"]), vec(P["g_post_mix"]), vec(P["g_pre_ffn"]), vec(P["g_post_ffn"])

    h, proj, flog = _inproj_fwd(x2d, g_pre, W["w_in"], tm)
    bq, bk, fl_row = _gate_fwd(flog.reshape(Bl, S, LANES), bf_row)
    ya = _sgu_fwd(proj, g_sgu, ws_tril, bs_full, tm)
    yb, lse = _fox_fwd(proj, bq, bk, Bl, S)
    memn, kv = _memkv_fwd(mem, g_mem, W["w_mem_kv"])
    ym = _memattn_fwd(proj, kv, Bl, S, min(2048, S))
    y, o, x1, h2 = _outproj_fwd(ya, yb, ym, x2d, ga, gb, gm, g_post, g_pre2, W["w_out"], tm)
    gs, us, dff, dx2, dg_post2, loss = _ffn_fwd(h2, x1, t2d, W["w_gate"], W["w_up"], W["w_down"], g_post2, tm)

    dh2, d_w_gate, d_w_up, d_w_down = _ffn_bwd(dff, h2, gs, us, W["w_gate"], W["w_up"], W["w_down"], min(1024, T))
    ffn = [d_w_gate, d_w_up, d_w_down]
    if reduce is not None:
        pending, _ = reduce.begin("ffn", ffn)
    dx1, d_w_out, dya, dyb, dym, dga, dgb, dgm, dg_post, dg_pre2 = _outproj_bwd(
        dh2, x1, dx2, o, y, ya, yb, ym, ga, gb, gm, g_post, g_pre2, W["w_out"], tm)
    if reduce is not None:
        ffn, (dya, dyb, dym) = reduce.finish("ffn", pending, (dya, dyb, dym))
    dzu, dzv, dws, dbs_cols, dg_sgu = _sgu_bwd(proj, dya, g_sgu, ws_tril, bs_full, tm)
    dqm, dkm, dvm = _memattn_bwd(proj, kv, dym, Bl, S, min(2048, S))
    d_w_kv, dg_mem = _memkv_bwd(dkm, dvm, memn, mem, g_mem, W["w_mem_kv"])
    mid = [d_w_kv, d_w_out]
    dq, dk, dv, dc_row = _fox_bwd(proj, dyb, lse, bq, bk, Bl, S)
    if reduce is not None:
        done = reduce.apply(BIG[3:], ffn)
        pending, after = reduce.begin("mid", mid, (dc_row,) + done)
        dc_row = after[0]
    dfl = _gate_bwd(dc_row, fl_row).reshape(T, LANES)
    if reduce is not None:
        mid, (dfl,) = reduce.finish("mid", pending, (dfl,), first=(dzu, dzv))
    pieces = (dzu, dzv, dq, dk, dv, dfl, dqm)
    d_w_in = _dw_in(pieces, h, W["w_in"].shape[0], 1024)
    if reduce is None:
        big = dict(zip(BIG, [d_w_in] + mid + ffn))
    else:
        done = reduce.apply(BIG[1:3], mid)
        pending, after = reduce.begin("in", [d_w_in], done + (dx1,))
        dx1 = after[-1]
        big = {"w_in": pending}
    grad_x, dg_pre, dbf = _inproj_bwd(pieces, x2d, dx1, g_pre, W["w_in"], tm)
    small = {"g_pre_mix": dg_pre, "b_f": dbf[:, :N_FOX_HEADS], "g_sgu": dg_sgu, "w_s": dws, "b_s": dbs_cols[:, :N_FOX_HEADS].T,
             "g_out_a": dga, "g_out_b": dgb, "g_out_m": dgm, "g_mem": dg_mem, "g_post_mix": dg_post,
             "g_pre_ffn": dg_pre2, "g_post_ffn": dg_post2, "loss": loss[:, :1]}
    return grad_x.reshape(Bl, S, D), big, small


def _place():
    return lax.axis_index("x"), lax.axis_index("y"), lax.axis_index("c")


def _exchange_on_sequencer(srcs, own_full, name, collective_id):
    n = len(srcs)

    def body(*refs):
        src, dst = refs[:n], refs[n:2 * n]
        lsem, isend, irecv, dsend, drecv = refs[2 * n:]
        x, y, c = _place()
        oc = 1 - c
        s_me = 2 * x + y
        sib = (x, y, oc)
        chips = [(1 - x, y), (x, 1 - y), (1 - x, 1 - y)]
        barrier = pltpu.get_barrier_semaphore()
        for dev in [(cx, cy, c) for cx, cy in chips] + [sib]:
            pl.semaphore_signal(barrier, inc=1, device_id=dev, device_id_type=MESH)
        pl.semaphore_wait(barrier, 4)

        def remote(a, b, ssem, rsem, dev):
            return pltpu.make_async_remote_copy(src_ref=a, dst_ref=b, send_sem=ssem, recv_sem=rsem,
                                                device_id=dev, device_id_type=MESH)

        sends, local = [], []
        for w in range(n):
            for j, (cx, cy) in enumerate(chips):
                half = src[w].at[c] if own_full else src[w].at[2 * cx + cy]
                cp = remote(half, dst[w].at[s_me, c], isend.at[w, j], irecv.at[w, j], (cx, cy, c))
                cp.start()
                sends.append(cp)
            if own_full:
                cp = remote(src[w], dst[w].at[s_me], dsend.at[w, 3], drecv.at[w, 3], sib)
            else:
                cp = remote(src[w].at[s_me], dst[w].at[s_me, c], dsend.at[w, 3], drecv.at[w, 3], sib)
                loc = pltpu.make_async_copy(src[w].at[s_me], dst[w].at[s_me, c], lsem.at[w])
                loc.start()
                local.append(loc)
            cp.start()
            sends.append(cp)
        for w in range(n):
            for j, (cx, cy) in enumerate(chips):
                landed = dst[w].at[2 * cx + cy, c]
                remote(landed, landed, isend.at[w, j], irecv.at[w, j], (cx, cy, c)).wait_recv()
                cp = remote(landed, landed, dsend.at[w, j], drecv.at[w, j], sib)
                cp.start()
                sends.append(cp)
        for w in range(n):
            for j, (cx, cy) in enumerate(chips):
                landed = dst[w].at[2 * cx + cy, oc]
                remote(landed, landed, dsend.at[w, j], drecv.at[w, j], sib).wait_recv()
            landed = dst[w].at[s_me] if own_full else dst[w].at[s_me, oc]
            remote(landed, landed, dsend.at[w, 3], drecv.at[w, 3], sib).wait_recv()
        for cp in sends:
            cp.wait_send()
        for loc in local:
            loc.wait()

    return pl.kernel(
        body, out_type=[jax.ShapeDtypeStruct((4, 2) + s.shape[1:], s.dtype) for s in srcs],
        mesh=plsc.ScalarSubcoreMesh(axis_name="sequencer", num_cores=1), name=name,
        scratch_types=[pltpu.SemaphoreType.DMA((n,)), pltpu.SemaphoreType.DMA((n, 3)), pltpu.SemaphoreType.DMA((n, 3)),
                       pltpu.SemaphoreType.DMA((n, 4)), pltpu.SemaphoreType.DMA((n, 4))],
        compiler_params=pltpu.CompilerParams(collective_id=collective_id),
    )(*srcs)


def _sibling_swap(grads, name, collective_id):
    n = len(grads)

    def body(*refs):
        g, theirs = refs[:n], refs[n:2 * n]
        ssem, rsem = refs[2 * n:]
        x, y, c = _place()
        sib = (x, y, 1 - c)
        barrier = pltpu.get_barrier_semaphore()
        pl.semaphore_signal(barrier, inc=1, device_id=sib, device_id_type=MESH)
        pl.semaphore_wait(barrier, 1)
        cps = []
        for w in range(n):
            cp = pltpu.make_async_remote_copy(src_ref=g[w].at[:, 1 - c], dst_ref=theirs[w], send_sem=ssem.at[w],
                                              recv_sem=rsem.at[w], device_id=sib, device_id_type=MESH)
            cp.start()
            cps.append(cp)
        for cp in cps:
            cp.wait()

    return pl.kernel(
        body, out_type=[jax.ShapeDtypeStruct((4,) + g.shape[2:], g.dtype) for g in grads],
        mesh=plsc.ScalarSubcoreMesh(axis_name="sequencer", num_cores=1), name=name,
        scratch_types=[pltpu.SemaphoreType.DMA((n,)), pltpu.SemaphoreType.DMA((n,))],
        compiler_params=pltpu.CompilerParams(collective_id=collective_id),
    )(*grads)


def _add_pairs(core, gs, theirs, name):
    n = len(gs)

    def body(core_ref, *refs):
        for g_ref, t_ref, o_ref in zip(refs[:n], refs[n:2 * n], refs[2 * n:]):
            o_ref[...] = (g_ref[:, 0].astype(F32) + t_ref[...].astype(F32)).astype(BF16)

    def specs(g):
        _, _, hr, C = g.shape
        return (BS((2, 1, hr, C), lambda s, core_ref: (s, core_ref[0], 0, 0)), BS((2, hr, C), lambda s, core_ref: (s, 0, 0)))

    return pl.pallas_call(
        body, name=name,
        grid_spec=pltpu.PrefetchScalarGridSpec(
            num_scalar_prefetch=1, grid=(2,),
            in_specs=[specs(g)[0] for g in gs] + [specs(g)[1] for g in gs], out_specs=[specs(g)[1] for g in gs]),
        out_shape=[jax.ShapeDtypeStruct(t.shape, BF16) for t in theirs],
        compiler_params=_cp(("arbitrary",)))(core, *gs, *theirs)


def _sum_chips(r, name):
    _, _, hr, C = r.shape

    def body(r_ref, o_ref):
        o_ref[...] = ((r_ref[0, 0].astype(F32) + r_ref[1, 0].astype(F32)) + r_ref[2, 0].astype(F32)) + r_ref[3, 0].astype(F32)

    return pl.pallas_call(body, name=name, grid=(2,), in_specs=[BS((4, 1, hr, C), lambda h: (0, h, 0, 0))],
                          out_specs=BS((hr, C), lambda h: (h, 0)), out_shape=jax.ShapeDtypeStruct((2 * hr, C), F32),
                          compiler_params=_cp(("arbitrary",)))(r)


class _Reducer:
    IDS = {"ffn": (4, 5), "mid": (6, 7), "in": (8, 9)}

    def __init__(self, core, apply):
        self.core = core
        self.apply = apply

    def begin(self, tag, grads, after=()):
        grads, after = lax.optimization_barrier((list(grads), after))
        g4 = [g.reshape(4, 2, -1, g.shape[-1]) for g in grads]
        return (g4, _sibling_swap(g4, "swap_" + tag, self.IDS[tag][0])), after

    def finish(self, tag, pending, hold, first=()):
        pending, first = lax.optimization_barrier((pending, first))
        g4, theirs = pending
        sums = _add_pairs(self.core, g4, theirs, "chip_sum_" + tag)
        sums, hold = lax.optimization_barrier((sums, hold))
        return _exchange_on_sequencer(sums, False, "scatter_" + tag, self.IDS[tag][1]), hold


def _small_allreduce(part):
    R = part.shape[0]
    rs = R // 8
    masks = [(mx, my, mc) for mx in (0, 1) for my in (0, 1) for mc in (0, 1)][1:]

    def body(p_ref, o_ref, buf_ref, s1, r1, s2, r2):
        x, y, c = _place()
        d = 4 * x + 2 * y + c
        mine = pl.ds(pl.multiple_of(d * rs, 8), rs)
        peers = [((x + mx) % 2, (y + my) % 2, (c + mc) % 2) for mx, my, mc in masks]
        first, second = [], []
        for k, (px, py, pc) in enumerate(peers):
            theirs = pl.ds(pl.multiple_of((4 * px + 2 * py + pc) * rs, 8), rs)
            cp = pltpu.make_async_remote_copy(src_ref=p_ref.at[theirs, :], dst_ref=buf_ref.at[d], send_sem=s1.at[k],
                                              recv_sem=r1.at[k], device_id=(px, py, pc), device_id_type=MESH)
            cp.start()
            first.append(cp)
        buf_ref[d] = p_ref[mine, :]
        for k, (px, py, pc) in enumerate(peers):
            slot = buf_ref.at[4 * px + 2 * py + pc]
            pltpu.make_async_remote_copy(src_ref=slot, dst_ref=slot, send_sem=s1.at[k], recv_sem=r1.at[k],
                                         device_id=(px, py, pc), device_id_type=MESH).wait_recv()
        total = buf_ref[0]
        for k in range(1, 8):
            total = total + buf_ref[k]
        o_ref[mine, :] = total
        for k, (px, py, pc) in enumerate(peers):
            cp = pltpu.make_async_remote_copy(src_ref=o_ref.at[mine, :], dst_ref=o_ref.at[mine, :], send_sem=s2.at[k],
                                              recv_sem=r2.at[k], device_id=(px, py, pc), device_id_type=MESH)
            cp.start()
            second.append(cp)
        for k, (px, py, pc) in enumerate(peers):
            rows = o_ref.at[pl.ds(pl.multiple_of((4 * px + 2 * py + pc) * rs, 8), rs), :]
            pltpu.make_async_remote_copy(src_ref=rows, dst_ref=rows, send_sem=s2.at[k], recv_sem=r2.at[k],
                                         device_id=(px, py, pc), device_id_type=MESH).wait_recv()
        for cp in first + second:
            cp.wait_send()

    vm = pl.BlockSpec(memory_space=pltpu.VMEM)
    return pl.pallas_call(
        body, name="small_allreduce", in_specs=[vm], out_specs=vm, out_shape=jax.ShapeDtypeStruct(part.shape, F32),
        scratch_shapes=[pltpu.VMEM((8, rs, LANES), F32)] + [pltpu.SemaphoreType.DMA((7,))] * 4,
    )(part)


def _adamw(w, g, m, v, name):
    R, C = w.shape
    summed = g.ndim == 4
    if summed:
        tr = R // 2
    else:
        tr = R if R * C * 4 <= (1 << 21) else R // 2
        if tr % 8:
            tr = R
    c1 = 1.0 / (1.0 - ADAM_B1 ** ADAM_STEP)
    c2 = 1.0 / (1.0 - ADAM_B2 ** ADAM_STEP)

    def body(w_ref, g_ref, m_ref, v_ref, *outs):
        if summed:
            g_ = ((g_ref[0, 0].astype(F32) + g_ref[1, 0].astype(F32)) + g_ref[2, 0].astype(F32)) + g_ref[3, 0].astype(F32)
            outs[0][...] = g_
        else:
            g_ = g_ref[...]
        d_ref, mo_ref, vo_ref = outs[-3:]
        m_ = ADAM_B1 * m_ref[...] + (1.0 - ADAM_B1) * g_
        v_ = ADAM_B2 * v_ref[...] + (1.0 - ADAM_B2) * (g_ * g_)
        mo_ref[...] = m_
        vo_ref[...] = v_
        d_ref[...] = -ADAM_LR * ((m_ * c1) / (jnp.sqrt(v_ * c2) + ADAM_EPS) + ADAM_WD * w_ref[...])

    blk = BS((tr, C), lambda i: (i, 0))
    g_blk = BS((4, 1, tr, C), lambda i: (0, i, 0, 0)) if summed else blk
    nout = 4 if summed else 3
    return pl.pallas_call(body, name=name, grid=(R // tr,), in_specs=[blk, g_blk, blk, blk], out_specs=[blk] * nout,
                          out_shape=[jax.ShapeDtypeStruct((R, C), F32)] * nout,
                          compiler_params=_cp(("arbitrary",)))(w, g, m, v)


def _adamw_unit_rows(w, g, m, v, name):
    C, _, R = w.shape
    tc = C // 2 if C % 2 == 0 else C
    c1 = 1.0 / (1.0 - ADAM_B1 ** ADAM_STEP)
    c2 = 1.0 / (1.0 - ADAM_B2 ** ADAM_STEP)

    def body(w_ref, g_ref, m_ref, v_ref, go_ref, d_ref, mo_ref, vo_ref):
        g_ = g_ref[...]
        go_ref[...] = g_
        m_ = ADAM_B1 * m_ref[...] + (1.0 - ADAM_B1) * g_
        v_ = ADAM_B2 * v_ref[...] + (1.0 - ADAM_B2) * (g_ * g_)
        mo_ref[...] = m_
        vo_ref[...] = v_
        d_ref[...] = -ADAM_LR * ((m_ * c1) / (jnp.sqrt(v_ * c2) + ADAM_EPS) + ADAM_WD * w_ref[...])

    blk = BS((tc, 1, R), lambda i: (i, 0, 0))
    return pl.pallas_call(body, name=name, grid=(C // tc,), in_specs=[blk] * 4, out_specs=[blk] * 4,
                          out_shape=[jax.ShapeDtypeStruct((C, 1, R), F32)] * 4,
                          compiler_params=_cp(("arbitrary",)))(w, g, m, v)


SMALL = ("g_pre_mix", "b_f", "g_sgu", "w_s", "b_s", "g_out_a", "g_out_b", "g_out_m", "g_mem", "g_post_mix",
         "g_pre_ffn", "g_post_ffn")
BIG = ("w_in", "w_mem_kv", "w_out", "w_gate", "w_up", "w_down")
TRANSPOSED = ("w_in", "w_gate", "w_up")
WEIGHTS = ("g_pre_mix", "w_in", "b_f", "g_sgu", "w_s", "b_s", "g_out_a", "g_out_b", "g_out_m", "g_mem", "w_mem_kv",
           "w_out", "g_post_mix", "g_pre_ffn", "w_gate", "w_up", "w_down", "g_post_ffn")


VECTORS = ("g_pre_mix", "b_f", "g_sgu", "g_out_a", "g_out_b", "g_out_m", "g_mem", "g_post_mix", "g_pre_ffn", "g_post_ffn")
VEC_ROWS = 16
WS_ROWS = N_FOX_HEADS * CHUNK
BS_ROWS = 8


def _pack_small(small, vw):
    stack = jnp.zeros((VEC_ROWS, vw), F32)
    for k, n in enumerate(VECTORS + ("loss",)):
        row = small[n].reshape(1, -1)
        stack = stack + jnp.pad(row, ((k, VEC_ROWS - 1 - k), (0, vw - row.shape[1])))
    parts = [small["w_s"].reshape(WS_ROWS, LANES), jnp.pad(small["b_s"], ((0, BS_ROWS - N_FOX_HEADS), (0, 0))),
             stack.reshape(-1, LANES)]
    rows = sum(p.shape[0] for p in parts)
    return jnp.concatenate(parts + [jnp.zeros((-rows % 64, LANES), F32)], axis=0)


def _adamw_small(vec_g, vec_wmv, ws, bs):
    c1 = 1.0 / (1.0 - ADAM_B1 ** ADAM_STEP)
    c2 = 1.0 / (1.0 - ADAM_B2 ** ADAM_STEP)
    nv = len(vec_wmv)

    def adam(g, w, m, v):
        m_ = ADAM_B1 * m + (1.0 - ADAM_B1) * g
        v_ = ADAM_B2 * v + (1.0 - ADAM_B2) * (g * g)
        return -ADAM_LR * ((m_ * c1) / (jnp.sqrt(v_ * c2) + ADAM_EPS) + ADAM_WD * w), m_, v_

    def body(*refs):
        vg_ref, ins, outs = refs[0], refs[1:1 + 3 * nv + 8], refs[1 + 3 * nv + 8:]
        for k in range(nv):
            w_ref, m_ref, v_ref = ins[3 * k:3 * k + 3]
            g = vg_ref[k:k + 1, 0:w_ref.shape[1]]
            d, m_, v_ = adam(g, w_ref[...], m_ref[...], v_ref[...])
            for o_ref, val in zip(outs[4 * k:4 * k + 4], (g, d, m_, v_)):
                o_ref[...] = val
        for j in range(2):
            g_ref, w_ref, m_ref, v_ref = ins[3 * nv + 4 * j:3 * nv + 4 * j + 4]
            for o_ref, val in zip(outs[4 * nv + 3 * j:4 * nv + 3 * j + 3], adam(g_ref[...], w_ref[...], m_ref[...], v_ref[...])):
                o_ref[...] = val

    vm = pl.BlockSpec(memory_space=pltpu.VMEM)
    operands = [vec_g] + [a for wmv in vec_wmv for a in wmv] + list(ws) + list(bs)
    out_shape = ([jax.ShapeDtypeStruct(wmv[0].shape, F32) for wmv in vec_wmv for _ in range(4)]
                 + [jax.ShapeDtypeStruct(ws[1].shape, F32)] * 3 + [jax.ShapeDtypeStruct(bs[1].shape, F32)] * 3)
    outs = pl.pallas_call(body, name="adamw_small", in_specs=[vm] * len(operands), out_specs=[vm] * len(out_shape),
                          out_shape=out_shape)(*operands)
    return [outs[4 * k:4 * k + 4] for k in range(nv)], outs[4 * nv:4 * nv + 3], outs[4 * nv + 3:]


def kernel(x, mem, g_pre_mix, w_in, b_f, g_sgu, w_s, b_s, g_out_a, g_out_b, g_out_m, g_mem, w_mem_kv, w_out, g_post_mix, g_pre_ffn, w_gate, w_up, w_down, g_post_ffn, loss_target, m_g_pre_mix, m_w_in, m_b_f, m_g_sgu, m_w_s, m_b_s, m_g_out_a, m_g_out_b, m_g_out_m, m_g_mem, m_w_mem_kv, m_w_out, m_g_post_mix, m_g_pre_ffn, m_w_gate, m_w_up, m_w_down, m_g_post_ffn, v_g_pre_mix, v_w_in, v_b_f, v_g_sgu, v_w_s, v_b_s, v_g_out_a, v_g_out_b, v_g_out_m, v_g_mem, v_w_mem_kv, v_w_out, v_g_post_mix, v_g_pre_ffn, v_w_gate, v_w_up, v_w_down, v_g_post_ffn):
    Wt = dict(g_pre_mix=g_pre_mix, w_in=w_in, b_f=b_f, g_sgu=g_sgu, w_s=w_s, b_s=b_s, g_out_a=g_out_a, g_out_b=g_out_b,
              g_out_m=g_out_m, g_mem=g_mem, w_mem_kv=w_mem_kv, w_out=w_out, g_post_mix=g_post_mix, g_pre_ffn=g_pre_ffn,
              w_gate=w_gate, w_up=w_up, w_down=w_down, g_post_ffn=g_post_ffn)
    Mo = dict(g_pre_mix=m_g_pre_mix, w_in=m_w_in, b_f=m_b_f, g_sgu=m_g_sgu, w_s=m_w_s, b_s=m_b_s, g_out_a=m_g_out_a,
              g_out_b=m_g_out_b, g_out_m=m_g_out_m, g_mem=m_g_mem, w_mem_kv=m_w_mem_kv, w_out=m_w_out,
              g_post_mix=m_g_post_mix, g_pre_ffn=m_g_pre_ffn, w_gate=m_w_gate, w_up=m_w_up, w_down=m_w_down,
              g_post_ffn=m_g_post_ffn)
    Vo = dict(g_pre_mix=v_g_pre_mix, w_in=v_w_in, b_f=v_b_f, g_sgu=v_g_sgu, w_s=v_w_s, b_s=v_b_s, g_out_a=v_g_out_a,
              g_out_b=v_g_out_b, g_out_m=v_g_out_m, g_mem=v_g_mem, w_mem_kv=v_w_mem_kv, w_out=v_w_out,
              g_post_mix=v_g_post_mix, g_pre_ffn=v_g_pre_ffn, w_gate=v_w_gate, w_up=v_w_up, w_down=v_w_down,
              g_post_ffn=v_g_post_ffn)

    gap = P_COLS - IN_COLS

    def to_kernel(n, w):
        if n in TRANSPOSED:
            w = w.T
        if n == "w_in":
            w = jnp.pad(w[:F_END], ((0, P_COLS - F_END), (0, 0))) + jnp.pad(w[F_END:], ((F_END + gap, 0), (0, 0)))
        return w

    def ungroup(g):
        return jnp.pad(g[:F_END], ((0, IN_COLS - F_END), (0, 0))) + jnp.pad(g[F_END + gap:], ((F_END, 0), (0, 0)))

    shards = {n: to_kernel(n, Wt[n][0]) for n in BIG}
    srcs = [shards[n].astype(BF16).reshape(2, shards[n].shape[0] // 2, shards[n].shape[1]) for n in BIG]
    fulls = (_exchange_on_sequencer(srcs[:1], True, "gather_w_in", 1)
             + _exchange_on_sequencer(srcs[1:3], True, "gather_kv_out", 2)
             + _exchange_on_sequencer(srcs[3:], True, "gather_ffn", 3))
    W = {}
    for n, f in zip(BIG, fulls):
        _, _, hr, C = f.shape
        W[n] = f.reshape(8 * hr, C) if n in ("w_mem_kv", "w_out") else f.reshape(4, 2 * hr, C)

    P = {n: Wt[n] for n in SMALL}
    grads, deltas, new_m, new_v = {}, {}, {}, {}

    def apply(names, landed):
        for n, r in zip(names, landed):
            if n == "w_in":
                g_t = ungroup(_sum_chips(r, "sum_chips_" + n))
                lift = lambda a: jnp.transpose(a, (2, 0, 1))
                outs = _adamw_unit_rows(lift(Wt[n]), g_t[:, None, :], lift(Mo[n]), lift(Vo[n]), "adamw_" + n)
                g, d, m1, v1 = [jnp.transpose(a, (1, 2, 0))[0] for a in outs]
            elif n in TRANSPOSED:
                g, d, m1, v1 = [a.T for a in _adamw(Wt[n][0].T, r, Mo[n][0].T, Vo[n][0].T, "adamw_" + n)]
            else:
                g, d, m1, v1 = _adamw(Wt[n][0], r, Mo[n][0], Vo[n][0], "adamw_" + n)
            grads[n], deltas[n], new_m[n], new_v[n] = g[None], d[None], m1[None], v1[None]
        return tuple(deltas[n] for n in names)

    core = lax.axis_index("c").astype(jnp.int32).reshape(1)
    reducer = _Reducer(core, apply)
    grad_x, pending, small = _local_step(x, mem, loss_target, W, P, reducer)

    vw = -(-max(x.shape[-1], A_W) // LANES) * LANES
    landed, (packed,) = reducer.finish("in", pending["w_in"], (_pack_small(small, vw),))
    total = _small_allreduce(packed)
    apply(BIG[:1], landed)

    lane_row = lambda a: jnp.pad(a, ((0, 0), (0, -a.shape[1] % LANES)))
    vec_g = total[WS_ROWS + BS_ROWS:WS_ROWS + BS_ROWS + VEC_ROWS * vw // LANES].reshape(VEC_ROWS, vw)
    ws_g = total[:WS_ROWS]
    bs_g = total[WS_ROWS:WS_ROWS + N_FOX_HEADS]
    rows = lambda a, r: a.reshape(r, LANES)
    per_vec, ws_out, bs_out = _adamw_small(
        vec_g, [tuple(lane_row(a[n]) for a in (Wt, Mo, Vo)) for n in VECTORS],
        (ws_g,) + tuple(rows(a["w_s"], WS_ROWS) for a in (Wt, Mo, Vo)),
        (bs_g,) + tuple(rows(a["b_s"], N_FOX_HEADS) for a in (Wt, Mo, Vo)))
    for n, outs in zip(VECTORS, per_vec):
        grads[n], deltas[n], new_m[n], new_v[n] = [o[:, :Wt[n].shape[1]] for o in outs]
    for n, g, outs in (("w_s", ws_g, ws_out), ("b_s", bs_g, bs_out)):
        grads[n], deltas[n], new_m[n], new_v[n] = [o.reshape(Wt[n].shape) for o in (g,) + tuple(outs)]
    loss = vec_g[len(VECTORS), 0]

    return (loss, grad_x, *[grads[n] for n in WEIGHTS], *[deltas[n] for n in WEIGHTS],
            *[new_m[n] for n in WEIGHTS], *[new_v[n] for n in WEIGHTS])
```

```python
import jax
import jax.numpy as jnp
from jax import lax
from jax.experimental import pallas as pl
from jax.experimental.pallas import tpu as pltpu
from jax.experimental.pallas import tpu_sc as plsc

F32 = jnp.float32
BF16 = jnp.bfloat16
EPS = 1e-6
NEG = -1e30
HEAD = 64
A_W, B_W, M_W = 384, 384, 256
N_FOX_HEADS = 6
CHUNK = 128
IN_COLS = 2 * A_W + 3 * B_W + N_FOX_HEADS + M_W
P_MAIN = 2 * A_W + 3 * B_W + M_W
LANES = 128
P_COLS = P_MAIN + LANES
F_END = 2 * A_W + 3 * B_W + N_FOX_HEADS
Q_BLK = 512
ROW_SPLIT = 4
ADAM_LR, ADAM_B1, ADAM_B2, ADAM_EPS, ADAM_WD, ADAM_STEP = 0.001, 0.9, 0.999, 1e-08, 0.01, 10
VMEM_LIMIT = 56 * 1024 * 1024
MESH = pl.DeviceIdType.MESH
BS = pl.BlockSpec


def _cp(sem=None):
    return pltpu.CompilerParams(dimension_semantics=sem, vmem_limit_bytes=VMEM_LIMIT)


def _iota(shape, dim):
    return lax.broadcasted_iota(jnp.int32, shape, dim)


def _dot(a, b):
    return jnp.dot(a.astype(BF16), b.astype(BF16), preferred_element_type=F32)


def _dot_nt(a, b):
    return lax.dot_general(a.astype(BF16), b.astype(BF16), (((1,), (1,)), ((), ())), preferred_element_type=F32)


def _dot_tn(a, b):
    return lax.dot_general(a.astype(BF16), b.astype(BF16), (((0,), (0,)), ((), ())), preferred_element_type=F32)


def _rms(x, g):
    return x * lax.rsqrt(jnp.mean(x * x, axis=-1, keepdims=True) + EPS) * g


def _rms_bwd(x, g, dy):
    r = lax.rsqrt(jnp.mean(x * x, axis=-1, keepdims=True) + EPS)
    xr = x * r
    gd = dy * g
    m = jnp.mean(gd * xr, axis=-1, keepdims=True)
    return (gd - xr * m) * r, _colsum(dy * xr)


def _gelu(x):
    return 0.5 * x * (1.0 + jnp.tanh(0.7978845608028654 * (x + 0.044715 * (x * x * x))))


def _sigmoid(x):
    return 1.0 / (1.0 + jnp.exp(-x))


def _silu_mul(g, u):
    return g * _sigmoid(g) * u


def _logsig(x):
    return jnp.minimum(x, 0.0) - jnp.log(1.0 + jnp.exp(-jnp.abs(x)))


def _colsum(x):
    return jnp.sum(x, axis=0, keepdims=True)


def _acc(ref, val, first):
    @pl.when(first)
    def _():
        ref[...] = val

    @pl.when(jnp.logical_not(first))
    def _():
        ref[...] += val


def _inproj_fwd(x2d, g_pre, w_in_p, tm):
    T, D = x2d.shape
    CH = 768
    nchunk = P_COLS // CH
    ns, _, dsh = w_in_p.shape

    def body(x_ref, g_ref, w_ref, h_ref, proj_ref, fl_ref):
        h = _rms(x_ref[...], g_ref[...]).astype(BF16)
        h_ref[...] = h
        for n in range(nchunk):
            rows = slice(n * CH, (n + 1) * CH)
            r = _dot_nt(h[:, 0:dsh], w_ref[0, rows, :])
            for s in range(1, ns):
                r = r + _dot_nt(h[:, s * dsh:(s + 1) * dsh], w_ref[s, rows, :])
            if n < nchunk - 1:
                proj_ref[:, rows] = r.astype(BF16)
            else:
                fg = 1920 - n * CH
                proj_ref[:, n * CH:1920] = r[:, :fg].astype(BF16)
                fl_ref[...] = r[:, fg:fg + LANES]
                proj_ref[:, 1920:P_MAIN] = r[:, fg + LANES:].astype(BF16)

    return pl.pallas_call(
        body, name="inproj_fwd", grid=(T // tm,),
        in_specs=[BS((tm, D), lambda i: (i, 0)), BS((1, D), lambda i: (0, 0)),
                  BS((ns, P_COLS, dsh), lambda i: (0, 0, 0))],
        out_specs=[BS((tm, D), lambda i: (i, 0)), BS((tm, P_MAIN), lambda i: (i, 0)), BS((tm, LANES), lambda i: (i, 0))],
        out_shape=[jax.ShapeDtypeStruct((T, D), BF16), jax.ShapeDtypeStruct((T, P_MAIN), BF16),
                   jax.ShapeDtypeStruct((T, LANES), F32)],
        compiler_params=_cp(("arbitrary",)),
    )(x2d, g_pre, w_in_p)


def _gate_fwd(flog3, bf_row):
    Bl, S, _ = flog3.shape
    nb = S // LANES

    def body(f_ref, b_ref, bq_ref, bk_ref, fr_ref):
        row = _iota((LANES, LANES), 0)
        lane = _iota((LANES, LANES), 1)
        one = jnp.ones((LANES, LANES), BF16)
        zero = jnp.zeros((LANES, LANES), BF16)

        carry = jnp.zeros((1, LANES), F32)
        for j in range(nb):
            r0 = j * LANES
            fl = f_ref[0, pl.ds(r0, LANES), :] + b_ref[...]
            fr_ref[0, j] = fl.T[0:8, :]
            c = _logsig(fl)
            for k in (1, 2, 4, 8, 16, 32, 64):
                c = c + jnp.where(row >= k, pltpu.roll(c, k, 0), 0.0)
            total = _colsum(jnp.where(row == LANES - 1, c, 0.0))
            c = c + carry
            carry = carry + total
            for h in range(N_FOX_HEADS):
                col = jnp.sum(jnp.where(lane == h, c, 0.0), axis=1, keepdims=True)
                hi = col.astype(BF16)
                rest = col - hi.astype(F32)
                mid = rest.astype(BF16)
                lo = (rest - mid.astype(F32)).astype(BF16)
                base = _bias_lane(h)
                bq = jnp.where(lane == base, hi, jnp.where(lane == base + 1, mid, jnp.where(lane == base + 2, lo, zero)))
                bq = jnp.where((lane >= base + 3) & (lane < base + 6), one, bq)
                bk = jnp.where(lane == base + 3, -hi, jnp.where(lane == base + 4, -mid, jnp.where(lane == base + 5, -lo, zero)))
                bk = jnp.where((lane >= base) & (lane < base + 3), one, bk)
                bq_ref[0, h, pl.ds(r0, LANES), :] = bq
                bk_ref[0, h, pl.ds(r0, LANES), :] = bk

    slab = BS((1, N_FOX_HEADS, S, LANES), lambda b: (b, 0, 0, 0))
    return pl.pallas_call(
        body, name="gate_fwd", grid=(Bl,),
        in_specs=[BS((1, S, LANES), lambda b: (b, 0, 0)), BS((1, LANES), lambda b: (0, 0))],
        out_specs=[slab, slab, BS((1, nb, 8, LANES), lambda b: (b, 0, 0, 0))],
        out_shape=[jax.ShapeDtypeStruct((Bl, N_FOX_HEADS, S, LANES), BF16),
                   jax.ShapeDtypeStruct((Bl, N_FOX_HEADS, S, LANES), BF16),
                   jax.ShapeDtypeStruct((Bl, nb, 8, LANES), F32)],
        compiler_params=_cp(("arbitrary",)),
    )(flog3, bf_row)


def _bias_lane(h):
    return HEAD if h % 2 == 0 else 0


def _sgu_pre(zu, zv, g_sgu):
    return _gelu(zu), _rms(_gelu(zv), g_sgu)


def _sgu_fwd(proj, g_sgu, ws_tril, bs_full, tm):
    T = proj.shape[0]
    nch = tm // CHUNK

    def body(zu_ref, zv_ref, g_ref, ws_ref, b_ref, ya_ref):
        lane = _iota((CHUNK, LANES), 1)
        u, vn = _sgu_pre(zu_ref[...].astype(F32), zv_ref[...].astype(F32), g_ref[...])
        vn = vn.astype(BF16)
        for c in range(nch):
            rs = slice(c * CHUNK, (c + 1) * CHUNK)
            for j in range(3):
                cs = slice(j * LANES, (j + 1) * LANES)
                vp = vn[rs, cs]
                z = jnp.where(lane < HEAD, _dot(ws_ref[2 * j], vp), _dot(ws_ref[2 * j + 1], vp)) + b_ref[:, cs]
                ya_ref[rs, cs] = (u[rs, cs] * z).astype(BF16)

    return pl.pallas_call(
        body, name="sgu_fwd", grid=(T // tm,),
        in_specs=[BS((tm, A_W), lambda i: (i, 0)), BS((tm, A_W), lambda i: (i, 1)), BS((1, A_W), lambda i: (0, 0)),
                  BS((6, CHUNK, CHUNK), lambda i: (0, 0, 0)), BS((CHUNK, A_W), lambda i: (0, 0))],
        out_specs=BS((tm, A_W), lambda i: (i, 0)),
        out_shape=jax.ShapeDtypeStruct((T, A_W), BF16),
        compiler_params=_cp(("arbitrary",)),
    )(proj, proj, g_sgu, ws_tril, bs_full)


def _fox_fwd(proj, bq, bk, Bl, S):
    T = Bl * S
    nq = S // Q_BLK
    qc, kc, vc = 768 // LANES, 1152 // LANES, 1536 // LANES

    def body(q_ref, k_ref, v_ref, bq_ref, bk_ref, o_ref, lse_ref, ka_ref, va_ref):
        lane_s = _iota((S, LANES), 1)
        lane = _iota((Q_BLK, LANES), 1)
        tri = _iota((Q_BLK, Q_BLK), 1) <= _iota((Q_BLK, Q_BLK), 0)
        k = k_ref[...]
        v = v_ref[...]
        for hh in range(2):
            data = (lane_s < HEAD) if hh == 0 else (lane_s >= HEAD)
            ka_ref[hh] = jnp.where(data, k, bk_ref[0, hh])
            va_ref[hh] = jnp.where(lane_s == _bias_lane(hh), jnp.ones_like(v), v)
        for i in range(nq):
            r0 = i * Q_BLK
            q = q_ref[r0:r0 + Q_BLK, :]
            o_out = jnp.zeros((Q_BLK, LANES), F32)
            lse_out = jnp.zeros((Q_BLK, LANES), F32)
            for hh in range(2):
                hmask = (lane < HEAD) if hh == 0 else (lane >= HEAD)
                qa = jnp.where(hmask, q * 0.125, bq_ref[0, hh, r0:r0 + Q_BLK, :])
                sd = jnp.where(tri, _dot_nt(qa, ka_ref[hh, r0:r0 + Q_BLK, :]), NEG)
                m = jnp.max(sd, axis=1, keepdims=True)
                if i:
                    sf = _dot_nt(qa, ka_ref[hh, 0:r0, :])
                    m = jnp.maximum(m, jnp.max(sf, axis=1, keepdims=True))
                acc = _dot(jnp.exp(sd - m), va_ref[hh, r0:r0 + Q_BLK, :])
                if i:
                    acc = acc + _dot(jnp.exp(sf - m), va_ref[hh, 0:r0, :])
                l = jnp.sum(jnp.where(lane == _bias_lane(hh), acc, 0.0), axis=1, keepdims=True)
                o_out = jnp.where(hmask, acc / l, o_out)
                lse_out = jnp.where(hmask, m + jnp.log(l), lse_out)
            o_ref[r0:r0 + Q_BLK, :] = o_out.astype(BF16)
            lse_ref[0, r0:r0 + Q_BLK, :] = lse_out

    seq = lambda c0: BS((S, LANES), lambda b, p: (b, c0 + p))
    pair = BS((1, 2, S, LANES), lambda b, p: (b, p, 0, 0))
    return pl.pallas_call(
        body, name="fox_fwd", grid=(Bl, 3),
        in_specs=[seq(qc), seq(kc), seq(vc), pair, pair],
        out_specs=[seq(0), BS((1, S, LANES), lambda b, p: (p, b, 0))],
        out_shape=[jax.ShapeDtypeStruct((T, B_W), BF16), jax.ShapeDtypeStruct((3, T, LANES), F32)],
        scratch_shapes=[pltpu.VMEM((2, S, LANES), BF16), pltpu.VMEM((2, S, LANES), BF16)],
        compiler_params=_cp(("arbitrary", "arbitrary")),
    )(proj, proj, proj, bq, bk)


def _memkv_fwd(mem, g_mem, w_kv):
    Bl, Mt, D = mem.shape

    def body(m_ref, g_ref, w_ref, mn_ref, kv_ref):
        mn = _rms(m_ref[0], g_ref[...]).astype(BF16)
        mn_ref[0] = mn
        kv_ref[0] = jnp.dot(mn, w_ref[...], preferred_element_type=F32).astype(BF16)

    return pl.pallas_call(
        body, name="memkv_fwd", grid=(Bl,),
        in_specs=[BS((1, Mt, D), lambda b: (b, 0, 0)), BS((1, D), lambda b: (0, 0)), BS((D, 2 * M_W), lambda b: (0, 0))],
        out_specs=[BS((1, Mt, D), lambda b: (b, 0, 0)), BS((1, Mt, 2 * M_W), lambda b: (b, 0, 0))],
        out_shape=[jax.ShapeDtypeStruct((Bl, Mt, D), BF16), jax.ShapeDtypeStruct((Bl, Mt, 2 * M_W), BF16)],
        compiler_params=_cp(("arbitrary",)),
    )(mem, g_mem, w_kv)


def _memattn_fwd(proj, kv, Bl, S, tq):
    T = Bl * S
    nq = S // tq
    Mt = kv.shape[1]
    qc = 1920 // LANES

    def body(q_ref, km_ref, vm_ref, o_ref):
        lane = _iota((tq, LANES), 1)
        q = q_ref[...]
        out = jnp.zeros((tq, LANES), F32)
        for hh in range(2):
            hmask = (lane < HEAD) if hh == 0 else (lane >= HEAD)
            qs = jnp.where(hmask, q, jnp.zeros_like(q)) * 0.125
            s = _dot_nt(qs, km_ref[0])
            pe = jnp.exp(s - jnp.max(s, axis=1, keepdims=True))
            pn = pe / jnp.sum(pe, axis=1, keepdims=True)
            out = jnp.where(hmask, _dot(pn, vm_ref[0]), out)
        o_ref[...] = out.astype(BF16)

    return pl.pallas_call(
        body, name="memattn_fwd", grid=(Bl, 2, nq),
        in_specs=[BS((tq, LANES), lambda b, p, i: (b * nq + i, qc + p)),
                  BS((1, Mt, LANES), lambda b, p, i: (b, 0, p)),
                  BS((1, Mt, LANES), lambda b, p, i: (b, 0, 2 + p))],
        out_specs=BS((tq, LANES), lambda b, p, i: (b * nq + i, p)),
        out_shape=jax.ShapeDtypeStruct((T, M_W), BF16),
        compiler_params=_cp(("arbitrary", "arbitrary", "arbitrary")),
    )(proj, kv, kv)


def _mix_norms(ya, yb, ym, ga, gb, gm):
    return _rms(ya, ga), _rms(yb, gb), _rms(ym, gm)


def _outproj_fwd(ya, yb, ym, x2d, ga, gb, gm, g_post, g_pre2, w_out, tm):
    T, D = x2d.shape

    def body(ya_ref, yb_ref, ym_ref, x_ref, ga_ref, gb_ref, gm_ref, gp_ref, g2_ref, w_ref,
             y_ref, o_ref, x1_ref, h2_ref):
        na, nb_, nm = _mix_norms(ya_ref[...].astype(F32), yb_ref[...].astype(F32), ym_ref[...].astype(F32),
                                 ga_ref[...], gb_ref[...], gm_ref[...])
        y_ref[:, 0:A_W] = na.astype(BF16)
        y_ref[:, A_W:A_W + B_W] = nb_.astype(BF16)
        y_ref[:, A_W + B_W:] = nm.astype(BF16)
        o = jnp.dot(y_ref[...], w_ref[...], preferred_element_type=F32).astype(BF16)
        o_ref[...] = o
        x1 = x_ref[...] + _rms(o.astype(F32), gp_ref[...])
        x1_ref[...] = x1
        h2_ref[...] = _rms(x1, g2_ref[...]).astype(BF16)

    row = lambda w: BS((tm, w), lambda i: (i, 0))
    vec = lambda w: BS((1, w), lambda i: (0, 0))
    return pl.pallas_call(
        body, name="outproj_fwd", grid=(T // tm,),
        in_specs=[row(A_W), row(B_W), row(M_W), row(D), vec(A_W), vec(B_W), vec(M_W), vec(D), vec(D),
                  BS((A_W + B_W + M_W, D), lambda i: (0, 0))],
        out_specs=[row(A_W + B_W + M_W), row(D), row(D), row(D)],
        out_shape=[jax.ShapeDtypeStruct((T, A_W + B_W + M_W), BF16), jax.ShapeDtypeStruct((T, D), BF16),
                   jax.ShapeDtypeStruct((T, D), F32), jax.ShapeDtypeStruct((T, D), BF16)],
        compiler_params=_cp(("arbitrary",)),
    )(ya, yb, ym, x2d, ga, gb, gm, g_post, g_pre2, w_out)


def _ffn_fwd(h2, x1, target, wg, wu, wd, g_post, tm):
    T, D = x1.shape
    ns, F, _ = wg.shape

    def body(h_ref, x1_ref, t_ref, wg_ref, wu_ref, wd_ref, gp_ref,
             gs_ref, us_ref, dff_ref, dx2_ref, dgp_ref, loss_ref, acc_ref):
        j = pl.program_id(0)
        i = pl.program_id(1)
        rows = pl.ds(pl.multiple_of(i * tm, tm), tm)
        h = h_ref[...]
        g = _dot_nt(h, wg_ref[0])
        u = _dot_nt(h, wu_ref[0])
        gs_ref[0] = g.astype(BF16)
        us_ref[0] = u.astype(BF16)
        part = _dot(_silu_mul(g, u), wd_ref[0])

        @pl.when(j == 0)
        def _():
            acc_ref[rows, :] = part

        @pl.when(j != 0)
        def _():
            acc_ref[rows, :] += part

        @pl.when(j == ns - 1)
        def _():
            ff = acc_ref[rows, :]
            diff = x1_ref[...] + _rms(ff, gp_ref[...]) - t_ref[...]
            dx2 = diff * (1.0 / D)
            dff, dgp = _rms_bwd(ff, gp_ref[...], dx2)
            dx2_ref[...] = dx2
            dff_ref[...] = dff.astype(BF16)
            lpart = jnp.sum(_colsum(diff * diff), axis=1, keepdims=True) * (0.5 / D)
            _acc(dgp_ref, dgp, i == 0)
            _acc(loss_ref, jnp.broadcast_to(lpart, (1, LANES)), i == 0)

    last = lambda j, i: (jnp.where(j == ns - 1, i, 0), 0)
    wsh = BS((1, F, D), lambda j, i: (j, 0, 0))
    sh = BS((1, tm, F), lambda j, i: (j, i, 0))
    return pl.pallas_call(
        body, name="ffn_fwd", grid=(ns, T // tm),
        in_specs=[BS((tm, D), lambda j, i: (i, 0)), BS((tm, D), last), BS((tm, D), last), wsh, wsh, wsh,
                  BS((1, D), lambda j, i: (0, 0))],
        out_specs=[sh, sh, BS((tm, D), last), BS((tm, D), last),
                   BS((1, D), lambda j, i: (0, 0)), BS((1, LANES), lambda j, i: (0, 0))],
        out_shape=[jax.ShapeDtypeStruct((ns, T, F), BF16), jax.ShapeDtypeStruct((ns, T, F), BF16),
                   jax.ShapeDtypeStruct((T, D), BF16), jax.ShapeDtypeStruct((T, D), F32),
                   jax.ShapeDtypeStruct((1, D), F32), jax.ShapeDtypeStruct((1, LANES), F32)],
        scratch_shapes=[pltpu.VMEM((T, D), F32)],
        compiler_params=_cp(("arbitrary", "arbitrary")),
    )(h2, x1, target, wg, wu, wd, g_post)


def _ffn_bwd(dff, h2, gs, us, wg, wu, wd, tm):
    T, D = h2.shape
    ns, F, _ = wg.shape

    def body(dff_ref, h_ref, gs_ref, us_ref, wg_ref, wu_ref, wd_ref, dh_ref, dwg_out, dwu_out, dwd_out,
             dwg_ref, dwu_ref, dwd_ref):
        first = pl.program_id(1) == 0
        dff = dff_ref[...]
        h = h_ref[...]
        parts = []
        for r in range(ROW_SPLIT):
            rows = slice(r * (tm // ROW_SPLIT), (r + 1) * (tm // ROW_SPLIT))
            dact = _dot_nt(dff[rows], wd_ref[0])
            g = gs_ref[0, rows, :].astype(F32)
            u = us_ref[0, rows, :].astype(F32)
            sig = _sigmoid(g)
            gsig = g * sig
            dg = (dact * u * (sig + gsig * (1.0 - sig))).astype(BF16)
            du = (dact * gsig).astype(BF16)
            dh_ref[0, rows, :] = (_dot(dg, wg_ref[0]) + _dot(du, wu_ref[0])).astype(BF16)
            parts.append(((gsig * u).astype(BF16), dg, du))
        a, dg, du = [jnp.concatenate(p, axis=0) for p in zip(*parts)]
        _acc(dwd_ref, _dot_tn(a, dff), first)
        _acc(dwg_ref, _dot_tn(dg, h), first)
        _acc(dwu_ref, _dot_tn(du, h), first)

        @pl.when(pl.program_id(1) == pl.num_programs(1) - 1)
        def _():
            dwg_out[0] = dwg_ref[...].astype(BF16)
            dwu_out[0] = dwu_ref[...].astype(BF16)
            dwd_out[0] = dwd_ref[...].astype(BF16)

    row = BS((tm, D), lambda j, i: (i, 0))
    sh = BS((1, tm, F), lambda j, i: (j, i, 0))
    wsh = BS((1, F, D), lambda j, i: (j, 0, 0))
    return pl.pallas_call(
        body, name="ffn_bwd", grid=(ns, T // tm),
        in_specs=[row, row, sh, sh, wsh, wsh, wsh],
        out_specs=[BS((1, tm, D), lambda j, i: (j, i, 0)), wsh, wsh, wsh],
        out_shape=[jax.ShapeDtypeStruct((ns, T, D), BF16)] + [jax.ShapeDtypeStruct((ns, F, D), BF16)] * 3,
        scratch_shapes=[pltpu.VMEM((F, D), F32)] * 3,
        compiler_params=_cp(("arbitrary", "arbitrary")),
    )(dff, h2, gs, us, wg, wu, wd)


DPROJ_PIECES = ((0, A_W), (A_W, A_W), (768, B_W), (1152, B_W), (1536, B_W), (1920, LANES), (2048, M_W))


def _put_dproj(dp_ref, piece_refs):
    for (c0, w), ref in zip(DPROJ_PIECES, piece_refs):
        dp_ref[:, c0:c0 + w] = ref[...].astype(BF16)


def _dw_in(pieces, h, ns, tk):
    T, D = h.shape
    M = P_COLS
    dsh = D // ns
    tk = min(tk, T)

    def body(*refs):
        piece_refs, h_ref, o_ref, acc_ref, dp_ref = refs[:7], refs[7], refs[8], refs[9], refs[10]
        t = pl.program_id(0)
        _put_dproj(dp_ref, piece_refs)
        _acc(acc_ref, _dot_tn(h_ref[...], dp_ref[...]), t == 0)

        @pl.when(t == pl.num_programs(0) - 1)
        def _():
            for s in range(ns):
                o_ref[s] = acc_ref[s * dsh:(s + 1) * dsh, :].T.astype(BF16)

    return pl.pallas_call(
        body, name="dw_in", grid=(T // tk,),
        in_specs=[BS((tk, w), lambda t: (t, 0)) for _, w in DPROJ_PIECES] + [BS((tk, D), lambda t: (t, 0))],
        out_specs=BS((ns, M, dsh), lambda t: (0, 0, 0)),
        out_shape=jax.ShapeDtypeStruct((ns, M, dsh), BF16),
        scratch_shapes=[pltpu.VMEM((D, M), F32), pltpu.VMEM((tk, M), BF16)],
        compiler_params=_cp(("arbitrary",)),
    )(*pieces, h)


def _outproj_bwd(dh2, x1, dx2, o, y, ya, yb, ym, ga, gb, gm, g_post, g_pre2, w_out, tm):
    T, D = x1.shape
    ns = dh2.shape[0]

    def body(dh_ref, x1_ref, dx2_ref, o_ref, y_ref, ya_ref, yb_ref, ym_ref, ga_ref, gb_ref, gm_ref, gp_ref, g2_ref, w_ref,
             dx1_ref, dw_ref, dya_ref, dyb_ref, dym_ref, dga_ref, dgb_ref, dgm_ref, dgp_ref, dg2_ref):
        first = pl.program_id(0) == 0
        dh = dh_ref[0].astype(F32)
        for j in range(1, ns):
            dh = dh + dh_ref[j].astype(F32)
        dxa, dg2 = _rms_bwd(x1_ref[...], g2_ref[...], dh)
        dx1 = dx2_ref[...] + dxa
        dx1_ref[...] = dx1
        _acc(dg2_ref, dg2, first)
        do, dgp = _rms_bwd(o_ref[...].astype(F32), gp_ref[...], dx1)
        do = do.astype(BF16)
        _acc(dw_ref, _dot_tn(y_ref[...], do), first)
        dy = _dot_nt(do, w_ref[...])
        dya, dga = _rms_bwd(ya_ref[...].astype(F32), ga_ref[...], dy[:, 0:A_W])
        dyb, dgb = _rms_bwd(yb_ref[...].astype(F32), gb_ref[...], dy[:, A_W:A_W + B_W])
        dym, dgm = _rms_bwd(ym_ref[...].astype(F32), gm_ref[...], dy[:, A_W + B_W:])
        dya_ref[...] = dya.astype(BF16)
        dyb_ref[...] = dyb.astype(BF16)
        dym_ref[...] = dym.astype(BF16)
        _acc(dga_ref, dga, first)
        _acc(dgb_ref, dgb, first)
        _acc(dgm_ref, dgm, first)
        _acc(dgp_ref, dgp, first)

    row = lambda w: BS((tm, w), lambda i: (i, 0))
    vec = lambda w: BS((1, w), lambda i: (0, 0))
    sds = jax.ShapeDtypeStruct
    return pl.pallas_call(
        body, name="outproj_bwd", grid=(T // tm,),
        in_specs=[BS((ns, tm, D), lambda i: (0, i, 0)), row(D), row(D), row(D), row(A_W + B_W + M_W), row(A_W), row(B_W),
                  row(M_W), vec(A_W), vec(B_W), vec(M_W), vec(D), vec(D), BS((A_W + B_W + M_W, D), lambda i: (0, 0))],
        out_specs=[row(D), BS((A_W + B_W + M_W, D), lambda i: (0, 0)), row(A_W), row(B_W), row(M_W),
                   vec(A_W), vec(B_W), vec(M_W), vec(D), vec(D)],
        out_shape=[sds((T, D), F32), sds((A_W + B_W + M_W, D), F32), sds((T, A_W), BF16), sds((T, B_W), BF16),
                   sds((T, M_W), BF16), sds((1, A_W), F32), sds((1, B_W), F32), sds((1, M_W), F32), sds((1, D), F32),
                   sds((1, D), F32)],
        compiler_params=_cp(("arbitrary",)),
    )(dh2, x1, dx2, o, y, ya, yb, ym, ga, gb, gm, g_post, g_pre2, w_out)


def _sgu_bwd(proj, dya, g_sgu, ws_tril, bs_full, tm):
    T = proj.shape[0]
    nch = tm // CHUNK

    def body(zu_ref, zv_ref, dy_ref, g_ref, ws_ref, b_ref, dzu_ref, dzv_ref, dws_ref, dbs_ref, dg_ref,
             du_ref, dvn_ref, dbf_ref):
        step = pl.program_id(0)
        first = step == 0
        lane = _iota((CHUNK, LANES), 1)
        tril = _iota((CHUNK, CHUNK), 0) >= _iota((CHUNK, CHUNK), 1)
        (u, vn), vjp = jax.vjp(_sgu_pre, zu_ref[...].astype(F32), zv_ref[...].astype(F32), g_ref[...])
        vnb = vn.astype(BF16)
        dy = dy_ref[...].astype(F32)

        @pl.when(first)
        def _():
            dws_ref[...] = jnp.zeros_like(dws_ref)
            dbf_ref[...] = jnp.zeros_like(dbf_ref)

        for c in range(nch):
            rs = slice(c * CHUNK, (c + 1) * CHUNK)
            for j in range(3):
                cs = slice(j * LANES, (j + 1) * LANES)
                vp = vnb[rs, cs]
                z = jnp.where(lane < HEAD, _dot(ws_ref[2 * j], vp), _dot(ws_ref[2 * j + 1], vp)) + b_ref[:, cs]
                du_ref[rs, cs] = dy[rs, cs] * z
                dz = dy[rs, cs] * u[rs, cs]
                dbf_ref[:, cs] += dz
                dzb = dz.astype(BF16)
                dz0 = jnp.where(lane < HEAD, dzb, jnp.zeros_like(dzb))
                dz1 = jnp.where(lane >= HEAD, dzb, jnp.zeros_like(dzb))
                dvn_ref[rs, cs] = jnp.where(lane < HEAD, _dot_tn(ws_ref[2 * j], dzb), _dot_tn(ws_ref[2 * j + 1], dzb))
                dws_ref[2 * j] += jnp.where(tril, _dot_nt(dz0, vp), 0.0)
                dws_ref[2 * j + 1] += jnp.where(tril, _dot_nt(dz1, vp), 0.0)
        dzu, dzv, dg = vjp((du_ref[...], dvn_ref[...]))
        dzu_ref[...] = dzu.astype(BF16)
        dzv_ref[...] = dzv.astype(BF16)
        _acc(dg_ref, dg, first)

        @pl.when(step == pl.num_programs(0) - 1)
        def _():
            out = jnp.zeros((CHUNK, LANES), F32)
            for j in range(3):
                slab = dbf_ref[:, j * LANES:(j + 1) * LANES]
                lo = jnp.sum(jnp.where(lane < HEAD, slab, 0.0), axis=1, keepdims=True)
                hi = jnp.sum(jnp.where(lane >= HEAD, slab, 0.0), axis=1, keepdims=True)
                out = out + jnp.where(lane == 2 * j, lo, 0.0) + jnp.where(lane == 2 * j + 1, hi, 0.0)
            dbs_ref[...] = out

    return pl.pallas_call(
        body, name="sgu_bwd", grid=(T // tm,),
        in_specs=[BS((tm, A_W), lambda i: (i, 0)), BS((tm, A_W), lambda i: (i, 1)), BS((tm, A_W), lambda i: (i, 0)),
                  BS((1, A_W), lambda i: (0, 0)), BS((6, CHUNK, CHUNK), lambda i: (0, 0, 0)),
                  BS((CHUNK, A_W), lambda i: (0, 0))],
        out_specs=[BS((tm, A_W), lambda i: (i, 0)), BS((tm, A_W), lambda i: (i, 0)),
                   BS((6, CHUNK, CHUNK), lambda i: (0, 0, 0)), BS((CHUNK, LANES), lambda i: (0, 0)),
                   BS((1, A_W), lambda i: (0, 0))],
        out_shape=[jax.ShapeDtypeStruct((T, A_W), BF16), jax.ShapeDtypeStruct((T, A_W), BF16),
                   jax.ShapeDtypeStruct((6, CHUNK, CHUNK), F32), jax.ShapeDtypeStruct((CHUNK, LANES), F32),
                   jax.ShapeDtypeStruct((1, A_W), F32)],
        scratch_shapes=[pltpu.VMEM((tm, A_W), F32), pltpu.VMEM((tm, A_W), F32), pltpu.VMEM((CHUNK, A_W), F32)],
        compiler_params=_cp(("arbitrary",)),
    )(proj, proj, dya, g_sgu, ws_tril, bs_full)


def _memattn_bwd(proj, kv, dym, Bl, S, tq):
    T = Bl * S
    nq = S // tq
    Mt = kv.shape[1]
    qc = 1920 // LANES

    def body(q_ref, km_ref, vm_ref, do_ref, dq_ref, dkm_ref, dvm_ref):
        first = pl.program_id(2) == 0
        lane = _iota((tq, LANES), 1)
        q = q_ref[...]
        do = do_ref[...]
        dq_out = jnp.zeros((tq, LANES), F32)
        dkm = jnp.zeros((Mt, LANES), F32)
        dvm = jnp.zeros((Mt, LANES), F32)
        for hh in range(2):
            hmask = (lane < HEAD) if hh == 0 else (lane >= HEAD)
            qs = jnp.where(hmask, q, jnp.zeros_like(q)) * 0.125
            dom = jnp.where(hmask, do, 0.0).astype(BF16)
            s = _dot_nt(qs, km_ref[0])
            pe = jnp.exp(s - jnp.max(s, axis=1, keepdims=True))
            pn = pe / jnp.sum(pe, axis=1, keepdims=True)
            dp = _dot_nt(dom, vm_ref[0])
            ds = (pn * (dp - jnp.sum(pn * dp, axis=1, keepdims=True))).astype(BF16)
            dq_out = jnp.where(hmask, _dot(ds, km_ref[0]) * 0.125, dq_out)
            dkm = dkm + _dot_tn(ds, qs)
            dvm = dvm + _dot_tn(pn, dom)
        dq_ref[...] = dq_out.astype(BF16)
        _acc(dkm_ref, dkm[None], first)
        _acc(dvm_ref, dvm[None], first)

    return pl.pallas_call(
        body, name="memattn_bwd", grid=(Bl, 2, nq),
        in_specs=[BS((tq, LANES), lambda b, p, i: (b * nq + i, qc + p)),
                  BS((1, Mt, LANES), lambda b, p, i: (b, 0, p)),
                  BS((1, Mt, LANES), lambda b, p, i: (b, 0, 2 + p)),
                  BS((tq, LANES), lambda b, p, i: (b * nq + i, p))],
        out_specs=[BS((tq, LANES), lambda b, p, i: (b * nq + i, p)),
                   BS((1, Mt, LANES), lambda b, p, i: (b, 0, p)),
                   BS((1, Mt, LANES), lambda b, p, i: (b, 0, p))],
        out_shape=[jax.ShapeDtypeStruct((T, M_W), BF16), jax.ShapeDtypeStruct((Bl, Mt, M_W), F32),
                   jax.ShapeDtypeStruct((Bl, Mt, M_W), F32)],
        compiler_params=_cp(("arbitrary", "arbitrary", "arbitrary")),
    )(proj, kv, kv, dym)


def _memkv_bwd(dkm, dvm, memn, mem, g_mem, w_kv):
    Bl, Mt, D = mem.shape

    def body(dk_ref, dv_ref, mn_ref, m_ref, g_ref, w_ref, dw_ref, dg_ref):
        first = pl.program_id(0) == 0
        dk = dk_ref[0].astype(BF16)
        dv = dv_ref[0].astype(BF16)
        mn = mn_ref[0]
        dmn = _dot_nt(dk, w_ref[:, 0:M_W]) + _dot_nt(dv, w_ref[:, M_W:])
        _, dg = _rms_bwd(m_ref[0], g_ref[...], dmn)
        _acc(dg_ref, dg, first)

        @pl.when(first)
        def _():
            dw_ref[...] = jnp.zeros_like(dw_ref)

        dw_ref[:, 0:M_W] += _dot_tn(mn, dk)
        dw_ref[:, M_W:] += _dot_tn(mn, dv)

    return pl.pallas_call(
        body, name="memkv_bwd", grid=(Bl,),
        in_specs=[BS((1, Mt, M_W), lambda b: (b, 0, 0)), BS((1, Mt, M_W), lambda b: (b, 0, 0)),
                  BS((1, Mt, D), lambda b: (b, 0, 0)), BS((1, Mt, D), lambda b: (b, 0, 0)),
                  BS((1, D), lambda b: (0, 0)), BS((D, 2 * M_W), lambda b: (0, 0))],
        out_specs=[BS((D, 2 * M_W), lambda b: (0, 0)), BS((1, D), lambda b: (0, 0))],
        out_shape=[jax.ShapeDtypeStruct((D, 2 * M_W), F32), jax.ShapeDtypeStruct((1, D), F32)],
        compiler_params=_cp(("arbitrary",)),
    )(dkm, dvm, memn, mem, g_mem, w_kv)


def _fox_bwd(proj, dyb, lse, bq, bk, Bl, S):
    T = Bl * S
    nq = S // Q_BLK
    nb = S // LANES
    qc, kc, vc = 768 // LANES, 1152 // LANES, 1536 // LANES

    def body(q_ref, k_ref, v_ref, do_ref, lse_ref, bq_ref, bk_ref,
             dq_ref, dk_ref, dv_ref, dcr_ref, ka_ref, dka_ref, dva_ref):
        p = pl.program_id(1)
        lane_s = _iota((S, LANES), 1)
        lane = _iota((Q_BLK, LANES), 1)
        sub = _iota((8, LANES), 0)
        tri = _iota((Q_BLK, Q_BLK), 1) <= _iota((Q_BLK, Q_BLK), 0)
        k = k_ref[...]
        for hh in range(2):
            data = (lane_s < HEAD) if hh == 0 else (lane_s >= HEAD)
            ka_ref[hh] = jnp.where(data, k, bk_ref[0, hh])
        dka_ref[...] = jnp.zeros_like(dka_ref)
        dva_ref[...] = jnp.zeros_like(dva_ref)

        @pl.when(p == 0)
        def _():
            dcr_ref[...] = jnp.zeros_like(dcr_ref)

        def add_colsums(ds, first_blk, h):
            cs = _colsum(ds)
            for jb in range(ds.shape[1] // LANES):
                dcr_ref[0, first_blk + jb] += jnp.where(sub == h, cs[:, jb * LANES:(jb + 1) * LANES], 0.0)

        for i in range(nq):
            r0 = i * Q_BLK
            r1 = r0 + Q_BLK
            q = q_ref[r0:r1, :]
            do = do_ref[r0:r1, :]
            lse_b = lse_ref[0, r0:r1, :]
            dq_out = jnp.zeros((Q_BLK, LANES), F32)
            for hh in range(2):
                hmask = (lane < HEAD) if hh == 0 else (lane >= HEAD)
                h = 2 * p + hh
                qs = jnp.where(hmask, q * 0.125, jnp.zeros_like(q))
                qa = jnp.where(hmask, q * 0.125, bq_ref[0, hh, r0:r1, :])
                dob = jnp.where(hmask, do, 0.0).astype(BF16)
                lse_h = jnp.sum(jnp.where(lane == hh * HEAD, lse_b, 0.0), axis=1, keepdims=True)
                pd = jnp.where(tri, jnp.exp(_dot_nt(qa, ka_ref[hh, r0:r1, :]) - lse_h), 0.0)
                dpd = _dot_nt(dob, v_ref[r0:r1, :])
                delta = jnp.sum(pd * dpd, axis=1, keepdims=True)
                psum = jnp.sum(pd, axis=1, keepdims=True)
                if i:
                    pf = jnp.exp(_dot_nt(qa, ka_ref[hh, 0:r0, :]) - lse_h)
                    dpf = _dot_nt(dob, v_ref[0:r0, :])
                    delta = delta + jnp.sum(pf * dpf, axis=1, keepdims=True)
                    psum = psum + jnp.sum(pf, axis=1, keepdims=True)
                delta = delta / psum
                dsd = pd * (dpd - delta)
                add_colsums(dsd, r0 // LANES, h)
                dsd = dsd.astype(BF16)
                dq_h = _dot(dsd, k_ref[r0:r1, :])
                dka_ref[r0:r1, :] += _dot_tn(dsd, qs)
                dva_ref[r0:r1, :] += _dot_tn(pd, dob)
                if i:
                    dsf = pf * (dpf - delta)
                    add_colsums(dsf, 0, h)
                    dsf = dsf.astype(BF16)
                    dq_h = dq_h + _dot(dsf, k_ref[0:r0, :])
                    dka_ref[0:r0, :] += _dot_tn(dsf, qs)
                    dva_ref[0:r0, :] += _dot_tn(pf, dob)
                dq_out = jnp.where(hmask, dq_h * 0.125, dq_out)
            dq_ref[r0:r1, :] = dq_out.astype(BF16)
        dk_ref[...] = dka_ref[...].astype(BF16)
        dv_ref[...] = dva_ref[...].astype(BF16)

    seq = lambda c0: BS((S, LANES), lambda b, p: (b, c0 + p))
    pair = BS((1, 2, S, LANES), lambda b, p: (b, p, 0, 0))
    rowblk = BS((1, nb, 8, LANES), lambda b, p: (b, 0, 0, 0))
    return pl.pallas_call(
        body, name="fox_bwd", grid=(Bl, 3),
        in_specs=[seq(qc), seq(kc), seq(vc), seq(0), BS((1, S, LANES), lambda b, p: (p, b, 0)), pair, pair],
        out_specs=[seq(0), seq(0), seq(0), rowblk],
        out_shape=[jax.ShapeDtypeStruct((T, B_W), BF16)] * 3 + [jax.ShapeDtypeStruct((Bl, nb, 8, LANES), F32)],
        scratch_shapes=[pltpu.VMEM((2, S, LANES), BF16), pltpu.VMEM((S, LANES), F32), pltpu.VMEM((S, LANES), F32)],
        compiler_params=_cp(("arbitrary", "arbitrary")),
    )(proj, proj, proj, dyb, lse, bq, bk)


def _gate_bwd(dc_row, fl_row):
    Bl, nb, _, _ = dc_row.shape

    def body(dc_ref, fl_ref, o_ref):
        lane = _iota((8, LANES), 1)

        carry = jnp.zeros((8, 1), F32)
        for j in reversed(range(nb)):
            r = -dc_ref[0, j]
            for k in (1, 2, 4, 8, 16, 32, 64):
                r = r + jnp.where(lane < LANES - k, pltpu.roll(r, LANES - k, 1), 0.0)
            total = jnp.sum(jnp.where(lane == 0, r, 0.0), axis=1, keepdims=True)
            dfl = (r + carry) * _sigmoid(-fl_ref[0, j])
            carry = carry + total
            o_ref[0, j * LANES:(j + 1) * LANES, :] = jnp.concatenate(
                [dfl, jnp.zeros((LANES - 8, LANES), F32)], axis=0).T

    rowblk = BS((1, nb, 8, LANES), lambda b: (b, 0, 0, 0))
    return pl.pallas_call(
        body, name="gate_bwd", grid=(Bl,),
        in_specs=[rowblk, rowblk],
        out_specs=BS((1, nb * LANES, LANES), lambda b: (b, 0, 0)),
        out_shape=jax.ShapeDtypeStruct((Bl, nb * LANES, LANES), F32),
        compiler_params=_cp(("arbitrary",)),
    )(dc_row, fl_row)


def _inproj_bwd(pieces, x2d, dx1, g_pre, w_in_p, tm):
    T, D = x2d.shape
    ns, _, dsh = w_in_p.shape

    def body(*refs):
        piece_refs = refs[:7]
        x_ref, dx1_ref, g_ref, w_ref, gx_ref, dg_ref, dbf_ref, dp_ref = refs[7:]
        first = pl.program_id(0) == 0
        _put_dproj(dp_ref, piece_refs)
        dh = jnp.concatenate([_dot(dp_ref[...], w_ref[s]) for s in range(ns)], axis=1)
        dxa, dg = _rms_bwd(x_ref[...], g_ref[...], dh)
        gx_ref[...] = dx1_ref[...] + dxa
        _acc(dg_ref, dg, first)
        _acc(dbf_ref, _colsum(piece_refs[5][...]), first)

    row = lambda w: BS((tm, w), lambda i: (i, 0))
    return pl.pallas_call(
        body, name="inproj_bwd", grid=(T // tm,),
        in_specs=[row(w) for _, w in DPROJ_PIECES] + [row(D), row(D), BS((1, D), lambda i: (0, 0)),
                                                      BS((ns, P_COLS, dsh), lambda i: (0, 0, 0))],
        out_specs=[row(D), BS((1, D), lambda i: (0, 0)), BS((1, LANES), lambda i: (0, 0))],
        out_shape=[jax.ShapeDtypeStruct((T, D), F32), jax.ShapeDtypeStruct((1, D), F32),
                   jax.ShapeDtypeStruct((1, LANES), F32)],
        scratch_shapes=[pltpu.VMEM((tm, P_COLS), BF16)],
        compiler_params=_cp(("arbitrary",)),
    )(*pieces, x2d, dx1, g_pre, w_in_p)


def _local_step(x, mem, target, W, P, reduce=None):
    Bl, S, D = x.shape
    T = Bl * S
    tm = min(512, T)
    x2d = x.reshape(T, D)
    t2d = target.reshape(T, D)
    vec = lambda a: a.reshape(1, -1)
    bf_row = jnp.pad(P["b_f"].reshape(1, -1), ((0, 0), (0, LANES - N_FOX_HEADS)))
    tril = jnp.tril(jnp.ones((CHUNK, CHUNK), bool))
    ws_tril = jnp.where(tril[None], P["w_s"][0], 0.0).astype(BF16)
    bs_full = jnp.repeat(P["b_s"][0].T, HEAD, axis=1)
    g_pre, g_sgu = vec(P["g_pre_mix"]), vec(P["g_sgu"])
    ga, gb, gm = vec(P["g_out_a"]), vec(P["g_out_b"]), vec(P["g_out_m"])
    g_mem, g_post, g_pre2, g_post2 = vec(P["g_mem"]), vec(P["g_post_mix"]), vec(P["g_pre_ffn"]), vec(P["g_post_ffn"])

    h, proj, flog = _inproj_fwd(x2d, g_pre, W["w_in"], tm)
    bq, bk, fl_row = _gate_fwd(flog.reshape(Bl, S, LANES), bf_row)
    ya = _sgu_fwd(proj, g_sgu, ws_tril, bs_full, tm)
    yb, lse = _fox_fwd(proj, bq, bk, Bl, S)
    memn, kv = _memkv_fwd(mem, g_mem, W["w_mem_kv"])
    ym = _memattn_fwd(proj, kv, Bl, S, min(2048, S))
    y, o, x1, h2 = _outproj_fwd(ya, yb, ym, x2d, ga, gb, gm, g_post, g_pre2, W["w_out"], tm)
    gs, us, dff, dx2, dg_post2, loss = _ffn_fwd(h2, x1, t2d, W["w_gate"], W["w_up"], W["w_down"], g_post2, tm)

    dh2, d_w_gate, d_w_up, d_w_down = _ffn_bwd(dff, h2, gs, us, W["w_gate"], W["w_up"], W["w_down"], min(1024, T))
    ffn = [d_w_gate, d_w_up, d_w_down]
    if reduce is not None:
        pending, _ = reduce.begin("ffn", ffn)
    dx1, d_w_out, dya, dyb, dym, dga, dgb, dgm, dg_post, dg_pre2 = _outproj_bwd(
        dh2, x1, dx2, o, y, ya, yb, ym, ga, gb, gm, g_post, g_pre2, W["w_out"], tm)
    if reduce is not None:
        ffn, (dya, dyb, dym) = reduce.finish("ffn", pending, (dya, dyb, dym))
    dzu, dzv, dws, dbs_cols, dg_sgu = _sgu_bwd(proj, dya, g_sgu, ws_tril, bs_full, tm)
    dqm, dkm, dvm = _memattn_bwd(proj, kv, dym, Bl, S, min(2048, S))
    d_w_kv, dg_mem = _memkv_bwd(dkm, dvm, memn, mem, g_mem, W["w_mem_kv"])
    mid = [d_w_kv, d_w_out]
    dq, dk, dv, dc_row = _fox_bwd(proj, dyb, lse, bq, bk, Bl, S)
    if reduce is not None:
        done = reduce.apply(BIG[3:], ffn)
        pending, after = reduce.begin("mid", mid, (dc_row,) + done)
        dc_row = after[0]
    dfl = _gate_bwd(dc_row, fl_row).reshape(T, LANES)
    if reduce is not None:
        mid, (dfl,) = reduce.finish("mid", pending, (dfl,), first=(dzu, dzv))
    pieces = (dzu, dzv, dq, dk, dv, dfl, dqm)
    d_w_in = _dw_in(pieces, h, W["w_in"].shape[0], 1024)
    if reduce is None:
        big = dict(zip(BIG, [d_w_in] + mid + ffn))
    else:
        done = reduce.apply(BIG[1:3], mid)
        pending, after = reduce.begin("in", [d_w_in], done + (dx1,))
        dx1 = after[-1]
        big = {"w_in": pending}
    grad_x, dg_pre, dbf = _inproj_bwd(pieces, x2d, dx1, g_pre, W["w_in"], tm)
    small = {"g_pre_mix": dg_pre, "b_f": dbf[:, :N_FOX_HEADS], "g_sgu": dg_sgu, "w_s": dws, "b_s": dbs_cols[:, :N_FOX_HEADS].T,
             "g_out_a": dga, "g_out_b": dgb, "g_out_m": dgm, "g_mem": dg_mem, "g_post_mix": dg_post,
             "g_pre_ffn": dg_pre2, "g_post_ffn": dg_post2, "loss": loss[:, :1]}
    return grad_x.reshape(Bl, S, D), big, small


def _place():
    return lax.axis_index("x"), lax.axis_index("y"), lax.axis_index("c")


def _exchange_on_sequencer(srcs, own_full, name, collective_id):
    n = len(srcs)

    def body(*refs):
        src, dst = refs[:n], refs[n:2 * n]
        lsem, isend, irecv, dsend, drecv = refs[2 * n:]
        x, y, c = _place()
        oc = 1 - c
        s_me = 2 * x + y
        sib = (x, y, oc)
        chips = [(1 - x, y), (x, 1 - y), (1 - x, 1 - y)]
        barrier = pltpu.get_barrier_semaphore()
        for dev in [(cx, cy, c) for cx, cy in chips] + [sib]:
            pl.semaphore_signal(barrier, inc=1, device_id=dev, device_id_type=MESH)
        pl.semaphore_wait(barrier, 4)

        def remote(a, b, ssem, rsem, dev):
            return pltpu.make_async_remote_copy(src_ref=a, dst_ref=b, send_sem=ssem, recv_sem=rsem,
                                                device_id=dev, device_id_type=MESH)

        sends, local = [], []
        for w in range(n):
            for j, (cx, cy) in enumerate(chips):
                half = src[w].at[c] if own_full else src[w].at[2 * cx + cy]
                cp = remote(half, dst[w].at[s_me, c], isend.at[w, j], irecv.at[w, j], (cx, cy, c))
                cp.start()
                sends.append(cp)
            if own_full:
                cp = remote(src[w], dst[w].at[s_me], dsend.at[w, 3], drecv.at[w, 3], sib)
            else:
                cp = remote(src[w].at[s_me], dst[w].at[s_me, c], dsend.at[w, 3], drecv.at[w, 3], sib)
                loc = pltpu.make_async_copy(src[w].at[s_me], dst[w].at[s_me, c], lsem.at[w])
                loc.start()
                local.append(loc)
            cp.start()
            sends.append(cp)
        for w in range(n):
            for j, (cx, cy) in enumerate(chips):
                landed = dst[w].at[2 * cx + cy, c]
                remote(landed, landed, isend.at[w, j], irecv.at[w, j], (cx, cy, c)).wait_recv()
                cp = remote(landed, landed, dsend.at[w, j], drecv.at[w, j], sib)
                cp.start()
                sends.append(cp)
        for w in range(n):
            for j, (cx, cy) in enumerate(chips):
                landed = dst[w].at[2 * cx + cy, oc]
                remote(landed, landed, dsend.at[w, j], drecv.at[w, j], sib).wait_recv()
            landed = dst[w].at[s_me] if own_full else dst[w].at[s_me, oc]
            remote(landed, landed, dsend.at[w, 3], drecv.at[w, 3], sib).wait_recv()
        for cp in sends:
            cp.wait_send()
        for loc in local:
            loc.wait()

    return pl.kernel(
        body, out_type=[jax.ShapeDtypeStruct((4, 2) + s.shape[1:], s.dtype) for s in srcs],
        mesh=plsc.ScalarSubcoreMesh(axis_name="sequencer", num_cores=1), name=name,
        scratch_types=[pltpu.SemaphoreType.DMA((n,)), pltpu.SemaphoreType.DMA((n, 3)), pltpu.SemaphoreType.DMA((n, 3)),
                       pltpu.SemaphoreType.DMA((n, 4)), pltpu.SemaphoreType.DMA((n, 4))],
        compiler_params=pltpu.CompilerParams(collective_id=collective_id),
    )(*srcs)


def _sibling_swap(grads, name, collective_id):
    n = len(grads)

    def body(*refs):
        g, theirs = refs[:n], refs[n:2 * n]
        ssem, rsem = refs[2 * n:]
        x, y, c = _place()
        sib = (x, y, 1 - c)
        barrier = pltpu.get_barrier_semaphore()
        pl.semaphore_signal(barrier, inc=1, device_id=sib, device_id_type=MESH)
        pl.semaphore_wait(barrier, 1)
        cps = []
        for w in range(n):
            cp = pltpu.make_async_remote_copy(src_ref=g[w].at[:, 1 - c], dst_ref=theirs[w], send_sem=ssem.at[w],
                                              recv_sem=rsem.at[w], device_id=sib, device_id_type=MESH)
            cp.start()
            cps.append(cp)
        for cp in cps:
            cp.wait()

    return pl.kernel(
        body, out_type=[jax.ShapeDtypeStruct((4,) + g.shape[2:], g.dtype) for g in grads],
        mesh=plsc.ScalarSubcoreMesh(axis_name="sequencer", num_cores=1), name=name,
        scratch_types=[pltpu.SemaphoreType.DMA((n,)), pltpu.SemaphoreType.DMA((n,))],
        compiler_params=pltpu.CompilerParams(collective_id=collective_id),
    )(*grads)


def _add_pairs(core, gs, theirs, name):
    n = len(gs)

    def body(core_ref, *refs):
        for g_ref, t_ref, o_ref in zip(refs[:n], refs[n:2 * n], refs[2 * n:]):
            o_ref[...] = (g_ref[:, 0].astype(F32) + t_ref[...].astype(F32)).astype(BF16)

    def specs(g):
        _, _, hr, C = g.shape
        return (BS((2, 1, hr, C), lambda s, core_ref: (s, core_ref[0], 0, 0)), BS((2, hr, C), lambda s, core_ref: (s, 0, 0)))

    return pl.pallas_call(
        body, name=name,
        grid_spec=pltpu.PrefetchScalarGridSpec(
            num_scalar_prefetch=1, grid=(2,),
            in_specs=[specs(g)[0] for g in gs] + [specs(g)[1] for g in gs], out_specs=[specs(g)[1] for g in gs]),
        out_shape=[jax.ShapeDtypeStruct(t.shape, BF16) for t in theirs],
        compiler_params=_cp(("arbitrary",)))(core, *gs, *theirs)


def _sum_chips(r, name):
    _, _, hr, C = r.shape

    def body(r_ref, o_ref):
        o_ref[...] = ((r_ref[0, 0].astype(F32) + r_ref[1, 0].astype(F32)) + r_ref[2, 0].astype(F32)) + r_ref[3, 0].astype(F32)

    return pl.pallas_call(body, name=name, grid=(2,), in_specs=[BS((4, 1, hr, C), lambda h: (0, h, 0, 0))],
                          out_specs=BS((hr, C), lambda h: (h, 0)), out_shape=jax.ShapeDtypeStruct((2 * hr, C), F32),
                          compiler_params=_cp(("arbitrary",)))(r)


class _Reducer:
    IDS = {"ffn": (4, 5), "mid": (6, 7), "in": (8, 9)}

    def __init__(self, core, apply):
        self.core = core
        self.apply = apply

    def begin(self, tag, grads, after=()):
        grads, after = lax.optimization_barrier((list(grads), after))
        g4 = [g.reshape(4, 2, -1, g.shape[-1]) for g in grads]
        return (g4, _sibling_swap(g4, "swap_" + tag, self.IDS[tag][0])), after

    def finish(self, tag, pending, hold, first=()):
        pending, first = lax.optimization_barrier((pending, first))
        g4, theirs = pending
        sums = _add_pairs(self.core, g4, theirs, "chip_sum_" + tag)
        sums, hold = lax.optimization_barrier((sums, hold))
        return _exchange_on_sequencer(sums, False, "scatter_" + tag, self.IDS[tag][1]), hold


def _small_allreduce(part):
    R = part.shape[0]
    rs = R // 8
    masks = [(mx, my, mc) for mx in (0, 1) for my in (0, 1) for mc in (0, 1)][1:]

    def body(p_ref, o_ref, buf_ref, s1, r1, s2, r2):
        x, y, c = _place()
        d = 4 * x + 2 * y + c
        mine = pl.ds(pl.multiple_of(d * rs, 8), rs)
        peers = [((x + mx) % 2, (y + my) % 2, (c + mc) % 2) for mx, my, mc in masks]
        first, second = [], []
        for k, (px, py, pc) in enumerate(peers):
            theirs = pl.ds(pl.multiple_of((4 * px + 2 * py + pc) * rs, 8), rs)
            cp = pltpu.make_async_remote_copy(src_ref=p_ref.at[theirs, :], dst_ref=buf_ref.at[d], send_sem=s1.at[k],
                                              recv_sem=r1.at[k], device_id=(px, py, pc), device_id_type=MESH)
            cp.start()
            first.append(cp)
        buf_ref[d] = p_ref[mine, :]
        for k, (px, py, pc) in enumerate(peers):
            slot = buf_ref.at[4 * px + 2 * py + pc]
            pltpu.make_async_remote_copy(src_ref=slot, dst_ref=slot, send_sem=s1.at[k], recv_sem=r1.at[k],
                                         device_id=(px, py, pc), device_id_type=MESH).wait_recv()
        total = buf_ref[0]
        for k in range(1, 8):
            total = total + buf_ref[k]
        o_ref[mine, :] = total
        for k, (px, py, pc) in enumerate(peers):
            cp = pltpu.make_async_remote_copy(src_ref=o_ref.at[mine, :], dst_ref=o_ref.at[mine, :], send_sem=s2.at[k],
                                              recv_sem=r2.at[k], device_id=(px, py, pc), device_id_type=MESH)
            cp.start()
            second.append(cp)
        for k, (px, py, pc) in enumerate(peers):
            rows = o_ref.at[pl.ds(pl.multiple_of((4 * px + 2 * py + pc) * rs, 8), rs), :]
            pltpu.make_async_remote_copy(src_ref=rows, dst_ref=rows, send_sem=s2.at[k], recv_sem=r2.at[k],
                                         device_id=(px, py, pc), device_id_type=MESH).wait_recv()
        for cp in first + second:
            cp.wait_send()

    vm = pl.BlockSpec(memory_space=pltpu.VMEM)
    return pl.pallas_call(
        body, name="small_allreduce", in_specs=[vm], out_specs=vm, out_shape=jax.ShapeDtypeStruct(part.shape, F32),
        scratch_shapes=[pltpu.VMEM((8, rs, LANES), F32)] + [pltpu.SemaphoreType.DMA((7,))] * 4,
    )(part)


def _adamw(w, g, m, v, name):
    R, C = w.shape
    summed = g.ndim == 4
    if summed:
        tr = R // 2
    else:
        tr = R if R * C * 4 <= (1 << 21) else R // 2
        if tr % 8:
            tr = R
    c1 = 1.0 / (1.0 - ADAM_B1 ** ADAM_STEP)
    c2 = 1.0 / (1.0 - ADAM_B2 ** ADAM_STEP)

    def body(w_ref, g_ref, m_ref, v_ref, *outs):
        if summed:
            g_ = ((g_ref[0, 0].astype(F32) + g_ref[1, 0].astype(F32)) + g_ref[2, 0].astype(F32)) + g_ref[3, 0].astype(F32)
            outs[0][...] = g_
        else:
            g_ = g_ref[...]
        d_ref, mo_ref, vo_ref = outs[-3:]
        m_ = ADAM_B1 * m_ref[...] + (1.0 - ADAM_B1) * g_
        v_ = ADAM_B2 * v_ref[...] + (1.0 - ADAM_B2) * (g_ * g_)
        mo_ref[...] = m_
        vo_ref[...] = v_
        d_ref[...] = -ADAM_LR * ((m_ * c1) / (jnp.sqrt(v_ * c2) + ADAM_EPS) + ADAM_WD * w_ref[...])

    blk = BS((tr, C), lambda i: (i, 0))
    g_blk = BS((4, 1, tr, C), lambda i: (0, i, 0, 0)) if summed else blk
    nout = 4 if summed else 3
    return pl.pallas_call(body, name=name, grid=(R // tr,), in_specs=[blk, g_blk, blk, blk], out_specs=[blk] * nout,
                          out_shape=[jax.ShapeDtypeStruct((R, C), F32)] * nout,
                          compiler_params=_cp(("arbitrary",)))(w, g, m, v)


def _adamw_unit_rows(w, g, m, v, name):
    C, _, R = w.shape
    tc = min(256, C)
    c1 = 1.0 / (1.0 - ADAM_B1 ** ADAM_STEP)
    c2 = 1.0 / (1.0 - ADAM_B2 ** ADAM_STEP)

    def body(w_ref, g_ref, m_ref, v_ref, go_ref, d_ref, mo_ref, vo_ref):
        g_ = g_ref[...]
        go_ref[...] = g_
        m_ = ADAM_B1 * m_ref[...] + (1.0 - ADAM_B1) * g_
        v_ = ADAM_B2 * v_ref[...] + (1.0 - ADAM_B2) * (g_ * g_)
        mo_ref[...] = m_
        vo_ref[...] = v_
        d_ref[...] = -ADAM_LR * ((m_ * c1) / (jnp.sqrt(v_ * c2) + ADAM_EPS) + ADAM_WD * w_ref[...])

    blk = BS((tc, 1, R), lambda i: (i, 0, 0))
    return pl.pallas_call(body, name=name, grid=(pl.cdiv(C, tc),), in_specs=[blk] * 4, out_specs=[blk] * 4,
                          out_shape=[jax.ShapeDtypeStruct((C, 1, R), F32)] * 4,
                          compiler_params=_cp(("arbitrary",)))(w, g, m, v)


SMALL = ("g_pre_mix", "b_f", "g_sgu", "w_s", "b_s", "g_out_a", "g_out_b", "g_out_m", "g_mem", "g_post_mix",
         "g_pre_ffn", "g_post_ffn")
BIG = ("w_in", "w_mem_kv", "w_out", "w_gate", "w_up", "w_down")
TRANSPOSED = ("w_in", "w_gate", "w_up")
WEIGHTS = ("g_pre_mix", "w_in", "b_f", "g_sgu", "w_s", "b_s", "g_out_a", "g_out_b", "g_out_m", "g_mem", "w_mem_kv",
           "w_out", "g_post_mix", "g_pre_ffn", "w_gate", "w_up", "w_down", "g_post_ffn")


VECTORS = ("g_pre_mix", "b_f", "g_sgu", "g_out_a", "g_out_b", "g_out_m", "g_mem", "g_post_mix", "g_pre_ffn", "g_post_ffn")
VEC_ROWS = 16
WS_ROWS = N_FOX_HEADS * CHUNK
BS_ROWS = 8


def _pack_small(small, vw):
    stack = jnp.zeros((VEC_ROWS, vw), F32)
    for k, n in enumerate(VECTORS + ("loss",)):
        row = small[n].reshape(1, -1)
        stack = stack + jnp.pad(row, ((k, VEC_ROWS - 1 - k), (0, vw - row.shape[1])))
    parts = [small["w_s"].reshape(WS_ROWS, LANES), jnp.pad(small["b_s"], ((0, BS_ROWS - N_FOX_HEADS), (0, 0))),
             stack.reshape(-1, LANES)]
    rows = sum(p.shape[0] for p in parts)
    return jnp.concatenate(parts + [jnp.zeros((-rows % 64, LANES), F32)], axis=0)


def _adamw_small(vec_g, vec_wmv, ws, bs):
    c1 = 1.0 / (1.0 - ADAM_B1 ** ADAM_STEP)
    c2 = 1.0 / (1.0 - ADAM_B2 ** ADAM_STEP)
    nv = len(vec_wmv)

    def adam(g, w, m, v):
        m_ = ADAM_B1 * m + (1.0 - ADAM_B1) * g
        v_ = ADAM_B2 * v + (1.0 - ADAM_B2) * (g * g)
        return -ADAM_LR * ((m_ * c1) / (jnp.sqrt(v_ * c2) + ADAM_EPS) + ADAM_WD * w), m_, v_

    def body(*refs):
        vg_ref, ins, outs = refs[0], refs[1:1 + 3 * nv + 8], refs[1 + 3 * nv + 8:]
        for k in range(nv):
            w_ref, m_ref, v_ref = ins[3 * k:3 * k + 3]
            g = vg_ref[k:k + 1, 0:w_ref.shape[1]]
            d, m_, v_ = adam(g, w_ref[...], m_ref[...], v_ref[...])
            for o_ref, val in zip(outs[4 * k:4 * k + 4], (g, d, m_, v_)):
                o_ref[...] = val
        for j in range(2):
            g_ref, w_ref, m_ref, v_ref = ins[3 * nv + 4 * j:3 * nv + 4 * j + 4]
            for o_ref, val in zip(outs[4 * nv + 3 * j:4 * nv + 3 * j + 3], adam(g_ref[...], w_ref[...], m_ref[...], v_ref[...])):
                o_ref[...] = val

    vm = pl.BlockSpec(memory_space=pltpu.VMEM)
    operands = [vec_g] + [a for wmv in vec_wmv for a in wmv] + list(ws) + list(bs)
    out_shape = ([jax.ShapeDtypeStruct(wmv[0].shape, F32) for wmv in vec_wmv for _ in range(4)]
                 + [jax.ShapeDtypeStruct(ws[1].shape, F32)] * 3 + [jax.ShapeDtypeStruct(bs[1].shape, F32)] * 3)
    outs = pl.pallas_call(body, name="adamw_small", in_specs=[vm] * len(operands), out_specs=[vm] * len(out_shape),
                          out_shape=out_shape)(*operands)
    return [outs[4 * k:4 * k + 4] for k in range(nv)], outs[4 * nv:4 * nv + 3], outs[4 * nv + 3:]


def kernel(x, mem, g_pre_mix, w_in, b_f, g_sgu, w_s, b_s, g_out_a, g_out_b, g_out_m, g_mem, w_mem_kv, w_out, g_post_mix, g_pre_ffn, w_gate, w_up, w_down, g_post_ffn, loss_target, m_g_pre_mix, m_w_in, m_b_f, m_g_sgu, m_w_s, m_b_s, m_g_out_a, m_g_out_b, m_g_out_m, m_g_mem, m_w_mem_kv, m_w_out, m_g_post_mix, m_g_pre_ffn, m_w_gate, m_w_up, m_w_down, m_g_post_ffn, v_g_pre_mix, v_w_in, v_b_f, v_g_sgu, v_w_s, v_b_s, v_g_out_a, v_g_out_b, v_g_out_m, v_g_mem, v_w_mem_kv, v_w_out, v_g_post_mix, v_g_pre_ffn, v_w_gate, v_w_up, v_w_down, v_g_post_ffn):
    Wt = dict(g_pre_mix=g_pre_mix, w_in=w_in, b_f=b_f, g_sgu=g_sgu, w_s=w_s, b_s=b_s, g_out_a=g_out_a, g_out_b=g_out_b,
              g_out_m=g_out_m, g_mem=g_mem, w_mem_kv=w_mem_kv, w_out=w_out, g_post_mix=g_post_mix, g_pre_ffn=g_pre_ffn,
              w_gate=w_gate, w_up=w_up, w_down=w_down, g_post_ffn=g_post_ffn)
    Mo = dict(g_pre_mix=m_g_pre_mix, w_in=m_w_in, b_f=m_b_f, g_sgu=m_g_sgu, w_s=m_w_s, b_s=m_b_s, g_out_a=m_g_out_a,
              g_out_b=m_g_out_b, g_out_m=m_g_out_m, g_mem=m_g_mem, w_mem_kv=m_w_mem_kv, w_out=m_w_out,
              g_post_mix=m_g_post_mix, g_pre_ffn=m_g_pre_ffn, w_gate=m_w_gate, w_up=m_w_up, w_down=m_w_down,
              g_post_ffn=m_g_post_ffn)
    Vo = dict(g_pre_mix=v_g_pre_mix, w_in=v_w_in, b_f=v_b_f, g_sgu=v_g_sgu, w_s=v_w_s, b_s=v_b_s, g_out_a=v_g_out_a,
              g_out_b=v_g_out_b, g_out_m=v_g_out_m, g_mem=v_g_mem, w_mem_kv=v_w_mem_kv, w_out=v_w_out,
              g_post_mix=v_g_post_mix, g_pre_ffn=v_g_pre_ffn, w_gate=v_w_gate, w_up=v_w_up, w_down=v_w_down,
              g_post_ffn=v_g_post_ffn)

    gap = P_COLS - IN_COLS

    def to_kernel(n, w):
        if n in TRANSPOSED:
            w = w.T
        if n == "w_in":
            w = jnp.pad(w[:F_END], ((0, P_COLS - F_END), (0, 0))) + jnp.pad(w[F_END:], ((F_END + gap, 0), (0, 0)))
        return w

    def ungroup(g):
        return jnp.pad(g[:F_END], ((0, IN_COLS - F_END), (0, 0))) + jnp.pad(g[F_END + gap:], ((F_END, 0), (0, 0)))

    shards = {n: to_kernel(n, Wt[n][0]) for n in BIG}
    srcs = [shards[n].astype(BF16).reshape(2, shards[n].shape[0] // 2, shards[n].shape[1]) for n in BIG]
    fulls = (_exchange_on_sequencer(srcs[:1], True, "gather_w_in", 1)
             + _exchange_on_sequencer(srcs[1:3], True, "gather_kv_out", 2)
             + _exchange_on_sequencer(srcs[3:], True, "gather_ffn", 3))
    W = {}
    for n, f in zip(BIG, fulls):
        _, _, hr, C = f.shape
        W[n] = f.reshape(8 * hr, C) if n in ("w_mem_kv", "w_out") else f.reshape(4, 2 * hr, C)

    P = {n: Wt[n] for n in SMALL}
    grads, deltas, new_m, new_v = {}, {}, {}, {}

    def apply(names, landed):
        for n, r in zip(names, landed):
            if n == "w_in":
                g_t = ungroup(_sum_chips(r, "sum_chips_" + n))
                lift = lambda a: jnp.transpose(a, (2, 0, 1))
                outs = _adamw_unit_rows(lift(Wt[n]), g_t[:, None, :], lift(Mo[n]), lift(Vo[n]), "adamw_" + n)
                g, d, m1, v1 = [jnp.transpose(a, (1, 2, 0))[0] for a in outs]
            elif n in TRANSPOSED:
                g, d, m1, v1 = [a.T for a in _adamw(Wt[n][0].T, r, Mo[n][0].T, Vo[n][0].T, "adamw_" + n)]
            else:
                g, d, m1, v1 = _adamw(Wt[n][0], r, Mo[n][0], Vo[n][0], "adamw_" + n)
            grads[n], deltas[n], new_m[n], new_v[n] = g[None], d[None], m1[None], v1[None]
        return tuple(deltas[n] for n in names)

    core = lax.axis_index("c").astype(jnp.int32).reshape(1)
    reducer = _Reducer(core, apply)
    grad_x, pending, small = _local_step(x, mem, loss_target, W, P, reducer)

    vw = -(-max(x.shape[-1], A_W) // LANES) * LANES
    landed, (packed,) = reducer.finish("in", pending["w_in"], (_pack_small(small, vw),))
    total = _small_allreduce(packed)
    apply(BIG[:1], landed)

    lane_row = lambda a: jnp.pad(a, ((0, 0), (0, -a.shape[1] % LANES)))
    vec_g = total[WS_ROWS + BS_ROWS:WS_ROWS + BS_ROWS + VEC_ROWS * vw // LANES].reshape(VEC_ROWS, vw)
    ws_g = total[:WS_ROWS]
    bs_g = total[WS_ROWS:WS_ROWS + N_FOX_HEADS]
    rows = lambda a, r: a.reshape(r, LANES)
    per_vec, ws_out, bs_out = _adamw_small(
        vec_g, [tuple(lane_row(a[n]) for a in (Wt, Mo, Vo)) for n in VECTORS],
        (ws_g,) + tuple(rows(a["w_s"], WS_ROWS) for a in (Wt, Mo, Vo)),
        (bs_g,) + tuple(rows(a["b_s"], N_FOX_HEADS) for a in (Wt, Mo, Vo)))
    for n, outs in zip(VECTORS, per_vec):
        grads[n], deltas[n], new_m[n], new_v[n] = [o[:, :Wt[n].shape[1]] for o in outs]
    for n, g, outs in (("w_s", ws_g, ws_out), ("b_s", bs_g, bs_out)):
        grads[n], deltas[n], new_m[n], new_v[n] = [o.reshape(Wt[n].shape) for o in (g,) + tuple(outs)]
    loss = vec_g[len(VECTORS), 0]

    return (loss, grad_x, *[grads[n] for n in WEIGHTS], *[deltas[n] for n in WEIGHTS],
            *[new_m[n] for n in WEIGHTS], *[new_v[n] for n in WEIGHTS])
```

```python
import jax
import jax.numpy as jnp
from jax import lax
from jax.experimental import pallas as pl
from jax.experimental.pallas import tpu as pltpu
from jax.experimental.pallas import tpu_sc as plsc

F32 = jnp.float32
BF16 = jnp.bfloat16
EPS = 1e-6
NEG = -1e30
HEAD = 64
A_W, B_W, M_W = 384, 384, 256
N_FOX_HEADS = 6
CHUNK = 128
IN_COLS = 2 * A_W + 3 * B_W + N_FOX_HEADS + M_W
P_MAIN = 2 * A_W + 3 * B_W + M_W
LANES = 128
P_COLS = P_MAIN + LANES
F_END = 2 * A_W + 3 * B_W + N_FOX_HEADS
Q_BLK = 512
ROW_SPLIT = 4
ADAM_LR, ADAM_B1, ADAM_B2, ADAM_EPS, ADAM_WD, ADAM_STEP = 0.001, 0.9, 0.999, 1e-08, 0.01, 10
VMEM_LIMIT = 56 * 1024 * 1024
MESH = pl.DeviceIdType.MESH
BS = pl.BlockSpec


def _cp(sem=None):
    return pltpu.CompilerParams(dimension_semantics=sem, vmem_limit_bytes=VMEM_LIMIT)


def _iota(shape, dim):
    return lax.broadcasted_iota(jnp.int32, shape, dim)


def _dot(a, b):
    return jnp.dot(a.astype(BF16), b.astype(BF16), preferred_element_type=F32)


def _dot_nt(a, b):
    return lax.dot_general(a.astype(BF16), b.astype(BF16), (((1,), (1,)), ((), ())), preferred_element_type=F32)


def _dot_tn(a, b):
    return lax.dot_general(a.astype(BF16), b.astype(BF16), (((0,), (0,)), ((), ())), preferred_element_type=F32)


def _rms(x, g):
    return x * lax.rsqrt(jnp.mean(x * x, axis=-1, keepdims=True) + EPS) * g


def _rms_bwd(x, g, dy):
    r = lax.rsqrt(jnp.mean(x * x, axis=-1, keepdims=True) + EPS)
    xr = x * r
    gd = dy * g
    m = jnp.mean(gd * xr, axis=-1, keepdims=True)
    return (gd - xr * m) * r, _colsum(dy * xr)


def _gelu(x):
    return 0.5 * x * (1.0 + jnp.tanh(0.7978845608028654 * (x + 0.044715 * (x * x * x))))


def _sigmoid(x):
    return 1.0 / (1.0 + jnp.exp(-x))


def _silu_mul(g, u):
    return g * _sigmoid(g) * u


def _logsig(x):
    return jnp.minimum(x, 0.0) - jnp.log(1.0 + jnp.exp(-jnp.abs(x)))


def _colsum(x):
    return jnp.sum(x, axis=0, keepdims=True)


def _acc(ref, val, first):
    @pl.when(first)
    def _():
        ref[...] = val

    @pl.when(jnp.logical_not(first))
    def _():
        ref[...] += val


def _inproj_fwd(x2d, g_pre, w_in_p, tm):
    T, D = x2d.shape
    CH = 768
    nchunk = P_COLS // CH
    ns, _, dsh = w_in_p.shape

    def body(x_ref, g_ref, w_ref, h_ref, proj_ref, fl_ref):
        h = _rms(x_ref[...], g_ref[...]).astype(BF16)
        h_ref[...] = h
        for n in range(nchunk):
            rows = slice(n * CH, (n + 1) * CH)
            r = _dot_nt(h[:, 0:dsh], w_ref[0, rows, :])
            for s in range(1, ns):
                r = r + _dot_nt(h[:, s * dsh:(s + 1) * dsh], w_ref[s, rows, :])
            if n < nchunk - 1:
                proj_ref[:, rows] = r.astype(BF16)
            else:
                fg = 1920 - n * CH
                proj_ref[:, n * CH:1920] = r[:, :fg].astype(BF16)
                fl_ref[...] = r[:, fg:fg + LANES]
                proj_ref[:, 1920:P_MAIN] = r[:, fg + LANES:].astype(BF16)

    return pl.pallas_call(
        body, name="inproj_fwd", grid=(T // tm,),
        in_specs=[BS((tm, D), lambda i: (i, 0)), BS((1, D), lambda i: (0, 0)),
                  BS((ns, P_COLS, dsh), lambda i: (0, 0, 0))],
        out_specs=[BS((tm, D), lambda i: (i, 0)), BS((tm, P_MAIN), lambda i: (i, 0)), BS((tm, LANES), lambda i: (i, 0))],
        out_shape=[jax.ShapeDtypeStruct((T, D), BF16), jax.ShapeDtypeStruct((T, P_MAIN), BF16),
                   jax.ShapeDtypeStruct((T, LANES), F32)],
        compiler_params=_cp(("arbitrary",)),
    )(x2d, g_pre, w_in_p)


def _gate_fwd(flog3, bf_row):
    Bl, S, _ = flog3.shape
    nb = S // LANES

    def body(f_ref, b_ref, bq_ref, bk_ref, fr_ref):
        row = _iota((LANES, LANES), 0)
        lane = _iota((LANES, LANES), 1)
        one = jnp.ones((LANES, LANES), BF16)
        zero = jnp.zeros((LANES, LANES), BF16)

        carry = jnp.zeros((1, LANES), F32)
        for j in range(nb):
            r0 = j * LANES
            fl = f_ref[0, pl.ds(r0, LANES), :] + b_ref[...]
            fr_ref[0, j] = fl.T[0:8, :]
            c = _logsig(fl)
            for k in (1, 2, 4, 8, 16, 32, 64):
                c = c + jnp.where(row >= k, pltpu.roll(c, k, 0), 0.0)
            total = _colsum(jnp.where(row == LANES - 1, c, 0.0))
            c = c + carry
            carry = carry + total
            for h in range(N_FOX_HEADS):
                col = jnp.sum(jnp.where(lane == h, c, 0.0), axis=1, keepdims=True)
                hi = col.astype(BF16)
                rest = col - hi.astype(F32)
                mid = rest.astype(BF16)
                lo = (rest - mid.astype(F32)).astype(BF16)
                base = _bias_lane(h)
                bq = jnp.where(lane == base, hi, jnp.where(lane == base + 1, mid, jnp.where(lane == base + 2, lo, zero)))
                bq = jnp.where((lane >= base + 3) & (lane < base + 6), one, bq)
                bk = jnp.where(lane == base + 3, -hi, jnp.where(lane == base + 4, -mid, jnp.where(lane == base + 5, -lo, zero)))
                bk = jnp.where((lane >= base) & (lane < base + 3), one, bk)
                bq_ref[0, h, pl.ds(r0, LANES), :] = bq
                bk_ref[0, h, pl.ds(r0, LANES), :] = bk

    slab = BS((1, N_FOX_HEADS, S, LANES), lambda b: (b, 0, 0, 0))
    return pl.pallas_call(
        body, name="gate_fwd", grid=(Bl,),
        in_specs=[BS((1, S, LANES), lambda b: (b, 0, 0)), BS((1, LANES), lambda b: (0, 0))],
        out_specs=[slab, slab, BS((1, nb, 8, LANES), lambda b: (b, 0, 0, 0))],
        out_shape=[jax.ShapeDtypeStruct((Bl, N_FOX_HEADS, S, LANES), BF16),
                   jax.ShapeDtypeStruct((Bl, N_FOX_HEADS, S, LANES), BF16),
                   jax.ShapeDtypeStruct((Bl, nb, 8, LANES), F32)],
        compiler_params=_cp(("arbitrary",)),
    )(flog3, bf_row)


def _bias_lane(h):
    return HEAD if h % 2 == 0 else 0


def _sgu_pre(zu, zv, g_sgu):
    return _gelu(zu), _rms(_gelu(zv), g_sgu)


def _sgu_fwd(proj, g_sgu, ws_tril, bs_full, tm):
    T = proj.shape[0]
    nch = tm // CHUNK

    def body(zu_ref, zv_ref, g_ref, ws_ref, b_ref, ya_ref):
        lane = _iota((CHUNK, LANES), 1)
        u, vn = _sgu_pre(zu_ref[...].astype(F32), zv_ref[...].astype(F32), g_ref[...])
        vn = vn.astype(BF16)
        for c in range(nch):
            rs = slice(c * CHUNK, (c + 1) * CHUNK)
            for j in range(3):
                cs = slice(j * LANES, (j + 1) * LANES)
                vp = vn[rs, cs]
                z = jnp.where(lane < HEAD, _dot(ws_ref[2 * j], vp), _dot(ws_ref[2 * j + 1], vp)) + b_ref[:, cs]
                ya_ref[rs, cs] = (u[rs, cs] * z).astype(BF16)

    return pl.pallas_call(
        body, name="sgu_fwd", grid=(T // tm,),
        in_specs=[BS((tm, A_W), lambda i: (i, 0)), BS((tm, A_W), lambda i: (i, 1)), BS((1, A_W), lambda i: (0, 0)),
                  BS((6, CHUNK, CHUNK), lambda i: (0, 0, 0)), BS((CHUNK, A_W), lambda i: (0, 0))],
        out_specs=BS((tm, A_W), lambda i: (i, 0)),
        out_shape=jax.ShapeDtypeStruct((T, A_W), BF16),
        compiler_params=_cp(("arbitrary",)),
    )(proj, proj, g_sgu, ws_tril, bs_full)


def _fox_fwd(proj, bq, bk, Bl, S):
    T = Bl * S
    nq = S // Q_BLK
    qc, kc, vc = 768 // LANES, 1152 // LANES, 1536 // LANES

    def body(q_ref, k_ref, v_ref, bq_ref, bk_ref, o_ref, lse_ref, ka_ref, va_ref):
        lane_s = _iota((S, LANES), 1)
        lane = _iota((Q_BLK, LANES), 1)
        tri = _iota((Q_BLK, Q_BLK), 1) <= _iota((Q_BLK, Q_BLK), 0)
        k = k_ref[...]
        v = v_ref[...]
        for hh in range(2):
            data = (lane_s < HEAD) if hh == 0 else (lane_s >= HEAD)
            ka_ref[hh] = jnp.where(data, k, bk_ref[0, hh])
            va_ref[hh] = jnp.where(lane_s == _bias_lane(hh), jnp.ones_like(v), v)
        for i in range(nq):
            r0 = i * Q_BLK
            q = q_ref[r0:r0 + Q_BLK, :]
            o_out = jnp.zeros((Q_BLK, LANES), F32)
            lse_out = jnp.zeros((Q_BLK, LANES), F32)
            for hh in range(2):
                hmask = (lane < HEAD) if hh == 0 else (lane >= HEAD)
                qa = jnp.where(hmask, q * 0.125, bq_ref[0, hh, r0:r0 + Q_BLK, :])
                sd = jnp.where(tri, _dot_nt(qa, ka_ref[hh, r0:r0 + Q_BLK, :]), NEG)
                m = jnp.max(sd, axis=1, keepdims=True)
                if i:
                    sf = _dot_nt(qa, ka_ref[hh, 0:r0, :])
                    m = jnp.maximum(m, jnp.max(sf, axis=1, keepdims=True))
                acc = _dot(jnp.exp(sd - m), va_ref[hh, r0:r0 + Q_BLK, :])
                if i:
                    acc = acc + _dot(jnp.exp(sf - m), va_ref[hh, 0:r0, :])
                l = jnp.sum(jnp.where(lane == _bias_lane(hh), acc, 0.0), axis=1, keepdims=True)
                o_out = jnp.where(hmask, acc / l, o_out)
                lse_out = jnp.where(hmask, m + jnp.log(l), lse_out)
            o_ref[r0:r0 + Q_BLK, :] = o_out.astype(BF16)
            lse_ref[0, r0:r0 + Q_BLK, :] = lse_out

    seq = lambda c0: BS((S, LANES), lambda b, p: (b, c0 + p))
    pair = BS((1, 2, S, LANES), lambda b, p: (b, p, 0, 0))
    return pl.pallas_call(
        body, name="fox_fwd", grid=(Bl, 3),
        in_specs=[seq(qc), seq(kc), seq(vc), pair, pair],
        out_specs=[seq(0), BS((1, S, LANES), lambda b, p: (p, b, 0))],
        out_shape=[jax.ShapeDtypeStruct((T, B_W), BF16), jax.ShapeDtypeStruct((3, T, LANES), F32)],
        scratch_shapes=[pltpu.VMEM((2, S, LANES), BF16), pltpu.VMEM((2, S, LANES), BF16)],
        compiler_params=_cp(("arbitrary", "arbitrary")),
    )(proj, proj, proj, bq, bk)


def _memkv_fwd(mem, g_mem, w_kv):
    Bl, Mt, D = mem.shape

    def body(m_ref, g_ref, w_ref, mn_ref, kv_ref):
        mn = _rms(m_ref[0], g_ref[...]).astype(BF16)
        mn_ref[0] = mn
        kv_ref[0] = jnp.dot(mn, w_ref[...], preferred_element_type=F32).astype(BF16)

    return pl.pallas_call(
        body, name="memkv_fwd", grid=(Bl,),
        in_specs=[BS((1, Mt, D), lambda b: (b, 0, 0)), BS((1, D), lambda b: (0, 0)), BS((D, 2 * M_W), lambda b: (0, 0))],
        out_specs=[BS((1, Mt, D), lambda b: (b, 0, 0)), BS((1, Mt, 2 * M_W), lambda b: (b, 0, 0))],
        out_shape=[jax.ShapeDtypeStruct((Bl, Mt, D), BF16), jax.ShapeDtypeStruct((Bl, Mt, 2 * M_W), BF16)],
        compiler_params=_cp(("arbitrary",)),
    )(mem, g_mem, w_kv)


def _memattn_fwd(proj, kv, Bl, S, tq):
    T = Bl * S
    nq = S // tq
    Mt = kv.shape[1]
    qc = 1920 // LANES

    def body(q_ref, km_ref, vm_ref, o_ref):
        lane = _iota((tq, LANES), 1)
        q = q_ref[...]
        out = jnp.zeros((tq, LANES), F32)
        for hh in range(2):
            hmask = (lane < HEAD) if hh == 0 else (lane >= HEAD)
            qs = jnp.where(hmask, q, jnp.zeros_like(q)) * 0.125
            s = _dot_nt(qs, km_ref[0])
            pe = jnp.exp(s - jnp.max(s, axis=1, keepdims=True))
            pn = pe / jnp.sum(pe, axis=1, keepdims=True)
            out = jnp.where(hmask, _dot(pn, vm_ref[0]), out)
        o_ref[...] = out.astype(BF16)

    return pl.pallas_call(
        body, name="memattn_fwd", grid=(Bl, 2, nq),
        in_specs=[BS((tq, LANES), lambda b, p, i: (b * nq + i, qc + p)),
                  BS((1, Mt, LANES), lambda b, p, i: (b, 0, p)),
                  BS((1, Mt, LANES), lambda b, p, i: (b, 0, 2 + p))],
        out_specs=BS((tq, LANES), lambda b, p, i: (b * nq + i, p)),
        out_shape=jax.ShapeDtypeStruct((T, M_W), BF16),
        compiler_params=_cp(("arbitrary", "arbitrary", "arbitrary")),
    )(proj, kv, kv)


def _mix_norms(ya, yb, ym, ga, gb, gm):
    return _rms(ya, ga), _rms(yb, gb), _rms(ym, gm)


def _outproj_fwd(ya, yb, ym, x2d, ga, gb, gm, g_post, g_pre2, w_out, tm):
    T, D = x2d.shape

    def body(ya_ref, yb_ref, ym_ref, x_ref, ga_ref, gb_ref, gm_ref, gp_ref, g2_ref, w_ref,
             y_ref, o_ref, x1_ref, h2_ref):
        na, nb_, nm = _mix_norms(ya_ref[...].astype(F32), yb_ref[...].astype(F32), ym_ref[...].astype(F32),
                                 ga_ref[...], gb_ref[...], gm_ref[...])
        y_ref[:, 0:A_W] = na.astype(BF16)
        y_ref[:, A_W:A_W + B_W] = nb_.astype(BF16)
        y_ref[:, A_W + B_W:] = nm.astype(BF16)
        o = jnp.dot(y_ref[...], w_ref[...], preferred_element_type=F32).astype(BF16)
        o_ref[...] = o
        x1 = x_ref[...] + _rms(o.astype(F32), gp_ref[...])
        x1_ref[...] = x1
        h2_ref[...] = _rms(x1, g2_ref[...]).astype(BF16)

    row = lambda w: BS((tm, w), lambda i: (i, 0))
    vec = lambda w: BS((1, w), lambda i: (0, 0))
    return pl.pallas_call(
        body, name="outproj_fwd", grid=(T // tm,),
        in_specs=[row(A_W), row(B_W), row(M_W), row(D), vec(A_W), vec(B_W), vec(M_W), vec(D), vec(D),
                  BS((A_W + B_W + M_W, D), lambda i: (0, 0))],
        out_specs=[row(A_W + B_W + M_W), row(D), row(D), row(D)],
        out_shape=[jax.ShapeDtypeStruct((T, A_W + B_W + M_W), BF16), jax.ShapeDtypeStruct((T, D), BF16),
                   jax.ShapeDtypeStruct((T, D), F32), jax.ShapeDtypeStruct((T, D), BF16)],
        compiler_params=_cp(("arbitrary",)),
    )(ya, yb, ym, x2d, ga, gb, gm, g_post, g_pre2, w_out)


def _ffn_fwd(h2, x1, target, wg, wu, wd, g_post, tm):
    T, D = x1.shape
    ns, F, _ = wg.shape

    def body(h_ref, x1_ref, t_ref, wg_ref, wu_ref, wd_ref, gp_ref,
             gs_ref, us_ref, dff_ref, dx2_ref, dgp_ref, loss_ref, acc_ref):
        j = pl.program_id(0)
        i = pl.program_id(1)
        rows = pl.ds(pl.multiple_of(i * tm, tm), tm)
        h = h_ref[...]
        g = _dot_nt(h, wg_ref[0])
        u = _dot_nt(h, wu_ref[0])
        gs_ref[0] = g.astype(BF16)
        us_ref[0] = u.astype(BF16)
        part = _dot(_silu_mul(g, u), wd_ref[0])

        @pl.when(j == 0)
        def _():
            acc_ref[rows, :] = part

        @pl.when(j != 0)
        def _():
            acc_ref[rows, :] += part

        @pl.when(j == ns - 1)
        def _():
            ff = acc_ref[rows, :]
            diff = x1_ref[...] + _rms(ff, gp_ref[...]) - t_ref[...]
            dx2 = diff * (1.0 / D)
            dff, dgp = _rms_bwd(ff, gp_ref[...], dx2)
            dx2_ref[...] = dx2
            dff_ref[...] = dff.astype(BF16)
            lpart = jnp.sum(_colsum(diff * diff), axis=1, keepdims=True) * (0.5 / D)
            _acc(dgp_ref, dgp, i == 0)
            _acc(loss_ref, jnp.broadcast_to(lpart, (1, LANES)), i == 0)

    last = lambda j, i: (jnp.where(j == ns - 1, i, 0), 0)
    wsh = BS((1, F, D), lambda j, i: (j, 0, 0))
    sh = BS((1, tm, F), lambda j, i: (j, i, 0))
    return pl.pallas_call(
        body, name="ffn_fwd", grid=(ns, T // tm),
        in_specs=[BS((tm, D), lambda j, i: (i, 0)), BS((tm, D), last), BS((tm, D), last), wsh, wsh, wsh,
                  BS((1, D), lambda j, i: (0, 0))],
        out_specs=[sh, sh, BS((tm, D), last), BS((tm, D), last),
                   BS((1, D), lambda j, i: (0, 0)), BS((1, LANES), lambda j, i: (0, 0))],
        out_shape=[jax.ShapeDtypeStruct((ns, T, F), BF16), jax.ShapeDtypeStruct((ns, T, F), BF16),
                   jax.ShapeDtypeStruct((T, D), BF16), jax.ShapeDtypeStruct((T, D), F32),
                   jax.ShapeDtypeStruct((1, D), F32), jax.ShapeDtypeStruct((1, LANES), F32)],
        scratch_shapes=[pltpu.VMEM((T, D), F32)],
        compiler_params=_cp(("arbitrary", "arbitrary")),
    )(h2, x1, target, wg, wu, wd, g_post)


def _ffn_bwd(dff, h2, gs, us, wg, wu, wd, tm):
    T, D = h2.shape
    ns, F, _ = wg.shape

    def body(dff_ref, h_ref, gs_ref, us_ref, wg_ref, wu_ref, wd_ref, dh_ref, dwg_out, dwu_out, dwd_out,
             dwg_ref, dwu_ref, dwd_ref):
        first = pl.program_id(1) == 0
        dff = dff_ref[...]
        h = h_ref[...]
        parts = []
        for r in range(ROW_SPLIT):
            rows = slice(r * (tm // ROW_SPLIT), (r + 1) * (tm // ROW_SPLIT))
            dact = _dot_nt(dff[rows], wd_ref[0])
            g = gs_ref[0, rows, :].astype(F32)
            u = us_ref[0, rows, :].astype(F32)
            sig = _sigmoid(g)
            gsig = g * sig
            dg = (dact * u * (sig + gsig * (1.0 - sig))).astype(BF16)
            du = (dact * gsig).astype(BF16)
            dh_ref[0, rows, :] = (_dot(dg, wg_ref[0]) + _dot(du, wu_ref[0])).astype(BF16)
            parts.append(((gsig * u).astype(BF16), dg, du))
        a, dg, du = [jnp.concatenate(p, axis=0) for p in zip(*parts)]
        _acc(dwd_ref, _dot_tn(a, dff), first)
        _acc(dwg_ref, _dot_tn(dg, h), first)
        _acc(dwu_ref, _dot_tn(du, h), first)

        @pl.when(pl.program_id(1) == pl.num_programs(1) - 1)
        def _():
            dwg_out[0] = dwg_ref[...].astype(BF16)
            dwu_out[0] = dwu_ref[...].astype(BF16)
            dwd_out[0] = dwd_ref[...].astype(BF16)

    row = BS((tm, D), lambda j, i: (i, 0))
    sh = BS((1, tm, F), lambda j, i: (j, i, 0))
    wsh = BS((1, F, D), lambda j, i: (j, 0, 0))
    return pl.pallas_call(
        body, name="ffn_bwd", grid=(ns, T // tm),
        in_specs=[row, row, sh, sh, wsh, wsh, wsh],
        out_specs=[BS((1, tm, D), lambda j, i: (j, i, 0)), wsh, wsh, wsh],
        out_shape=[jax.ShapeDtypeStruct((ns, T, D), BF16)] + [jax.ShapeDtypeStruct((ns, F, D), BF16)] * 3,
        scratch_shapes=[pltpu.VMEM((F, D), F32)] * 3,
        compiler_params=_cp(("arbitrary", "arbitrary")),
    )(dff, h2, gs, us, wg, wu, wd)


DPROJ_PIECES = ((0, A_W), (A_W, A_W), (768, B_W), (1152, B_W), (1536, B_W), (1920, LANES), (2048, M_W))


def _put_dproj(dp_ref, piece_refs):
    for (c0, w), ref in zip(DPROJ_PIECES, piece_refs):
        dp_ref[:, c0:c0 + w] = ref[...].astype(BF16)


def _dw_in(pieces, h, ns, tk):
    T, D = h.shape
    M = P_COLS
    dsh = D // ns
    tk = min(tk, T)

    def body(*refs):
        piece_refs, h_ref, o_ref, acc_ref, dp_ref = refs[:7], refs[7], refs[8], refs[9], refs[10]
        t = pl.program_id(0)
        _put_dproj(dp_ref, piece_refs)
        _acc(acc_ref, _dot_tn(h_ref[...], dp_ref[...]), t == 0)

        @pl.when(t == pl.num_programs(0) - 1)
        def _():
            for s in range(ns):
                o_ref[s] = acc_ref[s * dsh:(s + 1) * dsh, :].T.astype(BF16)

    return pl.pallas_call(
        body, name="dw_in", grid=(T // tk,),
        in_specs=[BS((tk, w), lambda t: (t, 0)) for _, w in DPROJ_PIECES] + [BS((tk, D), lambda t: (t, 0))],
        out_specs=BS((ns, M, dsh), lambda t: (0, 0, 0)),
        out_shape=jax.ShapeDtypeStruct((ns, M, dsh), BF16),
        scratch_shapes=[pltpu.VMEM((D, M), F32), pltpu.VMEM((tk, M), BF16)],
        compiler_params=_cp(("arbitrary",)),
    )(*pieces, h)


def _outproj_bwd(dh2, x1, dx2, o, y, ya, yb, ym, ga, gb, gm, g_post, g_pre2, w_out, tm):
    T, D = x1.shape
    ns = dh2.shape[0]

    def body(dh_ref, x1_ref, dx2_ref, o_ref, y_ref, ya_ref, yb_ref, ym_ref, ga_ref, gb_ref, gm_ref, gp_ref, g2_ref, w_ref,
             dx1_ref, dw_ref, dya_ref, dyb_ref, dym_ref, dga_ref, dgb_ref, dgm_ref, dgp_ref, dg2_ref):
        first = pl.program_id(0) == 0
        dh = dh_ref[0].astype(F32)
        for j in range(1, ns):
            dh = dh + dh_ref[j].astype(F32)
        dxa, dg2 = _rms_bwd(x1_ref[...], g2_ref[...], dh)
        dx1 = dx2_ref[...] + dxa
        dx1_ref[...] = dx1
        _acc(dg2_ref, dg2, first)
        do, dgp = _rms_bwd(o_ref[...].astype(F32), gp_ref[...], dx1)
        do = do.astype(BF16)
        _acc(dw_ref, _dot_tn(y_ref[...], do), first)
        dy = _dot_nt(do, w_ref[...])
        dya, dga = _rms_bwd(ya_ref[...].astype(F32), ga_ref[...], dy[:, 0:A_W])
        dyb, dgb = _rms_bwd(yb_ref[...].astype(F32), gb_ref[...], dy[:, A_W:A_W + B_W])
        dym, dgm = _rms_bwd(ym_ref[...].astype(F32), gm_ref[...], dy[:, A_W + B_W:])
        dya_ref[...] = dya.astype(BF16)
        dyb_ref[...] = dyb.astype(BF16)
        dym_ref[...] = dym.astype(BF16)
        _acc(dga_ref, dga, first)
        _acc(dgb_ref, dgb, first)
        _acc(dgm_ref, dgm, first)
        _acc(dgp_ref, dgp, first)

    row = lambda w: BS((tm, w), lambda i: (i, 0))
    vec = lambda w: BS((1, w), lambda i: (0, 0))
    sds = jax.ShapeDtypeStruct
    return pl.pallas_call(
        body, name="outproj_bwd", grid=(T // tm,),
        in_specs=[BS((ns, tm, D), lambda i: (0, i, 0)), row(D), row(D), row(D), row(A_W + B_W + M_W), row(A_W), row(B_W),
                  row(M_W), vec(A_W), vec(B_W), vec(M_W), vec(D), vec(D), BS((A_W + B_W + M_W, D), lambda i: (0, 0))],
        out_specs=[row(D), BS((A_W + B_W + M_W, D), lambda i: (0, 0)), row(A_W), row(B_W), row(M_W),
                   vec(A_W), vec(B_W), vec(M_W), vec(D), vec(D)],
        out_shape=[sds((T, D), F32), sds((A_W + B_W + M_W, D), F32), sds((T, A_W), BF16), sds((T, B_W), BF16),
                   sds((T, M_W), BF16), sds((1, A_W), F32), sds((1, B_W), F32), sds((1, M_W), F32), sds((1, D), F32),
                   sds((1, D), F32)],
        compiler_params=_cp(("arbitrary",)),
    )(dh2, x1, dx2, o, y, ya, yb, ym, ga, gb, gm, g_post, g_pre2, w_out)


def _sgu_bwd(proj, dya, g_sgu, ws_tril, bs_full, tm):
    T = proj.shape[0]
    nch = tm // CHUNK

    def body(zu_ref, zv_ref, dy_ref, g_ref, ws_ref, b_ref, dzu_ref, dzv_ref, dws_ref, dbs_ref, dg_ref,
             du_ref, dvn_ref, dbf_ref):
        step = pl.program_id(0)
        first = step == 0
        lane = _iota((CHUNK, LANES), 1)
        tril = _iota((CHUNK, CHUNK), 0) >= _iota((CHUNK, CHUNK), 1)
        (u, vn), vjp = jax.vjp(_sgu_pre, zu_ref[...].astype(F32), zv_ref[...].astype(F32), g_ref[...])
        vnb = vn.astype(BF16)
        dy = dy_ref[...].astype(F32)

        @pl.when(first)
        def _():
            dws_ref[...] = jnp.zeros_like(dws_ref)
            dbf_ref[...] = jnp.zeros_like(dbf_ref)

        for c in range(nch):
            rs = slice(c * CHUNK, (c + 1) * CHUNK)
            for j in range(3):
                cs = slice(j * LANES, (j + 1) * LANES)
                vp = vnb[rs, cs]
                z = jnp.where(lane < HEAD, _dot(ws_ref[2 * j], vp), _dot(ws_ref[2 * j + 1], vp)) + b_ref[:, cs]
                du_ref[rs, cs] = dy[rs, cs] * z
                dz = dy[rs, cs] * u[rs, cs]
                dbf_ref[:, cs] += dz
                dzb = dz.astype(BF16)
                dz0 = jnp.where(lane < HEAD, dzb, jnp.zeros_like(dzb))
                dz1 = jnp.where(lane >= HEAD, dzb, jnp.zeros_like(dzb))
                dvn_ref[rs, cs] = jnp.where(lane < HEAD, _dot_tn(ws_ref[2 * j], dzb), _dot_tn(ws_ref[2 * j + 1], dzb))
                dws_ref[2 * j] += jnp.where(tril, _dot_nt(dz0, vp), 0.0)
                dws_ref[2 * j + 1] += jnp.where(tril, _dot_nt(dz1, vp), 0.0)
        dzu, dzv, dg = vjp((du_ref[...], dvn_ref[...]))
        dzu_ref[...] = dzu.astype(BF16)
        dzv_ref[...] = dzv.astype(BF16)
        _acc(dg_ref, dg, first)

        @pl.when(step == pl.num_programs(0) - 1)
        def _():
            out = jnp.zeros((CHUNK, LANES), F32)
            for j in range(3):
                slab = dbf_ref[:, j * LANES:(j + 1) * LANES]
                lo = jnp.sum(jnp.where(lane < HEAD, slab, 0.0), axis=1, keepdims=True)
                hi = jnp.sum(jnp.where(lane >= HEAD, slab, 0.0), axis=1, keepdims=True)
                out = out + jnp.where(lane == 2 * j, lo, 0.0) + jnp.where(lane == 2 * j + 1, hi, 0.0)
            dbs_ref[...] = out

    return pl.pallas_call(
        body, name="sgu_bwd", grid=(T // tm,),
        in_specs=[BS((tm, A_W), lambda i: (i, 0)), BS((tm, A_W), lambda i: (i, 1)), BS((tm, A_W), lambda i: (i, 0)),
                  BS((1, A_W), lambda i: (0, 0)), BS((6, CHUNK, CHUNK), lambda i: (0, 0, 0)),
                  BS((CHUNK, A_W), lambda i: (0, 0))],
        out_specs=[BS((tm, A_W), lambda i: (i, 0)), BS((tm, A_W), lambda i: (i, 0)),
                   BS((6, CHUNK, CHUNK), lambda i: (0, 0, 0)), BS((CHUNK, LANES), lambda i: (0, 0)),
                   BS((1, A_W), lambda i: (0, 0))],
        out_shape=[jax.ShapeDtypeStruct((T, A_W), BF16), jax.ShapeDtypeStruct((T, A_W), BF16),
                   jax.ShapeDtypeStruct((6, CHUNK, CHUNK), F32), jax.ShapeDtypeStruct((CHUNK, LANES), F32),
                   jax.ShapeDtypeStruct((1, A_W), F32)],
        scratch_shapes=[pltpu.VMEM((tm, A_W), F32), pltpu.VMEM((tm, A_W), F32), pltpu.VMEM((CHUNK, A_W), F32)],
        compiler_params=_cp(("arbitrary",)),
    )(proj, proj, dya, g_sgu, ws_tril, bs_full)


def _memattn_bwd(proj, kv, dym, Bl, S, tq):
    T = Bl * S
    nq = S // tq
    Mt = kv.shape[1]
    qc = 1920 // LANES

    def body(q_ref, km_ref, vm_ref, do_ref, dq_ref, dkm_ref, dvm_ref):
        first = pl.program_id(2) == 0
        lane = _iota((tq, LANES), 1)
        q = q_ref[...]
        do = do_ref[...]
        dq_out = jnp.zeros((tq, LANES), F32)
        dkm = jnp.zeros((Mt, LANES), F32)
        dvm = jnp.zeros((Mt, LANES), F32)
        for hh in range(2):
            hmask = (lane < HEAD) if hh == 0 else (lane >= HEAD)
            qs = jnp.where(hmask, q, jnp.zeros_like(q)) * 0.125
            dom = jnp.where(hmask, do, 0.0).astype(BF16)
            s = _dot_nt(qs, km_ref[0])
            pe = jnp.exp(s - jnp.max(s, axis=1, keepdims=True))
            pn = pe / jnp.sum(pe, axis=1, keepdims=True)
            dp = _dot_nt(dom, vm_ref[0])
            ds = (pn * (dp - jnp.sum(pn * dp, axis=1, keepdims=True))).astype(BF16)
            dq_out = jnp.where(hmask, _dot(ds, km_ref[0]) * 0.125, dq_out)
            dkm = dkm + _dot_tn(ds, qs)
            dvm = dvm + _dot_tn(pn, dom)
        dq_ref[...] = dq_out.astype(BF16)
        _acc(dkm_ref, dkm[None], first)
        _acc(dvm_ref, dvm[None], first)

    return pl.pallas_call(
        body, name="memattn_bwd", grid=(Bl, 2, nq),
        in_specs=[BS((tq, LANES), lambda b, p, i: (b * nq + i, qc + p)),
                  BS((1, Mt, LANES), lambda b, p, i: (b, 0, p)),
                  BS((1, Mt, LANES), lambda b, p, i: (b, 0, 2 + p)),
                  BS((tq, LANES), lambda b, p, i: (b * nq + i, p))],
        out_specs=[BS((tq, LANES), lambda b, p, i: (b * nq + i, p)),
                   BS((1, Mt, LANES), lambda b, p, i: (b, 0, p)),
                   BS((1, Mt, LANES), lambda b, p, i: (b, 0, p))],
        out_shape=[jax.ShapeDtypeStruct((T, M_W), BF16), jax.ShapeDtypeStruct((Bl, Mt, M_W), F32),
                   jax.ShapeDtypeStruct((Bl, Mt, M_W), F32)],
        compiler_params=_cp(("arbitrary", "arbitrary", "arbitrary")),
    )(proj, kv, kv, dym)


def _memkv_bwd(dkm, dvm, memn, mem, g_mem, w_kv):
    Bl, Mt, D = mem.shape

    def body(dk_ref, dv_ref, mn_ref, m_ref, g_ref, w_ref, dw_ref, dg_ref):
        first = pl.program_id(0) == 0
        dk = dk_ref[0].astype(BF16)
        dv = dv_ref[0].astype(BF16)
        mn = mn_ref[0]
        dmn = _dot_nt(dk, w_ref[:, 0:M_W]) + _dot_nt(dv, w_ref[:, M_W:])
        _, dg = _rms_bwd(m_ref[0], g_ref[...], dmn)
        _acc(dg_ref, dg, first)

        @pl.when(first)
        def _():
            dw_ref[...] = jnp.zeros_like(dw_ref)

        dw_ref[:, 0:M_W] += _dot_tn(mn, dk)
        dw_ref[:, M_W:] += _dot_tn(mn, dv)

    return pl.pallas_call(
        body, name="memkv_bwd", grid=(Bl,),
        in_specs=[BS((1, Mt, M_W), lambda b: (b, 0, 0)), BS((1, Mt, M_W), lambda b: (b, 0, 0)),
                  BS((1, Mt, D), lambda b: (b, 0, 0)), BS((1, Mt, D), lambda b: (b, 0, 0)),
                  BS((1, D), lambda b: (0, 0)), BS((D, 2 * M_W), lambda b: (0, 0))],
        out_specs=[BS((D, 2 * M_W), lambda b: (0, 0)), BS((1, D), lambda b: (0, 0))],
        out_shape=[jax.ShapeDtypeStruct((D, 2 * M_W), F32), jax.ShapeDtypeStruct((1, D), F32)],
        compiler_params=_cp(("arbitrary",)),
    )(dkm, dvm, memn, mem, g_mem, w_kv)


def _fox_bwd(proj, dyb, lse, bq, bk, Bl, S):
    T = Bl * S
    nq = S // Q_BLK
    nb = S // LANES
    qc, kc, vc = 768 // LANES, 1152 // LANES, 1536 // LANES

    def body(q_ref, k_ref, v_ref, do_ref, lse_ref, bq_ref, bk_ref,
             dq_ref, dk_ref, dv_ref, dcr_ref, ka_ref, dka_ref, dva_ref):
        p = pl.program_id(1)
        lane_s = _iota((S, LANES), 1)
        lane = _iota((Q_BLK, LANES), 1)
        sub = _iota((8, LANES), 0)
        tri = _iota((Q_BLK, Q_BLK), 1) <= _iota((Q_BLK, Q_BLK), 0)
        k = k_ref[...]
        for hh in range(2):
            data = (lane_s < HEAD) if hh == 0 else (lane_s >= HEAD)
            ka_ref[hh] = jnp.where(data, k, bk_ref[0, hh])
        dka_ref[...] = jnp.zeros_like(dka_ref)
        dva_ref[...] = jnp.zeros_like(dva_ref)

        @pl.when(p == 0)
        def _():
            dcr_ref[...] = jnp.zeros_like(dcr_ref)

        def add_colsums(ds, first_blk, h):
            cs = _colsum(ds)
            for jb in range(ds.shape[1] // LANES):
                dcr_ref[0, first_blk + jb] += jnp.where(sub == h, cs[:, jb * LANES:(jb + 1) * LANES], 0.0)

        for i in range(nq):
            r0 = i * Q_BLK
            r1 = r0 + Q_BLK
            q = q_ref[r0:r1, :]
            do = do_ref[r0:r1, :]
            lse_b = lse_ref[0, r0:r1, :]
            dq_out = jnp.zeros((Q_BLK, LANES), F32)
            for hh in range(2):
                hmask = (lane < HEAD) if hh == 0 else (lane >= HEAD)
                h = 2 * p + hh
                qs = jnp.where(hmask, q * 0.125, jnp.zeros_like(q))
                qa = jnp.where(hmask, q * 0.125, bq_ref[0, hh, r0:r1, :])
                dob = jnp.where(hmask, do, 0.0).astype(BF16)
                lse_h = jnp.sum(jnp.where(lane == hh * HEAD, lse_b, 0.0), axis=1, keepdims=True)
                pd = jnp.where(tri, jnp.exp(_dot_nt(qa, ka_ref[hh, r0:r1, :]) - lse_h), 0.0)
                dpd = _dot_nt(dob, v_ref[r0:r1, :])
                delta = jnp.sum(pd * dpd, axis=1, keepdims=True)
                psum = jnp.sum(pd, axis=1, keepdims=True)
                if i:
                    pf = jnp.exp(_dot_nt(qa, ka_ref[hh, 0:r0, :]) - lse_h)
                    dpf = _dot_nt(dob, v_ref[0:r0, :])
                    delta = delta + jnp.sum(pf * dpf, axis=1, keepdims=True)
                    psum = psum + jnp.sum(pf, axis=1, keepdims=True)
                delta = delta / psum
                dsd = pd * (dpd - delta)
                add_colsums(dsd, r0 // LANES, h)
                dsd = dsd.astype(BF16)
                dq_h = _dot(dsd, k_ref[r0:r1, :])
                dka_ref[r0:r1, :] += _dot_tn(dsd, qs)
                dva_ref[r0:r1, :] += _dot_tn(pd, dob)
                if i:
                    dsf = pf * (dpf - delta)
                    add_colsums(dsf, 0, h)
                    dsf = dsf.astype(BF16)
                    dq_h = dq_h + _dot(dsf, k_ref[0:r0, :])
                    dka_ref[0:r0, :] += _dot_tn(dsf, qs)
                    dva_ref[0:r0, :] += _dot_tn(pf, dob)
                dq_out = jnp.where(hmask, dq_h * 0.125, dq_out)
            dq_ref[r0:r1, :] = dq_out.astype(BF16)
        dk_ref[...] = dka_ref[...].astype(BF16)
        dv_ref[...] = dva_ref[...].astype(BF16)

    seq = lambda c0: BS((S, LANES), lambda b, p: (b, c0 + p))
    pair = BS((1, 2, S, LANES), lambda b, p: (b, p, 0, 0))
    rowblk = BS((1, nb, 8, LANES), lambda b, p: (b, 0, 0, 0))
    return pl.pallas_call(
        body, name="fox_bwd", grid=(Bl, 3),
        in_specs=[seq(qc), seq(kc), seq(vc), seq(0), BS((1, S, LANES), lambda b, p: (p, b, 0)), pair, pair],
        out_specs=[seq(0), seq(0), seq(0), rowblk],
        out_shape=[jax.ShapeDtypeStruct((T, B_W), BF16)] * 3 + [jax.ShapeDtypeStruct((Bl, nb, 8, LANES), F32)],
        scratch_shapes=[pltpu.VMEM((2, S, LANES), BF16), pltpu.VMEM((S, LANES), F32), pltpu.VMEM((S, LANES), F32)],
        compiler_params=_cp(("arbitrary", "arbitrary")),
    )(proj, proj, proj, dyb, lse, bq, bk)


def _gate_bwd(dc_row, fl_row):
    Bl, nb, _, _ = dc_row.shape

    def body(dc_ref, fl_ref, o_ref):
        lane = _iota((8, LANES), 1)

        carry = jnp.zeros((8, 1), F32)
        for j in reversed(range(nb)):
            r = -dc_ref[0, j]
            for k in (1, 2, 4, 8, 16, 32, 64):
                r = r + jnp.where(lane < LANES - k, pltpu.roll(r, LANES - k, 1), 0.0)
            total = jnp.sum(jnp.where(lane == 0, r, 0.0), axis=1, keepdims=True)
            dfl = (r + carry) * _sigmoid(-fl_ref[0, j])
            carry = carry + total
            o_ref[0, j * LANES:(j + 1) * LANES, :] = jnp.concatenate(
                [dfl, jnp.zeros((LANES - 8, LANES), F32)], axis=0).T

    rowblk = BS((1, nb, 8, LANES), lambda b: (b, 0, 0, 0))
    return pl.pallas_call(
        body, name="gate_bwd", grid=(Bl,),
        in_specs=[rowblk, rowblk],
        out_specs=BS((1, nb * LANES, LANES), lambda b: (b, 0, 0)),
        out_shape=jax.ShapeDtypeStruct((Bl, nb * LANES, LANES), F32),
        compiler_params=_cp(("arbitrary",)),
    )(dc_row, fl_row)


def _inproj_bwd(pieces, x2d, dx1, g_pre, w_in_p, tm):
    T, D = x2d.shape
    ns, _, dsh = w_in_p.shape

    def body(*refs):
        piece_refs = refs[:7]
        x_ref, dx1_ref, g_ref, w_ref, gx_ref, dg_ref, dbf_ref, dp_ref = refs[7:]
        first = pl.program_id(0) == 0
        _put_dproj(dp_ref, piece_refs)
        dh = jnp.concatenate([_dot(dp_ref[...], w_ref[s]) for s in range(ns)], axis=1)
        dxa, dg = _rms_bwd(x_ref[...], g_ref[...], dh)
        gx_ref[...] = dx1_ref[...] + dxa
        _acc(dg_ref, dg, first)
        _acc(dbf_ref, _colsum(piece_refs[5][...]), first)

    row = lambda w: BS((tm, w), lambda i: (i, 0))
    return pl.pallas_call(
        body, name="inproj_bwd", grid=(T // tm,),
        in_specs=[row(w) for _, w in DPROJ_PIECES] + [row(D), row(D), BS((1, D), lambda i: (0, 0)),
                                                      BS((ns, P_COLS, dsh), lambda i: (0, 0, 0))],
        out_specs=[row(D), BS((1, D), lambda i: (0, 0)), BS((1, LANES), lambda i: (0, 0))],
        out_shape=[jax.ShapeDtypeStruct((T, D), F32), jax.ShapeDtypeStruct((1, D), F32),
                   jax.ShapeDtypeStruct((1, LANES), F32)],
        scratch_shapes=[pltpu.VMEM((tm, P_COLS), BF16)],
        compiler_params=_cp(("arbitrary",)),
    )(*pieces, x2d, dx1, g_pre, w_in_p)


def _local_step(x, mem, target, W, P, reduce=None):
    Bl, S, D = x.shape
    T = Bl * S
    tm = min(512, T)
    x2d = x.reshape(T, D)
    t2d = target.reshape(T, D)
    vec = lambda a: a.reshape(1, -1)
    bf_row = jnp.pad(P["b_f"].reshape(1, -1), ((0, 0), (0, LANES - N_FOX_HEADS)))
    tril = jnp.tril(jnp.ones((CHUNK, CHUNK), bool))
    ws_tril = jnp.where(tril[None], P["w_s"][0], 0.0).astype(BF16)
    bs_full = jnp.repeat(P["b_s"][0].T, HEAD, axis=1)
    g_pre, g_sgu = vec(P["g_pre_mix"]), vec(P["g_sgu"])
    ga, gb, gm = vec(P["g_out_a"]), vec(P["g_out_b"]), vec(P["g_out_m"])
    g_mem, g_post, g_pre2, g_post2 = vec(P["g_mem"]), vec(P["g_post_mix"]), vec(P["g_pre_ffn"]), vec(P["g_post_ffn"])

    h, proj, flog = _inproj_fwd(x2d, g_pre, W["w_in"], tm)
    bq, bk, fl_row = _gate_fwd(flog.reshape(Bl, S, LANES), bf_row)
    ya = _sgu_fwd(proj, g_sgu, ws_tril, bs_full, tm)
    yb, lse = _fox_fwd(proj, bq, bk, Bl, S)
    memn, kv = _memkv_fwd(mem, g_mem, W["w_mem_kv"])
    ym = _memattn_fwd(proj, kv, Bl, S, min(2048, S))
    y, o, x1, h2 = _outproj_fwd(ya, yb, ym, x2d, ga, gb, gm, g_post, g_pre2, W["w_out"], tm)
    gs, us, dff, dx2, dg_post2, loss = _ffn_fwd(h2, x1, t2d, W["w_gate"], W["w_up"], W["w_down"], g_post2, tm)

    dh2, d_w_gate, d_w_up, d_w_down = _ffn_bwd(dff, h2, gs, us, W["w_gate"], W["w_up"], W["w_down"], min(1024, T))
    ffn = [d_w_gate, d_w_up, d_w_down]
    if reduce is not None:
        pending, _ = reduce.begin("ffn", ffn)
    dx1, d_w_out, dya, dyb, dym, dga, dgb, dgm, dg_post, dg_pre2 = _outproj_bwd(
        dh2, x1, dx2, o, y, ya, yb, ym, ga, gb, gm, g_post, g_pre2, W["w_out"], tm)
    if reduce is not None:
        ffn, (dya, dyb, dym) = reduce.finish("ffn", pending, (dya, dyb, dym))
    dzu, dzv, dws, dbs_cols, dg_sgu = _sgu_bwd(proj, dya, g_sgu, ws_tril, bs_full, tm)
    dqm, dkm, dvm = _memattn_bwd(proj, kv, dym, Bl, S, min(2048, S))
    d_w_kv, dg_mem = _memkv_bwd(dkm, dvm, memn, mem, g_mem, W["w_mem_kv"])
    mid = [d_w_kv, d_w_out]
    dq, dk, dv, dc_row = _fox_bwd(proj, dyb, lse, bq, bk, Bl, S)
    if reduce is not None:
        pending, after = reduce.begin("mid", mid, (dc_row,) + tuple(ffn))
        dc_row, ffn = after[0], list(after[1:])
    dfl = _gate_bwd(dc_row, fl_row).reshape(T, LANES)
    if reduce is not None:
        mid, (dfl,) = reduce.finish("mid", pending, (dfl,), first=(dzu, dzv))
    pieces = (dzu, dzv, dq, dk, dv, dfl, dqm)
    d_w_in = _dw_in(pieces, h, W["w_in"].shape[0], 1024)
    if reduce is None:
        big = dict(zip(BIG, [d_w_in] + mid + ffn))
    else:
        done = reduce.apply(BIG[1:3], mid)
        pending, after = reduce.begin("in", [d_w_in], done + (dx1,))
        dx1 = after[-1]
        big = {"w_in": pending, "ffn": ffn}
    grad_x, dg_pre, dbf = _inproj_bwd(pieces, x2d, dx1, g_pre, W["w_in"], tm)
    small = {"g_pre_mix": dg_pre, "b_f": dbf[:, :N_FOX_HEADS], "g_sgu": dg_sgu, "w_s": dws, "b_s": dbs_cols[:, :N_FOX_HEADS].T,
             "g_out_a": dga, "g_out_b": dgb, "g_out_m": dgm, "g_mem": dg_mem, "g_post_mix": dg_post,
             "g_pre_ffn": dg_pre2, "g_post_ffn": dg_post2, "loss": loss[:, :1]}
    return grad_x.reshape(Bl, S, D), big, small


def _place():
    return lax.axis_index("x"), lax.axis_index("y"), lax.axis_index("c")


def _exchange_on_sequencer(srcs, own_full, name, collective_id):
    n = len(srcs)

    def body(*refs):
        src, dst = refs[:n], refs[n:2 * n]
        lsem, isend, irecv, dsend, drecv = refs[2 * n:]
        x, y, c = _place()
        oc = 1 - c
        s_me = 2 * x + y
        sib = (x, y, oc)
        chips = [(1 - x, y), (x, 1 - y), (1 - x, 1 - y)]
        barrier = pltpu.get_barrier_semaphore()
        for dev in [(cx, cy, c) for cx, cy in chips] + [sib]:
            pl.semaphore_signal(barrier, inc=1, device_id=dev, device_id_type=MESH)
        pl.semaphore_wait(barrier, 4)

        def remote(a, b, ssem, rsem, dev):
            return pltpu.make_async_remote_copy(src_ref=a, dst_ref=b, send_sem=ssem, recv_sem=rsem,
                                                device_id=dev, device_id_type=MESH)

        sends, local = [], []
        for w in range(n):
            for j, (cx, cy) in enumerate(chips):
                half = src[w].at[c] if own_full else src[w].at[2 * cx + cy]
                cp = remote(half, dst[w].at[s_me, c], isend.at[w, j], irecv.at[w, j], (cx, cy, c))
                cp.start()
                sends.append(cp)
            if own_full:
                cp = remote(src[w], dst[w].at[s_me], dsend.at[w, 3], drecv.at[w, 3], sib)
            else:
                cp = remote(src[w].at[s_me], dst[w].at[s_me, c], dsend.at[w, 3], drecv.at[w, 3], sib)
                loc = pltpu.make_async_copy(src[w].at[s_me], dst[w].at[s_me, c], lsem.at[w])
                loc.start()
                local.append(loc)
            cp.start()
            sends.append(cp)
        for w in range(n):
            for j, (cx, cy) in enumerate(chips):
                landed = dst[w].at[2 * cx + cy, c]
                remote(landed, landed, isend.at[w, j], irecv.at[w, j], (cx, cy, c)).wait_recv()
                cp = remote(landed, landed, dsend.at[w, j], drecv.at[w, j], sib)
                cp.start()
                sends.append(cp)
        for w in range(n):
            for j, (cx, cy) in enumerate(chips):
                landed = dst[w].at[2 * cx + cy, oc]
                remote(landed, landed, dsend.at[w, j], drecv.at[w, j], sib).wait_recv()
            landed = dst[w].at[s_me] if own_full else dst[w].at[s_me, oc]
            remote(landed, landed, dsend.at[w, 3], drecv.at[w, 3], sib).wait_recv()
        for cp in sends:
            cp.wait_send()
        for loc in local:
            loc.wait()

    return pl.kernel(
        body, out_type=[jax.ShapeDtypeStruct((4, 2) + s.shape[1:], s.dtype) for s in srcs],
        mesh=plsc.ScalarSubcoreMesh(axis_name="sequencer", num_cores=1), name=name,
        scratch_types=[pltpu.SemaphoreType.DMA((n,)), pltpu.SemaphoreType.DMA((n, 3)), pltpu.SemaphoreType.DMA((n, 3)),
                       pltpu.SemaphoreType.DMA((n, 4)), pltpu.SemaphoreType.DMA((n, 4))],
        compiler_params=pltpu.CompilerParams(collective_id=collective_id),
    )(*srcs)


def _sibling_swap(grads, name, collective_id):
    n = len(grads)

    def body(*refs):
        g, theirs = refs[:n], refs[n:2 * n]
        ssem, rsem = refs[2 * n:]
        x, y, c = _place()
        sib = (x, y, 1 - c)
        barrier = pltpu.get_barrier_semaphore()
        pl.semaphore_signal(barrier, inc=1, device_id=sib, device_id_type=MESH)
        pl.semaphore_wait(barrier, 1)
        cps = []
        for w in range(n):
            cp = pltpu.make_async_remote_copy(src_ref=g[w].at[:, 1 - c], dst_ref=theirs[w], send_sem=ssem.at[w],
                                              recv_sem=rsem.at[w], device_id=sib, device_id_type=MESH)
            cp.start()
            cps.append(cp)
        for cp in cps:
            cp.wait()

    return pl.kernel(
        body, out_type=[jax.ShapeDtypeStruct((4,) + g.shape[2:], g.dtype) for g in grads],
        mesh=plsc.ScalarSubcoreMesh(axis_name="sequencer", num_cores=1), name=name,
        scratch_types=[pltpu.SemaphoreType.DMA((n,)), pltpu.SemaphoreType.DMA((n,))],
        compiler_params=pltpu.CompilerParams(collective_id=collective_id),
    )(*grads)


def _add_pairs(core, gs, theirs, name):
    n = len(gs)

    def body(core_ref, *refs):
        for g_ref, t_ref, o_ref in zip(refs[:n], refs[n:2 * n], refs[2 * n:]):
            o_ref[...] = (g_ref[:, 0].astype(F32) + t_ref[...].astype(F32)).astype(BF16)

    def specs(g):
        _, _, hr, C = g.shape
        return (BS((2, 1, hr, C), lambda s, core_ref: (s, core_ref[0], 0, 0)), BS((2, hr, C), lambda s, core_ref: (s, 0, 0)))

    return pl.pallas_call(
        body, name=name,
        grid_spec=pltpu.PrefetchScalarGridSpec(
            num_scalar_prefetch=1, grid=(2,),
            in_specs=[specs(g)[0] for g in gs] + [specs(g)[1] for g in gs], out_specs=[specs(g)[1] for g in gs]),
        out_shape=[jax.ShapeDtypeStruct(t.shape, BF16) for t in theirs],
        compiler_params=_cp(("arbitrary",)))(core, *gs, *theirs)


def _sum_chips(r, name):
    _, _, hr, C = r.shape

    def body(r_ref, o_ref):
        o_ref[...] = ((r_ref[0, 0].astype(F32) + r_ref[1, 0].astype(F32)) + r_ref[2, 0].astype(F32)) + r_ref[3, 0].astype(F32)

    return pl.pallas_call(body, name=name, grid=(2,), in_specs=[BS((4, 1, hr, C), lambda h: (0, h, 0, 0))],
                          out_specs=BS((hr, C), lambda h: (h, 0)), out_shape=jax.ShapeDtypeStruct((2 * hr, C), F32),
                          compiler_params=_cp(("arbitrary",)))(r)


class _Reducer:
    IDS = {"ffn": (4, 5), "mid": (6, 7), "in": (8, 9)}

    def __init__(self, core, apply):
        self.core = core
        self.apply = apply

    def begin(self, tag, grads, after=()):
        grads, after = lax.optimization_barrier((list(grads), after))
        g4 = [g.reshape(4, 2, -1, g.shape[-1]) for g in grads]
        return (g4, _sibling_swap(g4, "swap_" + tag, self.IDS[tag][0])), after

    def finish(self, tag, pending, hold, first=()):
        pending, first = lax.optimization_barrier((pending, first))
        g4, theirs = pending
        sums = _add_pairs(self.core, g4, theirs, "chip_sum_" + tag)
        sums, hold = lax.optimization_barrier((sums, hold))
        return _exchange_on_sequencer(sums, False, "scatter_" + tag, self.IDS[tag][1]), hold


def _small_allreduce(part):
    R = part.shape[0]
    rs = R // 8
    masks = [(mx, my, mc) for mx in (0, 1) for my in (0, 1) for mc in (0, 1)][1:]

    def body(p_ref, o_ref, buf_ref, s1, r1, s2, r2):
        x, y, c = _place()
        d = 4 * x + 2 * y + c
        mine = pl.ds(pl.multiple_of(d * rs, 8), rs)
        peers = [((x + mx) % 2, (y + my) % 2, (c + mc) % 2) for mx, my, mc in masks]
        first, second = [], []
        for k, (px, py, pc) in enumerate(peers):
            theirs = pl.ds(pl.multiple_of((4 * px + 2 * py + pc) * rs, 8), rs)
            cp = pltpu.make_async_remote_copy(src_ref=p_ref.at[theirs, :], dst_ref=buf_ref.at[d], send_sem=s1.at[k],
                                              recv_sem=r1.at[k], device_id=(px, py, pc), device_id_type=MESH)
            cp.start()
            first.append(cp)
        buf_ref[d] = p_ref[mine, :]
        for k, (px, py, pc) in enumerate(peers):
            slot = buf_ref.at[4 * px + 2 * py + pc]
            pltpu.make_async_remote_copy(src_ref=slot, dst_ref=slot, send_sem=s1.at[k], recv_sem=r1.at[k],
                                         device_id=(px, py, pc), device_id_type=MESH).wait_recv()
        total = buf_ref[0]
        for k in range(1, 8):
            total = total + buf_ref[k]
        o_ref[mine, :] = total
        for k, (px, py, pc) in enumerate(peers):
            cp = pltpu.make_async_remote_copy(src_ref=o_ref.at[mine, :], dst_ref=o_ref.at[mine, :], send_sem=s2.at[k],
                                              recv_sem=r2.at[k], device_id=(px, py, pc), device_id_type=MESH)
            cp.start()
            second.append(cp)
        for k, (px, py, pc) in enumerate(peers):
            rows = o_ref.at[pl.ds(pl.multiple_of((4 * px + 2 * py + pc) * rs, 8), rs), :]
            pltpu.make_async_remote_copy(src_ref=rows, dst_ref=rows, send_sem=s2.at[k], recv_sem=r2.at[k],
                                         device_id=(px, py, pc), device_id_type=MESH).wait_recv()
        for cp in first + second:
            cp.wait_send()

    vm = pl.BlockSpec(memory_space=pltpu.VMEM)
    return pl.pallas_call(
        body, name="small_allreduce", in_specs=[vm], out_specs=vm, out_shape=jax.ShapeDtypeStruct(part.shape, F32),
        scratch_shapes=[pltpu.VMEM((8, rs, LANES), F32)] + [pltpu.SemaphoreType.DMA((7,))] * 4,
    )(part)


def _adamw(w, g, m, v, name):
    R, C = w.shape
    summed = g.ndim == 4
    if summed:
        tr = R // 2
    else:
        tr = R if R * C * 4 <= (1 << 21) else R // 2
        if tr % 8:
            tr = R
    c1 = 1.0 / (1.0 - ADAM_B1 ** ADAM_STEP)
    c2 = 1.0 / (1.0 - ADAM_B2 ** ADAM_STEP)

    def body(w_ref, g_ref, m_ref, v_ref, *outs):
        if summed:
            g_ = ((g_ref[0, 0].astype(F32) + g_ref[1, 0].astype(F32)) + g_ref[2, 0].astype(F32)) + g_ref[3, 0].astype(F32)
            outs[0][...] = g_
        else:
            g_ = g_ref[...]
        d_ref, mo_ref, vo_ref = outs[-3:]
        m_ = ADAM_B1 * m_ref[...] + (1.0 - ADAM_B1) * g_
        v_ = ADAM_B2 * v_ref[...] + (1.0 - ADAM_B2) * (g_ * g_)
        mo_ref[...] = m_
        vo_ref[...] = v_
        d_ref[...] = -ADAM_LR * ((m_ * c1) / (jnp.sqrt(v_ * c2) + ADAM_EPS) + ADAM_WD * w_ref[...])

    blk = BS((tr, C), lambda i: (i, 0))
    g_blk = BS((4, 1, tr, C), lambda i: (0, i, 0, 0)) if summed else blk
    nout = 4 if summed else 3
    return pl.pallas_call(body, name=name, grid=(R // tr,), in_specs=[blk, g_blk, blk, blk], out_specs=[blk] * nout,
                          out_shape=[jax.ShapeDtypeStruct((R, C), F32)] * nout,
                          compiler_params=_cp(("arbitrary",)))(w, g, m, v)


def _adamw_unit_rows(w, g, m, v, name):
    C, _, R = w.shape
    tc = C // 2 if C % 2 == 0 else C
    c1 = 1.0 / (1.0 - ADAM_B1 ** ADAM_STEP)
    c2 = 1.0 / (1.0 - ADAM_B2 ** ADAM_STEP)

    def body(w_ref, g_ref, m_ref, v_ref, go_ref, d_ref, mo_ref, vo_ref):
        g_ = g_ref[...]
        go_ref[...] = g_
        m_ = ADAM_B1 * m_ref[...] + (1.0 - ADAM_B1) * g_
        v_ = ADAM_B2 * v_ref[...] + (1.0 - ADAM_B2) * (g_ * g_)
        mo_ref[...] = m_
        vo_ref[...] = v_
        d_ref[...] = -ADAM_LR * ((m_ * c1) / (jnp.sqrt(v_ * c2) + ADAM_EPS) + ADAM_WD * w_ref[...])

    blk = BS((tc, 1, R), lambda i: (i, 0, 0))
    return pl.pallas_call(body, name=name, grid=(C // tc,), in_specs=[blk] * 4, out_specs=[blk] * 4,
                          out_shape=[jax.ShapeDtypeStruct((C, 1, R), F32)] * 4,
                          compiler_params=_cp(("arbitrary",)))(w, g, m, v)


SMALL = ("g_pre_mix", "b_f", "g_sgu", "w_s", "b_s", "g_out_a", "g_out_b", "g_out_m", "g_mem", "g_post_mix",
         "g_pre_ffn", "g_post_ffn")
BIG = ("w_in", "w_mem_kv", "w_out", "w_gate", "w_up", "w_down")
TRANSPOSED = ("w_in", "w_gate", "w_up")
WEIGHTS = ("g_pre_mix", "w_in", "b_f", "g_sgu", "w_s", "b_s", "g_out_a", "g_out_b", "g_out_m", "g_mem", "w_mem_kv",
           "w_out", "g_post_mix", "g_pre_ffn", "w_gate", "w_up", "w_down", "g_post_ffn")


VECTORS = ("g_pre_mix", "b_f", "g_sgu", "g_out_a", "g_out_b", "g_out_m", "g_mem", "g_post_mix", "g_pre_ffn", "g_post_ffn")
VEC_ROWS = 16
WS_ROWS = N_FOX_HEADS * CHUNK
BS_ROWS = 8


def _pack_small(small, vw):
    stack = jnp.zeros((VEC_ROWS, vw), F32)
    for k, n in enumerate(VECTORS + ("loss",)):
        row = small[n].reshape(1, -1)
        stack = stack + jnp.pad(row, ((k, VEC_ROWS - 1 - k), (0, vw - row.shape[1])))
    parts = [small["w_s"].reshape(WS_ROWS, LANES), jnp.pad(small["b_s"], ((0, BS_ROWS - N_FOX_HEADS), (0, 0))),
             stack.reshape(-1, LANES)]
    rows = sum(p.shape[0] for p in parts)
    return jnp.concatenate(parts + [jnp.zeros((-rows % 64, LANES), F32)], axis=0)


def _adamw_small(vec_g, vec_wmv, ws, bs):
    c1 = 1.0 / (1.0 - ADAM_B1 ** ADAM_STEP)
    c2 = 1.0 / (1.0 - ADAM_B2 ** ADAM_STEP)
    nv = len(vec_wmv)

    def adam(g, w, m, v):
        m_ = ADAM_B1 * m + (1.0 - ADAM_B1) * g
        v_ = ADAM_B2 * v + (1.0 - ADAM_B2) * (g * g)
        return -ADAM_LR * ((m_ * c1) / (jnp.sqrt(v_ * c2) + ADAM_EPS) + ADAM_WD * w), m_, v_

    def body(*refs):
        vg_ref, ins, outs = refs[0], refs[1:1 + 3 * nv + 8], refs[1 + 3 * nv + 8:]
        for k in range(nv):
            w_ref, m_ref, v_ref = ins[3 * k:3 * k + 3]
            g = vg_ref[k:k + 1, 0:w_ref.shape[1]]
            d, m_, v_ = adam(g, w_ref[...], m_ref[...], v_ref[...])
            for o_ref, val in zip(outs[4 * k:4 * k + 4], (g, d, m_, v_)):
                o_ref[...] = val
        for j in range(2):
            g_ref, w_ref, m_ref, v_ref = ins[3 * nv + 4 * j:3 * nv + 4 * j + 4]
            for o_ref, val in zip(outs[4 * nv + 3 * j:4 * nv + 3 * j + 3], adam(g_ref[...], w_ref[...], m_ref[...], v_ref[...])):
                o_ref[...] = val

    vm = pl.BlockSpec(memory_space=pltpu.VMEM)
    operands = [vec_g] + [a for wmv in vec_wmv for a in wmv] + list(ws) + list(bs)
    out_shape = ([jax.ShapeDtypeStruct(wmv[0].shape, F32) for wmv in vec_wmv for _ in range(4)]
                 + [jax.ShapeDtypeStruct(ws[1].shape, F32)] * 3 + [jax.ShapeDtypeStruct(bs[1].shape, F32)] * 3)
    outs = pl.pallas_call(body, name="adamw_small", in_specs=[vm] * len(operands), out_specs=[vm] * len(out_shape),
                          out_shape=out_shape)(*operands)
    return [outs[4 * k:4 * k + 4] for k in range(nv)], outs[4 * nv:4 * nv + 3], outs[4 * nv + 3:]


def kernel(x, mem, g_pre_mix, w_in, b_f, g_sgu, w_s, b_s, g_out_a, g_out_b, g_out_m, g_mem, w_mem_kv, w_out, g_post_mix, g_pre_ffn, w_gate, w_up, w_down, g_post_ffn, loss_target, m_g_pre_mix, m_w_in, m_b_f, m_g_sgu, m_w_s, m_b_s, m_g_out_a, m_g_out_b, m_g_out_m, m_g_mem, m_w_mem_kv, m_w_out, m_g_post_mix, m_g_pre_ffn, m_w_gate, m_w_up, m_w_down, m_g_post_ffn, v_g_pre_mix, v_w_in, v_b_f, v_g_sgu, v_w_s, v_b_s, v_g_out_a, v_g_out_b, v_g_out_m, v_g_mem, v_w_mem_kv, v_w_out, v_g_post_mix, v_g_pre_ffn, v_w_gate, v_w_up, v_w_down, v_g_post_ffn):
    Wt = dict(g_pre_mix=g_pre_mix, w_in=w_in, b_f=b_f, g_sgu=g_sgu, w_s=w_s, b_s=b_s, g_out_a=g_out_a, g_out_b=g_out_b,
              g_out_m=g_out_m, g_mem=g_mem, w_mem_kv=w_mem_kv, w_out=w_out, g_post_mix=g_post_mix, g_pre_ffn=g_pre_ffn,
              w_gate=w_gate, w_up=w_up, w_down=w_down, g_post_ffn=g_post_ffn)
    Mo = dict(g_pre_mix=m_g_pre_mix, w_in=m_w_in, b_f=m_b_f, g_sgu=m_g_sgu, w_s=m_w_s, b_s=m_b_s, g_out_a=m_g_out_a,
              g_out_b=m_g_out_b, g_out_m=m_g_out_m, g_mem=m_g_mem, w_mem_kv=m_w_mem_kv, w_out=m_w_out,
              g_post_mix=m_g_post_mix, g_pre_ffn=m_g_pre_ffn, w_gate=m_w_gate, w_up=m_w_up, w_down=m_w_down,
              g_post_ffn=m_g_post_ffn)
    Vo = dict(g_pre_mix=v_g_pre_mix, w_in=v_w_in, b_f=v_b_f, g_sgu=v_g_sgu, w_s=v_w_s, b_s=v_b_s, g_out_a=v_g_out_a,
              g_out_b=v_g_out_b, g_out_m=v_g_out_m, g_mem=v_g_mem, w_mem_kv=v_w_mem_kv, w_out=v_w_out,
              g_post_mix=v_g_post_mix, g_pre_ffn=v_g_pre_ffn, w_gate=v_w_gate, w_up=v_w_up, w_down=v_w_down,
              g_post_ffn=v_g_post_ffn)

    gap = P_COLS - IN_COLS

    def to_kernel(n, w):
        if n in TRANSPOSED:
            w = w.T
        if n == "w_in":
            w = jnp.pad(w[:F_END], ((0, P_COLS - F_END), (0, 0))) + jnp.pad(w[F_END:], ((F_END + gap, 0), (0, 0)))
        return w

    def ungroup(g):
        return jnp.pad(g[:F_END], ((0, IN_COLS - F_END), (0, 0))) + jnp.pad(g[F_END + gap:], ((F_END, 0), (0, 0)))

    shards = {n: to_kernel(n, Wt[n][0]) for n in BIG}
    srcs = [shards[n].astype(BF16).reshape(2, shards[n].shape[0] // 2, shards[n].shape[1]) for n in BIG]
    fulls = (_exchange_on_sequencer(srcs[:1], True, "gather_w_in", 1)
             + _exchange_on_sequencer(srcs[1:3], True, "gather_kv_out", 2)
             + _exchange_on_sequencer(srcs[3:], True, "gather_ffn", 3))
    W = {}
    for n, f in zip(BIG, fulls):
        _, _, hr, C = f.shape
        W[n] = f.reshape(8 * hr, C) if n in ("w_mem_kv", "w_out") else f.reshape(4, 2 * hr, C)

    P = {n: Wt[n] for n in SMALL}
    grads, deltas, new_m, new_v = {}, {}, {}, {}

    def apply(names, landed):
        for n, r in zip(names, landed):
            if n == "w_in":
                g_t = ungroup(_sum_chips(r, "sum_chips_" + n))
                lift = lambda a: jnp.transpose(a, (2, 0, 1))
                outs = _adamw_unit_rows(lift(Wt[n]), g_t[:, None, :], lift(Mo[n]), lift(Vo[n]), "adamw_" + n)
                g, d, m1, v1 = [jnp.transpose(a, (1, 2, 0))[0] for a in outs]
            elif n in TRANSPOSED:
                g, d, m1, v1 = [a.T for a in _adamw(Wt[n][0].T, r, Mo[n][0].T, Vo[n][0].T, "adamw_" + n)]
            else:
                g, d, m1, v1 = _adamw(Wt[n][0], r, Mo[n][0], Vo[n][0], "adamw_" + n)
            grads[n], deltas[n], new_m[n], new_v[n] = g[None], d[None], m1[None], v1[None]
        return tuple(deltas[n] for n in names)

    core = lax.axis_index("c").astype(jnp.int32).reshape(1)
    reducer = _Reducer(core, apply)
    grad_x, pending, small = _local_step(x, mem, loss_target, W, P, reducer)

    vw = -(-max(x.shape[-1], A_W) // LANES) * LANES
    total = _small_allreduce(_pack_small(small, vw))
    landed, ffn_landed = reducer.finish("in", pending["w_in"], tuple(pending["ffn"]), first=(total,))
    apply(BIG[3:], ffn_landed)
    apply(BIG[:1], landed)

    lane_row = lambda a: jnp.pad(a, ((0, 0), (0, -a.shape[1] % LANES)))
    vec_g = total[WS_ROWS + BS_ROWS:WS_ROWS + BS_ROWS + VEC_ROWS * vw // LANES].reshape(VEC_ROWS, vw)
    ws_g = total[:WS_ROWS]
    bs_g = total[WS_ROWS:WS_ROWS + N_FOX_HEADS]
    rows = lambda a, r: a.reshape(r, LANES)
    per_vec, ws_out, bs_out = _adamw_small(
        vec_g, [tuple(lane_row(a[n]) for a in (Wt, Mo, Vo)) for n in VECTORS],
        (ws_g,) + tuple(rows(a["w_s"], WS_ROWS) for a in (Wt, Mo, Vo)),
        (bs_g,) + tuple(rows(a["b_s"], N_FOX_HEADS) for a in (Wt, Mo, Vo)))
    for n, outs in zip(VECTORS, per_vec):
        grads[n], deltas[n], new_m[n], new_v[n] = [o[:, :Wt[n].shape[1]] for o in outs]
    for n, g, outs in (("w_s", ws_g, ws_out), ("b_s", bs_g, bs_out)):
        grads[n], deltas[n], new_m[n], new_v[n] = [o.reshape(Wt[n].shape) for o in (g,) + tuple(outs)]
    loss = vec_g[len(VECTORS), 0]

    return (loss, grad_x, *[grads[n] for n in WEIGHTS], *[deltas[n] for n in WEIGHTS],
            *[new_m[n] for n in WEIGHTS], *[new_v[n] for n in WEIGHTS])
```

```python
import jax
import jax.numpy as jnp
from jax import lax
from jax.experimental import pallas as pl
from jax.experimental.pallas import tpu as pltpu
from jax.experimental.pallas import tpu_sc as plsc

F32 = jnp.float32
BF16 = jnp.bfloat16
EPS = 1e-6
NEG = -1e30
HEAD = 64
A_W, B_W, M_W = 384, 384, 256
N_FOX_HEADS = 6
CHUNK = 128
IN_COLS = 2 * A_W + 3 * B_W + N_FOX_HEADS + M_W
P_MAIN = 2 * A_W + 3 * B_W + M_W
LANES = 128
P_COLS = P_MAIN + LANES
F_END = 2 * A_W + 3 * B_W + N_FOX_HEADS
Q_BLK = 512
ROW_SPLIT = 4
ADAM_LR, ADAM_B1, ADAM_B2, ADAM_EPS, ADAM_WD, ADAM_STEP = 0.001, 0.9, 0.999, 1e-08, 0.01, 10
VMEM_LIMIT = 56 * 1024 * 1024
MESH = pl.DeviceIdType.MESH
BS = pl.BlockSpec


def _cp(sem=None):
    return pltpu.CompilerParams(dimension_semantics=sem, vmem_limit_bytes=VMEM_LIMIT)


def _iota(shape, dim):
    return lax.broadcasted_iota(jnp.int32, shape, dim)


def _dot(a, b):
    return jnp.dot(a.astype(BF16), b.astype(BF16), preferred_element_type=F32)


def _dot_nt(a, b):
    return lax.dot_general(a.astype(BF16), b.astype(BF16), (((1,), (1,)), ((), ())), preferred_element_type=F32)


def _dot_tn(a, b):
    return lax.dot_general(a.astype(BF16), b.astype(BF16), (((0,), (0,)), ((), ())), preferred_element_type=F32)


def _rms(x, g):
    return x * lax.rsqrt(jnp.mean(x * x, axis=-1, keepdims=True) + EPS) * g


def _rms_bwd(x, g, dy):
    r = lax.rsqrt(jnp.mean(x * x, axis=-1, keepdims=True) + EPS)
    xr = x * r
    gd = dy * g
    m = jnp.mean(gd * xr, axis=-1, keepdims=True)
    return (gd - xr * m) * r, _colsum(dy * xr)


def _gelu(x):
    return 0.5 * x * (1.0 + jnp.tanh(0.7978845608028654 * (x + 0.044715 * (x * x * x))))


def _sigmoid(x):
    return 1.0 / (1.0 + jnp.exp(-x))


def _silu_mul(g, u):
    return g * _sigmoid(g) * u


def _logsig(x):
    return jnp.minimum(x, 0.0) - jnp.log(1.0 + jnp.exp(-jnp.abs(x)))


def _colsum(x):
    return jnp.sum(x, axis=0, keepdims=True)


def _acc(ref, val, first):
    @pl.when(first)
    def _():
        ref[...] = val

    @pl.when(jnp.logical_not(first))
    def _():
        ref[...] += val


SPLIT = 1536
CH = 768


def _inproj_fwd_a(x2d, g_pre, w_a, tm):
    T, D = x2d.shape
    ns, _, dsh = w_a.shape

    def body(x_ref, g_ref, w_ref, h_ref, proj_ref):
        h = _rms(x_ref[...], g_ref[...]).astype(BF16)
        h_ref[...] = h
        for n in range(SPLIT // CH):
            rows = slice(n * CH, (n + 1) * CH)
            r = _dot_nt(h[:, 0:dsh], w_ref[0, rows, :])
            for s in range(1, ns):
                r = r + _dot_nt(h[:, s * dsh:(s + 1) * dsh], w_ref[s, rows, :])
            proj_ref[:, rows] = r.astype(BF16)

    return pl.pallas_call(
        body, name="inproj_fwd_a", grid=(T // tm,),
        in_specs=[BS((tm, D), lambda i: (i, 0)), BS((1, D), lambda i: (0, 0)), BS((ns, SPLIT, dsh), lambda i: (0, 0, 0))],
        out_specs=[BS((tm, D), lambda i: (i, 0)), BS((tm, SPLIT), lambda i: (i, 0))],
        out_shape=[jax.ShapeDtypeStruct((T, D), BF16), jax.ShapeDtypeStruct((T, SPLIT), BF16)],
        compiler_params=_cp(("arbitrary",)),
    )(x2d, g_pre, w_a)


def _inproj_fwd_b(h, w_b, tm):
    T, D = h.shape
    ns, rows_b, dsh = w_b.shape

    def body(h_ref, w_ref, proj_ref, fl_ref):
        h = h_ref[...]
        r = _dot_nt(h[:, 0:dsh], w_ref[0])
        for s in range(1, ns):
            r = r + _dot_nt(h[:, s * dsh:(s + 1) * dsh], w_ref[s])
        proj_ref[:, 0:B_W] = r[:, 0:B_W].astype(BF16)
        fl_ref[...] = r[:, B_W:B_W + LANES]
        proj_ref[:, B_W:] = r[:, B_W + LANES:].astype(BF16)

    return pl.pallas_call(
        body, name="inproj_fwd_b", grid=(T // tm,),
        in_specs=[BS((tm, D), lambda i: (i, 0)), BS((ns, rows_b, dsh), lambda i: (0, 0, 0))],
        out_specs=[BS((tm, B_W + M_W), lambda i: (i, 0)), BS((tm, LANES), lambda i: (i, 0))],
        out_shape=[jax.ShapeDtypeStruct((T, B_W + M_W), BF16), jax.ShapeDtypeStruct((T, LANES), F32)],
        compiler_params=_cp(("arbitrary",)),
    )(h, w_b)


def _gate_fwd(flog3, bf_row):
    Bl, S, _ = flog3.shape
    nb = S // LANES

    def body(f_ref, b_ref, bq_ref, bk_ref, fr_ref):
        row = _iota((LANES, LANES), 0)
        lane = _iota((LANES, LANES), 1)
        one = jnp.ones((LANES, LANES), BF16)
        zero = jnp.zeros((LANES, LANES), BF16)

        carry = jnp.zeros((1, LANES), F32)
        for j in range(nb):
            r0 = j * LANES
            fl = f_ref[0, pl.ds(r0, LANES), :] + b_ref[...]
            fr_ref[0, j] = fl.T[0:8, :]
            c = _logsig(fl)
            for k in (1, 2, 4, 8, 16, 32, 64):
                c = c + jnp.where(row >= k, pltpu.roll(c, k, 0), 0.0)
            total = _colsum(jnp.where(row == LANES - 1, c, 0.0))
            c = c + carry
            carry = carry + total
            for h in range(N_FOX_HEADS):
                col = jnp.sum(jnp.where(lane == h, c, 0.0), axis=1, keepdims=True)
                hi = col.astype(BF16)
                rest = col - hi.astype(F32)
                mid = rest.astype(BF16)
                lo = (rest - mid.astype(F32)).astype(BF16)
                base = _bias_lane(h)
                bq = jnp.where(lane == base, hi, jnp.where(lane == base + 1, mid, jnp.where(lane == base + 2, lo, zero)))
                bq = jnp.where((lane >= base + 3) & (lane < base + 6), one, bq)
                bk = jnp.where(lane == base + 3, -hi, jnp.where(lane == base + 4, -mid, jnp.where(lane == base + 5, -lo, zero)))
                bk = jnp.where((lane >= base) & (lane < base + 3), one, bk)
                bq_ref[0, h, pl.ds(r0, LANES), :] = bq
                bk_ref[0, h, pl.ds(r0, LANES), :] = bk

    slab = BS((1, N_FOX_HEADS, S, LANES), lambda b: (b, 0, 0, 0))
    return pl.pallas_call(
        body, name="gate_fwd", grid=(Bl,),
        in_specs=[BS((1, S, LANES), lambda b: (b, 0, 0)), BS((1, LANES), lambda b: (0, 0))],
        out_specs=[slab, slab, BS((1, nb, 8, LANES), lambda b: (b, 0, 0, 0))],
        out_shape=[jax.ShapeDtypeStruct((Bl, N_FOX_HEADS, S, LANES), BF16),
                   jax.ShapeDtypeStruct((Bl, N_FOX_HEADS, S, LANES), BF16),
                   jax.ShapeDtypeStruct((Bl, nb, 8, LANES), F32)],
        compiler_params=_cp(("arbitrary",)),
    )(flog3, bf_row)


def _bias_lane(h):
    return HEAD if h % 2 == 0 else 0


def _sgu_pre(zu, zv, g_sgu):
    return _gelu(zu), _rms(_gelu(zv), g_sgu)


def _sgu_fwd(proj, g_sgu, ws_tril, bs_full, tm):
    T = proj.shape[0]
    nch = tm // CHUNK

    def body(zu_ref, zv_ref, g_ref, ws_ref, b_ref, ya_ref):
        lane = _iota((CHUNK, LANES), 1)
        u, vn = _sgu_pre(zu_ref[...].astype(F32), zv_ref[...].astype(F32), g_ref[...])
        vn = vn.astype(BF16)
        for c in range(nch):
            rs = slice(c * CHUNK, (c + 1) * CHUNK)
            for j in range(3):
                cs = slice(j * LANES, (j + 1) * LANES)
                vp = vn[rs, cs]
                z = jnp.where(lane < HEAD, _dot(ws_ref[2 * j], vp), _dot(ws_ref[2 * j + 1], vp)) + b_ref[:, cs]
                ya_ref[rs, cs] = (u[rs, cs] * z).astype(BF16)

    return pl.pallas_call(
        body, name="sgu_fwd", grid=(T // tm,),
        in_specs=[BS((tm, A_W), lambda i: (i, 0)), BS((tm, A_W), lambda i: (i, 1)), BS((1, A_W), lambda i: (0, 0)),
                  BS((6, CHUNK, CHUNK), lambda i: (0, 0, 0)), BS((CHUNK, A_W), lambda i: (0, 0))],
        out_specs=BS((tm, A_W), lambda i: (i, 0)),
        out_shape=jax.ShapeDtypeStruct((T, A_W), BF16),
        compiler_params=_cp(("arbitrary",)),
    )(proj, proj, g_sgu, ws_tril, bs_full)


def _fox_fwd(proj, proj_b, bq, bk, Bl, S):
    T = Bl * S
    nq = S // Q_BLK
    qc, kc, vc = 768 // LANES, 1152 // LANES, 0

    def body(q_ref, k_ref, v_ref, bq_ref, bk_ref, o_ref, lse_ref, ka_ref, va_ref):
        lane_s = _iota((S, LANES), 1)
        lane = _iota((Q_BLK, LANES), 1)
        tri = _iota((Q_BLK, Q_BLK), 1) <= _iota((Q_BLK, Q_BLK), 0)
        k = k_ref[...]
        v = v_ref[...]
        for hh in range(2):
            data = (lane_s < HEAD) if hh == 0 else (lane_s >= HEAD)
            ka_ref[hh] = jnp.where(data, k, bk_ref[0, hh])
            va_ref[hh] = jnp.where(lane_s == _bias_lane(hh), jnp.ones_like(v), v)
        for i in range(nq):
            r0 = i * Q_BLK
            q = q_ref[r0:r0 + Q_BLK, :]
            o_out = jnp.zeros((Q_BLK, LANES), F32)
            lse_out = jnp.zeros((Q_BLK, LANES), F32)
            for hh in range(2):
                hmask = (lane < HEAD) if hh == 0 else (lane >= HEAD)
                qa = jnp.where(hmask, q * 0.125, bq_ref[0, hh, r0:r0 + Q_BLK, :])
                sd = jnp.where(tri, _dot_nt(qa, ka_ref[hh, r0:r0 + Q_BLK, :]), NEG)
                m = jnp.max(sd, axis=1, keepdims=True)
                if i:
                    sf = _dot_nt(qa, ka_ref[hh, 0:r0, :])
                    m = jnp.maximum(m, jnp.max(sf, axis=1, keepdims=True))
                acc = _dot(jnp.exp(sd - m), va_ref[hh, r0:r0 + Q_BLK, :])
                if i:
                    acc = acc + _dot(jnp.exp(sf - m), va_ref[hh, 0:r0, :])
                l = jnp.sum(jnp.where(lane == _bias_lane(hh), acc, 0.0), axis=1, keepdims=True)
                o_out = jnp.where(hmask, acc / l, o_out)
                lse_out = jnp.where(hmask, m + jnp.log(l), lse_out)
            o_ref[r0:r0 + Q_BLK, :] = o_out.astype(BF16)
            lse_ref[0, r0:r0 + Q_BLK, :] = lse_out

    seq = lambda c0: BS((S, LANES), lambda b, p: (b, c0 + p))
    pair = BS((1, 2, S, LANES), lambda b, p: (b, p, 0, 0))
    return pl.pallas_call(
        body, name="fox_fwd", grid=(Bl, 3),
        in_specs=[seq(qc), seq(kc), seq(vc), pair, pair],
        out_specs=[seq(0), BS((1, S, LANES), lambda b, p: (p, b, 0))],
        out_shape=[jax.ShapeDtypeStruct((T, B_W), BF16), jax.ShapeDtypeStruct((3, T, LANES), F32)],
        scratch_shapes=[pltpu.VMEM((2, S, LANES), BF16), pltpu.VMEM((2, S, LANES), BF16)],
        compiler_params=_cp(("arbitrary", "arbitrary")),
    )(proj, proj, proj_b, bq, bk)


def _memkv_fwd(mem, g_mem, w_kv):
    Bl, Mt, D = mem.shape

    def body(m_ref, g_ref, w_ref, mn_ref, kv_ref):
        mn = _rms(m_ref[0], g_ref[...]).astype(BF16)
        mn_ref[0] = mn
        kv_ref[0] = jnp.dot(mn, w_ref[...], preferred_element_type=F32).astype(BF16)

    return pl.pallas_call(
        body, name="memkv_fwd", grid=(Bl,),
        in_specs=[BS((1, Mt, D), lambda b: (b, 0, 0)), BS((1, D), lambda b: (0, 0)), BS((D, 2 * M_W), lambda b: (0, 0))],
        out_specs=[BS((1, Mt, D), lambda b: (b, 0, 0)), BS((1, Mt, 2 * M_W), lambda b: (b, 0, 0))],
        out_shape=[jax.ShapeDtypeStruct((Bl, Mt, D), BF16), jax.ShapeDtypeStruct((Bl, Mt, 2 * M_W), BF16)],
        compiler_params=_cp(("arbitrary",)),
    )(mem, g_mem, w_kv)


def _memattn_fwd(proj, kv, Bl, S, tq):
    T = Bl * S
    nq = S // tq
    Mt = kv.shape[1]
    qc = B_W // LANES

    def body(q_ref, km_ref, vm_ref, o_ref):
        lane = _iota((tq, LANES), 1)
        q = q_ref[...]
        out = jnp.zeros((tq, LANES), F32)
        for hh in range(2):
            hmask = (lane < HEAD) if hh == 0 else (lane >= HEAD)
            qs = jnp.where(hmask, q, jnp.zeros_like(q)) * 0.125
            s = _dot_nt(qs, km_ref[0])
            pe = jnp.exp(s - jnp.max(s, axis=1, keepdims=True))
            pn = pe / jnp.sum(pe, axis=1, keepdims=True)
            out = jnp.where(hmask, _dot(pn, vm_ref[0]), out)
        o_ref[...] = out.astype(BF16)

    return pl.pallas_call(
        body, name="memattn_fwd", grid=(Bl, 2, nq),
        in_specs=[BS((tq, LANES), lambda b, p, i: (b * nq + i, qc + p)),
                  BS((1, Mt, LANES), lambda b, p, i: (b, 0, p)),
                  BS((1, Mt, LANES), lambda b, p, i: (b, 0, 2 + p))],
        out_specs=BS((tq, LANES), lambda b, p, i: (b * nq + i, p)),
        out_shape=jax.ShapeDtypeStruct((T, M_W), BF16),
        compiler_params=_cp(("arbitrary", "arbitrary", "arbitrary")),
    )(proj, kv, kv)


def _mix_norms(ya, yb, ym, ga, gb, gm):
    return _rms(ya, ga), _rms(yb, gb), _rms(ym, gm)


def _outproj_fwd(ya, yb, ym, x2d, ga, gb, gm, g_post, g_pre2, w_out, tm):
    T, D = x2d.shape

    def body(ya_ref, yb_ref, ym_ref, x_ref, ga_ref, gb_ref, gm_ref, gp_ref, g2_ref, w_ref,
             y_ref, o_ref, x1_ref, h2_ref):
        na, nb_, nm = _mix_norms(ya_ref[...].astype(F32), yb_ref[...].astype(F32), ym_ref[...].astype(F32),
                                 ga_ref[...], gb_ref[...], gm_ref[...])
        y_ref[:, 0:A_W] = na.astype(BF16)
        y_ref[:, A_W:A_W + B_W] = nb_.astype(BF16)
        y_ref[:, A_W + B_W:] = nm.astype(BF16)
        o = jnp.dot(y_ref[...], w_ref[...], preferred_element_type=F32).astype(BF16)
        o_ref[...] = o
        x1 = x_ref[...] + _rms(o.astype(F32), gp_ref[...])
        x1_ref[...] = x1
        h2_ref[...] = _rms(x1, g2_ref[...]).astype(BF16)

    row = lambda w: BS((tm, w), lambda i: (i, 0))
    vec = lambda w: BS((1, w), lambda i: (0, 0))
    return pl.pallas_call(
        body, name="outproj_fwd", grid=(T // tm,),
        in_specs=[row(A_W), row(B_W), row(M_W), row(D), vec(A_W), vec(B_W), vec(M_W), vec(D), vec(D),
                  BS((A_W + B_W + M_W, D), lambda i: (0, 0))],
        out_specs=[row(A_W + B_W + M_W), row(D), row(D), row(D)],
        out_shape=[jax.ShapeDtypeStruct((T, A_W + B_W + M_W), BF16), jax.ShapeDtypeStruct((T, D), BF16),
                   jax.ShapeDtypeStruct((T, D), F32), jax.ShapeDtypeStruct((T, D), BF16)],
        compiler_params=_cp(("arbitrary",)),
    )(ya, yb, ym, x2d, ga, gb, gm, g_post, g_pre2, w_out)


def _ffn_fwd(h2, x1, target, wg, wu, wd, g_post, tm):
    T, D = x1.shape
    ns, F, _ = wg.shape

    def body(h_ref, x1_ref, t_ref, wg_ref, wu_ref, wd_ref, gp_ref,
             gs_ref, us_ref, dff_ref, dx2_ref, dgp_ref, loss_ref, acc_ref):
        j = pl.program_id(0)
        i = pl.program_id(1)
        rows = pl.ds(pl.multiple_of(i * tm, tm), tm)
        h = h_ref[...]
        g = _dot_nt(h, wg_ref[0])
        u = _dot_nt(h, wu_ref[0])
        gs_ref[0] = g.astype(BF16)
        us_ref[0] = u.astype(BF16)
        part = _dot(_silu_mul(g, u), wd_ref[0])

        @pl.when(j == 0)
        def _():
            acc_ref[rows, :] = part

        @pl.when(j != 0)
        def _():
            acc_ref[rows, :] += part

        @pl.when(j == ns - 1)
        def _():
            ff = acc_ref[rows, :]
            diff = x1_ref[...] + _rms(ff, gp_ref[...]) - t_ref[...]
            dx2 = diff * (1.0 / D)
            dff, dgp = _rms_bwd(ff, gp_ref[...], dx2)
            dx2_ref[...] = dx2
            dff_ref[...] = dff.astype(BF16)
            lpart = jnp.sum(_colsum(diff * diff), axis=1, keepdims=True) * (0.5 / D)
            _acc(dgp_ref, dgp, i == 0)
            _acc(loss_ref, jnp.broadcast_to(lpart, (1, LANES)), i == 0)

    last = lambda j, i: (jnp.where(j == ns - 1, i, 0), 0)
    wsh = BS((1, F, D), lambda j, i: (j, 0, 0))
    sh = BS((1, tm, F), lambda j, i: (j, i, 0))
    return pl.pallas_call(
        body, name="ffn_fwd", grid=(ns, T // tm),
        in_specs=[BS((tm, D), lambda j, i: (i, 0)), BS((tm, D), last), BS((tm, D), last), wsh, wsh, wsh,
                  BS((1, D), lambda j, i: (0, 0))],
        out_specs=[sh, sh, BS((tm, D), last), BS((tm, D), last),
                   BS((1, D), lambda j, i: (0, 0)), BS((1, LANES), lambda j, i: (0, 0))],
        out_shape=[jax.ShapeDtypeStruct((ns, T, F), BF16), jax.ShapeDtypeStruct((ns, T, F), BF16),
                   jax.ShapeDtypeStruct((T, D), BF16), jax.ShapeDtypeStruct((T, D), F32),
                   jax.ShapeDtypeStruct((1, D), F32), jax.ShapeDtypeStruct((1, LANES), F32)],
        scratch_shapes=[pltpu.VMEM((T, D), F32)],
        compiler_params=_cp(("arbitrary", "arbitrary")),
    )(h2, x1, target, wg, wu, wd, g_post)


def _ffn_bwd(dff, h2, gs, us, wg, wu, wd, tm):
    T, D = h2.shape
    ns, F, _ = wg.shape

    def body(dff_ref, h_ref, gs_ref, us_ref, wg_ref, wu_ref, wd_ref, dh_ref, dwg_out, dwu_out, dwd_out,
             dwg_ref, dwu_ref, dwd_ref):
        first = pl.program_id(1) == 0
        dff = dff_ref[...]
        h = h_ref[...]
        parts = []
        for r in range(ROW_SPLIT):
            rows = slice(r * (tm // ROW_SPLIT), (r + 1) * (tm // ROW_SPLIT))
            dact = _dot_nt(dff[rows], wd_ref[0])
            g = gs_ref[0, rows, :].astype(F32)
            u = us_ref[0, rows, :].astype(F32)
            sig = _sigmoid(g)
            gsig = g * sig
            dg = (dact * u * (sig + gsig * (1.0 - sig))).astype(BF16)
            du = (dact * gsig).astype(BF16)
            dh_ref[0, rows, :] = (_dot(dg, wg_ref[0]) + _dot(du, wu_ref[0])).astype(BF16)
            parts.append(((gsig * u).astype(BF16), dg, du))
        a, dg, du = [jnp.concatenate(p, axis=0) for p in zip(*parts)]
        _acc(dwd_ref, _dot_tn(a, dff), first)
        _acc(dwg_ref, _dot_tn(dg, h), first)
        _acc(dwu_ref, _dot_tn(du, h), first)

        @pl.when(pl.program_id(1) == pl.num_programs(1) - 1)
        def _():
            dwg_out[0] = dwg_ref[...].astype(BF16)
            dwu_out[0] = dwu_ref[...].astype(BF16)
            dwd_out[0] = dwd_ref[...].astype(BF16)

    row = BS((tm, D), lambda j, i: (i, 0))
    sh = BS((1, tm, F), lambda j, i: (j, i, 0))
    wsh = BS((1, F, D), lambda j, i: (j, 0, 0))
    return pl.pallas_call(
        body, name="ffn_bwd", grid=(ns, T // tm),
        in_specs=[row, row, sh, sh, wsh, wsh, wsh],
        out_specs=[BS((1, tm, D), lambda j, i: (j, i, 0)), wsh, wsh, wsh],
        out_shape=[jax.ShapeDtypeStruct((ns, T, D), BF16)] + [jax.ShapeDtypeStruct((ns, F, D), BF16)] * 3,
        scratch_shapes=[pltpu.VMEM((F, D), F32)] * 3,
        compiler_params=_cp(("arbitrary", "arbitrary")),
    )(dff, h2, gs, us, wg, wu, wd)


DPROJ_PIECES = ((0, A_W), (A_W, A_W), (768, B_W), (1152, B_W), (1536, B_W), (1920, LANES), (2048, M_W))


def _put_dproj(dp_ref, piece_refs):
    for (c0, w), ref in zip(DPROJ_PIECES, piece_refs):
        dp_ref[:, c0:c0 + w] = ref[...].astype(BF16)


def _dw_in(pieces, h, ns, tk):
    T, D = h.shape
    M = P_COLS
    dsh = D // ns
    tk = min(tk, T)

    def body(*refs):
        piece_refs, h_ref, o_ref, acc_ref, dp_ref = refs[:7], refs[7], refs[8], refs[9], refs[10]
        t = pl.program_id(0)
        _put_dproj(dp_ref, piece_refs)
        _acc(acc_ref, _dot_tn(h_ref[...], dp_ref[...]), t == 0)

        @pl.when(t == pl.num_programs(0) - 1)
        def _():
            for s in range(ns):
                o_ref[s] = acc_ref[s * dsh:(s + 1) * dsh, :].T.astype(BF16)

    return pl.pallas_call(
        body, name="dw_in", grid=(T // tk,),
        in_specs=[BS((tk, w), lambda t: (t, 0)) for _, w in DPROJ_PIECES] + [BS((tk, D), lambda t: (t, 0))],
        out_specs=BS((ns, M, dsh), lambda t: (0, 0, 0)),
        out_shape=jax.ShapeDtypeStruct((ns, M, dsh), BF16),
        scratch_shapes=[pltpu.VMEM((D, M), F32), pltpu.VMEM((tk, M), BF16)],
        compiler_params=_cp(("arbitrary",)),
    )(*pieces, h)


def _outproj_bwd(dh2, x1, dx2, o, y, ya, yb, ym, ga, gb, gm, g_post, g_pre2, w_out, tm):
    T, D = x1.shape
    ns = dh2.shape[0]

    def body(dh_ref, x1_ref, dx2_ref, o_ref, y_ref, ya_ref, yb_ref, ym_ref, ga_ref, gb_ref, gm_ref, gp_ref, g2_ref, w_ref,
             dx1_ref, dw_ref, dya_ref, dyb_ref, dym_ref, dga_ref, dgb_ref, dgm_ref, dgp_ref, dg2_ref):
        first = pl.program_id(0) == 0
        dh = dh_ref[0].astype(F32)
        for j in range(1, ns):
            dh = dh + dh_ref[j].astype(F32)
        dxa, dg2 = _rms_bwd(x1_ref[...], g2_ref[...], dh)
        dx1 = dx2_ref[...] + dxa
        dx1_ref[...] = dx1
        _acc(dg2_ref, dg2, first)
        do, dgp = _rms_bwd(o_ref[...].astype(F32), gp_ref[...], dx1)
        do = do.astype(BF16)
        _acc(dw_ref, _dot_tn(y_ref[...], do), first)
        dy = _dot_nt(do, w_ref[...])
        dya, dga = _rms_bwd(ya_ref[...].astype(F32), ga_ref[...], dy[:, 0:A_W])
        dyb, dgb = _rms_bwd(yb_ref[...].astype(F32), gb_ref[...], dy[:, A_W:A_W + B_W])
        dym, dgm = _rms_bwd(ym_ref[...].astype(F32), gm_ref[...], dy[:, A_W + B_W:])
        dya_ref[...] = dya.astype(BF16)
        dyb_ref[...] = dyb.astype(BF16)
        dym_ref[...] = dym.astype(BF16)
        _acc(dga_ref, dga, first)
        _acc(dgb_ref, dgb, first)
        _acc(dgm_ref, dgm, first)
        _acc(dgp_ref, dgp, first)

    row = lambda w: BS((tm, w), lambda i: (i, 0))
    vec = lambda w: BS((1, w), lambda i: (0, 0))
    sds = jax.ShapeDtypeStruct
    return pl.pallas_call(
        body, name="outproj_bwd", grid=(T // tm,),
        in_specs=[BS((ns, tm, D), lambda i: (0, i, 0)), row(D), row(D), row(D), row(A_W + B_W + M_W), row(A_W), row(B_W),
                  row(M_W), vec(A_W), vec(B_W), vec(M_W), vec(D), vec(D), BS((A_W + B_W + M_W, D), lambda i: (0, 0))],
        out_specs=[row(D), BS((A_W + B_W + M_W, D), lambda i: (0, 0)), row(A_W), row(B_W), row(M_W),
                   vec(A_W), vec(B_W), vec(M_W), vec(D), vec(D)],
        out_shape=[sds((T, D), F32), sds((A_W + B_W + M_W, D), F32), sds((T, A_W), BF16), sds((T, B_W), BF16),
                   sds((T, M_W), BF16), sds((1, A_W), F32), sds((1, B_W), F32), sds((1, M_W), F32), sds((1, D), F32),
                   sds((1, D), F32)],
        compiler_params=_cp(("arbitrary",)),
    )(dh2, x1, dx2, o, y, ya, yb, ym, ga, gb, gm, g_post, g_pre2, w_out)


def _sgu_bwd(proj, dya, g_sgu, ws_tril, bs_full, tm):
    T = proj.shape[0]
    nch = tm // CHUNK

    def body(zu_ref, zv_ref, dy_ref, g_ref, ws_ref, b_ref, dzu_ref, dzv_ref, dws_ref, dbs_ref, dg_ref,
             du_ref, dvn_ref, dbf_ref):
        step = pl.program_id(0)
        first = step == 0
        lane = _iota((CHUNK, LANES), 1)
        tril = _iota((CHUNK, CHUNK), 0) >= _iota((CHUNK, CHUNK), 1)
        (u, vn), vjp = jax.vjp(_sgu_pre, zu_ref[...].astype(F32), zv_ref[...].astype(F32), g_ref[...])
        vnb = vn.astype(BF16)
        dy = dy_ref[...].astype(F32)

        @pl.when(first)
        def _():
            dws_ref[...] = jnp.zeros_like(dws_ref)
            dbf_ref[...] = jnp.zeros_like(dbf_ref)

        for c in range(nch):
            rs = slice(c * CHUNK, (c + 1) * CHUNK)
            for j in range(3):
                cs = slice(j * LANES, (j + 1) * LANES)
                vp = vnb[rs, cs]
                z = jnp.where(lane < HEAD, _dot(ws_ref[2 * j], vp), _dot(ws_ref[2 * j + 1], vp)) + b_ref[:, cs]
                du_ref[rs, cs] = dy[rs, cs] * z
                dz = dy[rs, cs] * u[rs, cs]
                dbf_ref[:, cs] += dz
                dzb = dz.astype(BF16)
                dz0 = jnp.where(lane < HEAD, dzb, jnp.zeros_like(dzb))
                dz1 = jnp.where(lane >= HEAD, dzb, jnp.zeros_like(dzb))
                dvn_ref[rs, cs] = jnp.where(lane < HEAD, _dot_tn(ws_ref[2 * j], dzb), _dot_tn(ws_ref[2 * j + 1], dzb))
                dws_ref[2 * j] += jnp.where(tril, _dot_nt(dz0, vp), 0.0)
                dws_ref[2 * j + 1] += jnp.where(tril, _dot_nt(dz1, vp), 0.0)
        dzu, dzv, dg = vjp((du_ref[...], dvn_ref[...]))
        dzu_ref[...] = dzu.astype(BF16)
        dzv_ref[...] = dzv.astype(BF16)
        _acc(dg_ref, dg, first)

        @pl.when(step == pl.num_programs(0) - 1)
        def _():
            out = jnp.zeros((CHUNK, LANES), F32)
            for j in range(3):
                slab = dbf_ref[:, j * LANES:(j + 1) * LANES]
                lo = jnp.sum(jnp.where(lane < HEAD, slab, 0.0), axis=1, keepdims=True)
                hi = jnp.sum(jnp.where(lane >= HEAD, slab, 0.0), axis=1, keepdims=True)
                out = out + jnp.where(lane == 2 * j, lo, 0.0) + jnp.where(lane == 2 * j + 1, hi, 0.0)
            dbs_ref[...] = out

    return pl.pallas_call(
        body, name="sgu_bwd", grid=(T // tm,),
        in_specs=[BS((tm, A_W), lambda i: (i, 0)), BS((tm, A_W), lambda i: (i, 1)), BS((tm, A_W), lambda i: (i, 0)),
                  BS((1, A_W), lambda i: (0, 0)), BS((6, CHUNK, CHUNK), lambda i: (0, 0, 0)),
                  BS((CHUNK, A_W), lambda i: (0, 0))],
        out_specs=[BS((tm, A_W), lambda i: (i, 0)), BS((tm, A_W), lambda i: (i, 0)),
                   BS((6, CHUNK, CHUNK), lambda i: (0, 0, 0)), BS((CHUNK, LANES), lambda i: (0, 0)),
                   BS((1, A_W), lambda i: (0, 0))],
        out_shape=[jax.ShapeDtypeStruct((T, A_W), BF16), jax.ShapeDtypeStruct((T, A_W), BF16),
                   jax.ShapeDtypeStruct((6, CHUNK, CHUNK), F32), jax.ShapeDtypeStruct((CHUNK, LANES), F32),
                   jax.ShapeDtypeStruct((1, A_W), F32)],
        scratch_shapes=[pltpu.VMEM((tm, A_W), F32), pltpu.VMEM((tm, A_W), F32), pltpu.VMEM((CHUNK, A_W), F32)],
        compiler_params=_cp(("arbitrary",)),
    )(proj, proj, dya, g_sgu, ws_tril, bs_full)


def _memattn_bwd(proj, kv, dym, Bl, S, tq):
    T = Bl * S
    nq = S // tq
    Mt = kv.shape[1]
    qc = B_W // LANES

    def body(q_ref, km_ref, vm_ref, do_ref, dq_ref, dkm_ref, dvm_ref):
        first = pl.program_id(2) == 0
        lane = _iota((tq, LANES), 1)
        q = q_ref[...]
        do = do_ref[...]
        dq_out = jnp.zeros((tq, LANES), F32)
        dkm = jnp.zeros((Mt, LANES), F32)
        dvm = jnp.zeros((Mt, LANES), F32)
        for hh in range(2):
            hmask = (lane < HEAD) if hh == 0 else (lane >= HEAD)
            qs = jnp.where(hmask, q, jnp.zeros_like(q)) * 0.125
            dom = jnp.where(hmask, do, 0.0).astype(BF16)
            s = _dot_nt(qs, km_ref[0])
            pe = jnp.exp(s - jnp.max(s, axis=1, keepdims=True))
            pn = pe / jnp.sum(pe, axis=1, keepdims=True)
            dp = _dot_nt(dom, vm_ref[0])
            ds = (pn * (dp - jnp.sum(pn * dp, axis=1, keepdims=True))).astype(BF16)
            dq_out = jnp.where(hmask, _dot(ds, km_ref[0]) * 0.125, dq_out)
            dkm = dkm + _dot_tn(ds, qs)
            dvm = dvm + _dot_tn(pn, dom)
        dq_ref[...] = dq_out.astype(BF16)
        _acc(dkm_ref, dkm[None], first)
        _acc(dvm_ref, dvm[None], first)

    return pl.pallas_call(
        body, name="memattn_bwd", grid=(Bl, 2, nq),
        in_specs=[BS((tq, LANES), lambda b, p, i: (b * nq + i, qc + p)),
                  BS((1, Mt, LANES), lambda b, p, i: (b, 0, p)),
                  BS((1, Mt, LANES), lambda b, p, i: (b, 0, 2 + p)),
                  BS((tq, LANES), lambda b, p, i: (b * nq + i, p))],
        out_specs=[BS((tq, LANES), lambda b, p, i: (b * nq + i, p)),
                   BS((1, Mt, LANES), lambda b, p, i: (b, 0, p)),
                   BS((1, Mt, LANES), lambda b, p, i: (b, 0, p))],
        out_shape=[jax.ShapeDtypeStruct((T, M_W), BF16), jax.ShapeDtypeStruct((Bl, Mt, M_W), F32),
                   jax.ShapeDtypeStruct((Bl, Mt, M_W), F32)],
        compiler_params=_cp(("arbitrary", "arbitrary", "arbitrary")),
    )(proj, kv, kv, dym)


def _memkv_bwd(dkm, dvm, memn, mem, g_mem, w_kv):
    Bl, Mt, D = mem.shape

    def body(dk_ref, dv_ref, mn_ref, m_ref, g_ref, w_ref, dw_ref, dg_ref):
        first = pl.program_id(0) == 0
        dk = dk_ref[0].astype(BF16)
        dv = dv_ref[0].astype(BF16)
        mn = mn_ref[0]
        dmn = _dot_nt(dk, w_ref[:, 0:M_W]) + _dot_nt(dv, w_ref[:, M_W:])
        _, dg = _rms_bwd(m_ref[0], g_ref[...], dmn)
        _acc(dg_ref, dg, first)

        @pl.when(first)
        def _():
            dw_ref[...] = jnp.zeros_like(dw_ref)

        dw_ref[:, 0:M_W] += _dot_tn(mn, dk)
        dw_ref[:, M_W:] += _dot_tn(mn, dv)

    return pl.pallas_call(
        body, name="memkv_bwd", grid=(Bl,),
        in_specs=[BS((1, Mt, M_W), lambda b: (b, 0, 0)), BS((1, Mt, M_W), lambda b: (b, 0, 0)),
                  BS((1, Mt, D), lambda b: (b, 0, 0)), BS((1, Mt, D), lambda b: (b, 0, 0)),
                  BS((1, D), lambda b: (0, 0)), BS((D, 2 * M_W), lambda b: (0, 0))],
        out_specs=[BS((D, 2 * M_W), lambda b: (0, 0)), BS((1, D), lambda b: (0, 0))],
        out_shape=[jax.ShapeDtypeStruct((D, 2 * M_W), F32), jax.ShapeDtypeStruct((1, D), F32)],
        compiler_params=_cp(("arbitrary",)),
    )(dkm, dvm, memn, mem, g_mem, w_kv)


def _fox_bwd(proj, proj_b, dyb, lse, bq, bk, Bl, S):
    T = Bl * S
    nq = S // Q_BLK
    nb = S // LANES
    qc, kc, vc = 768 // LANES, 1152 // LANES, 0

    def body(q_ref, k_ref, v_ref, do_ref, lse_ref, bq_ref, bk_ref,
             dq_ref, dk_ref, dv_ref, dcr_ref, ka_ref, dka_ref, dva_ref):
        p = pl.program_id(1)
        lane_s = _iota((S, LANES), 1)
        lane = _iota((Q_BLK, LANES), 1)
        sub = _iota((8, LANES), 0)
        tri = _iota((Q_BLK, Q_BLK), 1) <= _iota((Q_BLK, Q_BLK), 0)
        k = k_ref[...]
        for hh in range(2):
            data = (lane_s < HEAD) if hh == 0 else (lane_s >= HEAD)
            ka_ref[hh] = jnp.where(data, k, bk_ref[0, hh])
        dka_ref[...] = jnp.zeros_like(dka_ref)
        dva_ref[...] = jnp.zeros_like(dva_ref)

        @pl.when(p == 0)
        def _():
            dcr_ref[...] = jnp.zeros_like(dcr_ref)

        def add_colsums(ds, first_blk, h):
            cs = _colsum(ds)
            for jb in range(ds.shape[1] // LANES):
                dcr_ref[0, first_blk + jb] += jnp.where(sub == h, cs[:, jb * LANES:(jb + 1) * LANES], 0.0)

        for i in range(nq):
            r0 = i * Q_BLK
            r1 = r0 + Q_BLK
            q = q_ref[r0:r1, :]
            do = do_ref[r0:r1, :]
            lse_b = lse_ref[0, r0:r1, :]
            dq_out = jnp.zeros((Q_BLK, LANES), F32)
            for hh in range(2):
                hmask = (lane < HEAD) if hh == 0 else (lane >= HEAD)
                h = 2 * p + hh
                qs = jnp.where(hmask, q * 0.125, jnp.zeros_like(q))
                qa = jnp.where(hmask, q * 0.125, bq_ref[0, hh, r0:r1, :])
                dob = jnp.where(hmask, do, 0.0).astype(BF16)
                lse_h = jnp.sum(jnp.where(lane == hh * HEAD, lse_b, 0.0), axis=1, keepdims=True)
                pd = jnp.where(tri, jnp.exp(_dot_nt(qa, ka_ref[hh, r0:r1, :]) - lse_h), 0.0)
                dpd = _dot_nt(dob, v_ref[r0:r1, :])
                delta = jnp.sum(pd * dpd, axis=1, keepdims=True)
                psum = jnp.sum(pd, axis=1, keepdims=True)
                if i:
                    pf = jnp.exp(_dot_nt(qa, ka_ref[hh, 0:r0, :]) - lse_h)
                    dpf = _dot_nt(dob, v_ref[0:r0, :])
                    delta = delta + jnp.sum(pf * dpf, axis=1, keepdims=True)
                    psum = psum + jnp.sum(pf, axis=1, keepdims=True)
                delta = delta / psum
                dsd = pd * (dpd - delta)
                add_colsums(dsd, r0 // LANES, h)
                dsd = dsd.astype(BF16)
                dq_h = _dot(dsd, k_ref[r0:r1, :])
                dka_ref[r0:r1, :] += _dot_tn(dsd, qs)
                dva_ref[r0:r1, :] += _dot_tn(pd, dob)
                if i:
                    dsf = pf * (dpf - delta)
                    add_colsums(dsf, 0, h)
                    dsf = dsf.astype(BF16)
                    dq_h = dq_h + _dot(dsf, k_ref[0:r0, :])
                    dka_ref[0:r0, :] += _dot_tn(dsf, qs)
                    dva_ref[0:r0, :] += _dot_tn(pf, dob)
                dq_out = jnp.where(hmask, dq_h * 0.125, dq_out)
            dq_ref[r0:r1, :] = dq_out.astype(BF16)
        dk_ref[...] = dka_ref[...].astype(BF16)
        dv_ref[...] = dva_ref[...].astype(BF16)

    seq = lambda c0: BS((S, LANES), lambda b, p: (b, c0 + p))
    pair = BS((1, 2, S, LANES), lambda b, p: (b, p, 0, 0))
    rowblk = BS((1, nb, 8, LANES), lambda b, p: (b, 0, 0, 0))
    return pl.pallas_call(
        body, name="fox_bwd", grid=(Bl, 3),
        in_specs=[seq(qc), seq(kc), seq(vc), seq(0), BS((1, S, LANES), lambda b, p: (p, b, 0)), pair, pair],
        out_specs=[seq(0), seq(0), seq(0), rowblk],
        out_shape=[jax.ShapeDtypeStruct((T, B_W), BF16)] * 3 + [jax.ShapeDtypeStruct((Bl, nb, 8, LANES), F32)],
        scratch_shapes=[pltpu.VMEM((2, S, LANES), BF16), pltpu.VMEM((S, LANES), F32), pltpu.VMEM((S, LANES), F32)],
        compiler_params=_cp(("arbitrary", "arbitrary")),
    )(proj, proj, proj_b, dyb, lse, bq, bk)


def _gate_bwd(dc_row, fl_row):
    Bl, nb, _, _ = dc_row.shape

    def body(dc_ref, fl_ref, o_ref):
        lane = _iota((8, LANES), 1)

        carry = jnp.zeros((8, 1), F32)
        for j in reversed(range(nb)):
            r = -dc_ref[0, j]
            for k in (1, 2, 4, 8, 16, 32, 64):
                r = r + jnp.where(lane < LANES - k, pltpu.roll(r, LANES - k, 1), 0.0)
            total = jnp.sum(jnp.where(lane == 0, r, 0.0), axis=1, keepdims=True)
            dfl = (r + carry) * _sigmoid(-fl_ref[0, j])
            carry = carry + total
            o_ref[0, j * LANES:(j + 1) * LANES, :] = jnp.concatenate(
                [dfl, jnp.zeros((LANES - 8, LANES), F32)], axis=0).T

    rowblk = BS((1, nb, 8, LANES), lambda b: (b, 0, 0, 0))
    return pl.pallas_call(
        body, name="gate_bwd", grid=(Bl,),
        in_specs=[rowblk, rowblk],
        out_specs=BS((1, nb * LANES, LANES), lambda b: (b, 0, 0)),
        out_shape=jax.ShapeDtypeStruct((Bl, nb * LANES, LANES), F32),
        compiler_params=_cp(("arbitrary",)),
    )(dc_row, fl_row)


def _inproj_bwd(pieces, x2d, dx1, g_pre, w_in_p, tm):
    T, D = x2d.shape
    ns, _, dsh = w_in_p.shape

    def body(*refs):
        piece_refs = refs[:7]
        x_ref, dx1_ref, g_ref, w_ref, gx_ref, dg_ref, dbf_ref, dp_ref = refs[7:]
        first = pl.program_id(0) == 0
        _put_dproj(dp_ref, piece_refs)
        dh = jnp.concatenate([_dot(dp_ref[...], w_ref[s]) for s in range(ns)], axis=1)
        dxa, dg = _rms_bwd(x_ref[...], g_ref[...], dh)
        gx_ref[...] = dx1_ref[...] + dxa
        _acc(dg_ref, dg, first)
        _acc(dbf_ref, _colsum(piece_refs[5][...]), first)

    row = lambda w: BS((tm, w), lambda i: (i, 0))
    return pl.pallas_call(
        body, name="inproj_bwd", grid=(T // tm,),
        in_specs=[row(w) for _, w in DPROJ_PIECES] + [row(D), row(D), BS((1, D), lambda i: (0, 0)),
                                                      BS((ns, P_COLS, dsh), lambda i: (0, 0, 0))],
        out_specs=[row(D), BS((1, D), lambda i: (0, 0)), BS((1, LANES), lambda i: (0, 0))],
        out_shape=[jax.ShapeDtypeStruct((T, D), F32), jax.ShapeDtypeStruct((1, D), F32),
                   jax.ShapeDtypeStruct((1, LANES), F32)],
        scratch_shapes=[pltpu.VMEM((tm, P_COLS), BF16)],
        compiler_params=_cp(("arbitrary",)),
    )(*pieces, x2d, dx1, g_pre, w_in_p)


def _local_step(x, mem, target, W, P, reduce=None):
    Bl, S, D = x.shape
    T = Bl * S
    tm = min(512, T)
    x2d = x.reshape(T, D)
    t2d = target.reshape(T, D)
    vec = lambda a: a.reshape(1, -1)
    bf_row = jnp.pad(P["b_f"].reshape(1, -1), ((0, 0), (0, LANES - N_FOX_HEADS)))
    tril = jnp.tril(jnp.ones((CHUNK, CHUNK), bool))
    ws_tril = jnp.where(tril[None], P["w_s"][0], 0.0).astype(BF16)
    bs_full = jnp.repeat(P["b_s"][0].T, HEAD, axis=1)
    g_pre, g_sgu = vec(P["g_pre_mix"]), vec(P["g_sgu"])
    ga, gb, gm = vec(P["g_out_a"]), vec(P["g_out_b"]), vec(P["g_out_m"])
    g_mem, g_post, g_pre2, g_post2 = vec(P["g_mem"]), vec(P["g_post_mix"]), vec(P["g_pre_ffn"]), vec(P["g_post_ffn"])

    h, proj = _inproj_fwd_a(x2d, g_pre, W["w_in_a"], tm)
    proj_b, flog = _inproj_fwd_b(h, W["w_in_b"], tm)
    bq, bk, fl_row = _gate_fwd(flog.reshape(Bl, S, LANES), bf_row)
    ya = _sgu_fwd(proj, g_sgu, ws_tril, bs_full, tm)
    yb, lse = _fox_fwd(proj, proj_b, bq, bk, Bl, S)
    memn, kv = _memkv_fwd(mem, g_mem, W["w_mem_kv"])
    ym = _memattn_fwd(proj_b, kv, Bl, S, min(2048, S))
    y, o, x1, h2 = _outproj_fwd(ya, yb, ym, x2d, ga, gb, gm, g_post, g_pre2, W["w_out"], tm)
    gs, us, dff, dx2, dg_post2, loss = _ffn_fwd(h2, x1, t2d, W["w_gate"], W["w_up"], W["w_down"], g_post2, tm)

    dh2, d_w_gate, d_w_up, d_w_down = _ffn_bwd(dff, h2, gs, us, W["w_gate"], W["w_up"], W["w_down"], min(1024, T))
    ffn = [d_w_gate, d_w_up, d_w_down]
    if reduce is not None:
        pending, _ = reduce.begin("ffn", ffn)
    dx1, d_w_out, dya, dyb, dym, dga, dgb, dgm, dg_post, dg_pre2 = _outproj_bwd(
        dh2, x1, dx2, o, y, ya, yb, ym, ga, gb, gm, g_post, g_pre2, W["w_out"], tm)
    if reduce is not None:
        ffn, (dya, dyb, dym) = reduce.finish("ffn", pending, (dya, dyb, dym))
    dzu, dzv, dws, dbs_cols, dg_sgu = _sgu_bwd(proj, dya, g_sgu, ws_tril, bs_full, tm)
    dqm, dkm, dvm = _memattn_bwd(proj_b, kv, dym, Bl, S, min(2048, S))
    d_w_kv, dg_mem = _memkv_bwd(dkm, dvm, memn, mem, g_mem, W["w_mem_kv"])
    mid = [d_w_kv, d_w_out]
    dq, dk, dv, dc_row = _fox_bwd(proj, proj_b, dyb, lse, bq, bk, Bl, S)
    if reduce is not None:
        pending, after = reduce.begin("mid", mid, (dc_row,) + tuple(ffn))
        dc_row, ffn = after[0], list(after[1:])
    dfl = _gate_bwd(dc_row, fl_row).reshape(T, LANES)
    if reduce is not None:
        mid, (dfl,) = reduce.finish("mid", pending, (dfl,), first=(dzu, dzv))
    pieces = (dzu, dzv, dq, dk, dv, dfl, dqm)
    d_w_in = _dw_in(pieces, h, W["w_in"].shape[0], 1024)
    if reduce is None:
        big = dict(zip(BIG, [d_w_in] + mid + ffn))
    else:
        done = reduce.apply(BIG[1:3], mid)
        pending, after = reduce.begin("in", [d_w_in], done + (dx1,))
        dx1 = after[-1]
        big = {"w_in": pending, "ffn": ffn}
    grad_x, dg_pre, dbf = _inproj_bwd(pieces, x2d, dx1, g_pre, W["w_in"], tm)
    small = {"g_pre_mix": dg_pre, "b_f": dbf[:, :N_FOX_HEADS], "g_sgu": dg_sgu, "w_s": dws, "b_s": dbs_cols[:, :N_FOX_HEADS].T,
             "g_out_a": dga, "g_out_b": dgb, "g_out_m": dgm, "g_mem": dg_mem, "g_post_mix": dg_post,
             "g_pre_ffn": dg_pre2, "g_post_ffn": dg_post2, "loss": loss[:, :1]}
    return grad_x.reshape(Bl, S, D), big, small


def _place():
    return lax.axis_index("x"), lax.axis_index("y"), lax.axis_index("c")


def _exchange_on_sequencer(srcs, own_full, name, collective_id):
    n = len(srcs)

    def body(*refs):
        src, dst = refs[:n], refs[n:2 * n]
        lsem, isend, irecv, dsend, drecv = refs[2 * n:]
        x, y, c = _place()
        oc = 1 - c
        s_me = 2 * x + y
        sib = (x, y, oc)
        chips = [(1 - x, y), (x, 1 - y), (1 - x, 1 - y)]
        barrier = pltpu.get_barrier_semaphore()
        for dev in [(cx, cy, c) for cx, cy in chips] + [sib]:
            pl.semaphore_signal(barrier, inc=1, device_id=dev, device_id_type=MESH)
        pl.semaphore_wait(barrier, 4)

        def remote(a, b, ssem, rsem, dev):
            return pltpu.make_async_remote_copy(src_ref=a, dst_ref=b, send_sem=ssem, recv_sem=rsem,
                                                device_id=dev, device_id_type=MESH)

        sends, local = [], []
        for w in range(n):
            for j, (cx, cy) in enumerate(chips):
                half = src[w].at[c] if own_full else src[w].at[2 * cx + cy]
                cp = remote(half, dst[w].at[s_me, c], isend.at[w, j], irecv.at[w, j], (cx, cy, c))
                cp.start()
                sends.append(cp)
            if own_full:
                cp = remote(src[w], dst[w].at[s_me], dsend.at[w, 3], drecv.at[w, 3], sib)
            else:
                cp = remote(src[w].at[s_me], dst[w].at[s_me, c], dsend.at[w, 3], drecv.at[w, 3], sib)
                loc = pltpu.make_async_copy(src[w].at[s_me], dst[w].at[s_me, c], lsem.at[w])
                loc.start()
                local.append(loc)
            cp.start()
            sends.append(cp)
        for w in range(n):
            for j, (cx, cy) in enumerate(chips):
                landed = dst[w].at[2 * cx + cy, c]
                remote(landed, landed, isend.at[w, j], irecv.at[w, j], (cx, cy, c)).wait_recv()
                cp = remote(landed, landed, dsend.at[w, j], drecv.at[w, j], sib)
                cp.start()
                sends.append(cp)
        for w in range(n):
            for j, (cx, cy) in enumerate(chips):
                landed = dst[w].at[2 * cx + cy, oc]
                remote(landed, landed, dsend.at[w, j], drecv.at[w, j], sib).wait_recv()
            landed = dst[w].at[s_me] if own_full else dst[w].at[s_me, oc]
            remote(landed, landed, dsend.at[w, 3], drecv.at[w, 3], sib).wait_recv()
        for cp in sends:
            cp.wait_send()
        for loc in local:
            loc.wait()

    return pl.kernel(
        body, out_type=[jax.ShapeDtypeStruct((4, 2) + s.shape[1:], s.dtype) for s in srcs],
        mesh=plsc.ScalarSubcoreMesh(axis_name="sequencer", num_cores=1), name=name,
        scratch_types=[pltpu.SemaphoreType.DMA((n,)), pltpu.SemaphoreType.DMA((n, 3)), pltpu.SemaphoreType.DMA((n, 3)),
                       pltpu.SemaphoreType.DMA((n, 4)), pltpu.SemaphoreType.DMA((n, 4))],
        compiler_params=pltpu.CompilerParams(collective_id=collective_id),
    )(*srcs)


def _sibling_swap(grads, name, collective_id):
    n = len(grads)

    def body(*refs):
        g, theirs = refs[:n], refs[n:2 * n]
        ssem, rsem = refs[2 * n:]
        x, y, c = _place()
        sib = (x, y, 1 - c)
        barrier = pltpu.get_barrier_semaphore()
        pl.semaphore_signal(barrier, inc=1, device_id=sib, device_id_type=MESH)
        pl.semaphore_wait(barrier, 1)
        cps = []
        for w in range(n):
            cp = pltpu.make_async_remote_copy(src_ref=g[w].at[:, 1 - c], dst_ref=theirs[w], send_sem=ssem.at[w],
                                              recv_sem=rsem.at[w], device_id=sib, device_id_type=MESH)
            cp.start()
            cps.append(cp)
        for cp in cps:
            cp.wait()

    return pl.kernel(
        body, out_type=[jax.ShapeDtypeStruct((4,) + g.shape[2:], g.dtype) for g in grads],
        mesh=plsc.ScalarSubcoreMesh(axis_name="sequencer", num_cores=1), name=name,
        scratch_types=[pltpu.SemaphoreType.DMA((n,)), pltpu.SemaphoreType.DMA((n,))],
        compiler_params=pltpu.CompilerParams(collective_id=collective_id),
    )(*grads)


def _add_pairs(core, gs, theirs, name):
    n = len(gs)

    def body(core_ref, *refs):
        for g_ref, t_ref, o_ref in zip(refs[:n], refs[n:2 * n], refs[2 * n:]):
            o_ref[...] = (g_ref[:, 0].astype(F32) + t_ref[...].astype(F32)).astype(BF16)

    def specs(g):
        _, _, hr, C = g.shape
        return (BS((2, 1, hr, C), lambda s, core_ref: (s, core_ref[0], 0, 0)), BS((2, hr, C), lambda s, core_ref: (s, 0, 0)))

    return pl.pallas_call(
        body, name=name,
        grid_spec=pltpu.PrefetchScalarGridSpec(
            num_scalar_prefetch=1, grid=(2,),
            in_specs=[specs(g)[0] for g in gs] + [specs(g)[1] for g in gs], out_specs=[specs(g)[1] for g in gs]),
        out_shape=[jax.ShapeDtypeStruct(t.shape, BF16) for t in theirs],
        compiler_params=_cp(("arbitrary",)))(core, *gs, *theirs)


def _sum_chips(r, name):
    _, _, hr, C = r.shape

    def body(r_ref, o_ref):
        o_ref[...] = ((r_ref[0, 0].astype(F32) + r_ref[1, 0].astype(F32)) + r_ref[2, 0].astype(F32)) + r_ref[3, 0].astype(F32)

    return pl.pallas_call(body, name=name, grid=(2,), in_specs=[BS((4, 1, hr, C), lambda h: (0, h, 0, 0))],
                          out_specs=BS((hr, C), lambda h: (h, 0)), out_shape=jax.ShapeDtypeStruct((2 * hr, C), F32),
                          compiler_params=_cp(("arbitrary",)))(r)


class _Reducer:
    IDS = {"ffn": (4, 5), "mid": (6, 7), "in": (8, 9)}

    def __init__(self, core, apply):
        self.core = core
        self.apply = apply

    def begin(self, tag, grads, after=()):
        grads, after = lax.optimization_barrier((list(grads), after))
        g4 = [g.reshape(4, 2, -1, g.shape[-1]) for g in grads]
        return (g4, _sibling_swap(g4, "swap_" + tag, self.IDS[tag][0])), after

    def finish(self, tag, pending, hold, first=()):
        pending, first = lax.optimization_barrier((pending, first))
        g4, theirs = pending
        sums = _add_pairs(self.core, g4, theirs, "chip_sum_" + tag)
        sums, hold = lax.optimization_barrier((sums, hold))
        return _exchange_on_sequencer(sums, False, "scatter_" + tag, self.IDS[tag][1]), hold


def _small_allreduce(part):
    R = part.shape[0]
    rs = R // 8
    masks = [(mx, my, mc) for mx in (0, 1) for my in (0, 1) for mc in (0, 1)][1:]

    def body(p_ref, o_ref, buf_ref, s1, r1, s2, r2):
        x, y, c = _place()
        d = 4 * x + 2 * y + c
        mine = pl.ds(pl.multiple_of(d * rs, 8), rs)
        peers = [((x + mx) % 2, (y + my) % 2, (c + mc) % 2) for mx, my, mc in masks]
        first, second = [], []
        for k, (px, py, pc) in enumerate(peers):
            theirs = pl.ds(pl.multiple_of((4 * px + 2 * py + pc) * rs, 8), rs)
            cp = pltpu.make_async_remote_copy(src_ref=p_ref.at[theirs, :], dst_ref=buf_ref.at[d], send_sem=s1.at[k],
                                              recv_sem=r1.at[k], device_id=(px, py, pc), device_id_type=MESH)
            cp.start()
            first.append(cp)
        buf_ref[d] = p_ref[mine, :]
        for k, (px, py, pc) in enumerate(peers):
            slot = buf_ref.at[4 * px + 2 * py + pc]
            pltpu.make_async_remote_copy(src_ref=slot, dst_ref=slot, send_sem=s1.at[k], recv_sem=r1.at[k],
                                         device_id=(px, py, pc), device_id_type=MESH).wait_recv()
        total = buf_ref[0]
        for k in range(1, 8):
            total = total + buf_ref[k]
        o_ref[mine, :] = total
        for k, (px, py, pc) in enumerate(peers):
            cp = pltpu.make_async_remote_copy(src_ref=o_ref.at[mine, :], dst_ref=o_ref.at[mine, :], send_sem=s2.at[k],
                                              recv_sem=r2.at[k], device_id=(px, py, pc), device_id_type=MESH)
            cp.start()
            second.append(cp)
        for k, (px, py, pc) in enumerate(peers):
            rows = o_ref.at[pl.ds(pl.multiple_of((4 * px + 2 * py + pc) * rs, 8), rs), :]
            pltpu.make_async_remote_copy(src_ref=rows, dst_ref=rows, send_sem=s2.at[k], recv_sem=r2.at[k],
                                         device_id=(px, py, pc), device_id_type=MESH).wait_recv()
        for cp in first + second:
            cp.wait_send()

    vm = pl.BlockSpec(memory_space=pltpu.VMEM)
    return pl.pallas_call(
        body, name="small_allreduce", in_specs=[vm], out_specs=vm, out_shape=jax.ShapeDtypeStruct(part.shape, F32),
        scratch_shapes=[pltpu.VMEM((8, rs, LANES), F32)] + [pltpu.SemaphoreType.DMA((7,))] * 4,
    )(part)


def _adamw(w, g, m, v, name):
    R, C = w.shape
    summed = g.ndim == 4
    if summed:
        tr = R // 2
    else:
        tr = R if R * C * 4 <= (1 << 21) else R // 2
        if tr % 8:
            tr = R
    c1 = 1.0 / (1.0 - ADAM_B1 ** ADAM_STEP)
    c2 = 1.0 / (1.0 - ADAM_B2 ** ADAM_STEP)

    def body(w_ref, g_ref, m_ref, v_ref, *outs):
        if summed:
            g_ = ((g_ref[0, 0].astype(F32) + g_ref[1, 0].astype(F32)) + g_ref[2, 0].astype(F32)) + g_ref[3, 0].astype(F32)
            outs[0][...] = g_
        else:
            g_ = g_ref[...]
        d_ref, mo_ref, vo_ref = outs[-3:]
        m_ = ADAM_B1 * m_ref[...] + (1.0 - ADAM_B1) * g_
        v_ = ADAM_B2 * v_ref[...] + (1.0 - ADAM_B2) * (g_ * g_)
        mo_ref[...] = m_
        vo_ref[...] = v_
        d_ref[...] = -ADAM_LR * ((m_ * c1) / (jnp.sqrt(v_ * c2) + ADAM_EPS) + ADAM_WD * w_ref[...])

    blk = BS((tr, C), lambda i: (i, 0))
    g_blk = BS((4, 1, tr, C), lambda i: (0, i, 0, 0)) if summed else blk
    nout = 4 if summed else 3
    return pl.pallas_call(body, name=name, grid=(R // tr,), in_specs=[blk, g_blk, blk, blk], out_specs=[blk] * nout,
                          out_shape=[jax.ShapeDtypeStruct((R, C), F32)] * nout,
                          compiler_params=_cp(("arbitrary",)))(w, g, m, v)


def _adamw_unit_rows(w, g, m, v, name):
    C, _, R = w.shape
    tc = C // 2 if C % 2 == 0 else C
    c1 = 1.0 / (1.0 - ADAM_B1 ** ADAM_STEP)
    c2 = 1.0 / (1.0 - ADAM_B2 ** ADAM_STEP)

    def body(w_ref, g_ref, m_ref, v_ref, go_ref, d_ref, mo_ref, vo_ref):
        g_ = g_ref[...]
        go_ref[...] = g_
        m_ = ADAM_B1 * m_ref[...] + (1.0 - ADAM_B1) * g_
        v_ = ADAM_B2 * v_ref[...] + (1.0 - ADAM_B2) * (g_ * g_)
        mo_ref[...] = m_
        vo_ref[...] = v_
        d_ref[...] = -ADAM_LR * ((m_ * c1) / (jnp.sqrt(v_ * c2) + ADAM_EPS) + ADAM_WD * w_ref[...])

    blk = BS((tc, 1, R), lambda i: (i, 0, 0))
    return pl.pallas_call(body, name=name, grid=(C // tc,), in_specs=[blk] * 4, out_specs=[blk] * 4,
                          out_shape=[jax.ShapeDtypeStruct((C, 1, R), F32)] * 4,
                          compiler_params=_cp(("arbitrary",)))(w, g, m, v)


SMALL = ("g_pre_mix", "b_f", "g_sgu", "w_s", "b_s", "g_out_a", "g_out_b", "g_out_m", "g_mem", "g_post_mix",
         "g_pre_ffn", "g_post_ffn")
BIG = ("w_in", "w_mem_kv", "w_out", "w_gate", "w_up", "w_down")
TRANSPOSED = ("w_in", "w_gate", "w_up")
WEIGHTS = ("g_pre_mix", "w_in", "b_f", "g_sgu", "w_s", "b_s", "g_out_a", "g_out_b", "g_out_m", "g_mem", "w_mem_kv",
           "w_out", "g_post_mix", "g_pre_ffn", "w_gate", "w_up", "w_down", "g_post_ffn")


VECTORS = ("g_pre_mix", "b_f", "g_sgu", "g_out_a", "g_out_b", "g_out_m", "g_mem", "g_post_mix", "g_pre_ffn", "g_post_ffn")
VEC_ROWS = 16
WS_ROWS = N_FOX_HEADS * CHUNK
BS_ROWS = 8


def _pack_small(small, vw):
    stack = jnp.zeros((VEC_ROWS, vw), F32)
    for k, n in enumerate(VECTORS + ("loss",)):
        row = small[n].reshape(1, -1)
        stack = stack + jnp.pad(row, ((k, VEC_ROWS - 1 - k), (0, vw - row.shape[1])))
    parts = [small["w_s"].reshape(WS_ROWS, LANES), jnp.pad(small["b_s"], ((0, BS_ROWS - N_FOX_HEADS), (0, 0))),
             stack.reshape(-1, LANES)]
    rows = sum(p.shape[0] for p in parts)
    return jnp.concatenate(parts + [jnp.zeros((-rows % 64, LANES), F32)], axis=0)


def _adamw_small(vec_g, vec_wmv, ws, bs):
    c1 = 1.0 / (1.0 - ADAM_B1 ** ADAM_STEP)
    c2 = 1.0 / (1.0 - ADAM_B2 ** ADAM_STEP)
    nv = len(vec_wmv)

    def adam(g, w, m, v):
        m_ = ADAM_B1 * m + (1.0 - ADAM_B1) * g
        v_ = ADAM_B2 * v + (1.0 - ADAM_B2) * (g * g)
        return -ADAM_LR * ((m_ * c1) / (jnp.sqrt(v_ * c2) + ADAM_EPS) + ADAM_WD * w), m_, v_

    def body(*refs):
        vg_ref, ins, outs = refs[0], refs[1:1 + 3 * nv + 8], refs[1 + 3 * nv + 8:]
        for k in range(nv):
            w_ref, m_ref, v_ref = ins[3 * k:3 * k + 3]
            g = vg_ref[k:k + 1, 0:w_ref.shape[1]]
            d, m_, v_ = adam(g, w_ref[...], m_ref[...], v_ref[...])
            for o_ref, val in zip(outs[4 * k:4 * k + 4], (g, d, m_, v_)):
                o_ref[...] = val
        for j in range(2):
            g_ref, w_ref, m_ref, v_ref = ins[3 * nv + 4 * j:3 * nv + 4 * j + 4]
            for o_ref, val in zip(outs[4 * nv + 3 * j:4 * nv + 3 * j + 3], adam(g_ref[...], w_ref[...], m_ref[...], v_ref[...])):
                o_ref[...] = val

    vm = pl.BlockSpec(memory_space=pltpu.VMEM)
    operands = [vec_g] + [a for wmv in vec_wmv for a in wmv] + list(ws) + list(bs)
    out_shape = ([jax.ShapeDtypeStruct(wmv[0].shape, F32) for wmv in vec_wmv for _ in range(4)]
                 + [jax.ShapeDtypeStruct(ws[1].shape, F32)] * 3 + [jax.ShapeDtypeStruct(bs[1].shape, F32)] * 3)
    outs = pl.pallas_call(body, name="adamw_small", in_specs=[vm] * len(operands), out_specs=[vm] * len(out_shape),
                          out_shape=out_shape)(*operands)
    return [outs[4 * k:4 * k + 4] for k in range(nv)], outs[4 * nv:4 * nv + 3], outs[4 * nv + 3:]


def kernel(x, mem, g_pre_mix, w_in, b_f, g_sgu, w_s, b_s, g_out_a, g_out_b, g_out_m, g_mem, w_mem_kv, w_out, g_post_mix, g_pre_ffn, w_gate, w_up, w_down, g_post_ffn, loss_target, m_g_pre_mix, m_w_in, m_b_f, m_g_sgu, m_w_s, m_b_s, m_g_out_a, m_g_out_b, m_g_out_m, m_g_mem, m_w_mem_kv, m_w_out, m_g_post_mix, m_g_pre_ffn, m_w_gate, m_w_up, m_w_down, m_g_post_ffn, v_g_pre_mix, v_w_in, v_b_f, v_g_sgu, v_w_s, v_b_s, v_g_out_a, v_g_out_b, v_g_out_m, v_g_mem, v_w_mem_kv, v_w_out, v_g_post_mix, v_g_pre_ffn, v_w_gate, v_w_up, v_w_down, v_g_post_ffn):
    Wt = dict(g_pre_mix=g_pre_mix, w_in=w_in, b_f=b_f, g_sgu=g_sgu, w_s=w_s, b_s=b_s, g_out_a=g_out_a, g_out_b=g_out_b,
              g_out_m=g_out_m, g_mem=g_mem, w_mem_kv=w_mem_kv, w_out=w_out, g_post_mix=g_post_mix, g_pre_ffn=g_pre_ffn,
              w_gate=w_gate, w_up=w_up, w_down=w_down, g_post_ffn=g_post_ffn)
    Mo = dict(g_pre_mix=m_g_pre_mix, w_in=m_w_in, b_f=m_b_f, g_sgu=m_g_sgu, w_s=m_w_s, b_s=m_b_s, g_out_a=m_g_out_a,
              g_out_b=m_g_out_b, g_out_m=m_g_out_m, g_mem=m_g_mem, w_mem_kv=m_w_mem_kv, w_out=m_w_out,
              g_post_mix=m_g_post_mix, g_pre_ffn=m_g_pre_ffn, w_gate=m_w_gate, w_up=m_w_up, w_down=m_w_down,
              g_post_ffn=m_g_post_ffn)
    Vo = dict(g_pre_mix=v_g_pre_mix, w_in=v_w_in, b_f=v_b_f, g_sgu=v_g_sgu, w_s=v_w_s, b_s=v_b_s, g_out_a=v_g_out_a,
              g_out_b=v_g_out_b, g_out_m=v_g_out_m, g_mem=v_g_mem, w_mem_kv=v_w_mem_kv, w_out=v_w_out,
              g_post_mix=v_g_post_mix, g_pre_ffn=v_g_pre_ffn, w_gate=v_w_gate, w_up=v_w_up, w_down=v_w_down,
              g_post_ffn=v_g_post_ffn)

    gap = P_COLS - IN_COLS

    def to_kernel(n, w):
        if n in TRANSPOSED:
            w = w.T
        if n == "w_in":
            w = jnp.pad(w[:F_END], ((0, P_COLS - F_END), (0, 0))) + jnp.pad(w[F_END:], ((F_END + gap, 0), (0, 0)))
        return w

    def ungroup(g):
        return jnp.pad(g[:F_END], ((0, IN_COLS - F_END), (0, 0))) + jnp.pad(g[F_END + gap:], ((F_END, 0), (0, 0)))

    shards = {n: to_kernel(n, Wt[n][0]) for n in BIG}
    halves = lambda a: a.astype(BF16).reshape(2, a.shape[0] // 2, a.shape[1])
    srcs = [halves(shards[n]) for n in BIG]
    in_a = _exchange_on_sequencer([halves(shards["w_in"][:SPLIT])], True, "gather_w_in_a", 1)[0]
    in_b = _exchange_on_sequencer([halves(shards["w_in"][SPLIT:])], True, "gather_w_in_b", 10)[0]
    fulls = ([None] + _exchange_on_sequencer(srcs[1:3], True, "gather_kv_out", 2)
             + _exchange_on_sequencer(srcs[3:], True, "gather_ffn", 3))
    W = {}
    for n, f in zip(BIG[1:], fulls[1:]):
        _, _, hr, C = f.shape
        W[n] = f.reshape(8 * hr, C) if n in ("w_mem_kv", "w_out") else f.reshape(4, 2 * hr, C)
    W["w_in_a"] = in_a.reshape(4, SPLIT, -1)
    W["w_in_b"] = in_b.reshape(4, P_COLS - SPLIT, -1)
    W["w_in"] = jnp.concatenate([W["w_in_a"], W["w_in_b"]], axis=1)

    P = {n: Wt[n] for n in SMALL}
    grads, deltas, new_m, new_v = {}, {}, {}, {}

    def apply(names, landed):
        for n, r in zip(names, landed):
            if n == "w_in":
                g_t = ungroup(_sum_chips(r, "sum_chips_" + n))
                lift = lambda a: jnp.transpose(a, (2, 0, 1))
                outs = _adamw_unit_rows(lift(Wt[n]), g_t[:, None, :], lift(Mo[n]), lift(Vo[n]), "adamw_" + n)
                g, d, m1, v1 = [jnp.transpose(a, (1, 2, 0))[0] for a in outs]
            elif n in TRANSPOSED:
                g, d, m1, v1 = [a.T for a in _adamw(Wt[n][0].T, r, Mo[n][0].T, Vo[n][0].T, "adamw_" + n)]
            else:
                g, d, m1, v1 = _adamw(Wt[n][0], r, Mo[n][0], Vo[n][0], "adamw_" + n)
            grads[n], deltas[n], new_m[n], new_v[n] = g[None], d[None], m1[None], v1[None]
        return tuple(deltas[n] for n in names)

    core = lax.axis_index("c").astype(jnp.int32).reshape(1)
    reducer = _Reducer(core, apply)
    grad_x, pending, small = _local_step(x, mem, loss_target, W, P, reducer)

    vw = -(-max(x.shape[-1], A_W) // LANES) * LANES
    total = _small_allreduce(_pack_small(small, vw))
    landed, ffn_landed = reducer.finish("in", pending["w_in"], tuple(pending["ffn"]), first=(total,))
    apply(BIG[3:], ffn_landed)
    apply(BIG[:1], landed)

    lane_row = lambda a: jnp.pad(a, ((0, 0), (0, -a.shape[1] % LANES)))
    vec_g = total[WS_ROWS + BS_ROWS:WS_ROWS + BS_ROWS + VEC_ROWS * vw // LANES].reshape(VEC_ROWS, vw)
    ws_g = total[:WS_ROWS]
    bs_g = total[WS_ROWS:WS_ROWS + N_FOX_HEADS]
    rows = lambda a, r: a.reshape(r, LANES)
    per_vec, ws_out, bs_out = _adamw_small(
        vec_g, [tuple(lane_row(a[n]) for a in (Wt, Mo, Vo)) for n in VECTORS],
        (ws_g,) + tuple(rows(a["w_s"], WS_ROWS) for a in (Wt, Mo, Vo)),
        (bs_g,) + tuple(rows(a["b_s"], N_FOX_HEADS) for a in (Wt, Mo, Vo)))
    for n, outs in zip(VECTORS, per_vec):
        grads[n], deltas[n], new_m[n], new_v[n] = [o[:, :Wt[n].shape[1]] for o in outs]
    for n, g, outs in (("w_s", ws_g, ws_out), ("b_s", bs_g, bs_out)):
        grads[n], deltas[n], new_m[n], new_v[n] = [o.reshape(Wt[n].shape) for o in (g,) + tuple(outs)]
    loss = vec_g[len(VECTORS), 0]

    return (loss, grad_x, *[grads[n] for n in WEIGHTS], *[deltas[n] for n in WEIGHTS],
            *[new_m[n] for n in WEIGHTS], *[new_v[n] for n in WEIGHTS])
```
